```python
import math
import jax, jax.numpy as jnp
from jax import lax
import numpy as np

D_MODEL = 1024
BATCH = 8
SEQ = 8192
DEPTH = 1

HEAD_DIM = 64
DIL_GROUPS = ((128, 1), (512, 4), (2048, 16))
DIL_HEADS_PER_GROUP = 4
N_DIL_HEADS = DIL_HEADS_PER_GROUP * len(DIL_GROUPS)
N_SB_HEADS = 8
DIL_WIDTH = N_DIL_HEADS * HEAD_DIM
DIL_OUT_WIDTH = DIL_HEADS_PER_GROUP * HEAD_DIM
SB_WIDTH = N_SB_HEADS * HEAD_DIM
D_FF = 4 * D_MODEL
BLOCK = 128
RMS_EPS = 1e-6
NEG_INF = -1e30
IN_COLS = 3 * DIL_WIDTH + 3 * SB_WIDTH + 2 * D_MODEL
SPLITS = (DIL_WIDTH, 2 * DIL_WIDTH, 3 * DIL_WIDTH,
          3 * DIL_WIDTH + SB_WIDTH, 3 * DIL_WIDTH + 2 * SB_WIDTH, 3 * DIL_WIDTH + 3 * SB_WIDTH,
          3 * DIL_WIDTH + 3 * SB_WIDTH + D_MODEL)

kernel_name = "hybrid_dilated_stickbreaking_gated_block"


def rmsnorm(x, g):
    xf = x.astype(jnp.float32)
    y = xf * lax.rsqrt(jnp.mean(xf * xf, axis=-1, keepdims=True) + RMS_EPS)
    return (y * g.astype(jnp.float32)).astype(x.dtype)


def alibi_slopes(n):
    return jnp.exp2(-8.0 * jnp.arange(1, n + 1, dtype=jnp.float32) / n)


def dilated_window_group(q, k, v, slopes, window, dilation):
    b, s, h, dh = q.shape
    n_steps = window // dilation
    nb = -(-s // (dilation * BLOCK))
    sub_len = nb * BLOCK
    s_pad = sub_len * dilation

    def to_blocks(t):
        t = jnp.pad(t, ((0, 0), (0, s_pad - s), (0, 0), (0, 0)))
        t = t.reshape(b, sub_len, dilation, h, dh)
        t = t.transpose(0, 2, 3, 1, 4)
        return t.reshape(b, dilation, h, nb, BLOCK, dh)

    qb, kb, vb = to_blocks(q), to_blocks(k), to_blocks(v)

    def with_prev(t):
        prev = jnp.pad(t[:, :, :, :-1], ((0, 0), (0, 0), (0, 0), (1, 0), (0, 0), (0, 0)))
        return jnp.concatenate([prev, t], axis=4)

    kk, vv = with_prev(kb), with_prev(vb)
    scores = jnp.einsum('brhnqd,brhnkd->brhnqk', qb, kk).astype(jnp.float32) / math.sqrt(dh)
    qi = jnp.arange(BLOCK)[:, None]
    kj = jnp.arange(2 * BLOCK)[None, :]
    steps = qi + BLOCK - kj
    key_sub_idx = jnp.arange(nb)[:, None, None] * BLOCK + kj[None] - BLOCK
    valid = (steps >= 0) & (steps <= n_steps) & (key_sub_idx >= 0)
    bias = -slopes[:, None, None].astype(jnp.float32) * (steps * dilation).astype(jnp.float32)
    logits = scores + bias[None, None, :, None]
    logits = jnp.where(valid[None, None, None], logits, NEG_INF)
    lse = jax.nn.logsumexp(logits, axis=-1)
    p = jnp.exp(logits - lse[..., None])
    o = jnp.einsum('brhnqk,brhnkd->brhnqd', p.astype(v.dtype), vv)

    def from_blocks(t):
        extra = t.shape[5:]
        t = t.reshape((b, dilation, h, sub_len) + extra)
        t = jnp.moveaxis(t, 3, 1)
        t = t.reshape((b, s_pad, h) + extra)
        return t[:, :s]

    return from_blocks(o), from_blocks(lse)


def dilated_attention(q, k, v):
    b, s = q.shape[:2]
    slopes = alibi_slopes(N_DIL_HEADS)
    outs, lses = [], []
    for g, (window, dilation) in enumerate(DIL_GROUPS):
        sl = slice(g * DIL_HEADS_PER_GROUP, (g + 1) * DIL_HEADS_PER_GROUP)
        o, l = dilated_window_group(q[:, :, sl], k[:, :, sl], v[:, :, sl], slopes[sl], window, dilation)
        outs.append(o)
        lses.append(l)
    o_all = jnp.stack(outs, axis=0).astype(jnp.float32)
    w = jax.nn.softmax(jnp.stack(lses, axis=0), axis=0)
    out = jnp.sum(w[..., None] * o_all, axis=0).astype(q.dtype)
    return out.reshape(b, s, DIL_OUT_WIDTH)


def stick_breaking_attention(q, k, v):
    b, s, h, dh = q.shape
    nb = s // BLOCK
    scale = 1.0 / math.sqrt(dh)
    kt = k.transpose(0, 2, 1, 3)
    vt = v.transpose(0, 2, 1, 3)
    q_blocks = q.transpose(0, 2, 1, 3).reshape(b, h, nb, BLOCK, dh).transpose(2, 0, 1, 3, 4)
    key_pos = jnp.arange(s)

    def one_block(args):
        i, q_blk = args
        z = jnp.einsum('bhqd,bhkd->bhqk', q_blk, kt).astype(jnp.float32) * scale
        q_pos = i * BLOCK + jnp.arange(BLOCK)
        causal = key_pos[None, :] < q_pos[:, None]
        log_one_minus = jnp.where(causal, jax.nn.log_sigmoid(-z), 0.0)
        suffix = lax.cumsum(log_one_minus, axis=3, reverse=True) - log_one_minus
        log_a = jax.nn.log_sigmoid(z) + suffix
        a = jnp.where(causal, jnp.exp(log_a), 0.0)
        return jnp.einsum('bhqk,bhkd->bhqd', a.astype(vt.dtype), vt)

    o_blocks = lax.map(one_block, (jnp.arange(nb), q_blocks))
    return o_blocks.transpose(1, 0, 3, 2, 4).reshape(b, s, h * dh)


def _fwd_setup_inputs(seed: int = 0) -> dict:
    key = jax.random.key(seed)
    ks = jax.random.split(key, 12)
    f32 = jnp.float32
    x = jax.random.normal(ks[0], (BATCH, SEQ, D_MODEL), f32)
    norm_mix_g = 1.0 + 0.02 * jax.random.normal(ks[1], (DEPTH, D_MODEL), f32)
    w_in = jax.random.normal(ks[2], (DEPTH, D_MODEL, IN_COLS), f32) * D_MODEL ** -0.5
    b_gate = 0.02 * jax.random.normal(ks[3], (DEPTH, 2 * D_MODEL), f32)
    w_up_dil = jax.random.normal(ks[4], (DEPTH, DIL_OUT_WIDTH, D_MODEL), f32) * DIL_OUT_WIDTH ** -0.5
    w_up_sb = jax.random.normal(ks[5], (DEPTH, SB_WIDTH, D_MODEL), f32) * SB_WIDTH ** -0.5
    w_out = jax.random.normal(ks[6], (DEPTH, D_MODEL, D_MODEL), f32) * D_MODEL ** -0.5
    norm_mlp_g = 1.0 + 0.02 * jax.random.normal(ks[7], (DEPTH, D_MODEL), f32)
    w_mlp_in = jax.random.normal(ks[8], (DEPTH, D_MODEL, D_FF), f32) * D_MODEL ** -0.5
    w_mlp_out = jax.random.normal(ks[9], (DEPTH, D_FF, D_MODEL), f32) * D_FF ** -0.5
    norm_final_g = 1.0 + 0.02 * jax.random.normal(ks[10], (D_MODEL,), f32)
    return {"x": x, "norm_mix_g": norm_mix_g, "w_in": w_in, "b_gate": b_gate,
            "w_up_dil": w_up_dil, "w_up_sb": w_up_sb, "w_out": w_out,
            "norm_mlp_g": norm_mlp_g, "w_mlp_in": w_mlp_in, "w_mlp_out": w_mlp_out,
            "norm_final_g": norm_final_g}


def _fwd_reference(x, norm_mix_g, w_in, b_gate, w_up_dil, w_up_sb, w_out,
              norm_mlp_g, w_mlp_in, w_mlp_out, norm_final_g):
    b, s, _ = x.shape
    for layer in range(DEPTH):
        h = rmsnorm(x, norm_mix_g[layer])
        proj = h @ w_in[layer]
        q_a, k_a, v_a, q_b, k_b, v_b, gl_a, gl_b = jnp.split(proj, SPLITS, axis=-1)
        heads_a = lambda t: t.reshape(b, s, N_DIL_HEADS, HEAD_DIM)
        heads_b = lambda t: t.reshape(b, s, N_SB_HEADS, HEAD_DIM)
        o_a = dilated_attention(heads_a(q_a), heads_a(k_a), heads_a(v_a))
        o_b = stick_breaking_attention(heads_b(q_b), heads_b(k_b), heads_b(v_b))
        bg_a, bg_b = jnp.split(b_gate[layer], 2)
        g_a = jax.nn.sigmoid(gl_a + bg_a)
        g_b = jax.nn.sigmoid(gl_b + bg_b)
        merged = g_a * (o_a @ w_up_dil[layer]) + g_b * (o_b @ w_up_sb[layer])
        x = x + merged @ w_out[layer]
        h2 = rmsnorm(x, norm_mlp_g[layer])
        x = x + jnp.square(jax.nn.relu(h2 @ w_mlp_in[layer])) @ w_mlp_out[layer]
    return rmsnorm(x, norm_final_g)


import jax as _jax
import jax.numpy as _jnp

TWIN_FORMAT = 'train_step'
FWD_PARAMS = ['x', 'norm_mix_g', 'w_in', 'b_gate', 'w_up_dil', 'w_up_sb', 'w_out', 'norm_mlp_g', 'w_mlp_in', 'w_mlp_out', 'norm_final_g']
TWIN_WEIGHTS = ['norm_mix_g', 'w_in', 'b_gate', 'w_up_dil', 'w_up_sb', 'w_out', 'norm_mlp_g', 'w_mlp_in', 'w_mlp_out', 'norm_final_g']
TWIN_DIFF_INPUT = 'x'
TWIN_INPUTS = ['x', 'norm_mix_g', 'w_in', 'b_gate', 'w_up_dil', 'w_up_sb', 'w_out', 'norm_mlp_g', 'w_mlp_in', 'w_mlp_out', 'norm_final_g', 'loss_target', 'm_norm_mix_g', 'm_w_in', 'm_b_gate', 'm_w_up_dil', 'm_w_up_sb', 'm_w_out', 'm_norm_mlp_g', 'm_w_mlp_in', 'm_w_mlp_out', 'm_norm_final_g', 'v_norm_mix_g', 'v_w_in', 'v_b_gate', 'v_w_up_dil', 'v_w_up_sb', 'v_w_out', 'v_norm_mlp_g', 'v_w_mlp_in', 'v_w_mlp_out', 'v_norm_final_g']
TWIN_OUTPUTS = ['loss', 'grad_x', 'grad_norm_mix_g', 'grad_w_in', 'grad_b_gate', 'grad_w_up_dil', 'grad_w_up_sb', 'grad_w_out', 'grad_norm_mlp_g', 'grad_w_mlp_in', 'grad_w_mlp_out', 'grad_norm_final_g', 'delta_norm_mix_g', 'delta_w_in', 'delta_b_gate', 'delta_w_up_dil', 'delta_w_up_sb', 'delta_w_out', 'delta_norm_mlp_g', 'delta_w_mlp_in', 'delta_w_mlp_out', 'delta_norm_final_g', 'new_m_norm_mix_g', 'new_m_w_in', 'new_m_b_gate', 'new_m_w_up_dil', 'new_m_w_up_sb', 'new_m_w_out', 'new_m_norm_mlp_g', 'new_m_w_mlp_in', 'new_m_w_mlp_out', 'new_m_norm_final_g', 'new_v_norm_mix_g', 'new_v_w_in', 'new_v_b_gate', 'new_v_w_up_dil', 'new_v_w_up_sb', 'new_v_w_out', 'new_v_norm_mlp_g', 'new_v_w_mlp_in', 'new_v_w_mlp_out', 'new_v_norm_final_g']
TWIN_LEAF_KINDS = {'loss': 'loss', 'grad_x': 'grad_x', 'grad_norm_mix_g': 'grad_w', 'grad_w_in': 'grad_w', 'grad_b_gate': 'grad_w', 'grad_w_up_dil': 'grad_w', 'grad_w_up_sb': 'grad_w', 'grad_w_out': 'grad_w', 'grad_norm_mlp_g': 'grad_w', 'grad_w_mlp_in': 'grad_w', 'grad_w_mlp_out': 'grad_w', 'grad_norm_final_g': 'grad_w', 'delta_norm_mix_g': 'delta_w', 'delta_w_in': 'delta_w', 'delta_b_gate': 'delta_w', 'delta_w_up_dil': 'delta_w', 'delta_w_up_sb': 'delta_w', 'delta_w_out': 'delta_w', 'delta_norm_mlp_g': 'delta_w', 'delta_w_mlp_in': 'delta_w', 'delta_w_mlp_out': 'delta_w', 'delta_norm_final_g': 'delta_w', 'new_m_norm_mix_g': 'new_m', 'new_m_w_in': 'new_m', 'new_m_b_gate': 'new_m', 'new_m_w_up_dil': 'new_m', 'new_m_w_up_sb': 'new_m', 'new_m_w_out': 'new_m', 'new_m_norm_mlp_g': 'new_m', 'new_m_w_mlp_in': 'new_m', 'new_m_w_mlp_out': 'new_m', 'new_m_norm_final_g': 'new_m', 'new_v_norm_mix_g': 'new_v', 'new_v_w_in': 'new_v', 'new_v_b_gate': 'new_v', 'new_v_w_up_dil': 'new_v', 'new_v_w_up_sb': 'new_v', 'new_v_w_out': 'new_v', 'new_v_norm_mlp_g': 'new_v', 'new_v_w_mlp_in': 'new_v', 'new_v_w_mlp_out': 'new_v', 'new_v_norm_final_g': 'new_v'}


def _forward(args):
    return _fwd_reference(*[args[k] for k in FWD_PARAMS])


def _output_shape():
    def fwd():
        inp = _fwd_setup_inputs(0)
        return _fwd_reference(*[inp[k] for k in FWD_PARAMS])
    out = _jax.eval_shape(fwd)
    return out.shape, out.dtype

N_MICROBATCH = 1
ADAM_LR = 0.001
ADAM_B1 = 0.9
ADAM_B2 = 0.999
ADAM_EPS = 1e-08
ADAM_WD = 0.01
ADAM_STEP = 10
PER_EXAMPLE_BATCH_AXIS = {'x': 0, 'loss_target': 0}
SHARED_INPUTS = []
_WEIGHT_DTYPES = {'norm_mix_g': _jnp.float32, 'w_in': _jnp.float32, 'b_gate': _jnp.float32, 'w_up_dil': _jnp.float32, 'w_up_sb': _jnp.float32, 'w_out': _jnp.float32, 'norm_mlp_g': _jnp.float32, 'w_mlp_in': _jnp.float32, 'w_mlp_out': _jnp.float32, 'norm_final_g': _jnp.float32}
MOMENT_SCALE = {'norm_mix_g': 1.384151e-01, 'w_in': 5.516005e-02, 'b_gate': 2.805899e-02, 'w_up_dil': 4.547418e-02, 'w_up_sb': 9.015832e-02, 'w_out': 9.938753e-02, 'norm_mlp_g': 2.222452e-01, 'w_mlp_in': 1.096580e-01, 'w_mlp_out': 2.349207e-01, 'norm_final_g': 6.449809e+01}


def _to_microbatches(a, axis):
    t = _jnp.moveaxis(a, axis, 0)
    t = t.reshape((N_MICROBATCH, t.shape[0] // N_MICROBATCH) + t.shape[1:])
    return _jnp.moveaxis(t, 1, axis + 1)


def setup_inputs(seed: int = 0) -> dict:
    inp = _fwd_setup_inputs(seed)
    key = _jax.random.fold_in(_jax.random.key(seed), 7919)
    shape, _ = _output_shape()
    out = dict(inp)
    out["loss_target"] = _jax.random.normal(_jax.random.fold_in(key, 0), shape, _jnp.float32)
    for i, name in enumerate(TWIN_WEIGHTS):
        w = inp[name].astype(_jnp.float32)
        if MOMENT_SCALE is None:
            s = _jnp.sqrt(_jnp.mean(_jnp.square(w)) + 1e-30)
        else:
            s = MOMENT_SCALE[name]
        km, kv = _jax.random.split(_jax.random.fold_in(key, i + 1))
        out[name] = w
        out["m_" + name] = s * _jax.random.normal(km, w.shape, _jnp.float32)
        out["v_" + name] = (s * s) * _jax.random.uniform(kv, w.shape, _jnp.float32, 0.5, 1.5)
    if N_MICROBATCH > 1:
        for name, axis in PER_EXAMPLE_BATCH_AXIS.items():
            out[name] = _to_microbatches(out[name], axis)
    return {'x': out['x'], 'norm_mix_g': out['norm_mix_g'], 'w_in': out['w_in'], 'b_gate': out['b_gate'], 'w_up_dil': out['w_up_dil'], 'w_up_sb': out['w_up_sb'], 'w_out': out['w_out'], 'norm_mlp_g': out['norm_mlp_g'], 'w_mlp_in': out['w_mlp_in'], 'w_mlp_out': out['w_mlp_out'], 'norm_final_g': out['norm_final_g'], 'loss_target': out['loss_target'], 'm_norm_mix_g': out['m_norm_mix_g'], 'm_w_in': out['m_w_in'], 'm_b_gate': out['m_b_gate'], 'm_w_up_dil': out['m_w_up_dil'], 'm_w_up_sb': out['m_w_up_sb'], 'm_w_out': out['m_w_out'], 'm_norm_mlp_g': out['m_norm_mlp_g'], 'm_w_mlp_in': out['m_w_mlp_in'], 'm_w_mlp_out': out['m_w_mlp_out'], 'm_norm_final_g': out['m_norm_final_g'], 'v_norm_mix_g': out['v_norm_mix_g'], 'v_w_in': out['v_w_in'], 'v_b_gate': out['v_b_gate'], 'v_w_up_dil': out['v_w_up_dil'], 'v_w_up_sb': out['v_w_up_sb'], 'v_w_out': out['v_w_out'], 'v_norm_mlp_g': out['v_norm_mlp_g'], 'v_w_mlp_in': out['v_w_mlp_in'], 'v_w_mlp_out': out['v_w_mlp_out'], 'v_norm_final_g': out['v_norm_final_g']}


def _loss(weights, diff, rest, loss_target):
    with _jax.named_scope("forward"):
        args = {**rest, TWIN_DIFF_INPUT: diff, **{k: w.astype(_WEIGHT_DTYPES[k]) for k, w in weights.items()}}
        y = _forward(args)
    with _jax.named_scope("loss_head"):
        err = _jnp.square(y.astype(_jnp.float32) - loss_target)
        return 0.5 * _jnp.sum(_jnp.mean(err, axis=-1)) if err.ndim else 0.5 * err


def _adamw(w, g, m, v):
    m = ADAM_B1 * m + (1.0 - ADAM_B1) * g
    v = ADAM_B2 * v + (1.0 - ADAM_B2) * _jnp.square(g)
    m_hat = m / (1.0 - ADAM_B1 ** ADAM_STEP)
    v_hat = v / (1.0 - ADAM_B2 ** ADAM_STEP)
    delta = -ADAM_LR * (m_hat / (_jnp.sqrt(v_hat) + ADAM_EPS) + ADAM_WD * w)
    return delta, m, v


def reference(x, norm_mix_g, w_in, b_gate, w_up_dil, w_up_sb, w_out, norm_mlp_g, w_mlp_in, w_mlp_out, norm_final_g, loss_target, m_norm_mix_g, m_w_in, m_b_gate, m_w_up_dil, m_w_up_sb, m_w_out, m_norm_mlp_g, m_w_mlp_in, m_w_mlp_out, m_norm_final_g, v_norm_mix_g, v_w_in, v_b_gate, v_w_up_dil, v_w_up_sb, v_w_out, v_norm_mlp_g, v_w_mlp_in, v_w_mlp_out, v_norm_final_g):
    given = dict(x=x, norm_mix_g=norm_mix_g, w_in=w_in, b_gate=b_gate, w_up_dil=w_up_dil, w_up_sb=w_up_sb, w_out=w_out, norm_mlp_g=norm_mlp_g, w_mlp_in=w_mlp_in, w_mlp_out=w_mlp_out, norm_final_g=norm_final_g, loss_target=loss_target, m_norm_mix_g=m_norm_mix_g, m_w_in=m_w_in, m_b_gate=m_b_gate, m_w_up_dil=m_w_up_dil, m_w_up_sb=m_w_up_sb, m_w_out=m_w_out, m_norm_mlp_g=m_norm_mlp_g, m_w_mlp_in=m_w_mlp_in, m_w_mlp_out=m_w_mlp_out, m_norm_final_g=m_norm_final_g, v_norm_mix_g=v_norm_mix_g, v_w_in=v_w_in, v_b_gate=v_b_gate, v_w_up_dil=v_w_up_dil, v_w_up_sb=v_w_up_sb, v_w_out=v_w_out, v_norm_mlp_g=v_norm_mlp_g, v_w_mlp_in=v_w_mlp_in, v_w_mlp_out=v_w_mlp_out, v_norm_final_g=v_norm_final_g)
    weights = {n: given[n] for n in TWIN_WEIGHTS}
    shared = {n: given[n] for n in SHARED_INPUTS}
    per_example = {n: given[n] for n in ['x']}
    grad_fn = _jax.value_and_grad(_loss, argnums=(0, 1))

    def one_microbatch(ex, loss_target):
        ex = dict(ex)
        diff = ex.pop(TWIN_DIFF_INPUT)
        return grad_fn(weights, diff, {**shared, **ex}, loss_target)

    if N_MICROBATCH == 1:
        loss, (grad_w, grad_x) = one_microbatch(per_example, given["loss_target"])
    else:
        def body(carry, xs):
            loss_sum, grad_sum = carry
            l_k, (gw_k, gx_k) = one_microbatch(xs[0], xs[1])
            with _jax.named_scope("update"):
                return (loss_sum + l_k, _jax.tree.map(_jnp.add, grad_sum, gw_k)), gx_k

        init = (_jnp.zeros((), _jnp.float32), _jax.tree.map(_jnp.zeros_like, weights))
        (loss, grad_w), grad_x = _jax.lax.scan(body, init, (per_example, given["loss_target"]))
    with _jax.named_scope("update"):
        delta_w, new_m, new_v = {}, {}, {}
        for n in TWIN_WEIGHTS:
            delta_w[n], new_m[n], new_v[n] = _adamw(weights[n], grad_w[n], given["m_" + n], given["v_" + n])
    return (loss, grad_x, *[grad_w[n] for n in TWIN_WEIGHTS], *[delta_w[n] for n in TWIN_WEIGHTS],
            *[new_m[n] for n in TWIN_WEIGHTS], *[new_v[n] for n in TWIN_WEIGHTS])
```

```python
import functools
import math

import jax
import jax.numpy as jnp
from jax import lax
from jax.experimental import pallas as pl
from jax.experimental.pallas import tpu as pltpu

_pcall = pl.pallas_call

F32 = jnp.float32
BF16 = jnp.bfloat16

D_MODEL = 1024
HEAD_DIM = 64
DIL_GROUPS = ((128, 1), (512, 4), (2048, 16))
DIL_HEADS_PER_GROUP = 4
N_DIL_HEADS = 12
N_SB_HEADS = 8
DIL_WIDTH = 768
DIL_OUT_WIDTH = 256
SB_WIDTH = 512
D_FF = 4096
BLOCK = 128
RMS_EPS = 1e-6
NEG_INF = -1e30
QKV_COLS = 3 * DIL_WIDTH + 3 * SB_WIDTH
IN_COLS = QKV_COLS + 2 * D_MODEL
N_DEV = 8

ADAM_LR = 0.001
ADAM_B1 = 0.9
ADAM_B2 = 0.999
ADAM_EPS = 1e-08
ADAM_WD = 0.01
ADAM_STEP = 10

VMEM_LIMIT = 56 * 1024 * 1024
SB_TILE = 256
LANES = 128

_ARB = pltpu.ARBITRARY


def _cparams(n_axes, **kw):
    return pltpu.CompilerParams(dimension_semantics=(_ARB,) * n_axes, vmem_limit_bytes=VMEM_LIMIT, **kw)


def _dot(a, b):
    return jnp.dot(a, b, preferred_element_type=F32)


def _dot_nt(a, b):
    return lax.dot_general(a, b, (((1,), (1,)), ((), ())), preferred_element_type=F32)


def _dot_tn(a, b):
    return lax.dot_general(a, b, (((0,), (0,)), ((), ())), preferred_element_type=F32)


def _split_hi_lo(x):
    hi = x.astype(BF16)
    lo = (x - hi.astype(F32)).astype(BF16)
    return hi, lo


def _dot_hi_lo(x, m):
    hi, lo = _split_hi_lo(x)
    return _dot(hi, m) + _dot(lo, m)


def _sigmoid(x):
    return 1.0 / (1.0 + jnp.exp(-x))


def _mm(name, a, b, *, ta=False, tb=False, out_dtype, tm, tn, tk, epilogue=None, extra=None):
    m = a.shape[1] if ta else a.shape[0]
    k = a.shape[0] if ta else a.shape[1]
    n = b.shape[0] if tb else b.shape[1]
    assert (b.shape[1] if tb else b.shape[0]) == k
    tm, tn, tk = min(tm, m), min(tn, n), min(tk, k)
    assert m % tm == 0 and n % tn == 0 and k % tk == 0, (name, m, n, k, tm, tn, tk)
    nk = k // tk
    dn = (((0 if ta else 1,), (1 if tb else 0,)), ((), ()))

    def body(*refs):
        if extra is not None:
            a_ref, b_ref, e_ref, o_ref = refs[:4]
        else:
            a_ref, b_ref, o_ref = refs[:3]
            e_ref = None

        def finish(r):
            if epilogue is not None:
                r = epilogue(r, None if e_ref is None else e_ref[...])
            o_ref[...] = r.astype(out_dtype)

        part = lax.dot_general(a_ref[...].astype(BF16), b_ref[...].astype(BF16), dn, preferred_element_type=F32)
        if nk == 1:
            finish(part)
        else:
            acc_ref = refs[-1]
            kk = pl.program_id(2)

            @pl.when(kk == 0)
            def _():
                acc_ref[...] = part

            @pl.when(kk > 0)
            def _():
                acc_ref[...] += part

            @pl.when(kk == nk - 1)
            def _():
                finish(acc_ref[...])

    a_spec = pl.BlockSpec((tk, tm), lambda i, j, kk: (kk, i)) if ta else pl.BlockSpec((tm, tk), lambda i, j, kk: (i, kk))
    b_spec = pl.BlockSpec((tn, tk), lambda i, j, kk: (j, kk)) if tb else pl.BlockSpec((tk, tn), lambda i, j, kk: (kk, j))
    o_spec = pl.BlockSpec((tm, tn), lambda i, j, kk: (i, j))
    in_specs = [a_spec, b_spec]
    args = [a, b]
    if extra is not None:
        in_specs.append(o_spec)
        args.append(extra)
    return _pcall(
        body,
        name=name,
        grid=(m // tm, n // tn, nk),
        in_specs=in_specs,
        out_specs=o_spec,
        out_shape=jax.ShapeDtypeStruct((m, n), out_dtype),
        scratch_shapes=[pltpu.VMEM((tm, tn), F32)] if nk > 1 else [],
        compiler_params=_cparams(3),
    )(*args)


def _rowk(name, *, a=None, w=None, nt=False, tm, tk=None, rows=(), consts=(), row_outs=(), acc_outs=(), epilogue):
    has_mm = a is not None
    m = a.shape[0] if has_mm else rows[0].shape[0]
    assert m % tm == 0
    nm = m // tm
    if has_mm:
        k = a.shape[1]
        n = w.shape[0] if nt else w.shape[1]
        tk = min(tk, k)
        assert k % tk == 0
        nk = k // tk
    else:
        nk = 1
    n_rows, n_consts, n_ro, n_ao = len(rows), len(consts), len(row_outs), len(acc_outs)

    def body(*refs):
        pos = 0
        if has_mm:
            a_ref, w_ref = refs[0], refs[1]
            pos = 2
        row_refs = refs[pos:pos + n_rows]
        pos += n_rows
        const_refs = refs[pos:pos + n_consts]
        pos += n_consts
        ro_refs = refs[pos:pos + n_ro]
        pos += n_ro
        ao_refs = refs[pos:pos + n_ao]
        pos += n_ao
        i = pl.program_id(0)
        kk = pl.program_id(1)

        def finish(acc):
            ro_vals, ao_vals = epilogue(acc, [r[...] for r in row_refs], [c[...] for c in const_refs])
            for r, v in zip(ro_refs, ro_vals):
                r[...] = v.astype(r.dtype)
            for r, v in zip(ao_refs, ao_vals):

                @pl.when(i == 0)
                def _(r=r, v=v):
                    r[...] = v

                @pl.when(i > 0)
                def _(r=r, v=v):
                    r[...] += v

        if not has_mm:
            finish(None)
            return
        if nt:
            part = _dot_nt(a_ref[...].astype(BF16), w_ref[...])
        else:
            part = _dot(a_ref[...].astype(BF16), w_ref[...])
        if nk == 1:
            finish(part)
        else:
            acc_ref = refs[pos]

            @pl.when(kk == 0)
            def _():
                acc_ref[...] = part

            @pl.when(kk > 0)
            def _():
                acc_ref[...] += part

            @pl.when(kk == nk - 1)
            def _():
                finish(acc_ref[...])

    in_specs, args = [], []
    if has_mm:
        in_specs.append(pl.BlockSpec((tm, tk), lambda i, kk: (i, kk)))
        in_specs.append(pl.BlockSpec((n, tk), lambda i, kk: (0, kk)) if nt else pl.BlockSpec((tk, n), lambda i, kk: (kk, 0)))
        args += [a, w]
    for r in rows:
        in_specs.append(pl.BlockSpec((tm, r.shape[1]), lambda i, kk: (i, 0)))
        args.append(r)
    for c in consts:
        in_specs.append(pl.BlockSpec(c.shape, lambda i, kk: (0,) * c.ndim))
        args.append(c)
    out_specs, out_shape = [], []
    for width, dt in row_outs:
        out_specs.append(pl.BlockSpec((tm, width), lambda i, kk: (i, 0)))
        out_shape.append(jax.ShapeDtypeStruct((m, width), dt))
    for width in acc_outs:
        out_specs.append(pl.BlockSpec((1, width), lambda i, kk: (0, 0)))
        out_shape.append(jax.ShapeDtypeStruct((1, width), F32))
    return _pcall(
        body,
        name=name,
        grid=(nm, nk),
        in_specs=in_specs,
        out_specs=out_specs,
        out_shape=out_shape,
        scratch_shapes=[pltpu.VMEM((tm, n), F32)] if (has_mm and nk > 1) else [],
        compiler_params=_cparams(2),
    )(*args)


def _rms_stats(x):
    r = lax.rsqrt(jnp.mean(x * x, axis=-1, keepdims=True) + RMS_EPS)
    return r, x * r


def _rms_bwd(dh, xh, r, g):
    gy = dh * g
    dx = r * (gy - xh * jnp.mean(gy * xh, axis=-1, keepdims=True))
    return dx, jnp.sum(dh * xh, axis=0, keepdims=True)


def _alibi_slope(head):
    return 2.0 ** (-8.0 * (head + 1) / N_DIL_HEADS)


def _dil_masks(i):
    qi = lax.broadcasted_iota(jnp.int32, (BLOCK, 2 * BLOCK), 0)
    kj = lax.broadcasted_iota(jnp.int32, (BLOCK, 2 * BLOCK), 1)
    steps = qi + BLOCK - kj
    valid = (steps >= 0) & (steps <= BLOCK) & ((kj >= BLOCK) | (i > 0))
    return steps.astype(F32), valid


def _dil_specs(group, dilation, nb, clamp):
    ncb = QKV_COLS // DIL_OUT_WIDTH

    def cur(col):
        return pl.BlockSpec((BLOCK, DIL_OUT_WIDTH), lambda r, i: (clamp(i), r * ncb + col))

    def prev(col):
        return pl.BlockSpec((BLOCK, DIL_OUT_WIDTH), lambda r, i: (jnp.maximum(clamp(i) - 1, 0), r * ncb + col))

    return [cur(group), cur(3 + group), prev(3 + group), cur(6 + group), prev(6 + group)]


def _dil_fwd(qkv, group):
    window, dilation = DIL_GROUPS[group]
    s = qkv.shape[0]
    sub = s // dilation
    nb = sub // BLOCK
    assert nb * BLOCK * dilation == s and window // dilation == BLOCK
    slopes = [_alibi_slope(group * DIL_HEADS_PER_GROUP + h) * dilation for h in range(DIL_HEADS_PER_GROUP)]

    def body(q_ref, kc_ref, kp_ref, vc_ref, vp_ref, o_ref, lse_ref):
        i = pl.program_id(1)
        q = q_ref[...]
        kk = jnp.concatenate([kp_ref[...], kc_ref[...]], axis=0)
        vv = jnp.concatenate([vp_ref[...], vc_ref[...]], axis=0)
        head_id = lax.broadcasted_iota(jnp.int32, (1, DIL_OUT_WIDTH), 1) // HEAD_DIM
        steps, valid = _dil_masks(i)
        o = jnp.zeros((BLOCK, DIL_OUT_WIDTH), F32)
        lse_all = jnp.zeros((BLOCK, DIL_OUT_WIDTH), F32)
        for h in range(DIL_HEADS_PER_GROUP):
            qm = jnp.where(head_id == h, q, jnp.zeros_like(q))
            logits = _dot_nt(qm, kk) * (1.0 / math.sqrt(HEAD_DIM)) - slopes[h] * steps
            logits = jnp.where(valid, logits, NEG_INF)
            mx = jnp.max(logits, axis=1, keepdims=True)
            e = jnp.exp(logits - mx)
            den = jnp.sum(e, axis=1, keepdims=True)
            lse = mx + jnp.log(den)
            p = jnp.exp(logits - lse)
            o = jnp.where(head_id == h, _dot(p.astype(BF16), vv), o)
            lse_all = jnp.where(head_id == h, lse, lse_all)
        o_ref[...] = o
        lse_ref[...] = lse_all

    qkv_v = qkv.reshape(sub, dilation * QKV_COLS)
    out_spec = pl.BlockSpec((BLOCK, DIL_OUT_WIDTH), lambda r, i: (i, r))
    o, lse = _pcall(
        body,
        name=f"dil_fwd_g{group}",
        grid=(dilation, nb),
        in_specs=_dil_specs(group, dilation, nb, lambda i: i),
        out_specs=[out_spec, out_spec],
        out_shape=[jax.ShapeDtypeStruct((sub, dilation * DIL_OUT_WIDTH), F32)] * 2,
        compiler_params=_cparams(2),
    )(qkv_v, qkv_v, qkv_v, qkv_v, qkv_v)
    return o.reshape(s, DIL_OUT_WIDTH), lse.reshape(s, DIL_OUT_WIDTH)


def _dil_bwd(qkv, do_g, lse_g, dterm_g, group):
    window, dilation = DIL_GROUPS[group]
    s = qkv.shape[0]
    sub = s // dilation
    nb = sub // BLOCK
    slopes = [_alibi_slope(group * DIL_HEADS_PER_GROUP + h) * dilation for h in range(DIL_HEADS_PER_GROUP)]
    scale = 1.0 / math.sqrt(HEAD_DIM)

    def body(q_ref, kc_ref, kp_ref, vc_ref, vp_ref, do_ref, lse_ref, dt_ref, dq_ref, dk_ref, dv_ref, ck_ref, cv_ref):
        i = pl.program_id(1)

        @pl.when(i == 0)
        def _():
            ck_ref[...] = jnp.zeros_like(ck_ref)
            cv_ref[...] = jnp.zeros_like(cv_ref)

        @pl.when(i < nb)
        def _():
            q = q_ref[...]
            do = do_ref[...]
            lse_all = lse_ref[...]
            dt_all = dt_ref[...]
            kk = jnp.concatenate([kp_ref[...], kc_ref[...]], axis=0)
            vv = jnp.concatenate([vp_ref[...], vc_ref[...]], axis=0)
            lane = lax.broadcasted_iota(jnp.int32, (1, DIL_OUT_WIDTH), 1)
            head_id = lane // HEAD_DIM
            steps, valid = _dil_masks(i)
            dq = jnp.zeros((BLOCK, DIL_OUT_WIDTH), F32)
            dkk = jnp.zeros((2 * BLOCK, DIL_OUT_WIDTH), F32)
            dvv = jnp.zeros((2 * BLOCK, DIL_OUT_WIDTH), F32)
            for h in range(DIL_HEADS_PER_GROUP):
                qm = jnp.where(head_id == h, q, jnp.zeros_like(q))
                dom = jnp.where(head_id == h, do, jnp.zeros_like(do))
                first = lane == h * HEAD_DIM
                lse = jnp.sum(jnp.where(first, lse_all, 0.0), axis=1, keepdims=True)
                dt = jnp.sum(jnp.where(first, dt_all, 0.0), axis=1, keepdims=True)
                logits = _dot_nt(qm, kk) * scale - slopes[h] * steps
                p = jnp.where(valid, jnp.exp(jnp.where(valid, logits, NEG_INF) - lse), 0.0)
                dp = _dot_nt(dom, vv)
                ds = (p * (dp + dt) * scale).astype(BF16)
                dq = jnp.where(head_id == h, _dot(ds, kk), dq)
                dkk = dkk + _dot_tn(ds, qm)
                dvv = dvv + _dot_tn(p.astype(BF16), dom)
            dq_ref[...] = dq.astype(dq_ref.dtype)
            dk_ref[...] = (ck_ref[...] + dkk[:BLOCK]).astype(dk_ref.dtype)
            dv_ref[...] = (cv_ref[...] + dvv[:BLOCK]).astype(dv_ref.dtype)
            ck_ref[...] = dkk[BLOCK:]
            cv_ref[...] = dvv[BLOCK:]

        @pl.when(i == nb)
        def _():
            dk_ref[...] = ck_ref[...].astype(dk_ref.dtype)
            dv_ref[...] = cv_ref[...].astype(dv_ref.dtype)

    clamp = lambda i: jnp.minimum(i, nb - 1)
    qkv_v = qkv.reshape(sub, dilation * QKV_COLS)
    view = lambda t: t.reshape(sub, dilation * DIL_OUT_WIDTH)
    row_spec = pl.BlockSpec((BLOCK, DIL_OUT_WIDTH), lambda r, i: (clamp(i), r))
    late_spec = pl.BlockSpec((BLOCK, DIL_OUT_WIDTH), lambda r, i: (jnp.maximum(i - 1, 0), r))
    dq, dk, dv = _pcall(
        body,
        name=f"dil_bwd_g{group}",
        grid=(dilation, nb + 1),
        in_specs=_dil_specs(group, dilation, nb, clamp) + [row_spec, row_spec, row_spec],
        out_specs=[row_spec, late_spec, late_spec],
        out_shape=[jax.ShapeDtypeStruct((sub, dilation * DIL_OUT_WIDTH), BF16)] * 3,
        scratch_shapes=[pltpu.VMEM((BLOCK, DIL_OUT_WIDTH), F32)] * 2,
        compiler_params=_cparams(2),
    )(qkv_v, qkv_v, qkv_v, qkv_v, qkv_v, view(do_g), view(lse_g), view(dterm_g))
    return dq.reshape(s, DIL_OUT_WIDTH), dk.reshape(s, DIL_OUT_WIDTH), dv.reshape(s, DIL_OUT_WIDTH)


def _head_block_ones():
    r = lax.broadcasted_iota(jnp.int32, (DIL_OUT_WIDTH, DIL_OUT_WIDTH), 0) // HEAD_DIM
    c = lax.broadcasted_iota(jnp.int32, (DIL_OUT_WIDTH, DIL_OUT_WIDTH), 1) // HEAD_DIM
    return jnp.where(r == c, 1.0, 0.0).astype(BF16)


def _dil_mix_weights(l0, l1, l2):
    mx = jnp.maximum(jnp.maximum(l0, l1), l2)
    e0, e1, e2 = jnp.exp(l0 - mx), jnp.exp(l1 - mx), jnp.exp(l2 - mx)
    inv = 1.0 / (e0 + e1 + e2)
    return e0 * inv, e1 * inv, e2 * inv


def _dil_mix_fwd(os_, lses, tm):
    def epi(_, rows, consts):
        o0, o1, o2, l0, l1, l2 = rows
        w0, w1, w2 = _dil_mix_weights(l0, l1, l2)
        return [w0 * o0 + w1 * o1 + w2 * o2], []

    (o_a,) = _rowk("dil_mix_fwd", tm=tm, rows=list(os_) + list(lses), row_outs=[(DIL_OUT_WIDTH, BF16)], epilogue=epi)
    return o_a


def _dil_mix_bwd(do_a, os_, lses, tm):
    def epi(_, rows, consts):
        do, o0, o1, o2, l0, l1, l2 = rows
        do = do.astype(F32)
        w0, w1, w2 = _dil_mix_weights(l0, l1, l2)
        mixed = w0 * o0 + w1 * o1 + w2 * o2
        tot = _dot_hi_lo(do * mixed, _head_block_ones())
        return [w0 * do, w1 * do, w2 * do, -w0 * tot, -w1 * tot, -w2 * tot], []

    return _rowk(
        "dil_mix_bwd", tm=tm, rows=[do_a] + list(os_) + list(lses),
        row_outs=[(DIL_OUT_WIDTH, BF16)] * 3 + [(DIL_OUT_WIDTH, F32)] * 3, epilogue=epi)


_SB_Q0 = 3 * DIL_WIDTH // LANES
_SB_K0 = _SB_Q0 + SB_WIDTH // LANES
_SB_V0 = _SB_K0 + SB_WIDTH // LANES


def _tri(t, op):
    r = lax.broadcasted_iota(jnp.int32, (t, t), 0)
    c = lax.broadcasted_iota(jnp.int32, (t, t), 1)
    return jnp.where(op(r, c), 1.0, 0.0).astype(BF16)


def _sb_logs(z):
    e = jnp.exp(-jnp.abs(z))
    sp = jnp.maximum(z, 0.0) + jnp.log(1.0 + e)
    return -sp, z - sp, e


def _sb_fwd(qkv):
    s = qkv.shape[0]
    t = SB_TILE
    nq = s // t
    n_pairs = SB_WIDTH // LANES

    def body(q_ref, k_ref, v_ref, o_ref, tot_ref):
        i = pl.program_id(1)
        q = q_ref[...] * (1.0 / math.sqrt(HEAD_DIM))
        lane_hi = lax.broadcasted_iota(jnp.int32, (1, LANES), 1) // HEAD_DIM
        later = _tri(t, lambda r, c: r > c)
        row = lax.broadcasted_iota(jnp.int32, (t, t), 0)
        col = lax.broadcasted_iota(jnp.int32, (t, t), 1)
        causal = col < row
        out = jnp.zeros((t, LANES), F32)
        tot = jnp.zeros((t, LANES), F32)
        for hh in range(2):
            qm = jnp.where(lane_hi == hh, q, jnp.zeros_like(q))

            def tile(j, c, acc, diag):
                off = pl.multiple_of(j * t, t)
                kj = k_ref[pl.ds(off, t), :]
                vj = v_ref[pl.ds(off, t), :]
                lneg, lpos, _ = _sb_logs(_dot_nt(qm, kj))
                if diag:
                    lneg = jnp.where(causal, lneg, 0.0)
                loga = lpos + _dot_hi_lo(lneg, later) + c
                a = jnp.exp(loga)
                if diag:
                    a = jnp.where(causal, a, 0.0)
                acc = acc + _dot(a.astype(BF16), vj)
                c = c + jnp.sum(lneg, axis=1, keepdims=True)
                return c, acc

            c, acc = tile(i, jnp.zeros((t, 1), F32), jnp.zeros((t, LANES), F32), True)
            c, acc = lax.fori_loop(0, i, lambda jj, ca: tile(i - 1 - jj, ca[0], ca[1], False), (c, acc))
            out = jnp.where(lane_hi == hh, acc, out)
            tot = jnp.where(lane_hi == hh, c, tot)
        o_ref[...] = out.astype(o_ref.dtype)
        tot_ref[...] = tot

    o, tot = _pcall(
        body,
        name="sb_fwd",
        grid=(n_pairs, nq),
        in_specs=[
            pl.BlockSpec((t, LANES), lambda p, i: (i, _SB_Q0 + p)),
            pl.BlockSpec((s, LANES), lambda p, i: (0, _SB_K0 + p)),
            pl.BlockSpec((s, LANES), lambda p, i: (0, _SB_V0 + p)),
        ],
        out_specs=[pl.BlockSpec((t, LANES), lambda p, i: (i, p))] * 2,
        out_shape=[jax.ShapeDtypeStruct((s, SB_WIDTH), BF16), jax.ShapeDtypeStruct((s, SB_WIDTH), F32)],
        compiler_params=_cparams(2),
    )(qkv, qkv, qkv)
    return o, tot


def _sb_bwd(qkv, do_b, tot_b):
    s = qkv.shape[0]
    t = SB_TILE
    nq = s // t
    n_pairs = SB_WIDTH // LANES
    scale = 1.0 / math.sqrt(HEAD_DIM)

    def body(q_ref, k_ref, v_ref, do_ref, tot_ref, dq_ref, dk_ref, dv_ref):
        i = pl.program_id(1)

        @pl.when(i == 0)
        def _():
            dk_ref[...] = jnp.zeros_like(dk_ref)
            dv_ref[...] = jnp.zeros_like(dv_ref)

        q = q_ref[...] * scale
        do = do_ref[...]
        tot_all = tot_ref[...]
        lane = lax.broadcasted_iota(jnp.int32, (1, LANES), 1)
        lane_hi = lane // HEAD_DIM
        upto = _tri(t, lambda r, c: r <= c)
        before = _tri(t, lambda r, c: r < c)
        row = lax.broadcasted_iota(jnp.int32, (t, t), 0)
        col = lax.broadcasted_iota(jnp.int32, (t, t), 1)
        causal = col < row
        dq = jnp.zeros((t, LANES), F32)
        for hh in range(2):
            qm = jnp.where(lane_hi == hh, q, jnp.zeros_like(q))
            dom = jnp.where(lane_hi == hh, do, jnp.zeros_like(do))
            tot = jnp.sum(jnp.where(lane == hh * HEAD_DIM, tot_all, 0.0), axis=1, keepdims=True)

            def tile(j, cl, cg, acc, diag):
                off = pl.multiple_of(j * t, t)
                kj = k_ref[pl.ds(off, t), :]
                vj = v_ref[pl.ds(off, t), :]
                z = _dot_nt(qm, kj)
                lneg, lpos, e = _sb_logs(z)
                if diag:
                    lneg = jnp.where(causal, lneg, 0.0)
                loga = lpos + (tot - cl - _dot_hi_lo(lneg, upto))
                a = jnp.exp(loga)
                if diag:
                    a = jnp.where(causal, a, 0.0)
                g = a * _dot_nt(dom, vj)
                pref = cg + _dot_hi_lo(g, before)
                rcp = 1.0 / (1.0 + e)
                sig = jnp.where(z >= 0, rcp, e * rcp)
                dz = g * (1.0 - sig) - pref * sig
                if diag:
                    dz = jnp.where(causal, dz, 0.0)
                dzb = dz.astype(BF16)
                acc = acc + _dot(dzb, kj)
                dk_ref[pl.ds(off, t), :] += _dot_tn(dzb, qm)
                dv_ref[pl.ds(off, t), :] += _dot_tn(a.astype(BF16), dom)
                cl = cl + jnp.sum(lneg, axis=1, keepdims=True)
                cg = cg + jnp.sum(g, axis=1, keepdims=True)
                return cl, cg, acc

            zero = jnp.zeros((t, 1), F32)
            cl, cg, acc = lax.fori_loop(
                0, i, lambda j, ca: tile(j, ca[0], ca[1], ca[2], False), (zero, zero, jnp.zeros((t, LANES), F32)))
            cl, cg, acc = tile(i, cl, cg, acc, True)
            dq = jnp.where(lane_hi == hh, acc, dq)
        dq_ref[...] = (dq * scale).astype(dq_ref.dtype)

    row_spec = pl.BlockSpec((t, LANES), lambda p, i: (i, p))
    full_spec = pl.BlockSpec((s, LANES), lambda p, i: (0, p))
    return _pcall(
        body,
        name="sb_bwd",
        grid=(n_pairs, nq),
        in_specs=[
            pl.BlockSpec((t, LANES), lambda p, i: (i, _SB_Q0 + p)),
            pl.BlockSpec((s, LANES), lambda p, i: (0, _SB_K0 + p)),
            pl.BlockSpec((s, LANES), lambda p, i: (0, _SB_V0 + p)),
            row_spec, row_spec,
        ],
        out_specs=[row_spec, full_spec, full_spec],
        out_shape=[jax.ShapeDtypeStruct((s, SB_WIDTH), BF16), jax.ShapeDtypeStruct((s, SB_WIDTH), F32),
                   jax.ShapeDtypeStruct((s, SB_WIDTH), F32)],
        compiler_params=_cparams(2),
    )(qkv, qkv, qkv, do_b, tot_b)


def _gates(gl, bg):
    return _sigmoid(gl[:, :D_MODEL] + bg[:, :D_MODEL]), _sigmoid(gl[:, D_MODEL:] + bg[:, D_MODEL:])


def _mixer_fwd(o_a, o_b, gl, x0, bg, g2, w_ud, w_us, w_out, tm):
    def epi(_, rows, consts):
        oa, ob, glv, x = rows
        bgv, g2v, wud, wus, wout = consts
        ga, gb = _gates(glv, bgv)
        merged = ga * _dot(oa, wud) + gb * _dot(ob, wus)
        x1 = x + _dot(merged.astype(BF16), wout)
        r, xh = _rms_stats(x1)
        return [x1, xh * g2v], []

    return _rowk("mixer_fwd", tm=tm, rows=[o_a, o_b, gl, x0], consts=[bg, g2, w_ud, w_us, w_out],
                 row_outs=[(D_MODEL, F32), (D_MODEL, BF16)], epilogue=epi)


def _mixer_bwd(dx1, o_a, o_b, gl, bg, w_ud, w_us, w_out, tm):
    s = dx1.shape[0]
    nm = s // tm

    def body(dx_ref, oa_ref, ob_ref, gl_ref, bg_ref, wud_ref, wus_ref, wout_ref,
             doa_ref, dob_ref, dgl_ref, gwout_ref, gwud_ref, gwus_ref, gbg_ref):
        i = pl.program_id(0)
        dxb = dx_ref[...].astype(BF16)
        oa, ob = oa_ref[...], ob_ref[...]
        ga, gb = _gates(gl_ref[...], bg_ref[...])
        ua, ub = _dot(oa, wud_ref[...]), _dot(ob, wus_ref[...])
        merged = (ga * ua + gb * ub).astype(BF16)
        dm = _dot_nt(dxb, wout_ref[...])
        dua = (dm * ga).astype(BF16)
        dub = (dm * gb).astype(BF16)
        dgla = dm * ua * ga * (1.0 - ga)
        dglb = dm * ub * gb * (1.0 - gb)
        doa_ref[...] = _dot_nt(dua, wud_ref[...]).astype(doa_ref.dtype)
        dob_ref[...] = _dot_nt(dub, wus_ref[...]).astype(dob_ref.dtype)
        dgl_ref[:, :D_MODEL] = dgla.astype(dgl_ref.dtype)
        dgl_ref[:, D_MODEL:] = dglb.astype(dgl_ref.dtype)
        parts = [(gwout_ref, _dot_tn(merged, dxb)), (gwud_ref, _dot_tn(oa, dua)), (gwus_ref, _dot_tn(ob, dub))]
        for r, v in parts:

            @pl.when(i == 0)
            def _(r=r, v=v):
                r[...] = v

            @pl.when(i > 0)
            def _(r=r, v=v):
                r[...] += v

        sa = jnp.sum(dgla, axis=0, keepdims=True)
        sb = jnp.sum(dglb, axis=0, keepdims=True)

        @pl.when(i == 0)
        def _():
            gbg_ref[:, :D_MODEL] = sa
            gbg_ref[:, D_MODEL:] = sb

        @pl.when(i > 0)
        def _():
            gbg_ref[:, :D_MODEL] += sa
            gbg_ref[:, D_MODEL:] += sb

    row = lambda w: pl.BlockSpec((tm, w), lambda i: (i, 0))
    full = lambda a: pl.BlockSpec(a.shape, lambda i: (0, 0))
    fshape = lambda r, c: jax.ShapeDtypeStruct((r, c), F32)
    return _pcall(
        body,
        name="mixer_bwd",
        grid=(nm,),
        in_specs=[row(D_MODEL), row(DIL_OUT_WIDTH), row(SB_WIDTH), row(2 * D_MODEL),
                  full(bg), full(w_ud), full(w_us), full(w_out)],
        out_specs=[row(DIL_OUT_WIDTH), row(SB_WIDTH), row(2 * D_MODEL),
                   pl.BlockSpec((D_MODEL, D_MODEL), lambda i: (0, 0)),
                   pl.BlockSpec((DIL_OUT_WIDTH, D_MODEL), lambda i: (0, 0)),
                   pl.BlockSpec((SB_WIDTH, D_MODEL), lambda i: (0, 0)),
                   pl.BlockSpec((1, 2 * D_MODEL), lambda i: (0, 0))],
        out_shape=[jax.ShapeDtypeStruct((s, DIL_OUT_WIDTH), BF16), jax.ShapeDtypeStruct((s, SB_WIDTH), BF16),
                   jax.ShapeDtypeStruct((s, 2 * D_MODEL), BF16),
                   fshape(D_MODEL, D_MODEL), fshape(DIL_OUT_WIDTH, D_MODEL), fshape(SB_WIDTH, D_MODEL),
                   fshape(1, 2 * D_MODEL)],
        compiler_params=_cparams(1),
    )(dx1, o_a, o_b, gl, bg, w_ud, w_us, w_out)


_HBM = pl.BlockSpec(memory_space=pltpu.HBM)
_MESH = pl.DeviceIdType.MESH


def _all_gather(shard):
    rows, cols = shard.shape

    def body(x_ref, out_ref, send_sems, recv_sems, local_sem):
        x, y, c = lax.axis_index("x"), lax.axis_index("y"), lax.axis_index("c")
        me, sibling = (x, y, c), (x, y, 1 - c)
        chips = [(1 - x, y), (x, 1 - y), (1 - x, 1 - y)]

        def slot(px, py, pc):
            return out_ref.at[4 * px + 2 * py + pc]

        def copy(k, block, to, src=None):
            return pltpu.make_async_remote_copy(
                src_ref=slot(*block) if src is None else src, dst_ref=slot(*block),
                send_sem=send_sems.at[k], recv_sem=recv_sems.at[k], device_id=to, device_id_type=_MESH)

        mine = pltpu.make_async_copy(x_ref, slot(*me), local_sem)
        mine.start()
        first = [copy(0, me, sibling, src=x_ref)]
        first += [copy(1 + j, me, (*chip, c), src=x_ref) for j, chip in enumerate(chips)]
        for cp in first:
            cp.start()
        passed = [copy(4 + j, (*chip, c), sibling) for j, chip in enumerate(chips)]
        for j, chip in enumerate(chips):
            copy(1 + j, (*chip, c), me).wait_recv()
            passed[j].start()
        copy(0, sibling, me).wait_recv()
        for j, chip in enumerate(chips):
            copy(4 + j, (*chip, 1 - c), me).wait_recv()
        for cp in first + passed:
            cp.wait_send()
        mine.wait()

    return _pcall(
        body,
        name="all_gather_weights",
        in_specs=[_HBM],
        out_specs=_HBM,
        out_shape=jax.ShapeDtypeStruct((N_DEV, rows, cols), shard.dtype),
        scratch_shapes=[pltpu.SemaphoreType.DMA((7,)), pltpu.SemaphoreType.DMA((7,)), pltpu.SemaphoreType.DMA],
    )(shard)


def _exchange(chunks, vec):
    _, rows, cols = chunks.shape
    n = vec.shape[1]

    def body(g_ref, v_ref, og_ref, ov_ref, send_sems, recv_sems, local_sems):
        x, y, c = lax.axis_index("x"), lax.axis_index("y"), lax.axis_index("c")
        me = 4 * x + 2 * y + c
        own_g = pltpu.make_async_copy(g_ref.at[me], og_ref.at[me], local_sems.at[0])
        own_v = pltpu.make_async_copy(v_ref, ov_ref.at[me], local_sems.at[1])
        own_g.start()
        own_v.start()
        copies = []
        for k in range(1, N_DEV):
            px, py, pc = x ^ (k >> 2), y ^ ((k >> 1) & 1), c ^ (k & 1)
            peer = 4 * px + 2 * py + pc
            copies.append(pltpu.make_async_remote_copy(
                src_ref=g_ref.at[peer], dst_ref=og_ref.at[me], send_sem=send_sems.at[k - 1],
                recv_sem=recv_sems.at[k - 1], device_id=(px, py, pc), device_id_type=_MESH))
            copies.append(pltpu.make_async_remote_copy(
                src_ref=v_ref, dst_ref=ov_ref.at[me], send_sem=send_sems.at[6 + k],
                recv_sem=recv_sems.at[6 + k], device_id=(px, py, pc), device_id_type=_MESH))
        for cp in copies:
            cp.start()
        for cp in copies:
            cp.wait()
        own_g.wait()
        own_v.wait()

    return _pcall(
        body,
        name="exchange_grads",
        in_specs=[_HBM, _HBM],
        out_specs=[_HBM, _HBM],
        out_shape=[jax.ShapeDtypeStruct((N_DEV, rows, cols), chunks.dtype),
                   jax.ShapeDtypeStruct((N_DEV, 1, n), vec.dtype)],
        scratch_shapes=[pltpu.SemaphoreType.DMA((14,)), pltpu.SemaphoreType.DMA((14,)), pltpu.SemaphoreType.DMA((2,))],
    )(chunks, vec)


def _reduce_adamw(name, parts, w, m, v, tr):
    _, rows, cols = parts.shape
    tr = min(tr, rows)
    assert rows % tr == 0
    c1 = 1.0 / (1.0 - ADAM_B1 ** ADAM_STEP)
    c2 = 1.0 / (1.0 - ADAM_B2 ** ADAM_STEP)

    def body(p_ref, w_ref, m_ref, v_ref, g_out, d_out, m_out, v_out):
        g = p_ref[0].astype(F32)
        for d in range(1, N_DEV):
            g = g + p_ref[d].astype(F32)
        mn = ADAM_B1 * m_ref[...] + (1.0 - ADAM_B1) * g
        vn = ADAM_B2 * v_ref[...] + (1.0 - ADAM_B2) * (g * g)
        g_out[...] = g
        m_out[...] = mn
        v_out[...] = vn
        d_out[...] = -ADAM_LR * ((mn * c1) / (jnp.sqrt(vn * c2) + ADAM_EPS) + ADAM_WD * w_ref[...])

    spec = pl.BlockSpec((tr, cols), lambda i: (i, 0))
    return _pcall(
        body,
        name=name,
        grid=(rows // tr,),
        in_specs=[pl.BlockSpec((N_DEV, tr, cols), lambda i: (0, i, 0)), spec, spec, spec],
        out_specs=[spec] * 4,
        out_shape=[jax.ShapeDtypeStruct((rows, cols), F32)] * 4,
        compiler_params=_cparams(1),
    )(parts, w, m, v)


_SHARDED = ("w_in", "w_up_dil", "w_up_sb", "w_out", "w_mlp_in", "w_mlp_out")
_FULL_SHAPES = {"w_in": (D_MODEL, IN_COLS), "w_up_dil": (DIL_OUT_WIDTH, D_MODEL), "w_up_sb": (SB_WIDTH, D_MODEL),
                "w_out": (D_MODEL, D_MODEL), "w_mlp_in": (D_MODEL, D_FF), "w_mlp_out": (D_FF, D_MODEL)}
_ROW_SHARDED = ("w_out", "w_mlp_out")
_PACK_COLS = 1024


def _shard_shape(name):
    r, c = _FULL_SHAPES[name]
    return (r // N_DEV, c) if name in _ROW_SHARDED else (r, c // N_DEV)


def _pack(shards):
    flat = jnp.concatenate([shards[n].reshape(-1) for n in _SHARDED])
    return flat.reshape(-1, _PACK_COLS)


def _unpack(packed):
    flat = packed.reshape(-1)
    out, pos = {}, 0
    for n in _SHARDED:
        r, c = _shard_shape(n)
        out[n] = flat[pos:pos + r * c].reshape(r, c)
        pos += r * c
    return out


def _assemble(gathered):
    flat = gathered.reshape(N_DEV, -1)
    out, pos = {}, 0
    for n in _SHARDED:
        r, c = _shard_shape(n)
        blk = flat[:, pos:pos + r * c].reshape(N_DEV, r, c)
        pos += r * c
        if n in _ROW_SHARDED:
            out[n] = blk.reshape(N_DEV * r, c)
        else:
            out[n] = blk.transpose(1, 0, 2).reshape(r, N_DEV * c)
    return out


def _chunk(full):
    cols = []
    for n in _SHARDED:
        r, c = _shard_shape(n)
        g = full[n]
        if n in _ROW_SHARDED:
            cols.append(g.reshape(N_DEV, r * c))
        else:
            cols.append(g.reshape(r, N_DEV, c).transpose(1, 0, 2).reshape(N_DEV, r * c))
    return jnp.concatenate(cols, axis=1).reshape(N_DEV, -1, _PACK_COLS)


_SMALL = (("norm_mix_g", D_MODEL), ("b_gate", 2 * D_MODEL), ("norm_mlp_g", D_MODEL), ("norm_final_g", D_MODEL))
_SMALL_N = sum(n for _, n in _SMALL) + LANES


def _pack_small(vals, tail):
    return jnp.concatenate([vals[n].reshape(1, -1) for n, _ in _SMALL] + [tail], axis=1)


def _unpack_small(vec, shapes):
    out, pos = {}, 0
    for n, width in _SMALL:
        out[n] = vec[:, pos:pos + width].reshape(shapes[n])
        pos += width
    return out, vec[:, pos:]


def kernel(x, norm_mix_g, w_in, b_gate, w_up_dil, w_up_sb, w_out, norm_mlp_g, w_mlp_in, w_mlp_out, norm_final_g, loss_target, m_norm_mix_g, m_w_in, m_b_gate, m_w_up_dil, m_w_up_sb, m_w_out, m_norm_mlp_g, m_w_mlp_in, m_w_mlp_out, m_norm_final_g, v_norm_mix_g, v_w_in, v_b_gate, v_w_up_dil, v_w_up_sb, v_w_out, v_norm_mlp_g, v_w_mlp_in, v_w_mlp_out, v_norm_final_g):
    given = dict(locals())
    s = x.shape[1]
    x0 = x.reshape(s, D_MODEL)
    target = loss_target.reshape(s, D_MODEL)
    g1 = norm_mix_g.reshape(1, D_MODEL)
    g2 = norm_mlp_g.reshape(1, D_MODEL)
    g3 = norm_final_g.reshape(1, D_MODEL)
    bg = b_gate.reshape(1, 2 * D_MODEL)
    w_shards = {n: given[n].reshape(_shard_shape(n)) for n in _SHARDED}
    m_shards = {n: given["m_" + n].reshape(_shard_shape(n)) for n in _SHARDED}
    v_shards = {n: given["v_" + n].reshape(_shard_shape(n)) for n in _SHARDED}

    w_pack = _pack(w_shards)
    full = _assemble(_all_gather(w_pack.astype(BF16)))
    w_in_f = full["w_in"]
    w_qkv, w_gl = w_in_f[:, :QKV_COLS], w_in_f[:, QKV_COLS:]

    def norm1(_, rows, consts):
        _, xh = _rms_stats(rows[0])
        return [xh * consts[0]], []

    (h1,) = _rowk("norm_mix", tm=512, rows=[x0], consts=[g1], row_outs=[(D_MODEL, BF16)], epilogue=norm1)
    qkv = _mm("proj_qkv", h1, w_qkv, out_dtype=BF16, tm=512, tn=768, tk=D_MODEL)
    gl = _mm("proj_gates", h1, w_gl, out_dtype=F32, tm=512, tn=1024, tk=D_MODEL)
    dil = [_dil_fwd(qkv, g) for g in range(len(DIL_GROUPS))]
    os_, lses = [d[0] for d in dil], [d[1] for d in dil]
    o_a = _dil_mix_fwd(os_, lses, 512)
    o_b, tot_b = _sb_fwd(qkv)
    x1, h2 = _mixer_fwd(o_a, o_b, gl, x0, bg, g2, full["w_up_dil"], full["w_up_sb"], full["w_out"], 256)
    f = _mm("mlp_in", h2, full["w_mlp_in"], out_dtype=BF16, tm=512, tn=1024, tk=D_MODEL,
            epilogue=lambda r, _: jnp.square(jnp.maximum(r, 0.0)))

    def head(acc, rows, consts):
        x1v, tv = rows
        g3v = consts[0]
        x2 = x1v + acc
        r, xh = _rms_stats(x2)
        diff = xh * g3v - tv
        loss = (0.5 / D_MODEL) * jnp.sum(jnp.sum(diff * diff, axis=0, keepdims=True), axis=1, keepdims=True)
        dy = diff * (1.0 / D_MODEL)
        dx2, dg = _rms_bwd(dy, xh, r, g3v)
        return [dx2], [dg, jnp.broadcast_to(loss, (1, LANES))]

    dx2, gg3, loss_part = _rowk("mlp_out_loss", a=f, w=full["w_mlp_out"], tm=512, tk=1024, rows=[x1, target],
                                consts=[g3], row_outs=[(D_MODEL, F32)], acc_outs=[D_MODEL, LANES], epilogue=head)

    da = _mm("mlp_out_bwd", dx2, full["w_mlp_out"], tb=True, out_dtype=BF16, tm=512, tn=1024, tk=D_MODEL, extra=f,
             epilogue=lambda r, fv: r * (2.0 * jnp.sqrt(fv.astype(F32))))
    g_w_mlp_out = _mm("grad_w_mlp_out", f, dx2, ta=True, out_dtype=F32, tm=1024, tn=1024, tk=512)
    g_w_mlp_in = _mm("grad_w_mlp_in", h2, da, ta=True, out_dtype=F32, tm=1024, tn=1024, tk=512)

    def norm_bwd(acc, rows, consts):
        xv, dres = rows
        r, xh = _rms_stats(xv)
        dx, dg = _rms_bwd(acc, xh, r, consts[0])
        return [dres + dx], [dg]

    dx1, gg2 = _rowk("mlp_in_bwd", a=da, w=full["w_mlp_in"], nt=True, tm=512, tk=1024, rows=[x1, dx2], consts=[g2],
                     row_outs=[(D_MODEL, F32)], acc_outs=[D_MODEL], epilogue=norm_bwd)
    do_a, do_b, dgl, g_w_out, g_w_ud, g_w_us, g_bg = _mixer_bwd(
        dx1, o_a, o_b, gl, bg, full["w_up_dil"], full["w_up_sb"], full["w_out"], 256)
    mix = _dil_mix_bwd(do_a, os_, lses, 512)
    dil_b = [_dil_bwd(qkv, mix[g], lses[g], mix[3 + g], g) for g in range(len(DIL_GROUPS))]
    dq_b, dk_b, dv_b = _sb_bwd(qkv, do_b, tot_b)
    dproj = jnp.concatenate(
        [d[0] for d in dil_b] + [d[1] for d in dil_b] + [d[2] for d in dil_b]
        + [dq_b, dk_b.astype(BF16), dv_b.astype(BF16), dgl], axis=1)
    g_w_in = _mm("grad_w_in", h1, dproj, ta=True, out_dtype=F32, tm=512, tn=IN_COLS // 2, tk=512)
    grad_x, gg1 = _rowk("in_proj_bwd", a=dproj, w=w_in_f, nt=True, tm=512, tk=IN_COLS // 2, rows=[x0, dx1],
                        consts=[g1], row_outs=[(D_MODEL, F32)], acc_outs=[D_MODEL], epilogue=norm_bwd)

    g_full = {"w_in": g_w_in, "w_up_dil": g_w_ud, "w_up_sb": g_w_us, "w_out": g_w_out,
              "w_mlp_in": g_w_mlp_in, "w_mlp_out": g_w_mlp_out}
    small_part = _pack_small({"norm_mix_g": gg1, "b_gate": g_bg, "norm_mlp_g": gg2, "norm_final_g": gg3}, loss_part)
    parts, small_parts = _exchange(_chunk(g_full).astype(BF16), small_part)

    g_p, d_p, m_p, v_p = _reduce_adamw("adamw_sharded", parts, w_pack, _pack(m_shards), _pack(v_shards), 64)
    small_w = _pack_small(given, jnp.zeros((1, LANES), F32))
    small_m = _pack_small({n: given["m_" + n] for n, _ in _SMALL}, jnp.zeros((1, LANES), F32))
    small_v = _pack_small({n: given["v_" + n] for n, _ in _SMALL}, jnp.ones((1, LANES), F32))
    g_s, d_s, m_s, v_s = _reduce_adamw("adamw_replicated", small_parts, small_w, small_m, small_v, 8)

    small_shapes = {n: given[n].shape for n, _ in _SMALL}
    outs = {}
    for tag, packed, small in (("grad_", g_p, g_s), ("delta_", d_p, d_s), ("new_m_", m_p, m_s), ("new_v_", v_p, v_s)):
        for n, val in _unpack(packed).items():
            outs[tag + n] = val.reshape(given[n].shape)
        small_vals, tail = _unpack_small(small, small_shapes)
        for n, val in small_vals.items():
            outs[tag + n] = val
        if tag == "grad_":
            loss = tail[0, 0]
    names = ["norm_mix_g", "w_in", "b_gate", "w_up_dil", "w_up_sb", "w_out", "norm_mlp_g", "w_mlp_in", "w_mlp_out",
             "norm_final_g"]
    return (loss, grad_x.reshape(x.shape), *[outs["grad_" + n] for n in names], *[outs["delta_" + n] for n in names],
            *[outs["new_m_" + n] for n in names], *[outs["new_v_" + n] for n in names])
```

```python
import functools
import math

import jax
import jax.numpy as jnp
from jax import lax
from jax.experimental import pallas as pl
from jax.experimental.pallas import tpu as pltpu

_pcall = pl.pallas_call

F32 = jnp.float32
BF16 = jnp.bfloat16

D_MODEL = 1024
HEAD_DIM = 64
DIL_GROUPS = ((128, 1), (512, 4), (2048, 16))
DIL_HEADS_PER_GROUP = 4
N_DIL_HEADS = 12
N_SB_HEADS = 8
DIL_WIDTH = 768
DIL_OUT_WIDTH = 256
SB_WIDTH = 512
D_FF = 4096
BLOCK = 128
RMS_EPS = 1e-6
NEG_INF = -1e30
QKV_COLS = 3 * DIL_WIDTH + 3 * SB_WIDTH
IN_COLS = QKV_COLS + 2 * D_MODEL
N_DEV = 8

ADAM_LR = 0.001
ADAM_B1 = 0.9
ADAM_B2 = 0.999
ADAM_EPS = 1e-08
ADAM_WD = 0.01
ADAM_STEP = 10

VMEM_LIMIT = 56 * 1024 * 1024
SB_TILE = 256
LANES = 128

_ARB = pltpu.ARBITRARY


def _cparams(n_axes, **kw):
    return pltpu.CompilerParams(dimension_semantics=(_ARB,) * n_axes, vmem_limit_bytes=VMEM_LIMIT, **kw)


def _dot(a, b):
    return jnp.dot(a, b, preferred_element_type=F32)


def _dot_nt(a, b):
    return lax.dot_general(a, b, (((1,), (1,)), ((), ())), preferred_element_type=F32)


def _dot_tn(a, b):
    return lax.dot_general(a, b, (((0,), (0,)), ((), ())), preferred_element_type=F32)


def _split_hi_lo(x):
    hi = x.astype(BF16)
    lo = (x - hi.astype(F32)).astype(BF16)
    return hi, lo


def _dot_hi_lo(x, m):
    hi, lo = _split_hi_lo(x)
    return _dot(hi, m) + _dot(lo, m)


def _sigmoid(x):
    return 1.0 / (1.0 + jnp.exp(-x))


def _mm(name, a, b, *, ta=False, tb=False, out_dtype, tm, tn, tk, epilogue=None, extra=None):
    m = a.shape[1] if ta else a.shape[0]
    k = a.shape[0] if ta else a.shape[1]
    n = b.shape[0] if tb else b.shape[1]
    assert (b.shape[1] if tb else b.shape[0]) == k
    tm, tn, tk = min(tm, m), min(tn, n), min(tk, k)
    assert m % tm == 0 and n % tn == 0 and k % tk == 0, (name, m, n, k, tm, tn, tk)
    nk = k // tk
    dn = (((0 if ta else 1,), (1 if tb else 0,)), ((), ()))

    def body(*refs):
        if extra is not None:
            a_ref, b_ref, e_ref, o_ref = refs[:4]
        else:
            a_ref, b_ref, o_ref = refs[:3]
            e_ref = None

        def finish(r):
            if epilogue is not None:
                r = epilogue(r, None if e_ref is None else e_ref[...])
            o_ref[...] = r.astype(out_dtype)

        part = lax.dot_general(a_ref[...].astype(BF16), b_ref[...].astype(BF16), dn, preferred_element_type=F32)
        if nk == 1:
            finish(part)
        else:
            acc_ref = refs[-1]
            kk = pl.program_id(2)

            @pl.when(kk == 0)
            def _():
                acc_ref[...] = part

            @pl.when(kk > 0)
            def _():
                acc_ref[...] += part

            @pl.when(kk == nk - 1)
            def _():
                finish(acc_ref[...])

    a_spec = pl.BlockSpec((tk, tm), lambda i, j, kk: (kk, i)) if ta else pl.BlockSpec((tm, tk), lambda i, j, kk: (i, kk))
    b_spec = pl.BlockSpec((tn, tk), lambda i, j, kk: (j, kk)) if tb else pl.BlockSpec((tk, tn), lambda i, j, kk: (kk, j))
    o_spec = pl.BlockSpec((tm, tn), lambda i, j, kk: (i, j))
    in_specs = [a_spec, b_spec]
    args = [a, b]
    if extra is not None:
        in_specs.append(o_spec)
        args.append(extra)
    return _pcall(
        body,
        name=name,
        grid=(m // tm, n // tn, nk),
        in_specs=in_specs,
        out_specs=o_spec,
        out_shape=jax.ShapeDtypeStruct((m, n), out_dtype),
        scratch_shapes=[pltpu.VMEM((tm, tn), F32)] if nk > 1 else [],
        compiler_params=_cparams(3),
    )(*args)


def _rowk(name, *, a=None, w=None, nt=False, tm, tk=None, rows=(), consts=(), row_outs=(), acc_outs=(), epilogue):
    has_mm = a is not None
    m = a.shape[0] if has_mm else rows[0].shape[0]
    assert m % tm == 0
    nm = m // tm
    if has_mm:
        k = a.shape[1]
        n = w.shape[0] if nt else w.shape[1]
        tk = min(tk, k)
        assert k % tk == 0
        nk = k // tk
    else:
        nk = 1
    n_rows, n_consts, n_ro, n_ao = len(rows), len(consts), len(row_outs), len(acc_outs)

    def body(*refs):
        pos = 0
        if has_mm:
            a_ref, w_ref = refs[0], refs[1]
            pos = 2
        row_refs = refs[pos:pos + n_rows]
        pos += n_rows
        const_refs = refs[pos:pos + n_consts]
        pos += n_consts
        ro_refs = refs[pos:pos + n_ro]
        pos += n_ro
        ao_refs = refs[pos:pos + n_ao]
        pos += n_ao
        i = pl.program_id(0)
        kk = pl.program_id(1)

        def finish(acc):
            ro_vals, ao_vals = epilogue(acc, [r[...] for r in row_refs], [c[...] for c in const_refs])
            for r, v in zip(ro_refs, ro_vals):
                r[...] = v.astype(r.dtype)
            for r, v in zip(ao_refs, ao_vals):

                @pl.when(i == 0)
                def _(r=r, v=v):
                    r[...] = v

                @pl.when(i > 0)
                def _(r=r, v=v):
                    r[...] += v

        if not has_mm:
            finish(None)
            return
        if nt:
            part = _dot_nt(a_ref[...].astype(BF16), w_ref[...])
        else:
            part = _dot(a_ref[...].astype(BF16), w_ref[...])
        if nk == 1:
            finish(part)
        else:
            acc_ref = refs[pos]

            @pl.when(kk == 0)
            def _():
                acc_ref[...] = part

            @pl.when(kk > 0)
            def _():
                acc_ref[...] += part

            @pl.when(kk == nk - 1)
            def _():
                finish(acc_ref[...])

    in_specs, args = [], []
    if has_mm:
        in_specs.append(pl.BlockSpec((tm, tk), lambda i, kk: (i, kk)))
        in_specs.append(pl.BlockSpec((n, tk), lambda i, kk: (0, kk)) if nt else pl.BlockSpec((tk, n), lambda i, kk: (kk, 0)))
        args += [a, w]
    for r in rows:
        in_specs.append(pl.BlockSpec((tm, r.shape[1]), lambda i, kk: (i, 0)))
        args.append(r)
    for c in consts:
        in_specs.append(pl.BlockSpec(c.shape, lambda i, kk: (0,) * c.ndim))
        args.append(c)
    out_specs, out_shape = [], []
    for width, dt in row_outs:
        out_specs.append(pl.BlockSpec((tm, width), lambda i, kk: (i, 0)))
        out_shape.append(jax.ShapeDtypeStruct((m, width), dt))
    for width in acc_outs:
        out_specs.append(pl.BlockSpec((1, width), lambda i, kk: (0, 0)))
        out_shape.append(jax.ShapeDtypeStruct((1, width), F32))
    return _pcall(
        body,
        name=name,
        grid=(nm, nk),
        in_specs=in_specs,
        out_specs=out_specs,
        out_shape=out_shape,
        scratch_shapes=[pltpu.VMEM((tm, n), F32)] if (has_mm and nk > 1) else [],
        compiler_params=_cparams(2),
    )(*args)


def _rms_stats(x):
    r = lax.rsqrt(jnp.mean(x * x, axis=-1, keepdims=True) + RMS_EPS)
    return r, x * r


def _rms_bwd(dh, xh, r, g):
    gy = dh * g
    dx = r * (gy - xh * jnp.mean(gy * xh, axis=-1, keepdims=True))
    return dx, jnp.sum(dh * xh, axis=0, keepdims=True)


def _alibi_slope(head):
    return 2.0 ** (-8.0 * (head + 1) / N_DIL_HEADS)


def _dil_masks(i):
    qi = lax.broadcasted_iota(jnp.int32, (BLOCK, 2 * BLOCK), 0)
    kj = lax.broadcasted_iota(jnp.int32, (BLOCK, 2 * BLOCK), 1)
    steps = qi + BLOCK - kj
    valid = (steps >= 0) & (steps <= BLOCK) & ((kj >= BLOCK) | (i > 0))
    return steps.astype(F32), valid


def _dil_specs(group, dilation, nb, clamp):
    ncb = QKV_COLS // DIL_OUT_WIDTH

    def cur(col):
        return pl.BlockSpec((BLOCK, DIL_OUT_WIDTH), lambda r, i: (clamp(i), r * ncb + col))

    def prev(col):
        return pl.BlockSpec((BLOCK, DIL_OUT_WIDTH), lambda r, i: (jnp.maximum(clamp(i) - 1, 0), r * ncb + col))

    return [cur(group), cur(3 + group), prev(3 + group), cur(6 + group), prev(6 + group)]


def _dil_fwd(qkv, group):
    window, dilation = DIL_GROUPS[group]
    s = qkv.shape[0]
    sub = s // dilation
    nb = sub // BLOCK
    assert nb * BLOCK * dilation == s and window // dilation == BLOCK
    slopes = [_alibi_slope(group * DIL_HEADS_PER_GROUP + h) * dilation for h in range(DIL_HEADS_PER_GROUP)]

    def body(q_ref, kc_ref, kp_ref, vc_ref, vp_ref, o_ref, lse_ref):
        i = pl.program_id(1)
        q = q_ref[...]
        kk = jnp.concatenate([kp_ref[...], kc_ref[...]], axis=0)
        vv = jnp.concatenate([vp_ref[...], vc_ref[...]], axis=0)
        head_id = lax.broadcasted_iota(jnp.int32, (1, DIL_OUT_WIDTH), 1) // HEAD_DIM
        steps, valid = _dil_masks(i)
        o = jnp.zeros((BLOCK, DIL_OUT_WIDTH), F32)
        lse_all = jnp.zeros((BLOCK, DIL_OUT_WIDTH), F32)
        for h in range(DIL_HEADS_PER_GROUP):
            qm = jnp.where(head_id == h, q, jnp.zeros_like(q))
            logits = _dot_nt(qm, kk) * (1.0 / math.sqrt(HEAD_DIM)) - slopes[h] * steps
            logits = jnp.where(valid, logits, NEG_INF)
            mx = jnp.max(logits, axis=1, keepdims=True)
            e = jnp.exp(logits - mx)
            den = jnp.sum(e, axis=1, keepdims=True)
            lse = mx + jnp.log(den)
            p = jnp.exp(logits - lse)
            o = jnp.where(head_id == h, _dot(p.astype(BF16), vv), o)
            lse_all = jnp.where(head_id == h, lse, lse_all)
        o_ref[...] = o
        lse_ref[...] = lse_all

    qkv_v = qkv.reshape(sub, dilation * QKV_COLS)
    out_spec = pl.BlockSpec((BLOCK, DIL_OUT_WIDTH), lambda r, i: (i, r))
    o, lse = _pcall(
        body,
        name=f"dil_fwd_g{group}",
        grid=(dilation, nb),
        in_specs=_dil_specs(group, dilation, nb, lambda i: i),
        out_specs=[out_spec, out_spec],
        out_shape=[jax.ShapeDtypeStruct((sub, dilation * DIL_OUT_WIDTH), F32)] * 2,
        compiler_params=_cparams(2),
    )(qkv_v, qkv_v, qkv_v, qkv_v, qkv_v)
    return o.reshape(s, DIL_OUT_WIDTH), lse.reshape(s, DIL_OUT_WIDTH)


def _dil_bwd(qkv, do_g, lse_g, dterm_g, group):
    window, dilation = DIL_GROUPS[group]
    s = qkv.shape[0]
    sub = s // dilation
    nb = sub // BLOCK
    slopes = [_alibi_slope(group * DIL_HEADS_PER_GROUP + h) * dilation for h in range(DIL_HEADS_PER_GROUP)]
    scale = 1.0 / math.sqrt(HEAD_DIM)

    def body(q_ref, kc_ref, kp_ref, vc_ref, vp_ref, do_ref, lse_ref, dt_ref, dq_ref, dk_ref, dv_ref, ck_ref, cv_ref):
        i = pl.program_id(1)

        @pl.when(i == 0)
        def _():
            ck_ref[...] = jnp.zeros_like(ck_ref)
            cv_ref[...] = jnp.zeros_like(cv_ref)

        @pl.when(i < nb)
        def _():
            q = q_ref[...]
            do = do_ref[...]
            lse_all = lse_ref[...]
            dt_all = dt_ref[...]
            kk = jnp.concatenate([kp_ref[...], kc_ref[...]], axis=0)
            vv = jnp.concatenate([vp_ref[...], vc_ref[...]], axis=0)
            lane = lax.broadcasted_iota(jnp.int32, (1, DIL_OUT_WIDTH), 1)
            head_id = lane // HEAD_DIM
            steps, valid = _dil_masks(i)
            dq = jnp.zeros((BLOCK, DIL_OUT_WIDTH), F32)
            dkk = jnp.zeros((2 * BLOCK, DIL_OUT_WIDTH), F32)
            dvv = jnp.zeros((2 * BLOCK, DIL_OUT_WIDTH), F32)
            for h in range(DIL_HEADS_PER_GROUP):
                qm = jnp.where(head_id == h, q, jnp.zeros_like(q))
                dom = jnp.where(head_id == h, do, jnp.zeros_like(do))
                first = lane == h * HEAD_DIM
                lse = jnp.sum(jnp.where(first, lse_all, 0.0), axis=1, keepdims=True)
                dt = jnp.sum(jnp.where(first, dt_all, 0.0), axis=1, keepdims=True)
                logits = _dot_nt(qm, kk) * scale - slopes[h] * steps
                p = jnp.where(valid, jnp.exp(jnp.where(valid, logits, NEG_INF) - lse), 0.0)
                dp = _dot_nt(dom, vv)
                ds = (p * (dp + dt) * scale).astype(BF16)
                dq = jnp.where(head_id == h, _dot(ds, kk), dq)
                dkk = dkk + _dot_tn(ds, qm)
                dvv = dvv + _dot_tn(p.astype(BF16), dom)
            dq_ref[...] = dq.astype(dq_ref.dtype)
            dk_ref[...] = (ck_ref[...] + dkk[:BLOCK]).astype(dk_ref.dtype)
            dv_ref[...] = (cv_ref[...] + dvv[:BLOCK]).astype(dv_ref.dtype)
            ck_ref[...] = dkk[BLOCK:]
            cv_ref[...] = dvv[BLOCK:]

        @pl.when(i == nb)
        def _():
            dk_ref[...] = ck_ref[...].astype(dk_ref.dtype)
            dv_ref[...] = cv_ref[...].astype(dv_ref.dtype)

    clamp = lambda i: jnp.minimum(i, nb - 1)
    qkv_v = qkv.reshape(sub, dilation * QKV_COLS)
    view = lambda t: t.reshape(sub, dilation * DIL_OUT_WIDTH)
    row_spec = pl.BlockSpec((BLOCK, DIL_OUT_WIDTH), lambda r, i: (clamp(i), r))
    late_spec = pl.BlockSpec((BLOCK, DIL_OUT_WIDTH), lambda r, i: (jnp.maximum(i - 1, 0), r))
    dq, dk, dv = _pcall(
        body,
        name=f"dil_bwd_g{group}",
        grid=(dilation, nb + 1),
        in_specs=_dil_specs(group, dilation, nb, clamp) + [row_spec, row_spec, row_spec],
        out_specs=[row_spec, late_spec, late_spec],
        out_shape=[jax.ShapeDtypeStruct((sub, dilation * DIL_OUT_WIDTH), BF16)] * 3,
        scratch_shapes=[pltpu.VMEM((BLOCK, DIL_OUT_WIDTH), F32)] * 2,
        compiler_params=_cparams(2),
    )(qkv_v, qkv_v, qkv_v, qkv_v, qkv_v, view(do_g), view(lse_g), view(dterm_g))
    return dq.reshape(s, DIL_OUT_WIDTH), dk.reshape(s, DIL_OUT_WIDTH), dv.reshape(s, DIL_OUT_WIDTH)


def _head_block_ones():
    r = lax.broadcasted_iota(jnp.int32, (DIL_OUT_WIDTH, DIL_OUT_WIDTH), 0) // HEAD_DIM
    c = lax.broadcasted_iota(jnp.int32, (DIL_OUT_WIDTH, DIL_OUT_WIDTH), 1) // HEAD_DIM
    return jnp.where(r == c, 1.0, 0.0).astype(BF16)


def _dil_mix_weights(l0, l1, l2):
    mx = jnp.maximum(jnp.maximum(l0, l1), l2)
    e0, e1, e2 = jnp.exp(l0 - mx), jnp.exp(l1 - mx), jnp.exp(l2 - mx)
    inv = 1.0 / (e0 + e1 + e2)
    return e0 * inv, e1 * inv, e2 * inv


def _dil_mix_fwd(os_, lses, tm):
    def epi(_, rows, consts):
        o0, o1, o2, l0, l1, l2 = rows
        w0, w1, w2 = _dil_mix_weights(l0, l1, l2)
        return [w0 * o0 + w1 * o1 + w2 * o2], []

    (o_a,) = _rowk("dil_mix_fwd", tm=tm, rows=list(os_) + list(lses), row_outs=[(DIL_OUT_WIDTH, BF16)], epilogue=epi)
    return o_a


def _dil_mix_bwd(do_a, os_, lses, tm):
    def epi(_, rows, consts):
        do, o0, o1, o2, l0, l1, l2 = rows
        do = do.astype(F32)
        w0, w1, w2 = _dil_mix_weights(l0, l1, l2)
        mixed = w0 * o0 + w1 * o1 + w2 * o2
        tot = _dot_hi_lo(do * mixed, _head_block_ones())
        return [w0 * do, w1 * do, w2 * do, -w0 * tot, -w1 * tot, -w2 * tot], []

    return _rowk(
        "dil_mix_bwd", tm=tm, rows=[do_a] + list(os_) + list(lses),
        row_outs=[(DIL_OUT_WIDTH, BF16)] * 3 + [(DIL_OUT_WIDTH, F32)] * 3, epilogue=epi)


_SB_Q0 = 3 * DIL_WIDTH // LANES
_SB_K0 = _SB_Q0 + SB_WIDTH // LANES
_SB_V0 = _SB_K0 + SB_WIDTH // LANES


_LOG2E = 1.4426950408889634
_EXP2_CLAMP = 126.0


def _tri2(t, op):
    r = lax.broadcasted_iota(jnp.int32, (2 * t, t), 0) % t
    c = lax.broadcasted_iota(jnp.int32, (2 * t, t), 1)
    return jnp.where(op(r, c), 1.0, 0.0).astype(BF16)


def _hi_lo(x):
    hi, lo = _split_hi_lo(x)
    return jnp.concatenate([hi, lo], axis=1)


def _softplus2(z2):
    return jnp.maximum(z2, jnp.log2(1.0 + jnp.exp2(jnp.minimum(z2, _EXP2_CLAMP))))


def _sb_fwd(qkv):
    s = qkv.shape[0]
    t = SB_TILE
    nq = s // t
    n_pairs = SB_WIDTH // LANES

    def body(q_ref, k_ref, v_ref, o_ref, tot_ref):
        i = pl.program_id(1)
        q = q_ref[...] * (1.0 / math.sqrt(HEAD_DIM))
        lane_hi = lax.broadcasted_iota(jnp.int32, (1, LANES), 1) // HEAD_DIM
        later = _tri2(t, lambda r, c: r > c)
        row = lax.broadcasted_iota(jnp.int32, (t, t), 0)
        col = lax.broadcasted_iota(jnp.int32, (t, t), 1)
        qms = [jnp.where(lane_hi == hh, q, jnp.zeros_like(q)) for hh in range(2)]

        def step(jj, carry, diag):
            tiles = (2 * jj + 1, 2 * jj)
            offs = [pl.multiple_of(j * t, t) for j in tiles]
            ks = [k_ref[pl.ds(off, t), :] for off in offs]
            vs = [v_ref[pl.ds(off, t), :] for off in offs]
            masks = [(j * t + col) < (i * t + row) for j in tiles] if diag else None
            chains = [(n, hh) for n in range(2) for hh in range(2)]
            z2s = [_dot_nt(qms[hh], ks[n]) * _LOG2E for n, hh in chains]
            sps, lposs = [], []
            for (n, hh), z2 in zip(chains, z2s):
                sp = _softplus2(z2)
                lposs.append(z2 - sp)
                sps.append(jnp.where(masks[n], sp, 0.0) if diag else sp)
            sufs = [_dot(_hi_lo(sp), later) for sp in sps]
            cs = [carry[0], carry[2]]
            accs = [carry[1], carry[3]]
            for idx, (n, hh) in enumerate(chains):
                a = jnp.exp2(lposs[idx] - sufs[idx] - cs[hh])
                if diag:
                    a = jnp.where(masks[n], a, 0.0)
                accs[hh] = accs[hh] + _dot(a.astype(BF16), vs[n])
                cs[hh] = cs[hh] + jnp.sum(sps[idx], axis=1, keepdims=True)
            return cs[0], accs[0], cs[1], accs[1]

        zc, za = jnp.zeros((t, 1), F32), jnp.zeros((t, LANES), F32)
        half = i // 2
        carry = step(half, (zc, za, zc, za), True)
        carry = lax.fori_loop(0, half, lambda n, ca: step(half - 1 - n, ca, False), carry)
        out = jnp.where(lane_hi == 0, carry[1], carry[3])
        tot = jnp.where(lane_hi == 0, carry[0], carry[2])
        o_ref[...] = out.astype(o_ref.dtype)
        tot_ref[...] = tot

    o, tot = _pcall(
        body,
        name="sb_fwd",
        grid=(n_pairs, nq),
        in_specs=[
            pl.BlockSpec((t, LANES), lambda p, i: (i, _SB_Q0 + p)),
            pl.BlockSpec((s, LANES), lambda p, i: (0, _SB_K0 + p)),
            pl.BlockSpec((s, LANES), lambda p, i: (0, _SB_V0 + p)),
        ],
        out_specs=[pl.BlockSpec((t, LANES), lambda p, i: (i, p))] * 2,
        out_shape=[jax.ShapeDtypeStruct((s, SB_WIDTH), BF16), jax.ShapeDtypeStruct((s, SB_WIDTH), F32)],
        compiler_params=_cparams(2),
    )(qkv, qkv, qkv)
    return o, tot


def _sb_bwd(qkv, do_b, tot_b):
    s = qkv.shape[0]
    t = SB_TILE
    nq = s // t
    n_pairs = SB_WIDTH // LANES
    scale = 1.0 / math.sqrt(HEAD_DIM)

    def body(q_ref, k_ref, v_ref, do_ref, tot_ref, dq_ref, dk_ref, dv_ref):
        i = pl.program_id(1)

        @pl.when(i == 0)
        def _():
            dk_ref[...] = jnp.zeros_like(dk_ref)
            dv_ref[...] = jnp.zeros_like(dv_ref)

        q = q_ref[...] * scale
        do = do_ref[...]
        tot_all = tot_ref[...]
        lane = lax.broadcasted_iota(jnp.int32, (1, LANES), 1)
        lane_hi = lane // HEAD_DIM
        upto = _tri2(t, lambda r, c: r <= c)
        before = _tri2(t, lambda r, c: r < c)[:t]
        row = lax.broadcasted_iota(jnp.int32, (t, t), 0)
        col = lax.broadcasted_iota(jnp.int32, (t, t), 1)
        qms = [jnp.where(lane_hi == hh, q, jnp.zeros_like(q)) for hh in range(2)]
        doms = [jnp.where(lane_hi == hh, do, jnp.zeros_like(do)) for hh in range(2)]
        tots = [jnp.sum(jnp.where(lane == hh * HEAD_DIM, tot_all, 0.0), axis=1, keepdims=True) for hh in range(2)]

        def step(jj, carry, diag):
            tiles = (2 * jj, 2 * jj + 1)
            offs = [pl.multiple_of(j * t, t) for j in tiles]
            ks = [k_ref[pl.ds(off, t), :] for off in offs]
            vs = [v_ref[pl.ds(off, t), :] for off in offs]
            masks = [(j * t + col) < (i * t + row) for j in tiles] if diag else None
            chains = [(n, hh) for n in range(2) for hh in range(2)]
            z2s = [_dot_nt(qms[hh], ks[n]) * _LOG2E for n, hh in chains]
            das = [_dot_nt(doms[hh], vs[n]) for n, hh in chains]
            sps, lposs = [], []
            for (n, hh), z2 in zip(chains, z2s):
                sp = _softplus2(z2)
                lposs.append(z2 - sp)
                sps.append(jnp.where(masks[n], sp, 0.0) if diag else sp)
            incls = [_dot(_hi_lo(sp), upto) for sp in sps]
            cls = [carry[0], carry[3]]
            cgs = [carry[1], carry[4]]
            accs = [carry[2], carry[5]]
            gs, abs_, cg_at = [], [], []
            for idx, (n, hh) in enumerate(chains):
                a = jnp.exp2(lposs[idx] - (tots[hh] - cls[hh] - incls[idx]))
                if diag:
                    a = jnp.where(masks[n], a, 0.0)
                g = a * das[idx]
                gs.append(g)
                abs_.append(a.astype(BF16))
                cg_at.append(cgs[hh])
                cls[hh] = cls[hh] + jnp.sum(sps[idx], axis=1, keepdims=True)
                cgs[hh] = cgs[hh] + jnp.sum(g, axis=1, keepdims=True)
            prefs = [_dot(g.astype(BF16), before) for g in gs]
            dvs = [_dot_tn(abs_[idx], doms[hh]) for idx, (n, hh) in enumerate(chains)]
            dzs = []
            for idx, (n, hh) in enumerate(chains):
                sig = jnp.exp2(lposs[idx])
                dz = gs[idx] - sig * (gs[idx] + prefs[idx] + cg_at[idx])
                if diag:
                    dz = jnp.where(masks[n], dz, 0.0)
                dzs.append(dz.astype(BF16))
            for idx, (n, hh) in enumerate(chains):
                accs[hh] = accs[hh] + _dot(dzs[idx], ks[n])
            dks = [_dot_tn(dzs[idx], qms[hh]) for idx, (n, hh) in enumerate(chains)]
            for n in range(2):
                dk_ref[pl.ds(offs[n], t), :] += dks[2 * n] + dks[2 * n + 1]
                dv_ref[pl.ds(offs[n], t), :] += dvs[2 * n] + dvs[2 * n + 1]
            return cls[0], cgs[0], accs[0], cls[1], cgs[1], accs[1]

        zc, za = jnp.zeros((t, 1), F32), jnp.zeros((t, LANES), F32)
        half = i // 2
        carry = lax.fori_loop(0, half, lambda jj, ca: step(jj, ca, False), (zc, zc, za, zc, zc, za))
        carry = step(half, carry, True)
        dq = jnp.where(lane_hi == 0, carry[2], carry[5])
        dq_ref[...] = (dq * scale).astype(dq_ref.dtype)

    row_spec = pl.BlockSpec((t, LANES), lambda p, i: (i, p))
    full_spec = pl.BlockSpec((s, LANES), lambda p, i: (0, p))
    return _pcall(
        body,
        name="sb_bwd",
        grid=(n_pairs, nq),
        in_specs=[
            pl.BlockSpec((t, LANES), lambda p, i: (i, _SB_Q0 + p)),
            pl.BlockSpec((s, LANES), lambda p, i: (0, _SB_K0 + p)),
            pl.BlockSpec((s, LANES), lambda p, i: (0, _SB_V0 + p)),
            row_spec, row_spec,
        ],
        out_specs=[row_spec, full_spec, full_spec],
        out_shape=[jax.ShapeDtypeStruct((s, SB_WIDTH), BF16), jax.ShapeDtypeStruct((s, SB_WIDTH), F32),
                   jax.ShapeDtypeStruct((s, SB_WIDTH), F32)],
        compiler_params=_cparams(2),
    )(qkv, qkv, qkv, do_b, tot_b)


def _gates(gl, bg):
    return _sigmoid(gl[:, :D_MODEL] + bg[:, :D_MODEL]), _sigmoid(gl[:, D_MODEL:] + bg[:, D_MODEL:])


def _mixer_fwd(o_a, o_b, gl, x0, bg, g2, w_ud, w_us, w_out, tm):
    def epi(_, rows, consts):
        oa, ob, glv, x = rows
        bgv, g2v, wud, wus, wout = consts
        ga, gb = _gates(glv, bgv)
        merged = ga * _dot(oa, wud) + gb * _dot(ob, wus)
        x1 = x + _dot(merged.astype(BF16), wout)
        r, xh = _rms_stats(x1)
        return [x1, xh * g2v], []

    return _rowk("mixer_fwd", tm=tm, rows=[o_a, o_b, gl, x0], consts=[bg, g2, w_ud, w_us, w_out],
                 row_outs=[(D_MODEL, F32), (D_MODEL, BF16)], epilogue=epi)


def _mixer_bwd(dx1, o_a, o_b, gl, bg, w_ud, w_us, w_out, tm):
    s = dx1.shape[0]
    nm = s // tm

    def body(dx_ref, oa_ref, ob_ref, gl_ref, bg_ref, wud_ref, wus_ref, wout_ref,
             doa_ref, dob_ref, dgl_ref, gwout_ref, gwud_ref, gwus_ref, gbg_ref):
        i = pl.program_id(0)
        dxb = dx_ref[...].astype(BF16)
        oa, ob = oa_ref[...], ob_ref[...]
        ga, gb = _gates(gl_ref[...], bg_ref[...])
        ua, ub = _dot(oa, wud_ref[...]), _dot(ob, wus_ref[...])
        merged = (ga * ua + gb * ub).astype(BF16)
        dm = _dot_nt(dxb, wout_ref[...])
        dua = (dm * ga).astype(BF16)
        dub = (dm * gb).astype(BF16)
        dgla = dm * ua * ga * (1.0 - ga)
        dglb = dm * ub * gb * (1.0 - gb)
        doa_ref[...] = _dot_nt(dua, wud_ref[...]).astype(doa_ref.dtype)
        dob_ref[...] = _dot_nt(dub, wus_ref[...]).astype(dob_ref.dtype)
        dgl_ref[:, :D_MODEL] = dgla.astype(dgl_ref.dtype)
        dgl_ref[:, D_MODEL:] = dglb.astype(dgl_ref.dtype)
        parts = [(gwout_ref, _dot_tn(merged, dxb)), (gwud_ref, _dot_tn(oa, dua)), (gwus_ref, _dot_tn(ob, dub))]
        for r, v in parts:

            @pl.when(i == 0)
            def _(r=r, v=v):
                r[...] = v

            @pl.when(i > 0)
            def _(r=r, v=v):
                r[...] += v

        sa = jnp.sum(dgla, axis=0, keepdims=True)
        sb = jnp.sum(dglb, axis=0, keepdims=True)

        @pl.when(i == 0)
        def _():
            gbg_ref[:, :D_MODEL] = sa
            gbg_ref[:, D_MODEL:] = sb

        @pl.when(i > 0)
        def _():
            gbg_ref[:, :D_MODEL] += sa
            gbg_ref[:, D_MODEL:] += sb

    row = lambda w: pl.BlockSpec((tm, w), lambda i: (i, 0))
    full = lambda a: pl.BlockSpec(a.shape, lambda i: (0, 0))
    fshape = lambda r, c: jax.ShapeDtypeStruct((r, c), F32)
    return _pcall(
        body,
        name="mixer_bwd",
        grid=(nm,),
        in_specs=[row(D_MODEL), row(DIL_OUT_WIDTH), row(SB_WIDTH), row(2 * D_MODEL),
                  full(bg), full(w_ud), full(w_us), full(w_out)],
        out_specs=[row(DIL_OUT_WIDTH), row(SB_WIDTH), row(2 * D_MODEL),
                   pl.BlockSpec((D_MODEL, D_MODEL), lambda i: (0, 0)),
                   pl.BlockSpec((DIL_OUT_WIDTH, D_MODEL), lambda i: (0, 0)),
                   pl.BlockSpec((SB_WIDTH, D_MODEL), lambda i: (0, 0)),
                   pl.BlockSpec((1, 2 * D_MODEL), lambda i: (0, 0))],
        out_shape=[jax.ShapeDtypeStruct((s, DIL_OUT_WIDTH), BF16), jax.ShapeDtypeStruct((s, SB_WIDTH), BF16),
                   jax.ShapeDtypeStruct((s, 2 * D_MODEL), BF16),
                   fshape(D_MODEL, D_MODEL), fshape(DIL_OUT_WIDTH, D_MODEL), fshape(SB_WIDTH, D_MODEL),
                   fshape(1, 2 * D_MODEL)],
        compiler_params=_cparams(1),
    )(dx1, o_a, o_b, gl, bg, w_ud, w_us, w_out)


_HBM = pl.BlockSpec(memory_space=pltpu.HBM)
_MESH = pl.DeviceIdType.MESH


def _all_gather(shards):
    n = len(shards)

    def body(*refs):
        x_refs, out_refs = refs[:n], refs[n:2 * n]
        send_sems, recv_sems, local_sems = refs[2 * n:]
        x, y, c = lax.axis_index("x"), lax.axis_index("y"), lax.axis_index("c")
        me, sibling = (x, y, c), (x, y, 1 - c)
        chips = [(1 - x, y), (x, 1 - y), (1 - x, 1 - y)]

        def slot(a, px, py, pc):
            return out_refs[a].at[4 * px + 2 * py + pc]

        def copy(a, k, block, to, own=False):
            return pltpu.make_async_remote_copy(
                src_ref=x_refs[a] if own else slot(a, *block), dst_ref=slot(a, *block),
                send_sem=send_sems.at[7 * a + k], recv_sem=recv_sems.at[7 * a + k], device_id=to, device_id_type=_MESH)

        mine = [pltpu.make_async_copy(x_refs[a], slot(a, *me), local_sems.at[a]) for a in range(n)]
        for cp in mine:
            cp.start()
        first = []
        for a in range(n):
            first.append(copy(a, 0, me, sibling, own=True))
            first += [copy(a, 1 + j, me, (*chip, c), own=True) for j, chip in enumerate(chips)]
        for cp in first:
            cp.start()
        passed = []
        for a in range(n):
            for j, chip in enumerate(chips):
                copy(a, 1 + j, (*chip, c), me).wait_recv()
                passed.append(copy(a, 4 + j, (*chip, c), sibling))
                passed[-1].start()
        for a in range(n):
            copy(a, 0, sibling, me).wait_recv()
            for j, chip in enumerate(chips):
                copy(a, 4 + j, (*chip, 1 - c), me).wait_recv()
        for cp in first + passed:
            cp.wait_send()
        for cp in mine:
            cp.wait()

    return _pcall(
        body,
        name="all_gather_weights",
        in_specs=[_HBM] * n,
        out_specs=[_HBM] * n,
        out_shape=[jax.ShapeDtypeStruct((N_DEV,) + s.shape, s.dtype) for s in shards],
        scratch_shapes=[pltpu.SemaphoreType.DMA((7 * n,)), pltpu.SemaphoreType.DMA((7 * n,)),
                        pltpu.SemaphoreType.DMA((n,))],
    )(*shards)


def _exchange(chunks):
    n = len(chunks)

    def body(*refs):
        g_refs, o_refs = refs[:n], refs[n:2 * n]
        send_sems, recv_sems, local_sems = refs[2 * n:]
        x, y, c = lax.axis_index("x"), lax.axis_index("y"), lax.axis_index("c")
        me = 4 * x + 2 * y + c
        own = [pltpu.make_async_copy(g_refs[a].at[me], o_refs[a].at[me], local_sems.at[a]) for a in range(n)]
        for cp in own:
            cp.start()
        copies = []
        for a in range(n):
            for k in range(1, N_DEV):
                px, py, pc = x ^ (k >> 2), y ^ ((k >> 1) & 1), c ^ (k & 1)
                peer = 4 * px + 2 * py + pc
                copies.append(pltpu.make_async_remote_copy(
                    src_ref=g_refs[a].at[peer], dst_ref=o_refs[a].at[me], send_sem=send_sems.at[7 * a + k - 1],
                    recv_sem=recv_sems.at[7 * a + k - 1], device_id=(px, py, pc), device_id_type=_MESH))
        for cp in copies:
            cp.start()
        for cp in copies:
            cp.wait()
        for cp in own:
            cp.wait()

    return _pcall(
        body,
        name="exchange_grads",
        in_specs=[_HBM] * n,
        out_specs=[_HBM] * n,
        out_shape=[jax.ShapeDtypeStruct(g.shape, g.dtype) for g in chunks],
        scratch_shapes=[pltpu.SemaphoreType.DMA((7 * n,)), pltpu.SemaphoreType.DMA((7 * n,)),
                        pltpu.SemaphoreType.DMA((n,))],
    )(*chunks)


def _reduce_adamw(name, parts, w, m, v, tr):
    _, rows, cols = parts.shape
    tr = min(tr, rows)
    assert rows % tr == 0
    c1 = 1.0 / (1.0 - ADAM_B1 ** ADAM_STEP)
    c2 = 1.0 / (1.0 - ADAM_B2 ** ADAM_STEP)

    def body(p_ref, w_ref, m_ref, v_ref, g_out, d_out, m_out, v_out):
        g = p_ref[0].astype(F32)
        for d in range(1, N_DEV):
            g = g + p_ref[d].astype(F32)
        mn = ADAM_B1 * m_ref[...] + (1.0 - ADAM_B1) * g
        vn = ADAM_B2 * v_ref[...] + (1.0 - ADAM_B2) * (g * g)
        g_out[...] = g
        m_out[...] = mn
        v_out[...] = vn
        d_out[...] = -ADAM_LR * ((mn * c1) / (jnp.sqrt(vn * c2) + ADAM_EPS) + ADAM_WD * w_ref[...])

    spec = pl.BlockSpec((tr, cols), lambda i: (i, 0))
    return _pcall(
        body,
        name=name,
        grid=(rows // tr,),
        in_specs=[pl.BlockSpec((N_DEV, tr, cols), lambda i: (0, i, 0)), spec, spec, spec],
        out_specs=[spec] * 4,
        out_shape=[jax.ShapeDtypeStruct((rows, cols), F32)] * 4,
        compiler_params=_cparams(1),
    )(parts, w, m, v)


_SHARDED = ("w_in", "w_up_dil", "w_up_sb", "w_out", "w_mlp_in", "w_mlp_out")
_FULL_SHAPES = {"w_in": (D_MODEL, IN_COLS), "w_up_dil": (DIL_OUT_WIDTH, D_MODEL), "w_up_sb": (SB_WIDTH, D_MODEL),
                "w_out": (D_MODEL, D_MODEL), "w_mlp_in": (D_MODEL, D_FF), "w_mlp_out": (D_FF, D_MODEL)}
_ROW_SHARDED = ("w_out", "w_mlp_out")


def _shard_shape(name):
    r, c = _FULL_SHAPES[name]
    return (r // N_DEV, c) if name in _ROW_SHARDED else (r, c // N_DEV)


def _assemble(name, gathered):
    r, c = _shard_shape(name)
    if name in _ROW_SHARDED:
        return gathered.reshape(N_DEV * r, c)
    return gathered.transpose(1, 0, 2).reshape(r, N_DEV * c)


def _chunk(name, full):
    r, c = _shard_shape(name)
    if name in _ROW_SHARDED:
        return full.reshape(N_DEV, r, c)
    return full.reshape(r, N_DEV, c).transpose(1, 0, 2)


_SMALL = (("norm_mix_g", D_MODEL), ("b_gate", 2 * D_MODEL), ("norm_mlp_g", D_MODEL), ("norm_final_g", D_MODEL))
_SMALL_N = sum(n for _, n in _SMALL) + LANES


def _pack_small(vals, tail):
    return jnp.concatenate([vals[n].reshape(1, -1) for n, _ in _SMALL] + [tail], axis=1)


def _unpack_small(vec, shapes):
    out, pos = {}, 0
    for n, width in _SMALL:
        out[n] = vec[:, pos:pos + width].reshape(shapes[n])
        pos += width
    return out, vec[:, pos:]


def kernel(x, norm_mix_g, w_in, b_gate, w_up_dil, w_up_sb, w_out, norm_mlp_g, w_mlp_in, w_mlp_out, norm_final_g, loss_target, m_norm_mix_g, m_w_in, m_b_gate, m_w_up_dil, m_w_up_sb, m_w_out, m_norm_mlp_g, m_w_mlp_in, m_w_mlp_out, m_norm_final_g, v_norm_mix_g, v_w_in, v_b_gate, v_w_up_dil, v_w_up_sb, v_w_out, v_norm_mlp_g, v_w_mlp_in, v_w_mlp_out, v_norm_final_g):
    given = dict(locals())
    s = x.shape[1]
    x0 = x.reshape(s, D_MODEL)
    target = loss_target.reshape(s, D_MODEL)
    g1 = norm_mix_g.reshape(1, D_MODEL)
    g2 = norm_mlp_g.reshape(1, D_MODEL)
    g3 = norm_final_g.reshape(1, D_MODEL)
    bg = b_gate.reshape(1, 2 * D_MODEL)
    w_shards = {n: given[n].reshape(_shard_shape(n)) for n in _SHARDED}
    m_shards = {n: given["m_" + n].reshape(_shard_shape(n)) for n in _SHARDED}
    v_shards = {n: given["v_" + n].reshape(_shard_shape(n)) for n in _SHARDED}

    gathered = _all_gather([w_shards[n].astype(BF16) for n in _SHARDED])
    full = {n: _assemble(n, g) for n, g in zip(_SHARDED, gathered)}
    w_in_f = full["w_in"]
    w_qkv, w_gl = w_in_f[:, :QKV_COLS], w_in_f[:, QKV_COLS:]

    def norm1(_, rows, consts):
        _, xh = _rms_stats(rows[0])
        return [xh * consts[0]], []

    (h1,) = _rowk("norm_mix", tm=512, rows=[x0], consts=[g1], row_outs=[(D_MODEL, BF16)], epilogue=norm1)
    qkv = _mm("proj_qkv", h1, w_qkv, out_dtype=BF16, tm=512, tn=768, tk=D_MODEL)
    gl = _mm("proj_gates", h1, w_gl, out_dtype=F32, tm=512, tn=1024, tk=D_MODEL)
    dil = [_dil_fwd(qkv, g) for g in range(len(DIL_GROUPS))]
    os_, lses = [d[0] for d in dil], [d[1] for d in dil]
    o_a = _dil_mix_fwd(os_, lses, 512)
    o_b, tot_b = _sb_fwd(qkv)
    x1, h2 = _mixer_fwd(o_a, o_b, gl, x0, bg, g2, full["w_up_dil"], full["w_up_sb"], full["w_out"], 256)
    f = _mm("mlp_in", h2, full["w_mlp_in"], out_dtype=BF16, tm=512, tn=1024, tk=D_MODEL,
            epilogue=lambda r, _: jnp.square(jnp.maximum(r, 0.0)))

    def head(acc, rows, consts):
        x1v, tv = rows
        g3v = consts[0]
        x2 = x1v + acc
        r, xh = _rms_stats(x2)
        diff = xh * g3v - tv
        loss = (0.5 / D_MODEL) * jnp.sum(jnp.sum(diff * diff, axis=0, keepdims=True), axis=1, keepdims=True)
        dy = diff * (1.0 / D_MODEL)
        dx2, dg = _rms_bwd(dy, xh, r, g3v)
        return [dx2], [dg, jnp.broadcast_to(loss, (1, LANES))]

    dx2, gg3, loss_part = _rowk("mlp_out_loss", a=f, w=full["w_mlp_out"], tm=512, tk=1024, rows=[x1, target],
                                consts=[g3], row_outs=[(D_MODEL, F32)], acc_outs=[D_MODEL, LANES], epilogue=head)

    da = _mm("mlp_out_bwd", dx2, full["w_mlp_out"], tb=True, out_dtype=BF16, tm=512, tn=1024, tk=D_MODEL, extra=f,
             epilogue=lambda r, fv: r * (2.0 * jnp.sqrt(fv.astype(F32))))
    g_w_mlp_out = _mm("grad_w_mlp_out", f, dx2, ta=True, out_dtype=F32, tm=1024, tn=1024, tk=512)
    g_w_mlp_in = _mm("grad_w_mlp_in", h2, da, ta=True, out_dtype=F32, tm=1024, tn=1024, tk=512)

    def norm_bwd(acc, rows, consts):
        xv, dres = rows
        r, xh = _rms_stats(xv)
        dx, dg = _rms_bwd(acc, xh, r, consts[0])
        return [dres + dx], [dg]

    dx1, gg2 = _rowk("mlp_in_bwd", a=da, w=full["w_mlp_in"], nt=True, tm=512, tk=1024, rows=[x1, dx2], consts=[g2],
                     row_outs=[(D_MODEL, F32)], acc_outs=[D_MODEL], epilogue=norm_bwd)
    do_a, do_b, dgl, g_w_out, g_w_ud, g_w_us, g_bg = _mixer_bwd(
        dx1, o_a, o_b, gl, bg, full["w_up_dil"], full["w_up_sb"], full["w_out"], 256)
    mix = _dil_mix_bwd(do_a, os_, lses, 512)
    dil_b = [_dil_bwd(qkv, mix[g], lses[g], mix[3 + g], g) for g in range(len(DIL_GROUPS))]
    dq_b, dk_b, dv_b = _sb_bwd(qkv, do_b, tot_b)
    dproj = jnp.concatenate(
        [d[0] for d in dil_b] + [d[1] for d in dil_b] + [d[2] for d in dil_b]
        + [dq_b, dk_b.astype(BF16), dv_b.astype(BF16), dgl], axis=1)
    g_w_in = _mm("grad_w_in", h1, dproj, ta=True, out_dtype=F32, tm=512, tn=IN_COLS // 2, tk=512)
    grad_x, gg1 = _rowk("in_proj_bwd", a=dproj, w=w_in_f, nt=True, tm=512, tk=IN_COLS // 2, rows=[x0, dx1],
                        consts=[g1], row_outs=[(D_MODEL, F32)], acc_outs=[D_MODEL], epilogue=norm_bwd)

    g_full = {"w_in": g_w_in, "w_up_dil": g_w_ud, "w_up_sb": g_w_us, "w_out": g_w_out,
              "w_mlp_in": g_w_mlp_in, "w_mlp_out": g_w_mlp_out}
    small_part = _pack_small({"norm_mix_g": gg1, "b_gate": g_bg, "norm_mlp_g": gg2, "norm_final_g": gg3}, loss_part)
    chunks = [_chunk(n, g_full[n]).astype(BF16) for n in _SHARDED]
    chunks.append(jnp.broadcast_to(small_part[None], (N_DEV, 1, _SMALL_N)))
    *parts, small_parts = _exchange(chunks)

    tags = ("grad_", "delta_", "new_m_", "new_v_")
    outs = {}
    for n, p in zip(_SHARDED, parts):
        res = _reduce_adamw("adamw_" + n, p, w_shards[n], m_shards[n], v_shards[n], 128)
        for tag, val in zip(tags, res):
            outs[tag + n] = val.reshape(given[n].shape)
    small_w = _pack_small(given, jnp.zeros((1, LANES), F32))
    small_m = _pack_small({n: given["m_" + n] for n, _ in _SMALL}, jnp.zeros((1, LANES), F32))
    small_v = _pack_small({n: given["v_" + n] for n, _ in _SMALL}, jnp.ones((1, LANES), F32))
    small_res = _reduce_adamw("adamw_replicated", small_parts, small_w, small_m, small_v, 8)

    small_shapes = {n: given[n].shape for n, _ in _SMALL}
    for tag, small in zip(tags, small_res):
        small_vals, tail = _unpack_small(small, small_shapes)
        for n, val in small_vals.items():
            outs[tag + n] = val
        if tag == "grad_":
            loss = tail[0, 0]
    names = ["norm_mix_g", "w_in", "b_gate", "w_up_dil", "w_up_sb", "w_out", "norm_mlp_g", "w_mlp_in", "w_mlp_out",
             "norm_final_g"]
    return (loss, grad_x.reshape(x.shape), *[outs["grad_" + n] for n in names], *[outs["delta_" + n] for n in names],
            *[outs["new_m_" + n] for n in names], *[outs["new_v_" + n] for n in names])
```

```python
import functools
import math

import jax
import jax.numpy as jnp
from jax import lax
from jax.experimental import pallas as pl
from jax.experimental.pallas import tpu as pltpu

_pcall = pl.pallas_call

F32 = jnp.float32
BF16 = jnp.bfloat16

D_MODEL = 1024
HEAD_DIM = 64
DIL_GROUPS = ((128, 1), (512, 4), (2048, 16))
DIL_HEADS_PER_GROUP = 4
N_DIL_HEADS = 12
N_SB_HEADS = 8
DIL_WIDTH = 768
DIL_OUT_WIDTH = 256
SB_WIDTH = 512
D_FF = 4096
BLOCK = 128
RMS_EPS = 1e-6
NEG_INF = -1e30
QKV_COLS = 3 * DIL_WIDTH + 3 * SB_WIDTH
IN_COLS = QKV_COLS + 2 * D_MODEL
N_DEV = 8

ADAM_LR = 0.001
ADAM_B1 = 0.9
ADAM_B2 = 0.999
ADAM_EPS = 1e-08
ADAM_WD = 0.01
ADAM_STEP = 10

VMEM_LIMIT = 56 * 1024 * 1024
SB_TK = 256
SB_TQ_FWD = 512
SB_TQ_BWD = 256
LANES = 128

_ARB = pltpu.ARBITRARY


def _cparams(n_axes, **kw):
    return pltpu.CompilerParams(dimension_semantics=(_ARB,) * n_axes, vmem_limit_bytes=VMEM_LIMIT, **kw)


def _dot(a, b):
    return jnp.dot(a, b, preferred_element_type=F32)


def _dot_nt(a, b):
    return lax.dot_general(a, b, (((1,), (1,)), ((), ())), preferred_element_type=F32)


def _dot_tn(a, b):
    return lax.dot_general(a, b, (((0,), (0,)), ((), ())), preferred_element_type=F32)


def _split_hi_lo(x):
    hi = x.astype(BF16)
    lo = (x - hi.astype(F32)).astype(BF16)
    return hi, lo


def _dot_hi_lo(x, m):
    hi, lo = _split_hi_lo(x)
    return _dot(hi, m) + _dot(lo, m)


def _sigmoid(x):
    return 1.0 / (1.0 + jnp.exp(-x))


def _mm(name, a, b, *, ta=False, tb=False, out_dtype, tm, tn, tk, epilogue=None, extra=None):
    m = a.shape[1] if ta else a.shape[0]
    k = a.shape[0] if ta else a.shape[1]
    n = b.shape[0] if tb else b.shape[1]
    assert (b.shape[1] if tb else b.shape[0]) == k
    tm, tn, tk = min(tm, m), min(tn, n), min(tk, k)
    assert m % tm == 0 and n % tn == 0 and k % tk == 0, (name, m, n, k, tm, tn, tk)
    nk = k // tk
    dn = (((0 if ta else 1,), (1 if tb else 0,)), ((), ()))

    def body(*refs):
        if extra is not None:
            a_ref, b_ref, e_ref, o_ref = refs[:4]
        else:
            a_ref, b_ref, o_ref = refs[:3]
            e_ref = None

        def finish(r):
            if epilogue is not None:
                r = epilogue(r, None if e_ref is None else e_ref[...])
            o_ref[...] = r.astype(out_dtype)

        part = lax.dot_general(a_ref[...].astype(BF16), b_ref[...].astype(BF16), dn, preferred_element_type=F32)
        if nk == 1:
            finish(part)
        else:
            acc_ref = refs[-1]
            kk = pl.program_id(2)

            @pl.when(kk == 0)
            def _():
                acc_ref[...] = part

            @pl.when(kk > 0)
            def _():
                acc_ref[...] += part

            @pl.when(kk == nk - 1)
            def _():
                finish(acc_ref[...])

    a_spec = pl.BlockSpec((tk, tm), lambda i, j, kk: (kk, i)) if ta else pl.BlockSpec((tm, tk), lambda i, j, kk: (i, kk))
    b_spec = pl.BlockSpec((tn, tk), lambda i, j, kk: (j, kk)) if tb else pl.BlockSpec((tk, tn), lambda i, j, kk: (kk, j))
    o_spec = pl.BlockSpec((tm, tn), lambda i, j, kk: (i, j))
    in_specs = [a_spec, b_spec]
    args = [a, b]
    if extra is not None:
        in_specs.append(o_spec)
        args.append(extra)
    return _pcall(
        body,
        name=name,
        grid=(m // tm, n // tn, nk),
        in_specs=in_specs,
        out_specs=o_spec,
        out_shape=jax.ShapeDtypeStruct((m, n), out_dtype),
        scratch_shapes=[pltpu.VMEM((tm, tn), F32)] if nk > 1 else [],
        compiler_params=_cparams(3),
    )(*args)


def _rowk(name, *, a=None, w=None, nt=False, tm, tk=None, rows=(), consts=(), row_outs=(), acc_outs=(), epilogue):
    has_mm = a is not None
    m = a.shape[0] if has_mm else rows[0].shape[0]
    assert m % tm == 0
    nm = m // tm
    if has_mm:
        k = a.shape[1]
        n = w.shape[0] if nt else w.shape[1]
        tk = min(tk, k)
        assert k % tk == 0
        nk = k // tk
    else:
        nk = 1
    n_rows, n_consts, n_ro, n_ao = len(rows), len(consts), len(row_outs), len(acc_outs)

    def body(*refs):
        pos = 0
        if has_mm:
            a_ref, w_ref = refs[0], refs[1]
            pos = 2
        row_refs = refs[pos:pos + n_rows]
        pos += n_rows
        const_refs = refs[pos:pos + n_consts]
        pos += n_consts
        ro_refs = refs[pos:pos + n_ro]
        pos += n_ro
        ao_refs = refs[pos:pos + n_ao]
        pos += n_ao
        i = pl.program_id(0)
        kk = pl.program_id(1)

        def finish(acc):
            ro_vals, ao_vals = epilogue(acc, [r[...] for r in row_refs], [c[...] for c in const_refs])
            for r, v in zip(ro_refs, ro_vals):
                r[...] = v.astype(r.dtype)
            for r, v in zip(ao_refs, ao_vals):

                @pl.when(i == 0)
                def _(r=r, v=v):
                    r[...] = v

                @pl.when(i > 0)
                def _(r=r, v=v):
                    r[...] += v

        if not has_mm:
            finish(None)
            return
        if nt:
            part = _dot_nt(a_ref[...].astype(BF16), w_ref[...])
        else:
            part = _dot(a_ref[...].astype(BF16), w_ref[...])
        if nk == 1:
            finish(part)
        else:
            acc_ref = refs[pos]

            @pl.when(kk == 0)
            def _():
                acc_ref[...] = part

            @pl.when(kk > 0)
            def _():
                acc_ref[...] += part

            @pl.when(kk == nk - 1)
            def _():
                finish(acc_ref[...])

    in_specs, args = [], []
    if has_mm:
        in_specs.append(pl.BlockSpec((tm, tk), lambda i, kk: (i, kk)))
        in_specs.append(pl.BlockSpec((n, tk), lambda i, kk: (0, kk)) if nt else pl.BlockSpec((tk, n), lambda i, kk: (kk, 0)))
        args += [a, w]
    for r in rows:
        in_specs.append(pl.BlockSpec((tm, r.shape[1]), lambda i, kk: (i, 0)))
        args.append(r)
    for c in consts:
        in_specs.append(pl.BlockSpec(c.shape, lambda i, kk: (0,) * c.ndim))
        args.append(c)
    out_specs, out_shape = [], []
    for width, dt in row_outs:
        out_specs.append(pl.BlockSpec((tm, width), lambda i, kk: (i, 0)))
        out_shape.append(jax.ShapeDtypeStruct((m, width), dt))
    for width in acc_outs:
        out_specs.append(pl.BlockSpec((1, width), lambda i, kk: (0, 0)))
        out_shape.append(jax.ShapeDtypeStruct((1, width), F32))
    return _pcall(
        body,
        name=name,
        grid=(nm, nk),
        in_specs=in_specs,
        out_specs=out_specs,
        out_shape=out_shape,
        scratch_shapes=[pltpu.VMEM((tm, n), F32)] if (has_mm and nk > 1) else [],
        compiler_params=_cparams(2),
    )(*args)


def _rms_stats(x):
    r = lax.rsqrt(jnp.mean(x * x, axis=-1, keepdims=True) + RMS_EPS)
    return r, x * r


def _rms_bwd(dh, xh, r, g):
    gy = dh * g
    dx = r * (gy - xh * jnp.mean(gy * xh, axis=-1, keepdims=True))
    return dx, jnp.sum(dh * xh, axis=0, keepdims=True)


def _alibi_slope(head):
    return 2.0 ** (-8.0 * (head + 1) / N_DIL_HEADS)


def _dil_masks(i):
    qi = lax.broadcasted_iota(jnp.int32, (BLOCK, 2 * BLOCK), 0)
    kj = lax.broadcasted_iota(jnp.int32, (BLOCK, 2 * BLOCK), 1)
    steps = qi + BLOCK - kj
    valid = (steps >= 0) & (steps <= BLOCK) & ((kj >= BLOCK) | (i > 0))
    return steps.astype(F32), valid


def _dil_view(qkv, group):
    _, dilation = DIL_GROUPS[group]
    if dilation == 1:
        return qkv, QKV_COLS // DIL_OUT_WIDTH, (group, 3 + group, 6 + group)
    w = DIL_OUT_WIDTH
    own = jnp.concatenate([qkv[:, (3 * part + group) * w:(3 * part + group + 1) * w] for part in range(3)], axis=1)
    return own.reshape(qkv.shape[0] // dilation, dilation * 3 * w), 3, (0, 1, 2)


def _dil_specs(ncb, cols, clamp):
    def cur(col):
        return pl.BlockSpec((BLOCK, DIL_OUT_WIDTH), lambda r, i: (clamp(i), r * ncb + col))

    def prev(col):
        return pl.BlockSpec((BLOCK, DIL_OUT_WIDTH), lambda r, i: (jnp.maximum(clamp(i) - 1, 0), r * ncb + col))

    return [cur(cols[0]), cur(cols[1]), prev(cols[1]), cur(cols[2]), prev(cols[2])]


def _dil_fwd(qkv, group):
    window, dilation = DIL_GROUPS[group]
    s = qkv.shape[0]
    sub = s // dilation
    nb = sub // BLOCK
    assert nb * BLOCK * dilation == s and window // dilation == BLOCK
    slopes = [_alibi_slope(group * DIL_HEADS_PER_GROUP + h) * dilation for h in range(DIL_HEADS_PER_GROUP)]

    def body(q_ref, kc_ref, kp_ref, vc_ref, vp_ref, o_ref, lse_ref):
        i = pl.program_id(1)
        q = q_ref[...]
        kk = jnp.concatenate([kp_ref[...], kc_ref[...]], axis=0)
        vv = jnp.concatenate([vp_ref[...], vc_ref[...]], axis=0)
        head_id = lax.broadcasted_iota(jnp.int32, (1, DIL_OUT_WIDTH), 1) // HEAD_DIM
        steps, valid = _dil_masks(i)
        heads = range(DIL_HEADS_PER_GROUP)
        scores = [_dot_nt(jnp.where(head_id == h, q, jnp.zeros_like(q)), kk) for h in heads]
        ps, lses = [], []
        for h in heads:
            logits = scores[h] * (1.0 / math.sqrt(HEAD_DIM)) - slopes[h] * steps
            logits = jnp.where(valid, logits, NEG_INF)
            mx = jnp.max(logits, axis=1, keepdims=True)
            e = jnp.exp(logits - mx)
            den = jnp.sum(e, axis=1, keepdims=True)
            lses.append(mx + jnp.log(den))
            ps.append((e * (1.0 / den)).astype(BF16))
        outs = [_dot(ps[h], vv) for h in heads]
        o, lse_all = outs[0], lses[0]
        for h in heads[1:]:
            o = jnp.where(head_id == h, outs[h], o)
            lse_all = jnp.where(head_id == h, lses[h], lse_all)
        o_ref[...] = o
        lse_ref[...] = jnp.broadcast_to(lse_all, o.shape)

    qkv_v, ncb, cols = _dil_view(qkv, group)
    out_spec = pl.BlockSpec((BLOCK, DIL_OUT_WIDTH), lambda r, i: (i, r))
    o, lse = _pcall(
        body,
        name=f"dil_fwd_g{group}",
        grid=(dilation, nb),
        in_specs=_dil_specs(ncb, cols, lambda i: i),
        out_specs=[out_spec, out_spec],
        out_shape=[jax.ShapeDtypeStruct((sub, dilation * DIL_OUT_WIDTH), F32)] * 2,
        compiler_params=_cparams(2),
    )(qkv_v, qkv_v, qkv_v, qkv_v, qkv_v)
    return o.reshape(s, DIL_OUT_WIDTH), lse.reshape(s, DIL_OUT_WIDTH)


def _dil_bwd(qkv, do_g, lse_g, dterm_g, group):
    window, dilation = DIL_GROUPS[group]
    s = qkv.shape[0]
    sub = s // dilation
    nb = sub // BLOCK
    slopes = [_alibi_slope(group * DIL_HEADS_PER_GROUP + h) * dilation for h in range(DIL_HEADS_PER_GROUP)]
    scale = 1.0 / math.sqrt(HEAD_DIM)

    def body(q_ref, kc_ref, kp_ref, vc_ref, vp_ref, do_ref, lse_ref, dt_ref, dq_ref, dk_ref, dv_ref, ck_ref, cv_ref):
        i = pl.program_id(1)

        @pl.when(i == 0)
        def _():
            ck_ref[...] = jnp.zeros_like(ck_ref)
            cv_ref[...] = jnp.zeros_like(cv_ref)

        @pl.when(i < nb)
        def _():
            q = q_ref[...]
            do = do_ref[...]
            lse_all = lse_ref[...]
            dt_all = dt_ref[...]
            kk = jnp.concatenate([kp_ref[...], kc_ref[...]], axis=0)
            vv = jnp.concatenate([vp_ref[...], vc_ref[...]], axis=0)
            lane = lax.broadcasted_iota(jnp.int32, (1, DIL_OUT_WIDTH), 1)
            head_id = lane // HEAD_DIM
            steps, valid = _dil_masks(i)
            heads = range(DIL_HEADS_PER_GROUP)
            qms = [jnp.where(head_id == h, q, jnp.zeros_like(q)) for h in heads]
            doms = [jnp.where(head_id == h, do, jnp.zeros_like(do)) for h in heads]
            scores = [_dot_nt(qms[h], kk) for h in heads]
            dps = [_dot_nt(doms[h], vv) for h in heads]
            pbs, dss = [], []
            for h in heads:
                first = lane == h * HEAD_DIM
                lse = jnp.sum(jnp.where(first, lse_all, 0.0), axis=1, keepdims=True)
                dt = jnp.sum(jnp.where(first, dt_all, 0.0), axis=1, keepdims=True)
                logits = scores[h] * scale - slopes[h] * steps
                p = jnp.where(valid, jnp.exp(jnp.where(valid, logits, NEG_INF) - lse), 0.0)
                pbs.append(p.astype(BF16))
                dss.append((p * (dps[h] + dt) * scale).astype(BF16))
            dqs = [_dot(dss[h], kk) for h in heads]
            dks = [_dot_tn(dss[h], qms[h]) for h in heads]
            dvs = [_dot_tn(pbs[h], doms[h]) for h in heads]
            dq = dqs[0]
            for h in heads[1:]:
                dq = jnp.where(head_id == h, dqs[h], dq)
            dkk = (dks[0] + dks[1]) + (dks[2] + dks[3])
            dvv = (dvs[0] + dvs[1]) + (dvs[2] + dvs[3])
            dq_ref[...] = dq.astype(dq_ref.dtype)
            dk_ref[...] = (ck_ref[...] + dkk[:BLOCK]).astype(dk_ref.dtype)
            dv_ref[...] = (cv_ref[...] + dvv[:BLOCK]).astype(dv_ref.dtype)
            ck_ref[...] = dkk[BLOCK:]
            cv_ref[...] = dvv[BLOCK:]

        @pl.when(i == nb)
        def _():
            dk_ref[...] = ck_ref[...].astype(dk_ref.dtype)
            dv_ref[...] = cv_ref[...].astype(dv_ref.dtype)

    clamp = lambda i: jnp.minimum(i, nb - 1)
    qkv_v, ncb, cols = _dil_view(qkv, group)
    view = lambda t: t.reshape(sub, dilation * DIL_OUT_WIDTH)
    row_spec = pl.BlockSpec((BLOCK, DIL_OUT_WIDTH), lambda r, i: (clamp(i), r))
    late_spec = pl.BlockSpec((BLOCK, DIL_OUT_WIDTH), lambda r, i: (jnp.maximum(i - 1, 0), r))
    dq, dk, dv = _pcall(
        body,
        name=f"dil_bwd_g{group}",
        grid=(dilation, nb + 1),
        in_specs=_dil_specs(ncb, cols, clamp) + [row_spec, row_spec, row_spec],
        out_specs=[row_spec, late_spec, late_spec],
        out_shape=[jax.ShapeDtypeStruct((sub, dilation * DIL_OUT_WIDTH), BF16)] * 3,
        scratch_shapes=[pltpu.VMEM((BLOCK, DIL_OUT_WIDTH), F32)] * 2,
        compiler_params=_cparams(2),
    )(qkv_v, qkv_v, qkv_v, qkv_v, qkv_v, view(do_g), view(lse_g), view(dterm_g))
    return dq.reshape(s, DIL_OUT_WIDTH), dk.reshape(s, DIL_OUT_WIDTH), dv.reshape(s, DIL_OUT_WIDTH)


def _head_block_ones():
    r = lax.broadcasted_iota(jnp.int32, (DIL_OUT_WIDTH, DIL_OUT_WIDTH), 0) // HEAD_DIM
    c = lax.broadcasted_iota(jnp.int32, (DIL_OUT_WIDTH, DIL_OUT_WIDTH), 1) // HEAD_DIM
    return jnp.where(r == c, 1.0, 0.0).astype(BF16)


def _dil_mix_weights(l0, l1, l2):
    mx = jnp.maximum(jnp.maximum(l0, l1), l2)
    e0, e1, e2 = jnp.exp(l0 - mx), jnp.exp(l1 - mx), jnp.exp(l2 - mx)
    inv = 1.0 / (e0 + e1 + e2)
    return e0 * inv, e1 * inv, e2 * inv


def _dil_mix_fwd(os_, lses, tm):
    def epi(_, rows, consts):
        o0, o1, o2, l0, l1, l2 = rows
        w0, w1, w2 = _dil_mix_weights(l0, l1, l2)
        return [w0 * o0 + w1 * o1 + w2 * o2], []

    (o_a,) = _rowk("dil_mix_fwd", tm=tm, rows=list(os_) + list(lses), row_outs=[(DIL_OUT_WIDTH, BF16)], epilogue=epi)
    return o_a


def _dil_mix_bwd(do_a, os_, lses, tm):
    def epi(_, rows, consts):
        do, o0, o1, o2, l0, l1, l2 = rows
        do = do.astype(F32)
        w0, w1, w2 = _dil_mix_weights(l0, l1, l2)
        mixed = w0 * o0 + w1 * o1 + w2 * o2
        tot = _dot_hi_lo(do * mixed, _head_block_ones())
        return [w0 * do, w1 * do, w2 * do, -w0 * tot, -w1 * tot, -w2 * tot], []

    return _rowk(
        "dil_mix_bwd", tm=tm, rows=[do_a] + list(os_) + list(lses),
        row_outs=[(DIL_OUT_WIDTH, BF16)] * 3 + [(DIL_OUT_WIDTH, F32)] * 3, epilogue=epi)


_SB_Q0 = 3 * DIL_WIDTH // LANES
_SB_K0 = _SB_Q0 + SB_WIDTH // LANES
_SB_V0 = _SB_K0 + SB_WIDTH // LANES


_LOG2E = 1.4426950408889634
_EXP2_CLAMP = 126.0


def _tri(t, op):
    r = lax.broadcasted_iota(jnp.int32, (t, t), 0)
    c = lax.broadcasted_iota(jnp.int32, (t, t), 1)
    return jnp.where(op(r, c), 1.0, 0.0).astype(BF16)


def _softplus2(z2):
    return jnp.maximum(z2, jnp.log2(1.0 + jnp.exp2(jnp.minimum(z2, _EXP2_CLAMP))))


def _sb_fwd(qkv):
    s = qkv.shape[0]
    t, tq = SB_TK, min(SB_TQ_FWD, s)
    assert tq in (t, 2 * t) and s % (2 * t) == 0
    nq = s // tq
    n_pairs = SB_WIDTH // LANES

    def body(q_ref, k_ref, v_ref, o_ref, tot_ref):
        i = pl.program_id(1)
        q = q_ref[...] * (1.0 / math.sqrt(HEAD_DIM))
        lane_hi = lax.broadcasted_iota(jnp.int32, (1, LANES), 1) // HEAD_DIM
        later = _tri(t, lambda r, c: r > c)
        row = lax.broadcasted_iota(jnp.int32, (tq, t), 0)
        col = lax.broadcasted_iota(jnp.int32, (tq, t), 1)
        qms = [jnp.where(lane_hi == hh, q, jnp.zeros_like(q)) for hh in range(2)]

        def step(jj, carry, diag):
            tiles = (2 * jj + 1, 2 * jj)
            offs = [pl.multiple_of(j * t, t) for j in tiles]
            ks = [k_ref[pl.ds(off, t), :] for off in offs]
            vs = [v_ref[pl.ds(off, t), :] for off in offs]
            masks = [(j * t + col) < (i * tq + row) for j in tiles] if diag else None
            chains = [(n, hh) for n in range(2) for hh in range(2)]
            z2s = [_dot_nt(qms[hh], ks[n]) * _LOG2E for n, hh in chains]
            sps, lposs = [], []
            for (n, hh), z2 in zip(chains, z2s):
                sp = _softplus2(z2)
                lposs.append(z2 - sp)
                sps.append(jnp.where(masks[n], sp, 0.0) if diag else sp)
            sufs = [_dot(sp.astype(BF16), later) for sp in sps]
            cs = [carry[0], carry[2]]
            accs = [carry[1], carry[3]]
            for idx, (n, hh) in enumerate(chains):
                a = jnp.exp2(lposs[idx] - sufs[idx] - cs[hh])
                if diag:
                    a = jnp.where(masks[n], a, 0.0)
                accs[hh] = accs[hh] + _dot(a.astype(BF16), vs[n])
                cs[hh] = cs[hh] + jnp.sum(sps[idx], axis=1, keepdims=True)
            return cs[0], accs[0], cs[1], accs[1]

        zc, za = jnp.zeros((tq, 1), F32), jnp.zeros((tq, LANES), F32)
        last = (i * tq) // (2 * t)
        carry = step(last, (zc, za, zc, za), True)
        carry = lax.fori_loop(0, last, lambda n, ca: step(last - 1 - n, ca, False), carry)
        out = jnp.where(lane_hi == 0, carry[1], carry[3])
        tot = jnp.where(lane_hi == 0, carry[0], carry[2])
        o_ref[...] = out.astype(o_ref.dtype)
        tot_ref[...] = tot

    o, tot = _pcall(
        body,
        name="sb_fwd",
        grid=(n_pairs, nq),
        in_specs=[
            pl.BlockSpec((tq, LANES), lambda p, i: (i, _SB_Q0 + p)),
            pl.BlockSpec((s, LANES), lambda p, i: (0, _SB_K0 + p)),
            pl.BlockSpec((s, LANES), lambda p, i: (0, _SB_V0 + p)),
        ],
        out_specs=[pl.BlockSpec((tq, LANES), lambda p, i: (i, p))] * 2,
        out_shape=[jax.ShapeDtypeStruct((s, SB_WIDTH), BF16), jax.ShapeDtypeStruct((s, SB_WIDTH), F32)],
        compiler_params=_cparams(2),
    )(qkv, qkv, qkv)
    return o, tot


def _sb_bwd(qkv, do_b, tot_b):
    s = qkv.shape[0]
    t, tq = SB_TK, min(SB_TQ_BWD, s)
    assert tq in (t, 2 * t) and s % (2 * t) == 0
    nq = s // tq
    n_pairs = SB_WIDTH // LANES
    scale = 1.0 / math.sqrt(HEAD_DIM)

    def body(q_ref, k_ref, v_ref, do_ref, tot_ref, dq_ref, dk_ref, dv_ref):
        i = pl.program_id(1)

        @pl.when(i == 0)
        def _():
            dk_ref[...] = jnp.zeros_like(dk_ref)
            dv_ref[...] = jnp.zeros_like(dv_ref)

        q = q_ref[...] * scale
        do = do_ref[...]
        tot_all = tot_ref[...]
        lane = lax.broadcasted_iota(jnp.int32, (1, LANES), 1)
        lane_hi = lane // HEAD_DIM
        later = _tri(t, lambda r, c: r > c)
        before = _tri(t, lambda r, c: r < c)
        row = lax.broadcasted_iota(jnp.int32, (tq, t), 0)
        col = lax.broadcasted_iota(jnp.int32, (tq, t), 1)
        qms = [jnp.where(lane_hi == hh, q, jnp.zeros_like(q)) for hh in range(2)]
        doms = [jnp.where(lane_hi == hh, do, jnp.zeros_like(do)) for hh in range(2)]
        tots = [jnp.sum(jnp.where(lane == hh * HEAD_DIM, tot_all, 0.0), axis=1, keepdims=True) for hh in range(2)]

        def step(jj, carry, diag):
            tiles = (2 * jj, 2 * jj + 1)
            offs = [pl.multiple_of(j * t, t) for j in tiles]
            ks = [k_ref[pl.ds(off, t), :] for off in offs]
            vs = [v_ref[pl.ds(off, t), :] for off in offs]
            masks = [(j * t + col) < (i * tq + row) for j in tiles] if diag else None
            chains = [(n, hh) for n in range(2) for hh in range(2)]
            z2s = [_dot_nt(qms[hh], ks[n]) * _LOG2E for n, hh in chains]
            sps, sigs = [], []
            for (n, hh), z2 in zip(chains, z2s):
                sp = _softplus2(z2)
                sigs.append(jnp.exp2(z2 - sp))
                sps.append(jnp.where(masks[n], sp, 0.0) if diag else sp)
            sufs = [_dot(sp.astype(BF16), later) for sp in sps]
            das = [_dot_nt(doms[hh], vs[n]) for n, hh in chains]
            cls = [carry[0], carry[3]]
            cgs = [carry[1], carry[4]]
            accs = [carry[2], carry[5]]
            gs, abs_, cg_at = [], [], []
            for idx, (n, hh) in enumerate(chains):
                cls[hh] = cls[hh] + jnp.sum(sps[idx], axis=1, keepdims=True)
                a = sigs[idx] * jnp.exp2(-sufs[idx] - (tots[hh] - cls[hh]))
                if diag:
                    a = jnp.where(masks[n], a, 0.0)
                g = a * das[idx]
                gs.append(g)
                abs_.append(a.astype(BF16))
                cg_at.append(cgs[hh])
                cgs[hh] = cgs[hh] + jnp.sum(g, axis=1, keepdims=True)
            prefs = [_dot(g.astype(BF16), before) for g in gs]
            dvs = [_dot_tn(abs_[idx], doms[hh]) for idx, (n, hh) in enumerate(chains)]
            dzs = []
            for idx, (n, hh) in enumerate(chains):
                g = gs[idx]
                dz = g - sigs[idx] * (g + prefs[idx] + cg_at[idx])
                if diag:
                    dz = jnp.where(masks[n], dz, 0.0)
                dzs.append(dz.astype(BF16))
            for idx, (n, hh) in enumerate(chains):
                accs[hh] = accs[hh] + _dot(dzs[idx], ks[n])
            dks = [_dot_tn(dzs[idx], qms[hh]) for idx, (n, hh) in enumerate(chains)]
            for n in range(2):
                dk_ref[pl.ds(offs[n], t), :] += dks[2 * n] + dks[2 * n + 1]
                dv_ref[pl.ds(offs[n], t), :] += dvs[2 * n] + dvs[2 * n + 1]
            return cls[0], cgs[0], accs[0], cls[1], cgs[1], accs[1]

        zc, za = jnp.zeros((tq, 1), F32), jnp.zeros((tq, LANES), F32)
        last = (i * tq) // (2 * t)
        carry = lax.fori_loop(0, last, lambda jj, ca: step(jj, ca, False), (zc, zc, za, zc, zc, za))
        carry = step(last, carry, True)
        dq = jnp.where(lane_hi == 0, carry[2], carry[5])
        dq_ref[...] = (dq * scale).astype(dq_ref.dtype)

    row_spec = pl.BlockSpec((tq, LANES), lambda p, i: (i, p))
    full_spec = pl.BlockSpec((s, LANES), lambda p, i: (0, p))
    return _pcall(
        body,
        name="sb_bwd",
        grid=(n_pairs, nq),
        in_specs=[
            pl.BlockSpec((tq, LANES), lambda p, i: (i, _SB_Q0 + p)),
            pl.BlockSpec((s, LANES), lambda p, i: (0, _SB_K0 + p)),
            pl.BlockSpec((s, LANES), lambda p, i: (0, _SB_V0 + p)),
            row_spec, row_spec,
        ],
        out_specs=[row_spec, full_spec, full_spec],
        out_shape=[jax.ShapeDtypeStruct((s, SB_WIDTH), BF16), jax.ShapeDtypeStruct((s, SB_WIDTH), F32),
                   jax.ShapeDtypeStruct((s, SB_WIDTH), F32)],
        compiler_params=_cparams(2),
    )(qkv, qkv, qkv, do_b, tot_b)


def _gates(gl, bg):
    return _sigmoid(gl[:, :D_MODEL] + bg[:, :D_MODEL]), _sigmoid(gl[:, D_MODEL:] + bg[:, D_MODEL:])


def _mixer_fwd(o_a, o_b, gl, x0, bg, g2, w_ud, w_us, w_out, tm):
    def epi(_, rows, consts):
        oa, ob, glv, x = rows
        bgv, g2v, wud, wus, wout = consts
        ga, gb = _gates(glv, bgv)
        merged = ga * _dot(oa, wud) + gb * _dot(ob, wus)
        x1 = x + _dot(merged.astype(BF16), wout)
        r, xh = _rms_stats(x1)
        return [x1, xh * g2v], []

    return _rowk("mixer_fwd", tm=tm, rows=[o_a, o_b, gl, x0], consts=[bg, g2, w_ud, w_us, w_out],
                 row_outs=[(D_MODEL, F32), (D_MODEL, BF16)], epilogue=epi)


def _mixer_bwd(dx1, o_a, o_b, gl, bg, w_ud, w_us, w_out, tm):
    s = dx1.shape[0]
    nm = s // tm

    def body(dx_ref, oa_ref, ob_ref, gl_ref, bg_ref, wud_ref, wus_ref, wout_ref,
             doa_ref, dob_ref, dgl_ref, gwout_ref, gwud_ref, gwus_ref, gbg_ref):
        i = pl.program_id(0)
        dxb = dx_ref[...].astype(BF16)
        oa, ob = oa_ref[...], ob_ref[...]
        ga, gb = _gates(gl_ref[...], bg_ref[...])
        ua, ub = _dot(oa, wud_ref[...]), _dot(ob, wus_ref[...])
        merged = (ga * ua + gb * ub).astype(BF16)
        dm = _dot_nt(dxb, wout_ref[...])
        dua = (dm * ga).astype(BF16)
        dub = (dm * gb).astype(BF16)
        dgla = dm * ua * ga * (1.0 - ga)
        dglb = dm * ub * gb * (1.0 - gb)
        doa_ref[...] = _dot_nt(dua, wud_ref[...]).astype(doa_ref.dtype)
        dob_ref[...] = _dot_nt(dub, wus_ref[...]).astype(dob_ref.dtype)
        dgl_ref[:, :D_MODEL] = dgla.astype(dgl_ref.dtype)
        dgl_ref[:, D_MODEL:] = dglb.astype(dgl_ref.dtype)
        parts = [(gwout_ref, _dot_tn(merged, dxb)), (gwud_ref, _dot_tn(oa, dua)), (gwus_ref, _dot_tn(ob, dub))]
        for r, v in parts:

            @pl.when(i == 0)
            def _(r=r, v=v):
                r[...] = v

            @pl.when(i > 0)
            def _(r=r, v=v):
                r[...] += v

        sa = jnp.sum(dgla, axis=0, keepdims=True)
        sb = jnp.sum(dglb, axis=0, keepdims=True)

        @pl.when(i == 0)
        def _():
            gbg_ref[:, :D_MODEL] = sa
            gbg_ref[:, D_MODEL:] = sb

        @pl.when(i > 0)
        def _():
            gbg_ref[:, :D_MODEL] += sa
            gbg_ref[:, D_MODEL:] += sb

    row = lambda w: pl.BlockSpec((tm, w), lambda i: (i, 0))
    full = lambda a: pl.BlockSpec(a.shape, lambda i: (0, 0))
    fshape = lambda r, c: jax.ShapeDtypeStruct((r, c), F32)
    return _pcall(
        body,
        name="mixer_bwd",
        grid=(nm,),
        in_specs=[row(D_MODEL), row(DIL_OUT_WIDTH), row(SB_WIDTH), row(2 * D_MODEL),
                  full(bg), full(w_ud), full(w_us), full(w_out)],
        out_specs=[row(DIL_OUT_WIDTH), row(SB_WIDTH), row(2 * D_MODEL),
                   pl.BlockSpec((D_MODEL, D_MODEL), lambda i: (0, 0)),
                   pl.BlockSpec((DIL_OUT_WIDTH, D_MODEL), lambda i: (0, 0)),
                   pl.BlockSpec((SB_WIDTH, D_MODEL), lambda i: (0, 0)),
                   pl.BlockSpec((1, 2 * D_MODEL), lambda i: (0, 0))],
        out_shape=[jax.ShapeDtypeStruct((s, DIL_OUT_WIDTH), BF16), jax.ShapeDtypeStruct((s, SB_WIDTH), BF16),
                   jax.ShapeDtypeStruct((s, 2 * D_MODEL), BF16),
                   fshape(D_MODEL, D_MODEL), fshape(DIL_OUT_WIDTH, D_MODEL), fshape(SB_WIDTH, D_MODEL),
                   fshape(1, 2 * D_MODEL)],
        compiler_params=_cparams(1),
    )(dx1, o_a, o_b, gl, bg, w_ud, w_us, w_out)


_HBM = pl.BlockSpec(memory_space=pltpu.HBM)
_MESH = pl.DeviceIdType.MESH


def _all_gather(shards):
    n = len(shards)

    def body(*refs):
        x_refs, out_refs = refs[:n], refs[n:2 * n]
        send_sems, recv_sems, local_sems = refs[2 * n:]
        x, y, c = lax.axis_index("x"), lax.axis_index("y"), lax.axis_index("c")
        me, sibling = (x, y, c), (x, y, 1 - c)
        chips = [(1 - x, y), (x, 1 - y), (1 - x, 1 - y)]

        def slot(a, px, py, pc):
            return out_refs[a].at[4 * px + 2 * py + pc]

        def copy(a, k, block, to, own=False):
            return pltpu.make_async_remote_copy(
                src_ref=x_refs[a] if own else slot(a, *block), dst_ref=slot(a, *block),
                send_sem=send_sems.at[7 * a + k], recv_sem=recv_sems.at[7 * a + k], device_id=to, device_id_type=_MESH)

        mine = [pltpu.make_async_copy(x_refs[a], slot(a, *me), local_sems.at[a]) for a in range(n)]
        for cp in mine:
            cp.start()
        first = []
        for a in range(n):
            first.append(copy(a, 0, me, sibling, own=True))
            first += [copy(a, 1 + j, me, (*chip, c), own=True) for j, chip in enumerate(chips)]
        for cp in first:
            cp.start()
        passed = []
        for a in range(n):
            for j, chip in enumerate(chips):
                copy(a, 1 + j, (*chip, c), me).wait_recv()
                passed.append(copy(a, 4 + j, (*chip, c), sibling))
                passed[-1].start()
        for a in range(n):
            copy(a, 0, sibling, me).wait_recv()
            for j, chip in enumerate(chips):
                copy(a, 4 + j, (*chip, 1 - c), me).wait_recv()
        for cp in first + passed:
            cp.wait_send()
        for cp in mine:
            cp.wait()

    return _pcall(
        body,
        name="all_gather_weights",
        in_specs=[_HBM] * n,
        out_specs=[_HBM] * n,
        out_shape=[jax.ShapeDtypeStruct((N_DEV,) + s.shape, s.dtype) for s in shards],
        scratch_shapes=[pltpu.SemaphoreType.DMA((7 * n,)), pltpu.SemaphoreType.DMA((7 * n,)),
                        pltpu.SemaphoreType.DMA((n,))],
    )(*shards)


def _exchange(chunks):
    n = len(chunks)

    def body(*refs):
        g_refs, o_refs = refs[:n], refs[n:2 * n]
        send_sems, recv_sems, local_sems = refs[2 * n:]
        x, y, c = lax.axis_index("x"), lax.axis_index("y"), lax.axis_index("c")
        me = 4 * x + 2 * y + c
        own = [pltpu.make_async_copy(g_refs[a].at[me], o_refs[a].at[me], local_sems.at[a]) for a in range(n)]
        for cp in own:
            cp.start()
        copies = []
        for a in range(n):
            for k in range(1, N_DEV):
                px, py, pc = x ^ (k >> 2), y ^ ((k >> 1) & 1), c ^ (k & 1)
                peer = 4 * px + 2 * py + pc
                copies.append(pltpu.make_async_remote_copy(
                    src_ref=g_refs[a].at[peer], dst_ref=o_refs[a].at[me], send_sem=send_sems.at[7 * a + k - 1],
                    recv_sem=recv_sems.at[7 * a + k - 1], device_id=(px, py, pc), device_id_type=_MESH))
        for cp in copies:
            cp.start()
        for cp in copies:
            cp.wait()
        for cp in own:
            cp.wait()

    return _pcall(
        body,
        name="exchange_grads",
        in_specs=[_HBM] * n,
        out_specs=[_HBM] * n,
        out_shape=[jax.ShapeDtypeStruct(g.shape, g.dtype) for g in chunks],
        scratch_shapes=[pltpu.SemaphoreType.DMA((7 * n,)), pltpu.SemaphoreType.DMA((7 * n,)),
                        pltpu.SemaphoreType.DMA((n,))],
    )(*chunks)


def _reduce_adamw(name, parts, w, m, v, tr):
    _, rows, cols = parts.shape
    tr = min(tr, rows)
    assert rows % tr == 0
    c1 = 1.0 / (1.0 - ADAM_B1 ** ADAM_STEP)
    c2 = 1.0 / (1.0 - ADAM_B2 ** ADAM_STEP)

    def body(p_ref, w_ref, m_ref, v_ref, g_out, d_out, m_out, v_out):
        g = p_ref[0].astype(F32)
        for d in range(1, N_DEV):
            g = g + p_ref[d].astype(F32)
        mn = ADAM_B1 * m_ref[...] + (1.0 - ADAM_B1) * g
        vn = ADAM_B2 * v_ref[...] + (1.0 - ADAM_B2) * (g * g)
        g_out[...] = g
        m_out[...] = mn
        v_out[...] = vn
        d_out[...] = -ADAM_LR * ((mn * c1) / (jnp.sqrt(vn * c2) + ADAM_EPS) + ADAM_WD * w_ref[...])

    spec = pl.BlockSpec((tr, cols), lambda i: (i, 0))
    return _pcall(
        body,
        name=name,
        grid=(rows // tr,),
        in_specs=[pl.BlockSpec((N_DEV, tr, cols), lambda i: (0, i, 0)), spec, spec, spec],
        out_specs=[spec] * 4,
        out_shape=[jax.ShapeDtypeStruct((rows, cols), F32)] * 4,
        compiler_params=_cparams(1),
    )(parts, w, m, v)


_SHARDED = ("w_in", "w_up_dil", "w_up_sb", "w_out", "w_mlp_in", "w_mlp_out")
_FULL_SHAPES = {"w_in": (D_MODEL, IN_COLS), "w_up_dil": (DIL_OUT_WIDTH, D_MODEL), "w_up_sb": (SB_WIDTH, D_MODEL),
                "w_out": (D_MODEL, D_MODEL), "w_mlp_in": (D_MODEL, D_FF), "w_mlp_out": (D_FF, D_MODEL)}
_ROW_SHARDED = ("w_out", "w_mlp_out")


def _shard_shape(name):
    r, c = _FULL_SHAPES[name]
    return (r // N_DEV, c) if name in _ROW_SHARDED else (r, c // N_DEV)


def _assemble(name, gathered):
    r, c = _shard_shape(name)
    if name in _ROW_SHARDED:
        return gathered.reshape(N_DEV * r, c)
    return gathered.transpose(1, 0, 2).reshape(r, N_DEV * c)


def _chunk(name, full):
    r, c = _shard_shape(name)
    if name in _ROW_SHARDED:
        return full.reshape(N_DEV, r, c)
    return full.reshape(r, N_DEV, c).transpose(1, 0, 2)


_SMALL = (("norm_mix_g", D_MODEL), ("b_gate", 2 * D_MODEL), ("norm_mlp_g", D_MODEL), ("norm_final_g", D_MODEL))
_SMALL_N = sum(n for _, n in _SMALL) + LANES


def _pack_small(vals, tail):
    return jnp.concatenate([vals[n].reshape(1, -1) for n, _ in _SMALL] + [tail], axis=1)


def _unpack_small(vec, shapes):
    out, pos = {}, 0
    for n, width in _SMALL:
        out[n] = vec[:, pos:pos + width].reshape(shapes[n])
        pos += width
    return out, vec[:, pos:]


def kernel(x, norm_mix_g, w_in, b_gate, w_up_dil, w_up_sb, w_out, norm_mlp_g, w_mlp_in, w_mlp_out, norm_final_g, loss_target, m_norm_mix_g, m_w_in, m_b_gate, m_w_up_dil, m_w_up_sb, m_w_out, m_norm_mlp_g, m_w_mlp_in, m_w_mlp_out, m_norm_final_g, v_norm_mix_g, v_w_in, v_b_gate, v_w_up_dil, v_w_up_sb, v_w_out, v_norm_mlp_g, v_w_mlp_in, v_w_mlp_out, v_norm_final_g):
    given = dict(locals())
    s = x.shape[1]
    x0 = x.reshape(s, D_MODEL)
    target = loss_target.reshape(s, D_MODEL)
    g1 = norm_mix_g.reshape(1, D_MODEL)
    g2 = norm_mlp_g.reshape(1, D_MODEL)
    g3 = norm_final_g.reshape(1, D_MODEL)
    bg = b_gate.reshape(1, 2 * D_MODEL)
    w_shards = {n: given[n].reshape(_shard_shape(n)) for n in _SHARDED}
    m_shards = {n: given["m_" + n].reshape(_shard_shape(n)) for n in _SHARDED}
    v_shards = {n: given["v_" + n].reshape(_shard_shape(n)) for n in _SHARDED}

    gathered = _all_gather([w_shards[n].astype(BF16) for n in _SHARDED])
    full = {n: _assemble(n, g) for n, g in zip(_SHARDED, gathered)}
    w_in_f = full["w_in"]
    w_qkv, w_gl = w_in_f[:, :QKV_COLS], w_in_f[:, QKV_COLS:]

    def norm1(_, rows, consts):
        _, xh = _rms_stats(rows[0])
        return [xh * consts[0]], []

    (h1,) = _rowk("norm_mix", tm=512, rows=[x0], consts=[g1], row_outs=[(D_MODEL, BF16)], epilogue=norm1)
    qkv = _mm("proj_qkv", h1, w_qkv, out_dtype=BF16, tm=512, tn=768, tk=D_MODEL)
    gl = _mm("proj_gates", h1, w_gl, out_dtype=F32, tm=512, tn=1024, tk=D_MODEL)
    dil = [_dil_fwd(qkv, g) for g in range(len(DIL_GROUPS))]
    os_, lses = [d[0] for d in dil], [d[1] for d in dil]
    o_a = _dil_mix_fwd(os_, lses, 512)
    o_b, tot_b = _sb_fwd(qkv)
    x1, h2 = _mixer_fwd(o_a, o_b, gl, x0, bg, g2, full["w_up_dil"], full["w_up_sb"], full["w_out"], 256)
    f = _mm("mlp_in", h2, full["w_mlp_in"], out_dtype=BF16, tm=512, tn=1024, tk=D_MODEL,
            epilogue=lambda r, _: jnp.square(jnp.maximum(r, 0.0)))

    def head(acc, rows, consts):
        x1v, tv = rows
        g3v = consts[0]
        x2 = x1v + acc
        r, xh = _rms_stats(x2)
        diff = xh * g3v - tv
        loss = (0.5 / D_MODEL) * jnp.sum(jnp.sum(diff * diff, axis=0, keepdims=True), axis=1, keepdims=True)
        dy = diff * (1.0 / D_MODEL)
        dx2, dg = _rms_bwd(dy, xh, r, g3v)
        return [dx2], [dg, jnp.broadcast_to(loss, (1, LANES))]

    dx2, gg3, loss_part = _rowk("mlp_out_loss", a=f, w=full["w_mlp_out"], tm=512, tk=1024, rows=[x1, target],
                                consts=[g3], row_outs=[(D_MODEL, F32)], acc_outs=[D_MODEL, LANES], epilogue=head)

    da = _mm("mlp_out_bwd", dx2, full["w_mlp_out"], tb=True, out_dtype=BF16, tm=512, tn=1024, tk=D_MODEL, extra=f,
             epilogue=lambda r, fv: r * (2.0 * jnp.sqrt(fv.astype(F32))))
    g_w_mlp_out = _mm("grad_w_mlp_out", f, dx2, ta=True, out_dtype=F32, tm=1024, tn=1024, tk=512)
    g_w_mlp_in = _mm("grad_w_mlp_in", h2, da, ta=True, out_dtype=F32, tm=1024, tn=1024, tk=512)

    def norm_bwd(acc, rows, consts):
        xv, dres = rows
        r, xh = _rms_stats(xv)
        dx, dg = _rms_bwd(acc, xh, r, consts[0])
        return [dres + dx], [dg]

    dx1, gg2 = _rowk("mlp_in_bwd", a=da, w=full["w_mlp_in"], nt=True, tm=512, tk=1024, rows=[x1, dx2], consts=[g2],
                     row_outs=[(D_MODEL, F32)], acc_outs=[D_MODEL], epilogue=norm_bwd)
    do_a, do_b, dgl, g_w_out, g_w_ud, g_w_us, g_bg = _mixer_bwd(
        dx1, o_a, o_b, gl, bg, full["w_up_dil"], full["w_up_sb"], full["w_out"], 256)
    mix = _dil_mix_bwd(do_a, os_, lses, 512)
    dil_b = [_dil_bwd(qkv, mix[g], lses[g], mix[3 + g], g) for g in range(len(DIL_GROUPS))]
    dq_b, dk_b, dv_b = _sb_bwd(qkv, do_b, tot_b)
    dproj = jnp.concatenate(
        [d[0] for d in dil_b] + [d[1] for d in dil_b] + [d[2] for d in dil_b]
        + [dq_b, dk_b.astype(BF16), dv_b.astype(BF16), dgl], axis=1)
    g_w_in = _mm("grad_w_in", h1, dproj, ta=True, out_dtype=F32, tm=512, tn=IN_COLS // 2, tk=512)
    grad_x, gg1 = _rowk("in_proj_bwd", a=dproj, w=w_in_f, nt=True, tm=512, tk=IN_COLS // 2, rows=[x0, dx1],
                        consts=[g1], row_outs=[(D_MODEL, F32)], acc_outs=[D_MODEL], epilogue=norm_bwd)

    g_full = {"w_in": g_w_in, "w_up_dil": g_w_ud, "w_up_sb": g_w_us, "w_out": g_w_out,
              "w_mlp_in": g_w_mlp_in, "w_mlp_out": g_w_mlp_out}
    small_part = _pack_small({"norm_mix_g": gg1, "b_gate": g_bg, "norm_mlp_g": gg2, "norm_final_g": gg3}, loss_part)
    chunks = [_chunk(n, g_full[n]).astype(BF16) for n in _SHARDED]
    chunks.append(jnp.broadcast_to(small_part[None], (N_DEV, 1, _SMALL_N)))
    *parts, small_parts = _exchange(chunks)

    tags = ("grad_", "delta_", "new_m_", "new_v_")
    outs = {}
    for n, p in zip(_SHARDED, parts):
        res = _reduce_adamw("adamw_" + n, p, w_shards[n], m_shards[n], v_shards[n], 128)
        for tag, val in zip(tags, res):
            outs[tag + n] = val.reshape(given[n].shape)
    small_w = _pack_small(given, jnp.zeros((1, LANES), F32))
    small_m = _pack_small({n: given["m_" + n] for n, _ in _SMALL}, jnp.zeros((1, LANES), F32))
    small_v = _pack_small({n: given["v_" + n] for n, _ in _SMALL}, jnp.ones((1, LANES), F32))
    small_res = _reduce_adamw("adamw_replicated", small_parts, small_w, small_m, small_v, 8)

    small_shapes = {n: given[n].shape for n, _ in _SMALL}
    for tag, small in zip(tags, small_res):
        small_vals, tail = _unpack_small(small, small_shapes)
        for n, val in small_vals.items():
            outs[tag + n] = val
        if tag == "grad_":
            loss = tail[0, 0]
    names = ["norm_mix_g", "w_in", "b_gate", "w_up_dil", "w_up_sb", "w_out", "norm_mlp_g", "w_mlp_in", "w_mlp_out",
             "norm_final_g"]
    return (loss, grad_x.reshape(x.shape), *[outs["grad_" + n] for n in names], *[outs["delta_" + n] for n in names],
            *[outs["new_m_" + n] for n in names], *[outs["new_v_" + n] for n in names])
```

```python
import functools
import math

import jax
import jax.numpy as jnp
from jax import lax
from jax.experimental import pallas as pl
from jax.experimental.pallas import tpu as pltpu

_pcall = pl.pallas_call

F32 = jnp.float32
BF16 = jnp.bfloat16

D_MODEL = 1024
HEAD_DIM = 64
DIL_GROUPS = ((128, 1), (512, 4), (2048, 16))
DIL_HEADS_PER_GROUP = 4
N_DIL_HEADS = 12
N_SB_HEADS = 8
DIL_WIDTH = 768
DIL_OUT_WIDTH = 256
SB_WIDTH = 512
D_FF = 4096
BLOCK = 128
RMS_EPS = 1e-6
NEG_INF = -1e30
QKV_COLS = 3 * DIL_WIDTH + 3 * SB_WIDTH
IN_COLS = QKV_COLS + 2 * D_MODEL
N_DEV = 8

ADAM_LR = 0.001
ADAM_B1 = 0.9
ADAM_B2 = 0.999
ADAM_EPS = 1e-08
ADAM_WD = 0.01
ADAM_STEP = 10

VMEM_LIMIT = 56 * 1024 * 1024
SB_TK = 256
SB_TQ_FWD = 512
SB_TQ_BWD = 256
LANES = 128

_ARB = pltpu.ARBITRARY


def _cparams(n_axes, **kw):
    return pltpu.CompilerParams(dimension_semantics=(_ARB,) * n_axes, vmem_limit_bytes=VMEM_LIMIT, **kw)


def _dot(a, b):
    return jnp.dot(a, b, preferred_element_type=F32)


def _dot_nt(a, b):
    return lax.dot_general(a, b, (((1,), (1,)), ((), ())), preferred_element_type=F32)


def _dot_tn(a, b):
    return lax.dot_general(a, b, (((0,), (0,)), ((), ())), preferred_element_type=F32)


def _split_hi_lo(x):
    hi = x.astype(BF16)
    lo = (x - hi.astype(F32)).astype(BF16)
    return hi, lo


def _dot_hi_lo(x, m):
    hi, lo = _split_hi_lo(x)
    return _dot(hi, m) + _dot(lo, m)


def _sigmoid(x):
    return 1.0 / (1.0 + jnp.exp(-x))


def _mm(name, a, b, *, ta=False, tb=False, out_dtype, tm, tn, tk, epilogue=None, extra=None):
    m = a.shape[1] if ta else a.shape[0]
    k = a.shape[0] if ta else a.shape[1]
    n = b.shape[0] if tb else b.shape[1]
    assert (b.shape[1] if tb else b.shape[0]) == k
    tm, tn, tk = min(tm, m), min(tn, n), min(tk, k)
    assert m % tm == 0 and n % tn == 0 and k % tk == 0, (name, m, n, k, tm, tn, tk)
    nk = k // tk
    dn = (((0 if ta else 1,), (1 if tb else 0,)), ((), ()))
    in_place = nk > 1 and epilogue is None and out_dtype == F32

    def body(*refs):
        if extra is not None:
            a_ref, b_ref, e_ref, o_ref = refs[:4]
        else:
            a_ref, b_ref, o_ref = refs[:3]
            e_ref = None

        def finish(r):
            if epilogue is not None:
                r = epilogue(r, None if e_ref is None else e_ref[...])
            o_ref[...] = r.astype(out_dtype)

        part = lax.dot_general(a_ref[...].astype(BF16), b_ref[...].astype(BF16), dn, preferred_element_type=F32)
        if nk == 1:
            finish(part)
        else:
            acc_ref = o_ref if in_place else refs[-1]
            kk = pl.program_id(2)

            @pl.when(kk == 0)
            def _():
                acc_ref[...] = part

            @pl.when(kk > 0)
            def _():
                acc_ref[...] += part

            if not in_place:

                @pl.when(kk == nk - 1)
                def _():
                    finish(acc_ref[...])

    a_spec = pl.BlockSpec((tk, tm), lambda j, i, kk: (kk, i)) if ta else pl.BlockSpec((tm, tk), lambda j, i, kk: (i, kk))
    b_spec = pl.BlockSpec((tn, tk), lambda j, i, kk: (j, kk)) if tb else pl.BlockSpec((tk, tn), lambda j, i, kk: (kk, j))
    o_spec = pl.BlockSpec((tm, tn), lambda j, i, kk: (i, j))
    in_specs = [a_spec, b_spec]
    args = [a, b]
    if extra is not None:
        in_specs.append(o_spec)
        args.append(extra)
    return _pcall(
        body,
        name=name,
        grid=(n // tn, m // tm, nk),
        in_specs=in_specs,
        out_specs=o_spec,
        out_shape=jax.ShapeDtypeStruct((m, n), out_dtype),
        scratch_shapes=[pltpu.VMEM((tm, tn), F32)] if (nk > 1 and not in_place) else [],
        compiler_params=_cparams(3),
    )(*args)


def _rowk(name, *, a=None, w=None, nt=False, tm, tk=None, rows=(), consts=(), row_outs=(), acc_outs=(), epilogue):
    has_mm = a is not None
    m = a.shape[0] if has_mm else rows[0].shape[0]
    assert m % tm == 0
    nm = m // tm
    if has_mm:
        k = a.shape[1]
        n = w.shape[0] if nt else w.shape[1]
        tk = min(tk, k)
        assert k % tk == 0
        nk = k // tk
    else:
        nk = 1
    n_rows, n_consts, n_ro, n_ao = len(rows), len(consts), len(row_outs), len(acc_outs)

    def body(*refs):
        pos = 0
        if has_mm:
            a_ref, w_ref = refs[0], refs[1]
            pos = 2
        row_refs = refs[pos:pos + n_rows]
        pos += n_rows
        const_refs = refs[pos:pos + n_consts]
        pos += n_consts
        ro_refs = refs[pos:pos + n_ro]
        pos += n_ro
        ao_refs = refs[pos:pos + n_ao]
        pos += n_ao
        i = pl.program_id(0)
        kk = pl.program_id(1)

        def finish(acc):
            ro_vals, ao_vals = epilogue(acc, [r[...] for r in row_refs], [c[...] for c in const_refs])
            for r, v in zip(ro_refs, ro_vals):
                r[...] = v.astype(r.dtype)
            for r, v in zip(ao_refs, ao_vals):

                @pl.when(i == 0)
                def _(r=r, v=v):
                    r[...] = v

                @pl.when(i > 0)
                def _(r=r, v=v):
                    r[...] += v

        if not has_mm:
            finish(None)
            return
        if nt:
            part = _dot_nt(a_ref[...].astype(BF16), w_ref[...])
        else:
            part = _dot(a_ref[...].astype(BF16), w_ref[...])
        if nk == 1:
            finish(part)
        else:
            acc_ref = refs[pos]

            @pl.when(kk == 0)
            def _():
                acc_ref[...] = part

            @pl.when(kk > 0)
            def _():
                acc_ref[...] += part

            @pl.when(kk == nk - 1)
            def _():
                finish(acc_ref[...])

    once = pl.Buffered(1)
    in_specs, args = [], []
    if has_mm:
        in_specs.append(pl.BlockSpec((tm, tk), lambda i, kk: (i, kk)))
        w_mode = once if nk == 1 else None
        in_specs.append(pl.BlockSpec((n, tk), lambda i, kk: (0, kk), pipeline_mode=w_mode) if nt
                        else pl.BlockSpec((tk, n), lambda i, kk: (kk, 0), pipeline_mode=w_mode))
        args += [a, w]
    for r in rows:
        in_specs.append(pl.BlockSpec((tm, r.shape[1]), lambda i, kk: (i, 0)))
        args.append(r)
    for c in consts:
        in_specs.append(pl.BlockSpec(c.shape, lambda i, kk: (0,) * c.ndim, pipeline_mode=once))
        args.append(c)
    out_specs, out_shape = [], []
    for width, dt in row_outs:
        out_specs.append(pl.BlockSpec((tm, width), lambda i, kk: (i, 0)))
        out_shape.append(jax.ShapeDtypeStruct((m, width), dt))
    for width in acc_outs:
        out_specs.append(pl.BlockSpec((1, width), lambda i, kk: (0, 0)))
        out_shape.append(jax.ShapeDtypeStruct((1, width), F32))
    return _pcall(
        body,
        name=name,
        grid=(nm, nk),
        in_specs=in_specs,
        out_specs=out_specs,
        out_shape=out_shape,
        scratch_shapes=[pltpu.VMEM((tm, n), F32)] if (has_mm and nk > 1) else [],
        compiler_params=_cparams(2),
    )(*args)


def _rms_stats(x):
    r = lax.rsqrt(jnp.mean(x * x, axis=-1, keepdims=True) + RMS_EPS)
    return r, x * r


def _rms_bwd(dh, xh, r, g):
    gy = dh * g
    dx = r * (gy - xh * jnp.mean(gy * xh, axis=-1, keepdims=True))
    return dx, jnp.sum(dh * xh, axis=0, keepdims=True)


def _alibi_slope(head):
    return 2.0 ** (-8.0 * (head + 1) / N_DIL_HEADS)


def _dil_masks(i):
    qi = lax.broadcasted_iota(jnp.int32, (BLOCK, 2 * BLOCK), 0)
    kj = lax.broadcasted_iota(jnp.int32, (BLOCK, 2 * BLOCK), 1)
    steps = qi + BLOCK - kj
    valid = (steps >= 0) & (steps <= BLOCK) & ((kj >= BLOCK) | (i > 0))
    return steps.astype(F32), valid


def _dil_view(qkv, group):
    _, dilation = DIL_GROUPS[group]
    if dilation == 1:
        return qkv, QKV_COLS // DIL_OUT_WIDTH, (group, 3 + group, 6 + group)
    w = DIL_OUT_WIDTH
    own = jnp.concatenate([qkv[:, (3 * part + group) * w:(3 * part + group + 1) * w] for part in range(3)], axis=1)
    return own.reshape(qkv.shape[0] // dilation, dilation * 3 * w), 3, (0, 1, 2)


def _dil_specs(ncb, cols, clamp):
    def cur(col):
        return pl.BlockSpec((BLOCK, DIL_OUT_WIDTH), lambda r, i: (clamp(i), r * ncb + col))

    def prev(col):
        return pl.BlockSpec((BLOCK, DIL_OUT_WIDTH), lambda r, i: (jnp.maximum(clamp(i) - 1, 0), r * ncb + col))

    return [cur(cols[0]), cur(cols[1]), prev(cols[1]), cur(cols[2]), prev(cols[2])]


def _dil_fwd(qkv, group):
    window, dilation = DIL_GROUPS[group]
    s = qkv.shape[0]
    sub = s // dilation
    nb = sub // BLOCK
    assert nb * BLOCK * dilation == s and window // dilation == BLOCK
    slopes = [_alibi_slope(group * DIL_HEADS_PER_GROUP + h) * dilation for h in range(DIL_HEADS_PER_GROUP)]

    def body(q_ref, kc_ref, kp_ref, vc_ref, vp_ref, o_ref, lse_ref):
        i = pl.program_id(1)
        q = q_ref[...]
        kk = jnp.concatenate([kp_ref[...], kc_ref[...]], axis=0)
        vv = jnp.concatenate([vp_ref[...], vc_ref[...]], axis=0)
        head_id = lax.broadcasted_iota(jnp.int32, (1, DIL_OUT_WIDTH), 1) // HEAD_DIM
        steps, valid = _dil_masks(i)
        heads = range(DIL_HEADS_PER_GROUP)
        scores = [_dot_nt(jnp.where(head_id == h, q, jnp.zeros_like(q)), kk) for h in heads]
        ps, lses = [], []
        for h in heads:
            logits = scores[h] * (1.0 / math.sqrt(HEAD_DIM)) - slopes[h] * steps
            logits = jnp.where(valid, logits, NEG_INF)
            mx = jnp.max(logits, axis=1, keepdims=True)
            e = jnp.exp(logits - mx)
            den = jnp.sum(e, axis=1, keepdims=True)
            lses.append(mx + jnp.log(den))
            ps.append((e * (1.0 / den)).astype(BF16))
        outs = [_dot(ps[h], vv) for h in heads]
        o, lse_all = outs[0], lses[0]
        for h in heads[1:]:
            o = jnp.where(head_id == h, outs[h], o)
            lse_all = jnp.where(head_id == h, lses[h], lse_all)
        o_ref[...] = o
        lse_ref[...] = jnp.broadcast_to(lse_all, o.shape)

    qkv_v, ncb, cols = _dil_view(qkv, group)
    out_spec = pl.BlockSpec((BLOCK, DIL_OUT_WIDTH), lambda r, i: (i, r))
    o, lse = _pcall(
        body,
        name=f"dil_fwd_g{group}",
        grid=(dilation, nb),
        in_specs=_dil_specs(ncb, cols, lambda i: i),
        out_specs=[out_spec, out_spec],
        out_shape=[jax.ShapeDtypeStruct((sub, dilation * DIL_OUT_WIDTH), F32)] * 2,
        compiler_params=_cparams(2),
    )(qkv_v, qkv_v, qkv_v, qkv_v, qkv_v)
    return o.reshape(s, DIL_OUT_WIDTH), lse.reshape(s, DIL_OUT_WIDTH)


def _dil_bwd(qkv, do_g, lse_g, dterm_g, group):
    window, dilation = DIL_GROUPS[group]
    s = qkv.shape[0]
    sub = s // dilation
    nb = sub // BLOCK
    slopes = [_alibi_slope(group * DIL_HEADS_PER_GROUP + h) * dilation for h in range(DIL_HEADS_PER_GROUP)]
    scale = 1.0 / math.sqrt(HEAD_DIM)

    def body(q_ref, kc_ref, kp_ref, vc_ref, vp_ref, do_ref, lse_ref, dt_ref, dq_ref, dk_ref, dv_ref, ck_ref, cv_ref):
        i = pl.program_id(1)

        @pl.when(i == 0)
        def _():
            ck_ref[...] = jnp.zeros_like(ck_ref)
            cv_ref[...] = jnp.zeros_like(cv_ref)

        @pl.when(i < nb)
        def _():
            q = q_ref[...]
            do = do_ref[...]
            lse_all = lse_ref[...]
            dt_all = dt_ref[...]
            kk = jnp.concatenate([kp_ref[...], kc_ref[...]], axis=0)
            vv = jnp.concatenate([vp_ref[...], vc_ref[...]], axis=0)
            lane = lax.broadcasted_iota(jnp.int32, (1, DIL_OUT_WIDTH), 1)
            head_id = lane // HEAD_DIM
            steps, valid = _dil_masks(i)
            heads = range(DIL_HEADS_PER_GROUP)
            qms = [jnp.where(head_id == h, q, jnp.zeros_like(q)) for h in heads]
            doms = [jnp.where(head_id == h, do, jnp.zeros_like(do)) for h in heads]
            scores = [_dot_nt(qms[h], kk) for h in heads]
            dps = [_dot_nt(doms[h], vv) for h in heads]
            pbs, dss = [], []
            for h in heads:
                first = lane == h * HEAD_DIM
                lse = jnp.sum(jnp.where(first, lse_all, 0.0), axis=1, keepdims=True)
                dt = jnp.sum(jnp.where(first, dt_all, 0.0), axis=1, keepdims=True)
                logits = scores[h] * scale - slopes[h] * steps
                p = jnp.where(valid, jnp.exp(jnp.where(valid, logits, NEG_INF) - lse), 0.0)
                pbs.append(p.astype(BF16))
                dss.append((p * (dps[h] + dt) * scale).astype(BF16))
            dqs = [_dot(dss[h], kk) for h in heads]
            dks = [_dot_tn(dss[h], qms[h]) for h in heads]
            dvs = [_dot_tn(pbs[h], doms[h]) for h in heads]
            dq = dqs[0]
            for h in heads[1:]:
                dq = jnp.where(head_id == h, dqs[h], dq)
            dkk = (dks[0] + dks[1]) + (dks[2] + dks[3])
            dvv = (dvs[0] + dvs[1]) + (dvs[2] + dvs[3])
            dq_ref[...] = dq.astype(dq_ref.dtype)
            dk_ref[...] = (ck_ref[...] + dkk[:BLOCK]).astype(dk_ref.dtype)
            dv_ref[...] = (cv_ref[...] + dvv[:BLOCK]).astype(dv_ref.dtype)
            ck_ref[...] = dkk[BLOCK:]
            cv_ref[...] = dvv[BLOCK:]

        @pl.when(i == nb)
        def _():
            dk_ref[...] = ck_ref[...].astype(dk_ref.dtype)
            dv_ref[...] = cv_ref[...].astype(dv_ref.dtype)

    clamp = lambda i: jnp.minimum(i, nb - 1)
    qkv_v, ncb, cols = _dil_view(qkv, group)
    view = lambda t: t.reshape(sub, dilation * DIL_OUT_WIDTH)
    row_spec = pl.BlockSpec((BLOCK, DIL_OUT_WIDTH), lambda r, i: (clamp(i), r))
    late_spec = pl.BlockSpec((BLOCK, DIL_OUT_WIDTH), lambda r, i: (jnp.maximum(i - 1, 0), r))
    dq, dk, dv = _pcall(
        body,
        name=f"dil_bwd_g{group}",
        grid=(dilation, nb + 1),
        in_specs=_dil_specs(ncb, cols, clamp) + [row_spec, row_spec, row_spec],
        out_specs=[row_spec, late_spec, late_spec],
        out_shape=[jax.ShapeDtypeStruct((sub, dilation * DIL_OUT_WIDTH), BF16)] * 3,
        scratch_shapes=[pltpu.VMEM((BLOCK, DIL_OUT_WIDTH), F32)] * 2,
        compiler_params=_cparams(2),
    )(qkv_v, qkv_v, qkv_v, qkv_v, qkv_v, view(do_g), view(lse_g), view(dterm_g))
    return dq.reshape(s, DIL_OUT_WIDTH), dk.reshape(s, DIL_OUT_WIDTH), dv.reshape(s, DIL_OUT_WIDTH)


def _head_block_ones():
    r = lax.broadcasted_iota(jnp.int32, (DIL_OUT_WIDTH, DIL_OUT_WIDTH), 0) // HEAD_DIM
    c = lax.broadcasted_iota(jnp.int32, (DIL_OUT_WIDTH, DIL_OUT_WIDTH), 1) // HEAD_DIM
    return jnp.where(r == c, 1.0, 0.0).astype(BF16)


def _dil_mix_weights(l0, l1, l2):
    mx = jnp.maximum(jnp.maximum(l0, l1), l2)
    e0, e1, e2 = jnp.exp(l0 - mx), jnp.exp(l1 - mx), jnp.exp(l2 - mx)
    inv = 1.0 / (e0 + e1 + e2)
    return e0 * inv, e1 * inv, e2 * inv


def _dil_mix_fwd(os_, lses, tm):
    def epi(_, rows, consts):
        o0, o1, o2, l0, l1, l2 = rows
        w0, w1, w2 = _dil_mix_weights(l0, l1, l2)
        return [w0 * o0 + w1 * o1 + w2 * o2], []

    (o_a,) = _rowk("dil_mix_fwd", tm=tm, rows=list(os_) + list(lses), row_outs=[(DIL_OUT_WIDTH, BF16)], epilogue=epi)
    return o_a


def _dil_mix_bwd(do_a, os_, lses, tm):
    def epi(_, rows, consts):
        do, o0, o1, o2, l0, l1, l2 = rows
        do = do.astype(F32)
        w0, w1, w2 = _dil_mix_weights(l0, l1, l2)
        mixed = w0 * o0 + w1 * o1 + w2 * o2
        tot = _dot_hi_lo(do * mixed, _head_block_ones())
        return [w0 * do, w1 * do, w2 * do, -w0 * tot, -w1 * tot, -w2 * tot], []

    return _rowk(
        "dil_mix_bwd", tm=tm, rows=[do_a] + list(os_) + list(lses),
        row_outs=[(DIL_OUT_WIDTH, BF16)] * 3 + [(DIL_OUT_WIDTH, F32)] * 3, epilogue=epi)


_SB_Q0 = 3 * DIL_WIDTH // LANES
_SB_K0 = _SB_Q0 + SB_WIDTH // LANES
_SB_V0 = _SB_K0 + SB_WIDTH // LANES


_LOG2E = 1.4426950408889634
_EXP2_CLAMP = 126.0


def _tri(t, op):
    r = lax.broadcasted_iota(jnp.int32, (t, t), 0)
    c = lax.broadcasted_iota(jnp.int32, (t, t), 1)
    return jnp.where(op(r, c), 1.0, 0.0).astype(BF16)


def _softplus2(z2):
    return jnp.maximum(z2, jnp.log2(1.0 + jnp.exp2(jnp.minimum(z2, _EXP2_CLAMP))))


def _sb_fwd(qkv):
    s = qkv.shape[0]
    t, tq = SB_TK, min(SB_TQ_FWD, s)
    assert tq in (t, 2 * t) and s % (2 * t) == 0
    nq = s // tq
    n_pairs = SB_WIDTH // LANES

    def body(q_ref, k_ref, v_ref, o_ref, tot_ref):
        i = pl.program_id(1)
        q = q_ref[...] * (1.0 / math.sqrt(HEAD_DIM))
        lane_hi = lax.broadcasted_iota(jnp.int32, (1, LANES), 1) // HEAD_DIM
        later = _tri(t, lambda r, c: r > c)
        row = lax.broadcasted_iota(jnp.int32, (tq, t), 0)
        col = lax.broadcasted_iota(jnp.int32, (tq, t), 1)
        qms = [jnp.where(lane_hi == hh, q, jnp.zeros_like(q)) for hh in range(2)]

        def step(jj, carry, diag):
            tiles = (2 * jj + 1, 2 * jj)
            offs = [pl.multiple_of(j * t, t) for j in tiles]
            ks = [k_ref[pl.ds(off, t), :] for off in offs]
            vs = [v_ref[pl.ds(off, t), :] for off in offs]
            masks = [(j * t + col) < (i * tq + row) for j in tiles] if diag else None
            chains = [(n, hh) for n in range(2) for hh in range(2)]
            z2s = [_dot_nt(qms[hh], ks[n]) * _LOG2E for n, hh in chains]
            sps, lposs = [], []
            for (n, hh), z2 in zip(chains, z2s):
                sp = _softplus2(z2)
                lposs.append(z2 - sp)
                sps.append(jnp.where(masks[n], sp, 0.0) if diag else sp)
            sufs = [_dot(sp.astype(BF16), later) for sp in sps]
            cs = [carry[0], carry[2]]
            accs = [carry[1], carry[3]]
            for idx, (n, hh) in enumerate(chains):
                a = jnp.exp2(lposs[idx] - sufs[idx] - cs[hh])
                if diag:
                    a = jnp.where(masks[n], a, 0.0)
                accs[hh] = accs[hh] + _dot(a.astype(BF16), vs[n])
                cs[hh] = cs[hh] + jnp.sum(sps[idx], axis=1, keepdims=True)
            return cs[0], accs[0], cs[1], accs[1]

        zc, za = jnp.zeros((tq, 1), F32), jnp.zeros((tq, LANES), F32)
        last = (i * tq) // (2 * t)
        carry = step(last, (zc, za, zc, za), True)
        carry = lax.fori_loop(0, last, lambda n, ca: step(last - 1 - n, ca, False), carry)
        out = jnp.where(lane_hi == 0, carry[1], carry[3])
        tot = jnp.where(lane_hi == 0, carry[0], carry[2])
        o_ref[...] = out.astype(o_ref.dtype)
        tot_ref[...] = tot

    o, tot = _pcall(
        body,
        name="sb_fwd",
        grid=(n_pairs, nq),
        in_specs=[
            pl.BlockSpec((tq, LANES), lambda p, i: (i, _SB_Q0 + p)),
            pl.BlockSpec((s, LANES), lambda p, i: (0, _SB_K0 + p)),
            pl.BlockSpec((s, LANES), lambda p, i: (0, _SB_V0 + p)),
        ],
        out_specs=[pl.BlockSpec((tq, LANES), lambda p, i: (i, p))] * 2,
        out_shape=[jax.ShapeDtypeStruct((s, SB_WIDTH), BF16), jax.ShapeDtypeStruct((s, SB_WIDTH), F32)],
        compiler_params=_cparams(2),
    )(qkv, qkv, qkv)
    return o, tot


def _sb_bwd(qkv, do_b, tot_b):
    s = qkv.shape[0]
    t, tq = SB_TK, min(SB_TQ_BWD, s)
    assert tq in (t, 2 * t) and s % (2 * t) == 0
    nq = s // tq
    n_pairs = SB_WIDTH // LANES
    scale = 1.0 / math.sqrt(HEAD_DIM)

    def body(q_ref, k_ref, v_ref, do_ref, tot_ref, dq_ref, dk_ref, dv_ref):
        i = pl.program_id(1)

        @pl.when(i == 0)
        def _():
            dk_ref[...] = jnp.zeros_like(dk_ref)
            dv_ref[...] = jnp.zeros_like(dv_ref)

        q = q_ref[...] * scale
        do = do_ref[...]
        tot_all = tot_ref[...]
        lane = lax.broadcasted_iota(jnp.int32, (1, LANES), 1)
        lane_hi = lane // HEAD_DIM
        later = _tri(t, lambda r, c: r > c)
        before = _tri(t, lambda r, c: r < c)
        row = lax.broadcasted_iota(jnp.int32, (tq, t), 0)
        col = lax.broadcasted_iota(jnp.int32, (tq, t), 1)
        qms = [jnp.where(lane_hi == hh, q, jnp.zeros_like(q)) for hh in range(2)]
        doms = [jnp.where(lane_hi == hh, do, jnp.zeros_like(do)) for hh in range(2)]
        tots = [jnp.sum(jnp.where(lane == hh * HEAD_DIM, tot_all, 0.0), axis=1, keepdims=True) for hh in range(2)]

        def step(jj, carry, diag):
            tiles = (2 * jj, 2 * jj + 1)
            offs = [pl.multiple_of(j * t, t) for j in tiles]
            ks = [k_ref[pl.ds(off, t), :] for off in offs]
            vs = [v_ref[pl.ds(off, t), :] for off in offs]
            masks = [(j * t + col) < (i * tq + row) for j in tiles] if diag else None
            chains = [(n, hh) for n in range(2) for hh in range(2)]
            z2s = [_dot_nt(qms[hh], ks[n]) * _LOG2E for n, hh in chains]
            sps, sigs = [], []
            for (n, hh), z2 in zip(chains, z2s):
                sp = _softplus2(z2)
                sigs.append(jnp.exp2(z2 - sp))
                sps.append(jnp.where(masks[n], sp, 0.0) if diag else sp)
            sufs = [_dot(sp.astype(BF16), later) for sp in sps]
            das = [_dot_nt(doms[hh], vs[n]) for n, hh in chains]
            cls = [carry[0], carry[3]]
            cgs = [carry[1], carry[4]]
            accs = [carry[2], carry[5]]
            gs, abs_, cg_at = [], [], []
            for idx, (n, hh) in enumerate(chains):
                cls[hh] = cls[hh] + jnp.sum(sps[idx], axis=1, keepdims=True)
                a = sigs[idx] * jnp.exp2(-sufs[idx] - (tots[hh] - cls[hh]))
                if diag:
                    a = jnp.where(masks[n], a, 0.0)
                g = a * das[idx]
                gs.append(g)
                abs_.append(a.astype(BF16))
                cg_at.append(cgs[hh])
                cgs[hh] = cgs[hh] + jnp.sum(g, axis=1, keepdims=True)
            prefs = [_dot(g.astype(BF16), before) for g in gs]
            dvs = [_dot_tn(abs_[idx], doms[hh]) for idx, (n, hh) in enumerate(chains)]
            dzs = []
            for idx, (n, hh) in enumerate(chains):
                g = gs[idx]
                dz = g - sigs[idx] * (g + prefs[idx] + cg_at[idx])
                if diag:
                    dz = jnp.where(masks[n], dz, 0.0)
                dzs.append(dz.astype(BF16))
            for idx, (n, hh) in enumerate(chains):
                accs[hh] = accs[hh] + _dot(dzs[idx], ks[n])
            dks = [_dot_tn(dzs[idx], qms[hh]) for idx, (n, hh) in enumerate(chains)]
            for n in range(2):
                dk_ref[pl.ds(offs[n], t), :] += dks[2 * n] + dks[2 * n + 1]
                dv_ref[pl.ds(offs[n], t), :] += dvs[2 * n] + dvs[2 * n + 1]
            return cls[0], cgs[0], accs[0], cls[1], cgs[1], accs[1]

        zc, za = jnp.zeros((tq, 1), F32), jnp.zeros((tq, LANES), F32)
        last = (i * tq) // (2 * t)
        carry = lax.fori_loop(0, last, lambda jj, ca: step(jj, ca, False), (zc, zc, za, zc, zc, za))
        carry = step(last, carry, True)
        dq = jnp.where(lane_hi == 0, carry[2], carry[5])
        dq_ref[...] = (dq * scale).astype(dq_ref.dtype)

    row_spec = pl.BlockSpec((tq, LANES), lambda p, i: (i, p))
    full_spec = pl.BlockSpec((s, LANES), lambda p, i: (0, p))
    return _pcall(
        body,
        name="sb_bwd",
        grid=(n_pairs, nq),
        in_specs=[
            pl.BlockSpec((tq, LANES), lambda p, i: (i, _SB_Q0 + p)),
            pl.BlockSpec((s, LANES), lambda p, i: (0, _SB_K0 + p)),
            pl.BlockSpec((s, LANES), lambda p, i: (0, _SB_V0 + p)),
            row_spec, row_spec,
        ],
        out_specs=[row_spec, full_spec, full_spec],
        out_shape=[jax.ShapeDtypeStruct((s, SB_WIDTH), BF16), jax.ShapeDtypeStruct((s, SB_WIDTH), F32),
                   jax.ShapeDtypeStruct((s, SB_WIDTH), F32)],
        compiler_params=_cparams(2),
    )(qkv, qkv, qkv, do_b, tot_b)


def _gates(gl, bg):
    return _sigmoid(gl[:, :D_MODEL] + bg[:, :D_MODEL]), _sigmoid(gl[:, D_MODEL:] + bg[:, D_MODEL:])


def _mixer_fwd(o_a, o_b, gl, x0, bg, g2, w_ud, w_us, w_out, tm):
    def epi(_, rows, consts):
        oa, ob, glv, x = rows
        bgv, g2v, wud, wus, wout = consts
        ga, gb = _gates(glv, bgv)
        merged = ga * _dot(oa, wud) + gb * _dot(ob, wus)
        x1 = x + _dot(merged.astype(BF16), wout)
        r, xh = _rms_stats(x1)
        return [x1, xh * g2v], []

    return _rowk("mixer_fwd", tm=tm, rows=[o_a, o_b, gl, x0], consts=[bg, g2, w_ud, w_us, w_out],
                 row_outs=[(D_MODEL, F32), (D_MODEL, BF16)], epilogue=epi)


def _mixer_bwd(dx1, o_a, o_b, gl, bg, w_ud, w_us, w_out, tm):
    s = dx1.shape[0]
    nm = s // tm

    def body(dx_ref, oa_ref, ob_ref, gl_ref, bg_ref, wud_ref, wus_ref, wout_ref,
             doa_ref, dob_ref, dgl_ref, gwout_ref, gwud_ref, gwus_ref, gbg_ref):
        i = pl.program_id(0)
        dxb = dx_ref[...].astype(BF16)
        oa, ob = oa_ref[...], ob_ref[...]
        ga, gb = _gates(gl_ref[...], bg_ref[...])
        ua, ub = _dot(oa, wud_ref[...]), _dot(ob, wus_ref[...])
        merged = (ga * ua + gb * ub).astype(BF16)
        dm = _dot_nt(dxb, wout_ref[...])
        dua = (dm * ga).astype(BF16)
        dub = (dm * gb).astype(BF16)
        dgla = dm * ua * ga * (1.0 - ga)
        dglb = dm * ub * gb * (1.0 - gb)
        doa_ref[...] = _dot_nt(dua, wud_ref[...]).astype(doa_ref.dtype)
        dob_ref[...] = _dot_nt(dub, wus_ref[...]).astype(dob_ref.dtype)
        dgl_ref[:, :D_MODEL] = dgla.astype(dgl_ref.dtype)
        dgl_ref[:, D_MODEL:] = dglb.astype(dgl_ref.dtype)
        parts = [(gwout_ref, _dot_tn(merged, dxb)), (gwud_ref, _dot_tn(oa, dua)), (gwus_ref, _dot_tn(ob, dub))]
        for r, v in parts:

            @pl.when(i == 0)
            def _(r=r, v=v):
                r[...] = v

            @pl.when(i > 0)
            def _(r=r, v=v):
                r[...] += v

        sa = jnp.sum(dgla, axis=0, keepdims=True)
        sb = jnp.sum(dglb, axis=0, keepdims=True)

        @pl.when(i == 0)
        def _():
            gbg_ref[:, :D_MODEL] = sa
            gbg_ref[:, D_MODEL:] = sb

        @pl.when(i > 0)
        def _():
            gbg_ref[:, :D_MODEL] += sa
            gbg_ref[:, D_MODEL:] += sb

    row = lambda w: pl.BlockSpec((tm, w), lambda i: (i, 0))
    full = lambda a: pl.BlockSpec(a.shape, lambda i: (0, 0))
    fshape = lambda r, c: jax.ShapeDtypeStruct((r, c), F32)
    return _pcall(
        body,
        name="mixer_bwd",
        grid=(nm,),
        in_specs=[row(D_MODEL), row(DIL_OUT_WIDTH), row(SB_WIDTH), row(2 * D_MODEL),
                  full(bg), full(w_ud), full(w_us), full(w_out)],
        out_specs=[row(DIL_OUT_WIDTH), row(SB_WIDTH), row(2 * D_MODEL),
                   pl.BlockSpec((D_MODEL, D_MODEL), lambda i: (0, 0)),
                   pl.BlockSpec((DIL_OUT_WIDTH, D_MODEL), lambda i: (0, 0)),
                   pl.BlockSpec((SB_WIDTH, D_MODEL), lambda i: (0, 0)),
                   pl.BlockSpec((1, 2 * D_MODEL), lambda i: (0, 0))],
        out_shape=[jax.ShapeDtypeStruct((s, DIL_OUT_WIDTH), BF16), jax.ShapeDtypeStruct((s, SB_WIDTH), BF16),
                   jax.ShapeDtypeStruct((s, 2 * D_MODEL), BF16),
                   fshape(D_MODEL, D_MODEL), fshape(DIL_OUT_WIDTH, D_MODEL), fshape(SB_WIDTH, D_MODEL),
                   fshape(1, 2 * D_MODEL)],
        compiler_params=_cparams(1),
    )(dx1, o_a, o_b, gl, bg, w_ud, w_us, w_out)


_HBM = pl.BlockSpec(memory_space=pltpu.HBM)
_MESH = pl.DeviceIdType.MESH


def _all_gather(shards):
    n = len(shards)

    def body(*refs):
        x_refs, out_refs = refs[:n], refs[n:2 * n]
        send_sems, recv_sems, local_sems = refs[2 * n:]
        x, y, c = lax.axis_index("x"), lax.axis_index("y"), lax.axis_index("c")
        me, sibling = (x, y, c), (x, y, 1 - c)
        chips = [(1 - x, y), (x, 1 - y), (1 - x, 1 - y)]

        def slot(a, px, py, pc):
            return out_refs[a].at[4 * px + 2 * py + pc]

        def copy(a, k, block, to, own=False):
            return pltpu.make_async_remote_copy(
                src_ref=x_refs[a] if own else slot(a, *block), dst_ref=slot(a, *block),
                send_sem=send_sems.at[7 * a + k], recv_sem=recv_sems.at[7 * a + k], device_id=to, device_id_type=_MESH)

        mine = [pltpu.make_async_copy(x_refs[a], slot(a, *me), local_sems.at[a]) for a in range(n)]
        for cp in mine:
            cp.start()
        first = []
        for a in range(n):
            first.append(copy(a, 0, me, sibling, own=True))
            first += [copy(a, 1 + j, me, (*chip, c), own=True) for j, chip in enumerate(chips)]
        for cp in first:
            cp.start()
        passed = []
        for a in range(n):
            for j, chip in enumerate(chips):
                copy(a, 1 + j, (*chip, c), me).wait_recv()
                passed.append(copy(a, 4 + j, (*chip, c), sibling))
                passed[-1].start()
        for a in range(n):
            copy(a, 0, sibling, me).wait_recv()
            for j, chip in enumerate(chips):
                copy(a, 4 + j, (*chip, 1 - c), me).wait_recv()
        for cp in first + passed:
            cp.wait_send()
        for cp in mine:
            cp.wait()

    return _pcall(
        body,
        name="all_gather_weights",
        in_specs=[_HBM] * n,
        out_specs=[_HBM] * n,
        out_shape=[jax.ShapeDtypeStruct((N_DEV,) + s.shape, s.dtype) for s in shards],
        scratch_shapes=[pltpu.SemaphoreType.DMA((7 * n,)), pltpu.SemaphoreType.DMA((7 * n,)),
                        pltpu.SemaphoreType.DMA((n,))],
    )(*shards)


def _exchange(chunks):
    n = len(chunks)

    def body(*refs):
        g_refs, o_refs = refs[:n], refs[n:2 * n]
        send_sems, recv_sems, local_sems = refs[2 * n:]
        x, y, c = lax.axis_index("x"), lax.axis_index("y"), lax.axis_index("c")
        me = 4 * x + 2 * y + c
        own = [pltpu.make_async_copy(g_refs[a].at[me], o_refs[a].at[me], local_sems.at[a]) for a in range(n)]
        for cp in own:
            cp.start()
        copies = []
        for a in range(n):
            for k in range(1, N_DEV):
                px, py, pc = x ^ (k >> 2), y ^ ((k >> 1) & 1), c ^ (k & 1)
                peer = 4 * px + 2 * py + pc
                copies.append(pltpu.make_async_remote_copy(
                    src_ref=g_refs[a].at[peer], dst_ref=o_refs[a].at[me], send_sem=send_sems.at[7 * a + k - 1],
                    recv_sem=recv_sems.at[7 * a + k - 1], device_id=(px, py, pc), device_id_type=_MESH))
        for cp in copies:
            cp.start()
        for cp in copies:
            cp.wait()
        for cp in own:
            cp.wait()

    return _pcall(
        body,
        name="exchange_grads",
        in_specs=[_HBM] * n,
        out_specs=[_HBM] * n,
        out_shape=[jax.ShapeDtypeStruct(g.shape, g.dtype) for g in chunks],
        scratch_shapes=[pltpu.SemaphoreType.DMA((7 * n,)), pltpu.SemaphoreType.DMA((7 * n,)),
                        pltpu.SemaphoreType.DMA((n,))],
    )(*chunks)


def _reduce_adamw(name, parts, w, m, v, tr):
    _, rows, cols = parts.shape
    tr = min(tr, rows)
    assert rows % tr == 0
    c1 = 1.0 / (1.0 - ADAM_B1 ** ADAM_STEP)
    c2 = 1.0 / (1.0 - ADAM_B2 ** ADAM_STEP)

    def body(p_ref, w_ref, m_ref, v_ref, g_out, d_out, m_out, v_out):
        g = p_ref[0].astype(F32)
        for d in range(1, N_DEV):
            g = g + p_ref[d].astype(F32)
        mn = ADAM_B1 * m_ref[...] + (1.0 - ADAM_B1) * g
        vn = ADAM_B2 * v_ref[...] + (1.0 - ADAM_B2) * (g * g)
        g_out[...] = g
        m_out[...] = mn
        v_out[...] = vn
        d_out[...] = -ADAM_LR * ((mn * c1) / (jnp.sqrt(vn * c2) + ADAM_EPS) + ADAM_WD * w_ref[...])

    spec = pl.BlockSpec((tr, cols), lambda i: (i, 0))
    return _pcall(
        body,
        name=name,
        grid=(rows // tr,),
        in_specs=[pl.BlockSpec((N_DEV, tr, cols), lambda i: (0, i, 0)), spec, spec, spec],
        out_specs=[spec] * 4,
        out_shape=[jax.ShapeDtypeStruct((rows, cols), F32)] * 4,
        compiler_params=_cparams(1),
    )(parts, w, m, v)


_SHARDED = ("w_in", "w_up_dil", "w_up_sb", "w_out", "w_mlp_in", "w_mlp_out")
_FULL_SHAPES = {"w_in": (D_MODEL, IN_COLS), "w_up_dil": (DIL_OUT_WIDTH, D_MODEL), "w_up_sb": (SB_WIDTH, D_MODEL),
                "w_out": (D_MODEL, D_MODEL), "w_mlp_in": (D_MODEL, D_FF), "w_mlp_out": (D_FF, D_MODEL)}
_ROW_SHARDED = ("w_out", "w_mlp_out")


def _shard_shape(name):
    r, c = _FULL_SHAPES[name]
    return (r // N_DEV, c) if name in _ROW_SHARDED else (r, c // N_DEV)


def _assemble(name, gathered):
    r, c = _shard_shape(name)
    if name in _ROW_SHARDED:
        return gathered.reshape(N_DEV * r, c)
    return gathered.transpose(1, 0, 2).reshape(r, N_DEV * c)


def _chunk(name, full):
    r, c = _shard_shape(name)
    if name in _ROW_SHARDED:
        return full.reshape(N_DEV, r, c)
    return full.reshape(r, N_DEV, c).transpose(1, 0, 2)


_SMALL = (("norm_mix_g", D_MODEL), ("b_gate", 2 * D_MODEL), ("norm_mlp_g", D_MODEL), ("norm_final_g", D_MODEL))
_SMALL_N = sum(n for _, n in _SMALL) + LANES


def _pack_small(vals, tail):
    return jnp.concatenate([vals[n].reshape(1, -1) for n, _ in _SMALL] + [tail], axis=1)


def _unpack_small(vec, shapes):
    out, pos = {}, 0
    for n, width in _SMALL:
        out[n] = vec[:, pos:pos + width].reshape(shapes[n])
        pos += width
    return out, vec[:, pos:]


def kernel(x, norm_mix_g, w_in, b_gate, w_up_dil, w_up_sb, w_out, norm_mlp_g, w_mlp_in, w_mlp_out, norm_final_g, loss_target, m_norm_mix_g, m_w_in, m_b_gate, m_w_up_dil, m_w_up_sb, m_w_out, m_norm_mlp_g, m_w_mlp_in, m_w_mlp_out, m_norm_final_g, v_norm_mix_g, v_w_in, v_b_gate, v_w_up_dil, v_w_up_sb, v_w_out, v_norm_mlp_g, v_w_mlp_in, v_w_mlp_out, v_norm_final_g):
    given = dict(locals())
    s = x.shape[1]
    x0 = x.reshape(s, D_MODEL)
    target = loss_target.reshape(s, D_MODEL)
    g1 = norm_mix_g.reshape(1, D_MODEL)
    g2 = norm_mlp_g.reshape(1, D_MODEL)
    g3 = norm_final_g.reshape(1, D_MODEL)
    bg = b_gate.reshape(1, 2 * D_MODEL)
    w_shards = {n: given[n].reshape(_shard_shape(n)) for n in _SHARDED}
    m_shards = {n: given["m_" + n].reshape(_shard_shape(n)) for n in _SHARDED}
    v_shards = {n: given["v_" + n].reshape(_shard_shape(n)) for n in _SHARDED}

    gathered = _all_gather([w_shards[n].astype(BF16) for n in _SHARDED])
    full = {n: _assemble(n, g) for n, g in zip(_SHARDED, gathered)}
    w_in_f = full["w_in"]
    w_qkv, w_gl = w_in_f[:, :QKV_COLS], w_in_f[:, QKV_COLS:]

    def norm1(_, rows, consts):
        _, xh = _rms_stats(rows[0])
        return [xh * consts[0]], []

    (h1,) = _rowk("norm_mix", tm=512, rows=[x0], consts=[g1], row_outs=[(D_MODEL, BF16)], epilogue=norm1)
    qkv = _mm("proj_qkv", h1, w_qkv, out_dtype=BF16, tm=1024, tn=768, tk=D_MODEL)
    gl = _mm("proj_gates", h1, w_gl, out_dtype=F32, tm=512, tn=2048, tk=D_MODEL)
    dil = [_dil_fwd(qkv, g) for g in range(len(DIL_GROUPS))]
    os_, lses = [d[0] for d in dil], [d[1] for d in dil]
    o_a = _dil_mix_fwd(os_, lses, 512)
    o_b, tot_b = _sb_fwd(qkv)
    x1, h2 = _mixer_fwd(o_a, o_b, gl, x0, bg, g2, full["w_up_dil"], full["w_up_sb"], full["w_out"], 256)
    f = _mm("mlp_in", h2, full["w_mlp_in"], out_dtype=BF16, tm=1024, tn=1024, tk=D_MODEL,
            epilogue=lambda r, _: jnp.square(jnp.maximum(r, 0.0)))

    def head(acc, rows, consts):
        x1v, tv = rows
        g3v = consts[0]
        x2 = x1v + acc
        r, xh = _rms_stats(x2)
        diff = xh * g3v - tv
        loss = (0.5 / D_MODEL) * jnp.sum(jnp.sum(diff * diff, axis=0, keepdims=True), axis=1, keepdims=True)
        dy = diff * (1.0 / D_MODEL)
        dx2, dg = _rms_bwd(dy, xh, r, g3v)
        return [dx2, dx2], [dg, jnp.broadcast_to(loss, (1, LANES))]

    dx2, dx2b, gg3, loss_part = _rowk(
        "mlp_out_loss", a=f, w=full["w_mlp_out"], tm=512, tk=D_FF, rows=[x1, target], consts=[g3],
        row_outs=[(D_MODEL, F32), (D_MODEL, BF16)], acc_outs=[D_MODEL, LANES], epilogue=head)

    da = _mm("mlp_out_bwd", dx2b, full["w_mlp_out"], tb=True, out_dtype=BF16, tm=1024, tn=1024, tk=D_MODEL, extra=f,
             epilogue=lambda r, fv: r * (2.0 * jnp.sqrt(fv.astype(F32))))
    g_w_mlp_out = _mm("grad_w_mlp_out", f, dx2b, ta=True, out_dtype=F32, tm=1024, tn=1024, tk=2048)
    g_w_mlp_in = _mm("grad_w_mlp_in", h2, da, ta=True, out_dtype=F32, tm=1024, tn=1024, tk=2048)

    def norm_bwd(acc, rows, consts):
        xv, dres = rows
        r, xh = _rms_stats(xv)
        dx, dg = _rms_bwd(acc, xh, r, consts[0])
        return [dres + dx], [dg]

    dx1, gg2 = _rowk("mlp_in_bwd", a=da, w=full["w_mlp_in"], nt=True, tm=512, tk=D_FF, rows=[x1, dx2], consts=[g2],
                     row_outs=[(D_MODEL, F32)], acc_outs=[D_MODEL], epilogue=norm_bwd)
    do_a, do_b, dgl, g_w_out, g_w_ud, g_w_us, g_bg = _mixer_bwd(
        dx1, o_a, o_b, gl, bg, full["w_up_dil"], full["w_up_sb"], full["w_out"], 256)
    mix = _dil_mix_bwd(do_a, os_, lses, 512)
    dil_b = [_dil_bwd(qkv, mix[g], lses[g], mix[3 + g], g) for g in range(len(DIL_GROUPS))]
    dq_b, dk_b, dv_b = _sb_bwd(qkv, do_b, tot_b)
    dproj = jnp.concatenate(
        [d[0] for d in dil_b] + [d[1] for d in dil_b] + [d[2] for d in dil_b]
        + [dq_b, dk_b.astype(BF16), dv_b.astype(BF16), dgl], axis=1)
    g_w_in = _mm("grad_w_in", h1, dproj, ta=True, out_dtype=F32, tm=512, tn=IN_COLS // 2, tk=1024)
    grad_x, gg1 = _rowk("in_proj_bwd", a=dproj, w=w_in_f, nt=True, tm=512, tk=IN_COLS, rows=[x0, dx1],
                        consts=[g1], row_outs=[(D_MODEL, F32)], acc_outs=[D_MODEL], epilogue=norm_bwd)

    g_full = {"w_in": g_w_in, "w_up_dil": g_w_ud, "w_up_sb": g_w_us, "w_out": g_w_out,
              "w_mlp_in": g_w_mlp_in, "w_mlp_out": g_w_mlp_out}
    small_part = _pack_small({"norm_mix_g": gg1, "b_gate": g_bg, "norm_mlp_g": gg2, "norm_final_g": gg3}, loss_part)
    chunks = [_chunk(n, g_full[n]).astype(BF16) for n in _SHARDED]
    chunks.append(jnp.broadcast_to(small_part[None], (N_DEV, 1, _SMALL_N)))
    *parts, small_parts = _exchange(chunks)

    tags = ("grad_", "delta_", "new_m_", "new_v_")
    outs = {}
    for n, p in zip(_SHARDED, parts):
        res = _reduce_adamw("adamw_" + n, p, w_shards[n], m_shards[n], v_shards[n], 128)
        for tag, val in zip(tags, res):
            outs[tag + n] = val.reshape(given[n].shape)
    small_w = _pack_small(given, jnp.zeros((1, LANES), F32))
    small_m = _pack_small({n: given["m_" + n] for n, _ in _SMALL}, jnp.zeros((1, LANES), F32))
    small_v = _pack_small({n: given["v_" + n] for n, _ in _SMALL}, jnp.ones((1, LANES), F32))
    small_res = _reduce_adamw("adamw_replicated", small_parts, small_w, small_m, small_v, 8)

    small_shapes = {n: given[n].shape for n, _ in _SMALL}
    for tag, small in zip(tags, small_res):
        small_vals, tail = _unpack_small(small, small_shapes)
        for n, val in small_vals.items():
            outs[tag + n] = val
        if tag == "grad_":
            loss = tail[0, 0]
    names = ["norm_mix_g", "w_in", "b_gate", "w_up_dil", "w_up_sb", "w_out", "norm_mlp_g", "w_mlp_in", "w_mlp_out",
             "norm_final_g"]
    return (loss, grad_x.reshape(x.shape), *[outs["grad_" + n] for n in names], *[outs["delta_" + n] for n in names],
            *[outs["new_m_" + n] for n in names], *[outs["new_v_" + n] for n in names])
```

```python
import functools
import math

import jax
import jax.numpy as jnp
from jax import lax
from jax.experimental import pallas as pl
from jax.experimental.pallas import tpu as pltpu

_pcall = pl.pallas_call

F32 = jnp.float32
BF16 = jnp.bfloat16

D_MODEL = 1024
HEAD_DIM = 64
DIL_GROUPS = ((128, 1), (512, 4), (2048, 16))
DIL_HEADS_PER_GROUP = 4
N_DIL_HEADS = 12
N_SB_HEADS = 8
DIL_WIDTH = 768
DIL_OUT_WIDTH = 256
SB_WIDTH = 512
D_FF = 4096
BLOCK = 128
RMS_EPS = 1e-6
NEG_INF = -1e30
QKV_COLS = 3 * DIL_WIDTH + 3 * SB_WIDTH
IN_COLS = QKV_COLS + 2 * D_MODEL
N_DEV = 8

ADAM_LR = 0.001
ADAM_B1 = 0.9
ADAM_B2 = 0.999
ADAM_EPS = 1e-08
ADAM_WD = 0.01
ADAM_STEP = 10

VMEM_LIMIT = 56 * 1024 * 1024
SB_TK = 256
SB_TQ_FWD = 512
SB_TQ_BWD = 256
LANES = 128

_ARB = pltpu.ARBITRARY


def _cparams(n_axes, **kw):
    return pltpu.CompilerParams(dimension_semantics=(_ARB,) * n_axes, vmem_limit_bytes=VMEM_LIMIT, **kw)


def _dot(a, b):
    return jnp.dot(a, b, preferred_element_type=F32)


def _dot_nt(a, b):
    return lax.dot_general(a, b, (((1,), (1,)), ((), ())), preferred_element_type=F32)


def _dot_tn(a, b):
    return lax.dot_general(a, b, (((0,), (0,)), ((), ())), preferred_element_type=F32)


def _split_hi_lo(x):
    hi = x.astype(BF16)
    lo = (x - hi.astype(F32)).astype(BF16)
    return hi, lo


def _dot_hi_lo(x, m):
    hi, lo = _split_hi_lo(x)
    return _dot(hi, m) + _dot(lo, m)


def _sigmoid(x):
    return 1.0 / (1.0 + jnp.exp(-x))


def _mm(name, a, b, *, ta=False, tb=False, out_dtype, tm, tn, tk, epilogue=None, extra=None):
    m = a.shape[1] if ta else a.shape[0]
    k = a.shape[0] if ta else a.shape[1]
    n = b.shape[0] if tb else b.shape[1]
    assert (b.shape[1] if tb else b.shape[0]) == k
    tm, tn, tk = min(tm, m), min(tn, n), min(tk, k)
    assert m % tm == 0 and n % tn == 0 and k % tk == 0, (name, m, n, k, tm, tn, tk)
    nk = k // tk
    dn = (((0 if ta else 1,), (1 if tb else 0,)), ((), ()))
    in_place = nk > 1 and epilogue is None and out_dtype == F32

    def body(*refs):
        if extra is not None:
            a_ref, b_ref, e_ref, o_ref = refs[:4]
        else:
            a_ref, b_ref, o_ref = refs[:3]
            e_ref = None

        def finish(r):
            if epilogue is not None:
                r = epilogue(r, None if e_ref is None else e_ref[...])
            o_ref[...] = r.astype(out_dtype)

        part = lax.dot_general(a_ref[...].astype(BF16), b_ref[...].astype(BF16), dn, preferred_element_type=F32)
        if nk == 1:
            finish(part)
        else:
            acc_ref = o_ref if in_place else refs[-1]
            kk = pl.program_id(2)

            @pl.when(kk == 0)
            def _():
                acc_ref[...] = part

            @pl.when(kk > 0)
            def _():
                acc_ref[...] += part

            if not in_place:

                @pl.when(kk == nk - 1)
                def _():
                    finish(acc_ref[...])

    a_spec = pl.BlockSpec((tk, tm), lambda j, i, kk: (kk, i)) if ta else pl.BlockSpec((tm, tk), lambda j, i, kk: (i, kk))
    b_spec = pl.BlockSpec((tn, tk), lambda j, i, kk: (j, kk)) if tb else pl.BlockSpec((tk, tn), lambda j, i, kk: (kk, j))
    o_spec = pl.BlockSpec((tm, tn), lambda j, i, kk: (i, j))
    in_specs = [a_spec, b_spec]
    args = [a, b]
    if extra is not None:
        in_specs.append(o_spec)
        args.append(extra)
    return _pcall(
        body,
        name=name,
        grid=(n // tn, m // tm, nk),
        in_specs=in_specs,
        out_specs=o_spec,
        out_shape=jax.ShapeDtypeStruct((m, n), out_dtype),
        scratch_shapes=[pltpu.VMEM((tm, tn), F32)] if (nk > 1 and not in_place) else [],
        compiler_params=_cparams(3),
    )(*args)


def _rowk(name, *, a=None, w=None, nt=False, tm, tk=None, rows=(), consts=(), row_outs=(), acc_outs=(), epilogue):
    has_mm = a is not None
    m = a.shape[0] if has_mm else rows[0].shape[0]
    assert m % tm == 0
    nm = m // tm
    if has_mm:
        k = a.shape[1]
        n = w.shape[0] if nt else w.shape[1]
        tk = min(tk, k)
        assert k % tk == 0
        nk = k // tk
    else:
        nk = 1
    n_rows, n_consts, n_ro, n_ao = len(rows), len(consts), len(row_outs), len(acc_outs)

    def body(*refs):
        pos = 0
        if has_mm:
            a_ref, w_ref = refs[0], refs[1]
            pos = 2
        row_refs = refs[pos:pos + n_rows]
        pos += n_rows
        const_refs = refs[pos:pos + n_consts]
        pos += n_consts
        ro_refs = refs[pos:pos + n_ro]
        pos += n_ro
        ao_refs = refs[pos:pos + n_ao]
        pos += n_ao
        i = pl.program_id(0)
        kk = pl.program_id(1)

        def finish(acc):
            ro_vals, ao_vals = epilogue(acc, [r[...] for r in row_refs], [c[...] for c in const_refs])
            for r, v in zip(ro_refs, ro_vals):
                r[...] = v.astype(r.dtype)
            for r, v in zip(ao_refs, ao_vals):

                @pl.when(i == 0)
                def _(r=r, v=v):
                    r[...] = v

                @pl.when(i > 0)
                def _(r=r, v=v):
                    r[...] += v

        if not has_mm:
            finish(None)
            return
        if nt:
            part = _dot_nt(a_ref[...].astype(BF16), w_ref[...])
        else:
            part = _dot(a_ref[...].astype(BF16), w_ref[...])
        if nk == 1:
            finish(part)
        else:
            acc_ref = refs[pos]

            @pl.when(kk == 0)
            def _():
                acc_ref[...] = part

            @pl.when(kk > 0)
            def _():
                acc_ref[...] += part

            @pl.when(kk == nk - 1)
            def _():
                finish(acc_ref[...])

    once = pl.Buffered(1)
    in_specs, args = [], []
    if has_mm:
        in_specs.append(pl.BlockSpec((tm, tk), lambda i, kk: (i, kk)))
        w_mode = once if nk == 1 else None
        in_specs.append(pl.BlockSpec((n, tk), lambda i, kk: (0, kk), pipeline_mode=w_mode) if nt
                        else pl.BlockSpec((tk, n), lambda i, kk: (kk, 0), pipeline_mode=w_mode))
        args += [a, w]
    for r in rows:
        in_specs.append(pl.BlockSpec((tm, r.shape[1]), lambda i, kk: (i, 0)))
        args.append(r)
    for c in consts:
        in_specs.append(pl.BlockSpec(c.shape, lambda i, kk: (0,) * c.ndim, pipeline_mode=once))
        args.append(c)
    out_specs, out_shape = [], []
    for width, dt in row_outs:
        out_specs.append(pl.BlockSpec((tm, width), lambda i, kk: (i, 0)))
        out_shape.append(jax.ShapeDtypeStruct((m, width), dt))
    for width in acc_outs:
        out_specs.append(pl.BlockSpec((1, width), lambda i, kk: (0, 0)))
        out_shape.append(jax.ShapeDtypeStruct((1, width), F32))
    return _pcall(
        body,
        name=name,
        grid=(nm, nk),
        in_specs=in_specs,
        out_specs=out_specs,
        out_shape=out_shape,
        scratch_shapes=[pltpu.VMEM((tm, n), F32)] if (has_mm and nk > 1) else [],
        compiler_params=_cparams(2),
    )(*args)


def _rms_stats(x):
    r = lax.rsqrt(jnp.mean(x * x, axis=-1, keepdims=True) + RMS_EPS)
    return r, x * r


def _rms_bwd(dh, xh, r, g):
    gy = dh * g
    dx = r * (gy - xh * jnp.mean(gy * xh, axis=-1, keepdims=True))
    return dx, jnp.sum(dh * xh, axis=0, keepdims=True)


def _alibi_slope(head):
    return 2.0 ** (-8.0 * (head + 1) / N_DIL_HEADS)


def _dil_masks(i):
    qi = lax.broadcasted_iota(jnp.int32, (BLOCK, 2 * BLOCK), 0)
    kj = lax.broadcasted_iota(jnp.int32, (BLOCK, 2 * BLOCK), 1)
    steps = qi + BLOCK - kj
    valid = (steps >= 0) & (steps <= BLOCK) & ((kj >= BLOCK) | (i > 0))
    return steps.astype(F32), valid


def _dil_view(qkv, group):
    _, dilation = DIL_GROUPS[group]
    if dilation == 1:
        return qkv, QKV_COLS // DIL_OUT_WIDTH, (group, 3 + group, 6 + group)
    w = DIL_OUT_WIDTH
    own = jnp.concatenate([qkv[:, (3 * part + group) * w:(3 * part + group + 1) * w] for part in range(3)], axis=1)
    return own.reshape(qkv.shape[0] // dilation, dilation * 3 * w), 3, (0, 1, 2)


def _dil_specs(ncb, cols, clamp):
    def cur(col):
        return pl.BlockSpec((BLOCK, DIL_OUT_WIDTH), lambda r, i: (clamp(i), r * ncb + col))

    def prev(col):
        return pl.BlockSpec((BLOCK, DIL_OUT_WIDTH), lambda r, i: (jnp.maximum(clamp(i) - 1, 0), r * ncb + col))

    return [cur(cols[0]), cur(cols[1]), prev(cols[1]), cur(cols[2]), prev(cols[2])]


def _dil_fwd(qkv, group):
    window, dilation = DIL_GROUPS[group]
    s = qkv.shape[0]
    sub = s // dilation
    nb = sub // BLOCK
    assert nb * BLOCK * dilation == s and window // dilation == BLOCK
    slopes = [_alibi_slope(group * DIL_HEADS_PER_GROUP + h) * dilation for h in range(DIL_HEADS_PER_GROUP)]

    def body(q_ref, kc_ref, kp_ref, vc_ref, vp_ref, o_ref, lse_ref):
        i = pl.program_id(1)
        q = q_ref[...]
        kk = jnp.concatenate([kp_ref[...], kc_ref[...]], axis=0)
        vv = jnp.concatenate([vp_ref[...], vc_ref[...]], axis=0)
        head_id = lax.broadcasted_iota(jnp.int32, (1, DIL_OUT_WIDTH), 1) // HEAD_DIM
        steps, valid = _dil_masks(i)
        heads = range(DIL_HEADS_PER_GROUP)
        scores = [_dot_nt(jnp.where(head_id == h, q, jnp.zeros_like(q)), kk) for h in heads]
        ps, lses = [], []
        for h in heads:
            logits = scores[h] * (1.0 / math.sqrt(HEAD_DIM)) - slopes[h] * steps
            logits = jnp.where(valid, logits, NEG_INF)
            mx = jnp.max(logits, axis=1, keepdims=True)
            e = jnp.exp(logits - mx)
            den = jnp.sum(e, axis=1, keepdims=True)
            lses.append(mx + jnp.log(den))
            ps.append((e * (1.0 / den)).astype(BF16))
        outs = [_dot(ps[h], vv) for h in heads]
        o, lse_all = outs[0], lses[0]
        for h in heads[1:]:
            o = jnp.where(head_id == h, outs[h], o)
            lse_all = jnp.where(head_id == h, lses[h], lse_all)
        o_ref[...] = o
        lse_ref[...] = jnp.broadcast_to(lse_all, o.shape)

    qkv_v, ncb, cols = _dil_view(qkv, group)
    out_spec = pl.BlockSpec((BLOCK, DIL_OUT_WIDTH), lambda r, i: (i, r))
    o, lse = _pcall(
        body,
        name=f"dil_fwd_g{group}",
        grid=(dilation, nb),
        in_specs=_dil_specs(ncb, cols, lambda i: i),
        out_specs=[out_spec, out_spec],
        out_shape=[jax.ShapeDtypeStruct((sub, dilation * DIL_OUT_WIDTH), F32)] * 2,
        compiler_params=_cparams(2),
    )(qkv_v, qkv_v, qkv_v, qkv_v, qkv_v)
    return o.reshape(s, DIL_OUT_WIDTH), lse.reshape(s, DIL_OUT_WIDTH)


def _dil_bwd(qkv, do_g, lse_g, dterm_g, group):
    window, dilation = DIL_GROUPS[group]
    s = qkv.shape[0]
    sub = s // dilation
    nb = sub // BLOCK
    slopes = [_alibi_slope(group * DIL_HEADS_PER_GROUP + h) * dilation for h in range(DIL_HEADS_PER_GROUP)]
    scale = 1.0 / math.sqrt(HEAD_DIM)

    def body(q_ref, kc_ref, kp_ref, vc_ref, vp_ref, do_ref, lse_ref, dt_ref, dq_ref, dk_ref, dv_ref, ck_ref, cv_ref):
        i = pl.program_id(1)

        @pl.when(i == 0)
        def _():
            ck_ref[...] = jnp.zeros_like(ck_ref)
            cv_ref[...] = jnp.zeros_like(cv_ref)

        @pl.when(i < nb)
        def _():
            q = q_ref[...]
            do = do_ref[...]
            lse_all = lse_ref[...]
            dt_all = dt_ref[...]
            kk = jnp.concatenate([kp_ref[...], kc_ref[...]], axis=0)
            vv = jnp.concatenate([vp_ref[...], vc_ref[...]], axis=0)
            lane = lax.broadcasted_iota(jnp.int32, (1, DIL_OUT_WIDTH), 1)
            head_id = lane // HEAD_DIM
            steps, valid = _dil_masks(i)
            heads = range(DIL_HEADS_PER_GROUP)
            qms = [jnp.where(head_id == h, q, jnp.zeros_like(q)) for h in heads]
            doms = [jnp.where(head_id == h, do, jnp.zeros_like(do)) for h in heads]
            scores = [_dot_nt(qms[h], kk) for h in heads]
            dps = [_dot_nt(doms[h], vv) for h in heads]
            pbs, dss = [], []
            for h in heads:
                first = lane == h * HEAD_DIM
                lse = jnp.sum(jnp.where(first, lse_all, 0.0), axis=1, keepdims=True)
                dt = jnp.sum(jnp.where(first, dt_all, 0.0), axis=1, keepdims=True)
                logits = scores[h] * scale - slopes[h] * steps
                p = jnp.where(valid, jnp.exp(jnp.where(valid, logits, NEG_INF) - lse), 0.0)
                pbs.append(p.astype(BF16))
                dss.append((p * (dps[h] + dt) * scale).astype(BF16))
            dqs = [_dot(dss[h], kk) for h in heads]
            dks = [_dot_tn(dss[h], qms[h]) for h in heads]
            dvs = [_dot_tn(pbs[h], doms[h]) for h in heads]
            dq = dqs[0]
            for h in heads[1:]:
                dq = jnp.where(head_id == h, dqs[h], dq)
            dkk = (dks[0] + dks[1]) + (dks[2] + dks[3])
            dvv = (dvs[0] + dvs[1]) + (dvs[2] + dvs[3])
            dq_ref[...] = dq.astype(dq_ref.dtype)
            dk_ref[...] = (ck_ref[...] + dkk[:BLOCK]).astype(dk_ref.dtype)
            dv_ref[...] = (cv_ref[...] + dvv[:BLOCK]).astype(dv_ref.dtype)
            ck_ref[...] = dkk[BLOCK:]
            cv_ref[...] = dvv[BLOCK:]

        @pl.when(i == nb)
        def _():
            dk_ref[...] = ck_ref[...].astype(dk_ref.dtype)
            dv_ref[...] = cv_ref[...].astype(dv_ref.dtype)

    clamp = lambda i: jnp.minimum(i, nb - 1)
    qkv_v, ncb, cols = _dil_view(qkv, group)
    view = lambda t: t.reshape(sub, dilation * DIL_OUT_WIDTH)
    row_spec = pl.BlockSpec((BLOCK, DIL_OUT_WIDTH), lambda r, i: (clamp(i), r))
    late_spec = pl.BlockSpec((BLOCK, DIL_OUT_WIDTH), lambda r, i: (jnp.maximum(i - 1, 0), r))
    dq, dk, dv = _pcall(
        body,
        name=f"dil_bwd_g{group}",
        grid=(dilation, nb + 1),
        in_specs=_dil_specs(ncb, cols, clamp) + [row_spec, row_spec, row_spec],
        out_specs=[row_spec, late_spec, late_spec],
        out_shape=[jax.ShapeDtypeStruct((sub, dilation * DIL_OUT_WIDTH), BF16)] * 3,
        scratch_shapes=[pltpu.VMEM((BLOCK, DIL_OUT_WIDTH), F32)] * 2,
        compiler_params=_cparams(2),
    )(qkv_v, qkv_v, qkv_v, qkv_v, qkv_v, view(do_g), view(lse_g), view(dterm_g))
    return dq.reshape(s, DIL_OUT_WIDTH), dk.reshape(s, DIL_OUT_WIDTH), dv.reshape(s, DIL_OUT_WIDTH)


def _head_block_ones():
    r = lax.broadcasted_iota(jnp.int32, (DIL_OUT_WIDTH, DIL_OUT_WIDTH), 0) // HEAD_DIM
    c = lax.broadcasted_iota(jnp.int32, (DIL_OUT_WIDTH, DIL_OUT_WIDTH), 1) // HEAD_DIM
    return jnp.where(r == c, 1.0, 0.0).astype(BF16)


def _dil_mix_weights(l0, l1, l2):
    mx = jnp.maximum(jnp.maximum(l0, l1), l2)
    e0, e1, e2 = jnp.exp(l0 - mx), jnp.exp(l1 - mx), jnp.exp(l2 - mx)
    inv = 1.0 / (e0 + e1 + e2)
    return e0 * inv, e1 * inv, e2 * inv


def _dil_mix_fwd(os_, lses, tm):
    def epi(_, rows, consts):
        o0, o1, o2, l0, l1, l2 = rows
        w0, w1, w2 = _dil_mix_weights(l0, l1, l2)
        return [w0 * o0 + w1 * o1 + w2 * o2], []

    (o_a,) = _rowk("dil_mix_fwd", tm=tm, rows=list(os_) + list(lses), row_outs=[(DIL_OUT_WIDTH, BF16)], epilogue=epi)
    return o_a


def _dil_mix_bwd(do_a, os_, lses, tm):
    def epi(_, rows, consts):
        do, o0, o1, o2, l0, l1, l2 = rows
        do = do.astype(F32)
        w0, w1, w2 = _dil_mix_weights(l0, l1, l2)
        mixed = w0 * o0 + w1 * o1 + w2 * o2
        tot = _dot_hi_lo(do * mixed, _head_block_ones())
        return [w0 * do, w1 * do, w2 * do, -w0 * tot, -w1 * tot, -w2 * tot], []

    return _rowk(
        "dil_mix_bwd", tm=tm, rows=[do_a] + list(os_) + list(lses),
        row_outs=[(DIL_OUT_WIDTH, BF16)] * 3 + [(DIL_OUT_WIDTH, F32)] * 3, epilogue=epi)


_SB_Q0 = 3 * DIL_WIDTH // LANES
_SB_K0 = _SB_Q0 + SB_WIDTH // LANES
_SB_V0 = _SB_K0 + SB_WIDTH // LANES


_LOG2E = 1.4426950408889634
_EXP2_CLAMP = 126.0


def _tri(t, op):
    r = lax.broadcasted_iota(jnp.int32, (t, t), 0)
    c = lax.broadcasted_iota(jnp.int32, (t, t), 1)
    return jnp.where(op(r, c), 1.0, 0.0).astype(BF16)


def _softplus2(z2):
    return jnp.maximum(z2, jnp.log2(1.0 + jnp.exp2(jnp.minimum(z2, _EXP2_CLAMP))))


def _sb_fwd(qkv):
    s = qkv.shape[0]
    t, tq = SB_TK, min(SB_TQ_FWD, s)
    assert tq in (t, 2 * t) and s % (2 * t) == 0
    nq = s // tq
    n_pairs = SB_WIDTH // LANES

    def body(q_ref, k_ref, v_ref, o_ref, tot_ref):
        i = pl.program_id(1)
        q = q_ref[...] * (1.0 / math.sqrt(HEAD_DIM))
        lane_hi = lax.broadcasted_iota(jnp.int32, (1, LANES), 1) // HEAD_DIM
        later = _tri(t, lambda r, c: r > c)
        row = lax.broadcasted_iota(jnp.int32, (tq, t), 0)
        col = lax.broadcasted_iota(jnp.int32, (tq, t), 1)
        qms = [jnp.where(lane_hi == hh, q, jnp.zeros_like(q)) for hh in range(2)]

        def step(jj, carry, diag):
            tiles = (2 * jj + 1, 2 * jj)
            offs = [pl.multiple_of(j * t, t) for j in tiles]
            ks = [k_ref[pl.ds(off, t), :] for off in offs]
            vs = [v_ref[pl.ds(off, t), :] for off in offs]
            masks = [(j * t + col) < (i * tq + row) for j in tiles] if diag else None
            chains = [(n, hh) for n in range(2) for hh in range(2)]
            z2s = [_dot_nt(qms[hh], ks[n]) * _LOG2E for n, hh in chains]
            sps, lposs = [], []
            for (n, hh), z2 in zip(chains, z2s):
                sp = _softplus2(z2)
                lposs.append(z2 - sp)
                sps.append(jnp.where(masks[n], sp, 0.0) if diag else sp)
            sufs = [_dot(sp.astype(BF16), later) for sp in sps]
            cs = [carry[0], carry[2]]
            accs = [carry[1], carry[3]]
            for idx, (n, hh) in enumerate(chains):
                a = jnp.exp2(lposs[idx] - sufs[idx] - cs[hh])
                if diag:
                    a = jnp.where(masks[n], a, 0.0)
                accs[hh] = accs[hh] + _dot(a.astype(BF16), vs[n])
                cs[hh] = cs[hh] + jnp.sum(sps[idx], axis=1, keepdims=True)
            return cs[0], accs[0], cs[1], accs[1]

        zc, za = jnp.zeros((tq, 1), F32), jnp.zeros((tq, LANES), F32)
        last = (i * tq) // (2 * t)
        carry = step(last, (zc, za, zc, za), True)
        carry = lax.fori_loop(0, last, lambda n, ca: step(last - 1 - n, ca, False), carry)
        out = jnp.where(lane_hi == 0, carry[1], carry[3])
        tot = jnp.where(lane_hi == 0, carry[0], carry[2])
        o_ref[...] = out.astype(o_ref.dtype)
        tot_ref[...] = tot

    o, tot = _pcall(
        body,
        name="sb_fwd",
        grid=(n_pairs, nq),
        in_specs=[
            pl.BlockSpec((tq, LANES), lambda p, i: (i, _SB_Q0 + p)),
            pl.BlockSpec((s, LANES), lambda p, i: (0, _SB_K0 + p)),
            pl.BlockSpec((s, LANES), lambda p, i: (0, _SB_V0 + p)),
        ],
        out_specs=[pl.BlockSpec((tq, LANES), lambda p, i: (i, p))] * 2,
        out_shape=[jax.ShapeDtypeStruct((s, SB_WIDTH), BF16), jax.ShapeDtypeStruct((s, SB_WIDTH), F32)],
        compiler_params=_cparams(2),
    )(qkv, qkv, qkv)
    return o, tot


def _sb_bwd(qkv, do_b, tot_b):
    s = qkv.shape[0]
    t, tq = SB_TK, min(SB_TQ_BWD, s)
    assert tq in (t, 2 * t) and s % (2 * t) == 0
    nq = s // tq
    n_pairs = SB_WIDTH // LANES
    scale = 1.0 / math.sqrt(HEAD_DIM)

    def body(q_ref, k_ref, v_ref, do_ref, tot_ref, dq_ref, dk_ref, dv_ref):
        i = pl.program_id(1)

        @pl.when(i == 0)
        def _():
            dk_ref[...] = jnp.zeros_like(dk_ref)
            dv_ref[...] = jnp.zeros_like(dv_ref)

        q = q_ref[...] * scale
        do = do_ref[...]
        tot_all = tot_ref[...]
        lane = lax.broadcasted_iota(jnp.int32, (1, LANES), 1)
        lane_hi = lane // HEAD_DIM
        later = _tri(t, lambda r, c: r > c)
        before = _tri(t, lambda r, c: r < c)
        row = lax.broadcasted_iota(jnp.int32, (tq, t), 0)
        col = lax.broadcasted_iota(jnp.int32, (tq, t), 1)
        qms = [jnp.where(lane_hi == hh, q, jnp.zeros_like(q)) for hh in range(2)]
        doms = [jnp.where(lane_hi == hh, do, jnp.zeros_like(do)) for hh in range(2)]
        tots = [jnp.sum(jnp.where(lane == hh * HEAD_DIM, tot_all, 0.0), axis=1, keepdims=True) for hh in range(2)]

        def step(jj, carry, diag):
            tiles = (2 * jj, 2 * jj + 1)
            offs = [pl.multiple_of(j * t, t) for j in tiles]
            ks = [k_ref[pl.ds(off, t), :] for off in offs]
            vs = [v_ref[pl.ds(off, t), :] for off in offs]
            masks = [(j * t + col) < (i * tq + row) for j in tiles] if diag else None
            chains = [(n, hh) for n in range(2) for hh in range(2)]
            z2s = [_dot_nt(qms[hh], ks[n]) * _LOG2E for n, hh in chains]
            sps, sigs = [], []
            for (n, hh), z2 in zip(chains, z2s):
                sp = _softplus2(z2)
                sigs.append(jnp.exp2(z2 - sp))
                sps.append(jnp.where(masks[n], sp, 0.0) if diag else sp)
            sufs = [_dot(sp.astype(BF16), later) for sp in sps]
            das = [_dot_nt(doms[hh], vs[n]) for n, hh in chains]
            cls = [carry[0], carry[3]]
            cgs = [carry[1], carry[4]]
            accs = [carry[2], carry[5]]
            gs, abs_, cg_at = [], [], []
            for idx, (n, hh) in enumerate(chains):
                cls[hh] = cls[hh] + jnp.sum(sps[idx], axis=1, keepdims=True)
                a = sigs[idx] * jnp.exp2(-sufs[idx] - (tots[hh] - cls[hh]))
                if diag:
                    a = jnp.where(masks[n], a, 0.0)
                g = a * das[idx]
                gs.append(g)
                abs_.append(a.astype(BF16))
                cg_at.append(cgs[hh])
                cgs[hh] = cgs[hh] + jnp.sum(g, axis=1, keepdims=True)
            prefs = [_dot(g.astype(BF16), before) for g in gs]
            dvs = [_dot_tn(abs_[idx], doms[hh]) for idx, (n, hh) in enumerate(chains)]
            dzs = []
            for idx, (n, hh) in enumerate(chains):
                g = gs[idx]
                dz = g - sigs[idx] * (g + prefs[idx] + cg_at[idx])
                if diag:
                    dz = jnp.where(masks[n], dz, 0.0)
                dzs.append(dz.astype(BF16))
            for idx, (n, hh) in enumerate(chains):
                accs[hh] = accs[hh] + _dot(dzs[idx], ks[n])
            dks = [_dot_tn(dzs[idx], qms[hh]) for idx, (n, hh) in enumerate(chains)]
            for n in range(2):
                dk_ref[pl.ds(offs[n], t), :] += dks[2 * n] + dks[2 * n + 1]
                dv_ref[pl.ds(offs[n], t), :] += dvs[2 * n] + dvs[2 * n + 1]
            return cls[0], cgs[0], accs[0], cls[1], cgs[1], accs[1]

        zc, za = jnp.zeros((tq, 1), F32), jnp.zeros((tq, LANES), F32)
        last = (i * tq) // (2 * t)
        carry = lax.fori_loop(0, last, lambda jj, ca: step(jj, ca, False), (zc, zc, za, zc, zc, za))
        carry = step(last, carry, True)
        dq = jnp.where(lane_hi == 0, carry[2], carry[5])
        dq_ref[...] = (dq * scale).astype(dq_ref.dtype)

    row_spec = pl.BlockSpec((tq, LANES), lambda p, i: (i, p))
    full_spec = pl.BlockSpec((s, LANES), lambda p, i: (0, p))
    return _pcall(
        body,
        name="sb_bwd",
        grid=(n_pairs, nq),
        in_specs=[
            pl.BlockSpec((tq, LANES), lambda p, i: (i, _SB_Q0 + p)),
            pl.BlockSpec((s, LANES), lambda p, i: (0, _SB_K0 + p)),
            pl.BlockSpec((s, LANES), lambda p, i: (0, _SB_V0 + p)),
            row_spec, row_spec,
        ],
        out_specs=[row_spec, full_spec, full_spec],
        out_shape=[jax.ShapeDtypeStruct((s, SB_WIDTH), BF16), jax.ShapeDtypeStruct((s, SB_WIDTH), F32),
                   jax.ShapeDtypeStruct((s, SB_WIDTH), F32)],
        compiler_params=_cparams(2),
    )(qkv, qkv, qkv, do_b, tot_b)


def _gates(gl, bg):
    return _sigmoid(gl[:, :D_MODEL] + bg[:, :D_MODEL]), _sigmoid(gl[:, D_MODEL:] + bg[:, D_MODEL:])


def _mixer_fwd(o_a, o_b, gl, x0, bg, g2, w_ud, w_us, w_out, tm):
    def epi(_, rows, consts):
        oa, ob, glv, x = rows
        bgv, g2v, wud, wus, wout = consts
        ga, gb = _gates(glv, bgv)
        merged = ga * _dot(oa, wud) + gb * _dot(ob, wus)
        x1 = x + _dot(merged.astype(BF16), wout)
        r, xh = _rms_stats(x1)
        return [x1, xh * g2v], []

    return _rowk("mixer_fwd", tm=tm, rows=[o_a, o_b, gl, x0], consts=[bg, g2, w_ud, w_us, w_out],
                 row_outs=[(D_MODEL, F32), (D_MODEL, BF16)], epilogue=epi)


def _mixer_bwd(dx1, o_a, o_b, gl, bg, w_ud, w_us, w_out, tm):
    s = dx1.shape[0]
    nm = s // tm

    def body(dx_ref, oa_ref, ob_ref, gl_ref, bg_ref, wud_ref, wus_ref, wout_ref,
             doa_ref, dob_ref, dgl_ref, gwout_ref, gwud_ref, gwus_ref, gbg_ref):
        i = pl.program_id(0)
        dxb = dx_ref[...].astype(BF16)
        oa, ob = oa_ref[...], ob_ref[...]
        ga, gb = _gates(gl_ref[...], bg_ref[...])
        ua, ub = _dot(oa, wud_ref[...]), _dot(ob, wus_ref[...])
        merged = (ga * ua + gb * ub).astype(BF16)
        dm = _dot_nt(dxb, wout_ref[...])
        dua = (dm * ga).astype(BF16)
        dub = (dm * gb).astype(BF16)
        dgla = dm * ua * ga * (1.0 - ga)
        dglb = dm * ub * gb * (1.0 - gb)
        doa_ref[...] = _dot_nt(dua, wud_ref[...]).astype(doa_ref.dtype)
        dob_ref[...] = _dot_nt(dub, wus_ref[...]).astype(dob_ref.dtype)
        dgl_ref[:, :D_MODEL] = dgla.astype(dgl_ref.dtype)
        dgl_ref[:, D_MODEL:] = dglb.astype(dgl_ref.dtype)
        parts = [(gwout_ref, _dot_tn(merged, dxb)), (gwud_ref, _dot_tn(oa, dua)), (gwus_ref, _dot_tn(ob, dub))]
        for r, v in parts:

            @pl.when(i == 0)
            def _(r=r, v=v):
                r[...] = v

            @pl.when(i > 0)
            def _(r=r, v=v):
                r[...] += v

        sa = jnp.sum(dgla, axis=0, keepdims=True)
        sb = jnp.sum(dglb, axis=0, keepdims=True)

        @pl.when(i == 0)
        def _():
            gbg_ref[:, :D_MODEL] = sa
            gbg_ref[:, D_MODEL:] = sb

        @pl.when(i > 0)
        def _():
            gbg_ref[:, :D_MODEL] += sa
            gbg_ref[:, D_MODEL:] += sb

    row = lambda w: pl.BlockSpec((tm, w), lambda i: (i, 0))
    full = lambda a: pl.BlockSpec(a.shape, lambda i: (0, 0))
    fshape = lambda r, c: jax.ShapeDtypeStruct((r, c), F32)
    return _pcall(
        body,
        name="mixer_bwd",
        grid=(nm,),
        in_specs=[row(D_MODEL), row(DIL_OUT_WIDTH), row(SB_WIDTH), row(2 * D_MODEL),
                  full(bg), full(w_ud), full(w_us), full(w_out)],
        out_specs=[row(DIL_OUT_WIDTH), row(SB_WIDTH), row(2 * D_MODEL),
                   pl.BlockSpec((D_MODEL, D_MODEL), lambda i: (0, 0)),
                   pl.BlockSpec((DIL_OUT_WIDTH, D_MODEL), lambda i: (0, 0)),
                   pl.BlockSpec((SB_WIDTH, D_MODEL), lambda i: (0, 0)),
                   pl.BlockSpec((1, 2 * D_MODEL), lambda i: (0, 0))],
        out_shape=[jax.ShapeDtypeStruct((s, DIL_OUT_WIDTH), BF16), jax.ShapeDtypeStruct((s, SB_WIDTH), BF16),
                   jax.ShapeDtypeStruct((s, 2 * D_MODEL), BF16),
                   fshape(D_MODEL, D_MODEL), fshape(DIL_OUT_WIDTH, D_MODEL), fshape(SB_WIDTH, D_MODEL),
                   fshape(1, 2 * D_MODEL)],
        compiler_params=_cparams(1),
    )(dx1, o_a, o_b, gl, bg, w_ud, w_us, w_out)


_HBM = pl.BlockSpec(memory_space=pltpu.HBM)
_MESH = pl.DeviceIdType.MESH


def _all_gather(shards):
    n = len(shards)

    def body(*refs):
        x_refs, out_refs = refs[:n], refs[n:2 * n]
        send_sems, recv_sems, local_sems = refs[2 * n:]
        x, y, c = lax.axis_index("x"), lax.axis_index("y"), lax.axis_index("c")
        me, sibling = (x, y, c), (x, y, 1 - c)
        chips = [(1 - x, y), (x, 1 - y), (1 - x, 1 - y)]

        def slot(a, px, py, pc):
            return out_refs[a].at[4 * px + 2 * py + pc]

        def copy(a, k, block, to, own=False):
            return pltpu.make_async_remote_copy(
                src_ref=x_refs[a] if own else slot(a, *block), dst_ref=slot(a, *block),
                send_sem=send_sems.at[7 * a + k], recv_sem=recv_sems.at[7 * a + k], device_id=to, device_id_type=_MESH)

        mine = [pltpu.make_async_copy(x_refs[a], slot(a, *me), local_sems.at[a]) for a in range(n)]
        for cp in mine:
            cp.start()
        first = []
        for a in range(n):
            first.append(copy(a, 0, me, sibling, own=True))
            first += [copy(a, 1 + j, me, (*chip, c), own=True) for j, chip in enumerate(chips)]
        for cp in first:
            cp.start()
        passed = []
        for a in range(n):
            for j, chip in enumerate(chips):
                copy(a, 1 + j, (*chip, c), me).wait_recv()
                passed.append(copy(a, 4 + j, (*chip, c), sibling))
                passed[-1].start()
        for a in range(n):
            copy(a, 0, sibling, me).wait_recv()
            for j, chip in enumerate(chips):
                copy(a, 4 + j, (*chip, 1 - c), me).wait_recv()
        for cp in first + passed:
            cp.wait_send()
        for cp in mine:
            cp.wait()

    return _pcall(
        body,
        name="all_gather_weights",
        in_specs=[_HBM] * n,
        out_specs=[_HBM] * n,
        out_shape=[jax.ShapeDtypeStruct((N_DEV,) + s.shape, s.dtype) for s in shards],
        scratch_shapes=[pltpu.SemaphoreType.DMA((7 * n,)), pltpu.SemaphoreType.DMA((7 * n,)),
                        pltpu.SemaphoreType.DMA((n,))],
    )(*shards)


def _exchange(chunks):
    n = len(chunks)

    def body(*refs):
        g_refs, o_refs = refs[:n], refs[n:2 * n]
        send_sems, recv_sems, local_sems = refs[2 * n:]
        x, y, c = lax.axis_index("x"), lax.axis_index("y"), lax.axis_index("c")
        me = 4 * x + 2 * y + c
        own = [pltpu.make_async_copy(g_refs[a].at[me], o_refs[a].at[me], local_sems.at[a]) for a in range(n)]
        for cp in own:
            cp.start()
        copies = []
        for a in range(n):
            for k in range(1, N_DEV):
                px, py, pc = x ^ (k >> 2), y ^ ((k >> 1) & 1), c ^ (k & 1)
                peer = 4 * px + 2 * py + pc
                copies.append(pltpu.make_async_remote_copy(
                    src_ref=g_refs[a].at[peer], dst_ref=o_refs[a].at[me], send_sem=send_sems.at[7 * a + k - 1],
                    recv_sem=recv_sems.at[7 * a + k - 1], device_id=(px, py, pc), device_id_type=_MESH))
        for cp in copies:
            cp.start()
        for cp in copies:
            cp.wait()
        for cp in own:
            cp.wait()

    return _pcall(
        body,
        name="exchange_grads",
        in_specs=[_HBM] * n,
        out_specs=[_HBM] * n,
        out_shape=[jax.ShapeDtypeStruct(g.shape, g.dtype) for g in chunks],
        scratch_shapes=[pltpu.SemaphoreType.DMA((7 * n,)), pltpu.SemaphoreType.DMA((7 * n,)),
                        pltpu.SemaphoreType.DMA((n,))],
    )(*chunks)


_SEM = pl.BlockSpec(memory_space=pltpu.SEMAPHORE)
_EFFECT = pltpu.SideEffectType.DATAFLOW_SIDE_EFFECTING


def _peers(x, y, c):
    out = []
    for k in range(1, N_DEV):
        px, py, pc = x ^ (k >> 2), y ^ ((k >> 1) & 1), c ^ (k & 1)
        out.append(((px, py, pc), 4 * px + 2 * py + pc))
    return out


def _spread_copies(src_refs, land_refs, send_sems, recv_sems, chunked):
    x, y, c = lax.axis_index("x"), lax.axis_index("y"), lax.axis_index("c")
    me = 4 * x + 2 * y + c
    copies = []
    for a, (src, land) in enumerate(zip(src_refs, land_refs)):
        for k, (peer_id, peer) in enumerate(_peers(x, y, c)):
            copies.append(pltpu.make_async_remote_copy(
                src_ref=src.at[peer] if chunked else src, dst_ref=land.at[me], send_sem=send_sems.at[7 * a + k],
                recv_sem=recv_sems.at[7 * a + k], device_id=peer_id, device_id_type=_MESH))
    return copies


def _spread_start(name, srcs, chunked):
    n = len(srcs)
    lands = [lax.empty((N_DEV,) + (s.shape[1:] if chunked else s.shape), s.dtype) for s in srcs]

    def body(*refs):
        src_refs, land_refs = refs[:n], refs[n:2 * n]
        send_sems, recv_sems = refs[2 * n], refs[2 * n + 1]
        token = refs[-1]
        for cp in _spread_copies(src_refs, land_refs, send_sems, recv_sems, chunked):
            cp.start()
        token[...] = jnp.zeros_like(token)

    hbm = lambda a: pltpu.HBM(a.shape, a.dtype)
    outs = _pcall(
        body,
        name=name,
        out_shape=(pltpu.SemaphoreType.DMA((7 * n,)), pltpu.SemaphoreType.DMA((7 * n,)),
                   *[hbm(s) for s in srcs], *[hbm(l) for l in lands], jax.ShapeDtypeStruct((8, LANES), F32)),
        in_specs=[_HBM] * (2 * n),
        out_specs=(_SEM, _SEM, *([_HBM] * (2 * n)), pl.BlockSpec(memory_space=pltpu.VMEM)),
        input_output_aliases={i: 2 + i for i in range(2 * n)},
        compiler_params=pltpu.CompilerParams(has_side_effects=_EFFECT),
    )(*[pltpu.with_memory_space_constraint(a, pltpu.HBM) for a in list(srcs) + lands])
    return outs[0], outs[1], list(outs[2:2 + n]), list(outs[2 + n:2 + 2 * n]), outs[-1]


def _spread_wait(name, send_sems, recv_sems, srcs, lands, after, chunked):
    n = len(srcs)

    def body(*refs):
        src_refs, land_refs = refs[:n], refs[n:2 * n]
        for cp in _spread_copies(src_refs, land_refs, refs[2 * n], refs[2 * n + 1], chunked):
            cp.wait_send()
            cp.wait_recv()

    hbm = lambda a: pltpu.HBM(a.shape, a.dtype)
    outs = _pcall(
        body,
        name=name,
        out_shape=tuple(hbm(a) for a in list(srcs) + list(lands)),
        in_specs=[_HBM] * (2 * n) + [_SEM, _SEM, pl.BlockSpec(memory_space=pl.ANY)],
        out_specs=tuple([_HBM] * (2 * n)),
        input_output_aliases={i: i for i in range(2 * n)},
        compiler_params=pltpu.CompilerParams(has_side_effects=_EFFECT),
    )(*srcs, *lands, send_sems, recv_sems, after)
    return list(outs[:n]), list(outs[n:])


def _with_own(land, own):
    me = 4 * lax.axis_index("x") + 2 * lax.axis_index("y") + lax.axis_index("c")
    return lax.dynamic_update_slice(land, own[None], (me,) + (0,) * own.ndim)


def _reduce_adamw(name, parts, w, m, v, tr):
    _, rows, cols = parts.shape
    tr = min(tr, rows)
    assert rows % tr == 0
    c1 = 1.0 / (1.0 - ADAM_B1 ** ADAM_STEP)
    c2 = 1.0 / (1.0 - ADAM_B2 ** ADAM_STEP)

    def body(p_ref, w_ref, m_ref, v_ref, g_out, d_out, m_out, v_out):
        g = p_ref[0].astype(F32)
        for d in range(1, N_DEV):
            g = g + p_ref[d].astype(F32)
        mn = ADAM_B1 * m_ref[...] + (1.0 - ADAM_B1) * g
        vn = ADAM_B2 * v_ref[...] + (1.0 - ADAM_B2) * (g * g)
        g_out[...] = g
        m_out[...] = mn
        v_out[...] = vn
        d_out[...] = -ADAM_LR * ((mn * c1) / (jnp.sqrt(vn * c2) + ADAM_EPS) + ADAM_WD * w_ref[...])

    spec = pl.BlockSpec((tr, cols), lambda i: (i, 0))
    return _pcall(
        body,
        name=name,
        grid=(rows // tr,),
        in_specs=[pl.BlockSpec((N_DEV, tr, cols), lambda i: (0, i, 0)), spec, spec, spec],
        out_specs=[spec] * 4,
        out_shape=[jax.ShapeDtypeStruct((rows, cols), F32)] * 4,
        compiler_params=_cparams(1),
    )(parts, w, m, v)


_SHARDED = ("w_in", "w_up_dil", "w_up_sb", "w_out", "w_mlp_in", "w_mlp_out")
_FULL_SHAPES = {"w_in": (D_MODEL, IN_COLS), "w_up_dil": (DIL_OUT_WIDTH, D_MODEL), "w_up_sb": (SB_WIDTH, D_MODEL),
                "w_out": (D_MODEL, D_MODEL), "w_mlp_in": (D_MODEL, D_FF), "w_mlp_out": (D_FF, D_MODEL)}
_ROW_SHARDED = ("w_out", "w_mlp_out")


def _shard_shape(name):
    r, c = _FULL_SHAPES[name]
    return (r // N_DEV, c) if name in _ROW_SHARDED else (r, c // N_DEV)


def _assemble(name, gathered):
    r, c = _shard_shape(name)
    if name in _ROW_SHARDED:
        return gathered.reshape(N_DEV * r, c)
    return gathered.transpose(1, 0, 2).reshape(r, N_DEV * c)


def _chunk(name, full):
    r, c = _shard_shape(name)
    if name in _ROW_SHARDED:
        return full.reshape(N_DEV, r, c)
    return full.reshape(r, N_DEV, c).transpose(1, 0, 2)


_SMALL = (("norm_mix_g", D_MODEL), ("b_gate", 2 * D_MODEL), ("norm_mlp_g", D_MODEL), ("norm_final_g", D_MODEL))
_SMALL_N = sum(n for _, n in _SMALL) + LANES


def _pack_small(vals, tail):
    return jnp.concatenate([vals[n].reshape(1, -1) for n, _ in _SMALL] + [tail], axis=1)


def _unpack_small(vec, shapes):
    out, pos = {}, 0
    for n, width in _SMALL:
        out[n] = vec[:, pos:pos + width].reshape(shapes[n])
        pos += width
    return out, vec[:, pos:]


def kernel(x, norm_mix_g, w_in, b_gate, w_up_dil, w_up_sb, w_out, norm_mlp_g, w_mlp_in, w_mlp_out, norm_final_g, loss_target, m_norm_mix_g, m_w_in, m_b_gate, m_w_up_dil, m_w_up_sb, m_w_out, m_norm_mlp_g, m_w_mlp_in, m_w_mlp_out, m_norm_final_g, v_norm_mix_g, v_w_in, v_b_gate, v_w_up_dil, v_w_up_sb, v_w_out, v_norm_mlp_g, v_w_mlp_in, v_w_mlp_out, v_norm_final_g):
    given = dict(locals())
    s = x.shape[1]
    x0 = x.reshape(s, D_MODEL)
    target = loss_target.reshape(s, D_MODEL)
    g1 = norm_mix_g.reshape(1, D_MODEL)
    g2 = norm_mlp_g.reshape(1, D_MODEL)
    g3 = norm_final_g.reshape(1, D_MODEL)
    bg = b_gate.reshape(1, 2 * D_MODEL)
    w_shards = {n: given[n].reshape(_shard_shape(n)) for n in _SHARDED}
    m_shards = {n: given["m_" + n].reshape(_shard_shape(n)) for n in _SHARDED}
    v_shards = {n: given["v_" + n].reshape(_shard_shape(n)) for n in _SHARDED}

    late = [n for n in _SHARDED if n != "w_in"]
    (gathered_w_in,) = _all_gather([w_shards["w_in"].astype(BF16)])
    w_in_f = _assemble("w_in", gathered_w_in)
    w_qkv, w_gl = w_in_f[:, :QKV_COLS], w_in_f[:, QKV_COLS:]
    late_shards = [w_shards[n].astype(BF16) for n in late]
    g_send, g_recv, g_srcs, g_lands, g_token = _spread_start("gather_rest_start", late_shards, chunked=False)

    def norm1(_, rows, consts):
        _, xh = _rms_stats(rows[0])
        return [xh * consts[0]], []

    (h1,) = _rowk("norm_mix", tm=512, rows=[x0], consts=[g1 + g_token[0, 0]], row_outs=[(D_MODEL, BF16)],
                  epilogue=norm1)
    qkv = _mm("proj_qkv", h1, w_qkv, out_dtype=BF16, tm=1024, tn=768, tk=D_MODEL)
    gl = _mm("proj_gates", h1, w_gl, out_dtype=F32, tm=512, tn=2048, tk=D_MODEL)
    dil = [_dil_fwd(qkv, g) for g in range(len(DIL_GROUPS))]
    os_, lses = [d[0] for d in dil], [d[1] for d in dil]
    o_a = _dil_mix_fwd(os_, lses, 512)
    o_b, tot_b = _sb_fwd(qkv)
    g_srcs, g_lands = _spread_wait("gather_rest_wait", g_send, g_recv, g_srcs, g_lands, tot_b, chunked=False)
    full = {n: _assemble(n, _with_own(l, own)) for n, l, own in zip(late, g_lands, g_srcs)}
    x1, h2 = _mixer_fwd(o_a, o_b, gl, x0, bg, g2, full["w_up_dil"], full["w_up_sb"], full["w_out"], 256)
    f = _mm("mlp_in", h2, full["w_mlp_in"], out_dtype=BF16, tm=1024, tn=1024, tk=D_MODEL,
            epilogue=lambda r, _: jnp.square(jnp.maximum(r, 0.0)))

    def head(acc, rows, consts):
        x1v, tv = rows
        g3v = consts[0]
        x2 = x1v + acc
        r, xh = _rms_stats(x2)
        diff = xh * g3v - tv
        loss = (0.5 / D_MODEL) * jnp.sum(jnp.sum(diff * diff, axis=0, keepdims=True), axis=1, keepdims=True)
        dy = diff * (1.0 / D_MODEL)
        dx2, dg = _rms_bwd(dy, xh, r, g3v)
        return [dx2, dx2], [dg, jnp.broadcast_to(loss, (1, LANES))]

    dx2, dx2b, gg3, loss_part = _rowk(
        "mlp_out_loss", a=f, w=full["w_mlp_out"], tm=512, tk=D_FF, rows=[x1, target], consts=[g3],
        row_outs=[(D_MODEL, F32), (D_MODEL, BF16)], acc_outs=[D_MODEL, LANES], epilogue=head)

    da = _mm("mlp_out_bwd", dx2b, full["w_mlp_out"], tb=True, out_dtype=BF16, tm=1024, tn=1024, tk=D_MODEL, extra=f,
             epilogue=lambda r, fv: r * (2.0 * jnp.sqrt(fv.astype(F32))))
    g_w_mlp_out = _mm("grad_w_mlp_out", f, dx2b, ta=True, out_dtype=F32, tm=1024, tn=1024, tk=2048)
    g_w_mlp_in = _mm("grad_w_mlp_in", h2, da, ta=True, out_dtype=F32, tm=1024, tn=1024, tk=2048)

    def norm_bwd(acc, rows, consts):
        xv, dres = rows
        r, xh = _rms_stats(xv)
        dx, dg = _rms_bwd(acc, xh, r, consts[0])
        return [dres + dx], [dg]

    dx1, gg2 = _rowk("mlp_in_bwd", a=da, w=full["w_mlp_in"], nt=True, tm=512, tk=D_FF, rows=[x1, dx2], consts=[g2],
                     row_outs=[(D_MODEL, F32)], acc_outs=[D_MODEL], epilogue=norm_bwd)
    do_a, do_b, dgl, g_w_out, g_w_ud, g_w_us, g_bg = _mixer_bwd(
        dx1, o_a, o_b, gl, bg, full["w_up_dil"], full["w_up_sb"], full["w_out"], 256)
    g_late = {"w_up_dil": g_w_ud, "w_up_sb": g_w_us, "w_out": g_w_out, "w_mlp_in": g_w_mlp_in, "w_mlp_out": g_w_mlp_out}
    late_chunks = [_chunk(n, g_late[n]).astype(BF16) for n in late]
    e_send, e_recv, e_srcs, e_lands, e_token = _spread_start("exchange_rest_start", late_chunks, chunked=True)
    do_a = do_a + e_token[0, 0].astype(do_a.dtype)
    mix = _dil_mix_bwd(do_a, os_, lses, 512)
    dil_b = [_dil_bwd(qkv, mix[g], lses[g], mix[3 + g], g) for g in range(len(DIL_GROUPS))]
    dq_b, dk_b, dv_b = _sb_bwd(qkv, do_b, tot_b)
    dproj = jnp.concatenate(
        [d[0] for d in dil_b] + [d[1] for d in dil_b] + [d[2] for d in dil_b]
        + [dq_b, dk_b.astype(BF16), dv_b.astype(BF16), dgl], axis=1)
    g_w_in = _mm("grad_w_in", h1, dproj, ta=True, out_dtype=F32, tm=512, tn=IN_COLS // 2, tk=1024)
    grad_x, gg1 = _rowk("in_proj_bwd", a=dproj, w=w_in_f, nt=True, tm=512, tk=IN_COLS, rows=[x0, dx1],
                        consts=[g1], row_outs=[(D_MODEL, F32)], acc_outs=[D_MODEL], epilogue=norm_bwd)

    small_part = _pack_small({"norm_mix_g": gg1, "b_gate": g_bg, "norm_mlp_g": gg2, "norm_final_g": gg3}, loss_part)
    w_in_parts, small_parts = _exchange([_chunk("w_in", g_w_in).astype(BF16),
                                         jnp.broadcast_to(small_part[None], (N_DEV, 1, _SMALL_N))])
    e_srcs, e_lands = _spread_wait("exchange_rest_wait", e_send, e_recv, e_srcs, e_lands, grad_x, chunked=True)
    me = 4 * lax.axis_index("x") + 2 * lax.axis_index("y") + lax.axis_index("c")
    parts = {"w_in": w_in_parts}
    for n, land, chunk in zip(late, e_lands, e_srcs):
        parts[n] = _with_own(land, lax.dynamic_index_in_dim(chunk, me, 0, keepdims=False))

    tags = ("grad_", "delta_", "new_m_", "new_v_")
    outs = {}
    for n, p in parts.items():
        res = _reduce_adamw("adamw_" + n, p, w_shards[n], m_shards[n], v_shards[n], 128)
        for tag, val in zip(tags, res):
            outs[tag + n] = val.reshape(given[n].shape)
    small_w = _pack_small(given, jnp.zeros((1, LANES), F32))
    small_m = _pack_small({n: given["m_" + n] for n, _ in _SMALL}, jnp.zeros((1, LANES), F32))
    small_v = _pack_small({n: given["v_" + n] for n, _ in _SMALL}, jnp.ones((1, LANES), F32))
    small_res = _reduce_adamw("adamw_replicated", small_parts, small_w, small_m, small_v, 8)

    small_shapes = {n: given[n].shape for n, _ in _SMALL}
    for tag, small in zip(tags, small_res):
        small_vals, tail = _unpack_small(small, small_shapes)
        for n, val in small_vals.items():
            outs[tag + n] = val
        if tag == "grad_":
            loss = tail[0, 0]
    names = ["norm_mix_g", "w_in", "b_gate", "w_up_dil", "w_up_sb", "w_out", "norm_mlp_g", "w_mlp_in", "w_mlp_out",
             "norm_final_g"]
    return (loss, grad_x.reshape(x.shape), *[outs["grad_" + n] for n in names], *[outs["delta_" + n] for n in names],
            *[outs["new_m_" + n] for n in names], *[outs["new_v_" + n] for n in names])
```

```python
import functools
import math

import jax
import jax.numpy as jnp
from jax import lax
from jax.experimental import pallas as pl
from jax.experimental.pallas import tpu as pltpu

_pcall = pl.pallas_call

F32 = jnp.float32
BF16 = jnp.bfloat16

D_MODEL = 1024
HEAD_DIM = 64
DIL_GROUPS = ((128, 1), (512, 4), (2048, 16))
DIL_HEADS_PER_GROUP = 4
N_DIL_HEADS = 12
N_SB_HEADS = 8
DIL_WIDTH = 768
DIL_OUT_WIDTH = 256
SB_WIDTH = 512
D_FF = 4096
BLOCK = 128
RMS_EPS = 1e-6
NEG_INF = -1e30
QKV_COLS = 3 * DIL_WIDTH + 3 * SB_WIDTH
IN_COLS = QKV_COLS + 2 * D_MODEL
N_DEV = 8

ADAM_LR = 0.001
ADAM_B1 = 0.9
ADAM_B2 = 0.999
ADAM_EPS = 1e-08
ADAM_WD = 0.01
ADAM_STEP = 10

VMEM_LIMIT = 56 * 1024 * 1024
SB_TK = 256
SB_TQ_FWD = 512
SB_TQ_BWD = 256
LANES = 128

_ARB = pltpu.ARBITRARY


def _cparams(n_axes, **kw):
    return pltpu.CompilerParams(dimension_semantics=(_ARB,) * n_axes, vmem_limit_bytes=VMEM_LIMIT, **kw)


def _dot(a, b):
    return jnp.dot(a, b, preferred_element_type=F32)


def _dot_nt(a, b):
    return lax.dot_general(a, b, (((1,), (1,)), ((), ())), preferred_element_type=F32)


def _dot_tn(a, b):
    return lax.dot_general(a, b, (((0,), (0,)), ((), ())), preferred_element_type=F32)


def _split_hi_lo(x):
    hi = x.astype(BF16)
    lo = (x - hi.astype(F32)).astype(BF16)
    return hi, lo


def _dot_hi_lo(x, m):
    hi, lo = _split_hi_lo(x)
    return _dot(hi, m) + _dot(lo, m)


def _sigmoid(x):
    return 1.0 / (1.0 + jnp.exp(-x))


def _mm(name, a, b, *, ta=False, tb=False, out_dtype, tm, tn, tk, epilogue=None, extra=None):
    m = a.shape[1] if ta else a.shape[0]
    k = a.shape[0] if ta else a.shape[1]
    n = b.shape[0] if tb else b.shape[1]
    assert (b.shape[1] if tb else b.shape[0]) == k
    tm, tn, tk = min(tm, m), min(tn, n), min(tk, k)
    assert m % tm == 0 and n % tn == 0 and k % tk == 0, (name, m, n, k, tm, tn, tk)
    nk = k // tk
    dn = (((0 if ta else 1,), (1 if tb else 0,)), ((), ()))
    in_place = nk > 1 and epilogue is None and out_dtype == F32

    def body(*refs):
        if extra is not None:
            a_ref, b_ref, e_ref, o_ref = refs[:4]
        else:
            a_ref, b_ref, o_ref = refs[:3]
            e_ref = None

        def finish(r):
            if epilogue is not None:
                r = epilogue(r, None if e_ref is None else e_ref[...])
            o_ref[...] = r.astype(out_dtype)

        part = lax.dot_general(a_ref[...].astype(BF16), b_ref[...].astype(BF16), dn, preferred_element_type=F32)
        if nk == 1:
            finish(part)
        else:
            acc_ref = o_ref if in_place else refs[-1]
            kk = pl.program_id(2)

            @pl.when(kk == 0)
            def _():
                acc_ref[...] = part

            @pl.when(kk > 0)
            def _():
                acc_ref[...] += part

            if not in_place:

                @pl.when(kk == nk - 1)
                def _():
                    finish(acc_ref[...])

    a_spec = pl.BlockSpec((tk, tm), lambda j, i, kk: (kk, i)) if ta else pl.BlockSpec((tm, tk), lambda j, i, kk: (i, kk))
    b_spec = pl.BlockSpec((tn, tk), lambda j, i, kk: (j, kk)) if tb else pl.BlockSpec((tk, tn), lambda j, i, kk: (kk, j))
    o_spec = pl.BlockSpec((tm, tn), lambda j, i, kk: (i, j))
    in_specs = [a_spec, b_spec]
    args = [a, b]
    if extra is not None:
        in_specs.append(o_spec)
        args.append(extra)
    return _pcall(
        body,
        name=name,
        grid=(n // tn, m // tm, nk),
        in_specs=in_specs,
        out_specs=o_spec,
        out_shape=jax.ShapeDtypeStruct((m, n), out_dtype),
        scratch_shapes=[pltpu.VMEM((tm, tn), F32)] if (nk > 1 and not in_place) else [],
        compiler_params=_cparams(3),
    )(*args)


def _rowk(name, *, a=None, w=None, nt=False, tm, tk=None, rows=(), consts=(), row_outs=(), acc_outs=(), epilogue):
    has_mm = a is not None
    m = a.shape[0] if has_mm else rows[0].shape[0]
    assert m % tm == 0
    nm = m // tm
    if has_mm:
        k = a.shape[1]
        n = w.shape[0] if nt else w.shape[1]
        tk = min(tk, k)
        assert k % tk == 0
        nk = k // tk
    else:
        nk = 1
    n_rows, n_consts, n_ro, n_ao = len(rows), len(consts), len(row_outs), len(acc_outs)

    def body(*refs):
        pos = 0
        if has_mm:
            a_ref, w_ref = refs[0], refs[1]
            pos = 2
        row_refs = refs[pos:pos + n_rows]
        pos += n_rows
        const_refs = refs[pos:pos + n_consts]
        pos += n_consts
        ro_refs = refs[pos:pos + n_ro]
        pos += n_ro
        ao_refs = refs[pos:pos + n_ao]
        pos += n_ao
        i = pl.program_id(0)
        kk = pl.program_id(1)

        def finish(acc):
            ro_vals, ao_vals = epilogue(acc, [r[...] for r in row_refs], [c[...] for c in const_refs])
            for r, v in zip(ro_refs, ro_vals):
                r[...] = v.astype(r.dtype)
            for r, v in zip(ao_refs, ao_vals):

                @pl.when(i == 0)
                def _(r=r, v=v):
                    r[...] = v

                @pl.when(i > 0)
                def _(r=r, v=v):
                    r[...] += v

        if not has_mm:
            finish(None)
            return
        if nt:
            part = _dot_nt(a_ref[...].astype(BF16), w_ref[...])
        else:
            part = _dot(a_ref[...].astype(BF16), w_ref[...])
        if nk == 1:
            finish(part)
        else:
            acc_ref = refs[pos]

            @pl.when(kk == 0)
            def _():
                acc_ref[...] = part

            @pl.when(kk > 0)
            def _():
                acc_ref[...] += part

            @pl.when(kk == nk - 1)
            def _():
                finish(acc_ref[...])

    once = pl.Buffered(1)
    in_specs, args = [], []
    if has_mm:
        in_specs.append(pl.BlockSpec((tm, tk), lambda i, kk: (i, kk)))
        w_mode = once if nk == 1 else None
        in_specs.append(pl.BlockSpec((n, tk), lambda i, kk: (0, kk), pipeline_mode=w_mode) if nt
                        else pl.BlockSpec((tk, n), lambda i, kk: (kk, 0), pipeline_mode=w_mode))
        args += [a, w]
    for r in rows:
        in_specs.append(pl.BlockSpec((tm, r.shape[1]), lambda i, kk: (i, 0)))
        args.append(r)
    for c in consts:
        in_specs.append(pl.BlockSpec(c.shape, lambda i, kk: (0,) * c.ndim, pipeline_mode=once))
        args.append(c)
    out_specs, out_shape = [], []
    for width, dt in row_outs:
        out_specs.append(pl.BlockSpec((tm, width), lambda i, kk: (i, 0)))
        out_shape.append(jax.ShapeDtypeStruct((m, width), dt))
    for width in acc_outs:
        out_specs.append(pl.BlockSpec((1, width), lambda i, kk: (0, 0)))
        out_shape.append(jax.ShapeDtypeStruct((1, width), F32))
    return _pcall(
        body,
        name=name,
        grid=(nm, nk),
        in_specs=in_specs,
        out_specs=out_specs,
        out_shape=out_shape,
        scratch_shapes=[pltpu.VMEM((tm, n), F32)] if (has_mm and nk > 1) else [],
        compiler_params=_cparams(2),
    )(*args)


def _rms_stats(x):
    r = lax.rsqrt(jnp.mean(x * x, axis=-1, keepdims=True) + RMS_EPS)
    return r, x * r


def _rms_bwd(dh, xh, r, g):
    gy = dh * g
    dx = r * (gy - xh * jnp.mean(gy * xh, axis=-1, keepdims=True))
    return dx, jnp.sum(dh * xh, axis=0, keepdims=True)


def _alibi_slope(head):
    return 2.0 ** (-8.0 * (head + 1) / N_DIL_HEADS)


def _dil_masks(i):
    qi = lax.broadcasted_iota(jnp.int32, (BLOCK, 2 * BLOCK), 0)
    kj = lax.broadcasted_iota(jnp.int32, (BLOCK, 2 * BLOCK), 1)
    steps = qi + BLOCK - kj
    valid = (steps >= 0) & (steps <= BLOCK) & ((kj >= BLOCK) | (i > 0))
    return steps.astype(F32), valid


def _dil_view(qkv, group):
    _, dilation = DIL_GROUPS[group]
    if dilation == 1:
        return qkv, QKV_COLS // DIL_OUT_WIDTH, (group, 3 + group, 6 + group)
    w = DIL_OUT_WIDTH
    own = jnp.concatenate([qkv[:, (3 * part + group) * w:(3 * part + group + 1) * w] for part in range(3)], axis=1)
    return own.reshape(qkv.shape[0] // dilation, dilation * 3 * w), 3, (0, 1, 2)


def _dil_specs(ncb, cols, clamp):
    def cur(col):
        return pl.BlockSpec((BLOCK, DIL_OUT_WIDTH), lambda r, i: (clamp(i), r * ncb + col))

    def prev(col):
        return pl.BlockSpec((BLOCK, DIL_OUT_WIDTH), lambda r, i: (jnp.maximum(clamp(i) - 1, 0), r * ncb + col))

    return [cur(cols[0]), cur(cols[1]), prev(cols[1]), cur(cols[2]), prev(cols[2])]


def _dil_fwd(qkv, group):
    window, dilation = DIL_GROUPS[group]
    s = qkv.shape[0]
    sub = s // dilation
    nb = sub // BLOCK
    assert nb * BLOCK * dilation == s and window // dilation == BLOCK
    slopes = [_alibi_slope(group * DIL_HEADS_PER_GROUP + h) * dilation for h in range(DIL_HEADS_PER_GROUP)]

    def body(q_ref, kc_ref, kp_ref, vc_ref, vp_ref, o_ref, lse_ref):
        i = pl.program_id(1)
        q = q_ref[...]
        kk = jnp.concatenate([kp_ref[...], kc_ref[...]], axis=0)
        vv = jnp.concatenate([vp_ref[...], vc_ref[...]], axis=0)
        head_id = lax.broadcasted_iota(jnp.int32, (1, DIL_OUT_WIDTH), 1) // HEAD_DIM
        steps, valid = _dil_masks(i)
        heads = range(DIL_HEADS_PER_GROUP)
        scores = [_dot_nt(jnp.where(head_id == h, q, jnp.zeros_like(q)), kk) for h in heads]
        ps, lses = [], []
        for h in heads:
            logits = scores[h] * (1.0 / math.sqrt(HEAD_DIM)) - slopes[h] * steps
            logits = jnp.where(valid, logits, NEG_INF)
            mx = jnp.max(logits, axis=1, keepdims=True)
            e = jnp.exp(logits - mx)
            den = jnp.sum(e, axis=1, keepdims=True)
            lses.append(mx + jnp.log(den))
            ps.append((e * (1.0 / den)).astype(BF16))
        outs = [_dot(ps[h], vv) for h in heads]
        o, lse_all = outs[0], lses[0]
        for h in heads[1:]:
            o = jnp.where(head_id == h, outs[h], o)
            lse_all = jnp.where(head_id == h, lses[h], lse_all)
        o_ref[...] = o
        lse_ref[...] = jnp.broadcast_to(lse_all, o.shape)

    qkv_v, ncb, cols = _dil_view(qkv, group)
    out_spec = pl.BlockSpec((BLOCK, DIL_OUT_WIDTH), lambda r, i: (i, r))
    o, lse = _pcall(
        body,
        name=f"dil_fwd_g{group}",
        grid=(dilation, nb),
        in_specs=_dil_specs(ncb, cols, lambda i: i),
        out_specs=[out_spec, out_spec],
        out_shape=[jax.ShapeDtypeStruct((sub, dilation * DIL_OUT_WIDTH), F32)] * 2,
        compiler_params=_cparams(2),
    )(qkv_v, qkv_v, qkv_v, qkv_v, qkv_v)
    return o.reshape(s, DIL_OUT_WIDTH), lse.reshape(s, DIL_OUT_WIDTH)


def _dil_bwd(qkv, do_g, lse_g, dterm_g, group):
    window, dilation = DIL_GROUPS[group]
    s = qkv.shape[0]
    sub = s // dilation
    nb = sub // BLOCK
    slopes = [_alibi_slope(group * DIL_HEADS_PER_GROUP + h) * dilation for h in range(DIL_HEADS_PER_GROUP)]
    scale = 1.0 / math.sqrt(HEAD_DIM)

    def body(q_ref, kc_ref, kp_ref, vc_ref, vp_ref, do_ref, lse_ref, dt_ref, dq_ref, dk_ref, dv_ref, ck_ref, cv_ref):
        i = pl.program_id(1)

        @pl.when(i == 0)
        def _():
            ck_ref[...] = jnp.zeros_like(ck_ref)
            cv_ref[...] = jnp.zeros_like(cv_ref)

        @pl.when(i < nb)
        def _():
            q = q_ref[...]
            do = do_ref[...]
            lse_all = lse_ref[...]
            dt_all = dt_ref[...]
            kk = jnp.concatenate([kp_ref[...], kc_ref[...]], axis=0)
            vv = jnp.concatenate([vp_ref[...], vc_ref[...]], axis=0)
            lane = lax.broadcasted_iota(jnp.int32, (1, DIL_OUT_WIDTH), 1)
            head_id = lane // HEAD_DIM
            steps, valid = _dil_masks(i)
            heads = range(DIL_HEADS_PER_GROUP)
            qms = [jnp.where(head_id == h, q, jnp.zeros_like(q)) for h in heads]
            doms = [jnp.where(head_id == h, do, jnp.zeros_like(do)) for h in heads]
            scores = [_dot_nt(qms[h], kk) for h in heads]
            dps = [_dot_nt(doms[h], vv) for h in heads]
            pbs, dss = [], []
            for h in heads:
                first = lane == h * HEAD_DIM
                lse = jnp.sum(jnp.where(first, lse_all, 0.0), axis=1, keepdims=True)
                dt = jnp.sum(jnp.where(first, dt_all, 0.0), axis=1, keepdims=True)
                logits = scores[h] * scale - slopes[h] * steps
                p = jnp.where(valid, jnp.exp(jnp.where(valid, logits, NEG_INF) - lse), 0.0)
                pbs.append(p.astype(BF16))
                dss.append((p * (dps[h] + dt) * scale).astype(BF16))
            dqs = [_dot(dss[h], kk) for h in heads]
            dks = [_dot_tn(dss[h], qms[h]) for h in heads]
            dvs = [_dot_tn(pbs[h], doms[h]) for h in heads]
            dq = dqs[0]
            for h in heads[1:]:
                dq = jnp.where(head_id == h, dqs[h], dq)
            dkk = (dks[0] + dks[1]) + (dks[2] + dks[3])
            dvv = (dvs[0] + dvs[1]) + (dvs[2] + dvs[3])
            dq_ref[...] = dq.astype(dq_ref.dtype)
            dk_ref[...] = (ck_ref[...] + dkk[:BLOCK]).astype(dk_ref.dtype)
            dv_ref[...] = (cv_ref[...] + dvv[:BLOCK]).astype(dv_ref.dtype)
            ck_ref[...] = dkk[BLOCK:]
            cv_ref[...] = dvv[BLOCK:]

        @pl.when(i == nb)
        def _():
            dk_ref[...] = ck_ref[...].astype(dk_ref.dtype)
            dv_ref[...] = cv_ref[...].astype(dv_ref.dtype)

    clamp = lambda i: jnp.minimum(i, nb - 1)
    qkv_v, ncb, cols = _dil_view(qkv, group)
    view = lambda t: t.reshape(sub, dilation * DIL_OUT_WIDTH)
    row_spec = pl.BlockSpec((BLOCK, DIL_OUT_WIDTH), lambda r, i: (clamp(i), r))
    late_spec = pl.BlockSpec((BLOCK, DIL_OUT_WIDTH), lambda r, i: (jnp.maximum(i - 1, 0), r))
    dq, dk, dv = _pcall(
        body,
        name=f"dil_bwd_g{group}",
        grid=(dilation, nb + 1),
        in_specs=_dil_specs(ncb, cols, clamp) + [row_spec, row_spec, row_spec],
        out_specs=[row_spec, late_spec, late_spec],
        out_shape=[jax.ShapeDtypeStruct((sub, dilation * DIL_OUT_WIDTH), BF16)] * 3,
        scratch_shapes=[pltpu.VMEM((BLOCK, DIL_OUT_WIDTH), F32)] * 2,
        compiler_params=_cparams(2),
    )(qkv_v, qkv_v, qkv_v, qkv_v, qkv_v, view(do_g), view(lse_g), view(dterm_g))
    return dq.reshape(s, DIL_OUT_WIDTH), dk.reshape(s, DIL_OUT_WIDTH), dv.reshape(s, DIL_OUT_WIDTH)


def _head_block_ones():
    r = lax.broadcasted_iota(jnp.int32, (DIL_OUT_WIDTH, DIL_OUT_WIDTH), 0) // HEAD_DIM
    c = lax.broadcasted_iota(jnp.int32, (DIL_OUT_WIDTH, DIL_OUT_WIDTH), 1) // HEAD_DIM
    return jnp.where(r == c, 1.0, 0.0).astype(BF16)


def _dil_mix_weights(l0, l1, l2):
    mx = jnp.maximum(jnp.maximum(l0, l1), l2)
    e0, e1, e2 = jnp.exp(l0 - mx), jnp.exp(l1 - mx), jnp.exp(l2 - mx)
    inv = 1.0 / (e0 + e1 + e2)
    return e0 * inv, e1 * inv, e2 * inv


def _dil_mix_fwd(os_, lses, tm):
    def epi(_, rows, consts):
        o0, o1, o2, l0, l1, l2 = rows
        w0, w1, w2 = _dil_mix_weights(l0, l1, l2)
        return [w0 * o0 + w1 * o1 + w2 * o2], []

    (o_a,) = _rowk("dil_mix_fwd", tm=tm, rows=list(os_) + list(lses), row_outs=[(DIL_OUT_WIDTH, BF16)], epilogue=epi)
    return o_a


def _dil_mix_bwd(do_a, os_, lses, tm):
    def epi(_, rows, consts):
        do, o0, o1, o2, l0, l1, l2 = rows
        do = do.astype(F32)
        w0, w1, w2 = _dil_mix_weights(l0, l1, l2)
        mixed = w0 * o0 + w1 * o1 + w2 * o2
        tot = _dot_hi_lo(do * mixed, _head_block_ones())
        return [w0 * do, w1 * do, w2 * do, -w0 * tot, -w1 * tot, -w2 * tot], []

    return _rowk(
        "dil_mix_bwd", tm=tm, rows=[do_a] + list(os_) + list(lses),
        row_outs=[(DIL_OUT_WIDTH, BF16)] * 3 + [(DIL_OUT_WIDTH, F32)] * 3, epilogue=epi)


_SB_Q0 = 3 * DIL_WIDTH // LANES
_SB_K0 = _SB_Q0 + SB_WIDTH // LANES
_SB_V0 = _SB_K0 + SB_WIDTH // LANES


_EXP_CLAMP = 88.0
_SB_DEAD = 104.0


def _tri(t, op):
    r = lax.broadcasted_iota(jnp.int32, (t, t), 0)
    c = lax.broadcasted_iota(jnp.int32, (t, t), 1)
    return jnp.where(op(r, c), 1.0, 0.0).astype(BF16)


def _softplus(z):
    return jnp.maximum(z, jnp.log(1.0 + jnp.exp(jnp.minimum(z, _EXP_CLAMP))))


def _sb_fwd(qkv):
    s = qkv.shape[0]
    t, tq = SB_TK, min(SB_TQ_FWD, s)
    assert tq in (t, 2 * t) and s % (2 * t) == 0
    nq = s // tq
    n_pairs = SB_WIDTH // LANES

    def body(q_ref, k_ref, v_ref, o_ref, tot_ref, steps_ref):
        p, i = pl.program_id(0), pl.program_id(1)
        q = q_ref[...] * (1.0 / math.sqrt(HEAD_DIM))
        lane_hi = lax.broadcasted_iota(jnp.int32, (1, LANES), 1) // HEAD_DIM
        later = _tri(t, lambda r, c: r > c)
        row = lax.broadcasted_iota(jnp.int32, (tq, t), 0)
        col = lax.broadcasted_iota(jnp.int32, (tq, t), 1)
        qms = [jnp.where(lane_hi == hh, q, jnp.zeros_like(q)) for hh in range(2)]

        def step(jj, carry, diag):
            tiles = (2 * jj + 1, 2 * jj)
            offs = [pl.multiple_of(j * t, t) for j in tiles]
            ks = [k_ref[pl.ds(off, t), :] for off in offs]
            vs = [v_ref[pl.ds(off, t), :] for off in offs]
            masks = [(j * t + col) < (i * tq + row) for j in tiles] if diag else None
            chains = [(n, hh) for n in range(2) for hh in range(2)]
            zs = [_dot_nt(qms[hh], ks[n]) for n, hh in chains]
            sps, lposs = [], []
            for (n, hh), z in zip(chains, zs):
                sp = _softplus(z)
                lposs.append(z - sp)
                sps.append(jnp.where(masks[n], sp, 0.0) if diag else sp)
            sufs = [_dot(sp.astype(BF16), later) for sp in sps]
            cs = [carry[0], carry[2]]
            accs = [carry[1], carry[3]]
            for idx, (n, hh) in enumerate(chains):
                a = jnp.exp(lposs[idx] - sufs[idx] - cs[hh])
                if diag:
                    a = jnp.where(masks[n], a, 0.0)
                accs[hh] = accs[hh] + _dot(a.astype(BF16), vs[n])
                cs[hh] = cs[hh] + jnp.sum(sps[idx], axis=1, keepdims=True)
            return cs[0], accs[0], cs[1], accs[1]

        zc, za = jnp.zeros((tq, 1), F32), jnp.zeros((tq, LANES), F32)
        last = (i * tq) // (2 * t)
        carry = step(last, (zc, za, zc, za), True)

        def alive(state):
            n, ca = state
            return jnp.logical_and(n < last, jnp.minimum(jnp.min(ca[0]), jnp.min(ca[2])) <= _SB_DEAD)

        n_off, carry = lax.while_loop(alive, lambda st: (st[0] + 1, step(last - 1 - st[0], st[1], False)),
                                      (jnp.int32(0), carry))
        out = jnp.where(lane_hi == 0, carry[1], carry[3])
        tot = jnp.where(lane_hi == 0, carry[0], carry[2])
        o_ref[...] = out.astype(o_ref.dtype)
        tot_ref[...] = tot
        steps_ref[p, i] = n_off

    o, tot, n_steps = _pcall(
        body,
        name="sb_fwd",
        grid=(n_pairs, nq),
        in_specs=[
            pl.BlockSpec((tq, LANES), lambda p, i: (i, _SB_Q0 + p)),
            pl.BlockSpec((s, LANES), lambda p, i: (0, _SB_K0 + p)),
            pl.BlockSpec((s, LANES), lambda p, i: (0, _SB_V0 + p)),
        ],
        out_specs=[pl.BlockSpec((tq, LANES), lambda p, i: (i, p))] * 2 + [pl.BlockSpec(memory_space=pltpu.SMEM)],
        out_shape=[jax.ShapeDtypeStruct((s, SB_WIDTH), BF16), jax.ShapeDtypeStruct((s, SB_WIDTH), F32),
                   jax.ShapeDtypeStruct((n_pairs, nq), jnp.int32)],
        compiler_params=_cparams(2),
    )(qkv, qkv, qkv)
    return o, tot, n_steps


def _sb_bwd(qkv, do_b, tot_b, n_steps):
    s = qkv.shape[0]
    t, tq = SB_TK, min(SB_TQ_BWD, s)
    assert tq in (t, 2 * t) and s % (2 * t) == 0
    nq = s // tq
    assert nq % n_steps.shape[1] == 0
    n_pairs = SB_WIDTH // LANES
    scale = 1.0 / math.sqrt(HEAD_DIM)

    def body(steps_ref, q_ref, k_ref, v_ref, do_ref, tot_ref, dq_ref, dk_ref, dv_ref):
        p, i = pl.program_id(0), pl.program_id(1)

        @pl.when(i == 0)
        def _():
            dk_ref[...] = jnp.zeros_like(dk_ref)
            dv_ref[...] = jnp.zeros_like(dv_ref)

        q = q_ref[...] * scale
        do = do_ref[...]
        tot_all = tot_ref[...]
        lane = lax.broadcasted_iota(jnp.int32, (1, LANES), 1)
        lane_hi = lane // HEAD_DIM
        later = _tri(t, lambda r, c: r > c)
        before = _tri(t, lambda r, c: r < c)
        row = lax.broadcasted_iota(jnp.int32, (tq, t), 0)
        col = lax.broadcasted_iota(jnp.int32, (tq, t), 1)
        qms = [jnp.where(lane_hi == hh, q, jnp.zeros_like(q)) for hh in range(2)]
        doms = [jnp.where(lane_hi == hh, do, jnp.zeros_like(do)) for hh in range(2)]
        tots = [jnp.sum(jnp.where(lane == hh * HEAD_DIM, tot_all, 0.0), axis=1, keepdims=True) for hh in range(2)]

        def step(jj, carry, diag):
            tiles = (2 * jj, 2 * jj + 1)
            offs = [pl.multiple_of(j * t, t) for j in tiles]
            ks = [k_ref[pl.ds(off, t), :] for off in offs]
            vs = [v_ref[pl.ds(off, t), :] for off in offs]
            masks = [(j * t + col) < (i * tq + row) for j in tiles] if diag else None
            chains = [(n, hh) for n in range(2) for hh in range(2)]
            zs = [_dot_nt(qms[hh], ks[n]) for n, hh in chains]
            sps, sigs = [], []
            for (n, hh), z in zip(chains, zs):
                sp = _softplus(z)
                sigs.append(jnp.exp(z - sp))
                sps.append(jnp.where(masks[n], sp, 0.0) if diag else sp)
            sufs = [_dot(sp.astype(BF16), later) for sp in sps]
            das = [_dot_nt(doms[hh], vs[n]) for n, hh in chains]
            cls = [carry[0], carry[3]]
            cgs = [carry[1], carry[4]]
            accs = [carry[2], carry[5]]
            gs, abs_, cg_at = [], [], []
            for idx, (n, hh) in enumerate(chains):
                cls[hh] = cls[hh] + jnp.sum(sps[idx], axis=1, keepdims=True)
                a = sigs[idx] * jnp.exp(-sufs[idx] - (tots[hh] - cls[hh]))
                if diag:
                    a = jnp.where(masks[n], a, 0.0)
                g = a * das[idx]
                gs.append(g)
                abs_.append(a.astype(BF16))
                cg_at.append(cgs[hh])
                cgs[hh] = cgs[hh] + jnp.sum(g, axis=1, keepdims=True)
            prefs = [_dot(g.astype(BF16), before) for g in gs]
            dvs = [_dot_tn(abs_[idx], doms[hh]) for idx, (n, hh) in enumerate(chains)]
            dzs = []
            for idx, (n, hh) in enumerate(chains):
                g = gs[idx]
                dz = g - sigs[idx] * (g + prefs[idx] + cg_at[idx])
                if diag:
                    dz = jnp.where(masks[n], dz, 0.0)
                dzs.append(dz.astype(BF16))
            for idx, (n, hh) in enumerate(chains):
                accs[hh] = accs[hh] + _dot(dzs[idx], ks[n])
            dks = [_dot_tn(dzs[idx], qms[hh]) for idx, (n, hh) in enumerate(chains)]
            for n in range(2):
                dk_ref[pl.ds(offs[n], t), :] += dks[2 * n] + dks[2 * n + 1]
                dv_ref[pl.ds(offs[n], t), :] += dvs[2 * n] + dvs[2 * n + 1]
            return cls[0], cgs[0], accs[0], cls[1], cgs[1], accs[1]

        zc, za = jnp.zeros((tq, 1), F32), jnp.zeros((tq, LANES), F32)
        last = (i * tq) // (2 * t)
        first = last - steps_ref[p, (i * n_steps.shape[1]) // nq]
        carry = lax.fori_loop(first, last, lambda jj, ca: step(jj, ca, False), (zc, zc, za, zc, zc, za))
        carry = step(last, carry, True)
        dq = jnp.where(lane_hi == 0, carry[2], carry[5])
        dq_ref[...] = (dq * scale).astype(dq_ref.dtype)

    row_spec = pl.BlockSpec((tq, LANES), lambda p, i, ns: (i, p))
    full_spec = pl.BlockSpec((s, LANES), lambda p, i, ns: (0, p))
    return _pcall(
        body,
        name="sb_bwd",
        grid_spec=pltpu.PrefetchScalarGridSpec(
            num_scalar_prefetch=1,
            grid=(n_pairs, nq),
            in_specs=[
                pl.BlockSpec((tq, LANES), lambda p, i, ns: (i, _SB_Q0 + p)),
                pl.BlockSpec((s, LANES), lambda p, i, ns: (0, _SB_K0 + p)),
                pl.BlockSpec((s, LANES), lambda p, i, ns: (0, _SB_V0 + p)),
                row_spec, row_spec,
            ],
            out_specs=[row_spec, full_spec, full_spec],
        ),
        out_shape=[jax.ShapeDtypeStruct((s, SB_WIDTH), BF16), jax.ShapeDtypeStruct((s, SB_WIDTH), F32),
                   jax.ShapeDtypeStruct((s, SB_WIDTH), F32)],
        compiler_params=_cparams(2),
    )(n_steps, qkv, qkv, qkv, do_b, tot_b)


def _gates(gl, bg):
    return _sigmoid(gl[:, :D_MODEL] + bg[:, :D_MODEL]), _sigmoid(gl[:, D_MODEL:] + bg[:, D_MODEL:])


def _mixer_fwd(o_a, o_b, gl, x0, bg, g2, w_ud, w_us, w_out, tm):
    def epi(_, rows, consts):
        oa, ob, glv, x = rows
        bgv, g2v, wud, wus, wout = consts
        ga, gb = _gates(glv, bgv)
        merged = ga * _dot(oa, wud) + gb * _dot(ob, wus)
        x1 = x + _dot(merged.astype(BF16), wout)
        r, xh = _rms_stats(x1)
        return [x1, xh * g2v], []

    return _rowk("mixer_fwd", tm=tm, rows=[o_a, o_b, gl, x0], consts=[bg, g2, w_ud, w_us, w_out],
                 row_outs=[(D_MODEL, F32), (D_MODEL, BF16)], epilogue=epi)


def _mixer_bwd(dx1, o_a, o_b, gl, bg, w_ud, w_us, w_out, tm):
    s = dx1.shape[0]
    nm = s // tm

    def body(dx_ref, oa_ref, ob_ref, gl_ref, bg_ref, wud_ref, wus_ref, wout_ref,
             doa_ref, dob_ref, dgl_ref, gwout_ref, gwud_ref, gwus_ref, gbg_ref):
        i = pl.program_id(0)
        dxb = dx_ref[...].astype(BF16)
        oa, ob = oa_ref[...], ob_ref[...]
        ga, gb = _gates(gl_ref[...], bg_ref[...])
        ua, ub = _dot(oa, wud_ref[...]), _dot(ob, wus_ref[...])
        merged = (ga * ua + gb * ub).astype(BF16)
        dm = _dot_nt(dxb, wout_ref[...])
        dua = (dm * ga).astype(BF16)
        dub = (dm * gb).astype(BF16)
        dgla = dm * ua * ga * (1.0 - ga)
        dglb = dm * ub * gb * (1.0 - gb)
        doa_ref[...] = _dot_nt(dua, wud_ref[...]).astype(doa_ref.dtype)
        dob_ref[...] = _dot_nt(dub, wus_ref[...]).astype(dob_ref.dtype)
        dgl_ref[:, :D_MODEL] = dgla.astype(dgl_ref.dtype)
        dgl_ref[:, D_MODEL:] = dglb.astype(dgl_ref.dtype)
        parts = [(gwout_ref, _dot_tn(merged, dxb)), (gwud_ref, _dot_tn(oa, dua)), (gwus_ref, _dot_tn(ob, dub))]
        for r, v in parts:

            @pl.when(i == 0)
            def _(r=r, v=v):
                r[...] = v

            @pl.when(i > 0)
            def _(r=r, v=v):
                r[...] += v

        sa = jnp.sum(dgla, axis=0, keepdims=True)
        sb = jnp.sum(dglb, axis=0, keepdims=True)

        @pl.when(i == 0)
        def _():
            gbg_ref[:, :D_MODEL] = sa
            gbg_ref[:, D_MODEL:] = sb

        @pl.when(i > 0)
        def _():
            gbg_ref[:, :D_MODEL] += sa
            gbg_ref[:, D_MODEL:] += sb

    row = lambda w: pl.BlockSpec((tm, w), lambda i: (i, 0))
    full = lambda a: pl.BlockSpec(a.shape, lambda i: (0, 0))
    fshape = lambda r, c: jax.ShapeDtypeStruct((r, c), F32)
    return _pcall(
        body,
        name="mixer_bwd",
        grid=(nm,),
        in_specs=[row(D_MODEL), row(DIL_OUT_WIDTH), row(SB_WIDTH), row(2 * D_MODEL),
                  full(bg), full(w_ud), full(w_us), full(w_out)],
        out_specs=[row(DIL_OUT_WIDTH), row(SB_WIDTH), row(2 * D_MODEL),
                   pl.BlockSpec((D_MODEL, D_MODEL), lambda i: (0, 0)),
                   pl.BlockSpec((DIL_OUT_WIDTH, D_MODEL), lambda i: (0, 0)),
                   pl.BlockSpec((SB_WIDTH, D_MODEL), lambda i: (0, 0)),
                   pl.BlockSpec((1, 2 * D_MODEL), lambda i: (0, 0))],
        out_shape=[jax.ShapeDtypeStruct((s, DIL_OUT_WIDTH), BF16), jax.ShapeDtypeStruct((s, SB_WIDTH), BF16),
                   jax.ShapeDtypeStruct((s, 2 * D_MODEL), BF16),
                   fshape(D_MODEL, D_MODEL), fshape(DIL_OUT_WIDTH, D_MODEL), fshape(SB_WIDTH, D_MODEL),
                   fshape(1, 2 * D_MODEL)],
        compiler_params=_cparams(1),
    )(dx1, o_a, o_b, gl, bg, w_ud, w_us, w_out)


_HBM = pl.BlockSpec(memory_space=pltpu.HBM)
_MESH = pl.DeviceIdType.MESH


def _all_gather(shards):
    n = len(shards)

    def body(*refs):
        x_refs, out_refs = refs[:n], refs[n:2 * n]
        send_sems, recv_sems, local_sems = refs[2 * n:]
        x, y, c = lax.axis_index("x"), lax.axis_index("y"), lax.axis_index("c")
        me, sibling = (x, y, c), (x, y, 1 - c)
        chips = [(1 - x, y), (x, 1 - y), (1 - x, 1 - y)]

        def slot(a, px, py, pc):
            return out_refs[a].at[4 * px + 2 * py + pc]

        def copy(a, k, block, to, own=False):
            return pltpu.make_async_remote_copy(
                src_ref=x_refs[a] if own else slot(a, *block), dst_ref=slot(a, *block),
                send_sem=send_sems.at[7 * a + k], recv_sem=recv_sems.at[7 * a + k], device_id=to, device_id_type=_MESH)

        mine = [pltpu.make_async_copy(x_refs[a], slot(a, *me), local_sems.at[a]) for a in range(n)]
        for cp in mine:
            cp.start()
        first = []
        for a in range(n):
            first.append(copy(a, 0, me, sibling, own=True))
            first += [copy(a, 1 + j, me, (*chip, c), own=True) for j, chip in enumerate(chips)]
        for cp in first:
            cp.start()
        passed = []
        for a in range(n):
            for j, chip in enumerate(chips):
                copy(a, 1 + j, (*chip, c), me).wait_recv()
                passed.append(copy(a, 4 + j, (*chip, c), sibling))
                passed[-1].start()
        for a in range(n):
            copy(a, 0, sibling, me).wait_recv()
            for j, chip in enumerate(chips):
                copy(a, 4 + j, (*chip, 1 - c), me).wait_recv()
        for cp in first + passed:
            cp.wait_send()
        for cp in mine:
            cp.wait()

    return _pcall(
        body,
        name="all_gather_weights",
        in_specs=[_HBM] * n,
        out_specs=[_HBM] * n,
        out_shape=[jax.ShapeDtypeStruct((N_DEV,) + s.shape, s.dtype) for s in shards],
        scratch_shapes=[pltpu.SemaphoreType.DMA((7 * n,)), pltpu.SemaphoreType.DMA((7 * n,)),
                        pltpu.SemaphoreType.DMA((n,))],
    )(*shards)


def _exchange(chunks):
    n = len(chunks)

    def body(*refs):
        g_refs, o_refs = refs[:n], refs[n:2 * n]
        send_sems, recv_sems, local_sems = refs[2 * n:]
        x, y, c = lax.axis_index("x"), lax.axis_index("y"), lax.axis_index("c")
        me = 4 * x + 2 * y + c
        own = [pltpu.make_async_copy(g_refs[a].at[me], o_refs[a].at[me], local_sems.at[a]) for a in range(n)]
        for cp in own:
            cp.start()
        copies = []
        for a in range(n):
            for k in range(1, N_DEV):
                px, py, pc = x ^ (k >> 2), y ^ ((k >> 1) & 1), c ^ (k & 1)
                peer = 4 * px + 2 * py + pc
                copies.append(pltpu.make_async_remote_copy(
                    src_ref=g_refs[a].at[peer], dst_ref=o_refs[a].at[me], send_sem=send_sems.at[7 * a + k - 1],
                    recv_sem=recv_sems.at[7 * a + k - 1], device_id=(px, py, pc), device_id_type=_MESH))
        for cp in copies:
            cp.start()
        for cp in copies:
            cp.wait()
        for cp in own:
            cp.wait()

    return _pcall(
        body,
        name="exchange_grads",
        in_specs=[_HBM] * n,
        out_specs=[_HBM] * n,
        out_shape=[jax.ShapeDtypeStruct(g.shape, g.dtype) for g in chunks],
        scratch_shapes=[pltpu.SemaphoreType.DMA((7 * n,)), pltpu.SemaphoreType.DMA((7 * n,)),
                        pltpu.SemaphoreType.DMA((n,))],
    )(*chunks)


_SEM = pl.BlockSpec(memory_space=pltpu.SEMAPHORE)
_EFFECT = pltpu.SideEffectType.DATAFLOW_SIDE_EFFECTING


def _peers(x, y, c):
    out = []
    for k in range(1, N_DEV):
        px, py, pc = x ^ (k >> 2), y ^ ((k >> 1) & 1), c ^ (k & 1)
        out.append(((px, py, pc), 4 * px + 2 * py + pc))
    return out


def _spread_copies(src_refs, land_refs, send_sems, recv_sems, chunked):
    x, y, c = lax.axis_index("x"), lax.axis_index("y"), lax.axis_index("c")
    me = 4 * x + 2 * y + c
    copies = []
    for a, (src, land) in enumerate(zip(src_refs, land_refs)):
        for k, (peer_id, peer) in enumerate(_peers(x, y, c)):
            copies.append(pltpu.make_async_remote_copy(
                src_ref=src.at[peer] if chunked else src, dst_ref=land.at[me], send_sem=send_sems.at[7 * a + k],
                recv_sem=recv_sems.at[7 * a + k], device_id=peer_id, device_id_type=_MESH))
    return copies


def _spread_start(name, srcs, chunked):
    n = len(srcs)
    lands = [lax.empty((N_DEV,) + (s.shape[1:] if chunked else s.shape), s.dtype) for s in srcs]

    def body(*refs):
        src_refs, land_refs = refs[:n], refs[n:2 * n]
        send_sems, recv_sems = refs[2 * n], refs[2 * n + 1]
        token = refs[-1]
        for cp in _spread_copies(src_refs, land_refs, send_sems, recv_sems, chunked):
            cp.start()
        token[...] = jnp.zeros_like(token)

    hbm = lambda a: pltpu.HBM(a.shape, a.dtype)
    outs = _pcall(
        body,
        name=name,
        out_shape=(pltpu.SemaphoreType.DMA((7 * n,)), pltpu.SemaphoreType.DMA((7 * n,)),
                   *[hbm(s) for s in srcs], *[hbm(l) for l in lands], jax.ShapeDtypeStruct((8, LANES), F32)),
        in_specs=[_HBM] * (2 * n),
        out_specs=(_SEM, _SEM, *([_HBM] * (2 * n)), pl.BlockSpec(memory_space=pltpu.VMEM)),
        input_output_aliases={i: 2 + i for i in range(2 * n)},
        compiler_params=pltpu.CompilerParams(has_side_effects=_EFFECT),
    )(*[pltpu.with_memory_space_constraint(a, pltpu.HBM) for a in list(srcs) + lands])
    return outs[0], outs[1], list(outs[2:2 + n]), list(outs[2 + n:2 + 2 * n]), outs[-1]


def _spread_wait(name, send_sems, recv_sems, srcs, lands, after, chunked):
    n = len(srcs)

    def body(*refs):
        src_refs, land_refs = refs[:n], refs[n:2 * n]
        for cp in _spread_copies(src_refs, land_refs, refs[2 * n], refs[2 * n + 1], chunked):
            cp.wait_send()
            cp.wait_recv()

    hbm = lambda a: pltpu.HBM(a.shape, a.dtype)
    outs = _pcall(
        body,
        name=name,
        out_shape=tuple(hbm(a) for a in list(srcs) + list(lands)),
        in_specs=[_HBM] * (2 * n) + [_SEM, _SEM, pl.BlockSpec(memory_space=pl.ANY)],
        out_specs=tuple([_HBM] * (2 * n)),
        input_output_aliases={i: i for i in range(2 * n)},
        compiler_params=pltpu.CompilerParams(has_side_effects=_EFFECT),
    )(*srcs, *lands, send_sems, recv_sems, after)
    return list(outs[:n]), list(outs[n:])


def _with_own(land, own):
    me = 4 * lax.axis_index("x") + 2 * lax.axis_index("y") + lax.axis_index("c")
    return lax.dynamic_update_slice(land, own[None], (me,) + (0,) * own.ndim)


def _reduce_adamw(name, parts, w, m, v, tr):
    _, rows, cols = parts.shape
    tr = min(tr, rows)
    assert rows % tr == 0
    c1 = 1.0 / (1.0 - ADAM_B1 ** ADAM_STEP)
    c2 = 1.0 / (1.0 - ADAM_B2 ** ADAM_STEP)

    def body(p_ref, w_ref, m_ref, v_ref, g_out, d_out, m_out, v_out):
        g = p_ref[0].astype(F32)
        for d in range(1, N_DEV):
            g = g + p_ref[d].astype(F32)
        mn = ADAM_B1 * m_ref[...] + (1.0 - ADAM_B1) * g
        vn = ADAM_B2 * v_ref[...] + (1.0 - ADAM_B2) * (g * g)
        g_out[...] = g
        m_out[...] = mn
        v_out[...] = vn
        d_out[...] = -ADAM_LR * ((mn * c1) / (jnp.sqrt(vn * c2) + ADAM_EPS) + ADAM_WD * w_ref[...])

    spec = pl.BlockSpec((tr, cols), lambda i: (i, 0))
    return _pcall(
        body,
        name=name,
        grid=(rows // tr,),
        in_specs=[pl.BlockSpec((N_DEV, tr, cols), lambda i: (0, i, 0)), spec, spec, spec],
        out_specs=[spec] * 4,
        out_shape=[jax.ShapeDtypeStruct((rows, cols), F32)] * 4,
        compiler_params=_cparams(1),
    )(parts, w, m, v)


_SHARDED = ("w_in", "w_up_dil", "w_up_sb", "w_out", "w_mlp_in", "w_mlp_out")
_FULL_SHAPES = {"w_in": (D_MODEL, IN_COLS), "w_up_dil": (DIL_OUT_WIDTH, D_MODEL), "w_up_sb": (SB_WIDTH, D_MODEL),
                "w_out": (D_MODEL, D_MODEL), "w_mlp_in": (D_MODEL, D_FF), "w_mlp_out": (D_FF, D_MODEL)}
_ROW_SHARDED = ("w_out", "w_mlp_out")


def _shard_shape(name):
    r, c = _FULL_SHAPES[name]
    return (r // N_DEV, c) if name in _ROW_SHARDED else (r, c // N_DEV)


def _assemble(name, gathered):
    r, c = _shard_shape(name)
    if name in _ROW_SHARDED:
        return gathered.reshape(N_DEV * r, c)
    return gathered.transpose(1, 0, 2).reshape(r, N_DEV * c)


def _chunk(name, full):
    r, c = _shard_shape(name)
    if name in _ROW_SHARDED:
        return full.reshape(N_DEV, r, c)
    return full.reshape(r, N_DEV, c).transpose(1, 0, 2)


_SMALL = (("norm_mix_g", D_MODEL), ("b_gate", 2 * D_MODEL), ("norm_mlp_g", D_MODEL), ("norm_final_g", D_MODEL))
_SMALL_N = sum(n for _, n in _SMALL) + LANES


def _pack_small(vals, tail):
    return jnp.concatenate([vals[n].reshape(1, -1) for n, _ in _SMALL] + [tail], axis=1)


def _unpack_small(vec, shapes):
    out, pos = {}, 0
    for n, width in _SMALL:
        out[n] = vec[:, pos:pos + width].reshape(shapes[n])
        pos += width
    return out, vec[:, pos:]


def kernel(x, norm_mix_g, w_in, b_gate, w_up_dil, w_up_sb, w_out, norm_mlp_g, w_mlp_in, w_mlp_out, norm_final_g, loss_target, m_norm_mix_g, m_w_in, m_b_gate, m_w_up_dil, m_w_up_sb, m_w_out, m_norm_mlp_g, m_w_mlp_in, m_w_mlp_out, m_norm_final_g, v_norm_mix_g, v_w_in, v_b_gate, v_w_up_dil, v_w_up_sb, v_w_out, v_norm_mlp_g, v_w_mlp_in, v_w_mlp_out, v_norm_final_g):
    given = dict(locals())
    s = x.shape[1]
    x0 = x.reshape(s, D_MODEL)
    target = loss_target.reshape(s, D_MODEL)
    g1 = norm_mix_g.reshape(1, D_MODEL)
    g2 = norm_mlp_g.reshape(1, D_MODEL)
    g3 = norm_final_g.reshape(1, D_MODEL)
    bg = b_gate.reshape(1, 2 * D_MODEL)
    w_shards = {n: given[n].reshape(_shard_shape(n)) for n in _SHARDED}
    m_shards = {n: given["m_" + n].reshape(_shard_shape(n)) for n in _SHARDED}
    v_shards = {n: given["v_" + n].reshape(_shard_shape(n)) for n in _SHARDED}

    gathered = _all_gather([w_shards[n].astype(BF16) for n in _SHARDED])
    full = {n: _assemble(n, g) for n, g in zip(_SHARDED, gathered)}
    w_in_f = full["w_in"]
    w_qkv, w_gl = w_in_f[:, :QKV_COLS], w_in_f[:, QKV_COLS:]

    def norm1(_, rows, consts):
        _, xh = _rms_stats(rows[0])
        return [xh * consts[0]], []

    (h1,) = _rowk("norm_mix", tm=512, rows=[x0], consts=[g1], row_outs=[(D_MODEL, BF16)], epilogue=norm1)
    qkv = _mm("proj_qkv", h1, w_qkv, out_dtype=BF16, tm=1024, tn=768, tk=D_MODEL)
    gl = _mm("proj_gates", h1, w_gl, out_dtype=F32, tm=512, tn=2048, tk=D_MODEL)
    dil = [_dil_fwd(qkv, g) for g in range(len(DIL_GROUPS))]
    os_, lses = [d[0] for d in dil], [d[1] for d in dil]
    o_a = _dil_mix_fwd(os_, lses, 512)
    o_b, tot_b, sb_steps = _sb_fwd(qkv)
    x1, h2 = _mixer_fwd(o_a, o_b, gl, x0, bg, g2, full["w_up_dil"], full["w_up_sb"], full["w_out"], 256)
    f = _mm("mlp_in", h2, full["w_mlp_in"], out_dtype=BF16, tm=1024, tn=1024, tk=D_MODEL,
            epilogue=lambda r, _: jnp.square(jnp.maximum(r, 0.0)))

    def head(acc, rows, consts):
        x1v, tv = rows
        g3v = consts[0]
        x2 = x1v + acc
        r, xh = _rms_stats(x2)
        diff = xh * g3v - tv
        loss = (0.5 / D_MODEL) * jnp.sum(jnp.sum(diff * diff, axis=0, keepdims=True), axis=1, keepdims=True)
        dy = diff * (1.0 / D_MODEL)
        dx2, dg = _rms_bwd(dy, xh, r, g3v)
        return [dx2, dx2], [dg, jnp.broadcast_to(loss, (1, LANES))]

    dx2, dx2b, gg3, loss_part = _rowk(
        "mlp_out_loss", a=f, w=full["w_mlp_out"], tm=512, tk=D_FF, rows=[x1, target], consts=[g3],
        row_outs=[(D_MODEL, F32), (D_MODEL, BF16)], acc_outs=[D_MODEL, LANES], epilogue=head)

    da = _mm("mlp_out_bwd", dx2b, full["w_mlp_out"], tb=True, out_dtype=BF16, tm=1024, tn=1024, tk=D_MODEL, extra=f,
             epilogue=lambda r, fv: r * (2.0 * jnp.sqrt(fv.astype(F32))))
    g_w_mlp_out = _mm("grad_w_mlp_out", f, dx2b, ta=True, out_dtype=F32, tm=1024, tn=1024, tk=2048)
    g_w_mlp_in = _mm("grad_w_mlp_in", h2, da, ta=True, out_dtype=F32, tm=1024, tn=1024, tk=2048)

    def norm_bwd(acc, rows, consts):
        xv, dres = rows
        r, xh = _rms_stats(xv)
        dx, dg = _rms_bwd(acc, xh, r, consts[0])
        return [dres + dx], [dg]

    dx1, gg2 = _rowk("mlp_in_bwd", a=da, w=full["w_mlp_in"], nt=True, tm=512, tk=D_FF, rows=[x1, dx2], consts=[g2],
                     row_outs=[(D_MODEL, F32)], acc_outs=[D_MODEL], epilogue=norm_bwd)
    do_a, do_b, dgl, g_w_out, g_w_ud, g_w_us, g_bg = _mixer_bwd(
        dx1, o_a, o_b, gl, bg, full["w_up_dil"], full["w_up_sb"], full["w_out"], 256)
    mix = _dil_mix_bwd(do_a, os_, lses, 512)
    dil_b = [_dil_bwd(qkv, mix[g], lses[g], mix[3 + g], g) for g in range(len(DIL_GROUPS))]
    dq_b, dk_b, dv_b = _sb_bwd(qkv, do_b, tot_b, sb_steps)
    dproj = jnp.concatenate(
        [d[0] for d in dil_b] + [d[1] for d in dil_b] + [d[2] for d in dil_b]
        + [dq_b, dk_b.astype(BF16), dv_b.astype(BF16), dgl], axis=1)
    g_w_in = _mm("grad_w_in", h1, dproj, ta=True, out_dtype=F32, tm=512, tn=IN_COLS // 2, tk=1024)
    grad_x, gg1 = _rowk("in_proj_bwd", a=dproj, w=w_in_f, nt=True, tm=512, tk=IN_COLS, rows=[x0, dx1],
                        consts=[g1], row_outs=[(D_MODEL, F32)], acc_outs=[D_MODEL], epilogue=norm_bwd)

    g_full = {"w_in": g_w_in, "w_up_dil": g_w_ud, "w_up_sb": g_w_us, "w_out": g_w_out,
              "w_mlp_in": g_w_mlp_in, "w_mlp_out": g_w_mlp_out}
    small_part = _pack_small({"norm_mix_g": gg1, "b_gate": g_bg, "norm_mlp_g": gg2, "norm_final_g": gg3}, loss_part)
    chunks = [_chunk(n, g_full[n]).astype(BF16) for n in _SHARDED]
    chunks.append(jnp.broadcast_to(small_part[None], (N_DEV, 1, _SMALL_N)))
    *parts, small_parts = _exchange(chunks)

    tags = ("grad_", "delta_", "new_m_", "new_v_")
    outs = {}
    for n, p in zip(_SHARDED, parts):
        res = _reduce_adamw("adamw_" + n, p, w_shards[n], m_shards[n], v_shards[n], 128)
        for tag, val in zip(tags, res):
            outs[tag + n] = val.reshape(given[n].shape)
    small_w = _pack_small(given, jnp.zeros((1, LANES), F32))
    small_m = _pack_small({n: given["m_" + n] for n, _ in _SMALL}, jnp.zeros((1, LANES), F32))
    small_v = _pack_small({n: given["v_" + n] for n, _ in _SMALL}, jnp.ones((1, LANES), F32))
    small_res = _reduce_adamw("adamw_replicated", small_parts, small_w, small_m, small_v, 8)

    small_shapes = {n: given[n].shape for n, _ in _SMALL}
    for tag, small in zip(tags, small_res):
        small_vals, tail = _unpack_small(small, small_shapes)
        for n, val in small_vals.items():
            outs[tag + n] = val
        if tag == "grad_":
            loss = tail[0, 0]
    names = ["norm_mix_g", "w_in", "b_gate", "w_up_dil", "w_up_sb", "w_out", "norm_mlp_g", "w_mlp_in", "w_mlp_out",
             "norm_final_g"]
    return (loss, grad_x.reshape(x.shape), *[outs["grad_" + n] for n in names], *[outs["delta_" + n] for n in names],
            *[outs["new_m_" + n] for n in names], *[outs["new_v_" + n] for n in names])
```

```python
import functools
import math

import jax
import jax.numpy as jnp
from jax import lax
from jax.experimental import pallas as pl
from jax.experimental.pallas import tpu as pltpu

_pcall = pl.pallas_call

F32 = jnp.float32
BF16 = jnp.bfloat16

D_MODEL = 1024
HEAD_DIM = 64
DIL_GROUPS = ((128, 1), (512, 4), (2048, 16))
DIL_HEADS_PER_GROUP = 4
N_DIL_HEADS = 12
N_SB_HEADS = 8
DIL_WIDTH = 768
DIL_OUT_WIDTH = 256
SB_WIDTH = 512
D_FF = 4096
BLOCK = 128
RMS_EPS = 1e-6
NEG_INF = -1e30
QKV_COLS = 3 * DIL_WIDTH + 3 * SB_WIDTH
IN_COLS = QKV_COLS + 2 * D_MODEL
N_DEV = 8

ADAM_LR = 0.001
ADAM_B1 = 0.9
ADAM_B2 = 0.999
ADAM_EPS = 1e-08
ADAM_WD = 0.01
ADAM_STEP = 10

VMEM_LIMIT = 56 * 1024 * 1024
SB_TK = 256
SB_TQ_FWD = 512
SB_TQ_BWD = 256
LANES = 128

_ARB = pltpu.ARBITRARY


def _cparams(n_axes, **kw):
    return pltpu.CompilerParams(dimension_semantics=(_ARB,) * n_axes, vmem_limit_bytes=VMEM_LIMIT, **kw)


def _dot(a, b):
    return jnp.dot(a, b, preferred_element_type=F32)


def _dot_nt(a, b):
    return lax.dot_general(a, b, (((1,), (1,)), ((), ())), preferred_element_type=F32)


def _dot_tn(a, b):
    return lax.dot_general(a, b, (((0,), (0,)), ((), ())), preferred_element_type=F32)


def _split_hi_lo(x):
    hi = x.astype(BF16)
    lo = (x - hi.astype(F32)).astype(BF16)
    return hi, lo


def _dot_hi_lo(x, m):
    hi, lo = _split_hi_lo(x)
    return _dot(hi, m) + _dot(lo, m)


def _sigmoid(x):
    return 1.0 / (1.0 + jnp.exp(-x))


def _mm(name, a, b, *, ta=False, tb=False, out_dtype, tm, tn, tk, epilogue=None, extra=None):
    m = a.shape[1] if ta else a.shape[0]
    k = a.shape[0] if ta else a.shape[1]
    n = b.shape[0] if tb else b.shape[1]
    assert (b.shape[1] if tb else b.shape[0]) == k
    tm, tn, tk = min(tm, m), min(tn, n), min(tk, k)
    assert m % tm == 0 and n % tn == 0 and k % tk == 0, (name, m, n, k, tm, tn, tk)
    nk = k // tk
    dn = (((0 if ta else 1,), (1 if tb else 0,)), ((), ()))
    in_place = nk > 1 and epilogue is None and out_dtype == F32

    def body(*refs):
        if extra is not None:
            a_ref, b_ref, e_ref, o_ref = refs[:4]
        else:
            a_ref, b_ref, o_ref = refs[:3]
            e_ref = None

        def finish(r):
            if epilogue is not None:
                r = epilogue(r, None if e_ref is None else e_ref[...])
            o_ref[...] = r.astype(out_dtype)

        part = lax.dot_general(a_ref[...].astype(BF16), b_ref[...].astype(BF16), dn, preferred_element_type=F32)
        if nk == 1:
            finish(part)
        else:
            acc_ref = o_ref if in_place else refs[-1]
            kk = pl.program_id(2)

            @pl.when(kk == 0)
            def _():
                acc_ref[...] = part

            @pl.when(kk > 0)
            def _():
                acc_ref[...] += part

            if not in_place:

                @pl.when(kk == nk - 1)
                def _():
                    finish(acc_ref[...])

    a_spec = pl.BlockSpec((tk, tm), lambda j, i, kk: (kk, i)) if ta else pl.BlockSpec((tm, tk), lambda j, i, kk: (i, kk))
    b_spec = pl.BlockSpec((tn, tk), lambda j, i, kk: (j, kk)) if tb else pl.BlockSpec((tk, tn), lambda j, i, kk: (kk, j))
    o_spec = pl.BlockSpec((tm, tn), lambda j, i, kk: (i, j))
    in_specs = [a_spec, b_spec]
    args = [a, b]
    if extra is not None:
        in_specs.append(o_spec)
        args.append(extra)
    return _pcall(
        body,
        name=name,
        grid=(n // tn, m // tm, nk),
        in_specs=in_specs,
        out_specs=o_spec,
        out_shape=jax.ShapeDtypeStruct((m, n), out_dtype),
        scratch_shapes=[pltpu.VMEM((tm, tn), F32)] if (nk > 1 and not in_place) else [],
        compiler_params=_cparams(3),
    )(*args)


def _rowk(name, *, a=None, w=None, nt=False, tm, tk=None, rows=(), consts=(), row_outs=(), acc_outs=(), epilogue):
    has_mm = a is not None
    m = a.shape[0] if has_mm else rows[0].shape[0]
    assert m % tm == 0
    nm = m // tm
    if has_mm:
        k = a.shape[1]
        n = w.shape[0] if nt else w.shape[1]
        tk = min(tk, k)
        assert k % tk == 0
        nk = k // tk
    else:
        nk = 1
    n_rows, n_consts, n_ro, n_ao = len(rows), len(consts), len(row_outs), len(acc_outs)

    def body(*refs):
        pos = 0
        if has_mm:
            a_ref, w_ref = refs[0], refs[1]
            pos = 2
        row_refs = refs[pos:pos + n_rows]
        pos += n_rows
        const_refs = refs[pos:pos + n_consts]
        pos += n_consts
        ro_refs = refs[pos:pos + n_ro]
        pos += n_ro
        ao_refs = refs[pos:pos + n_ao]
        pos += n_ao
        i = pl.program_id(0)
        kk = pl.program_id(1)

        def finish(acc):
            ro_vals, ao_vals = epilogue(acc, [r[...] for r in row_refs], [c[...] for c in const_refs])
            for r, v in zip(ro_refs, ro_vals):
                r[...] = v.astype(r.dtype)
            for r, v in zip(ao_refs, ao_vals):

                @pl.when(i == 0)
                def _(r=r, v=v):
                    r[...] = v

                @pl.when(i > 0)
                def _(r=r, v=v):
                    r[...] += v

        if not has_mm:
            finish(None)
            return
        if nt:
            part = _dot_nt(a_ref[...].astype(BF16), w_ref[...])
        else:
            part = _dot(a_ref[...].astype(BF16), w_ref[...])
        if nk == 1:
            finish(part)
        else:
            acc_ref = refs[pos]

            @pl.when(kk == 0)
            def _():
                acc_ref[...] = part

            @pl.when(kk > 0)
            def _():
                acc_ref[...] += part

            @pl.when(kk == nk - 1)
            def _():
                finish(acc_ref[...])

    once = pl.Buffered(1)
    in_specs, args = [], []
    if has_mm:
        in_specs.append(pl.BlockSpec((tm, tk), lambda i, kk: (i, kk)))
        w_mode = once if nk == 1 else None
        in_specs.append(pl.BlockSpec((n, tk), lambda i, kk: (0, kk), pipeline_mode=w_mode) if nt
                        else pl.BlockSpec((tk, n), lambda i, kk: (kk, 0), pipeline_mode=w_mode))
        args += [a, w]
    for r in rows:
        in_specs.append(pl.BlockSpec((tm, r.shape[1]), lambda i, kk: (i, 0)))
        args.append(r)
    for c in consts:
        in_specs.append(pl.BlockSpec(c.shape, lambda i, kk: (0,) * c.ndim, pipeline_mode=once))
        args.append(c)
    out_specs, out_shape = [], []
    for width, dt in row_outs:
        out_specs.append(pl.BlockSpec((tm, width), lambda i, kk: (i, 0)))
        out_shape.append(jax.ShapeDtypeStruct((m, width), dt))
    for width in acc_outs:
        out_specs.append(pl.BlockSpec((1, width), lambda i, kk: (0, 0)))
        out_shape.append(jax.ShapeDtypeStruct((1, width), F32))
    return _pcall(
        body,
        name=name,
        grid=(nm, nk),
        in_specs=in_specs,
        out_specs=out_specs,
        out_shape=out_shape,
        scratch_shapes=[pltpu.VMEM((tm, n), F32)] if (has_mm and nk > 1) else [],
        compiler_params=_cparams(2),
    )(*args)


def _rms_stats(x):
    r = lax.rsqrt(jnp.mean(x * x, axis=-1, keepdims=True) + RMS_EPS)
    return r, x * r


def _rms_bwd(dh, xh, r, g):
    gy = dh * g
    dx = r * (gy - xh * jnp.mean(gy * xh, axis=-1, keepdims=True))
    return dx, jnp.sum(dh * xh, axis=0, keepdims=True)


def _alibi_slope(head):
    return 2.0 ** (-8.0 * (head + 1) / N_DIL_HEADS)


def _dil_masks(i):
    qi = lax.broadcasted_iota(jnp.int32, (BLOCK, 2 * BLOCK), 0)
    kj = lax.broadcasted_iota(jnp.int32, (BLOCK, 2 * BLOCK), 1)
    steps = qi + BLOCK - kj
    valid = (steps >= 0) & (steps <= BLOCK) & ((kj >= BLOCK) | (i > 0))
    return steps.astype(F32), valid


def _dil_view(qkv, group):
    _, dilation = DIL_GROUPS[group]
    if dilation == 1:
        return qkv, QKV_COLS // DIL_OUT_WIDTH, (group, 3 + group, 6 + group)
    w = DIL_OUT_WIDTH
    own = jnp.concatenate([qkv[:, (3 * part + group) * w:(3 * part + group + 1) * w] for part in range(3)], axis=1)
    return own.reshape(qkv.shape[0] // dilation, dilation * 3 * w), 3, (0, 1, 2)


def _dil_specs(ncb, cols, clamp):
    def cur(col):
        return pl.BlockSpec((BLOCK, DIL_OUT_WIDTH), lambda r, i: (clamp(i), r * ncb + col))

    def prev(col):
        return pl.BlockSpec((BLOCK, DIL_OUT_WIDTH), lambda r, i: (jnp.maximum(clamp(i) - 1, 0), r * ncb + col))

    return [cur(cols[0]), cur(cols[1]), prev(cols[1]), cur(cols[2]), prev(cols[2])]


def _dil_fwd(qkv, group):
    window, dilation = DIL_GROUPS[group]
    s = qkv.shape[0]
    sub = s // dilation
    nb = sub // BLOCK
    assert nb * BLOCK * dilation == s and window // dilation == BLOCK
    slopes = [_alibi_slope(group * DIL_HEADS_PER_GROUP + h) * dilation for h in range(DIL_HEADS_PER_GROUP)]

    def body(q_ref, kc_ref, kp_ref, vc_ref, vp_ref, o_ref, lse_ref):
        i = pl.program_id(1)
        q = q_ref[...]
        kk = jnp.concatenate([kp_ref[...], kc_ref[...]], axis=0)
        vv = jnp.concatenate([vp_ref[...], vc_ref[...]], axis=0)
        head_id = lax.broadcasted_iota(jnp.int32, (1, DIL_OUT_WIDTH), 1) // HEAD_DIM
        steps, valid = _dil_masks(i)
        heads = range(DIL_HEADS_PER_GROUP)
        scores = [_dot_nt(jnp.where(head_id == h, q, jnp.zeros_like(q)), kk) for h in heads]
        ps, lses = [], []
        for h in heads:
            logits = scores[h] * (1.0 / math.sqrt(HEAD_DIM)) - slopes[h] * steps
            logits = jnp.where(valid, logits, NEG_INF)
            mx = jnp.max(logits, axis=1, keepdims=True)
            e = jnp.exp(logits - mx)
            den = jnp.sum(e, axis=1, keepdims=True)
            lses.append(mx + jnp.log(den))
            ps.append((e * (1.0 / den)).astype(BF16))
        outs = [_dot(ps[h], vv) for h in heads]
        o, lse_all = outs[0], lses[0]
        for h in heads[1:]:
            o = jnp.where(head_id == h, outs[h], o)
            lse_all = jnp.where(head_id == h, lses[h], lse_all)
        o_ref[...] = o
        lse_ref[...] = jnp.broadcast_to(lse_all, o.shape)

    qkv_v, ncb, cols = _dil_view(qkv, group)
    out_spec = pl.BlockSpec((BLOCK, DIL_OUT_WIDTH), lambda r, i: (i, r))
    o, lse = _pcall(
        body,
        name=f"dil_fwd_g{group}",
        grid=(dilation, nb),
        in_specs=_dil_specs(ncb, cols, lambda i: i),
        out_specs=[out_spec, out_spec],
        out_shape=[jax.ShapeDtypeStruct((sub, dilation * DIL_OUT_WIDTH), F32)] * 2,
        compiler_params=_cparams(2),
    )(qkv_v, qkv_v, qkv_v, qkv_v, qkv_v)
    return o.reshape(s, DIL_OUT_WIDTH), lse.reshape(s, DIL_OUT_WIDTH)


def _dil_bwd(qkv, do_g, lse_g, dterm_g, group):
    window, dilation = DIL_GROUPS[group]
    s = qkv.shape[0]
    sub = s // dilation
    nb = sub // BLOCK
    slopes = [_alibi_slope(group * DIL_HEADS_PER_GROUP + h) * dilation for h in range(DIL_HEADS_PER_GROUP)]
    scale = 1.0 / math.sqrt(HEAD_DIM)

    def body(q_ref, kc_ref, kp_ref, vc_ref, vp_ref, do_ref, lse_ref, dt_ref, dq_ref, dk_ref, dv_ref, ck_ref, cv_ref):
        i = pl.program_id(1)

        @pl.when(i == 0)
        def _():
            ck_ref[...] = jnp.zeros_like(ck_ref)
            cv_ref[...] = jnp.zeros_like(cv_ref)

        @pl.when(i < nb)
        def _():
            q = q_ref[...]
            do = do_ref[...]
            lse_all = lse_ref[...]
            dt_all = dt_ref[...]
            kk = jnp.concatenate([kp_ref[...], kc_ref[...]], axis=0)
            vv = jnp.concatenate([vp_ref[...], vc_ref[...]], axis=0)
            lane = lax.broadcasted_iota(jnp.int32, (1, DIL_OUT_WIDTH), 1)
            head_id = lane // HEAD_DIM
            steps, valid = _dil_masks(i)
            heads = range(DIL_HEADS_PER_GROUP)
            qms = [jnp.where(head_id == h, q, jnp.zeros_like(q)) for h in heads]
            doms = [jnp.where(head_id == h, do, jnp.zeros_like(do)) for h in heads]
            scores = [_dot_nt(qms[h], kk) for h in heads]
            dps = [_dot_nt(doms[h], vv) for h in heads]
            pbs, dss = [], []
            for h in heads:
                first = lane == h * HEAD_DIM
                lse = jnp.sum(jnp.where(first, lse_all, 0.0), axis=1, keepdims=True)
                dt = jnp.sum(jnp.where(first, dt_all, 0.0), axis=1, keepdims=True)
                logits = scores[h] * scale - slopes[h] * steps
                p = jnp.where(valid, jnp.exp(jnp.where(valid, logits, NEG_INF) - lse), 0.0)
                pbs.append(p.astype(BF16))
                dss.append((p * (dps[h] + dt) * scale).astype(BF16))
            dqs = [_dot(dss[h], kk) for h in heads]
            dks = [_dot_tn(dss[h], qms[h]) for h in heads]
            dvs = [_dot_tn(pbs[h], doms[h]) for h in heads]
            dq = dqs[0]
            for h in heads[1:]:
                dq = jnp.where(head_id == h, dqs[h], dq)
            dkk = (dks[0] + dks[1]) + (dks[2] + dks[3])
            dvv = (dvs[0] + dvs[1]) + (dvs[2] + dvs[3])
            dq_ref[...] = dq.astype(dq_ref.dtype)
            dk_ref[...] = (ck_ref[...] + dkk[:BLOCK]).astype(dk_ref.dtype)
            dv_ref[...] = (cv_ref[...] + dvv[:BLOCK]).astype(dv_ref.dtype)
            ck_ref[...] = dkk[BLOCK:]
            cv_ref[...] = dvv[BLOCK:]

        @pl.when(i == nb)
        def _():
            dk_ref[...] = ck_ref[...].astype(dk_ref.dtype)
            dv_ref[...] = cv_ref[...].astype(dv_ref.dtype)

    clamp = lambda i: jnp.minimum(i, nb - 1)
    qkv_v, ncb, cols = _dil_view(qkv, group)
    view = lambda t: t.reshape(sub, dilation * DIL_OUT_WIDTH)
    row_spec = pl.BlockSpec((BLOCK, DIL_OUT_WIDTH), lambda r, i: (clamp(i), r))
    late_spec = pl.BlockSpec((BLOCK, DIL_OUT_WIDTH), lambda r, i: (jnp.maximum(i - 1, 0), r))
    dq, dk, dv = _pcall(
        body,
        name=f"dil_bwd_g{group}",
        grid=(dilation, nb + 1),
        in_specs=_dil_specs(ncb, cols, clamp) + [row_spec, row_spec, row_spec],
        out_specs=[row_spec, late_spec, late_spec],
        out_shape=[jax.ShapeDtypeStruct((sub, dilation * DIL_OUT_WIDTH), BF16)] * 3,
        scratch_shapes=[pltpu.VMEM((BLOCK, DIL_OUT_WIDTH), F32)] * 2,
        compiler_params=_cparams(2),
    )(qkv_v, qkv_v, qkv_v, qkv_v, qkv_v, view(do_g), view(lse_g), view(dterm_g))
    return dq.reshape(s, DIL_OUT_WIDTH), dk.reshape(s, DIL_OUT_WIDTH), dv.reshape(s, DIL_OUT_WIDTH)


def _head_block_ones():
    r = lax.broadcasted_iota(jnp.int32, (DIL_OUT_WIDTH, DIL_OUT_WIDTH), 0) // HEAD_DIM
    c = lax.broadcasted_iota(jnp.int32, (DIL_OUT_WIDTH, DIL_OUT_WIDTH), 1) // HEAD_DIM
    return jnp.where(r == c, 1.0, 0.0).astype(BF16)


def _dil_mix_weights(l0, l1, l2):
    mx = jnp.maximum(jnp.maximum(l0, l1), l2)
    e0, e1, e2 = jnp.exp(l0 - mx), jnp.exp(l1 - mx), jnp.exp(l2 - mx)
    inv = 1.0 / (e0 + e1 + e2)
    return e0 * inv, e1 * inv, e2 * inv


def _dil_mix_fwd(os_, lses, tm):
    def epi(_, rows, consts):
        o0, o1, o2, l0, l1, l2 = rows
        w0, w1, w2 = _dil_mix_weights(l0, l1, l2)
        return [w0 * o0 + w1 * o1 + w2 * o2], []

    (o_a,) = _rowk("dil_mix_fwd", tm=tm, rows=list(os_) + list(lses), row_outs=[(DIL_OUT_WIDTH, BF16)], epilogue=epi)
    return o_a


def _dil_mix_bwd(do_a, os_, lses, tm):
    def epi(_, rows, consts):
        do, o0, o1, o2, l0, l1, l2 = rows
        do = do.astype(F32)
        w0, w1, w2 = _dil_mix_weights(l0, l1, l2)
        mixed = w0 * o0 + w1 * o1 + w2 * o2
        tot = _dot_hi_lo(do * mixed, _head_block_ones())
        return [w0 * do, w1 * do, w2 * do, -w0 * tot, -w1 * tot, -w2 * tot], []

    return _rowk(
        "dil_mix_bwd", tm=tm, rows=[do_a] + list(os_) + list(lses),
        row_outs=[(DIL_OUT_WIDTH, BF16)] * 3 + [(DIL_OUT_WIDTH, F32)] * 3, epilogue=epi)


_SB_Q0 = 3 * DIL_WIDTH // LANES
_SB_K0 = _SB_Q0 + SB_WIDTH // LANES
_SB_V0 = _SB_K0 + SB_WIDTH // LANES


_EXP_CLAMP = 88.0
_SB_DEAD = 104.0


def _tri(t, op):
    r = lax.broadcasted_iota(jnp.int32, (t, t), 0)
    c = lax.broadcasted_iota(jnp.int32, (t, t), 1)
    return jnp.where(op(r, c), 1.0, 0.0).astype(BF16)


def _softplus(z):
    return jnp.maximum(z, jnp.log(1.0 + jnp.exp(jnp.minimum(z, _EXP_CLAMP))))


def _sb_chain_head(qm, kj, mask):
    z = _dot_nt(qm, kj)
    sp = _softplus(z)
    return (sp if mask is None else jnp.where(mask, sp, 0.0)), z - sp


def _sb_fwd(qkv):
    s = qkv.shape[0]
    t = SB_TK
    assert s % (2 * t) == 0
    nq = s // (2 * t)
    n_pairs = SB_WIDTH // LANES

    def body(q_ref, k_ref, v_ref, o_ref, tot_ref, steps_ref):
        p, i = pl.program_id(0), pl.program_id(1)
        lane_hi = lax.broadcasted_iota(jnp.int32, (1, LANES), 1) // HEAD_DIM
        later = _tri(t, lambda r, c: r > c)
        causal = lax.broadcasted_iota(jnp.int32, (t, t), 1) < lax.broadcasted_iota(jnp.int32, (t, t), 0)
        qms = []
        for x in range(2):
            q = q_ref[pl.ds(x * t, t), :] * (1.0 / math.sqrt(HEAD_DIM))
            qms.append([jnp.where(lane_hi == hh, q, jnp.zeros_like(q)) for hh in range(2)])

        def tile(j):
            off = pl.multiple_of(j * t, t)
            return k_ref[pl.ds(off, t), :], v_ref[pl.ds(off, t), :]

        def step(tiles, carry, diag):
            chains = [(x, hh) for x in range(2) if tiles[x] is not None for hh in range(2)]
            kv = {x: tile(tiles[x]) for x in range(2) if tiles[x] is not None}
            heads = [_sb_chain_head(qms[x][hh], kv[x][0], causal if diag else None) for x, hh in chains]
            sufs = [_dot(sp.astype(BF16), later) for sp, _ in heads]
            new = [list(carry[0]), list(carry[1])]
            for (x, hh), (sp, lpos), suf in zip(chains, heads, sufs):
                c, acc = carry[x][hh]
                a = jnp.exp(lpos - suf - c)
                if diag:
                    a = jnp.where(causal, a, 0.0)
                new[x][hh] = (c + jnp.sum(sp, axis=1, keepdims=True), acc + _dot(a.astype(BF16), kv[x][1]))
            return (tuple(new[0]), tuple(new[1]))

        def lowest(carry):
            m = [jnp.min(carry[x][hh][0]) for x in range(2) for hh in range(2)]
            return jnp.minimum(jnp.minimum(m[0], m[1]), jnp.minimum(m[2], m[3]))

        zero = (jnp.zeros((t, 1), F32), jnp.zeros((t, LANES), F32))
        carry = step((2 * i, 2 * i + 1), ((zero, zero), (zero, zero)), True)

        n_full, carry = lax.while_loop(
            lambda st: jnp.logical_and(st[0] < 2 * i, lowest(st[1]) <= _SB_DEAD),
            lambda st: (st[0] + 1, step((2 * i - 1 - st[0], 2 * i - st[0]), st[1], False)),
            (jnp.int32(0), carry))
        b_last = jnp.logical_and(n_full == 2 * i, lowest(carry) <= _SB_DEAD)
        carry = lax.cond(b_last, lambda ca: step((None, 0), ca, False), lambda ca: ca, carry)
        for x in range(2):
            (c0, acc0), (c1, acc1) = carry[x]
            o_ref[pl.ds(x * t, t), :] = jnp.where(lane_hi == 0, acc0, acc1).astype(o_ref.dtype)
            tot_ref[pl.ds(x * t, t), :] = jnp.where(lane_hi == 0, c0, c1)
        steps_ref[p, i] = n_full + b_last.astype(jnp.int32)

    o, tot, n_steps = _pcall(
        body,
        name="sb_fwd",
        grid=(n_pairs, nq),
        in_specs=[
            pl.BlockSpec((2 * t, LANES), lambda p, i: (i, _SB_Q0 + p)),
            pl.BlockSpec((s, LANES), lambda p, i: (0, _SB_K0 + p)),
            pl.BlockSpec((s, LANES), lambda p, i: (0, _SB_V0 + p)),
        ],
        out_specs=[pl.BlockSpec((2 * t, LANES), lambda p, i: (i, p))] * 2 + [pl.BlockSpec(memory_space=pltpu.SMEM)],
        out_shape=[jax.ShapeDtypeStruct((s, SB_WIDTH), BF16), jax.ShapeDtypeStruct((s, SB_WIDTH), F32),
                   jax.ShapeDtypeStruct((n_pairs, nq), jnp.int32)],
        compiler_params=_cparams(2),
    )(qkv, qkv, qkv)
    return o, tot, n_steps


def _sb_bwd(qkv, do_b, tot_b, n_steps):
    s = qkv.shape[0]
    t = SB_TK
    nq = s // (2 * t)
    n_pairs = SB_WIDTH // LANES
    scale = 1.0 / math.sqrt(HEAD_DIM)

    def body(steps_ref, q_ref, k_ref, v_ref, do_ref, tot_ref, dq_ref, dk_ref, dv_ref):
        p, i = pl.program_id(0), pl.program_id(1)

        @pl.when(i == 0)
        def _():
            dk_ref[...] = jnp.zeros_like(dk_ref)
            dv_ref[...] = jnp.zeros_like(dv_ref)

        lane = lax.broadcasted_iota(jnp.int32, (1, LANES), 1)
        lane_hi = lane // HEAD_DIM
        later = _tri(t, lambda r, c: r > c)
        before = _tri(t, lambda r, c: r < c)
        causal = lax.broadcasted_iota(jnp.int32, (t, t), 1) < lax.broadcasted_iota(jnp.int32, (t, t), 0)
        qms, doms, tots = [], [], []
        for x in range(2):
            rows = pl.ds(x * t, t)
            q, do, tot_all = q_ref[rows, :] * scale, do_ref[rows, :], tot_ref[rows, :]
            qms.append([jnp.where(lane_hi == hh, q, jnp.zeros_like(q)) for hh in range(2)])
            doms.append([jnp.where(lane_hi == hh, do, jnp.zeros_like(do)) for hh in range(2)])
            tots.append([jnp.sum(jnp.where(lane == hh * HEAD_DIM, tot_all, 0.0), axis=1, keepdims=True)
                         for hh in range(2)])

        def step(tiles, carry, diag):
            chains = [(x, hh) for x in range(2) if tiles[x] is not None for hh in range(2)]
            offs = {x: pl.multiple_of(tiles[x] * t, t) for x in range(2) if tiles[x] is not None}
            ks = {x: k_ref[pl.ds(off, t), :] for x, off in offs.items()}
            vs = {x: v_ref[pl.ds(off, t), :] for x, off in offs.items()}
            heads = [_sb_chain_head(qms[x][hh], ks[x], causal if diag else None) for x, hh in chains]
            sufs = [_dot(sp.astype(BF16), later) for sp, _ in heads]
            das = [_dot_nt(doms[x][hh], vs[x]) for x, hh in chains]
            new = [list(carry[0]), list(carry[1])]
            sigs, gs, abs_ = [], [], []
            for (x, hh), (sp, lpos), suf, da in zip(chains, heads, sufs, das):
                cl = carry[x][hh][0] + jnp.sum(sp, axis=1, keepdims=True)
                sig = jnp.exp(lpos)
                a = sig * jnp.exp(-suf - (tots[x][hh] - cl))
                if diag:
                    a = jnp.where(causal, a, 0.0)
                g = a * da
                sigs.append(sig)
                gs.append(g)
                abs_.append(a.astype(BF16))
                new[x][hh] = (cl, carry[x][hh][1] + jnp.sum(g, axis=1, keepdims=True), carry[x][hh][2])
            prefs = [_dot(g.astype(BF16), before) for g in gs]
            dvs = [_dot_tn(ab, doms[x][hh]) for (x, hh), ab in zip(chains, abs_)]
            dzs = []
            for (x, hh), sig, g, pref in zip(chains, sigs, gs, prefs):
                dz = g - sig * (g + pref + carry[x][hh][1])
                if diag:
                    dz = jnp.where(causal, dz, 0.0)
                dzs.append(dz.astype(BF16))
            dqs = [_dot(dz, ks[x]) for (x, hh), dz in zip(chains, dzs)]
            dks = [_dot_tn(dz, qms[x][hh]) for (x, hh), dz in zip(chains, dzs)]
            for n, (x, hh) in enumerate(chains):
                cl, cg, dq = new[x][hh]
                new[x][hh] = (cl, cg, dq + dqs[n])
            for x in offs:
                mine = [n for n, ch in enumerate(chains) if ch[0] == x]
                dk_ref[pl.ds(offs[x], t), :] += dks[mine[0]] + dks[mine[1]]
                dv_ref[pl.ds(offs[x], t), :] += dvs[mine[0]] + dvs[mine[1]]
            return (tuple(new[0]), tuple(new[1]))

        taken = steps_ref[p, i]
        n_full = jnp.minimum(taken, 2 * i)
        zero = (jnp.zeros((t, 1), F32), jnp.zeros((t, 1), F32), jnp.zeros((t, LANES), F32))
        carry = ((zero, zero), (zero, zero))
        carry = lax.cond(taken > 2 * i, lambda ca: step((None, 0), ca, False), lambda ca: ca, carry)
        carry = lax.fori_loop(
            0, n_full, lambda n, ca: step((2 * i - n_full + n, 2 * i + 1 - n_full + n), ca, False), carry)
        carry = step((2 * i, 2 * i + 1), carry, True)
        for x in range(2):
            dq = jnp.where(lane_hi == 0, carry[x][0][2], carry[x][1][2])
            dq_ref[pl.ds(x * t, t), :] = (dq * scale).astype(dq_ref.dtype)

    row_spec = pl.BlockSpec((2 * t, LANES), lambda p, i, ns: (i, p))
    full_spec = pl.BlockSpec((s, LANES), lambda p, i, ns: (0, p))
    return _pcall(
        body,
        name="sb_bwd",
        grid_spec=pltpu.PrefetchScalarGridSpec(
            num_scalar_prefetch=1,
            grid=(n_pairs, nq),
            in_specs=[
                pl.BlockSpec((2 * t, LANES), lambda p, i, ns: (i, _SB_Q0 + p)),
                pl.BlockSpec((s, LANES), lambda p, i, ns: (0, _SB_K0 + p)),
                pl.BlockSpec((s, LANES), lambda p, i, ns: (0, _SB_V0 + p)),
                row_spec, row_spec,
            ],
            out_specs=[row_spec, full_spec, full_spec],
        ),
        out_shape=[jax.ShapeDtypeStruct((s, SB_WIDTH), BF16), jax.ShapeDtypeStruct((s, SB_WIDTH), F32),
                   jax.ShapeDtypeStruct((s, SB_WIDTH), F32)],
        compiler_params=_cparams(2),
    )(n_steps, qkv, qkv, qkv, do_b, tot_b)


def _sb_fwd_wide(qkv):
    s = qkv.shape[0]
    t, tq = SB_TK, min(SB_TQ_FWD, s)
    assert tq in (t, 2 * t) and s % (2 * t) == 0
    nq = s // tq
    n_pairs = SB_WIDTH // LANES

    def body(q_ref, k_ref, v_ref, o_ref, tot_ref, steps_ref):
        p, i = pl.program_id(0), pl.program_id(1)
        q = q_ref[...] * (1.0 / math.sqrt(HEAD_DIM))
        lane_hi = lax.broadcasted_iota(jnp.int32, (1, LANES), 1) // HEAD_DIM
        later = _tri(t, lambda r, c: r > c)
        row = lax.broadcasted_iota(jnp.int32, (tq, t), 0)
        col = lax.broadcasted_iota(jnp.int32, (tq, t), 1)
        qms = [jnp.where(lane_hi == hh, q, jnp.zeros_like(q)) for hh in range(2)]

        def step(jj, carry, diag):
            tiles = (2 * jj + 1, 2 * jj)
            offs = [pl.multiple_of(j * t, t) for j in tiles]
            ks = [k_ref[pl.ds(off, t), :] for off in offs]
            vs = [v_ref[pl.ds(off, t), :] for off in offs]
            masks = [(j * t + col) < (i * tq + row) for j in tiles] if diag else None
            chains = [(n, hh) for n in range(2) for hh in range(2)]
            zs = [_dot_nt(qms[hh], ks[n]) for n, hh in chains]
            sps, lposs = [], []
            for (n, hh), z in zip(chains, zs):
                sp = _softplus(z)
                lposs.append(z - sp)
                sps.append(jnp.where(masks[n], sp, 0.0) if diag else sp)
            sufs = [_dot(sp.astype(BF16), later) for sp in sps]
            cs = [carry[0], carry[2]]
            accs = [carry[1], carry[3]]
            for idx, (n, hh) in enumerate(chains):
                a = jnp.exp(lposs[idx] - sufs[idx] - cs[hh])
                if diag:
                    a = jnp.where(masks[n], a, 0.0)
                accs[hh] = accs[hh] + _dot(a.astype(BF16), vs[n])
                cs[hh] = cs[hh] + jnp.sum(sps[idx], axis=1, keepdims=True)
            return cs[0], accs[0], cs[1], accs[1]

        zc, za = jnp.zeros((tq, 1), F32), jnp.zeros((tq, LANES), F32)
        last = (i * tq) // (2 * t)
        carry = step(last, (zc, za, zc, za), True)

        def alive(state):
            n, ca = state
            return jnp.logical_and(n < last, jnp.minimum(jnp.min(ca[0]), jnp.min(ca[2])) <= _SB_DEAD)

        n_off, carry = lax.while_loop(alive, lambda st: (st[0] + 1, step(last - 1 - st[0], st[1], False)),
                                      (jnp.int32(0), carry))
        out = jnp.where(lane_hi == 0, carry[1], carry[3])
        tot = jnp.where(lane_hi == 0, carry[0], carry[2])
        o_ref[...] = out.astype(o_ref.dtype)
        tot_ref[...] = tot
        steps_ref[p, i] = n_off

    o, tot, n_steps = _pcall(
        body,
        name="sb_fwd",
        grid=(n_pairs, nq),
        in_specs=[
            pl.BlockSpec((tq, LANES), lambda p, i: (i, _SB_Q0 + p)),
            pl.BlockSpec((s, LANES), lambda p, i: (0, _SB_K0 + p)),
            pl.BlockSpec((s, LANES), lambda p, i: (0, _SB_V0 + p)),
        ],
        out_specs=[pl.BlockSpec((tq, LANES), lambda p, i: (i, p))] * 2 + [pl.BlockSpec(memory_space=pltpu.SMEM)],
        out_shape=[jax.ShapeDtypeStruct((s, SB_WIDTH), BF16), jax.ShapeDtypeStruct((s, SB_WIDTH), F32),
                   jax.ShapeDtypeStruct((n_pairs, nq), jnp.int32)],
        compiler_params=_cparams(2),
    )(qkv, qkv, qkv)
    return o, tot, n_steps


def _sb_bwd_wide(qkv, do_b, tot_b, n_steps):
    s = qkv.shape[0]
    t, tq = SB_TK, min(SB_TQ_BWD, s)
    assert tq in (t, 2 * t) and s % (2 * t) == 0
    nq = s // tq
    assert nq % n_steps.shape[1] == 0
    n_pairs = SB_WIDTH // LANES
    scale = 1.0 / math.sqrt(HEAD_DIM)

    def body(steps_ref, q_ref, k_ref, v_ref, do_ref, tot_ref, dq_ref, dk_ref, dv_ref):
        p, i = pl.program_id(0), pl.program_id(1)

        @pl.when(i == 0)
        def _():
            dk_ref[...] = jnp.zeros_like(dk_ref)
            dv_ref[...] = jnp.zeros_like(dv_ref)

        q = q_ref[...] * scale
        do = do_ref[...]
        tot_all = tot_ref[...]
        lane = lax.broadcasted_iota(jnp.int32, (1, LANES), 1)
        lane_hi = lane // HEAD_DIM
        later = _tri(t, lambda r, c: r > c)
        before = _tri(t, lambda r, c: r < c)
        row = lax.broadcasted_iota(jnp.int32, (tq, t), 0)
        col = lax.broadcasted_iota(jnp.int32, (tq, t), 1)
        qms = [jnp.where(lane_hi == hh, q, jnp.zeros_like(q)) for hh in range(2)]
        doms = [jnp.where(lane_hi == hh, do, jnp.zeros_like(do)) for hh in range(2)]
        tots = [jnp.sum(jnp.where(lane == hh * HEAD_DIM, tot_all, 0.0), axis=1, keepdims=True) for hh in range(2)]

        def step(jj, carry, diag):
            tiles = (2 * jj, 2 * jj + 1)
            offs = [pl.multiple_of(j * t, t) for j in tiles]
            ks = [k_ref[pl.ds(off, t), :] for off in offs]
            vs = [v_ref[pl.ds(off, t), :] for off in offs]
            masks = [(j * t + col) < (i * tq + row) for j in tiles] if diag else None
            chains = [(n, hh) for n in range(2) for hh in range(2)]
            zs = [_dot_nt(qms[hh], ks[n]) for n, hh in chains]
            sps, sigs = [], []
            for (n, hh), z in zip(chains, zs):
                sp = _softplus(z)
                sigs.append(jnp.exp(z - sp))
                sps.append(jnp.where(masks[n], sp, 0.0) if diag else sp)
            sufs = [_dot(sp.astype(BF16), later) for sp in sps]
            das = [_dot_nt(doms[hh], vs[n]) for n, hh in chains]
            cls = [carry[0], carry[3]]
            cgs = [carry[1], carry[4]]
            accs = [carry[2], carry[5]]
            gs, abs_, cg_at = [], [], []
            for idx, (n, hh) in enumerate(chains):
                cls[hh] = cls[hh] + jnp.sum(sps[idx], axis=1, keepdims=True)
                a = sigs[idx] * jnp.exp(-sufs[idx] - (tots[hh] - cls[hh]))
                if diag:
                    a = jnp.where(masks[n], a, 0.0)
                g = a * das[idx]
                gs.append(g)
                abs_.append(a.astype(BF16))
                cg_at.append(cgs[hh])
                cgs[hh] = cgs[hh] + jnp.sum(g, axis=1, keepdims=True)
            prefs = [_dot(g.astype(BF16), before) for g in gs]
            dvs = [_dot_tn(abs_[idx], doms[hh]) for idx, (n, hh) in enumerate(chains)]
            dzs = []
            for idx, (n, hh) in enumerate(chains):
                g = gs[idx]
                dz = g - sigs[idx] * (g + prefs[idx] + cg_at[idx])
                if diag:
                    dz = jnp.where(masks[n], dz, 0.0)
                dzs.append(dz.astype(BF16))
            for idx, (n, hh) in enumerate(chains):
                accs[hh] = accs[hh] + _dot(dzs[idx], ks[n])
            dks = [_dot_tn(dzs[idx], qms[hh]) for idx, (n, hh) in enumerate(chains)]
            for n in range(2):
                dk_ref[pl.ds(offs[n], t), :] += dks[2 * n] + dks[2 * n + 1]
                dv_ref[pl.ds(offs[n], t), :] += dvs[2 * n] + dvs[2 * n + 1]
            return cls[0], cgs[0], accs[0], cls[1], cgs[1], accs[1]

        zc, za = jnp.zeros((tq, 1), F32), jnp.zeros((tq, LANES), F32)
        last = (i * tq) // (2 * t)
        first = last - steps_ref[p, (i * n_steps.shape[1]) // nq]
        carry = lax.fori_loop(first, last, lambda jj, ca: step(jj, ca, False), (zc, zc, za, zc, zc, za))
        carry = step(last, carry, True)
        dq = jnp.where(lane_hi == 0, carry[2], carry[5])
        dq_ref[...] = (dq * scale).astype(dq_ref.dtype)

    row_spec = pl.BlockSpec((tq, LANES), lambda p, i, ns: (i, p))
    full_spec = pl.BlockSpec((s, LANES), lambda p, i, ns: (0, p))
    return _pcall(
        body,
        name="sb_bwd",
        grid_spec=pltpu.PrefetchScalarGridSpec(
            num_scalar_prefetch=1,
            grid=(n_pairs, nq),
            in_specs=[
                pl.BlockSpec((tq, LANES), lambda p, i, ns: (i, _SB_Q0 + p)),
                pl.BlockSpec((s, LANES), lambda p, i, ns: (0, _SB_K0 + p)),
                pl.BlockSpec((s, LANES), lambda p, i, ns: (0, _SB_V0 + p)),
                row_spec, row_spec,
            ],
            out_specs=[row_spec, full_spec, full_spec],
        ),
        out_shape=[jax.ShapeDtypeStruct((s, SB_WIDTH), BF16), jax.ShapeDtypeStruct((s, SB_WIDTH), F32),
                   jax.ShapeDtypeStruct((s, SB_WIDTH), F32)],
        compiler_params=_cparams(2),
    )(n_steps, qkv, qkv, qkv, do_b, tot_b)


def _gates(gl, bg):
    return _sigmoid(gl[:, :D_MODEL] + bg[:, :D_MODEL]), _sigmoid(gl[:, D_MODEL:] + bg[:, D_MODEL:])


def _mixer_fwd(o_a, o_b, gl, x0, bg, g2, w_ud, w_us, w_out, tm):
    def epi(_, rows, consts):
        oa, ob, glv, x = rows
        bgv, g2v, wud, wus, wout = consts
        ga, gb = _gates(glv, bgv)
        merged = ga * _dot(oa, wud) + gb * _dot(ob, wus)
        x1 = x + _dot(merged.astype(BF16), wout)
        r, xh = _rms_stats(x1)
        return [x1, xh * g2v], []

    return _rowk("mixer_fwd", tm=tm, rows=[o_a, o_b, gl, x0], consts=[bg, g2, w_ud, w_us, w_out],
                 row_outs=[(D_MODEL, F32), (D_MODEL, BF16)], epilogue=epi)


def _mixer_bwd(dx1, o_a, o_b, gl, bg, w_ud, w_us, w_out, tm):
    s = dx1.shape[0]
    nm = s // tm

    def body(dx_ref, oa_ref, ob_ref, gl_ref, bg_ref, wud_ref, wus_ref, wout_ref,
             doa_ref, dob_ref, dgl_ref, gwout_ref, gwud_ref, gwus_ref, gbg_ref):
        i = pl.program_id(0)
        dxb = dx_ref[...].astype(BF16)
        oa, ob = oa_ref[...], ob_ref[...]
        ga, gb = _gates(gl_ref[...], bg_ref[...])
        ua, ub = _dot(oa, wud_ref[...]), _dot(ob, wus_ref[...])
        merged = (ga * ua + gb * ub).astype(BF16)
        dm = _dot_nt(dxb, wout_ref[...])
        dua = (dm * ga).astype(BF16)
        dub = (dm * gb).astype(BF16)
        dgla = dm * ua * ga * (1.0 - ga)
        dglb = dm * ub * gb * (1.0 - gb)
        doa_ref[...] = _dot_nt(dua, wud_ref[...]).astype(doa_ref.dtype)
        dob_ref[...] = _dot_nt(dub, wus_ref[...]).astype(dob_ref.dtype)
        dgl_ref[:, :D_MODEL] = dgla.astype(dgl_ref.dtype)
        dgl_ref[:, D_MODEL:] = dglb.astype(dgl_ref.dtype)
        parts = [(gwout_ref, _dot_tn(merged, dxb)), (gwud_ref, _dot_tn(oa, dua)), (gwus_ref, _dot_tn(ob, dub))]
        for r, v in parts:

            @pl.when(i == 0)
            def _(r=r, v=v):
                r[...] = v

            @pl.when(i > 0)
            def _(r=r, v=v):
                r[...] += v

        sa = jnp.sum(dgla, axis=0, keepdims=True)
        sb = jnp.sum(dglb, axis=0, keepdims=True)

        @pl.when(i == 0)
        def _():
            gbg_ref[:, :D_MODEL] = sa
            gbg_ref[:, D_MODEL:] = sb

        @pl.when(i > 0)
        def _():
            gbg_ref[:, :D_MODEL] += sa
            gbg_ref[:, D_MODEL:] += sb

    row = lambda w: pl.BlockSpec((tm, w), lambda i: (i, 0))
    full = lambda a: pl.BlockSpec(a.shape, lambda i: (0, 0))
    fshape = lambda r, c: jax.ShapeDtypeStruct((r, c), F32)
    return _pcall(
        body,
        name="mixer_bwd",
        grid=(nm,),
        in_specs=[row(D_MODEL), row(DIL_OUT_WIDTH), row(SB_WIDTH), row(2 * D_MODEL),
                  full(bg), full(w_ud), full(w_us), full(w_out)],
        out_specs=[row(DIL_OUT_WIDTH), row(SB_WIDTH), row(2 * D_MODEL),
                   pl.BlockSpec((D_MODEL, D_MODEL), lambda i: (0, 0)),
                   pl.BlockSpec((DIL_OUT_WIDTH, D_MODEL), lambda i: (0, 0)),
                   pl.BlockSpec((SB_WIDTH, D_MODEL), lambda i: (0, 0)),
                   pl.BlockSpec((1, 2 * D_MODEL), lambda i: (0, 0))],
        out_shape=[jax.ShapeDtypeStruct((s, DIL_OUT_WIDTH), BF16), jax.ShapeDtypeStruct((s, SB_WIDTH), BF16),
                   jax.ShapeDtypeStruct((s, 2 * D_MODEL), BF16),
                   fshape(D_MODEL, D_MODEL), fshape(DIL_OUT_WIDTH, D_MODEL), fshape(SB_WIDTH, D_MODEL),
                   fshape(1, 2 * D_MODEL)],
        compiler_params=_cparams(1),
    )(dx1, o_a, o_b, gl, bg, w_ud, w_us, w_out)


_HBM = pl.BlockSpec(memory_space=pltpu.HBM)
_MESH = pl.DeviceIdType.MESH


def _all_gather(shards):
    n = len(shards)

    def body(*refs):
        x_refs, out_refs = refs[:n], refs[n:2 * n]
        send_sems, recv_sems, local_sems = refs[2 * n:]
        x, y, c = lax.axis_index("x"), lax.axis_index("y"), lax.axis_index("c")
        me, sibling = (x, y, c), (x, y, 1 - c)
        chips = [(1 - x, y), (x, 1 - y), (1 - x, 1 - y)]

        def slot(a, px, py, pc):
            return out_refs[a].at[4 * px + 2 * py + pc]

        def copy(a, k, block, to, own=False):
            return pltpu.make_async_remote_copy(
                src_ref=x_refs[a] if own else slot(a, *block), dst_ref=slot(a, *block),
                send_sem=send_sems.at[7 * a + k], recv_sem=recv_sems.at[7 * a + k], device_id=to, device_id_type=_MESH)

        mine = [pltpu.make_async_copy(x_refs[a], slot(a, *me), local_sems.at[a]) for a in range(n)]
        for cp in mine:
            cp.start()
        first = []
        for a in range(n):
            first.append(copy(a, 0, me, sibling, own=True))
            first += [copy(a, 1 + j, me, (*chip, c), own=True) for j, chip in enumerate(chips)]
        for cp in first:
            cp.start()
        passed = []
        for a in range(n):
            for j, chip in enumerate(chips):
                copy(a, 1 + j, (*chip, c), me).wait_recv()
                passed.append(copy(a, 4 + j, (*chip, c), sibling))
                passed[-1].start()
        for a in range(n):
            copy(a, 0, sibling, me).wait_recv()
            for j, chip in enumerate(chips):
                copy(a, 4 + j, (*chip, 1 - c), me).wait_recv()
        for cp in first + passed:
            cp.wait_send()
        for cp in mine:
            cp.wait()

    return _pcall(
        body,
        name="all_gather_weights",
        in_specs=[_HBM] * n,
        out_specs=[_HBM] * n,
        out_shape=[jax.ShapeDtypeStruct((N_DEV,) + s.shape, s.dtype) for s in shards],
        scratch_shapes=[pltpu.SemaphoreType.DMA((7 * n,)), pltpu.SemaphoreType.DMA((7 * n,)),
                        pltpu.SemaphoreType.DMA((n,))],
    )(*shards)


def _exchange(chunks):
    n = len(chunks)

    def body(*refs):
        g_refs, o_refs = refs[:n], refs[n:2 * n]
        send_sems, recv_sems, local_sems = refs[2 * n:]
        x, y, c = lax.axis_index("x"), lax.axis_index("y"), lax.axis_index("c")
        me = 4 * x + 2 * y + c
        own = [pltpu.make_async_copy(g_refs[a].at[me], o_refs[a].at[me], local_sems.at[a]) for a in range(n)]
        for cp in own:
            cp.start()
        copies = []
        for a in range(n):
            for k in range(1, N_DEV):
                px, py, pc = x ^ (k >> 2), y ^ ((k >> 1) & 1), c ^ (k & 1)
                peer = 4 * px + 2 * py + pc
                copies.append(pltpu.make_async_remote_copy(
                    src_ref=g_refs[a].at[peer], dst_ref=o_refs[a].at[me], send_sem=send_sems.at[7 * a + k - 1],
                    recv_sem=recv_sems.at[7 * a + k - 1], device_id=(px, py, pc), device_id_type=_MESH))
        for cp in copies:
            cp.start()
        for cp in copies:
            cp.wait()
        for cp in own:
            cp.wait()

    return _pcall(
        body,
        name="exchange_grads",
        in_specs=[_HBM] * n,
        out_specs=[_HBM] * n,
        out_shape=[jax.ShapeDtypeStruct(g.shape, g.dtype) for g in chunks],
        scratch_shapes=[pltpu.SemaphoreType.DMA((7 * n,)), pltpu.SemaphoreType.DMA((7 * n,)),
                        pltpu.SemaphoreType.DMA((n,))],
    )(*chunks)


_SEM = pl.BlockSpec(memory_space=pltpu.SEMAPHORE)
_EFFECT = pltpu.SideEffectType.DATAFLOW_SIDE_EFFECTING


def _peers(x, y, c):
    out = []
    for k in range(1, N_DEV):
        px, py, pc = x ^ (k >> 2), y ^ ((k >> 1) & 1), c ^ (k & 1)
        out.append(((px, py, pc), 4 * px + 2 * py + pc))
    return out


def _spread_copies(src_refs, land_refs, send_sems, recv_sems, chunked):
    x, y, c = lax.axis_index("x"), lax.axis_index("y"), lax.axis_index("c")
    me = 4 * x + 2 * y + c
    copies = []
    for a, (src, land) in enumerate(zip(src_refs, land_refs)):
        for k, (peer_id, peer) in enumerate(_peers(x, y, c)):
            copies.append(pltpu.make_async_remote_copy(
                src_ref=src.at[peer] if chunked else src, dst_ref=land.at[me], send_sem=send_sems.at[7 * a + k],
                recv_sem=recv_sems.at[7 * a + k], device_id=peer_id, device_id_type=_MESH))
    return copies


def _spread_start(name, srcs, chunked):
    n = len(srcs)
    lands = [lax.empty((N_DEV,) + (s.shape[1:] if chunked else s.shape), s.dtype) for s in srcs]

    def body(*refs):
        src_refs, land_refs = refs[:n], refs[n:2 * n]
        send_sems, recv_sems = refs[2 * n], refs[2 * n + 1]
        token = refs[-1]
        for cp in _spread_copies(src_refs, land_refs, send_sems, recv_sems, chunked):
            cp.start()
        token[...] = jnp.zeros_like(token)

    hbm = lambda a: pltpu.HBM(a.shape, a.dtype)
    outs = _pcall(
        body,
        name=name,
        out_shape=(pltpu.SemaphoreType.DMA((7 * n,)), pltpu.SemaphoreType.DMA((7 * n,)),
                   *[hbm(s) for s in srcs], *[hbm(l) for l in lands], jax.ShapeDtypeStruct((8, LANES), F32)),
        in_specs=[_HBM] * (2 * n),
        out_specs=(_SEM, _SEM, *([_HBM] * (2 * n)), pl.BlockSpec(memory_space=pltpu.VMEM)),
        input_output_aliases={i: 2 + i for i in range(2 * n)},
        compiler_params=pltpu.CompilerParams(has_side_effects=_EFFECT),
    )(*[pltpu.with_memory_space_constraint(a, pltpu.HBM) for a in list(srcs) + lands])
    return outs[0], outs[1], list(outs[2:2 + n]), list(outs[2 + n:2 + 2 * n]), outs[-1]


def _spread_wait(name, send_sems, recv_sems, srcs, lands, after, chunked):
    n = len(srcs)

    def body(*refs):
        src_refs, land_refs = refs[:n], refs[n:2 * n]
        for cp in _spread_copies(src_refs, land_refs, refs[2 * n], refs[2 * n + 1], chunked):
            cp.wait_send()
            cp.wait_recv()

    hbm = lambda a: pltpu.HBM(a.shape, a.dtype)
    outs = _pcall(
        body,
        name=name,
        out_shape=tuple(hbm(a) for a in list(srcs) + list(lands)),
        in_specs=[_HBM] * (2 * n) + [_SEM, _SEM, pl.BlockSpec(memory_space=pl.ANY)],
        out_specs=tuple([_HBM] * (2 * n)),
        input_output_aliases={i: i for i in range(2 * n)},
        compiler_params=pltpu.CompilerParams(has_side_effects=_EFFECT),
    )(*srcs, *lands, send_sems, recv_sems, after)
    return list(outs[:n]), list(outs[n:])


def _with_own(land, own):
    me = 4 * lax.axis_index("x") + 2 * lax.axis_index("y") + lax.axis_index("c")
    return lax.dynamic_update_slice(land, own[None], (me,) + (0,) * own.ndim)


def _reduce_adamw(name, parts, w, m, v, tr):
    _, rows, cols = parts.shape
    tr = min(tr, rows)
    assert rows % tr == 0
    c1 = 1.0 / (1.0 - ADAM_B1 ** ADAM_STEP)
    c2 = 1.0 / (1.0 - ADAM_B2 ** ADAM_STEP)

    def body(p_ref, w_ref, m_ref, v_ref, g_out, d_out, m_out, v_out):
        g = p_ref[0].astype(F32)
        for d in range(1, N_DEV):
            g = g + p_ref[d].astype(F32)
        mn = ADAM_B1 * m_ref[...] + (1.0 - ADAM_B1) * g
        vn = ADAM_B2 * v_ref[...] + (1.0 - ADAM_B2) * (g * g)
        g_out[...] = g
        m_out[...] = mn
        v_out[...] = vn
        d_out[...] = -ADAM_LR * ((mn * c1) / (jnp.sqrt(vn * c2) + ADAM_EPS) + ADAM_WD * w_ref[...])

    spec = pl.BlockSpec((tr, cols), lambda i: (i, 0))
    return _pcall(
        body,
        name=name,
        grid=(rows // tr,),
        in_specs=[pl.BlockSpec((N_DEV, tr, cols), lambda i: (0, i, 0)), spec, spec, spec],
        out_specs=[spec] * 4,
        out_shape=[jax.ShapeDtypeStruct((rows, cols), F32)] * 4,
        compiler_params=_cparams(1),
    )(parts, w, m, v)


_SHARDED = ("w_in", "w_up_dil", "w_up_sb", "w_out", "w_mlp_in", "w_mlp_out")
_FULL_SHAPES = {"w_in": (D_MODEL, IN_COLS), "w_up_dil": (DIL_OUT_WIDTH, D_MODEL), "w_up_sb": (SB_WIDTH, D_MODEL),
                "w_out": (D_MODEL, D_MODEL), "w_mlp_in": (D_MODEL, D_FF), "w_mlp_out": (D_FF, D_MODEL)}
_ROW_SHARDED = ("w_out", "w_mlp_out")


def _shard_shape(name):
    r, c = _FULL_SHAPES[name]
    return (r // N_DEV, c) if name in _ROW_SHARDED else (r, c // N_DEV)


def _assemble(name, gathered):
    r, c = _shard_shape(name)
    if name in _ROW_SHARDED:
        return gathered.reshape(N_DEV * r, c)
    return gathered.transpose(1, 0, 2).reshape(r, N_DEV * c)


def _chunk(name, full):
    r, c = _shard_shape(name)
    if name in _ROW_SHARDED:
        return full.reshape(N_DEV, r, c)
    return full.reshape(r, N_DEV, c).transpose(1, 0, 2)


_SMALL = (("norm_mix_g", D_MODEL), ("b_gate", 2 * D_MODEL), ("norm_mlp_g", D_MODEL), ("norm_final_g", D_MODEL))
_SMALL_N = sum(n for _, n in _SMALL) + LANES


def _pack_small(vals, tail):
    return jnp.concatenate([vals[n].reshape(1, -1) for n, _ in _SMALL] + [tail], axis=1)


def _unpack_small(vec, shapes):
    out, pos = {}, 0
    for n, width in _SMALL:
        out[n] = vec[:, pos:pos + width].reshape(shapes[n])
        pos += width
    return out, vec[:, pos:]


def kernel(x, norm_mix_g, w_in, b_gate, w_up_dil, w_up_sb, w_out, norm_mlp_g, w_mlp_in, w_mlp_out, norm_final_g, loss_target, m_norm_mix_g, m_w_in, m_b_gate, m_w_up_dil, m_w_up_sb, m_w_out, m_norm_mlp_g, m_w_mlp_in, m_w_mlp_out, m_norm_final_g, v_norm_mix_g, v_w_in, v_b_gate, v_w_up_dil, v_w_up_sb, v_w_out, v_norm_mlp_g, v_w_mlp_in, v_w_mlp_out, v_norm_final_g):
    given = dict(locals())
    s = x.shape[1]
    x0 = x.reshape(s, D_MODEL)
    target = loss_target.reshape(s, D_MODEL)
    g1 = norm_mix_g.reshape(1, D_MODEL)
    g2 = norm_mlp_g.reshape(1, D_MODEL)
    g3 = norm_final_g.reshape(1, D_MODEL)
    bg = b_gate.reshape(1, 2 * D_MODEL)
    w_shards = {n: given[n].reshape(_shard_shape(n)) for n in _SHARDED}
    m_shards = {n: given["m_" + n].reshape(_shard_shape(n)) for n in _SHARDED}
    v_shards = {n: given["v_" + n].reshape(_shard_shape(n)) for n in _SHARDED}

    gathered = _all_gather([w_shards[n].astype(BF16) for n in _SHARDED])
    full = {n: _assemble(n, g) for n, g in zip(_SHARDED, gathered)}
    w_in_f = full["w_in"]
    w_qkv, w_gl = w_in_f[:, :QKV_COLS], w_in_f[:, QKV_COLS:]

    def norm1(_, rows, consts):
        _, xh = _rms_stats(rows[0])
        return [xh * consts[0]], []

    (h1,) = _rowk("norm_mix", tm=512, rows=[x0], consts=[g1], row_outs=[(D_MODEL, BF16)], epilogue=norm1)
    qkv = _mm("proj_qkv", h1, w_qkv, out_dtype=BF16, tm=1024, tn=768, tk=D_MODEL)
    gl = _mm("proj_gates", h1, w_gl, out_dtype=F32, tm=512, tn=2048, tk=D_MODEL)
    dil = [_dil_fwd(qkv, g) for g in range(len(DIL_GROUPS))]
    os_, lses = [d[0] for d in dil], [d[1] for d in dil]
    o_a = _dil_mix_fwd(os_, lses, 512)
    o_b, tot_b, sb_steps = _sb_fwd(qkv)
    x1, h2 = _mixer_fwd(o_a, o_b, gl, x0, bg, g2, full["w_up_dil"], full["w_up_sb"], full["w_out"], 256)
    f = _mm("mlp_in", h2, full["w_mlp_in"], out_dtype=BF16, tm=1024, tn=1024, tk=D_MODEL,
            epilogue=lambda r, _: jnp.square(jnp.maximum(r, 0.0)))

    def head(acc, rows, consts):
        x1v, tv = rows
        g3v = consts[0]
        x2 = x1v + acc
        r, xh = _rms_stats(x2)
        diff = xh * g3v - tv
        loss = (0.5 / D_MODEL) * jnp.sum(jnp.sum(diff * diff, axis=0, keepdims=True), axis=1, keepdims=True)
        dy = diff * (1.0 / D_MODEL)
        dx2, dg = _rms_bwd(dy, xh, r, g3v)
        return [dx2, dx2], [dg, jnp.broadcast_to(loss, (1, LANES))]

    dx2, dx2b, gg3, loss_part = _rowk(
        "mlp_out_loss", a=f, w=full["w_mlp_out"], tm=512, tk=D_FF, rows=[x1, target], consts=[g3],
        row_outs=[(D_MODEL, F32), (D_MODEL, BF16)], acc_outs=[D_MODEL, LANES], epilogue=head)

    da = _mm("mlp_out_bwd", dx2b, full["w_mlp_out"], tb=True, out_dtype=BF16, tm=1024, tn=1024, tk=D_MODEL, extra=f,
             epilogue=lambda r, fv: r * (2.0 * jnp.sqrt(fv.astype(F32))))
    g_w_mlp_out = _mm("grad_w_mlp_out", f, dx2b, ta=True, out_dtype=F32, tm=1024, tn=1024, tk=2048)
    g_w_mlp_in = _mm("grad_w_mlp_in", h2, da, ta=True, out_dtype=F32, tm=1024, tn=1024, tk=2048)

    def norm_bwd(acc, rows, consts):
        xv, dres = rows
        r, xh = _rms_stats(xv)
        dx, dg = _rms_bwd(acc, xh, r, consts[0])
        return [dres + dx], [dg]

    dx1, gg2 = _rowk("mlp_in_bwd", a=da, w=full["w_mlp_in"], nt=True, tm=512, tk=D_FF, rows=[x1, dx2], consts=[g2],
                     row_outs=[(D_MODEL, F32)], acc_outs=[D_MODEL], epilogue=norm_bwd)
    do_a, do_b, dgl, g_w_out, g_w_ud, g_w_us, g_bg = _mixer_bwd(
        dx1, o_a, o_b, gl, bg, full["w_up_dil"], full["w_up_sb"], full["w_out"], 256)
    mix = _dil_mix_bwd(do_a, os_, lses, 512)
    dil_b = [_dil_bwd(qkv, mix[g], lses[g], mix[3 + g], g) for g in range(len(DIL_GROUPS))]
    dq_b, dk_b, dv_b = _sb_bwd(qkv, do_b, tot_b, sb_steps)
    dproj = jnp.concatenate(
        [d[0] for d in dil_b] + [d[1] for d in dil_b] + [d[2] for d in dil_b]
        + [dq_b, dk_b.astype(BF16), dv_b.astype(BF16), dgl], axis=1)
    g_w_in = _mm("grad_w_in", h1, dproj, ta=True, out_dtype=F32, tm=512, tn=IN_COLS // 2, tk=1024)
    grad_x, gg1 = _rowk("in_proj_bwd", a=dproj, w=w_in_f, nt=True, tm=512, tk=IN_COLS, rows=[x0, dx1],
                        consts=[g1], row_outs=[(D_MODEL, F32)], acc_outs=[D_MODEL], epilogue=norm_bwd)

    g_full = {"w_in": g_w_in, "w_up_dil": g_w_ud, "w_up_sb": g_w_us, "w_out": g_w_out,
              "w_mlp_in": g_w_mlp_in, "w_mlp_out": g_w_mlp_out}
    small_part = _pack_small({"norm_mix_g": gg1, "b_gate": g_bg, "norm_mlp_g": gg2, "norm_final_g": gg3}, loss_part)
    chunks = [_chunk(n, g_full[n]).astype(BF16) for n in _SHARDED]
    chunks.append(jnp.broadcast_to(small_part[None], (N_DEV, 1, _SMALL_N)))
    *parts, small_parts = _exchange(chunks)

    tags = ("grad_", "delta_", "new_m_", "new_v_")
    outs = {}
    for n, p in zip(_SHARDED, parts):
        res = _reduce_adamw("adamw_" + n, p, w_shards[n], m_shards[n], v_shards[n], 128)
        for tag, val in zip(tags, res):
            outs[tag + n] = val.reshape(given[n].shape)
    small_w = _pack_small(given, jnp.zeros((1, LANES), F32))
    small_m = _pack_small({n: given["m_" + n] for n, _ in _SMALL}, jnp.zeros((1, LANES), F32))
    small_v = _pack_small({n: given["v_" + n] for n, _ in _SMALL}, jnp.ones((1, LANES), F32))
    small_res = _reduce_adamw("adamw_replicated", small_parts, small_w, small_m, small_v, 8)

    small_shapes = {n: given[n].shape for n, _ in _SMALL}
    for tag, small in zip(tags, small_res):
        small_vals, tail = _unpack_small(small, small_shapes)
        for n, val in small_vals.items():
            outs[tag + n] = val
        if tag == "grad_":
            loss = tail[0, 0]
    names = ["norm_mix_g", "w_in", "b_gate", "w_up_dil", "w_up_sb", "w_out", "norm_mlp_g", "w_mlp_in", "w_mlp_out",
             "norm_final_g"]
    return (loss, grad_x.reshape(x.shape), *[outs["grad_" + n] for n in names], *[outs["delta_" + n] for n in names],
            *[outs["new_m_" + n] for n in names], *[outs["new_v_" + n] for n in names])
```

```python
import functools
import math

import jax
import jax.numpy as jnp
from jax import lax
from jax.experimental import pallas as pl
from jax.experimental.pallas import tpu as pltpu

_pcall = pl.pallas_call

F32 = jnp.float32
BF16 = jnp.bfloat16

D_MODEL = 1024
HEAD_DIM = 64
DIL_GROUPS = ((128, 1), (512, 4), (2048, 16))
DIL_HEADS_PER_GROUP = 4
N_DIL_HEADS = 12
N_SB_HEADS = 8
DIL_WIDTH = 768
DIL_OUT_WIDTH = 256
SB_WIDTH = 512
D_FF = 4096
BLOCK = 128
RMS_EPS = 1e-6
NEG_INF = -1e30
QKV_COLS = 3 * DIL_WIDTH + 3 * SB_WIDTH
IN_COLS = QKV_COLS + 2 * D_MODEL
N_DEV = 8

ADAM_LR = 0.001
ADAM_B1 = 0.9
ADAM_B2 = 0.999
ADAM_EPS = 1e-08
ADAM_WD = 0.01
ADAM_STEP = 10

VMEM_LIMIT = 56 * 1024 * 1024
SB_TK = 256
SB_TQ_FWD = 512
SB_TQ_BWD = 256
LANES = 128

_ARB = pltpu.ARBITRARY


def _cparams(n_axes, **kw):
    return pltpu.CompilerParams(dimension_semantics=(_ARB,) * n_axes, vmem_limit_bytes=VMEM_LIMIT, **kw)


def _dot(a, b):
    return jnp.dot(a, b, preferred_element_type=F32)


def _dot_nt(a, b):
    return lax.dot_general(a, b, (((1,), (1,)), ((), ())), preferred_element_type=F32)


def _dot_tn(a, b):
    return lax.dot_general(a, b, (((0,), (0,)), ((), ())), preferred_element_type=F32)


def _split_hi_lo(x):
    hi = x.astype(BF16)
    lo = (x - hi.astype(F32)).astype(BF16)
    return hi, lo


def _dot_hi_lo(x, m):
    hi, lo = _split_hi_lo(x)
    return _dot(hi, m) + _dot(lo, m)


def _sigmoid(x):
    return 1.0 / (1.0 + jnp.exp(-x))


_HBM = pl.BlockSpec(memory_space=pltpu.HBM)
_MESH = pl.DeviceIdType.MESH


class _Spread:
    def __init__(self, srcs, chunked):
        self.srcs, self.chunked, self.n = list(srcs), chunked, len(srcs)

    def land_shapes(self):
        return [jax.ShapeDtypeStruct((N_DEV,) + (s.shape[1:] if self.chunked else s.shape), s.dtype) for s in self.srcs]

    def scratch(self):
        dma = pltpu.SemaphoreType.DMA
        return [dma((7 * self.n,)), dma((7 * self.n,)), dma((self.n,))]

    def copies(self, src_refs, land_refs, send_sems, recv_sems, local_sems):
        x, y, c = lax.axis_index("x"), lax.axis_index("y"), lax.axis_index("c")
        me = 4 * x + 2 * y + c
        out = []
        for a, (src, land) in enumerate(zip(src_refs, land_refs)):
            out.append(pltpu.make_async_copy(src.at[me] if self.chunked else src, land.at[me], local_sems.at[a]))
            for k in range(1, N_DEV):
                px, py, pc = x ^ (k >> 2), y ^ ((k >> 1) & 1), c ^ (k & 1)
                out.append(pltpu.make_async_remote_copy(
                    src_ref=src.at[4 * px + 2 * py + pc] if self.chunked else src, dst_ref=land.at[me],
                    send_sem=send_sems.at[7 * a + k - 1], recv_sem=recv_sems.at[7 * a + k - 1],
                    device_id=(px, py, pc), device_id_type=_MESH))
        return out


def _call(body, args, rider=None, **kw):
    if rider is None:
        return _pcall(body, **kw)(*args)
    grid = kw["grid"]
    single = not isinstance(kw["out_shape"], (list, tuple))
    out_specs = [kw["out_specs"]] if single else list(kw["out_specs"])
    out_shape = [kw["out_shape"]] if single else list(kw["out_shape"])
    in_specs, scratch = list(kw["in_specs"]), list(kw.get("scratch_shapes", []))
    n_in, n_out, n_s, n = len(in_specs), len(out_shape), len(scratch), rider.n

    def hosted(*refs):
        ins, srcs = refs[:n_in], refs[n_in:n_in + n]
        outs, lands = refs[n_in + n:n_in + n + n_out], refs[n_in + n + n_out:n_in + 2 * n + n_out]
        own_scratch, sems = refs[n_in + 2 * n + n_out:n_in + 2 * n + n_out + n_s], refs[n_in + 2 * n + n_out + n_s:]
        ids = [pl.program_id(d) for d in range(len(grid))]
        first = functools.reduce(jnp.logical_and, [i == 0 for i in ids])
        last = functools.reduce(jnp.logical_and, [i == g - 1 for i, g in zip(ids, grid)])
        copies = rider.copies(srcs, lands, *sems)

        @pl.when(first)
        def _():
            for cp in copies:
                cp.start()

        body(*ins, *outs, *own_scratch)

        @pl.when(last)
        def _():
            for cp in copies:
                cp.wait()

    kw = dict(kw, in_specs=in_specs + [_HBM] * n, out_specs=out_specs + [_HBM] * n,
              out_shape=out_shape + rider.land_shapes(), scratch_shapes=scratch + rider.scratch())
    res = _pcall(hosted, **kw)(*args, *rider.srcs)
    return (res[0] if single else list(res[:n_out])), list(res[n_out:])


def _mm(name, a, b, *, ta=False, tb=False, out_dtype, tm, tn, tk, epilogue=None, extra=None, rider=None):
    m = a.shape[1] if ta else a.shape[0]
    k = a.shape[0] if ta else a.shape[1]
    n = b.shape[0] if tb else b.shape[1]
    assert (b.shape[1] if tb else b.shape[0]) == k
    tm, tn, tk = min(tm, m), min(tn, n), min(tk, k)
    assert m % tm == 0 and n % tn == 0 and k % tk == 0, (name, m, n, k, tm, tn, tk)
    nk = k // tk
    dn = (((0 if ta else 1,), (1 if tb else 0,)), ((), ()))
    in_place = nk > 1 and epilogue is None and out_dtype == F32

    def body(*refs):
        if extra is not None:
            a_ref, b_ref, e_ref, o_ref = refs[:4]
        else:
            a_ref, b_ref, o_ref = refs[:3]
            e_ref = None

        def finish(r):
            if epilogue is not None:
                r = epilogue(r, None if e_ref is None else e_ref[...])
            o_ref[...] = r.astype(out_dtype)

        part = lax.dot_general(a_ref[...].astype(BF16), b_ref[...].astype(BF16), dn, preferred_element_type=F32)
        if nk == 1:
            finish(part)
        else:
            acc_ref = o_ref if in_place else refs[-1]
            kk = pl.program_id(2)

            @pl.when(kk == 0)
            def _():
                acc_ref[...] = part

            @pl.when(kk > 0)
            def _():
                acc_ref[...] += part

            if not in_place:

                @pl.when(kk == nk - 1)
                def _():
                    finish(acc_ref[...])

    a_spec = pl.BlockSpec((tk, tm), lambda j, i, kk: (kk, i)) if ta else pl.BlockSpec((tm, tk), lambda j, i, kk: (i, kk))
    b_spec = pl.BlockSpec((tn, tk), lambda j, i, kk: (j, kk)) if tb else pl.BlockSpec((tk, tn), lambda j, i, kk: (kk, j))
    o_spec = pl.BlockSpec((tm, tn), lambda j, i, kk: (i, j))
    in_specs = [a_spec, b_spec]
    args = [a, b]
    if extra is not None:
        in_specs.append(o_spec)
        args.append(extra)
    return _call(
        body, args, rider,
        name=name,
        grid=(n // tn, m // tm, nk),
        in_specs=in_specs,
        out_specs=o_spec,
        out_shape=jax.ShapeDtypeStruct((m, n), out_dtype),
        scratch_shapes=[pltpu.VMEM((tm, tn), F32)] if (nk > 1 and not in_place) else [],
        compiler_params=_cparams(3),
    )


def _rowk(name, *, a=None, w=None, nt=False, tm, tk=None, rows=(), consts=(), row_outs=(), acc_outs=(), epilogue,
          rider=None):
    has_mm = a is not None
    m = a.shape[0] if has_mm else rows[0].shape[0]
    assert m % tm == 0
    nm = m // tm
    if has_mm:
        k = a.shape[1]
        n = w.shape[0] if nt else w.shape[1]
        tk = min(tk, k)
        assert k % tk == 0
        nk = k // tk
    else:
        nk = 1
    n_rows, n_consts, n_ro, n_ao = len(rows), len(consts), len(row_outs), len(acc_outs)

    def body(*refs):
        pos = 0
        if has_mm:
            a_ref, w_ref = refs[0], refs[1]
            pos = 2
        row_refs = refs[pos:pos + n_rows]
        pos += n_rows
        const_refs = refs[pos:pos + n_consts]
        pos += n_consts
        ro_refs = refs[pos:pos + n_ro]
        pos += n_ro
        ao_refs = refs[pos:pos + n_ao]
        pos += n_ao
        i = pl.program_id(0)
        kk = pl.program_id(1)

        def finish(acc):
            ro_vals, ao_vals = epilogue(acc, [r[...] for r in row_refs], [c[...] for c in const_refs])
            for r, v in zip(ro_refs, ro_vals):
                r[...] = v.astype(r.dtype)
            for r, v in zip(ao_refs, ao_vals):

                @pl.when(i == 0)
                def _(r=r, v=v):
                    r[...] = v

                @pl.when(i > 0)
                def _(r=r, v=v):
                    r[...] += v

        if not has_mm:
            finish(None)
            return
        if nt:
            part = _dot_nt(a_ref[...].astype(BF16), w_ref[...])
        else:
            part = _dot(a_ref[...].astype(BF16), w_ref[...])
        if nk == 1:
            finish(part)
        else:
            acc_ref = refs[pos]

            @pl.when(kk == 0)
            def _():
                acc_ref[...] = part

            @pl.when(kk > 0)
            def _():
                acc_ref[...] += part

            @pl.when(kk == nk - 1)
            def _():
                finish(acc_ref[...])

    once = pl.Buffered(1)
    in_specs, args = [], []
    if has_mm:
        in_specs.append(pl.BlockSpec((tm, tk), lambda i, kk: (i, kk)))
        w_mode = once if nk == 1 else None
        in_specs.append(pl.BlockSpec((n, tk), lambda i, kk: (0, kk), pipeline_mode=w_mode) if nt
                        else pl.BlockSpec((tk, n), lambda i, kk: (kk, 0), pipeline_mode=w_mode))
        args += [a, w]
    for r in rows:
        in_specs.append(pl.BlockSpec((tm, r.shape[1]), lambda i, kk: (i, 0)))
        args.append(r)
    for c in consts:
        in_specs.append(pl.BlockSpec(c.shape, lambda i, kk: (0,) * c.ndim, pipeline_mode=once))
        args.append(c)
    out_specs, out_shape = [], []
    for width, dt in row_outs:
        out_specs.append(pl.BlockSpec((tm, width), lambda i, kk: (i, 0)))
        out_shape.append(jax.ShapeDtypeStruct((m, width), dt))
    for width in acc_outs:
        out_specs.append(pl.BlockSpec((1, width), lambda i, kk: (0, 0)))
        out_shape.append(jax.ShapeDtypeStruct((1, width), F32))
    return _call(
        body, args, rider,
        name=name,
        grid=(nm, nk),
        in_specs=in_specs,
        out_specs=out_specs,
        out_shape=out_shape,
        scratch_shapes=[pltpu.VMEM((tm, n), F32)] if (has_mm and nk > 1) else [],
        compiler_params=_cparams(2),
    )


def _rms_stats(x):
    r = lax.rsqrt(jnp.mean(x * x, axis=-1, keepdims=True) + RMS_EPS)
    return r, x * r


def _rms_bwd(dh, xh, r, g):
    gy = dh * g
    dx = r * (gy - xh * jnp.mean(gy * xh, axis=-1, keepdims=True))
    return dx, jnp.sum(dh * xh, axis=0, keepdims=True)


def _alibi_slope(head):
    return 2.0 ** (-8.0 * (head + 1) / N_DIL_HEADS)


def _dil_masks(i):
    qi = lax.broadcasted_iota(jnp.int32, (BLOCK, 2 * BLOCK), 0)
    kj = lax.broadcasted_iota(jnp.int32, (BLOCK, 2 * BLOCK), 1)
    steps = qi + BLOCK - kj
    valid = (steps >= 0) & (steps <= BLOCK) & ((kj >= BLOCK) | (i > 0))
    return steps.astype(F32), valid


def _dil_view(qkv, group):
    _, dilation = DIL_GROUPS[group]
    if dilation == 1:
        return qkv, QKV_COLS // DIL_OUT_WIDTH, (group, 3 + group, 6 + group)
    w = DIL_OUT_WIDTH
    own = jnp.concatenate([qkv[:, (3 * part + group) * w:(3 * part + group + 1) * w] for part in range(3)], axis=1)
    return own.reshape(qkv.shape[0] // dilation, dilation * 3 * w), 3, (0, 1, 2)


def _dil_specs(ncb, cols, clamp):
    def cur(col):
        return pl.BlockSpec((BLOCK, DIL_OUT_WIDTH), lambda r, i: (clamp(i), r * ncb + col))

    def prev(col):
        return pl.BlockSpec((BLOCK, DIL_OUT_WIDTH), lambda r, i: (jnp.maximum(clamp(i) - 1, 0), r * ncb + col))

    return [cur(cols[0]), cur(cols[1]), prev(cols[1]), cur(cols[2]), prev(cols[2])]


def _dil_fwd(qkv, group):
    window, dilation = DIL_GROUPS[group]
    s = qkv.shape[0]
    sub = s // dilation
    nb = sub // BLOCK
    assert nb * BLOCK * dilation == s and window // dilation == BLOCK
    slopes = [_alibi_slope(group * DIL_HEADS_PER_GROUP + h) * dilation for h in range(DIL_HEADS_PER_GROUP)]

    def body(q_ref, kc_ref, kp_ref, vc_ref, vp_ref, o_ref, lse_ref):
        i = pl.program_id(1)
        q = q_ref[...]
        kk = jnp.concatenate([kp_ref[...], kc_ref[...]], axis=0)
        vv = jnp.concatenate([vp_ref[...], vc_ref[...]], axis=0)
        head_id = lax.broadcasted_iota(jnp.int32, (1, DIL_OUT_WIDTH), 1) // HEAD_DIM
        steps, valid = _dil_masks(i)
        heads = range(DIL_HEADS_PER_GROUP)
        scores = [_dot_nt(jnp.where(head_id == h, q, jnp.zeros_like(q)), kk) for h in heads]
        ps, lses = [], []
        for h in heads:
            logits = scores[h] * (1.0 / math.sqrt(HEAD_DIM)) - slopes[h] * steps
            logits = jnp.where(valid, logits, NEG_INF)
            mx = jnp.max(logits, axis=1, keepdims=True)
            e = jnp.exp(logits - mx)
            den = jnp.sum(e, axis=1, keepdims=True)
            lses.append(mx + jnp.log(den))
            ps.append((e * (1.0 / den)).astype(BF16))
        outs = [_dot(ps[h], vv) for h in heads]
        o, lse_all = outs[0], lses[0]
        for h in heads[1:]:
            o = jnp.where(head_id == h, outs[h], o)
            lse_all = jnp.where(head_id == h, lses[h], lse_all)
        o_ref[...] = o
        lse_ref[...] = jnp.broadcast_to(lse_all, o.shape)

    qkv_v, ncb, cols = _dil_view(qkv, group)
    out_spec = pl.BlockSpec((BLOCK, DIL_OUT_WIDTH), lambda r, i: (i, r))
    o, lse = _pcall(
        body,
        name=f"dil_fwd_g{group}",
        grid=(dilation, nb),
        in_specs=_dil_specs(ncb, cols, lambda i: i),
        out_specs=[out_spec, out_spec],
        out_shape=[jax.ShapeDtypeStruct((sub, dilation * DIL_OUT_WIDTH), F32)] * 2,
        compiler_params=_cparams(2),
    )(qkv_v, qkv_v, qkv_v, qkv_v, qkv_v)
    return o.reshape(s, DIL_OUT_WIDTH), lse.reshape(s, DIL_OUT_WIDTH)


def _dil_bwd(qkv, do_g, lse_g, dterm_g, group, rider=None):
    window, dilation = DIL_GROUPS[group]
    s = qkv.shape[0]
    sub = s // dilation
    nb = sub // BLOCK
    slopes = [_alibi_slope(group * DIL_HEADS_PER_GROUP + h) * dilation for h in range(DIL_HEADS_PER_GROUP)]
    scale = 1.0 / math.sqrt(HEAD_DIM)

    def body(q_ref, kc_ref, kp_ref, vc_ref, vp_ref, do_ref, lse_ref, dt_ref, dq_ref, dk_ref, dv_ref, ck_ref, cv_ref):
        i = pl.program_id(1)

        @pl.when(i == 0)
        def _():
            ck_ref[...] = jnp.zeros_like(ck_ref)
            cv_ref[...] = jnp.zeros_like(cv_ref)

        @pl.when(i < nb)
        def _():
            q = q_ref[...]
            do = do_ref[...]
            lse_all = lse_ref[...]
            dt_all = dt_ref[...]
            kk = jnp.concatenate([kp_ref[...], kc_ref[...]], axis=0)
            vv = jnp.concatenate([vp_ref[...], vc_ref[...]], axis=0)
            lane = lax.broadcasted_iota(jnp.int32, (1, DIL_OUT_WIDTH), 1)
            head_id = lane // HEAD_DIM
            steps, valid = _dil_masks(i)
            heads = range(DIL_HEADS_PER_GROUP)
            qms = [jnp.where(head_id == h, q, jnp.zeros_like(q)) for h in heads]
            doms = [jnp.where(head_id == h, do, jnp.zeros_like(do)) for h in heads]
            scores = [_dot_nt(qms[h], kk) for h in heads]
            dps = [_dot_nt(doms[h], vv) for h in heads]
            pbs, dss = [], []
            for h in heads:
                first = lane == h * HEAD_DIM
                lse = jnp.sum(jnp.where(first, lse_all, 0.0), axis=1, keepdims=True)
                dt = jnp.sum(jnp.where(first, dt_all, 0.0), axis=1, keepdims=True)
                logits = scores[h] * scale - slopes[h] * steps
                p = jnp.where(valid, jnp.exp(jnp.where(valid, logits, NEG_INF) - lse), 0.0)
                pbs.append(p.astype(BF16))
                dss.append((p * (dps[h] + dt) * scale).astype(BF16))
            dqs = [_dot(dss[h], kk) for h in heads]
            dks = [_dot_tn(dss[h], qms[h]) for h in heads]
            dvs = [_dot_tn(pbs[h], doms[h]) for h in heads]
            dq = dqs[0]
            for h in heads[1:]:
                dq = jnp.where(head_id == h, dqs[h], dq)
            dkk = (dks[0] + dks[1]) + (dks[2] + dks[3])
            dvv = (dvs[0] + dvs[1]) + (dvs[2] + dvs[3])
            dq_ref[...] = dq.astype(dq_ref.dtype)
            dk_ref[...] = (ck_ref[...] + dkk[:BLOCK]).astype(dk_ref.dtype)
            dv_ref[...] = (cv_ref[...] + dvv[:BLOCK]).astype(dv_ref.dtype)
            ck_ref[...] = dkk[BLOCK:]
            cv_ref[...] = dvv[BLOCK:]

        @pl.when(i == nb)
        def _():
            dk_ref[...] = ck_ref[...].astype(dk_ref.dtype)
            dv_ref[...] = cv_ref[...].astype(dv_ref.dtype)

    clamp = lambda i: jnp.minimum(i, nb - 1)
    qkv_v, ncb, cols = _dil_view(qkv, group)
    view = lambda t: t.reshape(sub, dilation * DIL_OUT_WIDTH)
    row_spec = pl.BlockSpec((BLOCK, DIL_OUT_WIDTH), lambda r, i: (clamp(i), r))
    late_spec = pl.BlockSpec((BLOCK, DIL_OUT_WIDTH), lambda r, i: (jnp.maximum(i - 1, 0), r))
    res = _call(
        body, (qkv_v, qkv_v, qkv_v, qkv_v, qkv_v, view(do_g), view(lse_g), view(dterm_g)), rider,
        name=f"dil_bwd_g{group}",
        grid=(dilation, nb + 1),
        in_specs=_dil_specs(ncb, cols, clamp) + [row_spec, row_spec, row_spec],
        out_specs=[row_spec, late_spec, late_spec],
        out_shape=[jax.ShapeDtypeStruct((sub, dilation * DIL_OUT_WIDTH), BF16)] * 3,
        scratch_shapes=[pltpu.VMEM((BLOCK, DIL_OUT_WIDTH), F32)] * 2,
        compiler_params=_cparams(2),
    )
    grads, lands = res if rider is not None else (res, None)
    grads = tuple(g.reshape(s, DIL_OUT_WIDTH) for g in grads)
    return grads if rider is None else (grads, lands)


def _head_block_ones():
    r = lax.broadcasted_iota(jnp.int32, (DIL_OUT_WIDTH, DIL_OUT_WIDTH), 0) // HEAD_DIM
    c = lax.broadcasted_iota(jnp.int32, (DIL_OUT_WIDTH, DIL_OUT_WIDTH), 1) // HEAD_DIM
    return jnp.where(r == c, 1.0, 0.0).astype(BF16)


def _dil_mix_weights(l0, l1, l2):
    mx = jnp.maximum(jnp.maximum(l0, l1), l2)
    e0, e1, e2 = jnp.exp(l0 - mx), jnp.exp(l1 - mx), jnp.exp(l2 - mx)
    inv = 1.0 / (e0 + e1 + e2)
    return e0 * inv, e1 * inv, e2 * inv


def _dil_mix_fwd(os_, lses, tm):
    def epi(_, rows, consts):
        o0, o1, o2, l0, l1, l2 = rows
        w0, w1, w2 = _dil_mix_weights(l0, l1, l2)
        return [w0 * o0 + w1 * o1 + w2 * o2], []

    (o_a,) = _rowk("dil_mix_fwd", tm=tm, rows=list(os_) + list(lses), row_outs=[(DIL_OUT_WIDTH, BF16)], epilogue=epi)
    return o_a


def _dil_mix_bwd(do_a, os_, lses, tm):
    def epi(_, rows, consts):
        do, o0, o1, o2, l0, l1, l2 = rows
        do = do.astype(F32)
        w0, w1, w2 = _dil_mix_weights(l0, l1, l2)
        mixed = w0 * o0 + w1 * o1 + w2 * o2
        tot = _dot_hi_lo(do * mixed, _head_block_ones())
        return [w0 * do, w1 * do, w2 * do, -w0 * tot, -w1 * tot, -w2 * tot], []

    return _rowk(
        "dil_mix_bwd", tm=tm, rows=[do_a] + list(os_) + list(lses),
        row_outs=[(DIL_OUT_WIDTH, BF16)] * 3 + [(DIL_OUT_WIDTH, F32)] * 3, epilogue=epi)


_SB_Q0 = 3 * DIL_WIDTH // LANES
_SB_K0 = _SB_Q0 + SB_WIDTH // LANES
_SB_V0 = _SB_K0 + SB_WIDTH // LANES


_EXP_CLAMP = 88.0
_SB_DEAD = 104.0


def _tri(t, op):
    r = lax.broadcasted_iota(jnp.int32, (t, t), 0)
    c = lax.broadcasted_iota(jnp.int32, (t, t), 1)
    return jnp.where(op(r, c), 1.0, 0.0).astype(BF16)


def _softplus(z):
    return jnp.maximum(z, jnp.log(1.0 + jnp.exp(jnp.minimum(z, _EXP_CLAMP))))


def _sb_chain_head(qm, kj, mask):
    z = _dot_nt(qm, kj)
    sp = _softplus(z)
    return (sp if mask is None else jnp.where(mask, sp, 0.0)), z - sp


def _sb_fwd(qkv, rider=None):
    s = qkv.shape[0]
    t = SB_TK
    assert s % (2 * t) == 0
    nq = s // (2 * t)
    n_pairs = SB_WIDTH // LANES

    def body(q_ref, k_ref, v_ref, o_ref, tot_ref, steps_ref):
        p, i = pl.program_id(0), pl.program_id(1)
        lane_hi = lax.broadcasted_iota(jnp.int32, (1, LANES), 1) // HEAD_DIM
        later = _tri(t, lambda r, c: r > c)
        causal = lax.broadcasted_iota(jnp.int32, (t, t), 1) < lax.broadcasted_iota(jnp.int32, (t, t), 0)
        qms = []
        for x in range(2):
            q = q_ref[pl.ds(x * t, t), :] * (1.0 / math.sqrt(HEAD_DIM))
            qms.append([jnp.where(lane_hi == hh, q, jnp.zeros_like(q)) for hh in range(2)])

        def tile(j):
            off = pl.multiple_of(j * t, t)
            return k_ref[pl.ds(off, t), :], v_ref[pl.ds(off, t), :]

        def step(tiles, carry, diag):
            chains = [(x, hh) for x in range(2) if tiles[x] is not None for hh in range(2)]
            kv = {x: tile(tiles[x]) for x in range(2) if tiles[x] is not None}
            heads = [_sb_chain_head(qms[x][hh], kv[x][0], causal if diag else None) for x, hh in chains]
            sufs = [_dot(sp.astype(BF16), later) for sp, _ in heads]
            new = [list(carry[0]), list(carry[1])]
            for (x, hh), (sp, lpos), suf in zip(chains, heads, sufs):
                c, acc = carry[x][hh]
                a = jnp.exp(lpos - suf - c)
                if diag:
                    a = jnp.where(causal, a, 0.0)
                new[x][hh] = (c + jnp.sum(sp, axis=1, keepdims=True), acc + _dot(a.astype(BF16), kv[x][1]))
            return (tuple(new[0]), tuple(new[1]))

        def lowest(carry):
            m = [jnp.min(carry[x][hh][0]) for x in range(2) for hh in range(2)]
            return jnp.minimum(jnp.minimum(m[0], m[1]), jnp.minimum(m[2], m[3]))

        zero = (jnp.zeros((t, 1), F32), jnp.zeros((t, LANES), F32))
        carry = step((2 * i, 2 * i + 1), ((zero, zero), (zero, zero)), True)

        n_full, carry = lax.while_loop(
            lambda st: jnp.logical_and(st[0] < 2 * i, lowest(st[1]) <= _SB_DEAD),
            lambda st: (st[0] + 1, step((2 * i - 1 - st[0], 2 * i - st[0]), st[1], False)),
            (jnp.int32(0), carry))
        b_last = jnp.logical_and(n_full == 2 * i, lowest(carry) <= _SB_DEAD)
        carry = lax.cond(b_last, lambda ca: step((None, 0), ca, False), lambda ca: ca, carry)
        for x in range(2):
            (c0, acc0), (c1, acc1) = carry[x]
            o_ref[pl.ds(x * t, t), :] = jnp.where(lane_hi == 0, acc0, acc1).astype(o_ref.dtype)
            tot_ref[pl.ds(x * t, t), :] = jnp.where(lane_hi == 0, c0, c1)
        steps_ref[p, i] = n_full + b_last.astype(jnp.int32)

    return _call(
        body, (qkv, qkv, qkv), rider,
        name="sb_fwd",
        grid=(n_pairs, nq),
        in_specs=[
            pl.BlockSpec((2 * t, LANES), lambda p, i: (i, _SB_Q0 + p)),
            pl.BlockSpec((s, LANES), lambda p, i: (0, _SB_K0 + p)),
            pl.BlockSpec((s, LANES), lambda p, i: (0, _SB_V0 + p)),
        ],
        out_specs=[pl.BlockSpec((2 * t, LANES), lambda p, i: (i, p))] * 2 + [pl.BlockSpec(memory_space=pltpu.SMEM)],
        out_shape=[jax.ShapeDtypeStruct((s, SB_WIDTH), BF16), jax.ShapeDtypeStruct((s, SB_WIDTH), F32),
                   jax.ShapeDtypeStruct((n_pairs, nq), jnp.int32)],
        compiler_params=_cparams(2),
    )


def _sb_bwd(qkv, do_b, tot_b, n_steps):
    s = qkv.shape[0]
    t = SB_TK
    nq = s // (2 * t)
    n_pairs = SB_WIDTH // LANES
    scale = 1.0 / math.sqrt(HEAD_DIM)

    def body(steps_ref, q_ref, k_ref, v_ref, do_ref, tot_ref, dq_ref, dk_ref, dv_ref):
        p, i = pl.program_id(0), pl.program_id(1)

        @pl.when(i == 0)
        def _():
            dk_ref[...] = jnp.zeros_like(dk_ref)
            dv_ref[...] = jnp.zeros_like(dv_ref)

        lane = lax.broadcasted_iota(jnp.int32, (1, LANES), 1)
        lane_hi = lane // HEAD_DIM
        later = _tri(t, lambda r, c: r > c)
        before = _tri(t, lambda r, c: r < c)
        causal = lax.broadcasted_iota(jnp.int32, (t, t), 1) < lax.broadcasted_iota(jnp.int32, (t, t), 0)
        qms, doms, tots = [], [], []
        for x in range(2):
            rows = pl.ds(x * t, t)
            q, do, tot_all = q_ref[rows, :] * scale, do_ref[rows, :], tot_ref[rows, :]
            qms.append([jnp.where(lane_hi == hh, q, jnp.zeros_like(q)) for hh in range(2)])
            doms.append([jnp.where(lane_hi == hh, do, jnp.zeros_like(do)) for hh in range(2)])
            tots.append([jnp.sum(jnp.where(lane == hh * HEAD_DIM, tot_all, 0.0), axis=1, keepdims=True)
                         for hh in range(2)])

        def step(tiles, carry, diag):
            chains = [(x, hh) for x in range(2) if tiles[x] is not None for hh in range(2)]
            offs = {x: pl.multiple_of(tiles[x] * t, t) for x in range(2) if tiles[x] is not None}
            ks = {x: k_ref[pl.ds(off, t), :] for x, off in offs.items()}
            vs = {x: v_ref[pl.ds(off, t), :] for x, off in offs.items()}
            heads = [_sb_chain_head(qms[x][hh], ks[x], causal if diag else None) for x, hh in chains]
            sufs = [_dot(sp.astype(BF16), later) for sp, _ in heads]
            das = [_dot_nt(doms[x][hh], vs[x]) for x, hh in chains]
            new = [list(carry[0]), list(carry[1])]
            sigs, gs, abs_ = [], [], []
            for (x, hh), (sp, lpos), suf, da in zip(chains, heads, sufs, das):
                cl = carry[x][hh][0] + jnp.sum(sp, axis=1, keepdims=True)
                sig = jnp.exp(lpos)
                a = sig * jnp.exp(-suf - (tots[x][hh] - cl))
                if diag:
                    a = jnp.where(causal, a, 0.0)
                g = a * da
                sigs.append(sig)
                gs.append(g)
                abs_.append(a.astype(BF16))
                new[x][hh] = (cl, carry[x][hh][1] + jnp.sum(g, axis=1, keepdims=True), carry[x][hh][2])
            prefs = [_dot(g.astype(BF16), before) for g in gs]
            dvs = [_dot_tn(ab, doms[x][hh]) for (x, hh), ab in zip(chains, abs_)]
            dzs = []
            for (x, hh), sig, g, pref in zip(chains, sigs, gs, prefs):
                dz = g - sig * (g + pref + carry[x][hh][1])
                if diag:
                    dz = jnp.where(causal, dz, 0.0)
                dzs.append(dz.astype(BF16))
            dqs = [_dot(dz, ks[x]) for (x, hh), dz in zip(chains, dzs)]
            dks = [_dot_tn(dz, qms[x][hh]) for (x, hh), dz in zip(chains, dzs)]
            for n, (x, hh) in enumerate(chains):
                cl, cg, dq = new[x][hh]
                new[x][hh] = (cl, cg, dq + dqs[n])
            for x in offs:
                mine = [n for n, ch in enumerate(chains) if ch[0] == x]
                dk_ref[pl.ds(offs[x], t), :] += dks[mine[0]] + dks[mine[1]]
                dv_ref[pl.ds(offs[x], t), :] += dvs[mine[0]] + dvs[mine[1]]
            return (tuple(new[0]), tuple(new[1]))

        taken = steps_ref[p, i]
        n_full = jnp.minimum(taken, 2 * i)
        zero = (jnp.zeros((t, 1), F32), jnp.zeros((t, 1), F32), jnp.zeros((t, LANES), F32))
        carry = ((zero, zero), (zero, zero))
        carry = lax.cond(taken > 2 * i, lambda ca: step((None, 0), ca, False), lambda ca: ca, carry)
        carry = lax.fori_loop(
            0, n_full, lambda n, ca: step((2 * i - n_full + n, 2 * i + 1 - n_full + n), ca, False), carry)
        carry = step((2 * i, 2 * i + 1), carry, True)
        for x in range(2):
            dq = jnp.where(lane_hi == 0, carry[x][0][2], carry[x][1][2])
            dq_ref[pl.ds(x * t, t), :] = (dq * scale).astype(dq_ref.dtype)

    row_spec = pl.BlockSpec((2 * t, LANES), lambda p, i, ns: (i, p))
    full_spec = pl.BlockSpec((s, LANES), lambda p, i, ns: (0, p))
    return _pcall(
        body,
        name="sb_bwd",
        grid_spec=pltpu.PrefetchScalarGridSpec(
            num_scalar_prefetch=1,
            grid=(n_pairs, nq),
            in_specs=[
                pl.BlockSpec((2 * t, LANES), lambda p, i, ns: (i, _SB_Q0 + p)),
                pl.BlockSpec((s, LANES), lambda p, i, ns: (0, _SB_K0 + p)),
                pl.BlockSpec((s, LANES), lambda p, i, ns: (0, _SB_V0 + p)),
                row_spec, row_spec,
            ],
            out_specs=[row_spec, full_spec, full_spec],
        ),
        out_shape=[jax.ShapeDtypeStruct((s, SB_WIDTH), BF16), jax.ShapeDtypeStruct((s, SB_WIDTH), F32),
                   jax.ShapeDtypeStruct((s, SB_WIDTH), F32)],
        compiler_params=_cparams(2),
    )(n_steps, qkv, qkv, qkv, do_b, tot_b)


def _sb_fwd_wide(qkv):
    s = qkv.shape[0]
    t, tq = SB_TK, min(SB_TQ_FWD, s)
    assert tq in (t, 2 * t) and s % (2 * t) == 0
    nq = s // tq
    n_pairs = SB_WIDTH // LANES

    def body(q_ref, k_ref, v_ref, o_ref, tot_ref, steps_ref):
        p, i = pl.program_id(0), pl.program_id(1)
        q = q_ref[...] * (1.0 / math.sqrt(HEAD_DIM))
        lane_hi = lax.broadcasted_iota(jnp.int32, (1, LANES), 1) // HEAD_DIM
        later = _tri(t, lambda r, c: r > c)
        row = lax.broadcasted_iota(jnp.int32, (tq, t), 0)
        col = lax.broadcasted_iota(jnp.int32, (tq, t), 1)
        qms = [jnp.where(lane_hi == hh, q, jnp.zeros_like(q)) for hh in range(2)]

        def step(jj, carry, diag):
            tiles = (2 * jj + 1, 2 * jj)
            offs = [pl.multiple_of(j * t, t) for j in tiles]
            ks = [k_ref[pl.ds(off, t), :] for off in offs]
            vs = [v_ref[pl.ds(off, t), :] for off in offs]
            masks = [(j * t + col) < (i * tq + row) for j in tiles] if diag else None
            chains = [(n, hh) for n in range(2) for hh in range(2)]
            zs = [_dot_nt(qms[hh], ks[n]) for n, hh in chains]
            sps, lposs = [], []
            for (n, hh), z in zip(chains, zs):
                sp = _softplus(z)
                lposs.append(z - sp)
                sps.append(jnp.where(masks[n], sp, 0.0) if diag else sp)
            sufs = [_dot(sp.astype(BF16), later) for sp in sps]
            cs = [carry[0], carry[2]]
            accs = [carry[1], carry[3]]
            for idx, (n, hh) in enumerate(chains):
                a = jnp.exp(lposs[idx] - sufs[idx] - cs[hh])
                if diag:
                    a = jnp.where(masks[n], a, 0.0)
                accs[hh] = accs[hh] + _dot(a.astype(BF16), vs[n])
                cs[hh] = cs[hh] + jnp.sum(sps[idx], axis=1, keepdims=True)
            return cs[0], accs[0], cs[1], accs[1]

        zc, za = jnp.zeros((tq, 1), F32), jnp.zeros((tq, LANES), F32)
        last = (i * tq) // (2 * t)
        carry = step(last, (zc, za, zc, za), True)

        def alive(state):
            n, ca = state
            return jnp.logical_and(n < last, jnp.minimum(jnp.min(ca[0]), jnp.min(ca[2])) <= _SB_DEAD)

        n_off, carry = lax.while_loop(alive, lambda st: (st[0] + 1, step(last - 1 - st[0], st[1], False)),
                                      (jnp.int32(0), carry))
        out = jnp.where(lane_hi == 0, carry[1], carry[3])
        tot = jnp.where(lane_hi == 0, carry[0], carry[2])
        o_ref[...] = out.astype(o_ref.dtype)
        tot_ref[...] = tot
        steps_ref[p, i] = n_off

    o, tot, n_steps = _pcall(
        body,
        name="sb_fwd",
        grid=(n_pairs, nq),
        in_specs=[
            pl.BlockSpec((tq, LANES), lambda p, i: (i, _SB_Q0 + p)),
            pl.BlockSpec((s, LANES), lambda p, i: (0, _SB_K0 + p)),
            pl.BlockSpec((s, LANES), lambda p, i: (0, _SB_V0 + p)),
        ],
        out_specs=[pl.BlockSpec((tq, LANES), lambda p, i: (i, p))] * 2 + [pl.BlockSpec(memory_space=pltpu.SMEM)],
        out_shape=[jax.ShapeDtypeStruct((s, SB_WIDTH), BF16), jax.ShapeDtypeStruct((s, SB_WIDTH), F32),
                   jax.ShapeDtypeStruct((n_pairs, nq), jnp.int32)],
        compiler_params=_cparams(2),
    )(qkv, qkv, qkv)
    return o, tot, n_steps


def _sb_bwd_wide(qkv, do_b, tot_b, n_steps):
    s = qkv.shape[0]
    t, tq = SB_TK, min(SB_TQ_BWD, s)
    assert tq in (t, 2 * t) and s % (2 * t) == 0
    nq = s // tq
    assert nq % n_steps.shape[1] == 0
    n_pairs = SB_WIDTH // LANES
    scale = 1.0 / math.sqrt(HEAD_DIM)

    def body(steps_ref, q_ref, k_ref, v_ref, do_ref, tot_ref, dq_ref, dk_ref, dv_ref):
        p, i = pl.program_id(0), pl.program_id(1)

        @pl.when(i == 0)
        def _():
            dk_ref[...] = jnp.zeros_like(dk_ref)
            dv_ref[...] = jnp.zeros_like(dv_ref)

        q = q_ref[...] * scale
        do = do_ref[...]
        tot_all = tot_ref[...]
        lane = lax.broadcasted_iota(jnp.int32, (1, LANES), 1)
        lane_hi = lane // HEAD_DIM
        later = _tri(t, lambda r, c: r > c)
        before = _tri(t, lambda r, c: r < c)
        row = lax.broadcasted_iota(jnp.int32, (tq, t), 0)
        col = lax.broadcasted_iota(jnp.int32, (tq, t), 1)
        qms = [jnp.where(lane_hi == hh, q, jnp.zeros_like(q)) for hh in range(2)]
        doms = [jnp.where(lane_hi == hh, do, jnp.zeros_like(do)) for hh in range(2)]
        tots = [jnp.sum(jnp.where(lane == hh * HEAD_DIM, tot_all, 0.0), axis=1, keepdims=True) for hh in range(2)]

        def step(jj, carry, diag):
            tiles = (2 * jj, 2 * jj + 1)
            offs = [pl.multiple_of(j * t, t) for j in tiles]
            ks = [k_ref[pl.ds(off, t), :] for off in offs]
            vs = [v_ref[pl.ds(off, t), :] for off in offs]
            masks = [(j * t + col) < (i * tq + row) for j in tiles] if diag else None
            chains = [(n, hh) for n in range(2) for hh in range(2)]
            zs = [_dot_nt(qms[hh], ks[n]) for n, hh in chains]
            sps, sigs = [], []
            for (n, hh), z in zip(chains, zs):
                sp = _softplus(z)
                sigs.append(jnp.exp(z - sp))
                sps.append(jnp.where(masks[n], sp, 0.0) if diag else sp)
            sufs = [_dot(sp.astype(BF16), later) for sp in sps]
            das = [_dot_nt(doms[hh], vs[n]) for n, hh in chains]
            cls = [carry[0], carry[3]]
            cgs = [carry[1], carry[4]]
            accs = [carry[2], carry[5]]
            gs, abs_, cg_at = [], [], []
            for idx, (n, hh) in enumerate(chains):
                cls[hh] = cls[hh] + jnp.sum(sps[idx], axis=1, keepdims=True)
                a = sigs[idx] * jnp.exp(-sufs[idx] - (tots[hh] - cls[hh]))
                if diag:
                    a = jnp.where(masks[n], a, 0.0)
                g = a * das[idx]
                gs.append(g)
                abs_.append(a.astype(BF16))
                cg_at.append(cgs[hh])
                cgs[hh] = cgs[hh] + jnp.sum(g, axis=1, keepdims=True)
            prefs = [_dot(g.astype(BF16), before) for g in gs]
            dvs = [_dot_tn(abs_[idx], doms[hh]) for idx, (n, hh) in enumerate(chains)]
            dzs = []
            for idx, (n, hh) in enumerate(chains):
                g = gs[idx]
                dz = g - sigs[idx] * (g + prefs[idx] + cg_at[idx])
                if diag:
                    dz = jnp.where(masks[n], dz, 0.0)
                dzs.append(dz.astype(BF16))
            for idx, (n, hh) in enumerate(chains):
                accs[hh] = accs[hh] + _dot(dzs[idx], ks[n])
            dks = [_dot_tn(dzs[idx], qms[hh]) for idx, (n, hh) in enumerate(chains)]
            for n in range(2):
                dk_ref[pl.ds(offs[n], t), :] += dks[2 * n] + dks[2 * n + 1]
                dv_ref[pl.ds(offs[n], t), :] += dvs[2 * n] + dvs[2 * n + 1]
            return cls[0], cgs[0], accs[0], cls[1], cgs[1], accs[1]

        zc, za = jnp.zeros((tq, 1), F32), jnp.zeros((tq, LANES), F32)
        last = (i * tq) // (2 * t)
        first = last - steps_ref[p, (i * n_steps.shape[1]) // nq]
        carry = lax.fori_loop(first, last, lambda jj, ca: step(jj, ca, False), (zc, zc, za, zc, zc, za))
        carry = step(last, carry, True)
        dq = jnp.where(lane_hi == 0, carry[2], carry[5])
        dq_ref[...] = (dq * scale).astype(dq_ref.dtype)

    row_spec = pl.BlockSpec((tq, LANES), lambda p, i, ns: (i, p))
    full_spec = pl.BlockSpec((s, LANES), lambda p, i, ns: (0, p))
    return _pcall(
        body,
        name="sb_bwd",
        grid_spec=pltpu.PrefetchScalarGridSpec(
            num_scalar_prefetch=1,
            grid=(n_pairs, nq),
            in_specs=[
                pl.BlockSpec((tq, LANES), lambda p, i, ns: (i, _SB_Q0 + p)),
                pl.BlockSpec((s, LANES), lambda p, i, ns: (0, _SB_K0 + p)),
                pl.BlockSpec((s, LANES), lambda p, i, ns: (0, _SB_V0 + p)),
                row_spec, row_spec,
            ],
            out_specs=[row_spec, full_spec, full_spec],
        ),
        out_shape=[jax.ShapeDtypeStruct((s, SB_WIDTH), BF16), jax.ShapeDtypeStruct((s, SB_WIDTH), F32),
                   jax.ShapeDtypeStruct((s, SB_WIDTH), F32)],
        compiler_params=_cparams(2),
    )(n_steps, qkv, qkv, qkv, do_b, tot_b)


def _gates(gl, bg):
    return _sigmoid(gl[:, :D_MODEL] + bg[:, :D_MODEL]), _sigmoid(gl[:, D_MODEL:] + bg[:, D_MODEL:])


def _mixer_fwd(o_a, o_b, gl, x0, bg, g2, w_ud, w_us, w_out, tm):
    def epi(_, rows, consts):
        oa, ob, glv, x = rows
        bgv, g2v, wud, wus, wout = consts
        ga, gb = _gates(glv, bgv)
        merged = ga * _dot(oa, wud) + gb * _dot(ob, wus)
        x1 = x + _dot(merged.astype(BF16), wout)
        r, xh = _rms_stats(x1)
        return [x1, xh * g2v], []

    return _rowk("mixer_fwd", tm=tm, rows=[o_a, o_b, gl, x0], consts=[bg, g2, w_ud, w_us, w_out],
                 row_outs=[(D_MODEL, F32), (D_MODEL, BF16)], epilogue=epi)


def _mixer_bwd(dx1, o_a, o_b, gl, bg, w_ud, w_us, w_out, tm):
    s = dx1.shape[0]
    nm = s // tm

    def body(dx_ref, oa_ref, ob_ref, gl_ref, bg_ref, wud_ref, wus_ref, wout_ref,
             doa_ref, dob_ref, dgl_ref, gwout_ref, gwud_ref, gwus_ref, gbg_ref):
        i = pl.program_id(0)
        dxb = dx_ref[...].astype(BF16)
        oa, ob = oa_ref[...], ob_ref[...]
        ga, gb = _gates(gl_ref[...], bg_ref[...])
        ua, ub = _dot(oa, wud_ref[...]), _dot(ob, wus_ref[...])
        merged = (ga * ua + gb * ub).astype(BF16)
        dm = _dot_nt(dxb, wout_ref[...])
        dua = (dm * ga).astype(BF16)
        dub = (dm * gb).astype(BF16)
        dgla = dm * ua * ga * (1.0 - ga)
        dglb = dm * ub * gb * (1.0 - gb)
        doa_ref[...] = _dot_nt(dua, wud_ref[...]).astype(doa_ref.dtype)
        dob_ref[...] = _dot_nt(dub, wus_ref[...]).astype(dob_ref.dtype)
        dgl_ref[:, :D_MODEL] = dgla.astype(dgl_ref.dtype)
        dgl_ref[:, D_MODEL:] = dglb.astype(dgl_ref.dtype)
        parts = [(gwout_ref, _dot_tn(merged, dxb)), (gwud_ref, _dot_tn(oa, dua)), (gwus_ref, _dot_tn(ob, dub))]
        for r, v in parts:

            @pl.when(i == 0)
            def _(r=r, v=v):
                r[...] = v

            @pl.when(i > 0)
            def _(r=r, v=v):
                r[...] += v

        sa = jnp.sum(dgla, axis=0, keepdims=True)
        sb = jnp.sum(dglb, axis=0, keepdims=True)

        @pl.when(i == 0)
        def _():
            gbg_ref[:, :D_MODEL] = sa
            gbg_ref[:, D_MODEL:] = sb

        @pl.when(i > 0)
        def _():
            gbg_ref[:, :D_MODEL] += sa
            gbg_ref[:, D_MODEL:] += sb

    row = lambda w: pl.BlockSpec((tm, w), lambda i: (i, 0))
    full = lambda a: pl.BlockSpec(a.shape, lambda i: (0, 0))
    fshape = lambda r, c: jax.ShapeDtypeStruct((r, c), F32)
    return _pcall(
        body,
        name="mixer_bwd",
        grid=(nm,),
        in_specs=[row(D_MODEL), row(DIL_OUT_WIDTH), row(SB_WIDTH), row(2 * D_MODEL),
                  full(bg), full(w_ud), full(w_us), full(w_out)],
        out_specs=[row(DIL_OUT_WIDTH), row(SB_WIDTH), row(2 * D_MODEL),
                   pl.BlockSpec((D_MODEL, D_MODEL), lambda i: (0, 0)),
                   pl.BlockSpec((DIL_OUT_WIDTH, D_MODEL), lambda i: (0, 0)),
                   pl.BlockSpec((SB_WIDTH, D_MODEL), lambda i: (0, 0)),
                   pl.BlockSpec((1, 2 * D_MODEL), lambda i: (0, 0))],
        out_shape=[jax.ShapeDtypeStruct((s, DIL_OUT_WIDTH), BF16), jax.ShapeDtypeStruct((s, SB_WIDTH), BF16),
                   jax.ShapeDtypeStruct((s, 2 * D_MODEL), BF16),
                   fshape(D_MODEL, D_MODEL), fshape(DIL_OUT_WIDTH, D_MODEL), fshape(SB_WIDTH, D_MODEL),
                   fshape(1, 2 * D_MODEL)],
        compiler_params=_cparams(1),
    )(dx1, o_a, o_b, gl, bg, w_ud, w_us, w_out)


_HBM = pl.BlockSpec(memory_space=pltpu.HBM)
_MESH = pl.DeviceIdType.MESH


def _all_gather(shards):
    n = len(shards)

    def body(*refs):
        x_refs, out_refs = refs[:n], refs[n:2 * n]
        send_sems, recv_sems, local_sems = refs[2 * n:]
        x, y, c = lax.axis_index("x"), lax.axis_index("y"), lax.axis_index("c")
        me, sibling = (x, y, c), (x, y, 1 - c)
        chips = [(1 - x, y), (x, 1 - y), (1 - x, 1 - y)]

        def slot(a, px, py, pc):
            return out_refs[a].at[4 * px + 2 * py + pc]

        def copy(a, k, block, to, own=False):
            return pltpu.make_async_remote_copy(
                src_ref=x_refs[a] if own else slot(a, *block), dst_ref=slot(a, *block),
                send_sem=send_sems.at[7 * a + k], recv_sem=recv_sems.at[7 * a + k], device_id=to, device_id_type=_MESH)

        mine = [pltpu.make_async_copy(x_refs[a], slot(a, *me), local_sems.at[a]) for a in range(n)]
        for cp in mine:
            cp.start()
        first = []
        for a in range(n):
            first.append(copy(a, 0, me, sibling, own=True))
            first += [copy(a, 1 + j, me, (*chip, c), own=True) for j, chip in enumerate(chips)]
        for cp in first:
            cp.start()
        passed = []
        for a in range(n):
            for j, chip in enumerate(chips):
                copy(a, 1 + j, (*chip, c), me).wait_recv()
                passed.append(copy(a, 4 + j, (*chip, c), sibling))
                passed[-1].start()
        for a in range(n):
            copy(a, 0, sibling, me).wait_recv()
            for j, chip in enumerate(chips):
                copy(a, 4 + j, (*chip, 1 - c), me).wait_recv()
        for cp in first + passed:
            cp.wait_send()
        for cp in mine:
            cp.wait()

    return _pcall(
        body,
        name="all_gather_weights",
        in_specs=[_HBM] * n,
        out_specs=[_HBM] * n,
        out_shape=[jax.ShapeDtypeStruct((N_DEV,) + s.shape, s.dtype) for s in shards],
        scratch_shapes=[pltpu.SemaphoreType.DMA((7 * n,)), pltpu.SemaphoreType.DMA((7 * n,)),
                        pltpu.SemaphoreType.DMA((n,))],
    )(*shards)


def _exchange(chunks):
    n = len(chunks)

    def body(*refs):
        g_refs, o_refs = refs[:n], refs[n:2 * n]
        send_sems, recv_sems, local_sems = refs[2 * n:]
        x, y, c = lax.axis_index("x"), lax.axis_index("y"), lax.axis_index("c")
        me = 4 * x + 2 * y + c
        own = [pltpu.make_async_copy(g_refs[a].at[me], o_refs[a].at[me], local_sems.at[a]) for a in range(n)]
        for cp in own:
            cp.start()
        copies = []
        for a in range(n):
            for k in range(1, N_DEV):
                px, py, pc = x ^ (k >> 2), y ^ ((k >> 1) & 1), c ^ (k & 1)
                peer = 4 * px + 2 * py + pc
                copies.append(pltpu.make_async_remote_copy(
                    src_ref=g_refs[a].at[peer], dst_ref=o_refs[a].at[me], send_sem=send_sems.at[7 * a + k - 1],
                    recv_sem=recv_sems.at[7 * a + k - 1], device_id=(px, py, pc), device_id_type=_MESH))
        for cp in copies:
            cp.start()
        for cp in copies:
            cp.wait()
        for cp in own:
            cp.wait()

    return _pcall(
        body,
        name="exchange_grads",
        in_specs=[_HBM] * n,
        out_specs=[_HBM] * n,
        out_shape=[jax.ShapeDtypeStruct(g.shape, g.dtype) for g in chunks],
        scratch_shapes=[pltpu.SemaphoreType.DMA((7 * n,)), pltpu.SemaphoreType.DMA((7 * n,)),
                        pltpu.SemaphoreType.DMA((n,))],
    )(*chunks)


_SEM = pl.BlockSpec(memory_space=pltpu.SEMAPHORE)
_EFFECT = pltpu.SideEffectType.DATAFLOW_SIDE_EFFECTING


def _peers(x, y, c):
    out = []
    for k in range(1, N_DEV):
        px, py, pc = x ^ (k >> 2), y ^ ((k >> 1) & 1), c ^ (k & 1)
        out.append(((px, py, pc), 4 * px + 2 * py + pc))
    return out


def _spread_copies(src_refs, land_refs, send_sems, recv_sems, chunked):
    x, y, c = lax.axis_index("x"), lax.axis_index("y"), lax.axis_index("c")
    me = 4 * x + 2 * y + c
    copies = []
    for a, (src, land) in enumerate(zip(src_refs, land_refs)):
        for k, (peer_id, peer) in enumerate(_peers(x, y, c)):
            copies.append(pltpu.make_async_remote_copy(
                src_ref=src.at[peer] if chunked else src, dst_ref=land.at[me], send_sem=send_sems.at[7 * a + k],
                recv_sem=recv_sems.at[7 * a + k], device_id=peer_id, device_id_type=_MESH))
    return copies


def _spread_start(name, srcs, chunked):
    n = len(srcs)
    lands = [lax.empty((N_DEV,) + (s.shape[1:] if chunked else s.shape), s.dtype) for s in srcs]

    def body(*refs):
        src_refs, land_refs = refs[:n], refs[n:2 * n]
        send_sems, recv_sems = refs[2 * n], refs[2 * n + 1]
        token = refs[-1]
        for cp in _spread_copies(src_refs, land_refs, send_sems, recv_sems, chunked):
            cp.start()
        token[...] = jnp.zeros_like(token)

    hbm = lambda a: pltpu.HBM(a.shape, a.dtype)
    outs = _pcall(
        body,
        name=name,
        out_shape=(pltpu.SemaphoreType.DMA((7 * n,)), pltpu.SemaphoreType.DMA((7 * n,)),
                   *[hbm(s) for s in srcs], *[hbm(l) for l in lands], jax.ShapeDtypeStruct((8, LANES), F32)),
        in_specs=[_HBM] * (2 * n),
        out_specs=(_SEM, _SEM, *([_HBM] * (2 * n)), pl.BlockSpec(memory_space=pltpu.VMEM)),
        input_output_aliases={i: 2 + i for i in range(2 * n)},
        compiler_params=pltpu.CompilerParams(has_side_effects=_EFFECT),
    )(*[pltpu.with_memory_space_constraint(a, pltpu.HBM) for a in list(srcs) + lands])
    return outs[0], outs[1], list(outs[2:2 + n]), list(outs[2 + n:2 + 2 * n]), outs[-1]


def _spread_wait(name, send_sems, recv_sems, srcs, lands, after, chunked):
    n = len(srcs)

    def body(*refs):
        src_refs, land_refs = refs[:n], refs[n:2 * n]
        for cp in _spread_copies(src_refs, land_refs, refs[2 * n], refs[2 * n + 1], chunked):
            cp.wait_send()
            cp.wait_recv()

    hbm = lambda a: pltpu.HBM(a.shape, a.dtype)
    outs = _pcall(
        body,
        name=name,
        out_shape=tuple(hbm(a) for a in list(srcs) + list(lands)),
        in_specs=[_HBM] * (2 * n) + [_SEM, _SEM, pl.BlockSpec(memory_space=pl.ANY)],
        out_specs=tuple([_HBM] * (2 * n)),
        input_output_aliases={i: i for i in range(2 * n)},
        compiler_params=pltpu.CompilerParams(has_side_effects=_EFFECT),
    )(*srcs, *lands, send_sems, recv_sems, after)
    return list(outs[:n]), list(outs[n:])


def _with_own(land, own):
    me = 4 * lax.axis_index("x") + 2 * lax.axis_index("y") + lax.axis_index("c")
    return lax.dynamic_update_slice(land, own[None], (me,) + (0,) * own.ndim)


def _reduce_adamw(name, parts, w, m, v, tr):
    _, rows, cols = parts.shape
    tr = min(tr, rows)
    assert rows % tr == 0
    c1 = 1.0 / (1.0 - ADAM_B1 ** ADAM_STEP)
    c2 = 1.0 / (1.0 - ADAM_B2 ** ADAM_STEP)

    def body(p_ref, w_ref, m_ref, v_ref, g_out, d_out, m_out, v_out):
        g = p_ref[0].astype(F32)
        for d in range(1, N_DEV):
            g = g + p_ref[d].astype(F32)
        mn = ADAM_B1 * m_ref[...] + (1.0 - ADAM_B1) * g
        vn = ADAM_B2 * v_ref[...] + (1.0 - ADAM_B2) * (g * g)
        g_out[...] = g
        m_out[...] = mn
        v_out[...] = vn
        d_out[...] = -ADAM_LR * ((mn * c1) / (jnp.sqrt(vn * c2) + ADAM_EPS) + ADAM_WD * w_ref[...])

    spec = pl.BlockSpec((tr, cols), lambda i: (i, 0))
    return _pcall(
        body,
        name=name,
        grid=(rows // tr,),
        in_specs=[pl.BlockSpec((N_DEV, tr, cols), lambda i: (0, i, 0)), spec, spec, spec],
        out_specs=[spec] * 4,
        out_shape=[jax.ShapeDtypeStruct((rows, cols), F32)] * 4,
        compiler_params=_cparams(1),
    )(parts, w, m, v)


_SHARDED = ("w_in", "w_up_dil", "w_up_sb", "w_out", "w_mlp_in", "w_mlp_out")
_FULL_SHAPES = {"w_in": (D_MODEL, IN_COLS), "w_up_dil": (DIL_OUT_WIDTH, D_MODEL), "w_up_sb": (SB_WIDTH, D_MODEL),
                "w_out": (D_MODEL, D_MODEL), "w_mlp_in": (D_MODEL, D_FF), "w_mlp_out": (D_FF, D_MODEL)}
_ROW_SHARDED = ("w_out", "w_mlp_out")


def _shard_shape(name):
    r, c = _FULL_SHAPES[name]
    return (r // N_DEV, c) if name in _ROW_SHARDED else (r, c // N_DEV)


def _assemble(name, gathered):
    r, c = _shard_shape(name)
    if name in _ROW_SHARDED:
        return gathered.reshape(N_DEV * r, c)
    return gathered.transpose(1, 0, 2).reshape(r, N_DEV * c)


def _chunk(name, full):
    r, c = _shard_shape(name)
    if name in _ROW_SHARDED:
        return full.reshape(N_DEV, r, c)
    return full.reshape(r, N_DEV, c).transpose(1, 0, 2)


_SMALL = (("norm_mix_g", D_MODEL), ("b_gate", 2 * D_MODEL), ("norm_mlp_g", D_MODEL), ("norm_final_g", D_MODEL))
_SMALL_N = sum(n for _, n in _SMALL) + LANES


def _pack_small(vals, tail):
    return jnp.concatenate([vals[n].reshape(1, -1) for n, _ in _SMALL] + [tail], axis=1)


def _unpack_small(vec, shapes):
    out, pos = {}, 0
    for n, width in _SMALL:
        out[n] = vec[:, pos:pos + width].reshape(shapes[n])
        pos += width
    return out, vec[:, pos:]


def kernel(x, norm_mix_g, w_in, b_gate, w_up_dil, w_up_sb, w_out, norm_mlp_g, w_mlp_in, w_mlp_out, norm_final_g, loss_target, m_norm_mix_g, m_w_in, m_b_gate, m_w_up_dil, m_w_up_sb, m_w_out, m_norm_mlp_g, m_w_mlp_in, m_w_mlp_out, m_norm_final_g, v_norm_mix_g, v_w_in, v_b_gate, v_w_up_dil, v_w_up_sb, v_w_out, v_norm_mlp_g, v_w_mlp_in, v_w_mlp_out, v_norm_final_g):
    given = dict(locals())
    s = x.shape[1]
    x0 = x.reshape(s, D_MODEL)
    target = loss_target.reshape(s, D_MODEL)
    g1 = norm_mix_g.reshape(1, D_MODEL)
    g2 = norm_mlp_g.reshape(1, D_MODEL)
    g3 = norm_final_g.reshape(1, D_MODEL)
    bg = b_gate.reshape(1, 2 * D_MODEL)
    w_shards = {n: given[n].reshape(_shard_shape(n)) for n in _SHARDED}
    m_shards = {n: given["m_" + n].reshape(_shard_shape(n)) for n in _SHARDED}
    v_shards = {n: given["v_" + n].reshape(_shard_shape(n)) for n in _SHARDED}

    shard_b = {n: w_shards[n].astype(BF16) for n in _SHARDED}
    (gathered_w_in,) = _all_gather([shard_b["w_in"]])
    w_in_f = _assemble("w_in", gathered_w_in)
    w_qkv, w_gl = w_in_f[:, :QKV_COLS], w_in_f[:, QKV_COLS:]
    full = {}

    def norm1(_, rows, consts):
        _, xh = _rms_stats(rows[0])
        return [xh * consts[0]], []

    (h1,) = _rowk("norm_mix", tm=512, rows=[x0], consts=[g1], row_outs=[(D_MODEL, BF16)], epilogue=norm1)
    qkv, (land,) = _mm("proj_qkv", h1, w_qkv, out_dtype=BF16, tm=1024, tn=768, tk=D_MODEL,
                       rider=_Spread([shard_b["w_mlp_in"]], chunked=False))
    full["w_mlp_in"] = _assemble("w_mlp_in", land)
    gl = _mm("proj_gates", h1, w_gl, out_dtype=F32, tm=512, tn=2048, tk=D_MODEL)
    dil = [_dil_fwd(qkv, g) for g in range(len(DIL_GROUPS))]
    os_, lses = [d[0] for d in dil], [d[1] for d in dil]
    o_a = _dil_mix_fwd(os_, lses, 512)
    riding = ("w_mlp_out", "w_out", "w_up_sb", "w_up_dil")
    (o_b, tot_b, sb_steps), lands = _sb_fwd(qkv, rider=_Spread([shard_b[n] for n in riding], chunked=False))
    full.update({n: _assemble(n, land) for n, land in zip(riding, lands)})
    x1, h2 = _mixer_fwd(o_a, o_b, gl, x0, bg, g2, full["w_up_dil"], full["w_up_sb"], full["w_out"], 256)
    f = _mm("mlp_in", h2, full["w_mlp_in"], out_dtype=BF16, tm=1024, tn=1024, tk=D_MODEL,
            epilogue=lambda r, _: jnp.square(jnp.maximum(r, 0.0)))

    def head(acc, rows, consts):
        x1v, tv = rows
        g3v = consts[0]
        x2 = x1v + acc
        r, xh = _rms_stats(x2)
        diff = xh * g3v - tv
        loss = (0.5 / D_MODEL) * jnp.sum(jnp.sum(diff * diff, axis=0, keepdims=True), axis=1, keepdims=True)
        dy = diff * (1.0 / D_MODEL)
        dx2, dg = _rms_bwd(dy, xh, r, g3v)
        return [dx2, dx2], [dg, jnp.broadcast_to(loss, (1, LANES))]

    dx2, dx2b, gg3, loss_part = _rowk(
        "mlp_out_loss", a=f, w=full["w_mlp_out"], tm=512, tk=D_FF, rows=[x1, target], consts=[g3],
        row_outs=[(D_MODEL, F32), (D_MODEL, BF16)], acc_outs=[D_MODEL, LANES], epilogue=head)

    da = _mm("mlp_out_bwd", dx2b, full["w_mlp_out"], tb=True, out_dtype=BF16, tm=1024, tn=1024, tk=D_MODEL, extra=f,
             epilogue=lambda r, fv: r * (2.0 * jnp.sqrt(fv.astype(F32))))
    g_w_mlp_out = _mm("grad_w_mlp_out", f, dx2b, ta=True, out_dtype=F32, tm=1024, tn=1024, tk=2048)
    g_w_mlp_in = _mm("grad_w_mlp_in", h2, da, ta=True, out_dtype=F32, tm=1024, tn=1024, tk=2048)

    def norm_bwd(acc, rows, consts):
        xv, dres = rows
        r, xh = _rms_stats(xv)
        dx, dg = _rms_bwd(acc, xh, r, consts[0])
        return [dres + dx], [dg]

    dx1, gg2 = _rowk("mlp_in_bwd", a=da, w=full["w_mlp_in"], nt=True, tm=512, tk=D_FF, rows=[x1, dx2], consts=[g2],
                     row_outs=[(D_MODEL, F32)], acc_outs=[D_MODEL], epilogue=norm_bwd)
    do_a, do_b, dgl, g_w_out, g_w_ud, g_w_us, g_bg = _mixer_bwd(
        dx1, o_a, o_b, gl, bg, full["w_up_dil"], full["w_up_sb"], full["w_out"], 256)
    bchunk = lambda n, g: _chunk(n, g).astype(BF16)
    parts = {}
    mix = _dil_mix_bwd(do_a, os_, lses, 512)
    dil_b = [_dil_bwd(qkv, mix[0], lses[0], mix[3], 0)]
    small_three = {"w_out": g_w_out, "w_up_sb": g_w_us, "w_up_dil": g_w_ud}
    grads, lands = _dil_bwd(qkv, mix[1], lses[1], mix[4], 1,
                            rider=_Spread([bchunk(n, g) for n, g in small_three.items()], chunked=True))
    dil_b.append(grads)
    parts.update(dict(zip(small_three, lands)))
    grads, (parts["w_mlp_out"],) = _dil_bwd(qkv, mix[2], lses[2], mix[5], 2,
                                            rider=_Spread([bchunk("w_mlp_out", g_w_mlp_out)], chunked=True))
    dil_b.append(grads)
    dq_b, dk_b, dv_b = _sb_bwd(qkv, do_b, tot_b, sb_steps)
    dproj = jnp.concatenate(
        [d[0] for d in dil_b] + [d[1] for d in dil_b] + [d[2] for d in dil_b]
        + [dq_b, dk_b.astype(BF16), dv_b.astype(BF16), dgl], axis=1)
    g_w_in, (parts["w_mlp_in"],) = _mm("grad_w_in", h1, dproj, ta=True, out_dtype=F32, tm=512, tn=IN_COLS // 2, tk=1024,
                                       rider=_Spread([bchunk("w_mlp_in", g_w_mlp_in)], chunked=True))
    (grad_x, gg1), (parts["w_in"],) = _rowk(
        "in_proj_bwd", a=dproj, w=w_in_f, nt=True, tm=512, tk=IN_COLS, rows=[x0, dx1], consts=[g1],
        row_outs=[(D_MODEL, F32)], acc_outs=[D_MODEL], epilogue=norm_bwd,
        rider=_Spread([bchunk("w_in", g_w_in)], chunked=True))

    small_part = _pack_small({"norm_mix_g": gg1, "b_gate": g_bg, "norm_mlp_g": gg2, "norm_final_g": gg3}, loss_part)
    (small_parts,) = _exchange([jnp.broadcast_to(small_part[None], (N_DEV, 1, _SMALL_N))])

    tags = ("grad_", "delta_", "new_m_", "new_v_")
    outs = {}
    for n, p in parts.items():
        res = _reduce_adamw("adamw_" + n, p, w_shards[n], m_shards[n], v_shards[n], 128)
        for tag, val in zip(tags, res):
            outs[tag + n] = val.reshape(given[n].shape)
    small_w = _pack_small(given, jnp.zeros((1, LANES), F32))
    small_m = _pack_small({n: given["m_" + n] for n, _ in _SMALL}, jnp.zeros((1, LANES), F32))
    small_v = _pack_small({n: given["v_" + n] for n, _ in _SMALL}, jnp.ones((1, LANES), F32))
    small_res = _reduce_adamw("adamw_replicated", small_parts, small_w, small_m, small_v, 8)

    small_shapes = {n: given[n].shape for n, _ in _SMALL}
    for tag, small in zip(tags, small_res):
        small_vals, tail = _unpack_small(small, small_shapes)
        for n, val in small_vals.items():
            outs[tag + n] = val
        if tag == "grad_":
            loss = tail[0, 0]
    names = ["norm_mix_g", "w_in", "b_gate", "w_up_dil", "w_up_sb", "w_out", "norm_mlp_g", "w_mlp_in", "w_mlp_out",
             "norm_final_g"]
    return (loss, grad_x.reshape(x.shape), *[outs["grad_" + n] for n in names], *[outs["delta_" + n] for n in names],
            *[outs["new_m_" + n] for n in names], *[outs["new_v_" + n] for n in names])
```

```python
import functools
import math

import jax
import jax.numpy as jnp
from jax import lax
from jax.experimental import pallas as pl
from jax.experimental.pallas import tpu as pltpu

_pcall = pl.pallas_call

F32 = jnp.float32
BF16 = jnp.bfloat16

D_MODEL = 1024
HEAD_DIM = 64
DIL_GROUPS = ((128, 1), (512, 4), (2048, 16))
DIL_HEADS_PER_GROUP = 4
N_DIL_HEADS = 12
N_SB_HEADS = 8
DIL_WIDTH = 768
DIL_OUT_WIDTH = 256
SB_WIDTH = 512
D_FF = 4096
BLOCK = 128
RMS_EPS = 1e-6
NEG_INF = -1e30
QKV_COLS = 3 * DIL_WIDTH + 3 * SB_WIDTH
IN_COLS = QKV_COLS + 2 * D_MODEL
N_DEV = 8

ADAM_LR = 0.001
ADAM_B1 = 0.9
ADAM_B2 = 0.999
ADAM_EPS = 1e-08
ADAM_WD = 0.01
ADAM_STEP = 10

VMEM_LIMIT = 56 * 1024 * 1024
SB_TK = 256
SB_TQ_FWD = 512
SB_TQ_BWD = 256
LANES = 128

_ARB = pltpu.ARBITRARY


def _cparams(n_axes, **kw):
    return pltpu.CompilerParams(dimension_semantics=(_ARB,) * n_axes, vmem_limit_bytes=VMEM_LIMIT, **kw)


def _dot(a, b):
    return jnp.dot(a, b, preferred_element_type=F32)


def _dot_nt(a, b):
    return lax.dot_general(a, b, (((1,), (1,)), ((), ())), preferred_element_type=F32)


def _dot_tn(a, b):
    return lax.dot_general(a, b, (((0,), (0,)), ((), ())), preferred_element_type=F32)


def _split_hi_lo(x):
    hi = x.astype(BF16)
    lo = (x - hi.astype(F32)).astype(BF16)
    return hi, lo


def _dot_hi_lo(x, m):
    hi, lo = _split_hi_lo(x)
    return _dot(hi, m) + _dot(lo, m)


def _sigmoid(x):
    return 1.0 / (1.0 + jnp.exp(-x))


_HBM = pl.BlockSpec(memory_space=pltpu.HBM)
_MESH = pl.DeviceIdType.MESH


class _Spread:
    def __init__(self, srcs, chunked):
        self.srcs, self.chunked, self.n = list(srcs), chunked, len(srcs)

    def land_shapes(self):
        return [jax.ShapeDtypeStruct((N_DEV,) + (s.shape[1:] if self.chunked else s.shape), s.dtype) for s in self.srcs]

    def scratch(self):
        dma = pltpu.SemaphoreType.DMA
        return [dma((7 * self.n,)), dma((7 * self.n,)), dma((self.n,))]

    def copies(self, src_refs, land_refs, send_sems, recv_sems, local_sems):
        x, y, c = lax.axis_index("x"), lax.axis_index("y"), lax.axis_index("c")
        me = 4 * x + 2 * y + c
        out = []
        for a, (src, land) in enumerate(zip(src_refs, land_refs)):
            out.append(pltpu.make_async_copy(src.at[me] if self.chunked else src, land.at[me], local_sems.at[a]))
            for k in range(1, N_DEV):
                px, py, pc = x ^ (k >> 2), y ^ ((k >> 1) & 1), c ^ (k & 1)
                out.append(pltpu.make_async_remote_copy(
                    src_ref=src.at[4 * px + 2 * py + pc] if self.chunked else src, dst_ref=land.at[me],
                    send_sem=send_sems.at[7 * a + k - 1], recv_sem=recv_sems.at[7 * a + k - 1],
                    device_id=(px, py, pc), device_id_type=_MESH))
        return out


def _call(body, args, rider=None, **kw):
    if rider is None:
        return _pcall(body, **kw)(*args)
    grid = kw["grid"]
    single = not isinstance(kw["out_shape"], (list, tuple))
    out_specs = [kw["out_specs"]] if single else list(kw["out_specs"])
    out_shape = [kw["out_shape"]] if single else list(kw["out_shape"])
    in_specs, scratch = list(kw["in_specs"]), list(kw.get("scratch_shapes", []))
    n_in, n_out, n_s, n = len(in_specs), len(out_shape), len(scratch), rider.n

    def hosted(*refs):
        ins, srcs = refs[:n_in], refs[n_in:n_in + n]
        outs, lands = refs[n_in + n:n_in + n + n_out], refs[n_in + n + n_out:n_in + 2 * n + n_out]
        own_scratch, sems = refs[n_in + 2 * n + n_out:n_in + 2 * n + n_out + n_s], refs[n_in + 2 * n + n_out + n_s:]
        ids = [pl.program_id(d) for d in range(len(grid))]
        first = functools.reduce(jnp.logical_and, [i == 0 for i in ids])
        last = functools.reduce(jnp.logical_and, [i == g - 1 for i, g in zip(ids, grid)])
        copies = rider.copies(srcs, lands, *sems)

        @pl.when(first)
        def _():
            for cp in copies:
                cp.start()

        body(*ins, *outs, *own_scratch)

        @pl.when(last)
        def _():
            for cp in copies:
                cp.wait()

    kw = dict(kw, in_specs=in_specs + [_HBM] * n, out_specs=out_specs + [_HBM] * n,
              out_shape=out_shape + rider.land_shapes(), scratch_shapes=scratch + rider.scratch())
    res = _pcall(hosted, **kw)(*args, *rider.srcs)
    return (res[0] if single else list(res[:n_out])), list(res[n_out:])


def _mm(name, a, b, *, ta=False, tb=False, out_dtype, tm, tn, tk, epilogue=None, extra=None, rider=None):
    m = a.shape[1] if ta else a.shape[0]
    k = a.shape[0] if ta else a.shape[1]
    n = b.shape[0] if tb else b.shape[1]
    assert (b.shape[1] if tb else b.shape[0]) == k
    tm, tn, tk = min(tm, m), min(tn, n), min(tk, k)
    assert m % tm == 0 and n % tn == 0 and k % tk == 0, (name, m, n, k, tm, tn, tk)
    nk = k // tk
    dn = (((0 if ta else 1,), (1 if tb else 0,)), ((), ()))
    in_place = nk > 1 and epilogue is None and out_dtype == F32

    def body(*refs):
        if extra is not None:
            a_ref, b_ref, e_ref, o_ref = refs[:4]
        else:
            a_ref, b_ref, o_ref = refs[:3]
            e_ref = None

        def finish(r):
            if epilogue is not None:
                r = epilogue(r, None if e_ref is None else e_ref[...])
            o_ref[...] = r.astype(out_dtype)

        part = lax.dot_general(a_ref[...].astype(BF16), b_ref[...].astype(BF16), dn, preferred_element_type=F32)
        if nk == 1:
            finish(part)
        else:
            acc_ref = o_ref if in_place else refs[-1]
            kk = pl.program_id(2)

            @pl.when(kk == 0)
            def _():
                acc_ref[...] = part

            @pl.when(kk > 0)
            def _():
                acc_ref[...] += part

            if not in_place:

                @pl.when(kk == nk - 1)
                def _():
                    finish(acc_ref[...])

    a_spec = pl.BlockSpec((tk, tm), lambda j, i, kk: (kk, i)) if ta else pl.BlockSpec((tm, tk), lambda j, i, kk: (i, kk))
    b_spec = pl.BlockSpec((tn, tk), lambda j, i, kk: (j, kk)) if tb else pl.BlockSpec((tk, tn), lambda j, i, kk: (kk, j))
    o_spec = pl.BlockSpec((tm, tn), lambda j, i, kk: (i, j))
    in_specs = [a_spec, b_spec]
    args = [a, b]
    if extra is not None:
        in_specs.append(o_spec)
        args.append(extra)
    return _call(
        body, args, rider,
        name=name,
        grid=(n // tn, m // tm, nk),
        in_specs=in_specs,
        out_specs=o_spec,
        out_shape=jax.ShapeDtypeStruct((m, n), out_dtype),
        scratch_shapes=[pltpu.VMEM((tm, tn), F32)] if (nk > 1 and not in_place) else [],
        compiler_params=_cparams(3),
    )


def _rowk(name, *, a=None, w=None, nt=False, tm, tk=None, rows=(), consts=(), row_outs=(), acc_outs=(), epilogue,
          rider=None):
    has_mm = a is not None
    m = a.shape[0] if has_mm else rows[0].shape[0]
    assert m % tm == 0
    nm = m // tm
    if has_mm:
        k = a.shape[1]
        n = w.shape[0] if nt else w.shape[1]
        tk = min(tk, k)
        assert k % tk == 0
        nk = k // tk
    else:
        nk = 1
    n_rows, n_consts, n_ro, n_ao = len(rows), len(consts), len(row_outs), len(acc_outs)

    def body(*refs):
        pos = 0
        if has_mm:
            a_ref, w_ref = refs[0], refs[1]
            pos = 2
        row_refs = refs[pos:pos + n_rows]
        pos += n_rows
        const_refs = refs[pos:pos + n_consts]
        pos += n_consts
        ro_refs = refs[pos:pos + n_ro]
        pos += n_ro
        ao_refs = refs[pos:pos + n_ao]
        pos += n_ao
        i = pl.program_id(0)
        kk = pl.program_id(1)

        def finish(acc):
            ro_vals, ao_vals = epilogue(acc, [r[...] for r in row_refs], [c[...] for c in const_refs])
            for r, v in zip(ro_refs, ro_vals):
                r[...] = v.astype(r.dtype)
            for r, v in zip(ao_refs, ao_vals):

                @pl.when(i == 0)
                def _(r=r, v=v):
                    r[...] = v

                @pl.when(i > 0)
                def _(r=r, v=v):
                    r[...] += v

        if not has_mm:
            finish(None)
            return
        if nt:
            part = _dot_nt(a_ref[...].astype(BF16), w_ref[...])
        else:
            part = _dot(a_ref[...].astype(BF16), w_ref[...])
        if nk == 1:
            finish(part)
        else:
            acc_ref = refs[pos]

            @pl.when(kk == 0)
            def _():
                acc_ref[...] = part

            @pl.when(kk > 0)
            def _():
                acc_ref[...] += part

            @pl.when(kk == nk - 1)
            def _():
                finish(acc_ref[...])

    once = pl.Buffered(1)
    in_specs, args = [], []
    if has_mm:
        in_specs.append(pl.BlockSpec((tm, tk), lambda i, kk: (i, kk)))
        w_mode = once if nk == 1 else None
        in_specs.append(pl.BlockSpec((n, tk), lambda i, kk: (0, kk), pipeline_mode=w_mode) if nt
                        else pl.BlockSpec((tk, n), lambda i, kk: (kk, 0), pipeline_mode=w_mode))
        args += [a, w]
    for r in rows:
        in_specs.append(pl.BlockSpec((tm, r.shape[1]), lambda i, kk: (i, 0)))
        args.append(r)
    for c in consts:
        in_specs.append(pl.BlockSpec(c.shape, lambda i, kk: (0,) * c.ndim, pipeline_mode=once))
        args.append(c)
    out_specs, out_shape = [], []
    for width, dt in row_outs:
        out_specs.append(pl.BlockSpec((tm, width), lambda i, kk: (i, 0)))
        out_shape.append(jax.ShapeDtypeStruct((m, width), dt))
    for width in acc_outs:
        out_specs.append(pl.BlockSpec((1, width), lambda i, kk: (0, 0)))
        out_shape.append(jax.ShapeDtypeStruct((1, width), F32))
    return _call(
        body, args, rider,
        name=name,
        grid=(nm, nk),
        in_specs=in_specs,
        out_specs=out_specs,
        out_shape=out_shape,
        scratch_shapes=[pltpu.VMEM((tm, n), F32)] if (has_mm and nk > 1) else [],
        compiler_params=_cparams(2),
    )


def _rms_stats(x):
    r = lax.rsqrt(jnp.mean(x * x, axis=-1, keepdims=True) + RMS_EPS)
    return r, x * r


def _rms_bwd(dh, xh, r, g):
    gy = dh * g
    dx = r * (gy - xh * jnp.mean(gy * xh, axis=-1, keepdims=True))
    return dx, jnp.sum(dh * xh, axis=0, keepdims=True)


def _alibi_slope(head):
    return 2.0 ** (-8.0 * (head + 1) / N_DIL_HEADS)


DIL_STEP_BLOCKS = 4


def _dil_band(first_block):
    qi = lax.broadcasted_iota(jnp.int32, (BLOCK, 2 * BLOCK), 0)
    kj = lax.broadcasted_iota(jnp.int32, (BLOCK, 2 * BLOCK), 1)
    steps = qi + BLOCK - kj
    valid = (steps >= 0) & (steps <= BLOCK)
    if first_block is not False:
        valid = valid & ((kj >= BLOCK) | jnp.logical_not(first_block))
    return steps.astype(F32), valid


def _dil_step_specs(ncb, cols, nblk, clamp):
    def own(col):
        return pl.BlockSpec((nblk * BLOCK, DIL_OUT_WIDTH), lambda r, i: (clamp(i), r * ncb + col))

    def before(col):
        return pl.BlockSpec((BLOCK, DIL_OUT_WIDTH), lambda r, i: (jnp.maximum(clamp(i) * nblk - 1, 0), r * ncb + col))

    return [own(cols[0]), own(cols[1]), before(cols[1]), own(cols[2]), before(cols[2])]


def _dil_fwd(qkv, group):
    window, dilation = DIL_GROUPS[group]
    s = qkv.shape[0]
    sub = s // dilation
    nb = sub // BLOCK
    assert nb * BLOCK * dilation == s and window // dilation == BLOCK
    nblk = min(DIL_STEP_BLOCKS, nb)
    assert nb % nblk == 0
    slopes = [_alibi_slope(group * DIL_HEADS_PER_GROUP + h) * dilation for h in range(DIL_HEADS_PER_GROUP)]

    def body(q_ref, kc_ref, kp_ref, vc_ref, vp_ref, o_ref, lse_ref):
        i = pl.program_id(1)
        kk_all = jnp.concatenate([kp_ref[...], kc_ref[...]], axis=0)
        vv_all = jnp.concatenate([vp_ref[...], vc_ref[...]], axis=0)
        head_id = lax.broadcasted_iota(jnp.int32, (1, DIL_OUT_WIDTH), 1) // HEAD_DIM
        chains = [(b, h) for b in range(nblk) for h in range(DIL_HEADS_PER_GROUP)]
        rows = lambda b: slice(b * BLOCK, (b + 1) * BLOCK)
        keys = lambda b: slice(b * BLOCK, (b + 2) * BLOCK)
        bands = [_dil_band(i == 0 if b == 0 else False) for b in range(nblk)]
        qs = [q_ref[rows(b), :] for b in range(nblk)]
        scores = [_dot_nt(jnp.where(head_id == h, qs[b], jnp.zeros_like(qs[b])), kk_all[keys(b)]) for b, h in chains]
        ps, lses = [], []
        for (b, h), sc in zip(chains, scores):
            steps, valid = bands[b]
            logits = jnp.where(valid, sc * (1.0 / math.sqrt(HEAD_DIM)) - slopes[h] * steps, NEG_INF)
            mx = jnp.max(logits, axis=1, keepdims=True)
            e = jnp.exp(logits - mx)
            den = jnp.sum(e, axis=1, keepdims=True)
            lses.append(mx + jnp.log(den))
            ps.append((e * (1.0 / den)).astype(BF16))
        outs = [_dot(p, vv_all[keys(b)]) for (b, h), p in zip(chains, ps)]
        for b in range(nblk):
            mine = [n for n, ch in enumerate(chains) if ch[0] == b]
            o, lse_all = outs[mine[0]], lses[mine[0]]
            for n in mine[1:]:
                o = jnp.where(head_id == chains[n][1], outs[n], o)
                lse_all = jnp.where(head_id == chains[n][1], lses[n], lse_all)
            o_ref[rows(b), :] = o
            lse_ref[rows(b), :] = jnp.broadcast_to(lse_all, o.shape)

    qkv_v, ncb, cols = _dil_view(qkv, group)
    out_spec = pl.BlockSpec((nblk * BLOCK, DIL_OUT_WIDTH), lambda r, i: (i, r))
    o, lse = _pcall(
        body,
        name=f"dil_fwd_g{group}",
        grid=(dilation, nb // nblk),
        in_specs=_dil_step_specs(ncb, cols, nblk, lambda i: i),
        out_specs=[out_spec, out_spec],
        out_shape=[jax.ShapeDtypeStruct((sub, dilation * DIL_OUT_WIDTH), F32)] * 2,
        compiler_params=_cparams(2),
    )(qkv_v, qkv_v, qkv_v, qkv_v, qkv_v)
    return o.reshape(s, DIL_OUT_WIDTH), lse.reshape(s, DIL_OUT_WIDTH)


def _dil_bwd(qkv, do_g, lse_g, dterm_g, group, rider=None):
    window, dilation = DIL_GROUPS[group]
    s = qkv.shape[0]
    sub = s // dilation
    nb = sub // BLOCK
    nblk = min(DIL_STEP_BLOCKS, nb)
    n_steps = nb // nblk
    slopes = [_alibi_slope(group * DIL_HEADS_PER_GROUP + h) * dilation for h in range(DIL_HEADS_PER_GROUP)]
    scale = 1.0 / math.sqrt(HEAD_DIM)
    tail = slice((nblk - 1) * BLOCK, nblk * BLOCK)

    def body(q_ref, kc_ref, kp_ref, vc_ref, vp_ref, do_ref, lse_ref, dt_ref, dq_ref, dk_ref, dv_ref, ck_ref, cv_ref):
        i = pl.program_id(1)

        @pl.when(i == 0)
        def _():
            ck_ref[...] = jnp.zeros_like(ck_ref)
            cv_ref[...] = jnp.zeros_like(cv_ref)

        @pl.when(i < n_steps)
        def _():
            kk_all = jnp.concatenate([kp_ref[...], kc_ref[...]], axis=0)
            vv_all = jnp.concatenate([vp_ref[...], vc_ref[...]], axis=0)
            lane = lax.broadcasted_iota(jnp.int32, (1, DIL_OUT_WIDTH), 1)
            head_id = lane // HEAD_DIM
            chains = [(b, h) for b in range(nblk) for h in range(DIL_HEADS_PER_GROUP)]
            rows = lambda b: slice(b * BLOCK, (b + 1) * BLOCK)
            keys = lambda b: slice(b * BLOCK, (b + 2) * BLOCK)
            bands = [_dil_band(i == 0 if b == 0 else False) for b in range(nblk)]
            qms, doms = [], []
            for b, h in chains:
                q, do = q_ref[rows(b), :], do_ref[rows(b), :]
                qms.append(jnp.where(head_id == h, q, jnp.zeros_like(q)))
                doms.append(jnp.where(head_id == h, do, jnp.zeros_like(do)))
            scores = [_dot_nt(qm, kk_all[keys(b)]) for (b, h), qm in zip(chains, qms)]
            dps = [_dot_nt(dom, vv_all[keys(b)]) for (b, h), dom in zip(chains, doms)]
            pbs, dss = [], []
            for n, (b, h) in enumerate(chains):
                steps, valid = bands[b]
                first = lane == h * HEAD_DIM
                lse = jnp.sum(jnp.where(first, lse_ref[rows(b), :], 0.0), axis=1, keepdims=True)
                dt = jnp.sum(jnp.where(first, dt_ref[rows(b), :], 0.0), axis=1, keepdims=True)
                logits = jnp.where(valid, scores[n] * scale - slopes[h] * steps, NEG_INF)
                p = jnp.where(valid, jnp.exp(logits - lse), 0.0)
                pbs.append(p.astype(BF16))
                dss.append((p * (dps[n] + dt) * scale).astype(BF16))
            dqs = [_dot(ds, kk_all[keys(b)]) for (b, h), ds in zip(chains, dss)]
            dks = [_dot_tn(ds, qm) for ds, qm in zip(dss, qms)]
            dvs = [_dot_tn(pb, dom) for pb, dom in zip(pbs, doms)]
            dkk, dvv = [], []
            for b in range(nblk):
                mine = [n for n, ch in enumerate(chains) if ch[0] == b]
                dq = dqs[mine[0]]
                for n in mine[1:]:
                    dq = jnp.where(head_id == chains[n][1], dqs[n], dq)
                dq_ref[rows(b), :] = dq.astype(dq_ref.dtype)
                dkk.append((dks[mine[0]] + dks[mine[1]]) + (dks[mine[2]] + dks[mine[3]]))
                dvv.append((dvs[mine[0]] + dvs[mine[1]]) + (dvs[mine[2]] + dvs[mine[3]]))
            for out_ref, carry_ref, parts in ((dk_ref, ck_ref, dkk), (dv_ref, cv_ref, dvv)):
                if nblk > 1:
                    out_ref[: (nblk - 1) * BLOCK, :] = carry_ref[: (nblk - 1) * BLOCK, :].astype(out_ref.dtype)
                out_ref[tail, :] = (carry_ref[tail, :] + parts[0][:BLOCK]).astype(out_ref.dtype)
                for b in range(nblk):
                    own = parts[b][BLOCK:]
                    carry_ref[rows(b), :] = own + parts[b + 1][:BLOCK] if b + 1 < nblk else own

        @pl.when(i == n_steps)
        def _():
            dk_ref[...] = ck_ref[...].astype(dk_ref.dtype)
            dv_ref[...] = cv_ref[...].astype(dv_ref.dtype)

    clamp = lambda i: jnp.minimum(i, n_steps - 1)
    qkv_v, ncb, cols = _dil_view(qkv, group)
    view = lambda t: t.reshape(sub, dilation * DIL_OUT_WIDTH)
    row_spec = pl.BlockSpec((nblk * BLOCK, DIL_OUT_WIDTH), lambda r, i: (clamp(i), r))
    late_spec = pl.BlockSpec((nblk * BLOCK, DIL_OUT_WIDTH), lambda r, i: (jnp.maximum(i - 1, 0), r))
    res = _call(
        body, (qkv_v, qkv_v, qkv_v, qkv_v, qkv_v, view(do_g), view(lse_g), view(dterm_g)), rider,
        name=f"dil_bwd_g{group}",
        grid=(dilation, n_steps + 1),
        in_specs=_dil_step_specs(ncb, cols, nblk, clamp) + [row_spec, row_spec, row_spec],
        out_specs=[row_spec, late_spec, late_spec],
        out_shape=[jax.ShapeDtypeStruct((sub, dilation * DIL_OUT_WIDTH), BF16)] * 3,
        scratch_shapes=[pltpu.VMEM((nblk * BLOCK, DIL_OUT_WIDTH), F32)] * 2,
        compiler_params=_cparams(2),
    )
    grads, lands = res if rider is not None else (res, None)
    grads = tuple(g.reshape(s, DIL_OUT_WIDTH) for g in grads)
    return grads if rider is None else (grads, lands)


def _dil_masks(i):
    qi = lax.broadcasted_iota(jnp.int32, (BLOCK, 2 * BLOCK), 0)
    kj = lax.broadcasted_iota(jnp.int32, (BLOCK, 2 * BLOCK), 1)
    steps = qi + BLOCK - kj
    valid = (steps >= 0) & (steps <= BLOCK) & ((kj >= BLOCK) | (i > 0))
    return steps.astype(F32), valid


def _dil_view(qkv, group):
    _, dilation = DIL_GROUPS[group]
    if dilation == 1:
        return qkv, QKV_COLS // DIL_OUT_WIDTH, (group, 3 + group, 6 + group)
    w = DIL_OUT_WIDTH
    own = jnp.concatenate([qkv[:, (3 * part + group) * w:(3 * part + group + 1) * w] for part in range(3)], axis=1)
    return own.reshape(qkv.shape[0] // dilation, dilation * 3 * w), 3, (0, 1, 2)


def _dil_specs(ncb, cols, clamp):
    def cur(col):
        return pl.BlockSpec((BLOCK, DIL_OUT_WIDTH), lambda r, i: (clamp(i), r * ncb + col))

    def prev(col):
        return pl.BlockSpec((BLOCK, DIL_OUT_WIDTH), lambda r, i: (jnp.maximum(clamp(i) - 1, 0), r * ncb + col))

    return [cur(cols[0]), cur(cols[1]), prev(cols[1]), cur(cols[2]), prev(cols[2])]


def _dil_fwd_one(qkv, group):
    window, dilation = DIL_GROUPS[group]
    s = qkv.shape[0]
    sub = s // dilation
    nb = sub // BLOCK
    assert nb * BLOCK * dilation == s and window // dilation == BLOCK
    slopes = [_alibi_slope(group * DIL_HEADS_PER_GROUP + h) * dilation for h in range(DIL_HEADS_PER_GROUP)]

    def body(q_ref, kc_ref, kp_ref, vc_ref, vp_ref, o_ref, lse_ref):
        i = pl.program_id(1)
        q = q_ref[...]
        kk = jnp.concatenate([kp_ref[...], kc_ref[...]], axis=0)
        vv = jnp.concatenate([vp_ref[...], vc_ref[...]], axis=0)
        head_id = lax.broadcasted_iota(jnp.int32, (1, DIL_OUT_WIDTH), 1) // HEAD_DIM
        steps, valid = _dil_masks(i)
        heads = range(DIL_HEADS_PER_GROUP)
        scores = [_dot_nt(jnp.where(head_id == h, q, jnp.zeros_like(q)), kk) for h in heads]
        ps, lses = [], []
        for h in heads:
            logits = scores[h] * (1.0 / math.sqrt(HEAD_DIM)) - slopes[h] * steps
            logits = jnp.where(valid, logits, NEG_INF)
            mx = jnp.max(logits, axis=1, keepdims=True)
            e = jnp.exp(logits - mx)
            den = jnp.sum(e, axis=1, keepdims=True)
            lses.append(mx + jnp.log(den))
            ps.append((e * (1.0 / den)).astype(BF16))
        outs = [_dot(ps[h], vv) for h in heads]
        o, lse_all = outs[0], lses[0]
        for h in heads[1:]:
            o = jnp.where(head_id == h, outs[h], o)
            lse_all = jnp.where(head_id == h, lses[h], lse_all)
        o_ref[...] = o
        lse_ref[...] = jnp.broadcast_to(lse_all, o.shape)

    qkv_v, ncb, cols = _dil_view(qkv, group)
    out_spec = pl.BlockSpec((BLOCK, DIL_OUT_WIDTH), lambda r, i: (i, r))
    o, lse = _pcall(
        body,
        name=f"dil_fwd_g{group}",
        grid=(dilation, nb),
        in_specs=_dil_specs(ncb, cols, lambda i: i),
        out_specs=[out_spec, out_spec],
        out_shape=[jax.ShapeDtypeStruct((sub, dilation * DIL_OUT_WIDTH), F32)] * 2,
        compiler_params=_cparams(2),
    )(qkv_v, qkv_v, qkv_v, qkv_v, qkv_v)
    return o.reshape(s, DIL_OUT_WIDTH), lse.reshape(s, DIL_OUT_WIDTH)


def _dil_bwd_one(qkv, do_g, lse_g, dterm_g, group, rider=None):
    window, dilation = DIL_GROUPS[group]
    s = qkv.shape[0]
    sub = s // dilation
    nb = sub // BLOCK
    slopes = [_alibi_slope(group * DIL_HEADS_PER_GROUP + h) * dilation for h in range(DIL_HEADS_PER_GROUP)]
    scale = 1.0 / math.sqrt(HEAD_DIM)

    def body(q_ref, kc_ref, kp_ref, vc_ref, vp_ref, do_ref, lse_ref, dt_ref, dq_ref, dk_ref, dv_ref, ck_ref, cv_ref):
        i = pl.program_id(1)

        @pl.when(i == 0)
        def _():
            ck_ref[...] = jnp.zeros_like(ck_ref)
            cv_ref[...] = jnp.zeros_like(cv_ref)

        @pl.when(i < nb)
        def _():
            q = q_ref[...]
            do = do_ref[...]
            lse_all = lse_ref[...]
            dt_all = dt_ref[...]
            kk = jnp.concatenate([kp_ref[...], kc_ref[...]], axis=0)
            vv = jnp.concatenate([vp_ref[...], vc_ref[...]], axis=0)
            lane = lax.broadcasted_iota(jnp.int32, (1, DIL_OUT_WIDTH), 1)
            head_id = lane // HEAD_DIM
            steps, valid = _dil_masks(i)
            heads = range(DIL_HEADS_PER_GROUP)
            qms = [jnp.where(head_id == h, q, jnp.zeros_like(q)) for h in heads]
            doms = [jnp.where(head_id == h, do, jnp.zeros_like(do)) for h in heads]
            scores = [_dot_nt(qms[h], kk) for h in heads]
            dps = [_dot_nt(doms[h], vv) for h in heads]
            pbs, dss = [], []
            for h in heads:
                first = lane == h * HEAD_DIM
                lse = jnp.sum(jnp.where(first, lse_all, 0.0), axis=1, keepdims=True)
                dt = jnp.sum(jnp.where(first, dt_all, 0.0), axis=1, keepdims=True)
                logits = scores[h] * scale - slopes[h] * steps
                p = jnp.where(valid, jnp.exp(jnp.where(valid, logits, NEG_INF) - lse), 0.0)
                pbs.append(p.astype(BF16))
                dss.append((p * (dps[h] + dt) * scale).astype(BF16))
            dqs = [_dot(dss[h], kk) for h in heads]
            dks = [_dot_tn(dss[h], qms[h]) for h in heads]
            dvs = [_dot_tn(pbs[h], doms[h]) for h in heads]
            dq = dqs[0]
            for h in heads[1:]:
                dq = jnp.where(head_id == h, dqs[h], dq)
            dkk = (dks[0] + dks[1]) + (dks[2] + dks[3])
            dvv = (dvs[0] + dvs[1]) + (dvs[2] + dvs[3])
            dq_ref[...] = dq.astype(dq_ref.dtype)
            dk_ref[...] = (ck_ref[...] + dkk[:BLOCK]).astype(dk_ref.dtype)
            dv_ref[...] = (cv_ref[...] + dvv[:BLOCK]).astype(dv_ref.dtype)
            ck_ref[...] = dkk[BLOCK:]
            cv_ref[...] = dvv[BLOCK:]

        @pl.when(i == nb)
        def _():
            dk_ref[...] = ck_ref[...].astype(dk_ref.dtype)
            dv_ref[...] = cv_ref[...].astype(dv_ref.dtype)

    clamp = lambda i: jnp.minimum(i, nb - 1)
    qkv_v, ncb, cols = _dil_view(qkv, group)
    view = lambda t: t.reshape(sub, dilation * DIL_OUT_WIDTH)
    row_spec = pl.BlockSpec((BLOCK, DIL_OUT_WIDTH), lambda r, i: (clamp(i), r))
    late_spec = pl.BlockSpec((BLOCK, DIL_OUT_WIDTH), lambda r, i: (jnp.maximum(i - 1, 0), r))
    res = _call(
        body, (qkv_v, qkv_v, qkv_v, qkv_v, qkv_v, view(do_g), view(lse_g), view(dterm_g)), rider,
        name=f"dil_bwd_g{group}",
        grid=(dilation, nb + 1),
        in_specs=_dil_specs(ncb, cols, clamp) + [row_spec, row_spec, row_spec],
        out_specs=[row_spec, late_spec, late_spec],
        out_shape=[jax.ShapeDtypeStruct((sub, dilation * DIL_OUT_WIDTH), BF16)] * 3,
        scratch_shapes=[pltpu.VMEM((BLOCK, DIL_OUT_WIDTH), F32)] * 2,
        compiler_params=_cparams(2),
    )
    grads, lands = res if rider is not None else (res, None)
    grads = tuple(g.reshape(s, DIL_OUT_WIDTH) for g in grads)
    return grads if rider is None else (grads, lands)


def _head_block_ones():
    r = lax.broadcasted_iota(jnp.int32, (DIL_OUT_WIDTH, DIL_OUT_WIDTH), 0) // HEAD_DIM
    c = lax.broadcasted_iota(jnp.int32, (DIL_OUT_WIDTH, DIL_OUT_WIDTH), 1) // HEAD_DIM
    return jnp.where(r == c, 1.0, 0.0).astype(BF16)


def _dil_mix_weights(l0, l1, l2):
    mx = jnp.maximum(jnp.maximum(l0, l1), l2)
    e0, e1, e2 = jnp.exp(l0 - mx), jnp.exp(l1 - mx), jnp.exp(l2 - mx)
    inv = 1.0 / (e0 + e1 + e2)
    return e0 * inv, e1 * inv, e2 * inv


def _dil_mix_fwd(os_, lses, tm):
    def epi(_, rows, consts):
        o0, o1, o2, l0, l1, l2 = rows
        w0, w1, w2 = _dil_mix_weights(l0, l1, l2)
        return [w0 * o0 + w1 * o1 + w2 * o2], []

    (o_a,) = _rowk("dil_mix_fwd", tm=tm, rows=list(os_) + list(lses), row_outs=[(DIL_OUT_WIDTH, BF16)], epilogue=epi)
    return o_a


def _dil_mix_bwd(do_a, os_, lses, tm):
    def epi(_, rows, consts):
        do, o0, o1, o2, l0, l1, l2 = rows
        do = do.astype(F32)
        w0, w1, w2 = _dil_mix_weights(l0, l1, l2)
        mixed = w0 * o0 + w1 * o1 + w2 * o2
        tot = _dot_hi_lo(do * mixed, _head_block_ones())
        return [w0 * do, w1 * do, w2 * do, -w0 * tot, -w1 * tot, -w2 * tot], []

    return _rowk(
        "dil_mix_bwd", tm=tm, rows=[do_a] + list(os_) + list(lses),
        row_outs=[(DIL_OUT_WIDTH, BF16)] * 3 + [(DIL_OUT_WIDTH, F32)] * 3, epilogue=epi)


_SB_Q0 = 3 * DIL_WIDTH // LANES
_SB_K0 = _SB_Q0 + SB_WIDTH // LANES
_SB_V0 = _SB_K0 + SB_WIDTH // LANES


_EXP_CLAMP = 88.0
_SB_DEAD = 104.0


def _tri(t, op):
    r = lax.broadcasted_iota(jnp.int32, (t, t), 0)
    c = lax.broadcasted_iota(jnp.int32, (t, t), 1)
    return jnp.where(op(r, c), 1.0, 0.0).astype(BF16)


def _softplus(z):
    return jnp.maximum(z, jnp.log(1.0 + jnp.exp(jnp.minimum(z, _EXP_CLAMP))))


def _sb_chain_head(qm, kj, mask):
    z = _dot_nt(qm, kj)
    sp = _softplus(z)
    return (sp if mask is None else jnp.where(mask, sp, 0.0)), z - sp


def _sb_fwd(qkv, rider=None):
    s = qkv.shape[0]
    t = SB_TK
    assert s % (2 * t) == 0
    nq = s // (2 * t)
    n_pairs = SB_WIDTH // LANES

    def body(q_ref, k_ref, v_ref, o_ref, tot_ref, steps_ref):
        p, i = pl.program_id(0), pl.program_id(1)
        lane_hi = lax.broadcasted_iota(jnp.int32, (1, LANES), 1) // HEAD_DIM
        later = _tri(t, lambda r, c: r > c)
        causal = lax.broadcasted_iota(jnp.int32, (t, t), 1) < lax.broadcasted_iota(jnp.int32, (t, t), 0)
        qms = []
        for x in range(2):
            q = q_ref[pl.ds(x * t, t), :] * (1.0 / math.sqrt(HEAD_DIM))
            qms.append([jnp.where(lane_hi == hh, q, jnp.zeros_like(q)) for hh in range(2)])

        def tile(j):
            off = pl.multiple_of(j * t, t)
            return k_ref[pl.ds(off, t), :], v_ref[pl.ds(off, t), :]

        def step(tiles, carry, diag):
            chains = [(x, hh) for x in range(2) if tiles[x] is not None for hh in range(2)]
            kv = {x: tile(tiles[x]) for x in range(2) if tiles[x] is not None}
            heads = [_sb_chain_head(qms[x][hh], kv[x][0], causal if diag else None) for x, hh in chains]
            sufs = [_dot(sp.astype(BF16), later) for sp, _ in heads]
            new = [list(carry[0]), list(carry[1])]
            for (x, hh), (sp, lpos), suf in zip(chains, heads, sufs):
                c, acc = carry[x][hh]
                a = jnp.exp(lpos - suf - c)
                if diag:
                    a = jnp.where(causal, a, 0.0)
                new[x][hh] = (c + jnp.sum(sp, axis=1, keepdims=True), acc + _dot(a.astype(BF16), kv[x][1]))
            return (tuple(new[0]), tuple(new[1]))

        def lowest(carry):
            m = [jnp.min(carry[x][hh][0]) for x in range(2) for hh in range(2)]
            return jnp.minimum(jnp.minimum(m[0], m[1]), jnp.minimum(m[2], m[3]))

        zero = (jnp.zeros((t, 1), F32), jnp.zeros((t, LANES), F32))
        carry = step((2 * i, 2 * i + 1), ((zero, zero), (zero, zero)), True)

        n_full, carry = lax.while_loop(
            lambda st: jnp.logical_and(st[0] < 2 * i, lowest(st[1]) <= _SB_DEAD),
            lambda st: (st[0] + 1, step((2 * i - 1 - st[0], 2 * i - st[0]), st[1], False)),
            (jnp.int32(0), carry))
        b_last = jnp.logical_and(n_full == 2 * i, lowest(carry) <= _SB_DEAD)
        carry = lax.cond(b_last, lambda ca: step((None, 0), ca, False), lambda ca: ca, carry)
        for x in range(2):
            (c0, acc0), (c1, acc1) = carry[x]
            o_ref[pl.ds(x * t, t), :] = jnp.where(lane_hi == 0, acc0, acc1).astype(o_ref.dtype)
            tot_ref[pl.ds(x * t, t), :] = jnp.where(lane_hi == 0, c0, c1)
        steps_ref[p, i] = n_full + b_last.astype(jnp.int32)

    return _call(
        body, (qkv, qkv, qkv), rider,
        name="sb_fwd",
        grid=(n_pairs, nq),
        in_specs=[
            pl.BlockSpec((2 * t, LANES), lambda p, i: (i, _SB_Q0 + p)),
            pl.BlockSpec((s, LANES), lambda p, i: (0, _SB_K0 + p)),
            pl.BlockSpec((s, LANES), lambda p, i: (0, _SB_V0 + p)),
        ],
        out_specs=[pl.BlockSpec((2 * t, LANES), lambda p, i: (i, p))] * 2 + [pl.BlockSpec(memory_space=pltpu.SMEM)],
        out_shape=[jax.ShapeDtypeStruct((s, SB_WIDTH), BF16), jax.ShapeDtypeStruct((s, SB_WIDTH), F32),
                   jax.ShapeDtypeStruct((n_pairs, nq), jnp.int32)],
        compiler_params=_cparams(2),
    )


def _sb_bwd(qkv, do_b, tot_b, n_steps):
    s = qkv.shape[0]
    t = SB_TK
    nq = s // (2 * t)
    n_pairs = SB_WIDTH // LANES
    scale = 1.0 / math.sqrt(HEAD_DIM)

    def body(steps_ref, q_ref, k_ref, v_ref, do_ref, tot_ref, dq_ref, dk_ref, dv_ref):
        p, i = pl.program_id(0), pl.program_id(1)

        @pl.when(i == 0)
        def _():
            dk_ref[...] = jnp.zeros_like(dk_ref)
            dv_ref[...] = jnp.zeros_like(dv_ref)

        lane = lax.broadcasted_iota(jnp.int32, (1, LANES), 1)
        lane_hi = lane // HEAD_DIM
        later = _tri(t, lambda r, c: r > c)
        before = _tri(t, lambda r, c: r < c)
        causal = lax.broadcasted_iota(jnp.int32, (t, t), 1) < lax.broadcasted_iota(jnp.int32, (t, t), 0)
        qms, doms, tots = [], [], []
        for x in range(2):
            rows = pl.ds(x * t, t)
            q, do, tot_all = q_ref[rows, :] * scale, do_ref[rows, :], tot_ref[rows, :]
            qms.append([jnp.where(lane_hi == hh, q, jnp.zeros_like(q)) for hh in range(2)])
            doms.append([jnp.where(lane_hi == hh, do, jnp.zeros_like(do)) for hh in range(2)])
            tots.append([jnp.sum(jnp.where(lane == hh * HEAD_DIM, tot_all, 0.0), axis=1, keepdims=True)
                         for hh in range(2)])

        def step(tiles, carry, diag):
            chains = [(x, hh) for x in range(2) if tiles[x] is not None for hh in range(2)]
            offs = {x: pl.multiple_of(tiles[x] * t, t) for x in range(2) if tiles[x] is not None}
            ks = {x: k_ref[pl.ds(off, t), :] for x, off in offs.items()}
            vs = {x: v_ref[pl.ds(off, t), :] for x, off in offs.items()}
            heads = [_sb_chain_head(qms[x][hh], ks[x], causal if diag else None) for x, hh in chains]
            sufs = [_dot(sp.astype(BF16), later) for sp, _ in heads]
            das = [_dot_nt(doms[x][hh], vs[x]) for x, hh in chains]
            new = [list(carry[0]), list(carry[1])]
            sigs, gs, abs_ = [], [], []
            for (x, hh), (sp, lpos), suf, da in zip(chains, heads, sufs, das):
                cl = carry[x][hh][0] + jnp.sum(sp, axis=1, keepdims=True)
                sig = jnp.exp(lpos)
                a = sig * jnp.exp(-suf - (tots[x][hh] - cl))
                if diag:
                    a = jnp.where(causal, a, 0.0)
                g = a * da
                sigs.append(sig)
                gs.append(g)
                abs_.append(a.astype(BF16))
                new[x][hh] = (cl, carry[x][hh][1] + jnp.sum(g, axis=1, keepdims=True), carry[x][hh][2])
            prefs = [_dot(g.astype(BF16), before) for g in gs]
            dvs = [_dot_tn(ab, doms[x][hh]) for (x, hh), ab in zip(chains, abs_)]
            dzs = []
            for (x, hh), sig, g, pref in zip(chains, sigs, gs, prefs):
                dz = g - sig * (g + pref + carry[x][hh][1])
                if diag:
                    dz = jnp.where(causal, dz, 0.0)
                dzs.append(dz.astype(BF16))
            dqs = [_dot(dz, ks[x]) for (x, hh), dz in zip(chains, dzs)]
            dks = [_dot_tn(dz, qms[x][hh]) for (x, hh), dz in zip(chains, dzs)]
            for n, (x, hh) in enumerate(chains):
                cl, cg, dq = new[x][hh]
                new[x][hh] = (cl, cg, dq + dqs[n])
            for x in offs:
                mine = [n for n, ch in enumerate(chains) if ch[0] == x]
                dk_ref[pl.ds(offs[x], t), :] += dks[mine[0]] + dks[mine[1]]
                dv_ref[pl.ds(offs[x], t), :] += dvs[mine[0]] + dvs[mine[1]]
            return (tuple(new[0]), tuple(new[1]))

        taken = steps_ref[p, i]
        n_full = jnp.minimum(taken, 2 * i)
        zero = (jnp.zeros((t, 1), F32), jnp.zeros((t, 1), F32), jnp.zeros((t, LANES), F32))
        carry = ((zero, zero), (zero, zero))
        carry = lax.cond(taken > 2 * i, lambda ca: step((None, 0), ca, False), lambda ca: ca, carry)
        carry = lax.fori_loop(
            0, n_full, lambda n, ca: step((2 * i - n_full + n, 2 * i + 1 - n_full + n), ca, False), carry)
        carry = step((2 * i, 2 * i + 1), carry, True)
        for x in range(2):
            dq = jnp.where(lane_hi == 0, carry[x][0][2], carry[x][1][2])
            dq_ref[pl.ds(x * t, t), :] = (dq * scale).astype(dq_ref.dtype)

    row_spec = pl.BlockSpec((2 * t, LANES), lambda p, i, ns: (i, p))
    full_spec = pl.BlockSpec((s, LANES), lambda p, i, ns: (0, p))
    return _pcall(
        body,
        name="sb_bwd",
        grid_spec=pltpu.PrefetchScalarGridSpec(
            num_scalar_prefetch=1,
            grid=(n_pairs, nq),
            in_specs=[
                pl.BlockSpec((2 * t, LANES), lambda p, i, ns: (i, _SB_Q0 + p)),
                pl.BlockSpec((s, LANES), lambda p, i, ns: (0, _SB_K0 + p)),
                pl.BlockSpec((s, LANES), lambda p, i, ns: (0, _SB_V0 + p)),
                row_spec, row_spec,
            ],
            out_specs=[row_spec, full_spec, full_spec],
        ),
        out_shape=[jax.ShapeDtypeStruct((s, SB_WIDTH), BF16), jax.ShapeDtypeStruct((s, SB_WIDTH), F32),
                   jax.ShapeDtypeStruct((s, SB_WIDTH), F32)],
        compiler_params=_cparams(2),
    )(n_steps, qkv, qkv, qkv, do_b, tot_b)


def _sb_fwd_wide(qkv):
    s = qkv.shape[0]
    t, tq = SB_TK, min(SB_TQ_FWD, s)
    assert tq in (t, 2 * t) and s % (2 * t) == 0
    nq = s // tq
    n_pairs = SB_WIDTH // LANES

    def body(q_ref, k_ref, v_ref, o_ref, tot_ref, steps_ref):
        p, i = pl.program_id(0), pl.program_id(1)
        q = q_ref[...] * (1.0 / math.sqrt(HEAD_DIM))
        lane_hi = lax.broadcasted_iota(jnp.int32, (1, LANES), 1) // HEAD_DIM
        later = _tri(t, lambda r, c: r > c)
        row = lax.broadcasted_iota(jnp.int32, (tq, t), 0)
        col = lax.broadcasted_iota(jnp.int32, (tq, t), 1)
        qms = [jnp.where(lane_hi == hh, q, jnp.zeros_like(q)) for hh in range(2)]

        def step(jj, carry, diag):
            tiles = (2 * jj + 1, 2 * jj)
            offs = [pl.multiple_of(j * t, t) for j in tiles]
            ks = [k_ref[pl.ds(off, t), :] for off in offs]
            vs = [v_ref[pl.ds(off, t), :] for off in offs]
            masks = [(j * t + col) < (i * tq + row) for j in tiles] if diag else None
            chains = [(n, hh) for n in range(2) for hh in range(2)]
            zs = [_dot_nt(qms[hh], ks[n]) for n, hh in chains]
            sps, lposs = [], []
            for (n, hh), z in zip(chains, zs):
                sp = _softplus(z)
                lposs.append(z - sp)
                sps.append(jnp.where(masks[n], sp, 0.0) if diag else sp)
            sufs = [_dot(sp.astype(BF16), later) for sp in sps]
            cs = [carry[0], carry[2]]
            accs = [carry[1], carry[3]]
            for idx, (n, hh) in enumerate(chains):
                a = jnp.exp(lposs[idx] - sufs[idx] - cs[hh])
                if diag:
                    a = jnp.where(masks[n], a, 0.0)
                accs[hh] = accs[hh] + _dot(a.astype(BF16), vs[n])
                cs[hh] = cs[hh] + jnp.sum(sps[idx], axis=1, keepdims=True)
            return cs[0], accs[0], cs[1], accs[1]

        zc, za = jnp.zeros((tq, 1), F32), jnp.zeros((tq, LANES), F32)
        last = (i * tq) // (2 * t)
        carry = step(last, (zc, za, zc, za), True)

        def alive(state):
            n, ca = state
            return jnp.logical_and(n < last, jnp.minimum(jnp.min(ca[0]), jnp.min(ca[2])) <= _SB_DEAD)

        n_off, carry = lax.while_loop(alive, lambda st: (st[0] + 1, step(last - 1 - st[0], st[1], False)),
                                      (jnp.int32(0), carry))
        out = jnp.where(lane_hi == 0, carry[1], carry[3])
        tot = jnp.where(lane_hi == 0, carry[0], carry[2])
        o_ref[...] = out.astype(o_ref.dtype)
        tot_ref[...] = tot
        steps_ref[p, i] = n_off

    o, tot, n_steps = _pcall(
        body,
        name="sb_fwd",
        grid=(n_pairs, nq),
        in_specs=[
            pl.BlockSpec((tq, LANES), lambda p, i: (i, _SB_Q0 + p)),
            pl.BlockSpec((s, LANES), lambda p, i: (0, _SB_K0 + p)),
            pl.BlockSpec((s, LANES), lambda p, i: (0, _SB_V0 + p)),
        ],
        out_specs=[pl.BlockSpec((tq, LANES), lambda p, i: (i, p))] * 2 + [pl.BlockSpec(memory_space=pltpu.SMEM)],
        out_shape=[jax.ShapeDtypeStruct((s, SB_WIDTH), BF16), jax.ShapeDtypeStruct((s, SB_WIDTH), F32),
                   jax.ShapeDtypeStruct((n_pairs, nq), jnp.int32)],
        compiler_params=_cparams(2),
    )(qkv, qkv, qkv)
    return o, tot, n_steps


def _sb_bwd_wide(qkv, do_b, tot_b, n_steps):
    s = qkv.shape[0]
    t, tq = SB_TK, min(SB_TQ_BWD, s)
    assert tq in (t, 2 * t) and s % (2 * t) == 0
    nq = s // tq
    assert nq % n_steps.shape[1] == 0
    n_pairs = SB_WIDTH // LANES
    scale = 1.0 / math.sqrt(HEAD_DIM)

    def body(steps_ref, q_ref, k_ref, v_ref, do_ref, tot_ref, dq_ref, dk_ref, dv_ref):
        p, i = pl.program_id(0), pl.program_id(1)

        @pl.when(i == 0)
        def _():
            dk_ref[...] = jnp.zeros_like(dk_ref)
            dv_ref[...] = jnp.zeros_like(dv_ref)

        q = q_ref[...] * scale
        do = do_ref[...]
        tot_all = tot_ref[...]
        lane = lax.broadcasted_iota(jnp.int32, (1, LANES), 1)
        lane_hi = lane // HEAD_DIM
        later = _tri(t, lambda r, c: r > c)
        before = _tri(t, lambda r, c: r < c)
        row = lax.broadcasted_iota(jnp.int32, (tq, t), 0)
        col = lax.broadcasted_iota(jnp.int32, (tq, t), 1)
        qms = [jnp.where(lane_hi == hh, q, jnp.zeros_like(q)) for hh in range(2)]
        doms = [jnp.where(lane_hi == hh, do, jnp.zeros_like(do)) for hh in range(2)]
        tots = [jnp.sum(jnp.where(lane == hh * HEAD_DIM, tot_all, 0.0), axis=1, keepdims=True) for hh in range(2)]

        def step(jj, carry, diag):
            tiles = (2 * jj, 2 * jj + 1)
            offs = [pl.multiple_of(j * t, t) for j in tiles]
            ks = [k_ref[pl.ds(off, t), :] for off in offs]
            vs = [v_ref[pl.ds(off, t), :] for off in offs]
            masks = [(j * t + col) < (i * tq + row) for j in tiles] if diag else None
            chains = [(n, hh) for n in range(2) for hh in range(2)]
            zs = [_dot_nt(qms[hh], ks[n]) for n, hh in chains]
            sps, sigs = [], []
            for (n, hh), z in zip(chains, zs):
                sp = _softplus(z)
                sigs.append(jnp.exp(z - sp))
                sps.append(jnp.where(masks[n], sp, 0.0) if diag else sp)
            sufs = [_dot(sp.astype(BF16), later) for sp in sps]
            das = [_dot_nt(doms[hh], vs[n]) for n, hh in chains]
            cls = [carry[0], carry[3]]
            cgs = [carry[1], carry[4]]
            accs = [carry[2], carry[5]]
            gs, abs_, cg_at = [], [], []
            for idx, (n, hh) in enumerate(chains):
                cls[hh] = cls[hh] + jnp.sum(sps[idx], axis=1, keepdims=True)
                a = sigs[idx] * jnp.exp(-sufs[idx] - (tots[hh] - cls[hh]))
                if diag:
                    a = jnp.where(masks[n], a, 0.0)
                g = a * das[idx]
                gs.append(g)
                abs_.append(a.astype(BF16))
                cg_at.append(cgs[hh])
                cgs[hh] = cgs[hh] + jnp.sum(g, axis=1, keepdims=True)
            prefs = [_dot(g.astype(BF16), before) for g in gs]
            dvs = [_dot_tn(abs_[idx], doms[hh]) for idx, (n, hh) in enumerate(chains)]
            dzs = []
            for idx, (n, hh) in enumerate(chains):
                g = gs[idx]
                dz = g - sigs[idx] * (g + prefs[idx] + cg_at[idx])
                if diag:
                    dz = jnp.where(masks[n], dz, 0.0)
                dzs.append(dz.astype(BF16))
            for idx, (n, hh) in enumerate(chains):
                accs[hh] = accs[hh] + _dot(dzs[idx], ks[n])
            dks = [_dot_tn(dzs[idx], qms[hh]) for idx, (n, hh) in enumerate(chains)]
            for n in range(2):
                dk_ref[pl.ds(offs[n], t), :] += dks[2 * n] + dks[2 * n + 1]
                dv_ref[pl.ds(offs[n], t), :] += dvs[2 * n] + dvs[2 * n + 1]
            return cls[0], cgs[0], accs[0], cls[1], cgs[1], accs[1]

        zc, za = jnp.zeros((tq, 1), F32), jnp.zeros((tq, LANES), F32)
        last = (i * tq) // (2 * t)
        first = last - steps_ref[p, (i * n_steps.shape[1]) // nq]
        carry = lax.fori_loop(first, last, lambda jj, ca: step(jj, ca, False), (zc, zc, za, zc, zc, za))
        carry = step(last, carry, True)
        dq = jnp.where(lane_hi == 0, carry[2], carry[5])
        dq_ref[...] = (dq * scale).astype(dq_ref.dtype)

    row_spec = pl.BlockSpec((tq, LANES), lambda p, i, ns: (i, p))
    full_spec = pl.BlockSpec((s, LANES), lambda p, i, ns: (0, p))
    return _pcall(
        body,
        name="sb_bwd",
        grid_spec=pltpu.PrefetchScalarGridSpec(
            num_scalar_prefetch=1,
            grid=(n_pairs, nq),
            in_specs=[
                pl.BlockSpec((tq, LANES), lambda p, i, ns: (i, _SB_Q0 + p)),
                pl.BlockSpec((s, LANES), lambda p, i, ns: (0, _SB_K0 + p)),
                pl.BlockSpec((s, LANES), lambda p, i, ns: (0, _SB_V0 + p)),
                row_spec, row_spec,
            ],
            out_specs=[row_spec, full_spec, full_spec],
        ),
        out_shape=[jax.ShapeDtypeStruct((s, SB_WIDTH), BF16), jax.ShapeDtypeStruct((s, SB_WIDTH), F32),
                   jax.ShapeDtypeStruct((s, SB_WIDTH), F32)],
        compiler_params=_cparams(2),
    )(n_steps, qkv, qkv, qkv, do_b, tot_b)


def _gates(gl, bg):
    return _sigmoid(gl[:, :D_MODEL] + bg[:, :D_MODEL]), _sigmoid(gl[:, D_MODEL:] + bg[:, D_MODEL:])


def _mixer_fwd(o_a, o_b, gl, x0, bg, g2, w_ud, w_us, w_out, tm):
    def epi(_, rows, consts):
        oa, ob, glv, x = rows
        bgv, g2v, wud, wus, wout = consts
        ga, gb = _gates(glv, bgv)
        merged = ga * _dot(oa, wud) + gb * _dot(ob, wus)
        x1 = x + _dot(merged.astype(BF16), wout)
        r, xh = _rms_stats(x1)
        return [x1, xh * g2v], []

    return _rowk("mixer_fwd", tm=tm, rows=[o_a, o_b, gl, x0], consts=[bg, g2, w_ud, w_us, w_out],
                 row_outs=[(D_MODEL, F32), (D_MODEL, BF16)], epilogue=epi)


def _mixer_bwd(dx1, o_a, o_b, gl, bg, w_ud, w_us, w_out, tm):
    s = dx1.shape[0]
    nm = s // tm

    def body(dx_ref, oa_ref, ob_ref, gl_ref, bg_ref, wud_ref, wus_ref, wout_ref,
             doa_ref, dob_ref, dgl_ref, gwout_ref, gwud_ref, gwus_ref, gbg_ref):
        i = pl.program_id(0)
        dxb = dx_ref[...].astype(BF16)
        oa, ob = oa_ref[...], ob_ref[...]
        ga, gb = _gates(gl_ref[...], bg_ref[...])
        ua, ub = _dot(oa, wud_ref[...]), _dot(ob, wus_ref[...])
        merged = (ga * ua + gb * ub).astype(BF16)
        dm = _dot_nt(dxb, wout_ref[...])
        dua = (dm * ga).astype(BF16)
        dub = (dm * gb).astype(BF16)
        dgla = dm * ua * ga * (1.0 - ga)
        dglb = dm * ub * gb * (1.0 - gb)
        doa_ref[...] = _dot_nt(dua, wud_ref[...]).astype(doa_ref.dtype)
        dob_ref[...] = _dot_nt(dub, wus_ref[...]).astype(dob_ref.dtype)
        dgl_ref[:, :D_MODEL] = dgla.astype(dgl_ref.dtype)
        dgl_ref[:, D_MODEL:] = dglb.astype(dgl_ref.dtype)
        parts = [(gwout_ref, _dot_tn(merged, dxb)), (gwud_ref, _dot_tn(oa, dua)), (gwus_ref, _dot_tn(ob, dub))]
        for r, v in parts:

            @pl.when(i == 0)
            def _(r=r, v=v):
                r[...] = v

            @pl.when(i > 0)
            def _(r=r, v=v):
                r[...] += v

        sa = jnp.sum(dgla, axis=0, keepdims=True)
        sb = jnp.sum(dglb, axis=0, keepdims=True)

        @pl.when(i == 0)
        def _():
            gbg_ref[:, :D_MODEL] = sa
            gbg_ref[:, D_MODEL:] = sb

        @pl.when(i > 0)
        def _():
            gbg_ref[:, :D_MODEL] += sa
            gbg_ref[:, D_MODEL:] += sb

    row = lambda w: pl.BlockSpec((tm, w), lambda i: (i, 0))
    full = lambda a: pl.BlockSpec(a.shape, lambda i: (0, 0))
    fshape = lambda r, c: jax.ShapeDtypeStruct((r, c), F32)
    return _pcall(
        body,
        name="mixer_bwd",
        grid=(nm,),
        in_specs=[row(D_MODEL), row(DIL_OUT_WIDTH), row(SB_WIDTH), row(2 * D_MODEL),
                  full(bg), full(w_ud), full(w_us), full(w_out)],
        out_specs=[row(DIL_OUT_WIDTH), row(SB_WIDTH), row(2 * D_MODEL),
                   pl.BlockSpec((D_MODEL, D_MODEL), lambda i: (0, 0)),
                   pl.BlockSpec((DIL_OUT_WIDTH, D_MODEL), lambda i: (0, 0)),
                   pl.BlockSpec((SB_WIDTH, D_MODEL), lambda i: (0, 0)),
                   pl.BlockSpec((1, 2 * D_MODEL), lambda i: (0, 0))],
        out_shape=[jax.ShapeDtypeStruct((s, DIL_OUT_WIDTH), BF16), jax.ShapeDtypeStruct((s, SB_WIDTH), BF16),
                   jax.ShapeDtypeStruct((s, 2 * D_MODEL), BF16),
                   fshape(D_MODEL, D_MODEL), fshape(DIL_OUT_WIDTH, D_MODEL), fshape(SB_WIDTH, D_MODEL),
                   fshape(1, 2 * D_MODEL)],
        compiler_params=_cparams(1),
    )(dx1, o_a, o_b, gl, bg, w_ud, w_us, w_out)


_HBM = pl.BlockSpec(memory_space=pltpu.HBM)
_MESH = pl.DeviceIdType.MESH


def _all_gather(shards):
    n = len(shards)

    def body(*refs):
        x_refs, out_refs = refs[:n], refs[n:2 * n]
        send_sems, recv_sems, local_sems = refs[2 * n:]
        x, y, c = lax.axis_index("x"), lax.axis_index("y"), lax.axis_index("c")
        me, sibling = (x, y, c), (x, y, 1 - c)
        chips = [(1 - x, y), (x, 1 - y), (1 - x, 1 - y)]

        def slot(a, px, py, pc):
            return out_refs[a].at[4 * px + 2 * py + pc]

        def copy(a, k, block, to, own=False):
            return pltpu.make_async_remote_copy(
                src_ref=x_refs[a] if own else slot(a, *block), dst_ref=slot(a, *block),
                send_sem=send_sems.at[7 * a + k], recv_sem=recv_sems.at[7 * a + k], device_id=to, device_id_type=_MESH)

        mine = [pltpu.make_async_copy(x_refs[a], slot(a, *me), local_sems.at[a]) for a in range(n)]
        for cp in mine:
            cp.start()
        first = []
        for a in range(n):
            first.append(copy(a, 0, me, sibling, own=True))
            first += [copy(a, 1 + j, me, (*chip, c), own=True) for j, chip in enumerate(chips)]
        for cp in first:
            cp.start()
        passed = []
        for a in range(n):
            for j, chip in enumerate(chips):
                copy(a, 1 + j, (*chip, c), me).wait_recv()
                passed.append(copy(a, 4 + j, (*chip, c), sibling))
                passed[-1].start()
        for a in range(n):
            copy(a, 0, sibling, me).wait_recv()
            for j, chip in enumerate(chips):
                copy(a, 4 + j, (*chip, 1 - c), me).wait_recv()
        for cp in first + passed:
            cp.wait_send()
        for cp in mine:
            cp.wait()

    return _pcall(
        body,
        name="all_gather_weights",
        in_specs=[_HBM] * n,
        out_specs=[_HBM] * n,
        out_shape=[jax.ShapeDtypeStruct((N_DEV,) + s.shape, s.dtype) for s in shards],
        scratch_shapes=[pltpu.SemaphoreType.DMA((7 * n,)), pltpu.SemaphoreType.DMA((7 * n,)),
                        pltpu.SemaphoreType.DMA((n,))],
    )(*shards)


def _exchange(chunks):
    n = len(chunks)

    def body(*refs):
        g_refs, o_refs = refs[:n], refs[n:2 * n]
        send_sems, recv_sems, local_sems = refs[2 * n:]
        x, y, c = lax.axis_index("x"), lax.axis_index("y"), lax.axis_index("c")
        me = 4 * x + 2 * y + c
        own = [pltpu.make_async_copy(g_refs[a].at[me], o_refs[a].at[me], local_sems.at[a]) for a in range(n)]
        for cp in own:
            cp.start()
        copies = []
        for a in range(n):
            for k in range(1, N_DEV):
                px, py, pc = x ^ (k >> 2), y ^ ((k >> 1) & 1), c ^ (k & 1)
                peer = 4 * px + 2 * py + pc
                copies.append(pltpu.make_async_remote_copy(
                    src_ref=g_refs[a].at[peer], dst_ref=o_refs[a].at[me], send_sem=send_sems.at[7 * a + k - 1],
                    recv_sem=recv_sems.at[7 * a + k - 1], device_id=(px, py, pc), device_id_type=_MESH))
        for cp in copies:
            cp.start()
        for cp in copies:
            cp.wait()
        for cp in own:
            cp.wait()

    return _pcall(
        body,
        name="exchange_grads",
        in_specs=[_HBM] * n,
        out_specs=[_HBM] * n,
        out_shape=[jax.ShapeDtypeStruct(g.shape, g.dtype) for g in chunks],
        scratch_shapes=[pltpu.SemaphoreType.DMA((7 * n,)), pltpu.SemaphoreType.DMA((7 * n,)),
                        pltpu.SemaphoreType.DMA((n,))],
    )(*chunks)


_SEM = pl.BlockSpec(memory_space=pltpu.SEMAPHORE)
_EFFECT = pltpu.SideEffectType.DATAFLOW_SIDE_EFFECTING


def _peers(x, y, c):
    out = []
    for k in range(1, N_DEV):
        px, py, pc = x ^ (k >> 2), y ^ ((k >> 1) & 1), c ^ (k & 1)
        out.append(((px, py, pc), 4 * px + 2 * py + pc))
    return out


def _spread_copies(src_refs, land_refs, send_sems, recv_sems, chunked):
    x, y, c = lax.axis_index("x"), lax.axis_index("y"), lax.axis_index("c")
    me = 4 * x + 2 * y + c
    copies = []
    for a, (src, land) in enumerate(zip(src_refs, land_refs)):
        for k, (peer_id, peer) in enumerate(_peers(x, y, c)):
            copies.append(pltpu.make_async_remote_copy(
                src_ref=src.at[peer] if chunked else src, dst_ref=land.at[me], send_sem=send_sems.at[7 * a + k],
                recv_sem=recv_sems.at[7 * a + k], device_id=peer_id, device_id_type=_MESH))
    return copies


def _spread_start(name, srcs, chunked):
    n = len(srcs)
    lands = [lax.empty((N_DEV,) + (s.shape[1:] if chunked else s.shape), s.dtype) for s in srcs]

    def body(*refs):
        src_refs, land_refs = refs[:n], refs[n:2 * n]
        send_sems, recv_sems = refs[2 * n], refs[2 * n + 1]
        token = refs[-1]
        for cp in _spread_copies(src_refs, land_refs, send_sems, recv_sems, chunked):
            cp.start()
        token[...] = jnp.zeros_like(token)

    hbm = lambda a: pltpu.HBM(a.shape, a.dtype)
    outs = _pcall(
        body,
        name=name,
        out_shape=(pltpu.SemaphoreType.DMA((7 * n,)), pltpu.SemaphoreType.DMA((7 * n,)),
                   *[hbm(s) for s in srcs], *[hbm(l) for l in lands], jax.ShapeDtypeStruct((8, LANES), F32)),
        in_specs=[_HBM] * (2 * n),
        out_specs=(_SEM, _SEM, *([_HBM] * (2 * n)), pl.BlockSpec(memory_space=pltpu.VMEM)),
        input_output_aliases={i: 2 + i for i in range(2 * n)},
        compiler_params=pltpu.CompilerParams(has_side_effects=_EFFECT),
    )(*[pltpu.with_memory_space_constraint(a, pltpu.HBM) for a in list(srcs) + lands])
    return outs[0], outs[1], list(outs[2:2 + n]), list(outs[2 + n:2 + 2 * n]), outs[-1]


def _spread_wait(name, send_sems, recv_sems, srcs, lands, after, chunked):
    n = len(srcs)

    def body(*refs):
        src_refs, land_refs = refs[:n], refs[n:2 * n]
        for cp in _spread_copies(src_refs, land_refs, refs[2 * n], refs[2 * n + 1], chunked):
            cp.wait_send()
            cp.wait_recv()

    hbm = lambda a: pltpu.HBM(a.shape, a.dtype)
    outs = _pcall(
        body,
        name=name,
        out_shape=tuple(hbm(a) for a in list(srcs) + list(lands)),
        in_specs=[_HBM] * (2 * n) + [_SEM, _SEM, pl.BlockSpec(memory_space=pl.ANY)],
        out_specs=tuple([_HBM] * (2 * n)),
        input_output_aliases={i: i for i in range(2 * n)},
        compiler_params=pltpu.CompilerParams(has_side_effects=_EFFECT),
    )(*srcs, *lands, send_sems, recv_sems, after)
    return list(outs[:n]), list(outs[n:])


def _with_own(land, own):
    me = 4 * lax.axis_index("x") + 2 * lax.axis_index("y") + lax.axis_index("c")
    return lax.dynamic_update_slice(land, own[None], (me,) + (0,) * own.ndim)


def _reduce_adamw(name, parts, w, m, v, tr):
    _, rows, cols = parts.shape
    tr = min(tr, rows)
    assert rows % tr == 0
    c1 = 1.0 / (1.0 - ADAM_B1 ** ADAM_STEP)
    c2 = 1.0 / (1.0 - ADAM_B2 ** ADAM_STEP)

    def body(p_ref, w_ref, m_ref, v_ref, g_out, d_out, m_out, v_out):
        g = p_ref[0].astype(F32)
        for d in range(1, N_DEV):
            g = g + p_ref[d].astype(F32)
        mn = ADAM_B1 * m_ref[...] + (1.0 - ADAM_B1) * g
        vn = ADAM_B2 * v_ref[...] + (1.0 - ADAM_B2) * (g * g)
        g_out[...] = g
        m_out[...] = mn
        v_out[...] = vn
        d_out[...] = -ADAM_LR * ((mn * c1) / (jnp.sqrt(vn * c2) + ADAM_EPS) + ADAM_WD * w_ref[...])

    spec = pl.BlockSpec((tr, cols), lambda i: (i, 0))
    return _pcall(
        body,
        name=name,
        grid=(rows // tr,),
        in_specs=[pl.BlockSpec((N_DEV, tr, cols), lambda i: (0, i, 0)), spec, spec, spec],
        out_specs=[spec] * 4,
        out_shape=[jax.ShapeDtypeStruct((rows, cols), F32)] * 4,
        compiler_params=_cparams(1),
    )(parts, w, m, v)


_SHARDED = ("w_in", "w_up_dil", "w_up_sb", "w_out", "w_mlp_in", "w_mlp_out")
_FULL_SHAPES = {"w_in": (D_MODEL, IN_COLS), "w_up_dil": (DIL_OUT_WIDTH, D_MODEL), "w_up_sb": (SB_WIDTH, D_MODEL),
                "w_out": (D_MODEL, D_MODEL), "w_mlp_in": (D_MODEL, D_FF), "w_mlp_out": (D_FF, D_MODEL)}
_ROW_SHARDED = ("w_out", "w_mlp_out")


def _shard_shape(name):
    r, c = _FULL_SHAPES[name]
    return (r // N_DEV, c) if name in _ROW_SHARDED else (r, c // N_DEV)


def _assemble(name, gathered):
    r, c = _shard_shape(name)
    if name in _ROW_SHARDED:
        return gathered.reshape(N_DEV * r, c)
    return gathered.transpose(1, 0, 2).reshape(r, N_DEV * c)


def _chunk(name, full):
    r, c = _shard_shape(name)
    if name in _ROW_SHARDED:
        return full.reshape(N_DEV, r, c)
    return full.reshape(r, N_DEV, c).transpose(1, 0, 2)


_SMALL = (("norm_mix_g", D_MODEL), ("b_gate", 2 * D_MODEL), ("norm_mlp_g", D_MODEL), ("norm_final_g", D_MODEL))
_SMALL_N = sum(n for _, n in _SMALL) + LANES


def _pack_small(vals, tail):
    return jnp.concatenate([vals[n].reshape(1, -1) for n, _ in _SMALL] + [tail], axis=1)


def _unpack_small(vec, shapes):
    out, pos = {}, 0
    for n, width in _SMALL:
        out[n] = vec[:, pos:pos + width].reshape(shapes[n])
        pos += width
    return out, vec[:, pos:]


def kernel(x, norm_mix_g, w_in, b_gate, w_up_dil, w_up_sb, w_out, norm_mlp_g, w_mlp_in, w_mlp_out, norm_final_g, loss_target, m_norm_mix_g, m_w_in, m_b_gate, m_w_up_dil, m_w_up_sb, m_w_out, m_norm_mlp_g, m_w_mlp_in, m_w_mlp_out, m_norm_final_g, v_norm_mix_g, v_w_in, v_b_gate, v_w_up_dil, v_w_up_sb, v_w_out, v_norm_mlp_g, v_w_mlp_in, v_w_mlp_out, v_norm_final_g):
    given = dict(locals())
    s = x.shape[1]
    x0 = x.reshape(s, D_MODEL)
    target = loss_target.reshape(s, D_MODEL)
    g1 = norm_mix_g.reshape(1, D_MODEL)
    g2 = norm_mlp_g.reshape(1, D_MODEL)
    g3 = norm_final_g.reshape(1, D_MODEL)
    bg = b_gate.reshape(1, 2 * D_MODEL)
    w_shards = {n: given[n].reshape(_shard_shape(n)) for n in _SHARDED}
    m_shards = {n: given["m_" + n].reshape(_shard_shape(n)) for n in _SHARDED}
    v_shards = {n: given["v_" + n].reshape(_shard_shape(n)) for n in _SHARDED}

    shard_b = {n: w_shards[n].astype(BF16) for n in _SHARDED}
    (gathered_w_in,) = _all_gather([shard_b["w_in"]])
    w_in_f = _assemble("w_in", gathered_w_in)
    w_qkv, w_gl = w_in_f[:, :QKV_COLS], w_in_f[:, QKV_COLS:]
    full = {}

    def norm1(_, rows, consts):
        _, xh = _rms_stats(rows[0])
        return [xh * consts[0]], []

    (h1,) = _rowk("norm_mix", tm=512, rows=[x0], consts=[g1], row_outs=[(D_MODEL, BF16)], epilogue=norm1)
    qkv, (land,) = _mm("proj_qkv", h1, w_qkv, out_dtype=BF16, tm=1024, tn=768, tk=D_MODEL,
                       rider=_Spread([shard_b["w_mlp_in"]], chunked=False))
    full["w_mlp_in"] = _assemble("w_mlp_in", land)
    gl = _mm("proj_gates", h1, w_gl, out_dtype=F32, tm=512, tn=2048, tk=D_MODEL)
    dil = [_dil_fwd(qkv, g) for g in range(len(DIL_GROUPS))]
    os_, lses = [d[0] for d in dil], [d[1] for d in dil]
    o_a = _dil_mix_fwd(os_, lses, 512)
    riding = ("w_mlp_out", "w_out", "w_up_sb", "w_up_dil")
    (o_b, tot_b, sb_steps), lands = _sb_fwd(qkv, rider=_Spread([shard_b[n] for n in riding], chunked=False))
    full.update({n: _assemble(n, land) for n, land in zip(riding, lands)})
    x1, h2 = _mixer_fwd(o_a, o_b, gl, x0, bg, g2, full["w_up_dil"], full["w_up_sb"], full["w_out"], 256)
    f = _mm("mlp_in", h2, full["w_mlp_in"], out_dtype=BF16, tm=1024, tn=1024, tk=D_MODEL,
            epilogue=lambda r, _: jnp.square(jnp.maximum(r, 0.0)))

    def head(acc, rows, consts):
        x1v, tv = rows
        g3v = consts[0]
        x2 = x1v + acc
        r, xh = _rms_stats(x2)
        diff = xh * g3v - tv
        loss = (0.5 / D_MODEL) * jnp.sum(jnp.sum(diff * diff, axis=0, keepdims=True), axis=1, keepdims=True)
        dy = diff * (1.0 / D_MODEL)
        dx2, dg = _rms_bwd(dy, xh, r, g3v)
        return [dx2, dx2], [dg, jnp.broadcast_to(loss, (1, LANES))]

    dx2, dx2b, gg3, loss_part = _rowk(
        "mlp_out_loss", a=f, w=full["w_mlp_out"], tm=512, tk=D_FF, rows=[x1, target], consts=[g3],
        row_outs=[(D_MODEL, F32), (D_MODEL, BF16)], acc_outs=[D_MODEL, LANES], epilogue=head)

    da = _mm("mlp_out_bwd", dx2b, full["w_mlp_out"], tb=True, out_dtype=BF16, tm=1024, tn=1024, tk=D_MODEL, extra=f,
             epilogue=lambda r, fv: r * (2.0 * jnp.sqrt(fv.astype(F32))))
    g_w_mlp_out = _mm("grad_w_mlp_out", f, dx2b, ta=True, out_dtype=F32, tm=1024, tn=1024, tk=2048)
    g_w_mlp_in = _mm("grad_w_mlp_in", h2, da, ta=True, out_dtype=F32, tm=1024, tn=1024, tk=2048)

    def norm_bwd(acc, rows, consts):
        xv, dres = rows
        r, xh = _rms_stats(xv)
        dx, dg = _rms_bwd(acc, xh, r, consts[0])
        return [dres + dx], [dg]

    dx1, gg2 = _rowk("mlp_in_bwd", a=da, w=full["w_mlp_in"], nt=True, tm=512, tk=D_FF, rows=[x1, dx2], consts=[g2],
                     row_outs=[(D_MODEL, F32)], acc_outs=[D_MODEL], epilogue=norm_bwd)
    do_a, do_b, dgl, g_w_out, g_w_ud, g_w_us, g_bg = _mixer_bwd(
        dx1, o_a, o_b, gl, bg, full["w_up_dil"], full["w_up_sb"], full["w_out"], 256)
    bchunk = lambda n, g: _chunk(n, g).astype(BF16)
    parts = {}
    mix = _dil_mix_bwd(do_a, os_, lses, 512)
    dil_b = [_dil_bwd(qkv, mix[0], lses[0], mix[3], 0)]
    small_three = {"w_out": g_w_out, "w_up_sb": g_w_us, "w_up_dil": g_w_ud}
    grads, lands = _dil_bwd(qkv, mix[1], lses[1], mix[4], 1,
                            rider=_Spread([bchunk(n, g) for n, g in small_three.items()], chunked=True))
    dil_b.append(grads)
    parts.update(dict(zip(small_three, lands)))
    grads, (parts["w_mlp_out"],) = _dil_bwd(qkv, mix[2], lses[2], mix[5], 2,
                                            rider=_Spread([bchunk("w_mlp_out", g_w_mlp_out)], chunked=True))
    dil_b.append(grads)
    dq_b, dk_b, dv_b = _sb_bwd(qkv, do_b, tot_b, sb_steps)
    dproj = jnp.concatenate(
        [d[0] for d in dil_b] + [d[1] for d in dil_b] + [d[2] for d in dil_b]
        + [dq_b, dk_b.astype(BF16), dv_b.astype(BF16), dgl], axis=1)
    g_w_in, (parts["w_mlp_in"],) = _mm("grad_w_in", h1, dproj, ta=True, out_dtype=F32, tm=512, tn=IN_COLS // 2, tk=1024,
                                       rider=_Spread([bchunk("w_mlp_in", g_w_mlp_in)], chunked=True))
    (grad_x, gg1), (parts["w_in"],) = _rowk(
        "in_proj_bwd", a=dproj, w=w_in_f, nt=True, tm=512, tk=IN_COLS, rows=[x0, dx1], consts=[g1],
        row_outs=[(D_MODEL, F32)], acc_outs=[D_MODEL], epilogue=norm_bwd,
        rider=_Spread([bchunk("w_in", g_w_in)], chunked=True))

    small_part = _pack_small({"norm_mix_g": gg1, "b_gate": g_bg, "norm_mlp_g": gg2, "norm_final_g": gg3}, loss_part)
    (small_parts,) = _exchange([jnp.broadcast_to(small_part[None], (N_DEV, 1, _SMALL_N))])

    tags = ("grad_", "delta_", "new_m_", "new_v_")
    outs = {}
    for n, p in parts.items():
        res = _reduce_adamw("adamw_" + n, p, w_shards[n], m_shards[n], v_shards[n], 128)
        for tag, val in zip(tags, res):
            outs[tag + n] = val.reshape(given[n].shape)
    small_w = _pack_small(given, jnp.zeros((1, LANES), F32))
    small_m = _pack_small({n: given["m_" + n] for n, _ in _SMALL}, jnp.zeros((1, LANES), F32))
    small_v = _pack_small({n: given["v_" + n] for n, _ in _SMALL}, jnp.ones((1, LANES), F32))
    small_res = _reduce_adamw("adamw_replicated", small_parts, small_w, small_m, small_v, 8)

    small_shapes = {n: given[n].shape for n, _ in _SMALL}
    for tag, small in zip(tags, small_res):
        small_vals, tail = _unpack_small(small, small_shapes)
        for n, val in small_vals.items():
            outs[tag + n] = val
        if tag == "grad_":
            loss = tail[0, 0]
    names = ["norm_mix_g", "w_in", "b_gate", "w_up_dil", "w_up_sb", "w_out", "norm_mlp_g", "w_mlp_in", "w_mlp_out",
             "norm_final_g"]
    return (loss, grad_x.reshape(x.shape), *[outs["grad_" + n] for n in names], *[outs["delta_" + n] for n in names],
            *[outs["new_m_" + n] for n in names], *[outs["new_v_" + n] for n in names])
```

```python
import functools
import math

import jax
import jax.numpy as jnp
from jax import lax
from jax.experimental import pallas as pl
from jax.experimental.pallas import tpu as pltpu

_pcall = pl.pallas_call

F32 = jnp.float32
BF16 = jnp.bfloat16

D_MODEL = 1024
HEAD_DIM = 64
DIL_GROUPS = ((128, 1), (512, 4), (2048, 16))
DIL_HEADS_PER_GROUP = 4
N_DIL_HEADS = 12
N_SB_HEADS = 8
DIL_WIDTH = 768
DIL_OUT_WIDTH = 256
SB_WIDTH = 512
D_FF = 4096
BLOCK = 128
RMS_EPS = 1e-6
NEG_INF = -1e30
QKV_COLS = 3 * DIL_WIDTH + 3 * SB_WIDTH
IN_COLS = QKV_COLS + 2 * D_MODEL
N_DEV = 8

ADAM_LR = 0.001
ADAM_B1 = 0.9
ADAM_B2 = 0.999
ADAM_EPS = 1e-08
ADAM_WD = 0.01
ADAM_STEP = 10

VMEM_LIMIT = 56 * 1024 * 1024
SB_TK = 256
SB_TQ_FWD = 512
SB_TQ_BWD = 256
LANES = 128

_ARB = pltpu.ARBITRARY


def _cparams(n_axes, **kw):
    return pltpu.CompilerParams(dimension_semantics=(_ARB,) * n_axes, vmem_limit_bytes=VMEM_LIMIT, **kw)


def _dot(a, b):
    return jnp.dot(a, b, preferred_element_type=F32)


def _dot_nt(a, b):
    return lax.dot_general(a, b, (((1,), (1,)), ((), ())), preferred_element_type=F32)


def _dot_tn(a, b):
    return lax.dot_general(a, b, (((0,), (0,)), ((), ())), preferred_element_type=F32)


def _split_hi_lo(x):
    hi = x.astype(BF16)
    lo = (x - hi.astype(F32)).astype(BF16)
    return hi, lo


def _dot_hi_lo(x, m):
    hi, lo = _split_hi_lo(x)
    return _dot(hi, m) + _dot(lo, m)


def _sigmoid(x):
    return 1.0 / (1.0 + jnp.exp(-x))


_HBM = pl.BlockSpec(memory_space=pltpu.HBM)
_MESH = pl.DeviceIdType.MESH


class _Spread:
    def __init__(self, srcs, chunked):
        self.srcs, self.chunked, self.n = list(srcs), chunked, len(srcs)

    def land_shapes(self):
        return [jax.ShapeDtypeStruct((N_DEV,) + (s.shape[1:] if self.chunked else s.shape), s.dtype) for s in self.srcs]

    def scratch(self):
        dma = pltpu.SemaphoreType.DMA
        return [dma((7 * self.n,)), dma((7 * self.n,)), dma((self.n,))]

    def copies(self, src_refs, land_refs, send_sems, recv_sems, local_sems):
        x, y, c = lax.axis_index("x"), lax.axis_index("y"), lax.axis_index("c")
        me = 4 * x + 2 * y + c
        out = []
        for a, (src, land) in enumerate(zip(src_refs, land_refs)):
            out.append(pltpu.make_async_copy(src.at[me] if self.chunked else src, land.at[me], local_sems.at[a]))
            for k in range(1, N_DEV):
                px, py, pc = x ^ (k >> 2), y ^ ((k >> 1) & 1), c ^ (k & 1)
                out.append(pltpu.make_async_remote_copy(
                    src_ref=src.at[4 * px + 2 * py + pc] if self.chunked else src, dst_ref=land.at[me],
                    send_sem=send_sems.at[7 * a + k - 1], recv_sem=recv_sems.at[7 * a + k - 1],
                    device_id=(px, py, pc), device_id_type=_MESH))
        return out


def _call(body, args, rider=None, **kw):
    if rider is None:
        return _pcall(body, **kw)(*args)
    grid = kw["grid"]
    single = not isinstance(kw["out_shape"], (list, tuple))
    out_specs = [kw["out_specs"]] if single else list(kw["out_specs"])
    out_shape = [kw["out_shape"]] if single else list(kw["out_shape"])
    in_specs, scratch = list(kw["in_specs"]), list(kw.get("scratch_shapes", []))
    n_in, n_out, n_s, n = len(in_specs), len(out_shape), len(scratch), rider.n

    def hosted(*refs):
        ins, srcs = refs[:n_in], refs[n_in:n_in + n]
        outs, lands = refs[n_in + n:n_in + n + n_out], refs[n_in + n + n_out:n_in + 2 * n + n_out]
        own_scratch, sems = refs[n_in + 2 * n + n_out:n_in + 2 * n + n_out + n_s], refs[n_in + 2 * n + n_out + n_s:]
        ids = [pl.program_id(d) for d in range(len(grid))]
        first = functools.reduce(jnp.logical_and, [i == 0 for i in ids])
        last = functools.reduce(jnp.logical_and, [i == g - 1 for i, g in zip(ids, grid)])
        copies = rider.copies(srcs, lands, *sems)

        @pl.when(first)
        def _():
            for cp in copies:
                cp.start()

        body(*ins, *outs, *own_scratch)

        @pl.when(last)
        def _():
            for cp in copies:
                cp.wait()

    kw = dict(kw, in_specs=in_specs + [_HBM] * n, out_specs=out_specs + [_HBM] * n,
              out_shape=out_shape + rider.land_shapes(), scratch_shapes=scratch + rider.scratch())
    res = _pcall(hosted, **kw)(*args, *rider.srcs)
    return (res[0] if single else list(res[:n_out])), list(res[n_out:])


def _mm(name, a, b, *, ta=False, tb=False, out_dtype, tm, tn, tk, epilogue=None, extra=None, rider=None):
    m = a.shape[1] if ta else a.shape[0]
    k = a.shape[0] if ta else a.shape[1]
    n = b.shape[0] if tb else b.shape[1]
    assert (b.shape[1] if tb else b.shape[0]) == k
    tm, tn, tk = min(tm, m), min(tn, n), min(tk, k)
    assert m % tm == 0 and n % tn == 0 and k % tk == 0, (name, m, n, k, tm, tn, tk)
    nk = k // tk
    dn = (((0 if ta else 1,), (1 if tb else 0,)), ((), ()))
    in_place = nk > 1 and epilogue is None and out_dtype == F32

    def body(*refs):
        if extra is not None:
            a_ref, b_ref, e_ref, o_ref = refs[:4]
        else:
            a_ref, b_ref, o_ref = refs[:3]
            e_ref = None

        def finish(r):
            if epilogue is not None:
                r = epilogue(r, None if e_ref is None else e_ref[...])
            o_ref[...] = r.astype(out_dtype)

        part = lax.dot_general(a_ref[...].astype(BF16), b_ref[...].astype(BF16), dn, preferred_element_type=F32)
        if nk == 1:
            finish(part)
        else:
            acc_ref = o_ref if in_place else refs[-1]
            kk = pl.program_id(2)

            @pl.when(kk == 0)
            def _():
                acc_ref[...] = part

            @pl.when(kk > 0)
            def _():
                acc_ref[...] += part

            if not in_place:

                @pl.when(kk == nk - 1)
                def _():
                    finish(acc_ref[...])

    a_spec = pl.BlockSpec((tk, tm), lambda j, i, kk: (kk, i)) if ta else pl.BlockSpec((tm, tk), lambda j, i, kk: (i, kk))
    b_spec = pl.BlockSpec((tn, tk), lambda j, i, kk: (j, kk)) if tb else pl.BlockSpec((tk, tn), lambda j, i, kk: (kk, j))
    o_spec = pl.BlockSpec((tm, tn), lambda j, i, kk: (i, j))
    in_specs = [a_spec, b_spec]
    args = [a, b]
    if extra is not None:
        in_specs.append(o_spec)
        args.append(extra)
    return _call(
        body, args, rider,
        name=name,
        grid=(n // tn, m // tm, nk),
        in_specs=in_specs,
        out_specs=o_spec,
        out_shape=jax.ShapeDtypeStruct((m, n), out_dtype),
        scratch_shapes=[pltpu.VMEM((tm, tn), F32)] if (nk > 1 and not in_place) else [],
        compiler_params=_cparams(3),
    )


def _grad_cols(name, a, parts, *, tm, tk, rider=None):
    k, m = a.shape
    n = sum(p.shape[1] for p in parts)
    assert m % tm == 0 and k % tk == 0
    nk = k // tk

    def body(*refs):
        a_ref, p_refs, o_ref = refs[0], refs[1:1 + len(parts)], refs[1 + len(parts)]
        kk = pl.program_id(1)
        av = a_ref[...].astype(BF16)
        off = 0
        for p_ref in p_refs:
            cols = slice(off, off + p_ref.shape[1])
            term = _dot_tn(av, p_ref[...].astype(BF16))

            @pl.when(kk == 0)
            def _(cols=cols, term=term):
                o_ref[:, cols] = term

            @pl.when(kk > 0)
            def _(cols=cols, term=term):
                o_ref[:, cols] += term

            off += p_ref.shape[1]

    return _call(
        body, [a] + list(parts), rider,
        name=name,
        grid=(m // tm, nk),
        in_specs=[pl.BlockSpec((tk, tm), lambda i, kk: (kk, i))]
        + [pl.BlockSpec((tk, p.shape[1]), lambda i, kk: (kk, 0)) for p in parts],
        out_specs=pl.BlockSpec((tm, n), lambda i, kk: (i, 0)),
        out_shape=jax.ShapeDtypeStruct((m, n), F32),
        compiler_params=_cparams(2),
    )


def _rowk(name, *, a=None, w=None, nt=False, tm, tk=None, rows=(), consts=(), row_outs=(), acc_outs=(), epilogue,
          rider=None):
    has_mm = a is not None
    a_parts = list(a) if isinstance(a, (list, tuple)) else ([a] if has_mm else [])
    n_a = len(a_parts)
    m = a_parts[0].shape[0] if has_mm else rows[0].shape[0]
    assert m % tm == 0
    nm = m // tm
    if has_mm:
        k = sum(p.shape[1] for p in a_parts)
        n = w.shape[0] if nt else w.shape[1]
        tk = min(tk, k)
        assert k % tk == 0 and (n_a == 1 or tk == k)
        nk = k // tk
    else:
        nk = 1
    n_rows, n_consts, n_ro, n_ao = len(rows), len(consts), len(row_outs), len(acc_outs)

    def body(*refs):
        pos = 0
        if has_mm:
            a_refs, w_ref = refs[:n_a], refs[n_a]
            pos = n_a + 1
        row_refs = refs[pos:pos + n_rows]
        pos += n_rows
        const_refs = refs[pos:pos + n_consts]
        pos += n_consts
        ro_refs = refs[pos:pos + n_ro]
        pos += n_ro
        ao_refs = refs[pos:pos + n_ao]
        pos += n_ao
        i = pl.program_id(0)
        kk = pl.program_id(1)

        def finish(acc):
            ro_vals, ao_vals = epilogue(acc, [r[...] for r in row_refs], [c[...] for c in const_refs])
            for r, v in zip(ro_refs, ro_vals):
                r[...] = v.astype(r.dtype)
            for r, v in zip(ao_refs, ao_vals):

                @pl.when(i == 0)
                def _(r=r, v=v):
                    r[...] = v

                @pl.when(i > 0)
                def _(r=r, v=v):
                    r[...] += v

        if not has_mm:
            finish(None)
            return
        part, off = None, 0
        for a_ref in a_refs:
            width = a_ref.shape[1]
            cols = slice(None) if n_a == 1 else slice(off, off + width)
            av = a_ref[...].astype(BF16)
            term = _dot_nt(av, w_ref[:, cols]) if nt else _dot(av, w_ref[cols, :])
            part = term if part is None else part + term
            off += width
        if nk == 1:
            finish(part)
        else:
            acc_ref = refs[pos]

            @pl.when(kk == 0)
            def _():
                acc_ref[...] = part

            @pl.when(kk > 0)
            def _():
                acc_ref[...] += part

            @pl.when(kk == nk - 1)
            def _():
                finish(acc_ref[...])

    once = pl.Buffered(1)
    in_specs, args = [], []
    if has_mm:
        for part in a_parts:
            in_specs.append(pl.BlockSpec((tm, tk if n_a == 1 else part.shape[1]), lambda i, kk: (i, kk)))
        w_mode = once if nk == 1 else None
        in_specs.append(pl.BlockSpec((n, tk), lambda i, kk: (0, kk), pipeline_mode=w_mode) if nt
                        else pl.BlockSpec((tk, n), lambda i, kk: (kk, 0), pipeline_mode=w_mode))
        args += a_parts + [w]
    for r in rows:
        in_specs.append(pl.BlockSpec((tm, r.shape[1]), lambda i, kk: (i, 0)))
        args.append(r)
    for c in consts:
        in_specs.append(pl.BlockSpec(c.shape, lambda i, kk: (0,) * c.ndim, pipeline_mode=once))
        args.append(c)
    out_specs, out_shape = [], []
    for width, dt in row_outs:
        out_specs.append(pl.BlockSpec((tm, width), lambda i, kk: (i, 0)))
        out_shape.append(jax.ShapeDtypeStruct((m, width), dt))
    for width in acc_outs:
        out_specs.append(pl.BlockSpec((1, width), lambda i, kk: (0, 0)))
        out_shape.append(jax.ShapeDtypeStruct((1, width), F32))
    return _call(
        body, args, rider,
        name=name,
        grid=(nm, nk),
        in_specs=in_specs,
        out_specs=out_specs,
        out_shape=out_shape,
        scratch_shapes=[pltpu.VMEM((tm, n), F32)] if (has_mm and nk > 1) else [],
        compiler_params=_cparams(2),
    )


def _rms_stats(x):
    r = lax.rsqrt(jnp.mean(x * x, axis=-1, keepdims=True) + RMS_EPS)
    return r, x * r


def _rms_bwd(dh, xh, r, g):
    gy = dh * g
    dx = r * (gy - xh * jnp.mean(gy * xh, axis=-1, keepdims=True))
    return dx, jnp.sum(dh * xh, axis=0, keepdims=True)


def _alibi_slope(head):
    return 2.0 ** (-8.0 * (head + 1) / N_DIL_HEADS)


DIL_STEP_BLOCKS = 4


def _dil_band(first_block):
    qi = lax.broadcasted_iota(jnp.int32, (BLOCK, 2 * BLOCK), 0)
    kj = lax.broadcasted_iota(jnp.int32, (BLOCK, 2 * BLOCK), 1)
    steps = qi + BLOCK - kj
    valid = (steps >= 0) & (steps <= BLOCK)
    if first_block is not False:
        valid = valid & ((kj >= BLOCK) | jnp.logical_not(first_block))
    return steps.astype(F32), valid


def _dil_step_specs(ncb, cols, nblk, clamp):
    def own(col):
        return pl.BlockSpec((nblk * BLOCK, DIL_OUT_WIDTH), lambda r, i: (clamp(i), r * ncb + col))

    def before(col):
        return pl.BlockSpec((BLOCK, DIL_OUT_WIDTH), lambda r, i: (jnp.maximum(clamp(i) * nblk - 1, 0), r * ncb + col))

    return [own(cols[0]), own(cols[1]), before(cols[1]), own(cols[2]), before(cols[2])]


def _dil_fwd(qkv, group):
    window, dilation = DIL_GROUPS[group]
    s = qkv.shape[0]
    sub = s // dilation
    nb = sub // BLOCK
    assert nb * BLOCK * dilation == s and window // dilation == BLOCK
    nblk = min(DIL_STEP_BLOCKS, nb)
    assert nb % nblk == 0
    slopes = [_alibi_slope(group * DIL_HEADS_PER_GROUP + h) * dilation for h in range(DIL_HEADS_PER_GROUP)]

    def body(q_ref, kc_ref, kp_ref, vc_ref, vp_ref, o_ref, lse_ref):
        i = pl.program_id(1)
        kk_all = jnp.concatenate([kp_ref[...], kc_ref[...]], axis=0)
        vv_all = jnp.concatenate([vp_ref[...], vc_ref[...]], axis=0)
        head_id = lax.broadcasted_iota(jnp.int32, (1, DIL_OUT_WIDTH), 1) // HEAD_DIM
        chains = [(b, h) for b in range(nblk) for h in range(DIL_HEADS_PER_GROUP)]
        rows = lambda b: slice(b * BLOCK, (b + 1) * BLOCK)
        keys = lambda b: slice(b * BLOCK, (b + 2) * BLOCK)
        bands = [_dil_band(i == 0 if b == 0 else False) for b in range(nblk)]
        qs = [q_ref[rows(b), :] for b in range(nblk)]
        scores = [_dot_nt(jnp.where(head_id == h, qs[b], jnp.zeros_like(qs[b])), kk_all[keys(b)]) for b, h in chains]
        ps, lses = [], []
        for (b, h), sc in zip(chains, scores):
            steps, valid = bands[b]
            logits = jnp.where(valid, sc * (1.0 / math.sqrt(HEAD_DIM)) - slopes[h] * steps, NEG_INF)
            mx = jnp.max(logits, axis=1, keepdims=True)
            e = jnp.exp(logits - mx)
            den = jnp.sum(e, axis=1, keepdims=True)
            lses.append(mx + jnp.log(den))
            ps.append((e * (1.0 / den)).astype(BF16))
        outs = [_dot(p, vv_all[keys(b)]) for (b, h), p in zip(chains, ps)]
        for b in range(nblk):
            mine = [n for n, ch in enumerate(chains) if ch[0] == b]
            o, lse_all = outs[mine[0]], lses[mine[0]]
            for n in mine[1:]:
                o = jnp.where(head_id == chains[n][1], outs[n], o)
                lse_all = jnp.where(head_id == chains[n][1], lses[n], lse_all)
            o_ref[rows(b), :] = o
            lse_ref[rows(b), :] = jnp.broadcast_to(lse_all, o.shape)

    qkv_v, ncb, cols = _dil_view(qkv, group)
    out_spec = pl.BlockSpec((nblk * BLOCK, DIL_OUT_WIDTH), lambda r, i: (i, r))
    o, lse = _pcall(
        body,
        name=f"dil_fwd_g{group}",
        grid=(dilation, nb // nblk),
        in_specs=_dil_step_specs(ncb, cols, nblk, lambda i: i),
        out_specs=[out_spec, out_spec],
        out_shape=[jax.ShapeDtypeStruct((sub, dilation * DIL_OUT_WIDTH), F32)] * 2,
        compiler_params=_cparams(2),
    )(qkv_v, qkv_v, qkv_v, qkv_v, qkv_v)
    return o.reshape(s, DIL_OUT_WIDTH), lse.reshape(s, DIL_OUT_WIDTH)


def _dil_bwd(qkv, do_g, lse_g, dterm_g, group, rider=None):
    window, dilation = DIL_GROUPS[group]
    s = qkv.shape[0]
    sub = s // dilation
    nb = sub // BLOCK
    nblk = min(DIL_STEP_BLOCKS, nb)
    n_steps = nb // nblk
    slopes = [_alibi_slope(group * DIL_HEADS_PER_GROUP + h) * dilation for h in range(DIL_HEADS_PER_GROUP)]
    scale = 1.0 / math.sqrt(HEAD_DIM)
    tail = slice((nblk - 1) * BLOCK, nblk * BLOCK)

    def body(q_ref, kc_ref, kp_ref, vc_ref, vp_ref, do_ref, lse_ref, dt_ref, dq_ref, dk_ref, dv_ref, ck_ref, cv_ref):
        i = pl.program_id(1)

        @pl.when(i == 0)
        def _():
            ck_ref[...] = jnp.zeros_like(ck_ref)
            cv_ref[...] = jnp.zeros_like(cv_ref)

        @pl.when(i < n_steps)
        def _():
            kk_all = jnp.concatenate([kp_ref[...], kc_ref[...]], axis=0)
            vv_all = jnp.concatenate([vp_ref[...], vc_ref[...]], axis=0)
            lane = lax.broadcasted_iota(jnp.int32, (1, DIL_OUT_WIDTH), 1)
            head_id = lane // HEAD_DIM
            chains = [(b, h) for b in range(nblk) for h in range(DIL_HEADS_PER_GROUP)]
            rows = lambda b: slice(b * BLOCK, (b + 1) * BLOCK)
            keys = lambda b: slice(b * BLOCK, (b + 2) * BLOCK)
            bands = [_dil_band(i == 0 if b == 0 else False) for b in range(nblk)]
            qms, doms = [], []
            for b, h in chains:
                q, do = q_ref[rows(b), :], do_ref[rows(b), :]
                qms.append(jnp.where(head_id == h, q, jnp.zeros_like(q)))
                doms.append(jnp.where(head_id == h, do, jnp.zeros_like(do)))
            scores = [_dot_nt(qm, kk_all[keys(b)]) for (b, h), qm in zip(chains, qms)]
            dps = [_dot_nt(dom, vv_all[keys(b)]) for (b, h), dom in zip(chains, doms)]
            pbs, dss = [], []
            for n, (b, h) in enumerate(chains):
                steps, valid = bands[b]
                first = lane == h * HEAD_DIM
                lse = jnp.sum(jnp.where(first, lse_ref[rows(b), :], 0.0), axis=1, keepdims=True)
                dt = jnp.sum(jnp.where(first, dt_ref[rows(b), :], 0.0), axis=1, keepdims=True)
                logits = jnp.where(valid, scores[n] * scale - slopes[h] * steps, NEG_INF)
                p = jnp.where(valid, jnp.exp(logits - lse), 0.0)
                pbs.append(p.astype(BF16))
                dss.append((p * (dps[n] + dt) * scale).astype(BF16))
            dqs = [_dot(ds, kk_all[keys(b)]) for (b, h), ds in zip(chains, dss)]
            dks = [_dot_tn(ds, qm) for ds, qm in zip(dss, qms)]
            dvs = [_dot_tn(pb, dom) for pb, dom in zip(pbs, doms)]
            dkk, dvv = [], []
            for b in range(nblk):
                mine = [n for n, ch in enumerate(chains) if ch[0] == b]
                dq = dqs[mine[0]]
                for n in mine[1:]:
                    dq = jnp.where(head_id == chains[n][1], dqs[n], dq)
                dq_ref[rows(b), :] = dq.astype(dq_ref.dtype)
                dkk.append((dks[mine[0]] + dks[mine[1]]) + (dks[mine[2]] + dks[mine[3]]))
                dvv.append((dvs[mine[0]] + dvs[mine[1]]) + (dvs[mine[2]] + dvs[mine[3]]))
            for out_ref, carry_ref, parts in ((dk_ref, ck_ref, dkk), (dv_ref, cv_ref, dvv)):
                if nblk > 1:
                    out_ref[: (nblk - 1) * BLOCK, :] = carry_ref[: (nblk - 1) * BLOCK, :].astype(out_ref.dtype)
                out_ref[tail, :] = (carry_ref[tail, :] + parts[0][:BLOCK]).astype(out_ref.dtype)
                for b in range(nblk):
                    own = parts[b][BLOCK:]
                    carry_ref[rows(b), :] = own + parts[b + 1][:BLOCK] if b + 1 < nblk else own

        @pl.when(i == n_steps)
        def _():
            dk_ref[...] = ck_ref[...].astype(dk_ref.dtype)
            dv_ref[...] = cv_ref[...].astype(dv_ref.dtype)

    clamp = lambda i: jnp.minimum(i, n_steps - 1)
    qkv_v, ncb, cols = _dil_view(qkv, group)
    view = lambda t: t.reshape(sub, dilation * DIL_OUT_WIDTH)
    row_spec = pl.BlockSpec((nblk * BLOCK, DIL_OUT_WIDTH), lambda r, i: (clamp(i), r))
    late_spec = pl.BlockSpec((nblk * BLOCK, DIL_OUT_WIDTH), lambda r, i: (jnp.maximum(i - 1, 0), r))
    res = _call(
        body, (qkv_v, qkv_v, qkv_v, qkv_v, qkv_v, view(do_g), view(lse_g), view(dterm_g)), rider,
        name=f"dil_bwd_g{group}",
        grid=(dilation, n_steps + 1),
        in_specs=_dil_step_specs(ncb, cols, nblk, clamp) + [row_spec, row_spec, row_spec],
        out_specs=[row_spec, late_spec, late_spec],
        out_shape=[jax.ShapeDtypeStruct((sub, dilation * DIL_OUT_WIDTH), BF16)] * 3,
        scratch_shapes=[pltpu.VMEM((nblk * BLOCK, DIL_OUT_WIDTH), F32)] * 2,
        compiler_params=_cparams(2),
    )
    grads, lands = res if rider is not None else (res, None)
    grads = tuple(g.reshape(s, DIL_OUT_WIDTH) for g in grads)
    return grads if rider is None else (grads, lands)


def _dil_masks(i):
    qi = lax.broadcasted_iota(jnp.int32, (BLOCK, 2 * BLOCK), 0)
    kj = lax.broadcasted_iota(jnp.int32, (BLOCK, 2 * BLOCK), 1)
    steps = qi + BLOCK - kj
    valid = (steps >= 0) & (steps <= BLOCK) & ((kj >= BLOCK) | (i > 0))
    return steps.astype(F32), valid


def _dil_view(qkv, group):
    _, dilation = DIL_GROUPS[group]
    if dilation == 1:
        return qkv, QKV_COLS // DIL_OUT_WIDTH, (group, 3 + group, 6 + group)
    w = DIL_OUT_WIDTH
    own = jnp.concatenate([qkv[:, (3 * part + group) * w:(3 * part + group + 1) * w] for part in range(3)], axis=1)
    return own.reshape(qkv.shape[0] // dilation, dilation * 3 * w), 3, (0, 1, 2)


def _dil_specs(ncb, cols, clamp):
    def cur(col):
        return pl.BlockSpec((BLOCK, DIL_OUT_WIDTH), lambda r, i: (clamp(i), r * ncb + col))

    def prev(col):
        return pl.BlockSpec((BLOCK, DIL_OUT_WIDTH), lambda r, i: (jnp.maximum(clamp(i) - 1, 0), r * ncb + col))

    return [cur(cols[0]), cur(cols[1]), prev(cols[1]), cur(cols[2]), prev(cols[2])]


def _dil_fwd_one(qkv, group):
    window, dilation = DIL_GROUPS[group]
    s = qkv.shape[0]
    sub = s // dilation
    nb = sub // BLOCK
    assert nb * BLOCK * dilation == s and window // dilation == BLOCK
    slopes = [_alibi_slope(group * DIL_HEADS_PER_GROUP + h) * dilation for h in range(DIL_HEADS_PER_GROUP)]

    def body(q_ref, kc_ref, kp_ref, vc_ref, vp_ref, o_ref, lse_ref):
        i = pl.program_id(1)
        q = q_ref[...]
        kk = jnp.concatenate([kp_ref[...], kc_ref[...]], axis=0)
        vv = jnp.concatenate([vp_ref[...], vc_ref[...]], axis=0)
        head_id = lax.broadcasted_iota(jnp.int32, (1, DIL_OUT_WIDTH), 1) // HEAD_DIM
        steps, valid = _dil_masks(i)
        heads = range(DIL_HEADS_PER_GROUP)
        scores = [_dot_nt(jnp.where(head_id == h, q, jnp.zeros_like(q)), kk) for h in heads]
        ps, lses = [], []
        for h in heads:
            logits = scores[h] * (1.0 / math.sqrt(HEAD_DIM)) - slopes[h] * steps
            logits = jnp.where(valid, logits, NEG_INF)
            mx = jnp.max(logits, axis=1, keepdims=True)
            e = jnp.exp(logits - mx)
            den = jnp.sum(e, axis=1, keepdims=True)
            lses.append(mx + jnp.log(den))
            ps.append((e * (1.0 / den)).astype(BF16))
        outs = [_dot(ps[h], vv) for h in heads]
        o, lse_all = outs[0], lses[0]
        for h in heads[1:]:
            o = jnp.where(head_id == h, outs[h], o)
            lse_all = jnp.where(head_id == h, lses[h], lse_all)
        o_ref[...] = o
        lse_ref[...] = jnp.broadcast_to(lse_all, o.shape)

    qkv_v, ncb, cols = _dil_view(qkv, group)
    out_spec = pl.BlockSpec((BLOCK, DIL_OUT_WIDTH), lambda r, i: (i, r))
    o, lse = _pcall(
        body,
        name=f"dil_fwd_g{group}",
        grid=(dilation, nb),
        in_specs=_dil_specs(ncb, cols, lambda i: i),
        out_specs=[out_spec, out_spec],
        out_shape=[jax.ShapeDtypeStruct((sub, dilation * DIL_OUT_WIDTH), F32)] * 2,
        compiler_params=_cparams(2),
    )(qkv_v, qkv_v, qkv_v, qkv_v, qkv_v)
    return o.reshape(s, DIL_OUT_WIDTH), lse.reshape(s, DIL_OUT_WIDTH)


def _dil_bwd_one(qkv, do_g, lse_g, dterm_g, group, rider=None):
    window, dilation = DIL_GROUPS[group]
    s = qkv.shape[0]
    sub = s // dilation
    nb = sub // BLOCK
    slopes = [_alibi_slope(group * DIL_HEADS_PER_GROUP + h) * dilation for h in range(DIL_HEADS_PER_GROUP)]
    scale = 1.0 / math.sqrt(HEAD_DIM)

    def body(q_ref, kc_ref, kp_ref, vc_ref, vp_ref, do_ref, lse_ref, dt_ref, dq_ref, dk_ref, dv_ref, ck_ref, cv_ref):
        i = pl.program_id(1)

        @pl.when(i == 0)
        def _():
            ck_ref[...] = jnp.zeros_like(ck_ref)
            cv_ref[...] = jnp.zeros_like(cv_ref)

        @pl.when(i < nb)
        def _():
            q = q_ref[...]
            do = do_ref[...]
            lse_all = lse_ref[...]
            dt_all = dt_ref[...]
            kk = jnp.concatenate([kp_ref[...], kc_ref[...]], axis=0)
            vv = jnp.concatenate([vp_ref[...], vc_ref[...]], axis=0)
            lane = lax.broadcasted_iota(jnp.int32, (1, DIL_OUT_WIDTH), 1)
            head_id = lane // HEAD_DIM
            steps, valid = _dil_masks(i)
            heads = range(DIL_HEADS_PER_GROUP)
            qms = [jnp.where(head_id == h, q, jnp.zeros_like(q)) for h in heads]
            doms = [jnp.where(head_id == h, do, jnp.zeros_like(do)) for h in heads]
            scores = [_dot_nt(qms[h], kk) for h in heads]
            dps = [_dot_nt(doms[h], vv) for h in heads]
            pbs, dss = [], []
            for h in heads:
                first = lane == h * HEAD_DIM
                lse = jnp.sum(jnp.where(first, lse_all, 0.0), axis=1, keepdims=True)
                dt = jnp.sum(jnp.where(first, dt_all, 0.0), axis=1, keepdims=True)
                logits = scores[h] * scale - slopes[h] * steps
                p = jnp.where(valid, jnp.exp(jnp.where(valid, logits, NEG_INF) - lse), 0.0)
                pbs.append(p.astype(BF16))
                dss.append((p * (dps[h] + dt) * scale).astype(BF16))
            dqs = [_dot(dss[h], kk) for h in heads]
            dks = [_dot_tn(dss[h], qms[h]) for h in heads]
            dvs = [_dot_tn(pbs[h], doms[h]) for h in heads]
            dq = dqs[0]
            for h in heads[1:]:
                dq = jnp.where(head_id == h, dqs[h], dq)
            dkk = (dks[0] + dks[1]) + (dks[2] + dks[3])
            dvv = (dvs[0] + dvs[1]) + (dvs[2] + dvs[3])
            dq_ref[...] = dq.astype(dq_ref.dtype)
            dk_ref[...] = (ck_ref[...] + dkk[:BLOCK]).astype(dk_ref.dtype)
            dv_ref[...] = (cv_ref[...] + dvv[:BLOCK]).astype(dv_ref.dtype)
            ck_ref[...] = dkk[BLOCK:]
            cv_ref[...] = dvv[BLOCK:]

        @pl.when(i == nb)
        def _():
            dk_ref[...] = ck_ref[...].astype(dk_ref.dtype)
            dv_ref[...] = cv_ref[...].astype(dv_ref.dtype)

    clamp = lambda i: jnp.minimum(i, nb - 1)
    qkv_v, ncb, cols = _dil_view(qkv, group)
    view = lambda t: t.reshape(sub, dilation * DIL_OUT_WIDTH)
    row_spec = pl.BlockSpec((BLOCK, DIL_OUT_WIDTH), lambda r, i: (clamp(i), r))
    late_spec = pl.BlockSpec((BLOCK, DIL_OUT_WIDTH), lambda r, i: (jnp.maximum(i - 1, 0), r))
    res = _call(
        body, (qkv_v, qkv_v, qkv_v, qkv_v, qkv_v, view(do_g), view(lse_g), view(dterm_g)), rider,
        name=f"dil_bwd_g{group}",
        grid=(dilation, nb + 1),
        in_specs=_dil_specs(ncb, cols, clamp) + [row_spec, row_spec, row_spec],
        out_specs=[row_spec, late_spec, late_spec],
        out_shape=[jax.ShapeDtypeStruct((sub, dilation * DIL_OUT_WIDTH), BF16)] * 3,
        scratch_shapes=[pltpu.VMEM((BLOCK, DIL_OUT_WIDTH), F32)] * 2,
        compiler_params=_cparams(2),
    )
    grads, lands = res if rider is not None else (res, None)
    grads = tuple(g.reshape(s, DIL_OUT_WIDTH) for g in grads)
    return grads if rider is None else (grads, lands)


def _head_block_ones():
    r = lax.broadcasted_iota(jnp.int32, (DIL_OUT_WIDTH, DIL_OUT_WIDTH), 0) // HEAD_DIM
    c = lax.broadcasted_iota(jnp.int32, (DIL_OUT_WIDTH, DIL_OUT_WIDTH), 1) // HEAD_DIM
    return jnp.where(r == c, 1.0, 0.0).astype(BF16)


def _dil_mix_weights(l0, l1, l2):
    mx = jnp.maximum(jnp.maximum(l0, l1), l2)
    e0, e1, e2 = jnp.exp(l0 - mx), jnp.exp(l1 - mx), jnp.exp(l2 - mx)
    inv = 1.0 / (e0 + e1 + e2)
    return e0 * inv, e1 * inv, e2 * inv


def _dil_mix_fwd(os_, lses, tm):
    def epi(_, rows, consts):
        o0, o1, o2, l0, l1, l2 = rows
        w0, w1, w2 = _dil_mix_weights(l0, l1, l2)
        return [w0 * o0 + w1 * o1 + w2 * o2], []

    (o_a,) = _rowk("dil_mix_fwd", tm=tm, rows=list(os_) + list(lses), row_outs=[(DIL_OUT_WIDTH, BF16)], epilogue=epi)
    return o_a


def _dil_mix_bwd(do_a, os_, lses, tm):
    def epi(_, rows, consts):
        do, o0, o1, o2, l0, l1, l2 = rows
        do = do.astype(F32)
        w0, w1, w2 = _dil_mix_weights(l0, l1, l2)
        mixed = w0 * o0 + w1 * o1 + w2 * o2
        tot = _dot_hi_lo(do * mixed, _head_block_ones())
        return [w0 * do, w1 * do, w2 * do, -w0 * tot, -w1 * tot, -w2 * tot], []

    return _rowk(
        "dil_mix_bwd", tm=tm, rows=[do_a] + list(os_) + list(lses),
        row_outs=[(DIL_OUT_WIDTH, BF16)] * 3 + [(DIL_OUT_WIDTH, F32)] * 3, epilogue=epi)


_SB_Q0 = 3 * DIL_WIDTH // LANES
_SB_K0 = _SB_Q0 + SB_WIDTH // LANES
_SB_V0 = _SB_K0 + SB_WIDTH // LANES


_EXP_CLAMP = 88.0
_SB_DEAD = 104.0


def _tri(t, op):
    r = lax.broadcasted_iota(jnp.int32, (t, t), 0)
    c = lax.broadcasted_iota(jnp.int32, (t, t), 1)
    return jnp.where(op(r, c), 1.0, 0.0).astype(BF16)


def _softplus(z):
    return jnp.maximum(z, jnp.log(1.0 + jnp.exp(jnp.minimum(z, _EXP_CLAMP))))


def _sb_chain_head(qm, kj, mask):
    z = _dot_nt(qm, kj)
    sp = _softplus(z)
    return (sp if mask is None else jnp.where(mask, sp, 0.0)), z - sp


def _sb_fwd(qkv, rider=None):
    s = qkv.shape[0]
    t = SB_TK
    assert s % (2 * t) == 0
    nq = s // (2 * t)
    n_pairs = SB_WIDTH // LANES

    def body(q_ref, k_ref, v_ref, o_ref, tot_ref, steps_ref):
        p, i = pl.program_id(0), pl.program_id(1)
        lane_hi = lax.broadcasted_iota(jnp.int32, (1, LANES), 1) // HEAD_DIM
        later = _tri(t, lambda r, c: r > c)
        causal = lax.broadcasted_iota(jnp.int32, (t, t), 1) < lax.broadcasted_iota(jnp.int32, (t, t), 0)
        qms = []
        for x in range(2):
            q = q_ref[pl.ds(x * t, t), :] * (1.0 / math.sqrt(HEAD_DIM))
            qms.append([jnp.where(lane_hi == hh, q, jnp.zeros_like(q)) for hh in range(2)])

        def tile(j):
            off = pl.multiple_of(j * t, t)
            return k_ref[pl.ds(off, t), :], v_ref[pl.ds(off, t), :]

        def step(tiles, carry, diag):
            chains = [(x, hh) for x in range(2) if tiles[x] is not None for hh in range(2)]
            kv = {x: tile(tiles[x]) for x in range(2) if tiles[x] is not None}
            heads = [_sb_chain_head(qms[x][hh], kv[x][0], causal if diag else None) for x, hh in chains]
            sufs = [_dot(sp.astype(BF16), later) for sp, _ in heads]
            new = [list(carry[0]), list(carry[1])]
            for (x, hh), (sp, lpos), suf in zip(chains, heads, sufs):
                c, acc = carry[x][hh]
                a = jnp.exp(lpos - suf - c)
                if diag:
                    a = jnp.where(causal, a, 0.0)
                new[x][hh] = (c + jnp.sum(sp, axis=1, keepdims=True), acc + _dot(a.astype(BF16), kv[x][1]))
            return (tuple(new[0]), tuple(new[1]))

        def lowest(carry):
            m = [jnp.min(carry[x][hh][0]) for x in range(2) for hh in range(2)]
            return jnp.minimum(jnp.minimum(m[0], m[1]), jnp.minimum(m[2], m[3]))

        zero = (jnp.zeros((t, 1), F32), jnp.zeros((t, LANES), F32))
        carry = step((2 * i, 2 * i + 1), ((zero, zero), (zero, zero)), True)

        n_full, carry = lax.while_loop(
            lambda st: jnp.logical_and(st[0] < 2 * i, lowest(st[1]) <= _SB_DEAD),
            lambda st: (st[0] + 1, step((2 * i - 1 - st[0], 2 * i - st[0]), st[1], False)),
            (jnp.int32(0), carry))
        b_last = jnp.logical_and(n_full == 2 * i, lowest(carry) <= _SB_DEAD)
        carry = lax.cond(b_last, lambda ca: step((None, 0), ca, False), lambda ca: ca, carry)
        for x in range(2):
            (c0, acc0), (c1, acc1) = carry[x]
            o_ref[pl.ds(x * t, t), :] = jnp.where(lane_hi == 0, acc0, acc1).astype(o_ref.dtype)
            tot_ref[pl.ds(x * t, t), :] = jnp.where(lane_hi == 0, c0, c1)
        steps_ref[p, i] = n_full + b_last.astype(jnp.int32)

    return _call(
        body, (qkv, qkv, qkv), rider,
        name="sb_fwd",
        grid=(n_pairs, nq),
        in_specs=[
            pl.BlockSpec((2 * t, LANES), lambda p, i: (i, _SB_Q0 + p)),
            pl.BlockSpec((s, LANES), lambda p, i: (0, _SB_K0 + p)),
            pl.BlockSpec((s, LANES), lambda p, i: (0, _SB_V0 + p)),
        ],
        out_specs=[pl.BlockSpec((2 * t, LANES), lambda p, i: (i, p))] * 2 + [pl.BlockSpec(memory_space=pltpu.SMEM)],
        out_shape=[jax.ShapeDtypeStruct((s, SB_WIDTH), BF16), jax.ShapeDtypeStruct((s, SB_WIDTH), F32),
                   jax.ShapeDtypeStruct((n_pairs, nq), jnp.int32)],
        compiler_params=_cparams(2),
    )


def _sb_bwd(qkv, do_b, tot_b, n_steps):
    s = qkv.shape[0]
    t = SB_TK
    nq = s // (2 * t)
    n_pairs = SB_WIDTH // LANES
    scale = 1.0 / math.sqrt(HEAD_DIM)

    def body(steps_ref, q_ref, k_ref, v_ref, do_ref, tot_ref, dq_ref, dk_ref, dv_ref):
        p, i = pl.program_id(0), pl.program_id(1)

        @pl.when(i == 0)
        def _():
            dk_ref[...] = jnp.zeros_like(dk_ref)
            dv_ref[...] = jnp.zeros_like(dv_ref)

        lane = lax.broadcasted_iota(jnp.int32, (1, LANES), 1)
        lane_hi = lane // HEAD_DIM
        later = _tri(t, lambda r, c: r > c)
        before = _tri(t, lambda r, c: r < c)
        causal = lax.broadcasted_iota(jnp.int32, (t, t), 1) < lax.broadcasted_iota(jnp.int32, (t, t), 0)
        qms, doms, tots = [], [], []
        for x in range(2):
            rows = pl.ds(x * t, t)
            q, do, tot_all = q_ref[rows, :] * scale, do_ref[rows, :], tot_ref[rows, :]
            qms.append([jnp.where(lane_hi == hh, q, jnp.zeros_like(q)) for hh in range(2)])
            doms.append([jnp.where(lane_hi == hh, do, jnp.zeros_like(do)) for hh in range(2)])
            tots.append([jnp.sum(jnp.where(lane == hh * HEAD_DIM, tot_all, 0.0), axis=1, keepdims=True)
                         for hh in range(2)])

        def step(tiles, carry, diag):
            chains = [(x, hh) for x in range(2) if tiles[x] is not None for hh in range(2)]
            offs = {x: pl.multiple_of(tiles[x] * t, t) for x in range(2) if tiles[x] is not None}
            ks = {x: k_ref[pl.ds(off, t), :] for x, off in offs.items()}
            vs = {x: v_ref[pl.ds(off, t), :] for x, off in offs.items()}
            heads = [_sb_chain_head(qms[x][hh], ks[x], causal if diag else None) for x, hh in chains]
            sufs = [_dot(sp.astype(BF16), later) for sp, _ in heads]
            das = [_dot_nt(doms[x][hh], vs[x]) for x, hh in chains]
            new = [list(carry[0]), list(carry[1])]
            sigs, gs, abs_ = [], [], []
            for (x, hh), (sp, lpos), suf, da in zip(chains, heads, sufs, das):
                cl = carry[x][hh][0] + jnp.sum(sp, axis=1, keepdims=True)
                sig = jnp.exp(lpos)
                a = sig * jnp.exp(-suf - (tots[x][hh] - cl))
                if diag:
                    a = jnp.where(causal, a, 0.0)
                g = a * da
                sigs.append(sig)
                gs.append(g)
                abs_.append(a.astype(BF16))
                new[x][hh] = (cl, carry[x][hh][1] + jnp.sum(g, axis=1, keepdims=True), carry[x][hh][2])
            prefs = [_dot(g.astype(BF16), before) for g in gs]
            dvs = [_dot_tn(ab, doms[x][hh]) for (x, hh), ab in zip(chains, abs_)]
            dzs = []
            for (x, hh), sig, g, pref in zip(chains, sigs, gs, prefs):
                dz = g - sig * (g + pref + carry[x][hh][1])
                if diag:
                    dz = jnp.where(causal, dz, 0.0)
                dzs.append(dz.astype(BF16))
            dqs = [_dot(dz, ks[x]) for (x, hh), dz in zip(chains, dzs)]
            dks = [_dot_tn(dz, qms[x][hh]) for (x, hh), dz in zip(chains, dzs)]
            for n, (x, hh) in enumerate(chains):
                cl, cg, dq = new[x][hh]
                new[x][hh] = (cl, cg, dq + dqs[n])
            for x in offs:
                mine = [n for n, ch in enumerate(chains) if ch[0] == x]
                dk_ref[pl.ds(offs[x], t), :] += dks[mine[0]] + dks[mine[1]]
                dv_ref[pl.ds(offs[x], t), :] += dvs[mine[0]] + dvs[mine[1]]
            return (tuple(new[0]), tuple(new[1]))

        taken = steps_ref[p, i]
        n_full = jnp.minimum(taken, 2 * i)
        zero = (jnp.zeros((t, 1), F32), jnp.zeros((t, 1), F32), jnp.zeros((t, LANES), F32))
        carry = ((zero, zero), (zero, zero))
        carry = lax.cond(taken > 2 * i, lambda ca: step((None, 0), ca, False), lambda ca: ca, carry)
        carry = lax.fori_loop(
            0, n_full, lambda n, ca: step((2 * i - n_full + n, 2 * i + 1 - n_full + n), ca, False), carry)
        carry = step((2 * i, 2 * i + 1), carry, True)
        for x in range(2):
            dq = jnp.where(lane_hi == 0, carry[x][0][2], carry[x][1][2])
            dq_ref[pl.ds(x * t, t), :] = (dq * scale).astype(dq_ref.dtype)

    row_spec = pl.BlockSpec((2 * t, LANES), lambda p, i, ns: (i, p))
    full_spec = pl.BlockSpec((s, LANES), lambda p, i, ns: (0, p))
    return _pcall(
        body,
        name="sb_bwd",
        grid_spec=pltpu.PrefetchScalarGridSpec(
            num_scalar_prefetch=1,
            grid=(n_pairs, nq),
            in_specs=[
                pl.BlockSpec((2 * t, LANES), lambda p, i, ns: (i, _SB_Q0 + p)),
                pl.BlockSpec((s, LANES), lambda p, i, ns: (0, _SB_K0 + p)),
                pl.BlockSpec((s, LANES), lambda p, i, ns: (0, _SB_V0 + p)),
                row_spec, row_spec,
            ],
            out_specs=[row_spec, full_spec, full_spec],
        ),
        out_shape=[jax.ShapeDtypeStruct((s, SB_WIDTH), BF16), jax.ShapeDtypeStruct((s, SB_WIDTH), F32),
                   jax.ShapeDtypeStruct((s, SB_WIDTH), F32)],
        compiler_params=_cparams(2),
    )(n_steps, qkv, qkv, qkv, do_b, tot_b)


def _sb_fwd_wide(qkv):
    s = qkv.shape[0]
    t, tq = SB_TK, min(SB_TQ_FWD, s)
    assert tq in (t, 2 * t) and s % (2 * t) == 0
    nq = s // tq
    n_pairs = SB_WIDTH // LANES

    def body(q_ref, k_ref, v_ref, o_ref, tot_ref, steps_ref):
        p, i = pl.program_id(0), pl.program_id(1)
        q = q_ref[...] * (1.0 / math.sqrt(HEAD_DIM))
        lane_hi = lax.broadcasted_iota(jnp.int32, (1, LANES), 1) // HEAD_DIM
        later = _tri(t, lambda r, c: r > c)
        row = lax.broadcasted_iota(jnp.int32, (tq, t), 0)
        col = lax.broadcasted_iota(jnp.int32, (tq, t), 1)
        qms = [jnp.where(lane_hi == hh, q, jnp.zeros_like(q)) for hh in range(2)]

        def step(jj, carry, diag):
            tiles = (2 * jj + 1, 2 * jj)
            offs = [pl.multiple_of(j * t, t) for j in tiles]
            ks = [k_ref[pl.ds(off, t), :] for off in offs]
            vs = [v_ref[pl.ds(off, t), :] for off in offs]
            masks = [(j * t + col) < (i * tq + row) for j in tiles] if diag else None
            chains = [(n, hh) for n in range(2) for hh in range(2)]
            zs = [_dot_nt(qms[hh], ks[n]) for n, hh in chains]
            sps, lposs = [], []
            for (n, hh), z in zip(chains, zs):
                sp = _softplus(z)
                lposs.append(z - sp)
                sps.append(jnp.where(masks[n], sp, 0.0) if diag else sp)
            sufs = [_dot(sp.astype(BF16), later) for sp in sps]
            cs = [carry[0], carry[2]]
            accs = [carry[1], carry[3]]
            for idx, (n, hh) in enumerate(chains):
                a = jnp.exp(lposs[idx] - sufs[idx] - cs[hh])
                if diag:
                    a = jnp.where(masks[n], a, 0.0)
                accs[hh] = accs[hh] + _dot(a.astype(BF16), vs[n])
                cs[hh] = cs[hh] + jnp.sum(sps[idx], axis=1, keepdims=True)
            return cs[0], accs[0], cs[1], accs[1]

        zc, za = jnp.zeros((tq, 1), F32), jnp.zeros((tq, LANES), F32)
        last = (i * tq) // (2 * t)
        carry = step(last, (zc, za, zc, za), True)

        def alive(state):
            n, ca = state
            return jnp.logical_and(n < last, jnp.minimum(jnp.min(ca[0]), jnp.min(ca[2])) <= _SB_DEAD)

        n_off, carry = lax.while_loop(alive, lambda st: (st[0] + 1, step(last - 1 - st[0], st[1], False)),
                                      (jnp.int32(0), carry))
        out = jnp.where(lane_hi == 0, carry[1], carry[3])
        tot = jnp.where(lane_hi == 0, carry[0], carry[2])
        o_ref[...] = out.astype(o_ref.dtype)
        tot_ref[...] = tot
        steps_ref[p, i] = n_off

    o, tot, n_steps = _pcall(
        body,
        name="sb_fwd",
        grid=(n_pairs, nq),
        in_specs=[
            pl.BlockSpec((tq, LANES), lambda p, i: (i, _SB_Q0 + p)),
            pl.BlockSpec((s, LANES), lambda p, i: (0, _SB_K0 + p)),
            pl.BlockSpec((s, LANES), lambda p, i: (0, _SB_V0 + p)),
        ],
        out_specs=[pl.BlockSpec((tq, LANES), lambda p, i: (i, p))] * 2 + [pl.BlockSpec(memory_space=pltpu.SMEM)],
        out_shape=[jax.ShapeDtypeStruct((s, SB_WIDTH), BF16), jax.ShapeDtypeStruct((s, SB_WIDTH), F32),
                   jax.ShapeDtypeStruct((n_pairs, nq), jnp.int32)],
        compiler_params=_cparams(2),
    )(qkv, qkv, qkv)
    return o, tot, n_steps


def _sb_bwd_wide(qkv, do_b, tot_b, n_steps):
    s = qkv.shape[0]
    t, tq = SB_TK, min(SB_TQ_BWD, s)
    assert tq in (t, 2 * t) and s % (2 * t) == 0
    nq = s // tq
    assert nq % n_steps.shape[1] == 0
    n_pairs = SB_WIDTH // LANES
    scale = 1.0 / math.sqrt(HEAD_DIM)

    def body(steps_ref, q_ref, k_ref, v_ref, do_ref, tot_ref, dq_ref, dk_ref, dv_ref):
        p, i = pl.program_id(0), pl.program_id(1)

        @pl.when(i == 0)
        def _():
            dk_ref[...] = jnp.zeros_like(dk_ref)
            dv_ref[...] = jnp.zeros_like(dv_ref)

        q = q_ref[...] * scale
        do = do_ref[...]
        tot_all = tot_ref[...]
        lane = lax.broadcasted_iota(jnp.int32, (1, LANES), 1)
        lane_hi = lane // HEAD_DIM
        later = _tri(t, lambda r, c: r > c)
        before = _tri(t, lambda r, c: r < c)
        row = lax.broadcasted_iota(jnp.int32, (tq, t), 0)
        col = lax.broadcasted_iota(jnp.int32, (tq, t), 1)
        qms = [jnp.where(lane_hi == hh, q, jnp.zeros_like(q)) for hh in range(2)]
        doms = [jnp.where(lane_hi == hh, do, jnp.zeros_like(do)) for hh in range(2)]
        tots = [jnp.sum(jnp.where(lane == hh * HEAD_DIM, tot_all, 0.0), axis=1, keepdims=True) for hh in range(2)]

        def step(jj, carry, diag):
            tiles = (2 * jj, 2 * jj + 1)
            offs = [pl.multiple_of(j * t, t) for j in tiles]
            ks = [k_ref[pl.ds(off, t), :] for off in offs]
            vs = [v_ref[pl.ds(off, t), :] for off in offs]
            masks = [(j * t + col) < (i * tq + row) for j in tiles] if diag else None
            chains = [(n, hh) for n in range(2) for hh in range(2)]
            zs = [_dot_nt(qms[hh], ks[n]) for n, hh in chains]
            sps, sigs = [], []
            for (n, hh), z in zip(chains, zs):
                sp = _softplus(z)
                sigs.append(jnp.exp(z - sp))
                sps.append(jnp.where(masks[n], sp, 0.0) if diag else sp)
            sufs = [_dot(sp.astype(BF16), later) for sp in sps]
            das = [_dot_nt(doms[hh], vs[n]) for n, hh in chains]
            cls = [carry[0], carry[3]]
            cgs = [carry[1], carry[4]]
            accs = [carry[2], carry[5]]
            gs, abs_, cg_at = [], [], []
            for idx, (n, hh) in enumerate(chains):
                cls[hh] = cls[hh] + jnp.sum(sps[idx], axis=1, keepdims=True)
                a = sigs[idx] * jnp.exp(-sufs[idx] - (tots[hh] - cls[hh]))
                if diag:
                    a = jnp.where(masks[n], a, 0.0)
                g = a * das[idx]
                gs.append(g)
                abs_.append(a.astype(BF16))
                cg_at.append(cgs[hh])
                cgs[hh] = cgs[hh] + jnp.sum(g, axis=1, keepdims=True)
            prefs = [_dot(g.astype(BF16), before) for g in gs]
            dvs = [_dot_tn(abs_[idx], doms[hh]) for idx, (n, hh) in enumerate(chains)]
            dzs = []
            for idx, (n, hh) in enumerate(chains):
                g = gs[idx]
                dz = g - sigs[idx] * (g + prefs[idx] + cg_at[idx])
                if diag:
                    dz = jnp.where(masks[n], dz, 0.0)
                dzs.append(dz.astype(BF16))
            for idx, (n, hh) in enumerate(chains):
                accs[hh] = accs[hh] + _dot(dzs[idx], ks[n])
            dks = [_dot_tn(dzs[idx], qms[hh]) for idx, (n, hh) in enumerate(chains)]
            for n in range(2):
                dk_ref[pl.ds(offs[n], t), :] += dks[2 * n] + dks[2 * n + 1]
                dv_ref[pl.ds(offs[n], t), :] += dvs[2 * n] + dvs[2 * n + 1]
            return cls[0], cgs[0], accs[0], cls[1], cgs[1], accs[1]

        zc, za = jnp.zeros((tq, 1), F32), jnp.zeros((tq, LANES), F32)
        last = (i * tq) // (2 * t)
        first = last - steps_ref[p, (i * n_steps.shape[1]) // nq]
        carry = lax.fori_loop(first, last, lambda jj, ca: step(jj, ca, False), (zc, zc, za, zc, zc, za))
        carry = step(last, carry, True)
        dq = jnp.where(lane_hi == 0, carry[2], carry[5])
        dq_ref[...] = (dq * scale).astype(dq_ref.dtype)

    row_spec = pl.BlockSpec((tq, LANES), lambda p, i, ns: (i, p))
    full_spec = pl.BlockSpec((s, LANES), lambda p, i, ns: (0, p))
    return _pcall(
        body,
        name="sb_bwd",
        grid_spec=pltpu.PrefetchScalarGridSpec(
            num_scalar_prefetch=1,
            grid=(n_pairs, nq),
            in_specs=[
                pl.BlockSpec((tq, LANES), lambda p, i, ns: (i, _SB_Q0 + p)),
                pl.BlockSpec((s, LANES), lambda p, i, ns: (0, _SB_K0 + p)),
                pl.BlockSpec((s, LANES), lambda p, i, ns: (0, _SB_V0 + p)),
                row_spec, row_spec,
            ],
            out_specs=[row_spec, full_spec, full_spec],
        ),
        out_shape=[jax.ShapeDtypeStruct((s, SB_WIDTH), BF16), jax.ShapeDtypeStruct((s, SB_WIDTH), F32),
                   jax.ShapeDtypeStruct((s, SB_WIDTH), F32)],
        compiler_params=_cparams(2),
    )(n_steps, qkv, qkv, qkv, do_b, tot_b)


def _gates(gl, bg):
    return _sigmoid(gl[:, :D_MODEL] + bg[:, :D_MODEL]), _sigmoid(gl[:, D_MODEL:] + bg[:, D_MODEL:])


def _mixer_fwd(o_a, o_b, gl, x0, bg, g2, w_ud, w_us, w_out, tm):
    def epi(_, rows, consts):
        oa, ob, glv, x = rows
        bgv, g2v, wud, wus, wout = consts
        ga, gb = _gates(glv, bgv)
        merged = ga * _dot(oa, wud) + gb * _dot(ob, wus)
        x1 = x + _dot(merged.astype(BF16), wout)
        r, xh = _rms_stats(x1)
        return [x1, xh * g2v], []

    return _rowk("mixer_fwd", tm=tm, rows=[o_a, o_b, gl, x0], consts=[bg, g2, w_ud, w_us, w_out],
                 row_outs=[(D_MODEL, F32), (D_MODEL, BF16)], epilogue=epi)


def _mixer_bwd(dx1, o_a, o_b, gl, bg, w_ud, w_us, w_out, tm, rider=None):
    s = dx1.shape[0]
    nm = s // tm

    def body(dx_ref, oa_ref, ob_ref, gl_ref, bg_ref, wud_ref, wus_ref, wout_ref,
             doa_ref, dob_ref, dgl_ref, gwout_ref, gwud_ref, gwus_ref, gbg_ref):
        i = pl.program_id(0)
        dxb = dx_ref[...].astype(BF16)
        oa, ob = oa_ref[...], ob_ref[...]
        ga, gb = _gates(gl_ref[...], bg_ref[...])
        ua, ub = _dot(oa, wud_ref[...]), _dot(ob, wus_ref[...])
        merged = (ga * ua + gb * ub).astype(BF16)
        dm = _dot_nt(dxb, wout_ref[...])
        dua = (dm * ga).astype(BF16)
        dub = (dm * gb).astype(BF16)
        dgla = dm * ua * ga * (1.0 - ga)
        dglb = dm * ub * gb * (1.0 - gb)
        doa_ref[...] = _dot_nt(dua, wud_ref[...]).astype(doa_ref.dtype)
        dob_ref[...] = _dot_nt(dub, wus_ref[...]).astype(dob_ref.dtype)
        dgl_ref[:, :D_MODEL] = dgla.astype(dgl_ref.dtype)
        dgl_ref[:, D_MODEL:] = dglb.astype(dgl_ref.dtype)
        parts = [(gwout_ref, _dot_tn(merged, dxb)), (gwud_ref, _dot_tn(oa, dua)), (gwus_ref, _dot_tn(ob, dub))]
        for r, v in parts:

            @pl.when(i == 0)
            def _(r=r, v=v):
                r[...] = v

            @pl.when(i > 0)
            def _(r=r, v=v):
                r[...] += v

        sa = jnp.sum(dgla, axis=0, keepdims=True)
        sb = jnp.sum(dglb, axis=0, keepdims=True)

        @pl.when(i == 0)
        def _():
            gbg_ref[:, :D_MODEL] = sa
            gbg_ref[:, D_MODEL:] = sb

        @pl.when(i > 0)
        def _():
            gbg_ref[:, :D_MODEL] += sa
            gbg_ref[:, D_MODEL:] += sb

    row = lambda w: pl.BlockSpec((tm, w), lambda i: (i, 0))
    full = lambda a: pl.BlockSpec(a.shape, lambda i: (0, 0))
    fshape = lambda r, c: jax.ShapeDtypeStruct((r, c), F32)
    return _call(
        body, (dx1, o_a, o_b, gl, bg, w_ud, w_us, w_out), rider,
        name="mixer_bwd",
        grid=(nm,),
        in_specs=[row(D_MODEL), row(DIL_OUT_WIDTH), row(SB_WIDTH), row(2 * D_MODEL),
                  full(bg), full(w_ud), full(w_us), full(w_out)],
        out_specs=[row(DIL_OUT_WIDTH), row(SB_WIDTH), row(2 * D_MODEL),
                   pl.BlockSpec((D_MODEL, D_MODEL), lambda i: (0, 0)),
                   pl.BlockSpec((DIL_OUT_WIDTH, D_MODEL), lambda i: (0, 0)),
                   pl.BlockSpec((SB_WIDTH, D_MODEL), lambda i: (0, 0)),
                   pl.BlockSpec((1, 2 * D_MODEL), lambda i: (0, 0))],
        out_shape=[jax.ShapeDtypeStruct((s, DIL_OUT_WIDTH), BF16), jax.ShapeDtypeStruct((s, SB_WIDTH), BF16),
                   jax.ShapeDtypeStruct((s, 2 * D_MODEL), BF16),
                   fshape(D_MODEL, D_MODEL), fshape(DIL_OUT_WIDTH, D_MODEL), fshape(SB_WIDTH, D_MODEL),
                   fshape(1, 2 * D_MODEL)],
        compiler_params=_cparams(1),
    )


_HBM = pl.BlockSpec(memory_space=pltpu.HBM)
_MESH = pl.DeviceIdType.MESH


def _all_gather(shards):
    n = len(shards)

    def body(*refs):
        x_refs, out_refs = refs[:n], refs[n:2 * n]
        send_sems, recv_sems, local_sems = refs[2 * n:]
        x, y, c = lax.axis_index("x"), lax.axis_index("y"), lax.axis_index("c")
        me, sibling = (x, y, c), (x, y, 1 - c)
        chips = [(1 - x, y), (x, 1 - y), (1 - x, 1 - y)]

        def slot(a, px, py, pc):
            return out_refs[a].at[4 * px + 2 * py + pc]

        def copy(a, k, block, to, own=False):
            return pltpu.make_async_remote_copy(
                src_ref=x_refs[a] if own else slot(a, *block), dst_ref=slot(a, *block),
                send_sem=send_sems.at[7 * a + k], recv_sem=recv_sems.at[7 * a + k], device_id=to, device_id_type=_MESH)

        mine = [pltpu.make_async_copy(x_refs[a], slot(a, *me), local_sems.at[a]) for a in range(n)]
        for cp in mine:
            cp.start()
        first = []
        for a in range(n):
            first.append(copy(a, 0, me, sibling, own=True))
            first += [copy(a, 1 + j, me, (*chip, c), own=True) for j, chip in enumerate(chips)]
        for cp in first:
            cp.start()
        passed = []
        for a in range(n):
            for j, chip in enumerate(chips):
                copy(a, 1 + j, (*chip, c), me).wait_recv()
                passed.append(copy(a, 4 + j, (*chip, c), sibling))
                passed[-1].start()
        for a in range(n):
            copy(a, 0, sibling, me).wait_recv()
            for j, chip in enumerate(chips):
                copy(a, 4 + j, (*chip, 1 - c), me).wait_recv()
        for cp in first + passed:
            cp.wait_send()
        for cp in mine:
            cp.wait()

    return _pcall(
        body,
        name="all_gather_weights",
        in_specs=[_HBM] * n,
        out_specs=[_HBM] * n,
        out_shape=[jax.ShapeDtypeStruct((N_DEV,) + s.shape, s.dtype) for s in shards],
        scratch_shapes=[pltpu.SemaphoreType.DMA((7 * n,)), pltpu.SemaphoreType.DMA((7 * n,)),
                        pltpu.SemaphoreType.DMA((n,))],
    )(*shards)


def _exchange(chunks):
    n = len(chunks)

    def body(*refs):
        g_refs, o_refs = refs[:n], refs[n:2 * n]
        send_sems, recv_sems, local_sems = refs[2 * n:]
        x, y, c = lax.axis_index("x"), lax.axis_index("y"), lax.axis_index("c")
        me = 4 * x + 2 * y + c
        own = [pltpu.make_async_copy(g_refs[a].at[me], o_refs[a].at[me], local_sems.at[a]) for a in range(n)]
        for cp in own:
            cp.start()
        copies = []
        for a in range(n):
            for k in range(1, N_DEV):
                px, py, pc = x ^ (k >> 2), y ^ ((k >> 1) & 1), c ^ (k & 1)
                peer = 4 * px + 2 * py + pc
                copies.append(pltpu.make_async_remote_copy(
                    src_ref=g_refs[a].at[peer], dst_ref=o_refs[a].at[me], send_sem=send_sems.at[7 * a + k - 1],
                    recv_sem=recv_sems.at[7 * a + k - 1], device_id=(px, py, pc), device_id_type=_MESH))
        for cp in copies:
            cp.start()
        for cp in copies:
            cp.wait()
        for cp in own:
            cp.wait()

    return _pcall(
        body,
        name="exchange_grads",
        in_specs=[_HBM] * n,
        out_specs=[_HBM] * n,
        out_shape=[jax.ShapeDtypeStruct(g.shape, g.dtype) for g in chunks],
        scratch_shapes=[pltpu.SemaphoreType.DMA((7 * n,)), pltpu.SemaphoreType.DMA((7 * n,)),
                        pltpu.SemaphoreType.DMA((n,))],
    )(*chunks)


_SEM = pl.BlockSpec(memory_space=pltpu.SEMAPHORE)
_EFFECT = pltpu.SideEffectType.DATAFLOW_SIDE_EFFECTING


def _peers(x, y, c):
    out = []
    for k in range(1, N_DEV):
        px, py, pc = x ^ (k >> 2), y ^ ((k >> 1) & 1), c ^ (k & 1)
        out.append(((px, py, pc), 4 * px + 2 * py + pc))
    return out


def _spread_copies(src_refs, land_refs, send_sems, recv_sems, chunked):
    x, y, c = lax.axis_index("x"), lax.axis_index("y"), lax.axis_index("c")
    me = 4 * x + 2 * y + c
    copies = []
    for a, (src, land) in enumerate(zip(src_refs, land_refs)):
        for k, (peer_id, peer) in enumerate(_peers(x, y, c)):
            copies.append(pltpu.make_async_remote_copy(
                src_ref=src.at[peer] if chunked else src, dst_ref=land.at[me], send_sem=send_sems.at[7 * a + k],
                recv_sem=recv_sems.at[7 * a + k], device_id=peer_id, device_id_type=_MESH))
    return copies


def _spread_start(name, srcs, chunked):
    n = len(srcs)
    lands = [lax.empty((N_DEV,) + (s.shape[1:] if chunked else s.shape), s.dtype) for s in srcs]

    def body(*refs):
        src_refs, land_refs = refs[:n], refs[n:2 * n]
        send_sems, recv_sems = refs[2 * n], refs[2 * n + 1]
        token = refs[-1]
        for cp in _spread_copies(src_refs, land_refs, send_sems, recv_sems, chunked):
            cp.start()
        token[...] = jnp.zeros_like(token)

    hbm = lambda a: pltpu.HBM(a.shape, a.dtype)
    outs = _pcall(
        body,
        name=name,
        out_shape=(pltpu.SemaphoreType.DMA((7 * n,)), pltpu.SemaphoreType.DMA((7 * n,)),
                   *[hbm(s) for s in srcs], *[hbm(l) for l in lands], jax.ShapeDtypeStruct((8, LANES), F32)),
        in_specs=[_HBM] * (2 * n),
        out_specs=(_SEM, _SEM, *([_HBM] * (2 * n)), pl.BlockSpec(memory_space=pltpu.VMEM)),
        input_output_aliases={i: 2 + i for i in range(2 * n)},
        compiler_params=pltpu.CompilerParams(has_side_effects=_EFFECT),
    )(*[pltpu.with_memory_space_constraint(a, pltpu.HBM) for a in list(srcs) + lands])
    return outs[0], outs[1], list(outs[2:2 + n]), list(outs[2 + n:2 + 2 * n]), outs[-1]


def _spread_wait(name, send_sems, recv_sems, srcs, lands, after, chunked):
    n = len(srcs)

    def body(*refs):
        src_refs, land_refs = refs[:n], refs[n:2 * n]
        for cp in _spread_copies(src_refs, land_refs, refs[2 * n], refs[2 * n + 1], chunked):
            cp.wait_send()
            cp.wait_recv()

    hbm = lambda a: pltpu.HBM(a.shape, a.dtype)
    outs = _pcall(
        body,
        name=name,
        out_shape=tuple(hbm(a) for a in list(srcs) + list(lands)),
        in_specs=[_HBM] * (2 * n) + [_SEM, _SEM, pl.BlockSpec(memory_space=pl.ANY)],
        out_specs=tuple([_HBM] * (2 * n)),
        input_output_aliases={i: i for i in range(2 * n)},
        compiler_params=pltpu.CompilerParams(has_side_effects=_EFFECT),
    )(*srcs, *lands, send_sems, recv_sems, after)
    return list(outs[:n]), list(outs[n:])


def _with_own(land, own):
    me = 4 * lax.axis_index("x") + 2 * lax.axis_index("y") + lax.axis_index("c")
    return lax.dynamic_update_slice(land, own[None], (me,) + (0,) * own.ndim)


def _reduce_adamw(name, parts, w, m, v, tr):
    _, rows, cols = parts.shape
    tr = min(tr, rows)
    assert rows % tr == 0
    c1 = 1.0 / (1.0 - ADAM_B1 ** ADAM_STEP)
    c2 = 1.0 / (1.0 - ADAM_B2 ** ADAM_STEP)

    def body(p_ref, w_ref, m_ref, v_ref, g_out, d_out, m_out, v_out):
        g = p_ref[0].astype(F32)
        for d in range(1, N_DEV):
            g = g + p_ref[d].astype(F32)
        mn = ADAM_B1 * m_ref[...] + (1.0 - ADAM_B1) * g
        vn = ADAM_B2 * v_ref[...] + (1.0 - ADAM_B2) * (g * g)
        g_out[...] = g
        m_out[...] = mn
        v_out[...] = vn
        d_out[...] = -ADAM_LR * ((mn * c1) / (jnp.sqrt(vn * c2) + ADAM_EPS) + ADAM_WD * w_ref[...])

    spec = pl.BlockSpec((tr, cols), lambda i: (i, 0))
    return _pcall(
        body,
        name=name,
        grid=(rows // tr,),
        in_specs=[pl.BlockSpec((N_DEV, tr, cols), lambda i: (0, i, 0)), spec, spec, spec],
        out_specs=[spec] * 4,
        out_shape=[jax.ShapeDtypeStruct((rows, cols), F32)] * 4,
        compiler_params=_cparams(1),
    )(parts, w, m, v)


_SHARDED = ("w_in", "w_up_dil", "w_up_sb", "w_out", "w_mlp_in", "w_mlp_out")
_FULL_SHAPES = {"w_in": (D_MODEL, IN_COLS), "w_up_dil": (DIL_OUT_WIDTH, D_MODEL), "w_up_sb": (SB_WIDTH, D_MODEL),
                "w_out": (D_MODEL, D_MODEL), "w_mlp_in": (D_MODEL, D_FF), "w_mlp_out": (D_FF, D_MODEL)}
_ROW_SHARDED = ("w_out", "w_mlp_out")


def _shard_shape(name):
    r, c = _FULL_SHAPES[name]
    return (r // N_DEV, c) if name in _ROW_SHARDED else (r, c // N_DEV)


def _assemble(name, gathered):
    r, c = _shard_shape(name)
    if name in _ROW_SHARDED:
        return gathered.reshape(N_DEV * r, c)
    return gathered.transpose(1, 0, 2).reshape(r, N_DEV * c)


def _chunk(name, full):
    r, c = _shard_shape(name)
    if name in _ROW_SHARDED:
        return full.reshape(N_DEV, r, c)
    return full.reshape(r, N_DEV, c).transpose(1, 0, 2)


_SMALL = (("norm_mix_g", D_MODEL), ("b_gate", 2 * D_MODEL), ("norm_mlp_g", D_MODEL), ("norm_final_g", D_MODEL))
_SMALL_N = sum(n for _, n in _SMALL) + LANES


def _pack_small(vals, tail):
    return jnp.concatenate([vals[n].reshape(1, -1) for n, _ in _SMALL] + [tail], axis=1)


def _unpack_small(vec, shapes):
    out, pos = {}, 0
    for n, width in _SMALL:
        out[n] = vec[:, pos:pos + width].reshape(shapes[n])
        pos += width
    return out, vec[:, pos:]


def kernel(x, norm_mix_g, w_in, b_gate, w_up_dil, w_up_sb, w_out, norm_mlp_g, w_mlp_in, w_mlp_out, norm_final_g, loss_target, m_norm_mix_g, m_w_in, m_b_gate, m_w_up_dil, m_w_up_sb, m_w_out, m_norm_mlp_g, m_w_mlp_in, m_w_mlp_out, m_norm_final_g, v_norm_mix_g, v_w_in, v_b_gate, v_w_up_dil, v_w_up_sb, v_w_out, v_norm_mlp_g, v_w_mlp_in, v_w_mlp_out, v_norm_final_g):
    given = dict(locals())
    s = x.shape[1]
    x0 = x.reshape(s, D_MODEL)
    target = loss_target.reshape(s, D_MODEL)
    g1 = norm_mix_g.reshape(1, D_MODEL)
    g2 = norm_mlp_g.reshape(1, D_MODEL)
    g3 = norm_final_g.reshape(1, D_MODEL)
    bg = b_gate.reshape(1, 2 * D_MODEL)
    w_shards = {n: given[n].reshape(_shard_shape(n)) for n in _SHARDED}
    m_shards = {n: given["m_" + n].reshape(_shard_shape(n)) for n in _SHARDED}
    v_shards = {n: given["v_" + n].reshape(_shard_shape(n)) for n in _SHARDED}

    shard_b = {n: w_shards[n].astype(BF16) for n in _SHARDED}
    (gathered_w_in,) = _all_gather([shard_b["w_in"]])
    w_in_f = _assemble("w_in", gathered_w_in)
    w_qkv, w_gl = w_in_f[:, :QKV_COLS], w_in_f[:, QKV_COLS:]
    full = {}

    def norm1(_, rows, consts):
        _, xh = _rms_stats(rows[0])
        return [xh * consts[0]], []

    (h1,) = _rowk("norm_mix", tm=512, rows=[x0], consts=[g1], row_outs=[(D_MODEL, BF16)], epilogue=norm1)
    qkv, (land,) = _mm("proj_qkv", h1, w_qkv, out_dtype=BF16, tm=1024, tn=768, tk=D_MODEL,
                       rider=_Spread([shard_b["w_mlp_in"]], chunked=False))
    full["w_mlp_in"] = _assemble("w_mlp_in", land)
    gl = _mm("proj_gates", h1, w_gl, out_dtype=F32, tm=512, tn=2048, tk=D_MODEL)
    dil = [_dil_fwd(qkv, g) for g in range(len(DIL_GROUPS))]
    os_, lses = [d[0] for d in dil], [d[1] for d in dil]
    o_a = _dil_mix_fwd(os_, lses, 512)
    riding = ("w_mlp_out", "w_out", "w_up_sb", "w_up_dil")
    (o_b, tot_b, sb_steps), lands = _sb_fwd(qkv, rider=_Spread([shard_b[n] for n in riding], chunked=False))
    full.update({n: _assemble(n, land) for n, land in zip(riding, lands)})
    x1, h2 = _mixer_fwd(o_a, o_b, gl, x0, bg, g2, full["w_up_dil"], full["w_up_sb"], full["w_out"], 256)
    f = _mm("mlp_in", h2, full["w_mlp_in"], out_dtype=BF16, tm=1024, tn=1024, tk=D_MODEL,
            epilogue=lambda r, _: jnp.square(jnp.maximum(r, 0.0)))

    def head(acc, rows, consts):
        x1v, tv = rows
        g3v = consts[0]
        x2 = x1v + acc
        r, xh = _rms_stats(x2)
        diff = xh * g3v - tv
        loss = (0.5 / D_MODEL) * jnp.sum(jnp.sum(diff * diff, axis=0, keepdims=True), axis=1, keepdims=True)
        dy = diff * (1.0 / D_MODEL)
        dx2, dg = _rms_bwd(dy, xh, r, g3v)
        return [dx2, dx2], [dg, jnp.broadcast_to(loss, (1, LANES))]

    dx2, dx2b, gg3, loss_part = _rowk(
        "mlp_out_loss", a=f, w=full["w_mlp_out"], tm=512, tk=D_FF, rows=[x1, target], consts=[g3],
        row_outs=[(D_MODEL, F32), (D_MODEL, BF16)], acc_outs=[D_MODEL, LANES], epilogue=head)

    da = _mm("mlp_out_bwd", dx2b, full["w_mlp_out"], tb=True, out_dtype=BF16, tm=1024, tn=1024, tk=D_MODEL, extra=f,
             epilogue=lambda r, fv: r * (2.0 * jnp.sqrt(fv.astype(F32))))
    g_w_mlp_out = _mm("grad_w_mlp_out", f, dx2b, ta=True, out_dtype=F32, tm=1024, tn=1024, tk=2048)
    g_w_mlp_in = _mm("grad_w_mlp_in", h2, da, ta=True, out_dtype=F32, tm=1024, tn=1024, tk=2048)

    def norm_bwd(acc, rows, consts):
        xv, dres = rows
        r, xh = _rms_stats(xv)
        dx, dg = _rms_bwd(acc, xh, r, consts[0])
        return [dres + dx], [dg]

    bchunk = lambda n, g: _chunk(n, g).astype(BF16)
    parts = {}
    (dx1, gg2), (parts["w_mlp_in"],) = _rowk(
        "mlp_in_bwd", a=da, w=full["w_mlp_in"], nt=True, tm=512, tk=D_FF, rows=[x1, dx2], consts=[g2],
        row_outs=[(D_MODEL, F32)], acc_outs=[D_MODEL], epilogue=norm_bwd,
        rider=_Spread([bchunk("w_mlp_in", g_w_mlp_in)], chunked=True))
    (do_a, do_b, dgl, g_w_out, g_w_ud, g_w_us, g_bg), (parts["w_mlp_out"],) = _mixer_bwd(
        dx1, o_a, o_b, gl, bg, full["w_up_dil"], full["w_up_sb"], full["w_out"], 256,
        rider=_Spread([bchunk("w_mlp_out", g_w_mlp_out)], chunked=True))
    mix = _dil_mix_bwd(do_a, os_, lses, 512)
    dil_b = [_dil_bwd(qkv, mix[g], lses[g], mix[3 + g], g) for g in range(2)]
    small_three = {"w_out": g_w_out, "w_up_sb": g_w_us, "w_up_dil": g_w_ud}
    grads, lands = _dil_bwd(qkv, mix[2], lses[2], mix[5], 2,
                            rider=_Spread([bchunk(n, g) for n, g in small_three.items()], chunked=True))
    dil_b.append(grads)
    parts.update(dict(zip(small_three, lands)))
    dq_b, dk_b, dv_b = _sb_bwd(qkv, do_b, tot_b, sb_steps)
    dproj = [d[0] for d in dil_b] + [d[1] for d in dil_b] + [d[2] for d in dil_b] + [dq_b, dk_b, dv_b, dgl]
    g_w_in = _grad_cols("grad_w_in", h1, dproj, tm=512, tk=512)
    (grad_x, gg1), (parts["w_in"],) = _rowk(
        "in_proj_bwd", a=dproj, w=w_in_f, nt=True, tm=512, tk=IN_COLS, rows=[x0, dx1], consts=[g1],
        row_outs=[(D_MODEL, F32)], acc_outs=[D_MODEL], epilogue=norm_bwd,
        rider=_Spread([bchunk("w_in", g_w_in)], chunked=True))

    small_part = _pack_small({"norm_mix_g": gg1, "b_gate": g_bg, "norm_mlp_g": gg2, "norm_final_g": gg3}, loss_part)
    (small_parts,) = _exchange([jnp.broadcast_to(small_part[None], (N_DEV, 1, _SMALL_N))])

    tags = ("grad_", "delta_", "new_m_", "new_v_")
    outs = {}
    for n, p in parts.items():
        res = _reduce_adamw("adamw_" + n, p, w_shards[n], m_shards[n], v_shards[n], 128)
        for tag, val in zip(tags, res):
            outs[tag + n] = val.reshape(given[n].shape)
    small_w = _pack_small(given, jnp.zeros((1, LANES), F32))
    small_m = _pack_small({n: given["m_" + n] for n, _ in _SMALL}, jnp.zeros((1, LANES), F32))
    small_v = _pack_small({n: given["v_" + n] for n, _ in _SMALL}, jnp.ones((1, LANES), F32))
    small_res = _reduce_adamw("adamw_replicated", small_parts, small_w, small_m, small_v, 8)

    small_shapes = {n: given[n].shape for n, _ in _SMALL}
    for tag, small in zip(tags, small_res):
        small_vals, tail = _unpack_small(small, small_shapes)
        for n, val in small_vals.items():
            outs[tag + n] = val
        if tag == "grad_":
            loss = tail[0, 0]
    names = ["norm_mix_g", "w_in", "b_gate", "w_up_dil", "w_up_sb", "w_out", "norm_mlp_g", "w_mlp_in", "w_mlp_out",
             "norm_final_g"]
    return (loss, grad_x.reshape(x.shape), *[outs["grad_" + n] for n in names], *[outs["delta_" + n] for n in names],
            *[outs["new_m_" + n] for n in names], *[outs["new_v_" + n] for n in names])
```

```python
import functools
import math

import jax
import jax.numpy as jnp
from jax import lax
from jax.experimental import pallas as pl
from jax.experimental.pallas import tpu as pltpu

_pcall = pl.pallas_call

F32 = jnp.float32
BF16 = jnp.bfloat16

D_MODEL = 1024
HEAD_DIM = 64
DIL_GROUPS = ((128, 1), (512, 4), (2048, 16))
DIL_HEADS_PER_GROUP = 4
N_DIL_HEADS = 12
N_SB_HEADS = 8
DIL_WIDTH = 768
DIL_OUT_WIDTH = 256
SB_WIDTH = 512
D_FF = 4096
BLOCK = 128
RMS_EPS = 1e-6
NEG_INF = -1e30
QKV_COLS = 3 * DIL_WIDTH + 3 * SB_WIDTH
IN_COLS = QKV_COLS + 2 * D_MODEL
N_DEV = 8

ADAM_LR = 0.001
ADAM_B1 = 0.9
ADAM_B2 = 0.999
ADAM_EPS = 1e-08
ADAM_WD = 0.01
ADAM_STEP = 10

VMEM_LIMIT = 56 * 1024 * 1024
SB_TK = 256
SB_TQ_FWD = 512
SB_TQ_BWD = 256
LANES = 128

_ARB = pltpu.ARBITRARY


def _cparams(n_axes, **kw):
    return pltpu.CompilerParams(dimension_semantics=(_ARB,) * n_axes, vmem_limit_bytes=VMEM_LIMIT, **kw)


def _dot(a, b):
    return jnp.dot(a, b, preferred_element_type=F32)


def _dot_nt(a, b):
    return lax.dot_general(a, b, (((1,), (1,)), ((), ())), preferred_element_type=F32)


def _dot_tn(a, b):
    return lax.dot_general(a, b, (((0,), (0,)), ((), ())), preferred_element_type=F32)


def _split_hi_lo(x):
    hi = x.astype(BF16)
    lo = (x - hi.astype(F32)).astype(BF16)
    return hi, lo


def _dot_hi_lo(x, m):
    hi, lo = _split_hi_lo(x)
    return _dot(hi, m) + _dot(lo, m)


def _sigmoid(x):
    return 1.0 / (1.0 + jnp.exp(-x))


_HBM = pl.BlockSpec(memory_space=pltpu.HBM)
_MESH = pl.DeviceIdType.MESH


class _Spread:
    def __init__(self, srcs, chunked):
        self.srcs, self.chunked, self.n = list(srcs), chunked, len(srcs)

    def land_shapes(self):
        return [jax.ShapeDtypeStruct((N_DEV,) + (s.shape[1:] if self.chunked else s.shape), s.dtype) for s in self.srcs]

    def scratch(self):
        dma = pltpu.SemaphoreType.DMA
        return [dma((7 * self.n,)), dma((7 * self.n,)), dma((self.n,))]

    def copies(self, src_refs, land_refs, send_sems, recv_sems, local_sems):
        x, y, c = lax.axis_index("x"), lax.axis_index("y"), lax.axis_index("c")
        me = 4 * x + 2 * y + c
        out = []
        for a, (src, land) in enumerate(zip(src_refs, land_refs)):
            out.append(pltpu.make_async_copy(src.at[me] if self.chunked else src, land.at[me], local_sems.at[a]))
            for k in range(1, N_DEV):
                px, py, pc = x ^ (k >> 2), y ^ ((k >> 1) & 1), c ^ (k & 1)
                out.append(pltpu.make_async_remote_copy(
                    src_ref=src.at[4 * px + 2 * py + pc] if self.chunked else src, dst_ref=land.at[me],
                    send_sem=send_sems.at[7 * a + k - 1], recv_sem=recv_sems.at[7 * a + k - 1],
                    device_id=(px, py, pc), device_id_type=_MESH))
        return out


def _call(body, args, rider=None, **kw):
    if rider is None:
        return _pcall(body, **kw)(*args)
    grid = kw["grid"]
    single = not isinstance(kw["out_shape"], (list, tuple))
    out_specs = [kw["out_specs"]] if single else list(kw["out_specs"])
    out_shape = [kw["out_shape"]] if single else list(kw["out_shape"])
    in_specs, scratch = list(kw["in_specs"]), list(kw.get("scratch_shapes", []))
    n_in, n_out, n_s, n = len(in_specs), len(out_shape), len(scratch), rider.n

    def hosted(*refs):
        ins, srcs = refs[:n_in], refs[n_in:n_in + n]
        outs, lands = refs[n_in + n:n_in + n + n_out], refs[n_in + n + n_out:n_in + 2 * n + n_out]
        own_scratch, sems = refs[n_in + 2 * n + n_out:n_in + 2 * n + n_out + n_s], refs[n_in + 2 * n + n_out + n_s:]
        ids = [pl.program_id(d) for d in range(len(grid))]
        first = functools.reduce(jnp.logical_and, [i == 0 for i in ids])
        last = functools.reduce(jnp.logical_and, [i == g - 1 for i, g in zip(ids, grid)])
        copies = rider.copies(srcs, lands, *sems)

        @pl.when(first)
        def _():
            for cp in copies:
                cp.start()

        body(*ins, *outs, *own_scratch)

        @pl.when(last)
        def _():
            for cp in copies:
                cp.wait()

    kw = dict(kw, in_specs=in_specs + [_HBM] * n, out_specs=out_specs + [_HBM] * n,
              out_shape=out_shape + rider.land_shapes(), scratch_shapes=scratch + rider.scratch())
    res = _pcall(hosted, **kw)(*args, *rider.srcs)
    return (res[0] if single else list(res[:n_out])), list(res[n_out:])


def _mm(name, a, b, *, ta=False, tb=False, out_dtype, tm, tn, tk, epilogue=None, extra=None, rider=None):
    m = a.shape[1] if ta else a.shape[0]
    k = a.shape[0] if ta else a.shape[1]
    n = b.shape[0] if tb else b.shape[1]
    assert (b.shape[1] if tb else b.shape[0]) == k
    tm, tn, tk = min(tm, m), min(tn, n), min(tk, k)
    assert m % tm == 0 and n % tn == 0 and k % tk == 0, (name, m, n, k, tm, tn, tk)
    nk = k // tk
    dn = (((0 if ta else 1,), (1 if tb else 0,)), ((), ()))
    in_place = nk > 1 and epilogue is None and out_dtype == F32

    def body(*refs):
        if extra is not None:
            a_ref, b_ref, e_ref, o_ref = refs[:4]
        else:
            a_ref, b_ref, o_ref = refs[:3]
            e_ref = None

        def finish(r):
            if epilogue is not None:
                r = epilogue(r, None if e_ref is None else e_ref[...])
            o_ref[...] = r.astype(out_dtype)

        part = lax.dot_general(a_ref[...].astype(BF16), b_ref[...].astype(BF16), dn, preferred_element_type=F32)
        if nk == 1:
            finish(part)
        else:
            acc_ref = o_ref if in_place else refs[-1]
            kk = pl.program_id(2)

            @pl.when(kk == 0)
            def _():
                acc_ref[...] = part

            @pl.when(kk > 0)
            def _():
                acc_ref[...] += part

            if not in_place:

                @pl.when(kk == nk - 1)
                def _():
                    finish(acc_ref[...])

    a_spec = pl.BlockSpec((tk, tm), lambda j, i, kk: (kk, i)) if ta else pl.BlockSpec((tm, tk), lambda j, i, kk: (i, kk))
    b_spec = pl.BlockSpec((tn, tk), lambda j, i, kk: (j, kk)) if tb else pl.BlockSpec((tk, tn), lambda j, i, kk: (kk, j))
    o_spec = pl.BlockSpec((tm, tn), lambda j, i, kk: (i, j))
    in_specs = [a_spec, b_spec]
    args = [a, b]
    if extra is not None:
        in_specs.append(o_spec)
        args.append(extra)
    return _call(
        body, args, rider,
        name=name,
        grid=(n // tn, m // tm, nk),
        in_specs=in_specs,
        out_specs=o_spec,
        out_shape=jax.ShapeDtypeStruct((m, n), out_dtype),
        scratch_shapes=[pltpu.VMEM((tm, tn), F32)] if (nk > 1 and not in_place) else [],
        compiler_params=_cparams(3),
    )


def _grad_cols(name, a, parts, *, tm, tk, rider=None):
    k, m = a.shape
    n = sum(p.shape[1] for p in parts)
    assert m % tm == 0 and k % tk == 0
    nk = k // tk

    def body(*refs):
        a_ref, p_refs, o_ref = refs[0], refs[1:1 + len(parts)], refs[1 + len(parts)]
        kk = pl.program_id(1)
        side_by_side = jnp.concatenate([p_ref[...].astype(BF16) for p_ref in p_refs], axis=1)
        term = _dot_tn(a_ref[...].astype(BF16), side_by_side)

        @pl.when(kk == 0)
        def _():
            o_ref[...] = term

        @pl.when(kk > 0)
        def _():
            o_ref[...] += term

    return _call(
        body, [a] + list(parts), rider,
        name=name,
        grid=(m // tm, nk),
        in_specs=[pl.BlockSpec((tk, tm), lambda i, kk: (kk, i))]
        + [pl.BlockSpec((tk, p.shape[1]), lambda i, kk: (kk, 0)) for p in parts],
        out_specs=pl.BlockSpec((tm, n), lambda i, kk: (i, 0)),
        out_shape=jax.ShapeDtypeStruct((m, n), F32),
        compiler_params=_cparams(2),
    )


def _rowk(name, *, a=None, w=None, nt=False, tm, tk=None, rows=(), consts=(), row_outs=(), acc_outs=(), epilogue,
          rider=None):
    has_mm = a is not None
    a_parts = list(a) if isinstance(a, (list, tuple)) else ([a] if has_mm else [])
    n_a = len(a_parts)
    m = a_parts[0].shape[0] if has_mm else rows[0].shape[0]
    assert m % tm == 0
    nm = m // tm
    if has_mm:
        k = sum(p.shape[1] for p in a_parts)
        n = w.shape[0] if nt else w.shape[1]
        tk = min(tk, k)
        assert k % tk == 0 and (n_a == 1 or tk == k)
        nk = k // tk
    else:
        nk = 1
    n_rows, n_consts, n_ro, n_ao = len(rows), len(consts), len(row_outs), len(acc_outs)

    def body(*refs):
        pos = 0
        if has_mm:
            a_refs, w_ref = refs[:n_a], refs[n_a]
            pos = n_a + 1
        row_refs = refs[pos:pos + n_rows]
        pos += n_rows
        const_refs = refs[pos:pos + n_consts]
        pos += n_consts
        ro_refs = refs[pos:pos + n_ro]
        pos += n_ro
        ao_refs = refs[pos:pos + n_ao]
        pos += n_ao
        i = pl.program_id(0)
        kk = pl.program_id(1)

        def finish(acc):
            ro_vals, ao_vals = epilogue(acc, [r[...] for r in row_refs], [c[...] for c in const_refs])
            for r, v in zip(ro_refs, ro_vals):
                r[...] = v.astype(r.dtype)
            for r, v in zip(ao_refs, ao_vals):

                @pl.when(i == 0)
                def _(r=r, v=v):
                    r[...] = v

                @pl.when(i > 0)
                def _(r=r, v=v):
                    r[...] += v

        if not has_mm:
            finish(None)
            return
        part, off = None, 0
        for a_ref in a_refs:
            width = a_ref.shape[1]
            cols = slice(None) if n_a == 1 else slice(off, off + width)
            av = a_ref[...].astype(BF16)
            term = _dot_nt(av, w_ref[:, cols]) if nt else _dot(av, w_ref[cols, :])
            part = term if part is None else part + term
            off += width
        if nk == 1:
            finish(part)
        else:
            acc_ref = refs[pos]

            @pl.when(kk == 0)
            def _():
                acc_ref[...] = part

            @pl.when(kk > 0)
            def _():
                acc_ref[...] += part

            @pl.when(kk == nk - 1)
            def _():
                finish(acc_ref[...])

    once = pl.Buffered(1)
    in_specs, args = [], []
    if has_mm:
        for part in a_parts:
            in_specs.append(pl.BlockSpec((tm, tk if n_a == 1 else part.shape[1]), lambda i, kk: (i, kk)))
        w_mode = once if nk == 1 else None
        in_specs.append(pl.BlockSpec((n, tk), lambda i, kk: (0, kk), pipeline_mode=w_mode) if nt
                        else pl.BlockSpec((tk, n), lambda i, kk: (kk, 0), pipeline_mode=w_mode))
        args += a_parts + [w]
    for r in rows:
        in_specs.append(pl.BlockSpec((tm, r.shape[1]), lambda i, kk: (i, 0)))
        args.append(r)
    for c in consts:
        in_specs.append(pl.BlockSpec(c.shape, lambda i, kk: (0,) * c.ndim, pipeline_mode=once))
        args.append(c)
    out_specs, out_shape = [], []
    for width, dt in row_outs:
        out_specs.append(pl.BlockSpec((tm, width), lambda i, kk: (i, 0)))
        out_shape.append(jax.ShapeDtypeStruct((m, width), dt))
    for width in acc_outs:
        out_specs.append(pl.BlockSpec((1, width), lambda i, kk: (0, 0)))
        out_shape.append(jax.ShapeDtypeStruct((1, width), F32))
    return _call(
        body, args, rider,
        name=name,
        grid=(nm, nk),
        in_specs=in_specs,
        out_specs=out_specs,
        out_shape=out_shape,
        scratch_shapes=[pltpu.VMEM((tm, n), F32)] if (has_mm and nk > 1) else [],
        compiler_params=_cparams(2),
    )


def _rms_stats(x):
    r = lax.rsqrt(jnp.mean(x * x, axis=-1, keepdims=True) + RMS_EPS)
    return r, x * r


def _rms_bwd(dh, xh, r, g):
    gy = dh * g
    dx = r * (gy - xh * jnp.mean(gy * xh, axis=-1, keepdims=True))
    return dx, jnp.sum(dh * xh, axis=0, keepdims=True)


def _alibi_slope(head):
    return 2.0 ** (-8.0 * (head + 1) / N_DIL_HEADS)


DIL_STEP_BLOCKS = 4


def _dil_band(first_block):
    qi = lax.broadcasted_iota(jnp.int32, (BLOCK, 2 * BLOCK), 0)
    kj = lax.broadcasted_iota(jnp.int32, (BLOCK, 2 * BLOCK), 1)
    steps = qi + BLOCK - kj
    valid = (steps >= 0) & (steps <= BLOCK)
    if first_block is not False:
        valid = valid & ((kj >= BLOCK) | jnp.logical_not(first_block))
    return steps.astype(F32), valid


def _dil_step_specs(ncb, cols, nblk, clamp):
    def own(col):
        return pl.BlockSpec((nblk * BLOCK, DIL_OUT_WIDTH), lambda r, i: (clamp(i), r * ncb + col))

    def before(col):
        return pl.BlockSpec((BLOCK, DIL_OUT_WIDTH), lambda r, i: (jnp.maximum(clamp(i) * nblk - 1, 0), r * ncb + col))

    return [own(cols[0]), own(cols[1]), before(cols[1]), own(cols[2]), before(cols[2])]


def _dil_fwd(qkv, group):
    window, dilation = DIL_GROUPS[group]
    s = qkv.shape[0]
    sub = s // dilation
    nb = sub // BLOCK
    assert nb * BLOCK * dilation == s and window // dilation == BLOCK
    nblk = min(DIL_STEP_BLOCKS, nb)
    assert nb % nblk == 0
    slopes = [_alibi_slope(group * DIL_HEADS_PER_GROUP + h) * dilation for h in range(DIL_HEADS_PER_GROUP)]

    def body(q_ref, kc_ref, kp_ref, vc_ref, vp_ref, o_ref, lse_ref):
        i = pl.program_id(1)
        kk_all = jnp.concatenate([kp_ref[...], kc_ref[...]], axis=0)
        vv_all = jnp.concatenate([vp_ref[...], vc_ref[...]], axis=0)
        head_id = lax.broadcasted_iota(jnp.int32, (1, DIL_OUT_WIDTH), 1) // HEAD_DIM
        chains = [(b, h) for b in range(nblk) for h in range(DIL_HEADS_PER_GROUP)]
        rows = lambda b: slice(b * BLOCK, (b + 1) * BLOCK)
        keys = lambda b: slice(b * BLOCK, (b + 2) * BLOCK)
        bands = [_dil_band(i == 0 if b == 0 else False) for b in range(nblk)]
        qs = [q_ref[rows(b), :] for b in range(nblk)]
        scores = [_dot_nt(jnp.where(head_id == h, qs[b], jnp.zeros_like(qs[b])), kk_all[keys(b)]) for b, h in chains]
        ps, lses = [], []
        for (b, h), sc in zip(chains, scores):
            steps, valid = bands[b]
            logits = jnp.where(valid, sc * (1.0 / math.sqrt(HEAD_DIM)) - slopes[h] * steps, NEG_INF)
            mx = jnp.max(logits, axis=1, keepdims=True)
            e = jnp.exp(logits - mx)
            den = jnp.sum(e, axis=1, keepdims=True)
            lses.append(mx + jnp.log(den))
            ps.append((e * (1.0 / den)).astype(BF16))
        outs = [_dot(p, vv_all[keys(b)]) for (b, h), p in zip(chains, ps)]
        for b in range(nblk):
            mine = [n for n, ch in enumerate(chains) if ch[0] == b]
            o, lse_all = outs[mine[0]], lses[mine[0]]
            for n in mine[1:]:
                o = jnp.where(head_id == chains[n][1], outs[n], o)
                lse_all = jnp.where(head_id == chains[n][1], lses[n], lse_all)
            o_ref[rows(b), :] = o
            lse_ref[rows(b), :] = jnp.broadcast_to(lse_all, o.shape)

    qkv_v, ncb, cols = _dil_view(qkv, group)
    out_spec = pl.BlockSpec((nblk * BLOCK, DIL_OUT_WIDTH), lambda r, i: (i, r))
    o, lse = _pcall(
        body,
        name=f"dil_fwd_g{group}",
        grid=(dilation, nb // nblk),
        in_specs=_dil_step_specs(ncb, cols, nblk, lambda i: i),
        out_specs=[out_spec, out_spec],
        out_shape=[jax.ShapeDtypeStruct((sub, dilation * DIL_OUT_WIDTH), F32)] * 2,
        compiler_params=_cparams(2),
    )(qkv_v, qkv_v, qkv_v, qkv_v, qkv_v)
    return o.reshape(s, DIL_OUT_WIDTH), lse.reshape(s, DIL_OUT_WIDTH)


def _dil_bwd(qkv, do_g, lse_g, dterm_g, group, rider=None):
    window, dilation = DIL_GROUPS[group]
    s = qkv.shape[0]
    sub = s // dilation
    nb = sub // BLOCK
    nblk = min(DIL_STEP_BLOCKS, nb)
    n_steps = nb // nblk
    slopes = [_alibi_slope(group * DIL_HEADS_PER_GROUP + h) * dilation for h in range(DIL_HEADS_PER_GROUP)]
    scale = 1.0 / math.sqrt(HEAD_DIM)
    tail = slice((nblk - 1) * BLOCK, nblk * BLOCK)

    def body(q_ref, kc_ref, kp_ref, vc_ref, vp_ref, do_ref, lse_ref, dt_ref, dq_ref, dk_ref, dv_ref, ck_ref, cv_ref):
        i = pl.program_id(1)

        @pl.when(i == 0)
        def _():
            ck_ref[...] = jnp.zeros_like(ck_ref)
            cv_ref[...] = jnp.zeros_like(cv_ref)

        @pl.when(i < n_steps)
        def _():
            kk_all = jnp.concatenate([kp_ref[...], kc_ref[...]], axis=0)
            vv_all = jnp.concatenate([vp_ref[...], vc_ref[...]], axis=0)
            lane = lax.broadcasted_iota(jnp.int32, (1, DIL_OUT_WIDTH), 1)
            head_id = lane // HEAD_DIM
            chains = [(b, h) for b in range(nblk) for h in range(DIL_HEADS_PER_GROUP)]
            rows = lambda b: slice(b * BLOCK, (b + 1) * BLOCK)
            keys = lambda b: slice(b * BLOCK, (b + 2) * BLOCK)
            bands = [_dil_band(i == 0 if b == 0 else False) for b in range(nblk)]
            qms, doms = [], []
            for b, h in chains:
                q, do = q_ref[rows(b), :], do_ref[rows(b), :]
                qms.append(jnp.where(head_id == h, q, jnp.zeros_like(q)))
                doms.append(jnp.where(head_id == h, do, jnp.zeros_like(do)))
            scores = [_dot_nt(qm, kk_all[keys(b)]) for (b, h), qm in zip(chains, qms)]
            dps = [_dot_nt(dom, vv_all[keys(b)]) for (b, h), dom in zip(chains, doms)]
            pbs, dss = [], []
            for n, (b, h) in enumerate(chains):
                steps, valid = bands[b]
                first = lane == h * HEAD_DIM
                lse = jnp.sum(jnp.where(first, lse_ref[rows(b), :], 0.0), axis=1, keepdims=True)
                dt = jnp.sum(jnp.where(first, dt_ref[rows(b), :], 0.0), axis=1, keepdims=True)
                logits = jnp.where(valid, scores[n] * scale - slopes[h] * steps, NEG_INF)
                p = jnp.where(valid, jnp.exp(logits - lse), 0.0)
                pbs.append(p.astype(BF16))
                dss.append((p * (dps[n] + dt) * scale).astype(BF16))
            dqs = [_dot(ds, kk_all[keys(b)]) for (b, h), ds in zip(chains, dss)]
            dks = [_dot_tn(ds, qm) for ds, qm in zip(dss, qms)]
            dvs = [_dot_tn(pb, dom) for pb, dom in zip(pbs, doms)]
            dkk, dvv = [], []
            for b in range(nblk):
                mine = [n for n, ch in enumerate(chains) if ch[0] == b]
                dq = dqs[mine[0]]
                for n in mine[1:]:
                    dq = jnp.where(head_id == chains[n][1], dqs[n], dq)
                dq_ref[rows(b), :] = dq.astype(dq_ref.dtype)
                dkk.append((dks[mine[0]] + dks[mine[1]]) + (dks[mine[2]] + dks[mine[3]]))
                dvv.append((dvs[mine[0]] + dvs[mine[1]]) + (dvs[mine[2]] + dvs[mine[3]]))
            for out_ref, carry_ref, parts in ((dk_ref, ck_ref, dkk), (dv_ref, cv_ref, dvv)):
                if nblk > 1:
                    out_ref[: (nblk - 1) * BLOCK, :] = carry_ref[: (nblk - 1) * BLOCK, :].astype(out_ref.dtype)
                out_ref[tail, :] = (carry_ref[tail, :] + parts[0][:BLOCK]).astype(out_ref.dtype)
                for b in range(nblk):
                    own = parts[b][BLOCK:]
                    carry_ref[rows(b), :] = own + parts[b + 1][:BLOCK] if b + 1 < nblk else own

        @pl.when(i == n_steps)
        def _():
            dk_ref[...] = ck_ref[...].astype(dk_ref.dtype)
            dv_ref[...] = cv_ref[...].astype(dv_ref.dtype)

    clamp = lambda i: jnp.minimum(i, n_steps - 1)
    qkv_v, ncb, cols = _dil_view(qkv, group)
    view = lambda t: t.reshape(sub, dilation * DIL_OUT_WIDTH)
    row_spec = pl.BlockSpec((nblk * BLOCK, DIL_OUT_WIDTH), lambda r, i: (clamp(i), r))
    late_spec = pl.BlockSpec((nblk * BLOCK, DIL_OUT_WIDTH), lambda r, i: (jnp.maximum(i - 1, 0), r))
    res = _call(
        body, (qkv_v, qkv_v, qkv_v, qkv_v, qkv_v, view(do_g), view(lse_g), view(dterm_g)), rider,
        name=f"dil_bwd_g{group}",
        grid=(dilation, n_steps + 1),
        in_specs=_dil_step_specs(ncb, cols, nblk, clamp) + [row_spec, row_spec, row_spec],
        out_specs=[row_spec, late_spec, late_spec],
        out_shape=[jax.ShapeDtypeStruct((sub, dilation * DIL_OUT_WIDTH), BF16)] * 3,
        scratch_shapes=[pltpu.VMEM((nblk * BLOCK, DIL_OUT_WIDTH), F32)] * 2,
        compiler_params=_cparams(2),
    )
    grads, lands = res if rider is not None else (res, None)
    grads = tuple(g.reshape(s, DIL_OUT_WIDTH) for g in grads)
    return grads if rider is None else (grads, lands)


def _dil_masks(i):
    qi = lax.broadcasted_iota(jnp.int32, (BLOCK, 2 * BLOCK), 0)
    kj = lax.broadcasted_iota(jnp.int32, (BLOCK, 2 * BLOCK), 1)
    steps = qi + BLOCK - kj
    valid = (steps >= 0) & (steps <= BLOCK) & ((kj >= BLOCK) | (i > 0))
    return steps.astype(F32), valid


def _dil_view(qkv, group):
    _, dilation = DIL_GROUPS[group]
    if dilation == 1:
        return qkv, QKV_COLS // DIL_OUT_WIDTH, (group, 3 + group, 6 + group)
    w = DIL_OUT_WIDTH
    own = jnp.concatenate([qkv[:, (3 * part + group) * w:(3 * part + group + 1) * w] for part in range(3)], axis=1)
    return own.reshape(qkv.shape[0] // dilation, dilation * 3 * w), 3, (0, 1, 2)


def _dil_specs(ncb, cols, clamp):
    def cur(col):
        return pl.BlockSpec((BLOCK, DIL_OUT_WIDTH), lambda r, i: (clamp(i), r * ncb + col))

    def prev(col):
        return pl.BlockSpec((BLOCK, DIL_OUT_WIDTH), lambda r, i: (jnp.maximum(clamp(i) - 1, 0), r * ncb + col))

    return [cur(cols[0]), cur(cols[1]), prev(cols[1]), cur(cols[2]), prev(cols[2])]


def _dil_fwd_one(qkv, group):
    window, dilation = DIL_GROUPS[group]
    s = qkv.shape[0]
    sub = s // dilation
    nb = sub // BLOCK
    assert nb * BLOCK * dilation == s and window // dilation == BLOCK
    slopes = [_alibi_slope(group * DIL_HEADS_PER_GROUP + h) * dilation for h in range(DIL_HEADS_PER_GROUP)]

    def body(q_ref, kc_ref, kp_ref, vc_ref, vp_ref, o_ref, lse_ref):
        i = pl.program_id(1)
        q = q_ref[...]
        kk = jnp.concatenate([kp_ref[...], kc_ref[...]], axis=0)
        vv = jnp.concatenate([vp_ref[...], vc_ref[...]], axis=0)
        head_id = lax.broadcasted_iota(jnp.int32, (1, DIL_OUT_WIDTH), 1) // HEAD_DIM
        steps, valid = _dil_masks(i)
        heads = range(DIL_HEADS_PER_GROUP)
        scores = [_dot_nt(jnp.where(head_id == h, q, jnp.zeros_like(q)), kk) for h in heads]
        ps, lses = [], []
        for h in heads:
            logits = scores[h] * (1.0 / math.sqrt(HEAD_DIM)) - slopes[h] * steps
            logits = jnp.where(valid, logits, NEG_INF)
            mx = jnp.max(logits, axis=1, keepdims=True)
            e = jnp.exp(logits - mx)
            den = jnp.sum(e, axis=1, keepdims=True)
            lses.append(mx + jnp.log(den))
            ps.append((e * (1.0 / den)).astype(BF16))
        outs = [_dot(ps[h], vv) for h in heads]
        o, lse_all = outs[0], lses[0]
        for h in heads[1:]:
            o = jnp.where(head_id == h, outs[h], o)
            lse_all = jnp.where(head_id == h, lses[h], lse_all)
        o_ref[...] = o
        lse_ref[...] = jnp.broadcast_to(lse_all, o.shape)

    qkv_v, ncb, cols = _dil_view(qkv, group)
    out_spec = pl.BlockSpec((BLOCK, DIL_OUT_WIDTH), lambda r, i: (i, r))
    o, lse = _pcall(
        body,
        name=f"dil_fwd_g{group}",
        grid=(dilation, nb),
        in_specs=_dil_specs(ncb, cols, lambda i: i),
        out_specs=[out_spec, out_spec],
        out_shape=[jax.ShapeDtypeStruct((sub, dilation * DIL_OUT_WIDTH), F32)] * 2,
        compiler_params=_cparams(2),
    )(qkv_v, qkv_v, qkv_v, qkv_v, qkv_v)
    return o.reshape(s, DIL_OUT_WIDTH), lse.reshape(s, DIL_OUT_WIDTH)


def _dil_bwd_one(qkv, do_g, lse_g, dterm_g, group, rider=None):
    window, dilation = DIL_GROUPS[group]
    s = qkv.shape[0]
    sub = s // dilation
    nb = sub // BLOCK
    slopes = [_alibi_slope(group * DIL_HEADS_PER_GROUP + h) * dilation for h in range(DIL_HEADS_PER_GROUP)]
    scale = 1.0 / math.sqrt(HEAD_DIM)

    def body(q_ref, kc_ref, kp_ref, vc_ref, vp_ref, do_ref, lse_ref, dt_ref, dq_ref, dk_ref, dv_ref, ck_ref, cv_ref):
        i = pl.program_id(1)

        @pl.when(i == 0)
        def _():
            ck_ref[...] = jnp.zeros_like(ck_ref)
            cv_ref[...] = jnp.zeros_like(cv_ref)

        @pl.when(i < nb)
        def _():
            q = q_ref[...]
            do = do_ref[...]
            lse_all = lse_ref[...]
            dt_all = dt_ref[...]
            kk = jnp.concatenate([kp_ref[...], kc_ref[...]], axis=0)
            vv = jnp.concatenate([vp_ref[...], vc_ref[...]], axis=0)
            lane = lax.broadcasted_iota(jnp.int32, (1, DIL_OUT_WIDTH), 1)
            head_id = lane // HEAD_DIM
            steps, valid = _dil_masks(i)
            heads = range(DIL_HEADS_PER_GROUP)
            qms = [jnp.where(head_id == h, q, jnp.zeros_like(q)) for h in heads]
            doms = [jnp.where(head_id == h, do, jnp.zeros_like(do)) for h in heads]
            scores = [_dot_nt(qms[h], kk) for h in heads]
            dps = [_dot_nt(doms[h], vv) for h in heads]
            pbs, dss = [], []
            for h in heads:
                first = lane == h * HEAD_DIM
                lse = jnp.sum(jnp.where(first, lse_all, 0.0), axis=1, keepdims=True)
                dt = jnp.sum(jnp.where(first, dt_all, 0.0), axis=1, keepdims=True)
                logits = scores[h] * scale - slopes[h] * steps
                p = jnp.where(valid, jnp.exp(jnp.where(valid, logits, NEG_INF) - lse), 0.0)
                pbs.append(p.astype(BF16))
                dss.append((p * (dps[h] + dt) * scale).astype(BF16))
            dqs = [_dot(dss[h], kk) for h in heads]
            dks = [_dot_tn(dss[h], qms[h]) for h in heads]
            dvs = [_dot_tn(pbs[h], doms[h]) for h in heads]
            dq = dqs[0]
            for h in heads[1:]:
                dq = jnp.where(head_id == h, dqs[h], dq)
            dkk = (dks[0] + dks[1]) + (dks[2] + dks[3])
            dvv = (dvs[0] + dvs[1]) + (dvs[2] + dvs[3])
            dq_ref[...] = dq.astype(dq_ref.dtype)
            dk_ref[...] = (ck_ref[...] + dkk[:BLOCK]).astype(dk_ref.dtype)
            dv_ref[...] = (cv_ref[...] + dvv[:BLOCK]).astype(dv_ref.dtype)
            ck_ref[...] = dkk[BLOCK:]
            cv_ref[...] = dvv[BLOCK:]

        @pl.when(i == nb)
        def _():
            dk_ref[...] = ck_ref[...].astype(dk_ref.dtype)
            dv_ref[...] = cv_ref[...].astype(dv_ref.dtype)

    clamp = lambda i: jnp.minimum(i, nb - 1)
    qkv_v, ncb, cols = _dil_view(qkv, group)
    view = lambda t: t.reshape(sub, dilation * DIL_OUT_WIDTH)
    row_spec = pl.BlockSpec((BLOCK, DIL_OUT_WIDTH), lambda r, i: (clamp(i), r))
    late_spec = pl.BlockSpec((BLOCK, DIL_OUT_WIDTH), lambda r, i: (jnp.maximum(i - 1, 0), r))
    res = _call(
        body, (qkv_v, qkv_v, qkv_v, qkv_v, qkv_v, view(do_g), view(lse_g), view(dterm_g)), rider,
        name=f"dil_bwd_g{group}",
        grid=(dilation, nb + 1),
        in_specs=_dil_specs(ncb, cols, clamp) + [row_spec, row_spec, row_spec],
        out_specs=[row_spec, late_spec, late_spec],
        out_shape=[jax.ShapeDtypeStruct((sub, dilation * DIL_OUT_WIDTH), BF16)] * 3,
        scratch_shapes=[pltpu.VMEM((BLOCK, DIL_OUT_WIDTH), F32)] * 2,
        compiler_params=_cparams(2),
    )
    grads, lands = res if rider is not None else (res, None)
    grads = tuple(g.reshape(s, DIL_OUT_WIDTH) for g in grads)
    return grads if rider is None else (grads, lands)


def _head_block_ones():
    r = lax.broadcasted_iota(jnp.int32, (DIL_OUT_WIDTH, DIL_OUT_WIDTH), 0) // HEAD_DIM
    c = lax.broadcasted_iota(jnp.int32, (DIL_OUT_WIDTH, DIL_OUT_WIDTH), 1) // HEAD_DIM
    return jnp.where(r == c, 1.0, 0.0).astype(BF16)


def _dil_mix_weights(l0, l1, l2):
    mx = jnp.maximum(jnp.maximum(l0, l1), l2)
    e0, e1, e2 = jnp.exp(l0 - mx), jnp.exp(l1 - mx), jnp.exp(l2 - mx)
    inv = 1.0 / (e0 + e1 + e2)
    return e0 * inv, e1 * inv, e2 * inv


def _dil_mix_fwd(os_, lses, tm):
    def epi(_, rows, consts):
        o0, o1, o2, l0, l1, l2 = rows
        w0, w1, w2 = _dil_mix_weights(l0, l1, l2)
        return [w0 * o0 + w1 * o1 + w2 * o2], []

    (o_a,) = _rowk("dil_mix_fwd", tm=tm, rows=list(os_) + list(lses), row_outs=[(DIL_OUT_WIDTH, BF16)], epilogue=epi)
    return o_a


def _dil_mix_bwd(do_a, os_, lses, tm):
    def epi(_, rows, consts):
        do, o0, o1, o2, l0, l1, l2 = rows
        do = do.astype(F32)
        w0, w1, w2 = _dil_mix_weights(l0, l1, l2)
        mixed = w0 * o0 + w1 * o1 + w2 * o2
        tot = _dot_hi_lo(do * mixed, _head_block_ones())
        return [w0 * do, w1 * do, w2 * do, -w0 * tot, -w1 * tot, -w2 * tot], []

    return _rowk(
        "dil_mix_bwd", tm=tm, rows=[do_a] + list(os_) + list(lses),
        row_outs=[(DIL_OUT_WIDTH, BF16)] * 3 + [(DIL_OUT_WIDTH, F32)] * 3, epilogue=epi)


_SB_Q0 = 3 * DIL_WIDTH // LANES
_SB_K0 = _SB_Q0 + SB_WIDTH // LANES
_SB_V0 = _SB_K0 + SB_WIDTH // LANES


_EXP_CLAMP = 88.0
_SB_DEAD = 104.0


def _tri(t, op):
    r = lax.broadcasted_iota(jnp.int32, (t, t), 0)
    c = lax.broadcasted_iota(jnp.int32, (t, t), 1)
    return jnp.where(op(r, c), 1.0, 0.0).astype(BF16)


def _softplus(z):
    return jnp.maximum(z, jnp.log(1.0 + jnp.exp(jnp.minimum(z, _EXP_CLAMP))))


def _sb_chain_head(qm, kj, mask):
    z = _dot_nt(qm, kj)
    sp = _softplus(z)
    return (sp if mask is None else jnp.where(mask, sp, 0.0)), z - sp


def _sb_fwd(qkv, rider=None):
    s = qkv.shape[0]
    t = SB_TK
    assert s % (2 * t) == 0
    nq = s // (2 * t)
    n_pairs = SB_WIDTH // LANES

    def body(q_ref, k_ref, v_ref, o_ref, tot_ref, steps_ref):
        p, i = pl.program_id(0), pl.program_id(1)
        lane_hi = lax.broadcasted_iota(jnp.int32, (1, LANES), 1) // HEAD_DIM
        later = _tri(t, lambda r, c: r > c)
        causal = lax.broadcasted_iota(jnp.int32, (t, t), 1) < lax.broadcasted_iota(jnp.int32, (t, t), 0)
        qms = []
        for x in range(2):
            q = q_ref[pl.ds(x * t, t), :] * (1.0 / math.sqrt(HEAD_DIM))
            qms.append([jnp.where(lane_hi == hh, q, jnp.zeros_like(q)) for hh in range(2)])

        def tile(j):
            off = pl.multiple_of(j * t, t)
            return k_ref[pl.ds(off, t), :], v_ref[pl.ds(off, t), :]

        def step(tiles, carry, diag):
            chains = [(x, hh) for x in range(2) if tiles[x] is not None for hh in range(2)]
            kv = {x: tile(tiles[x]) for x in range(2) if tiles[x] is not None}
            heads = [_sb_chain_head(qms[x][hh], kv[x][0], causal if diag else None) for x, hh in chains]
            sufs = [_dot(sp.astype(BF16), later) for sp, _ in heads]
            new = [list(carry[0]), list(carry[1])]
            for (x, hh), (sp, lpos), suf in zip(chains, heads, sufs):
                c, acc = carry[x][hh]
                a = jnp.exp(lpos - suf - c)
                if diag:
                    a = jnp.where(causal, a, 0.0)
                new[x][hh] = (c + jnp.sum(sp, axis=1, keepdims=True), acc + _dot(a.astype(BF16), kv[x][1]))
            return (tuple(new[0]), tuple(new[1]))

        def lowest(carry):
            m = [jnp.min(carry[x][hh][0]) for x in range(2) for hh in range(2)]
            return jnp.minimum(jnp.minimum(m[0], m[1]), jnp.minimum(m[2], m[3]))

        zero = (jnp.zeros((t, 1), F32), jnp.zeros((t, LANES), F32))
        carry = step((2 * i, 2 * i + 1), ((zero, zero), (zero, zero)), True)

        n_full, carry = lax.while_loop(
            lambda st: jnp.logical_and(st[0] < 2 * i, lowest(st[1]) <= _SB_DEAD),
            lambda st: (st[0] + 1, step((2 * i - 1 - st[0], 2 * i - st[0]), st[1], False)),
            (jnp.int32(0), carry))
        b_last = jnp.logical_and(n_full == 2 * i, lowest(carry) <= _SB_DEAD)
        carry = lax.cond(b_last, lambda ca: step((None, 0), ca, False), lambda ca: ca, carry)
        for x in range(2):
            (c0, acc0), (c1, acc1) = carry[x]
            o_ref[pl.ds(x * t, t), :] = jnp.where(lane_hi == 0, acc0, acc1).astype(o_ref.dtype)
            tot_ref[pl.ds(x * t, t), :] = jnp.where(lane_hi == 0, c0, c1)
        steps_ref[p, i] = n_full + b_last.astype(jnp.int32)

    return _call(
        body, (qkv, qkv, qkv), rider,
        name="sb_fwd",
        grid=(n_pairs, nq),
        in_specs=[
            pl.BlockSpec((2 * t, LANES), lambda p, i: (i, _SB_Q0 + p)),
            pl.BlockSpec((s, LANES), lambda p, i: (0, _SB_K0 + p)),
            pl.BlockSpec((s, LANES), lambda p, i: (0, _SB_V0 + p)),
        ],
        out_specs=[pl.BlockSpec((2 * t, LANES), lambda p, i: (i, p))] * 2 + [pl.BlockSpec(memory_space=pltpu.SMEM)],
        out_shape=[jax.ShapeDtypeStruct((s, SB_WIDTH), BF16), jax.ShapeDtypeStruct((s, SB_WIDTH), F32),
                   jax.ShapeDtypeStruct((n_pairs, nq), jnp.int32)],
        compiler_params=_cparams(2),
    )


def _sb_bwd(qkv, do_b, tot_b, n_steps):
    s = qkv.shape[0]
    t = SB_TK
    nq = s // (2 * t)
    n_pairs = SB_WIDTH // LANES
    scale = 1.0 / math.sqrt(HEAD_DIM)

    def body(steps_ref, q_ref, k_ref, v_ref, do_ref, tot_ref, dq_ref, dk_ref, dv_ref):
        p, i = pl.program_id(0), pl.program_id(1)

        @pl.when(i == 0)
        def _():
            dk_ref[...] = jnp.zeros_like(dk_ref)
            dv_ref[...] = jnp.zeros_like(dv_ref)

        lane = lax.broadcasted_iota(jnp.int32, (1, LANES), 1)
        lane_hi = lane // HEAD_DIM
        later = _tri(t, lambda r, c: r > c)
        before = _tri(t, lambda r, c: r < c)
        causal = lax.broadcasted_iota(jnp.int32, (t, t), 1) < lax.broadcasted_iota(jnp.int32, (t, t), 0)
        qms, doms, tots = [], [], []
        for x in range(2):
            rows = pl.ds(x * t, t)
            q, do, tot_all = q_ref[rows, :] * scale, do_ref[rows, :], tot_ref[rows, :]
            qms.append([jnp.where(lane_hi == hh, q, jnp.zeros_like(q)) for hh in range(2)])
            doms.append([jnp.where(lane_hi == hh, do, jnp.zeros_like(do)) for hh in range(2)])
            tots.append([jnp.sum(jnp.where(lane == hh * HEAD_DIM, tot_all, 0.0), axis=1, keepdims=True)
                         for hh in range(2)])

        def step(tiles, carry, diag):
            chains = [(x, hh) for x in range(2) if tiles[x] is not None for hh in range(2)]
            offs = {x: pl.multiple_of(tiles[x] * t, t) for x in range(2) if tiles[x] is not None}
            ks = {x: k_ref[pl.ds(off, t), :] for x, off in offs.items()}
            vs = {x: v_ref[pl.ds(off, t), :] for x, off in offs.items()}
            heads = [_sb_chain_head(qms[x][hh], ks[x], causal if diag else None) for x, hh in chains]
            sufs = [_dot(sp.astype(BF16), later) for sp, _ in heads]
            das = [_dot_nt(doms[x][hh], vs[x]) for x, hh in chains]
            new = [list(carry[0]), list(carry[1])]
            sigs, gs, abs_ = [], [], []
            for (x, hh), (sp, lpos), suf, da in zip(chains, heads, sufs, das):
                cl = carry[x][hh][0] + jnp.sum(sp, axis=1, keepdims=True)
                sig = jnp.exp(lpos)
                a = sig * jnp.exp(-suf - (tots[x][hh] - cl))
                if diag:
                    a = jnp.where(causal, a, 0.0)
                g = a * da
                sigs.append(sig)
                gs.append(g)
                abs_.append(a.astype(BF16))
                new[x][hh] = (cl, carry[x][hh][1] + jnp.sum(g, axis=1, keepdims=True), carry[x][hh][2])
            prefs = [_dot(g.astype(BF16), before) for g in gs]
            dvs = [_dot_tn(ab, doms[x][hh]) for (x, hh), ab in zip(chains, abs_)]
            dzs = []
            for (x, hh), sig, g, pref in zip(chains, sigs, gs, prefs):
                dz = g - sig * (g + pref + carry[x][hh][1])
                if diag:
                    dz = jnp.where(causal, dz, 0.0)
                dzs.append(dz.astype(BF16))
            dqs = [_dot(dz, ks[x]) for (x, hh), dz in zip(chains, dzs)]
            dks = [_dot_tn(dz, qms[x][hh]) for (x, hh), dz in zip(chains, dzs)]
            for n, (x, hh) in enumerate(chains):
                cl, cg, dq = new[x][hh]
                new[x][hh] = (cl, cg, dq + dqs[n])
            for x in offs:
                mine = [n for n, ch in enumerate(chains) if ch[0] == x]
                dk_ref[pl.ds(offs[x], t), :] += dks[mine[0]] + dks[mine[1]]
                dv_ref[pl.ds(offs[x], t), :] += dvs[mine[0]] + dvs[mine[1]]
            return (tuple(new[0]), tuple(new[1]))

        taken = steps_ref[p, i]
        n_full = jnp.minimum(taken, 2 * i)
        zero = (jnp.zeros((t, 1), F32), jnp.zeros((t, 1), F32), jnp.zeros((t, LANES), F32))
        carry = ((zero, zero), (zero, zero))
        carry = lax.cond(taken > 2 * i, lambda ca: step((None, 0), ca, False), lambda ca: ca, carry)
        carry = lax.fori_loop(
            0, n_full, lambda n, ca: step((2 * i - n_full + n, 2 * i + 1 - n_full + n), ca, False), carry)
        carry = step((2 * i, 2 * i + 1), carry, True)
        for x in range(2):
            dq = jnp.where(lane_hi == 0, carry[x][0][2], carry[x][1][2])
            dq_ref[pl.ds(x * t, t), :] = (dq * scale).astype(dq_ref.dtype)

    row_spec = pl.BlockSpec((2 * t, LANES), lambda p, i, ns: (i, p))
    full_spec = pl.BlockSpec((s, LANES), lambda p, i, ns: (0, p))
    return _pcall(
        body,
        name="sb_bwd",
        grid_spec=pltpu.PrefetchScalarGridSpec(
            num_scalar_prefetch=1,
            grid=(n_pairs, nq),
            in_specs=[
                pl.BlockSpec((2 * t, LANES), lambda p, i, ns: (i, _SB_Q0 + p)),
                pl.BlockSpec((s, LANES), lambda p, i, ns: (0, _SB_K0 + p)),
                pl.BlockSpec((s, LANES), lambda p, i, ns: (0, _SB_V0 + p)),
                row_spec, row_spec,
            ],
            out_specs=[row_spec, full_spec, full_spec],
        ),
        out_shape=[jax.ShapeDtypeStruct((s, SB_WIDTH), BF16), jax.ShapeDtypeStruct((s, SB_WIDTH), F32),
                   jax.ShapeDtypeStruct((s, SB_WIDTH), F32)],
        compiler_params=_cparams(2),
    )(n_steps, qkv, qkv, qkv, do_b, tot_b)


def _sb_fwd_wide(qkv):
    s = qkv.shape[0]
    t, tq = SB_TK, min(SB_TQ_FWD, s)
    assert tq in (t, 2 * t) and s % (2 * t) == 0
    nq = s // tq
    n_pairs = SB_WIDTH // LANES

    def body(q_ref, k_ref, v_ref, o_ref, tot_ref, steps_ref):
        p, i = pl.program_id(0), pl.program_id(1)
        q = q_ref[...] * (1.0 / math.sqrt(HEAD_DIM))
        lane_hi = lax.broadcasted_iota(jnp.int32, (1, LANES), 1) // HEAD_DIM
        later = _tri(t, lambda r, c: r > c)
        row = lax.broadcasted_iota(jnp.int32, (tq, t), 0)
        col = lax.broadcasted_iota(jnp.int32, (tq, t), 1)
        qms = [jnp.where(lane_hi == hh, q, jnp.zeros_like(q)) for hh in range(2)]

        def step(jj, carry, diag):
            tiles = (2 * jj + 1, 2 * jj)
            offs = [pl.multiple_of(j * t, t) for j in tiles]
            ks = [k_ref[pl.ds(off, t), :] for off in offs]
            vs = [v_ref[pl.ds(off, t), :] for off in offs]
            masks = [(j * t + col) < (i * tq + row) for j in tiles] if diag else None
            chains = [(n, hh) for n in range(2) for hh in range(2)]
            zs = [_dot_nt(qms[hh], ks[n]) for n, hh in chains]
            sps, lposs = [], []
            for (n, hh), z in zip(chains, zs):
                sp = _softplus(z)
                lposs.append(z - sp)
                sps.append(jnp.where(masks[n], sp, 0.0) if diag else sp)
            sufs = [_dot(sp.astype(BF16), later) for sp in sps]
            cs = [carry[0], carry[2]]
            accs = [carry[1], carry[3]]
            for idx, (n, hh) in enumerate(chains):
                a = jnp.exp(lposs[idx] - sufs[idx] - cs[hh])
                if diag:
                    a = jnp.where(masks[n], a, 0.0)
                accs[hh] = accs[hh] + _dot(a.astype(BF16), vs[n])
                cs[hh] = cs[hh] + jnp.sum(sps[idx], axis=1, keepdims=True)
            return cs[0], accs[0], cs[1], accs[1]

        zc, za = jnp.zeros((tq, 1), F32), jnp.zeros((tq, LANES), F32)
        last = (i * tq) // (2 * t)
        carry = step(last, (zc, za, zc, za), True)

        def alive(state):
            n, ca = state
            return jnp.logical_and(n < last, jnp.minimum(jnp.min(ca[0]), jnp.min(ca[2])) <= _SB_DEAD)

        n_off, carry = lax.while_loop(alive, lambda st: (st[0] + 1, step(last - 1 - st[0], st[1], False)),
                                      (jnp.int32(0), carry))
        out = jnp.where(lane_hi == 0, carry[1], carry[3])
        tot = jnp.where(lane_hi == 0, carry[0], carry[2])
        o_ref[...] = out.astype(o_ref.dtype)
        tot_ref[...] = tot
        steps_ref[p, i] = n_off

    o, tot, n_steps = _pcall(
        body,
        name="sb_fwd",
        grid=(n_pairs, nq),
        in_specs=[
            pl.BlockSpec((tq, LANES), lambda p, i: (i, _SB_Q0 + p)),
            pl.BlockSpec((s, LANES), lambda p, i: (0, _SB_K0 + p)),
            pl.BlockSpec((s, LANES), lambda p, i: (0, _SB_V0 + p)),
        ],
        out_specs=[pl.BlockSpec((tq, LANES), lambda p, i: (i, p))] * 2 + [pl.BlockSpec(memory_space=pltpu.SMEM)],
        out_shape=[jax.ShapeDtypeStruct((s, SB_WIDTH), BF16), jax.ShapeDtypeStruct((s, SB_WIDTH), F32),
                   jax.ShapeDtypeStruct((n_pairs, nq), jnp.int32)],
        compiler_params=_cparams(2),
    )(qkv, qkv, qkv)
    return o, tot, n_steps


def _sb_bwd_wide(qkv, do_b, tot_b, n_steps):
    s = qkv.shape[0]
    t, tq = SB_TK, min(SB_TQ_BWD, s)
    assert tq in (t, 2 * t) and s % (2 * t) == 0
    nq = s // tq
    assert nq % n_steps.shape[1] == 0
    n_pairs = SB_WIDTH // LANES
    scale = 1.0 / math.sqrt(HEAD_DIM)

    def body(steps_ref, q_ref, k_ref, v_ref, do_ref, tot_ref, dq_ref, dk_ref, dv_ref):
        p, i = pl.program_id(0), pl.program_id(1)

        @pl.when(i == 0)
        def _():
            dk_ref[...] = jnp.zeros_like(dk_ref)
            dv_ref[...] = jnp.zeros_like(dv_ref)

        q = q_ref[...] * scale
        do = do_ref[...]
        tot_all = tot_ref[...]
        lane = lax.broadcasted_iota(jnp.int32, (1, LANES), 1)
        lane_hi = lane // HEAD_DIM
        later = _tri(t, lambda r, c: r > c)
        before = _tri(t, lambda r, c: r < c)
        row = lax.broadcasted_iota(jnp.int32, (tq, t), 0)
        col = lax.broadcasted_iota(jnp.int32, (tq, t), 1)
        qms = [jnp.where(lane_hi == hh, q, jnp.zeros_like(q)) for hh in range(2)]
        doms = [jnp.where(lane_hi == hh, do, jnp.zeros_like(do)) for hh in range(2)]
        tots = [jnp.sum(jnp.where(lane == hh * HEAD_DIM, tot_all, 0.0), axis=1, keepdims=True) for hh in range(2)]

        def step(jj, carry, diag):
            tiles = (2 * jj, 2 * jj + 1)
            offs = [pl.multiple_of(j * t, t) for j in tiles]
            ks = [k_ref[pl.ds(off, t), :] for off in offs]
            vs = [v_ref[pl.ds(off, t), :] for off in offs]
            masks = [(j * t + col) < (i * tq + row) for j in tiles] if diag else None
            chains = [(n, hh) for n in range(2) for hh in range(2)]
            zs = [_dot_nt(qms[hh], ks[n]) for n, hh in chains]
            sps, sigs = [], []
            for (n, hh), z in zip(chains, zs):
                sp = _softplus(z)
                sigs.append(jnp.exp(z - sp))
                sps.append(jnp.where(masks[n], sp, 0.0) if diag else sp)
            sufs = [_dot(sp.astype(BF16), later) for sp in sps]
            das = [_dot_nt(doms[hh], vs[n]) for n, hh in chains]
            cls = [carry[0], carry[3]]
            cgs = [carry[1], carry[4]]
            accs = [carry[2], carry[5]]
            gs, abs_, cg_at = [], [], []
            for idx, (n, hh) in enumerate(chains):
                cls[hh] = cls[hh] + jnp.sum(sps[idx], axis=1, keepdims=True)
                a = sigs[idx] * jnp.exp(-sufs[idx] - (tots[hh] - cls[hh]))
                if diag:
                    a = jnp.where(masks[n], a, 0.0)
                g = a * das[idx]
                gs.append(g)
                abs_.append(a.astype(BF16))
                cg_at.append(cgs[hh])
                cgs[hh] = cgs[hh] + jnp.sum(g, axis=1, keepdims=True)
            prefs = [_dot(g.astype(BF16), before) for g in gs]
            dvs = [_dot_tn(abs_[idx], doms[hh]) for idx, (n, hh) in enumerate(chains)]
            dzs = []
            for idx, (n, hh) in enumerate(chains):
                g = gs[idx]
                dz = g - sigs[idx] * (g + prefs[idx] + cg_at[idx])
                if diag:
                    dz = jnp.where(masks[n], dz, 0.0)
                dzs.append(dz.astype(BF16))
            for idx, (n, hh) in enumerate(chains):
                accs[hh] = accs[hh] + _dot(dzs[idx], ks[n])
            dks = [_dot_tn(dzs[idx], qms[hh]) for idx, (n, hh) in enumerate(chains)]
            for n in range(2):
                dk_ref[pl.ds(offs[n], t), :] += dks[2 * n] + dks[2 * n + 1]
                dv_ref[pl.ds(offs[n], t), :] += dvs[2 * n] + dvs[2 * n + 1]
            return cls[0], cgs[0], accs[0], cls[1], cgs[1], accs[1]

        zc, za = jnp.zeros((tq, 1), F32), jnp.zeros((tq, LANES), F32)
        last = (i * tq) // (2 * t)
        first = last - steps_ref[p, (i * n_steps.shape[1]) // nq]
        carry = lax.fori_loop(first, last, lambda jj, ca: step(jj, ca, False), (zc, zc, za, zc, zc, za))
        carry = step(last, carry, True)
        dq = jnp.where(lane_hi == 0, carry[2], carry[5])
        dq_ref[...] = (dq * scale).astype(dq_ref.dtype)

    row_spec = pl.BlockSpec((tq, LANES), lambda p, i, ns: (i, p))
    full_spec = pl.BlockSpec((s, LANES), lambda p, i, ns: (0, p))
    return _pcall(
        body,
        name="sb_bwd",
        grid_spec=pltpu.PrefetchScalarGridSpec(
            num_scalar_prefetch=1,
            grid=(n_pairs, nq),
            in_specs=[
                pl.BlockSpec((tq, LANES), lambda p, i, ns: (i, _SB_Q0 + p)),
                pl.BlockSpec((s, LANES), lambda p, i, ns: (0, _SB_K0 + p)),
                pl.BlockSpec((s, LANES), lambda p, i, ns: (0, _SB_V0 + p)),
                row_spec, row_spec,
            ],
            out_specs=[row_spec, full_spec, full_spec],
        ),
        out_shape=[jax.ShapeDtypeStruct((s, SB_WIDTH), BF16), jax.ShapeDtypeStruct((s, SB_WIDTH), F32),
                   jax.ShapeDtypeStruct((s, SB_WIDTH), F32)],
        compiler_params=_cparams(2),
    )(n_steps, qkv, qkv, qkv, do_b, tot_b)


def _gates(gl, bg):
    return _sigmoid(gl[:, :D_MODEL] + bg[:, :D_MODEL]), _sigmoid(gl[:, D_MODEL:] + bg[:, D_MODEL:])


def _mixer_fwd(o_a, o_b, gl, x0, bg, g2, w_ud, w_us, w_out, tm):
    def epi(_, rows, consts):
        oa, ob, glv, x = rows
        bgv, g2v, wud, wus, wout = consts
        ga, gb = _gates(glv, bgv)
        merged = ga * _dot(oa, wud) + gb * _dot(ob, wus)
        x1 = x + _dot(merged.astype(BF16), wout)
        r, xh = _rms_stats(x1)
        return [x1, xh * g2v], []

    return _rowk("mixer_fwd", tm=tm, rows=[o_a, o_b, gl, x0], consts=[bg, g2, w_ud, w_us, w_out],
                 row_outs=[(D_MODEL, F32), (D_MODEL, BF16)], epilogue=epi)


def _mixer_bwd(dx1, o_a, o_b, gl, bg, w_ud, w_us, w_out, tm, rider=None):
    s = dx1.shape[0]
    nm = s // tm

    def body(dx_ref, oa_ref, ob_ref, gl_ref, bg_ref, wud_ref, wus_ref, wout_ref,
             doa_ref, dob_ref, dgl_ref, gwout_ref, gwud_ref, gwus_ref, gbg_ref):
        i = pl.program_id(0)
        dxb = dx_ref[...].astype(BF16)
        oa, ob = oa_ref[...], ob_ref[...]
        ga, gb = _gates(gl_ref[...], bg_ref[...])
        ua, ub = _dot(oa, wud_ref[...]), _dot(ob, wus_ref[...])
        merged = (ga * ua + gb * ub).astype(BF16)
        dm = _dot_nt(dxb, wout_ref[...])
        dua = (dm * ga).astype(BF16)
        dub = (dm * gb).astype(BF16)
        dgla = dm * ua * ga * (1.0 - ga)
        dglb = dm * ub * gb * (1.0 - gb)
        doa_ref[...] = _dot_nt(dua, wud_ref[...]).astype(doa_ref.dtype)
        dob_ref[...] = _dot_nt(dub, wus_ref[...]).astype(dob_ref.dtype)
        dgl_ref[:, :D_MODEL] = dgla.astype(dgl_ref.dtype)
        dgl_ref[:, D_MODEL:] = dglb.astype(dgl_ref.dtype)
        parts = [(gwout_ref, _dot_tn(merged, dxb)), (gwud_ref, _dot_tn(oa, dua)), (gwus_ref, _dot_tn(ob, dub))]
        for r, v in parts:

            @pl.when(i == 0)
            def _(r=r, v=v):
                r[...] = v

            @pl.when(i > 0)
            def _(r=r, v=v):
                r[...] += v

        sa = jnp.sum(dgla, axis=0, keepdims=True)
        sb = jnp.sum(dglb, axis=0, keepdims=True)

        @pl.when(i == 0)
        def _():
            gbg_ref[:, :D_MODEL] = sa
            gbg_ref[:, D_MODEL:] = sb

        @pl.when(i > 0)
        def _():
            gbg_ref[:, :D_MODEL] += sa
            gbg_ref[:, D_MODEL:] += sb

    row = lambda w: pl.BlockSpec((tm, w), lambda i: (i, 0))
    full = lambda a: pl.BlockSpec(a.shape, lambda i: (0, 0))
    fshape = lambda r, c: jax.ShapeDtypeStruct((r, c), F32)
    return _call(
        body, (dx1, o_a, o_b, gl, bg, w_ud, w_us, w_out), rider,
        name="mixer_bwd",
        grid=(nm,),
        in_specs=[row(D_MODEL), row(DIL_OUT_WIDTH), row(SB_WIDTH), row(2 * D_MODEL),
                  full(bg), full(w_ud), full(w_us), full(w_out)],
        out_specs=[row(DIL_OUT_WIDTH), row(SB_WIDTH), row(2 * D_MODEL),
                   pl.BlockSpec((D_MODEL, D_MODEL), lambda i: (0, 0)),
                   pl.BlockSpec((DIL_OUT_WIDTH, D_MODEL), lambda i: (0, 0)),
                   pl.BlockSpec((SB_WIDTH, D_MODEL), lambda i: (0, 0)),
                   pl.BlockSpec((1, 2 * D_MODEL), lambda i: (0, 0))],
        out_shape=[jax.ShapeDtypeStruct((s, DIL_OUT_WIDTH), BF16), jax.ShapeDtypeStruct((s, SB_WIDTH), BF16),
                   jax.ShapeDtypeStruct((s, 2 * D_MODEL), BF16),
                   fshape(D_MODEL, D_MODEL), fshape(DIL_OUT_WIDTH, D_MODEL), fshape(SB_WIDTH, D_MODEL),
                   fshape(1, 2 * D_MODEL)],
        compiler_params=_cparams(1),
    )


_HBM = pl.BlockSpec(memory_space=pltpu.HBM)
_MESH = pl.DeviceIdType.MESH


def _all_gather(shards):
    n = len(shards)

    def body(*refs):
        x_refs, out_refs = refs[:n], refs[n:2 * n]
        send_sems, recv_sems, local_sems = refs[2 * n:]
        x, y, c = lax.axis_index("x"), lax.axis_index("y"), lax.axis_index("c")
        me, sibling = (x, y, c), (x, y, 1 - c)
        chips = [(1 - x, y), (x, 1 - y), (1 - x, 1 - y)]

        def slot(a, px, py, pc):
            return out_refs[a].at[4 * px + 2 * py + pc]

        def copy(a, k, block, to, own=False):
            return pltpu.make_async_remote_copy(
                src_ref=x_refs[a] if own else slot(a, *block), dst_ref=slot(a, *block),
                send_sem=send_sems.at[7 * a + k], recv_sem=recv_sems.at[7 * a + k], device_id=to, device_id_type=_MESH)

        mine = [pltpu.make_async_copy(x_refs[a], slot(a, *me), local_sems.at[a]) for a in range(n)]
        for cp in mine:
            cp.start()
        first = []
        for a in range(n):
            first.append(copy(a, 0, me, sibling, own=True))
            first += [copy(a, 1 + j, me, (*chip, c), own=True) for j, chip in enumerate(chips)]
        for cp in first:
            cp.start()
        passed = []
        for a in range(n):
            for j, chip in enumerate(chips):
                copy(a, 1 + j, (*chip, c), me).wait_recv()
                passed.append(copy(a, 4 + j, (*chip, c), sibling))
                passed[-1].start()
        for a in range(n):
            copy(a, 0, sibling, me).wait_recv()
            for j, chip in enumerate(chips):
                copy(a, 4 + j, (*chip, 1 - c), me).wait_recv()
        for cp in first + passed:
            cp.wait_send()
        for cp in mine:
            cp.wait()

    return _pcall(
        body,
        name="all_gather_weights",
        in_specs=[_HBM] * n,
        out_specs=[_HBM] * n,
        out_shape=[jax.ShapeDtypeStruct((N_DEV,) + s.shape, s.dtype) for s in shards],
        scratch_shapes=[pltpu.SemaphoreType.DMA((7 * n,)), pltpu.SemaphoreType.DMA((7 * n,)),
                        pltpu.SemaphoreType.DMA((n,))],
    )(*shards)


def _exchange(chunks):
    n = len(chunks)

    def body(*refs):
        g_refs, o_refs = refs[:n], refs[n:2 * n]
        send_sems, recv_sems, local_sems = refs[2 * n:]
        x, y, c = lax.axis_index("x"), lax.axis_index("y"), lax.axis_index("c")
        me = 4 * x + 2 * y + c
        own = [pltpu.make_async_copy(g_refs[a].at[me], o_refs[a].at[me], local_sems.at[a]) for a in range(n)]
        for cp in own:
            cp.start()
        copies = []
        for a in range(n):
            for k in range(1, N_DEV):
                px, py, pc = x ^ (k >> 2), y ^ ((k >> 1) & 1), c ^ (k & 1)
                peer = 4 * px + 2 * py + pc
                copies.append(pltpu.make_async_remote_copy(
                    src_ref=g_refs[a].at[peer], dst_ref=o_refs[a].at[me], send_sem=send_sems.at[7 * a + k - 1],
                    recv_sem=recv_sems.at[7 * a + k - 1], device_id=(px, py, pc), device_id_type=_MESH))
        for cp in copies:
            cp.start()
        for cp in copies:
            cp.wait()
        for cp in own:
            cp.wait()

    return _pcall(
        body,
        name="exchange_grads",
        in_specs=[_HBM] * n,
        out_specs=[_HBM] * n,
        out_shape=[jax.ShapeDtypeStruct(g.shape, g.dtype) for g in chunks],
        scratch_shapes=[pltpu.SemaphoreType.DMA((7 * n,)), pltpu.SemaphoreType.DMA((7 * n,)),
                        pltpu.SemaphoreType.DMA((n,))],
    )(*chunks)


_SEM = pl.BlockSpec(memory_space=pltpu.SEMAPHORE)
_EFFECT = pltpu.SideEffectType.DATAFLOW_SIDE_EFFECTING


def _peers(x, y, c):
    out = []
    for k in range(1, N_DEV):
        px, py, pc = x ^ (k >> 2), y ^ ((k >> 1) & 1), c ^ (k & 1)
        out.append(((px, py, pc), 4 * px + 2 * py + pc))
    return out


def _spread_copies(src_refs, land_refs, send_sems, recv_sems, chunked):
    x, y, c = lax.axis_index("x"), lax.axis_index("y"), lax.axis_index("c")
    me = 4 * x + 2 * y + c
    copies = []
    for a, (src, land) in enumerate(zip(src_refs, land_refs)):
        for k, (peer_id, peer) in enumerate(_peers(x, y, c)):
            copies.append(pltpu.make_async_remote_copy(
                src_ref=src.at[peer] if chunked else src, dst_ref=land.at[me], send_sem=send_sems.at[7 * a + k],
                recv_sem=recv_sems.at[7 * a + k], device_id=peer_id, device_id_type=_MESH))
    return copies


def _spread_start(name, srcs, chunked):
    n = len(srcs)
    lands = [lax.empty((N_DEV,) + (s.shape[1:] if chunked else s.shape), s.dtype) for s in srcs]

    def body(*refs):
        src_refs, land_refs = refs[:n], refs[n:2 * n]
        send_sems, recv_sems = refs[2 * n], refs[2 * n + 1]
        token = refs[-1]
        for cp in _spread_copies(src_refs, land_refs, send_sems, recv_sems, chunked):
            cp.start()
        token[...] = jnp.zeros_like(token)

    hbm = lambda a: pltpu.HBM(a.shape, a.dtype)
    outs = _pcall(
        body,
        name=name,
        out_shape=(pltpu.SemaphoreType.DMA((7 * n,)), pltpu.SemaphoreType.DMA((7 * n,)),
                   *[hbm(s) for s in srcs], *[hbm(l) for l in lands], jax.ShapeDtypeStruct((8, LANES), F32)),
        in_specs=[_HBM] * (2 * n),
        out_specs=(_SEM, _SEM, *([_HBM] * (2 * n)), pl.BlockSpec(memory_space=pltpu.VMEM)),
        input_output_aliases={i: 2 + i for i in range(2 * n)},
        compiler_params=pltpu.CompilerParams(has_side_effects=_EFFECT),
    )(*[pltpu.with_memory_space_constraint(a, pltpu.HBM) for a in list(srcs) + lands])
    return outs[0], outs[1], list(outs[2:2 + n]), list(outs[2 + n:2 + 2 * n]), outs[-1]


def _spread_wait(name, send_sems, recv_sems, srcs, lands, after, chunked):
    n = len(srcs)

    def body(*refs):
        src_refs, land_refs = refs[:n], refs[n:2 * n]
        for cp in _spread_copies(src_refs, land_refs, refs[2 * n], refs[2 * n + 1], chunked):
            cp.wait_send()
            cp.wait_recv()

    hbm = lambda a: pltpu.HBM(a.shape, a.dtype)
    outs = _pcall(
        body,
        name=name,
        out_shape=tuple(hbm(a) for a in list(srcs) + list(lands)),
        in_specs=[_HBM] * (2 * n) + [_SEM, _SEM, pl.BlockSpec(memory_space=pl.ANY)],
        out_specs=tuple([_HBM] * (2 * n)),
        input_output_aliases={i: i for i in range(2 * n)},
        compiler_params=pltpu.CompilerParams(has_side_effects=_EFFECT),
    )(*srcs, *lands, send_sems, recv_sems, after)
    return list(outs[:n]), list(outs[n:])


def _with_own(land, own):
    me = 4 * lax.axis_index("x") + 2 * lax.axis_index("y") + lax.axis_index("c")
    return lax.dynamic_update_slice(land, own[None], (me,) + (0,) * own.ndim)


def _reduce_adamw(name, parts, w, m, v, tr):
    _, rows, cols = parts.shape
    tr = min(tr, rows)
    assert rows % tr == 0
    c1 = 1.0 / (1.0 - ADAM_B1 ** ADAM_STEP)
    c2 = 1.0 / (1.0 - ADAM_B2 ** ADAM_STEP)

    def body(p_ref, w_ref, m_ref, v_ref, g_out, d_out, m_out, v_out):
        g = p_ref[0].astype(F32)
        for d in range(1, N_DEV):
            g = g + p_ref[d].astype(F32)
        mn = ADAM_B1 * m_ref[...] + (1.0 - ADAM_B1) * g
        vn = ADAM_B2 * v_ref[...] + (1.0 - ADAM_B2) * (g * g)
        g_out[...] = g
        m_out[...] = mn
        v_out[...] = vn
        d_out[...] = -ADAM_LR * ((mn * c1) / (jnp.sqrt(vn * c2) + ADAM_EPS) + ADAM_WD * w_ref[...])

    spec = pl.BlockSpec((tr, cols), lambda i: (i, 0))
    return _pcall(
        body,
        name=name,
        grid=(rows // tr,),
        in_specs=[pl.BlockSpec((N_DEV, tr, cols), lambda i: (0, i, 0)), spec, spec, spec],
        out_specs=[spec] * 4,
        out_shape=[jax.ShapeDtypeStruct((rows, cols), F32)] * 4,
        compiler_params=_cparams(1),
    )(parts, w, m, v)


_SHARDED = ("w_in", "w_up_dil", "w_up_sb", "w_out", "w_mlp_in", "w_mlp_out")
_FULL_SHAPES = {"w_in": (D_MODEL, IN_COLS), "w_up_dil": (DIL_OUT_WIDTH, D_MODEL), "w_up_sb": (SB_WIDTH, D_MODEL),
                "w_out": (D_MODEL, D_MODEL), "w_mlp_in": (D_MODEL, D_FF), "w_mlp_out": (D_FF, D_MODEL)}
_ROW_SHARDED = ("w_out", "w_mlp_out")


def _shard_shape(name):
    r, c = _FULL_SHAPES[name]
    return (r // N_DEV, c) if name in _ROW_SHARDED else (r, c // N_DEV)


def _assemble(name, gathered):
    r, c = _shard_shape(name)
    if name in _ROW_SHARDED:
        return gathered.reshape(N_DEV * r, c)
    return gathered.transpose(1, 0, 2).reshape(r, N_DEV * c)


def _chunk(name, full):
    r, c = _shard_shape(name)
    if name in _ROW_SHARDED:
        return full.reshape(N_DEV, r, c)
    return full.reshape(r, N_DEV, c).transpose(1, 0, 2)


_SMALL = (("norm_mix_g", D_MODEL), ("b_gate", 2 * D_MODEL), ("norm_mlp_g", D_MODEL), ("norm_final_g", D_MODEL))
_SMALL_N = sum(n for _, n in _SMALL) + LANES


def _pack_small(vals, tail):
    return jnp.concatenate([vals[n].reshape(1, -1) for n, _ in _SMALL] + [tail], axis=1)


def _unpack_small(vec, shapes):
    out, pos = {}, 0
    for n, width in _SMALL:
        out[n] = vec[:, pos:pos + width].reshape(shapes[n])
        pos += width
    return out, vec[:, pos:]


def kernel(x, norm_mix_g, w_in, b_gate, w_up_dil, w_up_sb, w_out, norm_mlp_g, w_mlp_in, w_mlp_out, norm_final_g, loss_target, m_norm_mix_g, m_w_in, m_b_gate, m_w_up_dil, m_w_up_sb, m_w_out, m_norm_mlp_g, m_w_mlp_in, m_w_mlp_out, m_norm_final_g, v_norm_mix_g, v_w_in, v_b_gate, v_w_up_dil, v_w_up_sb, v_w_out, v_norm_mlp_g, v_w_mlp_in, v_w_mlp_out, v_norm_final_g):
    given = dict(locals())
    s = x.shape[1]
    x0 = x.reshape(s, D_MODEL)
    target = loss_target.reshape(s, D_MODEL)
    g1 = norm_mix_g.reshape(1, D_MODEL)
    g2 = norm_mlp_g.reshape(1, D_MODEL)
    g3 = norm_final_g.reshape(1, D_MODEL)
    bg = b_gate.reshape(1, 2 * D_MODEL)
    w_shards = {n: given[n].reshape(_shard_shape(n)) for n in _SHARDED}
    m_shards = {n: given["m_" + n].reshape(_shard_shape(n)) for n in _SHARDED}
    v_shards = {n: given["v_" + n].reshape(_shard_shape(n)) for n in _SHARDED}

    shard_b = {n: w_shards[n].astype(BF16) for n in _SHARDED}
    (gathered_w_in,) = _all_gather([shard_b["w_in"]])
    w_in_f = _assemble("w_in", gathered_w_in)
    w_qkv, w_gl = w_in_f[:, :QKV_COLS], w_in_f[:, QKV_COLS:]
    full = {}

    def norm1(_, rows, consts):
        _, xh = _rms_stats(rows[0])
        return [xh * consts[0]], []

    (h1,) = _rowk("norm_mix", tm=512, rows=[x0], consts=[g1], row_outs=[(D_MODEL, BF16)], epilogue=norm1)
    qkv, (land,) = _mm("proj_qkv", h1, w_qkv, out_dtype=BF16, tm=1024, tn=768, tk=D_MODEL,
                       rider=_Spread([shard_b["w_mlp_in"]], chunked=False))
    full["w_mlp_in"] = _assemble("w_mlp_in", land)
    gl = _mm("proj_gates", h1, w_gl, out_dtype=F32, tm=512, tn=2048, tk=D_MODEL)
    dil = [_dil_fwd(qkv, g) for g in range(len(DIL_GROUPS))]
    os_, lses = [d[0] for d in dil], [d[1] for d in dil]
    o_a = _dil_mix_fwd(os_, lses, 512)
    riding = ("w_mlp_out", "w_out", "w_up_sb", "w_up_dil")
    (o_b, tot_b, sb_steps), lands = _sb_fwd(qkv, rider=_Spread([shard_b[n] for n in riding], chunked=False))
    full.update({n: _assemble(n, land) for n, land in zip(riding, lands)})
    x1, h2 = _mixer_fwd(o_a, o_b, gl, x0, bg, g2, full["w_up_dil"], full["w_up_sb"], full["w_out"], 256)
    f = _mm("mlp_in", h2, full["w_mlp_in"], out_dtype=BF16, tm=1024, tn=1024, tk=D_MODEL,
            epilogue=lambda r, _: jnp.square(jnp.maximum(r, 0.0)))

    def head(acc, rows, consts):
        x1v, tv = rows
        g3v = consts[0]
        x2 = x1v + acc
        r, xh = _rms_stats(x2)
        diff = xh * g3v - tv
        loss = (0.5 / D_MODEL) * jnp.sum(jnp.sum(diff * diff, axis=0, keepdims=True), axis=1, keepdims=True)
        dy = diff * (1.0 / D_MODEL)
        dx2, dg = _rms_bwd(dy, xh, r, g3v)
        return [dx2, dx2], [dg, jnp.broadcast_to(loss, (1, LANES))]

    dx2, dx2b, gg3, loss_part = _rowk(
        "mlp_out_loss", a=f, w=full["w_mlp_out"], tm=512, tk=D_FF, rows=[x1, target], consts=[g3],
        row_outs=[(D_MODEL, F32), (D_MODEL, BF16)], acc_outs=[D_MODEL, LANES], epilogue=head)

    da = _mm("mlp_out_bwd", dx2b, full["w_mlp_out"], tb=True, out_dtype=BF16, tm=1024, tn=1024, tk=D_MODEL, extra=f,
             epilogue=lambda r, fv: r * (2.0 * jnp.sqrt(fv.astype(F32))))
    g_w_mlp_out = _mm("grad_w_mlp_out", f, dx2b, ta=True, out_dtype=F32, tm=1024, tn=1024, tk=2048)
    g_w_mlp_in = _mm("grad_w_mlp_in", h2, da, ta=True, out_dtype=F32, tm=1024, tn=1024, tk=2048)

    def norm_bwd(acc, rows, consts):
        xv, dres = rows
        r, xh = _rms_stats(xv)
        dx, dg = _rms_bwd(acc, xh, r, consts[0])
        return [dres + dx], [dg]

    bchunk = lambda n, g: _chunk(n, g).astype(BF16)
    parts = {}
    (dx1, gg2), (parts["w_mlp_in"],) = _rowk(
        "mlp_in_bwd", a=da, w=full["w_mlp_in"], nt=True, tm=512, tk=D_FF, rows=[x1, dx2], consts=[g2],
        row_outs=[(D_MODEL, F32)], acc_outs=[D_MODEL], epilogue=norm_bwd,
        rider=_Spread([bchunk("w_mlp_in", g_w_mlp_in)], chunked=True))
    (do_a, do_b, dgl, g_w_out, g_w_ud, g_w_us, g_bg), (parts["w_mlp_out"],) = _mixer_bwd(
        dx1, o_a, o_b, gl, bg, full["w_up_dil"], full["w_up_sb"], full["w_out"], 256,
        rider=_Spread([bchunk("w_mlp_out", g_w_mlp_out)], chunked=True))
    mix = _dil_mix_bwd(do_a, os_, lses, 512)
    dil_b = [_dil_bwd(qkv, mix[g], lses[g], mix[3 + g], g) for g in range(2)]
    small_three = {"w_out": g_w_out, "w_up_sb": g_w_us, "w_up_dil": g_w_ud}
    grads, lands = _dil_bwd(qkv, mix[2], lses[2], mix[5], 2,
                            rider=_Spread([bchunk(n, g) for n, g in small_three.items()], chunked=True))
    dil_b.append(grads)
    parts.update(dict(zip(small_three, lands)))
    dq_b, dk_b, dv_b = _sb_bwd(qkv, do_b, tot_b, sb_steps)
    dproj = [d[0] for d in dil_b] + [d[1] for d in dil_b] + [d[2] for d in dil_b] + [dq_b, dk_b, dv_b, dgl]
    g_w_in = jnp.concatenate([
        _grad_cols("grad_w_in_dil", h1, dproj[:9], tm=D_MODEL, tk=1024),
        _grad_cols("grad_w_in_sb", h1, dproj[9:12], tm=D_MODEL, tk=1024),
        _grad_cols("grad_w_in_gates", h1, dproj[12:], tm=D_MODEL, tk=1024)], axis=1)
    (grad_x, gg1), (parts["w_in"],) = _rowk(
        "in_proj_bwd", a=dproj, w=w_in_f, nt=True, tm=512, tk=IN_COLS, rows=[x0, dx1], consts=[g1],
        row_outs=[(D_MODEL, F32)], acc_outs=[D_MODEL], epilogue=norm_bwd,
        rider=_Spread([bchunk("w_in", g_w_in)], chunked=True))

    small_part = _pack_small({"norm_mix_g": gg1, "b_gate": g_bg, "norm_mlp_g": gg2, "norm_final_g": gg3}, loss_part)
    (small_parts,) = _exchange([jnp.broadcast_to(small_part[None], (N_DEV, 1, _SMALL_N))])

    tags = ("grad_", "delta_", "new_m_", "new_v_")
    outs = {}
    for n, p in parts.items():
        res = _reduce_adamw("adamw_" + n, p, w_shards[n], m_shards[n], v_shards[n], 128)
        for tag, val in zip(tags, res):
            outs[tag + n] = val.reshape(given[n].shape)
    small_w = _pack_small(given, jnp.zeros((1, LANES), F32))
    small_m = _pack_small({n: given["m_" + n] for n, _ in _SMALL}, jnp.zeros((1, LANES), F32))
    small_v = _pack_small({n: given["v_" + n] for n, _ in _SMALL}, jnp.ones((1, LANES), F32))
    small_res = _reduce_adamw("adamw_replicated", small_parts, small_w, small_m, small_v, 8)

    small_shapes = {n: given[n].shape for n, _ in _SMALL}
    for tag, small in zip(tags, small_res):
        small_vals, tail = _unpack_small(small, small_shapes)
        for n, val in small_vals.items():
            outs[tag + n] = val
        if tag == "grad_":
            loss = tail[0, 0]
    names = ["norm_mix_g", "w_in", "b_gate", "w_up_dil", "w_up_sb", "w_out", "norm_mlp_g", "w_mlp_in", "w_mlp_out",
             "norm_final_g"]
    return (loss, grad_x.reshape(x.shape), *[outs["grad_" + n] for n in names], *[outs["delta_" + n] for n in names],
            *[outs["new_m_" + n] for n in names], *[outs["new_v_" + n] for n in names])
```

```python
import functools
import math

import jax
import jax.numpy as jnp
from jax import lax
from jax.experimental import pallas as pl
from jax.experimental.pallas import tpu as pltpu

_pcall = pl.pallas_call

F32 = jnp.float32
BF16 = jnp.bfloat16

D_MODEL = 1024
HEAD_DIM = 64
DIL_GROUPS = ((128, 1), (512, 4), (2048, 16))
DIL_HEADS_PER_GROUP = 4
N_DIL_HEADS = 12
N_SB_HEADS = 8
DIL_WIDTH = 768
DIL_OUT_WIDTH = 256
SB_WIDTH = 512
D_FF = 4096
BLOCK = 128
RMS_EPS = 1e-6
NEG_INF = -1e30
QKV_COLS = 3 * DIL_WIDTH + 3 * SB_WIDTH
IN_COLS = QKV_COLS + 2 * D_MODEL
N_DEV = 8

ADAM_LR = 0.001
ADAM_B1 = 0.9
ADAM_B2 = 0.999
ADAM_EPS = 1e-08
ADAM_WD = 0.01
ADAM_STEP = 10

VMEM_LIMIT = 56 * 1024 * 1024
SB_TK = 256
SB_TQ_FWD = 512
SB_TQ_BWD = 256
LANES = 128

_ARB = pltpu.ARBITRARY


def _cparams(n_axes, **kw):
    return pltpu.CompilerParams(dimension_semantics=(_ARB,) * n_axes, vmem_limit_bytes=VMEM_LIMIT, **kw)


def _dot(a, b):
    return jnp.dot(a, b, preferred_element_type=F32)


def _dot_nt(a, b):
    return lax.dot_general(a, b, (((1,), (1,)), ((), ())), preferred_element_type=F32)


def _dot_tn(a, b):
    return lax.dot_general(a, b, (((0,), (0,)), ((), ())), preferred_element_type=F32)


def _split_hi_lo(x):
    hi = x.astype(BF16)
    lo = (x - hi.astype(F32)).astype(BF16)
    return hi, lo


def _dot_hi_lo(x, m):
    hi, lo = _split_hi_lo(x)
    return _dot(hi, m) + _dot(lo, m)


def _sigmoid(x):
    return 1.0 / (1.0 + jnp.exp(-x))


_HBM = pl.BlockSpec(memory_space=pltpu.HBM)
_MESH = pl.DeviceIdType.MESH


class _Spread:
    def __init__(self, srcs, chunked):
        self.srcs, self.chunked, self.n = list(srcs), chunked, len(srcs)

    def land_shapes(self):
        return [jax.ShapeDtypeStruct((N_DEV,) + (s.shape[1:] if self.chunked else s.shape), s.dtype) for s in self.srcs]

    def scratch(self):
        dma = pltpu.SemaphoreType.DMA
        return [dma((7 * self.n,)), dma((7 * self.n,)), dma((self.n,))]

    def copies(self, src_refs, land_refs, send_sems, recv_sems, local_sems):
        x, y, c = lax.axis_index("x"), lax.axis_index("y"), lax.axis_index("c")
        me = 4 * x + 2 * y + c
        out = []
        for a, (src, land) in enumerate(zip(src_refs, land_refs)):
            out.append(pltpu.make_async_copy(src.at[me] if self.chunked else src, land.at[me], local_sems.at[a]))
            for k in range(1, N_DEV):
                px, py, pc = x ^ (k >> 2), y ^ ((k >> 1) & 1), c ^ (k & 1)
                out.append(pltpu.make_async_remote_copy(
                    src_ref=src.at[4 * px + 2 * py + pc] if self.chunked else src, dst_ref=land.at[me],
                    send_sem=send_sems.at[7 * a + k - 1], recv_sem=recv_sems.at[7 * a + k - 1],
                    device_id=(px, py, pc), device_id_type=_MESH))
        return out


def _call(body, args, rider=None, **kw):
    if rider is None:
        return _pcall(body, **kw)(*args)
    grid = kw["grid"]
    single = not isinstance(kw["out_shape"], (list, tuple))
    out_specs = [kw["out_specs"]] if single else list(kw["out_specs"])
    out_shape = [kw["out_shape"]] if single else list(kw["out_shape"])
    in_specs, scratch = list(kw["in_specs"]), list(kw.get("scratch_shapes", []))
    n_in, n_out, n_s, n = len(in_specs), len(out_shape), len(scratch), rider.n

    def hosted(*refs):
        ins, srcs = refs[:n_in], refs[n_in:n_in + n]
        outs, lands = refs[n_in + n:n_in + n + n_out], refs[n_in + n + n_out:n_in + 2 * n + n_out]
        own_scratch, sems = refs[n_in + 2 * n + n_out:n_in + 2 * n + n_out + n_s], refs[n_in + 2 * n + n_out + n_s:]
        ids = [pl.program_id(d) for d in range(len(grid))]
        first = functools.reduce(jnp.logical_and, [i == 0 for i in ids])
        last = functools.reduce(jnp.logical_and, [i == g - 1 for i, g in zip(ids, grid)])
        copies = rider.copies(srcs, lands, *sems)

        @pl.when(first)
        def _():
            for cp in copies:
                cp.start()

        body(*ins, *outs, *own_scratch)

        @pl.when(last)
        def _():
            for cp in copies:
                cp.wait()

    kw = dict(kw, in_specs=in_specs + [_HBM] * n, out_specs=out_specs + [_HBM] * n,
              out_shape=out_shape + rider.land_shapes(), scratch_shapes=scratch + rider.scratch())
    res = _pcall(hosted, **kw)(*args, *rider.srcs)
    return (res[0] if single else list(res[:n_out])), list(res[n_out:])


def _mm(name, a, b, *, ta=False, tb=False, out_dtype, tm, tn, tk, epilogue=None, extra=None, rider=None):
    m = a.shape[1] if ta else a.shape[0]
    k = a.shape[0] if ta else a.shape[1]
    n = b.shape[0] if tb else b.shape[1]
    assert (b.shape[1] if tb else b.shape[0]) == k
    tm, tn, tk = min(tm, m), min(tn, n), min(tk, k)
    assert m % tm == 0 and n % tn == 0 and k % tk == 0, (name, m, n, k, tm, tn, tk)
    nk = k // tk
    dn = (((0 if ta else 1,), (1 if tb else 0,)), ((), ()))
    in_place = nk > 1 and epilogue is None and out_dtype == F32

    def body(*refs):
        if extra is not None:
            a_ref, b_ref, e_ref, o_ref = refs[:4]
        else:
            a_ref, b_ref, o_ref = refs[:3]
            e_ref = None

        def finish(r):
            if epilogue is not None:
                r = epilogue(r, None if e_ref is None else e_ref[...])
            o_ref[...] = r.astype(out_dtype)

        part = lax.dot_general(a_ref[...].astype(BF16), b_ref[...].astype(BF16), dn, preferred_element_type=F32)
        if nk == 1:
            finish(part)
        else:
            acc_ref = o_ref if in_place else refs[-1]
            kk = pl.program_id(2)

            @pl.when(kk == 0)
            def _():
                acc_ref[...] = part

            @pl.when(kk > 0)
            def _():
                acc_ref[...] += part

            if not in_place:

                @pl.when(kk == nk - 1)
                def _():
                    finish(acc_ref[...])

    a_spec = pl.BlockSpec((tk, tm), lambda j, i, kk: (kk, i)) if ta else pl.BlockSpec((tm, tk), lambda j, i, kk: (i, kk))
    b_spec = pl.BlockSpec((tn, tk), lambda j, i, kk: (j, kk)) if tb else pl.BlockSpec((tk, tn), lambda j, i, kk: (kk, j))
    o_spec = pl.BlockSpec((tm, tn), lambda j, i, kk: (i, j))
    in_specs = [a_spec, b_spec]
    args = [a, b]
    if extra is not None:
        in_specs.append(o_spec)
        args.append(extra)
    return _call(
        body, args, rider,
        name=name,
        grid=(n // tn, m // tm, nk),
        in_specs=in_specs,
        out_specs=o_spec,
        out_shape=jax.ShapeDtypeStruct((m, n), out_dtype),
        scratch_shapes=[pltpu.VMEM((tm, tn), F32)] if (nk > 1 and not in_place) else [],
        compiler_params=_cparams(3),
    )


def _grad_cols(name, a, parts, *, tm, tk, rider=None):
    k, m = a.shape
    n = sum(p.shape[1] for p in parts)
    assert m % tm == 0 and k % tk == 0
    nk = k // tk

    def body(*refs):
        a_ref, p_refs, o_ref = refs[0], refs[1:1 + len(parts)], refs[1 + len(parts)]
        kk = pl.program_id(1)
        side_by_side = jnp.concatenate([p_ref[...].astype(BF16) for p_ref in p_refs], axis=1)
        term = _dot_tn(a_ref[...].astype(BF16), side_by_side)

        @pl.when(kk == 0)
        def _():
            o_ref[...] = term

        @pl.when(kk > 0)
        def _():
            o_ref[...] += term

    return _call(
        body, [a] + list(parts), rider,
        name=name,
        grid=(m // tm, nk),
        in_specs=[pl.BlockSpec((tk, tm), lambda i, kk: (kk, i))]
        + [pl.BlockSpec((tk, p.shape[1]), lambda i, kk: (kk, 0)) for p in parts],
        out_specs=pl.BlockSpec((tm, n), lambda i, kk: (i, 0)),
        out_shape=jax.ShapeDtypeStruct((m, n), F32),
        compiler_params=_cparams(2),
    )


def _rowk(name, *, a=None, w=None, nt=False, tm, tk=None, rows=(), consts=(), row_outs=(), acc_outs=(), epilogue,
          rider=None):
    has_mm = a is not None
    a_parts = list(a) if isinstance(a, (list, tuple)) else ([a] if has_mm else [])
    n_a = len(a_parts)
    m = a_parts[0].shape[0] if has_mm else rows[0].shape[0]
    assert m % tm == 0
    nm = m // tm
    if has_mm:
        k = sum(p.shape[1] for p in a_parts)
        n = w.shape[0] if nt else w.shape[1]
        tk = min(tk, k)
        assert k % tk == 0 and (n_a == 1 or tk == k)
        nk = k // tk
    else:
        nk = 1
    n_rows, n_consts, n_ro, n_ao = len(rows), len(consts), len(row_outs), len(acc_outs)

    def body(*refs):
        pos = 0
        if has_mm:
            a_refs, w_ref = refs[:n_a], refs[n_a]
            pos = n_a + 1
        row_refs = refs[pos:pos + n_rows]
        pos += n_rows
        const_refs = refs[pos:pos + n_consts]
        pos += n_consts
        ro_refs = refs[pos:pos + n_ro]
        pos += n_ro
        ao_refs = refs[pos:pos + n_ao]
        pos += n_ao
        i = pl.program_id(0)
        kk = pl.program_id(1)

        def finish(acc):
            ro_vals, ao_vals = epilogue(acc, [r[...] for r in row_refs], [c[...] for c in const_refs])
            for r, v in zip(ro_refs, ro_vals):
                r[...] = v.astype(r.dtype)
            for r, v in zip(ao_refs, ao_vals):

                @pl.when(i == 0)
                def _(r=r, v=v):
                    r[...] = v

                @pl.when(i > 0)
                def _(r=r, v=v):
                    r[...] += v

        if not has_mm:
            finish(None)
            return
        part, off = None, 0
        for a_ref in a_refs:
            width = a_ref.shape[1]
            cols = slice(None) if n_a == 1 else slice(off, off + width)
            av = a_ref[...].astype(BF16)
            term = _dot_nt(av, w_ref[:, cols]) if nt else _dot(av, w_ref[cols, :])
            part = term if part is None else part + term
            off += width
        if nk == 1:
            finish(part)
        else:
            acc_ref = refs[pos]

            @pl.when(kk == 0)
            def _():
                acc_ref[...] = part

            @pl.when(kk > 0)
            def _():
                acc_ref[...] += part

            @pl.when(kk == nk - 1)
            def _():
                finish(acc_ref[...])

    once = pl.Buffered(1)
    in_specs, args = [], []
    if has_mm:
        for part in a_parts:
            in_specs.append(pl.BlockSpec((tm, tk if n_a == 1 else part.shape[1]), lambda i, kk: (i, kk)))
        w_mode = once if nk == 1 else None
        in_specs.append(pl.BlockSpec((n, tk), lambda i, kk: (0, kk), pipeline_mode=w_mode) if nt
                        else pl.BlockSpec((tk, n), lambda i, kk: (kk, 0), pipeline_mode=w_mode))
        args += a_parts + [w]
    for r in rows:
        in_specs.append(pl.BlockSpec((tm, r.shape[1]), lambda i, kk: (i, 0)))
        args.append(r)
    for c in consts:
        in_specs.append(pl.BlockSpec(c.shape, lambda i, kk: (0,) * c.ndim, pipeline_mode=once))
        args.append(c)
    out_specs, out_shape = [], []
    for width, dt in row_outs:
        out_specs.append(pl.BlockSpec((tm, width), lambda i, kk: (i, 0)))
        out_shape.append(jax.ShapeDtypeStruct((m, width), dt))
    for width in acc_outs:
        out_specs.append(pl.BlockSpec((1, width), lambda i, kk: (0, 0)))
        out_shape.append(jax.ShapeDtypeStruct((1, width), F32))
    return _call(
        body, args, rider,
        name=name,
        grid=(nm, nk),
        in_specs=in_specs,
        out_specs=out_specs,
        out_shape=out_shape,
        scratch_shapes=[pltpu.VMEM((tm, n), F32)] if (has_mm and nk > 1) else [],
        compiler_params=_cparams(2),
    )


def _rms_stats(x):
    r = lax.rsqrt(jnp.mean(x * x, axis=-1, keepdims=True) + RMS_EPS)
    return r, x * r


def _rms_bwd(dh, xh, r, g):
    gy = dh * g
    dx = r * (gy - xh * jnp.mean(gy * xh, axis=-1, keepdims=True))
    return dx, jnp.sum(dh * xh, axis=0, keepdims=True)


def _alibi_slope(head):
    return 2.0 ** (-8.0 * (head + 1) / N_DIL_HEADS)


DIL_STEP_BLOCKS = 4


def _dil_band(first_block):
    qi = lax.broadcasted_iota(jnp.int32, (BLOCK, 2 * BLOCK), 0)
    kj = lax.broadcasted_iota(jnp.int32, (BLOCK, 2 * BLOCK), 1)
    steps = qi + BLOCK - kj
    valid = (steps >= 0) & (steps <= BLOCK)
    if first_block is not False:
        valid = valid & ((kj >= BLOCK) | jnp.logical_not(first_block))
    return steps.astype(F32), valid


def _dil_step_specs(ncb, cols, nblk, clamp):
    def own(col):
        return pl.BlockSpec((nblk * BLOCK, DIL_OUT_WIDTH), lambda r, i: (clamp(i), r * ncb + col))

    def before(col):
        return pl.BlockSpec((BLOCK, DIL_OUT_WIDTH), lambda r, i: (jnp.maximum(clamp(i) * nblk - 1, 0), r * ncb + col))

    return [own(cols[0]), own(cols[1]), before(cols[1]), own(cols[2]), before(cols[2])]


def _dil_fwd(qkv, group):
    window, dilation = DIL_GROUPS[group]
    s = qkv.shape[0]
    sub = s // dilation
    nb = sub // BLOCK
    assert nb * BLOCK * dilation == s and window // dilation == BLOCK
    nblk = min(DIL_STEP_BLOCKS, nb)
    assert nb % nblk == 0
    slopes = [_alibi_slope(group * DIL_HEADS_PER_GROUP + h) * dilation for h in range(DIL_HEADS_PER_GROUP)]

    def body(q_ref, kc_ref, kp_ref, vc_ref, vp_ref, o_ref, lse_ref):
        i = pl.program_id(1)
        kk_all = jnp.concatenate([kp_ref[...], kc_ref[...]], axis=0)
        vv_all = jnp.concatenate([vp_ref[...], vc_ref[...]], axis=0)
        head_id = lax.broadcasted_iota(jnp.int32, (1, DIL_OUT_WIDTH), 1) // HEAD_DIM
        chains = [(b, h) for b in range(nblk) for h in range(DIL_HEADS_PER_GROUP)]
        rows = lambda b: slice(b * BLOCK, (b + 1) * BLOCK)
        keys = lambda b: slice(b * BLOCK, (b + 2) * BLOCK)
        bands = [_dil_band(i == 0 if b == 0 else False) for b in range(nblk)]
        qs = [q_ref[rows(b), :] for b in range(nblk)]
        scores = [_dot_nt(jnp.where(head_id == h, qs[b], jnp.zeros_like(qs[b])), kk_all[keys(b)]) for b, h in chains]
        ps, lses = [], []
        for (b, h), sc in zip(chains, scores):
            steps, valid = bands[b]
            logits = jnp.where(valid, sc * (1.0 / math.sqrt(HEAD_DIM)) - slopes[h] * steps, NEG_INF)
            mx = jnp.max(logits, axis=1, keepdims=True)
            e = jnp.exp(logits - mx)
            den = jnp.sum(e, axis=1, keepdims=True)
            lses.append(mx + jnp.log(den))
            ps.append((e * (1.0 / den)).astype(BF16))
        outs = [_dot(p, vv_all[keys(b)]) for (b, h), p in zip(chains, ps)]
        for b in range(nblk):
            mine = [n for n, ch in enumerate(chains) if ch[0] == b]
            o, lse_all = outs[mine[0]], lses[mine[0]]
            for n in mine[1:]:
                o = jnp.where(head_id == chains[n][1], outs[n], o)
                lse_all = jnp.where(head_id == chains[n][1], lses[n], lse_all)
            o_ref[rows(b), :] = o
            lse_ref[rows(b), :] = jnp.broadcast_to(lse_all, o.shape)

    qkv_v, ncb, cols = _dil_view(qkv, group)
    out_spec = pl.BlockSpec((nblk * BLOCK, DIL_OUT_WIDTH), lambda r, i: (i, r))
    o, lse = _pcall(
        body,
        name=f"dil_fwd_g{group}",
        grid=(dilation, nb // nblk),
        in_specs=_dil_step_specs(ncb, cols, nblk, lambda i: i),
        out_specs=[out_spec, out_spec],
        out_shape=[jax.ShapeDtypeStruct((sub, dilation * DIL_OUT_WIDTH), F32)] * 2,
        compiler_params=_cparams(2),
    )(qkv_v, qkv_v, qkv_v, qkv_v, qkv_v)
    return o.reshape(s, DIL_OUT_WIDTH), lse.reshape(s, DIL_OUT_WIDTH), lse


def _dil_bwd(qkv, do_g, lse_g, dterm_g, group, rider=None):
    window, dilation = DIL_GROUPS[group]
    s = qkv.shape[0]
    sub = s // dilation
    nb = sub // BLOCK
    nblk = min(DIL_STEP_BLOCKS, nb)
    n_steps = nb // nblk
    slopes = [_alibi_slope(group * DIL_HEADS_PER_GROUP + h) * dilation for h in range(DIL_HEADS_PER_GROUP)]
    scale = 1.0 / math.sqrt(HEAD_DIM)
    tail = slice((nblk - 1) * BLOCK, nblk * BLOCK)

    def body(q_ref, kc_ref, kp_ref, vc_ref, vp_ref, do_ref, lse_ref, dt_ref, dq_ref, dk_ref, dv_ref, ck_ref, cv_ref):
        i = pl.program_id(1)

        @pl.when(i == 0)
        def _():
            ck_ref[...] = jnp.zeros_like(ck_ref)
            cv_ref[...] = jnp.zeros_like(cv_ref)

        @pl.when(i < n_steps)
        def _():
            kk_all = jnp.concatenate([kp_ref[...], kc_ref[...]], axis=0)
            vv_all = jnp.concatenate([vp_ref[...], vc_ref[...]], axis=0)
            lane = lax.broadcasted_iota(jnp.int32, (1, DIL_OUT_WIDTH), 1)
            head_id = lane // HEAD_DIM
            chains = [(b, h) for b in range(nblk) for h in range(DIL_HEADS_PER_GROUP)]
            rows = lambda b: slice(b * BLOCK, (b + 1) * BLOCK)
            keys = lambda b: slice(b * BLOCK, (b + 2) * BLOCK)
            bands = [_dil_band(i == 0 if b == 0 else False) for b in range(nblk)]
            qms, doms = [], []
            for b, h in chains:
                q, do = q_ref[rows(b), :], do_ref[rows(b), :]
                qms.append(jnp.where(head_id == h, q, jnp.zeros_like(q)))
                doms.append(jnp.where(head_id == h, do, jnp.zeros_like(do)))
            scores = [_dot_nt(qm, kk_all[keys(b)]) for (b, h), qm in zip(chains, qms)]
            dps = [_dot_nt(dom, vv_all[keys(b)]) for (b, h), dom in zip(chains, doms)]
            pbs, dss = [], []
            for n, (b, h) in enumerate(chains):
                steps, valid = bands[b]
                first = lane == h * HEAD_DIM
                lse = jnp.sum(jnp.where(first, lse_ref[rows(b), :], 0.0), axis=1, keepdims=True)
                dt = jnp.sum(jnp.where(first, dt_ref[rows(b), :], 0.0), axis=1, keepdims=True)
                logits = jnp.where(valid, scores[n] * scale - slopes[h] * steps, NEG_INF)
                p = jnp.where(valid, jnp.exp(logits - lse), 0.0)
                pbs.append(p.astype(BF16))
                dss.append((p * (dps[n] + dt) * scale).astype(BF16))
            dqs = [_dot(ds, kk_all[keys(b)]) for (b, h), ds in zip(chains, dss)]
            dks = [_dot_tn(ds, qm) for ds, qm in zip(dss, qms)]
            dvs = [_dot_tn(pb, dom) for pb, dom in zip(pbs, doms)]
            dkk, dvv = [], []
            for b in range(nblk):
                mine = [n for n, ch in enumerate(chains) if ch[0] == b]
                dq = dqs[mine[0]]
                for n in mine[1:]:
                    dq = jnp.where(head_id == chains[n][1], dqs[n], dq)
                dq_ref[rows(b), :] = dq.astype(dq_ref.dtype)
                dkk.append((dks[mine[0]] + dks[mine[1]]) + (dks[mine[2]] + dks[mine[3]]))
                dvv.append((dvs[mine[0]] + dvs[mine[1]]) + (dvs[mine[2]] + dvs[mine[3]]))
            for out_ref, carry_ref, parts in ((dk_ref, ck_ref, dkk), (dv_ref, cv_ref, dvv)):
                if nblk > 1:
                    out_ref[: (nblk - 1) * BLOCK, :] = carry_ref[: (nblk - 1) * BLOCK, :].astype(out_ref.dtype)
                out_ref[tail, :] = (carry_ref[tail, :] + parts[0][:BLOCK]).astype(out_ref.dtype)
                for b in range(nblk):
                    own = parts[b][BLOCK:]
                    carry_ref[rows(b), :] = own + parts[b + 1][:BLOCK] if b + 1 < nblk else own

        @pl.when(i == n_steps)
        def _():
            dk_ref[...] = ck_ref[...].astype(dk_ref.dtype)
            dv_ref[...] = cv_ref[...].astype(dv_ref.dtype)

    clamp = lambda i: jnp.minimum(i, n_steps - 1)
    qkv_v, ncb, cols = _dil_view(qkv, group)
    view = lambda t: t.reshape(sub, dilation * DIL_OUT_WIDTH)
    row_spec = pl.BlockSpec((nblk * BLOCK, DIL_OUT_WIDTH), lambda r, i: (clamp(i), r))
    late_spec = pl.BlockSpec((nblk * BLOCK, DIL_OUT_WIDTH), lambda r, i: (jnp.maximum(i - 1, 0), r))
    res = _call(
        body, (qkv_v, qkv_v, qkv_v, qkv_v, qkv_v, view(do_g), view(lse_g), view(dterm_g)), rider,
        name=f"dil_bwd_g{group}",
        grid=(dilation, n_steps + 1),
        in_specs=_dil_step_specs(ncb, cols, nblk, clamp) + [row_spec, row_spec, row_spec],
        out_specs=[row_spec, late_spec, late_spec],
        out_shape=[jax.ShapeDtypeStruct((sub, dilation * DIL_OUT_WIDTH), BF16)] * 3,
        scratch_shapes=[pltpu.VMEM((nblk * BLOCK, DIL_OUT_WIDTH), F32)] * 2,
        compiler_params=_cparams(2),
    )
    grads, lands = res if rider is not None else (res, None)
    grads = tuple(g.reshape(s, DIL_OUT_WIDTH) for g in grads)
    return grads if rider is None else (grads, lands)


def _dil_masks(i):
    qi = lax.broadcasted_iota(jnp.int32, (BLOCK, 2 * BLOCK), 0)
    kj = lax.broadcasted_iota(jnp.int32, (BLOCK, 2 * BLOCK), 1)
    steps = qi + BLOCK - kj
    valid = (steps >= 0) & (steps <= BLOCK) & ((kj >= BLOCK) | (i > 0))
    return steps.astype(F32), valid


def _dil_view(qkv, group):
    _, dilation = DIL_GROUPS[group]
    if dilation == 1:
        return qkv, QKV_COLS // DIL_OUT_WIDTH, (group, 3 + group, 6 + group)
    w = DIL_OUT_WIDTH
    own = jnp.concatenate([qkv[:, (3 * part + group) * w:(3 * part + group + 1) * w] for part in range(3)], axis=1)
    return own.reshape(qkv.shape[0] // dilation, dilation * 3 * w), 3, (0, 1, 2)


def _dil_specs(ncb, cols, clamp):
    def cur(col):
        return pl.BlockSpec((BLOCK, DIL_OUT_WIDTH), lambda r, i: (clamp(i), r * ncb + col))

    def prev(col):
        return pl.BlockSpec((BLOCK, DIL_OUT_WIDTH), lambda r, i: (jnp.maximum(clamp(i) - 1, 0), r * ncb + col))

    return [cur(cols[0]), cur(cols[1]), prev(cols[1]), cur(cols[2]), prev(cols[2])]


def _dil_fwd_one(qkv, group):
    window, dilation = DIL_GROUPS[group]
    s = qkv.shape[0]
    sub = s // dilation
    nb = sub // BLOCK
    assert nb * BLOCK * dilation == s and window // dilation == BLOCK
    slopes = [_alibi_slope(group * DIL_HEADS_PER_GROUP + h) * dilation for h in range(DIL_HEADS_PER_GROUP)]

    def body(q_ref, kc_ref, kp_ref, vc_ref, vp_ref, o_ref, lse_ref):
        i = pl.program_id(1)
        q = q_ref[...]
        kk = jnp.concatenate([kp_ref[...], kc_ref[...]], axis=0)
        vv = jnp.concatenate([vp_ref[...], vc_ref[...]], axis=0)
        head_id = lax.broadcasted_iota(jnp.int32, (1, DIL_OUT_WIDTH), 1) // HEAD_DIM
        steps, valid = _dil_masks(i)
        heads = range(DIL_HEADS_PER_GROUP)
        scores = [_dot_nt(jnp.where(head_id == h, q, jnp.zeros_like(q)), kk) for h in heads]
        ps, lses = [], []
        for h in heads:
            logits = scores[h] * (1.0 / math.sqrt(HEAD_DIM)) - slopes[h] * steps
            logits = jnp.where(valid, logits, NEG_INF)
            mx = jnp.max(logits, axis=1, keepdims=True)
            e = jnp.exp(logits - mx)
            den = jnp.sum(e, axis=1, keepdims=True)
            lses.append(mx + jnp.log(den))
            ps.append((e * (1.0 / den)).astype(BF16))
        outs = [_dot(ps[h], vv) for h in heads]
        o, lse_all = outs[0], lses[0]
        for h in heads[1:]:
            o = jnp.where(head_id == h, outs[h], o)
            lse_all = jnp.where(head_id == h, lses[h], lse_all)
        o_ref[...] = o
        lse_ref[...] = jnp.broadcast_to(lse_all, o.shape)

    qkv_v, ncb, cols = _dil_view(qkv, group)
    out_spec = pl.BlockSpec((BLOCK, DIL_OUT_WIDTH), lambda r, i: (i, r))
    o, lse = _pcall(
        body,
        name=f"dil_fwd_g{group}",
        grid=(dilation, nb),
        in_specs=_dil_specs(ncb, cols, lambda i: i),
        out_specs=[out_spec, out_spec],
        out_shape=[jax.ShapeDtypeStruct((sub, dilation * DIL_OUT_WIDTH), F32)] * 2,
        compiler_params=_cparams(2),
    )(qkv_v, qkv_v, qkv_v, qkv_v, qkv_v)
    return o.reshape(s, DIL_OUT_WIDTH), lse.reshape(s, DIL_OUT_WIDTH)


def _dil_bwd_one(qkv, do_g, lse_g, dterm_g, group, rider=None):
    window, dilation = DIL_GROUPS[group]
    s = qkv.shape[0]
    sub = s // dilation
    nb = sub // BLOCK
    slopes = [_alibi_slope(group * DIL_HEADS_PER_GROUP + h) * dilation for h in range(DIL_HEADS_PER_GROUP)]
    scale = 1.0 / math.sqrt(HEAD_DIM)

    def body(q_ref, kc_ref, kp_ref, vc_ref, vp_ref, do_ref, lse_ref, dt_ref, dq_ref, dk_ref, dv_ref, ck_ref, cv_ref):
        i = pl.program_id(1)

        @pl.when(i == 0)
        def _():
            ck_ref[...] = jnp.zeros_like(ck_ref)
            cv_ref[...] = jnp.zeros_like(cv_ref)

        @pl.when(i < nb)
        def _():
            q = q_ref[...]
            do = do_ref[...]
            lse_all = lse_ref[...]
            dt_all = dt_ref[...]
            kk = jnp.concatenate([kp_ref[...], kc_ref[...]], axis=0)
            vv = jnp.concatenate([vp_ref[...], vc_ref[...]], axis=0)
            lane = lax.broadcasted_iota(jnp.int32, (1, DIL_OUT_WIDTH), 1)
            head_id = lane // HEAD_DIM
            steps, valid = _dil_masks(i)
            heads = range(DIL_HEADS_PER_GROUP)
            qms = [jnp.where(head_id == h, q, jnp.zeros_like(q)) for h in heads]
            doms = [jnp.where(head_id == h, do, jnp.zeros_like(do)) for h in heads]
            scores = [_dot_nt(qms[h], kk) for h in heads]
            dps = [_dot_nt(doms[h], vv) for h in heads]
            pbs, dss = [], []
            for h in heads:
                first = lane == h * HEAD_DIM
                lse = jnp.sum(jnp.where(first, lse_all, 0.0), axis=1, keepdims=True)
                dt = jnp.sum(jnp.where(first, dt_all, 0.0), axis=1, keepdims=True)
                logits = scores[h] * scale - slopes[h] * steps
                p = jnp.where(valid, jnp.exp(jnp.where(valid, logits, NEG_INF) - lse), 0.0)
                pbs.append(p.astype(BF16))
                dss.append((p * (dps[h] + dt) * scale).astype(BF16))
            dqs = [_dot(dss[h], kk) for h in heads]
            dks = [_dot_tn(dss[h], qms[h]) for h in heads]
            dvs = [_dot_tn(pbs[h], doms[h]) for h in heads]
            dq = dqs[0]
            for h in heads[1:]:
                dq = jnp.where(head_id == h, dqs[h], dq)
            dkk = (dks[0] + dks[1]) + (dks[2] + dks[3])
            dvv = (dvs[0] + dvs[1]) + (dvs[2] + dvs[3])
            dq_ref[...] = dq.astype(dq_ref.dtype)
            dk_ref[...] = (ck_ref[...] + dkk[:BLOCK]).astype(dk_ref.dtype)
            dv_ref[...] = (cv_ref[...] + dvv[:BLOCK]).astype(dv_ref.dtype)
            ck_ref[...] = dkk[BLOCK:]
            cv_ref[...] = dvv[BLOCK:]

        @pl.when(i == nb)
        def _():
            dk_ref[...] = ck_ref[...].astype(dk_ref.dtype)
            dv_ref[...] = cv_ref[...].astype(dv_ref.dtype)

    clamp = lambda i: jnp.minimum(i, nb - 1)
    qkv_v, ncb, cols = _dil_view(qkv, group)
    view = lambda t: t.reshape(sub, dilation * DIL_OUT_WIDTH)
    row_spec = pl.BlockSpec((BLOCK, DIL_OUT_WIDTH), lambda r, i: (clamp(i), r))
    late_spec = pl.BlockSpec((BLOCK, DIL_OUT_WIDTH), lambda r, i: (jnp.maximum(i - 1, 0), r))
    res = _call(
        body, (qkv_v, qkv_v, qkv_v, qkv_v, qkv_v, view(do_g), view(lse_g), view(dterm_g)), rider,
        name=f"dil_bwd_g{group}",
        grid=(dilation, nb + 1),
        in_specs=_dil_specs(ncb, cols, clamp) + [row_spec, row_spec, row_spec],
        out_specs=[row_spec, late_spec, late_spec],
        out_shape=[jax.ShapeDtypeStruct((sub, dilation * DIL_OUT_WIDTH), BF16)] * 3,
        scratch_shapes=[pltpu.VMEM((BLOCK, DIL_OUT_WIDTH), F32)] * 2,
        compiler_params=_cparams(2),
    )
    grads, lands = res if rider is not None else (res, None)
    grads = tuple(g.reshape(s, DIL_OUT_WIDTH) for g in grads)
    return grads if rider is None else (grads, lands)


def _head_block_ones():
    r = lax.broadcasted_iota(jnp.int32, (DIL_OUT_WIDTH, DIL_OUT_WIDTH), 0) // HEAD_DIM
    c = lax.broadcasted_iota(jnp.int32, (DIL_OUT_WIDTH, DIL_OUT_WIDTH), 1) // HEAD_DIM
    return jnp.where(r == c, 1.0, 0.0).astype(BF16)


def _dil_mix_weights(l0, l1, l2):
    mx = jnp.maximum(jnp.maximum(l0, l1), l2)
    e0, e1, e2 = jnp.exp(l0 - mx), jnp.exp(l1 - mx), jnp.exp(l2 - mx)
    inv = 1.0 / (e0 + e1 + e2)
    return e0 * inv, e1 * inv, e2 * inv


def _dil_mix_fwd(os_, lses, tm):
    def epi(_, rows, consts):
        o0, o1, o2, l0, l1, l2 = rows
        w0, w1, w2 = _dil_mix_weights(l0, l1, l2)
        return [w0 * o0 + w1 * o1 + w2 * o2], []

    (o_a,) = _rowk("dil_mix_fwd", tm=tm, rows=list(os_) + list(lses), row_outs=[(DIL_OUT_WIDTH, BF16)], epilogue=epi)
    return o_a


def _dil_mix_bwd(do_a, os_, lses, tm):
    def epi(_, rows, consts):
        do, o0, o1, o2, l0, l1, l2 = rows
        do = do.astype(F32)
        w0, w1, w2 = _dil_mix_weights(l0, l1, l2)
        mixed = w0 * o0 + w1 * o1 + w2 * o2
        tot = _dot_hi_lo(do * mixed, _head_block_ones())
        return [w0 * do, w1 * do, w2 * do, -w0 * tot, -w1 * tot, -w2 * tot], []

    return _rowk(
        "dil_mix_bwd", tm=tm, rows=[do_a] + list(os_) + list(lses),
        row_outs=[(DIL_OUT_WIDTH, BF16)] * 3 + [(DIL_OUT_WIDTH, F32)] * 3, epilogue=epi)


_SB_Q0 = 3 * DIL_WIDTH // LANES
_SB_K0 = _SB_Q0 + SB_WIDTH // LANES
_SB_V0 = _SB_K0 + SB_WIDTH // LANES


_EXP_CLAMP = 88.0
_SB_DEAD = 104.0


def _tri(t, op):
    r = lax.broadcasted_iota(jnp.int32, (t, t), 0)
    c = lax.broadcasted_iota(jnp.int32, (t, t), 1)
    return jnp.where(op(r, c), 1.0, 0.0).astype(BF16)


def _softplus(z):
    return jnp.maximum(z, jnp.log(1.0 + jnp.exp(jnp.minimum(z, _EXP_CLAMP))))


def _sb_chain_head(qm, kj, mask):
    z = _dot_nt(qm, kj)
    sp = _softplus(z)
    return (sp if mask is None else jnp.where(mask, sp, 0.0)), z - sp


def _sb_fwd(qkv, rider=None):
    s = qkv.shape[0]
    t = SB_TK
    assert s % (2 * t) == 0
    nq = s // (2 * t)
    n_pairs = SB_WIDTH // LANES

    def body(q_ref, k_ref, v_ref, o_ref, tot_ref, steps_ref):
        p, i = pl.program_id(0), pl.program_id(1)
        lane_hi = lax.broadcasted_iota(jnp.int32, (1, LANES), 1) // HEAD_DIM
        later = _tri(t, lambda r, c: r > c)
        causal = lax.broadcasted_iota(jnp.int32, (t, t), 1) < lax.broadcasted_iota(jnp.int32, (t, t), 0)
        qms = []
        for x in range(2):
            q = q_ref[pl.ds(x * t, t), :] * (1.0 / math.sqrt(HEAD_DIM))
            qms.append([jnp.where(lane_hi == hh, q, jnp.zeros_like(q)) for hh in range(2)])

        def tile(j):
            off = pl.multiple_of(j * t, t)
            return k_ref[pl.ds(off, t), :], v_ref[pl.ds(off, t), :]

        def step(tiles, carry, diag):
            chains = [(x, hh) for x in range(2) if tiles[x] is not None for hh in range(2)]
            kv = {x: tile(tiles[x]) for x in range(2) if tiles[x] is not None}
            heads = [_sb_chain_head(qms[x][hh], kv[x][0], causal if diag else None) for x, hh in chains]
            sufs = [_dot(sp.astype(BF16), later) for sp, _ in heads]
            new = [list(carry[0]), list(carry[1])]
            for (x, hh), (sp, lpos), suf in zip(chains, heads, sufs):
                c, acc = carry[x][hh]
                a = jnp.exp(lpos - suf - c)
                if diag:
                    a = jnp.where(causal, a, 0.0)
                new[x][hh] = (c + jnp.sum(sp, axis=1, keepdims=True), acc + _dot(a.astype(BF16), kv[x][1]))
            return (tuple(new[0]), tuple(new[1]))

        def lowest(carry):
            m = [jnp.min(carry[x][hh][0]) for x in range(2) for hh in range(2)]
            return jnp.minimum(jnp.minimum(m[0], m[1]), jnp.minimum(m[2], m[3]))

        zero = (jnp.zeros((t, 1), F32), jnp.zeros((t, LANES), F32))
        carry = step((2 * i, 2 * i + 1), ((zero, zero), (zero, zero)), True)

        n_full, carry = lax.while_loop(
            lambda st: jnp.logical_and(st[0] < 2 * i, lowest(st[1]) <= _SB_DEAD),
            lambda st: (st[0] + 1, step((2 * i - 1 - st[0], 2 * i - st[0]), st[1], False)),
            (jnp.int32(0), carry))
        b_last = jnp.logical_and(n_full == 2 * i, lowest(carry) <= _SB_DEAD)
        carry = lax.cond(b_last, lambda ca: step((None, 0), ca, False), lambda ca: ca, carry)
        for x in range(2):
            (c0, acc0), (c1, acc1) = carry[x]
            o_ref[pl.ds(x * t, t), :] = jnp.where(lane_hi == 0, acc0, acc1).astype(o_ref.dtype)
            tot_ref[pl.ds(x * t, t), :] = jnp.where(lane_hi == 0, c0, c1)
        steps_ref[p, i] = n_full + b_last.astype(jnp.int32)

    return _call(
        body, (qkv, qkv, qkv), rider,
        name="sb_fwd",
        grid=(n_pairs, nq),
        in_specs=[
            pl.BlockSpec((2 * t, LANES), lambda p, i: (i, _SB_Q0 + p)),
            pl.BlockSpec((s, LANES), lambda p, i: (0, _SB_K0 + p)),
            pl.BlockSpec((s, LANES), lambda p, i: (0, _SB_V0 + p)),
        ],
        out_specs=[pl.BlockSpec((2 * t, LANES), lambda p, i: (i, p))] * 2 + [pl.BlockSpec(memory_space=pltpu.SMEM)],
        out_shape=[jax.ShapeDtypeStruct((s, SB_WIDTH), BF16), jax.ShapeDtypeStruct((s, SB_WIDTH), F32),
                   jax.ShapeDtypeStruct((n_pairs, nq), jnp.int32)],
        compiler_params=_cparams(2),
    )


def _sb_bwd(qkv, do_b, tot_b, n_steps):
    s = qkv.shape[0]
    t = SB_TK
    nq = s // (2 * t)
    n_pairs = SB_WIDTH // LANES
    scale = 1.0 / math.sqrt(HEAD_DIM)

    def body(steps_ref, q_ref, k_ref, v_ref, do_ref, tot_ref, dq_ref, dk_ref, dv_ref):
        p, i = pl.program_id(0), pl.program_id(1)

        @pl.when(i == 0)
        def _():
            dk_ref[...] = jnp.zeros_like(dk_ref)
            dv_ref[...] = jnp.zeros_like(dv_ref)

        lane = lax.broadcasted_iota(jnp.int32, (1, LANES), 1)
        lane_hi = lane // HEAD_DIM
        later = _tri(t, lambda r, c: r > c)
        before = _tri(t, lambda r, c: r < c)
        causal = lax.broadcasted_iota(jnp.int32, (t, t), 1) < lax.broadcasted_iota(jnp.int32, (t, t), 0)
        qms, doms, tots = [], [], []
        for x in range(2):
            rows = pl.ds(x * t, t)
            q, do, tot_all = q_ref[rows, :] * scale, do_ref[rows, :], tot_ref[rows, :]
            qms.append([jnp.where(lane_hi == hh, q, jnp.zeros_like(q)) for hh in range(2)])
            doms.append([jnp.where(lane_hi == hh, do, jnp.zeros_like(do)) for hh in range(2)])
            tots.append([jnp.sum(jnp.where(lane == hh * HEAD_DIM, tot_all, 0.0), axis=1, keepdims=True)
                         for hh in range(2)])

        def step(tiles, carry, diag):
            chains = [(x, hh) for x in range(2) if tiles[x] is not None for hh in range(2)]
            offs = {x: pl.multiple_of(tiles[x] * t, t) for x in range(2) if tiles[x] is not None}
            ks = {x: k_ref[pl.ds(off, t), :] for x, off in offs.items()}
            vs = {x: v_ref[pl.ds(off, t), :] for x, off in offs.items()}
            heads = [_sb_chain_head(qms[x][hh], ks[x], causal if diag else None) for x, hh in chains]
            sufs = [_dot(sp.astype(BF16), later) for sp, _ in heads]
            das = [_dot_nt(doms[x][hh], vs[x]) for x, hh in chains]
            new = [list(carry[0]), list(carry[1])]
            sigs, gs, abs_ = [], [], []
            for (x, hh), (sp, lpos), suf, da in zip(chains, heads, sufs, das):
                cl = carry[x][hh][0] + jnp.sum(sp, axis=1, keepdims=True)
                sig = jnp.exp(lpos)
                a = sig * jnp.exp(-suf - (tots[x][hh] - cl))
                if diag:
                    a = jnp.where(causal, a, 0.0)
                g = a * da
                sigs.append(sig)
                gs.append(g)
                abs_.append(a.astype(BF16))
                new[x][hh] = (cl, carry[x][hh][1] + jnp.sum(g, axis=1, keepdims=True), carry[x][hh][2])
            prefs = [_dot(g.astype(BF16), before) for g in gs]
            dvs = [_dot_tn(ab, doms[x][hh]) for (x, hh), ab in zip(chains, abs_)]
            dzs = []
            for (x, hh), sig, g, pref in zip(chains, sigs, gs, prefs):
                dz = g - sig * (g + pref + carry[x][hh][1])
                if diag:
                    dz = jnp.where(causal, dz, 0.0)
                dzs.append(dz.astype(BF16))
            dqs = [_dot(dz, ks[x]) for (x, hh), dz in zip(chains, dzs)]
            dks = [_dot_tn(dz, qms[x][hh]) for (x, hh), dz in zip(chains, dzs)]
            for n, (x, hh) in enumerate(chains):
                cl, cg, dq = new[x][hh]
                new[x][hh] = (cl, cg, dq + dqs[n])
            for x in offs:
                mine = [n for n, ch in enumerate(chains) if ch[0] == x]
                dk_ref[pl.ds(offs[x], t), :] += dks[mine[0]] + dks[mine[1]]
                dv_ref[pl.ds(offs[x], t), :] += dvs[mine[0]] + dvs[mine[1]]
            return (tuple(new[0]), tuple(new[1]))

        taken = steps_ref[p, i]
        n_full = jnp.minimum(taken, 2 * i)
        zero = (jnp.zeros((t, 1), F32), jnp.zeros((t, 1), F32), jnp.zeros((t, LANES), F32))
        carry = ((zero, zero), (zero, zero))
        carry = lax.cond(taken > 2 * i, lambda ca: step((None, 0), ca, False), lambda ca: ca, carry)
        carry = lax.fori_loop(
            0, n_full, lambda n, ca: step((2 * i - n_full + n, 2 * i + 1 - n_full + n), ca, False), carry)
        carry = step((2 * i, 2 * i + 1), carry, True)
        for x in range(2):
            dq = jnp.where(lane_hi == 0, carry[x][0][2], carry[x][1][2])
            dq_ref[pl.ds(x * t, t), :] = (dq * scale).astype(dq_ref.dtype)

    row_spec = pl.BlockSpec((2 * t, LANES), lambda p, i, ns: (i, p))
    full_spec = pl.BlockSpec((s, LANES), lambda p, i, ns: (0, p))
    return _pcall(
        body,
        name="sb_bwd",
        grid_spec=pltpu.PrefetchScalarGridSpec(
            num_scalar_prefetch=1,
            grid=(n_pairs, nq),
            in_specs=[
                pl.BlockSpec((2 * t, LANES), lambda p, i, ns: (i, _SB_Q0 + p)),
                pl.BlockSpec((s, LANES), lambda p, i, ns: (0, _SB_K0 + p)),
                pl.BlockSpec((s, LANES), lambda p, i, ns: (0, _SB_V0 + p)),
                row_spec, row_spec,
            ],
            out_specs=[row_spec, full_spec, full_spec],
        ),
        out_shape=[jax.ShapeDtypeStruct((s, SB_WIDTH), BF16), jax.ShapeDtypeStruct((s, SB_WIDTH), F32),
                   jax.ShapeDtypeStruct((s, SB_WIDTH), F32)],
        compiler_params=_cparams(2),
    )(n_steps, qkv, qkv, qkv, do_b, tot_b)


def _sb_fwd_wide(qkv):
    s = qkv.shape[0]
    t, tq = SB_TK, min(SB_TQ_FWD, s)
    assert tq in (t, 2 * t) and s % (2 * t) == 0
    nq = s // tq
    n_pairs = SB_WIDTH // LANES

    def body(q_ref, k_ref, v_ref, o_ref, tot_ref, steps_ref):
        p, i = pl.program_id(0), pl.program_id(1)
        q = q_ref[...] * (1.0 / math.sqrt(HEAD_DIM))
        lane_hi = lax.broadcasted_iota(jnp.int32, (1, LANES), 1) // HEAD_DIM
        later = _tri(t, lambda r, c: r > c)
        row = lax.broadcasted_iota(jnp.int32, (tq, t), 0)
        col = lax.broadcasted_iota(jnp.int32, (tq, t), 1)
        qms = [jnp.where(lane_hi == hh, q, jnp.zeros_like(q)) for hh in range(2)]

        def step(jj, carry, diag):
            tiles = (2 * jj + 1, 2 * jj)
            offs = [pl.multiple_of(j * t, t) for j in tiles]
            ks = [k_ref[pl.ds(off, t), :] for off in offs]
            vs = [v_ref[pl.ds(off, t), :] for off in offs]
            masks = [(j * t + col) < (i * tq + row) for j in tiles] if diag else None
            chains = [(n, hh) for n in range(2) for hh in range(2)]
            zs = [_dot_nt(qms[hh], ks[n]) for n, hh in chains]
            sps, lposs = [], []
            for (n, hh), z in zip(chains, zs):
                sp = _softplus(z)
                lposs.append(z - sp)
                sps.append(jnp.where(masks[n], sp, 0.0) if diag else sp)
            sufs = [_dot(sp.astype(BF16), later) for sp in sps]
            cs = [carry[0], carry[2]]
            accs = [carry[1], carry[3]]
            for idx, (n, hh) in enumerate(chains):
                a = jnp.exp(lposs[idx] - sufs[idx] - cs[hh])
                if diag:
                    a = jnp.where(masks[n], a, 0.0)
                accs[hh] = accs[hh] + _dot(a.astype(BF16), vs[n])
                cs[hh] = cs[hh] + jnp.sum(sps[idx], axis=1, keepdims=True)
            return cs[0], accs[0], cs[1], accs[1]

        zc, za = jnp.zeros((tq, 1), F32), jnp.zeros((tq, LANES), F32)
        last = (i * tq) // (2 * t)
        carry = step(last, (zc, za, zc, za), True)

        def alive(state):
            n, ca = state
            return jnp.logical_and(n < last, jnp.minimum(jnp.min(ca[0]), jnp.min(ca[2])) <= _SB_DEAD)

        n_off, carry = lax.while_loop(alive, lambda st: (st[0] + 1, step(last - 1 - st[0], st[1], False)),
                                      (jnp.int32(0), carry))
        out = jnp.where(lane_hi == 0, carry[1], carry[3])
        tot = jnp.where(lane_hi == 0, carry[0], carry[2])
        o_ref[...] = out.astype(o_ref.dtype)
        tot_ref[...] = tot
        steps_ref[p, i] = n_off

    o, tot, n_steps = _pcall(
        body,
        name="sb_fwd",
        grid=(n_pairs, nq),
        in_specs=[
            pl.BlockSpec((tq, LANES), lambda p, i: (i, _SB_Q0 + p)),
            pl.BlockSpec((s, LANES), lambda p, i: (0, _SB_K0 + p)),
            pl.BlockSpec((s, LANES), lambda p, i: (0, _SB_V0 + p)),
        ],
        out_specs=[pl.BlockSpec((tq, LANES), lambda p, i: (i, p))] * 2 + [pl.BlockSpec(memory_space=pltpu.SMEM)],
        out_shape=[jax.ShapeDtypeStruct((s, SB_WIDTH), BF16), jax.ShapeDtypeStruct((s, SB_WIDTH), F32),
                   jax.ShapeDtypeStruct((n_pairs, nq), jnp.int32)],
        compiler_params=_cparams(2),
    )(qkv, qkv, qkv)
    return o, tot, n_steps


def _sb_bwd_wide(qkv, do_b, tot_b, n_steps):
    s = qkv.shape[0]
    t, tq = SB_TK, min(SB_TQ_BWD, s)
    assert tq in (t, 2 * t) and s % (2 * t) == 0
    nq = s // tq
    assert nq % n_steps.shape[1] == 0
    n_pairs = SB_WIDTH // LANES
    scale = 1.0 / math.sqrt(HEAD_DIM)

    def body(steps_ref, q_ref, k_ref, v_ref, do_ref, tot_ref, dq_ref, dk_ref, dv_ref):
        p, i = pl.program_id(0), pl.program_id(1)

        @pl.when(i == 0)
        def _():
            dk_ref[...] = jnp.zeros_like(dk_ref)
            dv_ref[...] = jnp.zeros_like(dv_ref)

        q = q_ref[...] * scale
        do = do_ref[...]
        tot_all = tot_ref[...]
        lane = lax.broadcasted_iota(jnp.int32, (1, LANES), 1)
        lane_hi = lane // HEAD_DIM
        later = _tri(t, lambda r, c: r > c)
        before = _tri(t, lambda r, c: r < c)
        row = lax.broadcasted_iota(jnp.int32, (tq, t), 0)
        col = lax.broadcasted_iota(jnp.int32, (tq, t), 1)
        qms = [jnp.where(lane_hi == hh, q, jnp.zeros_like(q)) for hh in range(2)]
        doms = [jnp.where(lane_hi == hh, do, jnp.zeros_like(do)) for hh in range(2)]
        tots = [jnp.sum(jnp.where(lane == hh * HEAD_DIM, tot_all, 0.0), axis=1, keepdims=True) for hh in range(2)]

        def step(jj, carry, diag):
            tiles = (2 * jj, 2 * jj + 1)
            offs = [pl.multiple_of(j * t, t) for j in tiles]
            ks = [k_ref[pl.ds(off, t), :] for off in offs]
            vs = [v_ref[pl.ds(off, t), :] for off in offs]
            masks = [(j * t + col) < (i * tq + row) for j in tiles] if diag else None
            chains = [(n, hh) for n in range(2) for hh in range(2)]
            zs = [_dot_nt(qms[hh], ks[n]) for n, hh in chains]
            sps, sigs = [], []
            for (n, hh), z in zip(chains, zs):
                sp = _softplus(z)
                sigs.append(jnp.exp(z - sp))
                sps.append(jnp.where(masks[n], sp, 0.0) if diag else sp)
            sufs = [_dot(sp.astype(BF16), later) for sp in sps]
            das = [_dot_nt(doms[hh], vs[n]) for n, hh in chains]
            cls = [carry[0], carry[3]]
            cgs = [carry[1], carry[4]]
            accs = [carry[2], carry[5]]
            gs, abs_, cg_at = [], [], []
            for idx, (n, hh) in enumerate(chains):
                cls[hh] = cls[hh] + jnp.sum(sps[idx], axis=1, keepdims=True)
                a = sigs[idx] * jnp.exp(-sufs[idx] - (tots[hh] - cls[hh]))
                if diag:
                    a = jnp.where(masks[n], a, 0.0)
                g = a * das[idx]
                gs.append(g)
                abs_.append(a.astype(BF16))
                cg_at.append(cgs[hh])
                cgs[hh] = cgs[hh] + jnp.sum(g, axis=1, keepdims=True)
            prefs = [_dot(g.astype(BF16), before) for g in gs]
            dvs = [_dot_tn(abs_[idx], doms[hh]) for idx, (n, hh) in enumerate(chains)]
            dzs = []
            for idx, (n, hh) in enumerate(chains):
                g = gs[idx]
                dz = g - sigs[idx] * (g + prefs[idx] + cg_at[idx])
                if diag:
                    dz = jnp.where(masks[n], dz, 0.0)
                dzs.append(dz.astype(BF16))
            for idx, (n, hh) in enumerate(chains):
                accs[hh] = accs[hh] + _dot(dzs[idx], ks[n])
            dks = [_dot_tn(dzs[idx], qms[hh]) for idx, (n, hh) in enumerate(chains)]
            for n in range(2):
                dk_ref[pl.ds(offs[n], t), :] += dks[2 * n] + dks[2 * n + 1]
                dv_ref[pl.ds(offs[n], t), :] += dvs[2 * n] + dvs[2 * n + 1]
            return cls[0], cgs[0], accs[0], cls[1], cgs[1], accs[1]

        zc, za = jnp.zeros((tq, 1), F32), jnp.zeros((tq, LANES), F32)
        last = (i * tq) // (2 * t)
        first = last - steps_ref[p, (i * n_steps.shape[1]) // nq]
        carry = lax.fori_loop(first, last, lambda jj, ca: step(jj, ca, False), (zc, zc, za, zc, zc, za))
        carry = step(last, carry, True)
        dq = jnp.where(lane_hi == 0, carry[2], carry[5])
        dq_ref[...] = (dq * scale).astype(dq_ref.dtype)

    row_spec = pl.BlockSpec((tq, LANES), lambda p, i, ns: (i, p))
    full_spec = pl.BlockSpec((s, LANES), lambda p, i, ns: (0, p))
    return _pcall(
        body,
        name="sb_bwd",
        grid_spec=pltpu.PrefetchScalarGridSpec(
            num_scalar_prefetch=1,
            grid=(n_pairs, nq),
            in_specs=[
                pl.BlockSpec((tq, LANES), lambda p, i, ns: (i, _SB_Q0 + p)),
                pl.BlockSpec((s, LANES), lambda p, i, ns: (0, _SB_K0 + p)),
                pl.BlockSpec((s, LANES), lambda p, i, ns: (0, _SB_V0 + p)),
                row_spec, row_spec,
            ],
            out_specs=[row_spec, full_spec, full_spec],
        ),
        out_shape=[jax.ShapeDtypeStruct((s, SB_WIDTH), BF16), jax.ShapeDtypeStruct((s, SB_WIDTH), F32),
                   jax.ShapeDtypeStruct((s, SB_WIDTH), F32)],
        compiler_params=_cparams(2),
    )(n_steps, qkv, qkv, qkv, do_b, tot_b)


def _gates(gl, bg):
    return _sigmoid(gl[:, :D_MODEL] + bg[:, :D_MODEL]), _sigmoid(gl[:, D_MODEL:] + bg[:, D_MODEL:])


def _mixer_fwd(o_a, o_b, gl, x0, bg, g2, w_ud, w_us, w_out, tm):
    def epi(_, rows, consts):
        oa, ob, glv, x = rows
        bgv, g2v, wud, wus, wout = consts
        ga, gb = _gates(glv, bgv)
        merged = ga * _dot(oa, wud) + gb * _dot(ob, wus)
        x1 = x + _dot(merged.astype(BF16), wout)
        r, xh = _rms_stats(x1)
        return [x1, xh * g2v], []

    return _rowk("mixer_fwd", tm=tm, rows=[o_a, o_b, gl, x0], consts=[bg, g2, w_ud, w_us, w_out],
                 row_outs=[(D_MODEL, F32), (D_MODEL, BF16)], epilogue=epi)


def _mixer_bwd(dx1, o_a, o_b, gl, bg, w_ud, w_us, w_out, tm, rider=None):
    s = dx1.shape[0]
    nm = s // tm

    def body(dx_ref, oa_ref, ob_ref, gl_ref, bg_ref, wud_ref, wus_ref, wout_ref,
             doa_ref, dob_ref, dgl_ref, gwout_ref, gwud_ref, gwus_ref, gbg_ref):
        i = pl.program_id(0)
        dxb = dx_ref[...].astype(BF16)
        oa, ob = oa_ref[...], ob_ref[...]
        ga, gb = _gates(gl_ref[...], bg_ref[...])
        ua, ub = _dot(oa, wud_ref[...]), _dot(ob, wus_ref[...])
        merged = (ga * ua + gb * ub).astype(BF16)
        dm = _dot_nt(dxb, wout_ref[...])
        dua = (dm * ga).astype(BF16)
        dub = (dm * gb).astype(BF16)
        dgla = dm * ua * ga * (1.0 - ga)
        dglb = dm * ub * gb * (1.0 - gb)
        doa_ref[...] = _dot_nt(dua, wud_ref[...]).astype(doa_ref.dtype)
        dob_ref[...] = _dot_nt(dub, wus_ref[...]).astype(dob_ref.dtype)
        dgl_ref[:, :D_MODEL] = dgla.astype(dgl_ref.dtype)
        dgl_ref[:, D_MODEL:] = dglb.astype(dgl_ref.dtype)
        parts = [(gwout_ref, _dot_tn(merged, dxb)), (gwud_ref, _dot_tn(oa, dua)), (gwus_ref, _dot_tn(ob, dub))]
        for r, v in parts:

            @pl.when(i == 0)
            def _(r=r, v=v):
                r[...] = v

            @pl.when(i > 0)
            def _(r=r, v=v):
                r[...] += v

        sa = jnp.sum(dgla, axis=0, keepdims=True)
        sb = jnp.sum(dglb, axis=0, keepdims=True)

        @pl.when(i == 0)
        def _():
            gbg_ref[:, :D_MODEL] = sa
            gbg_ref[:, D_MODEL:] = sb

        @pl.when(i > 0)
        def _():
            gbg_ref[:, :D_MODEL] += sa
            gbg_ref[:, D_MODEL:] += sb

    row = lambda w: pl.BlockSpec((tm, w), lambda i: (i, 0))
    full = lambda a: pl.BlockSpec(a.shape, lambda i: (0, 0))
    fshape = lambda r, c: jax.ShapeDtypeStruct((r, c), F32)
    return _call(
        body, (dx1, o_a, o_b, gl, bg, w_ud, w_us, w_out), rider,
        name="mixer_bwd",
        grid=(nm,),
        in_specs=[row(D_MODEL), row(DIL_OUT_WIDTH), row(SB_WIDTH), row(2 * D_MODEL),
                  full(bg), full(w_ud), full(w_us), full(w_out)],
        out_specs=[row(DIL_OUT_WIDTH), row(SB_WIDTH), row(2 * D_MODEL),
                   pl.BlockSpec((D_MODEL, D_MODEL), lambda i: (0, 0)),
                   pl.BlockSpec((DIL_OUT_WIDTH, D_MODEL), lambda i: (0, 0)),
                   pl.BlockSpec((SB_WIDTH, D_MODEL), lambda i: (0, 0)),
                   pl.BlockSpec((1, 2 * D_MODEL), lambda i: (0, 0))],
        out_shape=[jax.ShapeDtypeStruct((s, DIL_OUT_WIDTH), BF16), jax.ShapeDtypeStruct((s, SB_WIDTH), BF16),
                   jax.ShapeDtypeStruct((s, 2 * D_MODEL), BF16),
                   fshape(D_MODEL, D_MODEL), fshape(DIL_OUT_WIDTH, D_MODEL), fshape(SB_WIDTH, D_MODEL),
                   fshape(1, 2 * D_MODEL)],
        compiler_params=_cparams(1),
    )


_HBM = pl.BlockSpec(memory_space=pltpu.HBM)
_MESH = pl.DeviceIdType.MESH


def _all_gather(shards):
    n = len(shards)

    def body(*refs):
        x_refs, out_refs = refs[:n], refs[n:2 * n]
        send_sems, recv_sems, local_sems = refs[2 * n:]
        x, y, c = lax.axis_index("x"), lax.axis_index("y"), lax.axis_index("c")
        me, sibling = (x, y, c), (x, y, 1 - c)
        chips = [(1 - x, y), (x, 1 - y), (1 - x, 1 - y)]

        def slot(a, px, py, pc):
            return out_refs[a].at[4 * px + 2 * py + pc]

        def copy(a, k, block, to, own=False):
            return pltpu.make_async_remote_copy(
                src_ref=x_refs[a] if own else slot(a, *block), dst_ref=slot(a, *block),
                send_sem=send_sems.at[7 * a + k], recv_sem=recv_sems.at[7 * a + k], device_id=to, device_id_type=_MESH)

        mine = [pltpu.make_async_copy(x_refs[a], slot(a, *me), local_sems.at[a]) for a in range(n)]
        for cp in mine:
            cp.start()
        first = []
        for a in range(n):
            first.append(copy(a, 0, me, sibling, own=True))
            first += [copy(a, 1 + j, me, (*chip, c), own=True) for j, chip in enumerate(chips)]
        for cp in first:
            cp.start()
        passed = []
        for a in range(n):
            for j, chip in enumerate(chips):
                copy(a, 1 + j, (*chip, c), me).wait_recv()
                passed.append(copy(a, 4 + j, (*chip, c), sibling))
                passed[-1].start()
        for a in range(n):
            copy(a, 0, sibling, me).wait_recv()
            for j, chip in enumerate(chips):
                copy(a, 4 + j, (*chip, 1 - c), me).wait_recv()
        for cp in first + passed:
            cp.wait_send()
        for cp in mine:
            cp.wait()

    return _pcall(
        body,
        name="all_gather_weights",
        in_specs=[_HBM] * n,
        out_specs=[_HBM] * n,
        out_shape=[jax.ShapeDtypeStruct((N_DEV,) + s.shape, s.dtype) for s in shards],
        scratch_shapes=[pltpu.SemaphoreType.DMA((7 * n,)), pltpu.SemaphoreType.DMA((7 * n,)),
                        pltpu.SemaphoreType.DMA((n,))],
    )(*shards)


def _exchange(chunks):
    n = len(chunks)

    def body(*refs):
        g_refs, o_refs = refs[:n], refs[n:2 * n]
        send_sems, recv_sems, local_sems = refs[2 * n:]
        x, y, c = lax.axis_index("x"), lax.axis_index("y"), lax.axis_index("c")
        me = 4 * x + 2 * y + c
        own = [pltpu.make_async_copy(g_refs[a].at[me], o_refs[a].at[me], local_sems.at[a]) for a in range(n)]
        for cp in own:
            cp.start()
        copies = []
        for a in range(n):
            for k in range(1, N_DEV):
                px, py, pc = x ^ (k >> 2), y ^ ((k >> 1) & 1), c ^ (k & 1)
                peer = 4 * px + 2 * py + pc
                copies.append(pltpu.make_async_remote_copy(
                    src_ref=g_refs[a].at[peer], dst_ref=o_refs[a].at[me], send_sem=send_sems.at[7 * a + k - 1],
                    recv_sem=recv_sems.at[7 * a + k - 1], device_id=(px, py, pc), device_id_type=_MESH))
        for cp in copies:
            cp.start()
        for cp in copies:
            cp.wait()
        for cp in own:
            cp.wait()

    return _pcall(
        body,
        name="exchange_grads",
        in_specs=[_HBM] * n,
        out_specs=[_HBM] * n,
        out_shape=[jax.ShapeDtypeStruct(g.shape, g.dtype) for g in chunks],
        scratch_shapes=[pltpu.SemaphoreType.DMA((7 * n,)), pltpu.SemaphoreType.DMA((7 * n,)),
                        pltpu.SemaphoreType.DMA((n,))],
    )(*chunks)


_SEM = pl.BlockSpec(memory_space=pltpu.SEMAPHORE)
_EFFECT = pltpu.SideEffectType.DATAFLOW_SIDE_EFFECTING


def _peers(x, y, c):
    out = []
    for k in range(1, N_DEV):
        px, py, pc = x ^ (k >> 2), y ^ ((k >> 1) & 1), c ^ (k & 1)
        out.append(((px, py, pc), 4 * px + 2 * py + pc))
    return out


def _spread_copies(src_refs, land_refs, send_sems, recv_sems, chunked):
    x, y, c = lax.axis_index("x"), lax.axis_index("y"), lax.axis_index("c")
    me = 4 * x + 2 * y + c
    copies = []
    for a, (src, land) in enumerate(zip(src_refs, land_refs)):
        for k, (peer_id, peer) in enumerate(_peers(x, y, c)):
            copies.append(pltpu.make_async_remote_copy(
                src_ref=src.at[peer] if chunked else src, dst_ref=land.at[me], send_sem=send_sems.at[7 * a + k],
                recv_sem=recv_sems.at[7 * a + k], device_id=peer_id, device_id_type=_MESH))
    return copies


def _spread_start(name, srcs, chunked):
    n = len(srcs)
    lands = [lax.empty((N_DEV,) + (s.shape[1:] if chunked else s.shape), s.dtype) for s in srcs]

    def body(*refs):
        src_refs, land_refs = refs[:n], refs[n:2 * n]
        send_sems, recv_sems = refs[2 * n], refs[2 * n + 1]
        token = refs[-1]
        for cp in _spread_copies(src_refs, land_refs, send_sems, recv_sems, chunked):
            cp.start()
        token[...] = jnp.zeros_like(token)

    hbm = lambda a: pltpu.HBM(a.shape, a.dtype)
    outs = _pcall(
        body,
        name=name,
        out_shape=(pltpu.SemaphoreType.DMA((7 * n,)), pltpu.SemaphoreType.DMA((7 * n,)),
                   *[hbm(s) for s in srcs], *[hbm(l) for l in lands], jax.ShapeDtypeStruct((8, LANES), F32)),
        in_specs=[_HBM] * (2 * n),
        out_specs=(_SEM, _SEM, *([_HBM] * (2 * n)), pl.BlockSpec(memory_space=pltpu.VMEM)),
        input_output_aliases={i: 2 + i for i in range(2 * n)},
        compiler_params=pltpu.CompilerParams(has_side_effects=_EFFECT),
    )(*[pltpu.with_memory_space_constraint(a, pltpu.HBM) for a in list(srcs) + lands])
    return outs[0], outs[1], list(outs[2:2 + n]), list(outs[2 + n:2 + 2 * n]), outs[-1]


def _spread_wait(name, send_sems, recv_sems, srcs, lands, after, chunked):
    n = len(srcs)

    def body(*refs):
        src_refs, land_refs = refs[:n], refs[n:2 * n]
        for cp in _spread_copies(src_refs, land_refs, refs[2 * n], refs[2 * n + 1], chunked):
            cp.wait_send()
            cp.wait_recv()

    hbm = lambda a: pltpu.HBM(a.shape, a.dtype)
    outs = _pcall(
        body,
        name=name,
        out_shape=tuple(hbm(a) for a in list(srcs) + list(lands)),
        in_specs=[_HBM] * (2 * n) + [_SEM, _SEM, pl.BlockSpec(memory_space=pl.ANY)],
        out_specs=tuple([_HBM] * (2 * n)),
        input_output_aliases={i: i for i in range(2 * n)},
        compiler_params=pltpu.CompilerParams(has_side_effects=_EFFECT),
    )(*srcs, *lands, send_sems, recv_sems, after)
    return list(outs[:n]), list(outs[n:])


def _with_own(land, own):
    me = 4 * lax.axis_index("x") + 2 * lax.axis_index("y") + lax.axis_index("c")
    return lax.dynamic_update_slice(land, own[None], (me,) + (0,) * own.ndim)


def _reduce_adamw(name, parts, w, m, v, tr):
    _, rows, cols = parts.shape
    tr = min(tr, rows)
    assert rows % tr == 0
    c1 = 1.0 / (1.0 - ADAM_B1 ** ADAM_STEP)
    c2 = 1.0 / (1.0 - ADAM_B2 ** ADAM_STEP)

    def body(p_ref, w_ref, m_ref, v_ref, g_out, d_out, m_out, v_out):
        g = p_ref[0].astype(F32)
        for d in range(1, N_DEV):
            g = g + p_ref[d].astype(F32)
        mn = ADAM_B1 * m_ref[...] + (1.0 - ADAM_B1) * g
        vn = ADAM_B2 * v_ref[...] + (1.0 - ADAM_B2) * (g * g)
        g_out[...] = g
        m_out[...] = mn
        v_out[...] = vn
        d_out[...] = -ADAM_LR * ((mn * c1) / (jnp.sqrt(vn * c2) + ADAM_EPS) + ADAM_WD * w_ref[...])

    spec = pl.BlockSpec((tr, cols), lambda i: (i, 0))
    return _pcall(
        body,
        name=name,
        grid=(rows // tr,),
        in_specs=[pl.BlockSpec((N_DEV, tr, cols), lambda i: (0, i, 0)), spec, spec, spec],
        out_specs=[spec] * 4,
        out_shape=[jax.ShapeDtypeStruct((rows, cols), F32)] * 4,
        compiler_params=_cparams(1),
    )(parts, w, m, v)


_SHARDED = ("w_in", "w_up_dil", "w_up_sb", "w_out", "w_mlp_in", "w_mlp_out")
_FULL_SHAPES = {"w_in": (D_MODEL, IN_COLS), "w_up_dil": (DIL_OUT_WIDTH, D_MODEL), "w_up_sb": (SB_WIDTH, D_MODEL),
                "w_out": (D_MODEL, D_MODEL), "w_mlp_in": (D_MODEL, D_FF), "w_mlp_out": (D_FF, D_MODEL)}
_ROW_SHARDED = ("w_out", "w_mlp_out")


def _shard_shape(name):
    r, c = _FULL_SHAPES[name]
    return (r // N_DEV, c) if name in _ROW_SHARDED else (r, c // N_DEV)


def _assemble(name, gathered):
    r, c = _shard_shape(name)
    if name in _ROW_SHARDED:
        return gathered.reshape(N_DEV * r, c)
    return gathered.transpose(1, 0, 2).reshape(r, N_DEV * c)


def _chunk(name, full):
    r, c = _shard_shape(name)
    if name in _ROW_SHARDED:
        return full.reshape(N_DEV, r, c)
    return full.reshape(r, N_DEV, c).transpose(1, 0, 2)


_SMALL = (("norm_mix_g", D_MODEL), ("b_gate", 2 * D_MODEL), ("norm_mlp_g", D_MODEL), ("norm_final_g", D_MODEL))
_SMALL_N = sum(n for _, n in _SMALL) + LANES


def _pack_small(vals, tail):
    return jnp.concatenate([vals[n].reshape(1, -1) for n, _ in _SMALL] + [tail], axis=1)


def _unpack_small(vec, shapes):
    out, pos = {}, 0
    for n, width in _SMALL:
        out[n] = vec[:, pos:pos + width].reshape(shapes[n])
        pos += width
    return out, vec[:, pos:]


def kernel(x, norm_mix_g, w_in, b_gate, w_up_dil, w_up_sb, w_out, norm_mlp_g, w_mlp_in, w_mlp_out, norm_final_g, loss_target, m_norm_mix_g, m_w_in, m_b_gate, m_w_up_dil, m_w_up_sb, m_w_out, m_norm_mlp_g, m_w_mlp_in, m_w_mlp_out, m_norm_final_g, v_norm_mix_g, v_w_in, v_b_gate, v_w_up_dil, v_w_up_sb, v_w_out, v_norm_mlp_g, v_w_mlp_in, v_w_mlp_out, v_norm_final_g):
    given = dict(locals())
    s = x.shape[1]
    x0 = x.reshape(s, D_MODEL)
    target = loss_target.reshape(s, D_MODEL)
    g1 = norm_mix_g.reshape(1, D_MODEL)
    g2 = norm_mlp_g.reshape(1, D_MODEL)
    g3 = norm_final_g.reshape(1, D_MODEL)
    bg = b_gate.reshape(1, 2 * D_MODEL)
    w_shards = {n: given[n].reshape(_shard_shape(n)) for n in _SHARDED}
    m_shards = {n: given["m_" + n].reshape(_shard_shape(n)) for n in _SHARDED}
    v_shards = {n: given["v_" + n].reshape(_shard_shape(n)) for n in _SHARDED}

    shard_b = {n: w_shards[n].astype(BF16) for n in _SHARDED}
    (gathered_w_in,) = _all_gather([shard_b["w_in"]])
    w_in_f = _assemble("w_in", gathered_w_in)
    w_qkv, w_gl = w_in_f[:, :QKV_COLS], w_in_f[:, QKV_COLS:]
    full = {}

    def norm1(_, rows, consts):
        _, xh = _rms_stats(rows[0])
        return [xh * consts[0]], []

    (h1,) = _rowk("norm_mix", tm=512, rows=[x0], consts=[g1], row_outs=[(D_MODEL, BF16)], epilogue=norm1)
    qkv, (land,) = _mm("proj_qkv", h1, w_qkv, out_dtype=BF16, tm=1024, tn=768, tk=D_MODEL,
                       rider=_Spread([shard_b["w_mlp_in"]], chunked=False))
    full["w_mlp_in"] = _assemble("w_mlp_in", land)
    gl = _mm("proj_gates", h1, w_gl, out_dtype=BF16, tm=1024, tn=1024, tk=D_MODEL)
    dil = [_dil_fwd(qkv, g) for g in range(len(DIL_GROUPS))]
    os_, lses = [d[0] for d in dil], [d[1] for d in dil]
    o_a = _dil_mix_fwd(os_, lses, 512)
    riding = ("w_mlp_out", "w_out", "w_up_sb", "w_up_dil")
    (o_b, tot_b, sb_steps), lands = _sb_fwd(qkv, rider=_Spread([shard_b[n] for n in riding], chunked=False))
    full.update({n: _assemble(n, land) for n, land in zip(riding, lands)})
    x1, h2 = _mixer_fwd(o_a, o_b, gl, x0, bg, g2, full["w_up_dil"], full["w_up_sb"], full["w_out"], 512)
    f = _mm("mlp_in", h2, full["w_mlp_in"], out_dtype=BF16, tm=1024, tn=1024, tk=D_MODEL,
            epilogue=lambda r, _: jnp.square(jnp.maximum(r, 0.0)))

    def head(acc, rows, consts):
        x1v, tv = rows
        g3v = consts[0]
        x2 = x1v + acc
        r, xh = _rms_stats(x2)
        diff = xh * g3v - tv
        loss = (0.5 / D_MODEL) * jnp.sum(jnp.sum(diff * diff, axis=0, keepdims=True), axis=1, keepdims=True)
        dy = diff * (1.0 / D_MODEL)
        dx2, dg = _rms_bwd(dy, xh, r, g3v)
        return [dx2, dx2], [dg, jnp.broadcast_to(loss, (1, LANES))]

    dx2, dx2b, gg3, loss_part = _rowk(
        "mlp_out_loss", a=f, w=full["w_mlp_out"], tm=512, tk=D_FF, rows=[x1, target], consts=[g3],
        row_outs=[(D_MODEL, F32), (D_MODEL, BF16)], acc_outs=[D_MODEL, LANES], epilogue=head)

    da = _mm("mlp_out_bwd", dx2b, full["w_mlp_out"], tb=True, out_dtype=BF16, tm=1024, tn=1024, tk=D_MODEL, extra=f,
             epilogue=lambda r, fv: r * (2.0 * jnp.sqrt(fv.astype(F32))))
    g_w_mlp_out = _mm("grad_w_mlp_out", f, dx2b, ta=True, out_dtype=F32, tm=1024, tn=1024, tk=2048)
    g_w_mlp_in = _mm("grad_w_mlp_in", h2, da, ta=True, out_dtype=F32, tm=1024, tn=1024, tk=2048)

    def norm_bwd(acc, rows, consts):
        xv, dres = rows
        r, xh = _rms_stats(xv)
        dx, dg = _rms_bwd(acc, xh, r, consts[0])
        return [dres + dx], [dg]

    bchunk = lambda n, g: _chunk(n, g).astype(BF16)
    parts = {}
    (dx1, gg2), (parts["w_mlp_in"],) = _rowk(
        "mlp_in_bwd", a=da, w=full["w_mlp_in"], nt=True, tm=512, tk=D_FF, rows=[x1, dx2], consts=[g2],
        row_outs=[(D_MODEL, F32)], acc_outs=[D_MODEL], epilogue=norm_bwd,
        rider=_Spread([bchunk("w_mlp_in", g_w_mlp_in)], chunked=True))
    (do_a, do_b, dgl, g_w_out, g_w_ud, g_w_us, g_bg), (parts["w_mlp_out"],) = _mixer_bwd(
        dx1, o_a, o_b, gl, bg, full["w_up_dil"], full["w_up_sb"], full["w_out"], 256,
        rider=_Spread([bchunk("w_mlp_out", g_w_mlp_out)], chunked=True))
    mix = _dil_mix_bwd(do_a, os_, lses, 512)
    dil_b = [_dil_bwd(qkv, mix[g], dil[g][2], mix[3 + g], g) for g in range(2)]
    small_three = {"w_out": g_w_out, "w_up_sb": g_w_us, "w_up_dil": g_w_ud}
    grads, lands = _dil_bwd(qkv, mix[2], dil[2][2], mix[5], 2,
                            rider=_Spread([bchunk(n, g) for n, g in small_three.items()], chunked=True))
    dil_b.append(grads)
    parts.update(dict(zip(small_three, lands)))
    dq_b, dk_b, dv_b = _sb_bwd(qkv, do_b, tot_b, sb_steps)
    dproj = [d[0] for d in dil_b] + [d[1] for d in dil_b] + [d[2] for d in dil_b] + [dq_b, dk_b, dv_b, dgl]
    g_w_in = jnp.concatenate([
        _grad_cols("grad_w_in_dil", h1, dproj[:9], tm=D_MODEL, tk=1024),
        _grad_cols("grad_w_in_sb", h1, dproj[9:12], tm=D_MODEL, tk=1024),
        _grad_cols("grad_w_in_gates", h1, dproj[12:], tm=D_MODEL, tk=1024)], axis=1)
    (grad_x, gg1), (parts["w_in"],) = _rowk(
        "in_proj_bwd", a=dproj, w=w_in_f, nt=True, tm=512, tk=IN_COLS, rows=[x0, dx1], consts=[g1],
        row_outs=[(D_MODEL, F32)], acc_outs=[D_MODEL], epilogue=norm_bwd,
        rider=_Spread([bchunk("w_in", g_w_in)], chunked=True))

    small_part = _pack_small({"norm_mix_g": gg1, "b_gate": g_bg, "norm_mlp_g": gg2, "norm_final_g": gg3}, loss_part)
    (small_parts,) = _exchange([jnp.broadcast_to(small_part[None], (N_DEV, 1, _SMALL_N))])

    tags = ("grad_", "delta_", "new_m_", "new_v_")
    outs = {}
    for n, p in parts.items():
        res = _reduce_adamw("adamw_" + n, p, w_shards[n], m_shards[n], v_shards[n], 128)
        for tag, val in zip(tags, res):
            outs[tag + n] = val.reshape(given[n].shape)
    small_w = _pack_small(given, jnp.zeros((1, LANES), F32))
    small_m = _pack_small({n: given["m_" + n] for n, _ in _SMALL}, jnp.zeros((1, LANES), F32))
    small_v = _pack_small({n: given["v_" + n] for n, _ in _SMALL}, jnp.ones((1, LANES), F32))
    small_res = _reduce_adamw("adamw_replicated", small_parts, small_w, small_m, small_v, 8)

    small_shapes = {n: given[n].shape for n, _ in _SMALL}
    for tag, small in zip(tags, small_res):
        small_vals, tail = _unpack_small(small, small_shapes)
        for n, val in small_vals.items():
            outs[tag + n] = val
        if tag == "grad_":
            loss = tail[0, 0]
    names = ["norm_mix_g", "w_in", "b_gate", "w_up_dil", "w_up_sb", "w_out", "norm_mlp_g", "w_mlp_in", "w_mlp_out",
             "norm_final_g"]
    return (loss, grad_x.reshape(x.shape), *[outs["grad_" + n] for n in names], *[outs["delta_" + n] for n in names],
            *[outs["new_m_" + n] for n in names], *[outs["new_v_" + n] for n in names])
```

```python
import functools
import math

import jax
import jax.numpy as jnp
from jax import lax
from jax.experimental import pallas as pl
from jax.experimental.pallas import tpu as pltpu

_pcall = pl.pallas_call

F32 = jnp.float32
BF16 = jnp.bfloat16

D_MODEL = 1024
HEAD_DIM = 64
DIL_GROUPS = ((128, 1), (512, 4), (2048, 16))
DIL_HEADS_PER_GROUP = 4
N_DIL_HEADS = 12
N_SB_HEADS = 8
DIL_WIDTH = 768
DIL_OUT_WIDTH = 256
SB_WIDTH = 512
D_FF = 4096
BLOCK = 128
RMS_EPS = 1e-6
NEG_INF = -1e30
QKV_COLS = 3 * DIL_WIDTH + 3 * SB_WIDTH
IN_COLS = QKV_COLS + 2 * D_MODEL
N_DEV = 8

ADAM_LR = 0.001
ADAM_B1 = 0.9
ADAM_B2 = 0.999
ADAM_EPS = 1e-08
ADAM_WD = 0.01
ADAM_STEP = 10

VMEM_LIMIT = 56 * 1024 * 1024
SB_TK = 256
LANES = 128

_ARB = pltpu.ARBITRARY


def _cparams(n_axes, **kw):
    return pltpu.CompilerParams(dimension_semantics=(_ARB,) * n_axes, vmem_limit_bytes=VMEM_LIMIT, **kw)


def _dot(a, b):
    return jnp.dot(a, b, preferred_element_type=F32)


def _dot_nt(a, b):
    return lax.dot_general(a, b, (((1,), (1,)), ((), ())), preferred_element_type=F32)


def _dot_tn(a, b):
    return lax.dot_general(a, b, (((0,), (0,)), ((), ())), preferred_element_type=F32)


def _split_hi_lo(x):
    hi = x.astype(BF16)
    lo = (x - hi.astype(F32)).astype(BF16)
    return hi, lo


def _dot_hi_lo(x, m):
    hi, lo = _split_hi_lo(x)
    return _dot(hi, m) + _dot(lo, m)


def _sigmoid(x):
    return 1.0 / (1.0 + jnp.exp(-x))


_HBM = pl.BlockSpec(memory_space=pltpu.HBM)
_MESH = pl.DeviceIdType.MESH


class _Spread:
    def __init__(self, srcs, chunked):
        self.srcs, self.chunked, self.n = list(srcs), chunked, len(srcs)

    def land_shapes(self):
        return [jax.ShapeDtypeStruct((N_DEV,) + (s.shape[1:] if self.chunked else s.shape), s.dtype) for s in self.srcs]

    def scratch(self):
        dma = pltpu.SemaphoreType.DMA
        return [dma((7 * self.n,)), dma((7 * self.n,)), dma((self.n,))]

    def copies(self, src_refs, land_refs, send_sems, recv_sems, local_sems):
        x, y, c = lax.axis_index("x"), lax.axis_index("y"), lax.axis_index("c")
        me = 4 * x + 2 * y + c
        out = []
        for a, (src, land) in enumerate(zip(src_refs, land_refs)):
            out.append(pltpu.make_async_copy(src.at[me] if self.chunked else src, land.at[me], local_sems.at[a]))
            for k in range(1, N_DEV):
                px, py, pc = x ^ (k >> 2), y ^ ((k >> 1) & 1), c ^ (k & 1)
                out.append(pltpu.make_async_remote_copy(
                    src_ref=src.at[4 * px + 2 * py + pc] if self.chunked else src, dst_ref=land.at[me],
                    send_sem=send_sems.at[7 * a + k - 1], recv_sem=recv_sems.at[7 * a + k - 1],
                    device_id=(px, py, pc), device_id_type=_MESH))
        return out


def _call(body, args, rider=None, **kw):
    if rider is None:
        return _pcall(body, **kw)(*args)
    grid = kw["grid"]
    single = not isinstance(kw["out_shape"], (list, tuple))
    out_specs = [kw["out_specs"]] if single else list(kw["out_specs"])
    out_shape = [kw["out_shape"]] if single else list(kw["out_shape"])
    in_specs, scratch = list(kw["in_specs"]), list(kw.get("scratch_shapes", []))
    n_in, n_out, n_s, n = len(in_specs), len(out_shape), len(scratch), rider.n

    def hosted(*refs):
        ins, srcs = refs[:n_in], refs[n_in:n_in + n]
        outs, lands = refs[n_in + n:n_in + n + n_out], refs[n_in + n + n_out:n_in + 2 * n + n_out]
        own_scratch, sems = refs[n_in + 2 * n + n_out:n_in + 2 * n + n_out + n_s], refs[n_in + 2 * n + n_out + n_s:]
        ids = [pl.program_id(d) for d in range(len(grid))]
        first = functools.reduce(jnp.logical_and, [i == 0 for i in ids])
        last = functools.reduce(jnp.logical_and, [i == g - 1 for i, g in zip(ids, grid)])
        copies = rider.copies(srcs, lands, *sems)

        @pl.when(first)
        def _():
            for cp in copies:
                cp.start()

        body(*ins, *outs, *own_scratch)

        @pl.when(last)
        def _():
            for cp in copies:
                cp.wait()

    kw = dict(kw, in_specs=in_specs + [_HBM] * n, out_specs=out_specs + [_HBM] * n,
              out_shape=out_shape + rider.land_shapes(), scratch_shapes=scratch + rider.scratch())
    res = _pcall(hosted, **kw)(*args, *rider.srcs)
    return (res[0] if single else list(res[:n_out])), list(res[n_out:])


def _mm(name, a, b, *, ta=False, tb=False, out_dtype, tm, tn, tk, epilogue=None, extra=None, rider=None):
    m = a.shape[1] if ta else a.shape[0]
    k = a.shape[0] if ta else a.shape[1]
    n = b.shape[0] if tb else b.shape[1]
    assert (b.shape[1] if tb else b.shape[0]) == k
    tm, tn, tk = min(tm, m), min(tn, n), min(tk, k)
    assert m % tm == 0 and n % tn == 0 and k % tk == 0, (name, m, n, k, tm, tn, tk)
    nk = k // tk
    dn = (((0 if ta else 1,), (1 if tb else 0,)), ((), ()))
    in_place = nk > 1 and epilogue is None and out_dtype == F32

    def body(*refs):
        if extra is not None:
            a_ref, b_ref, e_ref, o_ref = refs[:4]
        else:
            a_ref, b_ref, o_ref = refs[:3]
            e_ref = None

        def finish(r):
            if epilogue is not None:
                r = epilogue(r, None if e_ref is None else e_ref[...])
            o_ref[...] = r.astype(out_dtype)

        part = lax.dot_general(a_ref[...].astype(BF16), b_ref[...].astype(BF16), dn, preferred_element_type=F32)
        if nk == 1:
            finish(part)
        else:
            acc_ref = o_ref if in_place else refs[-1]
            kk = pl.program_id(2)

            @pl.when(kk == 0)
            def _():
                acc_ref[...] = part

            @pl.when(kk > 0)
            def _():
                acc_ref[...] += part

            if not in_place:

                @pl.when(kk == nk - 1)
                def _():
                    finish(acc_ref[...])

    a_spec = pl.BlockSpec((tk, tm), lambda j, i, kk: (kk, i)) if ta else pl.BlockSpec((tm, tk), lambda j, i, kk: (i, kk))
    b_spec = pl.BlockSpec((tn, tk), lambda j, i, kk: (j, kk)) if tb else pl.BlockSpec((tk, tn), lambda j, i, kk: (kk, j))
    o_spec = pl.BlockSpec((tm, tn), lambda j, i, kk: (i, j))
    in_specs = [a_spec, b_spec]
    args = [a, b]
    if extra is not None:
        in_specs.append(o_spec)
        args.append(extra)
    return _call(
        body, args, rider,
        name=name,
        grid=(n // tn, m // tm, nk),
        in_specs=in_specs,
        out_specs=o_spec,
        out_shape=jax.ShapeDtypeStruct((m, n), out_dtype),
        scratch_shapes=[pltpu.VMEM((tm, tn), F32)] if (nk > 1 and not in_place) else [],
        compiler_params=_cparams(3),
    )


def _grad_cols(name, a, parts, *, tm, tk, rider=None):
    k, m = a.shape
    n = sum(p.shape[1] for p in parts)
    assert m % tm == 0 and k % tk == 0
    nk = k // tk

    def body(*refs):
        a_ref, p_refs, o_ref = refs[0], refs[1:1 + len(parts)], refs[1 + len(parts)]
        kk = pl.program_id(1)
        side_by_side = jnp.concatenate([p_ref[...].astype(BF16) for p_ref in p_refs], axis=1)
        term = _dot_tn(a_ref[...].astype(BF16), side_by_side)

        @pl.when(kk == 0)
        def _():
            o_ref[...] = term

        @pl.when(kk > 0)
        def _():
            o_ref[...] += term

    return _call(
        body, [a] + list(parts), rider,
        name=name,
        grid=(m // tm, nk),
        in_specs=[pl.BlockSpec((tk, tm), lambda i, kk: (kk, i))]
        + [pl.BlockSpec((tk, p.shape[1]), lambda i, kk: (kk, 0)) for p in parts],
        out_specs=pl.BlockSpec((tm, n), lambda i, kk: (i, 0)),
        out_shape=jax.ShapeDtypeStruct((m, n), F32),
        compiler_params=_cparams(2),
    )


def _rowk(name, *, a=None, w=None, nt=False, tm, tk=None, rows=(), consts=(), row_outs=(), acc_outs=(), epilogue,
          rider=None):
    has_mm = a is not None
    a_parts = list(a) if isinstance(a, (list, tuple)) else ([a] if has_mm else [])
    n_a = len(a_parts)
    m = a_parts[0].shape[0] if has_mm else rows[0].shape[0]
    assert m % tm == 0
    nm = m // tm
    if has_mm:
        k = sum(p.shape[1] for p in a_parts)
        n = w.shape[0] if nt else w.shape[1]
        tk = min(tk, k)
        assert k % tk == 0 and (n_a == 1 or tk == k)
        nk = k // tk
    else:
        nk = 1
    n_rows, n_consts, n_ro, n_ao = len(rows), len(consts), len(row_outs), len(acc_outs)

    def body(*refs):
        pos = 0
        if has_mm:
            a_refs, w_ref = refs[:n_a], refs[n_a]
            pos = n_a + 1
        row_refs = refs[pos:pos + n_rows]
        pos += n_rows
        const_refs = refs[pos:pos + n_consts]
        pos += n_consts
        ro_refs = refs[pos:pos + n_ro]
        pos += n_ro
        ao_refs = refs[pos:pos + n_ao]
        pos += n_ao
        i = pl.program_id(0)
        kk = pl.program_id(1)

        def finish(acc):
            ro_vals, ao_vals = epilogue(acc, [r[...] for r in row_refs], [c[...] for c in const_refs])
            for r, v in zip(ro_refs, ro_vals):
                r[...] = v.astype(r.dtype)
            for r, v in zip(ao_refs, ao_vals):

                @pl.when(i == 0)
                def _(r=r, v=v):
                    r[...] = v

                @pl.when(i > 0)
                def _(r=r, v=v):
                    r[...] += v

        if not has_mm:
            finish(None)
            return
        part, off = None, 0
        for a_ref in a_refs:
            width = a_ref.shape[1]
            cols = slice(None) if n_a == 1 else slice(off, off + width)
            av = a_ref[...].astype(BF16)
            term = _dot_nt(av, w_ref[:, cols]) if nt else _dot(av, w_ref[cols, :])
            part = term if part is None else part + term
            off += width
        if nk == 1:
            finish(part)
        else:
            acc_ref = refs[pos]

            @pl.when(kk == 0)
            def _():
                acc_ref[...] = part

            @pl.when(kk > 0)
            def _():
                acc_ref[...] += part

            @pl.when(kk == nk - 1)
            def _():
                finish(acc_ref[...])

    once = pl.Buffered(1)
    in_specs, args = [], []
    if has_mm:
        for part in a_parts:
            in_specs.append(pl.BlockSpec((tm, tk if n_a == 1 else part.shape[1]), lambda i, kk: (i, kk)))
        w_mode = once if nk == 1 else None
        in_specs.append(pl.BlockSpec((n, tk), lambda i, kk: (0, kk), pipeline_mode=w_mode) if nt
                        else pl.BlockSpec((tk, n), lambda i, kk: (kk, 0), pipeline_mode=w_mode))
        args += a_parts + [w]
    for r in rows:
        in_specs.append(pl.BlockSpec((tm, r.shape[1]), lambda i, kk: (i, 0)))
        args.append(r)
    for c in consts:
        in_specs.append(pl.BlockSpec(c.shape, lambda i, kk: (0,) * c.ndim, pipeline_mode=once))
        args.append(c)
    out_specs, out_shape = [], []
    for width, dt in row_outs:
        out_specs.append(pl.BlockSpec((tm, width), lambda i, kk: (i, 0)))
        out_shape.append(jax.ShapeDtypeStruct((m, width), dt))
    for width in acc_outs:
        out_specs.append(pl.BlockSpec((1, width), lambda i, kk: (0, 0)))
        out_shape.append(jax.ShapeDtypeStruct((1, width), F32))
    return _call(
        body, args, rider,
        name=name,
        grid=(nm, nk),
        in_specs=in_specs,
        out_specs=out_specs,
        out_shape=out_shape,
        scratch_shapes=[pltpu.VMEM((tm, n), F32)] if (has_mm and nk > 1) else [],
        compiler_params=_cparams(2),
    )


def _rms_stats(x):
    r = lax.rsqrt(jnp.mean(x * x, axis=-1, keepdims=True) + RMS_EPS)
    return r, x * r


def _rms_bwd(dh, xh, r, g):
    gy = dh * g
    dx = r * (gy - xh * jnp.mean(gy * xh, axis=-1, keepdims=True))
    return dx, jnp.sum(dh * xh, axis=0, keepdims=True)


def _alibi_slope(head):
    return 2.0 ** (-8.0 * (head + 1) / N_DIL_HEADS)


DIL_STEP_BLOCKS = 4


def _dil_band(first_block):
    qi = lax.broadcasted_iota(jnp.int32, (BLOCK, 2 * BLOCK), 0)
    kj = lax.broadcasted_iota(jnp.int32, (BLOCK, 2 * BLOCK), 1)
    steps = qi + BLOCK - kj
    valid = (steps >= 0) & (steps <= BLOCK)
    if first_block is not False:
        valid = valid & ((kj >= BLOCK) | jnp.logical_not(first_block))
    return steps.astype(F32), valid


def _dil_step_specs(ncb, cols, nblk, clamp):
    def own(col):
        return pl.BlockSpec((nblk * BLOCK, DIL_OUT_WIDTH), lambda r, i: (clamp(i), r * ncb + col))

    def before(col):
        return pl.BlockSpec((BLOCK, DIL_OUT_WIDTH), lambda r, i: (jnp.maximum(clamp(i) * nblk - 1, 0), r * ncb + col))

    return [own(cols[0]), own(cols[1]), before(cols[1]), own(cols[2]), before(cols[2])]


def _dil_fwd(qkv, group):
    window, dilation = DIL_GROUPS[group]
    s = qkv.shape[0]
    sub = s // dilation
    nb = sub // BLOCK
    assert nb * BLOCK * dilation == s and window // dilation == BLOCK
    nblk = min(DIL_STEP_BLOCKS, nb)
    assert nb % nblk == 0
    slopes = [_alibi_slope(group * DIL_HEADS_PER_GROUP + h) * dilation for h in range(DIL_HEADS_PER_GROUP)]

    def body(q_ref, kc_ref, kp_ref, vc_ref, vp_ref, o_ref, lse_ref):
        i = pl.program_id(1)
        kk_all = jnp.concatenate([kp_ref[...], kc_ref[...]], axis=0)
        vv_all = jnp.concatenate([vp_ref[...], vc_ref[...]], axis=0)
        head_id = lax.broadcasted_iota(jnp.int32, (1, DIL_OUT_WIDTH), 1) // HEAD_DIM
        chains = [(b, h) for b in range(nblk) for h in range(DIL_HEADS_PER_GROUP)]
        rows = lambda b: slice(b * BLOCK, (b + 1) * BLOCK)
        keys = lambda b: slice(b * BLOCK, (b + 2) * BLOCK)
        bands = [_dil_band(i == 0 if b == 0 else False) for b in range(nblk)]
        qs = [q_ref[rows(b), :] for b in range(nblk)]
        scores = [_dot_nt(jnp.where(head_id == h, qs[b], jnp.zeros_like(qs[b])), kk_all[keys(b)]) for b, h in chains]
        ps, lses = [], []
        for (b, h), sc in zip(chains, scores):
            steps, valid = bands[b]
            logits = jnp.where(valid, sc * (1.0 / math.sqrt(HEAD_DIM)) - slopes[h] * steps, NEG_INF)
            mx = jnp.max(logits, axis=1, keepdims=True)
            e = jnp.exp(logits - mx)
            den = jnp.sum(e, axis=1, keepdims=True)
            lses.append(mx + jnp.log(den))
            ps.append((e * (1.0 / den)).astype(BF16))
        outs = [_dot(p, vv_all[keys(b)]) for (b, h), p in zip(chains, ps)]
        for b in range(nblk):
            mine = [n for n, ch in enumerate(chains) if ch[0] == b]
            o, lse_all = outs[mine[0]], lses[mine[0]]
            for n in mine[1:]:
                o = jnp.where(head_id == chains[n][1], outs[n], o)
                lse_all = jnp.where(head_id == chains[n][1], lses[n], lse_all)
            o_ref[rows(b), :] = o
            lse_ref[rows(b), :] = jnp.broadcast_to(lse_all, o.shape)

    qkv_v, ncb, cols = _dil_view(qkv, group)
    out_spec = pl.BlockSpec((nblk * BLOCK, DIL_OUT_WIDTH), lambda r, i: (i, r))
    o, lse = _pcall(
        body,
        name=f"dil_fwd_g{group}",
        grid=(dilation, nb // nblk),
        in_specs=_dil_step_specs(ncb, cols, nblk, lambda i: i),
        out_specs=[out_spec, out_spec],
        out_shape=[jax.ShapeDtypeStruct((sub, dilation * DIL_OUT_WIDTH), F32)] * 2,
        compiler_params=_cparams(2),
    )(qkv_v, qkv_v, qkv_v, qkv_v, qkv_v)
    return o.reshape(s, DIL_OUT_WIDTH), lse.reshape(s, DIL_OUT_WIDTH), lse


def _dil_bwd(qkv, do_g, lse_g, dterm_g, group, rider=None):
    window, dilation = DIL_GROUPS[group]
    s = qkv.shape[0]
    sub = s // dilation
    nb = sub // BLOCK
    nblk = min(DIL_STEP_BLOCKS, nb)
    n_steps = nb // nblk
    slopes = [_alibi_slope(group * DIL_HEADS_PER_GROUP + h) * dilation for h in range(DIL_HEADS_PER_GROUP)]
    scale = 1.0 / math.sqrt(HEAD_DIM)
    tail = slice((nblk - 1) * BLOCK, nblk * BLOCK)

    def body(q_ref, kc_ref, kp_ref, vc_ref, vp_ref, do_ref, lse_ref, dt_ref, dq_ref, dk_ref, dv_ref, ck_ref, cv_ref):
        i = pl.program_id(1)

        @pl.when(i == 0)
        def _():
            ck_ref[...] = jnp.zeros_like(ck_ref)
            cv_ref[...] = jnp.zeros_like(cv_ref)

        @pl.when(i < n_steps)
        def _():
            kk_all = jnp.concatenate([kp_ref[...], kc_ref[...]], axis=0)
            vv_all = jnp.concatenate([vp_ref[...], vc_ref[...]], axis=0)
            lane = lax.broadcasted_iota(jnp.int32, (1, DIL_OUT_WIDTH), 1)
            head_id = lane // HEAD_DIM
            chains = [(b, h) for b in range(nblk) for h in range(DIL_HEADS_PER_GROUP)]
            rows = lambda b: slice(b * BLOCK, (b + 1) * BLOCK)
            keys = lambda b: slice(b * BLOCK, (b + 2) * BLOCK)
            bands = [_dil_band(i == 0 if b == 0 else False) for b in range(nblk)]
            qms, doms = [], []
            for b, h in chains:
                q, do = q_ref[rows(b), :], do_ref[rows(b), :]
                qms.append(jnp.where(head_id == h, q, jnp.zeros_like(q)))
                doms.append(jnp.where(head_id == h, do, jnp.zeros_like(do)))
            scores = [_dot_nt(qm, kk_all[keys(b)]) for (b, h), qm in zip(chains, qms)]
            dps = [_dot_nt(dom, vv_all[keys(b)]) for (b, h), dom in zip(chains, doms)]
            pbs, dss = [], []
            for n, (b, h) in enumerate(chains):
                steps, valid = bands[b]
                first = lane == h * HEAD_DIM
                lse = jnp.sum(jnp.where(first, lse_ref[rows(b), :], 0.0), axis=1, keepdims=True)
                dt = jnp.sum(jnp.where(first, dt_ref[rows(b), :], 0.0), axis=1, keepdims=True)
                logits = jnp.where(valid, scores[n] * scale - slopes[h] * steps, NEG_INF)
                p = jnp.where(valid, jnp.exp(logits - lse), 0.0)
                pbs.append(p.astype(BF16))
                dss.append((p * (dps[n] + dt) * scale).astype(BF16))
            dqs = [_dot(ds, kk_all[keys(b)]) for (b, h), ds in zip(chains, dss)]
            dks = [_dot_tn(ds, qm) for ds, qm in zip(dss, qms)]
            dvs = [_dot_tn(pb, dom) for pb, dom in zip(pbs, doms)]
            dkk, dvv = [], []
            for b in range(nblk):
                mine = [n for n, ch in enumerate(chains) if ch[0] == b]
                dq = dqs[mine[0]]
                for n in mine[1:]:
                    dq = jnp.where(head_id == chains[n][1], dqs[n], dq)
                dq_ref[rows(b), :] = dq.astype(dq_ref.dtype)
                dkk.append((dks[mine[0]] + dks[mine[1]]) + (dks[mine[2]] + dks[mine[3]]))
                dvv.append((dvs[mine[0]] + dvs[mine[1]]) + (dvs[mine[2]] + dvs[mine[3]]))
            for out_ref, carry_ref, parts in ((dk_ref, ck_ref, dkk), (dv_ref, cv_ref, dvv)):
                if nblk > 1:
                    out_ref[: (nblk - 1) * BLOCK, :] = carry_ref[: (nblk - 1) * BLOCK, :].astype(out_ref.dtype)
                out_ref[tail, :] = (carry_ref[tail, :] + parts[0][:BLOCK]).astype(out_ref.dtype)
                for b in range(nblk):
                    own = parts[b][BLOCK:]
                    carry_ref[rows(b), :] = own + parts[b + 1][:BLOCK] if b + 1 < nblk else own

        @pl.when(i == n_steps)
        def _():
            dk_ref[...] = ck_ref[...].astype(dk_ref.dtype)
            dv_ref[...] = cv_ref[...].astype(dv_ref.dtype)

    clamp = lambda i: jnp.minimum(i, n_steps - 1)
    qkv_v, ncb, cols = _dil_view(qkv, group)
    view = lambda t: t.reshape(sub, dilation * DIL_OUT_WIDTH)
    row_spec = pl.BlockSpec((nblk * BLOCK, DIL_OUT_WIDTH), lambda r, i: (clamp(i), r))
    late_spec = pl.BlockSpec((nblk * BLOCK, DIL_OUT_WIDTH), lambda r, i: (jnp.maximum(i - 1, 0), r))
    res = _call(
        body, (qkv_v, qkv_v, qkv_v, qkv_v, qkv_v, view(do_g), view(lse_g), view(dterm_g)), rider,
        name=f"dil_bwd_g{group}",
        grid=(dilation, n_steps + 1),
        in_specs=_dil_step_specs(ncb, cols, nblk, clamp) + [row_spec, row_spec, row_spec],
        out_specs=[row_spec, late_spec, late_spec],
        out_shape=[jax.ShapeDtypeStruct((sub, dilation * DIL_OUT_WIDTH), BF16)] * 3,
        scratch_shapes=[pltpu.VMEM((nblk * BLOCK, DIL_OUT_WIDTH), F32)] * 2,
        compiler_params=_cparams(2),
    )
    grads, lands = res if rider is not None else (res, None)
    grads = tuple(g.reshape(s, DIL_OUT_WIDTH) for g in grads)
    return grads if rider is None else (grads, lands)


def _dil_view(qkv, group):
    _, dilation = DIL_GROUPS[group]
    if dilation == 1:
        return qkv, QKV_COLS // DIL_OUT_WIDTH, (group, 3 + group, 6 + group)
    w = DIL_OUT_WIDTH
    own = jnp.concatenate([qkv[:, (3 * part + group) * w:(3 * part + group + 1) * w] for part in range(3)], axis=1)
    return own.reshape(qkv.shape[0] // dilation, dilation * 3 * w), 3, (0, 1, 2)


def _head_block_ones():
    r = lax.broadcasted_iota(jnp.int32, (DIL_OUT_WIDTH, DIL_OUT_WIDTH), 0) // HEAD_DIM
    c = lax.broadcasted_iota(jnp.int32, (DIL_OUT_WIDTH, DIL_OUT_WIDTH), 1) // HEAD_DIM
    return jnp.where(r == c, 1.0, 0.0).astype(BF16)


def _dil_mix_weights(l0, l1, l2):
    mx = jnp.maximum(jnp.maximum(l0, l1), l2)
    e0, e1, e2 = jnp.exp(l0 - mx), jnp.exp(l1 - mx), jnp.exp(l2 - mx)
    inv = 1.0 / (e0 + e1 + e2)
    return e0 * inv, e1 * inv, e2 * inv


def _dil_mix_fwd(os_, lses, tm):
    def epi(_, rows, consts):
        o0, o1, o2, l0, l1, l2 = rows
        w0, w1, w2 = _dil_mix_weights(l0, l1, l2)
        return [w0 * o0 + w1 * o1 + w2 * o2], []

    (o_a,) = _rowk("dil_mix_fwd", tm=tm, rows=list(os_) + list(lses), row_outs=[(DIL_OUT_WIDTH, BF16)], epilogue=epi)
    return o_a


def _dil_mix_bwd(do_a, os_, lses, tm):
    def epi(_, rows, consts):
        do, o0, o1, o2, l0, l1, l2 = rows
        do = do.astype(F32)
        w0, w1, w2 = _dil_mix_weights(l0, l1, l2)
        mixed = w0 * o0 + w1 * o1 + w2 * o2
        tot = _dot_hi_lo(do * mixed, _head_block_ones())
        return [w0 * do, w1 * do, w2 * do, -w0 * tot, -w1 * tot, -w2 * tot], []

    return _rowk(
        "dil_mix_bwd", tm=tm, rows=[do_a] + list(os_) + list(lses),
        row_outs=[(DIL_OUT_WIDTH, BF16)] * 3 + [(DIL_OUT_WIDTH, F32)] * 3, epilogue=epi)


_SB_Q0 = 3 * DIL_WIDTH // LANES
_SB_K0 = _SB_Q0 + SB_WIDTH // LANES
_SB_V0 = _SB_K0 + SB_WIDTH // LANES


_EXP_CLAMP = 88.0
_SB_DEAD = 104.0


def _tri(t, op):
    r = lax.broadcasted_iota(jnp.int32, (t, t), 0)
    c = lax.broadcasted_iota(jnp.int32, (t, t), 1)
    return jnp.where(op(r, c), 1.0, 0.0).astype(BF16)


def _softplus(z):
    return jnp.maximum(z, jnp.log(1.0 + jnp.exp(jnp.minimum(z, _EXP_CLAMP))))


def _sb_chain_head(qm, kj, mask):
    z = _dot_nt(qm, kj)
    sp = _softplus(z)
    return (sp if mask is None else jnp.where(mask, sp, 0.0)), z - sp


def _sb_fwd(qkv, rider=None):
    s = qkv.shape[0]
    t = SB_TK
    assert s % (2 * t) == 0
    nq = s // (2 * t)
    n_pairs = SB_WIDTH // LANES

    def body(q_ref, k_ref, v_ref, o_ref, tot_ref, steps_ref):
        p, i = pl.program_id(0), pl.program_id(1)
        lane_hi = lax.broadcasted_iota(jnp.int32, (1, LANES), 1) // HEAD_DIM
        later = _tri(t, lambda r, c: r > c)
        causal = lax.broadcasted_iota(jnp.int32, (t, t), 1) < lax.broadcasted_iota(jnp.int32, (t, t), 0)
        qms = []
        for x in range(2):
            q = q_ref[pl.ds(x * t, t), :] * (1.0 / math.sqrt(HEAD_DIM))
            qms.append([jnp.where(lane_hi == hh, q, jnp.zeros_like(q)) for hh in range(2)])

        def tile(j):
            off = pl.multiple_of(j * t, t)
            return k_ref[pl.ds(off, t), :], v_ref[pl.ds(off, t), :]

        def step(tiles, carry, diag):
            chains = [(x, hh) for x in range(2) if tiles[x] is not None for hh in range(2)]
            kv = {x: tile(tiles[x]) for x in range(2) if tiles[x] is not None}
            heads = [_sb_chain_head(qms[x][hh], kv[x][0], causal if diag else None) for x, hh in chains]
            sufs = [_dot(sp.astype(BF16), later) for sp, _ in heads]
            new = [list(carry[0]), list(carry[1])]
            for (x, hh), (sp, lpos), suf in zip(chains, heads, sufs):
                c, acc = carry[x][hh]
                a = jnp.exp(lpos - suf - c)
                if diag:
                    a = jnp.where(causal, a, 0.0)
                new[x][hh] = (c + jnp.sum(sp, axis=1, keepdims=True), acc + _dot(a.astype(BF16), kv[x][1]))
            return (tuple(new[0]), tuple(new[1]))

        def lowest(carry):
            m = [jnp.min(carry[x][hh][0]) for x in range(2) for hh in range(2)]
            return jnp.minimum(jnp.minimum(m[0], m[1]), jnp.minimum(m[2], m[3]))

        zero = (jnp.zeros((t, 1), F32), jnp.zeros((t, LANES), F32))
        carry = step((2 * i, 2 * i + 1), ((zero, zero), (zero, zero)), True)

        n_full, carry = lax.while_loop(
            lambda st: jnp.logical_and(st[0] < 2 * i, lowest(st[1]) <= _SB_DEAD),
            lambda st: (st[0] + 1, step((2 * i - 1 - st[0], 2 * i - st[0]), st[1], False)),
            (jnp.int32(0), carry))
        b_last = jnp.logical_and(n_full == 2 * i, lowest(carry) <= _SB_DEAD)
        carry = lax.cond(b_last, lambda ca: step((None, 0), ca, False), lambda ca: ca, carry)
        for x in range(2):
            (c0, acc0), (c1, acc1) = carry[x]
            o_ref[pl.ds(x * t, t), :] = jnp.where(lane_hi == 0, acc0, acc1).astype(o_ref.dtype)
            tot_ref[pl.ds(x * t, t), :] = jnp.where(lane_hi == 0, c0, c1)
        steps_ref[p, i] = n_full + b_last.astype(jnp.int32)

    return _call(
        body, (qkv, qkv, qkv), rider,
        name="sb_fwd",
        grid=(n_pairs, nq),
        in_specs=[
            pl.BlockSpec((2 * t, LANES), lambda p, i: (i, _SB_Q0 + p)),
            pl.BlockSpec((s, LANES), lambda p, i: (0, _SB_K0 + p)),
            pl.BlockSpec((s, LANES), lambda p, i: (0, _SB_V0 + p)),
        ],
        out_specs=[pl.BlockSpec((2 * t, LANES), lambda p, i: (i, p))] * 2 + [pl.BlockSpec(memory_space=pltpu.SMEM)],
        out_shape=[jax.ShapeDtypeStruct((s, SB_WIDTH), BF16), jax.ShapeDtypeStruct((s, SB_WIDTH), F32),
                   jax.ShapeDtypeStruct((n_pairs, nq), jnp.int32)],
        compiler_params=_cparams(2),
    )


def _sb_bwd(qkv, do_b, tot_b, n_steps):
    s = qkv.shape[0]
    t = SB_TK
    nq = s // (2 * t)
    n_pairs = SB_WIDTH // LANES
    scale = 1.0 / math.sqrt(HEAD_DIM)

    def body(steps_ref, q_ref, k_ref, v_ref, do_ref, tot_ref, dq_ref, dk_ref, dv_ref):
        p, i = pl.program_id(0), pl.program_id(1)

        @pl.when(i == 0)
        def _():
            dk_ref[...] = jnp.zeros_like(dk_ref)
            dv_ref[...] = jnp.zeros_like(dv_ref)

        lane = lax.broadcasted_iota(jnp.int32, (1, LANES), 1)
        lane_hi = lane // HEAD_DIM
        later = _tri(t, lambda r, c: r > c)
        before = _tri(t, lambda r, c: r < c)
        causal = lax.broadcasted_iota(jnp.int32, (t, t), 1) < lax.broadcasted_iota(jnp.int32, (t, t), 0)
        qms, doms, tots = [], [], []
        for x in range(2):
            rows = pl.ds(x * t, t)
            q, do, tot_all = q_ref[rows, :] * scale, do_ref[rows, :], tot_ref[rows, :]
            qms.append([jnp.where(lane_hi == hh, q, jnp.zeros_like(q)) for hh in range(2)])
            doms.append([jnp.where(lane_hi == hh, do, jnp.zeros_like(do)) for hh in range(2)])
            tots.append([jnp.sum(jnp.where(lane == hh * HEAD_DIM, tot_all, 0.0), axis=1, keepdims=True)
                         for hh in range(2)])

        def step(tiles, carry, diag):
            chains = [(x, hh) for x in range(2) if tiles[x] is not None for hh in range(2)]
            offs = {x: pl.multiple_of(tiles[x] * t, t) for x in range(2) if tiles[x] is not None}
            ks = {x: k_ref[pl.ds(off, t), :] for x, off in offs.items()}
            vs = {x: v_ref[pl.ds(off, t), :] for x, off in offs.items()}
            heads = [_sb_chain_head(qms[x][hh], ks[x], causal if diag else None) for x, hh in chains]
            sufs = [_dot(sp.astype(BF16), later) for sp, _ in heads]
            das = [_dot_nt(doms[x][hh], vs[x]) for x, hh in chains]
            new = [list(carry[0]), list(carry[1])]
            sigs, gs, abs_ = [], [], []
            for (x, hh), (sp, lpos), suf, da in zip(chains, heads, sufs, das):
                cl = carry[x][hh][0] + jnp.sum(sp, axis=1, keepdims=True)
                sig = jnp.exp(lpos)
                a = sig * jnp.exp(-suf - (tots[x][hh] - cl))
                if diag:
                    a = jnp.where(causal, a, 0.0)
                g = a * da
                sigs.append(sig)
                gs.append(g)
                abs_.append(a.astype(BF16))
                new[x][hh] = (cl, carry[x][hh][1] + jnp.sum(g, axis=1, keepdims=True), carry[x][hh][2])
            prefs = [_dot(g.astype(BF16), before) for g in gs]
            dvs = [_dot_tn(ab, doms[x][hh]) for (x, hh), ab in zip(chains, abs_)]
            dzs = []
            for (x, hh), sig, g, pref in zip(chains, sigs, gs, prefs):
                dz = g - sig * (g + pref + carry[x][hh][1])
                if diag:
                    dz = jnp.where(causal, dz, 0.0)
                dzs.append(dz.astype(BF16))
            dqs = [_dot(dz, ks[x]) for (x, hh), dz in zip(chains, dzs)]
            dks = [_dot_tn(dz, qms[x][hh]) for (x, hh), dz in zip(chains, dzs)]
            for n, (x, hh) in enumerate(chains):
                cl, cg, dq = new[x][hh]
                new[x][hh] = (cl, cg, dq + dqs[n])
            for x in offs:
                mine = [n for n, ch in enumerate(chains) if ch[0] == x]
                dk_ref[pl.ds(offs[x], t), :] += dks[mine[0]] + dks[mine[1]]
                dv_ref[pl.ds(offs[x], t), :] += dvs[mine[0]] + dvs[mine[1]]
            return (tuple(new[0]), tuple(new[1]))

        taken = steps_ref[p, i]
        n_full = jnp.minimum(taken, 2 * i)
        zero = (jnp.zeros((t, 1), F32), jnp.zeros((t, 1), F32), jnp.zeros((t, LANES), F32))
        carry = ((zero, zero), (zero, zero))
        carry = lax.cond(taken > 2 * i, lambda ca: step((None, 0), ca, False), lambda ca: ca, carry)
        carry = lax.fori_loop(
            0, n_full, lambda n, ca: step((2 * i - n_full + n, 2 * i + 1 - n_full + n), ca, False), carry)
        carry = step((2 * i, 2 * i + 1), carry, True)
        for x in range(2):
            dq = jnp.where(lane_hi == 0, carry[x][0][2], carry[x][1][2])
            dq_ref[pl.ds(x * t, t), :] = (dq * scale).astype(dq_ref.dtype)

    row_spec = pl.BlockSpec((2 * t, LANES), lambda p, i, ns: (i, p))
    full_spec = pl.BlockSpec((s, LANES), lambda p, i, ns: (0, p))
    return _pcall(
        body,
        name="sb_bwd",
        grid_spec=pltpu.PrefetchScalarGridSpec(
            num_scalar_prefetch=1,
            grid=(n_pairs, nq),
            in_specs=[
                pl.BlockSpec((2 * t, LANES), lambda p, i, ns: (i, _SB_Q0 + p)),
                pl.BlockSpec((s, LANES), lambda p, i, ns: (0, _SB_K0 + p)),
                pl.BlockSpec((s, LANES), lambda p, i, ns: (0, _SB_V0 + p)),
                row_spec, row_spec,
            ],
            out_specs=[row_spec, full_spec, full_spec],
        ),
        out_shape=[jax.ShapeDtypeStruct((s, SB_WIDTH), BF16), jax.ShapeDtypeStruct((s, SB_WIDTH), F32),
                   jax.ShapeDtypeStruct((s, SB_WIDTH), F32)],
        compiler_params=_cparams(2),
    )(n_steps, qkv, qkv, qkv, do_b, tot_b)


def _gates(gl, bg):
    return _sigmoid(gl[:, :D_MODEL] + bg[:, :D_MODEL]), _sigmoid(gl[:, D_MODEL:] + bg[:, D_MODEL:])


def _mixer_fwd(o_a, o_b, gl, x0, bg, g2, w_ud, w_us, w_out, tm):
    def epi(_, rows, consts):
        oa, ob, glv, x = rows
        bgv, g2v, wud, wus, wout = consts
        ga, gb = _gates(glv, bgv)
        merged = ga * _dot(oa, wud) + gb * _dot(ob, wus)
        x1 = x + _dot(merged.astype(BF16), wout)
        r, xh = _rms_stats(x1)
        return [x1, xh * g2v], []

    return _rowk("mixer_fwd", tm=tm, rows=[o_a, o_b, gl, x0], consts=[bg, g2, w_ud, w_us, w_out],
                 row_outs=[(D_MODEL, F32), (D_MODEL, BF16)], epilogue=epi)


def _mixer_bwd(dx1, o_a, o_b, gl, bg, w_ud, w_us, w_out, tm, rider=None):
    s = dx1.shape[0]
    nm = s // tm

    def body(dx_ref, oa_ref, ob_ref, gl_ref, bg_ref, wud_ref, wus_ref, wout_ref,
             doa_ref, dob_ref, dgl_ref, gwout_ref, gwud_ref, gwus_ref, gbg_ref):
        i = pl.program_id(0)
        dxb = dx_ref[...].astype(BF16)
        oa, ob = oa_ref[...], ob_ref[...]
        ga, gb = _gates(gl_ref[...], bg_ref[...])
        ua, ub = _dot(oa, wud_ref[...]), _dot(ob, wus_ref[...])
        merged = (ga * ua + gb * ub).astype(BF16)
        dm = _dot_nt(dxb, wout_ref[...])
        dua = (dm * ga).astype(BF16)
        dub = (dm * gb).astype(BF16)
        dgla = dm * ua * ga * (1.0 - ga)
        dglb = dm * ub * gb * (1.0 - gb)
        doa_ref[...] = _dot_nt(dua, wud_ref[...]).astype(doa_ref.dtype)
        dob_ref[...] = _dot_nt(dub, wus_ref[...]).astype(dob_ref.dtype)
        dgl_ref[:, :D_MODEL] = dgla.astype(dgl_ref.dtype)
        dgl_ref[:, D_MODEL:] = dglb.astype(dgl_ref.dtype)
        parts = [(gwout_ref, _dot_tn(merged, dxb)), (gwud_ref, _dot_tn(oa, dua)), (gwus_ref, _dot_tn(ob, dub))]
        for r, v in parts:

            @pl.when(i == 0)
            def _(r=r, v=v):
                r[...] = v

            @pl.when(i > 0)
            def _(r=r, v=v):
                r[...] += v

        sa = jnp.sum(dgla, axis=0, keepdims=True)
        sb = jnp.sum(dglb, axis=0, keepdims=True)

        @pl.when(i == 0)
        def _():
            gbg_ref[:, :D_MODEL] = sa
            gbg_ref[:, D_MODEL:] = sb

        @pl.when(i > 0)
        def _():
            gbg_ref[:, :D_MODEL] += sa
            gbg_ref[:, D_MODEL:] += sb

    row = lambda w: pl.BlockSpec((tm, w), lambda i: (i, 0))
    full = lambda a: pl.BlockSpec(a.shape, lambda i: (0, 0), pipeline_mode=pl.Buffered(1))
    fshape = lambda r, c: jax.ShapeDtypeStruct((r, c), F32)
    return _call(
        body, (dx1, o_a, o_b, gl, bg, w_ud, w_us, w_out), rider,
        name="mixer_bwd",
        grid=(nm,),
        in_specs=[row(D_MODEL), row(DIL_OUT_WIDTH), row(SB_WIDTH), row(2 * D_MODEL),
                  full(bg), full(w_ud), full(w_us), full(w_out)],
        out_specs=[row(DIL_OUT_WIDTH), row(SB_WIDTH), row(2 * D_MODEL),
                   pl.BlockSpec((D_MODEL, D_MODEL), lambda i: (0, 0)),
                   pl.BlockSpec((DIL_OUT_WIDTH, D_MODEL), lambda i: (0, 0)),
                   pl.BlockSpec((SB_WIDTH, D_MODEL), lambda i: (0, 0)),
                   pl.BlockSpec((1, 2 * D_MODEL), lambda i: (0, 0))],
        out_shape=[jax.ShapeDtypeStruct((s, DIL_OUT_WIDTH), BF16), jax.ShapeDtypeStruct((s, SB_WIDTH), BF16),
                   jax.ShapeDtypeStruct((s, 2 * D_MODEL), BF16),
                   fshape(D_MODEL, D_MODEL), fshape(DIL_OUT_WIDTH, D_MODEL), fshape(SB_WIDTH, D_MODEL),
                   fshape(1, 2 * D_MODEL)],
        compiler_params=_cparams(1),
    )


def _all_gather(shards):
    n = len(shards)

    def body(*refs):
        x_refs, out_refs = refs[:n], refs[n:2 * n]
        send_sems, recv_sems, local_sems = refs[2 * n:]
        x, y, c = lax.axis_index("x"), lax.axis_index("y"), lax.axis_index("c")
        me, sibling = (x, y, c), (x, y, 1 - c)
        chips = [(1 - x, y), (x, 1 - y), (1 - x, 1 - y)]

        def slot(a, px, py, pc):
            return out_refs[a].at[4 * px + 2 * py + pc]

        def copy(a, k, block, to, own=False):
            return pltpu.make_async_remote_copy(
                src_ref=x_refs[a] if own else slot(a, *block), dst_ref=slot(a, *block),
                send_sem=send_sems.at[7 * a + k], recv_sem=recv_sems.at[7 * a + k], device_id=to, device_id_type=_MESH)

        mine = [pltpu.make_async_copy(x_refs[a], slot(a, *me), local_sems.at[a]) for a in range(n)]
        for cp in mine:
            cp.start()
        first = []
        for a in range(n):
            first.append(copy(a, 0, me, sibling, own=True))
            first += [copy(a, 1 + j, me, (*chip, c), own=True) for j, chip in enumerate(chips)]
        for cp in first:
            cp.start()
        passed = []
        for a in range(n):
            for j, chip in enumerate(chips):
                copy(a, 1 + j, (*chip, c), me).wait_recv()
                passed.append(copy(a, 4 + j, (*chip, c), sibling))
                passed[-1].start()
        for a in range(n):
            copy(a, 0, sibling, me).wait_recv()
            for j, chip in enumerate(chips):
                copy(a, 4 + j, (*chip, 1 - c), me).wait_recv()
        for cp in first + passed:
            cp.wait_send()
        for cp in mine:
            cp.wait()

    return _pcall(
        body,
        name="all_gather_weights",
        in_specs=[_HBM] * n,
        out_specs=[_HBM] * n,
        out_shape=[jax.ShapeDtypeStruct((N_DEV,) + s.shape, s.dtype) for s in shards],
        scratch_shapes=[pltpu.SemaphoreType.DMA((7 * n,)), pltpu.SemaphoreType.DMA((7 * n,)),
                        pltpu.SemaphoreType.DMA((n,))],
    )(*shards)


def _exchange(chunks):
    n = len(chunks)

    def body(*refs):
        g_refs, o_refs = refs[:n], refs[n:2 * n]
        send_sems, recv_sems, local_sems = refs[2 * n:]
        x, y, c = lax.axis_index("x"), lax.axis_index("y"), lax.axis_index("c")
        me = 4 * x + 2 * y + c
        own = [pltpu.make_async_copy(g_refs[a].at[me], o_refs[a].at[me], local_sems.at[a]) for a in range(n)]
        for cp in own:
            cp.start()
        copies = []
        for a in range(n):
            for k in range(1, N_DEV):
                px, py, pc = x ^ (k >> 2), y ^ ((k >> 1) & 1), c ^ (k & 1)
                peer = 4 * px + 2 * py + pc
                copies.append(pltpu.make_async_remote_copy(
                    src_ref=g_refs[a].at[peer], dst_ref=o_refs[a].at[me], send_sem=send_sems.at[7 * a + k - 1],
                    recv_sem=recv_sems.at[7 * a + k - 1], device_id=(px, py, pc), device_id_type=_MESH))
        for cp in copies:
            cp.start()
        for cp in copies:
            cp.wait()
        for cp in own:
            cp.wait()

    return _pcall(
        body,
        name="exchange_grads",
        in_specs=[_HBM] * n,
        out_specs=[_HBM] * n,
        out_shape=[jax.ShapeDtypeStruct(g.shape, g.dtype) for g in chunks],
        scratch_shapes=[pltpu.SemaphoreType.DMA((7 * n,)), pltpu.SemaphoreType.DMA((7 * n,)),
                        pltpu.SemaphoreType.DMA((n,))],
    )(*chunks)


def _reduce_adamw(name, parts, w, m, v, tr):
    _, rows, cols = parts.shape
    tr = min(tr, rows)
    assert rows % tr == 0
    c1 = 1.0 / (1.0 - ADAM_B1 ** ADAM_STEP)
    c2 = 1.0 / (1.0 - ADAM_B2 ** ADAM_STEP)

    def body(p_ref, w_ref, m_ref, v_ref, g_out, d_out, m_out, v_out):
        g = p_ref[0].astype(F32)
        for d in range(1, N_DEV):
            g = g + p_ref[d].astype(F32)
        mn = ADAM_B1 * m_ref[...] + (1.0 - ADAM_B1) * g
        vn = ADAM_B2 * v_ref[...] + (1.0 - ADAM_B2) * (g * g)
        g_out[...] = g
        m_out[...] = mn
        v_out[...] = vn
        d_out[...] = -ADAM_LR * ((mn * c1) / (jnp.sqrt(vn * c2) + ADAM_EPS) + ADAM_WD * w_ref[...])

    spec = pl.BlockSpec((tr, cols), lambda i: (i, 0))
    return _pcall(
        body,
        name=name,
        grid=(rows // tr,),
        in_specs=[pl.BlockSpec((N_DEV, tr, cols), lambda i: (0, i, 0)), spec, spec, spec],
        out_specs=[spec] * 4,
        out_shape=[jax.ShapeDtypeStruct((rows, cols), F32)] * 4,
        compiler_params=_cparams(1),
    )(parts, w, m, v)


_SHARDED = ("w_in", "w_up_dil", "w_up_sb", "w_out", "w_mlp_in", "w_mlp_out")
_FULL_SHAPES = {"w_in": (D_MODEL, IN_COLS), "w_up_dil": (DIL_OUT_WIDTH, D_MODEL), "w_up_sb": (SB_WIDTH, D_MODEL),
                "w_out": (D_MODEL, D_MODEL), "w_mlp_in": (D_MODEL, D_FF), "w_mlp_out": (D_FF, D_MODEL)}
_ROW_SHARDED = ("w_out", "w_mlp_out")


def _shard_shape(name):
    r, c = _FULL_SHAPES[name]
    return (r // N_DEV, c) if name in _ROW_SHARDED else (r, c // N_DEV)


def _assemble(name, gathered):
    r, c = _shard_shape(name)
    if name in _ROW_SHARDED:
        return gathered.reshape(N_DEV * r, c)
    return gathered.transpose(1, 0, 2).reshape(r, N_DEV * c)


def _chunk(name, full):
    r, c = _shard_shape(name)
    if name in _ROW_SHARDED:
        return full.reshape(N_DEV, r, c)
    return full.reshape(r, N_DEV, c).transpose(1, 0, 2)


_SMALL = (("norm_mix_g", D_MODEL), ("b_gate", 2 * D_MODEL), ("norm_mlp_g", D_MODEL), ("norm_final_g", D_MODEL))
_SMALL_N = sum(n for _, n in _SMALL) + LANES


def _pack_small(vals, tail):
    return jnp.concatenate([vals[n].reshape(1, -1) for n, _ in _SMALL] + [tail], axis=1)


def _unpack_small(vec, shapes):
    out, pos = {}, 0
    for n, width in _SMALL:
        out[n] = vec[:, pos:pos + width].reshape(shapes[n])
        pos += width
    return out, vec[:, pos:]


def kernel(x, norm_mix_g, w_in, b_gate, w_up_dil, w_up_sb, w_out, norm_mlp_g, w_mlp_in, w_mlp_out, norm_final_g, loss_target, m_norm_mix_g, m_w_in, m_b_gate, m_w_up_dil, m_w_up_sb, m_w_out, m_norm_mlp_g, m_w_mlp_in, m_w_mlp_out, m_norm_final_g, v_norm_mix_g, v_w_in, v_b_gate, v_w_up_dil, v_w_up_sb, v_w_out, v_norm_mlp_g, v_w_mlp_in, v_w_mlp_out, v_norm_final_g):
    given = dict(locals())
    s = x.shape[1]
    x0 = x.reshape(s, D_MODEL)
    target = loss_target.reshape(s, D_MODEL)
    g1 = norm_mix_g.reshape(1, D_MODEL)
    g2 = norm_mlp_g.reshape(1, D_MODEL)
    g3 = norm_final_g.reshape(1, D_MODEL)
    bg = b_gate.reshape(1, 2 * D_MODEL)
    w_shards = {n: given[n].reshape(_shard_shape(n)) for n in _SHARDED}
    m_shards = {n: given["m_" + n].reshape(_shard_shape(n)) for n in _SHARDED}
    v_shards = {n: given["v_" + n].reshape(_shard_shape(n)) for n in _SHARDED}

    shard_b = {n: w_shards[n].astype(BF16) for n in _SHARDED}
    (gathered_w_in,) = _all_gather([shard_b["w_in"]])
    w_in_f = _assemble("w_in", gathered_w_in)
    w_qkv, w_gl = w_in_f[:, :QKV_COLS], w_in_f[:, QKV_COLS:]
    full = {}

    def norm1(_, rows, consts):
        _, xh = _rms_stats(rows[0])
        return [xh * consts[0]], []

    (h1,) = _rowk("norm_mix", tm=512, rows=[x0], consts=[g1], row_outs=[(D_MODEL, BF16)], epilogue=norm1)
    qkv, (land,) = _mm("proj_qkv", h1, w_qkv, out_dtype=BF16, tm=1024, tn=768, tk=D_MODEL,
                       rider=_Spread([shard_b["w_mlp_in"]], chunked=False))
    full["w_mlp_in"] = _assemble("w_mlp_in", land)
    gl = _mm("proj_gates", h1, w_gl, out_dtype=BF16, tm=1024, tn=1024, tk=D_MODEL)
    dil = [_dil_fwd(qkv, g) for g in range(len(DIL_GROUPS))]
    os_, lses = [d[0] for d in dil], [d[1] for d in dil]
    o_a = _dil_mix_fwd(os_, lses, 512)
    riding = ("w_mlp_out", "w_out", "w_up_sb", "w_up_dil")
    (o_b, tot_b, sb_steps), lands = _sb_fwd(qkv, rider=_Spread([shard_b[n] for n in riding], chunked=False))
    full.update({n: _assemble(n, land) for n, land in zip(riding, lands)})
    x1, h2 = _mixer_fwd(o_a, o_b, gl, x0, bg, g2, full["w_up_dil"], full["w_up_sb"], full["w_out"], 512)
    f = _mm("mlp_in", h2, full["w_mlp_in"], out_dtype=BF16, tm=1024, tn=1024, tk=D_MODEL,
            epilogue=lambda r, _: jnp.square(jnp.maximum(r, 0.0)))

    def head(acc, rows, consts):
        x1v, tv = rows
        g3v = consts[0]
        x2 = x1v + acc
        r, xh = _rms_stats(x2)
        diff = xh * g3v - tv
        loss = (0.5 / D_MODEL) * jnp.sum(jnp.sum(diff * diff, axis=0, keepdims=True), axis=1, keepdims=True)
        dy = diff * (1.0 / D_MODEL)
        dx2, dg = _rms_bwd(dy, xh, r, g3v)
        return [dx2, dx2], [dg, jnp.broadcast_to(loss, (1, LANES))]

    dx2, dx2b, gg3, loss_part = _rowk(
        "mlp_out_loss", a=f, w=full["w_mlp_out"], tm=512, tk=D_FF, rows=[x1, target], consts=[g3],
        row_outs=[(D_MODEL, F32), (D_MODEL, BF16)], acc_outs=[D_MODEL, LANES], epilogue=head)

    da = _mm("mlp_out_bwd", dx2b, full["w_mlp_out"], tb=True, out_dtype=BF16, tm=1024, tn=1024, tk=D_MODEL, extra=f,
             epilogue=lambda r, fv: r * (2.0 * jnp.sqrt(fv.astype(F32))))
    g_w_mlp_out = _mm("grad_w_mlp_out", f, dx2b, ta=True, out_dtype=F32, tm=1024, tn=1024, tk=2048)
    g_w_mlp_in = _mm("grad_w_mlp_in", h2, da, ta=True, out_dtype=F32, tm=1024, tn=1024, tk=2048)

    def norm_bwd(acc, rows, consts):
        xv, dres = rows
        r, xh = _rms_stats(xv)
        dx, dg = _rms_bwd(acc, xh, r, consts[0])
        return [dres + dx], [dg]

    bchunk = lambda n, g: _chunk(n, g).astype(BF16)
    parts = {}
    (dx1, gg2), (parts["w_mlp_in"],) = _rowk(
        "mlp_in_bwd", a=da, w=full["w_mlp_in"], nt=True, tm=512, tk=D_FF, rows=[x1, dx2], consts=[g2],
        row_outs=[(D_MODEL, F32)], acc_outs=[D_MODEL], epilogue=norm_bwd,
        rider=_Spread([bchunk("w_mlp_in", g_w_mlp_in)], chunked=True))
    (do_a, do_b, dgl, g_w_out, g_w_ud, g_w_us, g_bg), (parts["w_mlp_out"],) = _mixer_bwd(
        dx1, o_a, o_b, gl, bg, full["w_up_dil"], full["w_up_sb"], full["w_out"], 512,
        rider=_Spread([bchunk("w_mlp_out", g_w_mlp_out)], chunked=True))
    mix = _dil_mix_bwd(do_a, os_, lses, 512)
    dil_b = [_dil_bwd(qkv, mix[g], dil[g][2], mix[3 + g], g) for g in range(2)]
    small_three = {"w_out": g_w_out, "w_up_sb": g_w_us, "w_up_dil": g_w_ud}
    grads, lands = _dil_bwd(qkv, mix[2], dil[2][2], mix[5], 2,
                            rider=_Spread([bchunk(n, g) for n, g in small_three.items()], chunked=True))
    dil_b.append(grads)
    parts.update(dict(zip(small_three, lands)))
    dq_b, dk_b, dv_b = _sb_bwd(qkv, do_b, tot_b, sb_steps)
    dproj = [d[0] for d in dil_b] + [d[1] for d in dil_b] + [d[2] for d in dil_b] + [dq_b, dk_b, dv_b, dgl]
    g_w_in = jnp.concatenate([
        _grad_cols("grad_w_in_dil", h1, dproj[:9], tm=D_MODEL, tk=1024),
        _grad_cols("grad_w_in_sb", h1, dproj[9:12], tm=D_MODEL, tk=1024),
        _grad_cols("grad_w_in_gates", h1, dproj[12:], tm=D_MODEL, tk=1024)], axis=1)
    (grad_x, gg1), (parts["w_in"],) = _rowk(
        "in_proj_bwd", a=dproj, w=w_in_f, nt=True, tm=512, tk=IN_COLS, rows=[x0, dx1], consts=[g1],
        row_outs=[(D_MODEL, F32)], acc_outs=[D_MODEL], epilogue=norm_bwd,
        rider=_Spread([bchunk("w_in", g_w_in)], chunked=True))

    small_part = _pack_small({"norm_mix_g": gg1, "b_gate": g_bg, "norm_mlp_g": gg2, "norm_final_g": gg3}, loss_part)
    (small_parts,) = _exchange([jnp.broadcast_to(small_part[None], (N_DEV, 1, _SMALL_N))])

    tags = ("grad_", "delta_", "new_m_", "new_v_")
    outs = {}
    for n, p in parts.items():
        res = _reduce_adamw("adamw_" + n, p, w_shards[n], m_shards[n], v_shards[n], 128)
        for tag, val in zip(tags, res):
            outs[tag + n] = val.reshape(given[n].shape)
    small_w = _pack_small(given, jnp.zeros((1, LANES), F32))
    small_m = _pack_small({n: given["m_" + n] for n, _ in _SMALL}, jnp.zeros((1, LANES), F32))
    small_v = _pack_small({n: given["v_" + n] for n, _ in _SMALL}, jnp.ones((1, LANES), F32))
    small_res = _reduce_adamw("adamw_replicated", small_parts, small_w, small_m, small_v, 8)

    small_shapes = {n: given[n].shape for n, _ in _SMALL}
    for tag, small in zip(tags, small_res):
        small_vals, tail = _unpack_small(small, small_shapes)
        for n, val in small_vals.items():
            outs[tag + n] = val
        if tag == "grad_":
            loss = tail[0, 0]
    names = ["norm_mix_g", "w_in", "b_gate", "w_up_dil", "w_up_sb", "w_out", "norm_mlp_g", "w_mlp_in", "w_mlp_out",
             "norm_final_g"]
    return (loss, grad_x.reshape(x.shape), *[outs["grad_" + n] for n in names], *[outs["delta_" + n] for n in names],
            *[outs["new_m_" + n] for n in names], *[outs["new_v_" + n] for n in names])
```

```python
import functools
import math

import jax
import jax.numpy as jnp
from jax import lax
from jax.experimental import pallas as pl
from jax.experimental.pallas import tpu as pltpu

_pcall = pl.pallas_call

F32 = jnp.float32
BF16 = jnp.bfloat16

D_MODEL = 1024
HEAD_DIM = 64
DIL_GROUPS = ((128, 1), (512, 4), (2048, 16))
DIL_HEADS_PER_GROUP = 4
N_DIL_HEADS = 12
N_SB_HEADS = 8
DIL_WIDTH = 768
DIL_OUT_WIDTH = 256
SB_WIDTH = 512
D_FF = 4096
BLOCK = 128
RMS_EPS = 1e-6
NEG_INF = -1e30
QKV_COLS = 3 * DIL_WIDTH + 3 * SB_WIDTH
IN_COLS = QKV_COLS + 2 * D_MODEL
N_DEV = 8

ADAM_LR = 0.001
ADAM_B1 = 0.9
ADAM_B2 = 0.999
ADAM_EPS = 1e-08
ADAM_WD = 0.01
ADAM_STEP = 10

VMEM_LIMIT = 56 * 1024 * 1024
SB_TK = 256
LANES = 128

_ARB = pltpu.ARBITRARY


def _cparams(n_axes, **kw):
    return pltpu.CompilerParams(dimension_semantics=(_ARB,) * n_axes, vmem_limit_bytes=VMEM_LIMIT, **kw)


def _dot(a, b):
    return jnp.dot(a, b, preferred_element_type=F32)


def _dot_nt(a, b):
    return lax.dot_general(a, b, (((1,), (1,)), ((), ())), preferred_element_type=F32)


def _dot_tn(a, b):
    return lax.dot_general(a, b, (((0,), (0,)), ((), ())), preferred_element_type=F32)


def _split_hi_lo(x):
    hi = x.astype(BF16)
    lo = (x - hi.astype(F32)).astype(BF16)
    return hi, lo


def _dot_hi_lo(x, m):
    hi, lo = _split_hi_lo(x)
    return _dot(hi, m) + _dot(lo, m)


def _sigmoid(x):
    return 1.0 / (1.0 + jnp.exp(-x))


_HBM = pl.BlockSpec(memory_space=pltpu.HBM)
_MESH = pl.DeviceIdType.MESH


class _Spread:
    def __init__(self, srcs, chunked):
        self.srcs, self.chunked, self.n = list(srcs), chunked, len(srcs)

    def land_shapes(self):
        return [jax.ShapeDtypeStruct((N_DEV,) + (s.shape[1:] if self.chunked else s.shape), s.dtype) for s in self.srcs]

    def scratch(self):
        dma = pltpu.SemaphoreType.DMA
        return [dma((7 * self.n,)), dma((7 * self.n,)), dma((self.n,))]

    def copies(self, src_refs, land_refs, send_sems, recv_sems, local_sems):
        x, y, c = lax.axis_index("x"), lax.axis_index("y"), lax.axis_index("c")
        me = 4 * x + 2 * y + c
        out = []
        for a, (src, land) in enumerate(zip(src_refs, land_refs)):
            out.append(pltpu.make_async_copy(src.at[me] if self.chunked else src, land.at[me], local_sems.at[a]))
            for k in range(1, N_DEV):
                px, py, pc = x ^ (k >> 2), y ^ ((k >> 1) & 1), c ^ (k & 1)
                out.append(pltpu.make_async_remote_copy(
                    src_ref=src.at[4 * px + 2 * py + pc] if self.chunked else src, dst_ref=land.at[me],
                    send_sem=send_sems.at[7 * a + k - 1], recv_sem=recv_sems.at[7 * a + k - 1],
                    device_id=(px, py, pc), device_id_type=_MESH))
        return out


def _call(body, args, rider=None, **kw):
    if rider is None:
        return _pcall(body, **kw)(*args)
    grid = kw["grid"]
    single = not isinstance(kw["out_shape"], (list, tuple))
    out_specs = [kw["out_specs"]] if single else list(kw["out_specs"])
    out_shape = [kw["out_shape"]] if single else list(kw["out_shape"])
    in_specs, scratch = list(kw["in_specs"]), list(kw.get("scratch_shapes", []))
    n_in, n_out, n_s, n = len(in_specs), len(out_shape), len(scratch), rider.n

    def hosted(*refs):
        ins, srcs = refs[:n_in], refs[n_in:n_in + n]
        outs, lands = refs[n_in + n:n_in + n + n_out], refs[n_in + n + n_out:n_in + 2 * n + n_out]
        own_scratch, sems = refs[n_in + 2 * n + n_out:n_in + 2 * n + n_out + n_s], refs[n_in + 2 * n + n_out + n_s:]
        ids = [pl.program_id(d) for d in range(len(grid))]
        first = functools.reduce(jnp.logical_and, [i == 0 for i in ids])
        last = functools.reduce(jnp.logical_and, [i == g - 1 for i, g in zip(ids, grid)])
        copies = rider.copies(srcs, lands, *sems)

        @pl.when(first)
        def _():
            for cp in copies:
                cp.start()

        body(*ins, *outs, *own_scratch)

        @pl.when(last)
        def _():
            for cp in copies:
                cp.wait()

    kw = dict(kw, in_specs=in_specs + [_HBM] * n, out_specs=out_specs + [_HBM] * n,
              out_shape=out_shape + rider.land_shapes(), scratch_shapes=scratch + rider.scratch())
    res = _pcall(hosted, **kw)(*args, *rider.srcs)
    return (res[0] if single else list(res[:n_out])), list(res[n_out:])


def _mm(name, a, b, *, ta=False, tb=False, out_dtype, tm, tn, tk, epilogue=None, extra=None, rider=None):
    m = a.shape[1] if ta else a.shape[0]
    k = a.shape[0] if ta else a.shape[1]
    n = b.shape[0] if tb else b.shape[1]
    assert (b.shape[1] if tb else b.shape[0]) == k
    tm, tn, tk = min(tm, m), min(tn, n), min(tk, k)
    assert m % tm == 0 and n % tn == 0 and k % tk == 0, (name, m, n, k, tm, tn, tk)
    nk = k // tk
    dn = (((0 if ta else 1,), (1 if tb else 0,)), ((), ()))
    in_place = nk > 1 and epilogue is None and out_dtype == F32

    def body(*refs):
        if extra is not None:
            a_ref, b_ref, e_ref, o_ref = refs[:4]
        else:
            a_ref, b_ref, o_ref = refs[:3]
            e_ref = None

        def finish(r):
            if epilogue is not None:
                r = epilogue(r, None if e_ref is None else e_ref[...])
            o_ref[...] = r.astype(out_dtype)

        part = lax.dot_general(a_ref[...].astype(BF16), b_ref[...].astype(BF16), dn, preferred_element_type=F32)
        if nk == 1:
            finish(part)
        else:
            acc_ref = o_ref if in_place else refs[-1]
            kk = pl.program_id(2)

            @pl.when(kk == 0)
            def _():
                acc_ref[...] = part

            @pl.when(kk > 0)
            def _():
                acc_ref[...] += part

            if not in_place:

                @pl.when(kk == nk - 1)
                def _():
                    finish(acc_ref[...])

    a_spec = pl.BlockSpec((tk, tm), lambda j, i, kk: (kk, i)) if ta else pl.BlockSpec((tm, tk), lambda j, i, kk: (i, kk))
    b_spec = pl.BlockSpec((tn, tk), lambda j, i, kk: (j, kk)) if tb else pl.BlockSpec((tk, tn), lambda j, i, kk: (kk, j))
    o_spec = pl.BlockSpec((tm, tn), lambda j, i, kk: (i, j))
    in_specs = [a_spec, b_spec]
    args = [a, b]
    if extra is not None:
        in_specs.append(o_spec)
        args.append(extra)
    return _call(
        body, args, rider,
        name=name,
        grid=(n // tn, m // tm, nk),
        in_specs=in_specs,
        out_specs=o_spec,
        out_shape=jax.ShapeDtypeStruct((m, n), out_dtype),
        scratch_shapes=[pltpu.VMEM((tm, tn), F32)] if (nk > 1 and not in_place) else [],
        compiler_params=_cparams(3),
    )


def _grad_cols(name, a, parts, *, tm, tk, rider=None):
    k, m = a.shape
    n = sum(p.shape[1] for p in parts)
    assert m % tm == 0 and k % tk == 0
    nk = k // tk

    def body(*refs):
        a_ref, p_refs, o_ref = refs[0], refs[1:1 + len(parts)], refs[1 + len(parts)]
        kk = pl.program_id(1)
        side_by_side = jnp.concatenate([p_ref[...].astype(BF16) for p_ref in p_refs], axis=1)
        term = _dot_tn(a_ref[...].astype(BF16), side_by_side)

        @pl.when(kk == 0)
        def _():
            o_ref[...] = term

        @pl.when(kk > 0)
        def _():
            o_ref[...] += term

    return _call(
        body, [a] + list(parts), rider,
        name=name,
        grid=(m // tm, nk),
        in_specs=[pl.BlockSpec((tk, tm), lambda i, kk: (kk, i))]
        + [pl.BlockSpec((tk, p.shape[1]), lambda i, kk: (kk, 0)) for p in parts],
        out_specs=pl.BlockSpec((tm, n), lambda i, kk: (i, 0)),
        out_shape=jax.ShapeDtypeStruct((m, n), F32),
        compiler_params=_cparams(2),
    )


def _rowk(name, *, a=None, w=None, nt=False, tm, tk=None, rows=(), consts=(), row_outs=(), acc_outs=(), epilogue,
          rider=None):
    has_mm = a is not None
    a_parts = list(a) if isinstance(a, (list, tuple)) else ([a] if has_mm else [])
    n_a = len(a_parts)
    m = a_parts[0].shape[0] if has_mm else rows[0].shape[0]
    assert m % tm == 0
    nm = m // tm
    if has_mm:
        k = sum(p.shape[1] for p in a_parts)
        n = w.shape[0] if nt else w.shape[1]
        tk = min(tk, k)
        assert k % tk == 0 and (n_a == 1 or tk == k)
        nk = k // tk
    else:
        nk = 1
    n_rows, n_consts, n_ro, n_ao = len(rows), len(consts), len(row_outs), len(acc_outs)

    def body(*refs):
        pos = 0
        if has_mm:
            a_refs, w_ref = refs[:n_a], refs[n_a]
            pos = n_a + 1
        row_refs = refs[pos:pos + n_rows]
        pos += n_rows
        const_refs = refs[pos:pos + n_consts]
        pos += n_consts
        ro_refs = refs[pos:pos + n_ro]
        pos += n_ro
        ao_refs = refs[pos:pos + n_ao]
        pos += n_ao
        i = pl.program_id(0)
        kk = pl.program_id(1)

        def finish(acc):
            ro_vals, ao_vals = epilogue(acc, [r[...] for r in row_refs], [c[...] for c in const_refs])
            for r, v in zip(ro_refs, ro_vals):
                r[...] = v.astype(r.dtype)
            for r, v in zip(ao_refs, ao_vals):

                @pl.when(i == 0)
                def _(r=r, v=v):
                    r[...] = v

                @pl.when(i > 0)
                def _(r=r, v=v):
                    r[...] += v

        if not has_mm:
            finish(None)
            return
        part, off = None, 0
        for a_ref in a_refs:
            width = a_ref.shape[1]
            cols = slice(None) if n_a == 1 else slice(off, off + width)
            av = a_ref[...].astype(BF16)
            term = _dot_nt(av, w_ref[:, cols]) if nt else _dot(av, w_ref[cols, :])
            part = term if part is None else part + term
            off += width
        if nk == 1:
            finish(part)
        else:
            acc_ref = refs[pos]

            @pl.when(kk == 0)
            def _():
                acc_ref[...] = part

            @pl.when(kk > 0)
            def _():
                acc_ref[...] += part

            @pl.when(kk == nk - 1)
            def _():
                finish(acc_ref[...])

    once = pl.Buffered(1)
    in_specs, args = [], []
    if has_mm:
        for part in a_parts:
            in_specs.append(pl.BlockSpec((tm, tk if n_a == 1 else part.shape[1]), lambda i, kk: (i, kk)))
        w_mode = once if nk == 1 else None
        in_specs.append(pl.BlockSpec((n, tk), lambda i, kk: (0, kk), pipeline_mode=w_mode) if nt
                        else pl.BlockSpec((tk, n), lambda i, kk: (kk, 0), pipeline_mode=w_mode))
        args += a_parts + [w]
    for r in rows:
        in_specs.append(pl.BlockSpec((tm, r.shape[1]), lambda i, kk: (i, 0)))
        args.append(r)
    for c in consts:
        in_specs.append(pl.BlockSpec(c.shape, lambda i, kk: (0,) * c.ndim, pipeline_mode=once))
        args.append(c)
    out_specs, out_shape = [], []
    for width, dt in row_outs:
        out_specs.append(pl.BlockSpec((tm, width), lambda i, kk: (i, 0)))
        out_shape.append(jax.ShapeDtypeStruct((m, width), dt))
    for width in acc_outs:
        out_specs.append(pl.BlockSpec((1, width), lambda i, kk: (0, 0)))
        out_shape.append(jax.ShapeDtypeStruct((1, width), F32))
    return _call(
        body, args, rider,
        name=name,
        grid=(nm, nk),
        in_specs=in_specs,
        out_specs=out_specs,
        out_shape=out_shape,
        scratch_shapes=[pltpu.VMEM((tm, n), F32)] if (has_mm and nk > 1) else [],
        compiler_params=_cparams(2),
    )


def _rms_stats(x):
    r = lax.rsqrt(jnp.mean(x * x, axis=-1, keepdims=True) + RMS_EPS)
    return r, x * r


def _rms_bwd(dh, xh, r, g):
    gy = dh * g
    dx = r * (gy - xh * jnp.mean(gy * xh, axis=-1, keepdims=True))
    return dx, jnp.sum(dh * xh, axis=0, keepdims=True)


def _alibi_slope(head):
    return 2.0 ** (-8.0 * (head + 1) / N_DIL_HEADS)


DIL_STEP_BLOCKS = 4


def _dil_band(first_block):
    qi = lax.broadcasted_iota(jnp.int32, (BLOCK, 2 * BLOCK), 0)
    kj = lax.broadcasted_iota(jnp.int32, (BLOCK, 2 * BLOCK), 1)
    steps = qi + BLOCK - kj
    valid = (steps >= 0) & (steps <= BLOCK)
    if first_block is not False:
        valid = valid & ((kj >= BLOCK) | jnp.logical_not(first_block))
    return steps.astype(F32), valid


def _dil_step_specs(ncb, cols, nblk, clamp):
    def own(col):
        return pl.BlockSpec((nblk * BLOCK, DIL_OUT_WIDTH), lambda r, i: (clamp(i), r * ncb + col))

    def before(col):
        return pl.BlockSpec((BLOCK, DIL_OUT_WIDTH), lambda r, i: (jnp.maximum(clamp(i) * nblk - 1, 0), r * ncb + col))

    return [own(cols[0]), own(cols[1]), before(cols[1]), own(cols[2]), before(cols[2])]


def _dil_fwd(qkv, group):
    window, dilation = DIL_GROUPS[group]
    s = qkv.shape[0]
    sub = s // dilation
    nb = sub // BLOCK
    assert nb * BLOCK * dilation == s and window // dilation == BLOCK
    nblk = min(DIL_STEP_BLOCKS, nb)
    assert nb % nblk == 0
    slopes = [_alibi_slope(group * DIL_HEADS_PER_GROUP + h) * dilation for h in range(DIL_HEADS_PER_GROUP)]

    def body(q_ref, kc_ref, kp_ref, vc_ref, vp_ref, o_ref, lse_ref):
        i = pl.program_id(1)
        kk_all = jnp.concatenate([kp_ref[...], kc_ref[...]], axis=0)
        vv_all = jnp.concatenate([vp_ref[...], vc_ref[...]], axis=0)
        head_id = lax.broadcasted_iota(jnp.int32, (1, DIL_OUT_WIDTH), 1) // HEAD_DIM
        chains = [(b, h) for b in range(nblk) for h in range(DIL_HEADS_PER_GROUP)]
        rows = lambda b: slice(b * BLOCK, (b + 1) * BLOCK)
        keys = lambda b: slice(b * BLOCK, (b + 2) * BLOCK)
        bands = [_dil_band(i == 0 if b == 0 else False) for b in range(nblk)]
        qs = [q_ref[rows(b), :] for b in range(nblk)]
        scores = [_dot_nt(jnp.where(head_id == h, qs[b], jnp.zeros_like(qs[b])), kk_all[keys(b)]) for b, h in chains]
        ps, lses = [], []
        for (b, h), sc in zip(chains, scores):
            steps, valid = bands[b]
            logits = jnp.where(valid, sc * (1.0 / math.sqrt(HEAD_DIM)) - slopes[h] * steps, NEG_INF)
            mx = jnp.max(logits, axis=1, keepdims=True)
            e = jnp.exp(logits - mx)
            den = jnp.sum(e, axis=1, keepdims=True)
            lses.append(mx + jnp.log(den))
            ps.append((e * (1.0 / den)).astype(BF16))
        outs = [_dot(p, vv_all[keys(b)]) for (b, h), p in zip(chains, ps)]
        for b in range(nblk):
            mine = [n for n, ch in enumerate(chains) if ch[0] == b]
            o, lse_all = outs[mine[0]], lses[mine[0]]
            for n in mine[1:]:
                o = jnp.where(head_id == chains[n][1], outs[n], o)
                lse_all = jnp.where(head_id == chains[n][1], lses[n], lse_all)
            o_ref[rows(b), :] = o
            lse_ref[rows(b), :] = jnp.broadcast_to(lse_all, o.shape)

    qkv_v, ncb, cols = _dil_view(qkv, group)
    out_spec = pl.BlockSpec((nblk * BLOCK, DIL_OUT_WIDTH), lambda r, i: (i, r))
    o, lse = _pcall(
        body,
        name=f"dil_fwd_g{group}",
        grid=(dilation, nb // nblk),
        in_specs=_dil_step_specs(ncb, cols, nblk, lambda i: i),
        out_specs=[out_spec, out_spec],
        out_shape=[jax.ShapeDtypeStruct((sub, dilation * DIL_OUT_WIDTH), F32)] * 2,
        compiler_params=_cparams(2),
    )(qkv_v, qkv_v, qkv_v, qkv_v, qkv_v)
    return o.reshape(s, DIL_OUT_WIDTH), lse.reshape(s, DIL_OUT_WIDTH), lse


def _dil_bwd(qkv, do_g, lse_g, dterm_g, group, rider=None):
    window, dilation = DIL_GROUPS[group]
    s = qkv.shape[0]
    sub = s // dilation
    nb = sub // BLOCK
    nblk = min(DIL_STEP_BLOCKS, nb)
    n_steps = nb // nblk
    slopes = [_alibi_slope(group * DIL_HEADS_PER_GROUP + h) * dilation for h in range(DIL_HEADS_PER_GROUP)]
    scale = 1.0 / math.sqrt(HEAD_DIM)
    tail = slice((nblk - 1) * BLOCK, nblk * BLOCK)

    def body(q_ref, kc_ref, kp_ref, vc_ref, vp_ref, do_ref, lse_ref, dt_ref, dq_ref, dk_ref, dv_ref, ck_ref, cv_ref):
        i = pl.program_id(1)

        @pl.when(i == 0)
        def _():
            ck_ref[...] = jnp.zeros_like(ck_ref)
            cv_ref[...] = jnp.zeros_like(cv_ref)

        @pl.when(i < n_steps)
        def _():
            kk_all = jnp.concatenate([kp_ref[...], kc_ref[...]], axis=0)
            vv_all = jnp.concatenate([vp_ref[...], vc_ref[...]], axis=0)
            lane = lax.broadcasted_iota(jnp.int32, (1, DIL_OUT_WIDTH), 1)
            head_id = lane // HEAD_DIM
            chains = [(b, h) for b in range(nblk) for h in range(DIL_HEADS_PER_GROUP)]
            rows = lambda b: slice(b * BLOCK, (b + 1) * BLOCK)
            keys = lambda b: slice(b * BLOCK, (b + 2) * BLOCK)
            bands = [_dil_band(i == 0 if b == 0 else False) for b in range(nblk)]
            qms, doms = [], []
            for b, h in chains:
                q, do = q_ref[rows(b), :], do_ref[rows(b), :]
                qms.append(jnp.where(head_id == h, q, jnp.zeros_like(q)))
                doms.append(jnp.where(head_id == h, do, jnp.zeros_like(do)))
            scores = [_dot_nt(qm, kk_all[keys(b)]) for (b, h), qm in zip(chains, qms)]
            dps = [_dot_nt(dom, vv_all[keys(b)]) for (b, h), dom in zip(chains, doms)]
            pbs, dss = [], []
            for n, (b, h) in enumerate(chains):
                steps, valid = bands[b]
                first = lane == h * HEAD_DIM
                lse = jnp.sum(jnp.where(first, lse_ref[rows(b), :], 0.0), axis=1, keepdims=True)
                dt = jnp.sum(jnp.where(first, dt_ref[rows(b), :], 0.0), axis=1, keepdims=True)
                logits = jnp.where(valid, scores[n] * scale - slopes[h] * steps, NEG_INF)
                p = jnp.where(valid, jnp.exp(logits - lse), 0.0)
                pbs.append(p.astype(BF16))
                dss.append((p * (dps[n] + dt) * scale).astype(BF16))
            dqs = [_dot(ds, kk_all[keys(b)]) for (b, h), ds in zip(chains, dss)]
            dks = [_dot_tn(ds, qm) for ds, qm in zip(dss, qms)]
            dvs = [_dot_tn(pb, dom) for pb, dom in zip(pbs, doms)]
            dkk, dvv = [], []
            for b in range(nblk):
                mine = [n for n, ch in enumerate(chains) if ch[0] == b]
                dq = dqs[mine[0]]
                for n in mine[1:]:
                    dq = jnp.where(head_id == chains[n][1], dqs[n], dq)
                dq_ref[rows(b), :] = dq.astype(dq_ref.dtype)
                dkk.append((dks[mine[0]] + dks[mine[1]]) + (dks[mine[2]] + dks[mine[3]]))
                dvv.append((dvs[mine[0]] + dvs[mine[1]]) + (dvs[mine[2]] + dvs[mine[3]]))
            for out_ref, carry_ref, parts in ((dk_ref, ck_ref, dkk), (dv_ref, cv_ref, dvv)):
                if nblk > 1:
                    out_ref[: (nblk - 1) * BLOCK, :] = carry_ref[: (nblk - 1) * BLOCK, :].astype(out_ref.dtype)
                out_ref[tail, :] = (carry_ref[tail, :] + parts[0][:BLOCK]).astype(out_ref.dtype)
                for b in range(nblk):
                    own = parts[b][BLOCK:]
                    carry_ref[rows(b), :] = own + parts[b + 1][:BLOCK] if b + 1 < nblk else own

        @pl.when(i == n_steps)
        def _():
            dk_ref[...] = ck_ref[...].astype(dk_ref.dtype)
            dv_ref[...] = cv_ref[...].astype(dv_ref.dtype)

    clamp = lambda i: jnp.minimum(i, n_steps - 1)
    qkv_v, ncb, cols = _dil_view(qkv, group)
    view = lambda t: t.reshape(sub, dilation * DIL_OUT_WIDTH)
    row_spec = pl.BlockSpec((nblk * BLOCK, DIL_OUT_WIDTH), lambda r, i: (clamp(i), r))
    late_spec = pl.BlockSpec((nblk * BLOCK, DIL_OUT_WIDTH), lambda r, i: (jnp.maximum(i - 1, 0), r))
    res = _call(
        body, (qkv_v, qkv_v, qkv_v, qkv_v, qkv_v, view(do_g), view(lse_g), view(dterm_g)), rider,
        name=f"dil_bwd_g{group}",
        grid=(dilation, n_steps + 1),
        in_specs=_dil_step_specs(ncb, cols, nblk, clamp) + [row_spec, row_spec, row_spec],
        out_specs=[row_spec, late_spec, late_spec],
        out_shape=[jax.ShapeDtypeStruct((sub, dilation * DIL_OUT_WIDTH), BF16)] * 3,
        scratch_shapes=[pltpu.VMEM((nblk * BLOCK, DIL_OUT_WIDTH), F32)] * 2,
        compiler_params=_cparams(2),
    )
    grads, lands = res if rider is not None else (res, None)
    grads = tuple(g.reshape(s, DIL_OUT_WIDTH) for g in grads)
    return grads if rider is None else (grads, lands)


def _dil_view(qkv, group):
    _, dilation = DIL_GROUPS[group]
    if dilation == 1:
        return qkv, QKV_COLS // DIL_OUT_WIDTH, (group, 3 + group, 6 + group)
    w = DIL_OUT_WIDTH
    own = jnp.concatenate([qkv[:, (3 * part + group) * w:(3 * part + group + 1) * w] for part in range(3)], axis=1)
    return own.reshape(qkv.shape[0] // dilation, dilation * 3 * w), 3, (0, 1, 2)


def _head_block_ones():
    r = lax.broadcasted_iota(jnp.int32, (DIL_OUT_WIDTH, DIL_OUT_WIDTH), 0) // HEAD_DIM
    c = lax.broadcasted_iota(jnp.int32, (DIL_OUT_WIDTH, DIL_OUT_WIDTH), 1) // HEAD_DIM
    return jnp.where(r == c, 1.0, 0.0).astype(BF16)


def _dil_mix_weights(l0, l1, l2):
    mx = jnp.maximum(jnp.maximum(l0, l1), l2)
    e0, e1, e2 = jnp.exp(l0 - mx), jnp.exp(l1 - mx), jnp.exp(l2 - mx)
    inv = 1.0 / (e0 + e1 + e2)
    return e0 * inv, e1 * inv, e2 * inv


def _dil_mix_fwd(os_, lses, tm):
    def epi(_, rows, consts):
        o0, o1, o2, l0, l1, l2 = rows
        w0, w1, w2 = _dil_mix_weights(l0, l1, l2)
        return [w0 * o0 + w1 * o1 + w2 * o2], []

    (o_a,) = _rowk("dil_mix_fwd", tm=tm, rows=list(os_) + list(lses), row_outs=[(DIL_OUT_WIDTH, BF16)], epilogue=epi)
    return o_a


def _dil_mix_bwd(do_a, os_, lses, tm):
    def epi(_, rows, consts):
        do, o0, o1, o2, l0, l1, l2 = rows
        do = do.astype(F32)
        w0, w1, w2 = _dil_mix_weights(l0, l1, l2)
        mixed = w0 * o0 + w1 * o1 + w2 * o2
        tot = _dot_hi_lo(do * mixed, _head_block_ones())
        return [w0 * do, w1 * do, w2 * do, -w0 * tot, -w1 * tot, -w2 * tot], []

    return _rowk(
        "dil_mix_bwd", tm=tm, rows=[do_a] + list(os_) + list(lses),
        row_outs=[(DIL_OUT_WIDTH, BF16)] * 3 + [(DIL_OUT_WIDTH, F32)] * 3, epilogue=epi)


_SB_Q0 = 3 * DIL_WIDTH // LANES
_SB_K0 = _SB_Q0 + SB_WIDTH // LANES
_SB_V0 = _SB_K0 + SB_WIDTH // LANES


_EXP_CLAMP = 88.0
_SB_DEAD = 104.0


def _tri(t, op):
    r = lax.broadcasted_iota(jnp.int32, (t, t), 0)
    c = lax.broadcasted_iota(jnp.int32, (t, t), 1)
    return jnp.where(op(r, c), 1.0, 0.0).astype(BF16)


def _softplus(z):
    return jnp.maximum(z, jnp.log(1.0 + jnp.exp(jnp.minimum(z, _EXP_CLAMP))))


def _sb_chain_head(qm, kj, mask):
    z = _dot_nt(qm, kj)
    sp = _softplus(z)
    return (sp if mask is None else jnp.where(mask, sp, 0.0)), z - sp


def _sb_fwd(qkv, rider=None):
    s = qkv.shape[0]
    t = SB_TK
    assert s % (2 * t) == 0
    nq = s // (2 * t)
    n_pairs = SB_WIDTH // LANES

    def body(q_ref, k_ref, v_ref, o_ref, tot_ref, steps_ref):
        p, i = pl.program_id(0), pl.program_id(1)
        lane_hi = lax.broadcasted_iota(jnp.int32, (1, LANES), 1) // HEAD_DIM
        later = _tri(t, lambda r, c: r > c)
        causal = lax.broadcasted_iota(jnp.int32, (t, t), 1) < lax.broadcasted_iota(jnp.int32, (t, t), 0)
        qms = []
        for x in range(2):
            q = q_ref[pl.ds(x * t, t), :] * (1.0 / math.sqrt(HEAD_DIM))
            qms.append([jnp.where(lane_hi == hh, q, jnp.zeros_like(q)) for hh in range(2)])

        def tile(j):
            off = pl.multiple_of(j * t, t)
            return k_ref[pl.ds(off, t), :], v_ref[pl.ds(off, t), :]

        def step(groups, carry):
            kv = [tile(j) for _, j, _ in groups]
            chains = [(g, x, hh) for g, (x, _, _) in enumerate(groups) for hh in range(2)]
            heads = [_sb_chain_head(qms[x][hh], kv[g][0], causal if groups[g][2] else None) for g, x, hh in chains]
            sufs = [_dot(sp.astype(BF16), later) for sp, _ in heads]
            cur = [list(carry[0]), list(carry[1])]
            for (g, x, hh), (sp, lpos), suf in zip(chains, heads, sufs):
                c, acc = cur[x][hh]
                a = jnp.exp(lpos - suf - c)
                if groups[g][2]:
                    a = jnp.where(causal, a, 0.0)
                cur[x][hh] = (c + jnp.sum(sp, axis=1, keepdims=True), acc + _dot(a.astype(BF16), kv[g][1]))
            return (tuple(cur[0]), tuple(cur[1]))

        def lowest(carry):
            m = [jnp.min(carry[x][hh][0]) for x in range(2) for hh in range(2)]
            return jnp.minimum(jnp.minimum(m[0], m[1]), jnp.minimum(m[2], m[3]))

        zero = (jnp.zeros((t, 1), F32), jnp.zeros((t, LANES), F32))
        start = ((zero, zero), (zero, zero))
        carry = lax.cond(
            i == 0,
            lambda ca: step([(0, 0, True), (1, 1, True), (1, 0, False)], ca),
            lambda ca: step([(0, 2 * i, True), (1, 2 * i + 1, True), (0, 2 * i - 1, False), (1, 2 * i, False)], ca),
            start)

        n_more, carry = lax.while_loop(
            lambda st: jnp.logical_and(st[0] + 1 < 2 * i, lowest(st[1]) <= _SB_DEAD),
            lambda st: (st[0] + 1, step([(0, 2 * i - 2 - st[0], False), (1, 2 * i - 1 - st[0], False)], st[1])),
            (jnp.int32(0), carry))
        b_last = jnp.logical_and(jnp.logical_and(i > 0, n_more + 1 == 2 * i), lowest(carry) <= _SB_DEAD)
        carry = lax.cond(b_last, lambda ca: step([(1, 0, False)], ca), lambda ca: ca, carry)
        for x in range(2):
            (c0, acc0), (c1, acc1) = carry[x]
            o_ref[pl.ds(x * t, t), :] = jnp.where(lane_hi == 0, acc0, acc1).astype(o_ref.dtype)
            tot_ref[pl.ds(x * t, t), :] = jnp.where(lane_hi == 0, c0, c1)
        steps_ref[p, i] = 1 + n_more + b_last.astype(jnp.int32)

    return _call(
        body, (qkv, qkv, qkv), rider,
        name="sb_fwd",
        grid=(n_pairs, nq),
        in_specs=[
            pl.BlockSpec((2 * t, LANES), lambda p, i: (i, _SB_Q0 + p)),
            pl.BlockSpec((s, LANES), lambda p, i: (0, _SB_K0 + p)),
            pl.BlockSpec((s, LANES), lambda p, i: (0, _SB_V0 + p)),
        ],
        out_specs=[pl.BlockSpec((2 * t, LANES), lambda p, i: (i, p))] * 2 + [pl.BlockSpec(memory_space=pltpu.SMEM)],
        out_shape=[jax.ShapeDtypeStruct((s, SB_WIDTH), BF16), jax.ShapeDtypeStruct((s, SB_WIDTH), F32),
                   jax.ShapeDtypeStruct((n_pairs, nq), jnp.int32)],
        compiler_params=_cparams(2),
    )


def _sb_bwd(qkv, do_b, tot_b, n_steps):
    s = qkv.shape[0]
    t = SB_TK
    nq = s // (2 * t)
    n_pairs = SB_WIDTH // LANES
    scale = 1.0 / math.sqrt(HEAD_DIM)

    def body(steps_ref, q_ref, k_ref, v_ref, do_ref, tot_ref, dq_ref, dk_ref, dv_ref):
        p, i = pl.program_id(0), pl.program_id(1)

        @pl.when(i == 0)
        def _():
            dk_ref[...] = jnp.zeros_like(dk_ref)
            dv_ref[...] = jnp.zeros_like(dv_ref)

        lane = lax.broadcasted_iota(jnp.int32, (1, LANES), 1)
        lane_hi = lane // HEAD_DIM
        later = _tri(t, lambda r, c: r > c)
        before = _tri(t, lambda r, c: r < c)
        causal = lax.broadcasted_iota(jnp.int32, (t, t), 1) < lax.broadcasted_iota(jnp.int32, (t, t), 0)
        qms, doms, tots = [], [], []
        for x in range(2):
            rows = pl.ds(x * t, t)
            q, do, tot_all = q_ref[rows, :] * scale, do_ref[rows, :], tot_ref[rows, :]
            qms.append([jnp.where(lane_hi == hh, q, jnp.zeros_like(q)) for hh in range(2)])
            doms.append([jnp.where(lane_hi == hh, do, jnp.zeros_like(do)) for hh in range(2)])
            tots.append([jnp.sum(jnp.where(lane == hh * HEAD_DIM, tot_all, 0.0), axis=1, keepdims=True)
                         for hh in range(2)])

        def step(groups, carry):
            offs = [pl.multiple_of(j * t, t) for _, j, _ in groups]
            ks = [k_ref[pl.ds(off, t), :] for off in offs]
            vs = [v_ref[pl.ds(off, t), :] for off in offs]
            chains = [(g, x, hh) for g, (x, _, _) in enumerate(groups) for hh in range(2)]
            heads = [_sb_chain_head(qms[x][hh], ks[g], causal if groups[g][2] else None) for g, x, hh in chains]
            sufs = [_dot(sp.astype(BF16), later) for sp, _ in heads]
            das = [_dot_nt(doms[x][hh], vs[g]) for g, x, hh in chains]
            cur = [list(carry[0]), list(carry[1])]
            sigs, gs, abs_, cg_before = [], [], [], []
            for (g_, x, hh), (sp, lpos), suf, da in zip(chains, heads, sufs, das):
                cl, cg, dq = cur[x][hh]
                cl = cl + jnp.sum(sp, axis=1, keepdims=True)
                sig = jnp.exp(lpos)
                a = sig * jnp.exp(-suf - (tots[x][hh] - cl))
                if groups[g_][2]:
                    a = jnp.where(causal, a, 0.0)
                g = a * da
                sigs.append(sig)
                gs.append(g)
                abs_.append(a.astype(BF16))
                cg_before.append(cg)
                cur[x][hh] = (cl, cg + jnp.sum(g, axis=1, keepdims=True), dq)
            prefs = [_dot(g.astype(BF16), before) for g in gs]
            dvs = [_dot_tn(ab, doms[x][hh]) for (_, x, hh), ab in zip(chains, abs_)]
            dzs = []
            for (g_, x, hh), sig, g, pref, cg in zip(chains, sigs, gs, prefs, cg_before):
                dz = g - sig * (g + pref + cg)
                if groups[g_][2]:
                    dz = jnp.where(causal, dz, 0.0)
                dzs.append(dz.astype(BF16))
            dqs = [_dot(dz, ks[g_]) for (g_, x, hh), dz in zip(chains, dzs)]
            dks = [_dot_tn(dz, qms[x][hh]) for (_, x, hh), dz in zip(chains, dzs)]
            for n, (_, x, hh) in enumerate(chains):
                cl, cg, dq = cur[x][hh]
                cur[x][hh] = (cl, cg, dq + dqs[n])
            for g_, off in enumerate(offs):
                dk_ref[pl.ds(off, t), :] += dks[2 * g_] + dks[2 * g_ + 1]
                dv_ref[pl.ds(off, t), :] += dvs[2 * g_] + dvs[2 * g_ + 1]
            return (tuple(cur[0]), tuple(cur[1]))

        taken = steps_ref[p, i]
        n_full = jnp.minimum(taken, 2 * i)
        zero = (jnp.zeros((t, 1), F32), jnp.zeros((t, 1), F32), jnp.zeros((t, LANES), F32))
        carry = ((zero, zero), (zero, zero))
        carry = lax.cond(jnp.logical_and(i > 0, taken > 2 * i), lambda ca: step([(1, 0, False)], ca), lambda ca: ca,
                         carry)
        carry = lax.fori_loop(
            0, n_full - 1,
            lambda n, ca: step([(0, 2 * i - n_full + n, False), (1, 2 * i + 1 - n_full + n, False)], ca), carry)
        carry = lax.cond(
            i == 0,
            lambda ca: step([(1, 0, False), (0, 0, True), (1, 1, True)], ca),
            lambda ca: step([(0, 2 * i - 1, False), (1, 2 * i, False), (0, 2 * i, True), (1, 2 * i + 1, True)], ca),
            carry)
        for x in range(2):
            dq = jnp.where(lane_hi == 0, carry[x][0][2], carry[x][1][2])
            dq_ref[pl.ds(x * t, t), :] = (dq * scale).astype(dq_ref.dtype)

    row_spec = pl.BlockSpec((2 * t, LANES), lambda p, i, ns: (i, p))
    full_spec = pl.BlockSpec((s, LANES), lambda p, i, ns: (0, p))
    return _pcall(
        body,
        name="sb_bwd",
        grid_spec=pltpu.PrefetchScalarGridSpec(
            num_scalar_prefetch=1,
            grid=(n_pairs, nq),
            in_specs=[
                pl.BlockSpec((2 * t, LANES), lambda p, i, ns: (i, _SB_Q0 + p)),
                pl.BlockSpec((s, LANES), lambda p, i, ns: (0, _SB_K0 + p)),
                pl.BlockSpec((s, LANES), lambda p, i, ns: (0, _SB_V0 + p)),
                row_spec, row_spec,
            ],
            out_specs=[row_spec, full_spec, full_spec],
        ),
        out_shape=[jax.ShapeDtypeStruct((s, SB_WIDTH), BF16), jax.ShapeDtypeStruct((s, SB_WIDTH), F32),
                   jax.ShapeDtypeStruct((s, SB_WIDTH), F32)],
        compiler_params=_cparams(2),
    )(n_steps, qkv, qkv, qkv, do_b, tot_b)


def _gates(gl, bg):
    return _sigmoid(gl[:, :D_MODEL] + bg[:, :D_MODEL]), _sigmoid(gl[:, D_MODEL:] + bg[:, D_MODEL:])


def _mixer_fwd(o_a, o_b, gl, x0, bg, g2, w_ud, w_us, w_out, tm):
    def epi(_, rows, consts):
        oa, ob, glv, x = rows
        bgv, g2v, wud, wus, wout = consts
        ga, gb = _gates(glv, bgv)
        merged = ga * _dot(oa, wud) + gb * _dot(ob, wus)
        x1 = x + _dot(merged.astype(BF16), wout)
        r, xh = _rms_stats(x1)
        return [x1, xh * g2v], []

    return _rowk("mixer_fwd", tm=tm, rows=[o_a, o_b, gl, x0], consts=[bg, g2, w_ud, w_us, w_out],
                 row_outs=[(D_MODEL, F32), (D_MODEL, BF16)], epilogue=epi)


def _mixer_bwd(dx1, o_a, o_b, gl, bg, w_ud, w_us, w_out, tm, rider=None):
    s = dx1.shape[0]
    nm = s // tm

    def body(dx_ref, oa_ref, ob_ref, gl_ref, bg_ref, wud_ref, wus_ref, wout_ref,
             doa_ref, dob_ref, dgl_ref, gwout_ref, gwud_ref, gwus_ref, gbg_ref):
        i = pl.program_id(0)
        dxb = dx_ref[...].astype(BF16)
        oa, ob = oa_ref[...], ob_ref[...]
        ga, gb = _gates(gl_ref[...], bg_ref[...])
        ua, ub = _dot(oa, wud_ref[...]), _dot(ob, wus_ref[...])
        merged = (ga * ua + gb * ub).astype(BF16)
        dm = _dot_nt(dxb, wout_ref[...])
        dua = (dm * ga).astype(BF16)
        dub = (dm * gb).astype(BF16)
        dgla = dm * ua * ga * (1.0 - ga)
        dglb = dm * ub * gb * (1.0 - gb)
        doa_ref[...] = _dot_nt(dua, wud_ref[...]).astype(doa_ref.dtype)
        dob_ref[...] = _dot_nt(dub, wus_ref[...]).astype(dob_ref.dtype)
        dgl_ref[:, :D_MODEL] = dgla.astype(dgl_ref.dtype)
        dgl_ref[:, D_MODEL:] = dglb.astype(dgl_ref.dtype)
        parts = [(gwout_ref, _dot_tn(merged, dxb)), (gwud_ref, _dot_tn(oa, dua)), (gwus_ref, _dot_tn(ob, dub))]
        for r, v in parts:

            @pl.when(i == 0)
            def _(r=r, v=v):
                r[...] = v

            @pl.when(i > 0)
            def _(r=r, v=v):
                r[...] += v

        sa = jnp.sum(dgla, axis=0, keepdims=True)
        sb = jnp.sum(dglb, axis=0, keepdims=True)

        @pl.when(i == 0)
        def _():
            gbg_ref[:, :D_MODEL] = sa
            gbg_ref[:, D_MODEL:] = sb

        @pl.when(i > 0)
        def _():
            gbg_ref[:, :D_MODEL] += sa
            gbg_ref[:, D_MODEL:] += sb

    row = lambda w: pl.BlockSpec((tm, w), lambda i: (i, 0))
    full = lambda a: pl.BlockSpec(a.shape, lambda i: (0, 0), pipeline_mode=pl.Buffered(1))
    fshape = lambda r, c: jax.ShapeDtypeStruct((r, c), F32)
    return _call(
        body, (dx1, o_a, o_b, gl, bg, w_ud, w_us, w_out), rider,
        name="mixer_bwd",
        grid=(nm,),
        in_specs=[row(D_MODEL), row(DIL_OUT_WIDTH), row(SB_WIDTH), row(2 * D_MODEL),
                  full(bg), full(w_ud), full(w_us), full(w_out)],
        out_specs=[row(DIL_OUT_WIDTH), row(SB_WIDTH), row(2 * D_MODEL),
                   pl.BlockSpec((D_MODEL, D_MODEL), lambda i: (0, 0)),
                   pl.BlockSpec((DIL_OUT_WIDTH, D_MODEL), lambda i: (0, 0)),
                   pl.BlockSpec((SB_WIDTH, D_MODEL), lambda i: (0, 0)),
                   pl.BlockSpec((1, 2 * D_MODEL), lambda i: (0, 0))],
        out_shape=[jax.ShapeDtypeStruct((s, DIL_OUT_WIDTH), BF16), jax.ShapeDtypeStruct((s, SB_WIDTH), BF16),
                   jax.ShapeDtypeStruct((s, 2 * D_MODEL), BF16),
                   fshape(D_MODEL, D_MODEL), fshape(DIL_OUT_WIDTH, D_MODEL), fshape(SB_WIDTH, D_MODEL),
                   fshape(1, 2 * D_MODEL)],
        compiler_params=_cparams(1),
    )


def _all_gather(shards):
    n = len(shards)

    def body(*refs):
        x_refs, out_refs = refs[:n], refs[n:2 * n]
        send_sems, recv_sems, local_sems = refs[2 * n:]
        x, y, c = lax.axis_index("x"), lax.axis_index("y"), lax.axis_index("c")
        me, sibling = (x, y, c), (x, y, 1 - c)
        chips = [(1 - x, y), (x, 1 - y), (1 - x, 1 - y)]

        def slot(a, px, py, pc):
            return out_refs[a].at[4 * px + 2 * py + pc]

        def copy(a, k, block, to, own=False):
            return pltpu.make_async_remote_copy(
                src_ref=x_refs[a] if own else slot(a, *block), dst_ref=slot(a, *block),
                send_sem=send_sems.at[7 * a + k], recv_sem=recv_sems.at[7 * a + k], device_id=to, device_id_type=_MESH)

        mine = [pltpu.make_async_copy(x_refs[a], slot(a, *me), local_sems.at[a]) for a in range(n)]
        for cp in mine:
            cp.start()
        first = []
        for a in range(n):
            first.append(copy(a, 0, me, sibling, own=True))
            first += [copy(a, 1 + j, me, (*chip, c), own=True) for j, chip in enumerate(chips)]
        for cp in first:
            cp.start()
        passed = []
        for a in range(n):
            for j, chip in enumerate(chips):
                copy(a, 1 + j, (*chip, c), me).wait_recv()
                passed.append(copy(a, 4 + j, (*chip, c), sibling))
                passed[-1].start()
        for a in range(n):
            copy(a, 0, sibling, me).wait_recv()
            for j, chip in enumerate(chips):
                copy(a, 4 + j, (*chip, 1 - c), me).wait_recv()
        for cp in first + passed:
            cp.wait_send()
        for cp in mine:
            cp.wait()

    return _pcall(
        body,
        name="all_gather_weights",
        in_specs=[_HBM] * n,
        out_specs=[_HBM] * n,
        out_shape=[jax.ShapeDtypeStruct((N_DEV,) + s.shape, s.dtype) for s in shards],
        scratch_shapes=[pltpu.SemaphoreType.DMA((7 * n,)), pltpu.SemaphoreType.DMA((7 * n,)),
                        pltpu.SemaphoreType.DMA((n,))],
    )(*shards)


def _exchange(chunks):
    n = len(chunks)

    def body(*refs):
        g_refs, o_refs = refs[:n], refs[n:2 * n]
        send_sems, recv_sems, local_sems = refs[2 * n:]
        x, y, c = lax.axis_index("x"), lax.axis_index("y"), lax.axis_index("c")
        me = 4 * x + 2 * y + c
        own = [pltpu.make_async_copy(g_refs[a].at[me], o_refs[a].at[me], local_sems.at[a]) for a in range(n)]
        for cp in own:
            cp.start()
        copies = []
        for a in range(n):
            for k in range(1, N_DEV):
                px, py, pc = x ^ (k >> 2), y ^ ((k >> 1) & 1), c ^ (k & 1)
                peer = 4 * px + 2 * py + pc
                copies.append(pltpu.make_async_remote_copy(
                    src_ref=g_refs[a].at[peer], dst_ref=o_refs[a].at[me], send_sem=send_sems.at[7 * a + k - 1],
                    recv_sem=recv_sems.at[7 * a + k - 1], device_id=(px, py, pc), device_id_type=_MESH))
        for cp in copies:
            cp.start()
        for cp in copies:
            cp.wait()
        for cp in own:
            cp.wait()

    return _pcall(
        body,
        name="exchange_grads",
        in_specs=[_HBM] * n,
        out_specs=[_HBM] * n,
        out_shape=[jax.ShapeDtypeStruct(g.shape, g.dtype) for g in chunks],
        scratch_shapes=[pltpu.SemaphoreType.DMA((7 * n,)), pltpu.SemaphoreType.DMA((7 * n,)),
                        pltpu.SemaphoreType.DMA((n,))],
    )(*chunks)


def _reduce_adamw(name, parts, w, m, v, tr):
    _, rows, cols = parts.shape
    tr = min(tr, rows)
    assert rows % tr == 0
    c1 = 1.0 / (1.0 - ADAM_B1 ** ADAM_STEP)
    c2 = 1.0 / (1.0 - ADAM_B2 ** ADAM_STEP)

    def body(p_ref, w_ref, m_ref, v_ref, g_out, d_out, m_out, v_out):
        g = p_ref[0].astype(F32)
        for d in range(1, N_DEV):
            g = g + p_ref[d].astype(F32)
        mn = ADAM_B1 * m_ref[...] + (1.0 - ADAM_B1) * g
        vn = ADAM_B2 * v_ref[...] + (1.0 - ADAM_B2) * (g * g)
        g_out[...] = g
        m_out[...] = mn
        v_out[...] = vn
        d_out[...] = -ADAM_LR * ((mn * c1) / (jnp.sqrt(vn * c2) + ADAM_EPS) + ADAM_WD * w_ref[...])

    spec = pl.BlockSpec((tr, cols), lambda i: (i, 0))
    return _pcall(
        body,
        name=name,
        grid=(rows // tr,),
        in_specs=[pl.BlockSpec((N_DEV, tr, cols), lambda i: (0, i, 0)), spec, spec, spec],
        out_specs=[spec] * 4,
        out_shape=[jax.ShapeDtypeStruct((rows, cols), F32)] * 4,
        compiler_params=_cparams(1),
    )(parts, w, m, v)


_SHARDED = ("w_in", "w_up_dil", "w_up_sb", "w_out", "w_mlp_in", "w_mlp_out")
_FULL_SHAPES = {"w_in": (D_MODEL, IN_COLS), "w_up_dil": (DIL_OUT_WIDTH, D_MODEL), "w_up_sb": (SB_WIDTH, D_MODEL),
                "w_out": (D_MODEL, D_MODEL), "w_mlp_in": (D_MODEL, D_FF), "w_mlp_out": (D_FF, D_MODEL)}
_ROW_SHARDED = ("w_out", "w_mlp_out")


def _shard_shape(name):
    r, c = _FULL_SHAPES[name]
    return (r // N_DEV, c) if name in _ROW_SHARDED else (r, c // N_DEV)


def _assemble(name, gathered):
    r, c = _shard_shape(name)
    if name in _ROW_SHARDED:
        return gathered.reshape(N_DEV * r, c)
    return gathered.transpose(1, 0, 2).reshape(r, N_DEV * c)


def _chunk(name, full):
    r, c = _shard_shape(name)
    if name in _ROW_SHARDED:
        return full.reshape(N_DEV, r, c)
    return full.reshape(r, N_DEV, c).transpose(1, 0, 2)


_SMALL = (("norm_mix_g", D_MODEL), ("b_gate", 2 * D_MODEL), ("norm_mlp_g", D_MODEL), ("norm_final_g", D_MODEL))
_SMALL_N = sum(n for _, n in _SMALL) + LANES


def _pack_small(vals, tail):
    return jnp.concatenate([vals[n].reshape(1, -1) for n, _ in _SMALL] + [tail], axis=1)


def _unpack_small(vec, shapes):
    out, pos = {}, 0
    for n, width in _SMALL:
        out[n] = vec[:, pos:pos + width].reshape(shapes[n])
        pos += width
    return out, vec[:, pos:]


def kernel(x, norm_mix_g, w_in, b_gate, w_up_dil, w_up_sb, w_out, norm_mlp_g, w_mlp_in, w_mlp_out, norm_final_g, loss_target, m_norm_mix_g, m_w_in, m_b_gate, m_w_up_dil, m_w_up_sb, m_w_out, m_norm_mlp_g, m_w_mlp_in, m_w_mlp_out, m_norm_final_g, v_norm_mix_g, v_w_in, v_b_gate, v_w_up_dil, v_w_up_sb, v_w_out, v_norm_mlp_g, v_w_mlp_in, v_w_mlp_out, v_norm_final_g):
    given = dict(locals())
    s = x.shape[1]
    x0 = x.reshape(s, D_MODEL)
    target = loss_target.reshape(s, D_MODEL)
    g1 = norm_mix_g.reshape(1, D_MODEL)
    g2 = norm_mlp_g.reshape(1, D_MODEL)
    g3 = norm_final_g.reshape(1, D_MODEL)
    bg = b_gate.reshape(1, 2 * D_MODEL)
    w_shards = {n: given[n].reshape(_shard_shape(n)) for n in _SHARDED}
    m_shards = {n: given["m_" + n].reshape(_shard_shape(n)) for n in _SHARDED}
    v_shards = {n: given["v_" + n].reshape(_shard_shape(n)) for n in _SHARDED}

    shard_b = {n: w_shards[n].astype(BF16) for n in _SHARDED}
    (gathered_w_in,) = _all_gather([shard_b["w_in"]])
    w_in_f = _assemble("w_in", gathered_w_in)
    w_qkv, w_gl = w_in_f[:, :QKV_COLS], w_in_f[:, QKV_COLS:]
    full = {}

    def norm1(_, rows, consts):
        _, xh = _rms_stats(rows[0])
        return [xh * consts[0]], []

    (h1,) = _rowk("norm_mix", tm=512, rows=[x0], consts=[g1], row_outs=[(D_MODEL, BF16)], epilogue=norm1)
    qkv, (land,) = _mm("proj_qkv", h1, w_qkv, out_dtype=BF16, tm=1024, tn=768, tk=D_MODEL,
                       rider=_Spread([shard_b["w_mlp_in"]], chunked=False))
    full["w_mlp_in"] = _assemble("w_mlp_in", land)
    gl = _mm("proj_gates", h1, w_gl, out_dtype=BF16, tm=1024, tn=1024, tk=D_MODEL)
    dil = [_dil_fwd(qkv, g) for g in range(len(DIL_GROUPS))]
    os_, lses = [d[0] for d in dil], [d[1] for d in dil]
    o_a = _dil_mix_fwd(os_, lses, 512)
    riding = ("w_mlp_out", "w_out", "w_up_sb", "w_up_dil")
    (o_b, tot_b, sb_steps), lands = _sb_fwd(qkv, rider=_Spread([shard_b[n] for n in riding], chunked=False))
    full.update({n: _assemble(n, land) for n, land in zip(riding, lands)})
    x1, h2 = _mixer_fwd(o_a, o_b, gl, x0, bg, g2, full["w_up_dil"], full["w_up_sb"], full["w_out"], 512)
    f = _mm("mlp_in", h2, full["w_mlp_in"], out_dtype=BF16, tm=1024, tn=1024, tk=D_MODEL,
            epilogue=lambda r, _: jnp.square(jnp.maximum(r, 0.0)))

    def head(acc, rows, consts):
        x1v, tv = rows
        g3v = consts[0]
        x2 = x1v + acc
        r, xh = _rms_stats(x2)
        diff = xh * g3v - tv
        loss = (0.5 / D_MODEL) * jnp.sum(jnp.sum(diff * diff, axis=0, keepdims=True), axis=1, keepdims=True)
        dy = diff * (1.0 / D_MODEL)
        dx2, dg = _rms_bwd(dy, xh, r, g3v)
        return [dx2, dx2], [dg, jnp.broadcast_to(loss, (1, LANES))]

    dx2, dx2b, gg3, loss_part = _rowk(
        "mlp_out_loss", a=f, w=full["w_mlp_out"], tm=512, tk=D_FF, rows=[x1, target], consts=[g3],
        row_outs=[(D_MODEL, F32), (D_MODEL, BF16)], acc_outs=[D_MODEL, LANES], epilogue=head)

    da = _mm("mlp_out_bwd", dx2b, full["w_mlp_out"], tb=True, out_dtype=BF16, tm=1024, tn=1024, tk=D_MODEL, extra=f,
             epilogue=lambda r, fv: r * (2.0 * jnp.sqrt(fv.astype(F32))))
    g_w_mlp_out = _mm("grad_w_mlp_out", f, dx2b, ta=True, out_dtype=F32, tm=1024, tn=1024, tk=2048)
    g_w_mlp_in = _mm("grad_w_mlp_in", h2, da, ta=True, out_dtype=F32, tm=1024, tn=1024, tk=2048)

    def norm_bwd(acc, rows, consts):
        xv, dres = rows
        r, xh = _rms_stats(xv)
        dx, dg = _rms_bwd(acc, xh, r, consts[0])
        return [dres + dx], [dg]

    bchunk = lambda n, g: _chunk(n, g).astype(BF16)
    parts = {}
    (dx1, gg2), (parts["w_mlp_in"],) = _rowk(
        "mlp_in_bwd", a=da, w=full["w_mlp_in"], nt=True, tm=512, tk=D_FF, rows=[x1, dx2], consts=[g2],
        row_outs=[(D_MODEL, F32)], acc_outs=[D_MODEL], epilogue=norm_bwd,
        rider=_Spread([bchunk("w_mlp_in", g_w_mlp_in)], chunked=True))
    (do_a, do_b, dgl, g_w_out, g_w_ud, g_w_us, g_bg), (parts["w_mlp_out"],) = _mixer_bwd(
        dx1, o_a, o_b, gl, bg, full["w_up_dil"], full["w_up_sb"], full["w_out"], 512,
        rider=_Spread([bchunk("w_mlp_out", g_w_mlp_out)], chunked=True))
    mix = _dil_mix_bwd(do_a, os_, lses, 512)
    dil_b = [_dil_bwd(qkv, mix[g], dil[g][2], mix[3 + g], g) for g in range(2)]
    small_three = {"w_out": g_w_out, "w_up_sb": g_w_us, "w_up_dil": g_w_ud}
    grads, lands = _dil_bwd(qkv, mix[2], dil[2][2], mix[5], 2,
                            rider=_Spread([bchunk(n, g) for n, g in small_three.items()], chunked=True))
    dil_b.append(grads)
    parts.update(dict(zip(small_three, lands)))
    dq_b, dk_b, dv_b = _sb_bwd(qkv, do_b, tot_b, sb_steps)
    dproj = [d[0] for d in dil_b] + [d[1] for d in dil_b] + [d[2] for d in dil_b] + [dq_b, dk_b, dv_b, dgl]
    g_w_in = jnp.concatenate([
        _grad_cols("grad_w_in_dil", h1, dproj[:9], tm=D_MODEL, tk=1024),
        _grad_cols("grad_w_in_sb", h1, dproj[9:12], tm=D_MODEL, tk=1024),
        _grad_cols("grad_w_in_gates", h1, dproj[12:], tm=D_MODEL, tk=1024)], axis=1)
    (grad_x, gg1), (parts["w_in"],) = _rowk(
        "in_proj_bwd", a=dproj, w=w_in_f, nt=True, tm=512, tk=IN_COLS, rows=[x0, dx1], consts=[g1],
        row_outs=[(D_MODEL, F32)], acc_outs=[D_MODEL], epilogue=norm_bwd,
        rider=_Spread([bchunk("w_in", g_w_in)], chunked=True))

    small_part = _pack_small({"norm_mix_g": gg1, "b_gate": g_bg, "norm_mlp_g": gg2, "norm_final_g": gg3}, loss_part)
    (small_parts,) = _exchange([jnp.broadcast_to(small_part[None], (N_DEV, 1, _SMALL_N))])

    tags = ("grad_", "delta_", "new_m_", "new_v_")
    outs = {}
    for n, p in parts.items():
        res = _reduce_adamw("adamw_" + n, p, w_shards[n], m_shards[n], v_shards[n], 128)
        for tag, val in zip(tags, res):
            outs[tag + n] = val.reshape(given[n].shape)
    small_w = _pack_small(given, jnp.zeros((1, LANES), F32))
    small_m = _pack_small({n: given["m_" + n] for n, _ in _SMALL}, jnp.zeros((1, LANES), F32))
    small_v = _pack_small({n: given["v_" + n] for n, _ in _SMALL}, jnp.ones((1, LANES), F32))
    small_res = _reduce_adamw("adamw_replicated", small_parts, small_w, small_m, small_v, 8)

    small_shapes = {n: given[n].shape for n, _ in _SMALL}
    for tag, small in zip(tags, small_res):
        small_vals, tail = _unpack_small(small, small_shapes)
        for n, val in small_vals.items():
            outs[tag + n] = val
        if tag == "grad_":
            loss = tail[0, 0]
    names = ["norm_mix_g", "w_in", "b_gate", "w_up_dil", "w_up_sb", "w_out", "norm_mlp_g", "w_mlp_in", "w_mlp_out",
             "norm_final_g"]
    return (loss, grad_x.reshape(x.shape), *[outs["grad_" + n] for n in names], *[outs["delta_" + n] for n in names],
            *[outs["new_m_" + n] for n in names], *[outs["new_v_" + n] for n in names])
```

```python
import functools
import math

import jax
import jax.numpy as jnp
from jax import lax
from jax.experimental import pallas as pl
from jax.experimental.pallas import tpu as pltpu

_pcall = pl.pallas_call

F32 = jnp.float32
BF16 = jnp.bfloat16

D_MODEL = 1024
HEAD_DIM = 64
DIL_GROUPS = ((128, 1), (512, 4), (2048, 16))
DIL_HEADS_PER_GROUP = 4
N_DIL_HEADS = 12
N_SB_HEADS = 8
DIL_WIDTH = 768
DIL_OUT_WIDTH = 256
SB_WIDTH = 512
D_FF = 4096
BLOCK = 128
RMS_EPS = 1e-6
NEG_INF = -1e30
QKV_COLS = 3 * DIL_WIDTH + 3 * SB_WIDTH
IN_COLS = QKV_COLS + 2 * D_MODEL
N_DEV = 8

ADAM_LR = 0.001
ADAM_B1 = 0.9
ADAM_B2 = 0.999
ADAM_EPS = 1e-08
ADAM_WD = 0.01
ADAM_STEP = 10

VMEM_LIMIT = 56 * 1024 * 1024
SB_TK = 256
LANES = 128

_ARB = pltpu.ARBITRARY


def _cparams(n_axes, **kw):
    return pltpu.CompilerParams(dimension_semantics=(_ARB,) * n_axes, vmem_limit_bytes=VMEM_LIMIT, **kw)


def _dot(a, b):
    return jnp.dot(a, b, preferred_element_type=F32)


def _dot_nt(a, b):
    return lax.dot_general(a, b, (((1,), (1,)), ((), ())), preferred_element_type=F32)


def _dot_tn(a, b):
    return lax.dot_general(a, b, (((0,), (0,)), ((), ())), preferred_element_type=F32)


def _split_hi_lo(x):
    hi = x.astype(BF16)
    lo = (x - hi.astype(F32)).astype(BF16)
    return hi, lo


def _dot_hi_lo(x, m):
    hi, lo = _split_hi_lo(x)
    return _dot(hi, m) + _dot(lo, m)


def _sigmoid(x):
    return 1.0 / (1.0 + jnp.exp(-x))


_HBM = pl.BlockSpec(memory_space=pltpu.HBM)
_MESH = pl.DeviceIdType.MESH


class _Spread:
    def __init__(self, srcs, chunked):
        self.srcs, self.chunked, self.n = list(srcs), chunked, len(srcs)

    def land_shapes(self):
        return [jax.ShapeDtypeStruct((N_DEV,) + (s.shape[1:] if self.chunked else s.shape), s.dtype) for s in self.srcs]

    def scratch(self):
        dma = pltpu.SemaphoreType.DMA
        return [dma((7 * self.n,)), dma((7 * self.n,)), dma((self.n,))]

    def copies(self, src_refs, land_refs, send_sems, recv_sems, local_sems):
        x, y, c = lax.axis_index("x"), lax.axis_index("y"), lax.axis_index("c")
        me = 4 * x + 2 * y + c
        out = []
        for a, (src, land) in enumerate(zip(src_refs, land_refs)):
            out.append(pltpu.make_async_copy(src.at[me] if self.chunked else src, land.at[me], local_sems.at[a]))
            for k in range(1, N_DEV):
                px, py, pc = x ^ (k >> 2), y ^ ((k >> 1) & 1), c ^ (k & 1)
                out.append(pltpu.make_async_remote_copy(
                    src_ref=src.at[4 * px + 2 * py + pc] if self.chunked else src, dst_ref=land.at[me],
                    send_sem=send_sems.at[7 * a + k - 1], recv_sem=recv_sems.at[7 * a + k - 1],
                    device_id=(px, py, pc), device_id_type=_MESH))
        return out


def _call(body, args, rider=None, **kw):
    if rider is None:
        return _pcall(body, **kw)(*args)
    grid = kw["grid"]
    single = not isinstance(kw["out_shape"], (list, tuple))
    out_specs = [kw["out_specs"]] if single else list(kw["out_specs"])
    out_shape = [kw["out_shape"]] if single else list(kw["out_shape"])
    in_specs, scratch = list(kw["in_specs"]), list(kw.get("scratch_shapes", []))
    n_in, n_out, n_s, n = len(in_specs), len(out_shape), len(scratch), rider.n

    def hosted(*refs):
        ins, srcs = refs[:n_in], refs[n_in:n_in + n]
        outs, lands = refs[n_in + n:n_in + n + n_out], refs[n_in + n + n_out:n_in + 2 * n + n_out]
        own_scratch, sems = refs[n_in + 2 * n + n_out:n_in + 2 * n + n_out + n_s], refs[n_in + 2 * n + n_out + n_s:]
        ids = [pl.program_id(d) for d in range(len(grid))]
        first = functools.reduce(jnp.logical_and, [i == 0 for i in ids])
        last = functools.reduce(jnp.logical_and, [i == g - 1 for i, g in zip(ids, grid)])
        copies = rider.copies(srcs, lands, *sems)

        @pl.when(first)
        def _():
            for cp in copies:
                cp.start()

        body(*ins, *outs, *own_scratch)

        @pl.when(last)
        def _():
            for cp in copies:
                cp.wait()

    kw = dict(kw, in_specs=in_specs + [_HBM] * n, out_specs=out_specs + [_HBM] * n,
              out_shape=out_shape + rider.land_shapes(), scratch_shapes=scratch + rider.scratch())
    res = _pcall(hosted, **kw)(*args, *rider.srcs)
    return (res[0] if single else list(res[:n_out])), list(res[n_out:])


def _mm(name, a, b, *, ta=False, tb=False, out_dtype, tm, tn, tk, epilogue=None, extra=None, rider=None):
    m = a.shape[1] if ta else a.shape[0]
    k = a.shape[0] if ta else a.shape[1]
    n = b.shape[0] if tb else b.shape[1]
    assert (b.shape[1] if tb else b.shape[0]) == k
    tm, tn, tk = min(tm, m), min(tn, n), min(tk, k)
    assert m % tm == 0 and n % tn == 0 and k % tk == 0, (name, m, n, k, tm, tn, tk)
    nk = k // tk
    dn = (((0 if ta else 1,), (1 if tb else 0,)), ((), ()))
    in_place = nk > 1 and epilogue is None and out_dtype == F32

    def body(*refs):
        if extra is not None:
            a_ref, b_ref, e_ref, o_ref = refs[:4]
        else:
            a_ref, b_ref, o_ref = refs[:3]
            e_ref = None

        def finish(r):
            if epilogue is not None:
                r = epilogue(r, None if e_ref is None else e_ref[...])
            o_ref[...] = r.astype(out_dtype)

        part = lax.dot_general(a_ref[...].astype(BF16), b_ref[...].astype(BF16), dn, preferred_element_type=F32)
        if nk == 1:
            finish(part)
        else:
            acc_ref = o_ref if in_place else refs[-1]
            kk = pl.program_id(2)

            @pl.when(kk == 0)
            def _():
                acc_ref[...] = part

            @pl.when(kk > 0)
            def _():
                acc_ref[...] += part

            if not in_place:

                @pl.when(kk == nk - 1)
                def _():
                    finish(acc_ref[...])

    a_spec = pl.BlockSpec((tk, tm), lambda j, i, kk: (kk, i)) if ta else pl.BlockSpec((tm, tk), lambda j, i, kk: (i, kk))
    b_spec = pl.BlockSpec((tn, tk), lambda j, i, kk: (j, kk)) if tb else pl.BlockSpec((tk, tn), lambda j, i, kk: (kk, j))
    o_spec = pl.BlockSpec((tm, tn), lambda j, i, kk: (i, j))
    in_specs = [a_spec, b_spec]
    args = [a, b]
    if extra is not None:
        in_specs.append(o_spec)
        args.append(extra)
    return _call(
        body, args, rider,
        name=name,
        grid=(n // tn, m // tm, nk),
        in_specs=in_specs,
        out_specs=o_spec,
        out_shape=jax.ShapeDtypeStruct((m, n), out_dtype),
        scratch_shapes=[pltpu.VMEM((tm, tn), F32)] if (nk > 1 and not in_place) else [],
        compiler_params=_cparams(3),
    )


def _grad_cols(name, a, parts, *, tm, tk, rider=None):
    k, m = a.shape
    n = sum(p.shape[1] for p in parts)
    assert m % tm == 0 and k % tk == 0
    nk = k // tk

    def body(*refs):
        a_ref, p_refs, o_ref = refs[0], refs[1:1 + len(parts)], refs[1 + len(parts)]
        kk = pl.program_id(1)
        side_by_side = jnp.concatenate([p_ref[...].astype(BF16) for p_ref in p_refs], axis=1)
        term = _dot_tn(a_ref[...].astype(BF16), side_by_side)

        @pl.when(kk == 0)
        def _():
            o_ref[...] = term

        @pl.when(kk > 0)
        def _():
            o_ref[...] += term

    return _call(
        body, [a] + list(parts), rider,
        name=name,
        grid=(m // tm, nk),
        in_specs=[pl.BlockSpec((tk, tm), lambda i, kk: (kk, i))]
        + [pl.BlockSpec((tk, p.shape[1]), lambda i, kk: (kk, 0)) for p in parts],
        out_specs=pl.BlockSpec((tm, n), lambda i, kk: (i, 0)),
        out_shape=jax.ShapeDtypeStruct((m, n), F32),
        compiler_params=_cparams(2),
    )


def _rowk(name, *, a=None, w=None, nt=False, tm, tk=None, rows=(), consts=(), row_outs=(), acc_outs=(), epilogue,
          rider=None):
    has_mm = a is not None
    a_parts = list(a) if isinstance(a, (list, tuple)) else ([a] if has_mm else [])
    n_a = len(a_parts)
    m = a_parts[0].shape[0] if has_mm else rows[0].shape[0]
    assert m % tm == 0
    nm = m // tm
    if has_mm:
        k = sum(p.shape[1] for p in a_parts)
        n = w.shape[0] if nt else w.shape[1]
        tk = min(tk, k)
        assert k % tk == 0 and (n_a == 1 or tk == k)
        nk = k // tk
    else:
        nk = 1
    n_rows, n_consts, n_ro, n_ao = len(rows), len(consts), len(row_outs), len(acc_outs)

    def body(*refs):
        pos = 0
        if has_mm:
            a_refs, w_ref = refs[:n_a], refs[n_a]
            pos = n_a + 1
        row_refs = refs[pos:pos + n_rows]
        pos += n_rows
        const_refs = refs[pos:pos + n_consts]
        pos += n_consts
        ro_refs = refs[pos:pos + n_ro]
        pos += n_ro
        ao_refs = refs[pos:pos + n_ao]
        pos += n_ao
        i = pl.program_id(0)
        kk = pl.program_id(1)

        def finish(acc):
            ro_vals, ao_vals = epilogue(acc, [r[...] for r in row_refs], [c[...] for c in const_refs])
            for r, v in zip(ro_refs, ro_vals):
                r[...] = v.astype(r.dtype)
            for r, v in zip(ao_refs, ao_vals):

                @pl.when(i == 0)
                def _(r=r, v=v):
                    r[...] = v

                @pl.when(i > 0)
                def _(r=r, v=v):
                    r[...] += v

        if not has_mm:
            finish(None)
            return
        part, off = None, 0
        for a_ref in a_refs:
            width = a_ref.shape[1]
            cols = slice(None) if n_a == 1 else slice(off, off + width)
            av = a_ref[...].astype(BF16)
            term = _dot_nt(av, w_ref[:, cols]) if nt else _dot(av, w_ref[cols, :])
            part = term if part is None else part + term
            off += width
        if nk == 1:
            finish(part)
        else:
            acc_ref = refs[pos]

            @pl.when(kk == 0)
            def _():
                acc_ref[...] = part

            @pl.when(kk > 0)
            def _():
                acc_ref[...] += part

            @pl.when(kk == nk - 1)
            def _():
                finish(acc_ref[...])

    once = pl.Buffered(1)
    in_specs, args = [], []
    if has_mm:
        for part in a_parts:
            in_specs.append(pl.BlockSpec((tm, tk if n_a == 1 else part.shape[1]), lambda i, kk: (i, kk)))
        w_mode = once if nk == 1 else None
        in_specs.append(pl.BlockSpec((n, tk), lambda i, kk: (0, kk), pipeline_mode=w_mode) if nt
                        else pl.BlockSpec((tk, n), lambda i, kk: (kk, 0), pipeline_mode=w_mode))
        args += a_parts + [w]
    for r in rows:
        in_specs.append(pl.BlockSpec((tm, r.shape[1]), lambda i, kk: (i, 0)))
        args.append(r)
    for c in consts:
        in_specs.append(pl.BlockSpec(c.shape, lambda i, kk: (0,) * c.ndim, pipeline_mode=once))
        args.append(c)
    out_specs, out_shape = [], []
    for width, dt in row_outs:
        out_specs.append(pl.BlockSpec((tm, width), lambda i, kk: (i, 0)))
        out_shape.append(jax.ShapeDtypeStruct((m, width), dt))
    for width in acc_outs:
        out_specs.append(pl.BlockSpec((1, width), lambda i, kk: (0, 0)))
        out_shape.append(jax.ShapeDtypeStruct((1, width), F32))
    return _call(
        body, args, rider,
        name=name,
        grid=(nm, nk),
        in_specs=in_specs,
        out_specs=out_specs,
        out_shape=out_shape,
        scratch_shapes=[pltpu.VMEM((tm, n), F32)] if (has_mm and nk > 1) else [],
        compiler_params=_cparams(2),
    )


def _rms_stats(x):
    r = lax.rsqrt(jnp.mean(x * x, axis=-1, keepdims=True) + RMS_EPS)
    return r, x * r


def _rms_bwd(dh, xh, r, g):
    gy = dh * g
    dx = r * (gy - xh * jnp.mean(gy * xh, axis=-1, keepdims=True))
    return dx, jnp.sum(dh * xh, axis=0, keepdims=True)


def _alibi_slope(head):
    return 2.0 ** (-8.0 * (head + 1) / N_DIL_HEADS)


DIL_STEP_BLOCKS = 4


def _dil_band(first_block):
    qi = lax.broadcasted_iota(jnp.int32, (BLOCK, 2 * BLOCK), 0)
    kj = lax.broadcasted_iota(jnp.int32, (BLOCK, 2 * BLOCK), 1)
    steps = qi + BLOCK - kj
    valid = (steps >= 0) & (steps <= BLOCK)
    if first_block is not False:
        valid = valid & ((kj >= BLOCK) | jnp.logical_not(first_block))
    return steps.astype(F32), valid


def _dil_step_specs(ncb, cols, nblk, clamp):
    def own(col):
        return pl.BlockSpec((nblk * BLOCK, DIL_OUT_WIDTH), lambda r, i: (clamp(i), r * ncb + col))

    def before(col):
        return pl.BlockSpec((BLOCK, DIL_OUT_WIDTH), lambda r, i: (jnp.maximum(clamp(i) * nblk - 1, 0), r * ncb + col))

    return [own(cols[0]), own(cols[1]), before(cols[1]), own(cols[2]), before(cols[2])]


def _dil_fwd(qkv, group):
    window, dilation = DIL_GROUPS[group]
    s = qkv.shape[0]
    sub = s // dilation
    nb = sub // BLOCK
    assert nb * BLOCK * dilation == s and window // dilation == BLOCK
    nblk = min(DIL_STEP_BLOCKS, nb)
    assert nb % nblk == 0
    slopes = [_alibi_slope(group * DIL_HEADS_PER_GROUP + h) * dilation for h in range(DIL_HEADS_PER_GROUP)]

    def body(q_ref, kc_ref, kp_ref, vc_ref, vp_ref, o_ref, lse_ref):
        i = pl.program_id(1)
        kk_all = jnp.concatenate([kp_ref[...], kc_ref[...]], axis=0)
        vv_all = jnp.concatenate([vp_ref[...], vc_ref[...]], axis=0)
        head_id = lax.broadcasted_iota(jnp.int32, (1, DIL_OUT_WIDTH), 1) // HEAD_DIM
        chains = [(b, h) for b in range(nblk) for h in range(DIL_HEADS_PER_GROUP)]
        rows = lambda b: slice(b * BLOCK, (b + 1) * BLOCK)
        keys = lambda b: slice(b * BLOCK, (b + 2) * BLOCK)
        bands = [_dil_band(i == 0 if b == 0 else False) for b in range(nblk)]
        qs = [q_ref[rows(b), :] for b in range(nblk)]
        scores = [_dot_nt(jnp.where(head_id == h, qs[b], jnp.zeros_like(qs[b])), kk_all[keys(b)]) for b, h in chains]
        ps, lses = [], []
        for (b, h), sc in zip(chains, scores):
            steps, valid = bands[b]
            logits = jnp.where(valid, sc * (1.0 / math.sqrt(HEAD_DIM)) - slopes[h] * steps, NEG_INF)
            mx = jnp.max(logits, axis=1, keepdims=True)
            e = jnp.exp(logits - mx)
            den = jnp.sum(e, axis=1, keepdims=True)
            lses.append(mx + jnp.log(den))
            ps.append((e * (1.0 / den)).astype(BF16))
        outs = [_dot(p, vv_all[keys(b)]) for (b, h), p in zip(chains, ps)]
        for b in range(nblk):
            mine = [n for n, ch in enumerate(chains) if ch[0] == b]
            o, lse_all = outs[mine[0]], lses[mine[0]]
            for n in mine[1:]:
                o = jnp.where(head_id == chains[n][1], outs[n], o)
                lse_all = jnp.where(head_id == chains[n][1], lses[n], lse_all)
            o_ref[rows(b), :] = o
            lse_ref[rows(b), :] = jnp.broadcast_to(lse_all, o.shape)

    qkv_v, ncb, cols = _dil_view(qkv, group)
    out_spec = pl.BlockSpec((nblk * BLOCK, DIL_OUT_WIDTH), lambda r, i: (i, r))
    o, lse = _pcall(
        body,
        name=f"dil_fwd_g{group}",
        grid=(dilation, nb // nblk),
        in_specs=_dil_step_specs(ncb, cols, nblk, lambda i: i),
        out_specs=[out_spec, out_spec],
        out_shape=[jax.ShapeDtypeStruct((sub, dilation * DIL_OUT_WIDTH), F32)] * 2,
        compiler_params=_cparams(2),
    )(qkv_v, qkv_v, qkv_v, qkv_v, qkv_v)
    return o.reshape(s, DIL_OUT_WIDTH), lse.reshape(s, DIL_OUT_WIDTH), lse


def _dil_bwd(qkv, do_g, lse_g, dterm_g, group, rider=None):
    window, dilation = DIL_GROUPS[group]
    s = qkv.shape[0]
    sub = s // dilation
    nb = sub // BLOCK
    nblk = min(DIL_STEP_BLOCKS, nb)
    n_steps = nb // nblk
    slopes = [_alibi_slope(group * DIL_HEADS_PER_GROUP + h) * dilation for h in range(DIL_HEADS_PER_GROUP)]
    scale = 1.0 / math.sqrt(HEAD_DIM)
    tail = slice((nblk - 1) * BLOCK, nblk * BLOCK)

    def body(q_ref, kc_ref, kp_ref, vc_ref, vp_ref, do_ref, lse_ref, dt_ref, dq_ref, dk_ref, dv_ref, ck_ref, cv_ref):
        i = pl.program_id(1)

        @pl.when(i == 0)
        def _():
            ck_ref[...] = jnp.zeros_like(ck_ref)
            cv_ref[...] = jnp.zeros_like(cv_ref)

        @pl.when(i < n_steps)
        def _():
            kk_all = jnp.concatenate([kp_ref[...], kc_ref[...]], axis=0)
            vv_all = jnp.concatenate([vp_ref[...], vc_ref[...]], axis=0)
            lane = lax.broadcasted_iota(jnp.int32, (1, DIL_OUT_WIDTH), 1)
            head_id = lane // HEAD_DIM
            chains = [(b, h) for b in range(nblk) for h in range(DIL_HEADS_PER_GROUP)]
            rows = lambda b: slice(b * BLOCK, (b + 1) * BLOCK)
            keys = lambda b: slice(b * BLOCK, (b + 2) * BLOCK)
            bands = [_dil_band(i == 0 if b == 0 else False) for b in range(nblk)]
            qms, doms = [], []
            for b, h in chains:
                q, do = q_ref[rows(b), :], do_ref[rows(b), :]
                qms.append(jnp.where(head_id == h, q, jnp.zeros_like(q)))
                doms.append(jnp.where(head_id == h, do, jnp.zeros_like(do)))
            scores = [_dot_nt(qm, kk_all[keys(b)]) for (b, h), qm in zip(chains, qms)]
            dps = [_dot_nt(dom, vv_all[keys(b)]) for (b, h), dom in zip(chains, doms)]
            pbs, dss = [], []
            for n, (b, h) in enumerate(chains):
                steps, valid = bands[b]
                first = lane == h * HEAD_DIM
                lse = jnp.sum(jnp.where(first, lse_ref[rows(b), :], 0.0), axis=1, keepdims=True)
                dt = jnp.sum(jnp.where(first, dt_ref[rows(b), :], 0.0), axis=1, keepdims=True)
                logits = jnp.where(valid, scores[n] * scale - slopes[h] * steps, NEG_INF)
                p = jnp.where(valid, jnp.exp(logits - lse), 0.0)
                pbs.append(p.astype(BF16))
                dss.append((p * (dps[n] + dt) * scale).astype(BF16))
            dqs = [_dot(ds, kk_all[keys(b)]) for (b, h), ds in zip(chains, dss)]
            dks = [_dot_tn(ds, qm) for ds, qm in zip(dss, qms)]
            dvs = [_dot_tn(pb, dom) for pb, dom in zip(pbs, doms)]
            dkk, dvv = [], []
            for b in range(nblk):
                mine = [n for n, ch in enumerate(chains) if ch[0] == b]
                dq = dqs[mine[0]]
                for n in mine[1:]:
                    dq = jnp.where(head_id == chains[n][1], dqs[n], dq)
                dq_ref[rows(b), :] = dq.astype(dq_ref.dtype)
                dkk.append((dks[mine[0]] + dks[mine[1]]) + (dks[mine[2]] + dks[mine[3]]))
                dvv.append((dvs[mine[0]] + dvs[mine[1]]) + (dvs[mine[2]] + dvs[mine[3]]))
            for out_ref, carry_ref, parts in ((dk_ref, ck_ref, dkk), (dv_ref, cv_ref, dvv)):
                if nblk > 1:
                    out_ref[: (nblk - 1) * BLOCK, :] = carry_ref[: (nblk - 1) * BLOCK, :].astype(out_ref.dtype)
                out_ref[tail, :] = (carry_ref[tail, :] + parts[0][:BLOCK]).astype(out_ref.dtype)
                for b in range(nblk):
                    own = parts[b][BLOCK:]
                    carry_ref[rows(b), :] = own + parts[b + 1][:BLOCK] if b + 1 < nblk else own

        @pl.when(i == n_steps)
        def _():
            dk_ref[...] = ck_ref[...].astype(dk_ref.dtype)
            dv_ref[...] = cv_ref[...].astype(dv_ref.dtype)

    clamp = lambda i: jnp.minimum(i, n_steps - 1)
    qkv_v, ncb, cols = _dil_view(qkv, group)
    view = lambda t: t.reshape(sub, dilation * DIL_OUT_WIDTH)
    row_spec = pl.BlockSpec((nblk * BLOCK, DIL_OUT_WIDTH), lambda r, i: (clamp(i), r))
    late_spec = pl.BlockSpec((nblk * BLOCK, DIL_OUT_WIDTH), lambda r, i: (jnp.maximum(i - 1, 0), r))
    res = _call(
        body, (qkv_v, qkv_v, qkv_v, qkv_v, qkv_v, view(do_g), view(lse_g), view(dterm_g)), rider,
        name=f"dil_bwd_g{group}",
        grid=(dilation, n_steps + 1),
        in_specs=_dil_step_specs(ncb, cols, nblk, clamp) + [row_spec, row_spec, row_spec],
        out_specs=[row_spec, late_spec, late_spec],
        out_shape=[jax.ShapeDtypeStruct((sub, dilation * DIL_OUT_WIDTH), BF16)] * 3,
        scratch_shapes=[pltpu.VMEM((nblk * BLOCK, DIL_OUT_WIDTH), F32)] * 2,
        compiler_params=_cparams(2),
    )
    grads, lands = res if rider is not None else (res, None)
    grads = tuple(g.reshape(s, DIL_OUT_WIDTH) for g in grads)
    return grads if rider is None else (grads, lands)


def _dil_view(qkv, group):
    _, dilation = DIL_GROUPS[group]
    if dilation == 1:
        return qkv, QKV_COLS // DIL_OUT_WIDTH, (group, 3 + group, 6 + group)
    w = DIL_OUT_WIDTH
    own = jnp.concatenate([qkv[:, (3 * part + group) * w:(3 * part + group + 1) * w] for part in range(3)], axis=1)
    return own.reshape(qkv.shape[0] // dilation, dilation * 3 * w), 3, (0, 1, 2)


def _head_block_ones():
    r = lax.broadcasted_iota(jnp.int32, (DIL_OUT_WIDTH, DIL_OUT_WIDTH), 0) // HEAD_DIM
    c = lax.broadcasted_iota(jnp.int32, (DIL_OUT_WIDTH, DIL_OUT_WIDTH), 1) // HEAD_DIM
    return jnp.where(r == c, 1.0, 0.0).astype(BF16)


def _dil_mix_weights(l0, l1, l2):
    mx = jnp.maximum(jnp.maximum(l0, l1), l2)
    e0, e1, e2 = jnp.exp(l0 - mx), jnp.exp(l1 - mx), jnp.exp(l2 - mx)
    inv = 1.0 / (e0 + e1 + e2)
    return e0 * inv, e1 * inv, e2 * inv


def _dil_mix_fwd(os_, lses, tm):
    def epi(_, rows, consts):
        o0, o1, o2, l0, l1, l2 = rows
        w0, w1, w2 = _dil_mix_weights(l0, l1, l2)
        return [w0 * o0 + w1 * o1 + w2 * o2], []

    (o_a,) = _rowk("dil_mix_fwd", tm=tm, rows=list(os_) + list(lses), row_outs=[(DIL_OUT_WIDTH, BF16)], epilogue=epi)
    return o_a


def _dil_mix_bwd(do_a, os_, lses, tm):
    def epi(_, rows, consts):
        do, o0, o1, o2, l0, l1, l2 = rows
        do = do.astype(F32)
        w0, w1, w2 = _dil_mix_weights(l0, l1, l2)
        mixed = w0 * o0 + w1 * o1 + w2 * o2
        tot = _dot_hi_lo(do * mixed, _head_block_ones())
        return [w0 * do, w1 * do, w2 * do, -w0 * tot, -w1 * tot, -w2 * tot], []

    return _rowk(
        "dil_mix_bwd", tm=tm, rows=[do_a] + list(os_) + list(lses),
        row_outs=[(DIL_OUT_WIDTH, BF16)] * 3 + [(DIL_OUT_WIDTH, F32)] * 3, epilogue=epi)


_SB_Q0 = 3 * DIL_WIDTH // LANES
_SB_K0 = _SB_Q0 + SB_WIDTH // LANES
_SB_V0 = _SB_K0 + SB_WIDTH // LANES


_EXP_CLAMP = 88.0
_SB_DEAD = 104.0


def _tri(t, op):
    r = lax.broadcasted_iota(jnp.int32, (t, t), 0)
    c = lax.broadcasted_iota(jnp.int32, (t, t), 1)
    return jnp.where(op(r, c), 1.0, 0.0).astype(BF16)


def _softplus(z):
    return jnp.maximum(z, jnp.log(1.0 + jnp.exp(jnp.minimum(z, _EXP_CLAMP))))


def _sb_chain_head(qm, kj, mask):
    z = _dot_nt(qm, kj)
    sp = _softplus(z)
    return (sp if mask is None else jnp.where(mask, sp, 0.0)), z - sp


def _sb_fwd(qkv, rider=None):
    s = qkv.shape[0]
    t = SB_TK
    assert s % (2 * t) == 0
    nq = s // (2 * t)
    n_pairs = SB_WIDTH // LANES

    def body(q_ref, k_ref, v_ref, o_ref, tot_ref, steps_ref):
        p, i = pl.program_id(0), pl.program_id(1)
        lane_hi = lax.broadcasted_iota(jnp.int32, (1, LANES), 1) // HEAD_DIM
        later = _tri(t, lambda r, c: r > c)
        causal = lax.broadcasted_iota(jnp.int32, (t, t), 1) < lax.broadcasted_iota(jnp.int32, (t, t), 0)
        qms = []
        for x in range(2):
            q = q_ref[pl.ds(x * t, t), :] * (1.0 / math.sqrt(HEAD_DIM))
            qms.append([jnp.where(lane_hi == hh, q, jnp.zeros_like(q)) for hh in range(2)])

        def tile(j):
            off = pl.multiple_of(j * t, t)
            return k_ref[pl.ds(off, t), :], v_ref[pl.ds(off, t), :]

        def step(groups, carry):
            kv = [tile(j) for _, j, _ in groups]
            chains = [(g, x, hh) for g, (x, _, _) in enumerate(groups) for hh in range(2)]
            heads = [_sb_chain_head(qms[x][hh], kv[g][0], causal if groups[g][2] else None) for g, x, hh in chains]
            sufs = [_dot(sp.astype(BF16), later) for sp, _ in heads]
            cur = [list(carry[0]), list(carry[1])]
            for (g, x, hh), (sp, lpos), suf in zip(chains, heads, sufs):
                c, acc = cur[x][hh]
                a = jnp.exp(lpos - suf - c)
                if groups[g][2]:
                    a = jnp.where(causal, a, 0.0)
                cur[x][hh] = (c + jnp.sum(sp, axis=1, keepdims=True), acc + _dot(a.astype(BF16), kv[g][1]))
            return (tuple(cur[0]), tuple(cur[1]))

        def lowest(carry):
            m = [jnp.min(carry[x][hh][0]) for x in range(2) for hh in range(2)]
            return jnp.minimum(jnp.minimum(m[0], m[1]), jnp.minimum(m[2], m[3]))

        zero = (jnp.zeros((t, 1), F32), jnp.zeros((t, LANES), F32))
        start = ((zero, zero), (zero, zero))
        carry = lax.cond(
            i == 0,
            lambda ca: step([(0, 0, True), (1, 1, True), (1, 0, False)], ca),
            lambda ca: step([(0, 2 * i, True), (1, 2 * i + 1, True), (0, 2 * i - 1, False), (1, 2 * i, False)], ca),
            start)

        n_more, carry = lax.while_loop(
            lambda st: jnp.logical_and(st[0] + 1 < 2 * i, lowest(st[1]) <= _SB_DEAD),
            lambda st: (st[0] + 1, step([(0, 2 * i - 2 - st[0], False), (1, 2 * i - 1 - st[0], False)], st[1])),
            (jnp.int32(0), carry))
        b_last = jnp.logical_and(jnp.logical_and(i > 0, n_more + 1 == 2 * i), lowest(carry) <= _SB_DEAD)
        carry = lax.cond(b_last, lambda ca: step([(1, 0, False)], ca), lambda ca: ca, carry)
        for x in range(2):
            (c0, acc0), (c1, acc1) = carry[x]
            o_ref[pl.ds(x * t, t), :] = jnp.where(lane_hi == 0, acc0, acc1).astype(o_ref.dtype)
            tot_ref[pl.ds(x * t, t), :] = jnp.where(lane_hi == 0, c0, c1)
        steps_ref[p, i] = 1 + n_more + b_last.astype(jnp.int32)

    return _call(
        body, (qkv, qkv, qkv), rider,
        name="sb_fwd",
        grid=(n_pairs, nq),
        in_specs=[
            pl.BlockSpec((2 * t, LANES), lambda p, i: (i, _SB_Q0 + p)),
            pl.BlockSpec((s, LANES), lambda p, i: (0, _SB_K0 + p)),
            pl.BlockSpec((s, LANES), lambda p, i: (0, _SB_V0 + p)),
        ],
        out_specs=[pl.BlockSpec((2 * t, LANES), lambda p, i: (i, p))] * 2 + [pl.BlockSpec(memory_space=pltpu.SMEM)],
        out_shape=[jax.ShapeDtypeStruct((s, SB_WIDTH), BF16), jax.ShapeDtypeStruct((s, SB_WIDTH), F32),
                   jax.ShapeDtypeStruct((n_pairs, nq), jnp.int32)],
        compiler_params=_cparams(2),
    )


def _sb_bwd(qkv, do_b, tot_b, n_steps):
    s = qkv.shape[0]
    t = SB_TK
    nq = s // (2 * t)
    n_pairs = SB_WIDTH // LANES
    scale = 1.0 / math.sqrt(HEAD_DIM)

    def body(steps_ref, q_ref, k_ref, v_ref, do_ref, tot_ref, dq_ref, dk_ref, dv_ref):
        p, i = pl.program_id(0), pl.program_id(1)

        @pl.when(i == 0)
        def _():
            dk_ref[...] = jnp.zeros_like(dk_ref)
            dv_ref[...] = jnp.zeros_like(dv_ref)

        lane = lax.broadcasted_iota(jnp.int32, (1, LANES), 1)
        lane_hi = lane // HEAD_DIM
        later = _tri(t, lambda r, c: r > c)
        before = _tri(t, lambda r, c: r < c)
        causal = lax.broadcasted_iota(jnp.int32, (t, t), 1) < lax.broadcasted_iota(jnp.int32, (t, t), 0)
        qms, doms, tots = [], [], []
        for x in range(2):
            rows = pl.ds(x * t, t)
            q, do, tot_all = q_ref[rows, :] * scale, do_ref[rows, :], tot_ref[rows, :]
            qms.append([jnp.where(lane_hi == hh, q, jnp.zeros_like(q)) for hh in range(2)])
            doms.append([jnp.where(lane_hi == hh, do, jnp.zeros_like(do)) for hh in range(2)])
            tots.append([jnp.sum(jnp.where(lane == hh * HEAD_DIM, tot_all, 0.0), axis=1, keepdims=True)
                         for hh in range(2)])

        def step(groups, carry):
            offs = [pl.multiple_of(j * t, t) for _, j, _ in groups]
            ks = [k_ref[pl.ds(off, t), :] for off in offs]
            vs = [v_ref[pl.ds(off, t), :] for off in offs]
            chains = [(g, x, hh) for g, (x, _, _) in enumerate(groups) for hh in range(2)]
            heads = [_sb_chain_head(qms[x][hh], ks[g], causal if groups[g][2] else None) for g, x, hh in chains]
            sufs = [_dot(sp.astype(BF16), later) for sp, _ in heads]
            das = [_dot_nt(doms[x][hh], vs[g]) for g, x, hh in chains]
            cur = [list(carry[0]), list(carry[1])]
            sigs, gs, abs_, cg_before = [], [], [], []
            for (g_, x, hh), (sp, lpos), suf, da in zip(chains, heads, sufs, das):
                cl, cg, dq = cur[x][hh]
                cl = cl + jnp.sum(sp, axis=1, keepdims=True)
                sig = jnp.exp(lpos)
                a = sig * jnp.exp(-suf - (tots[x][hh] - cl))
                if groups[g_][2]:
                    a = jnp.where(causal, a, 0.0)
                g = a * da
                sigs.append(sig)
                gs.append(g)
                abs_.append(a.astype(BF16))
                cg_before.append(cg)
                cur[x][hh] = (cl, cg + jnp.sum(g, axis=1, keepdims=True), dq)
            prefs = [_dot(g.astype(BF16), before) for g in gs]
            dvs = [_dot_tn(ab, doms[x][hh]) for (_, x, hh), ab in zip(chains, abs_)]
            dzs = []
            for (g_, x, hh), sig, g, pref, cg in zip(chains, sigs, gs, prefs, cg_before):
                dz = g - sig * (g + pref + cg)
                if groups[g_][2]:
                    dz = jnp.where(causal, dz, 0.0)
                dzs.append(dz.astype(BF16))
            dqs = [_dot(dz, ks[g_]) for (g_, x, hh), dz in zip(chains, dzs)]
            dks = [_dot_tn(dz, qms[x][hh]) for (_, x, hh), dz in zip(chains, dzs)]
            for n, (_, x, hh) in enumerate(chains):
                cl, cg, dq = cur[x][hh]
                cur[x][hh] = (cl, cg, dq + dqs[n])
            for g_, off in enumerate(offs):
                dk_ref[pl.ds(off, t), :] += dks[2 * g_] + dks[2 * g_ + 1]
                dv_ref[pl.ds(off, t), :] += dvs[2 * g_] + dvs[2 * g_ + 1]
            return (tuple(cur[0]), tuple(cur[1]))

        taken = steps_ref[p, i]
        n_full = jnp.minimum(taken, 2 * i)
        zero = (jnp.zeros((t, 1), F32), jnp.zeros((t, 1), F32), jnp.zeros((t, LANES), F32))
        carry = ((zero, zero), (zero, zero))
        carry = lax.cond(jnp.logical_and(i > 0, taken > 2 * i), lambda ca: step([(1, 0, False)], ca), lambda ca: ca,
                         carry)
        carry = lax.fori_loop(
            0, n_full - 1,
            lambda n, ca: step([(0, 2 * i - n_full + n, False), (1, 2 * i + 1 - n_full + n, False)], ca), carry)
        carry = lax.cond(
            i == 0,
            lambda ca: step([(1, 0, False), (0, 0, True), (1, 1, True)], ca),
            lambda ca: step([(0, 2 * i - 1, False), (1, 2 * i, False), (0, 2 * i, True), (1, 2 * i + 1, True)], ca),
            carry)
        for x in range(2):
            dq = jnp.where(lane_hi == 0, carry[x][0][2], carry[x][1][2])
            dq_ref[pl.ds(x * t, t), :] = (dq * scale).astype(dq_ref.dtype)

    row_spec = pl.BlockSpec((2 * t, LANES), lambda p, i, ns: (i, p))
    full_spec = pl.BlockSpec((s, LANES), lambda p, i, ns: (0, p))
    return _pcall(
        body,
        name="sb_bwd",
        grid_spec=pltpu.PrefetchScalarGridSpec(
            num_scalar_prefetch=1,
            grid=(n_pairs, nq),
            in_specs=[
                pl.BlockSpec((2 * t, LANES), lambda p, i, ns: (i, _SB_Q0 + p)),
                pl.BlockSpec((s, LANES), lambda p, i, ns: (0, _SB_K0 + p)),
                pl.BlockSpec((s, LANES), lambda p, i, ns: (0, _SB_V0 + p)),
                row_spec, row_spec,
            ],
            out_specs=[row_spec, full_spec, full_spec],
        ),
        out_shape=[jax.ShapeDtypeStruct((s, SB_WIDTH), BF16), jax.ShapeDtypeStruct((s, SB_WIDTH), F32),
                   jax.ShapeDtypeStruct((s, SB_WIDTH), F32)],
        compiler_params=_cparams(2),
    )(n_steps, qkv, qkv, qkv, do_b, tot_b)


def _gates(gl, bg):
    return _sigmoid(gl[:, :D_MODEL] + bg[:, :D_MODEL]), _sigmoid(gl[:, D_MODEL:] + bg[:, D_MODEL:])


def _mixer_fwd(o_a, o_b, gl, x0, bg, g2, w_ud, w_us, w_out, tm):
    def epi(_, rows, consts):
        oa, ob, glv, x = rows
        bgv, g2v, wud, wus, wout = consts
        ga, gb = _gates(glv, bgv)
        merged = ga * _dot(oa, wud) + gb * _dot(ob, wus)
        x1 = x + _dot(merged.astype(BF16), wout)
        r, xh = _rms_stats(x1)
        return [x1, xh * g2v], []

    return _rowk("mixer_fwd", tm=tm, rows=[o_a, o_b, gl, x0], consts=[bg, g2, w_ud, w_us, w_out],
                 row_outs=[(D_MODEL, F32), (D_MODEL, BF16)], epilogue=epi)


def _mixer_bwd(dx1, o_a, o_b, gl, bg, w_ud, w_us, w_out, tm, rider=None):
    s = dx1.shape[0]
    nm = s // tm

    def body(dx_ref, oa_ref, ob_ref, gl_ref, bg_ref, wud_ref, wus_ref, wout_ref,
             doa_ref, dob_ref, dgl_ref, gwout_ref, gwud_ref, gwus_ref, gbg_ref):
        i = pl.program_id(0)
        dxb = dx_ref[...].astype(BF16)
        oa, ob = oa_ref[...], ob_ref[...]
        ga, gb = _gates(gl_ref[...], bg_ref[...])
        ua, ub = _dot(oa, wud_ref[...]), _dot(ob, wus_ref[...])
        merged = (ga * ua + gb * ub).astype(BF16)
        dm = _dot_nt(dxb, wout_ref[...])
        dua = (dm * ga).astype(BF16)
        dub = (dm * gb).astype(BF16)
        dgla = dm * ua * ga * (1.0 - ga)
        dglb = dm * ub * gb * (1.0 - gb)
        doa_ref[...] = _dot_nt(dua, wud_ref[...]).astype(doa_ref.dtype)
        dob_ref[...] = _dot_nt(dub, wus_ref[...]).astype(dob_ref.dtype)
        dgl_ref[:, :D_MODEL] = dgla.astype(dgl_ref.dtype)
        dgl_ref[:, D_MODEL:] = dglb.astype(dgl_ref.dtype)
        parts = [(gwout_ref, _dot_tn(merged, dxb)), (gwud_ref, _dot_tn(oa, dua)), (gwus_ref, _dot_tn(ob, dub))]
        for r, v in parts:

            @pl.when(i == 0)
            def _(r=r, v=v):
                r[...] = v

            @pl.when(i > 0)
            def _(r=r, v=v):
                r[...] += v

        sa = jnp.sum(dgla, axis=0, keepdims=True)
        sb = jnp.sum(dglb, axis=0, keepdims=True)

        @pl.when(i == 0)
        def _():
            gbg_ref[:, :D_MODEL] = sa
            gbg_ref[:, D_MODEL:] = sb

        @pl.when(i > 0)
        def _():
            gbg_ref[:, :D_MODEL] += sa
            gbg_ref[:, D_MODEL:] += sb

    row = lambda w: pl.BlockSpec((tm, w), lambda i: (i, 0))
    full = lambda a: pl.BlockSpec(a.shape, lambda i: (0, 0), pipeline_mode=pl.Buffered(1))
    fshape = lambda r, c: jax.ShapeDtypeStruct((r, c), F32)
    return _call(
        body, (dx1, o_a, o_b, gl, bg, w_ud, w_us, w_out), rider,
        name="mixer_bwd",
        grid=(nm,),
        in_specs=[row(D_MODEL), row(DIL_OUT_WIDTH), row(SB_WIDTH), row(2 * D_MODEL),
                  full(bg), full(w_ud), full(w_us), full(w_out)],
        out_specs=[row(DIL_OUT_WIDTH), row(SB_WIDTH), row(2 * D_MODEL),
                   pl.BlockSpec((D_MODEL, D_MODEL), lambda i: (0, 0)),
                   pl.BlockSpec((DIL_OUT_WIDTH, D_MODEL), lambda i: (0, 0)),
                   pl.BlockSpec((SB_WIDTH, D_MODEL), lambda i: (0, 0)),
                   pl.BlockSpec((1, 2 * D_MODEL), lambda i: (0, 0))],
        out_shape=[jax.ShapeDtypeStruct((s, DIL_OUT_WIDTH), BF16), jax.ShapeDtypeStruct((s, SB_WIDTH), BF16),
                   jax.ShapeDtypeStruct((s, 2 * D_MODEL), BF16),
                   fshape(D_MODEL, D_MODEL), fshape(DIL_OUT_WIDTH, D_MODEL), fshape(SB_WIDTH, D_MODEL),
                   fshape(1, 2 * D_MODEL)],
        compiler_params=_cparams(1),
    )


def _all_gather(shards):
    n = len(shards)

    def body(*refs):
        x_refs, out_refs = refs[:n], refs[n:2 * n]
        send_sems, recv_sems, local_sems = refs[2 * n:]
        x, y, c = lax.axis_index("x"), lax.axis_index("y"), lax.axis_index("c")
        me, sibling = (x, y, c), (x, y, 1 - c)
        chips = [(1 - x, y), (x, 1 - y), (1 - x, 1 - y)]

        def slot(a, px, py, pc):
            return out_refs[a].at[4 * px + 2 * py + pc]

        def copy(a, k, block, to, own=False):
            return pltpu.make_async_remote_copy(
                src_ref=x_refs[a] if own else slot(a, *block), dst_ref=slot(a, *block),
                send_sem=send_sems.at[7 * a + k], recv_sem=recv_sems.at[7 * a + k], device_id=to, device_id_type=_MESH)

        mine = [pltpu.make_async_copy(x_refs[a], slot(a, *me), local_sems.at[a]) for a in range(n)]
        for cp in mine:
            cp.start()
        first = []
        for a in range(n):
            first.append(copy(a, 0, me, sibling, own=True))
            first += [copy(a, 1 + j, me, (*chip, c), own=True) for j, chip in enumerate(chips)]
        for cp in first:
            cp.start()
        passed = []
        for a in range(n):
            for j, chip in enumerate(chips):
                copy(a, 1 + j, (*chip, c), me).wait_recv()
                passed.append(copy(a, 4 + j, (*chip, c), sibling))
                passed[-1].start()
        for a in range(n):
            copy(a, 0, sibling, me).wait_recv()
            for j, chip in enumerate(chips):
                copy(a, 4 + j, (*chip, 1 - c), me).wait_recv()
        for cp in first + passed:
            cp.wait_send()
        for cp in mine:
            cp.wait()

    return _pcall(
        body,
        name="all_gather_weights",
        in_specs=[_HBM] * n,
        out_specs=[_HBM] * n,
        out_shape=[jax.ShapeDtypeStruct((N_DEV,) + s.shape, s.dtype) for s in shards],
        scratch_shapes=[pltpu.SemaphoreType.DMA((7 * n,)), pltpu.SemaphoreType.DMA((7 * n,)),
                        pltpu.SemaphoreType.DMA((n,))],
    )(*shards)


def _exchange(chunks):
    n = len(chunks)

    def body(*refs):
        g_refs, o_refs = refs[:n], refs[n:2 * n]
        send_sems, recv_sems, local_sems = refs[2 * n:]
        x, y, c = lax.axis_index("x"), lax.axis_index("y"), lax.axis_index("c")
        me = 4 * x + 2 * y + c
        own = [pltpu.make_async_copy(g_refs[a].at[me], o_refs[a].at[me], local_sems.at[a]) for a in range(n)]
        for cp in own:
            cp.start()
        copies = []
        for a in range(n):
            for k in range(1, N_DEV):
                px, py, pc = x ^ (k >> 2), y ^ ((k >> 1) & 1), c ^ (k & 1)
                peer = 4 * px + 2 * py + pc
                copies.append(pltpu.make_async_remote_copy(
                    src_ref=g_refs[a].at[peer], dst_ref=o_refs[a].at[me], send_sem=send_sems.at[7 * a + k - 1],
                    recv_sem=recv_sems.at[7 * a + k - 1], device_id=(px, py, pc), device_id_type=_MESH))
        for cp in copies:
            cp.start()
        for cp in copies:
            cp.wait()
        for cp in own:
            cp.wait()

    return _pcall(
        body,
        name="exchange_grads",
        in_specs=[_HBM] * n,
        out_specs=[_HBM] * n,
        out_shape=[jax.ShapeDtypeStruct(g.shape, g.dtype) for g in chunks],
        scratch_shapes=[pltpu.SemaphoreType.DMA((7 * n,)), pltpu.SemaphoreType.DMA((7 * n,)),
                        pltpu.SemaphoreType.DMA((n,))],
    )(*chunks)


def _reduce_adamw(name, parts, w, m, v, tr):
    _, rows, cols = parts.shape
    tr = min(tr, rows)
    assert rows % tr == 0
    c1 = 1.0 / (1.0 - ADAM_B1 ** ADAM_STEP)
    c2 = 1.0 / (1.0 - ADAM_B2 ** ADAM_STEP)

    def body(p_ref, w_ref, m_ref, v_ref, g_out, d_out, m_out, v_out):
        g = p_ref[0].astype(F32)
        for d in range(1, N_DEV):
            g = g + p_ref[d].astype(F32)
        mn = ADAM_B1 * m_ref[...] + (1.0 - ADAM_B1) * g
        vn = ADAM_B2 * v_ref[...] + (1.0 - ADAM_B2) * (g * g)
        g_out[...] = g
        m_out[...] = mn
        v_out[...] = vn
        d_out[...] = -ADAM_LR * ((mn * c1) / (jnp.sqrt(vn * c2) + ADAM_EPS) + ADAM_WD * w_ref[...])

    spec = pl.BlockSpec((tr, cols), lambda i: (i, 0))
    return _pcall(
        body,
        name=name,
        grid=(rows // tr,),
        in_specs=[pl.BlockSpec((N_DEV, tr, cols), lambda i: (0, i, 0)), spec, spec, spec],
        out_specs=[spec] * 4,
        out_shape=[jax.ShapeDtypeStruct((rows, cols), F32)] * 4,
        compiler_params=_cparams(1),
    )(parts, w, m, v)


_SHARDED = ("w_in", "w_up_dil", "w_up_sb", "w_out", "w_mlp_in", "w_mlp_out")
_FULL_SHAPES = {"w_in": (D_MODEL, IN_COLS), "w_up_dil": (DIL_OUT_WIDTH, D_MODEL), "w_up_sb": (SB_WIDTH, D_MODEL),
                "w_out": (D_MODEL, D_MODEL), "w_mlp_in": (D_MODEL, D_FF), "w_mlp_out": (D_FF, D_MODEL)}
_ROW_SHARDED = ("w_out", "w_mlp_out")


def _shard_shape(name):
    r, c = _FULL_SHAPES[name]
    return (r // N_DEV, c) if name in _ROW_SHARDED else (r, c // N_DEV)


def _assemble(name, gathered):
    r, c = _shard_shape(name)
    if name in _ROW_SHARDED:
        return gathered.reshape(N_DEV * r, c)
    return gathered.transpose(1, 0, 2).reshape(r, N_DEV * c)


def _chunk(name, full):
    r, c = _shard_shape(name)
    if name in _ROW_SHARDED:
        return full.reshape(N_DEV, r, c)
    return full.reshape(r, N_DEV, c).transpose(1, 0, 2)


_SMALL = (("norm_mix_g", D_MODEL), ("b_gate", 2 * D_MODEL), ("norm_mlp_g", D_MODEL), ("norm_final_g", D_MODEL))
_SMALL_N = sum(n for _, n in _SMALL) + LANES


def _pack_small(vals, tail):
    return jnp.concatenate([vals[n].reshape(1, -1) for n, _ in _SMALL] + [tail], axis=1)


def _unpack_small(vec, shapes):
    out, pos = {}, 0
    for n, width in _SMALL:
        out[n] = vec[:, pos:pos + width].reshape(shapes[n])
        pos += width
    return out, vec[:, pos:]


def kernel(x, norm_mix_g, w_in, b_gate, w_up_dil, w_up_sb, w_out, norm_mlp_g, w_mlp_in, w_mlp_out, norm_final_g, loss_target, m_norm_mix_g, m_w_in, m_b_gate, m_w_up_dil, m_w_up_sb, m_w_out, m_norm_mlp_g, m_w_mlp_in, m_w_mlp_out, m_norm_final_g, v_norm_mix_g, v_w_in, v_b_gate, v_w_up_dil, v_w_up_sb, v_w_out, v_norm_mlp_g, v_w_mlp_in, v_w_mlp_out, v_norm_final_g):
    given = dict(locals())
    s = x.shape[1]
    x0 = x.reshape(s, D_MODEL)
    target = loss_target.reshape(s, D_MODEL)
    g1 = norm_mix_g.reshape(1, D_MODEL)
    g2 = norm_mlp_g.reshape(1, D_MODEL)
    g3 = norm_final_g.reshape(1, D_MODEL)
    bg = b_gate.reshape(1, 2 * D_MODEL)
    w_shards = {n: given[n].reshape(_shard_shape(n)) for n in _SHARDED}
    m_shards = {n: given["m_" + n].reshape(_shard_shape(n)) for n in _SHARDED}
    v_shards = {n: given["v_" + n].reshape(_shard_shape(n)) for n in _SHARDED}

    shard_b = {n: w_shards[n].astype(BF16) for n in _SHARDED}
    (gathered_w_in,) = _all_gather([shard_b["w_in"]])
    w_in_f = _assemble("w_in", gathered_w_in)
    w_qkv, w_gl = w_in_f[:, :QKV_COLS], w_in_f[:, QKV_COLS:]
    full = {}

    def norm1(_, rows, consts):
        _, xh = _rms_stats(rows[0])
        return [xh * consts[0]], []

    (h1,) = _rowk("norm_mix", tm=512, rows=[x0], consts=[g1], row_outs=[(D_MODEL, BF16)], epilogue=norm1)
    riding = ("w_out", "w_up_sb", "w_up_dil")
    qkv, lands = _mm("proj_qkv", h1, w_qkv, out_dtype=BF16, tm=1024, tn=768, tk=D_MODEL,
                     rider=_Spread([shard_b[n] for n in riding], chunked=False))
    full.update({n: _assemble(n, land) for n, land in zip(riding, lands)})
    gl = _mm("proj_gates", h1, w_gl, out_dtype=BF16, tm=1024, tn=1024, tk=D_MODEL)
    dil = [_dil_fwd(qkv, g) for g in range(len(DIL_GROUPS))]
    os_, lses = [d[0] for d in dil], [d[1] for d in dil]
    o_a = _dil_mix_fwd(os_, lses, 512)
    (o_b, tot_b, sb_steps), (land,) = _sb_fwd(qkv, rider=_Spread([shard_b["w_mlp_in"]], chunked=False))
    full["w_mlp_in"] = _assemble("w_mlp_in", land)
    x1, h2 = _mixer_fwd(o_a, o_b, gl, x0, bg, g2, full["w_up_dil"], full["w_up_sb"], full["w_out"], 512)
    f, (land,) = _mm("mlp_in", h2, full["w_mlp_in"], out_dtype=BF16, tm=1024, tn=1024, tk=D_MODEL,
                     epilogue=lambda r, _: jnp.square(jnp.maximum(r, 0.0)),
                     rider=_Spread([shard_b["w_mlp_out"]], chunked=False))
    full["w_mlp_out"] = _assemble("w_mlp_out", land)

    def head(acc, rows, consts):
        x1v, tv = rows
        g3v = consts[0]
        x2 = x1v + acc
        r, xh = _rms_stats(x2)
        diff = xh * g3v - tv
        loss = (0.5 / D_MODEL) * jnp.sum(jnp.sum(diff * diff, axis=0, keepdims=True), axis=1, keepdims=True)
        dy = diff * (1.0 / D_MODEL)
        dx2, dg = _rms_bwd(dy, xh, r, g3v)
        return [dx2, dx2], [dg, jnp.broadcast_to(loss, (1, LANES))]

    dx2, dx2b, gg3, loss_part = _rowk(
        "mlp_out_loss", a=f, w=full["w_mlp_out"], tm=512, tk=D_FF, rows=[x1, target], consts=[g3],
        row_outs=[(D_MODEL, F32), (D_MODEL, BF16)], acc_outs=[D_MODEL, LANES], epilogue=head)

    da = _mm("mlp_out_bwd", dx2b, full["w_mlp_out"], tb=True, out_dtype=BF16, tm=1024, tn=1024, tk=D_MODEL, extra=f,
             epilogue=lambda r, fv: r * (2.0 * jnp.sqrt(fv.astype(F32))))
    g_w_mlp_out = _mm("grad_w_mlp_out", f, dx2b, ta=True, out_dtype=F32, tm=1024, tn=1024, tk=2048)
    g_w_mlp_in = _mm("grad_w_mlp_in", h2, da, ta=True, out_dtype=F32, tm=1024, tn=1024, tk=2048)

    def norm_bwd(acc, rows, consts):
        xv, dres = rows
        r, xh = _rms_stats(xv)
        dx, dg = _rms_bwd(acc, xh, r, consts[0])
        return [dres + dx], [dg]

    bchunk = lambda n, g: _chunk(n, g).astype(BF16)
    parts = {}
    (dx1, gg2), (parts["w_mlp_in"],) = _rowk(
        "mlp_in_bwd", a=da, w=full["w_mlp_in"], nt=True, tm=512, tk=D_FF, rows=[x1, dx2], consts=[g2],
        row_outs=[(D_MODEL, F32)], acc_outs=[D_MODEL], epilogue=norm_bwd,
        rider=_Spread([bchunk("w_mlp_in", g_w_mlp_in)], chunked=True))
    (do_a, do_b, dgl, g_w_out, g_w_ud, g_w_us, g_bg), (parts["w_mlp_out"],) = _mixer_bwd(
        dx1, o_a, o_b, gl, bg, full["w_up_dil"], full["w_up_sb"], full["w_out"], 512,
        rider=_Spread([bchunk("w_mlp_out", g_w_mlp_out)], chunked=True))
    mix = _dil_mix_bwd(do_a, os_, lses, 512)
    dil_b = [_dil_bwd(qkv, mix[g], dil[g][2], mix[3 + g], g) for g in range(2)]
    small_three = {"w_out": g_w_out, "w_up_sb": g_w_us, "w_up_dil": g_w_ud}
    grads, lands = _dil_bwd(qkv, mix[2], dil[2][2], mix[5], 2,
                            rider=_Spread([bchunk(n, g) for n, g in small_three.items()], chunked=True))
    dil_b.append(grads)
    parts.update(dict(zip(small_three, lands)))
    dq_b, dk_b, dv_b = _sb_bwd(qkv, do_b, tot_b, sb_steps)
    dproj = [d[0] for d in dil_b] + [d[1] for d in dil_b] + [d[2] for d in dil_b] + [dq_b, dk_b, dv_b, dgl]
    g_w_in = jnp.concatenate([
        _grad_cols("grad_w_in_dil", h1, dproj[:9], tm=D_MODEL, tk=1024),
        _grad_cols("grad_w_in_sb", h1, dproj[9:12], tm=D_MODEL, tk=1024),
        _grad_cols("grad_w_in_gates", h1, dproj[12:], tm=D_MODEL, tk=1024)], axis=1)
    (grad_x, gg1), (parts["w_in"],) = _rowk(
        "in_proj_bwd", a=dproj, w=w_in_f, nt=True, tm=512, tk=IN_COLS, rows=[x0, dx1], consts=[g1],
        row_outs=[(D_MODEL, F32)], acc_outs=[D_MODEL], epilogue=norm_bwd,
        rider=_Spread([bchunk("w_in", g_w_in)], chunked=True))

    small_part = _pack_small({"norm_mix_g": gg1, "b_gate": g_bg, "norm_mlp_g": gg2, "norm_final_g": gg3}, loss_part)
    (small_parts,) = _exchange([jnp.broadcast_to(small_part[None], (N_DEV, 1, _SMALL_N))])

    tags = ("grad_", "delta_", "new_m_", "new_v_")
    outs = {}
    for n, p in parts.items():
        res = _reduce_adamw("adamw_" + n, p, w_shards[n], m_shards[n], v_shards[n], 128)
        for tag, val in zip(tags, res):
            outs[tag + n] = val.reshape(given[n].shape)
    small_w = _pack_small(given, jnp.zeros((1, LANES), F32))
    small_m = _pack_small({n: given["m_" + n] for n, _ in _SMALL}, jnp.zeros((1, LANES), F32))
    small_v = _pack_small({n: given["v_" + n] for n, _ in _SMALL}, jnp.ones((1, LANES), F32))
    small_res = _reduce_adamw("adamw_replicated", small_parts, small_w, small_m, small_v, 8)

    small_shapes = {n: given[n].shape for n, _ in _SMALL}
    for tag, small in zip(tags, small_res):
        small_vals, tail = _unpack_small(small, small_shapes)
        for n, val in small_vals.items():
            outs[tag + n] = val
        if tag == "grad_":
            loss = tail[0, 0]
    names = ["norm_mix_g", "w_in", "b_gate", "w_up_dil", "w_up_sb", "w_out", "norm_mlp_g", "w_mlp_in", "w_mlp_out",
             "norm_final_g"]
    return (loss, grad_x.reshape(x.shape), *[outs["grad_" + n] for n in names], *[outs["delta_" + n] for n in names],
            *[outs["new_m_" + n] for n in names], *[outs["new_v_" + n] for n in names])
```

```python
import functools
import math

import jax
import jax.numpy as jnp
from jax import lax
from jax.experimental import pallas as pl
from jax.experimental.pallas import tpu as pltpu

_pcall = pl.pallas_call

F32 = jnp.float32
BF16 = jnp.bfloat16

D_MODEL = 1024
HEAD_DIM = 64
DIL_GROUPS = ((128, 1), (512, 4), (2048, 16))
DIL_HEADS_PER_GROUP = 4
N_DIL_HEADS = 12
N_SB_HEADS = 8
DIL_WIDTH = 768
DIL_OUT_WIDTH = 256
SB_WIDTH = 512
D_FF = 4096
BLOCK = 128
RMS_EPS = 1e-6
NEG_INF = -1e30
QKV_COLS = 3 * DIL_WIDTH + 3 * SB_WIDTH
IN_COLS = QKV_COLS + 2 * D_MODEL
N_DEV = 8

ADAM_LR = 0.001
ADAM_B1 = 0.9
ADAM_B2 = 0.999
ADAM_EPS = 1e-08
ADAM_WD = 0.01
ADAM_STEP = 10

VMEM_LIMIT = 56 * 1024 * 1024
SB_TK = 256
LANES = 128

_ARB = pltpu.ARBITRARY


def _cparams(n_axes, **kw):
    return pltpu.CompilerParams(dimension_semantics=(_ARB,) * n_axes, vmem_limit_bytes=VMEM_LIMIT, **kw)


def _dot(a, b):
    return jnp.dot(a, b, preferred_element_type=F32)


def _dot_nt(a, b):
    return lax.dot_general(a, b, (((1,), (1,)), ((), ())), preferred_element_type=F32)


def _dot_tn(a, b):
    return lax.dot_general(a, b, (((0,), (0,)), ((), ())), preferred_element_type=F32)


def _split_hi_lo(x):
    hi = x.astype(BF16)
    lo = (x - hi.astype(F32)).astype(BF16)
    return hi, lo


def _dot_hi_lo(x, m):
    hi, lo = _split_hi_lo(x)
    return _dot(hi, m) + _dot(lo, m)


def _sigmoid(x):
    return 1.0 / (1.0 + jnp.exp(-x))


_HBM = pl.BlockSpec(memory_space=pltpu.HBM)
_MESH = pl.DeviceIdType.MESH


class _Spread:
    def __init__(self, srcs, chunked):
        self.srcs, self.chunked, self.n = list(srcs), chunked, len(srcs)

    def land_shapes(self):
        return [jax.ShapeDtypeStruct((N_DEV,) + (s.shape[1:] if self.chunked else s.shape), s.dtype) for s in self.srcs]

    def scratch(self):
        dma = pltpu.SemaphoreType.DMA
        return [dma((7 * self.n,)), dma((7 * self.n,)), dma((self.n,))]

    def copies(self, src_refs, land_refs, send_sems, recv_sems, local_sems):
        x, y, c = lax.axis_index("x"), lax.axis_index("y"), lax.axis_index("c")
        me = 4 * x + 2 * y + c
        out = []
        for a, (src, land) in enumerate(zip(src_refs, land_refs)):
            out.append(pltpu.make_async_copy(src.at[me] if self.chunked else src, land.at[me], local_sems.at[a]))
            for k in range(1, N_DEV):
                px, py, pc = x ^ (k >> 2), y ^ ((k >> 1) & 1), c ^ (k & 1)
                out.append(pltpu.make_async_remote_copy(
                    src_ref=src.at[4 * px + 2 * py + pc] if self.chunked else src, dst_ref=land.at[me],
                    send_sem=send_sems.at[7 * a + k - 1], recv_sem=recv_sems.at[7 * a + k - 1],
                    device_id=(px, py, pc), device_id_type=_MESH))
        return out


def _call(body, args, rider=None, **kw):
    if rider is None:
        return _pcall(body, **kw)(*args)
    grid = kw["grid"]
    single = not isinstance(kw["out_shape"], (list, tuple))
    out_specs = [kw["out_specs"]] if single else list(kw["out_specs"])
    out_shape = [kw["out_shape"]] if single else list(kw["out_shape"])
    in_specs, scratch = list(kw["in_specs"]), list(kw.get("scratch_shapes", []))
    n_in, n_out, n_s, n = len(in_specs), len(out_shape), len(scratch), rider.n

    def hosted(*refs):
        ins, srcs = refs[:n_in], refs[n_in:n_in + n]
        outs, lands = refs[n_in + n:n_in + n + n_out], refs[n_in + n + n_out:n_in + 2 * n + n_out]
        own_scratch, sems = refs[n_in + 2 * n + n_out:n_in + 2 * n + n_out + n_s], refs[n_in + 2 * n + n_out + n_s:]
        ids = [pl.program_id(d) for d in range(len(grid))]
        first = functools.reduce(jnp.logical_and, [i == 0 for i in ids])
        last = functools.reduce(jnp.logical_and, [i == g - 1 for i, g in zip(ids, grid)])
        copies = rider.copies(srcs, lands, *sems)

        @pl.when(first)
        def _():
            for cp in copies:
                cp.start()

        body(*ins, *outs, *own_scratch)

        @pl.when(last)
        def _():
            for cp in copies:
                cp.wait()

    kw = dict(kw, in_specs=in_specs + [_HBM] * n, out_specs=out_specs + [_HBM] * n,
              out_shape=out_shape + rider.land_shapes(), scratch_shapes=scratch + rider.scratch())
    res = _pcall(hosted, **kw)(*args, *rider.srcs)
    return (res[0] if single else list(res[:n_out])), list(res[n_out:])


def _mm(name, a, b, *, ta=False, tb=False, out_dtype, tm, tn, tk, epilogue=None, extra=None, rider=None):
    m = a.shape[1] if ta else a.shape[0]
    k = a.shape[0] if ta else a.shape[1]
    n = b.shape[0] if tb else b.shape[1]
    assert (b.shape[1] if tb else b.shape[0]) == k
    tm, tn, tk = min(tm, m), min(tn, n), min(tk, k)
    assert m % tm == 0 and n % tn == 0 and k % tk == 0, (name, m, n, k, tm, tn, tk)
    nk = k // tk
    dn = (((0 if ta else 1,), (1 if tb else 0,)), ((), ()))
    in_place = nk > 1 and epilogue is None and out_dtype == F32

    def body(*refs):
        if extra is not None:
            a_ref, b_ref, e_ref, o_ref = refs[:4]
        else:
            a_ref, b_ref, o_ref = refs[:3]
            e_ref = None

        def finish(r):
            if epilogue is not None:
                r = epilogue(r, None if e_ref is None else e_ref[...])
            o_ref[...] = r.astype(out_dtype)

        part = lax.dot_general(a_ref[...].astype(BF16), b_ref[...].astype(BF16), dn, preferred_element_type=F32)
        if nk == 1:
            finish(part)
        else:
            acc_ref = o_ref if in_place else refs[-1]
            kk = pl.program_id(2)

            @pl.when(kk == 0)
            def _():
                acc_ref[...] = part

            @pl.when(kk > 0)
            def _():
                acc_ref[...] += part

            if not in_place:

                @pl.when(kk == nk - 1)
                def _():
                    finish(acc_ref[...])

    a_spec = pl.BlockSpec((tk, tm), lambda j, i, kk: (kk, i)) if ta else pl.BlockSpec((tm, tk), lambda j, i, kk: (i, kk))
    b_spec = pl.BlockSpec((tn, tk), lambda j, i, kk: (j, kk)) if tb else pl.BlockSpec((tk, tn), lambda j, i, kk: (kk, j))
    o_spec = pl.BlockSpec((tm, tn), lambda j, i, kk: (i, j))
    in_specs = [a_spec, b_spec]
    args = [a, b]
    if extra is not None:
        in_specs.append(o_spec)
        args.append(extra)
    return _call(
        body, args, rider,
        name=name,
        grid=(n // tn, m // tm, nk),
        in_specs=in_specs,
        out_specs=o_spec,
        out_shape=jax.ShapeDtypeStruct((m, n), out_dtype),
        scratch_shapes=[pltpu.VMEM((tm, tn), F32)] if (nk > 1 and not in_place) else [],
        compiler_params=_cparams(3),
    )


def _grad_cols(name, a, parts, *, tm, tk, rider=None):
    k, m = a.shape
    n = sum(p.shape[1] for p in parts)
    assert m % tm == 0 and k % tk == 0
    nk = k // tk

    def body(*refs):
        a_ref, p_refs, o_ref = refs[0], refs[1:1 + len(parts)], refs[1 + len(parts)]
        kk = pl.program_id(1)
        side_by_side = jnp.concatenate([p_ref[...].astype(BF16) for p_ref in p_refs], axis=1)
        term = _dot_tn(a_ref[...].astype(BF16), side_by_side)

        @pl.when(kk == 0)
        def _():
            o_ref[...] = term

        @pl.when(kk > 0)
        def _():
            o_ref[...] += term

    return _call(
        body, [a] + list(parts), rider,
        name=name,
        grid=(m // tm, nk),
        in_specs=[pl.BlockSpec((tk, tm), lambda i, kk: (kk, i))]
        + [pl.BlockSpec((tk, p.shape[1]), lambda i, kk: (kk, 0)) for p in parts],
        out_specs=pl.BlockSpec((tm, n), lambda i, kk: (i, 0)),
        out_shape=jax.ShapeDtypeStruct((m, n), F32),
        compiler_params=_cparams(2),
    )


def _rowk(name, *, a=None, w=None, nt=False, tm, tk=None, rows=(), consts=(), row_outs=(), acc_outs=(), epilogue,
          rider=None):
    has_mm = a is not None
    a_parts = list(a) if isinstance(a, (list, tuple)) else ([a] if has_mm else [])
    n_a = len(a_parts)
    m = a_parts[0].shape[0] if has_mm else rows[0].shape[0]
    assert m % tm == 0
    nm = m // tm
    if has_mm:
        k = sum(p.shape[1] for p in a_parts)
        n = w.shape[0] if nt else w.shape[1]
        tk = min(tk, k)
        assert k % tk == 0 and (n_a == 1 or tk == k)
        nk = k // tk
    else:
        nk = 1
    n_rows, n_consts, n_ro, n_ao = len(rows), len(consts), len(row_outs), len(acc_outs)

    def body(*refs):
        pos = 0
        if has_mm:
            a_refs, w_ref = refs[:n_a], refs[n_a]
            pos = n_a + 1
        row_refs = refs[pos:pos + n_rows]
        pos += n_rows
        const_refs = refs[pos:pos + n_consts]
        pos += n_consts
        ro_refs = refs[pos:pos + n_ro]
        pos += n_ro
        ao_refs = refs[pos:pos + n_ao]
        pos += n_ao
        i = pl.program_id(0)
        kk = pl.program_id(1)

        def finish(acc):
            ro_vals, ao_vals = epilogue(acc, [r[...] for r in row_refs], [c[...] for c in const_refs])
            for r, v in zip(ro_refs, ro_vals):
                r[...] = v.astype(r.dtype)
            for r, v in zip(ao_refs, ao_vals):

                @pl.when(i == 0)
                def _(r=r, v=v):
                    r[...] = v

                @pl.when(i > 0)
                def _(r=r, v=v):
                    r[...] += v

        if not has_mm:
            finish(None)
            return
        part, off = None, 0
        for a_ref in a_refs:
            width = a_ref.shape[1]
            cols = slice(None) if n_a == 1 else slice(off, off + width)
            av = a_ref[...].astype(BF16)
            term = _dot_nt(av, w_ref[:, cols]) if nt else _dot(av, w_ref[cols, :])
            part = term if part is None else part + term
            off += width
        if nk == 1:
            finish(part)
        else:
            acc_ref = refs[pos]

            @pl.when(kk == 0)
            def _():
                acc_ref[...] = part

            @pl.when(kk > 0)
            def _():
                acc_ref[...] += part

            @pl.when(kk == nk - 1)
            def _():
                finish(acc_ref[...])

    once = pl.Buffered(1)
    in_specs, args = [], []
    if has_mm:
        for part in a_parts:
            in_specs.append(pl.BlockSpec((tm, tk if n_a == 1 else part.shape[1]), lambda i, kk: (i, kk)))
        w_mode = once if nk == 1 else None
        in_specs.append(pl.BlockSpec((n, tk), lambda i, kk: (0, kk), pipeline_mode=w_mode) if nt
                        else pl.BlockSpec((tk, n), lambda i, kk: (kk, 0), pipeline_mode=w_mode))
        args += a_parts + [w]
    for r in rows:
        in_specs.append(pl.BlockSpec((tm, r.shape[1]), lambda i, kk: (i, 0)))
        args.append(r)
    for c in consts:
        in_specs.append(pl.BlockSpec(c.shape, lambda i, kk: (0,) * c.ndim, pipeline_mode=once))
        args.append(c)
    out_specs, out_shape = [], []
    for width, dt in row_outs:
        out_specs.append(pl.BlockSpec((tm, width), lambda i, kk: (i, 0)))
        out_shape.append(jax.ShapeDtypeStruct((m, width), dt))
    for width in acc_outs:
        out_specs.append(pl.BlockSpec((1, width), lambda i, kk: (0, 0)))
        out_shape.append(jax.ShapeDtypeStruct((1, width), F32))
    return _call(
        body, args, rider,
        name=name,
        grid=(nm, nk),
        in_specs=in_specs,
        out_specs=out_specs,
        out_shape=out_shape,
        scratch_shapes=[pltpu.VMEM((tm, n), F32)] if (has_mm and nk > 1) else [],
        compiler_params=_cparams(2),
    )


def _rms_stats(x):
    r = lax.rsqrt(jnp.mean(x * x, axis=-1, keepdims=True) + RMS_EPS)
    return r, x * r


def _rms_bwd(dh, xh, r, g):
    gy = dh * g
    dx = r * (gy - xh * jnp.mean(gy * xh, axis=-1, keepdims=True))
    return dx, jnp.sum(dh * xh, axis=0, keepdims=True)


def _alibi_slope(head):
    return 2.0 ** (-8.0 * (head + 1) / N_DIL_HEADS)


DIL_STEP_BLOCKS = 4


def _dil_band(first_block):
    qi = lax.broadcasted_iota(jnp.int32, (BLOCK, 2 * BLOCK), 0)
    kj = lax.broadcasted_iota(jnp.int32, (BLOCK, 2 * BLOCK), 1)
    steps = qi + BLOCK - kj
    valid = (steps >= 0) & (steps <= BLOCK)
    if first_block is not False:
        valid = valid & ((kj >= BLOCK) | jnp.logical_not(first_block))
    return steps.astype(F32), valid


def _dil_step_specs(ncb, cols, nblk, clamp):
    def own(col):
        return pl.BlockSpec((nblk * BLOCK, DIL_OUT_WIDTH), lambda r, i: (clamp(i), r * ncb + col))

    def before(col):
        return pl.BlockSpec((BLOCK, DIL_OUT_WIDTH), lambda r, i: (jnp.maximum(clamp(i) * nblk - 1, 0), r * ncb + col))

    return [own(cols[0]), own(cols[1]), before(cols[1]), own(cols[2]), before(cols[2])]


def _dil_fwd(qkv, group):
    window, dilation = DIL_GROUPS[group]
    s = qkv.shape[0]
    sub = s // dilation
    nb = sub // BLOCK
    assert nb * BLOCK * dilation == s and window // dilation == BLOCK
    nblk = min(DIL_STEP_BLOCKS, nb)
    assert nb % nblk == 0
    slopes = [_alibi_slope(group * DIL_HEADS_PER_GROUP + h) * dilation for h in range(DIL_HEADS_PER_GROUP)]

    def body(q_ref, kc_ref, kp_ref, vc_ref, vp_ref, o_ref, lse_ref):
        i = pl.program_id(1)
        kk_all = jnp.concatenate([kp_ref[...], kc_ref[...]], axis=0)
        vv_all = jnp.concatenate([vp_ref[...], vc_ref[...]], axis=0)
        head_id = lax.broadcasted_iota(jnp.int32, (1, DIL_OUT_WIDTH), 1) // HEAD_DIM
        chains = [(b, h) for b in range(nblk) for h in range(DIL_HEADS_PER_GROUP)]
        rows = lambda b: slice(b * BLOCK, (b + 1) * BLOCK)
        keys = lambda b: slice(b * BLOCK, (b + 2) * BLOCK)
        bands = [_dil_band(i == 0 if b == 0 else False) for b in range(nblk)]
        qs = [q_ref[rows(b), :] for b in range(nblk)]
        scores = [_dot_nt(jnp.where(head_id == h, qs[b], jnp.zeros_like(qs[b])), kk_all[keys(b)]) for b, h in chains]
        ps, lses = [], []
        for (b, h), sc in zip(chains, scores):
            steps, valid = bands[b]
            logits = jnp.where(valid, sc * (1.0 / math.sqrt(HEAD_DIM)) - slopes[h] * steps, NEG_INF)
            mx = jnp.max(logits, axis=1, keepdims=True)
            e = jnp.exp(logits - mx)
            den = jnp.sum(e, axis=1, keepdims=True)
            lses.append(mx + jnp.log(den))
            ps.append((e * (1.0 / den)).astype(BF16))
        outs = [_dot(p, vv_all[keys(b)]) for (b, h), p in zip(chains, ps)]
        for b in range(nblk):
            mine = [n for n, ch in enumerate(chains) if ch[0] == b]
            o, lse_all = outs[mine[0]], lses[mine[0]]
            for n in mine[1:]:
                o = jnp.where(head_id == chains[n][1], outs[n], o)
                lse_all = jnp.where(head_id == chains[n][1], lses[n], lse_all)
            o_ref[rows(b), :] = o
            lse_ref[rows(b), :] = jnp.broadcast_to(lse_all, o.shape)

    qkv_v, ncb, cols = _dil_view(qkv, group)
    out_spec = pl.BlockSpec((nblk * BLOCK, DIL_OUT_WIDTH), lambda r, i: (i, r))
    o, lse = _pcall(
        body,
        name=f"dil_fwd_g{group}",
        grid=(dilation, nb // nblk),
        in_specs=_dil_step_specs(ncb, cols, nblk, lambda i: i),
        out_specs=[out_spec, out_spec],
        out_shape=[jax.ShapeDtypeStruct((sub, dilation * DIL_OUT_WIDTH), F32)] * 2,
        compiler_params=_cparams(2),
    )(qkv_v, qkv_v, qkv_v, qkv_v, qkv_v)
    return o.reshape(s, DIL_OUT_WIDTH), lse.reshape(s, DIL_OUT_WIDTH), lse


def _dil_bwd(qkv, do_g, lse_g, dterm_g, group, rider=None):
    window, dilation = DIL_GROUPS[group]
    s = qkv.shape[0]
    sub = s // dilation
    nb = sub // BLOCK
    nblk = min(DIL_STEP_BLOCKS, nb)
    n_steps = nb // nblk
    slopes = [_alibi_slope(group * DIL_HEADS_PER_GROUP + h) * dilation for h in range(DIL_HEADS_PER_GROUP)]
    scale = 1.0 / math.sqrt(HEAD_DIM)
    tail = slice((nblk - 1) * BLOCK, nblk * BLOCK)

    def body(q_ref, kc_ref, kp_ref, vc_ref, vp_ref, do_ref, lse_ref, dt_ref, dq_ref, dk_ref, dv_ref, ck_ref, cv_ref):
        i = pl.program_id(1)

        @pl.when(i == 0)
        def _():
            ck_ref[...] = jnp.zeros_like(ck_ref)
            cv_ref[...] = jnp.zeros_like(cv_ref)

        @pl.when(i < n_steps)
        def _():
            kk_all = jnp.concatenate([kp_ref[...], kc_ref[...]], axis=0)
            vv_all = jnp.concatenate([vp_ref[...], vc_ref[...]], axis=0)
            lane = lax.broadcasted_iota(jnp.int32, (1, DIL_OUT_WIDTH), 1)
            head_id = lane // HEAD_DIM
            chains = [(b, h) for b in range(nblk) for h in range(DIL_HEADS_PER_GROUP)]
            rows = lambda b: slice(b * BLOCK, (b + 1) * BLOCK)
            keys = lambda b: slice(b * BLOCK, (b + 2) * BLOCK)
            bands = [_dil_band(i == 0 if b == 0 else False) for b in range(nblk)]
            qms, doms = [], []
            for b, h in chains:
                q, do = q_ref[rows(b), :], do_ref[rows(b), :]
                qms.append(jnp.where(head_id == h, q, jnp.zeros_like(q)))
                doms.append(jnp.where(head_id == h, do, jnp.zeros_like(do)))
            scores = [_dot_nt(qm, kk_all[keys(b)]) for (b, h), qm in zip(chains, qms)]
            dps = [_dot_nt(dom, vv_all[keys(b)]) for (b, h), dom in zip(chains, doms)]
            pbs, dss = [], []
            for n, (b, h) in enumerate(chains):
                steps, valid = bands[b]
                first = lane == h * HEAD_DIM
                lse = jnp.sum(jnp.where(first, lse_ref[rows(b), :], 0.0), axis=1, keepdims=True)
                dt = jnp.sum(jnp.where(first, dt_ref[rows(b), :], 0.0), axis=1, keepdims=True)
                logits = jnp.where(valid, scores[n] * scale - slopes[h] * steps, NEG_INF)
                p = jnp.where(valid, jnp.exp(logits - lse), 0.0)
                pbs.append(p.astype(BF16))
                dss.append((p * (dps[n] + dt) * scale).astype(BF16))
            dqs = [_dot(ds, kk_all[keys(b)]) for (b, h), ds in zip(chains, dss)]
            dks = [_dot_tn(ds, qm) for ds, qm in zip(dss, qms)]
            dvs = [_dot_tn(pb, dom) for pb, dom in zip(pbs, doms)]
            dkk, dvv = [], []
            for b in range(nblk):
                mine = [n for n, ch in enumerate(chains) if ch[0] == b]
                dq = dqs[mine[0]]
                for n in mine[1:]:
                    dq = jnp.where(head_id == chains[n][1], dqs[n], dq)
                dq_ref[rows(b), :] = dq.astype(dq_ref.dtype)
                dkk.append((dks[mine[0]] + dks[mine[1]]) + (dks[mine[2]] + dks[mine[3]]))
                dvv.append((dvs[mine[0]] + dvs[mine[1]]) + (dvs[mine[2]] + dvs[mine[3]]))
            for out_ref, carry_ref, parts in ((dk_ref, ck_ref, dkk), (dv_ref, cv_ref, dvv)):
                if nblk > 1:
                    out_ref[: (nblk - 1) * BLOCK, :] = carry_ref[: (nblk - 1) * BLOCK, :].astype(out_ref.dtype)
                out_ref[tail, :] = (carry_ref[tail, :] + parts[0][:BLOCK]).astype(out_ref.dtype)
                for b in range(nblk):
                    own = parts[b][BLOCK:]
                    carry_ref[rows(b), :] = own + parts[b + 1][:BLOCK] if b + 1 < nblk else own

        @pl.when(i == n_steps)
        def _():
            dk_ref[...] = ck_ref[...].astype(dk_ref.dtype)
            dv_ref[...] = cv_ref[...].astype(dv_ref.dtype)

    clamp = lambda i: jnp.minimum(i, n_steps - 1)
    qkv_v, ncb, cols = _dil_view(qkv, group)
    view = lambda t: t.reshape(sub, dilation * DIL_OUT_WIDTH)
    row_spec = pl.BlockSpec((nblk * BLOCK, DIL_OUT_WIDTH), lambda r, i: (clamp(i), r))
    late_spec = pl.BlockSpec((nblk * BLOCK, DIL_OUT_WIDTH), lambda r, i: (jnp.maximum(i - 1, 0), r))
    res = _call(
        body, (qkv_v, qkv_v, qkv_v, qkv_v, qkv_v, view(do_g), view(lse_g), view(dterm_g)), rider,
        name=f"dil_bwd_g{group}",
        grid=(dilation, n_steps + 1),
        in_specs=_dil_step_specs(ncb, cols, nblk, clamp) + [row_spec, row_spec, row_spec],
        out_specs=[row_spec, late_spec, late_spec],
        out_shape=[jax.ShapeDtypeStruct((sub, dilation * DIL_OUT_WIDTH), BF16)] * 3,
        scratch_shapes=[pltpu.VMEM((nblk * BLOCK, DIL_OUT_WIDTH), F32)] * 2,
        compiler_params=_cparams(2),
    )
    grads, lands = res if rider is not None else (res, None)
    grads = tuple(g.reshape(s, DIL_OUT_WIDTH) for g in grads)
    return grads if rider is None else (grads, lands)


def _dil_view(qkv, group):
    _, dilation = DIL_GROUPS[group]
    if dilation == 1:
        return qkv, QKV_COLS // DIL_OUT_WIDTH, (group, 3 + group, 6 + group)
    w = DIL_OUT_WIDTH
    own = jnp.concatenate([qkv[:, (3 * part + group) * w:(3 * part + group + 1) * w] for part in range(3)], axis=1)
    return own.reshape(qkv.shape[0] // dilation, dilation * 3 * w), 3, (0, 1, 2)


def _head_block_ones():
    r = lax.broadcasted_iota(jnp.int32, (DIL_OUT_WIDTH, DIL_OUT_WIDTH), 0) // HEAD_DIM
    c = lax.broadcasted_iota(jnp.int32, (DIL_OUT_WIDTH, DIL_OUT_WIDTH), 1) // HEAD_DIM
    return jnp.where(r == c, 1.0, 0.0).astype(BF16)


def _dil_mix_weights(l0, l1, l2):
    mx = jnp.maximum(jnp.maximum(l0, l1), l2)
    e0, e1, e2 = jnp.exp(l0 - mx), jnp.exp(l1 - mx), jnp.exp(l2 - mx)
    inv = 1.0 / (e0 + e1 + e2)
    return e0 * inv, e1 * inv, e2 * inv


def _dil_mix_fwd(os_, lses, tm):
    def epi(_, rows, consts):
        o0, o1, o2, l0, l1, l2 = rows
        w0, w1, w2 = _dil_mix_weights(l0, l1, l2)
        return [w0 * o0 + w1 * o1 + w2 * o2], []

    (o_a,) = _rowk("dil_mix_fwd", tm=tm, rows=list(os_) + list(lses), row_outs=[(DIL_OUT_WIDTH, BF16)], epilogue=epi)
    return o_a


def _dil_mix_bwd(do_a, os_, lses, tm):
    def epi(_, rows, consts):
        do, o0, o1, o2, l0, l1, l2 = rows
        do = do.astype(F32)
        w0, w1, w2 = _dil_mix_weights(l0, l1, l2)
        mixed = w0 * o0 + w1 * o1 + w2 * o2
        tot = _dot_hi_lo(do * mixed, _head_block_ones())
        return [w0 * do, w1 * do, w2 * do, -w0 * tot, -w1 * tot, -w2 * tot], []

    return _rowk(
        "dil_mix_bwd", tm=tm, rows=[do_a] + list(os_) + list(lses),
        row_outs=[(DIL_OUT_WIDTH, BF16)] * 3 + [(DIL_OUT_WIDTH, F32)] * 3, epilogue=epi)


_SB_Q0 = 3 * DIL_WIDTH // LANES
_SB_K0 = _SB_Q0 + SB_WIDTH // LANES
_SB_V0 = _SB_K0 + SB_WIDTH // LANES


_EXP_CLAMP = 88.0
_SB_DEAD = 104.0


def _tri(t, op):
    r = lax.broadcasted_iota(jnp.int32, (t, t), 0)
    c = lax.broadcasted_iota(jnp.int32, (t, t), 1)
    return jnp.where(op(r, c), 1.0, 0.0).astype(BF16)


def _softplus(z):
    return jnp.maximum(z, jnp.log(1.0 + jnp.exp(jnp.minimum(z, _EXP_CLAMP))))


def _sb_chain_head(qm, kj, mask):
    z = _dot_nt(qm, kj)
    sp = _softplus(z)
    return (sp if mask is None else jnp.where(mask, sp, 0.0)), z - sp


def _sb_fwd(qkv, rider=None):
    s = qkv.shape[0]
    t = SB_TK
    assert s % (2 * t) == 0
    nq = s // (2 * t)
    n_pairs = SB_WIDTH // LANES

    def body(q_ref, k_ref, v_ref, o_ref, tot_ref, steps_ref):
        p, i = pl.program_id(0), pl.program_id(1)
        lane_hi = lax.broadcasted_iota(jnp.int32, (1, LANES), 1) // HEAD_DIM
        later = _tri(t, lambda r, c: r > c)
        causal = lax.broadcasted_iota(jnp.int32, (t, t), 1) < lax.broadcasted_iota(jnp.int32, (t, t), 0)
        qms = []
        for x in range(2):
            q = q_ref[pl.ds(x * t, t), :] * (1.0 / math.sqrt(HEAD_DIM))
            qms.append([jnp.where(lane_hi == hh, q, jnp.zeros_like(q)) for hh in range(2)])

        def tile(j):
            off = pl.multiple_of(j * t, t)
            return k_ref[pl.ds(off, t), :], v_ref[pl.ds(off, t), :]

        def step(groups, carry):
            kv = [tile(j) for _, j, _ in groups]
            chains = [(g, x, hh) for g, (x, _, _) in enumerate(groups) for hh in range(2)]
            heads = [_sb_chain_head(qms[x][hh], kv[g][0], causal if groups[g][2] else None) for g, x, hh in chains]
            sufs = [_dot(sp.astype(BF16), later) for sp, _ in heads]
            cur = [list(carry[0]), list(carry[1])]
            for (g, x, hh), (sp, lpos), suf in zip(chains, heads, sufs):
                c, acc = cur[x][hh]
                a = jnp.exp(lpos - suf - c)
                if groups[g][2]:
                    a = jnp.where(causal, a, 0.0)
                cur[x][hh] = (c + jnp.sum(sp, axis=1, keepdims=True), acc + _dot(a.astype(BF16), kv[g][1]))
            return (tuple(cur[0]), tuple(cur[1]))

        def lowest(carry):
            return jnp.min(jnp.minimum(jnp.minimum(carry[0][0][0], carry[0][1][0]),
                                       jnp.minimum(carry[1][0][0], carry[1][1][0])))

        zero = (jnp.zeros((t, 1), F32), jnp.zeros((t, LANES), F32))
        start = ((zero, zero), (zero, zero))
        carry = lax.cond(
            i == 0,
            lambda ca: step([(0, 0, True), (1, 1, True), (1, 0, False)], ca),
            lambda ca: step([(0, 2 * i, True), (1, 2 * i + 1, True), (0, 2 * i - 1, False), (1, 2 * i, False)], ca),
            start)

        def walk(state):
            n, ca, _ = state
            ca = step([(0, 2 * i - 2 - n, False), (1, 2 * i - 1 - n, False)], ca)
            return n + 1, ca, lowest(ca)

        n_more, carry, low = lax.while_loop(
            lambda st: jnp.logical_and(st[0] + 1 < 2 * i, st[2] <= _SB_DEAD), walk, (jnp.int32(0), carry, lowest(carry)))
        b_last = jnp.logical_and(jnp.logical_and(i > 0, n_more + 1 == 2 * i), low <= _SB_DEAD)
        carry = lax.cond(b_last, lambda ca: step([(1, 0, False)], ca), lambda ca: ca, carry)
        for x in range(2):
            (c0, acc0), (c1, acc1) = carry[x]
            o_ref[pl.ds(x * t, t), :] = jnp.where(lane_hi == 0, acc0, acc1).astype(o_ref.dtype)
            tot_ref[pl.ds(x * t, t), :] = jnp.where(lane_hi == 0, c0, c1)
        steps_ref[p, i] = 1 + n_more + b_last.astype(jnp.int32)

    return _call(
        body, (qkv, qkv, qkv), rider,
        name="sb_fwd",
        grid=(n_pairs, nq),
        in_specs=[
            pl.BlockSpec((2 * t, LANES), lambda p, i: (i, _SB_Q0 + p)),
            pl.BlockSpec((s, LANES), lambda p, i: (0, _SB_K0 + p)),
            pl.BlockSpec((s, LANES), lambda p, i: (0, _SB_V0 + p)),
        ],
        out_specs=[pl.BlockSpec((2 * t, LANES), lambda p, i: (i, p))] * 2 + [pl.BlockSpec(memory_space=pltpu.SMEM)],
        out_shape=[jax.ShapeDtypeStruct((s, SB_WIDTH), BF16), jax.ShapeDtypeStruct((s, SB_WIDTH), F32),
                   jax.ShapeDtypeStruct((n_pairs, nq), jnp.int32)],
        compiler_params=_cparams(2),
    )


def _sb_bwd(qkv, do_b, tot_b, n_steps):
    s = qkv.shape[0]
    t = SB_TK
    nq = s // (2 * t)
    n_pairs = SB_WIDTH // LANES
    scale = 1.0 / math.sqrt(HEAD_DIM)

    def body(steps_ref, q_ref, k_ref, v_ref, do_ref, tot_ref, dq_ref, dk_ref, dv_ref):
        p, i = pl.program_id(0), pl.program_id(1)

        @pl.when(i == 0)
        def _():
            dk_ref[...] = jnp.zeros_like(dk_ref)
            dv_ref[...] = jnp.zeros_like(dv_ref)

        lane = lax.broadcasted_iota(jnp.int32, (1, LANES), 1)
        lane_hi = lane // HEAD_DIM
        later = _tri(t, lambda r, c: r > c)
        before = _tri(t, lambda r, c: r < c)
        causal = lax.broadcasted_iota(jnp.int32, (t, t), 1) < lax.broadcasted_iota(jnp.int32, (t, t), 0)
        qms, doms, tots = [], [], []
        for x in range(2):
            rows = pl.ds(x * t, t)
            q, do, tot_all = q_ref[rows, :] * scale, do_ref[rows, :], tot_ref[rows, :]
            qms.append([jnp.where(lane_hi == hh, q, jnp.zeros_like(q)) for hh in range(2)])
            doms.append([jnp.where(lane_hi == hh, do, jnp.zeros_like(do)) for hh in range(2)])
            tots.append([jnp.sum(jnp.where(lane == hh * HEAD_DIM, tot_all, 0.0), axis=1, keepdims=True)
                         for hh in range(2)])

        def step(groups, carry):
            offs = [pl.multiple_of(j * t, t) for _, j, _ in groups]
            ks = [k_ref[pl.ds(off, t), :] for off in offs]
            vs = [v_ref[pl.ds(off, t), :] for off in offs]
            chains = [(g, x, hh) for g, (x, _, _) in enumerate(groups) for hh in range(2)]
            heads = [_sb_chain_head(qms[x][hh], ks[g], causal if groups[g][2] else None) for g, x, hh in chains]
            sufs = [_dot(sp.astype(BF16), later) for sp, _ in heads]
            das = [_dot_nt(doms[x][hh], vs[g]) for g, x, hh in chains]
            cur = [list(carry[0]), list(carry[1])]
            sigs, gs, abs_, cg_before = [], [], [], []
            for (g_, x, hh), (sp, lpos), suf, da in zip(chains, heads, sufs, das):
                cl, cg, dq = cur[x][hh]
                cl = cl + jnp.sum(sp, axis=1, keepdims=True)
                sig = jnp.exp(lpos)
                a = sig * jnp.exp(-suf - (tots[x][hh] - cl))
                if groups[g_][2]:
                    a = jnp.where(causal, a, 0.0)
                g = a * da
                sigs.append(sig)
                gs.append(g)
                abs_.append(a.astype(BF16))
                cg_before.append(cg)
                cur[x][hh] = (cl, cg + jnp.sum(g, axis=1, keepdims=True), dq)
            prefs = [_dot(g.astype(BF16), before) for g in gs]
            dvs = [_dot_tn(ab, doms[x][hh]) for (_, x, hh), ab in zip(chains, abs_)]
            dzs = []
            for (g_, x, hh), sig, g, pref, cg in zip(chains, sigs, gs, prefs, cg_before):
                dz = g - sig * (g + pref + cg)
                if groups[g_][2]:
                    dz = jnp.where(causal, dz, 0.0)
                dzs.append(dz.astype(BF16))
            dqs = [_dot(dz, ks[g_]) for (g_, x, hh), dz in zip(chains, dzs)]
            dks = [_dot_tn(dz, qms[x][hh]) for (_, x, hh), dz in zip(chains, dzs)]
            for n, (_, x, hh) in enumerate(chains):
                cl, cg, dq = cur[x][hh]
                cur[x][hh] = (cl, cg, dq + dqs[n])
            for g_, off in enumerate(offs):
                dk_ref[pl.ds(off, t), :] += dks[2 * g_] + dks[2 * g_ + 1]
                dv_ref[pl.ds(off, t), :] += dvs[2 * g_] + dvs[2 * g_ + 1]
            return (tuple(cur[0]), tuple(cur[1]))

        taken = steps_ref[p, i]
        n_full = jnp.minimum(taken, 2 * i)
        zero = (jnp.zeros((t, 1), F32), jnp.zeros((t, 1), F32), jnp.zeros((t, LANES), F32))
        carry = ((zero, zero), (zero, zero))
        carry = lax.cond(jnp.logical_and(i > 0, taken > 2 * i), lambda ca: step([(1, 0, False)], ca), lambda ca: ca,
                         carry)
        carry = lax.fori_loop(
            0, n_full - 1,
            lambda n, ca: step([(0, 2 * i - n_full + n, False), (1, 2 * i + 1 - n_full + n, False)], ca), carry)
        carry = lax.cond(
            i == 0,
            lambda ca: step([(1, 0, False), (0, 0, True), (1, 1, True)], ca),
            lambda ca: step([(0, 2 * i - 1, False), (1, 2 * i, False), (0, 2 * i, True), (1, 2 * i + 1, True)], ca),
            carry)
        for x in range(2):
            dq = jnp.where(lane_hi == 0, carry[x][0][2], carry[x][1][2])
            dq_ref[pl.ds(x * t, t), :] = (dq * scale).astype(dq_ref.dtype)

    row_spec = pl.BlockSpec((2 * t, LANES), lambda p, i, ns: (i, p))
    full_spec = pl.BlockSpec((s, LANES), lambda p, i, ns: (0, p))
    return _pcall(
        body,
        name="sb_bwd",
        grid_spec=pltpu.PrefetchScalarGridSpec(
            num_scalar_prefetch=1,
            grid=(n_pairs, nq),
            in_specs=[
                pl.BlockSpec((2 * t, LANES), lambda p, i, ns: (i, _SB_Q0 + p)),
                pl.BlockSpec((s, LANES), lambda p, i, ns: (0, _SB_K0 + p)),
                pl.BlockSpec((s, LANES), lambda p, i, ns: (0, _SB_V0 + p)),
                row_spec, row_spec,
            ],
            out_specs=[row_spec, full_spec, full_spec],
        ),
        out_shape=[jax.ShapeDtypeStruct((s, SB_WIDTH), BF16), jax.ShapeDtypeStruct((s, SB_WIDTH), F32),
                   jax.ShapeDtypeStruct((s, SB_WIDTH), F32)],
        compiler_params=_cparams(2),
    )(n_steps, qkv, qkv, qkv, do_b, tot_b)


def _gates(gl, bg):
    return _sigmoid(gl[:, :D_MODEL] + bg[:, :D_MODEL]), _sigmoid(gl[:, D_MODEL:] + bg[:, D_MODEL:])


def _mixer_fwd(o_a, o_b, gl, x0, bg, g2, w_ud, w_us, w_out, tm):
    def epi(_, rows, consts):
        oa, ob, glv, x = rows
        bgv, g2v, wud, wus, wout = consts
        ga, gb = _gates(glv, bgv)
        merged = ga * _dot(oa, wud) + gb * _dot(ob, wus)
        x1 = x + _dot(merged.astype(BF16), wout)
        r, xh = _rms_stats(x1)
        return [x1, xh * g2v], []

    return _rowk("mixer_fwd", tm=tm, rows=[o_a, o_b, gl, x0], consts=[bg, g2, w_ud, w_us, w_out],
                 row_outs=[(D_MODEL, F32), (D_MODEL, BF16)], epilogue=epi)


def _mixer_bwd(dx1, o_a, o_b, gl, bg, w_ud, w_us, w_out, tm, rider=None):
    s = dx1.shape[0]
    nm = s // tm

    def body(dx_ref, oa_ref, ob_ref, gl_ref, bg_ref, wud_ref, wus_ref, wout_ref,
             doa_ref, dob_ref, dgl_ref, gwout_ref, gwud_ref, gwus_ref, gbg_ref):
        i = pl.program_id(0)
        dxb = dx_ref[...].astype(BF16)
        oa, ob = oa_ref[...], ob_ref[...]
        ga, gb = _gates(gl_ref[...], bg_ref[...])
        ua, ub = _dot(oa, wud_ref[...]), _dot(ob, wus_ref[...])
        merged = (ga * ua + gb * ub).astype(BF16)
        dm = _dot_nt(dxb, wout_ref[...])
        dua = (dm * ga).astype(BF16)
        dub = (dm * gb).astype(BF16)
        dgla = dm * ua * ga * (1.0 - ga)
        dglb = dm * ub * gb * (1.0 - gb)
        doa_ref[...] = _dot_nt(dua, wud_ref[...]).astype(doa_ref.dtype)
        dob_ref[...] = _dot_nt(dub, wus_ref[...]).astype(dob_ref.dtype)
        dgl_ref[:, :D_MODEL] = dgla.astype(dgl_ref.dtype)
        dgl_ref[:, D_MODEL:] = dglb.astype(dgl_ref.dtype)
        parts = [(gwout_ref, _dot_tn(merged, dxb)), (gwud_ref, _dot_tn(oa, dua)), (gwus_ref, _dot_tn(ob, dub))]
        for r, v in parts:

            @pl.when(i == 0)
            def _(r=r, v=v):
                r[...] = v

            @pl.when(i > 0)
            def _(r=r, v=v):
                r[...] += v

        sa = jnp.sum(dgla, axis=0, keepdims=True)
        sb = jnp.sum(dglb, axis=0, keepdims=True)

        @pl.when(i == 0)
        def _():
            gbg_ref[:, :D_MODEL] = sa
            gbg_ref[:, D_MODEL:] = sb

        @pl.when(i > 0)
        def _():
            gbg_ref[:, :D_MODEL] += sa
            gbg_ref[:, D_MODEL:] += sb

    row = lambda w: pl.BlockSpec((tm, w), lambda i: (i, 0))
    full = lambda a: pl.BlockSpec(a.shape, lambda i: (0, 0), pipeline_mode=pl.Buffered(1))
    fshape = lambda r, c: jax.ShapeDtypeStruct((r, c), F32)
    return _call(
        body, (dx1, o_a, o_b, gl, bg, w_ud, w_us, w_out), rider,
        name="mixer_bwd",
        grid=(nm,),
        in_specs=[row(D_MODEL), row(DIL_OUT_WIDTH), row(SB_WIDTH), row(2 * D_MODEL),
                  full(bg), full(w_ud), full(w_us), full(w_out)],
        out_specs=[row(DIL_OUT_WIDTH), row(SB_WIDTH), row(2 * D_MODEL),
                   pl.BlockSpec((D_MODEL, D_MODEL), lambda i: (0, 0)),
                   pl.BlockSpec((DIL_OUT_WIDTH, D_MODEL), lambda i: (0, 0)),
                   pl.BlockSpec((SB_WIDTH, D_MODEL), lambda i: (0, 0)),
                   pl.BlockSpec((1, 2 * D_MODEL), lambda i: (0, 0))],
        out_shape=[jax.ShapeDtypeStruct((s, DIL_OUT_WIDTH), BF16), jax.ShapeDtypeStruct((s, SB_WIDTH), BF16),
                   jax.ShapeDtypeStruct((s, 2 * D_MODEL), BF16),
                   fshape(D_MODEL, D_MODEL), fshape(DIL_OUT_WIDTH, D_MODEL), fshape(SB_WIDTH, D_MODEL),
                   fshape(1, 2 * D_MODEL)],
        compiler_params=_cparams(1),
    )


def _all_gather(shards):
    n = len(shards)

    def body(*refs):
        x_refs, out_refs = refs[:n], refs[n:2 * n]
        send_sems, recv_sems, local_sems = refs[2 * n:]
        x, y, c = lax.axis_index("x"), lax.axis_index("y"), lax.axis_index("c")
        me, sibling = (x, y, c), (x, y, 1 - c)
        chips = [(1 - x, y), (x, 1 - y), (1 - x, 1 - y)]

        def slot(a, px, py, pc):
            return out_refs[a].at[4 * px + 2 * py + pc]

        def copy(a, k, block, to, own=False):
            return pltpu.make_async_remote_copy(
                src_ref=x_refs[a] if own else slot(a, *block), dst_ref=slot(a, *block),
                send_sem=send_sems.at[7 * a + k], recv_sem=recv_sems.at[7 * a + k], device_id=to, device_id_type=_MESH)

        mine = [pltpu.make_async_copy(x_refs[a], slot(a, *me), local_sems.at[a]) for a in range(n)]
        for cp in mine:
            cp.start()
        first = []
        for a in range(n):
            first.append(copy(a, 0, me, sibling, own=True))
            first += [copy(a, 1 + j, me, (*chip, c), own=True) for j, chip in enumerate(chips)]
        for cp in first:
            cp.start()
        passed = []
        for a in range(n):
            for j, chip in enumerate(chips):
                copy(a, 1 + j, (*chip, c), me).wait_recv()
                passed.append(copy(a, 4 + j, (*chip, c), sibling))
                passed[-1].start()
        for a in range(n):
            copy(a, 0, sibling, me).wait_recv()
            for j, chip in enumerate(chips):
                copy(a, 4 + j, (*chip, 1 - c), me).wait_recv()
        for cp in first + passed:
            cp.wait_send()
        for cp in mine:
            cp.wait()

    return _pcall(
        body,
        name="all_gather_weights",
        in_specs=[_HBM] * n,
        out_specs=[_HBM] * n,
        out_shape=[jax.ShapeDtypeStruct((N_DEV,) + s.shape, s.dtype) for s in shards],
        scratch_shapes=[pltpu.SemaphoreType.DMA((7 * n,)), pltpu.SemaphoreType.DMA((7 * n,)),
                        pltpu.SemaphoreType.DMA((n,))],
    )(*shards)


def _exchange(chunks):
    n = len(chunks)

    def body(*refs):
        g_refs, o_refs = refs[:n], refs[n:2 * n]
        send_sems, recv_sems, local_sems = refs[2 * n:]
        x, y, c = lax.axis_index("x"), lax.axis_index("y"), lax.axis_index("c")
        me = 4 * x + 2 * y + c
        own = [pltpu.make_async_copy(g_refs[a].at[me], o_refs[a].at[me], local_sems.at[a]) for a in range(n)]
        for cp in own:
            cp.start()
        copies = []
        for a in range(n):
            for k in range(1, N_DEV):
                px, py, pc = x ^ (k >> 2), y ^ ((k >> 1) & 1), c ^ (k & 1)
                peer = 4 * px + 2 * py + pc
                copies.append(pltpu.make_async_remote_copy(
                    src_ref=g_refs[a].at[peer], dst_ref=o_refs[a].at[me], send_sem=send_sems.at[7 * a + k - 1],
                    recv_sem=recv_sems.at[7 * a + k - 1], device_id=(px, py, pc), device_id_type=_MESH))
        for cp in copies:
            cp.start()
        for cp in copies:
            cp.wait()
        for cp in own:
            cp.wait()

    return _pcall(
        body,
        name="exchange_grads",
        in_specs=[_HBM] * n,
        out_specs=[_HBM] * n,
        out_shape=[jax.ShapeDtypeStruct(g.shape, g.dtype) for g in chunks],
        scratch_shapes=[pltpu.SemaphoreType.DMA((7 * n,)), pltpu.SemaphoreType.DMA((7 * n,)),
                        pltpu.SemaphoreType.DMA((n,))],
    )(*chunks)


def _reduce_adamw(name, parts, w, m, v, tr):
    _, rows, cols = parts.shape
    tr = min(tr, rows)
    assert rows % tr == 0
    c1 = 1.0 / (1.0 - ADAM_B1 ** ADAM_STEP)
    c2 = 1.0 / (1.0 - ADAM_B2 ** ADAM_STEP)

    def body(p_ref, w_ref, m_ref, v_ref, g_out, d_out, m_out, v_out):
        g = p_ref[0].astype(F32)
        for d in range(1, N_DEV):
            g = g + p_ref[d].astype(F32)
        mn = ADAM_B1 * m_ref[...] + (1.0 - ADAM_B1) * g
        vn = ADAM_B2 * v_ref[...] + (1.0 - ADAM_B2) * (g * g)
        g_out[...] = g
        m_out[...] = mn
        v_out[...] = vn
        d_out[...] = -ADAM_LR * ((mn * c1) / (jnp.sqrt(vn * c2) + ADAM_EPS) + ADAM_WD * w_ref[...])

    spec = pl.BlockSpec((tr, cols), lambda i: (i, 0))
    return _pcall(
        body,
        name=name,
        grid=(rows // tr,),
        in_specs=[pl.BlockSpec((N_DEV, tr, cols), lambda i: (0, i, 0)), spec, spec, spec],
        out_specs=[spec] * 4,
        out_shape=[jax.ShapeDtypeStruct((rows, cols), F32)] * 4,
        compiler_params=_cparams(1),
    )(parts, w, m, v)


_SHARDED = ("w_in", "w_up_dil", "w_up_sb", "w_out", "w_mlp_in", "w_mlp_out")
_FULL_SHAPES = {"w_in": (D_MODEL, IN_COLS), "w_up_dil": (DIL_OUT_WIDTH, D_MODEL), "w_up_sb": (SB_WIDTH, D_MODEL),
                "w_out": (D_MODEL, D_MODEL), "w_mlp_in": (D_MODEL, D_FF), "w_mlp_out": (D_FF, D_MODEL)}
_ROW_SHARDED = ("w_out", "w_mlp_out")


def _shard_shape(name):
    r, c = _FULL_SHAPES[name]
    return (r // N_DEV, c) if name in _ROW_SHARDED else (r, c // N_DEV)


def _assemble(name, gathered):
    r, c = _shard_shape(name)
    if name in _ROW_SHARDED:
        return gathered.reshape(N_DEV * r, c)
    return gathered.transpose(1, 0, 2).reshape(r, N_DEV * c)


def _chunk(name, full):
    r, c = _shard_shape(name)
    if name in _ROW_SHARDED:
        return full.reshape(N_DEV, r, c)
    return full.reshape(r, N_DEV, c).transpose(1, 0, 2)


_SMALL = (("norm_mix_g", D_MODEL), ("b_gate", 2 * D_MODEL), ("norm_mlp_g", D_MODEL), ("norm_final_g", D_MODEL))
_SMALL_N = sum(n for _, n in _SMALL) + LANES


def _pack_small(vals, tail):
    return jnp.concatenate([vals[n].reshape(1, -1) for n, _ in _SMALL] + [tail], axis=1)


def _unpack_small(vec, shapes):
    out, pos = {}, 0
    for n, width in _SMALL:
        out[n] = vec[:, pos:pos + width].reshape(shapes[n])
        pos += width
    return out, vec[:, pos:]


def kernel(x, norm_mix_g, w_in, b_gate, w_up_dil, w_up_sb, w_out, norm_mlp_g, w_mlp_in, w_mlp_out, norm_final_g, loss_target, m_norm_mix_g, m_w_in, m_b_gate, m_w_up_dil, m_w_up_sb, m_w_out, m_norm_mlp_g, m_w_mlp_in, m_w_mlp_out, m_norm_final_g, v_norm_mix_g, v_w_in, v_b_gate, v_w_up_dil, v_w_up_sb, v_w_out, v_norm_mlp_g, v_w_mlp_in, v_w_mlp_out, v_norm_final_g):
    given = dict(locals())
    s = x.shape[1]
    x0 = x.reshape(s, D_MODEL)
    target = loss_target.reshape(s, D_MODEL)
    g1 = norm_mix_g.reshape(1, D_MODEL)
    g2 = norm_mlp_g.reshape(1, D_MODEL)
    g3 = norm_final_g.reshape(1, D_MODEL)
    bg = b_gate.reshape(1, 2 * D_MODEL)
    w_shards = {n: given[n].reshape(_shard_shape(n)) for n in _SHARDED}
    m_shards = {n: given["m_" + n].reshape(_shard_shape(n)) for n in _SHARDED}
    v_shards = {n: given["v_" + n].reshape(_shard_shape(n)) for n in _SHARDED}

    shard_b = {n: w_shards[n].astype(BF16) for n in _SHARDED}
    (gathered_w_in,) = _all_gather([shard_b["w_in"]])
    w_in_f = _assemble("w_in", gathered_w_in)
    w_qkv, w_gl = w_in_f[:, :QKV_COLS], w_in_f[:, QKV_COLS:]
    full = {}

    def norm1(_, rows, consts):
        _, xh = _rms_stats(rows[0])
        return [xh * consts[0]], []

    (h1,) = _rowk("norm_mix", tm=1024, rows=[x0], consts=[g1], row_outs=[(D_MODEL, BF16)], epilogue=norm1)
    qkv, (land,) = _mm("proj_qkv", h1, w_qkv, out_dtype=BF16, tm=1024, tn=768, tk=D_MODEL,
                       rider=_Spread([shard_b["w_mlp_in"]], chunked=False))
    full["w_mlp_in"] = _assemble("w_mlp_in", land)
    gl = _mm("proj_gates", h1, w_gl, out_dtype=BF16, tm=1024, tn=1024, tk=D_MODEL)
    dil = [_dil_fwd(qkv, g) for g in range(len(DIL_GROUPS))]
    os_, lses = [d[0] for d in dil], [d[1] for d in dil]
    o_a = _dil_mix_fwd(os_, lses, 1024)
    riding = ("w_mlp_out", "w_out", "w_up_sb", "w_up_dil")
    (o_b, tot_b, sb_steps), lands = _sb_fwd(qkv, rider=_Spread([shard_b[n] for n in riding], chunked=False))
    full.update({n: _assemble(n, land) for n, land in zip(riding, lands)})
    x1, h2 = _mixer_fwd(o_a, o_b, gl, x0, bg, g2, full["w_up_dil"], full["w_up_sb"], full["w_out"], 512)
    f = _mm("mlp_in", h2, full["w_mlp_in"], out_dtype=BF16, tm=1024, tn=1024, tk=D_MODEL,
            epilogue=lambda r, _: jnp.square(jnp.maximum(r, 0.0)))

    def head(acc, rows, consts):
        x1v, tv = rows
        g3v = consts[0]
        x2 = x1v + acc
        r, xh = _rms_stats(x2)
        diff = xh * g3v - tv
        loss = (0.5 / D_MODEL) * jnp.sum(jnp.sum(diff * diff, axis=0, keepdims=True), axis=1, keepdims=True)
        dy = diff * (1.0 / D_MODEL)
        dx2, dg = _rms_bwd(dy, xh, r, g3v)
        return [dx2, dx2], [dg, jnp.broadcast_to(loss, (1, LANES))]

    dx2, dx2b, gg3, loss_part = _rowk(
        "mlp_out_loss", a=f, w=full["w_mlp_out"], tm=512, tk=D_FF, rows=[x1, target], consts=[g3],
        row_outs=[(D_MODEL, F32), (D_MODEL, BF16)], acc_outs=[D_MODEL, LANES], epilogue=head)

    da = _mm("mlp_out_bwd", dx2b, full["w_mlp_out"], tb=True, out_dtype=BF16, tm=1024, tn=1024, tk=D_MODEL, extra=f,
             epilogue=lambda r, fv: r * (2.0 * jnp.sqrt(fv.astype(F32))))
    g_w_mlp_out = _mm("grad_w_mlp_out", f, dx2b, ta=True, out_dtype=F32, tm=1024, tn=1024, tk=2048)
    g_w_mlp_in = _mm("grad_w_mlp_in", h2, da, ta=True, out_dtype=F32, tm=1024, tn=1024, tk=2048)

    def norm_bwd(acc, rows, consts):
        xv, dres = rows
        r, xh = _rms_stats(xv)
        dx, dg = _rms_bwd(acc, xh, r, consts[0])
        return [dres + dx], [dg]

    bchunk = lambda n, g: _chunk(n, g).astype(BF16)
    parts = {}
    (dx1, gg2), (parts["w_mlp_in"],) = _rowk(
        "mlp_in_bwd", a=da, w=full["w_mlp_in"], nt=True, tm=512, tk=D_FF, rows=[x1, dx2], consts=[g2],
        row_outs=[(D_MODEL, F32)], acc_outs=[D_MODEL], epilogue=norm_bwd,
        rider=_Spread([bchunk("w_mlp_in", g_w_mlp_in)], chunked=True))
    (do_a, do_b, dgl, g_w_out, g_w_ud, g_w_us, g_bg), (parts["w_mlp_out"],) = _mixer_bwd(
        dx1, o_a, o_b, gl, bg, full["w_up_dil"], full["w_up_sb"], full["w_out"], 512,
        rider=_Spread([bchunk("w_mlp_out", g_w_mlp_out)], chunked=True))
    mix = _dil_mix_bwd(do_a, os_, lses, 1024)
    small_three = {"w_out": g_w_out, "w_up_sb": g_w_us, "w_up_dil": g_w_ud}
    grads, lands = _dil_bwd(qkv, mix[0], dil[0][2], mix[3], 0,
                            rider=_Spread([bchunk(n, g) for n, g in small_three.items()], chunked=True))
    parts.update(dict(zip(small_three, lands)))
    dil_b = [grads] + [_dil_bwd(qkv, mix[g], dil[g][2], mix[3 + g], g) for g in (1, 2)]
    dq_b, dk_b, dv_b = _sb_bwd(qkv, do_b, tot_b, sb_steps)
    dproj = [d[0] for d in dil_b] + [d[1] for d in dil_b] + [d[2] for d in dil_b] + [dq_b, dk_b, dv_b, dgl]
    g_w_in = jnp.concatenate([
        _grad_cols("grad_w_in_dil", h1, dproj[:9], tm=D_MODEL, tk=1024),
        _grad_cols("grad_w_in_sb", h1, dproj[9:12], tm=D_MODEL, tk=1024),
        _grad_cols("grad_w_in_gates", h1, dproj[12:], tm=D_MODEL, tk=1024)], axis=1)
    (grad_x, gg1), (parts["w_in"],) = _rowk(
        "in_proj_bwd", a=dproj, w=w_in_f, nt=True, tm=512, tk=IN_COLS, rows=[x0, dx1], consts=[g1],
        row_outs=[(D_MODEL, F32)], acc_outs=[D_MODEL], epilogue=norm_bwd,
        rider=_Spread([bchunk("w_in", g_w_in)], chunked=True))

    small_part = _pack_small({"norm_mix_g": gg1, "b_gate": g_bg, "norm_mlp_g": gg2, "norm_final_g": gg3}, loss_part)
    (small_parts,) = _exchange([jnp.broadcast_to(small_part[None], (N_DEV, 1, _SMALL_N))])

    tags = ("grad_", "delta_", "new_m_", "new_v_")
    outs = {}
    for n, p in parts.items():
        res = _reduce_adamw("adamw_" + n, p, w_shards[n], m_shards[n], v_shards[n], 256)
        for tag, val in zip(tags, res):
            outs[tag + n] = val.reshape(given[n].shape)
    small_w = _pack_small(given, jnp.zeros((1, LANES), F32))
    small_m = _pack_small({n: given["m_" + n] for n, _ in _SMALL}, jnp.zeros((1, LANES), F32))
    small_v = _pack_small({n: given["v_" + n] for n, _ in _SMALL}, jnp.ones((1, LANES), F32))
    small_res = _reduce_adamw("adamw_replicated", small_parts, small_w, small_m, small_v, 8)

    small_shapes = {n: given[n].shape for n, _ in _SMALL}
    for tag, small in zip(tags, small_res):
        small_vals, tail = _unpack_small(small, small_shapes)
        for n, val in small_vals.items():
            outs[tag + n] = val
        if tag == "grad_":
            loss = tail[0, 0]
    names = ["norm_mix_g", "w_in", "b_gate", "w_up_dil", "w_up_sb", "w_out", "norm_mlp_g", "w_mlp_in", "w_mlp_out",
             "norm_final_g"]
    return (loss, grad_x.reshape(x.shape), *[outs["grad_" + n] for n in names], *[outs["delta_" + n] for n in names],
            *[outs["new_m_" + n] for n in names], *[outs["new_v_" + n] for n in names])
```

```python
import functools
import math

import jax
import jax.numpy as jnp
from jax import lax
from jax.experimental import pallas as pl
from jax.experimental.pallas import tpu as pltpu

_pcall = pl.pallas_call

F32 = jnp.float32
BF16 = jnp.bfloat16

D_MODEL = 1024
HEAD_DIM = 64
DIL_GROUPS = ((128, 1), (512, 4), (2048, 16))
DIL_HEADS_PER_GROUP = 4
N_DIL_HEADS = 12
N_SB_HEADS = 8
DIL_WIDTH = 768
DIL_OUT_WIDTH = 256
SB_WIDTH = 512
D_FF = 4096
BLOCK = 128
RMS_EPS = 1e-6
NEG_INF = -1e30
QKV_COLS = 3 * DIL_WIDTH + 3 * SB_WIDTH
IN_COLS = QKV_COLS + 2 * D_MODEL
N_DEV = 8

ADAM_LR = 0.001
ADAM_B1 = 0.9
ADAM_B2 = 0.999
ADAM_EPS = 1e-08
ADAM_WD = 0.01
ADAM_STEP = 10

VMEM_LIMIT = 56 * 1024 * 1024
SB_TK = 256
LANES = 128

_ARB = pltpu.ARBITRARY


def _cparams(n_axes, **kw):
    return pltpu.CompilerParams(dimension_semantics=(_ARB,) * n_axes, vmem_limit_bytes=VMEM_LIMIT, **kw)


def _dot(a, b):
    return jnp.dot(a, b, preferred_element_type=F32)


def _dot_nt(a, b):
    return lax.dot_general(a, b, (((1,), (1,)), ((), ())), preferred_element_type=F32)


def _dot_tn(a, b):
    return lax.dot_general(a, b, (((0,), (0,)), ((), ())), preferred_element_type=F32)


def _split_hi_lo(x):
    hi = x.astype(BF16)
    lo = (x - hi.astype(F32)).astype(BF16)
    return hi, lo


def _dot_hi_lo(x, m):
    hi, lo = _split_hi_lo(x)
    return _dot(hi, m) + _dot(lo, m)


def _sigmoid(x):
    return 1.0 / (1.0 + jnp.exp(-x))


_HBM = pl.BlockSpec(memory_space=pltpu.HBM)
_MESH = pl.DeviceIdType.MESH


class _Spread:
    def __init__(self, srcs, chunked):
        self.srcs, self.chunked, self.n = list(srcs), chunked, len(srcs)

    def land_shapes(self):
        return [jax.ShapeDtypeStruct((N_DEV,) + (s.shape[1:] if self.chunked else s.shape), s.dtype) for s in self.srcs]

    def scratch(self):
        dma = pltpu.SemaphoreType.DMA
        return [dma((7 * self.n,)), dma((7 * self.n,)), dma((self.n,))]

    def copies(self, src_refs, land_refs, send_sems, recv_sems, local_sems):
        x, y, c = lax.axis_index("x"), lax.axis_index("y"), lax.axis_index("c")
        me = 4 * x + 2 * y + c
        out = []
        for a, (src, land) in enumerate(zip(src_refs, land_refs)):
            out.append(pltpu.make_async_copy(src.at[me] if self.chunked else src, land.at[me], local_sems.at[a]))
            for k in range(1, N_DEV):
                px, py, pc = x ^ (k >> 2), y ^ ((k >> 1) & 1), c ^ (k & 1)
                out.append(pltpu.make_async_remote_copy(
                    src_ref=src.at[4 * px + 2 * py + pc] if self.chunked else src, dst_ref=land.at[me],
                    send_sem=send_sems.at[7 * a + k - 1], recv_sem=recv_sems.at[7 * a + k - 1],
                    device_id=(px, py, pc), device_id_type=_MESH))
        return out


def _call(body, args, rider=None, **kw):
    if rider is None:
        return _pcall(body, **kw)(*args)
    grid = kw["grid"]
    single = not isinstance(kw["out_shape"], (list, tuple))
    out_specs = [kw["out_specs"]] if single else list(kw["out_specs"])
    out_shape = [kw["out_shape"]] if single else list(kw["out_shape"])
    in_specs, scratch = list(kw["in_specs"]), list(kw.get("scratch_shapes", []))
    n_in, n_out, n_s, n = len(in_specs), len(out_shape), len(scratch), rider.n

    def hosted(*refs):
        ins, srcs = refs[:n_in], refs[n_in:n_in + n]
        outs, lands = refs[n_in + n:n_in + n + n_out], refs[n_in + n + n_out:n_in + 2 * n + n_out]
        own_scratch, sems = refs[n_in + 2 * n + n_out:n_in + 2 * n + n_out + n_s], refs[n_in + 2 * n + n_out + n_s:]
        ids = [pl.program_id(d) for d in range(len(grid))]
        first = functools.reduce(jnp.logical_and, [i == 0 for i in ids])
        last = functools.reduce(jnp.logical_and, [i == g - 1 for i, g in zip(ids, grid)])
        copies = rider.copies(srcs, lands, *sems)

        @pl.when(first)
        def _():
            for cp in copies:
                cp.start()

        body(*ins, *outs, *own_scratch)

        @pl.when(last)
        def _():
            for cp in copies:
                cp.wait()

    kw = dict(kw, in_specs=in_specs + [_HBM] * n, out_specs=out_specs + [_HBM] * n,
              out_shape=out_shape + rider.land_shapes(), scratch_shapes=scratch + rider.scratch())
    res = _pcall(hosted, **kw)(*args, *rider.srcs)
    return (res[0] if single else list(res[:n_out])), list(res[n_out:])


def _mm(name, a, b, *, ta=False, tb=False, out_dtype, tm, tn, tk, epilogue=None, extra=None, rider=None):
    m = a.shape[1] if ta else a.shape[0]
    k = a.shape[0] if ta else a.shape[1]
    n = b.shape[0] if tb else b.shape[1]
    assert (b.shape[1] if tb else b.shape[0]) == k
    tm, tn, tk = min(tm, m), min(tn, n), min(tk, k)
    assert m % tm == 0 and n % tn == 0 and k % tk == 0, (name, m, n, k, tm, tn, tk)
    nk = k // tk
    dn = (((0 if ta else 1,), (1 if tb else 0,)), ((), ()))
    in_place = nk > 1 and epilogue is None and out_dtype == F32

    def body(*refs):
        if extra is not None:
            a_ref, b_ref, e_ref, o_ref = refs[:4]
        else:
            a_ref, b_ref, o_ref = refs[:3]
            e_ref = None

        def finish(r):
            if epilogue is not None:
                r = epilogue(r, None if e_ref is None else e_ref[...])
            o_ref[...] = r.astype(out_dtype)

        part = lax.dot_general(a_ref[...].astype(BF16), b_ref[...].astype(BF16), dn, preferred_element_type=F32)
        if nk == 1:
            finish(part)
        else:
            acc_ref = o_ref if in_place else refs[-1]
            kk = pl.program_id(2)

            @pl.when(kk == 0)
            def _():
                acc_ref[...] = part

            @pl.when(kk > 0)
            def _():
                acc_ref[...] += part

            if not in_place:

                @pl.when(kk == nk - 1)
                def _():
                    finish(acc_ref[...])

    a_spec = pl.BlockSpec((tk, tm), lambda j, i, kk: (kk, i)) if ta else pl.BlockSpec((tm, tk), lambda j, i, kk: (i, kk))
    b_spec = pl.BlockSpec((tn, tk), lambda j, i, kk: (j, kk)) if tb else pl.BlockSpec((tk, tn), lambda j, i, kk: (kk, j))
    o_spec = pl.BlockSpec((tm, tn), lambda j, i, kk: (i, j))
    in_specs = [a_spec, b_spec]
    args = [a, b]
    if extra is not None:
        in_specs.append(o_spec)
        args.append(extra)
    return _call(
        body, args, rider,
        name=name,
        grid=(n // tn, m // tm, nk),
        in_specs=in_specs,
        out_specs=o_spec,
        out_shape=jax.ShapeDtypeStruct((m, n), out_dtype),
        scratch_shapes=[pltpu.VMEM((tm, tn), F32)] if (nk > 1 and not in_place) else [],
        compiler_params=_cparams(3),
    )


def _grad_cols(name, a, parts, *, tm, tk, rider=None):
    k, m = a.shape
    n = sum(p.shape[1] for p in parts)
    assert m % tm == 0 and k % tk == 0
    nk = k // tk

    def body(*refs):
        a_ref, p_refs, o_ref = refs[0], refs[1:1 + len(parts)], refs[1 + len(parts)]
        kk = pl.program_id(1)
        side_by_side = jnp.concatenate([p_ref[...].astype(BF16) for p_ref in p_refs], axis=1)
        term = _dot_tn(a_ref[...].astype(BF16), side_by_side)

        @pl.when(kk == 0)
        def _():
            o_ref[...] = term

        @pl.when(kk > 0)
        def _():
            o_ref[...] += term

    return _call(
        body, [a] + list(parts), rider,
        name=name,
        grid=(m // tm, nk),
        in_specs=[pl.BlockSpec((tk, tm), lambda i, kk: (kk, i))]
        + [pl.BlockSpec((tk, p.shape[1]), lambda i, kk: (kk, 0)) for p in parts],
        out_specs=pl.BlockSpec((tm, n), lambda i, kk: (i, 0)),
        out_shape=jax.ShapeDtypeStruct((m, n), F32),
        compiler_params=_cparams(2),
    )


def _rowk(name, *, a=None, w=None, nt=False, tm, tk=None, rows=(), consts=(), row_outs=(), acc_outs=(), epilogue,
          rider=None):
    has_mm = a is not None
    a_parts = list(a) if isinstance(a, (list, tuple)) else ([a] if has_mm else [])
    n_a = len(a_parts)
    m = a_parts[0].shape[0] if has_mm else rows[0].shape[0]
    assert m % tm == 0
    nm = m // tm
    if has_mm:
        k = sum(p.shape[1] for p in a_parts)
        n = w.shape[0] if nt else w.shape[1]
        tk = min(tk, k)
        assert k % tk == 0 and (n_a == 1 or tk == k)
        nk = k // tk
    else:
        nk = 1
    n_rows, n_consts, n_ro, n_ao = len(rows), len(consts), len(row_outs), len(acc_outs)

    def body(*refs):
        pos = 0
        if has_mm:
            a_refs, w_ref = refs[:n_a], refs[n_a]
            pos = n_a + 1
        row_refs = refs[pos:pos + n_rows]
        pos += n_rows
        const_refs = refs[pos:pos + n_consts]
        pos += n_consts
        ro_refs = refs[pos:pos + n_ro]
        pos += n_ro
        ao_refs = refs[pos:pos + n_ao]
        pos += n_ao
        i = pl.program_id(0)
        kk = pl.program_id(1)

        def finish(acc):
            ro_vals, ao_vals = epilogue(acc, [r[...] for r in row_refs], [c[...] for c in const_refs])
            for r, v in zip(ro_refs, ro_vals):
                r[...] = v.astype(r.dtype)
            for r, v in zip(ao_refs, ao_vals):

                @pl.when(i == 0)
                def _(r=r, v=v):
                    r[...] = v

                @pl.when(i > 0)
                def _(r=r, v=v):
                    r[...] += v

        if not has_mm:
            finish(None)
            return
        part, off = None, 0
        for a_ref in a_refs:
            width = a_ref.shape[1]
            cols = slice(None) if n_a == 1 else slice(off, off + width)
            av = a_ref[...].astype(BF16)
            term = _dot_nt(av, w_ref[:, cols]) if nt else _dot(av, w_ref[cols, :])
            part = term if part is None else part + term
            off += width
        if nk == 1:
            finish(part)
        else:
            acc_ref = refs[pos]

            @pl.when(kk == 0)
            def _():
                acc_ref[...] = part

            @pl.when(kk > 0)
            def _():
                acc_ref[...] += part

            @pl.when(kk == nk - 1)
            def _():
                finish(acc_ref[...])

    once = pl.Buffered(1)
    in_specs, args = [], []
    if has_mm:
        for part in a_parts:
            in_specs.append(pl.BlockSpec((tm, tk if n_a == 1 else part.shape[1]), lambda i, kk: (i, kk)))
        w_mode = once if nk == 1 else None
        in_specs.append(pl.BlockSpec((n, tk), lambda i, kk: (0, kk), pipeline_mode=w_mode) if nt
                        else pl.BlockSpec((tk, n), lambda i, kk: (kk, 0), pipeline_mode=w_mode))
        args += a_parts + [w]
    for r in rows:
        in_specs.append(pl.BlockSpec((tm, r.shape[1]), lambda i, kk: (i, 0)))
        args.append(r)
    for c in consts:
        in_specs.append(pl.BlockSpec(c.shape, lambda i, kk: (0,) * c.ndim, pipeline_mode=once))
        args.append(c)
    out_specs, out_shape = [], []
    for width, dt in row_outs:
        out_specs.append(pl.BlockSpec((tm, width), lambda i, kk: (i, 0)))
        out_shape.append(jax.ShapeDtypeStruct((m, width), dt))
    for width in acc_outs:
        out_specs.append(pl.BlockSpec((1, width), lambda i, kk: (0, 0)))
        out_shape.append(jax.ShapeDtypeStruct((1, width), F32))
    return _call(
        body, args, rider,
        name=name,
        grid=(nm, nk),
        in_specs=in_specs,
        out_specs=out_specs,
        out_shape=out_shape,
        scratch_shapes=[pltpu.VMEM((tm, n), F32)] if (has_mm and nk > 1) else [],
        compiler_params=_cparams(2),
    )


def _rms_stats(x):
    r = lax.rsqrt(jnp.mean(x * x, axis=-1, keepdims=True) + RMS_EPS)
    return r, x * r


def _rms_bwd(dh, xh, r, g):
    gy = dh * g
    dx = r * (gy - xh * jnp.mean(gy * xh, axis=-1, keepdims=True))
    return dx, jnp.sum(dh * xh, axis=0, keepdims=True)


def _alibi_slope(head):
    return 2.0 ** (-8.0 * (head + 1) / N_DIL_HEADS)


DIL_STEP_BLOCKS = 4


def _dil_band(first_block):
    qi = lax.broadcasted_iota(jnp.int32, (BLOCK, 2 * BLOCK), 0)
    kj = lax.broadcasted_iota(jnp.int32, (BLOCK, 2 * BLOCK), 1)
    steps = qi + BLOCK - kj
    valid = (steps >= 0) & (steps <= BLOCK)
    if first_block is not False:
        valid = valid & ((kj >= BLOCK) | jnp.logical_not(first_block))
    return steps.astype(F32), valid


def _dil_step_specs(ncb, cols, nblk, clamp):
    def own(col):
        return pl.BlockSpec((nblk * BLOCK, DIL_OUT_WIDTH), lambda r, i: (clamp(i), r * ncb + col))

    def before(col):
        return pl.BlockSpec((BLOCK, DIL_OUT_WIDTH), lambda r, i: (jnp.maximum(clamp(i) * nblk - 1, 0), r * ncb + col))

    return [own(cols[0]), own(cols[1]), before(cols[1]), own(cols[2]), before(cols[2])]


def _dil_fwd(qkv, group):
    window, dilation = DIL_GROUPS[group]
    s = qkv.shape[0]
    sub = s // dilation
    nb = sub // BLOCK
    assert nb * BLOCK * dilation == s and window // dilation == BLOCK
    nblk = min(DIL_STEP_BLOCKS, nb)
    assert nb % nblk == 0
    slopes = [_alibi_slope(group * DIL_HEADS_PER_GROUP + h) * dilation for h in range(DIL_HEADS_PER_GROUP)]

    def body(q_ref, kc_ref, kp_ref, vc_ref, vp_ref, o_ref, lse_ref):
        i = pl.program_id(1)
        kk_all = jnp.concatenate([kp_ref[...], kc_ref[...]], axis=0)
        vv_all = jnp.concatenate([vp_ref[...], vc_ref[...]], axis=0)
        head_id = lax.broadcasted_iota(jnp.int32, (1, DIL_OUT_WIDTH), 1) // HEAD_DIM
        chains = [(b, h) for b in range(nblk) for h in range(DIL_HEADS_PER_GROUP)]
        rows = lambda b: slice(b * BLOCK, (b + 1) * BLOCK)
        keys = lambda b: slice(b * BLOCK, (b + 2) * BLOCK)
        bands = [_dil_band(i == 0 if b == 0 else False) for b in range(nblk)]
        qs = [q_ref[rows(b), :] for b in range(nblk)]
        scores = [_dot_nt(jnp.where(head_id == h, qs[b], jnp.zeros_like(qs[b])), kk_all[keys(b)]) for b, h in chains]
        ps, lses = [], []
        for (b, h), sc in zip(chains, scores):
            steps, valid = bands[b]
            logits = jnp.where(valid, sc * (1.0 / math.sqrt(HEAD_DIM)) - slopes[h] * steps, NEG_INF)
            mx = jnp.max(logits, axis=1, keepdims=True)
            e = jnp.exp(logits - mx)
            den = jnp.sum(e, axis=1, keepdims=True)
            lses.append(mx + jnp.log(den))
            ps.append((e * (1.0 / den)).astype(BF16))
        outs = [_dot(p, vv_all[keys(b)]) for (b, h), p in zip(chains, ps)]
        for b in range(nblk):
            mine = [n for n, ch in enumerate(chains) if ch[0] == b]
            o, lse_all = outs[mine[0]], lses[mine[0]]
            for n in mine[1:]:
                o = jnp.where(head_id == chains[n][1], outs[n], o)
                lse_all = jnp.where(head_id == chains[n][1], lses[n], lse_all)
            o_ref[rows(b), :] = o
            lse_ref[rows(b), :] = jnp.broadcast_to(lse_all, o.shape)

    qkv_v, ncb, cols = _dil_view(qkv, group)
    out_spec = pl.BlockSpec((nblk * BLOCK, DIL_OUT_WIDTH), lambda r, i: (i, r))
    o, lse = _pcall(
        body,
        name=f"dil_fwd_g{group}",
        grid=(dilation, nb // nblk),
        in_specs=_dil_step_specs(ncb, cols, nblk, lambda i: i),
        out_specs=[out_spec, out_spec],
        out_shape=[jax.ShapeDtypeStruct((sub, dilation * DIL_OUT_WIDTH), F32)] * 2,
        compiler_params=_cparams(2),
    )(qkv_v, qkv_v, qkv_v, qkv_v, qkv_v)
    return o.reshape(s, DIL_OUT_WIDTH), lse.reshape(s, DIL_OUT_WIDTH), lse


def _dil_bwd(qkv, do_g, lse_g, dterm_g, group, rider=None):
    window, dilation = DIL_GROUPS[group]
    s = qkv.shape[0]
    sub = s // dilation
    nb = sub // BLOCK
    nblk = min(DIL_STEP_BLOCKS, nb)
    n_steps = nb // nblk
    slopes = [_alibi_slope(group * DIL_HEADS_PER_GROUP + h) * dilation for h in range(DIL_HEADS_PER_GROUP)]
    scale = 1.0 / math.sqrt(HEAD_DIM)
    tail = slice((nblk - 1) * BLOCK, nblk * BLOCK)

    def body(q_ref, kc_ref, kp_ref, vc_ref, vp_ref, do_ref, lse_ref, dt_ref, dq_ref, dk_ref, dv_ref, ck_ref, cv_ref):
        i = pl.program_id(1)

        @pl.when(i == 0)
        def _():
            ck_ref[...] = jnp.zeros_like(ck_ref)
            cv_ref[...] = jnp.zeros_like(cv_ref)

        @pl.when(i < n_steps)
        def _():
            kk_all = jnp.concatenate([kp_ref[...], kc_ref[...]], axis=0)
            vv_all = jnp.concatenate([vp_ref[...], vc_ref[...]], axis=0)
            lane = lax.broadcasted_iota(jnp.int32, (1, DIL_OUT_WIDTH), 1)
            head_id = lane // HEAD_DIM
            chains = [(b, h) for b in range(nblk) for h in range(DIL_HEADS_PER_GROUP)]
            rows = lambda b: slice(b * BLOCK, (b + 1) * BLOCK)
            keys = lambda b: slice(b * BLOCK, (b + 2) * BLOCK)
            bands = [_dil_band(i == 0 if b == 0 else False) for b in range(nblk)]
            qms, doms = [], []
            for b, h in chains:
                q, do = q_ref[rows(b), :], do_ref[rows(b), :]
                qms.append(jnp.where(head_id == h, q, jnp.zeros_like(q)))
                doms.append(jnp.where(head_id == h, do, jnp.zeros_like(do)))
            scores = [_dot_nt(qm, kk_all[keys(b)]) for (b, h), qm in zip(chains, qms)]
            dps = [_dot_nt(dom, vv_all[keys(b)]) for (b, h), dom in zip(chains, doms)]
            pbs, dss = [], []
            for n, (b, h) in enumerate(chains):
                steps, valid = bands[b]
                first = lane == h * HEAD_DIM
                lse = jnp.sum(jnp.where(first, lse_ref[rows(b), :], 0.0), axis=1, keepdims=True)
                dt = jnp.sum(jnp.where(first, dt_ref[rows(b), :], 0.0), axis=1, keepdims=True)
                logits = jnp.where(valid, scores[n] * scale - slopes[h] * steps, NEG_INF)
                p = jnp.where(valid, jnp.exp(logits - lse), 0.0)
                pbs.append(p.astype(BF16))
                dss.append((p * (dps[n] + dt) * scale).astype(BF16))
            dqs = [_dot(ds, kk_all[keys(b)]) for (b, h), ds in zip(chains, dss)]
            dks = [_dot_tn(ds, qm) for ds, qm in zip(dss, qms)]
            dvs = [_dot_tn(pb, dom) for pb, dom in zip(pbs, doms)]
            dkk, dvv = [], []
            for b in range(nblk):
                mine = [n for n, ch in enumerate(chains) if ch[0] == b]
                dq = dqs[mine[0]]
                for n in mine[1:]:
                    dq = jnp.where(head_id == chains[n][1], dqs[n], dq)
                dq_ref[rows(b), :] = dq.astype(dq_ref.dtype)
                dkk.append((dks[mine[0]] + dks[mine[1]]) + (dks[mine[2]] + dks[mine[3]]))
                dvv.append((dvs[mine[0]] + dvs[mine[1]]) + (dvs[mine[2]] + dvs[mine[3]]))
            for out_ref, carry_ref, parts in ((dk_ref, ck_ref, dkk), (dv_ref, cv_ref, dvv)):
                if nblk > 1:
                    out_ref[: (nblk - 1) * BLOCK, :] = carry_ref[: (nblk - 1) * BLOCK, :].astype(out_ref.dtype)
                out_ref[tail, :] = (carry_ref[tail, :] + parts[0][:BLOCK]).astype(out_ref.dtype)
                for b in range(nblk):
                    own = parts[b][BLOCK:]
                    carry_ref[rows(b), :] = own + parts[b + 1][:BLOCK] if b + 1 < nblk else own

        @pl.when(i == n_steps)
        def _():
            dk_ref[...] = ck_ref[...].astype(dk_ref.dtype)
            dv_ref[...] = cv_ref[...].astype(dv_ref.dtype)

    clamp = lambda i: jnp.minimum(i, n_steps - 1)
    qkv_v, ncb, cols = _dil_view(qkv, group)
    view = lambda t: t.reshape(sub, dilation * DIL_OUT_WIDTH)
    row_spec = pl.BlockSpec((nblk * BLOCK, DIL_OUT_WIDTH), lambda r, i: (clamp(i), r))
    late_spec = pl.BlockSpec((nblk * BLOCK, DIL_OUT_WIDTH), lambda r, i: (jnp.maximum(i - 1, 0), r))
    res = _call(
        body, (qkv_v, qkv_v, qkv_v, qkv_v, qkv_v, view(do_g), view(lse_g), view(dterm_g)), rider,
        name=f"dil_bwd_g{group}",
        grid=(dilation, n_steps + 1),
        in_specs=_dil_step_specs(ncb, cols, nblk, clamp) + [row_spec, row_spec, row_spec],
        out_specs=[row_spec, late_spec, late_spec],
        out_shape=[jax.ShapeDtypeStruct((sub, dilation * DIL_OUT_WIDTH), BF16)] * 3,
        scratch_shapes=[pltpu.VMEM((nblk * BLOCK, DIL_OUT_WIDTH), F32)] * 2,
        compiler_params=_cparams(2),
    )
    grads, lands = res if rider is not None else (res, None)
    grads = tuple(g.reshape(s, DIL_OUT_WIDTH) for g in grads)
    return grads if rider is None else (grads, lands)


def _dil_view(qkv, group):
    _, dilation = DIL_GROUPS[group]
    w = DIL_OUT_WIDTH
    if dilation == 1:
        return qkv, QKV_COLS // w, (3 * group, 3 * group + 1, 3 * group + 2)
    own = qkv[:, 3 * group * w:3 * (group + 1) * w]
    return own.reshape(qkv.shape[0] // dilation, dilation * 3 * w), 3, (0, 1, 2)


def _group_major(w_qkv):
    w = DIL_OUT_WIDTH
    ng = len(DIL_GROUPS)
    cols = [w_qkv[:, (part * ng + g) * w:(part * ng + g + 1) * w] for g in range(ng) for part in range(3)]
    return jnp.concatenate(cols + [w_qkv[:, 3 * DIL_WIDTH:]], axis=1)


def _head_block_ones():
    r = lax.broadcasted_iota(jnp.int32, (DIL_OUT_WIDTH, DIL_OUT_WIDTH), 0) // HEAD_DIM
    c = lax.broadcasted_iota(jnp.int32, (DIL_OUT_WIDTH, DIL_OUT_WIDTH), 1) // HEAD_DIM
    return jnp.where(r == c, 1.0, 0.0).astype(BF16)


def _dil_mix_weights(l0, l1, l2):
    mx = jnp.maximum(jnp.maximum(l0, l1), l2)
    e0, e1, e2 = jnp.exp(l0 - mx), jnp.exp(l1 - mx), jnp.exp(l2 - mx)
    inv = 1.0 / (e0 + e1 + e2)
    return e0 * inv, e1 * inv, e2 * inv


def _dil_mix_fwd(os_, lses, tm):
    def epi(_, rows, consts):
        o0, o1, o2, l0, l1, l2 = rows
        w0, w1, w2 = _dil_mix_weights(l0, l1, l2)
        return [w0 * o0 + w1 * o1 + w2 * o2], []

    (o_a,) = _rowk("dil_mix_fwd", tm=tm, rows=list(os_) + list(lses), row_outs=[(DIL_OUT_WIDTH, BF16)], epilogue=epi)
    return o_a


def _dil_mix_bwd(do_a, os_, lses, tm):
    def epi(_, rows, consts):
        do, o0, o1, o2, l0, l1, l2 = rows
        do = do.astype(F32)
        w0, w1, w2 = _dil_mix_weights(l0, l1, l2)
        mixed = w0 * o0 + w1 * o1 + w2 * o2
        tot = _dot_hi_lo(do * mixed, _head_block_ones())
        return [w0 * do, w1 * do, w2 * do, -w0 * tot, -w1 * tot, -w2 * tot], []

    return _rowk(
        "dil_mix_bwd", tm=tm, rows=[do_a] + list(os_) + list(lses),
        row_outs=[(DIL_OUT_WIDTH, BF16)] * 3 + [(DIL_OUT_WIDTH, F32)] * 3, epilogue=epi)


_SB_Q0 = 3 * DIL_WIDTH // LANES
_SB_K0 = _SB_Q0 + SB_WIDTH // LANES
_SB_V0 = _SB_K0 + SB_WIDTH // LANES


_EXP_CLAMP = 88.0
_SB_DEAD = 104.0


def _tri(t, op):
    r = lax.broadcasted_iota(jnp.int32, (t, t), 0)
    c = lax.broadcasted_iota(jnp.int32, (t, t), 1)
    return jnp.where(op(r, c), 1.0, 0.0).astype(BF16)


def _softplus(z):
    return jnp.maximum(z, jnp.log(1.0 + jnp.exp(jnp.minimum(z, _EXP_CLAMP))))


def _sb_chain_head(qm, kj, mask):
    z = _dot_nt(qm, kj)
    sp = _softplus(z)
    return (sp if mask is None else jnp.where(mask, sp, 0.0)), z - sp


def _sb_fwd(qkv, rider=None):
    s = qkv.shape[0]
    t = SB_TK
    assert s % (2 * t) == 0
    nq = s // (2 * t)
    n_pairs = SB_WIDTH // LANES

    def body(q_ref, k_ref, v_ref, o_ref, tot_ref, steps_ref):
        p, i = pl.program_id(0), pl.program_id(1)
        lane_hi = lax.broadcasted_iota(jnp.int32, (1, LANES), 1) // HEAD_DIM
        later = _tri(t, lambda r, c: r > c)
        causal = lax.broadcasted_iota(jnp.int32, (t, t), 1) < lax.broadcasted_iota(jnp.int32, (t, t), 0)
        qms = []
        for x in range(2):
            q = q_ref[pl.ds(x * t, t), :] * (1.0 / math.sqrt(HEAD_DIM))
            qms.append([jnp.where(lane_hi == hh, q, jnp.zeros_like(q)) for hh in range(2)])

        def tile(j):
            off = pl.multiple_of(j * t, t)
            return k_ref[pl.ds(off, t), :], v_ref[pl.ds(off, t), :]

        def step(groups, carry):
            kv = [tile(j) for _, j, _ in groups]
            chains = [(g, x, hh) for g, (x, _, _) in enumerate(groups) for hh in range(2)]
            heads = [_sb_chain_head(qms[x][hh], kv[g][0], causal if groups[g][2] else None) for g, x, hh in chains]
            sufs = [_dot(sp.astype(BF16), later) for sp, _ in heads]
            cur = [list(carry[0]), list(carry[1])]
            for (g, x, hh), (sp, lpos), suf in zip(chains, heads, sufs):
                c, acc = cur[x][hh]
                a = jnp.exp(lpos - suf - c)
                if groups[g][2]:
                    a = jnp.where(causal, a, 0.0)
                cur[x][hh] = (c + jnp.sum(sp, axis=1, keepdims=True), acc + _dot(a.astype(BF16), kv[g][1]))
            return (tuple(cur[0]), tuple(cur[1]))

        def lowest(carry):
            return jnp.min(jnp.minimum(jnp.minimum(carry[0][0][0], carry[0][1][0]),
                                       jnp.minimum(carry[1][0][0], carry[1][1][0])))

        zero = (jnp.zeros((t, 1), F32), jnp.zeros((t, LANES), F32))
        start = ((zero, zero), (zero, zero))
        carry = lax.cond(
            i == 0,
            lambda ca: step([(0, 0, True), (1, 1, True), (1, 0, False)], ca),
            lambda ca: step([(0, 2 * i, True), (1, 2 * i + 1, True), (0, 2 * i - 1, False), (1, 2 * i, False)], ca),
            start)

        def walk(state):
            n, ca, _ = state
            ca = step([(0, 2 * i - 2 - n, False), (1, 2 * i - 1 - n, False)], ca)
            return n + 1, ca, lowest(ca)

        n_more, carry, low = lax.while_loop(
            lambda st: jnp.logical_and(st[0] + 1 < 2 * i, st[2] <= _SB_DEAD), walk, (jnp.int32(0), carry, lowest(carry)))
        b_last = jnp.logical_and(jnp.logical_and(i > 0, n_more + 1 == 2 * i), low <= _SB_DEAD)
        carry = lax.cond(b_last, lambda ca: step([(1, 0, False)], ca), lambda ca: ca, carry)
        for x in range(2):
            (c0, acc0), (c1, acc1) = carry[x]
            o_ref[pl.ds(x * t, t), :] = jnp.where(lane_hi == 0, acc0, acc1).astype(o_ref.dtype)
            tot_ref[pl.ds(x * t, t), :] = jnp.where(lane_hi == 0, c0, c1)
        steps_ref[p, i] = 1 + n_more + b_last.astype(jnp.int32)

    return _call(
        body, (qkv, qkv, qkv), rider,
        name="sb_fwd",
        grid=(n_pairs, nq),
        in_specs=[
            pl.BlockSpec((2 * t, LANES), lambda p, i: (i, _SB_Q0 + p)),
            pl.BlockSpec((s, LANES), lambda p, i: (0, _SB_K0 + p)),
            pl.BlockSpec((s, LANES), lambda p, i: (0, _SB_V0 + p)),
        ],
        out_specs=[pl.BlockSpec((2 * t, LANES), lambda p, i: (i, p))] * 2 + [pl.BlockSpec(memory_space=pltpu.SMEM)],
        out_shape=[jax.ShapeDtypeStruct((s, SB_WIDTH), BF16), jax.ShapeDtypeStruct((s, SB_WIDTH), F32),
                   jax.ShapeDtypeStruct((n_pairs, nq), jnp.int32)],
        compiler_params=_cparams(2),
    )


def _sb_bwd(qkv, do_b, tot_b, n_steps):
    s = qkv.shape[0]
    t = SB_TK
    nq = s // (2 * t)
    n_pairs = SB_WIDTH // LANES
    scale = 1.0 / math.sqrt(HEAD_DIM)

    def body(steps_ref, q_ref, k_ref, v_ref, do_ref, tot_ref, dq_ref, dk_ref, dv_ref):
        p, i = pl.program_id(0), pl.program_id(1)

        @pl.when(i == 0)
        def _():
            dk_ref[...] = jnp.zeros_like(dk_ref)
            dv_ref[...] = jnp.zeros_like(dv_ref)

        lane = lax.broadcasted_iota(jnp.int32, (1, LANES), 1)
        lane_hi = lane // HEAD_DIM
        later = _tri(t, lambda r, c: r > c)
        before = _tri(t, lambda r, c: r < c)
        causal = lax.broadcasted_iota(jnp.int32, (t, t), 1) < lax.broadcasted_iota(jnp.int32, (t, t), 0)
        qms, doms, tots = [], [], []
        for x in range(2):
            rows = pl.ds(x * t, t)
            q, do, tot_all = q_ref[rows, :] * scale, do_ref[rows, :], tot_ref[rows, :]
            qms.append([jnp.where(lane_hi == hh, q, jnp.zeros_like(q)) for hh in range(2)])
            doms.append([jnp.where(lane_hi == hh, do, jnp.zeros_like(do)) for hh in range(2)])
            tots.append([jnp.sum(jnp.where(lane == hh * HEAD_DIM, tot_all, 0.0), axis=1, keepdims=True)
                         for hh in range(2)])

        def step(groups, carry):
            offs = [pl.multiple_of(j * t, t) for _, j, _ in groups]
            ks = [k_ref[pl.ds(off, t), :] for off in offs]
            vs = [v_ref[pl.ds(off, t), :] for off in offs]
            chains = [(g, x, hh) for g, (x, _, _) in enumerate(groups) for hh in range(2)]
            heads = [_sb_chain_head(qms[x][hh], ks[g], causal if groups[g][2] else None) for g, x, hh in chains]
            sufs = [_dot(sp.astype(BF16), later) for sp, _ in heads]
            das = [_dot_nt(doms[x][hh], vs[g]) for g, x, hh in chains]
            cur = [list(carry[0]), list(carry[1])]
            sigs, gs, abs_, cg_before = [], [], [], []
            for (g_, x, hh), (sp, lpos), suf, da in zip(chains, heads, sufs, das):
                cl, cg, dq = cur[x][hh]
                cl = cl + jnp.sum(sp, axis=1, keepdims=True)
                sig = jnp.exp(lpos)
                a = sig * jnp.exp(-suf - (tots[x][hh] - cl))
                if groups[g_][2]:
                    a = jnp.where(causal, a, 0.0)
                g = a * da
                sigs.append(sig)
                gs.append(g)
                abs_.append(a.astype(BF16))
                cg_before.append(cg)
                cur[x][hh] = (cl, cg + jnp.sum(g, axis=1, keepdims=True), dq)
            prefs = [_dot(g.astype(BF16), before) for g in gs]
            dvs = [_dot_tn(ab, doms[x][hh]) for (_, x, hh), ab in zip(chains, abs_)]
            dzs = []
            for (g_, x, hh), sig, g, pref, cg in zip(chains, sigs, gs, prefs, cg_before):
                dz = g - sig * (g + pref + cg)
                if groups[g_][2]:
                    dz = jnp.where(causal, dz, 0.0)
                dzs.append(dz.astype(BF16))
            dqs = [_dot(dz, ks[g_]) for (g_, x, hh), dz in zip(chains, dzs)]
            dks = [_dot_tn(dz, qms[x][hh]) for (_, x, hh), dz in zip(chains, dzs)]
            for n, (_, x, hh) in enumerate(chains):
                cl, cg, dq = cur[x][hh]
                cur[x][hh] = (cl, cg, dq + dqs[n])
            for g_, off in enumerate(offs):
                dk_ref[pl.ds(off, t), :] += dks[2 * g_] + dks[2 * g_ + 1]
                dv_ref[pl.ds(off, t), :] += dvs[2 * g_] + dvs[2 * g_ + 1]
            return (tuple(cur[0]), tuple(cur[1]))

        taken = steps_ref[p, i]
        n_full = jnp.minimum(taken, 2 * i)
        zero = (jnp.zeros((t, 1), F32), jnp.zeros((t, 1), F32), jnp.zeros((t, LANES), F32))
        carry = ((zero, zero), (zero, zero))
        carry = lax.cond(jnp.logical_and(i > 0, taken > 2 * i), lambda ca: step([(1, 0, False)], ca), lambda ca: ca,
                         carry)
        carry = lax.fori_loop(
            0, n_full - 1,
            lambda n, ca: step([(0, 2 * i - n_full + n, False), (1, 2 * i + 1 - n_full + n, False)], ca), carry)
        carry = lax.cond(
            i == 0,
            lambda ca: step([(1, 0, False), (0, 0, True), (1, 1, True)], ca),
            lambda ca: step([(0, 2 * i - 1, False), (1, 2 * i, False), (0, 2 * i, True), (1, 2 * i + 1, True)], ca),
            carry)
        for x in range(2):
            dq = jnp.where(lane_hi == 0, carry[x][0][2], carry[x][1][2])
            dq_ref[pl.ds(x * t, t), :] = (dq * scale).astype(dq_ref.dtype)

    row_spec = pl.BlockSpec((2 * t, LANES), lambda p, i, ns: (i, p))
    full_spec = pl.BlockSpec((s, LANES), lambda p, i, ns: (0, p))
    return _pcall(
        body,
        name="sb_bwd",
        grid_spec=pltpu.PrefetchScalarGridSpec(
            num_scalar_prefetch=1,
            grid=(n_pairs, nq),
            in_specs=[
                pl.BlockSpec((2 * t, LANES), lambda p, i, ns: (i, _SB_Q0 + p)),
                pl.BlockSpec((s, LANES), lambda p, i, ns: (0, _SB_K0 + p)),
                pl.BlockSpec((s, LANES), lambda p, i, ns: (0, _SB_V0 + p)),
                row_spec, row_spec,
            ],
            out_specs=[row_spec, full_spec, full_spec],
        ),
        out_shape=[jax.ShapeDtypeStruct((s, SB_WIDTH), BF16), jax.ShapeDtypeStruct((s, SB_WIDTH), F32),
                   jax.ShapeDtypeStruct((s, SB_WIDTH), F32)],
        compiler_params=_cparams(2),
    )(n_steps, qkv, qkv, qkv, do_b, tot_b)


def _gates(gl, bg):
    return _sigmoid(gl[:, :D_MODEL] + bg[:, :D_MODEL]), _sigmoid(gl[:, D_MODEL:] + bg[:, D_MODEL:])


def _mixer_fwd(o_a, o_b, gl, x0, bg, g2, w_ud, w_us, w_out, tm):
    def epi(_, rows, consts):
        oa, ob, glv, x = rows
        bgv, g2v, wud, wus, wout = consts
        ga, gb = _gates(glv, bgv)
        merged = ga * _dot(oa, wud) + gb * _dot(ob, wus)
        x1 = x + _dot(merged.astype(BF16), wout)
        r, xh = _rms_stats(x1)
        return [x1, xh * g2v], []

    return _rowk("mixer_fwd", tm=tm, rows=[o_a, o_b, gl, x0], consts=[bg, g2, w_ud, w_us, w_out],
                 row_outs=[(D_MODEL, F32), (D_MODEL, BF16)], epilogue=epi)


def _mixer_bwd(dx1, o_a, o_b, gl, bg, w_ud, w_us, w_out, tm, rider=None):
    s = dx1.shape[0]
    nm = s // tm

    def body(dx_ref, oa_ref, ob_ref, gl_ref, bg_ref, wud_ref, wus_ref, wout_ref,
             doa_ref, dob_ref, dgl_ref, gwout_ref, gwud_ref, gwus_ref, gbg_ref):
        i = pl.program_id(0)
        dxb = dx_ref[...].astype(BF16)
        oa, ob = oa_ref[...], ob_ref[...]
        ga, gb = _gates(gl_ref[...], bg_ref[...])
        ua, ub = _dot(oa, wud_ref[...]), _dot(ob, wus_ref[...])
        merged = (ga * ua + gb * ub).astype(BF16)
        dm = _dot_nt(dxb, wout_ref[...])
        dua = (dm * ga).astype(BF16)
        dub = (dm * gb).astype(BF16)
        dgla = dm * ua * ga * (1.0 - ga)
        dglb = dm * ub * gb * (1.0 - gb)
        doa_ref[...] = _dot_nt(dua, wud_ref[...]).astype(doa_ref.dtype)
        dob_ref[...] = _dot_nt(dub, wus_ref[...]).astype(dob_ref.dtype)
        dgl_ref[:, :D_MODEL] = dgla.astype(dgl_ref.dtype)
        dgl_ref[:, D_MODEL:] = dglb.astype(dgl_ref.dtype)
        parts = [(gwout_ref, _dot_tn(merged, dxb)), (gwud_ref, _dot_tn(oa, dua)), (gwus_ref, _dot_tn(ob, dub))]
        for r, v in parts:

            @pl.when(i == 0)
            def _(r=r, v=v):
                r[...] = v

            @pl.when(i > 0)
            def _(r=r, v=v):
                r[...] += v

        sa = jnp.sum(dgla, axis=0, keepdims=True)
        sb = jnp.sum(dglb, axis=0, keepdims=True)

        @pl.when(i == 0)
        def _():
            gbg_ref[:, :D_MODEL] = sa
            gbg_ref[:, D_MODEL:] = sb

        @pl.when(i > 0)
        def _():
            gbg_ref[:, :D_MODEL] += sa
            gbg_ref[:, D_MODEL:] += sb

    row = lambda w: pl.BlockSpec((tm, w), lambda i: (i, 0))
    full = lambda a: pl.BlockSpec(a.shape, lambda i: (0, 0), pipeline_mode=pl.Buffered(1))
    fshape = lambda r, c: jax.ShapeDtypeStruct((r, c), F32)
    return _call(
        body, (dx1, o_a, o_b, gl, bg, w_ud, w_us, w_out), rider,
        name="mixer_bwd",
        grid=(nm,),
        in_specs=[row(D_MODEL), row(DIL_OUT_WIDTH), row(SB_WIDTH), row(2 * D_MODEL),
                  full(bg), full(w_ud), full(w_us), full(w_out)],
        out_specs=[row(DIL_OUT_WIDTH), row(SB_WIDTH), row(2 * D_MODEL),
                   pl.BlockSpec((D_MODEL, D_MODEL), lambda i: (0, 0)),
                   pl.BlockSpec((DIL_OUT_WIDTH, D_MODEL), lambda i: (0, 0)),
                   pl.BlockSpec((SB_WIDTH, D_MODEL), lambda i: (0, 0)),
                   pl.BlockSpec((1, 2 * D_MODEL), lambda i: (0, 0))],
        out_shape=[jax.ShapeDtypeStruct((s, DIL_OUT_WIDTH), BF16), jax.ShapeDtypeStruct((s, SB_WIDTH), BF16),
                   jax.ShapeDtypeStruct((s, 2 * D_MODEL), BF16),
                   fshape(D_MODEL, D_MODEL), fshape(DIL_OUT_WIDTH, D_MODEL), fshape(SB_WIDTH, D_MODEL),
                   fshape(1, 2 * D_MODEL)],
        compiler_params=_cparams(1),
    )


def _all_gather(shards):
    n = len(shards)

    def body(*refs):
        x_refs, out_refs = refs[:n], refs[n:2 * n]
        send_sems, recv_sems, local_sems = refs[2 * n:]
        x, y, c = lax.axis_index("x"), lax.axis_index("y"), lax.axis_index("c")
        me, sibling = (x, y, c), (x, y, 1 - c)
        chips = [(1 - x, y), (x, 1 - y), (1 - x, 1 - y)]

        def slot(a, px, py, pc):
            return out_refs[a].at[4 * px + 2 * py + pc]

        def copy(a, k, block, to, own=False):
            return pltpu.make_async_remote_copy(
                src_ref=x_refs[a] if own else slot(a, *block), dst_ref=slot(a, *block),
                send_sem=send_sems.at[7 * a + k], recv_sem=recv_sems.at[7 * a + k], device_id=to, device_id_type=_MESH)

        mine = [pltpu.make_async_copy(x_refs[a], slot(a, *me), local_sems.at[a]) for a in range(n)]
        for cp in mine:
            cp.start()
        first = []
        for a in range(n):
            first.append(copy(a, 0, me, sibling, own=True))
            first += [copy(a, 1 + j, me, (*chip, c), own=True) for j, chip in enumerate(chips)]
        for cp in first:
            cp.start()
        passed = []
        for a in range(n):
            for j, chip in enumerate(chips):
                copy(a, 1 + j, (*chip, c), me).wait_recv()
                passed.append(copy(a, 4 + j, (*chip, c), sibling))
                passed[-1].start()
        for a in range(n):
            copy(a, 0, sibling, me).wait_recv()
            for j, chip in enumerate(chips):
                copy(a, 4 + j, (*chip, 1 - c), me).wait_recv()
        for cp in first + passed:
            cp.wait_send()
        for cp in mine:
            cp.wait()

    return _pcall(
        body,
        name="all_gather_weights",
        in_specs=[_HBM] * n,
        out_specs=[_HBM] * n,
        out_shape=[jax.ShapeDtypeStruct((N_DEV,) + s.shape, s.dtype) for s in shards],
        scratch_shapes=[pltpu.SemaphoreType.DMA((7 * n,)), pltpu.SemaphoreType.DMA((7 * n,)),
                        pltpu.SemaphoreType.DMA((n,))],
    )(*shards)


def _exchange(chunks):
    n = len(chunks)

    def body(*refs):
        g_refs, o_refs = refs[:n], refs[n:2 * n]
        send_sems, recv_sems, local_sems = refs[2 * n:]
        x, y, c = lax.axis_index("x"), lax.axis_index("y"), lax.axis_index("c")
        me = 4 * x + 2 * y + c
        own = [pltpu.make_async_copy(g_refs[a].at[me], o_refs[a].at[me], local_sems.at[a]) for a in range(n)]
        for cp in own:
            cp.start()
        copies = []
        for a in range(n):
            for k in range(1, N_DEV):
                px, py, pc = x ^ (k >> 2), y ^ ((k >> 1) & 1), c ^ (k & 1)
                peer = 4 * px + 2 * py + pc
                copies.append(pltpu.make_async_remote_copy(
                    src_ref=g_refs[a].at[peer], dst_ref=o_refs[a].at[me], send_sem=send_sems.at[7 * a + k - 1],
                    recv_sem=recv_sems.at[7 * a + k - 1], device_id=(px, py, pc), device_id_type=_MESH))
        for cp in copies:
            cp.start()
        for cp in copies:
            cp.wait()
        for cp in own:
            cp.wait()

    return _pcall(
        body,
        name="exchange_grads",
        in_specs=[_HBM] * n,
        out_specs=[_HBM] * n,
        out_shape=[jax.ShapeDtypeStruct(g.shape, g.dtype) for g in chunks],
        scratch_shapes=[pltpu.SemaphoreType.DMA((7 * n,)), pltpu.SemaphoreType.DMA((7 * n,)),
                        pltpu.SemaphoreType.DMA((n,))],
    )(*chunks)


def _reduce_adamw(name, parts, w, m, v, tr):
    _, rows, cols = parts.shape
    tr = min(tr, rows)
    assert rows % tr == 0
    c1 = 1.0 / (1.0 - ADAM_B1 ** ADAM_STEP)
    c2 = 1.0 / (1.0 - ADAM_B2 ** ADAM_STEP)

    def body(p_ref, w_ref, m_ref, v_ref, g_out, d_out, m_out, v_out):
        g = p_ref[0].astype(F32)
        for d in range(1, N_DEV):
            g = g + p_ref[d].astype(F32)
        mn = ADAM_B1 * m_ref[...] + (1.0 - ADAM_B1) * g
        vn = ADAM_B2 * v_ref[...] + (1.0 - ADAM_B2) * (g * g)
        g_out[...] = g
        m_out[...] = mn
        v_out[...] = vn
        d_out[...] = -ADAM_LR * ((mn * c1) / (jnp.sqrt(vn * c2) + ADAM_EPS) + ADAM_WD * w_ref[...])

    spec = pl.BlockSpec((tr, cols), lambda i: (i, 0))
    return _pcall(
        body,
        name=name,
        grid=(rows // tr,),
        in_specs=[pl.BlockSpec((N_DEV, tr, cols), lambda i: (0, i, 0)), spec, spec, spec],
        out_specs=[spec] * 4,
        out_shape=[jax.ShapeDtypeStruct((rows, cols), F32)] * 4,
        compiler_params=_cparams(1),
    )(parts, w, m, v)


_SHARDED = ("w_in", "w_up_dil", "w_up_sb", "w_out", "w_mlp_in", "w_mlp_out")
_FULL_SHAPES = {"w_in": (D_MODEL, IN_COLS), "w_up_dil": (DIL_OUT_WIDTH, D_MODEL), "w_up_sb": (SB_WIDTH, D_MODEL),
                "w_out": (D_MODEL, D_MODEL), "w_mlp_in": (D_MODEL, D_FF), "w_mlp_out": (D_FF, D_MODEL)}
_ROW_SHARDED = ("w_out", "w_mlp_out")


def _shard_shape(name):
    r, c = _FULL_SHAPES[name]
    return (r // N_DEV, c) if name in _ROW_SHARDED else (r, c // N_DEV)


def _assemble(name, gathered):
    r, c = _shard_shape(name)
    if name in _ROW_SHARDED:
        return gathered.reshape(N_DEV * r, c)
    return gathered.transpose(1, 0, 2).reshape(r, N_DEV * c)


def _chunk(name, full):
    r, c = _shard_shape(name)
    if name in _ROW_SHARDED:
        return full.reshape(N_DEV, r, c)
    return full.reshape(r, N_DEV, c).transpose(1, 0, 2)


_SMALL = (("norm_mix_g", D_MODEL), ("b_gate", 2 * D_MODEL), ("norm_mlp_g", D_MODEL), ("norm_final_g", D_MODEL))
_SMALL_N = sum(n for _, n in _SMALL) + LANES


def _pack_small(vals, tail):
    return jnp.concatenate([vals[n].reshape(1, -1) for n, _ in _SMALL] + [tail], axis=1)


def _unpack_small(vec, shapes):
    out, pos = {}, 0
    for n, width in _SMALL:
        out[n] = vec[:, pos:pos + width].reshape(shapes[n])
        pos += width
    return out, vec[:, pos:]


def kernel(x, norm_mix_g, w_in, b_gate, w_up_dil, w_up_sb, w_out, norm_mlp_g, w_mlp_in, w_mlp_out, norm_final_g, loss_target, m_norm_mix_g, m_w_in, m_b_gate, m_w_up_dil, m_w_up_sb, m_w_out, m_norm_mlp_g, m_w_mlp_in, m_w_mlp_out, m_norm_final_g, v_norm_mix_g, v_w_in, v_b_gate, v_w_up_dil, v_w_up_sb, v_w_out, v_norm_mlp_g, v_w_mlp_in, v_w_mlp_out, v_norm_final_g):
    given = dict(locals())
    s = x.shape[1]
    x0 = x.reshape(s, D_MODEL)
    target = loss_target.reshape(s, D_MODEL)
    g1 = norm_mix_g.reshape(1, D_MODEL)
    g2 = norm_mlp_g.reshape(1, D_MODEL)
    g3 = norm_final_g.reshape(1, D_MODEL)
    bg = b_gate.reshape(1, 2 * D_MODEL)
    w_shards = {n: given[n].reshape(_shard_shape(n)) for n in _SHARDED}
    m_shards = {n: given["m_" + n].reshape(_shard_shape(n)) for n in _SHARDED}
    v_shards = {n: given["v_" + n].reshape(_shard_shape(n)) for n in _SHARDED}

    shard_b = {n: w_shards[n].astype(BF16) for n in _SHARDED}
    (gathered_w_in,) = _all_gather([shard_b["w_in"]])
    w_in_f = _assemble("w_in", gathered_w_in)
    w_qkv, w_gl = _group_major(w_in_f[:, :QKV_COLS]), w_in_f[:, QKV_COLS:]
    full = {}

    def norm1(_, rows, consts):
        _, xh = _rms_stats(rows[0])
        return [xh * consts[0]], []

    (h1,) = _rowk("norm_mix", tm=1024, rows=[x0], consts=[g1], row_outs=[(D_MODEL, BF16)], epilogue=norm1)
    qkv, (land,) = _mm("proj_qkv", h1, w_qkv, out_dtype=BF16, tm=1024, tn=768, tk=D_MODEL,
                       rider=_Spread([shard_b["w_mlp_in"]], chunked=False))
    full["w_mlp_in"] = _assemble("w_mlp_in", land)
    gl = _mm("proj_gates", h1, w_gl, out_dtype=BF16, tm=1024, tn=1024, tk=D_MODEL)
    dil = [_dil_fwd(qkv, g) for g in range(len(DIL_GROUPS))]
    os_, lses = [d[0] for d in dil], [d[1] for d in dil]
    o_a = _dil_mix_fwd(os_, lses, 1024)
    riding = ("w_mlp_out", "w_out", "w_up_sb", "w_up_dil")
    (o_b, tot_b, sb_steps), lands = _sb_fwd(qkv, rider=_Spread([shard_b[n] for n in riding], chunked=False))
    full.update({n: _assemble(n, land) for n, land in zip(riding, lands)})
    x1, h2 = _mixer_fwd(o_a, o_b, gl, x0, bg, g2, full["w_up_dil"], full["w_up_sb"], full["w_out"], 512)
    f = _mm("mlp_in", h2, full["w_mlp_in"], out_dtype=BF16, tm=1024, tn=1024, tk=D_MODEL,
            epilogue=lambda r, _: jnp.square(jnp.maximum(r, 0.0)))

    def head(acc, rows, consts):
        x1v, tv = rows
        g3v = consts[0]
        x2 = x1v + acc
        r, xh = _rms_stats(x2)
        diff = xh * g3v - tv
        loss = (0.5 / D_MODEL) * jnp.sum(jnp.sum(diff * diff, axis=0, keepdims=True), axis=1, keepdims=True)
        dy = diff * (1.0 / D_MODEL)
        dx2, dg = _rms_bwd(dy, xh, r, g3v)
        return [dx2, dx2], [dg, jnp.broadcast_to(loss, (1, LANES))]

    dx2, dx2b, gg3, loss_part = _rowk(
        "mlp_out_loss", a=f, w=full["w_mlp_out"], tm=512, tk=D_FF, rows=[x1, target], consts=[g3],
        row_outs=[(D_MODEL, F32), (D_MODEL, BF16)], acc_outs=[D_MODEL, LANES], epilogue=head)

    da = _mm("mlp_out_bwd", dx2b, full["w_mlp_out"], tb=True, out_dtype=BF16, tm=1024, tn=1024, tk=D_MODEL, extra=f,
             epilogue=lambda r, fv: r * (2.0 * jnp.sqrt(fv.astype(F32))))
    g_w_mlp_out = _mm("grad_w_mlp_out", f, dx2b, ta=True, out_dtype=F32, tm=1024, tn=1024, tk=2048)
    g_w_mlp_in = _mm("grad_w_mlp_in", h2, da, ta=True, out_dtype=F32, tm=1024, tn=1024, tk=2048)

    def norm_bwd(acc, rows, consts):
        xv, dres = rows
        r, xh = _rms_stats(xv)
        dx, dg = _rms_bwd(acc, xh, r, consts[0])
        return [dres + dx], [dg]

    bchunk = lambda n, g: _chunk(n, g).astype(BF16)
    parts = {}
    (dx1, gg2), (parts["w_mlp_in"],) = _rowk(
        "mlp_in_bwd", a=da, w=full["w_mlp_in"], nt=True, tm=512, tk=D_FF, rows=[x1, dx2], consts=[g2],
        row_outs=[(D_MODEL, F32)], acc_outs=[D_MODEL], epilogue=norm_bwd,
        rider=_Spread([bchunk("w_mlp_in", g_w_mlp_in)], chunked=True))
    (do_a, do_b, dgl, g_w_out, g_w_ud, g_w_us, g_bg), (parts["w_mlp_out"],) = _mixer_bwd(
        dx1, o_a, o_b, gl, bg, full["w_up_dil"], full["w_up_sb"], full["w_out"], 512,
        rider=_Spread([bchunk("w_mlp_out", g_w_mlp_out)], chunked=True))
    mix = _dil_mix_bwd(do_a, os_, lses, 1024)
    small_three = {"w_out": g_w_out, "w_up_sb": g_w_us, "w_up_dil": g_w_ud}
    grads, lands = _dil_bwd(qkv, mix[0], dil[0][2], mix[3], 0,
                            rider=_Spread([bchunk(n, g) for n, g in small_three.items()], chunked=True))
    parts.update(dict(zip(small_three, lands)))
    dil_b = [grads] + [_dil_bwd(qkv, mix[g], dil[g][2], mix[3 + g], g) for g in (1, 2)]
    dq_b, dk_b, dv_b = _sb_bwd(qkv, do_b, tot_b, sb_steps)
    dproj = [d[0] for d in dil_b] + [d[1] for d in dil_b] + [d[2] for d in dil_b] + [dq_b, dk_b, dv_b, dgl]
    g_w_in = jnp.concatenate([
        _grad_cols("grad_w_in_dil", h1, dproj[:9], tm=D_MODEL, tk=1024),
        _grad_cols("grad_w_in_sb", h1, dproj[9:12], tm=D_MODEL, tk=1024),
        _grad_cols("grad_w_in_gates", h1, dproj[12:], tm=D_MODEL, tk=1024)], axis=1)
    (grad_x, gg1), (parts["w_in"],) = _rowk(
        "in_proj_bwd", a=dproj, w=w_in_f, nt=True, tm=512, tk=IN_COLS, rows=[x0, dx1], consts=[g1],
        row_outs=[(D_MODEL, F32)], acc_outs=[D_MODEL], epilogue=norm_bwd,
        rider=_Spread([bchunk("w_in", g_w_in)], chunked=True))

    small_part = _pack_small({"norm_mix_g": gg1, "b_gate": g_bg, "norm_mlp_g": gg2, "norm_final_g": gg3}, loss_part)
    (small_parts,) = _exchange([jnp.broadcast_to(small_part[None], (N_DEV, 1, _SMALL_N))])

    tags = ("grad_", "delta_", "new_m_", "new_v_")
    outs = {}
    for n, p in parts.items():
        res = _reduce_adamw("adamw_" + n, p, w_shards[n], m_shards[n], v_shards[n], 256)
        for tag, val in zip(tags, res):
            outs[tag + n] = val.reshape(given[n].shape)
    small_w = _pack_small(given, jnp.zeros((1, LANES), F32))
    small_m = _pack_small({n: given["m_" + n] for n, _ in _SMALL}, jnp.zeros((1, LANES), F32))
    small_v = _pack_small({n: given["v_" + n] for n, _ in _SMALL}, jnp.ones((1, LANES), F32))
    small_res = _reduce_adamw("adamw_replicated", small_parts, small_w, small_m, small_v, 8)

    small_shapes = {n: given[n].shape for n, _ in _SMALL}
    for tag, small in zip(tags, small_res):
        small_vals, tail = _unpack_small(small, small_shapes)
        for n, val in small_vals.items():
            outs[tag + n] = val
        if tag == "grad_":
            loss = tail[0, 0]
    names = ["norm_mix_g", "w_in", "b_gate", "w_up_dil", "w_up_sb", "w_out", "norm_mlp_g", "w_mlp_in", "w_mlp_out",
             "norm_final_g"]
    return (loss, grad_x.reshape(x.shape), *[outs["grad_" + n] for n in names], *[outs["delta_" + n] for n in names],
            *[outs["new_m_" + n] for n in names], *[outs["new_v_" + n] for n in names])
```

```python
import functools
import math

import jax
import jax.numpy as jnp
from jax import lax
from jax.experimental import pallas as pl
from jax.experimental.pallas import tpu as pltpu

_pcall = pl.pallas_call

F32 = jnp.float32
BF16 = jnp.bfloat16

D_MODEL = 1024
HEAD_DIM = 64
DIL_GROUPS = ((128, 1), (512, 4), (2048, 16))
DIL_HEADS_PER_GROUP = 4
N_DIL_HEADS = 12
N_SB_HEADS = 8
DIL_WIDTH = 768
DIL_OUT_WIDTH = 256
SB_WIDTH = 512
D_FF = 4096
BLOCK = 128
RMS_EPS = 1e-6
NEG_INF = -1e30
QKV_COLS = 3 * DIL_WIDTH + 3 * SB_WIDTH
IN_COLS = QKV_COLS + 2 * D_MODEL
N_DEV = 8

ADAM_LR = 0.001
ADAM_B1 = 0.9
ADAM_B2 = 0.999
ADAM_EPS = 1e-08
ADAM_WD = 0.01
ADAM_STEP = 10

VMEM_LIMIT = 56 * 1024 * 1024
SB_TK = 256
LANES = 128

_ARB = pltpu.ARBITRARY


def _cparams(n_axes, **kw):
    return pltpu.CompilerParams(dimension_semantics=(_ARB,) * n_axes, vmem_limit_bytes=VMEM_LIMIT, **kw)


def _dot(a, b):
    return jnp.dot(a, b, preferred_element_type=F32)


def _dot_nt(a, b):
    return lax.dot_general(a, b, (((1,), (1,)), ((), ())), preferred_element_type=F32)


def _dot_tn(a, b):
    return lax.dot_general(a, b, (((0,), (0,)), ((), ())), preferred_element_type=F32)


def _split_hi_lo(x):
    hi = x.astype(BF16)
    lo = (x - hi.astype(F32)).astype(BF16)
    return hi, lo


def _dot_hi_lo(x, m):
    hi, lo = _split_hi_lo(x)
    return _dot(hi, m) + _dot(lo, m)


def _sigmoid(x):
    return 1.0 / (1.0 + jnp.exp(-x))


_HBM = pl.BlockSpec(memory_space=pltpu.HBM)
_MESH = pl.DeviceIdType.MESH


class _Spread:
    def __init__(self, srcs, chunked):
        self.srcs, self.chunked, self.n = list(srcs), chunked, len(srcs)

    def land_shapes(self):
        return [jax.ShapeDtypeStruct((N_DEV,) + (s.shape[1:] if self.chunked else s.shape), s.dtype) for s in self.srcs]

    def scratch(self):
        dma = pltpu.SemaphoreType.DMA
        return [dma((7 * self.n,)), dma((7 * self.n,)), dma((self.n,))]

    def copies(self, src_refs, land_refs, send_sems, recv_sems, local_sems):
        x, y, c = lax.axis_index("x"), lax.axis_index("y"), lax.axis_index("c")
        me = 4 * x + 2 * y + c
        out = []
        for a, (src, land) in enumerate(zip(src_refs, land_refs)):
            out.append(pltpu.make_async_copy(src.at[me] if self.chunked else src, land.at[me], local_sems.at[a]))
            for k in range(1, N_DEV):
                px, py, pc = x ^ (k >> 2), y ^ ((k >> 1) & 1), c ^ (k & 1)
                out.append(pltpu.make_async_remote_copy(
                    src_ref=src.at[4 * px + 2 * py + pc] if self.chunked else src, dst_ref=land.at[me],
                    send_sem=send_sems.at[7 * a + k - 1], recv_sem=recv_sems.at[7 * a + k - 1],
                    device_id=(px, py, pc), device_id_type=_MESH))
        return out


def _call(body, args, rider=None, **kw):
    if rider is None:
        return _pcall(body, **kw)(*args)
    grid = kw["grid"]
    single = not isinstance(kw["out_shape"], (list, tuple))
    out_specs = [kw["out_specs"]] if single else list(kw["out_specs"])
    out_shape = [kw["out_shape"]] if single else list(kw["out_shape"])
    in_specs, scratch = list(kw["in_specs"]), list(kw.get("scratch_shapes", []))
    n_in, n_out, n_s, n = len(in_specs), len(out_shape), len(scratch), rider.n

    def hosted(*refs):
        ins, srcs = refs[:n_in], refs[n_in:n_in + n]
        outs, lands = refs[n_in + n:n_in + n + n_out], refs[n_in + n + n_out:n_in + 2 * n + n_out]
        own_scratch, sems = refs[n_in + 2 * n + n_out:n_in + 2 * n + n_out + n_s], refs[n_in + 2 * n + n_out + n_s:]
        ids = [pl.program_id(d) for d in range(len(grid))]
        first = functools.reduce(jnp.logical_and, [i == 0 for i in ids])
        last = functools.reduce(jnp.logical_and, [i == g - 1 for i, g in zip(ids, grid)])
        copies = rider.copies(srcs, lands, *sems)

        @pl.when(first)
        def _():
            for cp in copies:
                cp.start()

        body(*ins, *outs, *own_scratch)

        @pl.when(last)
        def _():
            for cp in copies:
                cp.wait()

    kw = dict(kw, in_specs=in_specs + [_HBM] * n, out_specs=out_specs + [_HBM] * n,
              out_shape=out_shape + rider.land_shapes(), scratch_shapes=scratch + rider.scratch())
    res = _pcall(hosted, **kw)(*args, *rider.srcs)
    return (res[0] if single else list(res[:n_out])), list(res[n_out:])


def _mm(name, a, b, *, ta=False, tb=False, out_dtype, tm, tn, tk, epilogue=None, extra=None, rider=None):
    m = a.shape[1] if ta else a.shape[0]
    k = a.shape[0] if ta else a.shape[1]
    n = b.shape[0] if tb else b.shape[1]
    assert (b.shape[1] if tb else b.shape[0]) == k
    tm, tn, tk = min(tm, m), min(tn, n), min(tk, k)
    assert m % tm == 0 and n % tn == 0 and k % tk == 0, (name, m, n, k, tm, tn, tk)
    nk = k // tk
    dn = (((0 if ta else 1,), (1 if tb else 0,)), ((), ()))
    in_place = nk > 1 and epilogue is None and out_dtype == F32

    def body(*refs):
        if extra is not None:
            a_ref, b_ref, e_ref, o_ref = refs[:4]
        else:
            a_ref, b_ref, o_ref = refs[:3]
            e_ref = None

        def finish(r):
            if epilogue is not None:
                r = epilogue(r, None if e_ref is None else e_ref[...])
            o_ref[...] = r.astype(out_dtype)

        part = lax.dot_general(a_ref[...].astype(BF16), b_ref[...].astype(BF16), dn, preferred_element_type=F32)
        if nk == 1:
            finish(part)
        else:
            acc_ref = o_ref if in_place else refs[-1]
            kk = pl.program_id(2)

            @pl.when(kk == 0)
            def _():
                acc_ref[...] = part

            @pl.when(kk > 0)
            def _():
                acc_ref[...] += part

            if not in_place:

                @pl.when(kk == nk - 1)
                def _():
                    finish(acc_ref[...])

    a_spec = pl.BlockSpec((tk, tm), lambda j, i, kk: (kk, i)) if ta else pl.BlockSpec((tm, tk), lambda j, i, kk: (i, kk))
    b_spec = pl.BlockSpec((tn, tk), lambda j, i, kk: (j, kk)) if tb else pl.BlockSpec((tk, tn), lambda j, i, kk: (kk, j))
    o_spec = pl.BlockSpec((tm, tn), lambda j, i, kk: (i, j))
    in_specs = [a_spec, b_spec]
    args = [a, b]
    if extra is not None:
        in_specs.append(o_spec)
        args.append(extra)
    return _call(
        body, args, rider,
        name=name,
        grid=(n // tn, m // tm, nk),
        in_specs=in_specs,
        out_specs=o_spec,
        out_shape=jax.ShapeDtypeStruct((m, n), out_dtype),
        scratch_shapes=[pltpu.VMEM((tm, tn), F32)] if (nk > 1 and not in_place) else [],
        compiler_params=_cparams(3),
    )


def _grad_cols(name, a, parts, *, tm, tk, rider=None):
    k, m = a.shape
    n = sum(p.shape[1] for p in parts)
    assert m % tm == 0 and k % tk == 0
    nk = k // tk

    def body(*refs):
        a_ref, p_refs, o_ref = refs[0], refs[1:1 + len(parts)], refs[1 + len(parts)]
        kk = pl.program_id(1)
        side_by_side = jnp.concatenate([p_ref[...].astype(BF16) for p_ref in p_refs], axis=1)
        term = _dot_tn(a_ref[...].astype(BF16), side_by_side)

        @pl.when(kk == 0)
        def _():
            o_ref[...] = term

        @pl.when(kk > 0)
        def _():
            o_ref[...] += term

    return _call(
        body, [a] + list(parts), rider,
        name=name,
        grid=(m // tm, nk),
        in_specs=[pl.BlockSpec((tk, tm), lambda i, kk: (kk, i))]
        + [pl.BlockSpec((tk, p.shape[1]), lambda i, kk: (kk, 0)) for p in parts],
        out_specs=pl.BlockSpec((tm, n), lambda i, kk: (i, 0)),
        out_shape=jax.ShapeDtypeStruct((m, n), F32),
        compiler_params=_cparams(2),
    )


def _rowk(name, *, a=None, w=None, nt=False, tm, tk=None, rows=(), consts=(), row_outs=(), acc_outs=(), epilogue,
          rider=None):
    has_mm = a is not None
    a_parts = list(a) if isinstance(a, (list, tuple)) else ([a] if has_mm else [])
    n_a = len(a_parts)
    m = a_parts[0].shape[0] if has_mm else rows[0].shape[0]
    assert m % tm == 0
    nm = m // tm
    if has_mm:
        k = sum(p.shape[1] for p in a_parts)
        n = w.shape[0] if nt else w.shape[1]
        tk = min(tk, k)
        assert k % tk == 0 and (n_a == 1 or tk == k)
        nk = k // tk
    else:
        nk = 1
    n_rows, n_consts, n_ro, n_ao = len(rows), len(consts), len(row_outs), len(acc_outs)

    def body(*refs):
        pos = 0
        if has_mm:
            a_refs, w_ref = refs[:n_a], refs[n_a]
            pos = n_a + 1
        row_refs = refs[pos:pos + n_rows]
        pos += n_rows
        const_refs = refs[pos:pos + n_consts]
        pos += n_consts
        ro_refs = refs[pos:pos + n_ro]
        pos += n_ro
        ao_refs = refs[pos:pos + n_ao]
        pos += n_ao
        i = pl.program_id(0)
        kk = pl.program_id(1)

        def finish(acc):
            ro_vals, ao_vals = epilogue(acc, [r[...] for r in row_refs], [c[...] for c in const_refs])
            for r, v in zip(ro_refs, ro_vals):
                r[...] = v.astype(r.dtype)
            for r, v in zip(ao_refs, ao_vals):

                @pl.when(i == 0)
                def _(r=r, v=v):
                    r[...] = v

                @pl.when(i > 0)
                def _(r=r, v=v):
                    r[...] += v

        if not has_mm:
            finish(None)
            return
        part, off = None, 0
        for a_ref in a_refs:
            width = a_ref.shape[1]
            cols = slice(None) if n_a == 1 else slice(off, off + width)
            av = a_ref[...].astype(BF16)
            term = _dot_nt(av, w_ref[:, cols]) if nt else _dot(av, w_ref[cols, :])
            part = term if part is None else part + term
            off += width
        if nk == 1:
            finish(part)
        else:
            acc_ref = refs[pos]

            @pl.when(kk == 0)
            def _():
                acc_ref[...] = part

            @pl.when(kk > 0)
            def _():
                acc_ref[...] += part

            @pl.when(kk == nk - 1)
            def _():
                finish(acc_ref[...])

    once = pl.Buffered(1)
    in_specs, args = [], []
    if has_mm:
        for part in a_parts:
            in_specs.append(pl.BlockSpec((tm, tk if n_a == 1 else part.shape[1]), lambda i, kk: (i, kk)))
        w_mode = once if nk == 1 else None
        in_specs.append(pl.BlockSpec((n, tk), lambda i, kk: (0, kk), pipeline_mode=w_mode) if nt
                        else pl.BlockSpec((tk, n), lambda i, kk: (kk, 0), pipeline_mode=w_mode))
        args += a_parts + [w]
    for r in rows:
        in_specs.append(pl.BlockSpec((tm, r.shape[1]), lambda i, kk: (i, 0)))
        args.append(r)
    for c in consts:
        in_specs.append(pl.BlockSpec(c.shape, lambda i, kk: (0,) * c.ndim, pipeline_mode=once))
        args.append(c)
    out_specs, out_shape = [], []
    for width, dt in row_outs:
        out_specs.append(pl.BlockSpec((tm, width), lambda i, kk: (i, 0)))
        out_shape.append(jax.ShapeDtypeStruct((m, width), dt))
    for width in acc_outs:
        out_specs.append(pl.BlockSpec((1, width), lambda i, kk: (0, 0)))
        out_shape.append(jax.ShapeDtypeStruct((1, width), F32))
    return _call(
        body, args, rider,
        name=name,
        grid=(nm, nk),
        in_specs=in_specs,
        out_specs=out_specs,
        out_shape=out_shape,
        scratch_shapes=[pltpu.VMEM((tm, n), F32)] if (has_mm and nk > 1) else [],
        compiler_params=_cparams(2),
    )


def _rms_stats(x):
    r = lax.rsqrt(jnp.mean(x * x, axis=-1, keepdims=True) + RMS_EPS)
    return r, x * r


def _rms_bwd(dh, xh, r, g):
    gy = dh * g
    dx = r * (gy - xh * jnp.mean(gy * xh, axis=-1, keepdims=True))
    return dx, jnp.sum(dh * xh, axis=0, keepdims=True)


def _alibi_slope(head):
    return 2.0 ** (-8.0 * (head + 1) / N_DIL_HEADS)


DIL_STEP_BLOCKS = 4


def _dil_band(first_block):
    qi = lax.broadcasted_iota(jnp.int32, (BLOCK, 2 * BLOCK), 0)
    kj = lax.broadcasted_iota(jnp.int32, (BLOCK, 2 * BLOCK), 1)
    steps = qi + BLOCK - kj
    valid = (steps >= 0) & (steps <= BLOCK)
    if first_block is not False:
        valid = valid & ((kj >= BLOCK) | jnp.logical_not(first_block))
    return steps.astype(F32), valid


def _dil_step_specs(ncb, cols, nblk, clamp):
    def own(col):
        return pl.BlockSpec((nblk * BLOCK, DIL_OUT_WIDTH), lambda r, i: (clamp(i), r * ncb + col))

    def before(col):
        return pl.BlockSpec((BLOCK, DIL_OUT_WIDTH), lambda r, i: (jnp.maximum(clamp(i) * nblk - 1, 0), r * ncb + col))

    return [own(cols[0]), own(cols[1]), before(cols[1]), own(cols[2]), before(cols[2])]


DIL_RELAYOUT_ROWS = 1024


def _dil_relayout(name, xs, dilation, to_view, col_block=0, width=None):
    d = dilation
    tm = DIL_RELAYOUT_ROWS
    rows = tm // d
    if to_view:
        s = xs[0].shape[0]
        widths = [width or x.shape[1] for x in xs]
    else:
        s = xs[0].shape[0] * d
        widths = [v.shape[1] // d for v in xs]
    assert s % tm == 0 and all(w % LANES == 0 for w in widths)
    n = len(xs)

    def body(*refs):
        in_refs, out_refs, scratch = refs[:n], refs[n:2 * n], refs[2 * n:]
        for src, dst, scr, w in zip(in_refs, out_refs, scratch, widths):
            for j in range(w // LANES):
                slab = slice(j * LANES, (j + 1) * LANES)
                if to_view:
                    scr[j] = src[:, slab].astype(F32)
                    for r in range(d):
                        dst[:, r * w + j * LANES:r * w + (j + 1) * LANES] = (
                            scr[j, pl.ds(r, rows, stride=d), :].astype(dst.dtype))
                else:
                    for r in range(d):
                        scr[j, pl.ds(r, rows, stride=d), :] = (
                            src[:, r * w + j * LANES:r * w + (j + 1) * LANES].astype(F32))
                    dst[:, slab] = scr[j].astype(dst.dtype)

    natural = [pl.BlockSpec((tm, w), lambda i: (i, col_block)) for w in widths]
    viewed = [pl.BlockSpec((rows, d * w), lambda i: (i, 0)) for w in widths]
    return _pcall(
        body,
        name=name,
        grid=(s // tm,),
        in_specs=natural if to_view else viewed,
        out_specs=viewed if to_view else natural,
        out_shape=[jax.ShapeDtypeStruct((s // d, d * w) if to_view else (s, w), x.dtype) for x, w in zip(xs, widths)],
        scratch_shapes=[pltpu.VMEM((w // LANES, tm, LANES), F32) for w in widths],
        compiler_params=_cparams(1),
    )(*xs)


def _dil_fwd(view, group):
    window, dilation = DIL_GROUPS[group]
    qkv_v, ncb, cols = view
    sub = qkv_v.shape[0]
    s = sub * dilation
    nb = sub // BLOCK
    assert nb * BLOCK * dilation == s and window // dilation == BLOCK
    nblk = min(DIL_STEP_BLOCKS, nb)
    assert nb % nblk == 0
    slopes = [_alibi_slope(group * DIL_HEADS_PER_GROUP + h) * dilation for h in range(DIL_HEADS_PER_GROUP)]

    def body(q_ref, kc_ref, kp_ref, vc_ref, vp_ref, o_ref, lse_ref):
        i = pl.program_id(1)
        kk_all = jnp.concatenate([kp_ref[...], kc_ref[...]], axis=0)
        vv_all = jnp.concatenate([vp_ref[...], vc_ref[...]], axis=0)
        head_id = lax.broadcasted_iota(jnp.int32, (1, DIL_OUT_WIDTH), 1) // HEAD_DIM
        chains = [(b, h) for b in range(nblk) for h in range(DIL_HEADS_PER_GROUP)]
        rows = lambda b: slice(b * BLOCK, (b + 1) * BLOCK)
        keys = lambda b: slice(b * BLOCK, (b + 2) * BLOCK)
        bands = [_dil_band(i == 0 if b == 0 else False) for b in range(nblk)]
        qs = [q_ref[rows(b), :] for b in range(nblk)]
        scores = [_dot_nt(jnp.where(head_id == h, qs[b], jnp.zeros_like(qs[b])), kk_all[keys(b)]) for b, h in chains]
        ps, lses = [], []
        for (b, h), sc in zip(chains, scores):
            steps, valid = bands[b]
            logits = jnp.where(valid, sc * (1.0 / math.sqrt(HEAD_DIM)) - slopes[h] * steps, NEG_INF)
            mx = jnp.max(logits, axis=1, keepdims=True)
            e = jnp.exp(logits - mx)
            den = jnp.sum(e, axis=1, keepdims=True)
            lses.append(mx + jnp.log(den))
            ps.append((e * (1.0 / den)).astype(BF16))
        outs = [_dot(p, vv_all[keys(b)]) for (b, h), p in zip(chains, ps)]
        for b in range(nblk):
            mine = [n for n, ch in enumerate(chains) if ch[0] == b]
            o, lse_all = outs[mine[0]], lses[mine[0]]
            for n in mine[1:]:
                o = jnp.where(head_id == chains[n][1], outs[n], o)
                lse_all = jnp.where(head_id == chains[n][1], lses[n], lse_all)
            o_ref[rows(b), :] = o
            lse_ref[rows(b), :] = jnp.broadcast_to(lse_all, o.shape)

    out_spec = pl.BlockSpec((nblk * BLOCK, DIL_OUT_WIDTH), lambda r, i: (i, r))
    o, lse = _pcall(
        body,
        name=f"dil_fwd_g{group}",
        grid=(dilation, nb // nblk),
        in_specs=_dil_step_specs(ncb, cols, nblk, lambda i: i),
        out_specs=[out_spec, out_spec],
        out_shape=[jax.ShapeDtypeStruct((sub, dilation * DIL_OUT_WIDTH), F32)] * 2,
        compiler_params=_cparams(2),
    )(qkv_v, qkv_v, qkv_v, qkv_v, qkv_v)
    if dilation == 1:
        return o, lse, lse
    o_n, lse_n = _dil_relayout(f"dil_fwd_rows_g{group}", [o, lse], dilation, to_view=False)
    return o_n, lse_n, lse


def _dil_bwd(view, do_g, lse_g, dterm_g, group, rider=None):
    window, dilation = DIL_GROUPS[group]
    qkv_v, ncb, cols = view
    sub = qkv_v.shape[0]
    nb = sub // BLOCK
    nblk = min(DIL_STEP_BLOCKS, nb)
    n_steps = nb // nblk
    slopes = [_alibi_slope(group * DIL_HEADS_PER_GROUP + h) * dilation for h in range(DIL_HEADS_PER_GROUP)]
    scale = 1.0 / math.sqrt(HEAD_DIM)
    tail = slice((nblk - 1) * BLOCK, nblk * BLOCK)

    def body(q_ref, kc_ref, kp_ref, vc_ref, vp_ref, do_ref, lse_ref, dt_ref, dq_ref, dk_ref, dv_ref, ck_ref, cv_ref):
        i = pl.program_id(1)

        @pl.when(i == 0)
        def _():
            ck_ref[...] = jnp.zeros_like(ck_ref)
            cv_ref[...] = jnp.zeros_like(cv_ref)

        @pl.when(i < n_steps)
        def _():
            kk_all = jnp.concatenate([kp_ref[...], kc_ref[...]], axis=0)
            vv_all = jnp.concatenate([vp_ref[...], vc_ref[...]], axis=0)
            lane = lax.broadcasted_iota(jnp.int32, (1, DIL_OUT_WIDTH), 1)
            head_id = lane // HEAD_DIM
            chains = [(b, h) for b in range(nblk) for h in range(DIL_HEADS_PER_GROUP)]
            rows = lambda b: slice(b * BLOCK, (b + 1) * BLOCK)
            keys = lambda b: slice(b * BLOCK, (b + 2) * BLOCK)
            bands = [_dil_band(i == 0 if b == 0 else False) for b in range(nblk)]
            qms, doms = [], []
            for b, h in chains:
                q, do = q_ref[rows(b), :], do_ref[rows(b), :]
                qms.append(jnp.where(head_id == h, q, jnp.zeros_like(q)))
                doms.append(jnp.where(head_id == h, do, jnp.zeros_like(do)))
            scores = [_dot_nt(qm, kk_all[keys(b)]) for (b, h), qm in zip(chains, qms)]
            dps = [_dot_nt(dom, vv_all[keys(b)]) for (b, h), dom in zip(chains, doms)]
            pbs, dss = [], []
            for n, (b, h) in enumerate(chains):
                steps, valid = bands[b]
                first = lane == h * HEAD_DIM
                lse = jnp.sum(jnp.where(first, lse_ref[rows(b), :], 0.0), axis=1, keepdims=True)
                dt = jnp.sum(jnp.where(first, dt_ref[rows(b), :], 0.0), axis=1, keepdims=True)
                logits = jnp.where(valid, scores[n] * scale - slopes[h] * steps, NEG_INF)
                p = jnp.where(valid, jnp.exp(logits - lse), 0.0)
                pbs.append(p.astype(BF16))
                dss.append((p * (dps[n] + dt) * scale).astype(BF16))
            dqs = [_dot(ds, kk_all[keys(b)]) for (b, h), ds in zip(chains, dss)]
            dks = [_dot_tn(ds, qm) for ds, qm in zip(dss, qms)]
            dvs = [_dot_tn(pb, dom) for pb, dom in zip(pbs, doms)]
            dkk, dvv = [], []
            for b in range(nblk):
                mine = [n for n, ch in enumerate(chains) if ch[0] == b]
                dq = dqs[mine[0]]
                for n in mine[1:]:
                    dq = jnp.where(head_id == chains[n][1], dqs[n], dq)
                dq_ref[rows(b), :] = dq.astype(dq_ref.dtype)
                dkk.append((dks[mine[0]] + dks[mine[1]]) + (dks[mine[2]] + dks[mine[3]]))
                dvv.append((dvs[mine[0]] + dvs[mine[1]]) + (dvs[mine[2]] + dvs[mine[3]]))
            for out_ref, carry_ref, parts in ((dk_ref, ck_ref, dkk), (dv_ref, cv_ref, dvv)):
                if nblk > 1:
                    out_ref[: (nblk - 1) * BLOCK, :] = carry_ref[: (nblk - 1) * BLOCK, :].astype(out_ref.dtype)
                out_ref[tail, :] = (carry_ref[tail, :] + parts[0][:BLOCK]).astype(out_ref.dtype)
                for b in range(nblk):
                    own = parts[b][BLOCK:]
                    carry_ref[rows(b), :] = own + parts[b + 1][:BLOCK] if b + 1 < nblk else own

        @pl.when(i == n_steps)
        def _():
            dk_ref[...] = ck_ref[...].astype(dk_ref.dtype)
            dv_ref[...] = cv_ref[...].astype(dv_ref.dtype)

    clamp = lambda i: jnp.minimum(i, n_steps - 1)
    if dilation > 1:
        do_g, dterm_g = _dil_relayout(f"dil_bwd_view_g{group}", [do_g, dterm_g], dilation, to_view=True)
    row_spec = pl.BlockSpec((nblk * BLOCK, DIL_OUT_WIDTH), lambda r, i: (clamp(i), r))
    late_spec = pl.BlockSpec((nblk * BLOCK, DIL_OUT_WIDTH), lambda r, i: (jnp.maximum(i - 1, 0), r))
    res = _call(
        body, (qkv_v, qkv_v, qkv_v, qkv_v, qkv_v, do_g, lse_g, dterm_g), rider,
        name=f"dil_bwd_g{group}",
        grid=(dilation, n_steps + 1),
        in_specs=_dil_step_specs(ncb, cols, nblk, clamp) + [row_spec, row_spec, row_spec],
        out_specs=[row_spec, late_spec, late_spec],
        out_shape=[jax.ShapeDtypeStruct((sub, dilation * DIL_OUT_WIDTH), BF16)] * 3,
        scratch_shapes=[pltpu.VMEM((nblk * BLOCK, DIL_OUT_WIDTH), F32)] * 2,
        compiler_params=_cparams(2),
    )
    grads, lands = res if rider is not None else (res, None)
    if dilation > 1:
        grads = _dil_relayout(f"dil_bwd_rows_g{group}", list(grads), dilation, to_view=False)
    return tuple(grads) if rider is None else (tuple(grads), lands)


def _dil_view(qkv, group):
    _, dilation = DIL_GROUPS[group]
    w = DIL_OUT_WIDTH
    if dilation == 1:
        return qkv, QKV_COLS // w, (3 * group, 3 * group + 1, 3 * group + 2)
    (own,) = _dil_relayout(f"dil_view_g{group}", [qkv], dilation, to_view=True, col_block=group, width=3 * w)
    return own, 3, (0, 1, 2)


def _group_major(w_qkv):
    w = DIL_OUT_WIDTH
    ng = len(DIL_GROUPS)
    cols = [w_qkv[:, (part * ng + g) * w:(part * ng + g + 1) * w] for g in range(ng) for part in range(3)]
    return jnp.concatenate(cols + [w_qkv[:, 3 * DIL_WIDTH:]], axis=1)


def _head_block_ones():
    r = lax.broadcasted_iota(jnp.int32, (DIL_OUT_WIDTH, DIL_OUT_WIDTH), 0) // HEAD_DIM
    c = lax.broadcasted_iota(jnp.int32, (DIL_OUT_WIDTH, DIL_OUT_WIDTH), 1) // HEAD_DIM
    return jnp.where(r == c, 1.0, 0.0).astype(BF16)


def _dil_mix_weights(l0, l1, l2):
    mx = jnp.maximum(jnp.maximum(l0, l1), l2)
    e0, e1, e2 = jnp.exp(l0 - mx), jnp.exp(l1 - mx), jnp.exp(l2 - mx)
    inv = 1.0 / (e0 + e1 + e2)
    return e0 * inv, e1 * inv, e2 * inv


def _dil_mix_fwd(os_, lses, tm):
    def epi(_, rows, consts):
        o0, o1, o2, l0, l1, l2 = rows
        w0, w1, w2 = _dil_mix_weights(l0, l1, l2)
        return [w0 * o0 + w1 * o1 + w2 * o2], []

    (o_a,) = _rowk("dil_mix_fwd", tm=tm, rows=list(os_) + list(lses), row_outs=[(DIL_OUT_WIDTH, BF16)], epilogue=epi)
    return o_a


def _dil_mix_bwd(do_a, os_, lses, tm):
    def epi(_, rows, consts):
        do, o0, o1, o2, l0, l1, l2 = rows
        do = do.astype(F32)
        w0, w1, w2 = _dil_mix_weights(l0, l1, l2)
        mixed = w0 * o0 + w1 * o1 + w2 * o2
        tot = _dot_hi_lo(do * mixed, _head_block_ones())
        return [w0 * do, w1 * do, w2 * do, -w0 * tot, -w1 * tot, -w2 * tot], []

    return _rowk(
        "dil_mix_bwd", tm=tm, rows=[do_a] + list(os_) + list(lses),
        row_outs=[(DIL_OUT_WIDTH, BF16)] * 3 + [(DIL_OUT_WIDTH, F32)] * 3, epilogue=epi)


_SB_Q0 = 3 * DIL_WIDTH // LANES
_SB_K0 = _SB_Q0 + SB_WIDTH // LANES
_SB_V0 = _SB_K0 + SB_WIDTH // LANES


_EXP_CLAMP = 88.0
_SB_DEAD = 104.0


def _tri(t, op):
    r = lax.broadcasted_iota(jnp.int32, (t, t), 0)
    c = lax.broadcasted_iota(jnp.int32, (t, t), 1)
    return jnp.where(op(r, c), 1.0, 0.0).astype(BF16)


def _softplus(z):
    return jnp.maximum(z, jnp.log(1.0 + jnp.exp(jnp.minimum(z, _EXP_CLAMP))))


def _sb_chain_head(qm, kj, mask):
    z = _dot_nt(qm, kj)
    sp = _softplus(z)
    return (sp if mask is None else jnp.where(mask, sp, 0.0)), z - sp


def _sb_fwd(qkv, rider=None):
    s = qkv.shape[0]
    t = SB_TK
    assert s % (2 * t) == 0
    nq = s // (2 * t)
    n_pairs = SB_WIDTH // LANES

    def body(q_ref, k_ref, v_ref, o_ref, tot_ref, steps_ref):
        p, i = pl.program_id(0), pl.program_id(1)
        lane_hi = lax.broadcasted_iota(jnp.int32, (1, LANES), 1) // HEAD_DIM
        later = _tri(t, lambda r, c: r > c)
        causal = lax.broadcasted_iota(jnp.int32, (t, t), 1) < lax.broadcasted_iota(jnp.int32, (t, t), 0)
        qms = []
        for x in range(2):
            q = q_ref[pl.ds(x * t, t), :] * (1.0 / math.sqrt(HEAD_DIM))
            qms.append([jnp.where(lane_hi == hh, q, jnp.zeros_like(q)) for hh in range(2)])

        def tile(j):
            off = pl.multiple_of(j * t, t)
            return k_ref[pl.ds(off, t), :], v_ref[pl.ds(off, t), :]

        def step(groups, carry):
            kv = [tile(j) for _, j, _ in groups]
            chains = [(g, x, hh) for g, (x, _, _) in enumerate(groups) for hh in range(2)]
            heads = [_sb_chain_head(qms[x][hh], kv[g][0], causal if groups[g][2] else None) for g, x, hh in chains]
            sufs = [_dot(sp.astype(BF16), later) for sp, _ in heads]
            cur = [list(carry[0]), list(carry[1])]
            for (g, x, hh), (sp, lpos), suf in zip(chains, heads, sufs):
                c, acc = cur[x][hh]
                a = jnp.exp(lpos - suf - c)
                if groups[g][2]:
                    a = jnp.where(causal, a, 0.0)
                cur[x][hh] = (c + jnp.sum(sp, axis=1, keepdims=True), acc + _dot(a.astype(BF16), kv[g][1]))
            return (tuple(cur[0]), tuple(cur[1]))

        def lowest(carry):
            return jnp.min(jnp.minimum(jnp.minimum(carry[0][0][0], carry[0][1][0]),
                                       jnp.minimum(carry[1][0][0], carry[1][1][0])))

        zero = (jnp.zeros((t, 1), F32), jnp.zeros((t, LANES), F32))
        start = ((zero, zero), (zero, zero))
        carry = lax.cond(
            i == 0,
            lambda ca: step([(0, 0, True), (1, 1, True), (1, 0, False)], ca),
            lambda ca: step([(0, 2 * i, True), (1, 2 * i + 1, True), (0, 2 * i - 1, False), (1, 2 * i, False)], ca),
            start)

        def walk(state):
            n, ca, _ = state
            ca = step([(0, 2 * i - 2 - n, False), (1, 2 * i - 1 - n, False)], ca)
            return n + 1, ca, lowest(ca)

        n_more, carry, low = lax.while_loop(
            lambda st: jnp.logical_and(st[0] + 1 < 2 * i, st[2] <= _SB_DEAD), walk, (jnp.int32(0), carry, lowest(carry)))
        b_last = jnp.logical_and(jnp.logical_and(i > 0, n_more + 1 == 2 * i), low <= _SB_DEAD)
        carry = lax.cond(b_last, lambda ca: step([(1, 0, False)], ca), lambda ca: ca, carry)
        for x in range(2):
            (c0, acc0), (c1, acc1) = carry[x]
            o_ref[pl.ds(x * t, t), :] = jnp.where(lane_hi == 0, acc0, acc1).astype(o_ref.dtype)
            tot_ref[pl.ds(x * t, t), :] = jnp.where(lane_hi == 0, c0, c1)
        steps_ref[p, i] = 1 + n_more + b_last.astype(jnp.int32)

    return _call(
        body, (qkv, qkv, qkv), rider,
        name="sb_fwd",
        grid=(n_pairs, nq),
        in_specs=[
            pl.BlockSpec((2 * t, LANES), lambda p, i: (i, _SB_Q0 + p)),
            pl.BlockSpec((s, LANES), lambda p, i: (0, _SB_K0 + p)),
            pl.BlockSpec((s, LANES), lambda p, i: (0, _SB_V0 + p)),
        ],
        out_specs=[pl.BlockSpec((2 * t, LANES), lambda p, i: (i, p))] * 2 + [pl.BlockSpec(memory_space=pltpu.SMEM)],
        out_shape=[jax.ShapeDtypeStruct((s, SB_WIDTH), BF16), jax.ShapeDtypeStruct((s, SB_WIDTH), F32),
                   jax.ShapeDtypeStruct((n_pairs, nq), jnp.int32)],
        compiler_params=_cparams(2),
    )


def _sb_bwd(qkv, do_b, tot_b, n_steps):
    s = qkv.shape[0]
    t = SB_TK
    nq = s // (2 * t)
    n_pairs = SB_WIDTH // LANES
    scale = 1.0 / math.sqrt(HEAD_DIM)

    def body(steps_ref, q_ref, k_ref, v_ref, do_ref, tot_ref, dq_ref, dk_ref, dv_ref):
        p, i = pl.program_id(0), pl.program_id(1)

        @pl.when(i == 0)
        def _():
            dk_ref[...] = jnp.zeros_like(dk_ref)
            dv_ref[...] = jnp.zeros_like(dv_ref)

        lane = lax.broadcasted_iota(jnp.int32, (1, LANES), 1)
        lane_hi = lane // HEAD_DIM
        later = _tri(t, lambda r, c: r > c)
        before = _tri(t, lambda r, c: r < c)
        causal = lax.broadcasted_iota(jnp.int32, (t, t), 1) < lax.broadcasted_iota(jnp.int32, (t, t), 0)
        qms, doms, tots = [], [], []
        for x in range(2):
            rows = pl.ds(x * t, t)
            q, do, tot_all = q_ref[rows, :] * scale, do_ref[rows, :], tot_ref[rows, :]
            qms.append([jnp.where(lane_hi == hh, q, jnp.zeros_like(q)) for hh in range(2)])
            doms.append([jnp.where(lane_hi == hh, do, jnp.zeros_like(do)) for hh in range(2)])
            tots.append([jnp.sum(jnp.where(lane == hh * HEAD_DIM, tot_all, 0.0), axis=1, keepdims=True)
                         for hh in range(2)])

        def step(groups, carry):
            offs = [pl.multiple_of(j * t, t) for _, j, _ in groups]
            ks = [k_ref[pl.ds(off, t), :] for off in offs]
            vs = [v_ref[pl.ds(off, t), :] for off in offs]
            chains = [(g, x, hh) for g, (x, _, _) in enumerate(groups) for hh in range(2)]
            heads = [_sb_chain_head(qms[x][hh], ks[g], causal if groups[g][2] else None) for g, x, hh in chains]
            sufs = [_dot(sp.astype(BF16), later) for sp, _ in heads]
            das = [_dot_nt(doms[x][hh], vs[g]) for g, x, hh in chains]
            cur = [list(carry[0]), list(carry[1])]
            sigs, gs, abs_, cg_before = [], [], [], []
            for (g_, x, hh), (sp, lpos), suf, da in zip(chains, heads, sufs, das):
                cl, cg, dq = cur[x][hh]
                cl = cl + jnp.sum(sp, axis=1, keepdims=True)
                sig = jnp.exp(lpos)
                a = sig * jnp.exp(-suf - (tots[x][hh] - cl))
                if groups[g_][2]:
                    a = jnp.where(causal, a, 0.0)
                g = a * da
                sigs.append(sig)
                gs.append(g)
                abs_.append(a.astype(BF16))
                cg_before.append(cg)
                cur[x][hh] = (cl, cg + jnp.sum(g, axis=1, keepdims=True), dq)
            prefs = [_dot(g.astype(BF16), before) for g in gs]
            dvs = [_dot_tn(ab, doms[x][hh]) for (_, x, hh), ab in zip(chains, abs_)]
            dzs = []
            for (g_, x, hh), sig, g, pref, cg in zip(chains, sigs, gs, prefs, cg_before):
                dz = g - sig * (g + pref + cg)
                if groups[g_][2]:
                    dz = jnp.where(causal, dz, 0.0)
                dzs.append(dz.astype(BF16))
            dqs = [_dot(dz, ks[g_]) for (g_, x, hh), dz in zip(chains, dzs)]
            dks = [_dot_tn(dz, qms[x][hh]) for (_, x, hh), dz in zip(chains, dzs)]
            for n, (_, x, hh) in enumerate(chains):
                cl, cg, dq = cur[x][hh]
                cur[x][hh] = (cl, cg, dq + dqs[n])
            for g_, off in enumerate(offs):
                dk_ref[pl.ds(off, t), :] += dks[2 * g_] + dks[2 * g_ + 1]
                dv_ref[pl.ds(off, t), :] += dvs[2 * g_] + dvs[2 * g_ + 1]
            return (tuple(cur[0]), tuple(cur[1]))

        taken = steps_ref[p, i]
        n_full = jnp.minimum(taken, 2 * i)
        zero = (jnp.zeros((t, 1), F32), jnp.zeros((t, 1), F32), jnp.zeros((t, LANES), F32))
        carry = ((zero, zero), (zero, zero))
        carry = lax.cond(jnp.logical_and(i > 0, taken > 2 * i), lambda ca: step([(1, 0, False)], ca), lambda ca: ca,
                         carry)
        carry = lax.fori_loop(
            0, n_full - 1,
            lambda n, ca: step([(0, 2 * i - n_full + n, False), (1, 2 * i + 1 - n_full + n, False)], ca), carry)
        carry = lax.cond(
            i == 0,
            lambda ca: step([(1, 0, False), (0, 0, True), (1, 1, True)], ca),
            lambda ca: step([(0, 2 * i - 1, False), (1, 2 * i, False), (0, 2 * i, True), (1, 2 * i + 1, True)], ca),
            carry)
        for x in range(2):
            dq = jnp.where(lane_hi == 0, carry[x][0][2], carry[x][1][2])
            dq_ref[pl.ds(x * t, t), :] = (dq * scale).astype(dq_ref.dtype)

    row_spec = pl.BlockSpec((2 * t, LANES), lambda p, i, ns: (i, p))
    full_spec = pl.BlockSpec((s, LANES), lambda p, i, ns: (0, p))
    return _pcall(
        body,
        name="sb_bwd",
        grid_spec=pltpu.PrefetchScalarGridSpec(
            num_scalar_prefetch=1,
            grid=(n_pairs, nq),
            in_specs=[
                pl.BlockSpec((2 * t, LANES), lambda p, i, ns: (i, _SB_Q0 + p)),
                pl.BlockSpec((s, LANES), lambda p, i, ns: (0, _SB_K0 + p)),
                pl.BlockSpec((s, LANES), lambda p, i, ns: (0, _SB_V0 + p)),
                row_spec, row_spec,
            ],
            out_specs=[row_spec, full_spec, full_spec],
        ),
        out_shape=[jax.ShapeDtypeStruct((s, SB_WIDTH), BF16), jax.ShapeDtypeStruct((s, SB_WIDTH), F32),
                   jax.ShapeDtypeStruct((s, SB_WIDTH), F32)],
        compiler_params=_cparams(2),
    )(n_steps, qkv, qkv, qkv, do_b, tot_b)


def _gates(gl, bg):
    return _sigmoid(gl[:, :D_MODEL] + bg[:, :D_MODEL]), _sigmoid(gl[:, D_MODEL:] + bg[:, D_MODEL:])


def _mixer_fwd(o_a, o_b, gl, x0, bg, g2, w_ud, w_us, w_out, tm):
    def epi(_, rows, consts):
        oa, ob, glv, x = rows
        bgv, g2v, wud, wus, wout = consts
        ga, gb = _gates(glv, bgv)
        merged = ga * _dot(oa, wud) + gb * _dot(ob, wus)
        x1 = x + _dot(merged.astype(BF16), wout)
        r, xh = _rms_stats(x1)
        return [x1, xh * g2v], []

    return _rowk("mixer_fwd", tm=tm, rows=[o_a, o_b, gl, x0], consts=[bg, g2, w_ud, w_us, w_out],
                 row_outs=[(D_MODEL, F32), (D_MODEL, BF16)], epilogue=epi)


def _mixer_bwd(dx1, o_a, o_b, gl, bg, w_ud, w_us, w_out, tm, rider=None):
    s = dx1.shape[0]
    nm = s // tm

    def body(dx_ref, oa_ref, ob_ref, gl_ref, bg_ref, wud_ref, wus_ref, wout_ref,
             doa_ref, dob_ref, dgl_ref, gwout_ref, gwud_ref, gwus_ref, gbg_ref):
        i = pl.program_id(0)
        dxb = dx_ref[...].astype(BF16)
        oa, ob = oa_ref[...], ob_ref[...]
        ga, gb = _gates(gl_ref[...], bg_ref[...])
        ua, ub = _dot(oa, wud_ref[...]), _dot(ob, wus_ref[...])
        merged = (ga * ua + gb * ub).astype(BF16)
        dm = _dot_nt(dxb, wout_ref[...])
        dua = (dm * ga).astype(BF16)
        dub = (dm * gb).astype(BF16)
        dgla = dm * ua * ga * (1.0 - ga)
        dglb = dm * ub * gb * (1.0 - gb)
        doa_ref[...] = _dot_nt(dua, wud_ref[...]).astype(doa_ref.dtype)
        dob_ref[...] = _dot_nt(dub, wus_ref[...]).astype(dob_ref.dtype)
        dgl_ref[:, :D_MODEL] = dgla.astype(dgl_ref.dtype)
        dgl_ref[:, D_MODEL:] = dglb.astype(dgl_ref.dtype)
        parts = [(gwout_ref, _dot_tn(merged, dxb)), (gwud_ref, _dot_tn(oa, dua)), (gwus_ref, _dot_tn(ob, dub))]
        for r, v in parts:

            @pl.when(i == 0)
            def _(r=r, v=v):
                r[...] = v

            @pl.when(i > 0)
            def _(r=r, v=v):
                r[...] += v

        sa = jnp.sum(dgla, axis=0, keepdims=True)
        sb = jnp.sum(dglb, axis=0, keepdims=True)

        @pl.when(i == 0)
        def _():
            gbg_ref[:, :D_MODEL] = sa
            gbg_ref[:, D_MODEL:] = sb

        @pl.when(i > 0)
        def _():
            gbg_ref[:, :D_MODEL] += sa
            gbg_ref[:, D_MODEL:] += sb

    row = lambda w: pl.BlockSpec((tm, w), lambda i: (i, 0))
    full = lambda a: pl.BlockSpec(a.shape, lambda i: (0, 0), pipeline_mode=pl.Buffered(1))
    fshape = lambda r, c: jax.ShapeDtypeStruct((r, c), F32)
    return _call(
        body, (dx1, o_a, o_b, gl, bg, w_ud, w_us, w_out), rider,
        name="mixer_bwd",
        grid=(nm,),
        in_specs=[row(D_MODEL), row(DIL_OUT_WIDTH), row(SB_WIDTH), row(2 * D_MODEL),
                  full(bg), full(w_ud), full(w_us), full(w_out)],
        out_specs=[row(DIL_OUT_WIDTH), row(SB_WIDTH), row(2 * D_MODEL),
                   pl.BlockSpec((D_MODEL, D_MODEL), lambda i: (0, 0)),
                   pl.BlockSpec((DIL_OUT_WIDTH, D_MODEL), lambda i: (0, 0)),
                   pl.BlockSpec((SB_WIDTH, D_MODEL), lambda i: (0, 0)),
                   pl.BlockSpec((1, 2 * D_MODEL), lambda i: (0, 0))],
        out_shape=[jax.ShapeDtypeStruct((s, DIL_OUT_WIDTH), BF16), jax.ShapeDtypeStruct((s, SB_WIDTH), BF16),
                   jax.ShapeDtypeStruct((s, 2 * D_MODEL), BF16),
                   fshape(D_MODEL, D_MODEL), fshape(DIL_OUT_WIDTH, D_MODEL), fshape(SB_WIDTH, D_MODEL),
                   fshape(1, 2 * D_MODEL)],
        compiler_params=_cparams(1),
    )


def _all_gather(shards):
    n = len(shards)

    def body(*refs):
        x_refs, out_refs = refs[:n], refs[n:2 * n]
        send_sems, recv_sems, local_sems = refs[2 * n:]
        x, y, c = lax.axis_index("x"), lax.axis_index("y"), lax.axis_index("c")
        me, sibling = (x, y, c), (x, y, 1 - c)
        chips = [(1 - x, y), (x, 1 - y), (1 - x, 1 - y)]

        def slot(a, px, py, pc):
            return out_refs[a].at[4 * px + 2 * py + pc]

        def copy(a, k, block, to, own=False):
            return pltpu.make_async_remote_copy(
                src_ref=x_refs[a] if own else slot(a, *block), dst_ref=slot(a, *block),
                send_sem=send_sems.at[7 * a + k], recv_sem=recv_sems.at[7 * a + k], device_id=to, device_id_type=_MESH)

        mine = [pltpu.make_async_copy(x_refs[a], slot(a, *me), local_sems.at[a]) for a in range(n)]
        for cp in mine:
            cp.start()
        first = []
        for a in range(n):
            first.append(copy(a, 0, me, sibling, own=True))
            first += [copy(a, 1 + j, me, (*chip, c), own=True) for j, chip in enumerate(chips)]
        for cp in first:
            cp.start()
        passed = []
        for a in range(n):
            for j, chip in enumerate(chips):
                copy(a, 1 + j, (*chip, c), me).wait_recv()
                passed.append(copy(a, 4 + j, (*chip, c), sibling))
                passed[-1].start()
        for a in range(n):
            copy(a, 0, sibling, me).wait_recv()
            for j, chip in enumerate(chips):
                copy(a, 4 + j, (*chip, 1 - c), me).wait_recv()
        for cp in first + passed:
            cp.wait_send()
        for cp in mine:
            cp.wait()

    return _pcall(
        body,
        name="all_gather_weights",
        in_specs=[_HBM] * n,
        out_specs=[_HBM] * n,
        out_shape=[jax.ShapeDtypeStruct((N_DEV,) + s.shape, s.dtype) for s in shards],
        scratch_shapes=[pltpu.SemaphoreType.DMA((7 * n,)), pltpu.SemaphoreType.DMA((7 * n,)),
                        pltpu.SemaphoreType.DMA((n,))],
    )(*shards)


def _exchange(chunks):
    n = len(chunks)

    def body(*refs):
        g_refs, o_refs = refs[:n], refs[n:2 * n]
        send_sems, recv_sems, local_sems = refs[2 * n:]
        x, y, c = lax.axis_index("x"), lax.axis_index("y"), lax.axis_index("c")
        me = 4 * x + 2 * y + c
        own = [pltpu.make_async_copy(g_refs[a].at[me], o_refs[a].at[me], local_sems.at[a]) for a in range(n)]
        for cp in own:
            cp.start()
        copies = []
        for a in range(n):
            for k in range(1, N_DEV):
                px, py, pc = x ^ (k >> 2), y ^ ((k >> 1) & 1), c ^ (k & 1)
                peer = 4 * px + 2 * py + pc
                copies.append(pltpu.make_async_remote_copy(
                    src_ref=g_refs[a].at[peer], dst_ref=o_refs[a].at[me], send_sem=send_sems.at[7 * a + k - 1],
                    recv_sem=recv_sems.at[7 * a + k - 1], device_id=(px, py, pc), device_id_type=_MESH))
        for cp in copies:
            cp.start()
        for cp in copies:
            cp.wait()
        for cp in own:
            cp.wait()

    return _pcall(
        body,
        name="exchange_grads",
        in_specs=[_HBM] * n,
        out_specs=[_HBM] * n,
        out_shape=[jax.ShapeDtypeStruct(g.shape, g.dtype) for g in chunks],
        scratch_shapes=[pltpu.SemaphoreType.DMA((7 * n,)), pltpu.SemaphoreType.DMA((7 * n,)),
                        pltpu.SemaphoreType.DMA((n,))],
    )(*chunks)


def _reduce_adamw(name, parts, w, m, v, tr):
    _, rows, cols = parts.shape
    tr = min(tr, rows)
    assert rows % tr == 0
    c1 = 1.0 / (1.0 - ADAM_B1 ** ADAM_STEP)
    c2 = 1.0 / (1.0 - ADAM_B2 ** ADAM_STEP)

    def body(p_ref, w_ref, m_ref, v_ref, g_out, d_out, m_out, v_out):
        g = p_ref[0].astype(F32)
        for d in range(1, N_DEV):
            g = g + p_ref[d].astype(F32)
        mn = ADAM_B1 * m_ref[...] + (1.0 - ADAM_B1) * g
        vn = ADAM_B2 * v_ref[...] + (1.0 - ADAM_B2) * (g * g)
        g_out[...] = g
        m_out[...] = mn
        v_out[...] = vn
        d_out[...] = -ADAM_LR * ((mn * c1) / (jnp.sqrt(vn * c2) + ADAM_EPS) + ADAM_WD * w_ref[...])

    spec = pl.BlockSpec((tr, cols), lambda i: (i, 0))
    return _pcall(
        body,
        name=name,
        grid=(rows // tr,),
        in_specs=[pl.BlockSpec((N_DEV, tr, cols), lambda i: (0, i, 0)), spec, spec, spec],
        out_specs=[spec] * 4,
        out_shape=[jax.ShapeDtypeStruct((rows, cols), F32)] * 4,
        compiler_params=_cparams(1),
    )(parts, w, m, v)


_SHARDED = ("w_in", "w_up_dil", "w_up_sb", "w_out", "w_mlp_in", "w_mlp_out")
_FULL_SHAPES = {"w_in": (D_MODEL, IN_COLS), "w_up_dil": (DIL_OUT_WIDTH, D_MODEL), "w_up_sb": (SB_WIDTH, D_MODEL),
                "w_out": (D_MODEL, D_MODEL), "w_mlp_in": (D_MODEL, D_FF), "w_mlp_out": (D_FF, D_MODEL)}
_ROW_SHARDED = ("w_out", "w_mlp_out")


def _shard_shape(name):
    r, c = _FULL_SHAPES[name]
    return (r // N_DEV, c) if name in _ROW_SHARDED else (r, c // N_DEV)


def _assemble(name, gathered):
    r, c = _shard_shape(name)
    if name in _ROW_SHARDED:
        return gathered.reshape(N_DEV * r, c)
    return gathered.transpose(1, 0, 2).reshape(r, N_DEV * c)


def _chunk(name, full):
    r, c = _shard_shape(name)
    if name in _ROW_SHARDED:
        return full.reshape(N_DEV, r, c)
    return full.reshape(r, N_DEV, c).transpose(1, 0, 2)


_SMALL = (("norm_mix_g", D_MODEL), ("b_gate", 2 * D_MODEL), ("norm_mlp_g", D_MODEL), ("norm_final_g", D_MODEL))
_SMALL_N = sum(n for _, n in _SMALL) + LANES


def _pack_small(vals, tail):
    return jnp.concatenate([vals[n].reshape(1, -1) for n, _ in _SMALL] + [tail], axis=1)


def _unpack_small(vec, shapes):
    out, pos = {}, 0
    for n, width in _SMALL:
        out[n] = vec[:, pos:pos + width].reshape(shapes[n])
        pos += width
    return out, vec[:, pos:]


def kernel(x, norm_mix_g, w_in, b_gate, w_up_dil, w_up_sb, w_out, norm_mlp_g, w_mlp_in, w_mlp_out, norm_final_g, loss_target, m_norm_mix_g, m_w_in, m_b_gate, m_w_up_dil, m_w_up_sb, m_w_out, m_norm_mlp_g, m_w_mlp_in, m_w_mlp_out, m_norm_final_g, v_norm_mix_g, v_w_in, v_b_gate, v_w_up_dil, v_w_up_sb, v_w_out, v_norm_mlp_g, v_w_mlp_in, v_w_mlp_out, v_norm_final_g):
    given = dict(locals())
    s = x.shape[1]
    x0 = x.reshape(s, D_MODEL)
    target = loss_target.reshape(s, D_MODEL)
    g1 = norm_mix_g.reshape(1, D_MODEL)
    g2 = norm_mlp_g.reshape(1, D_MODEL)
    g3 = norm_final_g.reshape(1, D_MODEL)
    bg = b_gate.reshape(1, 2 * D_MODEL)
    w_shards = {n: given[n].reshape(_shard_shape(n)) for n in _SHARDED}
    m_shards = {n: given["m_" + n].reshape(_shard_shape(n)) for n in _SHARDED}
    v_shards = {n: given["v_" + n].reshape(_shard_shape(n)) for n in _SHARDED}

    shard_b = {n: w_shards[n].astype(BF16) for n in _SHARDED}
    (gathered_w_in,) = _all_gather([shard_b["w_in"]])
    w_in_f = _assemble("w_in", gathered_w_in)
    w_qkv, w_gl = _group_major(w_in_f[:, :QKV_COLS]), w_in_f[:, QKV_COLS:]
    full = {}

    def norm1(_, rows, consts):
        _, xh = _rms_stats(rows[0])
        return [xh * consts[0]], []

    (h1,) = _rowk("norm_mix", tm=1024, rows=[x0], consts=[g1], row_outs=[(D_MODEL, BF16)], epilogue=norm1)
    qkv, (land,) = _mm("proj_qkv", h1, w_qkv, out_dtype=BF16, tm=1024, tn=768, tk=D_MODEL,
                       rider=_Spread([shard_b["w_mlp_in"]], chunked=False))
    full["w_mlp_in"] = _assemble("w_mlp_in", land)
    gl = _mm("proj_gates", h1, w_gl, out_dtype=BF16, tm=1024, tn=1024, tk=D_MODEL)
    views = [_dil_view(qkv, g) for g in range(len(DIL_GROUPS))]
    dil = [_dil_fwd(views[g], g) for g in range(len(DIL_GROUPS))]
    os_, lses = [d[0] for d in dil], [d[1] for d in dil]
    o_a = _dil_mix_fwd(os_, lses, 1024)
    riding = ("w_mlp_out", "w_out", "w_up_sb", "w_up_dil")
    (o_b, tot_b, sb_steps), lands = _sb_fwd(qkv, rider=_Spread([shard_b[n] for n in riding], chunked=False))
    full.update({n: _assemble(n, land) for n, land in zip(riding, lands)})
    x1, h2 = _mixer_fwd(o_a, o_b, gl, x0, bg, g2, full["w_up_dil"], full["w_up_sb"], full["w_out"], 512)
    f = _mm("mlp_in", h2, full["w_mlp_in"], out_dtype=BF16, tm=1024, tn=1024, tk=D_MODEL,
            epilogue=lambda r, _: jnp.square(jnp.maximum(r, 0.0)))

    def head(acc, rows, consts):
        x1v, tv = rows
        g3v = consts[0]
        x2 = x1v + acc
        r, xh = _rms_stats(x2)
        diff = xh * g3v - tv
        loss = (0.5 / D_MODEL) * jnp.sum(jnp.sum(diff * diff, axis=0, keepdims=True), axis=1, keepdims=True)
        dy = diff * (1.0 / D_MODEL)
        dx2, dg = _rms_bwd(dy, xh, r, g3v)
        return [dx2, dx2], [dg, jnp.broadcast_to(loss, (1, LANES))]

    dx2, dx2b, gg3, loss_part = _rowk(
        "mlp_out_loss", a=f, w=full["w_mlp_out"], tm=512, tk=D_FF, rows=[x1, target], consts=[g3],
        row_outs=[(D_MODEL, F32), (D_MODEL, BF16)], acc_outs=[D_MODEL, LANES], epilogue=head)

    da = _mm("mlp_out_bwd", dx2b, full["w_mlp_out"], tb=True, out_dtype=BF16, tm=1024, tn=1024, tk=D_MODEL, extra=f,
             epilogue=lambda r, fv: r * (2.0 * jnp.sqrt(fv.astype(F32))))
    g_w_mlp_out = _mm("grad_w_mlp_out", f, dx2b, ta=True, out_dtype=F32, tm=1024, tn=1024, tk=2048)
    g_w_mlp_in = _mm("grad_w_mlp_in", h2, da, ta=True, out_dtype=F32, tm=1024, tn=1024, tk=2048)

    def norm_bwd(acc, rows, consts):
        xv, dres = rows
        r, xh = _rms_stats(xv)
        dx, dg = _rms_bwd(acc, xh, r, consts[0])
        return [dres + dx], [dg]

    bchunk = lambda n, g: _chunk(n, g).astype(BF16)
    parts = {}
    (dx1, gg2), (parts["w_mlp_in"],) = _rowk(
        "mlp_in_bwd", a=da, w=full["w_mlp_in"], nt=True, tm=512, tk=D_FF, rows=[x1, dx2], consts=[g2],
        row_outs=[(D_MODEL, F32)], acc_outs=[D_MODEL], epilogue=norm_bwd,
        rider=_Spread([bchunk("w_mlp_in", g_w_mlp_in)], chunked=True))
    (do_a, do_b, dgl, g_w_out, g_w_ud, g_w_us, g_bg), (parts["w_mlp_out"],) = _mixer_bwd(
        dx1, o_a, o_b, gl, bg, full["w_up_dil"], full["w_up_sb"], full["w_out"], 512,
        rider=_Spread([bchunk("w_mlp_out", g_w_mlp_out)], chunked=True))
    mix = _dil_mix_bwd(do_a, os_, lses, 1024)
    small_three = {"w_out": g_w_out, "w_up_sb": g_w_us, "w_up_dil": g_w_ud}
    grads, lands = _dil_bwd(views[0], mix[0], dil[0][2], mix[3], 0,
                            rider=_Spread([bchunk(n, g) for n, g in small_three.items()], chunked=True))
    parts.update(dict(zip(small_three, lands)))
    dil_b = [grads] + [_dil_bwd(views[g], mix[g], dil[g][2], mix[3 + g], g) for g in (1, 2)]
    dq_b, dk_b, dv_b = _sb_bwd(qkv, do_b, tot_b, sb_steps)
    dproj = [d[0] for d in dil_b] + [d[1] for d in dil_b] + [d[2] for d in dil_b] + [dq_b, dk_b, dv_b, dgl]
    g_w_in = jnp.concatenate([
        _grad_cols("grad_w_in_dil", h1, dproj[:9], tm=D_MODEL, tk=1024),
        _grad_cols("grad_w_in_sb", h1, dproj[9:12], tm=D_MODEL, tk=1024),
        _grad_cols("grad_w_in_gates", h1, dproj[12:], tm=D_MODEL, tk=1024)], axis=1)
    (grad_x, gg1), (parts["w_in"],) = _rowk(
        "in_proj_bwd", a=dproj, w=w_in_f, nt=True, tm=512, tk=IN_COLS, rows=[x0, dx1], consts=[g1],
        row_outs=[(D_MODEL, F32)], acc_outs=[D_MODEL], epilogue=norm_bwd,
        rider=_Spread([bchunk("w_in", g_w_in)], chunked=True))

    small_part = _pack_small({"norm_mix_g": gg1, "b_gate": g_bg, "norm_mlp_g": gg2, "norm_final_g": gg3}, loss_part)
    (small_parts,) = _exchange([jnp.broadcast_to(small_part[None], (N_DEV, 1, _SMALL_N))])

    tags = ("grad_", "delta_", "new_m_", "new_v_")
    outs = {}
    for n, p in parts.items():
        res = _reduce_adamw("adamw_" + n, p, w_shards[n], m_shards[n], v_shards[n], 256)
        for tag, val in zip(tags, res):
            outs[tag + n] = val.reshape(given[n].shape)
    small_w = _pack_small(given, jnp.zeros((1, LANES), F32))
    small_m = _pack_small({n: given["m_" + n] for n, _ in _SMALL}, jnp.zeros((1, LANES), F32))
    small_v = _pack_small({n: given["v_" + n] for n, _ in _SMALL}, jnp.ones((1, LANES), F32))
    small_res = _reduce_adamw("adamw_replicated", small_parts, small_w, small_m, small_v, 8)

    small_shapes = {n: given[n].shape for n, _ in _SMALL}
    for tag, small in zip(tags, small_res):
        small_vals, tail = _unpack_small(small, small_shapes)
        for n, val in small_vals.items():
            outs[tag + n] = val
        if tag == "grad_":
            loss = tail[0, 0]
    names = ["norm_mix_g", "w_in", "b_gate", "w_up_dil", "w_up_sb", "w_out", "norm_mlp_g", "w_mlp_in", "w_mlp_out",
             "norm_final_g"]
    return (loss, grad_x.reshape(x.shape), *[outs["grad_" + n] for n in names], *[outs["delta_" + n] for n in names],
            *[outs["new_m_" + n] for n in names], *[outs["new_v_" + n] for n in names])
```

```python
import functools
import math

import jax
import jax.numpy as jnp
from jax import lax
from jax.experimental import pallas as pl
from jax.experimental.pallas import tpu as pltpu

_pcall = pl.pallas_call

F32 = jnp.float32
BF16 = jnp.bfloat16

D_MODEL = 1024
HEAD_DIM = 64
DIL_GROUPS = ((128, 1), (512, 4), (2048, 16))
DIL_HEADS_PER_GROUP = 4
N_DIL_HEADS = 12
N_SB_HEADS = 8
DIL_WIDTH = 768
DIL_OUT_WIDTH = 256
SB_WIDTH = 512
D_FF = 4096
BLOCK = 128
RMS_EPS = 1e-6
NEG_INF = -1e30
QKV_COLS = 3 * DIL_WIDTH + 3 * SB_WIDTH
IN_COLS = QKV_COLS + 2 * D_MODEL
N_DEV = 8

ADAM_LR = 0.001
ADAM_B1 = 0.9
ADAM_B2 = 0.999
ADAM_EPS = 1e-08
ADAM_WD = 0.01
ADAM_STEP = 10

VMEM_LIMIT = 56 * 1024 * 1024
SB_TK = 256
LANES = 128

_ARB = pltpu.ARBITRARY


def _cparams(n_axes, **kw):
    return pltpu.CompilerParams(dimension_semantics=(_ARB,) * n_axes, vmem_limit_bytes=VMEM_LIMIT, **kw)


def _dot(a, b):
    return jnp.dot(a, b, preferred_element_type=F32)


def _dot_nt(a, b):
    return lax.dot_general(a, b, (((1,), (1,)), ((), ())), preferred_element_type=F32)


def _dot_tn(a, b):
    return lax.dot_general(a, b, (((0,), (0,)), ((), ())), preferred_element_type=F32)


def _split_hi_lo(x):
    hi = x.astype(BF16)
    lo = (x - hi.astype(F32)).astype(BF16)
    return hi, lo


def _dot_hi_lo(x, m):
    hi, lo = _split_hi_lo(x)
    return _dot(hi, m) + _dot(lo, m)


def _sigmoid(x):
    return 1.0 / (1.0 + jnp.exp(-x))


_HBM = pl.BlockSpec(memory_space=pltpu.HBM)
_MESH = pl.DeviceIdType.MESH


class _Spread:
    def __init__(self, srcs, chunked):
        self.srcs, self.chunked, self.n = list(srcs), chunked, len(srcs)

    def land_shapes(self):
        return [jax.ShapeDtypeStruct((N_DEV,) + (s.shape[1:] if self.chunked else s.shape), s.dtype) for s in self.srcs]

    def scratch(self):
        dma = pltpu.SemaphoreType.DMA
        return [dma((7 * self.n,)), dma((7 * self.n,)), dma((self.n,))]

    def copies(self, src_refs, land_refs, send_sems, recv_sems, local_sems):
        x, y, c = lax.axis_index("x"), lax.axis_index("y"), lax.axis_index("c")
        me = 4 * x + 2 * y + c
        out = []
        for a, (src, land) in enumerate(zip(src_refs, land_refs)):
            out.append(pltpu.make_async_copy(src.at[me] if self.chunked else src, land.at[me], local_sems.at[a]))
            for k in range(1, N_DEV):
                px, py, pc = x ^ (k >> 2), y ^ ((k >> 1) & 1), c ^ (k & 1)
                out.append(pltpu.make_async_remote_copy(
                    src_ref=src.at[4 * px + 2 * py + pc] if self.chunked else src, dst_ref=land.at[me],
                    send_sem=send_sems.at[7 * a + k - 1], recv_sem=recv_sems.at[7 * a + k - 1],
                    device_id=(px, py, pc), device_id_type=_MESH))
        return out


def _call(body, args, rider=None, **kw):
    if rider is None:
        return _pcall(body, **kw)(*args)
    grid = kw["grid"]
    single = not isinstance(kw["out_shape"], (list, tuple))
    out_specs = [kw["out_specs"]] if single else list(kw["out_specs"])
    out_shape = [kw["out_shape"]] if single else list(kw["out_shape"])
    in_specs, scratch = list(kw["in_specs"]), list(kw.get("scratch_shapes", []))
    n_in, n_out, n_s, n = len(in_specs), len(out_shape), len(scratch), rider.n

    def hosted(*refs):
        ins, srcs = refs[:n_in], refs[n_in:n_in + n]
        outs, lands = refs[n_in + n:n_in + n + n_out], refs[n_in + n + n_out:n_in + 2 * n + n_out]
        own_scratch, sems = refs[n_in + 2 * n + n_out:n_in + 2 * n + n_out + n_s], refs[n_in + 2 * n + n_out + n_s:]
        ids = [pl.program_id(d) for d in range(len(grid))]
        first = functools.reduce(jnp.logical_and, [i == 0 for i in ids])
        last = functools.reduce(jnp.logical_and, [i == g - 1 for i, g in zip(ids, grid)])
        copies = rider.copies(srcs, lands, *sems)

        @pl.when(first)
        def _():
            for cp in copies:
                cp.start()

        body(*ins, *outs, *own_scratch)

        @pl.when(last)
        def _():
            for cp in copies:
                cp.wait()

    kw = dict(kw, in_specs=in_specs + [_HBM] * n, out_specs=out_specs + [_HBM] * n,
              out_shape=out_shape + rider.land_shapes(), scratch_shapes=scratch + rider.scratch())
    res = _pcall(hosted, **kw)(*args, *rider.srcs)
    return (res[0] if single else list(res[:n_out])), list(res[n_out:])


def _mm(name, a, b, *, ta=False, tb=False, out_dtype, tm, tn, tk, epilogue=None, extra=None, rider=None):
    m = a.shape[1] if ta else a.shape[0]
    k = a.shape[0] if ta else a.shape[1]
    n = b.shape[0] if tb else b.shape[1]
    assert (b.shape[1] if tb else b.shape[0]) == k
    tm, tn, tk = min(tm, m), min(tn, n), min(tk, k)
    assert m % tm == 0 and n % tn == 0 and k % tk == 0, (name, m, n, k, tm, tn, tk)
    nk = k // tk
    dn = (((0 if ta else 1,), (1 if tb else 0,)), ((), ()))
    in_place = nk > 1 and epilogue is None and out_dtype == F32

    def body(*refs):
        if extra is not None:
            a_ref, b_ref, e_ref, o_ref = refs[:4]
        else:
            a_ref, b_ref, o_ref = refs[:3]
            e_ref = None

        def finish(r):
            if epilogue is not None:
                r = epilogue(r, None if e_ref is None else e_ref[...])
            o_ref[...] = r.astype(out_dtype)

        part = lax.dot_general(a_ref[...].astype(BF16), b_ref[...].astype(BF16), dn, preferred_element_type=F32)
        if nk == 1:
            finish(part)
        else:
            acc_ref = o_ref if in_place else refs[-1]
            kk = pl.program_id(2)

            @pl.when(kk == 0)
            def _():
                acc_ref[...] = part

            @pl.when(kk > 0)
            def _():
                acc_ref[...] += part

            if not in_place:

                @pl.when(kk == nk - 1)
                def _():
                    finish(acc_ref[...])

    a_spec = pl.BlockSpec((tk, tm), lambda j, i, kk: (kk, i)) if ta else pl.BlockSpec((tm, tk), lambda j, i, kk: (i, kk))
    b_spec = pl.BlockSpec((tn, tk), lambda j, i, kk: (j, kk)) if tb else pl.BlockSpec((tk, tn), lambda j, i, kk: (kk, j))
    o_spec = pl.BlockSpec((tm, tn), lambda j, i, kk: (i, j))
    in_specs = [a_spec, b_spec]
    args = [a, b]
    if extra is not None:
        in_specs.append(o_spec)
        args.append(extra)
    return _call(
        body, args, rider,
        name=name,
        grid=(n // tn, m // tm, nk),
        in_specs=in_specs,
        out_specs=o_spec,
        out_shape=jax.ShapeDtypeStruct((m, n), out_dtype),
        scratch_shapes=[pltpu.VMEM((tm, tn), F32)] if (nk > 1 and not in_place) else [],
        compiler_params=_cparams(3),
    )


def _grad_cols(name, a, parts, *, tm, tk, rider=None):
    k, m = a.shape
    n = sum(p.shape[1] for p in parts)
    assert m % tm == 0 and k % tk == 0
    nk = k // tk

    def body(*refs):
        a_ref, p_refs, o_ref = refs[0], refs[1:1 + len(parts)], refs[1 + len(parts)]
        kk = pl.program_id(1)
        side_by_side = jnp.concatenate([p_ref[...].astype(BF16) for p_ref in p_refs], axis=1)
        term = _dot_tn(a_ref[...].astype(BF16), side_by_side)

        @pl.when(kk == 0)
        def _():
            o_ref[...] = term

        @pl.when(kk > 0)
        def _():
            o_ref[...] += term

    return _call(
        body, [a] + list(parts), rider,
        name=name,
        grid=(m // tm, nk),
        in_specs=[pl.BlockSpec((tk, tm), lambda i, kk: (kk, i))]
        + [pl.BlockSpec((tk, p.shape[1]), lambda i, kk: (kk, 0)) for p in parts],
        out_specs=pl.BlockSpec((tm, n), lambda i, kk: (i, 0)),
        out_shape=jax.ShapeDtypeStruct((m, n), F32),
        compiler_params=_cparams(2),
    )


def _rowk(name, *, a=None, w=None, nt=False, tm, tk=None, rows=(), consts=(), row_outs=(), acc_outs=(), epilogue,
          rider=None):
    has_mm = a is not None
    a_parts = list(a) if isinstance(a, (list, tuple)) else ([a] if has_mm else [])
    n_a = len(a_parts)
    m = a_parts[0].shape[0] if has_mm else rows[0].shape[0]
    assert m % tm == 0
    nm = m // tm
    if has_mm:
        k = sum(p.shape[1] for p in a_parts)
        n = w.shape[0] if nt else w.shape[1]
        tk = min(tk, k)
        assert k % tk == 0 and (n_a == 1 or tk == k)
        nk = k // tk
    else:
        nk = 1
    n_rows, n_consts, n_ro, n_ao = len(rows), len(consts), len(row_outs), len(acc_outs)

    def body(*refs):
        pos = 0
        if has_mm:
            a_refs, w_ref = refs[:n_a], refs[n_a]
            pos = n_a + 1
        row_refs = refs[pos:pos + n_rows]
        pos += n_rows
        const_refs = refs[pos:pos + n_consts]
        pos += n_consts
        ro_refs = refs[pos:pos + n_ro]
        pos += n_ro
        ao_refs = refs[pos:pos + n_ao]
        pos += n_ao
        i = pl.program_id(0)
        kk = pl.program_id(1)

        def finish(acc):
            ro_vals, ao_vals = epilogue(acc, [r[...] for r in row_refs], [c[...] for c in const_refs])
            for r, v in zip(ro_refs, ro_vals):
                r[...] = v.astype(r.dtype)
            for r, v in zip(ao_refs, ao_vals):

                @pl.when(i == 0)
                def _(r=r, v=v):
                    r[...] = v

                @pl.when(i > 0)
                def _(r=r, v=v):
                    r[...] += v

        if not has_mm:
            finish(None)
            return
        part, off = None, 0
        for a_ref in a_refs:
            width = a_ref.shape[1]
            cols = slice(None) if n_a == 1 else slice(off, off + width)
            av = a_ref[...].astype(BF16)
            term = _dot_nt(av, w_ref[:, cols]) if nt else _dot(av, w_ref[cols, :])
            part = term if part is None else part + term
            off += width
        if nk == 1:
            finish(part)
        else:
            acc_ref = refs[pos]

            @pl.when(kk == 0)
            def _():
                acc_ref[...] = part

            @pl.when(kk > 0)
            def _():
                acc_ref[...] += part

            @pl.when(kk == nk - 1)
            def _():
                finish(acc_ref[...])

    once = pl.Buffered(1)
    in_specs, args = [], []
    if has_mm:
        for part in a_parts:
            in_specs.append(pl.BlockSpec((tm, tk if n_a == 1 else part.shape[1]), lambda i, kk: (i, kk)))
        w_mode = once if nk == 1 else None
        in_specs.append(pl.BlockSpec((n, tk), lambda i, kk: (0, kk), pipeline_mode=w_mode) if nt
                        else pl.BlockSpec((tk, n), lambda i, kk: (kk, 0), pipeline_mode=w_mode))
        args += a_parts + [w]
    for r in rows:
        in_specs.append(pl.BlockSpec((tm, r.shape[1]), lambda i, kk: (i, 0)))
        args.append(r)
    for c in consts:
        in_specs.append(pl.BlockSpec(c.shape, lambda i, kk: (0,) * c.ndim, pipeline_mode=once))
        args.append(c)
    out_specs, out_shape = [], []
    for width, dt in row_outs:
        out_specs.append(pl.BlockSpec((tm, width), lambda i, kk: (i, 0)))
        out_shape.append(jax.ShapeDtypeStruct((m, width), dt))
    for width in acc_outs:
        out_specs.append(pl.BlockSpec((1, width), lambda i, kk: (0, 0)))
        out_shape.append(jax.ShapeDtypeStruct((1, width), F32))
    return _call(
        body, args, rider,
        name=name,
        grid=(nm, nk),
        in_specs=in_specs,
        out_specs=out_specs,
        out_shape=out_shape,
        scratch_shapes=[pltpu.VMEM((tm, n), F32)] if (has_mm and nk > 1) else [],
        compiler_params=_cparams(2),
    )


def _rms_stats(x):
    r = lax.rsqrt(jnp.mean(x * x, axis=-1, keepdims=True) + RMS_EPS)
    return r, x * r


def _rms_bwd(dh, xh, r, g):
    gy = dh * g
    dx = r * (gy - xh * jnp.mean(gy * xh, axis=-1, keepdims=True))
    return dx, jnp.sum(dh * xh, axis=0, keepdims=True)


def _alibi_slope(head):
    return 2.0 ** (-8.0 * (head + 1) / N_DIL_HEADS)


DIL_STEP_BLOCKS = 4


def _dil_band(first_block):
    qi = lax.broadcasted_iota(jnp.int32, (BLOCK, 2 * BLOCK), 0)
    kj = lax.broadcasted_iota(jnp.int32, (BLOCK, 2 * BLOCK), 1)
    steps = qi + BLOCK - kj
    valid = (steps >= 0) & (steps <= BLOCK)
    if first_block is not False:
        valid = valid & ((kj >= BLOCK) | jnp.logical_not(first_block))
    return steps.astype(F32), valid


def _dil_step_specs(ncb, cols, nblk, clamp):
    def own(col):
        return pl.BlockSpec((nblk * BLOCK, DIL_OUT_WIDTH), lambda r, i: (clamp(i), r * ncb + col))

    def before(col):
        return pl.BlockSpec((BLOCK, DIL_OUT_WIDTH), lambda r, i: (jnp.maximum(clamp(i) * nblk - 1, 0), r * ncb + col))

    return [own(cols[0]), own(cols[1]), before(cols[1]), own(cols[2]), before(cols[2])]


DIL_RELAYOUT_ROWS = 1024


def _view_scratch(width):
    return pltpu.VMEM((width // LANES, DIL_RELAYOUT_ROWS, LANES), F32)


def _rows_from_view(src, scr, d, w):
    sub = src.shape[0]
    for j in range(w // LANES):
        for r in range(d):
            scr[j, pl.ds(r, sub, stride=d), :] = src[:, r * w + j * LANES:r * w + (j + 1) * LANES].astype(F32)


def _rows_to_view(scr, dst, d, w):
    sub = dst.shape[0]
    for j in range(w // LANES):
        for r in range(d):
            dst[:, r * w + j * LANES:r * w + (j + 1) * LANES] = scr[j, pl.ds(r, sub, stride=d), :].astype(dst.dtype)


def _dil_relayout(name, xs, dilation, to_view, col_block=0, width=None):
    d = dilation
    tm = DIL_RELAYOUT_ROWS
    rows = tm // d
    if to_view:
        s = xs[0].shape[0]
        widths = [width or x.shape[1] for x in xs]
    else:
        s = xs[0].shape[0] * d
        widths = [v.shape[1] // d for v in xs]
    assert s % tm == 0 and all(w % LANES == 0 for w in widths)
    n = len(xs)

    def body(*refs):
        in_refs, out_refs, scratch = refs[:n], refs[n:2 * n], refs[2 * n:]
        for src, dst, scr, w in zip(in_refs, out_refs, scratch, widths):
            slabs = [slice(j * LANES, (j + 1) * LANES) for j in range(w // LANES)]
            if to_view:
                for j, slab in enumerate(slabs):
                    scr[j] = src[:, slab].astype(F32)
                _rows_to_view(scr, dst, d, w)
            else:
                _rows_from_view(src, scr, d, w)
                for j, slab in enumerate(slabs):
                    dst[:, slab] = scr[j].astype(dst.dtype)

    natural = [pl.BlockSpec((tm, w), lambda i: (i, col_block)) for w in widths]
    viewed = [pl.BlockSpec((rows, d * w), lambda i: (i, 0)) for w in widths]
    return _pcall(
        body,
        name=name,
        grid=(s // tm,),
        in_specs=natural if to_view else viewed,
        out_specs=viewed if to_view else natural,
        out_shape=[jax.ShapeDtypeStruct((s // d, d * w) if to_view else (s, w), x.dtype) for x, w in zip(xs, widths)],
        scratch_shapes=[_view_scratch(w) for w in widths],
        compiler_params=_cparams(1),
    )(*xs)


def _dil_fwd(view, group):
    window, dilation = DIL_GROUPS[group]
    qkv_v, ncb, cols = view
    sub = qkv_v.shape[0]
    s = sub * dilation
    nb = sub // BLOCK
    assert nb * BLOCK * dilation == s and window // dilation == BLOCK
    nblk = min(DIL_STEP_BLOCKS, nb)
    assert nb % nblk == 0
    slopes = [_alibi_slope(group * DIL_HEADS_PER_GROUP + h) * dilation for h in range(DIL_HEADS_PER_GROUP)]

    def body(q_ref, kc_ref, kp_ref, vc_ref, vp_ref, o_ref, lse_ref):
        i = pl.program_id(1)
        kk_all = jnp.concatenate([kp_ref[...], kc_ref[...]], axis=0)
        vv_all = jnp.concatenate([vp_ref[...], vc_ref[...]], axis=0)
        head_id = lax.broadcasted_iota(jnp.int32, (1, DIL_OUT_WIDTH), 1) // HEAD_DIM
        chains = [(b, h) for b in range(nblk) for h in range(DIL_HEADS_PER_GROUP)]
        rows = lambda b: slice(b * BLOCK, (b + 1) * BLOCK)
        keys = lambda b: slice(b * BLOCK, (b + 2) * BLOCK)
        bands = [_dil_band(i == 0 if b == 0 else False) for b in range(nblk)]
        qs = [q_ref[rows(b), :] for b in range(nblk)]
        scores = [_dot_nt(jnp.where(head_id == h, qs[b], jnp.zeros_like(qs[b])), kk_all[keys(b)]) for b, h in chains]
        ps, lses = [], []
        for (b, h), sc in zip(chains, scores):
            steps, valid = bands[b]
            logits = jnp.where(valid, sc * (1.0 / math.sqrt(HEAD_DIM)) - slopes[h] * steps, NEG_INF)
            mx = jnp.max(logits, axis=1, keepdims=True)
            e = jnp.exp(logits - mx)
            den = jnp.sum(e, axis=1, keepdims=True)
            lses.append(mx + jnp.log(den))
            ps.append((e * (1.0 / den)).astype(BF16))
        outs = [_dot(p, vv_all[keys(b)]) for (b, h), p in zip(chains, ps)]
        for b in range(nblk):
            mine = [n for n, ch in enumerate(chains) if ch[0] == b]
            o, lse_all = outs[mine[0]], lses[mine[0]]
            for n in mine[1:]:
                o = jnp.where(head_id == chains[n][1], outs[n], o)
                lse_all = jnp.where(head_id == chains[n][1], lses[n], lse_all)
            o_ref[rows(b), :] = o
            lse_ref[rows(b), :] = jnp.broadcast_to(lse_all, o.shape)

    out_spec = pl.BlockSpec((nblk * BLOCK, DIL_OUT_WIDTH), lambda r, i: (i, r))
    o, lse = _pcall(
        body,
        name=f"dil_fwd_g{group}",
        grid=(dilation, nb // nblk),
        in_specs=_dil_step_specs(ncb, cols, nblk, lambda i: i),
        out_specs=[out_spec, out_spec],
        out_shape=[jax.ShapeDtypeStruct((sub, dilation * DIL_OUT_WIDTH), F32)] * 2,
        compiler_params=_cparams(2),
    )(qkv_v, qkv_v, qkv_v, qkv_v, qkv_v)
    return o, lse


def _dil_bwd(view, do_g, lse_g, dterm_g, group, rider=None):
    window, dilation = DIL_GROUPS[group]
    qkv_v, ncb, cols = view
    sub = qkv_v.shape[0]
    nb = sub // BLOCK
    nblk = min(DIL_STEP_BLOCKS, nb)
    n_steps = nb // nblk
    slopes = [_alibi_slope(group * DIL_HEADS_PER_GROUP + h) * dilation for h in range(DIL_HEADS_PER_GROUP)]
    scale = 1.0 / math.sqrt(HEAD_DIM)
    tail = slice((nblk - 1) * BLOCK, nblk * BLOCK)

    def body(q_ref, kc_ref, kp_ref, vc_ref, vp_ref, do_ref, lse_ref, dt_ref, dq_ref, dk_ref, dv_ref, ck_ref, cv_ref):
        i = pl.program_id(1)

        @pl.when(i == 0)
        def _():
            ck_ref[...] = jnp.zeros_like(ck_ref)
            cv_ref[...] = jnp.zeros_like(cv_ref)

        @pl.when(i < n_steps)
        def _():
            kk_all = jnp.concatenate([kp_ref[...], kc_ref[...]], axis=0)
            vv_all = jnp.concatenate([vp_ref[...], vc_ref[...]], axis=0)
            lane = lax.broadcasted_iota(jnp.int32, (1, DIL_OUT_WIDTH), 1)
            head_id = lane // HEAD_DIM
            chains = [(b, h) for b in range(nblk) for h in range(DIL_HEADS_PER_GROUP)]
            rows = lambda b: slice(b * BLOCK, (b + 1) * BLOCK)
            keys = lambda b: slice(b * BLOCK, (b + 2) * BLOCK)
            bands = [_dil_band(i == 0 if b == 0 else False) for b in range(nblk)]
            qms, doms = [], []
            for b, h in chains:
                q, do = q_ref[rows(b), :], do_ref[rows(b), :]
                qms.append(jnp.where(head_id == h, q, jnp.zeros_like(q)))
                doms.append(jnp.where(head_id == h, do, jnp.zeros_like(do)))
            scores = [_dot_nt(qm, kk_all[keys(b)]) for (b, h), qm in zip(chains, qms)]
            dps = [_dot_nt(dom, vv_all[keys(b)]) for (b, h), dom in zip(chains, doms)]
            pbs, dss = [], []
            for n, (b, h) in enumerate(chains):
                steps, valid = bands[b]
                first = lane == h * HEAD_DIM
                lse = jnp.sum(jnp.where(first, lse_ref[rows(b), :], 0.0), axis=1, keepdims=True)
                dt = jnp.sum(jnp.where(first, dt_ref[rows(b), :], 0.0), axis=1, keepdims=True)
                logits = jnp.where(valid, scores[n] * scale - slopes[h] * steps, NEG_INF)
                p = jnp.where(valid, jnp.exp(logits - lse), 0.0)
                pbs.append(p.astype(BF16))
                dss.append((p * (dps[n] + dt) * scale).astype(BF16))
            dqs = [_dot(ds, kk_all[keys(b)]) for (b, h), ds in zip(chains, dss)]
            dks = [_dot_tn(ds, qm) for ds, qm in zip(dss, qms)]
            dvs = [_dot_tn(pb, dom) for pb, dom in zip(pbs, doms)]
            dkk, dvv = [], []
            for b in range(nblk):
                mine = [n for n, ch in enumerate(chains) if ch[0] == b]
                dq = dqs[mine[0]]
                for n in mine[1:]:
                    dq = jnp.where(head_id == chains[n][1], dqs[n], dq)
                dq_ref[rows(b), :] = dq.astype(dq_ref.dtype)
                dkk.append((dks[mine[0]] + dks[mine[1]]) + (dks[mine[2]] + dks[mine[3]]))
                dvv.append((dvs[mine[0]] + dvs[mine[1]]) + (dvs[mine[2]] + dvs[mine[3]]))
            for out_ref, carry_ref, parts in ((dk_ref, ck_ref, dkk), (dv_ref, cv_ref, dvv)):
                if nblk > 1:
                    out_ref[: (nblk - 1) * BLOCK, :] = carry_ref[: (nblk - 1) * BLOCK, :].astype(out_ref.dtype)
                out_ref[tail, :] = (carry_ref[tail, :] + parts[0][:BLOCK]).astype(out_ref.dtype)
                for b in range(nblk):
                    own = parts[b][BLOCK:]
                    carry_ref[rows(b), :] = own + parts[b + 1][:BLOCK] if b + 1 < nblk else own

        @pl.when(i == n_steps)
        def _():
            dk_ref[...] = ck_ref[...].astype(dk_ref.dtype)
            dv_ref[...] = cv_ref[...].astype(dv_ref.dtype)

    clamp = lambda i: jnp.minimum(i, n_steps - 1)
    row_spec = pl.BlockSpec((nblk * BLOCK, DIL_OUT_WIDTH), lambda r, i: (clamp(i), r))
    late_spec = pl.BlockSpec((nblk * BLOCK, DIL_OUT_WIDTH), lambda r, i: (jnp.maximum(i - 1, 0), r))
    res = _call(
        body, (qkv_v, qkv_v, qkv_v, qkv_v, qkv_v, do_g, lse_g, dterm_g), rider,
        name=f"dil_bwd_g{group}",
        grid=(dilation, n_steps + 1),
        in_specs=_dil_step_specs(ncb, cols, nblk, clamp) + [row_spec, row_spec, row_spec],
        out_specs=[row_spec, late_spec, late_spec],
        out_shape=[jax.ShapeDtypeStruct((sub, dilation * DIL_OUT_WIDTH), BF16)] * 3,
        scratch_shapes=[pltpu.VMEM((nblk * BLOCK, DIL_OUT_WIDTH), F32)] * 2,
        compiler_params=_cparams(2),
    )
    grads, lands = res if rider is not None else (res, None)
    if dilation > 1:
        grads = _dil_relayout(f"dil_bwd_rows_g{group}", list(grads), dilation, to_view=False)
    return tuple(grads) if rider is None else (tuple(grads), lands)


def _dil_view(qkv, group):
    _, dilation = DIL_GROUPS[group]
    w = DIL_OUT_WIDTH
    if dilation == 1:
        return qkv, QKV_COLS // w, (3 * group, 3 * group + 1, 3 * group + 2)
    (own,) = _dil_relayout(f"dil_view_g{group}", [qkv], dilation, to_view=True, col_block=group, width=3 * w)
    return own, 3, (0, 1, 2)


def _group_major(w_qkv):
    w = DIL_OUT_WIDTH
    ng = len(DIL_GROUPS)
    cols = [w_qkv[:, (part * ng + g) * w:(part * ng + g + 1) * w] for g in range(ng) for part in range(3)]
    return jnp.concatenate(cols + [w_qkv[:, 3 * DIL_WIDTH:]], axis=1)


def _head_block_ones():
    r = lax.broadcasted_iota(jnp.int32, (DIL_OUT_WIDTH, DIL_OUT_WIDTH), 0) // HEAD_DIM
    c = lax.broadcasted_iota(jnp.int32, (DIL_OUT_WIDTH, DIL_OUT_WIDTH), 1) // HEAD_DIM
    return jnp.where(r == c, 1.0, 0.0).astype(BF16)


def _dil_mix_weights(l0, l1, l2):
    mx = jnp.maximum(jnp.maximum(l0, l1), l2)
    e0, e1, e2 = jnp.exp(l0 - mx), jnp.exp(l1 - mx), jnp.exp(l2 - mx)
    inv = 1.0 / (e0 + e1 + e2)
    return e0 * inv, e1 * inv, e2 * inv


def _dil_view_spec(dilation):
    return pl.BlockSpec((DIL_RELAYOUT_ROWS // dilation, dilation * DIL_OUT_WIDTH), lambda i: (i, 0))


def _dil_mix_call(name, body, s, ins, in_specs, outs, n_relaid):
    out_specs = [_dil_view_spec(d or 1) for d, _ in outs]
    out_shape = [jax.ShapeDtypeStruct((s // (d or 1), (d or 1) * DIL_OUT_WIDTH), dt) for d, dt in outs]
    return _pcall(
        body,
        name=name,
        grid=(s // DIL_RELAYOUT_ROWS,),
        in_specs=in_specs,
        out_specs=out_specs,
        out_shape=out_shape,
        scratch_shapes=[_view_scratch(DIL_OUT_WIDTH)] * n_relaid,
        compiler_params=_cparams(1),
    )(*ins)


DIL_MIX_CHUNK = 64
_DIL_SLABS = DIL_OUT_WIDTH // LANES


def _dil_rows(refs, scratch):
    dils = [d for _, d in DIL_GROUPS]
    assert dils[0] == 1
    readers = [lambda rows, j, ref=refs[0]: ref[rows, j * LANES:(j + 1) * LANES]]
    for ref, scr, d in zip(refs[1:], scratch, dils[1:]):
        _rows_from_view(ref, scr, d, DIL_OUT_WIDTH)
        readers.append(lambda rows, j, scr=scr: scr[j, rows, :])
    return readers


def _dil_mix_chunks(step):
    def chunk(c, carry):
        step(pl.ds(pl.multiple_of(c * DIL_MIX_CHUNK, DIL_MIX_CHUNK), DIL_MIX_CHUNK))
        return carry

    lax.fori_loop(0, DIL_RELAYOUT_ROWS // DIL_MIX_CHUNK, chunk, 0, unroll=4)


def _dil_mix_fwd(os_, lses):
    ng = len(DIL_GROUPS)
    s = os_[0].shape[0]

    def body(*refs):
        o_refs, l_refs, out_ref, scratch = refs[:ng], refs[ng:2 * ng], refs[2 * ng], refs[2 * ng + 1:]
        o_at = _dil_rows(o_refs, scratch[:ng - 1])
        l_at = _dil_rows(l_refs, scratch[ng - 1:])

        def step(rows):
            for j in range(_DIL_SLABS):
                w0, w1, w2 = _dil_mix_weights(*[at(rows, j) for at in l_at])
                o0, o1, o2 = [at(rows, j) for at in o_at]
                out_ref[rows, j * LANES:(j + 1) * LANES] = (w0 * o0 + w1 * o1 + w2 * o2).astype(out_ref.dtype)

        _dil_mix_chunks(step)

    specs = [_dil_view_spec(d) for _, d in DIL_GROUPS]
    (o_a,) = _dil_mix_call("dil_mix_fwd", body, s, list(os_) + list(lses), specs * 2, [(None, BF16)], 2 * (ng - 1))
    return o_a


def _dil_mix_bwd(do_a, os_, lses):
    ng = len(DIL_GROUPS)
    s = do_a.shape[0]
    dils = [d for _, d in DIL_GROUPS]

    def body(*refs):
        do_ref, o_refs, l_refs = refs[0], refs[1:1 + ng], refs[1 + ng:1 + 2 * ng]
        out_refs, scratch = refs[1 + 2 * ng:1 + 4 * ng], refs[1 + 4 * ng:]
        o_at = _dil_rows(o_refs, scratch[:ng - 1])
        l_at = _dil_rows(l_refs, scratch[ng - 1:2 * (ng - 1)])
        spare = iter(scratch[2 * (ng - 1):])
        staged = [None if dils[n % ng] == 1 else next(spare) for n in range(2 * ng)]
        ones = _head_block_ones()

        def step(rows):
            do = do_ref[rows, :].astype(F32)
            ws, prods = [], []
            for j in range(_DIL_SLABS):
                w0, w1, w2 = _dil_mix_weights(*[at(rows, j) for at in l_at])
                o0, o1, o2 = [at(rows, j) for at in o_at]
                ws.append((w0, w1, w2))
                prods.append(do[:, j * LANES:(j + 1) * LANES] * (w0 * o0 + w1 * o1 + w2 * o2))
            tot = _dot_hi_lo(jnp.concatenate(prods, axis=1), ones)
            for j in range(_DIL_SLABS):
                slab = slice(j * LANES, (j + 1) * LANES)
                vals = [w * do[:, slab] for w in ws[j]] + [-w * tot[:, slab] for w in ws[j]]
                for val, dst, scr in zip(vals, out_refs, staged):
                    if scr is None:
                        dst[rows, slab] = val.astype(dst.dtype)
                    else:
                        scr[j, rows, :] = val

        _dil_mix_chunks(step)
        for n, (dst, scr) in enumerate(zip(out_refs, staged)):
            if scr is not None:
                _rows_to_view(scr, dst, dils[n % ng], DIL_OUT_WIDTH)

    specs = [_dil_view_spec(d) for d in dils]
    return _dil_mix_call(
        "dil_mix_bwd", body, s, [do_a] + list(os_) + list(lses), [_dil_view_spec(1)] + specs * 2,
        [(d, BF16) for d in dils] + [(d, F32) for d in dils], 4 * (ng - 1))


_SB_Q0 = 3 * DIL_WIDTH // LANES
_SB_K0 = _SB_Q0 + SB_WIDTH // LANES
_SB_V0 = _SB_K0 + SB_WIDTH // LANES


_EXP_CLAMP = 88.0
_SB_DEAD = 104.0


def _tri(t, op):
    r = lax.broadcasted_iota(jnp.int32, (t, t), 0)
    c = lax.broadcasted_iota(jnp.int32, (t, t), 1)
    return jnp.where(op(r, c), 1.0, 0.0).astype(BF16)


def _softplus(z):
    return jnp.maximum(z, jnp.log(1.0 + jnp.exp(jnp.minimum(z, _EXP_CLAMP))))


def _sb_chain_head(qm, kj, mask):
    z = _dot_nt(qm, kj)
    sp = _softplus(z)
    return (sp if mask is None else jnp.where(mask, sp, 0.0)), z - sp


def _sb_fwd(qkv, rider=None):
    s = qkv.shape[0]
    t = SB_TK
    assert s % (2 * t) == 0
    nq = s // (2 * t)
    n_pairs = SB_WIDTH // LANES

    def body(q_ref, k_ref, v_ref, o_ref, tot_ref, steps_ref):
        p, i = pl.program_id(0), pl.program_id(1)
        lane_hi = lax.broadcasted_iota(jnp.int32, (1, LANES), 1) // HEAD_DIM
        later = _tri(t, lambda r, c: r > c)
        causal = lax.broadcasted_iota(jnp.int32, (t, t), 1) < lax.broadcasted_iota(jnp.int32, (t, t), 0)
        qms = []
        for x in range(2):
            q = q_ref[pl.ds(x * t, t), :] * (1.0 / math.sqrt(HEAD_DIM))
            qms.append([jnp.where(lane_hi == hh, q, jnp.zeros_like(q)) for hh in range(2)])

        def tile(j):
            off = pl.multiple_of(j * t, t)
            return k_ref[pl.ds(off, t), :], v_ref[pl.ds(off, t), :]

        def step(groups, carry):
            kv = [tile(j) for _, j, _ in groups]
            chains = [(g, x, hh) for g, (x, _, _) in enumerate(groups) for hh in range(2)]
            heads = [_sb_chain_head(qms[x][hh], kv[g][0], causal if groups[g][2] else None) for g, x, hh in chains]
            sufs = [_dot(sp.astype(BF16), later) for sp, _ in heads]
            cur = [list(carry[0]), list(carry[1])]
            for (g, x, hh), (sp, lpos), suf in zip(chains, heads, sufs):
                c, acc = cur[x][hh]
                a = jnp.exp(lpos - suf - c)
                if groups[g][2]:
                    a = jnp.where(causal, a, 0.0)
                cur[x][hh] = (c + jnp.sum(sp, axis=1, keepdims=True), acc + _dot(a.astype(BF16), kv[g][1]))
            return (tuple(cur[0]), tuple(cur[1]))

        def lowest(carry):
            return jnp.min(jnp.minimum(jnp.minimum(carry[0][0][0], carry[0][1][0]),
                                       jnp.minimum(carry[1][0][0], carry[1][1][0])))

        zero = (jnp.zeros((t, 1), F32), jnp.zeros((t, LANES), F32))
        start = ((zero, zero), (zero, zero))
        carry = lax.cond(
            i == 0,
            lambda ca: step([(0, 0, True), (1, 1, True), (1, 0, False)], ca),
            lambda ca: step([(0, 2 * i, True), (1, 2 * i + 1, True), (0, 2 * i - 1, False), (1, 2 * i, False)], ca),
            start)

        def walk(state):
            n, ca, _ = state
            ca = step([(0, 2 * i - 2 - n, False), (1, 2 * i - 1 - n, False)], ca)
            return n + 1, ca, lowest(ca)

        n_more, carry, low = lax.while_loop(
            lambda st: jnp.logical_and(st[0] + 1 < 2 * i, st[2] <= _SB_DEAD), walk, (jnp.int32(0), carry, lowest(carry)))
        b_last = jnp.logical_and(jnp.logical_and(i > 0, n_more + 1 == 2 * i), low <= _SB_DEAD)
        carry = lax.cond(b_last, lambda ca: step([(1, 0, False)], ca), lambda ca: ca, carry)
        for x in range(2):
            (c0, acc0), (c1, acc1) = carry[x]
            o_ref[pl.ds(x * t, t), :] = jnp.where(lane_hi == 0, acc0, acc1).astype(o_ref.dtype)
            tot_ref[pl.ds(x * t, t), :] = jnp.where(lane_hi == 0, c0, c1)
        steps_ref[p, i] = 1 + n_more + b_last.astype(jnp.int32)

    return _call(
        body, (qkv, qkv, qkv), rider,
        name="sb_fwd",
        grid=(n_pairs, nq),
        in_specs=[
            pl.BlockSpec((2 * t, LANES), lambda p, i: (i, _SB_Q0 + p)),
            pl.BlockSpec((s, LANES), lambda p, i: (0, _SB_K0 + p)),
            pl.BlockSpec((s, LANES), lambda p, i: (0, _SB_V0 + p)),
        ],
        out_specs=[pl.BlockSpec((2 * t, LANES), lambda p, i: (i, p))] * 2 + [pl.BlockSpec(memory_space=pltpu.SMEM)],
        out_shape=[jax.ShapeDtypeStruct((s, SB_WIDTH), BF16), jax.ShapeDtypeStruct((s, SB_WIDTH), F32),
                   jax.ShapeDtypeStruct((n_pairs, nq), jnp.int32)],
        compiler_params=_cparams(2),
    )


def _sb_bwd(qkv, do_b, tot_b, n_steps):
    s = qkv.shape[0]
    t = SB_TK
    nq = s // (2 * t)
    n_pairs = SB_WIDTH // LANES
    scale = 1.0 / math.sqrt(HEAD_DIM)

    def body(steps_ref, q_ref, k_ref, v_ref, do_ref, tot_ref, dq_ref, dk_ref, dv_ref):
        p, i = pl.program_id(0), pl.program_id(1)

        @pl.when(i == 0)
        def _():
            dk_ref[...] = jnp.zeros_like(dk_ref)
            dv_ref[...] = jnp.zeros_like(dv_ref)

        lane = lax.broadcasted_iota(jnp.int32, (1, LANES), 1)
        lane_hi = lane // HEAD_DIM
        later = _tri(t, lambda r, c: r > c)
        before = _tri(t, lambda r, c: r < c)
        causal = lax.broadcasted_iota(jnp.int32, (t, t), 1) < lax.broadcasted_iota(jnp.int32, (t, t), 0)
        qms, doms, tots = [], [], []
        for x in range(2):
            rows = pl.ds(x * t, t)
            q, do, tot_all = q_ref[rows, :] * scale, do_ref[rows, :], tot_ref[rows, :]
            qms.append([jnp.where(lane_hi == hh, q, jnp.zeros_like(q)) for hh in range(2)])
            doms.append([jnp.where(lane_hi == hh, do, jnp.zeros_like(do)) for hh in range(2)])
            tots.append([jnp.sum(jnp.where(lane == hh * HEAD_DIM, tot_all, 0.0), axis=1, keepdims=True)
                         for hh in range(2)])

        def step(groups, carry):
            offs = [pl.multiple_of(j * t, t) for _, j, _ in groups]
            ks = [k_ref[pl.ds(off, t), :] for off in offs]
            vs = [v_ref[pl.ds(off, t), :] for off in offs]
            chains = [(g, x, hh) for g, (x, _, _) in enumerate(groups) for hh in range(2)]
            heads = [_sb_chain_head(qms[x][hh], ks[g], causal if groups[g][2] else None) for g, x, hh in chains]
            sufs = [_dot(sp.astype(BF16), later) for sp, _ in heads]
            das = [_dot_nt(doms[x][hh], vs[g]) for g, x, hh in chains]
            cur = [list(carry[0]), list(carry[1])]
            sigs, gs, abs_, cg_before = [], [], [], []
            for (g_, x, hh), (sp, lpos), suf, da in zip(chains, heads, sufs, das):
                cl, cg, dq = cur[x][hh]
                cl = cl + jnp.sum(sp, axis=1, keepdims=True)
                sig = jnp.exp(lpos)
                a = sig * jnp.exp(-suf - (tots[x][hh] - cl))
                if groups[g_][2]:
                    a = jnp.where(causal, a, 0.0)
                g = a * da
                sigs.append(sig)
                gs.append(g)
                abs_.append(a.astype(BF16))
                cg_before.append(cg)
                cur[x][hh] = (cl, cg + jnp.sum(g, axis=1, keepdims=True), dq)
            prefs = [_dot(g.astype(BF16), before) for g in gs]
            dvs = [_dot_tn(ab, doms[x][hh]) for (_, x, hh), ab in zip(chains, abs_)]
            dzs = []
            for (g_, x, hh), sig, g, pref, cg in zip(chains, sigs, gs, prefs, cg_before):
                dz = g - sig * (g + pref + cg)
                if groups[g_][2]:
                    dz = jnp.where(causal, dz, 0.0)
                dzs.append(dz.astype(BF16))
            dqs = [_dot(dz, ks[g_]) for (g_, x, hh), dz in zip(chains, dzs)]
            dks = [_dot_tn(dz, qms[x][hh]) for (_, x, hh), dz in zip(chains, dzs)]
            for n, (_, x, hh) in enumerate(chains):
                cl, cg, dq = cur[x][hh]
                cur[x][hh] = (cl, cg, dq + dqs[n])
            for g_, off in enumerate(offs):
                dk_ref[pl.ds(off, t), :] += dks[2 * g_] + dks[2 * g_ + 1]
                dv_ref[pl.ds(off, t), :] += dvs[2 * g_] + dvs[2 * g_ + 1]
            return (tuple(cur[0]), tuple(cur[1]))

        taken = steps_ref[p, i]
        n_full = jnp.minimum(taken, 2 * i)
        zero = (jnp.zeros((t, 1), F32), jnp.zeros((t, 1), F32), jnp.zeros((t, LANES), F32))
        carry = ((zero, zero), (zero, zero))
        carry = lax.cond(jnp.logical_and(i > 0, taken > 2 * i), lambda ca: step([(1, 0, False)], ca), lambda ca: ca,
                         carry)
        carry = lax.fori_loop(
            0, n_full - 1,
            lambda n, ca: step([(0, 2 * i - n_full + n, False), (1, 2 * i + 1 - n_full + n, False)], ca), carry)
        carry = lax.cond(
            i == 0,
            lambda ca: step([(1, 0, False), (0, 0, True), (1, 1, True)], ca),
            lambda ca: step([(0, 2 * i - 1, False), (1, 2 * i, False), (0, 2 * i, True), (1, 2 * i + 1, True)], ca),
            carry)
        for x in range(2):
            dq = jnp.where(lane_hi == 0, carry[x][0][2], carry[x][1][2])
            dq_ref[pl.ds(x * t, t), :] = (dq * scale).astype(dq_ref.dtype)

    row_spec = pl.BlockSpec((2 * t, LANES), lambda p, i, ns: (i, p))
    full_spec = pl.BlockSpec((s, LANES), lambda p, i, ns: (0, p))
    return _pcall(
        body,
        name="sb_bwd",
        grid_spec=pltpu.PrefetchScalarGridSpec(
            num_scalar_prefetch=1,
            grid=(n_pairs, nq),
            in_specs=[
                pl.BlockSpec((2 * t, LANES), lambda p, i, ns: (i, _SB_Q0 + p)),
                pl.BlockSpec((s, LANES), lambda p, i, ns: (0, _SB_K0 + p)),
                pl.BlockSpec((s, LANES), lambda p, i, ns: (0, _SB_V0 + p)),
                row_spec, row_spec,
            ],
            out_specs=[row_spec, full_spec, full_spec],
        ),
        out_shape=[jax.ShapeDtypeStruct((s, SB_WIDTH), BF16), jax.ShapeDtypeStruct((s, SB_WIDTH), F32),
                   jax.ShapeDtypeStruct((s, SB_WIDTH), F32)],
        compiler_params=_cparams(2),
    )(n_steps, qkv, qkv, qkv, do_b, tot_b)


def _gates(gl, bg):
    return _sigmoid(gl[:, :D_MODEL] + bg[:, :D_MODEL]), _sigmoid(gl[:, D_MODEL:] + bg[:, D_MODEL:])


def _mixer_fwd(o_a, o_b, gl, x0, bg, g2, w_ud, w_us, w_out, tm):
    def epi(_, rows, consts):
        oa, ob, glv, x = rows
        bgv, g2v, wud, wus, wout = consts
        ga, gb = _gates(glv, bgv)
        merged = ga * _dot(oa, wud) + gb * _dot(ob, wus)
        x1 = x + _dot(merged.astype(BF16), wout)
        r, xh = _rms_stats(x1)
        return [x1, xh * g2v], []

    return _rowk("mixer_fwd", tm=tm, rows=[o_a, o_b, gl, x0], consts=[bg, g2, w_ud, w_us, w_out],
                 row_outs=[(D_MODEL, F32), (D_MODEL, BF16)], epilogue=epi)


def _mixer_bwd(dx1, o_a, o_b, gl, bg, w_ud, w_us, w_out, tm, rider=None):
    s = dx1.shape[0]
    nm = s // tm

    def body(dx_ref, oa_ref, ob_ref, gl_ref, bg_ref, wud_ref, wus_ref, wout_ref,
             doa_ref, dob_ref, dgl_ref, gwout_ref, gwud_ref, gwus_ref, gbg_ref):
        i = pl.program_id(0)
        dxb = dx_ref[...].astype(BF16)
        oa, ob = oa_ref[...], ob_ref[...]
        ga, gb = _gates(gl_ref[...], bg_ref[...])
        ua, ub = _dot(oa, wud_ref[...]), _dot(ob, wus_ref[...])
        merged = (ga * ua + gb * ub).astype(BF16)
        dm = _dot_nt(dxb, wout_ref[...])
        dua = (dm * ga).astype(BF16)
        dub = (dm * gb).astype(BF16)
        dgla = dm * ua * ga * (1.0 - ga)
        dglb = dm * ub * gb * (1.0 - gb)
        doa_ref[...] = _dot_nt(dua, wud_ref[...]).astype(doa_ref.dtype)
        dob_ref[...] = _dot_nt(dub, wus_ref[...]).astype(dob_ref.dtype)
        dgl_ref[:, :D_MODEL] = dgla.astype(dgl_ref.dtype)
        dgl_ref[:, D_MODEL:] = dglb.astype(dgl_ref.dtype)
        parts = [(gwout_ref, _dot_tn(merged, dxb)), (gwud_ref, _dot_tn(oa, dua)), (gwus_ref, _dot_tn(ob, dub))]
        for r, v in parts:

            @pl.when(i == 0)
            def _(r=r, v=v):
                r[...] = v

            @pl.when(i > 0)
            def _(r=r, v=v):
                r[...] += v

        sa = jnp.sum(dgla, axis=0, keepdims=True)
        sb = jnp.sum(dglb, axis=0, keepdims=True)

        @pl.when(i == 0)
        def _():
            gbg_ref[:, :D_MODEL] = sa
            gbg_ref[:, D_MODEL:] = sb

        @pl.when(i > 0)
        def _():
            gbg_ref[:, :D_MODEL] += sa
            gbg_ref[:, D_MODEL:] += sb

    row = lambda w: pl.BlockSpec((tm, w), lambda i: (i, 0))
    full = lambda a: pl.BlockSpec(a.shape, lambda i: (0, 0), pipeline_mode=pl.Buffered(1))
    fshape = lambda r, c: jax.ShapeDtypeStruct((r, c), F32)
    return _call(
        body, (dx1, o_a, o_b, gl, bg, w_ud, w_us, w_out), rider,
        name="mixer_bwd",
        grid=(nm,),
        in_specs=[row(D_MODEL), row(DIL_OUT_WIDTH), row(SB_WIDTH), row(2 * D_MODEL),
                  full(bg), full(w_ud), full(w_us), full(w_out)],
        out_specs=[row(DIL_OUT_WIDTH), row(SB_WIDTH), row(2 * D_MODEL),
                   pl.BlockSpec((D_MODEL, D_MODEL), lambda i: (0, 0)),
                   pl.BlockSpec((DIL_OUT_WIDTH, D_MODEL), lambda i: (0, 0)),
                   pl.BlockSpec((SB_WIDTH, D_MODEL), lambda i: (0, 0)),
                   pl.BlockSpec((1, 2 * D_MODEL), lambda i: (0, 0))],
        out_shape=[jax.ShapeDtypeStruct((s, DIL_OUT_WIDTH), BF16), jax.ShapeDtypeStruct((s, SB_WIDTH), BF16),
                   jax.ShapeDtypeStruct((s, 2 * D_MODEL), BF16),
                   fshape(D_MODEL, D_MODEL), fshape(DIL_OUT_WIDTH, D_MODEL), fshape(SB_WIDTH, D_MODEL),
                   fshape(1, 2 * D_MODEL)],
        compiler_params=_cparams(1),
    )


def _all_gather(shards):
    n = len(shards)

    def body(*refs):
        x_refs, out_refs = refs[:n], refs[n:2 * n]
        send_sems, recv_sems, local_sems = refs[2 * n:]
        x, y, c = lax.axis_index("x"), lax.axis_index("y"), lax.axis_index("c")
        me, sibling = (x, y, c), (x, y, 1 - c)
        chips = [(1 - x, y), (x, 1 - y), (1 - x, 1 - y)]

        def slot(a, px, py, pc):
            return out_refs[a].at[4 * px + 2 * py + pc]

        def copy(a, k, block, to, own=False):
            return pltpu.make_async_remote_copy(
                src_ref=x_refs[a] if own else slot(a, *block), dst_ref=slot(a, *block),
                send_sem=send_sems.at[7 * a + k], recv_sem=recv_sems.at[7 * a + k], device_id=to, device_id_type=_MESH)

        mine = [pltpu.make_async_copy(x_refs[a], slot(a, *me), local_sems.at[a]) for a in range(n)]
        for cp in mine:
            cp.start()
        first = []
        for a in range(n):
            first.append(copy(a, 0, me, sibling, own=True))
            first += [copy(a, 1 + j, me, (*chip, c), own=True) for j, chip in enumerate(chips)]
        for cp in first:
            cp.start()
        passed = []
        for a in range(n):
            for j, chip in enumerate(chips):
                copy(a, 1 + j, (*chip, c), me).wait_recv()
                passed.append(copy(a, 4 + j, (*chip, c), sibling))
                passed[-1].start()
        for a in range(n):
            copy(a, 0, sibling, me).wait_recv()
            for j, chip in enumerate(chips):
                copy(a, 4 + j, (*chip, 1 - c), me).wait_recv()
        for cp in first + passed:
            cp.wait_send()
        for cp in mine:
            cp.wait()

    return _pcall(
        body,
        name="all_gather_weights",
        in_specs=[_HBM] * n,
        out_specs=[_HBM] * n,
        out_shape=[jax.ShapeDtypeStruct((N_DEV,) + s.shape, s.dtype) for s in shards],
        scratch_shapes=[pltpu.SemaphoreType.DMA((7 * n,)), pltpu.SemaphoreType.DMA((7 * n,)),
                        pltpu.SemaphoreType.DMA((n,))],
    )(*shards)


def _exchange(chunks):
    n = len(chunks)

    def body(*refs):
        g_refs, o_refs = refs[:n], refs[n:2 * n]
        send_sems, recv_sems, local_sems = refs[2 * n:]
        x, y, c = lax.axis_index("x"), lax.axis_index("y"), lax.axis_index("c")
        me = 4 * x + 2 * y + c
        own = [pltpu.make_async_copy(g_refs[a].at[me], o_refs[a].at[me], local_sems.at[a]) for a in range(n)]
        for cp in own:
            cp.start()
        copies = []
        for a in range(n):
            for k in range(1, N_DEV):
                px, py, pc = x ^ (k >> 2), y ^ ((k >> 1) & 1), c ^ (k & 1)
                peer = 4 * px + 2 * py + pc
                copies.append(pltpu.make_async_remote_copy(
                    src_ref=g_refs[a].at[peer], dst_ref=o_refs[a].at[me], send_sem=send_sems.at[7 * a + k - 1],
                    recv_sem=recv_sems.at[7 * a + k - 1], device_id=(px, py, pc), device_id_type=_MESH))
        for cp in copies:
            cp.start()
        for cp in copies:
            cp.wait()
        for cp in own:
            cp.wait()

    return _pcall(
        body,
        name="exchange_grads",
        in_specs=[_HBM] * n,
        out_specs=[_HBM] * n,
        out_shape=[jax.ShapeDtypeStruct(g.shape, g.dtype) for g in chunks],
        scratch_shapes=[pltpu.SemaphoreType.DMA((7 * n,)), pltpu.SemaphoreType.DMA((7 * n,)),
                        pltpu.SemaphoreType.DMA((n,))],
    )(*chunks)


def _reduce_adamw(name, parts, w, m, v, tr):
    _, rows, cols = parts.shape
    tr = min(tr, rows)
    assert rows % tr == 0
    c1 = 1.0 / (1.0 - ADAM_B1 ** ADAM_STEP)
    c2 = 1.0 / (1.0 - ADAM_B2 ** ADAM_STEP)

    def body(p_ref, w_ref, m_ref, v_ref, g_out, d_out, m_out, v_out):
        g = p_ref[0].astype(F32)
        for d in range(1, N_DEV):
            g = g + p_ref[d].astype(F32)
        mn = ADAM_B1 * m_ref[...] + (1.0 - ADAM_B1) * g
        vn = ADAM_B2 * v_ref[...] + (1.0 - ADAM_B2) * (g * g)
        g_out[...] = g
        m_out[...] = mn
        v_out[...] = vn
        d_out[...] = -ADAM_LR * ((mn * c1) / (jnp.sqrt(vn * c2) + ADAM_EPS) + ADAM_WD * w_ref[...])

    spec = pl.BlockSpec((tr, cols), lambda i: (i, 0))
    return _pcall(
        body,
        name=name,
        grid=(rows // tr,),
        in_specs=[pl.BlockSpec((N_DEV, tr, cols), lambda i: (0, i, 0)), spec, spec, spec],
        out_specs=[spec] * 4,
        out_shape=[jax.ShapeDtypeStruct((rows, cols), F32)] * 4,
        compiler_params=_cparams(1),
    )(parts, w, m, v)


_SHARDED = ("w_in", "w_up_dil", "w_up_sb", "w_out", "w_mlp_in", "w_mlp_out")
_FULL_SHAPES = {"w_in": (D_MODEL, IN_COLS), "w_up_dil": (DIL_OUT_WIDTH, D_MODEL), "w_up_sb": (SB_WIDTH, D_MODEL),
                "w_out": (D_MODEL, D_MODEL), "w_mlp_in": (D_MODEL, D_FF), "w_mlp_out": (D_FF, D_MODEL)}
_ROW_SHARDED = ("w_out", "w_mlp_out")


def _shard_shape(name):
    r, c = _FULL_SHAPES[name]
    return (r // N_DEV, c) if name in _ROW_SHARDED else (r, c // N_DEV)


def _assemble(name, gathered):
    r, c = _shard_shape(name)
    if name in _ROW_SHARDED:
        return gathered.reshape(N_DEV * r, c)
    return gathered.transpose(1, 0, 2).reshape(r, N_DEV * c)


def _chunk(name, full):
    r, c = _shard_shape(name)
    if name in _ROW_SHARDED:
        return full.reshape(N_DEV, r, c)
    return full.reshape(r, N_DEV, c).transpose(1, 0, 2)


_SMALL = (("norm_mix_g", D_MODEL), ("b_gate", 2 * D_MODEL), ("norm_mlp_g", D_MODEL), ("norm_final_g", D_MODEL))
_SMALL_N = sum(n for _, n in _SMALL) + LANES


def _pack_small(vals, tail):
    return jnp.concatenate([vals[n].reshape(1, -1) for n, _ in _SMALL] + [tail], axis=1)


def _unpack_small(vec, shapes):
    out, pos = {}, 0
    for n, width in _SMALL:
        out[n] = vec[:, pos:pos + width].reshape(shapes[n])
        pos += width
    return out, vec[:, pos:]


def kernel(x, norm_mix_g, w_in, b_gate, w_up_dil, w_up_sb, w_out, norm_mlp_g, w_mlp_in, w_mlp_out, norm_final_g, loss_target, m_norm_mix_g, m_w_in, m_b_gate, m_w_up_dil, m_w_up_sb, m_w_out, m_norm_mlp_g, m_w_mlp_in, m_w_mlp_out, m_norm_final_g, v_norm_mix_g, v_w_in, v_b_gate, v_w_up_dil, v_w_up_sb, v_w_out, v_norm_mlp_g, v_w_mlp_in, v_w_mlp_out, v_norm_final_g):
    given = dict(locals())
    s = x.shape[1]
    x0 = x.reshape(s, D_MODEL)
    target = loss_target.reshape(s, D_MODEL)
    g1 = norm_mix_g.reshape(1, D_MODEL)
    g2 = norm_mlp_g.reshape(1, D_MODEL)
    g3 = norm_final_g.reshape(1, D_MODEL)
    bg = b_gate.reshape(1, 2 * D_MODEL)
    w_shards = {n: given[n].reshape(_shard_shape(n)) for n in _SHARDED}
    m_shards = {n: given["m_" + n].reshape(_shard_shape(n)) for n in _SHARDED}
    v_shards = {n: given["v_" + n].reshape(_shard_shape(n)) for n in _SHARDED}

    shard_b = {n: w_shards[n].astype(BF16) for n in _SHARDED}
    (gathered_w_in,) = _all_gather([shard_b["w_in"]])
    w_in_f = _assemble("w_in", gathered_w_in)
    w_qkv, w_gl = _group_major(w_in_f[:, :QKV_COLS]), w_in_f[:, QKV_COLS:]
    full = {}

    def norm1(_, rows, consts):
        _, xh = _rms_stats(rows[0])
        return [xh * consts[0]], []

    (h1,) = _rowk("norm_mix", tm=1024, rows=[x0], consts=[g1], row_outs=[(D_MODEL, BF16)], epilogue=norm1)
    qkv, (land,) = _mm("proj_qkv", h1, w_qkv, out_dtype=BF16, tm=1024, tn=768, tk=D_MODEL,
                       rider=_Spread([shard_b["w_mlp_in"]], chunked=False))
    full["w_mlp_in"] = _assemble("w_mlp_in", land)
    gl = _mm("proj_gates", h1, w_gl, out_dtype=BF16, tm=1024, tn=1024, tk=D_MODEL)
    views = [_dil_view(qkv, g) for g in range(len(DIL_GROUPS))]
    dil = [_dil_fwd(views[g], g) for g in range(len(DIL_GROUPS))]
    os_, lses = [d[0] for d in dil], [d[1] for d in dil]
    o_a = _dil_mix_fwd(os_, lses)
    riding = ("w_mlp_out", "w_out", "w_up_sb", "w_up_dil")
    (o_b, tot_b, sb_steps), lands = _sb_fwd(qkv, rider=_Spread([shard_b[n] for n in riding], chunked=False))
    full.update({n: _assemble(n, land) for n, land in zip(riding, lands)})
    x1, h2 = _mixer_fwd(o_a, o_b, gl, x0, bg, g2, full["w_up_dil"], full["w_up_sb"], full["w_out"], 512)
    f = _mm("mlp_in", h2, full["w_mlp_in"], out_dtype=BF16, tm=1024, tn=1024, tk=D_MODEL,
            epilogue=lambda r, _: jnp.square(jnp.maximum(r, 0.0)))

    def head(acc, rows, consts):
        x1v, tv = rows
        g3v = consts[0]
        x2 = x1v + acc
        r, xh = _rms_stats(x2)
        diff = xh * g3v - tv
        loss = (0.5 / D_MODEL) * jnp.sum(jnp.sum(diff * diff, axis=0, keepdims=True), axis=1, keepdims=True)
        dy = diff * (1.0 / D_MODEL)
        dx2, dg = _rms_bwd(dy, xh, r, g3v)
        return [dx2, dx2], [dg, jnp.broadcast_to(loss, (1, LANES))]

    dx2, dx2b, gg3, loss_part = _rowk(
        "mlp_out_loss", a=f, w=full["w_mlp_out"], tm=512, tk=D_FF, rows=[x1, target], consts=[g3],
        row_outs=[(D_MODEL, F32), (D_MODEL, BF16)], acc_outs=[D_MODEL, LANES], epilogue=head)

    da = _mm("mlp_out_bwd", dx2b, full["w_mlp_out"], tb=True, out_dtype=BF16, tm=1024, tn=1024, tk=D_MODEL, extra=f,
             epilogue=lambda r, fv: r * (2.0 * jnp.sqrt(fv.astype(F32))))
    g_w_mlp_out = _mm("grad_w_mlp_out", f, dx2b, ta=True, out_dtype=F32, tm=1024, tn=1024, tk=2048)
    g_w_mlp_in = _mm("grad_w_mlp_in", h2, da, ta=True, out_dtype=F32, tm=1024, tn=1024, tk=2048)

    def norm_bwd(acc, rows, consts):
        xv, dres = rows
        r, xh = _rms_stats(xv)
        dx, dg = _rms_bwd(acc, xh, r, consts[0])
        return [dres + dx], [dg]

    bchunk = lambda n, g: _chunk(n, g).astype(BF16)
    parts = {}
    (dx1, gg2), (parts["w_mlp_in"],) = _rowk(
        "mlp_in_bwd", a=da, w=full["w_mlp_in"], nt=True, tm=512, tk=D_FF, rows=[x1, dx2], consts=[g2],
        row_outs=[(D_MODEL, F32)], acc_outs=[D_MODEL], epilogue=norm_bwd,
        rider=_Spread([bchunk("w_mlp_in", g_w_mlp_in)], chunked=True))
    (do_a, do_b, dgl, g_w_out, g_w_ud, g_w_us, g_bg), (parts["w_mlp_out"],) = _mixer_bwd(
        dx1, o_a, o_b, gl, bg, full["w_up_dil"], full["w_up_sb"], full["w_out"], 512,
        rider=_Spread([bchunk("w_mlp_out", g_w_mlp_out)], chunked=True))
    mix = _dil_mix_bwd(do_a, os_, lses)
    small_three = {"w_out": g_w_out, "w_up_sb": g_w_us, "w_up_dil": g_w_ud}
    grads, lands = _dil_bwd(views[0], mix[0], lses[0], mix[3], 0,
                            rider=_Spread([bchunk(n, g) for n, g in small_three.items()], chunked=True))
    parts.update(dict(zip(small_three, lands)))
    dil_b = [grads] + [_dil_bwd(views[g], mix[g], lses[g], mix[3 + g], g) for g in (1, 2)]
    dq_b, dk_b, dv_b = _sb_bwd(qkv, do_b, tot_b, sb_steps)
    dproj = [d[0] for d in dil_b] + [d[1] for d in dil_b] + [d[2] for d in dil_b] + [dq_b, dk_b, dv_b, dgl]
    g_w_in = jnp.concatenate([
        _grad_cols("grad_w_in_dil", h1, dproj[:9], tm=D_MODEL, tk=1024),
        _grad_cols("grad_w_in_sb", h1, dproj[9:12], tm=D_MODEL, tk=1024),
        _grad_cols("grad_w_in_gates", h1, dproj[12:], tm=D_MODEL, tk=1024)], axis=1)
    (grad_x, gg1), (parts["w_in"],) = _rowk(
        "in_proj_bwd", a=dproj, w=w_in_f, nt=True, tm=512, tk=IN_COLS, rows=[x0, dx1], consts=[g1],
        row_outs=[(D_MODEL, F32)], acc_outs=[D_MODEL], epilogue=norm_bwd,
        rider=_Spread([bchunk("w_in", g_w_in)], chunked=True))

    small_part = _pack_small({"norm_mix_g": gg1, "b_gate": g_bg, "norm_mlp_g": gg2, "norm_final_g": gg3}, loss_part)
    (small_parts,) = _exchange([jnp.broadcast_to(small_part[None], (N_DEV, 1, _SMALL_N))])

    tags = ("grad_", "delta_", "new_m_", "new_v_")
    outs = {}
    for n, p in parts.items():
        res = _reduce_adamw("adamw_" + n, p, w_shards[n], m_shards[n], v_shards[n], 256)
        for tag, val in zip(tags, res):
            outs[tag + n] = val.reshape(given[n].shape)
    small_w = _pack_small(given, jnp.zeros((1, LANES), F32))
    small_m = _pack_small({n: given["m_" + n] for n, _ in _SMALL}, jnp.zeros((1, LANES), F32))
    small_v = _pack_small({n: given["v_" + n] for n, _ in _SMALL}, jnp.ones((1, LANES), F32))
    small_res = _reduce_adamw("adamw_replicated", small_parts, small_w, small_m, small_v, 8)

    small_shapes = {n: given[n].shape for n, _ in _SMALL}
    for tag, small in zip(tags, small_res):
        small_vals, tail = _unpack_small(small, small_shapes)
        for n, val in small_vals.items():
            outs[tag + n] = val
        if tag == "grad_":
            loss = tail[0, 0]
    names = ["norm_mix_g", "w_in", "b_gate", "w_up_dil", "w_up_sb", "w_out", "norm_mlp_g", "w_mlp_in", "w_mlp_out",
             "norm_final_g"]
    return (loss, grad_x.reshape(x.shape), *[outs["grad_" + n] for n in names], *[outs["delta_" + n] for n in names],
            *[outs["new_m_" + n] for n in names], *[outs["new_v_" + n] for n in names])
```

```python
import functools
import math

import jax
import jax.numpy as jnp
from jax import lax
from jax.experimental import pallas as pl
from jax.experimental.pallas import tpu as pltpu

_pcall = pl.pallas_call

F32 = jnp.float32
BF16 = jnp.bfloat16

D_MODEL = 1024
HEAD_DIM = 64
DIL_GROUPS = ((128, 1), (512, 4), (2048, 16))
DIL_HEADS_PER_GROUP = 4
N_DIL_HEADS = 12
N_SB_HEADS = 8
DIL_WIDTH = 768
DIL_OUT_WIDTH = 256
SB_WIDTH = 512
D_FF = 4096
BLOCK = 128
RMS_EPS = 1e-6
NEG_INF = -1e30
QKV_COLS = 3 * DIL_WIDTH + 3 * SB_WIDTH
IN_COLS = QKV_COLS + 2 * D_MODEL
N_DEV = 8

ADAM_LR = 0.001
ADAM_B1 = 0.9
ADAM_B2 = 0.999
ADAM_EPS = 1e-08
ADAM_WD = 0.01
ADAM_STEP = 10

VMEM_LIMIT = 56 * 1024 * 1024
SB_TK = 256
LANES = 128

_ARB = pltpu.ARBITRARY


def _cparams(n_axes, **kw):
    return pltpu.CompilerParams(dimension_semantics=(_ARB,) * n_axes, vmem_limit_bytes=VMEM_LIMIT, **kw)


def _dot(a, b):
    return jnp.dot(a, b, preferred_element_type=F32)


def _dot_nt(a, b):
    return lax.dot_general(a, b, (((1,), (1,)), ((), ())), preferred_element_type=F32)


def _dot_tn(a, b):
    return lax.dot_general(a, b, (((0,), (0,)), ((), ())), preferred_element_type=F32)


def _split_hi_lo(x):
    hi = x.astype(BF16)
    lo = (x - hi.astype(F32)).astype(BF16)
    return hi, lo


def _dot_hi_lo(x, m):
    hi, lo = _split_hi_lo(x)
    return _dot(hi, m) + _dot(lo, m)


def _sigmoid(x):
    return 1.0 / (1.0 + jnp.exp(-x))


_HBM = pl.BlockSpec(memory_space=pltpu.HBM)
_MESH = pl.DeviceIdType.MESH


class _Spread:
    def __init__(self, srcs, chunked):
        self.srcs, self.chunked, self.n = list(srcs), chunked, len(srcs)

    def land_shapes(self):
        return [jax.ShapeDtypeStruct((N_DEV,) + (s.shape[1:] if self.chunked else s.shape), s.dtype) for s in self.srcs]

    def scratch(self):
        dma = pltpu.SemaphoreType.DMA
        return [dma((7 * self.n,)), dma((7 * self.n,)), dma((self.n,))]

    def copies(self, src_refs, land_refs, send_sems, recv_sems, local_sems):
        x, y, c = lax.axis_index("x"), lax.axis_index("y"), lax.axis_index("c")
        me = 4 * x + 2 * y + c
        out = []
        for a, (src, land) in enumerate(zip(src_refs, land_refs)):
            out.append(pltpu.make_async_copy(src.at[me] if self.chunked else src, land.at[me], local_sems.at[a]))
            for k in range(1, N_DEV):
                px, py, pc = x ^ (k >> 2), y ^ ((k >> 1) & 1), c ^ (k & 1)
                out.append(pltpu.make_async_remote_copy(
                    src_ref=src.at[4 * px + 2 * py + pc] if self.chunked else src, dst_ref=land.at[me],
                    send_sem=send_sems.at[7 * a + k - 1], recv_sem=recv_sems.at[7 * a + k - 1],
                    device_id=(px, py, pc), device_id_type=_MESH))
        return out


def _call(body, args, rider=None, **kw):
    if rider is None:
        return _pcall(body, **kw)(*args)
    grid = kw["grid"]
    single = not isinstance(kw["out_shape"], (list, tuple))
    out_specs = [kw["out_specs"]] if single else list(kw["out_specs"])
    out_shape = [kw["out_shape"]] if single else list(kw["out_shape"])
    in_specs, scratch = list(kw["in_specs"]), list(kw.get("scratch_shapes", []))
    n_in, n_out, n_s, n = len(in_specs), len(out_shape), len(scratch), rider.n

    def hosted(*refs):
        ins, srcs = refs[:n_in], refs[n_in:n_in + n]
        outs, lands = refs[n_in + n:n_in + n + n_out], refs[n_in + n + n_out:n_in + 2 * n + n_out]
        own_scratch, sems = refs[n_in + 2 * n + n_out:n_in + 2 * n + n_out + n_s], refs[n_in + 2 * n + n_out + n_s:]
        ids = [pl.program_id(d) for d in range(len(grid))]
        first = functools.reduce(jnp.logical_and, [i == 0 for i in ids])
        last = functools.reduce(jnp.logical_and, [i == g - 1 for i, g in zip(ids, grid)])
        copies = rider.copies(srcs, lands, *sems)

        @pl.when(first)
        def _():
            for cp in copies:
                cp.start()

        body(*ins, *outs, *own_scratch)

        @pl.when(last)
        def _():
            for cp in copies:
                cp.wait()

    kw = dict(kw, in_specs=in_specs + [_HBM] * n, out_specs=out_specs + [_HBM] * n,
              out_shape=out_shape + rider.land_shapes(), scratch_shapes=scratch + rider.scratch())
    res = _pcall(hosted, **kw)(*args, *rider.srcs)
    return (res[0] if single else list(res[:n_out])), list(res[n_out:])


def _mm(name, a, b, *, ta=False, tb=False, out_dtype, tm, tn, tk, epilogue=None, extra=None, rider=None):
    m = a.shape[1] if ta else a.shape[0]
    k = a.shape[0] if ta else a.shape[1]
    n = b.shape[0] if tb else b.shape[1]
    assert (b.shape[1] if tb else b.shape[0]) == k
    tm, tn, tk = min(tm, m), min(tn, n), min(tk, k)
    assert m % tm == 0 and n % tn == 0 and k % tk == 0, (name, m, n, k, tm, tn, tk)
    nk = k // tk
    dn = (((0 if ta else 1,), (1 if tb else 0,)), ((), ()))
    in_place = nk > 1 and epilogue is None and out_dtype == F32

    def body(*refs):
        if extra is not None:
            a_ref, b_ref, e_ref, o_ref = refs[:4]
        else:
            a_ref, b_ref, o_ref = refs[:3]
            e_ref = None

        def finish(r):
            if epilogue is not None:
                r = epilogue(r, None if e_ref is None else e_ref[...])
            o_ref[...] = r.astype(out_dtype)

        part = lax.dot_general(a_ref[...].astype(BF16), b_ref[...].astype(BF16), dn, preferred_element_type=F32)
        if nk == 1:
            finish(part)
        else:
            acc_ref = o_ref if in_place else refs[-1]
            kk = pl.program_id(2)

            @pl.when(kk == 0)
            def _():
                acc_ref[...] = part

            @pl.when(kk > 0)
            def _():
                acc_ref[...] += part

            if not in_place:

                @pl.when(kk == nk - 1)
                def _():
                    finish(acc_ref[...])

    a_spec = pl.BlockSpec((tk, tm), lambda j, i, kk: (kk, i)) if ta else pl.BlockSpec((tm, tk), lambda j, i, kk: (i, kk))
    b_spec = pl.BlockSpec((tn, tk), lambda j, i, kk: (j, kk)) if tb else pl.BlockSpec((tk, tn), lambda j, i, kk: (kk, j))
    o_spec = pl.BlockSpec((tm, tn), lambda j, i, kk: (i, j))
    in_specs = [a_spec, b_spec]
    args = [a, b]
    if extra is not None:
        in_specs.append(o_spec)
        args.append(extra)
    return _call(
        body, args, rider,
        name=name,
        grid=(n // tn, m // tm, nk),
        in_specs=in_specs,
        out_specs=o_spec,
        out_shape=jax.ShapeDtypeStruct((m, n), out_dtype),
        scratch_shapes=[pltpu.VMEM((tm, tn), F32)] if (nk > 1 and not in_place) else [],
        compiler_params=_cparams(3),
    )


def _grad_cols(name, a, parts, *, tm, tk, rider=None):
    k, m = a.shape
    n = sum(p.shape[1] for p in parts)
    assert m % tm == 0 and k % tk == 0
    nk = k // tk

    def body(*refs):
        a_ref, p_refs, o_ref, acc_ref = refs[0], refs[1:1 + len(parts)], refs[1 + len(parts)], refs[2 + len(parts)]
        kk = pl.program_id(1)
        side_by_side = jnp.concatenate([p_ref[...].astype(BF16) for p_ref in p_refs], axis=1)
        term = _dot_tn(a_ref[...].astype(BF16), side_by_side)

        @pl.when(kk == 0)
        def _():
            acc_ref[...] = term

        @pl.when(kk > 0)
        def _():
            acc_ref[...] += term

        @pl.when(kk == nk - 1)
        def _():
            o_ref[...] = acc_ref[...].astype(o_ref.dtype)

    return _call(
        body, [a] + list(parts), rider,
        name=name,
        grid=(m // tm, nk),
        in_specs=[pl.BlockSpec((tk, tm), lambda i, kk: (kk, i))]
        + [pl.BlockSpec((tk, p.shape[1]), lambda i, kk: (kk, 0)) for p in parts],
        out_specs=pl.BlockSpec((tm, n), lambda i, kk: (i, 0)),
        out_shape=jax.ShapeDtypeStruct((m, n), BF16),
        scratch_shapes=[pltpu.VMEM((tm, n), F32)],
        compiler_params=_cparams(2),
    )


def _rowk(name, *, a=None, w=None, nt=False, tm, tk=None, rows=(), consts=(), row_outs=(), acc_outs=(), epilogue,
          rider=None):
    has_mm = a is not None
    a_parts = list(a) if isinstance(a, (list, tuple)) else ([a] if has_mm else [])
    n_a = len(a_parts)
    m = a_parts[0].shape[0] if has_mm else rows[0].shape[0]
    assert m % tm == 0
    nm = m // tm
    if has_mm:
        k = sum(p.shape[1] for p in a_parts)
        n = w.shape[0] if nt else w.shape[1]
        tk = min(tk, k)
        assert k % tk == 0 and (n_a == 1 or tk == k)
        nk = k // tk
    else:
        nk = 1
    n_rows, n_consts, n_ro, n_ao = len(rows), len(consts), len(row_outs), len(acc_outs)

    def body(*refs):
        pos = 0
        if has_mm:
            a_refs, w_ref = refs[:n_a], refs[n_a]
            pos = n_a + 1
        row_refs = refs[pos:pos + n_rows]
        pos += n_rows
        const_refs = refs[pos:pos + n_consts]
        pos += n_consts
        ro_refs = refs[pos:pos + n_ro]
        pos += n_ro
        ao_refs = refs[pos:pos + n_ao]
        pos += n_ao
        i = pl.program_id(0)
        kk = pl.program_id(1)

        def finish(acc):
            ro_vals, ao_vals = epilogue(acc, [r[...] for r in row_refs], [c[...] for c in const_refs])
            for r, v in zip(ro_refs, ro_vals):
                r[...] = v.astype(r.dtype)
            for r, v in zip(ao_refs, ao_vals):

                @pl.when(i == 0)
                def _(r=r, v=v):
                    r[...] = v

                @pl.when(i > 0)
                def _(r=r, v=v):
                    r[...] += v

        if not has_mm:
            finish(None)
            return
        part, off = None, 0
        for a_ref in a_refs:
            width = a_ref.shape[1]
            cols = slice(None) if n_a == 1 else slice(off, off + width)
            av = a_ref[...].astype(BF16)
            term = _dot_nt(av, w_ref[:, cols]) if nt else _dot(av, w_ref[cols, :])
            part = term if part is None else part + term
            off += width
        if nk == 1:
            finish(part)
        else:
            acc_ref = refs[pos]

            @pl.when(kk == 0)
            def _():
                acc_ref[...] = part

            @pl.when(kk > 0)
            def _():
                acc_ref[...] += part

            @pl.when(kk == nk - 1)
            def _():
                finish(acc_ref[...])

    once = pl.Buffered(1)
    in_specs, args = [], []
    if has_mm:
        for part in a_parts:
            in_specs.append(pl.BlockSpec((tm, tk if n_a == 1 else part.shape[1]), lambda i, kk: (i, kk)))
        w_mode = once if nk == 1 else None
        in_specs.append(pl.BlockSpec((n, tk), lambda i, kk: (0, kk), pipeline_mode=w_mode) if nt
                        else pl.BlockSpec((tk, n), lambda i, kk: (kk, 0), pipeline_mode=w_mode))
        args += a_parts + [w]
    for r in rows:
        in_specs.append(pl.BlockSpec((tm, r.shape[1]), lambda i, kk: (i, 0)))
        args.append(r)
    for c in consts:
        in_specs.append(pl.BlockSpec(c.shape, lambda i, kk: (0,) * c.ndim, pipeline_mode=once))
        args.append(c)
    out_specs, out_shape = [], []
    for width, dt in row_outs:
        out_specs.append(pl.BlockSpec((tm, width), lambda i, kk: (i, 0)))
        out_shape.append(jax.ShapeDtypeStruct((m, width), dt))
    for width in acc_outs:
        out_specs.append(pl.BlockSpec((1, width), lambda i, kk: (0, 0)))
        out_shape.append(jax.ShapeDtypeStruct((1, width), F32))
    return _call(
        body, args, rider,
        name=name,
        grid=(nm, nk),
        in_specs=in_specs,
        out_specs=out_specs,
        out_shape=out_shape,
        scratch_shapes=[pltpu.VMEM((tm, n), F32)] if (has_mm and nk > 1) else [],
        compiler_params=_cparams(2),
    )


def _rms_stats(x):
    r = lax.rsqrt(jnp.mean(x * x, axis=-1, keepdims=True) + RMS_EPS)
    return r, x * r


def _rms_bwd(dh, xh, r, g):
    gy = dh * g
    dx = r * (gy - xh * jnp.mean(gy * xh, axis=-1, keepdims=True))
    return dx, jnp.sum(dh * xh, axis=0, keepdims=True)


def _alibi_slope(head):
    return 2.0 ** (-8.0 * (head + 1) / N_DIL_HEADS)


DIL_STEP_BLOCKS = 4


def _dil_band(first_block):
    qi = lax.broadcasted_iota(jnp.int32, (BLOCK, 2 * BLOCK), 0)
    kj = lax.broadcasted_iota(jnp.int32, (BLOCK, 2 * BLOCK), 1)
    steps = qi + BLOCK - kj
    valid = (steps >= 0) & (steps <= BLOCK)
    if first_block is not False:
        valid = valid & ((kj >= BLOCK) | jnp.logical_not(first_block))
    return steps.astype(F32), valid


def _dil_step_specs(ncb, cols, nblk, clamp):
    def own(col):
        return pl.BlockSpec((nblk * BLOCK, DIL_OUT_WIDTH), lambda r, i: (clamp(i), r * ncb + col))

    def before(col):
        return pl.BlockSpec((BLOCK, DIL_OUT_WIDTH), lambda r, i: (jnp.maximum(clamp(i) * nblk - 1, 0), r * ncb + col))

    return [own(cols[0]), own(cols[1]), before(cols[1]), own(cols[2]), before(cols[2])]


DIL_RELAYOUT_ROWS = 1024


def _view_scratch(width):
    return pltpu.VMEM((width // LANES, DIL_RELAYOUT_ROWS, LANES), F32)


def _rows_from_view(src, scr, d, w):
    sub = src.shape[0]
    for j in range(w // LANES):
        for r in range(d):
            scr[j, pl.ds(r, sub, stride=d), :] = src[:, r * w + j * LANES:r * w + (j + 1) * LANES].astype(F32)


def _rows_to_view(scr, dst, d, w):
    sub = dst.shape[0]
    for j in range(w // LANES):
        for r in range(d):
            dst[:, r * w + j * LANES:r * w + (j + 1) * LANES] = scr[j, pl.ds(r, sub, stride=d), :].astype(dst.dtype)


def _dil_relayout(name, xs, dilation, to_view, col_block=0, width=None):
    d = dilation
    tm = DIL_RELAYOUT_ROWS
    rows = tm // d
    if to_view:
        s = xs[0].shape[0]
        widths = [width or x.shape[1] for x in xs]
    else:
        s = xs[0].shape[0] * d
        widths = [v.shape[1] // d for v in xs]
    assert s % tm == 0 and all(w % LANES == 0 for w in widths)
    n = len(xs)

    def body(*refs):
        in_refs, out_refs, scratch = refs[:n], refs[n:2 * n], refs[2 * n:]
        for src, dst, scr, w in zip(in_refs, out_refs, scratch, widths):
            slabs = [slice(j * LANES, (j + 1) * LANES) for j in range(w // LANES)]
            if to_view:
                for j, slab in enumerate(slabs):
                    scr[j] = src[:, slab].astype(F32)
                _rows_to_view(scr, dst, d, w)
            else:
                _rows_from_view(src, scr, d, w)
                for j, slab in enumerate(slabs):
                    dst[:, slab] = scr[j].astype(dst.dtype)

    natural = [pl.BlockSpec((tm, w), lambda i: (i, col_block)) for w in widths]
    viewed = [pl.BlockSpec((rows, d * w), lambda i: (i, 0)) for w in widths]
    return _pcall(
        body,
        name=name,
        grid=(s // tm,),
        in_specs=natural if to_view else viewed,
        out_specs=viewed if to_view else natural,
        out_shape=[jax.ShapeDtypeStruct((s // d, d * w) if to_view else (s, w), x.dtype) for x, w in zip(xs, widths)],
        scratch_shapes=[_view_scratch(w) for w in widths],
        compiler_params=_cparams(1),
    )(*xs)


def _dil_fwd(view, group):
    window, dilation = DIL_GROUPS[group]
    qkv_v, ncb, cols = view
    sub = qkv_v.shape[0]
    s = sub * dilation
    nb = sub // BLOCK
    assert nb * BLOCK * dilation == s and window // dilation == BLOCK
    nblk = min(DIL_STEP_BLOCKS, nb)
    assert nb % nblk == 0
    slopes = [_alibi_slope(group * DIL_HEADS_PER_GROUP + h) * dilation for h in range(DIL_HEADS_PER_GROUP)]

    def body(q_ref, kc_ref, kp_ref, vc_ref, vp_ref, o_ref, lse_ref):
        i = pl.program_id(1)
        kk_all = jnp.concatenate([kp_ref[...], kc_ref[...]], axis=0)
        vv_all = jnp.concatenate([vp_ref[...], vc_ref[...]], axis=0)
        head_id = lax.broadcasted_iota(jnp.int32, (1, DIL_OUT_WIDTH), 1) // HEAD_DIM
        chains = [(b, h) for b in range(nblk) for h in range(DIL_HEADS_PER_GROUP)]
        rows = lambda b: slice(b * BLOCK, (b + 1) * BLOCK)
        keys = lambda b: slice(b * BLOCK, (b + 2) * BLOCK)
        bands = [_dil_band(i == 0 if b == 0 else False) for b in range(nblk)]
        qs = [q_ref[rows(b), :] for b in range(nblk)]
        scores = [_dot_nt(jnp.where(head_id == h, qs[b], jnp.zeros_like(qs[b])), kk_all[keys(b)]) for b, h in chains]
        ps, lses = [], []
        for (b, h), sc in zip(chains, scores):
            steps, valid = bands[b]
            logits = jnp.where(valid, sc * (1.0 / math.sqrt(HEAD_DIM)) - slopes[h] * steps, NEG_INF)
            mx = jnp.max(logits, axis=1, keepdims=True)
            e = jnp.exp(logits - mx)
            den = jnp.sum(e, axis=1, keepdims=True)
            lses.append(mx + jnp.log(den))
            ps.append((e * (1.0 / den)).astype(BF16))
        outs = [_dot(p, vv_all[keys(b)]) for (b, h), p in zip(chains, ps)]
        for b in range(nblk):
            mine = [n for n, ch in enumerate(chains) if ch[0] == b]
            o, lse_all = outs[mine[0]], lses[mine[0]]
            for n in mine[1:]:
                o = jnp.where(head_id == chains[n][1], outs[n], o)
                lse_all = jnp.where(head_id == chains[n][1], lses[n], lse_all)
            o_ref[rows(b), :] = o
            lse_ref[rows(b), :] = jnp.broadcast_to(lse_all, o.shape)

    out_spec = pl.BlockSpec((nblk * BLOCK, DIL_OUT_WIDTH), lambda r, i: (i, r))
    o, lse = _pcall(
        body,
        name=f"dil_fwd_g{group}",
        grid=(dilation, nb // nblk),
        in_specs=_dil_step_specs(ncb, cols, nblk, lambda i: i),
        out_specs=[out_spec, out_spec],
        out_shape=[jax.ShapeDtypeStruct((sub, dilation * DIL_OUT_WIDTH), F32)] * 2,
        compiler_params=_cparams(2),
    )(qkv_v, qkv_v, qkv_v, qkv_v, qkv_v)
    return o, lse


def _dil_bwd(view, do_g, lse_g, dterm_g, group, rider=None):
    window, dilation = DIL_GROUPS[group]
    qkv_v, ncb, cols = view
    sub = qkv_v.shape[0]
    nb = sub // BLOCK
    nblk = min(DIL_STEP_BLOCKS, nb)
    n_steps = nb // nblk
    slopes = [_alibi_slope(group * DIL_HEADS_PER_GROUP + h) * dilation for h in range(DIL_HEADS_PER_GROUP)]
    scale = 1.0 / math.sqrt(HEAD_DIM)
    tail = slice((nblk - 1) * BLOCK, nblk * BLOCK)

    def body(q_ref, kc_ref, kp_ref, vc_ref, vp_ref, do_ref, lse_ref, dt_ref, dq_ref, dk_ref, dv_ref, ck_ref, cv_ref):
        i = pl.program_id(1)

        @pl.when(i == 0)
        def _():
            ck_ref[...] = jnp.zeros_like(ck_ref)
            cv_ref[...] = jnp.zeros_like(cv_ref)

        @pl.when(i < n_steps)
        def _():
            kk_all = jnp.concatenate([kp_ref[...], kc_ref[...]], axis=0)
            vv_all = jnp.concatenate([vp_ref[...], vc_ref[...]], axis=0)
            lane = lax.broadcasted_iota(jnp.int32, (1, DIL_OUT_WIDTH), 1)
            head_id = lane // HEAD_DIM
            chains = [(b, h) for b in range(nblk) for h in range(DIL_HEADS_PER_GROUP)]
            rows = lambda b: slice(b * BLOCK, (b + 1) * BLOCK)
            keys = lambda b: slice(b * BLOCK, (b + 2) * BLOCK)
            bands = [_dil_band(i == 0 if b == 0 else False) for b in range(nblk)]
            qms, doms = [], []
            for b, h in chains:
                q, do = q_ref[rows(b), :], do_ref[rows(b), :]
                qms.append(jnp.where(head_id == h, q, jnp.zeros_like(q)))
                doms.append(jnp.where(head_id == h, do, jnp.zeros_like(do)))
            scores = [_dot_nt(qm, kk_all[keys(b)]) for (b, h), qm in zip(chains, qms)]
            dps = [_dot_nt(dom, vv_all[keys(b)]) for (b, h), dom in zip(chains, doms)]
            pbs, dss = [], []
            for n, (b, h) in enumerate(chains):
                steps, valid = bands[b]
                first = lane == h * HEAD_DIM
                lse = jnp.sum(jnp.where(first, lse_ref[rows(b), :], 0.0), axis=1, keepdims=True)
                dt = jnp.sum(jnp.where(first, dt_ref[rows(b), :], 0.0), axis=1, keepdims=True)
                logits = jnp.where(valid, scores[n] * scale - slopes[h] * steps, NEG_INF)
                p = jnp.where(valid, jnp.exp(logits - lse), 0.0)
                pbs.append(p.astype(BF16))
                dss.append((p * (dps[n] + dt) * scale).astype(BF16))
            dqs = [_dot(ds, kk_all[keys(b)]) for (b, h), ds in zip(chains, dss)]
            dks = [_dot_tn(ds, qm) for ds, qm in zip(dss, qms)]
            dvs = [_dot_tn(pb, dom) for pb, dom in zip(pbs, doms)]
            dkk, dvv = [], []
            for b in range(nblk):
                mine = [n for n, ch in enumerate(chains) if ch[0] == b]
                dq = dqs[mine[0]]
                for n in mine[1:]:
                    dq = jnp.where(head_id == chains[n][1], dqs[n], dq)
                dq_ref[rows(b), :] = dq.astype(dq_ref.dtype)
                dkk.append((dks[mine[0]] + dks[mine[1]]) + (dks[mine[2]] + dks[mine[3]]))
                dvv.append((dvs[mine[0]] + dvs[mine[1]]) + (dvs[mine[2]] + dvs[mine[3]]))
            for out_ref, carry_ref, parts in ((dk_ref, ck_ref, dkk), (dv_ref, cv_ref, dvv)):
                if nblk > 1:
                    out_ref[: (nblk - 1) * BLOCK, :] = carry_ref[: (nblk - 1) * BLOCK, :].astype(out_ref.dtype)
                out_ref[tail, :] = (carry_ref[tail, :] + parts[0][:BLOCK]).astype(out_ref.dtype)
                for b in range(nblk):
                    own = parts[b][BLOCK:]
                    carry_ref[rows(b), :] = own + parts[b + 1][:BLOCK] if b + 1 < nblk else own

        @pl.when(i == n_steps)
        def _():
            dk_ref[...] = ck_ref[...].astype(dk_ref.dtype)
            dv_ref[...] = cv_ref[...].astype(dv_ref.dtype)

    clamp = lambda i: jnp.minimum(i, n_steps - 1)
    row_spec = pl.BlockSpec((nblk * BLOCK, DIL_OUT_WIDTH), lambda r, i: (clamp(i), r))
    late_spec = pl.BlockSpec((nblk * BLOCK, DIL_OUT_WIDTH), lambda r, i: (jnp.maximum(i - 1, 0), r))
    res = _call(
        body, (qkv_v, qkv_v, qkv_v, qkv_v, qkv_v, do_g, lse_g, dterm_g), rider,
        name=f"dil_bwd_g{group}",
        grid=(dilation, n_steps + 1),
        in_specs=_dil_step_specs(ncb, cols, nblk, clamp) + [row_spec, row_spec, row_spec],
        out_specs=[row_spec, late_spec, late_spec],
        out_shape=[jax.ShapeDtypeStruct((sub, dilation * DIL_OUT_WIDTH), BF16)] * 3,
        scratch_shapes=[pltpu.VMEM((nblk * BLOCK, DIL_OUT_WIDTH), F32)] * 2,
        compiler_params=_cparams(2),
    )
    grads, lands = res if rider is not None else (res, None)
    if dilation > 1:
        grads = _dil_relayout(f"dil_bwd_rows_g{group}", list(grads), dilation, to_view=False)
    return tuple(grads) if rider is None else (tuple(grads), lands)


def _dil_view(qkv, group):
    _, dilation = DIL_GROUPS[group]
    w = DIL_OUT_WIDTH
    if dilation == 1:
        return qkv, QKV_COLS // w, (3 * group, 3 * group + 1, 3 * group + 2)
    (own,) = _dil_relayout(f"dil_view_g{group}", [qkv], dilation, to_view=True, col_block=group, width=3 * w)
    return own, 3, (0, 1, 2)


def _group_major(w_qkv):
    w = DIL_OUT_WIDTH
    ng = len(DIL_GROUPS)
    cols = [w_qkv[:, (part * ng + g) * w:(part * ng + g + 1) * w] for g in range(ng) for part in range(3)]
    return jnp.concatenate(cols + [w_qkv[:, 3 * DIL_WIDTH:]], axis=1)


def _head_block_ones():
    r = lax.broadcasted_iota(jnp.int32, (DIL_OUT_WIDTH, DIL_OUT_WIDTH), 0) // HEAD_DIM
    c = lax.broadcasted_iota(jnp.int32, (DIL_OUT_WIDTH, DIL_OUT_WIDTH), 1) // HEAD_DIM
    return jnp.where(r == c, 1.0, 0.0).astype(BF16)


def _dil_mix_weights(l0, l1, l2):
    mx = jnp.maximum(jnp.maximum(l0, l1), l2)
    e0, e1, e2 = jnp.exp(l0 - mx), jnp.exp(l1 - mx), jnp.exp(l2 - mx)
    inv = 1.0 / (e0 + e1 + e2)
    return e0 * inv, e1 * inv, e2 * inv


def _dil_view_spec(dilation):
    return pl.BlockSpec((DIL_RELAYOUT_ROWS // dilation, dilation * DIL_OUT_WIDTH), lambda i: (i, 0))


def _dil_mix_call(name, body, s, ins, in_specs, outs, n_relaid):
    out_specs = [_dil_view_spec(d or 1) for d, _ in outs]
    out_shape = [jax.ShapeDtypeStruct((s // (d or 1), (d or 1) * DIL_OUT_WIDTH), dt) for d, dt in outs]
    return _pcall(
        body,
        name=name,
        grid=(s // DIL_RELAYOUT_ROWS,),
        in_specs=in_specs,
        out_specs=out_specs,
        out_shape=out_shape,
        scratch_shapes=[_view_scratch(DIL_OUT_WIDTH)] * n_relaid,
        compiler_params=_cparams(1),
    )(*ins)


DIL_MIX_CHUNK = 64
_DIL_SLABS = DIL_OUT_WIDTH // LANES


def _dil_rows(refs, scratch):
    dils = [d for _, d in DIL_GROUPS]
    assert dils[0] == 1
    readers = [lambda rows, j, ref=refs[0]: ref[rows, j * LANES:(j + 1) * LANES]]
    for ref, scr, d in zip(refs[1:], scratch, dils[1:]):
        _rows_from_view(ref, scr, d, DIL_OUT_WIDTH)
        readers.append(lambda rows, j, scr=scr: scr[j, rows, :])
    return readers


def _dil_mix_chunks(step):
    def chunk(c, carry):
        step(pl.ds(pl.multiple_of(c * DIL_MIX_CHUNK, DIL_MIX_CHUNK), DIL_MIX_CHUNK))
        return carry

    lax.fori_loop(0, DIL_RELAYOUT_ROWS // DIL_MIX_CHUNK, chunk, 0, unroll=4)


def _dil_mix_fwd(os_, lses):
    ng = len(DIL_GROUPS)
    s = os_[0].shape[0]

    def body(*refs):
        o_refs, l_refs, out_ref, scratch = refs[:ng], refs[ng:2 * ng], refs[2 * ng], refs[2 * ng + 1:]
        o_at = _dil_rows(o_refs, scratch[:ng - 1])
        l_at = _dil_rows(l_refs, scratch[ng - 1:])

        def step(rows):
            for j in range(_DIL_SLABS):
                w0, w1, w2 = _dil_mix_weights(*[at(rows, j) for at in l_at])
                o0, o1, o2 = [at(rows, j) for at in o_at]
                out_ref[rows, j * LANES:(j + 1) * LANES] = (w0 * o0 + w1 * o1 + w2 * o2).astype(out_ref.dtype)

        _dil_mix_chunks(step)

    specs = [_dil_view_spec(d) for _, d in DIL_GROUPS]
    (o_a,) = _dil_mix_call("dil_mix_fwd", body, s, list(os_) + list(lses), specs * 2, [(None, BF16)], 2 * (ng - 1))
    return o_a


def _dil_mix_bwd(do_a, os_, lses):
    ng = len(DIL_GROUPS)
    s = do_a.shape[0]
    dils = [d for _, d in DIL_GROUPS]

    def body(*refs):
        do_ref, o_refs, l_refs = refs[0], refs[1:1 + ng], refs[1 + ng:1 + 2 * ng]
        out_refs, scratch = refs[1 + 2 * ng:1 + 4 * ng], refs[1 + 4 * ng:]
        o_at = _dil_rows(o_refs, scratch[:ng - 1])
        l_at = _dil_rows(l_refs, scratch[ng - 1:2 * (ng - 1)])
        spare = iter(scratch[2 * (ng - 1):])
        staged = [None if dils[n % ng] == 1 else next(spare) for n in range(2 * ng)]
        ones = _head_block_ones()

        def step(rows):
            do = do_ref[rows, :].astype(F32)
            ws, prods = [], []
            for j in range(_DIL_SLABS):
                w0, w1, w2 = _dil_mix_weights(*[at(rows, j) for at in l_at])
                o0, o1, o2 = [at(rows, j) for at in o_at]
                ws.append((w0, w1, w2))
                prods.append(do[:, j * LANES:(j + 1) * LANES] * (w0 * o0 + w1 * o1 + w2 * o2))
            tot = _dot_hi_lo(jnp.concatenate(prods, axis=1), ones)
            for j in range(_DIL_SLABS):
                slab = slice(j * LANES, (j + 1) * LANES)
                vals = [w * do[:, slab] for w in ws[j]] + [-w * tot[:, slab] for w in ws[j]]
                for val, dst, scr in zip(vals, out_refs, staged):
                    if scr is None:
                        dst[rows, slab] = val.astype(dst.dtype)
                    else:
                        scr[j, rows, :] = val

        _dil_mix_chunks(step)
        for n, (dst, scr) in enumerate(zip(out_refs, staged)):
            if scr is not None:
                _rows_to_view(scr, dst, dils[n % ng], DIL_OUT_WIDTH)

    specs = [_dil_view_spec(d) for d in dils]
    return _dil_mix_call(
        "dil_mix_bwd", body, s, [do_a] + list(os_) + list(lses), [_dil_view_spec(1)] + specs * 2,
        [(d, BF16) for d in dils] + [(d, F32) for d in dils], 4 * (ng - 1))


_SB_Q0 = 3 * DIL_WIDTH // LANES
_SB_K0 = _SB_Q0 + SB_WIDTH // LANES
_SB_V0 = _SB_K0 + SB_WIDTH // LANES


_EXP_CLAMP = 88.0
_SB_DEAD = 104.0


def _tri(t, op):
    r = lax.broadcasted_iota(jnp.int32, (t, t), 0)
    c = lax.broadcasted_iota(jnp.int32, (t, t), 1)
    return jnp.where(op(r, c), 1.0, 0.0).astype(BF16)


def _softplus(z):
    return jnp.maximum(z, jnp.log(1.0 + jnp.exp(jnp.minimum(z, _EXP_CLAMP))))


def _sb_chain_head(qm, kj, mask):
    z = _dot_nt(qm, kj)
    sp = _softplus(z)
    return (sp if mask is None else jnp.where(mask, sp, 0.0)), z - sp


def _sb_fwd(qkv, rider=None):
    s = qkv.shape[0]
    t = SB_TK
    assert s % (2 * t) == 0
    nq = s // (2 * t)
    n_pairs = SB_WIDTH // LANES

    def body(q_ref, k_ref, v_ref, o_ref, tot_ref, steps_ref):
        p, i = pl.program_id(0), pl.program_id(1)
        lane_hi = lax.broadcasted_iota(jnp.int32, (1, LANES), 1) // HEAD_DIM
        later = _tri(t, lambda r, c: r > c)
        causal = lax.broadcasted_iota(jnp.int32, (t, t), 1) < lax.broadcasted_iota(jnp.int32, (t, t), 0)
        qms = []
        for x in range(2):
            q = q_ref[pl.ds(x * t, t), :] * (1.0 / math.sqrt(HEAD_DIM))
            qms.append([jnp.where(lane_hi == hh, q, jnp.zeros_like(q)) for hh in range(2)])

        def tile(j):
            off = pl.multiple_of(j * t, t)
            return k_ref[pl.ds(off, t), :], v_ref[pl.ds(off, t), :]

        def step(groups, carry):
            kv = [tile(j) for _, j, _ in groups]
            chains = [(g, x, hh) for g, (x, _, _) in enumerate(groups) for hh in range(2)]
            heads = [_sb_chain_head(qms[x][hh], kv[g][0], causal if groups[g][2] else None) for g, x, hh in chains]
            sufs = [_dot(sp.astype(BF16), later) for sp, _ in heads]
            cur = [list(carry[0]), list(carry[1])]
            for (g, x, hh), (sp, lpos), suf in zip(chains, heads, sufs):
                c, acc = cur[x][hh]
                a = jnp.exp(lpos - suf - c)
                if groups[g][2]:
                    a = jnp.where(causal, a, 0.0)
                cur[x][hh] = (c + jnp.sum(sp, axis=1, keepdims=True), acc + _dot(a.astype(BF16), kv[g][1]))
            return (tuple(cur[0]), tuple(cur[1]))

        def lowest(carry):
            return jnp.min(jnp.minimum(jnp.minimum(carry[0][0][0], carry[0][1][0]),
                                       jnp.minimum(carry[1][0][0], carry[1][1][0])))

        zero = (jnp.zeros((t, 1), F32), jnp.zeros((t, LANES), F32))
        start = ((zero, zero), (zero, zero))
        carry = lax.cond(
            i == 0,
            lambda ca: step([(0, 0, True), (1, 1, True), (1, 0, False)], ca),
            lambda ca: step([(0, 2 * i, True), (1, 2 * i + 1, True), (0, 2 * i - 1, False), (1, 2 * i, False)], ca),
            start)

        def walk(state):
            n, ca, _ = state
            ca = step([(0, 2 * i - 2 - n, False), (1, 2 * i - 1 - n, False)], ca)
            return n + 1, ca, lowest(ca)

        n_more, carry, low = lax.while_loop(
            lambda st: jnp.logical_and(st[0] + 1 < 2 * i, st[2] <= _SB_DEAD), walk, (jnp.int32(0), carry, lowest(carry)))
        b_last = jnp.logical_and(jnp.logical_and(i > 0, n_more + 1 == 2 * i), low <= _SB_DEAD)
        carry = lax.cond(b_last, lambda ca: step([(1, 0, False)], ca), lambda ca: ca, carry)
        for x in range(2):
            (c0, acc0), (c1, acc1) = carry[x]
            o_ref[pl.ds(x * t, t), :] = jnp.where(lane_hi == 0, acc0, acc1).astype(o_ref.dtype)
            tot_ref[pl.ds(x * t, t), :] = jnp.where(lane_hi == 0, c0, c1)
        steps_ref[p, i] = 1 + n_more + b_last.astype(jnp.int32)

    return _call(
        body, (qkv, qkv, qkv), rider,
        name="sb_fwd",
        grid=(n_pairs, nq),
        in_specs=[
            pl.BlockSpec((2 * t, LANES), lambda p, i: (i, _SB_Q0 + p)),
            pl.BlockSpec((s, LANES), lambda p, i: (0, _SB_K0 + p)),
            pl.BlockSpec((s, LANES), lambda p, i: (0, _SB_V0 + p)),
        ],
        out_specs=[pl.BlockSpec((2 * t, LANES), lambda p, i: (i, p))] * 2 + [pl.BlockSpec(memory_space=pltpu.SMEM)],
        out_shape=[jax.ShapeDtypeStruct((s, SB_WIDTH), BF16), jax.ShapeDtypeStruct((s, SB_WIDTH), F32),
                   jax.ShapeDtypeStruct((n_pairs, nq), jnp.int32)],
        compiler_params=_cparams(2),
    )


def _sb_bwd(qkv, do_b, tot_b, n_steps):
    s = qkv.shape[0]
    t = SB_TK
    nq = s // (2 * t)
    n_pairs = SB_WIDTH // LANES
    scale = 1.0 / math.sqrt(HEAD_DIM)

    def body(steps_ref, q_ref, k_ref, v_ref, do_ref, tot_ref, dq_ref, dk_ref, dv_ref):
        p, i = pl.program_id(0), pl.program_id(1)

        @pl.when(i == 0)
        def _():
            dk_ref[...] = jnp.zeros_like(dk_ref)
            dv_ref[...] = jnp.zeros_like(dv_ref)

        lane = lax.broadcasted_iota(jnp.int32, (1, LANES), 1)
        lane_hi = lane // HEAD_DIM
        later = _tri(t, lambda r, c: r > c)
        before = _tri(t, lambda r, c: r < c)
        causal = lax.broadcasted_iota(jnp.int32, (t, t), 1) < lax.broadcasted_iota(jnp.int32, (t, t), 0)
        qms, doms, tots = [], [], []
        for x in range(2):
            rows = pl.ds(x * t, t)
            q, do, tot_all = q_ref[rows, :] * scale, do_ref[rows, :], tot_ref[rows, :]
            qms.append([jnp.where(lane_hi == hh, q, jnp.zeros_like(q)) for hh in range(2)])
            doms.append([jnp.where(lane_hi == hh, do, jnp.zeros_like(do)) for hh in range(2)])
            tots.append([jnp.sum(jnp.where(lane == hh * HEAD_DIM, tot_all, 0.0), axis=1, keepdims=True)
                         for hh in range(2)])

        def step(groups, carry):
            offs = [pl.multiple_of(j * t, t) for _, j, _ in groups]
            ks = [k_ref[pl.ds(off, t), :] for off in offs]
            vs = [v_ref[pl.ds(off, t), :] for off in offs]
            chains = [(g, x, hh) for g, (x, _, _) in enumerate(groups) for hh in range(2)]
            heads = [_sb_chain_head(qms[x][hh], ks[g], causal if groups[g][2] else None) for g, x, hh in chains]
            sufs = [_dot(sp.astype(BF16), later) for sp, _ in heads]
            das = [_dot_nt(doms[x][hh], vs[g]) for g, x, hh in chains]
            cur = [list(carry[0]), list(carry[1])]
            sigs, gs, abs_, cg_before = [], [], [], []
            for (g_, x, hh), (sp, lpos), suf, da in zip(chains, heads, sufs, das):
                cl, cg, dq = cur[x][hh]
                cl = cl + jnp.sum(sp, axis=1, keepdims=True)
                sig = jnp.exp(lpos)
                a = sig * jnp.exp(-suf - (tots[x][hh] - cl))
                if groups[g_][2]:
                    a = jnp.where(causal, a, 0.0)
                g = a * da
                sigs.append(sig)
                gs.append(g)
                abs_.append(a.astype(BF16))
                cg_before.append(cg)
                cur[x][hh] = (cl, cg + jnp.sum(g, axis=1, keepdims=True), dq)
            prefs = [_dot(g.astype(BF16), before) for g in gs]
            dvs = [_dot_tn(ab, doms[x][hh]) for (_, x, hh), ab in zip(chains, abs_)]
            dzs = []
            for (g_, x, hh), sig, g, pref, cg in zip(chains, sigs, gs, prefs, cg_before):
                dz = g - sig * (g + pref + cg)
                if groups[g_][2]:
                    dz = jnp.where(causal, dz, 0.0)
                dzs.append(dz.astype(BF16))
            dqs = [_dot(dz, ks[g_]) for (g_, x, hh), dz in zip(chains, dzs)]
            dks = [_dot_tn(dz, qms[x][hh]) for (_, x, hh), dz in zip(chains, dzs)]
            for n, (_, x, hh) in enumerate(chains):
                cl, cg, dq = cur[x][hh]
                cur[x][hh] = (cl, cg, dq + dqs[n])
            for g_, off in enumerate(offs):
                dk_ref[pl.ds(off, t), :] += dks[2 * g_] + dks[2 * g_ + 1]
                dv_ref[pl.ds(off, t), :] += dvs[2 * g_] + dvs[2 * g_ + 1]
            return (tuple(cur[0]), tuple(cur[1]))

        taken = steps_ref[p, i]
        n_full = jnp.minimum(taken, 2 * i)
        zero = (jnp.zeros((t, 1), F32), jnp.zeros((t, 1), F32), jnp.zeros((t, LANES), F32))
        carry = ((zero, zero), (zero, zero))
        carry = lax.cond(jnp.logical_and(i > 0, taken > 2 * i), lambda ca: step([(1, 0, False)], ca), lambda ca: ca,
                         carry)
        carry = lax.fori_loop(
            0, n_full - 1,
            lambda n, ca: step([(0, 2 * i - n_full + n, False), (1, 2 * i + 1 - n_full + n, False)], ca), carry)
        carry = lax.cond(
            i == 0,
            lambda ca: step([(1, 0, False), (0, 0, True), (1, 1, True)], ca),
            lambda ca: step([(0, 2 * i - 1, False), (1, 2 * i, False), (0, 2 * i, True), (1, 2 * i + 1, True)], ca),
            carry)
        for x in range(2):
            dq = jnp.where(lane_hi == 0, carry[x][0][2], carry[x][1][2])
            dq_ref[pl.ds(x * t, t), :] = (dq * scale).astype(dq_ref.dtype)

    row_spec = pl.BlockSpec((2 * t, LANES), lambda p, i, ns: (i, p))
    full_spec = pl.BlockSpec((s, LANES), lambda p, i, ns: (0, p))
    return _pcall(
        body,
        name="sb_bwd",
        grid_spec=pltpu.PrefetchScalarGridSpec(
            num_scalar_prefetch=1,
            grid=(n_pairs, nq),
            in_specs=[
                pl.BlockSpec((2 * t, LANES), lambda p, i, ns: (i, _SB_Q0 + p)),
                pl.BlockSpec((s, LANES), lambda p, i, ns: (0, _SB_K0 + p)),
                pl.BlockSpec((s, LANES), lambda p, i, ns: (0, _SB_V0 + p)),
                row_spec, row_spec,
            ],
            out_specs=[row_spec, full_spec, full_spec],
        ),
        out_shape=[jax.ShapeDtypeStruct((s, SB_WIDTH), BF16), jax.ShapeDtypeStruct((s, SB_WIDTH), F32),
                   jax.ShapeDtypeStruct((s, SB_WIDTH), F32)],
        compiler_params=_cparams(2),
    )(n_steps, qkv, qkv, qkv, do_b, tot_b)


def _gates(gl, bg):
    return _sigmoid(gl[:, :D_MODEL] + bg[:, :D_MODEL]), _sigmoid(gl[:, D_MODEL:] + bg[:, D_MODEL:])


def _mixer_fwd(o_a, o_b, gl, x0, bg, g2, w_ud, w_us, w_out, tm):
    def epi(_, rows, consts):
        oa, ob, glv, x = rows
        bgv, g2v, wud, wus, wout = consts
        ga, gb = _gates(glv, bgv)
        merged = ga * _dot(oa, wud) + gb * _dot(ob, wus)
        x1 = x + _dot(merged.astype(BF16), wout)
        r, xh = _rms_stats(x1)
        return [x1, xh * g2v], []

    return _rowk("mixer_fwd", tm=tm, rows=[o_a, o_b, gl, x0], consts=[bg, g2, w_ud, w_us, w_out],
                 row_outs=[(D_MODEL, F32), (D_MODEL, BF16)], epilogue=epi)


def _mixer_bwd(dx1, o_a, o_b, gl, bg, w_ud, w_us, w_out, tm, rider=None):
    s = dx1.shape[0]
    nm = s // tm

    def body(dx_ref, oa_ref, ob_ref, gl_ref, bg_ref, wud_ref, wus_ref, wout_ref,
             doa_ref, dob_ref, dgl_ref, gwout_ref, gwud_ref, gwus_ref, gbg_ref, awout_ref, awud_ref, awus_ref):
        i = pl.program_id(0)
        dxb = dx_ref[...].astype(BF16)
        oa, ob = oa_ref[...], ob_ref[...]
        ga, gb = _gates(gl_ref[...], bg_ref[...])
        ua, ub = _dot(oa, wud_ref[...]), _dot(ob, wus_ref[...])
        merged = (ga * ua + gb * ub).astype(BF16)
        dm = _dot_nt(dxb, wout_ref[...])
        dua = (dm * ga).astype(BF16)
        dub = (dm * gb).astype(BF16)
        dgla = dm * ua * ga * (1.0 - ga)
        dglb = dm * ub * gb * (1.0 - gb)
        doa_ref[...] = _dot_nt(dua, wud_ref[...]).astype(doa_ref.dtype)
        dob_ref[...] = _dot_nt(dub, wus_ref[...]).astype(dob_ref.dtype)
        dgl_ref[:, :D_MODEL] = dgla.astype(dgl_ref.dtype)
        dgl_ref[:, D_MODEL:] = dglb.astype(dgl_ref.dtype)
        parts = [(gwout_ref, awout_ref, _dot_tn(merged, dxb)), (gwud_ref, awud_ref, _dot_tn(oa, dua)),
                 (gwus_ref, awus_ref, _dot_tn(ob, dub))]
        for out, r, v in parts:

            @pl.when(i == 0)
            def _(r=r, v=v):
                r[...] = v

            @pl.when(i > 0)
            def _(r=r, v=v):
                r[...] += v

            @pl.when(i == nm - 1)
            def _(out=out, r=r):
                out[...] = r[...].astype(out.dtype)

        sa = jnp.sum(dgla, axis=0, keepdims=True)
        sb = jnp.sum(dglb, axis=0, keepdims=True)

        @pl.when(i == 0)
        def _():
            gbg_ref[:, :D_MODEL] = sa
            gbg_ref[:, D_MODEL:] = sb

        @pl.when(i > 0)
        def _():
            gbg_ref[:, :D_MODEL] += sa
            gbg_ref[:, D_MODEL:] += sb

    row = lambda w: pl.BlockSpec((tm, w), lambda i: (i, 0))
    full = lambda a: pl.BlockSpec(a.shape, lambda i: (0, 0), pipeline_mode=pl.Buffered(1))
    wshape = lambda r, c: jax.ShapeDtypeStruct((r, c), BF16)
    return _call(
        body, (dx1, o_a, o_b, gl, bg, w_ud, w_us, w_out), rider,
        name="mixer_bwd",
        grid=(nm,),
        in_specs=[row(D_MODEL), row(DIL_OUT_WIDTH), row(SB_WIDTH), row(2 * D_MODEL),
                  full(bg), full(w_ud), full(w_us), full(w_out)],
        out_specs=[row(DIL_OUT_WIDTH), row(SB_WIDTH), row(2 * D_MODEL),
                   pl.BlockSpec((D_MODEL, D_MODEL), lambda i: (0, 0)),
                   pl.BlockSpec((DIL_OUT_WIDTH, D_MODEL), lambda i: (0, 0)),
                   pl.BlockSpec((SB_WIDTH, D_MODEL), lambda i: (0, 0)),
                   pl.BlockSpec((1, 2 * D_MODEL), lambda i: (0, 0))],
        out_shape=[jax.ShapeDtypeStruct((s, DIL_OUT_WIDTH), BF16), jax.ShapeDtypeStruct((s, SB_WIDTH), BF16),
                   jax.ShapeDtypeStruct((s, 2 * D_MODEL), BF16),
                   wshape(D_MODEL, D_MODEL), wshape(DIL_OUT_WIDTH, D_MODEL), wshape(SB_WIDTH, D_MODEL),
                   jax.ShapeDtypeStruct((1, 2 * D_MODEL), F32)],
        scratch_shapes=[pltpu.VMEM((D_MODEL, D_MODEL), F32), pltpu.VMEM((DIL_OUT_WIDTH, D_MODEL), F32),
                        pltpu.VMEM((SB_WIDTH, D_MODEL), F32)],
        compiler_params=_cparams(1),
    )


def _all_gather(shards):
    n = len(shards)

    def body(*refs):
        x_refs, out_refs = refs[:n], refs[n:2 * n]
        send_sems, recv_sems, local_sems = refs[2 * n:]
        x, y, c = lax.axis_index("x"), lax.axis_index("y"), lax.axis_index("c")
        me, sibling = (x, y, c), (x, y, 1 - c)
        chips = [(1 - x, y), (x, 1 - y), (1 - x, 1 - y)]

        def slot(a, px, py, pc):
            return out_refs[a].at[4 * px + 2 * py + pc]

        def copy(a, k, block, to, own=False):
            return pltpu.make_async_remote_copy(
                src_ref=x_refs[a] if own else slot(a, *block), dst_ref=slot(a, *block),
                send_sem=send_sems.at[7 * a + k], recv_sem=recv_sems.at[7 * a + k], device_id=to, device_id_type=_MESH)

        mine = [pltpu.make_async_copy(x_refs[a], slot(a, *me), local_sems.at[a]) for a in range(n)]
        for cp in mine:
            cp.start()
        first = []
        for a in range(n):
            first.append(copy(a, 0, me, sibling, own=True))
            first += [copy(a, 1 + j, me, (*chip, c), own=True) for j, chip in enumerate(chips)]
        for cp in first:
            cp.start()
        passed = []
        for a in range(n):
            for j, chip in enumerate(chips):
                copy(a, 1 + j, (*chip, c), me).wait_recv()
                passed.append(copy(a, 4 + j, (*chip, c), sibling))
                passed[-1].start()
        for a in range(n):
            copy(a, 0, sibling, me).wait_recv()
            for j, chip in enumerate(chips):
                copy(a, 4 + j, (*chip, 1 - c), me).wait_recv()
        for cp in first + passed:
            cp.wait_send()
        for cp in mine:
            cp.wait()

    return _pcall(
        body,
        name="all_gather_weights",
        in_specs=[_HBM] * n,
        out_specs=[_HBM] * n,
        out_shape=[jax.ShapeDtypeStruct((N_DEV,) + s.shape, s.dtype) for s in shards],
        scratch_shapes=[pltpu.SemaphoreType.DMA((7 * n,)), pltpu.SemaphoreType.DMA((7 * n,)),
                        pltpu.SemaphoreType.DMA((n,))],
    )(*shards)


def _exchange(chunks):
    n = len(chunks)

    def body(*refs):
        g_refs, o_refs = refs[:n], refs[n:2 * n]
        send_sems, recv_sems, local_sems = refs[2 * n:]
        x, y, c = lax.axis_index("x"), lax.axis_index("y"), lax.axis_index("c")
        me = 4 * x + 2 * y + c
        own = [pltpu.make_async_copy(g_refs[a].at[me], o_refs[a].at[me], local_sems.at[a]) for a in range(n)]
        for cp in own:
            cp.start()
        copies = []
        for a in range(n):
            for k in range(1, N_DEV):
                px, py, pc = x ^ (k >> 2), y ^ ((k >> 1) & 1), c ^ (k & 1)
                peer = 4 * px + 2 * py + pc
                copies.append(pltpu.make_async_remote_copy(
                    src_ref=g_refs[a].at[peer], dst_ref=o_refs[a].at[me], send_sem=send_sems.at[7 * a + k - 1],
                    recv_sem=recv_sems.at[7 * a + k - 1], device_id=(px, py, pc), device_id_type=_MESH))
        for cp in copies:
            cp.start()
        for cp in copies:
            cp.wait()
        for cp in own:
            cp.wait()

    return _pcall(
        body,
        name="exchange_grads",
        in_specs=[_HBM] * n,
        out_specs=[_HBM] * n,
        out_shape=[jax.ShapeDtypeStruct(g.shape, g.dtype) for g in chunks],
        scratch_shapes=[pltpu.SemaphoreType.DMA((7 * n,)), pltpu.SemaphoreType.DMA((7 * n,)),
                        pltpu.SemaphoreType.DMA((n,))],
    )(*chunks)


def _reduce_adamw(name, parts, w, m, v, tr):
    _, rows, cols = parts.shape
    tr = min(tr, rows)
    assert rows % tr == 0
    c1 = 1.0 / (1.0 - ADAM_B1 ** ADAM_STEP)
    c2 = 1.0 / (1.0 - ADAM_B2 ** ADAM_STEP)

    def body(p_ref, w_ref, m_ref, v_ref, g_out, d_out, m_out, v_out):
        g = p_ref[0].astype(F32)
        for d in range(1, N_DEV):
            g = g + p_ref[d].astype(F32)
        mn = ADAM_B1 * m_ref[...] + (1.0 - ADAM_B1) * g
        vn = ADAM_B2 * v_ref[...] + (1.0 - ADAM_B2) * (g * g)
        g_out[...] = g
        m_out[...] = mn
        v_out[...] = vn
        d_out[...] = -ADAM_LR * ((mn * c1) / (jnp.sqrt(vn * c2) + ADAM_EPS) + ADAM_WD * w_ref[...])

    spec = pl.BlockSpec((tr, cols), lambda i: (i, 0))
    return _pcall(
        body,
        name=name,
        grid=(rows // tr,),
        in_specs=[pl.BlockSpec((N_DEV, tr, cols), lambda i: (0, i, 0)), spec, spec, spec],
        out_specs=[spec] * 4,
        out_shape=[jax.ShapeDtypeStruct((rows, cols), F32)] * 4,
        compiler_params=_cparams(1),
    )(parts, w, m, v)


_SHARDED = ("w_in", "w_up_dil", "w_up_sb", "w_out", "w_mlp_in", "w_mlp_out")
_FULL_SHAPES = {"w_in": (D_MODEL, IN_COLS), "w_up_dil": (DIL_OUT_WIDTH, D_MODEL), "w_up_sb": (SB_WIDTH, D_MODEL),
                "w_out": (D_MODEL, D_MODEL), "w_mlp_in": (D_MODEL, D_FF), "w_mlp_out": (D_FF, D_MODEL)}
_ROW_SHARDED = ("w_out", "w_mlp_out")


def _shard_shape(name):
    r, c = _FULL_SHAPES[name]
    return (r // N_DEV, c) if name in _ROW_SHARDED else (r, c // N_DEV)


def _assemble(name, gathered):
    r, c = _shard_shape(name)
    if name in _ROW_SHARDED:
        return gathered.reshape(N_DEV * r, c)
    return gathered.transpose(1, 0, 2).reshape(r, N_DEV * c)


def _chunk(name, full):
    r, c = _shard_shape(name)
    if name in _ROW_SHARDED:
        return full.reshape(N_DEV, r, c)
    return full.reshape(r, N_DEV, c).transpose(1, 0, 2)


_SMALL = (("norm_mix_g", D_MODEL), ("b_gate", 2 * D_MODEL), ("norm_mlp_g", D_MODEL), ("norm_final_g", D_MODEL))
_SMALL_N = sum(n for _, n in _SMALL) + LANES


def _pack_small(vals, tail):
    return jnp.concatenate([vals[n].reshape(1, -1) for n, _ in _SMALL] + [tail], axis=1)


def _unpack_small(vec, shapes):
    out, pos = {}, 0
    for n, width in _SMALL:
        out[n] = vec[:, pos:pos + width].reshape(shapes[n])
        pos += width
    return out, vec[:, pos:]


def kernel(x, norm_mix_g, w_in, b_gate, w_up_dil, w_up_sb, w_out, norm_mlp_g, w_mlp_in, w_mlp_out, norm_final_g, loss_target, m_norm_mix_g, m_w_in, m_b_gate, m_w_up_dil, m_w_up_sb, m_w_out, m_norm_mlp_g, m_w_mlp_in, m_w_mlp_out, m_norm_final_g, v_norm_mix_g, v_w_in, v_b_gate, v_w_up_dil, v_w_up_sb, v_w_out, v_norm_mlp_g, v_w_mlp_in, v_w_mlp_out, v_norm_final_g):
    given = dict(locals())
    s = x.shape[1]
    x0 = x.reshape(s, D_MODEL)
    target = loss_target.reshape(s, D_MODEL)
    g1 = norm_mix_g.reshape(1, D_MODEL)
    g2 = norm_mlp_g.reshape(1, D_MODEL)
    g3 = norm_final_g.reshape(1, D_MODEL)
    bg = b_gate.reshape(1, 2 * D_MODEL)
    w_shards = {n: given[n].reshape(_shard_shape(n)) for n in _SHARDED}
    m_shards = {n: given["m_" + n].reshape(_shard_shape(n)) for n in _SHARDED}
    v_shards = {n: given["v_" + n].reshape(_shard_shape(n)) for n in _SHARDED}

    shard_b = {n: w_shards[n].astype(BF16) for n in _SHARDED}
    (gathered_w_in,) = _all_gather([shard_b["w_in"]])
    w_in_f = _assemble("w_in", gathered_w_in)
    w_qkv, w_gl = _group_major(w_in_f[:, :QKV_COLS]), w_in_f[:, QKV_COLS:]
    full = {}

    def norm1(_, rows, consts):
        _, xh = _rms_stats(rows[0])
        return [xh * consts[0]], []

    (h1,) = _rowk("norm_mix", tm=1024, rows=[x0], consts=[g1], row_outs=[(D_MODEL, BF16)], epilogue=norm1)
    qkv, (land,) = _mm("proj_qkv", h1, w_qkv, out_dtype=BF16, tm=1024, tn=768, tk=D_MODEL,
                       rider=_Spread([shard_b["w_mlp_in"]], chunked=False))
    full["w_mlp_in"] = _assemble("w_mlp_in", land)
    gl = _mm("proj_gates", h1, w_gl, out_dtype=BF16, tm=1024, tn=1024, tk=D_MODEL)
    views = [_dil_view(qkv, g) for g in range(len(DIL_GROUPS))]
    dil = [_dil_fwd(views[g], g) for g in range(len(DIL_GROUPS))]
    os_, lses = [d[0] for d in dil], [d[1] for d in dil]
    o_a = _dil_mix_fwd(os_, lses)
    riding = ("w_mlp_out", "w_out", "w_up_sb", "w_up_dil")
    (o_b, tot_b, sb_steps), lands = _sb_fwd(qkv, rider=_Spread([shard_b[n] for n in riding], chunked=False))
    full.update({n: _assemble(n, land) for n, land in zip(riding, lands)})
    x1, h2 = _mixer_fwd(o_a, o_b, gl, x0, bg, g2, full["w_up_dil"], full["w_up_sb"], full["w_out"], 512)
    f = _mm("mlp_in", h2, full["w_mlp_in"], out_dtype=BF16, tm=1024, tn=1024, tk=D_MODEL,
            epilogue=lambda r, _: jnp.square(jnp.maximum(r, 0.0)))

    def head(acc, rows, consts):
        x1v, tv = rows
        g3v = consts[0]
        x2 = x1v + acc
        r, xh = _rms_stats(x2)
        diff = xh * g3v - tv
        loss = (0.5 / D_MODEL) * jnp.sum(jnp.sum(diff * diff, axis=0, keepdims=True), axis=1, keepdims=True)
        dy = diff * (1.0 / D_MODEL)
        dx2, dg = _rms_bwd(dy, xh, r, g3v)
        return [dx2, dx2], [dg, jnp.broadcast_to(loss, (1, LANES))]

    dx2, dx2b, gg3, loss_part = _rowk(
        "mlp_out_loss", a=f, w=full["w_mlp_out"], tm=512, tk=D_FF, rows=[x1, target], consts=[g3],
        row_outs=[(D_MODEL, F32), (D_MODEL, BF16)], acc_outs=[D_MODEL, LANES], epilogue=head)

    da = _mm("mlp_out_bwd", dx2b, full["w_mlp_out"], tb=True, out_dtype=BF16, tm=1024, tn=1024, tk=D_MODEL, extra=f,
             epilogue=lambda r, fv: r * (2.0 * jnp.sqrt(fv.astype(F32))))
    g_w_mlp_out = _mm("grad_w_mlp_out", f, dx2b, ta=True, out_dtype=BF16, tm=1024, tn=1024, tk=2048)
    g_w_mlp_in = _mm("grad_w_mlp_in", h2, da, ta=True, out_dtype=BF16, tm=1024, tn=1024, tk=2048)

    def norm_bwd(acc, rows, consts):
        xv, dres = rows
        r, xh = _rms_stats(xv)
        dx, dg = _rms_bwd(acc, xh, r, consts[0])
        return [dres + dx], [dg]

    bchunk = lambda n, g: _chunk(n, g).astype(BF16)
    parts = {}
    (dx1, gg2), (parts["w_mlp_in"],) = _rowk(
        "mlp_in_bwd", a=da, w=full["w_mlp_in"], nt=True, tm=512, tk=D_FF, rows=[x1, dx2], consts=[g2],
        row_outs=[(D_MODEL, F32)], acc_outs=[D_MODEL], epilogue=norm_bwd,
        rider=_Spread([bchunk("w_mlp_in", g_w_mlp_in)], chunked=True))
    (do_a, do_b, dgl, g_w_out, g_w_ud, g_w_us, g_bg), (parts["w_mlp_out"],) = _mixer_bwd(
        dx1, o_a, o_b, gl, bg, full["w_up_dil"], full["w_up_sb"], full["w_out"], 512,
        rider=_Spread([bchunk("w_mlp_out", g_w_mlp_out)], chunked=True))
    mix = _dil_mix_bwd(do_a, os_, lses)
    small_three = {"w_out": g_w_out, "w_up_sb": g_w_us, "w_up_dil": g_w_ud}
    grads, lands = _dil_bwd(views[0], mix[0], lses[0], mix[3], 0,
                            rider=_Spread([bchunk(n, g) for n, g in small_three.items()], chunked=True))
    parts.update(dict(zip(small_three, lands)))
    dil_b = [grads] + [_dil_bwd(views[g], mix[g], lses[g], mix[3 + g], g) for g in (1, 2)]
    dq_b, dk_b, dv_b = _sb_bwd(qkv, do_b, tot_b, sb_steps)
    dproj = [d[0] for d in dil_b] + [d[1] for d in dil_b] + [d[2] for d in dil_b] + [dq_b, dk_b, dv_b, dgl]
    g_w_in = jnp.concatenate([
        _grad_cols("grad_w_in_dil", h1, dproj[:9], tm=D_MODEL, tk=1024),
        _grad_cols("grad_w_in_sb", h1, dproj[9:12], tm=D_MODEL, tk=1024),
        _grad_cols("grad_w_in_gates", h1, dproj[12:], tm=D_MODEL, tk=1024)], axis=1)
    (grad_x, gg1), (parts["w_in"],) = _rowk(
        "in_proj_bwd", a=dproj, w=w_in_f, nt=True, tm=512, tk=IN_COLS, rows=[x0, dx1], consts=[g1],
        row_outs=[(D_MODEL, F32)], acc_outs=[D_MODEL], epilogue=norm_bwd,
        rider=_Spread([bchunk("w_in", g_w_in)], chunked=True))

    small_part = _pack_small({"norm_mix_g": gg1, "b_gate": g_bg, "norm_mlp_g": gg2, "norm_final_g": gg3}, loss_part)
    (small_parts,) = _exchange([jnp.broadcast_to(small_part[None], (N_DEV, 1, _SMALL_N))])

    tags = ("grad_", "delta_", "new_m_", "new_v_")
    outs = {}
    for n, p in parts.items():
        res = _reduce_adamw("adamw_" + n, p, w_shards[n], m_shards[n], v_shards[n], 256)
        for tag, val in zip(tags, res):
            outs[tag + n] = val.reshape(given[n].shape)
    small_w = _pack_small(given, jnp.zeros((1, LANES), F32))
    small_m = _pack_small({n: given["m_" + n] for n, _ in _SMALL}, jnp.zeros((1, LANES), F32))
    small_v = _pack_small({n: given["v_" + n] for n, _ in _SMALL}, jnp.ones((1, LANES), F32))
    small_res = _reduce_adamw("adamw_replicated", small_parts, small_w, small_m, small_v, 8)

    small_shapes = {n: given[n].shape for n, _ in _SMALL}
    for tag, small in zip(tags, small_res):
        small_vals, tail = _unpack_small(small, small_shapes)
        for n, val in small_vals.items():
            outs[tag + n] = val
        if tag == "grad_":
            loss = tail[0, 0]
    names = ["norm_mix_g", "w_in", "b_gate", "w_up_dil", "w_up_sb", "w_out", "norm_mlp_g", "w_mlp_in", "w_mlp_out",
             "norm_final_g"]
    return (loss, grad_x.reshape(x.shape), *[outs["grad_" + n] for n in names], *[outs["delta_" + n] for n in names],
            *[outs["new_m_" + n] for n in names], *[outs["new_v_" + n] for n in names])
```

```python
import functools
import math

import jax
import jax.numpy as jnp
from jax import lax
from jax.experimental import pallas as pl
from jax.experimental.pallas import tpu as pltpu

_pcall = pl.pallas_call

F32 = jnp.float32
BF16 = jnp.bfloat16

D_MODEL = 1024
HEAD_DIM = 64
DIL_GROUPS = ((128, 1), (512, 4), (2048, 16))
DIL_HEADS_PER_GROUP = 4
N_DIL_HEADS = 12
N_SB_HEADS = 8
DIL_WIDTH = 768
DIL_OUT_WIDTH = 256
SB_WIDTH = 512
D_FF = 4096
BLOCK = 128
RMS_EPS = 1e-6
NEG_INF = -1e30
QKV_COLS = 3 * DIL_WIDTH + 3 * SB_WIDTH
IN_COLS = QKV_COLS + 2 * D_MODEL
N_DEV = 8

ADAM_LR = 0.001
ADAM_B1 = 0.9
ADAM_B2 = 0.999
ADAM_EPS = 1e-08
ADAM_WD = 0.01
ADAM_STEP = 10

VMEM_LIMIT = 56 * 1024 * 1024
SB_TK = 256
LANES = 128

_ARB = pltpu.ARBITRARY


def _cparams(n_axes, **kw):
    return pltpu.CompilerParams(dimension_semantics=(_ARB,) * n_axes, vmem_limit_bytes=VMEM_LIMIT, **kw)


def _dot(a, b):
    return jnp.dot(a, b, preferred_element_type=F32)


def _dot_nt(a, b):
    return lax.dot_general(a, b, (((1,), (1,)), ((), ())), preferred_element_type=F32)


def _dot_tn(a, b):
    return lax.dot_general(a, b, (((0,), (0,)), ((), ())), preferred_element_type=F32)


def _split_hi_lo(x):
    hi = x.astype(BF16)
    lo = (x - hi.astype(F32)).astype(BF16)
    return hi, lo


def _dot_hi_lo(x, m):
    hi, lo = _split_hi_lo(x)
    return _dot(hi, m) + _dot(lo, m)


def _sigmoid(x):
    return 1.0 / (1.0 + jnp.exp(-x))


_HBM = pl.BlockSpec(memory_space=pltpu.HBM)
_MESH = pl.DeviceIdType.MESH


class _Spread:
    def __init__(self, srcs, chunked):
        self.srcs, self.chunked, self.n = list(srcs), chunked, len(srcs)

    def land_shapes(self):
        return [jax.ShapeDtypeStruct((N_DEV,) + (s.shape[1:] if self.chunked else s.shape), s.dtype) for s in self.srcs]

    def scratch(self):
        dma = pltpu.SemaphoreType.DMA
        return [dma((7 * self.n,)), dma((7 * self.n,)), dma((self.n,))]

    def copies(self, src_refs, land_refs, send_sems, recv_sems, local_sems):
        x, y, c = lax.axis_index("x"), lax.axis_index("y"), lax.axis_index("c")
        me = 4 * x + 2 * y + c
        out = []
        for a, (src, land) in enumerate(zip(src_refs, land_refs)):
            out.append(pltpu.make_async_copy(src.at[me] if self.chunked else src, land.at[me], local_sems.at[a]))
            for k in range(1, N_DEV):
                px, py, pc = x ^ (k >> 2), y ^ ((k >> 1) & 1), c ^ (k & 1)
                out.append(pltpu.make_async_remote_copy(
                    src_ref=src.at[4 * px + 2 * py + pc] if self.chunked else src, dst_ref=land.at[me],
                    send_sem=send_sems.at[7 * a + k - 1], recv_sem=recv_sems.at[7 * a + k - 1],
                    device_id=(px, py, pc), device_id_type=_MESH))
        return out


def _call(body, args, rider=None, **kw):
    if rider is None:
        return _pcall(body, **kw)(*args)
    grid = kw["grid"]
    single = not isinstance(kw["out_shape"], (list, tuple))
    out_specs = [kw["out_specs"]] if single else list(kw["out_specs"])
    out_shape = [kw["out_shape"]] if single else list(kw["out_shape"])
    in_specs, scratch = list(kw["in_specs"]), list(kw.get("scratch_shapes", []))
    n_in, n_out, n_s, n = len(in_specs), len(out_shape), len(scratch), rider.n

    def hosted(*refs):
        ins, srcs = refs[:n_in], refs[n_in:n_in + n]
        outs, lands = refs[n_in + n:n_in + n + n_out], refs[n_in + n + n_out:n_in + 2 * n + n_out]
        own_scratch, sems = refs[n_in + 2 * n + n_out:n_in + 2 * n + n_out + n_s], refs[n_in + 2 * n + n_out + n_s:]
        ids = [pl.program_id(d) for d in range(len(grid))]
        first = functools.reduce(jnp.logical_and, [i == 0 for i in ids])
        last = functools.reduce(jnp.logical_and, [i == g - 1 for i, g in zip(ids, grid)])
        copies = rider.copies(srcs, lands, *sems)

        @pl.when(first)
        def _():
            for cp in copies:
                cp.start()

        body(*ins, *outs, *own_scratch)

        @pl.when(last)
        def _():
            for cp in copies:
                cp.wait()

    kw = dict(kw, in_specs=in_specs + [_HBM] * n, out_specs=out_specs + [_HBM] * n,
              out_shape=out_shape + rider.land_shapes(), scratch_shapes=scratch + rider.scratch())
    res = _pcall(hosted, **kw)(*args, *rider.srcs)
    return (res[0] if single else list(res[:n_out])), list(res[n_out:])


def _mm(name, a, b, *, ta=False, tb=False, out_dtype, tm, tn, tk, epilogue=None, extra=None, rider=None):
    m = a.shape[1] if ta else a.shape[0]
    k = a.shape[0] if ta else a.shape[1]
    n = b.shape[0] if tb else b.shape[1]
    assert (b.shape[1] if tb else b.shape[0]) == k
    tm, tn, tk = min(tm, m), min(tn, n), min(tk, k)
    assert m % tm == 0 and n % tn == 0 and k % tk == 0, (name, m, n, k, tm, tn, tk)
    nk = k // tk
    dn = (((0 if ta else 1,), (1 if tb else 0,)), ((), ()))
    in_place = nk > 1 and epilogue is None and out_dtype == F32

    def body(*refs):
        if extra is not None:
            a_ref, b_ref, e_ref, o_ref = refs[:4]
        else:
            a_ref, b_ref, o_ref = refs[:3]
            e_ref = None

        def finish(r):
            if epilogue is not None:
                r = epilogue(r, None if e_ref is None else e_ref[...])
            o_ref[...] = r.astype(out_dtype)

        part = lax.dot_general(a_ref[...].astype(BF16), b_ref[...].astype(BF16), dn, preferred_element_type=F32)
        if nk == 1:
            finish(part)
        else:
            acc_ref = o_ref if in_place else refs[-1]
            kk = pl.program_id(2)

            @pl.when(kk == 0)
            def _():
                acc_ref[...] = part

            @pl.when(kk > 0)
            def _():
                acc_ref[...] += part

            if not in_place:

                @pl.when(kk == nk - 1)
                def _():
                    finish(acc_ref[...])

    a_spec = pl.BlockSpec((tk, tm), lambda j, i, kk: (kk, i)) if ta else pl.BlockSpec((tm, tk), lambda j, i, kk: (i, kk))
    b_spec = pl.BlockSpec((tn, tk), lambda j, i, kk: (j, kk)) if tb else pl.BlockSpec((tk, tn), lambda j, i, kk: (kk, j))
    o_spec = pl.BlockSpec((tm, tn), lambda j, i, kk: (i, j))
    in_specs = [a_spec, b_spec]
    args = [a, b]
    if extra is not None:
        in_specs.append(o_spec)
        args.append(extra)
    return _call(
        body, args, rider,
        name=name,
        grid=(n // tn, m // tm, nk),
        in_specs=in_specs,
        out_specs=o_spec,
        out_shape=jax.ShapeDtypeStruct((m, n), out_dtype),
        scratch_shapes=[pltpu.VMEM((tm, tn), F32)] if (nk > 1 and not in_place) else [],
        compiler_params=_cparams(3),
    )


def _grad_cols(name, a, parts, *, tm, tk, rider=None):
    k, m = a.shape
    n = sum(p.shape[1] for p in parts)
    assert m % tm == 0 and k % tk == 0
    nk = k // tk

    def body(*refs):
        a_ref, p_refs, o_ref, acc_ref = refs[0], refs[1:1 + len(parts)], refs[1 + len(parts)], refs[2 + len(parts)]
        kk = pl.program_id(1)
        side_by_side = jnp.concatenate([p_ref[...].astype(BF16) for p_ref in p_refs], axis=1)
        term = _dot_tn(a_ref[...].astype(BF16), side_by_side)

        @pl.when(kk == 0)
        def _():
            acc_ref[...] = term

        @pl.when(kk > 0)
        def _():
            acc_ref[...] += term

        @pl.when(kk == nk - 1)
        def _():
            o_ref[...] = acc_ref[...].astype(o_ref.dtype)

    return _call(
        body, [a] + list(parts), rider,
        name=name,
        grid=(m // tm, nk),
        in_specs=[pl.BlockSpec((tk, tm), lambda i, kk: (kk, i))]
        + [pl.BlockSpec((tk, p.shape[1]), lambda i, kk: (kk, 0)) for p in parts],
        out_specs=pl.BlockSpec((tm, n), lambda i, kk: (i, 0)),
        out_shape=jax.ShapeDtypeStruct((m, n), BF16),
        scratch_shapes=[pltpu.VMEM((tm, n), F32)],
        compiler_params=_cparams(2),
    )


def _rowk(name, *, a=None, w=None, nt=False, tm, tk=None, rows=(), consts=(), row_outs=(), acc_outs=(), epilogue,
          rider=None):
    has_mm = a is not None
    a_parts = list(a) if isinstance(a, (list, tuple)) else ([a] if has_mm else [])
    n_a = len(a_parts)
    m = a_parts[0].shape[0] if has_mm else rows[0].shape[0]
    assert m % tm == 0
    nm = m // tm
    if has_mm:
        k = sum(p.shape[1] for p in a_parts)
        n = w.shape[0] if nt else w.shape[1]
        tk = min(tk, k)
        assert k % tk == 0 and (n_a == 1 or tk == k)
        nk = k // tk
    else:
        nk = 1
    n_rows, n_consts, n_ro, n_ao = len(rows), len(consts), len(row_outs), len(acc_outs)

    def body(*refs):
        pos = 0
        if has_mm:
            a_refs, w_ref = refs[:n_a], refs[n_a]
            pos = n_a + 1
        row_refs = refs[pos:pos + n_rows]
        pos += n_rows
        const_refs = refs[pos:pos + n_consts]
        pos += n_consts
        ro_refs = refs[pos:pos + n_ro]
        pos += n_ro
        ao_refs = refs[pos:pos + n_ao]
        pos += n_ao
        i = pl.program_id(0)
        kk = pl.program_id(1)

        def finish(acc):
            ro_vals, ao_vals = epilogue(acc, [r[...] for r in row_refs], [c[...] for c in const_refs])
            for r, v in zip(ro_refs, ro_vals):
                r[...] = v.astype(r.dtype)
            for r, v in zip(ao_refs, ao_vals):

                @pl.when(i == 0)
                def _(r=r, v=v):
                    r[...] = v

                @pl.when(i > 0)
                def _(r=r, v=v):
                    r[...] += v

        if not has_mm:
            finish(None)
            return
        part, off = None, 0
        for a_ref in a_refs:
            width = a_ref.shape[1]
            cols = slice(None) if n_a == 1 else slice(off, off + width)
            av = a_ref[...].astype(BF16)
            term = _dot_nt(av, w_ref[:, cols]) if nt else _dot(av, w_ref[cols, :])
            part = term if part is None else part + term
            off += width
        if nk == 1:
            finish(part)
        else:
            acc_ref = refs[pos]

            @pl.when(kk == 0)
            def _():
                acc_ref[...] = part

            @pl.when(kk > 0)
            def _():
                acc_ref[...] += part

            @pl.when(kk == nk - 1)
            def _():
                finish(acc_ref[...])

    once = pl.Buffered(1)
    in_specs, args = [], []
    if has_mm:
        for part in a_parts:
            in_specs.append(pl.BlockSpec((tm, tk if n_a == 1 else part.shape[1]), lambda i, kk: (i, kk)))
        w_mode = once if nk == 1 else None
        in_specs.append(pl.BlockSpec((n, tk), lambda i, kk: (0, kk), pipeline_mode=w_mode) if nt
                        else pl.BlockSpec((tk, n), lambda i, kk: (kk, 0), pipeline_mode=w_mode))
        args += a_parts + [w]
    for r in rows:
        in_specs.append(pl.BlockSpec((tm, r.shape[1]), lambda i, kk: (i, 0)))
        args.append(r)
    for c in consts:
        in_specs.append(pl.BlockSpec(c.shape, lambda i, kk: (0,) * c.ndim, pipeline_mode=once))
        args.append(c)
    out_specs, out_shape = [], []
    for width, dt in row_outs:
        out_specs.append(pl.BlockSpec((tm, width), lambda i, kk: (i, 0)))
        out_shape.append(jax.ShapeDtypeStruct((m, width), dt))
    for width in acc_outs:
        out_specs.append(pl.BlockSpec((1, width), lambda i, kk: (0, 0)))
        out_shape.append(jax.ShapeDtypeStruct((1, width), F32))
    return _call(
        body, args, rider,
        name=name,
        grid=(nm, nk),
        in_specs=in_specs,
        out_specs=out_specs,
        out_shape=out_shape,
        scratch_shapes=[pltpu.VMEM((tm, n), F32)] if (has_mm and nk > 1) else [],
        compiler_params=_cparams(2),
    )


def _rms_stats(x):
    r = lax.rsqrt(jnp.mean(x * x, axis=-1, keepdims=True) + RMS_EPS)
    return r, x * r


def _rms_bwd(dh, xh, r, g):
    gy = dh * g
    dx = r * (gy - xh * jnp.mean(gy * xh, axis=-1, keepdims=True))
    return dx, jnp.sum(dh * xh, axis=0, keepdims=True)


def _alibi_slope(head):
    return 2.0 ** (-8.0 * (head + 1) / N_DIL_HEADS)


DIL_STEP_BLOCKS = 4


def _dil_band(first_block):
    qi = lax.broadcasted_iota(jnp.int32, (BLOCK, 2 * BLOCK), 0)
    kj = lax.broadcasted_iota(jnp.int32, (BLOCK, 2 * BLOCK), 1)
    steps = qi + BLOCK - kj
    valid = (steps >= 0) & (steps <= BLOCK)
    if first_block is not False:
        valid = valid & ((kj >= BLOCK) | jnp.logical_not(first_block))
    return steps.astype(F32), valid


def _dil_step_specs(ncb, cols, nblk, clamp):
    def own(col):
        return pl.BlockSpec((nblk * BLOCK, DIL_OUT_WIDTH), lambda r, i: (clamp(i), r * ncb + col))

    def before(col):
        return pl.BlockSpec((BLOCK, DIL_OUT_WIDTH), lambda r, i: (jnp.maximum(clamp(i) * nblk - 1, 0), r * ncb + col))

    return [own(cols[0]), own(cols[1]), before(cols[1]), own(cols[2]), before(cols[2])]


DIL_RELAYOUT_ROWS = 1024


def _view_scratch(width):
    return pltpu.VMEM((width // LANES, DIL_RELAYOUT_ROWS, LANES), F32)


def _rows_from_view(src, scr, d, w):
    sub = src.shape[0]
    for j in range(w // LANES):
        for r in range(d):
            scr[j, pl.ds(r, sub, stride=d), :] = src[:, r * w + j * LANES:r * w + (j + 1) * LANES].astype(F32)


def _rows_to_view(scr, dst, d, w):
    sub = dst.shape[0]
    for j in range(w // LANES):
        for r in range(d):
            dst[:, r * w + j * LANES:r * w + (j + 1) * LANES] = scr[j, pl.ds(r, sub, stride=d), :].astype(dst.dtype)


def _dil_relayout(name, xs, dilation, to_view, col_block=0, width=None):
    d = dilation
    tm = DIL_RELAYOUT_ROWS
    rows = tm // d
    if to_view:
        s = xs[0].shape[0]
        widths = [width or x.shape[1] for x in xs]
    else:
        s = xs[0].shape[0] * d
        widths = [v.shape[1] // d for v in xs]
    assert s % tm == 0 and all(w % LANES == 0 for w in widths)
    n = len(xs)

    def body(*refs):
        in_refs, out_refs, scratch = refs[:n], refs[n:2 * n], refs[2 * n:]
        for src, dst, scr, w in zip(in_refs, out_refs, scratch, widths):
            slabs = [slice(j * LANES, (j + 1) * LANES) for j in range(w // LANES)]
            if to_view:
                for j, slab in enumerate(slabs):
                    scr[j] = src[:, slab].astype(F32)
                _rows_to_view(scr, dst, d, w)
            else:
                _rows_from_view(src, scr, d, w)
                for j, slab in enumerate(slabs):
                    dst[:, slab] = scr[j].astype(dst.dtype)

    natural = [pl.BlockSpec((tm, w), lambda i: (i, col_block)) for w in widths]
    viewed = [pl.BlockSpec((rows, d * w), lambda i: (i, 0)) for w in widths]
    return _pcall(
        body,
        name=name,
        grid=(s // tm,),
        in_specs=natural if to_view else viewed,
        out_specs=viewed if to_view else natural,
        out_shape=[jax.ShapeDtypeStruct((s // d, d * w) if to_view else (s, w), x.dtype) for x, w in zip(xs, widths)],
        scratch_shapes=[_view_scratch(w) for w in widths],
        compiler_params=_cparams(1),
    )(*xs)


def _dil_fwd(view, group):
    window, dilation = DIL_GROUPS[group]
    qkv_v, ncb, cols = view
    sub = qkv_v.shape[0]
    s = sub * dilation
    nb = sub // BLOCK
    assert nb * BLOCK * dilation == s and window // dilation == BLOCK
    nblk = min(DIL_STEP_BLOCKS, nb)
    assert nb % nblk == 0
    slopes = [_alibi_slope(group * DIL_HEADS_PER_GROUP + h) * dilation for h in range(DIL_HEADS_PER_GROUP)]

    def body(q_ref, kc_ref, kp_ref, vc_ref, vp_ref, o_ref, lse_ref):
        i = pl.program_id(1)
        kk_all = jnp.concatenate([kp_ref[...], kc_ref[...]], axis=0)
        vv_all = jnp.concatenate([vp_ref[...], vc_ref[...]], axis=0)
        head_id = lax.broadcasted_iota(jnp.int32, (1, DIL_OUT_WIDTH), 1) // HEAD_DIM
        chains = [(b, h) for b in range(nblk) for h in range(DIL_HEADS_PER_GROUP)]
        rows = lambda b: slice(b * BLOCK, (b + 1) * BLOCK)
        keys = lambda b: slice(b * BLOCK, (b + 2) * BLOCK)
        bands = [_dil_band(i == 0 if b == 0 else False) for b in range(nblk)]
        qs = [q_ref[rows(b), :] for b in range(nblk)]
        scores = [_dot_nt(jnp.where(head_id == h, qs[b], jnp.zeros_like(qs[b])), kk_all[keys(b)]) for b, h in chains]
        ps, lses = [], []
        for (b, h), sc in zip(chains, scores):
            steps, valid = bands[b]
            logits = jnp.where(valid, sc * (1.0 / math.sqrt(HEAD_DIM)) - slopes[h] * steps, NEG_INF)
            mx = jnp.max(logits, axis=1, keepdims=True)
            e = jnp.exp(logits - mx)
            den = jnp.sum(e, axis=1, keepdims=True)
            lses.append(mx + jnp.log(den))
            ps.append((e * (1.0 / den)).astype(BF16))
        outs = [_dot(p, vv_all[keys(b)]) for (b, h), p in zip(chains, ps)]
        for b in range(nblk):
            mine = [n for n, ch in enumerate(chains) if ch[0] == b]
            o, lse_all = outs[mine[0]], lses[mine[0]]
            for n in mine[1:]:
                o = jnp.where(head_id == chains[n][1], outs[n], o)
                lse_all = jnp.where(head_id == chains[n][1], lses[n], lse_all)
            o_ref[rows(b), :] = o
            lse_ref[rows(b), :] = jnp.broadcast_to(lse_all, o.shape)

    out_spec = pl.BlockSpec((nblk * BLOCK, DIL_OUT_WIDTH), lambda r, i: (i, r))
    o, lse = _pcall(
        body,
        name=f"dil_fwd_g{group}",
        grid=(dilation, nb // nblk),
        in_specs=_dil_step_specs(ncb, cols, nblk, lambda i: i),
        out_specs=[out_spec, out_spec],
        out_shape=[jax.ShapeDtypeStruct((sub, dilation * DIL_OUT_WIDTH), F32)] * 2,
        compiler_params=_cparams(2),
    )(qkv_v, qkv_v, qkv_v, qkv_v, qkv_v)
    return o, lse


def _dil_bwd(view, do_g, lse_g, dterm_g, group, rider=None):
    window, dilation = DIL_GROUPS[group]
    qkv_v, ncb, cols = view
    sub = qkv_v.shape[0]
    nb = sub // BLOCK
    nblk = min(DIL_STEP_BLOCKS, nb)
    n_steps = nb // nblk
    slopes = [_alibi_slope(group * DIL_HEADS_PER_GROUP + h) * dilation for h in range(DIL_HEADS_PER_GROUP)]
    scale = 1.0 / math.sqrt(HEAD_DIM)
    tail = slice((nblk - 1) * BLOCK, nblk * BLOCK)
    single = n_steps == 1

    def body(q_ref, kc_ref, kp_ref, vc_ref, vp_ref, do_ref, lse_ref, dt_ref, dq_ref, dk_ref, dv_ref, *carry_refs):
        i = pl.program_id(1)

        def init():
            for carry_ref in carry_refs:
                carry_ref[...] = jnp.zeros_like(carry_ref)

        def compute():
            kk_all = jnp.concatenate([kp_ref[...], kc_ref[...]], axis=0)
            vv_all = jnp.concatenate([vp_ref[...], vc_ref[...]], axis=0)
            lane = lax.broadcasted_iota(jnp.int32, (1, DIL_OUT_WIDTH), 1)
            head_id = lane // HEAD_DIM
            chains = [(b, h) for b in range(nblk) for h in range(DIL_HEADS_PER_GROUP)]
            rows = lambda b: slice(b * BLOCK, (b + 1) * BLOCK)
            keys = lambda b: slice(b * BLOCK, (b + 2) * BLOCK)
            bands = [_dil_band(i == 0 if b == 0 else False) for b in range(nblk)]
            qms, doms = [], []
            for b, h in chains:
                q, do = q_ref[rows(b), :], do_ref[rows(b), :]
                qms.append(jnp.where(head_id == h, q, jnp.zeros_like(q)))
                doms.append(jnp.where(head_id == h, do, jnp.zeros_like(do)))
            scores = [_dot_nt(qm, kk_all[keys(b)]) for (b, h), qm in zip(chains, qms)]
            dps = [_dot_nt(dom, vv_all[keys(b)]) for (b, h), dom in zip(chains, doms)]
            pbs, dss = [], []
            for n, (b, h) in enumerate(chains):
                steps, valid = bands[b]
                first = lane == h * HEAD_DIM
                lse = jnp.sum(jnp.where(first, lse_ref[rows(b), :], 0.0), axis=1, keepdims=True)
                dt = jnp.sum(jnp.where(first, dt_ref[rows(b), :], 0.0), axis=1, keepdims=True)
                logits = jnp.where(valid, scores[n] * scale - slopes[h] * steps, NEG_INF)
                p = jnp.where(valid, jnp.exp(logits - lse), 0.0)
                pbs.append(p.astype(BF16))
                dss.append((p * (dps[n] + dt) * scale).astype(BF16))
            dqs = [_dot(ds, kk_all[keys(b)]) for (b, h), ds in zip(chains, dss)]
            dks = [_dot_tn(ds, qm) for ds, qm in zip(dss, qms)]
            dvs = [_dot_tn(pb, dom) for pb, dom in zip(pbs, doms)]
            dkk, dvv = [], []
            for b in range(nblk):
                mine = [n for n, ch in enumerate(chains) if ch[0] == b]
                dq = dqs[mine[0]]
                for n in mine[1:]:
                    dq = jnp.where(head_id == chains[n][1], dqs[n], dq)
                dq_ref[rows(b), :] = dq.astype(dq_ref.dtype)
                dkk.append((dks[mine[0]] + dks[mine[1]]) + (dks[mine[2]] + dks[mine[3]]))
                dvv.append((dvs[mine[0]] + dvs[mine[1]]) + (dvs[mine[2]] + dvs[mine[3]]))
            for n, (out_ref, parts) in enumerate(((dk_ref, dkk), (dv_ref, dvv))):
                done = [parts[b][BLOCK:] + parts[b + 1][:BLOCK] if b + 1 < nblk else parts[b][BLOCK:]
                        for b in range(nblk)]
                if single:
                    for b in range(nblk):
                        out_ref[rows(b), :] = done[b].astype(out_ref.dtype)
                    continue
                carry_ref = carry_refs[n]
                if nblk > 1:
                    out_ref[: (nblk - 1) * BLOCK, :] = carry_ref[: (nblk - 1) * BLOCK, :].astype(out_ref.dtype)
                out_ref[tail, :] = (carry_ref[tail, :] + parts[0][:BLOCK]).astype(out_ref.dtype)
                for b in range(nblk):
                    carry_ref[rows(b), :] = done[b]

        def flush():
            for out_ref, carry_ref in zip((dk_ref, dv_ref), carry_refs):
                out_ref[...] = carry_ref[...].astype(out_ref.dtype)

        if single:
            compute()
        else:
            pl.when(i == 0)(init)
            pl.when(i < n_steps)(compute)
            pl.when(i == n_steps)(flush)

    clamp = lambda i: jnp.minimum(i, n_steps - 1)
    row_spec = pl.BlockSpec((nblk * BLOCK, DIL_OUT_WIDTH), lambda r, i: (clamp(i), r))
    late_spec = pl.BlockSpec((nblk * BLOCK, DIL_OUT_WIDTH), lambda r, i: (jnp.maximum(i - 1, 0), r))
    res = _call(
        body, (qkv_v, qkv_v, qkv_v, qkv_v, qkv_v, do_g, lse_g, dterm_g), rider,
        name=f"dil_bwd_g{group}",
        grid=(dilation, n_steps + (0 if single else 1)),
        in_specs=_dil_step_specs(ncb, cols, nblk, clamp) + [row_spec, row_spec, row_spec],
        out_specs=[row_spec, row_spec, row_spec] if single else [row_spec, late_spec, late_spec],
        out_shape=[jax.ShapeDtypeStruct((sub, dilation * DIL_OUT_WIDTH), BF16)] * 3,
        scratch_shapes=[] if single else [pltpu.VMEM((nblk * BLOCK, DIL_OUT_WIDTH), F32)] * 2,
        compiler_params=_cparams(2),
    )
    grads, lands = res if rider is not None else (res, None)
    if dilation > 1:
        grads = _dil_relayout(f"dil_bwd_rows_g{group}", list(grads), dilation, to_view=False)
    return tuple(grads) if rider is None else (tuple(grads), lands)


def _dil_view(qkv, group):
    _, dilation = DIL_GROUPS[group]
    w = DIL_OUT_WIDTH
    if dilation == 1:
        return qkv, QKV_COLS // w, (3 * group, 3 * group + 1, 3 * group + 2)
    (own,) = _dil_relayout(f"dil_view_g{group}", [qkv], dilation, to_view=True, col_block=group, width=3 * w)
    return own, 3, (0, 1, 2)


def _group_major(w_qkv):
    w = DIL_OUT_WIDTH
    ng = len(DIL_GROUPS)
    cols = [w_qkv[:, (part * ng + g) * w:(part * ng + g + 1) * w] for g in range(ng) for part in range(3)]
    return jnp.concatenate(cols + [w_qkv[:, 3 * DIL_WIDTH:]], axis=1)


def _head_block_ones():
    r = lax.broadcasted_iota(jnp.int32, (DIL_OUT_WIDTH, DIL_OUT_WIDTH), 0) // HEAD_DIM
    c = lax.broadcasted_iota(jnp.int32, (DIL_OUT_WIDTH, DIL_OUT_WIDTH), 1) // HEAD_DIM
    return jnp.where(r == c, 1.0, 0.0).astype(BF16)


def _dil_mix_weights(l0, l1, l2):
    mx = jnp.maximum(jnp.maximum(l0, l1), l2)
    e0, e1, e2 = jnp.exp(l0 - mx), jnp.exp(l1 - mx), jnp.exp(l2 - mx)
    inv = 1.0 / (e0 + e1 + e2)
    return e0 * inv, e1 * inv, e2 * inv


def _dil_view_spec(dilation):
    return pl.BlockSpec((DIL_RELAYOUT_ROWS // dilation, dilation * DIL_OUT_WIDTH), lambda i: (i, 0))


def _dil_mix_call(name, body, s, ins, in_specs, outs, n_relaid):
    out_specs = [_dil_view_spec(d or 1) for d, _ in outs]
    out_shape = [jax.ShapeDtypeStruct((s // (d or 1), (d or 1) * DIL_OUT_WIDTH), dt) for d, dt in outs]
    return _pcall(
        body,
        name=name,
        grid=(s // DIL_RELAYOUT_ROWS,),
        in_specs=in_specs,
        out_specs=out_specs,
        out_shape=out_shape,
        scratch_shapes=[_view_scratch(DIL_OUT_WIDTH)] * n_relaid,
        compiler_params=_cparams(1),
    )(*ins)


DIL_MIX_CHUNK = 64
_DIL_SLABS = DIL_OUT_WIDTH // LANES


def _dil_rows(refs, scratch):
    dils = [d for _, d in DIL_GROUPS]
    assert dils[0] == 1
    readers = [lambda rows, j, ref=refs[0]: ref[rows, j * LANES:(j + 1) * LANES]]
    for ref, scr, d in zip(refs[1:], scratch, dils[1:]):
        _rows_from_view(ref, scr, d, DIL_OUT_WIDTH)
        readers.append(lambda rows, j, scr=scr: scr[j, rows, :])
    return readers


def _dil_mix_chunks(step):
    def chunk(c, carry):
        step(pl.ds(pl.multiple_of(c * DIL_MIX_CHUNK, DIL_MIX_CHUNK), DIL_MIX_CHUNK))
        return carry

    lax.fori_loop(0, DIL_RELAYOUT_ROWS // DIL_MIX_CHUNK, chunk, 0, unroll=4)


def _dil_mix_fwd(os_, lses):
    ng = len(DIL_GROUPS)
    s = os_[0].shape[0]

    def body(*refs):
        o_refs, l_refs, out_ref, scratch = refs[:ng], refs[ng:2 * ng], refs[2 * ng], refs[2 * ng + 1:]
        o_at = _dil_rows(o_refs, scratch[:ng - 1])
        l_at = _dil_rows(l_refs, scratch[ng - 1:])

        def step(rows):
            for j in range(_DIL_SLABS):
                w0, w1, w2 = _dil_mix_weights(*[at(rows, j) for at in l_at])
                o0, o1, o2 = [at(rows, j) for at in o_at]
                out_ref[rows, j * LANES:(j + 1) * LANES] = (w0 * o0 + w1 * o1 + w2 * o2).astype(out_ref.dtype)

        _dil_mix_chunks(step)

    specs = [_dil_view_spec(d) for _, d in DIL_GROUPS]
    (o_a,) = _dil_mix_call("dil_mix_fwd", body, s, list(os_) + list(lses), specs * 2, [(None, BF16)], 2 * (ng - 1))
    return o_a


def _dil_mix_bwd(do_a, os_, lses):
    ng = len(DIL_GROUPS)
    s = do_a.shape[0]
    dils = [d for _, d in DIL_GROUPS]

    def body(*refs):
        do_ref, o_refs, l_refs = refs[0], refs[1:1 + ng], refs[1 + ng:1 + 2 * ng]
        out_refs, scratch = refs[1 + 2 * ng:1 + 4 * ng], refs[1 + 4 * ng:]
        o_at = _dil_rows(o_refs, scratch[:ng - 1])
        l_at = _dil_rows(l_refs, scratch[ng - 1:2 * (ng - 1)])
        spare = iter(scratch[2 * (ng - 1):])
        staged = [None if dils[n % ng] == 1 else next(spare) for n in range(2 * ng)]
        ones = _head_block_ones()

        def step(rows):
            do = do_ref[rows, :].astype(F32)
            ws, prods = [], []
            for j in range(_DIL_SLABS):
                w0, w1, w2 = _dil_mix_weights(*[at(rows, j) for at in l_at])
                o0, o1, o2 = [at(rows, j) for at in o_at]
                ws.append((w0, w1, w2))
                prods.append(do[:, j * LANES:(j + 1) * LANES] * (w0 * o0 + w1 * o1 + w2 * o2))
            tot = _dot_hi_lo(jnp.concatenate(prods, axis=1), ones)
            for j in range(_DIL_SLABS):
                slab = slice(j * LANES, (j + 1) * LANES)
                vals = [w * do[:, slab] for w in ws[j]] + [-w * tot[:, slab] for w in ws[j]]
                for val, dst, scr in zip(vals, out_refs, staged):
                    if scr is None:
                        dst[rows, slab] = val.astype(dst.dtype)
                    else:
                        scr[j, rows, :] = val

        _dil_mix_chunks(step)
        for n, (dst, scr) in enumerate(zip(out_refs, staged)):
            if scr is not None:
                _rows_to_view(scr, dst, dils[n % ng], DIL_OUT_WIDTH)

    specs = [_dil_view_spec(d) for d in dils]
    return _dil_mix_call(
        "dil_mix_bwd", body, s, [do_a] + list(os_) + list(lses), [_dil_view_spec(1)] + specs * 2,
        [(d, BF16) for d in dils] + [(d, F32) for d in dils], 4 * (ng - 1))


_SB_Q0 = 3 * DIL_WIDTH // LANES
_SB_K0 = _SB_Q0 + SB_WIDTH // LANES
_SB_V0 = _SB_K0 + SB_WIDTH // LANES


_EXP_CLAMP = 88.0
_SB_DEAD = 104.0


def _tri(t, op):
    r = lax.broadcasted_iota(jnp.int32, (t, t), 0)
    c = lax.broadcasted_iota(jnp.int32, (t, t), 1)
    return jnp.where(op(r, c), 1.0, 0.0).astype(BF16)


def _softplus(z):
    return jnp.maximum(z, jnp.log(1.0 + jnp.exp(jnp.minimum(z, _EXP_CLAMP))))


def _sb_chain_head(qm, kj, mask):
    z = _dot_nt(qm, kj)
    sp = _softplus(z)
    return (sp if mask is None else jnp.where(mask, sp, 0.0)), z - sp


def _sb_fwd(qkv, rider=None):
    s = qkv.shape[0]
    t = SB_TK
    assert s % (2 * t) == 0
    nq = s // (2 * t)
    n_pairs = SB_WIDTH // LANES

    def body(q_ref, k_ref, v_ref, o_ref, tot_ref, steps_ref):
        p, i = pl.program_id(0), pl.program_id(1)
        lane_hi = lax.broadcasted_iota(jnp.int32, (1, LANES), 1) // HEAD_DIM
        later = _tri(t, lambda r, c: r > c)
        causal = lax.broadcasted_iota(jnp.int32, (t, t), 1) < lax.broadcasted_iota(jnp.int32, (t, t), 0)
        qms = []
        for x in range(2):
            q = q_ref[pl.ds(x * t, t), :] * (1.0 / math.sqrt(HEAD_DIM))
            qms.append([jnp.where(lane_hi == hh, q, jnp.zeros_like(q)) for hh in range(2)])

        def tile(j):
            off = pl.multiple_of(j * t, t)
            return k_ref[pl.ds(off, t), :], v_ref[pl.ds(off, t), :]

        def step(groups, carry):
            kv = [tile(j) for _, j, _ in groups]
            chains = [(g, x, hh) for g, (x, _, _) in enumerate(groups) for hh in range(2)]
            heads = [_sb_chain_head(qms[x][hh], kv[g][0], causal if groups[g][2] else None) for g, x, hh in chains]
            sufs = [_dot(sp.astype(BF16), later) for sp, _ in heads]
            cur = [list(carry[0]), list(carry[1])]
            for (g, x, hh), (sp, lpos), suf in zip(chains, heads, sufs):
                c, acc = cur[x][hh]
                a = jnp.exp(lpos - suf - c)
                if groups[g][2]:
                    a = jnp.where(causal, a, 0.0)
                cur[x][hh] = (c + jnp.sum(sp, axis=1, keepdims=True), acc + _dot(a.astype(BF16), kv[g][1]))
            return (tuple(cur[0]), tuple(cur[1]))

        def lowest(carry):
            return jnp.min(jnp.minimum(jnp.minimum(carry[0][0][0], carry[0][1][0]),
                                       jnp.minimum(carry[1][0][0], carry[1][1][0])))

        zero = (jnp.zeros((t, 1), F32), jnp.zeros((t, LANES), F32))
        start = ((zero, zero), (zero, zero))
        carry = lax.cond(
            i == 0,
            lambda ca: step([(0, 0, True), (1, 1, True), (1, 0, False)], ca),
            lambda ca: step([(0, 2 * i, True), (1, 2 * i + 1, True), (0, 2 * i - 1, False), (1, 2 * i, False)], ca),
            start)

        def walk(state):
            n, ca, _ = state
            ca = step([(0, 2 * i - 2 - n, False), (1, 2 * i - 1 - n, False)], ca)
            return n + 1, ca, lowest(ca)

        n_more, carry, low = lax.while_loop(
            lambda st: jnp.logical_and(st[0] + 1 < 2 * i, st[2] <= _SB_DEAD), walk, (jnp.int32(0), carry, lowest(carry)))
        b_last = jnp.logical_and(jnp.logical_and(i > 0, n_more + 1 == 2 * i), low <= _SB_DEAD)
        carry = lax.cond(b_last, lambda ca: step([(1, 0, False)], ca), lambda ca: ca, carry)
        for x in range(2):
            (c0, acc0), (c1, acc1) = carry[x]
            o_ref[pl.ds(x * t, t), :] = jnp.where(lane_hi == 0, acc0, acc1).astype(o_ref.dtype)
            tot_ref[pl.ds(x * t, t), :] = jnp.where(lane_hi == 0, c0, c1)
        steps_ref[p, i] = 1 + n_more + b_last.astype(jnp.int32)

    return _call(
        body, (qkv, qkv, qkv), rider,
        name="sb_fwd",
        grid=(n_pairs, nq),
        in_specs=[
            pl.BlockSpec((2 * t, LANES), lambda p, i: (i, _SB_Q0 + p)),
            pl.BlockSpec((s, LANES), lambda p, i: (0, _SB_K0 + p)),
            pl.BlockSpec((s, LANES), lambda p, i: (0, _SB_V0 + p)),
        ],
        out_specs=[pl.BlockSpec((2 * t, LANES), lambda p, i: (i, p))] * 2 + [pl.BlockSpec(memory_space=pltpu.SMEM)],
        out_shape=[jax.ShapeDtypeStruct((s, SB_WIDTH), BF16), jax.ShapeDtypeStruct((s, SB_WIDTH), F32),
                   jax.ShapeDtypeStruct((n_pairs, nq), jnp.int32)],
        compiler_params=_cparams(2),
    )


def _sb_bwd(qkv, do_b, tot_b, n_steps):
    s = qkv.shape[0]
    t = SB_TK
    nq = s // (2 * t)
    n_pairs = SB_WIDTH // LANES
    scale = 1.0 / math.sqrt(HEAD_DIM)

    def body(steps_ref, q_ref, k_ref, v_ref, do_ref, tot_ref, dq_ref, dk_ref, dv_ref):
        p, i = pl.program_id(0), pl.program_id(1)

        @pl.when(i == 0)
        def _():
            dk_ref[...] = jnp.zeros_like(dk_ref)
            dv_ref[...] = jnp.zeros_like(dv_ref)

        lane = lax.broadcasted_iota(jnp.int32, (1, LANES), 1)
        lane_hi = lane // HEAD_DIM
        later = _tri(t, lambda r, c: r > c)
        before = _tri(t, lambda r, c: r < c)
        causal = lax.broadcasted_iota(jnp.int32, (t, t), 1) < lax.broadcasted_iota(jnp.int32, (t, t), 0)
        qms, doms, tots = [], [], []
        for x in range(2):
            rows = pl.ds(x * t, t)
            q, do, tot_all = q_ref[rows, :] * scale, do_ref[rows, :], tot_ref[rows, :]
            qms.append([jnp.where(lane_hi == hh, q, jnp.zeros_like(q)) for hh in range(2)])
            doms.append([jnp.where(lane_hi == hh, do, jnp.zeros_like(do)) for hh in range(2)])
            tots.append([jnp.sum(jnp.where(lane == hh * HEAD_DIM, tot_all, 0.0), axis=1, keepdims=True)
                         for hh in range(2)])

        def step(groups, carry):
            offs = [pl.multiple_of(j * t, t) for _, j, _ in groups]
            ks = [k_ref[pl.ds(off, t), :] for off in offs]
            vs = [v_ref[pl.ds(off, t), :] for off in offs]
            chains = [(g, x, hh) for g, (x, _, _) in enumerate(groups) for hh in range(2)]
            heads = [_sb_chain_head(qms[x][hh], ks[g], causal if groups[g][2] else None) for g, x, hh in chains]
            sufs = [_dot(sp.astype(BF16), later) for sp, _ in heads]
            das = [_dot_nt(doms[x][hh], vs[g]) for g, x, hh in chains]
            cur = [list(carry[0]), list(carry[1])]
            sigs, gs, abs_, cg_before = [], [], [], []
            for (g_, x, hh), (sp, lpos), suf, da in zip(chains, heads, sufs, das):
                cl, cg, dq = cur[x][hh]
                cl = cl + jnp.sum(sp, axis=1, keepdims=True)
                sig = jnp.exp(lpos)
                a = sig * jnp.exp(-suf - (tots[x][hh] - cl))
                if groups[g_][2]:
                    a = jnp.where(causal, a, 0.0)
                g = a * da
                sigs.append(sig)
                gs.append(g)
                abs_.append(a.astype(BF16))
                cg_before.append(cg)
                cur[x][hh] = (cl, cg + jnp.sum(g, axis=1, keepdims=True), dq)
            prefs = [_dot(g.astype(BF16), before) for g in gs]
            dvs = [_dot_tn(ab, doms[x][hh]) for (_, x, hh), ab in zip(chains, abs_)]
            dzs = []
            for (g_, x, hh), sig, g, pref, cg in zip(chains, sigs, gs, prefs, cg_before):
                dz = g - sig * (g + pref + cg)
                if groups[g_][2]:
                    dz = jnp.where(causal, dz, 0.0)
                dzs.append(dz.astype(BF16))
            dqs = [_dot(dz, ks[g_]) for (g_, x, hh), dz in zip(chains, dzs)]
            dks = [_dot_tn(dz, qms[x][hh]) for (_, x, hh), dz in zip(chains, dzs)]
            for n, (_, x, hh) in enumerate(chains):
                cl, cg, dq = cur[x][hh]
                cur[x][hh] = (cl, cg, dq + dqs[n])
            for g_, off in enumerate(offs):
                dk_ref[pl.ds(off, t), :] += dks[2 * g_] + dks[2 * g_ + 1]
                dv_ref[pl.ds(off, t), :] += dvs[2 * g_] + dvs[2 * g_ + 1]
            return (tuple(cur[0]), tuple(cur[1]))

        taken = steps_ref[p, i]
        n_full = jnp.minimum(taken, 2 * i)
        zero = (jnp.zeros((t, 1), F32), jnp.zeros((t, 1), F32), jnp.zeros((t, LANES), F32))
        carry = ((zero, zero), (zero, zero))
        carry = lax.cond(jnp.logical_and(i > 0, taken > 2 * i), lambda ca: step([(1, 0, False)], ca), lambda ca: ca,
                         carry)
        carry = lax.fori_loop(
            0, n_full - 1,
            lambda n, ca: step([(0, 2 * i - n_full + n, False), (1, 2 * i + 1 - n_full + n, False)], ca), carry)
        carry = lax.cond(
            i == 0,
            lambda ca: step([(1, 0, False), (0, 0, True), (1, 1, True)], ca),
            lambda ca: step([(0, 2 * i - 1, False), (1, 2 * i, False), (0, 2 * i, True), (1, 2 * i + 1, True)], ca),
            carry)
        for x in range(2):
            dq = jnp.where(lane_hi == 0, carry[x][0][2], carry[x][1][2])
            dq_ref[pl.ds(x * t, t), :] = (dq * scale).astype(dq_ref.dtype)

    row_spec = pl.BlockSpec((2 * t, LANES), lambda p, i, ns: (i, p))
    full_spec = pl.BlockSpec((s, LANES), lambda p, i, ns: (0, p))
    return _pcall(
        body,
        name="sb_bwd",
        grid_spec=pltpu.PrefetchScalarGridSpec(
            num_scalar_prefetch=1,
            grid=(n_pairs, nq),
            in_specs=[
                pl.BlockSpec((2 * t, LANES), lambda p, i, ns: (i, _SB_Q0 + p)),
                pl.BlockSpec((s, LANES), lambda p, i, ns: (0, _SB_K0 + p)),
                pl.BlockSpec((s, LANES), lambda p, i, ns: (0, _SB_V0 + p)),
                row_spec, row_spec,
            ],
            out_specs=[row_spec, full_spec, full_spec],
        ),
        out_shape=[jax.ShapeDtypeStruct((s, SB_WIDTH), BF16), jax.ShapeDtypeStruct((s, SB_WIDTH), F32),
                   jax.ShapeDtypeStruct((s, SB_WIDTH), F32)],
        compiler_params=_cparams(2),
    )(n_steps, qkv, qkv, qkv, do_b, tot_b)


def _gates(gl, bg):
    return _sigmoid(gl[:, :D_MODEL] + bg[:, :D_MODEL]), _sigmoid(gl[:, D_MODEL:] + bg[:, D_MODEL:])


def _mixer_fwd(o_a, o_b, gl, x0, bg, g2, w_ud, w_us, w_out, tm):
    def epi(_, rows, consts):
        oa, ob, glv, x = rows
        bgv, g2v, wud, wus, wout = consts
        ga, gb = _gates(glv, bgv)
        merged = ga * _dot(oa, wud) + gb * _dot(ob, wus)
        x1 = x + _dot(merged.astype(BF16), wout)
        r, xh = _rms_stats(x1)
        return [x1, xh * g2v], []

    return _rowk("mixer_fwd", tm=tm, rows=[o_a, o_b, gl, x0], consts=[bg, g2, w_ud, w_us, w_out],
                 row_outs=[(D_MODEL, F32), (D_MODEL, BF16)], epilogue=epi)


def _mixer_bwd(dx1, o_a, o_b, gl, bg, w_ud, w_us, w_out, tm, rider=None):
    s = dx1.shape[0]
    nm = s // tm

    def body(dx_ref, oa_ref, ob_ref, gl_ref, bg_ref, wud_ref, wus_ref, wout_ref,
             doa_ref, dob_ref, dgl_ref, gwout_ref, gwud_ref, gwus_ref, gbg_ref, awout_ref, awud_ref, awus_ref):
        i = pl.program_id(0)
        dxb = dx_ref[...].astype(BF16)
        oa, ob = oa_ref[...], ob_ref[...]
        ga, gb = _gates(gl_ref[...], bg_ref[...])
        ua, ub = _dot(oa, wud_ref[...]), _dot(ob, wus_ref[...])
        merged = (ga * ua + gb * ub).astype(BF16)
        dm = _dot_nt(dxb, wout_ref[...])
        dua = (dm * ga).astype(BF16)
        dub = (dm * gb).astype(BF16)
        dgla = dm * ua * ga * (1.0 - ga)
        dglb = dm * ub * gb * (1.0 - gb)
        doa_ref[...] = _dot_nt(dua, wud_ref[...]).astype(doa_ref.dtype)
        dob_ref[...] = _dot_nt(dub, wus_ref[...]).astype(dob_ref.dtype)
        dgl_ref[:, :D_MODEL] = dgla.astype(dgl_ref.dtype)
        dgl_ref[:, D_MODEL:] = dglb.astype(dgl_ref.dtype)
        parts = [(gwout_ref, awout_ref, _dot_tn(merged, dxb)), (gwud_ref, awud_ref, _dot_tn(oa, dua)),
                 (gwus_ref, awus_ref, _dot_tn(ob, dub))]
        for out, r, v in parts:

            @pl.when(i == 0)
            def _(r=r, v=v):
                r[...] = v

            @pl.when(i > 0)
            def _(r=r, v=v):
                r[...] += v

            @pl.when(i == nm - 1)
            def _(out=out, r=r):
                out[...] = r[...].astype(out.dtype)

        sa = jnp.sum(dgla, axis=0, keepdims=True)
        sb = jnp.sum(dglb, axis=0, keepdims=True)

        @pl.when(i == 0)
        def _():
            gbg_ref[:, :D_MODEL] = sa
            gbg_ref[:, D_MODEL:] = sb

        @pl.when(i > 0)
        def _():
            gbg_ref[:, :D_MODEL] += sa
            gbg_ref[:, D_MODEL:] += sb

    row = lambda w: pl.BlockSpec((tm, w), lambda i: (i, 0))
    full = lambda a: pl.BlockSpec(a.shape, lambda i: (0, 0), pipeline_mode=pl.Buffered(1))
    wshape = lambda r, c: jax.ShapeDtypeStruct((r, c), BF16)
    return _call(
        body, (dx1, o_a, o_b, gl, bg, w_ud, w_us, w_out), rider,
        name="mixer_bwd",
        grid=(nm,),
        in_specs=[row(D_MODEL), row(DIL_OUT_WIDTH), row(SB_WIDTH), row(2 * D_MODEL),
                  full(bg), full(w_ud), full(w_us), full(w_out)],
        out_specs=[row(DIL_OUT_WIDTH), row(SB_WIDTH), row(2 * D_MODEL),
                   pl.BlockSpec((D_MODEL, D_MODEL), lambda i: (0, 0)),
                   pl.BlockSpec((DIL_OUT_WIDTH, D_MODEL), lambda i: (0, 0)),
                   pl.BlockSpec((SB_WIDTH, D_MODEL), lambda i: (0, 0)),
                   pl.BlockSpec((1, 2 * D_MODEL), lambda i: (0, 0))],
        out_shape=[jax.ShapeDtypeStruct((s, DIL_OUT_WIDTH), BF16), jax.ShapeDtypeStruct((s, SB_WIDTH), BF16),
                   jax.ShapeDtypeStruct((s, 2 * D_MODEL), BF16),
                   wshape(D_MODEL, D_MODEL), wshape(DIL_OUT_WIDTH, D_MODEL), wshape(SB_WIDTH, D_MODEL),
                   jax.ShapeDtypeStruct((1, 2 * D_MODEL), F32)],
        scratch_shapes=[pltpu.VMEM((D_MODEL, D_MODEL), F32), pltpu.VMEM((DIL_OUT_WIDTH, D_MODEL), F32),
                        pltpu.VMEM((SB_WIDTH, D_MODEL), F32)],
        compiler_params=_cparams(1),
    )


def _all_gather(shards):
    n = len(shards)

    def body(*refs):
        x_refs, out_refs = refs[:n], refs[n:2 * n]
        send_sems, recv_sems, local_sems = refs[2 * n:]
        x, y, c = lax.axis_index("x"), lax.axis_index("y"), lax.axis_index("c")
        me, sibling = (x, y, c), (x, y, 1 - c)
        chips = [(1 - x, y), (x, 1 - y), (1 - x, 1 - y)]

        def slot(a, px, py, pc):
            return out_refs[a].at[4 * px + 2 * py + pc]

        def copy(a, k, block, to, own=False):
            return pltpu.make_async_remote_copy(
                src_ref=x_refs[a] if own else slot(a, *block), dst_ref=slot(a, *block),
                send_sem=send_sems.at[7 * a + k], recv_sem=recv_sems.at[7 * a + k], device_id=to, device_id_type=_MESH)

        mine = [pltpu.make_async_copy(x_refs[a], slot(a, *me), local_sems.at[a]) for a in range(n)]
        for cp in mine:
            cp.start()
        first = []
        for a in range(n):
            first.append(copy(a, 0, me, sibling, own=True))
            first += [copy(a, 1 + j, me, (*chip, c), own=True) for j, chip in enumerate(chips)]
        for cp in first:
            cp.start()
        passed = []
        for a in range(n):
            for j, chip in enumerate(chips):
                copy(a, 1 + j, (*chip, c), me).wait_recv()
                passed.append(copy(a, 4 + j, (*chip, c), sibling))
                passed[-1].start()
        for a in range(n):
            copy(a, 0, sibling, me).wait_recv()
            for j, chip in enumerate(chips):
                copy(a, 4 + j, (*chip, 1 - c), me).wait_recv()
        for cp in first + passed:
            cp.wait_send()
        for cp in mine:
            cp.wait()

    return _pcall(
        body,
        name="all_gather_weights",
        in_specs=[_HBM] * n,
        out_specs=[_HBM] * n,
        out_shape=[jax.ShapeDtypeStruct((N_DEV,) + s.shape, s.dtype) for s in shards],
        scratch_shapes=[pltpu.SemaphoreType.DMA((7 * n,)), pltpu.SemaphoreType.DMA((7 * n,)),
                        pltpu.SemaphoreType.DMA((n,))],
    )(*shards)


def _exchange(chunks):
    n = len(chunks)

    def body(*refs):
        g_refs, o_refs = refs[:n], refs[n:2 * n]
        send_sems, recv_sems, local_sems = refs[2 * n:]
        x, y, c = lax.axis_index("x"), lax.axis_index("y"), lax.axis_index("c")
        me = 4 * x + 2 * y + c
        own = [pltpu.make_async_copy(g_refs[a].at[me], o_refs[a].at[me], local_sems.at[a]) for a in range(n)]
        for cp in own:
            cp.start()
        copies = []
        for a in range(n):
            for k in range(1, N_DEV):
                px, py, pc = x ^ (k >> 2), y ^ ((k >> 1) & 1), c ^ (k & 1)
                peer = 4 * px + 2 * py + pc
                copies.append(pltpu.make_async_remote_copy(
                    src_ref=g_refs[a].at[peer], dst_ref=o_refs[a].at[me], send_sem=send_sems.at[7 * a + k - 1],
                    recv_sem=recv_sems.at[7 * a + k - 1], device_id=(px, py, pc), device_id_type=_MESH))
        for cp in copies:
            cp.start()
        for cp in copies:
            cp.wait()
        for cp in own:
            cp.wait()

    return _pcall(
        body,
        name="exchange_grads",
        in_specs=[_HBM] * n,
        out_specs=[_HBM] * n,
        out_shape=[jax.ShapeDtypeStruct(g.shape, g.dtype) for g in chunks],
        scratch_shapes=[pltpu.SemaphoreType.DMA((7 * n,)), pltpu.SemaphoreType.DMA((7 * n,)),
                        pltpu.SemaphoreType.DMA((n,))],
    )(*chunks)


def _reduce_adamw(name, parts, w, m, v, tr):
    _, rows, cols = parts.shape
    tr = min(tr, rows)
    assert rows % tr == 0
    c1 = 1.0 / (1.0 - ADAM_B1 ** ADAM_STEP)
    c2 = 1.0 / (1.0 - ADAM_B2 ** ADAM_STEP)

    def body(p_ref, w_ref, m_ref, v_ref, g_out, d_out, m_out, v_out):
        g = p_ref[0].astype(F32)
        for d in range(1, N_DEV):
            g = g + p_ref[d].astype(F32)
        mn = ADAM_B1 * m_ref[...] + (1.0 - ADAM_B1) * g
        vn = ADAM_B2 * v_ref[...] + (1.0 - ADAM_B2) * (g * g)
        g_out[...] = g
        m_out[...] = mn
        v_out[...] = vn
        d_out[...] = -ADAM_LR * ((mn * c1) / (jnp.sqrt(vn * c2) + ADAM_EPS) + ADAM_WD * w_ref[...])

    spec = pl.BlockSpec((tr, cols), lambda i: (i, 0))
    return _pcall(
        body,
        name=name,
        grid=(rows // tr,),
        in_specs=[pl.BlockSpec((N_DEV, tr, cols), lambda i: (0, i, 0)), spec, spec, spec],
        out_specs=[spec] * 4,
        out_shape=[jax.ShapeDtypeStruct((rows, cols), F32)] * 4,
        compiler_params=_cparams(1),
    )(parts, w, m, v)


_SHARDED = ("w_in", "w_up_dil", "w_up_sb", "w_out", "w_mlp_in", "w_mlp_out")
_FULL_SHAPES = {"w_in": (D_MODEL, IN_COLS), "w_up_dil": (DIL_OUT_WIDTH, D_MODEL), "w_up_sb": (SB_WIDTH, D_MODEL),
                "w_out": (D_MODEL, D_MODEL), "w_mlp_in": (D_MODEL, D_FF), "w_mlp_out": (D_FF, D_MODEL)}
_ROW_SHARDED = ("w_out", "w_mlp_out")


def _shard_shape(name):
    r, c = _FULL_SHAPES[name]
    return (r // N_DEV, c) if name in _ROW_SHARDED else (r, c // N_DEV)


def _assemble(name, gathered):
    r, c = _shard_shape(name)
    if name in _ROW_SHARDED:
        return gathered.reshape(N_DEV * r, c)
    return gathered.transpose(1, 0, 2).reshape(r, N_DEV * c)


def _chunk(name, full):
    r, c = _shard_shape(name)
    if name in _ROW_SHARDED:
        return full.reshape(N_DEV, r, c)
    return full.reshape(r, N_DEV, c).transpose(1, 0, 2)


_SMALL = (("norm_mix_g", D_MODEL), ("b_gate", 2 * D_MODEL), ("norm_mlp_g", D_MODEL), ("norm_final_g", D_MODEL))
_SMALL_N = sum(n for _, n in _SMALL) + LANES


def _pack_small(vals, tail):
    return jnp.concatenate([vals[n].reshape(1, -1) for n, _ in _SMALL] + [tail], axis=1)


def _unpack_small(vec, shapes):
    out, pos = {}, 0
    for n, width in _SMALL:
        out[n] = vec[:, pos:pos + width].reshape(shapes[n])
        pos += width
    return out, vec[:, pos:]


def kernel(x, norm_mix_g, w_in, b_gate, w_up_dil, w_up_sb, w_out, norm_mlp_g, w_mlp_in, w_mlp_out, norm_final_g, loss_target, m_norm_mix_g, m_w_in, m_b_gate, m_w_up_dil, m_w_up_sb, m_w_out, m_norm_mlp_g, m_w_mlp_in, m_w_mlp_out, m_norm_final_g, v_norm_mix_g, v_w_in, v_b_gate, v_w_up_dil, v_w_up_sb, v_w_out, v_norm_mlp_g, v_w_mlp_in, v_w_mlp_out, v_norm_final_g):
    given = dict(locals())
    s = x.shape[1]
    x0 = x.reshape(s, D_MODEL)
    target = loss_target.reshape(s, D_MODEL)
    g1 = norm_mix_g.reshape(1, D_MODEL)
    g2 = norm_mlp_g.reshape(1, D_MODEL)
    g3 = norm_final_g.reshape(1, D_MODEL)
    bg = b_gate.reshape(1, 2 * D_MODEL)
    w_shards = {n: given[n].reshape(_shard_shape(n)) for n in _SHARDED}
    m_shards = {n: given["m_" + n].reshape(_shard_shape(n)) for n in _SHARDED}
    v_shards = {n: given["v_" + n].reshape(_shard_shape(n)) for n in _SHARDED}

    shard_b = {n: w_shards[n].astype(BF16) for n in _SHARDED}
    (gathered_w_in,) = _all_gather([shard_b["w_in"]])
    w_in_f = _assemble("w_in", gathered_w_in)
    w_qkv, w_gl = _group_major(w_in_f[:, :QKV_COLS]), w_in_f[:, QKV_COLS:]
    full = {}

    def norm1(_, rows, consts):
        _, xh = _rms_stats(rows[0])
        return [xh * consts[0]], []

    (h1,) = _rowk("norm_mix", tm=1024, rows=[x0], consts=[g1], row_outs=[(D_MODEL, BF16)], epilogue=norm1)
    qkv, (land,) = _mm("proj_qkv", h1, w_qkv, out_dtype=BF16, tm=1024, tn=768, tk=D_MODEL,
                       rider=_Spread([shard_b["w_mlp_in"]], chunked=False))
    full["w_mlp_in"] = _assemble("w_mlp_in", land)
    gl = _mm("proj_gates", h1, w_gl, out_dtype=BF16, tm=1024, tn=1024, tk=D_MODEL)
    views = [_dil_view(qkv, g) for g in range(len(DIL_GROUPS))]
    dil = [_dil_fwd(views[g], g) for g in range(len(DIL_GROUPS))]
    os_, lses = [d[0] for d in dil], [d[1] for d in dil]
    o_a = _dil_mix_fwd(os_, lses)
    riding = ("w_mlp_out", "w_out", "w_up_sb", "w_up_dil")
    (o_b, tot_b, sb_steps), lands = _sb_fwd(qkv, rider=_Spread([shard_b[n] for n in riding], chunked=False))
    full.update({n: _assemble(n, land) for n, land in zip(riding, lands)})
    x1, h2 = _mixer_fwd(o_a, o_b, gl, x0, bg, g2, full["w_up_dil"], full["w_up_sb"], full["w_out"], 512)
    f = _mm("mlp_in", h2, full["w_mlp_in"], out_dtype=BF16, tm=1024, tn=1024, tk=D_MODEL,
            epilogue=lambda r, _: jnp.square(jnp.maximum(r, 0.0)))

    def head(acc, rows, consts):
        x1v, tv = rows
        g3v = consts[0]
        x2 = x1v + acc
        r, xh = _rms_stats(x2)
        diff = xh * g3v - tv
        loss = (0.5 / D_MODEL) * jnp.sum(jnp.sum(diff * diff, axis=0, keepdims=True), axis=1, keepdims=True)
        dy = diff * (1.0 / D_MODEL)
        dx2, dg = _rms_bwd(dy, xh, r, g3v)
        return [dx2, dx2], [dg, jnp.broadcast_to(loss, (1, LANES))]

    dx2, dx2b, gg3, loss_part = _rowk(
        "mlp_out_loss", a=f, w=full["w_mlp_out"], tm=512, tk=D_FF, rows=[x1, target], consts=[g3],
        row_outs=[(D_MODEL, F32), (D_MODEL, BF16)], acc_outs=[D_MODEL, LANES], epilogue=head)

    da = _mm("mlp_out_bwd", dx2b, full["w_mlp_out"], tb=True, out_dtype=BF16, tm=1024, tn=1024, tk=D_MODEL, extra=f,
             epilogue=lambda r, fv: r * (2.0 * jnp.sqrt(fv.astype(F32))))
    g_w_mlp_out = _mm("grad_w_mlp_out", f, dx2b, ta=True, out_dtype=BF16, tm=1024, tn=1024, tk=2048)
    g_w_mlp_in = _mm("grad_w_mlp_in", h2, da, ta=True, out_dtype=BF16, tm=1024, tn=1024, tk=2048)

    def norm_bwd(acc, rows, consts):
        xv, dres = rows
        r, xh = _rms_stats(xv)
        dx, dg = _rms_bwd(acc, xh, r, consts[0])
        return [dres + dx], [dg]

    bchunk = lambda n, g: _chunk(n, g).astype(BF16)
    parts = {}
    (dx1, gg2), (parts["w_mlp_in"],) = _rowk(
        "mlp_in_bwd", a=da, w=full["w_mlp_in"], nt=True, tm=512, tk=D_FF, rows=[x1, dx2], consts=[g2],
        row_outs=[(D_MODEL, F32)], acc_outs=[D_MODEL], epilogue=norm_bwd,
        rider=_Spread([bchunk("w_mlp_in", g_w_mlp_in)], chunked=True))
    (do_a, do_b, dgl, g_w_out, g_w_ud, g_w_us, g_bg), (parts["w_mlp_out"],) = _mixer_bwd(
        dx1, o_a, o_b, gl, bg, full["w_up_dil"], full["w_up_sb"], full["w_out"], 512,
        rider=_Spread([bchunk("w_mlp_out", g_w_mlp_out)], chunked=True))
    mix = _dil_mix_bwd(do_a, os_, lses)
    small_three = {"w_out": g_w_out, "w_up_sb": g_w_us, "w_up_dil": g_w_ud}
    grads, lands = _dil_bwd(views[0], mix[0], lses[0], mix[3], 0,
                            rider=_Spread([bchunk(n, g) for n, g in small_three.items()], chunked=True))
    parts.update(dict(zip(small_three, lands)))
    dil_b = [grads] + [_dil_bwd(views[g], mix[g], lses[g], mix[3 + g], g) for g in (1, 2)]
    dq_b, dk_b, dv_b = _sb_bwd(qkv, do_b, tot_b, sb_steps)
    dproj = [d[0] for d in dil_b] + [d[1] for d in dil_b] + [d[2] for d in dil_b] + [dq_b, dk_b, dv_b, dgl]
    g_w_in = jnp.concatenate([
        _grad_cols("grad_w_in_dil", h1, dproj[:9], tm=D_MODEL, tk=1024),
        _grad_cols("grad_w_in_sb", h1, dproj[9:12], tm=D_MODEL, tk=1024),
        _grad_cols("grad_w_in_gates", h1, dproj[12:], tm=D_MODEL, tk=1024)], axis=1)
    (grad_x, gg1), (parts["w_in"],) = _rowk(
        "in_proj_bwd", a=dproj, w=w_in_f, nt=True, tm=512, tk=IN_COLS, rows=[x0, dx1], consts=[g1],
        row_outs=[(D_MODEL, F32)], acc_outs=[D_MODEL], epilogue=norm_bwd,
        rider=_Spread([bchunk("w_in", g_w_in)], chunked=True))

    small_part = _pack_small({"norm_mix_g": gg1, "b_gate": g_bg, "norm_mlp_g": gg2, "norm_final_g": gg3}, loss_part)
    (small_parts,) = _exchange([jnp.broadcast_to(small_part[None], (N_DEV, 1, _SMALL_N))])

    tags = ("grad_", "delta_", "new_m_", "new_v_")
    outs = {}
    for n, p in parts.items():
        res = _reduce_adamw("adamw_" + n, p, w_shards[n], m_shards[n], v_shards[n], 256)
        for tag, val in zip(tags, res):
            outs[tag + n] = val.reshape(given[n].shape)
    small_w = _pack_small(given, jnp.zeros((1, LANES), F32))
    small_m = _pack_small({n: given["m_" + n] for n, _ in _SMALL}, jnp.zeros((1, LANES), F32))
    small_v = _pack_small({n: given["v_" + n] for n, _ in _SMALL}, jnp.ones((1, LANES), F32))
    small_res = _reduce_adamw("adamw_replicated", small_parts, small_w, small_m, small_v, 8)

    small_shapes = {n: given[n].shape for n, _ in _SMALL}
    for tag, small in zip(tags, small_res):
        small_vals, tail = _unpack_small(small, small_shapes)
        for n, val in small_vals.items():
            outs[tag + n] = val
        if tag == "grad_":
            loss = tail[0, 0]
    names = ["norm_mix_g", "w_in", "b_gate", "w_up_dil", "w_up_sb", "w_out", "norm_mlp_g", "w_mlp_in", "w_mlp_out",
             "norm_final_g"]
    return (loss, grad_x.reshape(x.shape), *[outs["grad_" + n] for n in names], *[outs["delta_" + n] for n in names],
            *[outs["new_m_" + n] for n in names], *[outs["new_v_" + n] for n in names])
```

```python
import functools
import math

import jax
import jax.numpy as jnp
from jax import lax
from jax.experimental import pallas as pl
from jax.experimental.pallas import tpu as pltpu

_pcall = pl.pallas_call

F32 = jnp.float32
BF16 = jnp.bfloat16

D_MODEL = 1024
HEAD_DIM = 64
DIL_GROUPS = ((128, 1), (512, 4), (2048, 16))
DIL_HEADS_PER_GROUP = 4
N_DIL_HEADS = 12
N_SB_HEADS = 8
DIL_WIDTH = 768
DIL_OUT_WIDTH = 256
SB_WIDTH = 512
D_FF = 4096
BLOCK = 128
RMS_EPS = 1e-6
NEG_INF = -1e30
QKV_COLS = 3 * DIL_WIDTH + 3 * SB_WIDTH
IN_COLS = QKV_COLS + 2 * D_MODEL
N_DEV = 8

ADAM_LR = 0.001
ADAM_B1 = 0.9
ADAM_B2 = 0.999
ADAM_EPS = 1e-08
ADAM_WD = 0.01
ADAM_STEP = 10

VMEM_LIMIT = 56 * 1024 * 1024
SB_TK = 256
LANES = 128

_ARB = pltpu.ARBITRARY


def _cparams(n_axes, **kw):
    return pltpu.CompilerParams(dimension_semantics=(_ARB,) * n_axes, vmem_limit_bytes=VMEM_LIMIT, **kw)


def _dot(a, b):
    return jnp.dot(a, b, preferred_element_type=F32)


def _dot_nt(a, b):
    return lax.dot_general(a, b, (((1,), (1,)), ((), ())), preferred_element_type=F32)


def _dot_tn(a, b):
    return lax.dot_general(a, b, (((0,), (0,)), ((), ())), preferred_element_type=F32)


def _split_hi_lo(x):
    hi = x.astype(BF16)
    lo = (x - hi.astype(F32)).astype(BF16)
    return hi, lo


def _dot_hi_lo(x, m):
    hi, lo = _split_hi_lo(x)
    return _dot(hi, m) + _dot(lo, m)


def _sigmoid(x):
    return 1.0 / (1.0 + jnp.exp(-x))


_HBM = pl.BlockSpec(memory_space=pltpu.HBM)
_MESH = pl.DeviceIdType.MESH


class _Spread:
    def __init__(self, srcs, chunked):
        self.srcs, self.chunked, self.n = list(srcs), chunked, len(srcs)

    def land_shapes(self):
        return [jax.ShapeDtypeStruct((N_DEV,) + (s.shape[1:] if self.chunked else s.shape), s.dtype) for s in self.srcs]

    def scratch(self):
        dma = pltpu.SemaphoreType.DMA
        return [dma((7 * self.n,)), dma((7 * self.n,)), dma((self.n,))]

    def copies(self, src_refs, land_refs, send_sems, recv_sems, local_sems):
        x, y, c = lax.axis_index("x"), lax.axis_index("y"), lax.axis_index("c")
        me = 4 * x + 2 * y + c
        out = []
        for a, (src, land) in enumerate(zip(src_refs, land_refs)):
            out.append(pltpu.make_async_copy(src.at[me] if self.chunked else src, land.at[me], local_sems.at[a]))
            for k in range(1, N_DEV):
                px, py, pc = x ^ (k >> 2), y ^ ((k >> 1) & 1), c ^ (k & 1)
                out.append(pltpu.make_async_remote_copy(
                    src_ref=src.at[4 * px + 2 * py + pc] if self.chunked else src, dst_ref=land.at[me],
                    send_sem=send_sems.at[7 * a + k - 1], recv_sem=recv_sems.at[7 * a + k - 1],
                    device_id=(px, py, pc), device_id_type=_MESH))
        return out


def _call(body, args, rider=None, **kw):
    if rider is None:
        return _pcall(body, **kw)(*args)
    grid = kw["grid"]
    single = not isinstance(kw["out_shape"], (list, tuple))
    out_specs = [kw["out_specs"]] if single else list(kw["out_specs"])
    out_shape = [kw["out_shape"]] if single else list(kw["out_shape"])
    in_specs, scratch = list(kw["in_specs"]), list(kw.get("scratch_shapes", []))
    n_in, n_out, n_s, n = len(in_specs), len(out_shape), len(scratch), rider.n

    def hosted(*refs):
        ins, srcs = refs[:n_in], refs[n_in:n_in + n]
        outs, lands = refs[n_in + n:n_in + n + n_out], refs[n_in + n + n_out:n_in + 2 * n + n_out]
        own_scratch, sems = refs[n_in + 2 * n + n_out:n_in + 2 * n + n_out + n_s], refs[n_in + 2 * n + n_out + n_s:]
        ids = [pl.program_id(d) for d in range(len(grid))]
        first = functools.reduce(jnp.logical_and, [i == 0 for i in ids])
        last = functools.reduce(jnp.logical_and, [i == g - 1 for i, g in zip(ids, grid)])
        copies = rider.copies(srcs, lands, *sems)

        @pl.when(first)
        def _():
            for cp in copies:
                cp.start()

        body(*ins, *outs, *own_scratch)

        @pl.when(last)
        def _():
            for cp in copies:
                cp.wait()

    kw = dict(kw, in_specs=in_specs + [_HBM] * n, out_specs=out_specs + [_HBM] * n,
              out_shape=out_shape + rider.land_shapes(), scratch_shapes=scratch + rider.scratch())
    res = _pcall(hosted, **kw)(*args, *rider.srcs)
    return (res[0] if single else list(res[:n_out])), list(res[n_out:])


def _mm(name, a, b, *, ta=False, tb=False, out_dtype, tm, tn, tk, epilogue=None, extra=None, rider=None):
    m = a.shape[1] if ta else a.shape[0]
    k = a.shape[0] if ta else a.shape[1]
    n = b.shape[0] if tb else b.shape[1]
    assert (b.shape[1] if tb else b.shape[0]) == k
    tm, tn, tk = min(tm, m), min(tn, n), min(tk, k)
    assert m % tm == 0 and n % tn == 0 and k % tk == 0, (name, m, n, k, tm, tn, tk)
    nk = k // tk
    dn = (((0 if ta else 1,), (1 if tb else 0,)), ((), ()))
    in_place = nk > 1 and epilogue is None and out_dtype == F32

    def body(*refs):
        if extra is not None:
            a_ref, b_ref, e_ref, o_ref = refs[:4]
        else:
            a_ref, b_ref, o_ref = refs[:3]
            e_ref = None

        def finish(r):
            if epilogue is not None:
                r = epilogue(r, None if e_ref is None else e_ref[...])
            o_ref[...] = r.astype(out_dtype)

        part = lax.dot_general(a_ref[...].astype(BF16), b_ref[...].astype(BF16), dn, preferred_element_type=F32)
        if nk == 1:
            finish(part)
        else:
            acc_ref = o_ref if in_place else refs[-1]
            kk = pl.program_id(2)

            @pl.when(kk == 0)
            def _():
                acc_ref[...] = part

            @pl.when(kk > 0)
            def _():
                acc_ref[...] += part

            if not in_place:

                @pl.when(kk == nk - 1)
                def _():
                    finish(acc_ref[...])

    a_spec = pl.BlockSpec((tk, tm), lambda j, i, kk: (kk, i)) if ta else pl.BlockSpec((tm, tk), lambda j, i, kk: (i, kk))
    b_spec = pl.BlockSpec((tn, tk), lambda j, i, kk: (j, kk)) if tb else pl.BlockSpec((tk, tn), lambda j, i, kk: (kk, j))
    o_spec = pl.BlockSpec((tm, tn), lambda j, i, kk: (i, j))
    in_specs = [a_spec, b_spec]
    args = [a, b]
    if extra is not None:
        in_specs.append(o_spec)
        args.append(extra)
    return _call(
        body, args, rider,
        name=name,
        grid=(n // tn, m // tm, nk),
        in_specs=in_specs,
        out_specs=o_spec,
        out_shape=jax.ShapeDtypeStruct((m, n), out_dtype),
        scratch_shapes=[pltpu.VMEM((tm, tn), F32)] if (nk > 1 and not in_place) else [],
        compiler_params=_cparams(3),
    )


def _grad_cols(name, a, parts, *, tm, tk, rider=None):
    k, m = a.shape
    n = sum(p.shape[1] for p in parts)
    assert m % tm == 0 and k % tk == 0
    nk = k // tk

    def body(*refs):
        a_ref, p_refs, o_ref, acc_ref = refs[0], refs[1:1 + len(parts)], refs[1 + len(parts)], refs[2 + len(parts)]
        kk = pl.program_id(1)
        side_by_side = jnp.concatenate([p_ref[...].astype(BF16) for p_ref in p_refs], axis=1)
        term = _dot_tn(a_ref[...].astype(BF16), side_by_side)

        @pl.when(kk == 0)
        def _():
            acc_ref[...] = term

        @pl.when(kk > 0)
        def _():
            acc_ref[...] += term

        @pl.when(kk == nk - 1)
        def _():
            o_ref[...] = acc_ref[...].astype(o_ref.dtype)

    return _call(
        body, [a] + list(parts), rider,
        name=name,
        grid=(m // tm, nk),
        in_specs=[pl.BlockSpec((tk, tm), lambda i, kk: (kk, i))]
        + [pl.BlockSpec((tk, p.shape[1]), lambda i, kk: (kk, 0)) for p in parts],
        out_specs=pl.BlockSpec((tm, n), lambda i, kk: (i, 0)),
        out_shape=jax.ShapeDtypeStruct((m, n), BF16),
        scratch_shapes=[pltpu.VMEM((tm, n), F32)],
        compiler_params=_cparams(2),
    )


def _rowk(name, *, a=None, w=None, nt=False, tm, tk=None, rows=(), consts=(), row_outs=(), acc_outs=(), epilogue,
          rider=None):
    has_mm = a is not None
    a_parts = list(a) if isinstance(a, (list, tuple)) else ([a] if has_mm else [])
    n_a = len(a_parts)
    m = a_parts[0].shape[0] if has_mm else rows[0].shape[0]
    assert m % tm == 0
    nm = m // tm
    if has_mm:
        k = sum(p.shape[1] for p in a_parts)
        n = w.shape[0] if nt else w.shape[1]
        tk = min(tk, k)
        assert k % tk == 0 and (n_a == 1 or tk == k)
        nk = k // tk
    else:
        nk = 1
    n_rows, n_consts, n_ro, n_ao = len(rows), len(consts), len(row_outs), len(acc_outs)

    def body(*refs):
        pos = 0
        if has_mm:
            a_refs, w_ref = refs[:n_a], refs[n_a]
            pos = n_a + 1
        row_refs = refs[pos:pos + n_rows]
        pos += n_rows
        const_refs = refs[pos:pos + n_consts]
        pos += n_consts
        ro_refs = refs[pos:pos + n_ro]
        pos += n_ro
        ao_refs = refs[pos:pos + n_ao]
        pos += n_ao
        i = pl.program_id(0)
        kk = pl.program_id(1)

        def finish(acc):
            ro_vals, ao_vals = epilogue(acc, [r[...] for r in row_refs], [c[...] for c in const_refs])
            for r, v in zip(ro_refs, ro_vals):
                r[...] = v.astype(r.dtype)
            for r, v in zip(ao_refs, ao_vals):

                @pl.when(i == 0)
                def _(r=r, v=v):
                    r[...] = v

                @pl.when(i > 0)
                def _(r=r, v=v):
                    r[...] += v

        if not has_mm:
            finish(None)
            return
        part, off = None, 0
        for a_ref in a_refs:
            width = a_ref.shape[1]
            cols = slice(None) if n_a == 1 else slice(off, off + width)
            av = a_ref[...].astype(BF16)
            term = _dot_nt(av, w_ref[:, cols]) if nt else _dot(av, w_ref[cols, :])
            part = term if part is None else part + term
            off += width
        if nk == 1:
            finish(part)
        else:
            acc_ref = refs[pos]

            @pl.when(kk == 0)
            def _():
                acc_ref[...] = part

            @pl.when(kk > 0)
            def _():
                acc_ref[...] += part

            @pl.when(kk == nk - 1)
            def _():
                finish(acc_ref[...])

    once = pl.Buffered(1)
    in_specs, args = [], []
    if has_mm:
        for part in a_parts:
            in_specs.append(pl.BlockSpec((tm, tk if n_a == 1 else part.shape[1]), lambda i, kk: (i, kk)))
        w_mode = once if nk == 1 else None
        in_specs.append(pl.BlockSpec((n, tk), lambda i, kk: (0, kk), pipeline_mode=w_mode) if nt
                        else pl.BlockSpec((tk, n), lambda i, kk: (kk, 0), pipeline_mode=w_mode))
        args += a_parts + [w]
    for r in rows:
        in_specs.append(pl.BlockSpec((tm, r.shape[1]), lambda i, kk: (i, 0)))
        args.append(r)
    for c in consts:
        in_specs.append(pl.BlockSpec(c.shape, lambda i, kk: (0,) * c.ndim, pipeline_mode=once))
        args.append(c)
    out_specs, out_shape = [], []
    for width, dt in row_outs:
        out_specs.append(pl.BlockSpec((tm, width), lambda i, kk: (i, 0)))
        out_shape.append(jax.ShapeDtypeStruct((m, width), dt))
    for width in acc_outs:
        out_specs.append(pl.BlockSpec((1, width), lambda i, kk: (0, 0)))
        out_shape.append(jax.ShapeDtypeStruct((1, width), F32))
    return _call(
        body, args, rider,
        name=name,
        grid=(nm, nk),
        in_specs=in_specs,
        out_specs=out_specs,
        out_shape=out_shape,
        scratch_shapes=[pltpu.VMEM((tm, n), F32)] if (has_mm and nk > 1) else [],
        compiler_params=_cparams(2),
    )


def _rms_stats(x):
    r = lax.rsqrt(jnp.mean(x * x, axis=-1, keepdims=True) + RMS_EPS)
    return r, x * r


def _rms_bwd(dh, xh, r, g):
    gy = dh * g
    dx = r * (gy - xh * jnp.mean(gy * xh, axis=-1, keepdims=True))
    return dx, jnp.sum(dh * xh, axis=0, keepdims=True)


def _alibi_slope(head):
    return 2.0 ** (-8.0 * (head + 1) / N_DIL_HEADS)


DIL_STEP_BLOCKS = 4


def _dil_band(first_block):
    qi = lax.broadcasted_iota(jnp.int32, (BLOCK, 2 * BLOCK), 0)
    kj = lax.broadcasted_iota(jnp.int32, (BLOCK, 2 * BLOCK), 1)
    steps = qi + BLOCK - kj
    valid = (steps >= 0) & (steps <= BLOCK)
    if first_block is not False:
        valid = valid & ((kj >= BLOCK) | jnp.logical_not(first_block))
    return steps.astype(F32), valid


def _dil_step_specs(ncb, cols, nblk, clamp):
    def own(col):
        return pl.BlockSpec((nblk * BLOCK, DIL_OUT_WIDTH), lambda r, i: (clamp(i), r * ncb + col))

    def before(col):
        return pl.BlockSpec((BLOCK, DIL_OUT_WIDTH), lambda r, i: (jnp.maximum(clamp(i) * nblk - 1, 0), r * ncb + col))

    return [own(cols[0]), own(cols[1]), before(cols[1]), own(cols[2]), before(cols[2])]


DIL_RELAYOUT_ROWS = 1024


def _view_scratch(width):
    return pltpu.VMEM((width // LANES, DIL_RELAYOUT_ROWS, LANES), F32)


def _rows_from_view(src, scr, d, w):
    sub = src.shape[0]
    for j in range(w // LANES):
        for r in range(d):
            scr[j, pl.ds(r, sub, stride=d), :] = src[:, r * w + j * LANES:r * w + (j + 1) * LANES].astype(F32)


def _rows_to_view(scr, dst, d, w):
    sub = dst.shape[0]
    for j in range(w // LANES):
        for r in range(d):
            dst[:, r * w + j * LANES:r * w + (j + 1) * LANES] = scr[j, pl.ds(r, sub, stride=d), :].astype(dst.dtype)


def _dil_relayout(name, xs, dilation, to_view, col_block=0, width=None):
    d = dilation
    tm = DIL_RELAYOUT_ROWS
    rows = tm // d
    if to_view:
        s = xs[0].shape[0]
        widths = [width or x.shape[1] for x in xs]
    else:
        s = xs[0].shape[0] * d
        widths = [v.shape[1] // d for v in xs]
    assert s % tm == 0 and all(w % LANES == 0 for w in widths) and all(x.dtype == BF16 for x in xs)
    n = len(xs)
    blk = 256
    per = blk // d
    assert per % 16 == 0 and tm % blk == 0

    def body(*refs):
        in_refs, out_refs = refs[:n], refs[n:]
        i0 = lax.broadcasted_iota(jnp.int32, (blk, blk), 0)
        i1 = lax.broadcasted_iota(jnp.int32, (blk, blk), 1)
        sort = (i1 == (i0 % per) * d + i0 // per) if to_view else (i0 == (i1 % per) * d + i1 // per)
        sort = jnp.where(sort, 1.0, 0.0).astype(BF16)
        for src, dst, w in zip(in_refs, out_refs, widths):
            for b in range(tm // blk):
                if to_view:
                    y = _dot(sort, src[b * blk:(b + 1) * blk, :]).astype(BF16)
                    for r in range(d):
                        dst[b * per:(b + 1) * per, r * w:(r + 1) * w] = y[r * per:(r + 1) * per, :]
                else:
                    by_residue = jnp.concatenate(
                        [src[b * per:(b + 1) * per, r * w:(r + 1) * w] for r in range(d)], axis=0)
                    dst[b * blk:(b + 1) * blk, :] = _dot(sort, by_residue).astype(BF16)

    natural = [pl.BlockSpec((tm, w), lambda i: (i, col_block)) for w in widths]
    viewed = [pl.BlockSpec((rows, d * w), lambda i: (i, 0)) for w in widths]
    return _pcall(
        body,
        name=name,
        grid=(s // tm,),
        in_specs=natural if to_view else viewed,
        out_specs=viewed if to_view else natural,
        out_shape=[jax.ShapeDtypeStruct((s // d, d * w) if to_view else (s, w), BF16) for w in widths],
        compiler_params=_cparams(1),
    )(*xs)


def _dil_fwd(view, group):
    window, dilation = DIL_GROUPS[group]
    qkv_v, ncb, cols = view
    sub = qkv_v.shape[0]
    s = sub * dilation
    nb = sub // BLOCK
    assert nb * BLOCK * dilation == s and window // dilation == BLOCK
    nblk = min(DIL_STEP_BLOCKS, nb)
    assert nb % nblk == 0
    slopes = [_alibi_slope(group * DIL_HEADS_PER_GROUP + h) * dilation for h in range(DIL_HEADS_PER_GROUP)]

    def body(q_ref, kc_ref, kp_ref, vc_ref, vp_ref, o_ref, lse_ref):
        i = pl.program_id(1)
        kk_all = jnp.concatenate([kp_ref[...], kc_ref[...]], axis=0)
        vv_all = jnp.concatenate([vp_ref[...], vc_ref[...]], axis=0)
        head_id = lax.broadcasted_iota(jnp.int32, (1, DIL_OUT_WIDTH), 1) // HEAD_DIM
        chains = [(b, h) for b in range(nblk) for h in range(DIL_HEADS_PER_GROUP)]
        rows = lambda b: slice(b * BLOCK, (b + 1) * BLOCK)
        keys = lambda b: slice(b * BLOCK, (b + 2) * BLOCK)
        bands = [_dil_band(i == 0 if b == 0 else False) for b in range(nblk)]
        qs = [q_ref[rows(b), :] for b in range(nblk)]
        scores = [_dot_nt(jnp.where(head_id == h, qs[b], jnp.zeros_like(qs[b])), kk_all[keys(b)]) for b, h in chains]
        ps, lses = [], []
        for (b, h), sc in zip(chains, scores):
            steps, valid = bands[b]
            logits = jnp.where(valid, sc * (1.0 / math.sqrt(HEAD_DIM)) - slopes[h] * steps, NEG_INF)
            mx = jnp.max(logits, axis=1, keepdims=True)
            e = jnp.exp(logits - mx)
            den = jnp.sum(e, axis=1, keepdims=True)
            lses.append(mx + jnp.log(den))
            ps.append((e * (1.0 / den)).astype(BF16))
        outs = [_dot(p, vv_all[keys(b)]) for (b, h), p in zip(chains, ps)]
        for b in range(nblk):
            mine = [n for n, ch in enumerate(chains) if ch[0] == b]
            o, lse_all = outs[mine[0]], lses[mine[0]]
            for n in mine[1:]:
                o = jnp.where(head_id == chains[n][1], outs[n], o)
                lse_all = jnp.where(head_id == chains[n][1], lses[n], lse_all)
            o_ref[rows(b), :] = o
            lse_ref[rows(b), :] = jnp.broadcast_to(lse_all, o.shape)

    out_spec = pl.BlockSpec((nblk * BLOCK, DIL_OUT_WIDTH), lambda r, i: (i, r))
    o, lse = _pcall(
        body,
        name=f"dil_fwd_g{group}",
        grid=(dilation, nb // nblk),
        in_specs=_dil_step_specs(ncb, cols, nblk, lambda i: i),
        out_specs=[out_spec, out_spec],
        out_shape=[jax.ShapeDtypeStruct((sub, dilation * DIL_OUT_WIDTH), F32)] * 2,
        compiler_params=_cparams(2),
    )(qkv_v, qkv_v, qkv_v, qkv_v, qkv_v)
    return o, lse


def _dil_bwd(view, do_g, lse_g, dterm_g, group, rider=None):
    window, dilation = DIL_GROUPS[group]
    qkv_v, ncb, cols = view
    sub = qkv_v.shape[0]
    nb = sub // BLOCK
    nblk = min(DIL_STEP_BLOCKS, nb)
    n_steps = nb // nblk
    slopes = [_alibi_slope(group * DIL_HEADS_PER_GROUP + h) * dilation for h in range(DIL_HEADS_PER_GROUP)]
    scale = 1.0 / math.sqrt(HEAD_DIM)
    tail = slice((nblk - 1) * BLOCK, nblk * BLOCK)
    single = n_steps == 1

    def body(q_ref, kc_ref, kp_ref, vc_ref, vp_ref, do_ref, lse_ref, dt_ref, dq_ref, dk_ref, dv_ref, *carry_refs):
        i = pl.program_id(1)

        def init():
            for carry_ref in carry_refs:
                carry_ref[...] = jnp.zeros_like(carry_ref)

        def compute():
            kk_all = jnp.concatenate([kp_ref[...], kc_ref[...]], axis=0)
            vv_all = jnp.concatenate([vp_ref[...], vc_ref[...]], axis=0)
            lane = lax.broadcasted_iota(jnp.int32, (1, DIL_OUT_WIDTH), 1)
            head_id = lane // HEAD_DIM
            chains = [(b, h) for b in range(nblk) for h in range(DIL_HEADS_PER_GROUP)]
            rows = lambda b: slice(b * BLOCK, (b + 1) * BLOCK)
            keys = lambda b: slice(b * BLOCK, (b + 2) * BLOCK)
            bands = [_dil_band(i == 0 if b == 0 else False) for b in range(nblk)]
            qms, doms = [], []
            for b, h in chains:
                q, do = q_ref[rows(b), :], do_ref[rows(b), :]
                qms.append(jnp.where(head_id == h, q, jnp.zeros_like(q)))
                doms.append(jnp.where(head_id == h, do, jnp.zeros_like(do)))
            scores = [_dot_nt(qm, kk_all[keys(b)]) for (b, h), qm in zip(chains, qms)]
            dps = [_dot_nt(dom, vv_all[keys(b)]) for (b, h), dom in zip(chains, doms)]
            pbs, dss = [], []
            for n, (b, h) in enumerate(chains):
                steps, valid = bands[b]
                first = lane == h * HEAD_DIM
                lse = jnp.sum(jnp.where(first, lse_ref[rows(b), :], 0.0), axis=1, keepdims=True)
                dt = jnp.sum(jnp.where(first, dt_ref[rows(b), :], 0.0), axis=1, keepdims=True)
                logits = jnp.where(valid, scores[n] * scale - slopes[h] * steps, NEG_INF)
                p = jnp.where(valid, jnp.exp(logits - lse), 0.0)
                pbs.append(p.astype(BF16))
                dss.append((p * (dps[n] + dt) * scale).astype(BF16))
            dqs = [_dot(ds, kk_all[keys(b)]) for (b, h), ds in zip(chains, dss)]
            dks = [_dot_tn(ds, qm) for ds, qm in zip(dss, qms)]
            dvs = [_dot_tn(pb, dom) for pb, dom in zip(pbs, doms)]
            dkk, dvv = [], []
            for b in range(nblk):
                mine = [n for n, ch in enumerate(chains) if ch[0] == b]
                dq = dqs[mine[0]]
                for n in mine[1:]:
                    dq = jnp.where(head_id == chains[n][1], dqs[n], dq)
                dq_ref[rows(b), :] = dq.astype(dq_ref.dtype)
                dkk.append((dks[mine[0]] + dks[mine[1]]) + (dks[mine[2]] + dks[mine[3]]))
                dvv.append((dvs[mine[0]] + dvs[mine[1]]) + (dvs[mine[2]] + dvs[mine[3]]))
            for n, (out_ref, parts) in enumerate(((dk_ref, dkk), (dv_ref, dvv))):
                done = [parts[b][BLOCK:] + parts[b + 1][:BLOCK] if b + 1 < nblk else parts[b][BLOCK:]
                        for b in range(nblk)]
                if single:
                    for b in range(nblk):
                        out_ref[rows(b), :] = done[b].astype(out_ref.dtype)
                    continue
                carry_ref = carry_refs[n]
                if nblk > 1:
                    out_ref[: (nblk - 1) * BLOCK, :] = carry_ref[: (nblk - 1) * BLOCK, :].astype(out_ref.dtype)
                out_ref[tail, :] = (carry_ref[tail, :] + parts[0][:BLOCK]).astype(out_ref.dtype)
                for b in range(nblk):
                    carry_ref[rows(b), :] = done[b]

        def flush():
            for out_ref, carry_ref in zip((dk_ref, dv_ref), carry_refs):
                out_ref[...] = carry_ref[...].astype(out_ref.dtype)

        if single:
            compute()
        else:
            pl.when(i == 0)(init)
            pl.when(i < n_steps)(compute)
            pl.when(i == n_steps)(flush)

    clamp = lambda i: jnp.minimum(i, n_steps - 1)
    row_spec = pl.BlockSpec((nblk * BLOCK, DIL_OUT_WIDTH), lambda r, i: (clamp(i), r))
    late_spec = pl.BlockSpec((nblk * BLOCK, DIL_OUT_WIDTH), lambda r, i: (jnp.maximum(i - 1, 0), r))
    res = _call(
        body, (qkv_v, qkv_v, qkv_v, qkv_v, qkv_v, do_g, lse_g, dterm_g), rider,
        name=f"dil_bwd_g{group}",
        grid=(dilation, n_steps + (0 if single else 1)),
        in_specs=_dil_step_specs(ncb, cols, nblk, clamp) + [row_spec, row_spec, row_spec],
        out_specs=[row_spec, row_spec, row_spec] if single else [row_spec, late_spec, late_spec],
        out_shape=[jax.ShapeDtypeStruct((sub, dilation * DIL_OUT_WIDTH), BF16)] * 3,
        scratch_shapes=[] if single else [pltpu.VMEM((nblk * BLOCK, DIL_OUT_WIDTH), F32)] * 2,
        compiler_params=_cparams(2),
    )
    grads, lands = res if rider is not None else (res, None)
    if dilation > 1:
        grads = _dil_relayout(f"dil_bwd_rows_g{group}", list(grads), dilation, to_view=False)
    return tuple(grads) if rider is None else (tuple(grads), lands)


def _dil_view(qkv, group):
    _, dilation = DIL_GROUPS[group]
    w = DIL_OUT_WIDTH
    if dilation == 1:
        return qkv, QKV_COLS // w, (3 * group, 3 * group + 1, 3 * group + 2)
    (own,) = _dil_relayout(f"dil_view_g{group}", [qkv], dilation, to_view=True, col_block=group, width=3 * w)
    return own, 3, (0, 1, 2)


def _group_major(w_qkv):
    w = DIL_OUT_WIDTH
    ng = len(DIL_GROUPS)
    cols = [w_qkv[:, (part * ng + g) * w:(part * ng + g + 1) * w] for g in range(ng) for part in range(3)]
    return jnp.concatenate(cols + [w_qkv[:, 3 * DIL_WIDTH:]], axis=1)


def _head_block_ones():
    r = lax.broadcasted_iota(jnp.int32, (DIL_OUT_WIDTH, DIL_OUT_WIDTH), 0) // HEAD_DIM
    c = lax.broadcasted_iota(jnp.int32, (DIL_OUT_WIDTH, DIL_OUT_WIDTH), 1) // HEAD_DIM
    return jnp.where(r == c, 1.0, 0.0).astype(BF16)


def _dil_mix_weights(l0, l1, l2):
    mx = jnp.maximum(jnp.maximum(l0, l1), l2)
    e0, e1, e2 = jnp.exp(l0 - mx), jnp.exp(l1 - mx), jnp.exp(l2 - mx)
    inv = 1.0 / (e0 + e1 + e2)
    return e0 * inv, e1 * inv, e2 * inv


def _dil_view_spec(dilation):
    return pl.BlockSpec((DIL_RELAYOUT_ROWS // dilation, dilation * DIL_OUT_WIDTH), lambda i: (i, 0))


def _dil_mix_call(name, body, s, ins, in_specs, outs, n_relaid):
    out_specs = [_dil_view_spec(d or 1) for d, _ in outs]
    out_shape = [jax.ShapeDtypeStruct((s // (d or 1), (d or 1) * DIL_OUT_WIDTH), dt) for d, dt in outs]
    return _pcall(
        body,
        name=name,
        grid=(s // DIL_RELAYOUT_ROWS,),
        in_specs=in_specs,
        out_specs=out_specs,
        out_shape=out_shape,
        scratch_shapes=[_view_scratch(DIL_OUT_WIDTH)] * n_relaid,
        compiler_params=_cparams(1),
    )(*ins)


DIL_MIX_CHUNK = 64
_DIL_SLABS = DIL_OUT_WIDTH // LANES


def _dil_rows(refs, scratch):
    dils = [d for _, d in DIL_GROUPS]
    assert dils[0] == 1
    readers = [lambda rows, j, ref=refs[0]: ref[rows, j * LANES:(j + 1) * LANES]]
    for ref, scr, d in zip(refs[1:], scratch, dils[1:]):
        _rows_from_view(ref, scr, d, DIL_OUT_WIDTH)
        readers.append(lambda rows, j, scr=scr: scr[j, rows, :])
    return readers


def _dil_mix_chunks(step):
    def chunk(c, carry):
        step(pl.ds(pl.multiple_of(c * DIL_MIX_CHUNK, DIL_MIX_CHUNK), DIL_MIX_CHUNK))
        return carry

    lax.fori_loop(0, DIL_RELAYOUT_ROWS // DIL_MIX_CHUNK, chunk, 0, unroll=4)


def _dil_mix_fwd(os_, lses):
    ng = len(DIL_GROUPS)
    s = os_[0].shape[0]

    def body(*refs):
        o_refs, l_refs, out_ref, scratch = refs[:ng], refs[ng:2 * ng], refs[2 * ng], refs[2 * ng + 1:]
        o_at = _dil_rows(o_refs, scratch[:ng - 1])
        l_at = _dil_rows(l_refs, scratch[ng - 1:])

        def step(rows):
            for j in range(_DIL_SLABS):
                w0, w1, w2 = _dil_mix_weights(*[at(rows, j) for at in l_at])
                o0, o1, o2 = [at(rows, j) for at in o_at]
                out_ref[rows, j * LANES:(j + 1) * LANES] = (w0 * o0 + w1 * o1 + w2 * o2).astype(out_ref.dtype)

        _dil_mix_chunks(step)

    specs = [_dil_view_spec(d) for _, d in DIL_GROUPS]
    (o_a,) = _dil_mix_call("dil_mix_fwd", body, s, list(os_) + list(lses), specs * 2, [(None, BF16)], 2 * (ng - 1))
    return o_a


def _dil_mix_bwd(do_a, os_, lses):
    ng = len(DIL_GROUPS)
    s = do_a.shape[0]
    dils = [d for _, d in DIL_GROUPS]

    def body(*refs):
        do_ref, o_refs, l_refs = refs[0], refs[1:1 + ng], refs[1 + ng:1 + 2 * ng]
        out_refs, scratch = refs[1 + 2 * ng:1 + 4 * ng], refs[1 + 4 * ng:]
        o_at = _dil_rows(o_refs, scratch[:ng - 1])
        l_at = _dil_rows(l_refs, scratch[ng - 1:2 * (ng - 1)])
        spare = iter(scratch[2 * (ng - 1):])
        staged = [None if dils[n % ng] == 1 else next(spare) for n in range(2 * ng)]
        ones = _head_block_ones()

        def step(rows):
            do = do_ref[rows, :].astype(F32)
            ws, prods = [], []
            for j in range(_DIL_SLABS):
                w0, w1, w2 = _dil_mix_weights(*[at(rows, j) for at in l_at])
                o0, o1, o2 = [at(rows, j) for at in o_at]
                ws.append((w0, w1, w2))
                prods.append(do[:, j * LANES:(j + 1) * LANES] * (w0 * o0 + w1 * o1 + w2 * o2))
            tot = _dot_hi_lo(jnp.concatenate(prods, axis=1), ones)
            for j in range(_DIL_SLABS):
                slab = slice(j * LANES, (j + 1) * LANES)
                vals = [w * do[:, slab] for w in ws[j]] + [-w * tot[:, slab] for w in ws[j]]
                for val, dst, scr in zip(vals, out_refs, staged):
                    if scr is None:
                        dst[rows, slab] = val.astype(dst.dtype)
                    else:
                        scr[j, rows, :] = val

        _dil_mix_chunks(step)
        for n, (dst, scr) in enumerate(zip(out_refs, staged)):
            if scr is not None:
                _rows_to_view(scr, dst, dils[n % ng], DIL_OUT_WIDTH)

    specs = [_dil_view_spec(d) for d in dils]
    return _dil_mix_call(
        "dil_mix_bwd", body, s, [do_a] + list(os_) + list(lses), [_dil_view_spec(1)] + specs * 2,
        [(d, BF16) for d in dils] + [(d, F32) for d in dils], 4 * (ng - 1))


_SB_Q0 = 3 * DIL_WIDTH // LANES
_SB_K0 = _SB_Q0 + SB_WIDTH // LANES
_SB_V0 = _SB_K0 + SB_WIDTH // LANES


_EXP_CLAMP = 88.0
_SB_DEAD = 104.0


def _tri(t, op):
    r = lax.broadcasted_iota(jnp.int32, (t, t), 0)
    c = lax.broadcasted_iota(jnp.int32, (t, t), 1)
    return jnp.where(op(r, c), 1.0, 0.0).astype(BF16)


def _softplus(z):
    return jnp.maximum(z, jnp.log(1.0 + jnp.exp(jnp.minimum(z, _EXP_CLAMP))))


def _sb_chain_head(qm, kj, mask):
    z = _dot_nt(qm, kj)
    sp = _softplus(z)
    return (sp if mask is None else jnp.where(mask, sp, 0.0)), z - sp


def _sb_fwd(qkv, rider=None):
    s = qkv.shape[0]
    t = SB_TK
    assert s % (2 * t) == 0
    nq = s // (2 * t)
    n_pairs = SB_WIDTH // LANES

    def body(q_ref, k_ref, v_ref, o_ref, tot_ref, steps_ref):
        p, i = pl.program_id(0), pl.program_id(1)
        lane_hi = lax.broadcasted_iota(jnp.int32, (1, LANES), 1) // HEAD_DIM
        later = _tri(t, lambda r, c: r > c)
        causal = lax.broadcasted_iota(jnp.int32, (t, t), 1) < lax.broadcasted_iota(jnp.int32, (t, t), 0)
        qms = []
        for x in range(2):
            q = q_ref[pl.ds(x * t, t), :] * (1.0 / math.sqrt(HEAD_DIM))
            qms.append([jnp.where(lane_hi == hh, q, jnp.zeros_like(q)) for hh in range(2)])

        def tile(j):
            off = pl.multiple_of(j * t, t)
            return k_ref[pl.ds(off, t), :], v_ref[pl.ds(off, t), :]

        def step(groups, carry):
            kv = [tile(j) for _, j, _ in groups]
            chains = [(g, x, hh) for g, (x, _, _) in enumerate(groups) for hh in range(2)]
            heads = [_sb_chain_head(qms[x][hh], kv[g][0], causal if groups[g][2] else None) for g, x, hh in chains]
            sufs = [_dot(sp.astype(BF16), later) for sp, _ in heads]
            cur = [list(carry[0]), list(carry[1])]
            for (g, x, hh), (sp, lpos), suf in zip(chains, heads, sufs):
                c, acc = cur[x][hh]
                a = jnp.exp(lpos - suf - c)
                if groups[g][2]:
                    a = jnp.where(causal, a, 0.0)
                cur[x][hh] = (c + jnp.sum(sp, axis=1, keepdims=True), acc + _dot(a.astype(BF16), kv[g][1]))
            return (tuple(cur[0]), tuple(cur[1]))

        def lowest(carry):
            return jnp.min(jnp.minimum(jnp.minimum(carry[0][0][0], carry[0][1][0]),
                                       jnp.minimum(carry[1][0][0], carry[1][1][0])))

        zero = (jnp.zeros((t, 1), F32), jnp.zeros((t, LANES), F32))
        start = ((zero, zero), (zero, zero))
        carry = lax.cond(
            i == 0,
            lambda ca: step([(0, 0, True), (1, 1, True), (1, 0, False)], ca),
            lambda ca: step([(0, 2 * i, True), (1, 2 * i + 1, True), (0, 2 * i - 1, False), (1, 2 * i, False)], ca),
            start)

        def walk(state):
            n, ca, _ = state
            ca = step([(0, 2 * i - 2 - n, False), (1, 2 * i - 1 - n, False)], ca)
            return n + 1, ca, lowest(ca)

        n_more, carry, low = lax.while_loop(
            lambda st: jnp.logical_and(st[0] + 1 < 2 * i, st[2] <= _SB_DEAD), walk, (jnp.int32(0), carry, lowest(carry)))
        b_last = jnp.logical_and(jnp.logical_and(i > 0, n_more + 1 == 2 * i), low <= _SB_DEAD)
        carry = lax.cond(b_last, lambda ca: step([(1, 0, False)], ca), lambda ca: ca, carry)
        for x in range(2):
            (c0, acc0), (c1, acc1) = carry[x]
            o_ref[pl.ds(x * t, t), :] = jnp.where(lane_hi == 0, acc0, acc1).astype(o_ref.dtype)
            tot_ref[pl.ds(x * t, t), :] = jnp.where(lane_hi == 0, c0, c1)
        steps_ref[p, i] = 1 + n_more + b_last.astype(jnp.int32)

    return _call(
        body, (qkv, qkv, qkv), rider,
        name="sb_fwd",
        grid=(n_pairs, nq),
        in_specs=[
            pl.BlockSpec((2 * t, LANES), lambda p, i: (i, _SB_Q0 + p)),
            pl.BlockSpec((s, LANES), lambda p, i: (0, _SB_K0 + p)),
            pl.BlockSpec((s, LANES), lambda p, i: (0, _SB_V0 + p)),
        ],
        out_specs=[pl.BlockSpec((2 * t, LANES), lambda p, i: (i, p))] * 2 + [pl.BlockSpec(memory_space=pltpu.SMEM)],
        out_shape=[jax.ShapeDtypeStruct((s, SB_WIDTH), BF16), jax.ShapeDtypeStruct((s, SB_WIDTH), F32),
                   jax.ShapeDtypeStruct((n_pairs, nq), jnp.int32)],
        compiler_params=_cparams(2),
    )


def _sb_bwd(qkv, do_b, tot_b, n_steps):
    s = qkv.shape[0]
    t = SB_TK
    nq = s // (2 * t)
    n_pairs = SB_WIDTH // LANES
    scale = 1.0 / math.sqrt(HEAD_DIM)

    def body(steps_ref, q_ref, k_ref, v_ref, do_ref, tot_ref, dq_ref, dk_ref, dv_ref):
        p, i = pl.program_id(0), pl.program_id(1)

        @pl.when(i == 0)
        def _():
            dk_ref[...] = jnp.zeros_like(dk_ref)
            dv_ref[...] = jnp.zeros_like(dv_ref)

        lane = lax.broadcasted_iota(jnp.int32, (1, LANES), 1)
        lane_hi = lane // HEAD_DIM
        later = _tri(t, lambda r, c: r > c)
        before = _tri(t, lambda r, c: r < c)
        causal = lax.broadcasted_iota(jnp.int32, (t, t), 1) < lax.broadcasted_iota(jnp.int32, (t, t), 0)
        qms, doms, tots = [], [], []
        for x in range(2):
            rows = pl.ds(x * t, t)
            q, do, tot_all = q_ref[rows, :] * scale, do_ref[rows, :], tot_ref[rows, :]
            qms.append([jnp.where(lane_hi == hh, q, jnp.zeros_like(q)) for hh in range(2)])
            doms.append([jnp.where(lane_hi == hh, do, jnp.zeros_like(do)) for hh in range(2)])
            tots.append([jnp.sum(jnp.where(lane == hh * HEAD_DIM, tot_all, 0.0), axis=1, keepdims=True)
                         for hh in range(2)])

        def step(groups, carry):
            offs = [pl.multiple_of(j * t, t) for _, j, _ in groups]
            ks = [k_ref[pl.ds(off, t), :] for off in offs]
            vs = [v_ref[pl.ds(off, t), :] for off in offs]
            chains = [(g, x, hh) for g, (x, _, _) in enumerate(groups) for hh in range(2)]
            heads = [_sb_chain_head(qms[x][hh], ks[g], causal if groups[g][2] else None) for g, x, hh in chains]
            sufs = [_dot(sp.astype(BF16), later) for sp, _ in heads]
            das = [_dot_nt(doms[x][hh], vs[g]) for g, x, hh in chains]
            cur = [list(carry[0]), list(carry[1])]
            sigs, gs, abs_, cg_before = [], [], [], []
            for (g_, x, hh), (sp, lpos), suf, da in zip(chains, heads, sufs, das):
                cl, cg, dq = cur[x][hh]
                cl = cl + jnp.sum(sp, axis=1, keepdims=True)
                sig = jnp.exp(lpos)
                a = sig * jnp.exp(-suf - (tots[x][hh] - cl))
                if groups[g_][2]:
                    a = jnp.where(causal, a, 0.0)
                g = a * da
                sigs.append(sig)
                gs.append(g)
                abs_.append(a.astype(BF16))
                cg_before.append(cg)
                cur[x][hh] = (cl, cg + jnp.sum(g, axis=1, keepdims=True), dq)
            prefs = [_dot(g.astype(BF16), before) for g in gs]
            dvs = [_dot_tn(ab, doms[x][hh]) for (_, x, hh), ab in zip(chains, abs_)]
            dzs = []
            for (g_, x, hh), sig, g, pref, cg in zip(chains, sigs, gs, prefs, cg_before):
                dz = g - sig * (g + pref + cg)
                if groups[g_][2]:
                    dz = jnp.where(causal, dz, 0.0)
                dzs.append(dz.astype(BF16))
            dqs = [_dot(dz, ks[g_]) for (g_, x, hh), dz in zip(chains, dzs)]
            dks = [_dot_tn(dz, qms[x][hh]) for (_, x, hh), dz in zip(chains, dzs)]
            for n, (_, x, hh) in enumerate(chains):
                cl, cg, dq = cur[x][hh]
                cur[x][hh] = (cl, cg, dq + dqs[n])
            for g_, off in enumerate(offs):
                dk_ref[pl.ds(off, t), :] += dks[2 * g_] + dks[2 * g_ + 1]
                dv_ref[pl.ds(off, t), :] += dvs[2 * g_] + dvs[2 * g_ + 1]
            return (tuple(cur[0]), tuple(cur[1]))

        taken = steps_ref[p, i]
        n_full = jnp.minimum(taken, 2 * i)
        zero = (jnp.zeros((t, 1), F32), jnp.zeros((t, 1), F32), jnp.zeros((t, LANES), F32))
        carry = ((zero, zero), (zero, zero))
        carry = lax.cond(jnp.logical_and(i > 0, taken > 2 * i), lambda ca: step([(1, 0, False)], ca), lambda ca: ca,
                         carry)
        carry = lax.fori_loop(
            0, n_full - 1,
            lambda n, ca: step([(0, 2 * i - n_full + n, False), (1, 2 * i + 1 - n_full + n, False)], ca), carry)
        carry = lax.cond(
            i == 0,
            lambda ca: step([(1, 0, False), (0, 0, True), (1, 1, True)], ca),
            lambda ca: step([(0, 2 * i - 1, False), (1, 2 * i, False), (0, 2 * i, True), (1, 2 * i + 1, True)], ca),
            carry)
        for x in range(2):
            dq = jnp.where(lane_hi == 0, carry[x][0][2], carry[x][1][2])
            dq_ref[pl.ds(x * t, t), :] = (dq * scale).astype(dq_ref.dtype)

    row_spec = pl.BlockSpec((2 * t, LANES), lambda p, i, ns: (i, p))
    full_spec = pl.BlockSpec((s, LANES), lambda p, i, ns: (0, p))
    return _pcall(
        body,
        name="sb_bwd",
        grid_spec=pltpu.PrefetchScalarGridSpec(
            num_scalar_prefetch=1,
            grid=(n_pairs, nq),
            in_specs=[
                pl.BlockSpec((2 * t, LANES), lambda p, i, ns: (i, _SB_Q0 + p)),
                pl.BlockSpec((s, LANES), lambda p, i, ns: (0, _SB_K0 + p)),
                pl.BlockSpec((s, LANES), lambda p, i, ns: (0, _SB_V0 + p)),
                row_spec, row_spec,
            ],
            out_specs=[row_spec, full_spec, full_spec],
        ),
        out_shape=[jax.ShapeDtypeStruct((s, SB_WIDTH), BF16), jax.ShapeDtypeStruct((s, SB_WIDTH), F32),
                   jax.ShapeDtypeStruct((s, SB_WIDTH), F32)],
        compiler_params=_cparams(2),
    )(n_steps, qkv, qkv, qkv, do_b, tot_b)


def _gates(gl, bg):
    return _sigmoid(gl[:, :D_MODEL] + bg[:, :D_MODEL]), _sigmoid(gl[:, D_MODEL:] + bg[:, D_MODEL:])


def _mixer_fwd(o_a, o_b, gl, x0, bg, g2, w_ud, w_us, w_out, tm):
    def epi(_, rows, consts):
        oa, ob, glv, x = rows
        bgv, g2v, wud, wus, wout = consts
        ga, gb = _gates(glv, bgv)
        merged = ga * _dot(oa, wud) + gb * _dot(ob, wus)
        x1 = x + _dot(merged.astype(BF16), wout)
        r, xh = _rms_stats(x1)
        return [x1, xh * g2v], []

    return _rowk("mixer_fwd", tm=tm, rows=[o_a, o_b, gl, x0], consts=[bg, g2, w_ud, w_us, w_out],
                 row_outs=[(D_MODEL, F32), (D_MODEL, BF16)], epilogue=epi)


def _mixer_bwd(dx1, o_a, o_b, gl, bg, w_ud, w_us, w_out, tm, rider=None):
    s = dx1.shape[0]
    nm = s // tm

    def body(dx_ref, oa_ref, ob_ref, gl_ref, bg_ref, wud_ref, wus_ref, wout_ref,
             doa_ref, dob_ref, dgl_ref, gwout_ref, gwud_ref, gwus_ref, gbg_ref, awout_ref, awud_ref, awus_ref):
        i = pl.program_id(0)
        dxb = dx_ref[...].astype(BF16)
        oa, ob = oa_ref[...], ob_ref[...]
        ga, gb = _gates(gl_ref[...], bg_ref[...])
        ua, ub = _dot(oa, wud_ref[...]), _dot(ob, wus_ref[...])
        merged = (ga * ua + gb * ub).astype(BF16)
        dm = _dot_nt(dxb, wout_ref[...])
        dua = (dm * ga).astype(BF16)
        dub = (dm * gb).astype(BF16)
        dgla = dm * ua * ga * (1.0 - ga)
        dglb = dm * ub * gb * (1.0 - gb)
        doa_ref[...] = _dot_nt(dua, wud_ref[...]).astype(doa_ref.dtype)
        dob_ref[...] = _dot_nt(dub, wus_ref[...]).astype(dob_ref.dtype)
        dgl_ref[:, :D_MODEL] = dgla.astype(dgl_ref.dtype)
        dgl_ref[:, D_MODEL:] = dglb.astype(dgl_ref.dtype)
        parts = [(gwout_ref, awout_ref, _dot_tn(merged, dxb)), (gwud_ref, awud_ref, _dot_tn(oa, dua)),
                 (gwus_ref, awus_ref, _dot_tn(ob, dub))]
        for out, r, v in parts:

            @pl.when(i == 0)
            def _(r=r, v=v):
                r[...] = v

            @pl.when(i > 0)
            def _(r=r, v=v):
                r[...] += v

            @pl.when(i == nm - 1)
            def _(out=out, r=r):
                out[...] = r[...].astype(out.dtype)

        sa = jnp.sum(dgla, axis=0, keepdims=True)
        sb = jnp.sum(dglb, axis=0, keepdims=True)

        @pl.when(i == 0)
        def _():
            gbg_ref[:, :D_MODEL] = sa
            gbg_ref[:, D_MODEL:] = sb

        @pl.when(i > 0)
        def _():
            gbg_ref[:, :D_MODEL] += sa
            gbg_ref[:, D_MODEL:] += sb

    row = lambda w: pl.BlockSpec((tm, w), lambda i: (i, 0))
    full = lambda a: pl.BlockSpec(a.shape, lambda i: (0, 0), pipeline_mode=pl.Buffered(1))
    wshape = lambda r, c: jax.ShapeDtypeStruct((r, c), BF16)
    return _call(
        body, (dx1, o_a, o_b, gl, bg, w_ud, w_us, w_out), rider,
        name="mixer_bwd",
        grid=(nm,),
        in_specs=[row(D_MODEL), row(DIL_OUT_WIDTH), row(SB_WIDTH), row(2 * D_MODEL),
                  full(bg), full(w_ud), full(w_us), full(w_out)],
        out_specs=[row(DIL_OUT_WIDTH), row(SB_WIDTH), row(2 * D_MODEL),
                   pl.BlockSpec((D_MODEL, D_MODEL), lambda i: (0, 0)),
                   pl.BlockSpec((DIL_OUT_WIDTH, D_MODEL), lambda i: (0, 0)),
                   pl.BlockSpec((SB_WIDTH, D_MODEL), lambda i: (0, 0)),
                   pl.BlockSpec((1, 2 * D_MODEL), lambda i: (0, 0))],
        out_shape=[jax.ShapeDtypeStruct((s, DIL_OUT_WIDTH), BF16), jax.ShapeDtypeStruct((s, SB_WIDTH), BF16),
                   jax.ShapeDtypeStruct((s, 2 * D_MODEL), BF16),
                   wshape(D_MODEL, D_MODEL), wshape(DIL_OUT_WIDTH, D_MODEL), wshape(SB_WIDTH, D_MODEL),
                   jax.ShapeDtypeStruct((1, 2 * D_MODEL), F32)],
        scratch_shapes=[pltpu.VMEM((D_MODEL, D_MODEL), F32), pltpu.VMEM((DIL_OUT_WIDTH, D_MODEL), F32),
                        pltpu.VMEM((SB_WIDTH, D_MODEL), F32)],
        compiler_params=_cparams(1),
    )


def _all_gather(shards):
    n = len(shards)

    def body(*refs):
        x_refs, out_refs = refs[:n], refs[n:2 * n]
        send_sems, recv_sems, local_sems = refs[2 * n:]
        x, y, c = lax.axis_index("x"), lax.axis_index("y"), lax.axis_index("c")
        me, sibling = (x, y, c), (x, y, 1 - c)
        chips = [(1 - x, y), (x, 1 - y), (1 - x, 1 - y)]

        def slot(a, px, py, pc):
            return out_refs[a].at[4 * px + 2 * py + pc]

        def copy(a, k, block, to, own=False):
            return pltpu.make_async_remote_copy(
                src_ref=x_refs[a] if own else slot(a, *block), dst_ref=slot(a, *block),
                send_sem=send_sems.at[7 * a + k], recv_sem=recv_sems.at[7 * a + k], device_id=to, device_id_type=_MESH)

        mine = [pltpu.make_async_copy(x_refs[a], slot(a, *me), local_sems.at[a]) for a in range(n)]
        for cp in mine:
            cp.start()
        first = []
        for a in range(n):
            first.append(copy(a, 0, me, sibling, own=True))
            first += [copy(a, 1 + j, me, (*chip, c), own=True) for j, chip in enumerate(chips)]
        for cp in first:
            cp.start()
        passed = []
        for a in range(n):
            for j, chip in enumerate(chips):
                copy(a, 1 + j, (*chip, c), me).wait_recv()
                passed.append(copy(a, 4 + j, (*chip, c), sibling))
                passed[-1].start()
        for a in range(n):
            copy(a, 0, sibling, me).wait_recv()
            for j, chip in enumerate(chips):
                copy(a, 4 + j, (*chip, 1 - c), me).wait_recv()
        for cp in first + passed:
            cp.wait_send()
        for cp in mine:
            cp.wait()

    return _pcall(
        body,
        name="all_gather_weights",
        in_specs=[_HBM] * n,
        out_specs=[_HBM] * n,
        out_shape=[jax.ShapeDtypeStruct((N_DEV,) + s.shape, s.dtype) for s in shards],
        scratch_shapes=[pltpu.SemaphoreType.DMA((7 * n,)), pltpu.SemaphoreType.DMA((7 * n,)),
                        pltpu.SemaphoreType.DMA((n,))],
    )(*shards)


def _exchange(chunks):
    n = len(chunks)

    def body(*refs):
        g_refs, o_refs = refs[:n], refs[n:2 * n]
        send_sems, recv_sems, local_sems = refs[2 * n:]
        x, y, c = lax.axis_index("x"), lax.axis_index("y"), lax.axis_index("c")
        me = 4 * x + 2 * y + c
        own = [pltpu.make_async_copy(g_refs[a].at[me], o_refs[a].at[me], local_sems.at[a]) for a in range(n)]
        for cp in own:
            cp.start()
        copies = []
        for a in range(n):
            for k in range(1, N_DEV):
                px, py, pc = x ^ (k >> 2), y ^ ((k >> 1) & 1), c ^ (k & 1)
                peer = 4 * px + 2 * py + pc
                copies.append(pltpu.make_async_remote_copy(
                    src_ref=g_refs[a].at[peer], dst_ref=o_refs[a].at[me], send_sem=send_sems.at[7 * a + k - 1],
                    recv_sem=recv_sems.at[7 * a + k - 1], device_id=(px, py, pc), device_id_type=_MESH))
        for cp in copies:
            cp.start()
        for cp in copies:
            cp.wait()
        for cp in own:
            cp.wait()

    return _pcall(
        body,
        name="exchange_grads",
        in_specs=[_HBM] * n,
        out_specs=[_HBM] * n,
        out_shape=[jax.ShapeDtypeStruct(g.shape, g.dtype) for g in chunks],
        scratch_shapes=[pltpu.SemaphoreType.DMA((7 * n,)), pltpu.SemaphoreType.DMA((7 * n,)),
                        pltpu.SemaphoreType.DMA((n,))],
    )(*chunks)


def _reduce_adamw(name, parts, w, m, v, tr):
    _, rows, cols = parts.shape
    tr = min(tr, rows)
    assert rows % tr == 0
    c1 = 1.0 / (1.0 - ADAM_B1 ** ADAM_STEP)
    c2 = 1.0 / (1.0 - ADAM_B2 ** ADAM_STEP)

    def body(p_ref, w_ref, m_ref, v_ref, g_out, d_out, m_out, v_out):
        g = p_ref[0].astype(F32)
        for d in range(1, N_DEV):
            g = g + p_ref[d].astype(F32)
        mn = ADAM_B1 * m_ref[...] + (1.0 - ADAM_B1) * g
        vn = ADAM_B2 * v_ref[...] + (1.0 - ADAM_B2) * (g * g)
        g_out[...] = g
        m_out[...] = mn
        v_out[...] = vn
        d_out[...] = -ADAM_LR * ((mn * c1) / (jnp.sqrt(vn * c2) + ADAM_EPS) + ADAM_WD * w_ref[...])

    spec = pl.BlockSpec((tr, cols), lambda i: (i, 0))
    return _pcall(
        body,
        name=name,
        grid=(rows // tr,),
        in_specs=[pl.BlockSpec((N_DEV, tr, cols), lambda i: (0, i, 0)), spec, spec, spec],
        out_specs=[spec] * 4,
        out_shape=[jax.ShapeDtypeStruct((rows, cols), F32)] * 4,
        compiler_params=_cparams(1),
    )(parts, w, m, v)


_SHARDED = ("w_in", "w_up_dil", "w_up_sb", "w_out", "w_mlp_in", "w_mlp_out")
_FULL_SHAPES = {"w_in": (D_MODEL, IN_COLS), "w_up_dil": (DIL_OUT_WIDTH, D_MODEL), "w_up_sb": (SB_WIDTH, D_MODEL),
                "w_out": (D_MODEL, D_MODEL), "w_mlp_in": (D_MODEL, D_FF), "w_mlp_out": (D_FF, D_MODEL)}
_ROW_SHARDED = ("w_out", "w_mlp_out")


def _shard_shape(name):
    r, c = _FULL_SHAPES[name]
    return (r // N_DEV, c) if name in _ROW_SHARDED else (r, c // N_DEV)


def _assemble(name, gathered):
    r, c = _shard_shape(name)
    if name in _ROW_SHARDED:
        return gathered.reshape(N_DEV * r, c)
    return gathered.transpose(1, 0, 2).reshape(r, N_DEV * c)


def _chunk(name, full):
    r, c = _shard_shape(name)
    if name in _ROW_SHARDED:
        return full.reshape(N_DEV, r, c)
    return full.reshape(r, N_DEV, c).transpose(1, 0, 2)


_SMALL = (("norm_mix_g", D_MODEL), ("b_gate", 2 * D_MODEL), ("norm_mlp_g", D_MODEL), ("norm_final_g", D_MODEL))
_SMALL_N = sum(n for _, n in _SMALL) + LANES


def _pack_small(vals, tail):
    return jnp.concatenate([vals[n].reshape(1, -1) for n, _ in _SMALL] + [tail], axis=1)


def _unpack_small(vec, shapes):
    out, pos = {}, 0
    for n, width in _SMALL:
        out[n] = vec[:, pos:pos + width].reshape(shapes[n])
        pos += width
    return out, vec[:, pos:]


def kernel(x, norm_mix_g, w_in, b_gate, w_up_dil, w_up_sb, w_out, norm_mlp_g, w_mlp_in, w_mlp_out, norm_final_g, loss_target, m_norm_mix_g, m_w_in, m_b_gate, m_w_up_dil, m_w_up_sb, m_w_out, m_norm_mlp_g, m_w_mlp_in, m_w_mlp_out, m_norm_final_g, v_norm_mix_g, v_w_in, v_b_gate, v_w_up_dil, v_w_up_sb, v_w_out, v_norm_mlp_g, v_w_mlp_in, v_w_mlp_out, v_norm_final_g):
    given = dict(locals())
    s = x.shape[1]
    x0 = x.reshape(s, D_MODEL)
    target = loss_target.reshape(s, D_MODEL)
    g1 = norm_mix_g.reshape(1, D_MODEL)
    g2 = norm_mlp_g.reshape(1, D_MODEL)
    g3 = norm_final_g.reshape(1, D_MODEL)
    bg = b_gate.reshape(1, 2 * D_MODEL)
    w_shards = {n: given[n].reshape(_shard_shape(n)) for n in _SHARDED}
    m_shards = {n: given["m_" + n].reshape(_shard_shape(n)) for n in _SHARDED}
    v_shards = {n: given["v_" + n].reshape(_shard_shape(n)) for n in _SHARDED}

    shard_b = {n: w_shards[n].astype(BF16) for n in _SHARDED}
    (gathered_w_in,) = _all_gather([shard_b["w_in"]])
    w_in_f = _assemble("w_in", gathered_w_in)
    w_qkv, w_gl = _group_major(w_in_f[:, :QKV_COLS]), w_in_f[:, QKV_COLS:]
    full = {}

    def norm1(_, rows, consts):
        _, xh = _rms_stats(rows[0])
        return [xh * consts[0]], []

    (h1,) = _rowk("norm_mix", tm=1024, rows=[x0], consts=[g1], row_outs=[(D_MODEL, BF16)], epilogue=norm1)
    qkv, (land,) = _mm("proj_qkv", h1, w_qkv, out_dtype=BF16, tm=1024, tn=768, tk=D_MODEL,
                       rider=_Spread([shard_b["w_mlp_in"]], chunked=False))
    full["w_mlp_in"] = _assemble("w_mlp_in", land)
    gl = _mm("proj_gates", h1, w_gl, out_dtype=BF16, tm=1024, tn=1024, tk=D_MODEL)
    views = [_dil_view(qkv, g) for g in range(len(DIL_GROUPS))]
    dil = [_dil_fwd(views[g], g) for g in range(len(DIL_GROUPS))]
    os_, lses = [d[0] for d in dil], [d[1] for d in dil]
    o_a = _dil_mix_fwd(os_, lses)
    riding = ("w_mlp_out", "w_out", "w_up_sb", "w_up_dil")
    (o_b, tot_b, sb_steps), lands = _sb_fwd(qkv, rider=_Spread([shard_b[n] for n in riding], chunked=False))
    full.update({n: _assemble(n, land) for n, land in zip(riding, lands)})
    x1, h2 = _mixer_fwd(o_a, o_b, gl, x0, bg, g2, full["w_up_dil"], full["w_up_sb"], full["w_out"], 512)
    f = _mm("mlp_in", h2, full["w_mlp_in"], out_dtype=BF16, tm=1024, tn=1024, tk=D_MODEL,
            epilogue=lambda r, _: jnp.square(jnp.maximum(r, 0.0)))

    def head(acc, rows, consts):
        x1v, tv = rows
        g3v = consts[0]
        x2 = x1v + acc
        r, xh = _rms_stats(x2)
        diff = xh * g3v - tv
        loss = (0.5 / D_MODEL) * jnp.sum(jnp.sum(diff * diff, axis=0, keepdims=True), axis=1, keepdims=True)
        dy = diff * (1.0 / D_MODEL)
        dx2, dg = _rms_bwd(dy, xh, r, g3v)
        return [dx2, dx2], [dg, jnp.broadcast_to(loss, (1, LANES))]

    dx2, dx2b, gg3, loss_part = _rowk(
        "mlp_out_loss", a=f, w=full["w_mlp_out"], tm=512, tk=D_FF, rows=[x1, target], consts=[g3],
        row_outs=[(D_MODEL, F32), (D_MODEL, BF16)], acc_outs=[D_MODEL, LANES], epilogue=head)

    da = _mm("mlp_out_bwd", dx2b, full["w_mlp_out"], tb=True, out_dtype=BF16, tm=1024, tn=1024, tk=D_MODEL, extra=f,
             epilogue=lambda r, fv: r * (2.0 * jnp.sqrt(fv.astype(F32))))
    g_w_mlp_out = _mm("grad_w_mlp_out", f, dx2b, ta=True, out_dtype=BF16, tm=1024, tn=1024, tk=2048)
    g_w_mlp_in = _mm("grad_w_mlp_in", h2, da, ta=True, out_dtype=BF16, tm=1024, tn=1024, tk=2048)

    def norm_bwd(acc, rows, consts):
        xv, dres = rows
        r, xh = _rms_stats(xv)
        dx, dg = _rms_bwd(acc, xh, r, consts[0])
        return [dres + dx], [dg]

    bchunk = lambda n, g: _chunk(n, g).astype(BF16)
    parts = {}
    (dx1, gg2), (parts["w_mlp_in"],) = _rowk(
        "mlp_in_bwd", a=da, w=full["w_mlp_in"], nt=True, tm=512, tk=D_FF, rows=[x1, dx2], consts=[g2],
        row_outs=[(D_MODEL, F32)], acc_outs=[D_MODEL], epilogue=norm_bwd,
        rider=_Spread([bchunk("w_mlp_in", g_w_mlp_in)], chunked=True))
    (do_a, do_b, dgl, g_w_out, g_w_ud, g_w_us, g_bg), (parts["w_mlp_out"],) = _mixer_bwd(
        dx1, o_a, o_b, gl, bg, full["w_up_dil"], full["w_up_sb"], full["w_out"], 512,
        rider=_Spread([bchunk("w_mlp_out", g_w_mlp_out)], chunked=True))
    mix = _dil_mix_bwd(do_a, os_, lses)
    small_three = {"w_out": g_w_out, "w_up_sb": g_w_us, "w_up_dil": g_w_ud}
    grads, lands = _dil_bwd(views[0], mix[0], lses[0], mix[3], 0,
                            rider=_Spread([bchunk(n, g) for n, g in small_three.items()], chunked=True))
    parts.update(dict(zip(small_three, lands)))
    dil_b = [grads] + [_dil_bwd(views[g], mix[g], lses[g], mix[3 + g], g) for g in (1, 2)]
    dq_b, dk_b, dv_b = _sb_bwd(qkv, do_b, tot_b, sb_steps)
    dproj = [d[0] for d in dil_b] + [d[1] for d in dil_b] + [d[2] for d in dil_b] + [dq_b, dk_b, dv_b, dgl]
    g_w_in = jnp.concatenate([
        _grad_cols("grad_w_in_dil", h1, dproj[:9], tm=D_MODEL, tk=1024),
        _grad_cols("grad_w_in_sb", h1, dproj[9:12], tm=D_MODEL, tk=1024),
        _grad_cols("grad_w_in_gates", h1, dproj[12:], tm=D_MODEL, tk=1024)], axis=1)
    (grad_x, gg1), (parts["w_in"],) = _rowk(
        "in_proj_bwd", a=dproj, w=w_in_f, nt=True, tm=512, tk=IN_COLS, rows=[x0, dx1], consts=[g1],
        row_outs=[(D_MODEL, F32)], acc_outs=[D_MODEL], epilogue=norm_bwd,
        rider=_Spread([bchunk("w_in", g_w_in)], chunked=True))

    small_part = _pack_small({"norm_mix_g": gg1, "b_gate": g_bg, "norm_mlp_g": gg2, "norm_final_g": gg3}, loss_part)
    (small_parts,) = _exchange([jnp.broadcast_to(small_part[None], (N_DEV, 1, _SMALL_N))])

    tags = ("grad_", "delta_", "new_m_", "new_v_")
    outs = {}
    for n, p in parts.items():
        res = _reduce_adamw("adamw_" + n, p, w_shards[n], m_shards[n], v_shards[n], 256)
        for tag, val in zip(tags, res):
            outs[tag + n] = val.reshape(given[n].shape)
    small_w = _pack_small(given, jnp.zeros((1, LANES), F32))
    small_m = _pack_small({n: given["m_" + n] for n, _ in _SMALL}, jnp.zeros((1, LANES), F32))
    small_v = _pack_small({n: given["v_" + n] for n, _ in _SMALL}, jnp.ones((1, LANES), F32))
    small_res = _reduce_adamw("adamw_replicated", small_parts, small_w, small_m, small_v, 8)

    small_shapes = {n: given[n].shape for n, _ in _SMALL}
    for tag, small in zip(tags, small_res):
        small_vals, tail = _unpack_small(small, small_shapes)
        for n, val in small_vals.items():
            outs[tag + n] = val
        if tag == "grad_":
            loss = tail[0, 0]
    names = ["norm_mix_g", "w_in", "b_gate", "w_up_dil", "w_up_sb", "w_out", "norm_mlp_g", "w_mlp_in", "w_mlp_out",
             "norm_final_g"]
    return (loss, grad_x.reshape(x.shape), *[outs["grad_" + n] for n in names], *[outs["delta_" + n] for n in names],
            *[outs["new_m_" + n] for n in names], *[outs["new_v_" + n] for n in names])
```

```python
import functools
import math

import jax
import jax.numpy as jnp
from jax import lax
from jax.experimental import pallas as pl
from jax.experimental.pallas import tpu as pltpu

_pcall = pl.pallas_call

F32 = jnp.float32
BF16 = jnp.bfloat16

D_MODEL = 1024
HEAD_DIM = 64
DIL_GROUPS = ((128, 1), (512, 4), (2048, 16))
DIL_HEADS_PER_GROUP = 4
N_DIL_HEADS = 12
N_SB_HEADS = 8
DIL_WIDTH = 768
DIL_OUT_WIDTH = 256
SB_WIDTH = 512
D_FF = 4096
BLOCK = 128
RMS_EPS = 1e-6
NEG_INF = -1e30
QKV_COLS = 3 * DIL_WIDTH + 3 * SB_WIDTH
IN_COLS = QKV_COLS + 2 * D_MODEL
N_DEV = 8

ADAM_LR = 0.001
ADAM_B1 = 0.9
ADAM_B2 = 0.999
ADAM_EPS = 1e-08
ADAM_WD = 0.01
ADAM_STEP = 10

VMEM_LIMIT = 56 * 1024 * 1024
SB_TK = 256
LANES = 128

_ARB = pltpu.ARBITRARY


def _cparams(n_axes, **kw):
    return pltpu.CompilerParams(dimension_semantics=(_ARB,) * n_axes, vmem_limit_bytes=VMEM_LIMIT, **kw)


def _dot(a, b):
    return jnp.dot(a, b, preferred_element_type=F32)


def _dot_nt(a, b):
    return lax.dot_general(a, b, (((1,), (1,)), ((), ())), preferred_element_type=F32)


def _dot_tn(a, b):
    return lax.dot_general(a, b, (((0,), (0,)), ((), ())), preferred_element_type=F32)


def _split_hi_lo(x):
    hi = x.astype(BF16)
    lo = (x - hi.astype(F32)).astype(BF16)
    return hi, lo


def _dot_hi_lo(x, m):
    hi, lo = _split_hi_lo(x)
    return _dot(hi, m) + _dot(lo, m)


def _sigmoid(x):
    return 1.0 / (1.0 + jnp.exp(-x))


_HBM = pl.BlockSpec(memory_space=pltpu.HBM)
_MESH = pl.DeviceIdType.MESH


class _Spread:
    def __init__(self, srcs, chunked):
        self.srcs, self.chunked, self.n = list(srcs), chunked, len(srcs)

    def land_shapes(self):
        return [jax.ShapeDtypeStruct((N_DEV,) + (s.shape[1:] if self.chunked else s.shape), s.dtype) for s in self.srcs]

    def scratch(self):
        dma = pltpu.SemaphoreType.DMA
        return [dma((7 * self.n,)), dma((7 * self.n,)), dma((self.n,))]

    def copies(self, src_refs, land_refs, send_sems, recv_sems, local_sems):
        x, y, c = lax.axis_index("x"), lax.axis_index("y"), lax.axis_index("c")
        me = 4 * x + 2 * y + c
        out = []
        for a, (src, land) in enumerate(zip(src_refs, land_refs)):
            out.append(pltpu.make_async_copy(src.at[me] if self.chunked else src, land.at[me], local_sems.at[a]))
            for k in range(1, N_DEV):
                px, py, pc = x ^ (k >> 2), y ^ ((k >> 1) & 1), c ^ (k & 1)
                out.append(pltpu.make_async_remote_copy(
                    src_ref=src.at[4 * px + 2 * py + pc] if self.chunked else src, dst_ref=land.at[me],
                    send_sem=send_sems.at[7 * a + k - 1], recv_sem=recv_sems.at[7 * a + k - 1],
                    device_id=(px, py, pc), device_id_type=_MESH))
        return out


def _call(body, args, rider=None, **kw):
    if rider is None:
        return _pcall(body, **kw)(*args)
    grid = kw["grid"]
    single = not isinstance(kw["out_shape"], (list, tuple))
    out_specs = [kw["out_specs"]] if single else list(kw["out_specs"])
    out_shape = [kw["out_shape"]] if single else list(kw["out_shape"])
    in_specs, scratch = list(kw["in_specs"]), list(kw.get("scratch_shapes", []))
    n_in, n_out, n_s, n = len(in_specs), len(out_shape), len(scratch), rider.n

    def hosted(*refs):
        ins, srcs = refs[:n_in], refs[n_in:n_in + n]
        outs, lands = refs[n_in + n:n_in + n + n_out], refs[n_in + n + n_out:n_in + 2 * n + n_out]
        own_scratch, sems = refs[n_in + 2 * n + n_out:n_in + 2 * n + n_out + n_s], refs[n_in + 2 * n + n_out + n_s:]
        ids = [pl.program_id(d) for d in range(len(grid))]
        first = functools.reduce(jnp.logical_and, [i == 0 for i in ids])
        last = functools.reduce(jnp.logical_and, [i == g - 1 for i, g in zip(ids, grid)])
        copies = rider.copies(srcs, lands, *sems)

        @pl.when(first)
        def _():
            for cp in copies:
                cp.start()

        body(*ins, *outs, *own_scratch)

        @pl.when(last)
        def _():
            for cp in copies:
                cp.wait()

    kw = dict(kw, in_specs=in_specs + [_HBM] * n, out_specs=out_specs + [_HBM] * n,
              out_shape=out_shape + rider.land_shapes(), scratch_shapes=scratch + rider.scratch())
    res = _pcall(hosted, **kw)(*args, *rider.srcs)
    return (res[0] if single else list(res[:n_out])), list(res[n_out:])


def _mm(name, a, b, *, ta=False, tb=False, out_dtype, tm, tn, tk, epilogue=None, extra=None, rider=None):
    m = a.shape[1] if ta else a.shape[0]
    k = a.shape[0] if ta else a.shape[1]
    n = b.shape[0] if tb else b.shape[1]
    assert (b.shape[1] if tb else b.shape[0]) == k
    tm, tn, tk = min(tm, m), min(tn, n), min(tk, k)
    assert m % tm == 0 and n % tn == 0 and k % tk == 0, (name, m, n, k, tm, tn, tk)
    nk = k // tk
    dn = (((0 if ta else 1,), (1 if tb else 0,)), ((), ()))
    in_place = nk > 1 and epilogue is None and out_dtype == F32

    def body(*refs):
        if extra is not None:
            a_ref, b_ref, e_ref, o_ref = refs[:4]
        else:
            a_ref, b_ref, o_ref = refs[:3]
            e_ref = None

        def finish(r):
            if epilogue is not None:
                r = epilogue(r, None if e_ref is None else e_ref[...])
            o_ref[...] = r.astype(out_dtype)

        part = lax.dot_general(a_ref[...].astype(BF16), b_ref[...].astype(BF16), dn, preferred_element_type=F32)
        if nk == 1:
            finish(part)
        else:
            acc_ref = o_ref if in_place else refs[-1]
            kk = pl.program_id(2)

            @pl.when(kk == 0)
            def _():
                acc_ref[...] = part

            @pl.when(kk > 0)
            def _():
                acc_ref[...] += part

            if not in_place:

                @pl.when(kk == nk - 1)
                def _():
                    finish(acc_ref[...])

    a_spec = pl.BlockSpec((tk, tm), lambda j, i, kk: (kk, i)) if ta else pl.BlockSpec((tm, tk), lambda j, i, kk: (i, kk))
    b_spec = pl.BlockSpec((tn, tk), lambda j, i, kk: (j, kk)) if tb else pl.BlockSpec((tk, tn), lambda j, i, kk: (kk, j))
    o_spec = pl.BlockSpec((tm, tn), lambda j, i, kk: (i, j))
    in_specs = [a_spec, b_spec]
    args = [a, b]
    if extra is not None:
        in_specs.append(o_spec)
        args.append(extra)
    return _call(
        body, args, rider,
        name=name,
        grid=(n // tn, m // tm, nk),
        in_specs=in_specs,
        out_specs=o_spec,
        out_shape=jax.ShapeDtypeStruct((m, n), out_dtype),
        scratch_shapes=[pltpu.VMEM((tm, tn), F32)] if (nk > 1 and not in_place) else [],
        compiler_params=_cparams(3),
    )


def _grad_cols(name, a, parts, *, tm, tk, rider=None):
    k, m = a.shape
    n = sum(p.shape[1] for p in parts)
    assert m % tm == 0 and k % tk == 0
    nk = k // tk

    def body(*refs):
        a_ref, p_refs, o_ref, acc_ref = refs[0], refs[1:1 + len(parts)], refs[1 + len(parts)], refs[2 + len(parts)]
        kk = pl.program_id(1)
        side_by_side = jnp.concatenate([p_ref[...].astype(BF16) for p_ref in p_refs], axis=1)
        term = _dot_tn(a_ref[...].astype(BF16), side_by_side)

        @pl.when(kk == 0)
        def _():
            acc_ref[...] = term

        @pl.when(kk > 0)
        def _():
            acc_ref[...] += term

        @pl.when(kk == nk - 1)
        def _():
            o_ref[...] = acc_ref[...].astype(o_ref.dtype)

    return _call(
        body, [a] + list(parts), rider,
        name=name,
        grid=(m // tm, nk),
        in_specs=[pl.BlockSpec((tk, tm), lambda i, kk: (kk, i))]
        + [pl.BlockSpec((tk, p.shape[1]), lambda i, kk: (kk, 0)) for p in parts],
        out_specs=pl.BlockSpec((tm, n), lambda i, kk: (i, 0)),
        out_shape=jax.ShapeDtypeStruct((m, n), BF16),
        scratch_shapes=[pltpu.VMEM((tm, n), F32)],
        compiler_params=_cparams(2),
    )


def _rowk(name, *, a=None, w=None, nt=False, tm, tk=None, rows=(), consts=(), row_outs=(), acc_outs=(), epilogue,
          rider=None, in_place=None):
    has_mm = a is not None
    a_parts = list(a) if isinstance(a, (list, tuple)) else ([a] if has_mm else [])
    n_a = len(a_parts)
    m = a_parts[0].shape[0] if has_mm else rows[0].shape[0]
    assert m % tm == 0
    nm = m // tm
    if has_mm:
        k = sum(p.shape[1] for p in a_parts)
        n = w.shape[0] if nt else w.shape[1]
        tk = min(tk, k)
        assert k % tk == 0 and (n_a == 1 or tk == k)
        nk = k // tk
    else:
        nk = 1
    n_rows, n_consts, n_ro, n_ao = len(rows), len(consts), len(row_outs), len(acc_outs)

    def body(*refs):
        pos = 0
        if has_mm:
            a_refs, w_ref = refs[:n_a], refs[n_a]
            pos = n_a + 1
        row_refs = refs[pos:pos + n_rows]
        pos += n_rows
        const_refs = refs[pos:pos + n_consts]
        pos += n_consts
        ro_refs = refs[pos:pos + n_ro]
        pos += n_ro
        ao_refs = refs[pos:pos + n_ao]
        pos += n_ao
        i = pl.program_id(0)
        kk = pl.program_id(1)

        def finish(acc):
            ro_vals, ao_vals = epilogue(acc, [r[...] for r in row_refs], [c[...] for c in const_refs])
            for r, v in zip(ro_refs, ro_vals):
                r[...] = v.astype(r.dtype)
            for r, v in zip(ao_refs, ao_vals):

                @pl.when(i == 0)
                def _(r=r, v=v):
                    r[...] = v

                @pl.when(i > 0)
                def _(r=r, v=v):
                    r[...] += v

        if not has_mm:
            finish(None)
            return
        part, off = None, 0
        for a_ref in a_refs:
            width = a_ref.shape[1]
            cols = slice(None) if n_a == 1 else slice(off, off + width)
            av = a_ref[...].astype(BF16)
            term = _dot_nt(av, w_ref[:, cols]) if nt else _dot(av, w_ref[cols, :])
            part = term if part is None else part + term
            off += width
        if nk == 1:
            finish(part)
        else:
            acc_ref = refs[pos]

            @pl.when(kk == 0)
            def _():
                acc_ref[...] = part

            @pl.when(kk > 0)
            def _():
                acc_ref[...] += part

            @pl.when(kk == nk - 1)
            def _():
                finish(acc_ref[...])

    once = pl.Buffered(1)
    in_specs, args = [], []
    if has_mm:
        for part in a_parts:
            in_specs.append(pl.BlockSpec((tm, tk if n_a == 1 else part.shape[1]), lambda i, kk: (i, kk)))
        w_mode = once if nk == 1 else None
        in_specs.append(pl.BlockSpec((n, tk), lambda i, kk: (0, kk), pipeline_mode=w_mode) if nt
                        else pl.BlockSpec((tk, n), lambda i, kk: (kk, 0), pipeline_mode=w_mode))
        args += a_parts + [w]
    for r in rows:
        in_specs.append(pl.BlockSpec((tm, r.shape[1]), lambda i, kk: (i, 0)))
        args.append(r)
    for c in consts:
        in_specs.append(pl.BlockSpec(c.shape, lambda i, kk: (0,) * c.ndim, pipeline_mode=once))
        args.append(c)
    out_specs, out_shape = [], []
    for width, dt in row_outs:
        out_specs.append(pl.BlockSpec((tm, width), lambda i, kk: (i, 0)))
        out_shape.append(jax.ShapeDtypeStruct((m, width), dt))
    for width in acc_outs:
        out_specs.append(pl.BlockSpec((1, width), lambda i, kk: (0, 0)))
        out_shape.append(jax.ShapeDtypeStruct((1, width), F32))
    aliases = {}
    if in_place is not None:
        row, out = in_place
        assert (rows[row].shape[1], rows[row].dtype) == row_outs[out]
        aliases = {(n_a + 1 if has_mm else 0) + row: out}
    return _call(
        body, args, rider,
        name=name,
        input_output_aliases=aliases,
        grid=(nm, nk),
        in_specs=in_specs,
        out_specs=out_specs,
        out_shape=out_shape,
        scratch_shapes=[pltpu.VMEM((tm, n), F32)] if (has_mm and nk > 1) else [],
        compiler_params=_cparams(2),
    )


def _rms_stats(x):
    r = lax.rsqrt(jnp.mean(x * x, axis=-1, keepdims=True) + RMS_EPS)
    return r, x * r


def _rms_bwd(dh, xh, r, g):
    gy = dh * g
    dx = r * (gy - xh * jnp.mean(gy * xh, axis=-1, keepdims=True))
    return dx, jnp.sum(dh * xh, axis=0, keepdims=True)


def _alibi_slope(head):
    return 2.0 ** (-8.0 * (head + 1) / N_DIL_HEADS)


DIL_STEP_BLOCKS = 4


def _dil_band(first_block):
    qi = lax.broadcasted_iota(jnp.int32, (BLOCK, 2 * BLOCK), 0)
    kj = lax.broadcasted_iota(jnp.int32, (BLOCK, 2 * BLOCK), 1)
    steps = qi + BLOCK - kj
    valid = (steps >= 0) & (steps <= BLOCK)
    if first_block is not False:
        valid = valid & ((kj >= BLOCK) | jnp.logical_not(first_block))
    return steps.astype(F32), valid


def _dil_step_specs(ncb, cols, nblk, clamp):
    def own(col):
        return pl.BlockSpec((nblk * BLOCK, DIL_OUT_WIDTH), lambda r, i: (clamp(i), r * ncb + col))

    def before(col):
        return pl.BlockSpec((BLOCK, DIL_OUT_WIDTH), lambda r, i: (jnp.maximum(clamp(i) * nblk - 1, 0), r * ncb + col))

    return [own(cols[0]), own(cols[1]), before(cols[1]), own(cols[2]), before(cols[2])]


DIL_RELAYOUT_ROWS = 1024


def _view_scratch(width):
    return pltpu.VMEM((width // LANES, DIL_RELAYOUT_ROWS, LANES), F32)


def _rows_from_view(src, scr, d, w):
    sub = src.shape[0]
    for j in range(w // LANES):
        for r in range(d):
            scr[j, pl.ds(r, sub, stride=d), :] = src[:, r * w + j * LANES:r * w + (j + 1) * LANES].astype(F32)


def _rows_to_view(scr, dst, d, w):
    sub = dst.shape[0]
    for j in range(w // LANES):
        for r in range(d):
            dst[:, r * w + j * LANES:r * w + (j + 1) * LANES] = scr[j, pl.ds(r, sub, stride=d), :].astype(dst.dtype)


def _dil_relayout(name, xs, dilation, to_view, col_block=0, width=None):
    d = dilation
    tm = DIL_RELAYOUT_ROWS
    rows = tm // d
    if to_view:
        s = xs[0].shape[0]
        widths = [width or x.shape[1] for x in xs]
    else:
        s = xs[0].shape[0] * d
        widths = [v.shape[1] // d for v in xs]
    assert s % tm == 0 and all(w % LANES == 0 for w in widths) and all(x.dtype == BF16 for x in xs)
    n = len(xs)
    blk = 256
    per = blk // d
    assert per % 16 == 0 and tm % blk == 0

    def body(*refs):
        in_refs, out_refs = refs[:n], refs[n:]
        i0 = lax.broadcasted_iota(jnp.int32, (blk, blk), 0)
        i1 = lax.broadcasted_iota(jnp.int32, (blk, blk), 1)
        sort = (i1 == (i0 % per) * d + i0 // per) if to_view else (i0 == (i1 % per) * d + i1 // per)
        sort = jnp.where(sort, 1.0, 0.0).astype(BF16)
        for src, dst, w in zip(in_refs, out_refs, widths):
            for b in range(tm // blk):
                if to_view:
                    y = _dot(sort, src[b * blk:(b + 1) * blk, :]).astype(BF16)
                    for r in range(d):
                        dst[b * per:(b + 1) * per, r * w:(r + 1) * w] = y[r * per:(r + 1) * per, :]
                else:
                    by_residue = jnp.concatenate(
                        [src[b * per:(b + 1) * per, r * w:(r + 1) * w] for r in range(d)], axis=0)
                    dst[b * blk:(b + 1) * blk, :] = _dot(sort, by_residue).astype(BF16)

    natural = [pl.BlockSpec((tm, w), lambda i: (i, col_block)) for w in widths]
    viewed = [pl.BlockSpec((rows, d * w), lambda i: (i, 0)) for w in widths]
    return _pcall(
        body,
        name=name,
        grid=(s // tm,),
        in_specs=natural if to_view else viewed,
        out_specs=viewed if to_view else natural,
        out_shape=[jax.ShapeDtypeStruct((s // d, d * w) if to_view else (s, w), BF16) for w in widths],
        compiler_params=_cparams(1),
    )(*xs)


def _dil_fwd(view, group):
    window, dilation = DIL_GROUPS[group]
    qkv_v, ncb, cols = view
    sub = qkv_v.shape[0]
    s = sub * dilation
    nb = sub // BLOCK
    assert nb * BLOCK * dilation == s and window // dilation == BLOCK
    nblk = min(DIL_STEP_BLOCKS, nb)
    assert nb % nblk == 0
    slopes = [_alibi_slope(group * DIL_HEADS_PER_GROUP + h) * dilation for h in range(DIL_HEADS_PER_GROUP)]

    def body(q_ref, kc_ref, kp_ref, vc_ref, vp_ref, o_ref, lse_ref):
        i = pl.program_id(1)
        kk_all = jnp.concatenate([kp_ref[...], kc_ref[...]], axis=0)
        vv_all = jnp.concatenate([vp_ref[...], vc_ref[...]], axis=0)
        head_id = lax.broadcasted_iota(jnp.int32, (1, DIL_OUT_WIDTH), 1) // HEAD_DIM
        chains = [(b, h) for b in range(nblk) for h in range(DIL_HEADS_PER_GROUP)]
        rows = lambda b: slice(b * BLOCK, (b + 1) * BLOCK)
        keys = lambda b: slice(b * BLOCK, (b + 2) * BLOCK)
        bands = [_dil_band(i == 0 if b == 0 else False) for b in range(nblk)]
        qs = [q_ref[rows(b), :] for b in range(nblk)]
        scores = [_dot_nt(jnp.where(head_id == h, qs[b], jnp.zeros_like(qs[b])), kk_all[keys(b)]) for b, h in chains]
        ps, lses = [], []
        for (b, h), sc in zip(chains, scores):
            steps, valid = bands[b]
            logits = jnp.where(valid, sc * (1.0 / math.sqrt(HEAD_DIM)) - slopes[h] * steps, NEG_INF)
            mx = jnp.max(logits, axis=1, keepdims=True)
            e = jnp.exp(logits - mx)
            den = jnp.sum(e, axis=1, keepdims=True)
            lses.append(mx + jnp.log(den))
            ps.append((e * (1.0 / den)).astype(BF16))
        outs = [_dot(p, vv_all[keys(b)]) for (b, h), p in zip(chains, ps)]
        for b in range(nblk):
            mine = [n for n, ch in enumerate(chains) if ch[0] == b]
            o, lse_all = outs[mine[0]], lses[mine[0]]
            for n in mine[1:]:
                o = jnp.where(head_id == chains[n][1], outs[n], o)
                lse_all = jnp.where(head_id == chains[n][1], lses[n], lse_all)
            o_ref[rows(b), :] = o
            lse_ref[rows(b), :] = jnp.broadcast_to(lse_all, o.shape)

    out_spec = pl.BlockSpec((nblk * BLOCK, DIL_OUT_WIDTH), lambda r, i: (i, r))
    o, lse = _pcall(
        body,
        name=f"dil_fwd_g{group}",
        grid=(dilation, nb // nblk),
        in_specs=_dil_step_specs(ncb, cols, nblk, lambda i: i),
        out_specs=[out_spec, out_spec],
        out_shape=[jax.ShapeDtypeStruct((sub, dilation * DIL_OUT_WIDTH), F32)] * 2,
        compiler_params=_cparams(2),
    )(qkv_v, qkv_v, qkv_v, qkv_v, qkv_v)
    return o, lse


def _dil_bwd(view, do_g, lse_g, dterm_g, group, rider=None):
    window, dilation = DIL_GROUPS[group]
    qkv_v, ncb, cols = view
    sub = qkv_v.shape[0]
    nb = sub // BLOCK
    nblk = min(DIL_STEP_BLOCKS, nb)
    n_steps = nb // nblk
    slopes = [_alibi_slope(group * DIL_HEADS_PER_GROUP + h) * dilation for h in range(DIL_HEADS_PER_GROUP)]
    scale = 1.0 / math.sqrt(HEAD_DIM)
    tail = slice((nblk - 1) * BLOCK, nblk * BLOCK)
    single = n_steps == 1

    def body(q_ref, kc_ref, kp_ref, vc_ref, vp_ref, do_ref, lse_ref, dt_ref, dq_ref, dk_ref, dv_ref, *carry_refs):
        i = pl.program_id(1)

        def init():
            for carry_ref in carry_refs:
                carry_ref[...] = jnp.zeros_like(carry_ref)

        def compute():
            kk_all = jnp.concatenate([kp_ref[...], kc_ref[...]], axis=0)
            vv_all = jnp.concatenate([vp_ref[...], vc_ref[...]], axis=0)
            lane = lax.broadcasted_iota(jnp.int32, (1, DIL_OUT_WIDTH), 1)
            head_id = lane // HEAD_DIM
            chains = [(b, h) for b in range(nblk) for h in range(DIL_HEADS_PER_GROUP)]
            rows = lambda b: slice(b * BLOCK, (b + 1) * BLOCK)
            keys = lambda b: slice(b * BLOCK, (b + 2) * BLOCK)
            bands = [_dil_band(i == 0 if b == 0 else False) for b in range(nblk)]
            qms, doms = [], []
            for b, h in chains:
                q, do = q_ref[rows(b), :], do_ref[rows(b), :]
                qms.append(jnp.where(head_id == h, q, jnp.zeros_like(q)))
                doms.append(jnp.where(head_id == h, do, jnp.zeros_like(do)))
            scores = [_dot_nt(qm, kk_all[keys(b)]) for (b, h), qm in zip(chains, qms)]
            dps = [_dot_nt(dom, vv_all[keys(b)]) for (b, h), dom in zip(chains, doms)]
            pbs, dss = [], []
            for n, (b, h) in enumerate(chains):
                steps, valid = bands[b]
                first = lane == h * HEAD_DIM
                lse = jnp.sum(jnp.where(first, lse_ref[rows(b), :], 0.0), axis=1, keepdims=True)
                dt = jnp.sum(jnp.where(first, dt_ref[rows(b), :], 0.0), axis=1, keepdims=True)
                logits = jnp.where(valid, scores[n] * scale - slopes[h] * steps, NEG_INF)
                p = jnp.where(valid, jnp.exp(logits - lse), 0.0)
                pbs.append(p.astype(BF16))
                dss.append((p * (dps[n] + dt) * scale).astype(BF16))
            dqs = [_dot(ds, kk_all[keys(b)]) for (b, h), ds in zip(chains, dss)]
            dks = [_dot_tn(ds, qm) for ds, qm in zip(dss, qms)]
            dvs = [_dot_tn(pb, dom) for pb, dom in zip(pbs, doms)]
            dkk, dvv = [], []
            for b in range(nblk):
                mine = [n for n, ch in enumerate(chains) if ch[0] == b]
                dq = dqs[mine[0]]
                for n in mine[1:]:
                    dq = jnp.where(head_id == chains[n][1], dqs[n], dq)
                dq_ref[rows(b), :] = dq.astype(dq_ref.dtype)
                dkk.append((dks[mine[0]] + dks[mine[1]]) + (dks[mine[2]] + dks[mine[3]]))
                dvv.append((dvs[mine[0]] + dvs[mine[1]]) + (dvs[mine[2]] + dvs[mine[3]]))
            for n, (out_ref, parts) in enumerate(((dk_ref, dkk), (dv_ref, dvv))):
                done = [parts[b][BLOCK:] + parts[b + 1][:BLOCK] if b + 1 < nblk else parts[b][BLOCK:]
                        for b in range(nblk)]
                if single:
                    for b in range(nblk):
                        out_ref[rows(b), :] = done[b].astype(out_ref.dtype)
                    continue
                carry_ref = carry_refs[n]
                if nblk > 1:
                    out_ref[: (nblk - 1) * BLOCK, :] = carry_ref[: (nblk - 1) * BLOCK, :].astype(out_ref.dtype)
                out_ref[tail, :] = (carry_ref[tail, :] + parts[0][:BLOCK]).astype(out_ref.dtype)
                for b in range(nblk):
                    carry_ref[rows(b), :] = done[b]

        def flush():
            for out_ref, carry_ref in zip((dk_ref, dv_ref), carry_refs):
                out_ref[...] = carry_ref[...].astype(out_ref.dtype)

        if single:
            compute()
        else:
            pl.when(i == 0)(init)
            pl.when(i < n_steps)(compute)
            pl.when(i == n_steps)(flush)

    clamp = lambda i: jnp.minimum(i, n_steps - 1)
    row_spec = pl.BlockSpec((nblk * BLOCK, DIL_OUT_WIDTH), lambda r, i: (clamp(i), r))
    late_spec = pl.BlockSpec((nblk * BLOCK, DIL_OUT_WIDTH), lambda r, i: (jnp.maximum(i - 1, 0), r))
    res = _call(
        body, (qkv_v, qkv_v, qkv_v, qkv_v, qkv_v, do_g, lse_g, dterm_g), rider,
        name=f"dil_bwd_g{group}",
        grid=(dilation, n_steps + (0 if single else 1)),
        in_specs=_dil_step_specs(ncb, cols, nblk, clamp) + [row_spec, row_spec, row_spec],
        out_specs=[row_spec, row_spec, row_spec] if single else [row_spec, late_spec, late_spec],
        out_shape=[jax.ShapeDtypeStruct((sub, dilation * DIL_OUT_WIDTH), BF16)] * 3,
        scratch_shapes=[] if single else [pltpu.VMEM((nblk * BLOCK, DIL_OUT_WIDTH), F32)] * 2,
        compiler_params=_cparams(2),
    )
    grads, lands = res if rider is not None else (res, None)
    if dilation > 1:
        grads = _dil_relayout(f"dil_bwd_rows_g{group}", list(grads), dilation, to_view=False)
    return tuple(grads) if rider is None else (tuple(grads), lands)


def _dil_view(qkv, group):
    _, dilation = DIL_GROUPS[group]
    w = DIL_OUT_WIDTH
    if dilation == 1:
        return qkv, QKV_COLS // w, (3 * group, 3 * group + 1, 3 * group + 2)
    (own,) = _dil_relayout(f"dil_view_g{group}", [qkv], dilation, to_view=True, col_block=group, width=3 * w)
    return own, 3, (0, 1, 2)


def _group_major(w_qkv):
    w = DIL_OUT_WIDTH
    ng = len(DIL_GROUPS)
    cols = [w_qkv[:, (part * ng + g) * w:(part * ng + g + 1) * w] for g in range(ng) for part in range(3)]
    return jnp.concatenate(cols + [w_qkv[:, 3 * DIL_WIDTH:]], axis=1)


def _head_block_ones():
    r = lax.broadcasted_iota(jnp.int32, (DIL_OUT_WIDTH, DIL_OUT_WIDTH), 0) // HEAD_DIM
    c = lax.broadcasted_iota(jnp.int32, (DIL_OUT_WIDTH, DIL_OUT_WIDTH), 1) // HEAD_DIM
    return jnp.where(r == c, 1.0, 0.0).astype(BF16)


def _dil_mix_weights(l0, l1, l2):
    mx = jnp.maximum(jnp.maximum(l0, l1), l2)
    e0, e1, e2 = jnp.exp(l0 - mx), jnp.exp(l1 - mx), jnp.exp(l2 - mx)
    inv = 1.0 / (e0 + e1 + e2)
    return e0 * inv, e1 * inv, e2 * inv


def _dil_view_spec(dilation):
    return pl.BlockSpec((DIL_RELAYOUT_ROWS // dilation, dilation * DIL_OUT_WIDTH), lambda i: (i, 0))


def _dil_mix_call(name, body, s, ins, in_specs, outs, n_relaid):
    out_specs = [_dil_view_spec(d or 1) for d, _ in outs]
    out_shape = [jax.ShapeDtypeStruct((s // (d or 1), (d or 1) * DIL_OUT_WIDTH), dt) for d, dt in outs]
    return _pcall(
        body,
        name=name,
        grid=(s // DIL_RELAYOUT_ROWS,),
        in_specs=in_specs,
        out_specs=out_specs,
        out_shape=out_shape,
        scratch_shapes=[_view_scratch(DIL_OUT_WIDTH)] * n_relaid,
        compiler_params=_cparams(1),
    )(*ins)


DIL_MIX_CHUNK = 64
_DIL_SLABS = DIL_OUT_WIDTH // LANES


def _dil_rows(refs, scratch):
    dils = [d for _, d in DIL_GROUPS]
    assert dils[0] == 1
    readers = [lambda rows, j, ref=refs[0]: ref[rows, j * LANES:(j + 1) * LANES]]
    for ref, scr, d in zip(refs[1:], scratch, dils[1:]):
        _rows_from_view(ref, scr, d, DIL_OUT_WIDTH)
        readers.append(lambda rows, j, scr=scr: scr[j, rows, :])
    return readers


def _dil_mix_chunks(step):
    def chunk(c, carry):
        step(pl.ds(pl.multiple_of(c * DIL_MIX_CHUNK, DIL_MIX_CHUNK), DIL_MIX_CHUNK))
        return carry

    lax.fori_loop(0, DIL_RELAYOUT_ROWS // DIL_MIX_CHUNK, chunk, 0, unroll=4)


def _dil_mix_fwd(os_, lses):
    ng = len(DIL_GROUPS)
    s = os_[0].shape[0]

    def body(*refs):
        o_refs, l_refs, out_ref, scratch = refs[:ng], refs[ng:2 * ng], refs[2 * ng], refs[2 * ng + 1:]
        o_at = _dil_rows(o_refs, scratch[:ng - 1])
        l_at = _dil_rows(l_refs, scratch[ng - 1:])

        def step(rows):
            for j in range(_DIL_SLABS):
                w0, w1, w2 = _dil_mix_weights(*[at(rows, j) for at in l_at])
                o0, o1, o2 = [at(rows, j) for at in o_at]
                out_ref[rows, j * LANES:(j + 1) * LANES] = (w0 * o0 + w1 * o1 + w2 * o2).astype(out_ref.dtype)

        _dil_mix_chunks(step)

    specs = [_dil_view_spec(d) for _, d in DIL_GROUPS]
    (o_a,) = _dil_mix_call("dil_mix_fwd", body, s, list(os_) + list(lses), specs * 2, [(None, BF16)], 2 * (ng - 1))
    return o_a


def _dil_mix_bwd(do_a, os_, lses):
    ng = len(DIL_GROUPS)
    s = do_a.shape[0]
    dils = [d for _, d in DIL_GROUPS]

    def body(*refs):
        do_ref, o_refs, l_refs = refs[0], refs[1:1 + ng], refs[1 + ng:1 + 2 * ng]
        out_refs, scratch = refs[1 + 2 * ng:1 + 4 * ng], refs[1 + 4 * ng:]
        o_at = _dil_rows(o_refs, scratch[:ng - 1])
        l_at = _dil_rows(l_refs, scratch[ng - 1:2 * (ng - 1)])
        spare = iter(scratch[2 * (ng - 1):])
        staged = [None if dils[n % ng] == 1 else next(spare) for n in range(2 * ng)]
        ones = _head_block_ones()

        def step(rows):
            do = do_ref[rows, :].astype(F32)
            ws, prods = [], []
            for j in range(_DIL_SLABS):
                w0, w1, w2 = _dil_mix_weights(*[at(rows, j) for at in l_at])
                o0, o1, o2 = [at(rows, j) for at in o_at]
                ws.append((w0, w1, w2))
                prods.append(do[:, j * LANES:(j + 1) * LANES] * (w0 * o0 + w1 * o1 + w2 * o2))
            tot = _dot_hi_lo(jnp.concatenate(prods, axis=1), ones)
            for j in range(_DIL_SLABS):
                slab = slice(j * LANES, (j + 1) * LANES)
                vals = [w * do[:, slab] for w in ws[j]] + [-w * tot[:, slab] for w in ws[j]]
                for val, dst, scr in zip(vals, out_refs, staged):
                    if scr is None:
                        dst[rows, slab] = val.astype(dst.dtype)
                    else:
                        scr[j, rows, :] = val

        _dil_mix_chunks(step)
        for n, (dst, scr) in enumerate(zip(out_refs, staged)):
            if scr is not None:
                _rows_to_view(scr, dst, dils[n % ng], DIL_OUT_WIDTH)

    specs = [_dil_view_spec(d) for d in dils]
    return _dil_mix_call(
        "dil_mix_bwd", body, s, [do_a] + list(os_) + list(lses), [_dil_view_spec(1)] + specs * 2,
        [(d, BF16) for d in dils] + [(d, F32) for d in dils], 4 * (ng - 1))


_SB_Q0 = 3 * DIL_WIDTH // LANES
_SB_K0 = _SB_Q0 + SB_WIDTH // LANES
_SB_V0 = _SB_K0 + SB_WIDTH // LANES


_EXP_CLAMP = 88.0
_SB_DEAD = 104.0


def _tri(t, op):
    r = lax.broadcasted_iota(jnp.int32, (t, t), 0)
    c = lax.broadcasted_iota(jnp.int32, (t, t), 1)
    return jnp.where(op(r, c), 1.0, 0.0).astype(BF16)


def _softplus(z):
    return jnp.maximum(z, jnp.log(1.0 + jnp.exp(jnp.minimum(z, _EXP_CLAMP))))


def _sb_chain_head(qm, kj, mask):
    z = _dot_nt(qm, kj)
    sp = _softplus(z)
    return (sp if mask is None else jnp.where(mask, sp, 0.0)), z - sp


def _sb_fwd(qkv, rider=None):
    s = qkv.shape[0]
    t = SB_TK
    assert s % (2 * t) == 0
    nq = s // (2 * t)
    n_pairs = SB_WIDTH // LANES

    def body(q_ref, k_ref, v_ref, o_ref, tot_ref, steps_ref):
        p, i = pl.program_id(0), pl.program_id(1)
        lane_hi = lax.broadcasted_iota(jnp.int32, (1, LANES), 1) // HEAD_DIM
        later = _tri(t, lambda r, c: r > c)
        causal = lax.broadcasted_iota(jnp.int32, (t, t), 1) < lax.broadcasted_iota(jnp.int32, (t, t), 0)
        qms = []
        for x in range(2):
            q = q_ref[pl.ds(x * t, t), :] * (1.0 / math.sqrt(HEAD_DIM))
            qms.append([jnp.where(lane_hi == hh, q, jnp.zeros_like(q)) for hh in range(2)])

        def tile(j):
            off = pl.multiple_of(j * t, t)
            return k_ref[pl.ds(off, t), :], v_ref[pl.ds(off, t), :]

        def step(groups, carry):
            kv = [tile(j) for _, j, _ in groups]
            chains = [(g, x, hh) for g, (x, _, _) in enumerate(groups) for hh in range(2)]
            heads = [_sb_chain_head(qms[x][hh], kv[g][0], causal if groups[g][2] else None) for g, x, hh in chains]
            sufs = [_dot(sp.astype(BF16), later) for sp, _ in heads]
            cur = [list(carry[0]), list(carry[1])]
            for (g, x, hh), (sp, lpos), suf in zip(chains, heads, sufs):
                c, acc = cur[x][hh]
                a = jnp.exp(lpos - suf - c)
                if groups[g][2]:
                    a = jnp.where(causal, a, 0.0)
                cur[x][hh] = (c + jnp.sum(sp, axis=1, keepdims=True), acc + _dot(a.astype(BF16), kv[g][1]))
            return (tuple(cur[0]), tuple(cur[1]))

        def lowest(carry):
            return jnp.min(jnp.minimum(jnp.minimum(carry[0][0][0], carry[0][1][0]),
                                       jnp.minimum(carry[1][0][0], carry[1][1][0])))

        zero = (jnp.zeros((t, 1), F32), jnp.zeros((t, LANES), F32))
        start = ((zero, zero), (zero, zero))
        carry = lax.cond(
            i == 0,
            lambda ca: step([(0, 0, True), (1, 1, True), (1, 0, False)], ca),
            lambda ca: step([(0, 2 * i, True), (1, 2 * i + 1, True), (0, 2 * i - 1, False), (1, 2 * i, False)], ca),
            start)

        def walk(state):
            n, ca, _ = state
            ca = step([(0, 2 * i - 2 - n, False), (1, 2 * i - 1 - n, False)], ca)
            return n + 1, ca, lowest(ca)

        n_more, carry, low = lax.while_loop(
            lambda st: jnp.logical_and(st[0] + 1 < 2 * i, st[2] <= _SB_DEAD), walk, (jnp.int32(0), carry, lowest(carry)))
        b_last = jnp.logical_and(jnp.logical_and(i > 0, n_more + 1 == 2 * i), low <= _SB_DEAD)
        carry = lax.cond(b_last, lambda ca: step([(1, 0, False)], ca), lambda ca: ca, carry)
        for x in range(2):
            (c0, acc0), (c1, acc1) = carry[x]
            o_ref[pl.ds(x * t, t), :] = jnp.where(lane_hi == 0, acc0, acc1).astype(o_ref.dtype)
            tot_ref[pl.ds(x * t, t), :] = jnp.where(lane_hi == 0, c0, c1)
        steps_ref[p, i] = 1 + n_more + b_last.astype(jnp.int32)

    return _call(
        body, (qkv, qkv, qkv), rider,
        name="sb_fwd",
        grid=(n_pairs, nq),
        in_specs=[
            pl.BlockSpec((2 * t, LANES), lambda p, i: (i, _SB_Q0 + p)),
            pl.BlockSpec((s, LANES), lambda p, i: (0, _SB_K0 + p)),
            pl.BlockSpec((s, LANES), lambda p, i: (0, _SB_V0 + p)),
        ],
        out_specs=[pl.BlockSpec((2 * t, LANES), lambda p, i: (i, p))] * 2 + [pl.BlockSpec(memory_space=pltpu.SMEM)],
        out_shape=[jax.ShapeDtypeStruct((s, SB_WIDTH), BF16), jax.ShapeDtypeStruct((s, SB_WIDTH), F32),
                   jax.ShapeDtypeStruct((n_pairs, nq), jnp.int32)],
        compiler_params=_cparams(2),
    )


def _sb_bwd(qkv, do_b, tot_b, n_steps):
    s = qkv.shape[0]
    t = SB_TK
    nq = s // (2 * t)
    n_pairs = SB_WIDTH // LANES
    scale = 1.0 / math.sqrt(HEAD_DIM)

    def body(steps_ref, q_ref, k_ref, v_ref, do_ref, tot_ref, dq_ref, dk_ref, dv_ref):
        p, i = pl.program_id(0), pl.program_id(1)

        @pl.when(i == 0)
        def _():
            dk_ref[...] = jnp.zeros_like(dk_ref)
            dv_ref[...] = jnp.zeros_like(dv_ref)

        lane = lax.broadcasted_iota(jnp.int32, (1, LANES), 1)
        lane_hi = lane // HEAD_DIM
        later = _tri(t, lambda r, c: r > c)
        before = _tri(t, lambda r, c: r < c)
        causal = lax.broadcasted_iota(jnp.int32, (t, t), 1) < lax.broadcasted_iota(jnp.int32, (t, t), 0)
        qms, doms, tots = [], [], []
        for x in range(2):
            rows = pl.ds(x * t, t)
            q, do, tot_all = q_ref[rows, :] * scale, do_ref[rows, :], tot_ref[rows, :]
            qms.append([jnp.where(lane_hi == hh, q, jnp.zeros_like(q)) for hh in range(2)])
            doms.append([jnp.where(lane_hi == hh, do, jnp.zeros_like(do)) for hh in range(2)])
            tots.append([jnp.sum(jnp.where(lane == hh * HEAD_DIM, tot_all, 0.0), axis=1, keepdims=True)
                         for hh in range(2)])

        def step(groups, carry):
            offs = [pl.multiple_of(j * t, t) for _, j, _ in groups]
            ks = [k_ref[pl.ds(off, t), :] for off in offs]
            vs = [v_ref[pl.ds(off, t), :] for off in offs]
            chains = [(g, x, hh) for g, (x, _, _) in enumerate(groups) for hh in range(2)]
            heads = [_sb_chain_head(qms[x][hh], ks[g], causal if groups[g][2] else None) for g, x, hh in chains]
            sufs = [_dot(sp.astype(BF16), later) for sp, _ in heads]
            das = [_dot_nt(doms[x][hh], vs[g]) for g, x, hh in chains]
            cur = [list(carry[0]), list(carry[1])]
            sigs, gs, abs_, cg_before = [], [], [], []
            for (g_, x, hh), (sp, lpos), suf, da in zip(chains, heads, sufs, das):
                cl, cg, dq = cur[x][hh]
                cl = cl + jnp.sum(sp, axis=1, keepdims=True)
                sig = jnp.exp(lpos)
                a = sig * jnp.exp(-suf - (tots[x][hh] - cl))
                if groups[g_][2]:
                    a = jnp.where(causal, a, 0.0)
                g = a * da
                sigs.append(sig)
                gs.append(g)
                abs_.append(a.astype(BF16))
                cg_before.append(cg)
                cur[x][hh] = (cl, cg + jnp.sum(g, axis=1, keepdims=True), dq)
            prefs = [_dot(g.astype(BF16), before) for g in gs]
            dvs = [_dot_tn(ab, doms[x][hh]) for (_, x, hh), ab in zip(chains, abs_)]
            dzs = []
            for (g_, x, hh), sig, g, pref, cg in zip(chains, sigs, gs, prefs, cg_before):
                dz = g - sig * (g + pref + cg)
                if groups[g_][2]:
                    dz = jnp.where(causal, dz, 0.0)
                dzs.append(dz.astype(BF16))
            dqs = [_dot(dz, ks[g_]) for (g_, x, hh), dz in zip(chains, dzs)]
            dks = [_dot_tn(dz, qms[x][hh]) for (_, x, hh), dz in zip(chains, dzs)]
            for n, (_, x, hh) in enumerate(chains):
                cl, cg, dq = cur[x][hh]
                cur[x][hh] = (cl, cg, dq + dqs[n])
            for g_, off in enumerate(offs):
                dk_ref[pl.ds(off, t), :] += dks[2 * g_] + dks[2 * g_ + 1]
                dv_ref[pl.ds(off, t), :] += dvs[2 * g_] + dvs[2 * g_ + 1]
            return (tuple(cur[0]), tuple(cur[1]))

        taken = steps_ref[p, i]
        n_full = jnp.minimum(taken, 2 * i)
        zero = (jnp.zeros((t, 1), F32), jnp.zeros((t, 1), F32), jnp.zeros((t, LANES), F32))
        carry = ((zero, zero), (zero, zero))
        carry = lax.cond(jnp.logical_and(i > 0, taken > 2 * i), lambda ca: step([(1, 0, False)], ca), lambda ca: ca,
                         carry)
        carry = lax.fori_loop(
            0, n_full - 1,
            lambda n, ca: step([(0, 2 * i - n_full + n, False), (1, 2 * i + 1 - n_full + n, False)], ca), carry)
        carry = lax.cond(
            i == 0,
            lambda ca: step([(1, 0, False), (0, 0, True), (1, 1, True)], ca),
            lambda ca: step([(0, 2 * i - 1, False), (1, 2 * i, False), (0, 2 * i, True), (1, 2 * i + 1, True)], ca),
            carry)
        for x in range(2):
            dq = jnp.where(lane_hi == 0, carry[x][0][2], carry[x][1][2])
            dq_ref[pl.ds(x * t, t), :] = (dq * scale).astype(dq_ref.dtype)

    row_spec = pl.BlockSpec((2 * t, LANES), lambda p, i, ns: (i, p))
    full_spec = pl.BlockSpec((s, LANES), lambda p, i, ns: (0, p))
    return _pcall(
        body,
        name="sb_bwd",
        grid_spec=pltpu.PrefetchScalarGridSpec(
            num_scalar_prefetch=1,
            grid=(n_pairs, nq),
            in_specs=[
                pl.BlockSpec((2 * t, LANES), lambda p, i, ns: (i, _SB_Q0 + p)),
                pl.BlockSpec((s, LANES), lambda p, i, ns: (0, _SB_K0 + p)),
                pl.BlockSpec((s, LANES), lambda p, i, ns: (0, _SB_V0 + p)),
                row_spec, row_spec,
            ],
            out_specs=[row_spec, full_spec, full_spec],
        ),
        out_shape=[jax.ShapeDtypeStruct((s, SB_WIDTH), BF16), jax.ShapeDtypeStruct((s, SB_WIDTH), F32),
                   jax.ShapeDtypeStruct((s, SB_WIDTH), F32)],
        compiler_params=_cparams(2),
    )(n_steps, qkv, qkv, qkv, do_b, tot_b)


def _gates(gl, bg):
    return _sigmoid(gl[:, :D_MODEL] + bg[:, :D_MODEL]), _sigmoid(gl[:, D_MODEL:] + bg[:, D_MODEL:])


def _mixer_fwd(o_a, o_b, gl, x0, bg, g2, w_ud, w_us, w_out, tm):
    def epi(_, rows, consts):
        oa, ob, glv, x = rows
        bgv, g2v, wud, wus, wout = consts
        ga, gb = _gates(glv, bgv)
        merged = ga * _dot(oa, wud) + gb * _dot(ob, wus)
        x1 = x + _dot(merged.astype(BF16), wout)
        r, xh = _rms_stats(x1)
        return [x1, xh * g2v], []

    return _rowk("mixer_fwd", tm=tm, rows=[o_a, o_b, gl, x0], consts=[bg, g2, w_ud, w_us, w_out],
                 row_outs=[(D_MODEL, F32), (D_MODEL, BF16)], epilogue=epi)


def _mixer_bwd(dx1, o_a, o_b, gl, bg, w_ud, w_us, w_out, tm, rider=None):
    s = dx1.shape[0]
    nm = s // tm

    def body(dx_ref, oa_ref, ob_ref, gl_ref, bg_ref, wud_ref, wus_ref, wout_ref,
             doa_ref, dob_ref, dgl_ref, gwout_ref, gwud_ref, gwus_ref, gbg_ref, awout_ref, awud_ref, awus_ref):
        i = pl.program_id(0)
        dxb = dx_ref[...].astype(BF16)
        oa, ob = oa_ref[...], ob_ref[...]
        ga, gb = _gates(gl_ref[...], bg_ref[...])
        ua, ub = _dot(oa, wud_ref[...]), _dot(ob, wus_ref[...])
        merged = (ga * ua + gb * ub).astype(BF16)
        dm = _dot_nt(dxb, wout_ref[...])
        dua = (dm * ga).astype(BF16)
        dub = (dm * gb).astype(BF16)
        dgla = dm * ua * ga * (1.0 - ga)
        dglb = dm * ub * gb * (1.0 - gb)
        doa_ref[...] = _dot_nt(dua, wud_ref[...]).astype(doa_ref.dtype)
        dob_ref[...] = _dot_nt(dub, wus_ref[...]).astype(dob_ref.dtype)
        dgl_ref[:, :D_MODEL] = dgla.astype(dgl_ref.dtype)
        dgl_ref[:, D_MODEL:] = dglb.astype(dgl_ref.dtype)
        parts = [(gwout_ref, awout_ref, _dot_tn(merged, dxb)), (gwud_ref, awud_ref, _dot_tn(oa, dua)),
                 (gwus_ref, awus_ref, _dot_tn(ob, dub))]
        for out, r, v in parts:

            @pl.when(i == 0)
            def _(r=r, v=v):
                r[...] = v

            @pl.when(i > 0)
            def _(r=r, v=v):
                r[...] += v

            @pl.when(i == nm - 1)
            def _(out=out, r=r):
                out[...] = r[...].astype(out.dtype)

        sa = jnp.sum(dgla, axis=0, keepdims=True)
        sb = jnp.sum(dglb, axis=0, keepdims=True)

        @pl.when(i == 0)
        def _():
            gbg_ref[:, :D_MODEL] = sa
            gbg_ref[:, D_MODEL:] = sb

        @pl.when(i > 0)
        def _():
            gbg_ref[:, :D_MODEL] += sa
            gbg_ref[:, D_MODEL:] += sb

    row = lambda w: pl.BlockSpec((tm, w), lambda i: (i, 0))
    full = lambda a: pl.BlockSpec(a.shape, lambda i: (0, 0), pipeline_mode=pl.Buffered(1))
    wshape = lambda r, c: jax.ShapeDtypeStruct((r, c), BF16)
    return _call(
        body, (dx1, o_a, o_b, gl, bg, w_ud, w_us, w_out), rider,
        name="mixer_bwd",
        grid=(nm,),
        in_specs=[row(D_MODEL), row(DIL_OUT_WIDTH), row(SB_WIDTH), row(2 * D_MODEL),
                  full(bg), full(w_ud), full(w_us), full(w_out)],
        out_specs=[row(DIL_OUT_WIDTH), row(SB_WIDTH), row(2 * D_MODEL),
                   pl.BlockSpec((D_MODEL, D_MODEL), lambda i: (0, 0)),
                   pl.BlockSpec((DIL_OUT_WIDTH, D_MODEL), lambda i: (0, 0)),
                   pl.BlockSpec((SB_WIDTH, D_MODEL), lambda i: (0, 0)),
                   pl.BlockSpec((1, 2 * D_MODEL), lambda i: (0, 0))],
        out_shape=[jax.ShapeDtypeStruct((s, DIL_OUT_WIDTH), BF16), jax.ShapeDtypeStruct((s, SB_WIDTH), BF16),
                   jax.ShapeDtypeStruct((s, 2 * D_MODEL), BF16),
                   wshape(D_MODEL, D_MODEL), wshape(DIL_OUT_WIDTH, D_MODEL), wshape(SB_WIDTH, D_MODEL),
                   jax.ShapeDtypeStruct((1, 2 * D_MODEL), F32)],
        scratch_shapes=[pltpu.VMEM((D_MODEL, D_MODEL), F32), pltpu.VMEM((DIL_OUT_WIDTH, D_MODEL), F32),
                        pltpu.VMEM((SB_WIDTH, D_MODEL), F32)],
        compiler_params=_cparams(1),
    )


def _all_gather(shards):
    n = len(shards)

    def body(*refs):
        x_refs, out_refs = refs[:n], refs[n:2 * n]
        send_sems, recv_sems, local_sems = refs[2 * n:]
        x, y, c = lax.axis_index("x"), lax.axis_index("y"), lax.axis_index("c")
        me, sibling = (x, y, c), (x, y, 1 - c)
        chips = [(1 - x, y), (x, 1 - y), (1 - x, 1 - y)]

        def slot(a, px, py, pc):
            return out_refs[a].at[4 * px + 2 * py + pc]

        def copy(a, k, block, to, own=False):
            return pltpu.make_async_remote_copy(
                src_ref=x_refs[a] if own else slot(a, *block), dst_ref=slot(a, *block),
                send_sem=send_sems.at[7 * a + k], recv_sem=recv_sems.at[7 * a + k], device_id=to, device_id_type=_MESH)

        mine = [pltpu.make_async_copy(x_refs[a], slot(a, *me), local_sems.at[a]) for a in range(n)]
        for cp in mine:
            cp.start()
        first = []
        for a in range(n):
            first.append(copy(a, 0, me, sibling, own=True))
            first += [copy(a, 1 + j, me, (*chip, c), own=True) for j, chip in enumerate(chips)]
        for cp in first:
            cp.start()
        passed = []
        for a in range(n):
            for j, chip in enumerate(chips):
                copy(a, 1 + j, (*chip, c), me).wait_recv()
                passed.append(copy(a, 4 + j, (*chip, c), sibling))
                passed[-1].start()
        for a in range(n):
            copy(a, 0, sibling, me).wait_recv()
            for j, chip in enumerate(chips):
                copy(a, 4 + j, (*chip, 1 - c), me).wait_recv()
        for cp in first + passed:
            cp.wait_send()
        for cp in mine:
            cp.wait()

    return _pcall(
        body,
        name="all_gather_weights",
        in_specs=[_HBM] * n,
        out_specs=[_HBM] * n,
        out_shape=[jax.ShapeDtypeStruct((N_DEV,) + s.shape, s.dtype) for s in shards],
        scratch_shapes=[pltpu.SemaphoreType.DMA((7 * n,)), pltpu.SemaphoreType.DMA((7 * n,)),
                        pltpu.SemaphoreType.DMA((n,))],
    )(*shards)


def _exchange(chunks):
    n = len(chunks)

    def body(*refs):
        g_refs, o_refs = refs[:n], refs[n:2 * n]
        send_sems, recv_sems, local_sems = refs[2 * n:]
        x, y, c = lax.axis_index("x"), lax.axis_index("y"), lax.axis_index("c")
        me = 4 * x + 2 * y + c
        own = [pltpu.make_async_copy(g_refs[a].at[me], o_refs[a].at[me], local_sems.at[a]) for a in range(n)]
        for cp in own:
            cp.start()
        copies = []
        for a in range(n):
            for k in range(1, N_DEV):
                px, py, pc = x ^ (k >> 2), y ^ ((k >> 1) & 1), c ^ (k & 1)
                peer = 4 * px + 2 * py + pc
                copies.append(pltpu.make_async_remote_copy(
                    src_ref=g_refs[a].at[peer], dst_ref=o_refs[a].at[me], send_sem=send_sems.at[7 * a + k - 1],
                    recv_sem=recv_sems.at[7 * a + k - 1], device_id=(px, py, pc), device_id_type=_MESH))
        for cp in copies:
            cp.start()
        for cp in copies:
            cp.wait()
        for cp in own:
            cp.wait()

    return _pcall(
        body,
        name="exchange_grads",
        in_specs=[_HBM] * n,
        out_specs=[_HBM] * n,
        out_shape=[jax.ShapeDtypeStruct(g.shape, g.dtype) for g in chunks],
        scratch_shapes=[pltpu.SemaphoreType.DMA((7 * n,)), pltpu.SemaphoreType.DMA((7 * n,)),
                        pltpu.SemaphoreType.DMA((n,))],
    )(*chunks)


def _reduce_adamw(name, parts, w, m, v, tr):
    _, rows, cols = parts.shape
    tr = min(tr, rows)
    assert rows % tr == 0
    c1 = 1.0 / (1.0 - ADAM_B1 ** ADAM_STEP)
    c2 = 1.0 / (1.0 - ADAM_B2 ** ADAM_STEP)

    def body(p_ref, w_ref, m_ref, v_ref, g_out, d_out, m_out, v_out):
        g = p_ref[0].astype(F32)
        for d in range(1, N_DEV):
            g = g + p_ref[d].astype(F32)
        mn = ADAM_B1 * m_ref[...] + (1.0 - ADAM_B1) * g
        vn = ADAM_B2 * v_ref[...] + (1.0 - ADAM_B2) * (g * g)
        g_out[...] = g
        m_out[...] = mn
        v_out[...] = vn
        d_out[...] = -ADAM_LR * ((mn * c1) / (jnp.sqrt(vn * c2) + ADAM_EPS) + ADAM_WD * w_ref[...])

    spec = pl.BlockSpec((tr, cols), lambda i: (i, 0))
    return _pcall(
        body,
        name=name,
        grid=(rows // tr,),
        in_specs=[pl.BlockSpec((N_DEV, tr, cols), lambda i: (0, i, 0)), spec, spec, spec],
        out_specs=[spec] * 4,
        out_shape=[jax.ShapeDtypeStruct((rows, cols), F32)] * 4,
        compiler_params=_cparams(1),
    )(parts, w, m, v)


_SHARDED = ("w_in", "w_up_dil", "w_up_sb", "w_out", "w_mlp_in", "w_mlp_out")
_FULL_SHAPES = {"w_in": (D_MODEL, IN_COLS), "w_up_dil": (DIL_OUT_WIDTH, D_MODEL), "w_up_sb": (SB_WIDTH, D_MODEL),
                "w_out": (D_MODEL, D_MODEL), "w_mlp_in": (D_MODEL, D_FF), "w_mlp_out": (D_FF, D_MODEL)}
_ROW_SHARDED = ("w_out", "w_mlp_out")


def _shard_shape(name):
    r, c = _FULL_SHAPES[name]
    return (r // N_DEV, c) if name in _ROW_SHARDED else (r, c // N_DEV)


def _assemble(name, gathered):
    r, c = _shard_shape(name)
    if name in _ROW_SHARDED:
        return gathered.reshape(N_DEV * r, c)
    return gathered.transpose(1, 0, 2).reshape(r, N_DEV * c)


def _chunk(name, full):
    r, c = _shard_shape(name)
    if name in _ROW_SHARDED:
        return full.reshape(N_DEV, r, c)
    return full.reshape(r, N_DEV, c).transpose(1, 0, 2)


_SMALL = (("norm_mix_g", D_MODEL), ("b_gate", 2 * D_MODEL), ("norm_mlp_g", D_MODEL), ("norm_final_g", D_MODEL))
_SMALL_N = sum(n for _, n in _SMALL) + LANES


def _pack_small(vals, tail):
    return jnp.concatenate([vals[n].reshape(1, -1) for n, _ in _SMALL] + [tail], axis=1)


def _unpack_small(vec, shapes):
    out, pos = {}, 0
    for n, width in _SMALL:
        out[n] = vec[:, pos:pos + width].reshape(shapes[n])
        pos += width
    return out, vec[:, pos:]


def kernel(x, norm_mix_g, w_in, b_gate, w_up_dil, w_up_sb, w_out, norm_mlp_g, w_mlp_in, w_mlp_out, norm_final_g, loss_target, m_norm_mix_g, m_w_in, m_b_gate, m_w_up_dil, m_w_up_sb, m_w_out, m_norm_mlp_g, m_w_mlp_in, m_w_mlp_out, m_norm_final_g, v_norm_mix_g, v_w_in, v_b_gate, v_w_up_dil, v_w_up_sb, v_w_out, v_norm_mlp_g, v_w_mlp_in, v_w_mlp_out, v_norm_final_g):
    given = dict(locals())
    s = x.shape[1]
    x0 = x.reshape(s, D_MODEL)
    target = loss_target.reshape(s, D_MODEL)
    g1 = norm_mix_g.reshape(1, D_MODEL)
    g2 = norm_mlp_g.reshape(1, D_MODEL)
    g3 = norm_final_g.reshape(1, D_MODEL)
    bg = b_gate.reshape(1, 2 * D_MODEL)
    w_shards = {n: given[n].reshape(_shard_shape(n)) for n in _SHARDED}
    m_shards = {n: given["m_" + n].reshape(_shard_shape(n)) for n in _SHARDED}
    v_shards = {n: given["v_" + n].reshape(_shard_shape(n)) for n in _SHARDED}

    shard_b = {n: w_shards[n].astype(BF16) for n in _SHARDED}
    (gathered_w_in,) = _all_gather([shard_b["w_in"]])
    w_in_f = _assemble("w_in", gathered_w_in)
    w_qkv, w_gl = _group_major(w_in_f[:, :QKV_COLS]), w_in_f[:, QKV_COLS:]
    full = {}

    def norm1(_, rows, consts):
        _, xh = _rms_stats(rows[0])
        return [xh * consts[0]], []

    (h1,) = _rowk("norm_mix", tm=1024, rows=[x0], consts=[g1], row_outs=[(D_MODEL, BF16)], epilogue=norm1)
    qkv, (land,) = _mm("proj_qkv", h1, w_qkv, out_dtype=BF16, tm=1024, tn=768, tk=D_MODEL,
                       rider=_Spread([shard_b["w_mlp_in"]], chunked=False))
    full["w_mlp_in"] = _assemble("w_mlp_in", land)
    gl = _mm("proj_gates", h1, w_gl, out_dtype=BF16, tm=1024, tn=1024, tk=D_MODEL)
    views = [_dil_view(qkv, g) for g in range(len(DIL_GROUPS))]
    dil = [_dil_fwd(views[g], g) for g in range(len(DIL_GROUPS))]
    os_, lses = [d[0] for d in dil], [d[1] for d in dil]
    o_a = _dil_mix_fwd(os_, lses)
    riding = ("w_mlp_out", "w_out", "w_up_sb", "w_up_dil")
    (o_b, tot_b, sb_steps), lands = _sb_fwd(qkv, rider=_Spread([shard_b[n] for n in riding], chunked=False))
    full.update({n: _assemble(n, land) for n, land in zip(riding, lands)})
    x1, h2 = _mixer_fwd(o_a, o_b, gl, x0, bg, g2, full["w_up_dil"], full["w_up_sb"], full["w_out"], 512)
    f = _mm("mlp_in", h2, full["w_mlp_in"], out_dtype=BF16, tm=1024, tn=1024, tk=D_MODEL,
            epilogue=lambda r, _: jnp.square(jnp.maximum(r, 0.0)))

    def head(acc, rows, consts):
        x1v, tv = rows
        g3v = consts[0]
        x2 = x1v + acc
        r, xh = _rms_stats(x2)
        diff = xh * g3v - tv
        loss = (0.5 / D_MODEL) * jnp.sum(jnp.sum(diff * diff, axis=0, keepdims=True), axis=1, keepdims=True)
        dy = diff * (1.0 / D_MODEL)
        dx2, dg = _rms_bwd(dy, xh, r, g3v)
        return [dx2, dx2], [dg, jnp.broadcast_to(loss, (1, LANES))]

    dx2, dx2b, gg3, loss_part = _rowk(
        "mlp_out_loss", a=f, w=full["w_mlp_out"], tm=512, tk=D_FF, rows=[x1, target], consts=[g3],
        row_outs=[(D_MODEL, F32), (D_MODEL, BF16)], acc_outs=[D_MODEL, LANES], epilogue=head)

    da = _mm("mlp_out_bwd", dx2b, full["w_mlp_out"], tb=True, out_dtype=BF16, tm=1024, tn=1024, tk=D_MODEL, extra=f,
             epilogue=lambda r, fv: r * (2.0 * jnp.sqrt(fv.astype(F32))))
    g_w_mlp_out = _mm("grad_w_mlp_out", f, dx2b, ta=True, out_dtype=BF16, tm=1024, tn=1024, tk=2048)
    g_w_mlp_in = _mm("grad_w_mlp_in", h2, da, ta=True, out_dtype=BF16, tm=1024, tn=1024, tk=2048)

    def norm_bwd(acc, rows, consts):
        xv, dres = rows
        r, xh = _rms_stats(xv)
        dx, dg = _rms_bwd(acc, xh, r, consts[0])
        return [dres + dx], [dg]

    bchunk = lambda n, g: _chunk(n, g).astype(BF16)
    parts = {}
    (dx1, gg2), (parts["w_mlp_in"],) = _rowk(
        "mlp_in_bwd", a=da, w=full["w_mlp_in"], nt=True, tm=512, tk=D_FF, rows=[x1, dx2], consts=[g2],
        row_outs=[(D_MODEL, F32)], acc_outs=[D_MODEL], epilogue=norm_bwd,
        rider=_Spread([bchunk("w_mlp_in", g_w_mlp_in)], chunked=True))
    (do_a, do_b, dgl, g_w_out, g_w_ud, g_w_us, g_bg), (parts["w_mlp_out"],) = _mixer_bwd(
        dx1, o_a, o_b, gl, bg, full["w_up_dil"], full["w_up_sb"], full["w_out"], 512,
        rider=_Spread([bchunk("w_mlp_out", g_w_mlp_out)], chunked=True))
    mix = _dil_mix_bwd(do_a, os_, lses)
    small_three = {"w_out": g_w_out, "w_up_sb": g_w_us, "w_up_dil": g_w_ud}
    grads, lands = _dil_bwd(views[0], mix[0], lses[0], mix[3], 0,
                            rider=_Spread([bchunk(n, g) for n, g in small_three.items()], chunked=True))
    parts.update(dict(zip(small_three, lands)))
    dil_b = [grads] + [_dil_bwd(views[g], mix[g], lses[g], mix[3 + g], g) for g in (1, 2)]
    dq_b, dk_b, dv_b = _sb_bwd(qkv, do_b, tot_b, sb_steps)
    dproj = [d[0] for d in dil_b] + [d[1] for d in dil_b] + [d[2] for d in dil_b] + [dq_b, dk_b, dv_b, dgl]
    g_w_in = jnp.concatenate([
        _grad_cols("grad_w_in_dil", h1, dproj[:9], tm=D_MODEL, tk=1024),
        _grad_cols("grad_w_in_sb", h1, dproj[9:12], tm=D_MODEL, tk=1024),
        _grad_cols("grad_w_in_gates", h1, dproj[12:], tm=D_MODEL, tk=1024)], axis=1)
    (grad_x, gg1), (parts["w_in"],) = _rowk(
        "in_proj_bwd", a=dproj, w=w_in_f, nt=True, tm=512, tk=IN_COLS, rows=[x0, dx1], consts=[g1],
        row_outs=[(D_MODEL, F32)], acc_outs=[D_MODEL], epilogue=norm_bwd, in_place=(1, 0),
        rider=_Spread([bchunk("w_in", g_w_in)], chunked=True))

    small_part = _pack_small({"norm_mix_g": gg1, "b_gate": g_bg, "norm_mlp_g": gg2, "norm_final_g": gg3}, loss_part)
    (small_parts,) = _exchange([jnp.broadcast_to(small_part[None], (N_DEV, 1, _SMALL_N))])

    tags = ("grad_", "delta_", "new_m_", "new_v_")
    outs = {}
    for n, p in parts.items():
        res = _reduce_adamw("adamw_" + n, p, w_shards[n], m_shards[n], v_shards[n], 256)
        for tag, val in zip(tags, res):
            outs[tag + n] = val.reshape(given[n].shape)
    small_w = _pack_small(given, jnp.zeros((1, LANES), F32))
    small_m = _pack_small({n: given["m_" + n] for n, _ in _SMALL}, jnp.zeros((1, LANES), F32))
    small_v = _pack_small({n: given["v_" + n] for n, _ in _SMALL}, jnp.ones((1, LANES), F32))
    small_res = _reduce_adamw("adamw_replicated", small_parts, small_w, small_m, small_v, 8)

    small_shapes = {n: given[n].shape for n, _ in _SMALL}
    for tag, small in zip(tags, small_res):
        small_vals, tail = _unpack_small(small, small_shapes)
        for n, val in small_vals.items():
            outs[tag + n] = val
        if tag == "grad_":
            loss = tail[0, 0]
    names = ["norm_mix_g", "w_in", "b_gate", "w_up_dil", "w_up_sb", "w_out", "norm_mlp_g", "w_mlp_in", "w_mlp_out",
             "norm_final_g"]
    return (loss, grad_x.reshape(x.shape), *[outs["grad_" + n] for n in names], *[outs["delta_" + n] for n in names],
            *[outs["new_m_" + n] for n in names], *[outs["new_v_" + n] for n in names])
```

```python
import functools
import math

import jax
import jax.numpy as jnp
from jax import lax
from jax.experimental import pallas as pl
from jax.experimental.pallas import tpu as pltpu

_pcall = pl.pallas_call

F32 = jnp.float32
BF16 = jnp.bfloat16

D_MODEL = 1024
HEAD_DIM = 64
DIL_GROUPS = ((128, 1), (512, 4), (2048, 16))
DIL_HEADS_PER_GROUP = 4
N_DIL_HEADS = 12
N_SB_HEADS = 8
DIL_WIDTH = 768
DIL_OUT_WIDTH = 256
SB_WIDTH = 512
D_FF = 4096
BLOCK = 128
RMS_EPS = 1e-6
NEG_INF = -1e30
QKV_COLS = 3 * DIL_WIDTH + 3 * SB_WIDTH
IN_COLS = QKV_COLS + 2 * D_MODEL
N_DEV = 8

ADAM_LR = 0.001
ADAM_B1 = 0.9
ADAM_B2 = 0.999
ADAM_EPS = 1e-08
ADAM_WD = 0.01
ADAM_STEP = 10

VMEM_LIMIT = 56 * 1024 * 1024
SB_TK = 256
LANES = 128

_ARB = pltpu.ARBITRARY


def _cparams(n_axes, **kw):
    return pltpu.CompilerParams(dimension_semantics=(_ARB,) * n_axes, vmem_limit_bytes=VMEM_LIMIT, **kw)


def _dot(a, b):
    return jnp.dot(a, b, preferred_element_type=F32)


def _dot_nt(a, b):
    return lax.dot_general(a, b, (((1,), (1,)), ((), ())), preferred_element_type=F32)


def _dot_tn(a, b):
    return lax.dot_general(a, b, (((0,), (0,)), ((), ())), preferred_element_type=F32)


def _split_hi_lo(x):
    hi = x.astype(BF16)
    lo = (x - hi.astype(F32)).astype(BF16)
    return hi, lo


def _dot_hi_lo(x, m):
    hi, lo = _split_hi_lo(x)
    return _dot(hi, m) + _dot(lo, m)


def _sigmoid(x):
    return 1.0 / (1.0 + jnp.exp(-x))


_HBM = pl.BlockSpec(memory_space=pltpu.HBM)
_MESH = pl.DeviceIdType.MESH


class _Spread:
    def __init__(self, srcs, chunked):
        self.srcs, self.chunked, self.n = list(srcs), chunked, len(srcs)

    def land_shapes(self):
        return [jax.ShapeDtypeStruct((N_DEV,) + (s.shape[1:] if self.chunked else s.shape), s.dtype) for s in self.srcs]

    def scratch(self):
        dma = pltpu.SemaphoreType.DMA
        return [dma((7 * self.n,)), dma((7 * self.n,)), dma((self.n,))]

    def copies(self, src_refs, land_refs, send_sems, recv_sems, local_sems):
        x, y, c = lax.axis_index("x"), lax.axis_index("y"), lax.axis_index("c")
        me = 4 * x + 2 * y + c
        out = []
        for a, (src, land) in enumerate(zip(src_refs, land_refs)):
            out.append(pltpu.make_async_copy(src.at[me] if self.chunked else src, land.at[me], local_sems.at[a]))
            for k in range(1, N_DEV):
                px, py, pc = x ^ (k >> 2), y ^ ((k >> 1) & 1), c ^ (k & 1)
                out.append(pltpu.make_async_remote_copy(
                    src_ref=src.at[4 * px + 2 * py + pc] if self.chunked else src, dst_ref=land.at[me],
                    send_sem=send_sems.at[7 * a + k - 1], recv_sem=recv_sems.at[7 * a + k - 1],
                    device_id=(px, py, pc), device_id_type=_MESH))
        return out


def _call(body, args, rider=None, **kw):
    if rider is None:
        return _pcall(body, **kw)(*args)
    grid = kw["grid"]
    single = not isinstance(kw["out_shape"], (list, tuple))
    out_specs = [kw["out_specs"]] if single else list(kw["out_specs"])
    out_shape = [kw["out_shape"]] if single else list(kw["out_shape"])
    in_specs, scratch = list(kw["in_specs"]), list(kw.get("scratch_shapes", []))
    n_in, n_out, n_s, n = len(in_specs), len(out_shape), len(scratch), rider.n

    def hosted(*refs):
        ins, srcs = refs[:n_in], refs[n_in:n_in + n]
        outs, lands = refs[n_in + n:n_in + n + n_out], refs[n_in + n + n_out:n_in + 2 * n + n_out]
        own_scratch, sems = refs[n_in + 2 * n + n_out:n_in + 2 * n + n_out + n_s], refs[n_in + 2 * n + n_out + n_s:]
        ids = [pl.program_id(d) for d in range(len(grid))]
        first = functools.reduce(jnp.logical_and, [i == 0 for i in ids])
        last = functools.reduce(jnp.logical_and, [i == g - 1 for i, g in zip(ids, grid)])
        copies = rider.copies(srcs, lands, *sems)

        @pl.when(first)
        def _():
            for cp in copies:
                cp.start()

        body(*ins, *outs, *own_scratch)

        @pl.when(last)
        def _():
            for cp in copies:
                cp.wait()

    kw = dict(kw, in_specs=in_specs + [_HBM] * n, out_specs=out_specs + [_HBM] * n,
              out_shape=out_shape + rider.land_shapes(), scratch_shapes=scratch + rider.scratch())
    res = _pcall(hosted, **kw)(*args, *rider.srcs)
    return (res[0] if single else list(res[:n_out])), list(res[n_out:])


def _mm(name, a, b, *, ta=False, tb=False, out_dtype, tm, tn, tk, epilogue=None, extra=None, rider=None,
        col_chunks=None):
    m = a.shape[1] if ta else a.shape[0]
    k = a.shape[0] if ta else a.shape[1]
    n = b.shape[0] if tb else b.shape[1]
    assert (b.shape[1] if tb else b.shape[0]) == k
    tm, tn, tk = min(tm, m), min(tn, n), min(tk, k)
    assert m % tm == 0 and n % tn == 0 and k % tk == 0, (name, m, n, k, tm, tn, tk)
    nk = k // tk
    dn = (((0 if ta else 1,), (1 if tb else 0,)), ((), ()))
    in_place = nk > 1 and epilogue is None and out_dtype == F32 and col_chunks is None
    cw = n // col_chunks if col_chunks else None
    assert cw is None or (tn % cw == 0 and cw % LANES == 0)

    def body(*refs):
        if extra is not None:
            a_ref, b_ref, e_ref, o_ref = refs[:4]
        else:
            a_ref, b_ref, o_ref = refs[:3]
            e_ref = None

        def finish(r):
            if epilogue is not None:
                r = epilogue(r, None if e_ref is None else e_ref[...])
            if cw is None:
                o_ref[...] = r.astype(out_dtype)
            else:
                for c in range(tn // cw):
                    o_ref[c] = r[:, c * cw:(c + 1) * cw].astype(out_dtype)

        part = lax.dot_general(a_ref[...].astype(BF16), b_ref[...].astype(BF16), dn, preferred_element_type=F32)
        if nk == 1:
            finish(part)
        else:
            acc_ref = o_ref if in_place else refs[-1]
            kk = pl.program_id(2)

            @pl.when(kk == 0)
            def _():
                acc_ref[...] = part

            @pl.when(kk > 0)
            def _():
                acc_ref[...] += part

            if not in_place:

                @pl.when(kk == nk - 1)
                def _():
                    finish(acc_ref[...])

    a_spec = pl.BlockSpec((tk, tm), lambda j, i, kk: (kk, i)) if ta else pl.BlockSpec((tm, tk), lambda j, i, kk: (i, kk))
    b_spec = pl.BlockSpec((tn, tk), lambda j, i, kk: (j, kk)) if tb else pl.BlockSpec((tk, tn), lambda j, i, kk: (kk, j))
    o_spec = pl.BlockSpec((tm, tn), lambda j, i, kk: (i, j))
    in_specs = [a_spec, b_spec]
    args = [a, b]
    if extra is not None:
        in_specs.append(o_spec)
        args.append(extra)
    out_shape = jax.ShapeDtypeStruct((m, n), out_dtype)
    if cw is not None:
        o_spec = pl.BlockSpec((tn // cw, tm, cw), lambda j, i, kk: (j, i, 0))
        out_shape = jax.ShapeDtypeStruct((col_chunks, m, cw), out_dtype)
    return _call(
        body, args, rider,
        name=name,
        grid=(n // tn, m // tm, nk),
        in_specs=in_specs,
        out_specs=o_spec,
        out_shape=out_shape,
        scratch_shapes=[pltpu.VMEM((tm, tn), F32)] if (nk > 1 and not in_place) else [],
        compiler_params=_cparams(3),
    )


def _grad_cols(name, a, parts, *, tm, tk, rider=None):
    k, m = a.shape
    n = sum(p.shape[1] for p in parts)
    assert m % tm == 0 and k % tk == 0
    nk = k // tk

    def body(*refs):
        a_ref, p_refs, o_ref, acc_ref = refs[0], refs[1:1 + len(parts)], refs[1 + len(parts)], refs[2 + len(parts)]
        kk = pl.program_id(1)
        side_by_side = jnp.concatenate([p_ref[...].astype(BF16) for p_ref in p_refs], axis=1)
        term = _dot_tn(a_ref[...].astype(BF16), side_by_side)

        @pl.when(kk == 0)
        def _():
            acc_ref[...] = term

        @pl.when(kk > 0)
        def _():
            acc_ref[...] += term

        @pl.when(kk == nk - 1)
        def _():
            o_ref[...] = acc_ref[...].astype(o_ref.dtype)

    return _call(
        body, [a] + list(parts), rider,
        name=name,
        grid=(m // tm, nk),
        in_specs=[pl.BlockSpec((tk, tm), lambda i, kk: (kk, i))]
        + [pl.BlockSpec((tk, p.shape[1]), lambda i, kk: (kk, 0)) for p in parts],
        out_specs=pl.BlockSpec((tm, n), lambda i, kk: (i, 0)),
        out_shape=jax.ShapeDtypeStruct((m, n), BF16),
        scratch_shapes=[pltpu.VMEM((tm, n), F32)],
        compiler_params=_cparams(2),
    )


def _rowk(name, *, a=None, w=None, nt=False, tm, tk=None, rows=(), consts=(), row_outs=(), acc_outs=(), epilogue,
          rider=None):
    has_mm = a is not None
    a_parts = list(a) if isinstance(a, (list, tuple)) else ([a] if has_mm else [])
    n_a = len(a_parts)
    m = a_parts[0].shape[0] if has_mm else rows[0].shape[0]
    assert m % tm == 0
    nm = m // tm
    if has_mm:
        k = sum(p.shape[1] for p in a_parts)
        n = w.shape[0] if nt else w.shape[1]
        tk = min(tk, k)
        assert k % tk == 0 and (n_a == 1 or tk == k)
        nk = k // tk
    else:
        nk = 1
    n_rows, n_consts, n_ro, n_ao = len(rows), len(consts), len(row_outs), len(acc_outs)

    def body(*refs):
        pos = 0
        if has_mm:
            a_refs, w_ref = refs[:n_a], refs[n_a]
            pos = n_a + 1
        row_refs = refs[pos:pos + n_rows]
        pos += n_rows
        const_refs = refs[pos:pos + n_consts]
        pos += n_consts
        ro_refs = refs[pos:pos + n_ro]
        pos += n_ro
        ao_refs = refs[pos:pos + n_ao]
        pos += n_ao
        i = pl.program_id(0)
        kk = pl.program_id(1)

        def finish(acc):
            ro_vals, ao_vals = epilogue(acc, [r[...] for r in row_refs], [c[...] for c in const_refs])
            for r, v in zip(ro_refs, ro_vals):
                r[...] = v.astype(r.dtype)
            for r, v in zip(ao_refs, ao_vals):

                @pl.when(i == 0)
                def _(r=r, v=v):
                    r[...] = v

                @pl.when(i > 0)
                def _(r=r, v=v):
                    r[...] += v

        if not has_mm:
            finish(None)
            return
        part, off = None, 0
        for a_ref in a_refs:
            width = a_ref.shape[1]
            cols = slice(None) if n_a == 1 else slice(off, off + width)
            av = a_ref[...].astype(BF16)
            term = _dot_nt(av, w_ref[:, cols]) if nt else _dot(av, w_ref[cols, :])
            part = term if part is None else part + term
            off += width
        if nk == 1:
            finish(part)
        else:
            acc_ref = refs[pos]

            @pl.when(kk == 0)
            def _():
                acc_ref[...] = part

            @pl.when(kk > 0)
            def _():
                acc_ref[...] += part

            @pl.when(kk == nk - 1)
            def _():
                finish(acc_ref[...])

    once = pl.Buffered(1)
    in_specs, args = [], []
    if has_mm:
        for part in a_parts:
            in_specs.append(pl.BlockSpec((tm, tk if n_a == 1 else part.shape[1]), lambda i, kk: (i, kk)))
        w_mode = once if nk == 1 else None
        in_specs.append(pl.BlockSpec((n, tk), lambda i, kk: (0, kk), pipeline_mode=w_mode) if nt
                        else pl.BlockSpec((tk, n), lambda i, kk: (kk, 0), pipeline_mode=w_mode))
        args += a_parts + [w]
    for r in rows:
        in_specs.append(pl.BlockSpec((tm, r.shape[1]), lambda i, kk: (i, 0)))
        args.append(r)
    for c in consts:
        in_specs.append(pl.BlockSpec(c.shape, lambda i, kk: (0,) * c.ndim, pipeline_mode=once))
        args.append(c)
    out_specs, out_shape = [], []
    for width, dt in row_outs:
        out_specs.append(pl.BlockSpec((tm, width), lambda i, kk: (i, 0)))
        out_shape.append(jax.ShapeDtypeStruct((m, width), dt))
    for width in acc_outs:
        out_specs.append(pl.BlockSpec((1, width), lambda i, kk: (0, 0)))
        out_shape.append(jax.ShapeDtypeStruct((1, width), F32))
    return _call(
        body, args, rider,
        name=name,
        grid=(nm, nk),
        in_specs=in_specs,
        out_specs=out_specs,
        out_shape=out_shape,
        scratch_shapes=[pltpu.VMEM((tm, n), F32)] if (has_mm and nk > 1) else [],
        compiler_params=_cparams(2),
    )


def _rms_stats(x):
    r = lax.rsqrt(jnp.mean(x * x, axis=-1, keepdims=True) + RMS_EPS)
    return r, x * r


def _rms_bwd(dh, xh, r, g):
    gy = dh * g
    dx = r * (gy - xh * jnp.mean(gy * xh, axis=-1, keepdims=True))
    return dx, jnp.sum(dh * xh, axis=0, keepdims=True)


def _alibi_slope(head):
    return 2.0 ** (-8.0 * (head + 1) / N_DIL_HEADS)


DIL_STEP_BLOCKS = 4


def _dil_band(first_block):
    qi = lax.broadcasted_iota(jnp.int32, (BLOCK, 2 * BLOCK), 0)
    kj = lax.broadcasted_iota(jnp.int32, (BLOCK, 2 * BLOCK), 1)
    steps = qi + BLOCK - kj
    valid = (steps >= 0) & (steps <= BLOCK)
    if first_block is not False:
        valid = valid & ((kj >= BLOCK) | jnp.logical_not(first_block))
    return steps.astype(F32), valid


def _dil_step_specs(ncb, cols, nblk, clamp):
    def own(col):
        return pl.BlockSpec((nblk * BLOCK, DIL_OUT_WIDTH), lambda r, i: (clamp(i), r * ncb + col))

    def before(col):
        return pl.BlockSpec((BLOCK, DIL_OUT_WIDTH), lambda r, i: (jnp.maximum(clamp(i) * nblk - 1, 0), r * ncb + col))

    return [own(cols[0]), own(cols[1]), before(cols[1]), own(cols[2]), before(cols[2])]


DIL_RELAYOUT_ROWS = 1024


def _view_scratch(width):
    return pltpu.VMEM((width // LANES, DIL_RELAYOUT_ROWS, LANES), F32)


def _rows_from_view(src, scr, d, w):
    sub = src.shape[0]
    for j in range(w // LANES):
        for r in range(d):
            scr[j, pl.ds(r, sub, stride=d), :] = src[:, r * w + j * LANES:r * w + (j + 1) * LANES].astype(F32)


def _rows_to_view(scr, dst, d, w):
    sub = dst.shape[0]
    for j in range(w // LANES):
        for r in range(d):
            dst[:, r * w + j * LANES:r * w + (j + 1) * LANES] = scr[j, pl.ds(r, sub, stride=d), :].astype(dst.dtype)


def _dil_relayout(name, xs, dilation, to_view, col_block=0, width=None):
    d = dilation
    tm = DIL_RELAYOUT_ROWS
    rows = tm // d
    if to_view:
        s = xs[0].shape[0]
        widths = [width or x.shape[1] for x in xs]
    else:
        s = xs[0].shape[0] * d
        widths = [v.shape[1] // d for v in xs]
    assert s % tm == 0 and all(w % LANES == 0 for w in widths) and all(x.dtype == BF16 for x in xs)
    n = len(xs)
    blk = 256
    per = blk // d
    assert per % 16 == 0 and tm % blk == 0

    def body(*refs):
        in_refs, out_refs = refs[:n], refs[n:]
        i0 = lax.broadcasted_iota(jnp.int32, (blk, blk), 0)
        i1 = lax.broadcasted_iota(jnp.int32, (blk, blk), 1)
        sort = (i1 == (i0 % per) * d + i0 // per) if to_view else (i0 == (i1 % per) * d + i1 // per)
        sort = jnp.where(sort, 1.0, 0.0).astype(BF16)
        for src, dst, w in zip(in_refs, out_refs, widths):
            for b in range(tm // blk):
                if to_view:
                    y = _dot(sort, src[b * blk:(b + 1) * blk, :]).astype(BF16)
                    for r in range(d):
                        dst[b * per:(b + 1) * per, r * w:(r + 1) * w] = y[r * per:(r + 1) * per, :]
                else:
                    by_residue = jnp.concatenate(
                        [src[b * per:(b + 1) * per, r * w:(r + 1) * w] for r in range(d)], axis=0)
                    dst[b * blk:(b + 1) * blk, :] = _dot(sort, by_residue).astype(BF16)

    natural = [pl.BlockSpec((tm, w), lambda i: (i, col_block)) for w in widths]
    viewed = [pl.BlockSpec((rows, d * w), lambda i: (i, 0)) for w in widths]
    return _pcall(
        body,
        name=name,
        grid=(s // tm,),
        in_specs=natural if to_view else viewed,
        out_specs=viewed if to_view else natural,
        out_shape=[jax.ShapeDtypeStruct((s // d, d * w) if to_view else (s, w), BF16) for w in widths],
        compiler_params=_cparams(1),
    )(*xs)


def _dil_fwd(view, group):
    window, dilation = DIL_GROUPS[group]
    qkv_v, ncb, cols = view
    sub = qkv_v.shape[0]
    s = sub * dilation
    nb = sub // BLOCK
    assert nb * BLOCK * dilation == s and window // dilation == BLOCK
    nblk = min(DIL_STEP_BLOCKS, nb)
    assert nb % nblk == 0
    slopes = [_alibi_slope(group * DIL_HEADS_PER_GROUP + h) * dilation for h in range(DIL_HEADS_PER_GROUP)]

    def body(q_ref, kc_ref, kp_ref, vc_ref, vp_ref, o_ref, lse_ref):
        i = pl.program_id(1)
        kk_all = jnp.concatenate([kp_ref[...], kc_ref[...]], axis=0)
        vv_all = jnp.concatenate([vp_ref[...], vc_ref[...]], axis=0)
        head_id = lax.broadcasted_iota(jnp.int32, (1, DIL_OUT_WIDTH), 1) // HEAD_DIM
        chains = [(b, h) for b in range(nblk) for h in range(DIL_HEADS_PER_GROUP)]
        rows = lambda b: slice(b * BLOCK, (b + 1) * BLOCK)
        keys = lambda b: slice(b * BLOCK, (b + 2) * BLOCK)
        bands = [_dil_band(i == 0 if b == 0 else False) for b in range(nblk)]
        qs = [q_ref[rows(b), :] for b in range(nblk)]
        scores = [_dot_nt(jnp.where(head_id == h, qs[b], jnp.zeros_like(qs[b])), kk_all[keys(b)]) for b, h in chains]
        ps, lses = [], []
        for (b, h), sc in zip(chains, scores):
            steps, valid = bands[b]
            logits = jnp.where(valid, sc * (1.0 / math.sqrt(HEAD_DIM)) - slopes[h] * steps, NEG_INF)
            mx = jnp.max(logits, axis=1, keepdims=True)
            e = jnp.exp(logits - mx)
            den = jnp.sum(e, axis=1, keepdims=True)
            lses.append(mx + jnp.log(den))
            ps.append((e * (1.0 / den)).astype(BF16))
        outs = [_dot(p, vv_all[keys(b)]) for (b, h), p in zip(chains, ps)]
        for b in range(nblk):
            mine = [n for n, ch in enumerate(chains) if ch[0] == b]
            o, lse_all = outs[mine[0]], lses[mine[0]]
            for n in mine[1:]:
                o = jnp.where(head_id == chains[n][1], outs[n], o)
                lse_all = jnp.where(head_id == chains[n][1], lses[n], lse_all)
            o_ref[rows(b), :] = o
            lse_ref[rows(b), :] = jnp.broadcast_to(lse_all, o.shape)

    out_spec = pl.BlockSpec((nblk * BLOCK, DIL_OUT_WIDTH), lambda r, i: (i, r))
    o, lse = _pcall(
        body,
        name=f"dil_fwd_g{group}",
        grid=(dilation, nb // nblk),
        in_specs=_dil_step_specs(ncb, cols, nblk, lambda i: i),
        out_specs=[out_spec, out_spec],
        out_shape=[jax.ShapeDtypeStruct((sub, dilation * DIL_OUT_WIDTH), F32)] * 2,
        compiler_params=_cparams(2),
    )(qkv_v, qkv_v, qkv_v, qkv_v, qkv_v)
    return o, lse


def _dil_bwd(view, do_g, lse_g, dterm_g, group, rider=None):
    window, dilation = DIL_GROUPS[group]
    qkv_v, ncb, cols = view
    sub = qkv_v.shape[0]
    nb = sub // BLOCK
    nblk = min(DIL_STEP_BLOCKS, nb)
    n_steps = nb // nblk
    slopes = [_alibi_slope(group * DIL_HEADS_PER_GROUP + h) * dilation for h in range(DIL_HEADS_PER_GROUP)]
    scale = 1.0 / math.sqrt(HEAD_DIM)
    tail = slice((nblk - 1) * BLOCK, nblk * BLOCK)
    single = n_steps == 1

    def body(q_ref, kc_ref, kp_ref, vc_ref, vp_ref, do_ref, lse_ref, dt_ref, dq_ref, dk_ref, dv_ref, *carry_refs):
        i = pl.program_id(1)

        def init():
            for carry_ref in carry_refs:
                carry_ref[...] = jnp.zeros_like(carry_ref)

        def compute():
            kk_all = jnp.concatenate([kp_ref[...], kc_ref[...]], axis=0)
            vv_all = jnp.concatenate([vp_ref[...], vc_ref[...]], axis=0)
            lane = lax.broadcasted_iota(jnp.int32, (1, DIL_OUT_WIDTH), 1)
            head_id = lane // HEAD_DIM
            chains = [(b, h) for b in range(nblk) for h in range(DIL_HEADS_PER_GROUP)]
            rows = lambda b: slice(b * BLOCK, (b + 1) * BLOCK)
            keys = lambda b: slice(b * BLOCK, (b + 2) * BLOCK)
            bands = [_dil_band(i == 0 if b == 0 else False) for b in range(nblk)]
            qms, doms = [], []
            for b, h in chains:
                q, do = q_ref[rows(b), :], do_ref[rows(b), :]
                qms.append(jnp.where(head_id == h, q, jnp.zeros_like(q)))
                doms.append(jnp.where(head_id == h, do, jnp.zeros_like(do)))
            scores = [_dot_nt(qm, kk_all[keys(b)]) for (b, h), qm in zip(chains, qms)]
            dps = [_dot_nt(dom, vv_all[keys(b)]) for (b, h), dom in zip(chains, doms)]
            pbs, dss = [], []
            for n, (b, h) in enumerate(chains):
                steps, valid = bands[b]
                first = lane == h * HEAD_DIM
                lse = jnp.sum(jnp.where(first, lse_ref[rows(b), :], 0.0), axis=1, keepdims=True)
                dt = jnp.sum(jnp.where(first, dt_ref[rows(b), :], 0.0), axis=1, keepdims=True)
                logits = jnp.where(valid, scores[n] * scale - slopes[h] * steps, NEG_INF)
                p = jnp.where(valid, jnp.exp(logits - lse), 0.0)
                pbs.append(p.astype(BF16))
                dss.append((p * (dps[n] + dt) * scale).astype(BF16))
            dqs = [_dot(ds, kk_all[keys(b)]) for (b, h), ds in zip(chains, dss)]
            dks = [_dot_tn(ds, qm) for ds, qm in zip(dss, qms)]
            dvs = [_dot_tn(pb, dom) for pb, dom in zip(pbs, doms)]
            dkk, dvv = [], []
            for b in range(nblk):
                mine = [n for n, ch in enumerate(chains) if ch[0] == b]
                dq = dqs[mine[0]]
                for n in mine[1:]:
                    dq = jnp.where(head_id == chains[n][1], dqs[n], dq)
                dq_ref[rows(b), :] = dq.astype(dq_ref.dtype)
                dkk.append((dks[mine[0]] + dks[mine[1]]) + (dks[mine[2]] + dks[mine[3]]))
                dvv.append((dvs[mine[0]] + dvs[mine[1]]) + (dvs[mine[2]] + dvs[mine[3]]))
            for n, (out_ref, parts) in enumerate(((dk_ref, dkk), (dv_ref, dvv))):
                done = [parts[b][BLOCK:] + parts[b + 1][:BLOCK] if b + 1 < nblk else parts[b][BLOCK:]
                        for b in range(nblk)]
                if single:
                    for b in range(nblk):
                        out_ref[rows(b), :] = done[b].astype(out_ref.dtype)
                    continue
                carry_ref = carry_refs[n]
                if nblk > 1:
                    out_ref[: (nblk - 1) * BLOCK, :] = carry_ref[: (nblk - 1) * BLOCK, :].astype(out_ref.dtype)
                out_ref[tail, :] = (carry_ref[tail, :] + parts[0][:BLOCK]).astype(out_ref.dtype)
                for b in range(nblk):
                    carry_ref[rows(b), :] = done[b]

        def flush():
            for out_ref, carry_ref in zip((dk_ref, dv_ref), carry_refs):
                out_ref[...] = carry_ref[...].astype(out_ref.dtype)

        if single:
            compute()
        else:
            pl.when(i == 0)(init)
            pl.when(i < n_steps)(compute)
            pl.when(i == n_steps)(flush)

    clamp = lambda i: jnp.minimum(i, n_steps - 1)
    row_spec = pl.BlockSpec((nblk * BLOCK, DIL_OUT_WIDTH), lambda r, i: (clamp(i), r))
    late_spec = pl.BlockSpec((nblk * BLOCK, DIL_OUT_WIDTH), lambda r, i: (jnp.maximum(i - 1, 0), r))
    res = _call(
        body, (qkv_v, qkv_v, qkv_v, qkv_v, qkv_v, do_g, lse_g, dterm_g), rider,
        name=f"dil_bwd_g{group}",
        grid=(dilation, n_steps + (0 if single else 1)),
        in_specs=_dil_step_specs(ncb, cols, nblk, clamp) + [row_spec, row_spec, row_spec],
        out_specs=[row_spec, row_spec, row_spec] if single else [row_spec, late_spec, late_spec],
        out_shape=[jax.ShapeDtypeStruct((sub, dilation * DIL_OUT_WIDTH), BF16)] * 3,
        scratch_shapes=[] if single else [pltpu.VMEM((nblk * BLOCK, DIL_OUT_WIDTH), F32)] * 2,
        compiler_params=_cparams(2),
    )
    grads, lands = res if rider is not None else (res, None)
    if dilation > 1:
        grads = _dil_relayout(f"dil_bwd_rows_g{group}", list(grads), dilation, to_view=False)
    return tuple(grads) if rider is None else (tuple(grads), lands)


def _dil_view(qkv, group):
    _, dilation = DIL_GROUPS[group]
    w = DIL_OUT_WIDTH
    if dilation == 1:
        return qkv, QKV_COLS // w, (3 * group, 3 * group + 1, 3 * group + 2)
    (own,) = _dil_relayout(f"dil_view_g{group}", [qkv], dilation, to_view=True, col_block=group, width=3 * w)
    return own, 3, (0, 1, 2)


def _group_major(w_qkv):
    w = DIL_OUT_WIDTH
    ng = len(DIL_GROUPS)
    cols = [w_qkv[:, (part * ng + g) * w:(part * ng + g + 1) * w] for g in range(ng) for part in range(3)]
    return jnp.concatenate(cols + [w_qkv[:, 3 * DIL_WIDTH:]], axis=1)


def _head_block_ones():
    r = lax.broadcasted_iota(jnp.int32, (DIL_OUT_WIDTH, DIL_OUT_WIDTH), 0) // HEAD_DIM
    c = lax.broadcasted_iota(jnp.int32, (DIL_OUT_WIDTH, DIL_OUT_WIDTH), 1) // HEAD_DIM
    return jnp.where(r == c, 1.0, 0.0).astype(BF16)


def _dil_mix_weights(l0, l1, l2):
    mx = jnp.maximum(jnp.maximum(l0, l1), l2)
    e0, e1, e2 = jnp.exp(l0 - mx), jnp.exp(l1 - mx), jnp.exp(l2 - mx)
    inv = 1.0 / (e0 + e1 + e2)
    return e0 * inv, e1 * inv, e2 * inv


def _dil_view_spec(dilation):
    return pl.BlockSpec((DIL_RELAYOUT_ROWS // dilation, dilation * DIL_OUT_WIDTH), lambda i: (i, 0))


def _dil_mix_call(name, body, s, ins, in_specs, outs, n_relaid):
    out_specs = [_dil_view_spec(d or 1) for d, _ in outs]
    out_shape = [jax.ShapeDtypeStruct((s // (d or 1), (d or 1) * DIL_OUT_WIDTH), dt) for d, dt in outs]
    return _pcall(
        body,
        name=name,
        grid=(s // DIL_RELAYOUT_ROWS,),
        in_specs=in_specs,
        out_specs=out_specs,
        out_shape=out_shape,
        scratch_shapes=[_view_scratch(DIL_OUT_WIDTH)] * n_relaid,
        compiler_params=_cparams(1),
    )(*ins)


DIL_MIX_CHUNK = 64
_DIL_SLABS = DIL_OUT_WIDTH // LANES


def _dil_rows(refs, scratch):
    dils = [d for _, d in DIL_GROUPS]
    assert dils[0] == 1
    readers = [lambda rows, j, ref=refs[0]: ref[rows, j * LANES:(j + 1) * LANES]]
    for ref, scr, d in zip(refs[1:], scratch, dils[1:]):
        _rows_from_view(ref, scr, d, DIL_OUT_WIDTH)
        readers.append(lambda rows, j, scr=scr: scr[j, rows, :])
    return readers


def _dil_mix_chunks(step):
    def chunk(c, carry):
        step(pl.ds(pl.multiple_of(c * DIL_MIX_CHUNK, DIL_MIX_CHUNK), DIL_MIX_CHUNK))
        return carry

    lax.fori_loop(0, DIL_RELAYOUT_ROWS // DIL_MIX_CHUNK, chunk, 0, unroll=4)


def _dil_mix_fwd(os_, lses):
    ng = len(DIL_GROUPS)
    s = os_[0].shape[0]

    def body(*refs):
        o_refs, l_refs, out_ref, scratch = refs[:ng], refs[ng:2 * ng], refs[2 * ng], refs[2 * ng + 1:]
        o_at = _dil_rows(o_refs, scratch[:ng - 1])
        l_at = _dil_rows(l_refs, scratch[ng - 1:])

        def step(rows):
            for j in range(_DIL_SLABS):
                w0, w1, w2 = _dil_mix_weights(*[at(rows, j) for at in l_at])
                o0, o1, o2 = [at(rows, j) for at in o_at]
                out_ref[rows, j * LANES:(j + 1) * LANES] = (w0 * o0 + w1 * o1 + w2 * o2).astype(out_ref.dtype)

        _dil_mix_chunks(step)

    specs = [_dil_view_spec(d) for _, d in DIL_GROUPS]
    (o_a,) = _dil_mix_call("dil_mix_fwd", body, s, list(os_) + list(lses), specs * 2, [(None, BF16)], 2 * (ng - 1))
    return o_a


def _dil_mix_bwd(do_a, os_, lses):
    ng = len(DIL_GROUPS)
    s = do_a.shape[0]
    dils = [d for _, d in DIL_GROUPS]

    def body(*refs):
        do_ref, o_refs, l_refs = refs[0], refs[1:1 + ng], refs[1 + ng:1 + 2 * ng]
        out_refs, scratch = refs[1 + 2 * ng:1 + 4 * ng], refs[1 + 4 * ng:]
        o_at = _dil_rows(o_refs, scratch[:ng - 1])
        l_at = _dil_rows(l_refs, scratch[ng - 1:2 * (ng - 1)])
        spare = iter(scratch[2 * (ng - 1):])
        staged = [None if dils[n % ng] == 1 else next(spare) for n in range(2 * ng)]
        ones = _head_block_ones()

        def step(rows):
            do = do_ref[rows, :].astype(F32)
            ws, prods = [], []
            for j in range(_DIL_SLABS):
                w0, w1, w2 = _dil_mix_weights(*[at(rows, j) for at in l_at])
                o0, o1, o2 = [at(rows, j) for at in o_at]
                ws.append((w0, w1, w2))
                prods.append(do[:, j * LANES:(j + 1) * LANES] * (w0 * o0 + w1 * o1 + w2 * o2))
            tot = _dot_hi_lo(jnp.concatenate(prods, axis=1), ones)
            for j in range(_DIL_SLABS):
                slab = slice(j * LANES, (j + 1) * LANES)
                vals = [w * do[:, slab] for w in ws[j]] + [-w * tot[:, slab] for w in ws[j]]
                for val, dst, scr in zip(vals, out_refs, staged):
                    if scr is None:
                        dst[rows, slab] = val.astype(dst.dtype)
                    else:
                        scr[j, rows, :] = val

        _dil_mix_chunks(step)
        for n, (dst, scr) in enumerate(zip(out_refs, staged)):
            if scr is not None:
                _rows_to_view(scr, dst, dils[n % ng], DIL_OUT_WIDTH)

    specs = [_dil_view_spec(d) for d in dils]
    return _dil_mix_call(
        "dil_mix_bwd", body, s, [do_a] + list(os_) + list(lses), [_dil_view_spec(1)] + specs * 2,
        [(d, BF16) for d in dils] + [(d, F32) for d in dils], 4 * (ng - 1))


_SB_Q0 = 3 * DIL_WIDTH // LANES
_SB_K0 = _SB_Q0 + SB_WIDTH // LANES
_SB_V0 = _SB_K0 + SB_WIDTH // LANES


_EXP_CLAMP = 88.0
_SB_DEAD = 104.0


def _tri(t, op):
    r = lax.broadcasted_iota(jnp.int32, (t, t), 0)
    c = lax.broadcasted_iota(jnp.int32, (t, t), 1)
    return jnp.where(op(r, c), 1.0, 0.0).astype(BF16)


def _softplus(z):
    return jnp.maximum(z, jnp.log(1.0 + jnp.exp(jnp.minimum(z, _EXP_CLAMP))))


def _sb_chain_head(qm, kj, mask):
    z = _dot_nt(qm, kj)
    sp = _softplus(z)
    return (sp if mask is None else jnp.where(mask, sp, 0.0)), z - sp


def _sb_fwd(qkv, rider=None):
    s = qkv.shape[0]
    t = SB_TK
    assert s % (2 * t) == 0
    nq = s // (2 * t)
    n_pairs = SB_WIDTH // LANES

    def body(q_ref, k_ref, v_ref, o_ref, tot_ref, steps_ref):
        p, i = pl.program_id(0), pl.program_id(1)
        lane_hi = lax.broadcasted_iota(jnp.int32, (1, LANES), 1) // HEAD_DIM
        later = _tri(t, lambda r, c: r > c)
        causal = lax.broadcasted_iota(jnp.int32, (t, t), 1) < lax.broadcasted_iota(jnp.int32, (t, t), 0)
        qms = []
        for x in range(2):
            q = q_ref[pl.ds(x * t, t), :] * (1.0 / math.sqrt(HEAD_DIM))
            qms.append([jnp.where(lane_hi == hh, q, jnp.zeros_like(q)) for hh in range(2)])

        def tile(j):
            off = pl.multiple_of(j * t, t)
            return k_ref[pl.ds(off, t), :], v_ref[pl.ds(off, t), :]

        def step(groups, carry):
            kv = [tile(j) for _, j, _ in groups]
            chains = [(g, x, hh) for g, (x, _, _) in enumerate(groups) for hh in range(2)]
            heads = [_sb_chain_head(qms[x][hh], kv[g][0], causal if groups[g][2] else None) for g, x, hh in chains]
            sufs = [_dot(sp.astype(BF16), later) for sp, _ in heads]
            cur = [list(carry[0]), list(carry[1])]
            for (g, x, hh), (sp, lpos), suf in zip(chains, heads, sufs):
                c, acc = cur[x][hh]
                a = jnp.exp(lpos - suf - c)
                if groups[g][2]:
                    a = jnp.where(causal, a, 0.0)
                cur[x][hh] = (c + jnp.sum(sp, axis=1, keepdims=True), acc + _dot(a.astype(BF16), kv[g][1]))
            return (tuple(cur[0]), tuple(cur[1]))

        def lowest(carry):
            return jnp.min(jnp.minimum(jnp.minimum(carry[0][0][0], carry[0][1][0]),
                                       jnp.minimum(carry[1][0][0], carry[1][1][0])))

        zero = (jnp.zeros((t, 1), F32), jnp.zeros((t, LANES), F32))
        start = ((zero, zero), (zero, zero))
        carry = lax.cond(
            i == 0,
            lambda ca: step([(0, 0, True), (1, 1, True), (1, 0, False)], ca),
            lambda ca: step([(0, 2 * i, True), (1, 2 * i + 1, True), (0, 2 * i - 1, False), (1, 2 * i, False)], ca),
            start)

        def walk(state):
            n, ca, _ = state
            ca = step([(0, 2 * i - 2 - n, False), (1, 2 * i - 1 - n, False)], ca)
            return n + 1, ca, lowest(ca)

        n_more, carry, low = lax.while_loop(
            lambda st: jnp.logical_and(st[0] + 1 < 2 * i, st[2] <= _SB_DEAD), walk, (jnp.int32(0), carry, lowest(carry)))
        b_last = jnp.logical_and(jnp.logical_and(i > 0, n_more + 1 == 2 * i), low <= _SB_DEAD)
        carry = lax.cond(b_last, lambda ca: step([(1, 0, False)], ca), lambda ca: ca, carry)
        for x in range(2):
            (c0, acc0), (c1, acc1) = carry[x]
            o_ref[pl.ds(x * t, t), :] = jnp.where(lane_hi == 0, acc0, acc1).astype(o_ref.dtype)
            tot_ref[pl.ds(x * t, t), :] = jnp.where(lane_hi == 0, c0, c1)
        steps_ref[p, i] = 1 + n_more + b_last.astype(jnp.int32)

    return _call(
        body, (qkv, qkv, qkv), rider,
        name="sb_fwd",
        grid=(n_pairs, nq),
        in_specs=[
            pl.BlockSpec((2 * t, LANES), lambda p, i: (i, _SB_Q0 + p)),
            pl.BlockSpec((s, LANES), lambda p, i: (0, _SB_K0 + p)),
            pl.BlockSpec((s, LANES), lambda p, i: (0, _SB_V0 + p)),
        ],
        out_specs=[pl.BlockSpec((2 * t, LANES), lambda p, i: (i, p))] * 2 + [pl.BlockSpec(memory_space=pltpu.SMEM)],
        out_shape=[jax.ShapeDtypeStruct((s, SB_WIDTH), BF16), jax.ShapeDtypeStruct((s, SB_WIDTH), F32),
                   jax.ShapeDtypeStruct((n_pairs, nq), jnp.int32)],
        compiler_params=_cparams(2),
    )


def _sb_bwd(qkv, do_b, tot_b, n_steps):
    s = qkv.shape[0]
    t = SB_TK
    nq = s // (2 * t)
    n_pairs = SB_WIDTH // LANES
    scale = 1.0 / math.sqrt(HEAD_DIM)

    def body(steps_ref, q_ref, k_ref, v_ref, do_ref, tot_ref, dq_ref, dk_ref, dv_ref):
        p, i = pl.program_id(0), pl.program_id(1)

        @pl.when(i == 0)
        def _():
            dk_ref[...] = jnp.zeros_like(dk_ref)
            dv_ref[...] = jnp.zeros_like(dv_ref)

        lane = lax.broadcasted_iota(jnp.int32, (1, LANES), 1)
        lane_hi = lane // HEAD_DIM
        later = _tri(t, lambda r, c: r > c)
        before = _tri(t, lambda r, c: r < c)
        causal = lax.broadcasted_iota(jnp.int32, (t, t), 1) < lax.broadcasted_iota(jnp.int32, (t, t), 0)
        qms, doms, tots = [], [], []
        for x in range(2):
            rows = pl.ds(x * t, t)
            q, do, tot_all = q_ref[rows, :] * scale, do_ref[rows, :], tot_ref[rows, :]
            qms.append([jnp.where(lane_hi == hh, q, jnp.zeros_like(q)) for hh in range(2)])
            doms.append([jnp.where(lane_hi == hh, do, jnp.zeros_like(do)) for hh in range(2)])
            tots.append([jnp.sum(jnp.where(lane == hh * HEAD_DIM, tot_all, 0.0), axis=1, keepdims=True)
                         for hh in range(2)])

        def step(groups, carry):
            offs = [pl.multiple_of(j * t, t) for _, j, _ in groups]
            ks = [k_ref[pl.ds(off, t), :] for off in offs]
            vs = [v_ref[pl.ds(off, t), :] for off in offs]
            chains = [(g, x, hh) for g, (x, _, _) in enumerate(groups) for hh in range(2)]
            heads = [_sb_chain_head(qms[x][hh], ks[g], causal if groups[g][2] else None) for g, x, hh in chains]
            sufs = [_dot(sp.astype(BF16), later) for sp, _ in heads]
            das = [_dot_nt(doms[x][hh], vs[g]) for g, x, hh in chains]
            cur = [list(carry[0]), list(carry[1])]
            sigs, gs, abs_, cg_before = [], [], [], []
            for (g_, x, hh), (sp, lpos), suf, da in zip(chains, heads, sufs, das):
                cl, cg, dq = cur[x][hh]
                cl = cl + jnp.sum(sp, axis=1, keepdims=True)
                sig = jnp.exp(lpos)
                a = sig * jnp.exp(-suf - (tots[x][hh] - cl))
                if groups[g_][2]:
                    a = jnp.where(causal, a, 0.0)
                g = a * da
                sigs.append(sig)
                gs.append(g)
                abs_.append(a.astype(BF16))
                cg_before.append(cg)
                cur[x][hh] = (cl, cg + jnp.sum(g, axis=1, keepdims=True), dq)
            prefs = [_dot(g.astype(BF16), before) for g in gs]
            dvs = [_dot_tn(ab, doms[x][hh]) for (_, x, hh), ab in zip(chains, abs_)]
            dzs = []
            for (g_, x, hh), sig, g, pref, cg in zip(chains, sigs, gs, prefs, cg_before):
                dz = g - sig * (g + pref + cg)
                if groups[g_][2]:
                    dz = jnp.where(causal, dz, 0.0)
                dzs.append(dz.astype(BF16))
            dqs = [_dot(dz, ks[g_]) for (g_, x, hh), dz in zip(chains, dzs)]
            dks = [_dot_tn(dz, qms[x][hh]) for (_, x, hh), dz in zip(chains, dzs)]
            for n, (_, x, hh) in enumerate(chains):
                cl, cg, dq = cur[x][hh]
                cur[x][hh] = (cl, cg, dq + dqs[n])
            for g_, off in enumerate(offs):
                dk_ref[pl.ds(off, t), :] += dks[2 * g_] + dks[2 * g_ + 1]
                dv_ref[pl.ds(off, t), :] += dvs[2 * g_] + dvs[2 * g_ + 1]
            return (tuple(cur[0]), tuple(cur[1]))

        taken = steps_ref[p, i]
        n_full = jnp.minimum(taken, 2 * i)
        zero = (jnp.zeros((t, 1), F32), jnp.zeros((t, 1), F32), jnp.zeros((t, LANES), F32))
        carry = ((zero, zero), (zero, zero))
        carry = lax.cond(jnp.logical_and(i > 0, taken > 2 * i), lambda ca: step([(1, 0, False)], ca), lambda ca: ca,
                         carry)
        carry = lax.fori_loop(
            0, n_full - 1,
            lambda n, ca: step([(0, 2 * i - n_full + n, False), (1, 2 * i + 1 - n_full + n, False)], ca), carry)
        carry = lax.cond(
            i == 0,
            lambda ca: step([(1, 0, False), (0, 0, True), (1, 1, True)], ca),
            lambda ca: step([(0, 2 * i - 1, False), (1, 2 * i, False), (0, 2 * i, True), (1, 2 * i + 1, True)], ca),
            carry)
        for x in range(2):
            dq = jnp.where(lane_hi == 0, carry[x][0][2], carry[x][1][2])
            dq_ref[pl.ds(x * t, t), :] = (dq * scale).astype(dq_ref.dtype)

    row_spec = pl.BlockSpec((2 * t, LANES), lambda p, i, ns: (i, p))
    full_spec = pl.BlockSpec((s, LANES), lambda p, i, ns: (0, p))
    return _pcall(
        body,
        name="sb_bwd",
        grid_spec=pltpu.PrefetchScalarGridSpec(
            num_scalar_prefetch=1,
            grid=(n_pairs, nq),
            in_specs=[
                pl.BlockSpec((2 * t, LANES), lambda p, i, ns: (i, _SB_Q0 + p)),
                pl.BlockSpec((s, LANES), lambda p, i, ns: (0, _SB_K0 + p)),
                pl.BlockSpec((s, LANES), lambda p, i, ns: (0, _SB_V0 + p)),
                row_spec, row_spec,
            ],
            out_specs=[row_spec, full_spec, full_spec],
        ),
        out_shape=[jax.ShapeDtypeStruct((s, SB_WIDTH), BF16), jax.ShapeDtypeStruct((s, SB_WIDTH), F32),
                   jax.ShapeDtypeStruct((s, SB_WIDTH), F32)],
        compiler_params=_cparams(2),
    )(n_steps, qkv, qkv, qkv, do_b, tot_b)


def _gates(gl, bg):
    return _sigmoid(gl[:, :D_MODEL] + bg[:, :D_MODEL]), _sigmoid(gl[:, D_MODEL:] + bg[:, D_MODEL:])


def _mixer_fwd(o_a, o_b, gl, x0, bg, g2, w_ud, w_us, w_out, tm):
    def epi(_, rows, consts):
        oa, ob, glv, x = rows
        bgv, g2v, wud, wus, wout = consts
        ga, gb = _gates(glv, bgv)
        merged = ga * _dot(oa, wud) + gb * _dot(ob, wus)
        x1 = x + _dot(merged.astype(BF16), wout)
        r, xh = _rms_stats(x1)
        return [x1, xh * g2v], []

    return _rowk("mixer_fwd", tm=tm, rows=[o_a, o_b, gl, x0], consts=[bg, g2, w_ud, w_us, w_out],
                 row_outs=[(D_MODEL, F32), (D_MODEL, BF16)], epilogue=epi)


def _mixer_bwd(dx1, o_a, o_b, gl, bg, w_ud, w_us, w_out, tm, rider=None):
    s = dx1.shape[0]
    nm = s // tm

    def body(dx_ref, oa_ref, ob_ref, gl_ref, bg_ref, wud_ref, wus_ref, wout_ref,
             doa_ref, dob_ref, dgl_ref, gwout_ref, gwud_ref, gwus_ref, gbg_ref, awout_ref, awud_ref, awus_ref):
        i = pl.program_id(0)
        dxb = dx_ref[...].astype(BF16)
        oa, ob = oa_ref[...], ob_ref[...]
        ga, gb = _gates(gl_ref[...], bg_ref[...])
        ua, ub = _dot(oa, wud_ref[...]), _dot(ob, wus_ref[...])
        merged = (ga * ua + gb * ub).astype(BF16)
        dm = _dot_nt(dxb, wout_ref[...])
        dua = (dm * ga).astype(BF16)
        dub = (dm * gb).astype(BF16)
        dgla = dm * ua * ga * (1.0 - ga)
        dglb = dm * ub * gb * (1.0 - gb)
        doa_ref[...] = _dot_nt(dua, wud_ref[...]).astype(doa_ref.dtype)
        dob_ref[...] = _dot_nt(dub, wus_ref[...]).astype(dob_ref.dtype)
        dgl_ref[:, :D_MODEL] = dgla.astype(dgl_ref.dtype)
        dgl_ref[:, D_MODEL:] = dglb.astype(dgl_ref.dtype)
        parts = [(gwout_ref, awout_ref, _dot_tn(merged, dxb)), (gwud_ref, awud_ref, _dot_tn(oa, dua)),
                 (gwus_ref, awus_ref, _dot_tn(ob, dub))]
        for out, r, v in parts:

            @pl.when(i == 0)
            def _(r=r, v=v):
                r[...] = v

            @pl.when(i > 0)
            def _(r=r, v=v):
                r[...] += v

            @pl.when(i == nm - 1)
            def _(out=out, r=r):
                if len(out.shape) == 2:
                    out[...] = r[...].astype(out.dtype)
                else:
                    for p in range(N_DEV):
                        out[p] = r[:, p * out.shape[2]:(p + 1) * out.shape[2]].astype(out.dtype)

        sa = jnp.sum(dgla, axis=0, keepdims=True)
        sb = jnp.sum(dglb, axis=0, keepdims=True)

        @pl.when(i == 0)
        def _():
            gbg_ref[:, :D_MODEL] = sa
            gbg_ref[:, D_MODEL:] = sb

        @pl.when(i > 0)
        def _():
            gbg_ref[:, :D_MODEL] += sa
            gbg_ref[:, D_MODEL:] += sb

    row = lambda w: pl.BlockSpec((tm, w), lambda i: (i, 0))
    full = lambda a: pl.BlockSpec(a.shape, lambda i: (0, 0), pipeline_mode=pl.Buffered(1))
    chunks = lambda r: (N_DEV, r, D_MODEL // N_DEV)
    return _call(
        body, (dx1, o_a, o_b, gl, bg, w_ud, w_us, w_out), rider,
        name="mixer_bwd",
        grid=(nm,),
        in_specs=[row(D_MODEL), row(DIL_OUT_WIDTH), row(SB_WIDTH), row(2 * D_MODEL),
                  full(bg), full(w_ud), full(w_us), full(w_out)],
        out_specs=[row(DIL_OUT_WIDTH), row(SB_WIDTH), row(2 * D_MODEL),
                   pl.BlockSpec((D_MODEL, D_MODEL), lambda i: (0, 0)),
                   pl.BlockSpec(chunks(DIL_OUT_WIDTH), lambda i: (0, 0, 0)),
                   pl.BlockSpec(chunks(SB_WIDTH), lambda i: (0, 0, 0)),
                   pl.BlockSpec((1, 2 * D_MODEL), lambda i: (0, 0))],
        out_shape=[jax.ShapeDtypeStruct((s, DIL_OUT_WIDTH), BF16), jax.ShapeDtypeStruct((s, SB_WIDTH), BF16),
                   jax.ShapeDtypeStruct((s, 2 * D_MODEL), BF16),
                   jax.ShapeDtypeStruct((D_MODEL, D_MODEL), BF16), jax.ShapeDtypeStruct(chunks(DIL_OUT_WIDTH), BF16),
                   jax.ShapeDtypeStruct(chunks(SB_WIDTH), BF16), jax.ShapeDtypeStruct((1, 2 * D_MODEL), F32)],
        scratch_shapes=[pltpu.VMEM((D_MODEL, D_MODEL), F32), pltpu.VMEM((DIL_OUT_WIDTH, D_MODEL), F32),
                        pltpu.VMEM((SB_WIDTH, D_MODEL), F32)],
        compiler_params=_cparams(1),
    )


def _all_gather(shards):
    n = len(shards)

    def body(*refs):
        x_refs, out_refs = refs[:n], refs[n:2 * n]
        send_sems, recv_sems, local_sems = refs[2 * n:]
        x, y, c = lax.axis_index("x"), lax.axis_index("y"), lax.axis_index("c")
        me, sibling = (x, y, c), (x, y, 1 - c)
        chips = [(1 - x, y), (x, 1 - y), (1 - x, 1 - y)]

        def slot(a, px, py, pc):
            return out_refs[a].at[4 * px + 2 * py + pc]

        def copy(a, k, block, to, own=False):
            return pltpu.make_async_remote_copy(
                src_ref=x_refs[a] if own else slot(a, *block), dst_ref=slot(a, *block),
                send_sem=send_sems.at[7 * a + k], recv_sem=recv_sems.at[7 * a + k], device_id=to, device_id_type=_MESH)

        mine = [pltpu.make_async_copy(x_refs[a], slot(a, *me), local_sems.at[a]) for a in range(n)]
        for cp in mine:
            cp.start()
        first = []
        for a in range(n):
            first.append(copy(a, 0, me, sibling, own=True))
            first += [copy(a, 1 + j, me, (*chip, c), own=True) for j, chip in enumerate(chips)]
        for cp in first:
            cp.start()
        passed = []
        for a in range(n):
            for j, chip in enumerate(chips):
                copy(a, 1 + j, (*chip, c), me).wait_recv()
                passed.append(copy(a, 4 + j, (*chip, c), sibling))
                passed[-1].start()
        for a in range(n):
            copy(a, 0, sibling, me).wait_recv()
            for j, chip in enumerate(chips):
                copy(a, 4 + j, (*chip, 1 - c), me).wait_recv()
        for cp in first + passed:
            cp.wait_send()
        for cp in mine:
            cp.wait()

    return _pcall(
        body,
        name="all_gather_weights",
        in_specs=[_HBM] * n,
        out_specs=[_HBM] * n,
        out_shape=[jax.ShapeDtypeStruct((N_DEV,) + s.shape, s.dtype) for s in shards],
        scratch_shapes=[pltpu.SemaphoreType.DMA((7 * n,)), pltpu.SemaphoreType.DMA((7 * n,)),
                        pltpu.SemaphoreType.DMA((n,))],
    )(*shards)


def _exchange(chunks):
    n = len(chunks)

    def body(*refs):
        g_refs, o_refs = refs[:n], refs[n:2 * n]
        send_sems, recv_sems, local_sems = refs[2 * n:]
        x, y, c = lax.axis_index("x"), lax.axis_index("y"), lax.axis_index("c")
        me = 4 * x + 2 * y + c
        own = [pltpu.make_async_copy(g_refs[a].at[me], o_refs[a].at[me], local_sems.at[a]) for a in range(n)]
        for cp in own:
            cp.start()
        copies = []
        for a in range(n):
            for k in range(1, N_DEV):
                px, py, pc = x ^ (k >> 2), y ^ ((k >> 1) & 1), c ^ (k & 1)
                peer = 4 * px + 2 * py + pc
                copies.append(pltpu.make_async_remote_copy(
                    src_ref=g_refs[a].at[peer], dst_ref=o_refs[a].at[me], send_sem=send_sems.at[7 * a + k - 1],
                    recv_sem=recv_sems.at[7 * a + k - 1], device_id=(px, py, pc), device_id_type=_MESH))
        for cp in copies:
            cp.start()
        for cp in copies:
            cp.wait()
        for cp in own:
            cp.wait()

    return _pcall(
        body,
        name="exchange_grads",
        in_specs=[_HBM] * n,
        out_specs=[_HBM] * n,
        out_shape=[jax.ShapeDtypeStruct(g.shape, g.dtype) for g in chunks],
        scratch_shapes=[pltpu.SemaphoreType.DMA((7 * n,)), pltpu.SemaphoreType.DMA((7 * n,)),
                        pltpu.SemaphoreType.DMA((n,))],
    )(*chunks)


def _reduce_adamw(name, parts, w, m, v, tr):
    _, rows, cols = parts.shape
    tr = min(tr, rows)
    assert rows % tr == 0
    c1 = 1.0 / (1.0 - ADAM_B1 ** ADAM_STEP)
    c2 = 1.0 / (1.0 - ADAM_B2 ** ADAM_STEP)

    def body(p_ref, w_ref, m_ref, v_ref, g_out, d_out, m_out, v_out):
        g = p_ref[0].astype(F32)
        for d in range(1, N_DEV):
            g = g + p_ref[d].astype(F32)
        mn = ADAM_B1 * m_ref[...] + (1.0 - ADAM_B1) * g
        vn = ADAM_B2 * v_ref[...] + (1.0 - ADAM_B2) * (g * g)
        g_out[...] = g
        m_out[...] = mn
        v_out[...] = vn
        d_out[...] = -ADAM_LR * ((mn * c1) / (jnp.sqrt(vn * c2) + ADAM_EPS) + ADAM_WD * w_ref[...])

    spec = pl.BlockSpec((tr, cols), lambda i: (i, 0))
    return _pcall(
        body,
        name=name,
        grid=(rows // tr,),
        in_specs=[pl.BlockSpec((N_DEV, tr, cols), lambda i: (0, i, 0)), spec, spec, spec],
        out_specs=[spec] * 4,
        out_shape=[jax.ShapeDtypeStruct((rows, cols), F32)] * 4,
        compiler_params=_cparams(1),
    )(parts, w, m, v)


_SHARDED = ("w_in", "w_up_dil", "w_up_sb", "w_out", "w_mlp_in", "w_mlp_out")
_FULL_SHAPES = {"w_in": (D_MODEL, IN_COLS), "w_up_dil": (DIL_OUT_WIDTH, D_MODEL), "w_up_sb": (SB_WIDTH, D_MODEL),
                "w_out": (D_MODEL, D_MODEL), "w_mlp_in": (D_MODEL, D_FF), "w_mlp_out": (D_FF, D_MODEL)}
_ROW_SHARDED = ("w_out", "w_mlp_out")


def _shard_shape(name):
    r, c = _FULL_SHAPES[name]
    return (r // N_DEV, c) if name in _ROW_SHARDED else (r, c // N_DEV)


def _assemble(name, gathered):
    r, c = _shard_shape(name)
    if name in _ROW_SHARDED:
        return gathered.reshape(N_DEV * r, c)
    return gathered.transpose(1, 0, 2).reshape(r, N_DEV * c)


def _chunk(name, full):
    r, c = _shard_shape(name)
    if name in _ROW_SHARDED:
        return full.reshape(N_DEV, r, c)
    return full.reshape(r, N_DEV, c).transpose(1, 0, 2)


_SMALL = (("norm_mix_g", D_MODEL), ("b_gate", 2 * D_MODEL), ("norm_mlp_g", D_MODEL), ("norm_final_g", D_MODEL))
_SMALL_N = sum(n for _, n in _SMALL) + LANES


def _pack_small(vals, tail):
    return jnp.concatenate([vals[n].reshape(1, -1) for n, _ in _SMALL] + [tail], axis=1)


def _unpack_small(vec, shapes):
    out, pos = {}, 0
    for n, width in _SMALL:
        out[n] = vec[:, pos:pos + width].reshape(shapes[n])
        pos += width
    return out, vec[:, pos:]


def kernel(x, norm_mix_g, w_in, b_gate, w_up_dil, w_up_sb, w_out, norm_mlp_g, w_mlp_in, w_mlp_out, norm_final_g, loss_target, m_norm_mix_g, m_w_in, m_b_gate, m_w_up_dil, m_w_up_sb, m_w_out, m_norm_mlp_g, m_w_mlp_in, m_w_mlp_out, m_norm_final_g, v_norm_mix_g, v_w_in, v_b_gate, v_w_up_dil, v_w_up_sb, v_w_out, v_norm_mlp_g, v_w_mlp_in, v_w_mlp_out, v_norm_final_g):
    given = dict(locals())
    s = x.shape[1]
    x0 = x.reshape(s, D_MODEL)
    target = loss_target.reshape(s, D_MODEL)
    g1 = norm_mix_g.reshape(1, D_MODEL)
    g2 = norm_mlp_g.reshape(1, D_MODEL)
    g3 = norm_final_g.reshape(1, D_MODEL)
    bg = b_gate.reshape(1, 2 * D_MODEL)
    w_shards = {n: given[n].reshape(_shard_shape(n)) for n in _SHARDED}
    m_shards = {n: given["m_" + n].reshape(_shard_shape(n)) for n in _SHARDED}
    v_shards = {n: given["v_" + n].reshape(_shard_shape(n)) for n in _SHARDED}

    shard_b = {n: w_shards[n].astype(BF16) for n in _SHARDED}
    (gathered_w_in,) = _all_gather([shard_b["w_in"]])
    w_in_f = _assemble("w_in", gathered_w_in)
    w_qkv, w_gl = _group_major(w_in_f[:, :QKV_COLS]), w_in_f[:, QKV_COLS:]
    full = {}

    def norm1(_, rows, consts):
        _, xh = _rms_stats(rows[0])
        return [xh * consts[0]], []

    (h1,) = _rowk("norm_mix", tm=1024, rows=[x0], consts=[g1], row_outs=[(D_MODEL, BF16)], epilogue=norm1)
    qkv, (land,) = _mm("proj_qkv", h1, w_qkv, out_dtype=BF16, tm=1024, tn=768, tk=D_MODEL,
                       rider=_Spread([shard_b["w_mlp_in"]], chunked=False))
    full["w_mlp_in"] = _assemble("w_mlp_in", land)
    gl = _mm("proj_gates", h1, w_gl, out_dtype=BF16, tm=1024, tn=1024, tk=D_MODEL)
    views = [_dil_view(qkv, g) for g in range(len(DIL_GROUPS))]
    dil = [_dil_fwd(views[g], g) for g in range(len(DIL_GROUPS))]
    os_, lses = [d[0] for d in dil], [d[1] for d in dil]
    o_a = _dil_mix_fwd(os_, lses)
    riding = ("w_mlp_out", "w_out", "w_up_sb", "w_up_dil")
    (o_b, tot_b, sb_steps), lands = _sb_fwd(qkv, rider=_Spread([shard_b[n] for n in riding], chunked=False))
    full.update({n: _assemble(n, land) for n, land in zip(riding, lands)})
    x1, h2 = _mixer_fwd(o_a, o_b, gl, x0, bg, g2, full["w_up_dil"], full["w_up_sb"], full["w_out"], 512)
    f = _mm("mlp_in", h2, full["w_mlp_in"], out_dtype=BF16, tm=1024, tn=1024, tk=D_MODEL,
            epilogue=lambda r, _: jnp.square(jnp.maximum(r, 0.0)))

    def head(acc, rows, consts):
        x1v, tv = rows
        g3v = consts[0]
        x2 = x1v + acc
        r, xh = _rms_stats(x2)
        diff = xh * g3v - tv
        loss = (0.5 / D_MODEL) * jnp.sum(jnp.sum(diff * diff, axis=0, keepdims=True), axis=1, keepdims=True)
        dy = diff * (1.0 / D_MODEL)
        dx2, dg = _rms_bwd(dy, xh, r, g3v)
        return [dx2, dx2], [dg, jnp.broadcast_to(loss, (1, LANES))]

    dx2, dx2b, gg3, loss_part = _rowk(
        "mlp_out_loss", a=f, w=full["w_mlp_out"], tm=512, tk=D_FF, rows=[x1, target], consts=[g3],
        row_outs=[(D_MODEL, F32), (D_MODEL, BF16)], acc_outs=[D_MODEL, LANES], epilogue=head)

    da = _mm("mlp_out_bwd", dx2b, full["w_mlp_out"], tb=True, out_dtype=BF16, tm=1024, tn=1024, tk=D_MODEL, extra=f,
             epilogue=lambda r, fv: r * (2.0 * jnp.sqrt(fv.astype(F32))))
    g_w_mlp_out = _mm("grad_w_mlp_out", f, dx2b, ta=True, out_dtype=BF16, tm=1024, tn=1024, tk=2048)
    g_w_mlp_in = _mm("grad_w_mlp_in", h2, da, ta=True, out_dtype=BF16, tm=1024, tn=1024, tk=2048, col_chunks=N_DEV)

    def norm_bwd(acc, rows, consts):
        xv, dres = rows
        r, xh = _rms_stats(xv)
        dx, dg = _rms_bwd(acc, xh, r, consts[0])
        return [dres + dx], [dg]

    parts = {}
    (dx1, gg2), (parts["w_mlp_in"],) = _rowk(
        "mlp_in_bwd", a=da, w=full["w_mlp_in"], nt=True, tm=512, tk=D_FF, rows=[x1, dx2], consts=[g2],
        row_outs=[(D_MODEL, F32)], acc_outs=[D_MODEL], epilogue=norm_bwd,
        rider=_Spread([g_w_mlp_in], chunked=True))
    (do_a, do_b, dgl, g_w_out, g_w_ud, g_w_us, g_bg), (parts["w_mlp_out"],) = _mixer_bwd(
        dx1, o_a, o_b, gl, bg, full["w_up_dil"], full["w_up_sb"], full["w_out"], 512,
        rider=_Spread([_chunk("w_mlp_out", g_w_mlp_out)], chunked=True))
    mix = _dil_mix_bwd(do_a, os_, lses)
    small_three = {"w_out": _chunk("w_out", g_w_out), "w_up_sb": g_w_us, "w_up_dil": g_w_ud}
    grads, lands = _dil_bwd(views[0], mix[0], lses[0], mix[3], 0,
                            rider=_Spread(list(small_three.values()), chunked=True))
    parts.update(dict(zip(small_three, lands)))
    dil_b = [grads] + [_dil_bwd(views[g], mix[g], lses[g], mix[3 + g], g) for g in (1, 2)]
    dq_b, dk_b, dv_b = _sb_bwd(qkv, do_b, tot_b, sb_steps)
    dproj = [d[0] for d in dil_b] + [d[1] for d in dil_b] + [d[2] for d in dil_b] + [dq_b, dk_b, dv_b, dgl]
    g_w_in = jnp.concatenate([
        _grad_cols("grad_w_in_dil", h1, dproj[:9], tm=D_MODEL, tk=1024),
        _grad_cols("grad_w_in_sb", h1, dproj[9:12], tm=D_MODEL, tk=1024),
        _grad_cols("grad_w_in_gates", h1, dproj[12:], tm=D_MODEL, tk=1024)], axis=1)
    (grad_x, gg1), (parts["w_in"],) = _rowk(
        "in_proj_bwd", a=dproj, w=w_in_f, nt=True, tm=512, tk=IN_COLS, rows=[x0, dx1], consts=[g1],
        row_outs=[(D_MODEL, F32)], acc_outs=[D_MODEL], epilogue=norm_bwd,
        rider=_Spread([_chunk("w_in", g_w_in)], chunked=True))

    small_part = _pack_small({"norm_mix_g": gg1, "b_gate": g_bg, "norm_mlp_g": gg2, "norm_final_g": gg3}, loss_part)
    (small_parts,) = _exchange([jnp.broadcast_to(small_part[None], (N_DEV, 1, _SMALL_N))])

    tags = ("grad_", "delta_", "new_m_", "new_v_")
    outs = {}
    for n, p in parts.items():
        res = _reduce_adamw("adamw_" + n, p, w_shards[n], m_shards[n], v_shards[n], 256)
        for tag, val in zip(tags, res):
            outs[tag + n] = val.reshape(given[n].shape)
    small_w = _pack_small(given, jnp.zeros((1, LANES), F32))
    small_m = _pack_small({n: given["m_" + n] for n, _ in _SMALL}, jnp.zeros((1, LANES), F32))
    small_v = _pack_small({n: given["v_" + n] for n, _ in _SMALL}, jnp.ones((1, LANES), F32))
    small_res = _reduce_adamw("adamw_replicated", small_parts, small_w, small_m, small_v, 8)

    small_shapes = {n: given[n].shape for n, _ in _SMALL}
    for tag, small in zip(tags, small_res):
        small_vals, tail = _unpack_small(small, small_shapes)
        for n, val in small_vals.items():
            outs[tag + n] = val
        if tag == "grad_":
            loss = tail[0, 0]
    names = ["norm_mix_g", "w_in", "b_gate", "w_up_dil", "w_up_sb", "w_out", "norm_mlp_g", "w_mlp_in", "w_mlp_out",
             "norm_final_g"]
    return (loss, grad_x.reshape(x.shape), *[outs["grad_" + n] for n in names], *[outs["delta_" + n] for n in names],
            *[outs["new_m_" + n] for n in names], *[outs["new_v_" + n] for n in names])
```

```python
import functools
import math

import jax
import jax.numpy as jnp
from jax import lax
from jax.experimental import pallas as pl
from jax.experimental.pallas import tpu as pltpu

_pcall = pl.pallas_call

F32 = jnp.float32
BF16 = jnp.bfloat16

D_MODEL = 1024
HEAD_DIM = 64
DIL_GROUPS = ((128, 1), (512, 4), (2048, 16))
DIL_HEADS_PER_GROUP = 4
N_DIL_HEADS = 12
N_SB_HEADS = 8
DIL_WIDTH = 768
DIL_OUT_WIDTH = 256
SB_WIDTH = 512
D_FF = 4096
BLOCK = 128
RMS_EPS = 1e-6
NEG_INF = -1e30
QKV_COLS = 3 * DIL_WIDTH + 3 * SB_WIDTH
IN_COLS = QKV_COLS + 2 * D_MODEL
N_DEV = 8

ADAM_LR = 0.001
ADAM_B1 = 0.9
ADAM_B2 = 0.999
ADAM_EPS = 1e-08
ADAM_WD = 0.01
ADAM_STEP = 10

VMEM_LIMIT = 56 * 1024 * 1024
SB_TK = 256
LANES = 128

_ARB = pltpu.ARBITRARY


def _cparams(n_axes, **kw):
    return pltpu.CompilerParams(dimension_semantics=(_ARB,) * n_axes, vmem_limit_bytes=VMEM_LIMIT, **kw)


def _dot(a, b):
    return jnp.dot(a, b, preferred_element_type=F32)


def _dot_nt(a, b):
    return lax.dot_general(a, b, (((1,), (1,)), ((), ())), preferred_element_type=F32)


def _dot_tn(a, b):
    return lax.dot_general(a, b, (((0,), (0,)), ((), ())), preferred_element_type=F32)


def _split_hi_lo(x):
    hi = x.astype(BF16)
    lo = (x - hi.astype(F32)).astype(BF16)
    return hi, lo


def _dot_hi_lo(x, m):
    hi, lo = _split_hi_lo(x)
    return _dot(hi, m) + _dot(lo, m)


def _sigmoid(x):
    return 1.0 / (1.0 + jnp.exp(-x))


_HBM = pl.BlockSpec(memory_space=pltpu.HBM)
_MESH = pl.DeviceIdType.MESH


class _Spread:
    def __init__(self, srcs, chunked):
        self.srcs, self.chunked, self.n = list(srcs), chunked, len(srcs)

    def land_shapes(self):
        return [jax.ShapeDtypeStruct((N_DEV,) + (s.shape[1:] if self.chunked else s.shape), s.dtype) for s in self.srcs]

    def scratch(self):
        dma = pltpu.SemaphoreType.DMA
        return [dma((7 * self.n,)), dma((7 * self.n,)), dma((self.n,))]

    def copies(self, src_refs, land_refs, send_sems, recv_sems, local_sems):
        x, y, c = lax.axis_index("x"), lax.axis_index("y"), lax.axis_index("c")
        me = 4 * x + 2 * y + c
        out = []
        for a, (src, land) in enumerate(zip(src_refs, land_refs)):
            out.append(pltpu.make_async_copy(src.at[me] if self.chunked else src, land.at[me], local_sems.at[a]))
            for k in range(1, N_DEV):
                px, py, pc = x ^ (k >> 2), y ^ ((k >> 1) & 1), c ^ (k & 1)
                out.append(pltpu.make_async_remote_copy(
                    src_ref=src.at[4 * px + 2 * py + pc] if self.chunked else src, dst_ref=land.at[me],
                    send_sem=send_sems.at[7 * a + k - 1], recv_sem=recv_sems.at[7 * a + k - 1],
                    device_id=(px, py, pc), device_id_type=_MESH))
        return out


def _call(body, args, rider=None, **kw):
    if rider is None:
        return _pcall(body, **kw)(*args)
    grid = kw["grid"]
    single = not isinstance(kw["out_shape"], (list, tuple))
    out_specs = [kw["out_specs"]] if single else list(kw["out_specs"])
    out_shape = [kw["out_shape"]] if single else list(kw["out_shape"])
    in_specs, scratch = list(kw["in_specs"]), list(kw.get("scratch_shapes", []))
    n_in, n_out, n_s, n = len(in_specs), len(out_shape), len(scratch), rider.n

    def hosted(*refs):
        ins, srcs = refs[:n_in], refs[n_in:n_in + n]
        outs, lands = refs[n_in + n:n_in + n + n_out], refs[n_in + n + n_out:n_in + 2 * n + n_out]
        own_scratch, sems = refs[n_in + 2 * n + n_out:n_in + 2 * n + n_out + n_s], refs[n_in + 2 * n + n_out + n_s:]
        ids = [pl.program_id(d) for d in range(len(grid))]
        first = functools.reduce(jnp.logical_and, [i == 0 for i in ids])
        last = functools.reduce(jnp.logical_and, [i == g - 1 for i, g in zip(ids, grid)])
        copies = rider.copies(srcs, lands, *sems)

        @pl.when(first)
        def _():
            for cp in copies:
                cp.start()

        body(*ins, *outs, *own_scratch)

        @pl.when(last)
        def _():
            for cp in copies:
                cp.wait()

    kw = dict(kw, in_specs=in_specs + [_HBM] * n, out_specs=out_specs + [_HBM] * n,
              out_shape=out_shape + rider.land_shapes(), scratch_shapes=scratch + rider.scratch())
    res = _pcall(hosted, **kw)(*args, *rider.srcs)
    return (res[0] if single else list(res[:n_out])), list(res[n_out:])


def _mm(name, a, b, *, ta=False, tb=False, out_dtype, tm, tn, tk, epilogue=None, extra=None, rider=None,
        col_chunks=None):
    m = a.shape[1] if ta else a.shape[0]
    k = a.shape[0] if ta else a.shape[1]
    n = b.shape[0] if tb else b.shape[1]
    assert (b.shape[1] if tb else b.shape[0]) == k
    tm, tn, tk = min(tm, m), min(tn, n), min(tk, k)
    assert m % tm == 0 and n % tn == 0 and k % tk == 0, (name, m, n, k, tm, tn, tk)
    nk = k // tk
    dn = (((0 if ta else 1,), (1 if tb else 0,)), ((), ()))
    in_place = nk > 1 and epilogue is None and out_dtype == F32 and col_chunks is None
    cw = n // col_chunks if col_chunks else None
    assert cw is None or (tn % cw == 0 and cw % LANES == 0)

    def body(*refs):
        if extra is not None:
            a_ref, b_ref, e_ref, o_ref = refs[:4]
        else:
            a_ref, b_ref, o_ref = refs[:3]
            e_ref = None

        def finish(r):
            if epilogue is not None:
                r = epilogue(r, None if e_ref is None else e_ref[...])
            if cw is None:
                o_ref[...] = r.astype(out_dtype)
            else:
                for c in range(tn // cw):
                    o_ref[c] = r[:, c * cw:(c + 1) * cw].astype(out_dtype)

        part = lax.dot_general(a_ref[...].astype(BF16), b_ref[...].astype(BF16), dn, preferred_element_type=F32)
        if nk == 1:
            finish(part)
        else:
            acc_ref = o_ref if in_place else refs[-1]
            kk = pl.program_id(2)

            @pl.when(kk == 0)
            def _():
                acc_ref[...] = part

            @pl.when(kk > 0)
            def _():
                acc_ref[...] += part

            if not in_place:

                @pl.when(kk == nk - 1)
                def _():
                    finish(acc_ref[...])

    a_spec = pl.BlockSpec((tk, tm), lambda j, i, kk: (kk, i)) if ta else pl.BlockSpec((tm, tk), lambda j, i, kk: (i, kk))
    b_spec = pl.BlockSpec((tn, tk), lambda j, i, kk: (j, kk)) if tb else pl.BlockSpec((tk, tn), lambda j, i, kk: (kk, j))
    o_spec = pl.BlockSpec((tm, tn), lambda j, i, kk: (i, j))
    in_specs = [a_spec, b_spec]
    args = [a, b]
    if extra is not None:
        in_specs.append(o_spec)
        args.append(extra)
    out_shape = jax.ShapeDtypeStruct((m, n), out_dtype)
    if cw is not None:
        o_spec = pl.BlockSpec((tn // cw, tm, cw), lambda j, i, kk: (j, i, 0))
        out_shape = jax.ShapeDtypeStruct((col_chunks, m, cw), out_dtype)
    return _call(
        body, args, rider,
        name=name,
        grid=(n // tn, m // tm, nk),
        in_specs=in_specs,
        out_specs=o_spec,
        out_shape=out_shape,
        scratch_shapes=[pltpu.VMEM((tm, tn), F32)] if (nk > 1 and not in_place) else [],
        compiler_params=_cparams(3),
    )


def _grad_cols(name, a, parts, *, tm, tk, rider=None):
    k, m = a.shape
    n = sum(p.shape[1] for p in parts)
    assert m % tm == 0 and k % tk == 0
    nk = k // tk

    def body(*refs):
        a_ref, p_refs, o_ref, acc_ref = refs[0], refs[1:1 + len(parts)], refs[1 + len(parts)], refs[2 + len(parts)]
        kk = pl.program_id(1)
        side_by_side = jnp.concatenate([p_ref[...].astype(BF16) for p_ref in p_refs], axis=1)
        term = _dot_tn(a_ref[...].astype(BF16), side_by_side)

        @pl.when(kk == 0)
        def _():
            acc_ref[...] = term

        @pl.when(kk > 0)
        def _():
            acc_ref[...] += term

        @pl.when(kk == nk - 1)
        def _():
            o_ref[...] = acc_ref[...].astype(o_ref.dtype)

    return _call(
        body, [a] + list(parts), rider,
        name=name,
        grid=(m // tm, nk),
        in_specs=[pl.BlockSpec((tk, tm), lambda i, kk: (kk, i))]
        + [pl.BlockSpec((tk, p.shape[1]), lambda i, kk: (kk, 0)) for p in parts],
        out_specs=pl.BlockSpec((tm, n), lambda i, kk: (i, 0)),
        out_shape=jax.ShapeDtypeStruct((m, n), BF16),
        scratch_shapes=[pltpu.VMEM((tm, n), F32)],
        compiler_params=_cparams(2),
    )


def _rowk(name, *, a=None, w=None, nt=False, tm, tk=None, rows=(), consts=(), row_outs=(), acc_outs=(), epilogue,
          rider=None):
    has_mm = a is not None
    a_parts = list(a) if isinstance(a, (list, tuple)) else ([a] if has_mm else [])
    n_a = len(a_parts)
    m = a_parts[0].shape[0] if has_mm else rows[0].shape[0]
    assert m % tm == 0
    nm = m // tm
    if has_mm:
        k = sum(p.shape[1] for p in a_parts)
        n = w.shape[0] if nt else w.shape[1]
        tk = min(tk, k)
        assert k % tk == 0 and (n_a == 1 or tk == k)
        nk = k // tk
    else:
        nk = 1
    n_rows, n_consts, n_ro, n_ao = len(rows), len(consts), len(row_outs), len(acc_outs)

    def body(*refs):
        pos = 0
        if has_mm:
            a_refs, w_ref = refs[:n_a], refs[n_a]
            pos = n_a + 1
        row_refs = refs[pos:pos + n_rows]
        pos += n_rows
        const_refs = refs[pos:pos + n_consts]
        pos += n_consts
        ro_refs = refs[pos:pos + n_ro]
        pos += n_ro
        ao_refs = refs[pos:pos + n_ao]
        pos += n_ao
        i = pl.program_id(0)
        kk = pl.program_id(1)

        def finish(acc):
            ro_vals, ao_vals = epilogue(acc, [r[...] for r in row_refs], [c[...] for c in const_refs])
            for r, v in zip(ro_refs, ro_vals):
                r[...] = v.astype(r.dtype)
            for r, v in zip(ao_refs, ao_vals):

                @pl.when(i == 0)
                def _(r=r, v=v):
                    r[...] = v

                @pl.when(i > 0)
                def _(r=r, v=v):
                    r[...] += v

        if not has_mm:
            finish(None)
            return
        part, off = None, 0
        for a_ref in a_refs:
            width = a_ref.shape[1]
            cols = slice(None) if n_a == 1 else slice(off, off + width)
            av = a_ref[...].astype(BF16)
            term = _dot_nt(av, w_ref[:, cols]) if nt else _dot(av, w_ref[cols, :])
            part = term if part is None else part + term
            off += width
        if nk == 1:
            finish(part)
        else:
            acc_ref = refs[pos]

            @pl.when(kk == 0)
            def _():
                acc_ref[...] = part

            @pl.when(kk > 0)
            def _():
                acc_ref[...] += part

            @pl.when(kk == nk - 1)
            def _():
                finish(acc_ref[...])

    once = pl.Buffered(1)
    in_specs, args = [], []
    if has_mm:
        for part in a_parts:
            in_specs.append(pl.BlockSpec((tm, tk if n_a == 1 else part.shape[1]), lambda i, kk: (i, kk)))
        w_mode = once if nk == 1 else None
        in_specs.append(pl.BlockSpec((n, tk), lambda i, kk: (0, kk), pipeline_mode=w_mode) if nt
                        else pl.BlockSpec((tk, n), lambda i, kk: (kk, 0), pipeline_mode=w_mode))
        args += a_parts + [w]
    for r in rows:
        in_specs.append(pl.BlockSpec((tm, r.shape[1]), lambda i, kk: (i, 0)))
        args.append(r)
    for c in consts:
        in_specs.append(pl.BlockSpec(c.shape, lambda i, kk: (0,) * c.ndim, pipeline_mode=once))
        args.append(c)
    out_specs, out_shape = [], []
    for width, dt in row_outs:
        out_specs.append(pl.BlockSpec((tm, width), lambda i, kk: (i, 0)))
        out_shape.append(jax.ShapeDtypeStruct((m, width), dt))
    for width in acc_outs:
        out_specs.append(pl.BlockSpec((1, width), lambda i, kk: (0, 0)))
        out_shape.append(jax.ShapeDtypeStruct((1, width), F32))
    return _call(
        body, args, rider,
        name=name,
        grid=(nm, nk),
        in_specs=in_specs,
        out_specs=out_specs,
        out_shape=out_shape,
        scratch_shapes=[pltpu.VMEM((tm, n), F32)] if (has_mm and nk > 1) else [],
        compiler_params=_cparams(2),
    )


def _rms_stats(x):
    r = lax.rsqrt(jnp.mean(x * x, axis=-1, keepdims=True) + RMS_EPS)
    return r, x * r


def _rms_bwd(dh, xh, r, g):
    gy = dh * g
    dx = r * (gy - xh * jnp.mean(gy * xh, axis=-1, keepdims=True))
    return dx, jnp.sum(dh * xh, axis=0, keepdims=True)


def _alibi_slope(head):
    return 2.0 ** (-8.0 * (head + 1) / N_DIL_HEADS)


DIL_STEP_BLOCKS = 4


def _dil_band(first_block):
    qi = lax.broadcasted_iota(jnp.int32, (BLOCK, 2 * BLOCK), 0)
    kj = lax.broadcasted_iota(jnp.int32, (BLOCK, 2 * BLOCK), 1)
    steps = qi + BLOCK - kj
    valid = (steps >= 0) & (steps <= BLOCK)
    if first_block is not False:
        valid = valid & ((kj >= BLOCK) | jnp.logical_not(first_block))
    return steps.astype(F32), valid


def _dil_step_specs(ncb, cols, nblk, clamp):
    def own(col):
        return pl.BlockSpec((nblk * BLOCK, DIL_OUT_WIDTH), lambda r, i: (clamp(i), r * ncb + col))

    def before(col):
        return pl.BlockSpec((BLOCK, DIL_OUT_WIDTH), lambda r, i: (jnp.maximum(clamp(i) * nblk - 1, 0), r * ncb + col))

    return [own(cols[0]), own(cols[1]), before(cols[1]), own(cols[2]), before(cols[2])]


DIL_RELAYOUT_ROWS = 1024


def _view_scratch(width):
    return pltpu.VMEM((width // LANES, DIL_RELAYOUT_ROWS, LANES), F32)


def _rows_from_view(src, scr, d, w):
    sub = src.shape[0]
    for j in range(w // LANES):
        for r in range(d):
            scr[j, pl.ds(r, sub, stride=d), :] = src[:, r * w + j * LANES:r * w + (j + 1) * LANES].astype(F32)


def _rows_to_view(scr, dst, d, w):
    sub = dst.shape[0]
    for j in range(w // LANES):
        for r in range(d):
            dst[:, r * w + j * LANES:r * w + (j + 1) * LANES] = scr[j, pl.ds(r, sub, stride=d), :].astype(dst.dtype)


def _dil_relayout(name, xs, dilation, to_view, col_block=0, width=None):
    d = dilation
    tm = DIL_RELAYOUT_ROWS
    rows = tm // d
    if to_view:
        s = xs[0].shape[0]
        widths = [width or x.shape[1] for x in xs]
    else:
        s = xs[0].shape[0] * d
        widths = [v.shape[1] // d for v in xs]
    assert s % tm == 0 and all(w % LANES == 0 for w in widths) and all(x.dtype == BF16 for x in xs)
    n = len(xs)
    blk = 256
    per = blk // d
    assert per % 16 == 0 and tm % blk == 0

    def body(*refs):
        in_refs, out_refs = refs[:n], refs[n:]
        i0 = lax.broadcasted_iota(jnp.int32, (blk, blk), 0)
        i1 = lax.broadcasted_iota(jnp.int32, (blk, blk), 1)
        sort = (i1 == (i0 % per) * d + i0 // per) if to_view else (i0 == (i1 % per) * d + i1 // per)
        sort = jnp.where(sort, 1.0, 0.0).astype(BF16)
        for src, dst, w in zip(in_refs, out_refs, widths):
            for b in range(tm // blk):
                if to_view:
                    y = _dot(sort, src[b * blk:(b + 1) * blk, :]).astype(BF16)
                    for r in range(d):
                        dst[b * per:(b + 1) * per, r * w:(r + 1) * w] = y[r * per:(r + 1) * per, :]
                else:
                    by_residue = jnp.concatenate(
                        [src[b * per:(b + 1) * per, r * w:(r + 1) * w] for r in range(d)], axis=0)
                    dst[b * blk:(b + 1) * blk, :] = _dot(sort, by_residue).astype(BF16)

    natural = [pl.BlockSpec((tm, w), lambda i: (i, col_block)) for w in widths]
    viewed = [pl.BlockSpec((rows, d * w), lambda i: (i, 0)) for w in widths]
    return _pcall(
        body,
        name=name,
        grid=(s // tm,),
        in_specs=natural if to_view else viewed,
        out_specs=viewed if to_view else natural,
        out_shape=[jax.ShapeDtypeStruct((s // d, d * w) if to_view else (s, w), BF16) for w in widths],
        compiler_params=_cparams(1),
    )(*xs)


def _dil_fwd(view, group):
    window, dilation = DIL_GROUPS[group]
    qkv_v, ncb, cols = view
    sub = qkv_v.shape[0]
    s = sub * dilation
    nb = sub // BLOCK
    assert nb * BLOCK * dilation == s and window // dilation == BLOCK
    nblk = min(DIL_STEP_BLOCKS, nb)
    assert nb % nblk == 0
    slopes = [_alibi_slope(group * DIL_HEADS_PER_GROUP + h) * dilation for h in range(DIL_HEADS_PER_GROUP)]

    def body(q_ref, kc_ref, kp_ref, vc_ref, vp_ref, o_ref, lse_ref):
        i = pl.program_id(1)
        kk_all = jnp.concatenate([kp_ref[...], kc_ref[...]], axis=0)
        vv_all = jnp.concatenate([vp_ref[...], vc_ref[...]], axis=0)
        head_id = lax.broadcasted_iota(jnp.int32, (1, DIL_OUT_WIDTH), 1) // HEAD_DIM
        chains = [(b, h) for b in range(nblk) for h in range(DIL_HEADS_PER_GROUP)]
        rows = lambda b: slice(b * BLOCK, (b + 1) * BLOCK)
        keys = lambda b: slice(b * BLOCK, (b + 2) * BLOCK)
        bands = [_dil_band(i == 0 if b == 0 else False) for b in range(nblk)]
        qs = [q_ref[rows(b), :] for b in range(nblk)]
        scores = [_dot_nt(jnp.where(head_id == h, qs[b], jnp.zeros_like(qs[b])), kk_all[keys(b)]) for b, h in chains]
        ps, lses = [], []
        for (b, h), sc in zip(chains, scores):
            steps, valid = bands[b]
            logits = jnp.where(valid, sc * (1.0 / math.sqrt(HEAD_DIM)) - slopes[h] * steps, NEG_INF)
            mx = jnp.max(logits, axis=1, keepdims=True)
            e = jnp.exp(logits - mx)
            den = jnp.sum(e, axis=1, keepdims=True)
            lses.append(mx + jnp.log(den))
            ps.append((e * (1.0 / den)).astype(BF16))
        outs = [_dot(p, vv_all[keys(b)]) for (b, h), p in zip(chains, ps)]
        for b in range(nblk):
            mine = [n for n, ch in enumerate(chains) if ch[0] == b]
            o, lse_all = outs[mine[0]], lses[mine[0]]
            for n in mine[1:]:
                o = jnp.where(head_id == chains[n][1], outs[n], o)
                lse_all = jnp.where(head_id == chains[n][1], lses[n], lse_all)
            o_ref[rows(b), :] = o
            lse_ref[rows(b), :] = jnp.broadcast_to(lse_all, o.shape)

    out_spec = pl.BlockSpec((nblk * BLOCK, DIL_OUT_WIDTH), lambda r, i: (i, r))
    o, lse = _pcall(
        body,
        name=f"dil_fwd_g{group}",
        grid=(dilation, nb // nblk),
        in_specs=_dil_step_specs(ncb, cols, nblk, lambda i: i),
        out_specs=[out_spec, out_spec],
        out_shape=[jax.ShapeDtypeStruct((sub, dilation * DIL_OUT_WIDTH), F32)] * 2,
        compiler_params=_cparams(2),
    )(qkv_v, qkv_v, qkv_v, qkv_v, qkv_v)
    return o, lse


def _dil_bwd(view, do_g, lse_g, dterm_g, group, rider=None):
    window, dilation = DIL_GROUPS[group]
    qkv_v, ncb, cols = view
    sub = qkv_v.shape[0]
    nb = sub // BLOCK
    nblk = min(DIL_STEP_BLOCKS, nb)
    n_steps = nb // nblk
    slopes = [_alibi_slope(group * DIL_HEADS_PER_GROUP + h) * dilation for h in range(DIL_HEADS_PER_GROUP)]
    scale = 1.0 / math.sqrt(HEAD_DIM)
    tail = slice((nblk - 1) * BLOCK, nblk * BLOCK)
    single = n_steps == 1

    def body(q_ref, kc_ref, kp_ref, vc_ref, vp_ref, do_ref, lse_ref, dt_ref, dq_ref, dk_ref, dv_ref, *carry_refs):
        i = pl.program_id(1)

        def init():
            for carry_ref in carry_refs:
                carry_ref[...] = jnp.zeros_like(carry_ref)

        def compute():
            kk_all = jnp.concatenate([kp_ref[...], kc_ref[...]], axis=0)
            vv_all = jnp.concatenate([vp_ref[...], vc_ref[...]], axis=0)
            lane = lax.broadcasted_iota(jnp.int32, (1, DIL_OUT_WIDTH), 1)
            head_id = lane // HEAD_DIM
            chains = [(b, h) for b in range(nblk) for h in range(DIL_HEADS_PER_GROUP)]
            rows = lambda b: slice(b * BLOCK, (b + 1) * BLOCK)
            keys = lambda b: slice(b * BLOCK, (b + 2) * BLOCK)
            bands = [_dil_band(i == 0 if b == 0 else False) for b in range(nblk)]
            qms, doms = [], []
            for b, h in chains:
                q, do = q_ref[rows(b), :], do_ref[rows(b), :]
                qms.append(jnp.where(head_id == h, q, jnp.zeros_like(q)))
                doms.append(jnp.where(head_id == h, do, jnp.zeros_like(do)))
            scores = [_dot_nt(qm, kk_all[keys(b)]) for (b, h), qm in zip(chains, qms)]
            dps = [_dot_nt(dom, vv_all[keys(b)]) for (b, h), dom in zip(chains, doms)]
            pbs, dss = [], []
            for n, (b, h) in enumerate(chains):
                steps, valid = bands[b]
                first = lane == h * HEAD_DIM
                lse = jnp.sum(jnp.where(first, lse_ref[rows(b), :], 0.0), axis=1, keepdims=True)
                dt = jnp.sum(jnp.where(first, dt_ref[rows(b), :], 0.0), axis=1, keepdims=True)
                logits = jnp.where(valid, scores[n] * scale - slopes[h] * steps, NEG_INF)
                p = jnp.where(valid, jnp.exp(logits - lse), 0.0)
                pbs.append(p.astype(BF16))
                dss.append((p * (dps[n] + dt) * scale).astype(BF16))
            dqs = [_dot(ds, kk_all[keys(b)]) for (b, h), ds in zip(chains, dss)]
            dks = [_dot_tn(ds, qm) for ds, qm in zip(dss, qms)]
            dvs = [_dot_tn(pb, dom) for pb, dom in zip(pbs, doms)]
            dkk, dvv = [], []
            for b in range(nblk):
                mine = [n for n, ch in enumerate(chains) if ch[0] == b]
                dq = dqs[mine[0]]
                for n in mine[1:]:
                    dq = jnp.where(head_id == chains[n][1], dqs[n], dq)
                dq_ref[rows(b), :] = dq.astype(dq_ref.dtype)
                dkk.append((dks[mine[0]] + dks[mine[1]]) + (dks[mine[2]] + dks[mine[3]]))
                dvv.append((dvs[mine[0]] + dvs[mine[1]]) + (dvs[mine[2]] + dvs[mine[3]]))
            for n, (out_ref, parts) in enumerate(((dk_ref, dkk), (dv_ref, dvv))):
                done = [parts[b][BLOCK:] + parts[b + 1][:BLOCK] if b + 1 < nblk else parts[b][BLOCK:]
                        for b in range(nblk)]
                if single:
                    for b in range(nblk):
                        out_ref[rows(b), :] = done[b].astype(out_ref.dtype)
                    continue
                carry_ref = carry_refs[n]
                if nblk > 1:
                    out_ref[: (nblk - 1) * BLOCK, :] = carry_ref[: (nblk - 1) * BLOCK, :].astype(out_ref.dtype)
                out_ref[tail, :] = (carry_ref[tail, :] + parts[0][:BLOCK]).astype(out_ref.dtype)
                for b in range(nblk):
                    carry_ref[rows(b), :] = done[b]

        def flush():
            for out_ref, carry_ref in zip((dk_ref, dv_ref), carry_refs):
                out_ref[...] = carry_ref[...].astype(out_ref.dtype)

        if single:
            compute()
        else:
            pl.when(i == 0)(init)
            pl.when(i < n_steps)(compute)
            pl.when(i == n_steps)(flush)

    clamp = lambda i: jnp.minimum(i, n_steps - 1)
    row_spec = pl.BlockSpec((nblk * BLOCK, DIL_OUT_WIDTH), lambda r, i: (clamp(i), r))
    late_spec = pl.BlockSpec((nblk * BLOCK, DIL_OUT_WIDTH), lambda r, i: (jnp.maximum(i - 1, 0), r))
    res = _call(
        body, (qkv_v, qkv_v, qkv_v, qkv_v, qkv_v, do_g, lse_g, dterm_g), rider,
        name=f"dil_bwd_g{group}",
        grid=(dilation, n_steps + (0 if single else 1)),
        in_specs=_dil_step_specs(ncb, cols, nblk, clamp) + [row_spec, row_spec, row_spec],
        out_specs=[row_spec, row_spec, row_spec] if single else [row_spec, late_spec, late_spec],
        out_shape=[jax.ShapeDtypeStruct((sub, dilation * DIL_OUT_WIDTH), BF16)] * 3,
        scratch_shapes=[] if single else [pltpu.VMEM((nblk * BLOCK, DIL_OUT_WIDTH), F32)] * 2,
        compiler_params=_cparams(2),
    )
    grads, lands = res if rider is not None else (res, None)
    if dilation > 1:
        grads = _dil_relayout(f"dil_bwd_rows_g{group}", list(grads), dilation, to_view=False)
    return tuple(grads) if rider is None else (tuple(grads), lands)


def _dil_view(qkv, group):
    _, dilation = DIL_GROUPS[group]
    w = DIL_OUT_WIDTH
    if dilation == 1:
        return qkv, QKV_COLS // w, (3 * group, 3 * group + 1, 3 * group + 2)
    (own,) = _dil_relayout(f"dil_view_g{group}", [qkv], dilation, to_view=True, col_block=group, width=3 * w)
    return own, 3, (0, 1, 2)


def _group_major(w_qkv):
    w = DIL_OUT_WIDTH
    ng = len(DIL_GROUPS)
    cols = [w_qkv[:, (part * ng + g) * w:(part * ng + g + 1) * w] for g in range(ng) for part in range(3)]
    return jnp.concatenate(cols + [w_qkv[:, 3 * DIL_WIDTH:]], axis=1)


def _head_block_ones():
    r = lax.broadcasted_iota(jnp.int32, (DIL_OUT_WIDTH, DIL_OUT_WIDTH), 0) // HEAD_DIM
    c = lax.broadcasted_iota(jnp.int32, (DIL_OUT_WIDTH, DIL_OUT_WIDTH), 1) // HEAD_DIM
    return jnp.where(r == c, 1.0, 0.0).astype(BF16)


def _dil_mix_weights(l0, l1, l2):
    mx = jnp.maximum(jnp.maximum(l0, l1), l2)
    e0, e1, e2 = jnp.exp(l0 - mx), jnp.exp(l1 - mx), jnp.exp(l2 - mx)
    inv = 1.0 / (e0 + e1 + e2)
    return e0 * inv, e1 * inv, e2 * inv


def _dil_view_spec(dilation):
    return pl.BlockSpec((DIL_RELAYOUT_ROWS // dilation, dilation * DIL_OUT_WIDTH), lambda i: (i, 0))


def _dil_mix_call(name, body, s, ins, in_specs, outs, n_relaid):
    out_specs = [_dil_view_spec(d or 1) for d, _ in outs]
    out_shape = [jax.ShapeDtypeStruct((s // (d or 1), (d or 1) * DIL_OUT_WIDTH), dt) for d, dt in outs]
    return _pcall(
        body,
        name=name,
        grid=(s // DIL_RELAYOUT_ROWS,),
        in_specs=in_specs,
        out_specs=out_specs,
        out_shape=out_shape,
        scratch_shapes=[_view_scratch(DIL_OUT_WIDTH)] * n_relaid,
        compiler_params=_cparams(1),
    )(*ins)


DIL_MIX_CHUNK = 64
_DIL_SLABS = DIL_OUT_WIDTH // LANES


def _dil_rows(refs, scratch):
    dils = [d for _, d in DIL_GROUPS]
    assert dils[0] == 1
    readers = [lambda rows, j, ref=refs[0]: ref[rows, j * LANES:(j + 1) * LANES]]
    for ref, scr, d in zip(refs[1:], scratch, dils[1:]):
        _rows_from_view(ref, scr, d, DIL_OUT_WIDTH)
        readers.append(lambda rows, j, scr=scr: scr[j, rows, :])
    return readers


def _dil_mix_chunks(step):
    def chunk(c, carry):
        step(pl.ds(pl.multiple_of(c * DIL_MIX_CHUNK, DIL_MIX_CHUNK), DIL_MIX_CHUNK))
        return carry

    lax.fori_loop(0, DIL_RELAYOUT_ROWS // DIL_MIX_CHUNK, chunk, 0, unroll=4)


def _dil_mix_fwd(os_, lses):
    ng = len(DIL_GROUPS)
    s = os_[0].shape[0]

    def body(*refs):
        o_refs, l_refs, out_ref, scratch = refs[:ng], refs[ng:2 * ng], refs[2 * ng], refs[2 * ng + 1:]
        o_at = _dil_rows(o_refs, scratch[:ng - 1])
        l_at = _dil_rows(l_refs, scratch[ng - 1:])

        def step(rows):
            for j in range(_DIL_SLABS):
                w0, w1, w2 = _dil_mix_weights(*[at(rows, j) for at in l_at])
                o0, o1, o2 = [at(rows, j) for at in o_at]
                out_ref[rows, j * LANES:(j + 1) * LANES] = (w0 * o0 + w1 * o1 + w2 * o2).astype(out_ref.dtype)

        _dil_mix_chunks(step)

    specs = [_dil_view_spec(d) for _, d in DIL_GROUPS]
    (o_a,) = _dil_mix_call("dil_mix_fwd", body, s, list(os_) + list(lses), specs * 2, [(None, BF16)], 2 * (ng - 1))
    return o_a


def _dil_mix_bwd(do_a, os_, lses):
    ng = len(DIL_GROUPS)
    s = do_a.shape[0]
    dils = [d for _, d in DIL_GROUPS]

    def body(*refs):
        do_ref, o_refs, l_refs = refs[0], refs[1:1 + ng], refs[1 + ng:1 + 2 * ng]
        out_refs, scratch = refs[1 + 2 * ng:1 + 4 * ng], refs[1 + 4 * ng:]
        o_at = _dil_rows(o_refs, scratch[:ng - 1])
        l_at = _dil_rows(l_refs, scratch[ng - 1:2 * (ng - 1)])
        spare = iter(scratch[2 * (ng - 1):])
        staged = [None if dils[n % ng] == 1 else next(spare) for n in range(2 * ng)]
        ones = _head_block_ones()

        def step(rows):
            do = do_ref[rows, :].astype(F32)
            ws, prods = [], []
            for j in range(_DIL_SLABS):
                w0, w1, w2 = _dil_mix_weights(*[at(rows, j) for at in l_at])
                o0, o1, o2 = [at(rows, j) for at in o_at]
                ws.append((w0, w1, w2))
                prods.append(do[:, j * LANES:(j + 1) * LANES] * (w0 * o0 + w1 * o1 + w2 * o2))
            tot = _dot_hi_lo(jnp.concatenate(prods, axis=1), ones)
            for j in range(_DIL_SLABS):
                slab = slice(j * LANES, (j + 1) * LANES)
                vals = [w * do[:, slab] for w in ws[j]] + [-w * tot[:, slab] for w in ws[j]]
                for val, dst, scr in zip(vals, out_refs, staged):
                    if scr is None:
                        dst[rows, slab] = val.astype(dst.dtype)
                    else:
                        scr[j, rows, :] = val

        _dil_mix_chunks(step)
        for n, (dst, scr) in enumerate(zip(out_refs, staged)):
            if scr is not None:
                _rows_to_view(scr, dst, dils[n % ng], DIL_OUT_WIDTH)

    specs = [_dil_view_spec(d) for d in dils]
    return _dil_mix_call(
        "dil_mix_bwd", body, s, [do_a] + list(os_) + list(lses), [_dil_view_spec(1)] + specs * 2,
        [(d, BF16) for d in dils] + [(d, F32) for d in dils], 4 * (ng - 1))


_SB_Q0 = 3 * DIL_WIDTH // LANES
_SB_K0 = _SB_Q0 + SB_WIDTH // LANES
_SB_V0 = _SB_K0 + SB_WIDTH // LANES


_EXP_CLAMP = 88.0
_SB_DEAD = 104.0


def _tri(t, op):
    r = lax.broadcasted_iota(jnp.int32, (t, t), 0)
    c = lax.broadcasted_iota(jnp.int32, (t, t), 1)
    return jnp.where(op(r, c), 1.0, 0.0).astype(BF16)


def _softplus(z):
    return jnp.maximum(z, jnp.log(1.0 + jnp.exp(jnp.minimum(z, _EXP_CLAMP))))


def _sb_chain_head(qm, kj, mask):
    z = _dot_nt(qm, kj)
    sp = _softplus(z)
    return (sp if mask is None else jnp.where(mask, sp, 0.0)), z - sp


def _sb_fwd(qkv, rider=None):
    s = qkv.shape[0]
    t = SB_TK
    assert s % (2 * t) == 0
    nq = s // (2 * t)
    n_pairs = SB_WIDTH // LANES

    def body(q_ref, k_ref, v_ref, o_ref, tot_ref, steps_ref):
        p, i = pl.program_id(0), pl.program_id(1)
        lane_hi = lax.broadcasted_iota(jnp.int32, (1, LANES), 1) // HEAD_DIM
        later = _tri(t, lambda r, c: r > c)
        causal = lax.broadcasted_iota(jnp.int32, (t, t), 1) < lax.broadcasted_iota(jnp.int32, (t, t), 0)
        qms = []
        for x in range(2):
            q = q_ref[pl.ds(x * t, t), :] * (1.0 / math.sqrt(HEAD_DIM))
            qms.append([jnp.where(lane_hi == hh, q, jnp.zeros_like(q)) for hh in range(2)])

        def tile(j):
            off = pl.multiple_of(j * t, t)
            return k_ref[pl.ds(off, t), :], v_ref[pl.ds(off, t), :]

        def step(groups, carry):
            kv = [tile(j) for _, j, _ in groups]
            chains = [(g, x, hh) for g, (x, _, _) in enumerate(groups) for hh in range(2)]
            heads = [_sb_chain_head(qms[x][hh], kv[g][0], causal if groups[g][2] else None) for g, x, hh in chains]
            sufs = [_dot(sp.astype(BF16), later) for sp, _ in heads]
            cur = [list(carry[0]), list(carry[1])]
            for (g, x, hh), (sp, lpos), suf in zip(chains, heads, sufs):
                c, acc = cur[x][hh]
                a = jnp.exp(lpos - suf - c)
                if groups[g][2]:
                    a = jnp.where(causal, a, 0.0)
                cur[x][hh] = (c + jnp.sum(sp, axis=1, keepdims=True), acc + _dot(a.astype(BF16), kv[g][1]))
            return (tuple(cur[0]), tuple(cur[1]))

        def lowest(carry):
            return jnp.min(jnp.minimum(jnp.minimum(carry[0][0][0], carry[0][1][0]),
                                       jnp.minimum(carry[1][0][0], carry[1][1][0])))

        zero = (jnp.zeros((t, 1), F32), jnp.zeros((t, LANES), F32))
        start = ((zero, zero), (zero, zero))
        carry = lax.cond(
            i == 0,
            lambda ca: step([(0, 0, True), (1, 1, True), (1, 0, False)], ca),
            lambda ca: step([(0, 2 * i, True), (1, 2 * i + 1, True), (0, 2 * i - 1, False), (1, 2 * i, False)], ca),
            start)

        def walk(state):
            n, ca, _ = state
            ca = step([(0, 2 * i - 2 - n, False), (1, 2 * i - 1 - n, False)], ca)
            return n + 1, ca, lowest(ca)

        n_more, carry, low = lax.while_loop(
            lambda st: jnp.logical_and(st[0] + 1 < 2 * i, st[2] <= _SB_DEAD), walk, (jnp.int32(0), carry, lowest(carry)))
        b_last = jnp.logical_and(jnp.logical_and(i > 0, n_more + 1 == 2 * i), low <= _SB_DEAD)
        carry = lax.cond(b_last, lambda ca: step([(1, 0, False)], ca), lambda ca: ca, carry)
        for x in range(2):
            (c0, acc0), (c1, acc1) = carry[x]
            o_ref[pl.ds(x * t, t), :] = jnp.where(lane_hi == 0, acc0, acc1).astype(o_ref.dtype)
            tot_ref[pl.ds(x * t, t), :] = jnp.where(lane_hi == 0, c0, c1)
        steps_ref[p, i] = 1 + n_more + b_last.astype(jnp.int32)

    return _call(
        body, (qkv, qkv, qkv), rider,
        name="sb_fwd",
        grid=(n_pairs, nq),
        in_specs=[
            pl.BlockSpec((2 * t, LANES), lambda p, i: (i, _SB_Q0 + p)),
            pl.BlockSpec((s, LANES), lambda p, i: (0, _SB_K0 + p)),
            pl.BlockSpec((s, LANES), lambda p, i: (0, _SB_V0 + p)),
        ],
        out_specs=[pl.BlockSpec((2 * t, LANES), lambda p, i: (i, p))] * 2 + [pl.BlockSpec(memory_space=pltpu.SMEM)],
        out_shape=[jax.ShapeDtypeStruct((s, SB_WIDTH), BF16), jax.ShapeDtypeStruct((s, SB_WIDTH), F32),
                   jax.ShapeDtypeStruct((n_pairs, nq), jnp.int32)],
        compiler_params=_cparams(2),
    )


def _sb_bwd(qkv, do_b, tot_b, n_steps):
    s = qkv.shape[0]
    t = SB_TK
    nq = s // (2 * t)
    n_pairs = SB_WIDTH // LANES
    scale = 1.0 / math.sqrt(HEAD_DIM)

    def body(steps_ref, q_ref, k_ref, v_ref, do_ref, tot_ref, dq_ref, dk_ref, dv_ref):
        p, i = pl.program_id(0), pl.program_id(1)

        @pl.when(i == 0)
        def _():
            dk_ref[...] = jnp.zeros_like(dk_ref)
            dv_ref[...] = jnp.zeros_like(dv_ref)

        lane = lax.broadcasted_iota(jnp.int32, (1, LANES), 1)
        lane_hi = lane // HEAD_DIM
        later = _tri(t, lambda r, c: r > c)
        before = _tri(t, lambda r, c: r < c)
        causal = lax.broadcasted_iota(jnp.int32, (t, t), 1) < lax.broadcasted_iota(jnp.int32, (t, t), 0)
        qms, doms, tots = [], [], []
        for x in range(2):
            rows = pl.ds(x * t, t)
            q, do, tot_all = q_ref[rows, :] * scale, do_ref[rows, :], tot_ref[rows, :]
            qms.append([jnp.where(lane_hi == hh, q, jnp.zeros_like(q)) for hh in range(2)])
            doms.append([jnp.where(lane_hi == hh, do, jnp.zeros_like(do)) for hh in range(2)])
            tots.append([jnp.sum(jnp.where(lane == hh * HEAD_DIM, tot_all, 0.0), axis=1, keepdims=True)
                         for hh in range(2)])

        def step(groups, carry):
            offs = [pl.multiple_of(j * t, t) for _, j, _ in groups]
            ks = [k_ref[pl.ds(off, t), :] for off in offs]
            vs = [v_ref[pl.ds(off, t), :] for off in offs]
            chains = [(g, x, hh) for g, (x, _, _) in enumerate(groups) for hh in range(2)]
            heads = [_sb_chain_head(qms[x][hh], ks[g], causal if groups[g][2] else None) for g, x, hh in chains]
            sufs = [_dot(sp.astype(BF16), later) for sp, _ in heads]
            das = [_dot_nt(doms[x][hh], vs[g]) for g, x, hh in chains]
            cur = [list(carry[0]), list(carry[1])]
            sigs, gs, abs_, cg_before = [], [], [], []
            for (g_, x, hh), (sp, lpos), suf, da in zip(chains, heads, sufs, das):
                cl, cg, dq = cur[x][hh]
                cl = cl + jnp.sum(sp, axis=1, keepdims=True)
                sig = jnp.exp(lpos)
                a = sig * jnp.exp(-suf - (tots[x][hh] - cl))
                if groups[g_][2]:
                    a = jnp.where(causal, a, 0.0)
                g = a * da
                sigs.append(sig)
                gs.append(g)
                abs_.append(a.astype(BF16))
                cg_before.append(cg)
                cur[x][hh] = (cl, cg + jnp.sum(g, axis=1, keepdims=True), dq)
            prefs = [_dot(g.astype(BF16), before) for g in gs]
            dvs = [_dot_tn(ab, doms[x][hh]) for (_, x, hh), ab in zip(chains, abs_)]
            dzs = []
            for (g_, x, hh), sig, g, pref, cg in zip(chains, sigs, gs, prefs, cg_before):
                dz = g - sig * (g + pref + cg)
                if groups[g_][2]:
                    dz = jnp.where(causal, dz, 0.0)
                dzs.append(dz.astype(BF16))
            dqs = [_dot(dz, ks[g_]) for (g_, x, hh), dz in zip(chains, dzs)]
            dks = [_dot_tn(dz, qms[x][hh]) for (_, x, hh), dz in zip(chains, dzs)]
            for n, (_, x, hh) in enumerate(chains):
                cl, cg, dq = cur[x][hh]
                cur[x][hh] = (cl, cg, dq + dqs[n])
            for g_, off in enumerate(offs):
                dk_ref[pl.ds(off, t), :] += dks[2 * g_] + dks[2 * g_ + 1]
                dv_ref[pl.ds(off, t), :] += dvs[2 * g_] + dvs[2 * g_ + 1]
            return (tuple(cur[0]), tuple(cur[1]))

        taken = steps_ref[p, i]
        n_full = jnp.minimum(taken, 2 * i)
        zero = (jnp.zeros((t, 1), F32), jnp.zeros((t, 1), F32), jnp.zeros((t, LANES), F32))
        carry = ((zero, zero), (zero, zero))
        carry = lax.cond(jnp.logical_and(i > 0, taken > 2 * i), lambda ca: step([(1, 0, False)], ca), lambda ca: ca,
                         carry)
        carry = lax.fori_loop(
            0, n_full - 1,
            lambda n, ca: step([(0, 2 * i - n_full + n, False), (1, 2 * i + 1 - n_full + n, False)], ca), carry)
        carry = lax.cond(
            i == 0,
            lambda ca: step([(1, 0, False), (0, 0, True), (1, 1, True)], ca),
            lambda ca: step([(0, 2 * i - 1, False), (1, 2 * i, False), (0, 2 * i, True), (1, 2 * i + 1, True)], ca),
            carry)
        for x in range(2):
            dq = jnp.where(lane_hi == 0, carry[x][0][2], carry[x][1][2])
            dq_ref[pl.ds(x * t, t), :] = (dq * scale).astype(dq_ref.dtype)

    row_spec = pl.BlockSpec((2 * t, LANES), lambda p, i, ns: (i, p))
    full_spec = pl.BlockSpec((s, LANES), lambda p, i, ns: (0, p))
    return _pcall(
        body,
        name="sb_bwd",
        grid_spec=pltpu.PrefetchScalarGridSpec(
            num_scalar_prefetch=1,
            grid=(n_pairs, nq),
            in_specs=[
                pl.BlockSpec((2 * t, LANES), lambda p, i, ns: (i, _SB_Q0 + p)),
                pl.BlockSpec((s, LANES), lambda p, i, ns: (0, _SB_K0 + p)),
                pl.BlockSpec((s, LANES), lambda p, i, ns: (0, _SB_V0 + p)),
                row_spec, row_spec,
            ],
            out_specs=[row_spec, full_spec, full_spec],
        ),
        out_shape=[jax.ShapeDtypeStruct((s, SB_WIDTH), BF16), jax.ShapeDtypeStruct((s, SB_WIDTH), F32),
                   jax.ShapeDtypeStruct((s, SB_WIDTH), F32)],
        compiler_params=_cparams(2),
    )(n_steps, qkv, qkv, qkv, do_b, tot_b)


def _gates(gl, bg):
    return _sigmoid(gl[:, :D_MODEL] + bg[:, :D_MODEL]), _sigmoid(gl[:, D_MODEL:] + bg[:, D_MODEL:])


def _mixer_fwd(o_a, o_b, gl, x0, bg, g2, w_ud, w_us, w_out, tm):
    def epi(_, rows, consts):
        oa, ob, glv, x = rows
        bgv, g2v, wud, wus, wout = consts
        ga, gb = _gates(glv, bgv)
        merged = ga * _dot(oa, wud) + gb * _dot(ob, wus)
        x1 = x + _dot(merged.astype(BF16), wout)
        r, xh = _rms_stats(x1)
        return [x1, xh * g2v], []

    return _rowk("mixer_fwd", tm=tm, rows=[o_a, o_b, gl, x0], consts=[bg, g2, w_ud, w_us, w_out],
                 row_outs=[(D_MODEL, F32), (D_MODEL, BF16)], epilogue=epi)


def _mixer_bwd(dx1, o_a, o_b, gl, bg, w_ud, w_us, w_out, tm, rider=None):
    s = dx1.shape[0]
    nm = s // tm

    def body(dx_ref, oa_ref, ob_ref, gl_ref, bg_ref, wud_ref, wus_ref, wout_ref,
             doa_ref, dob_ref, dgl_ref, gwout_ref, gwud_ref, gwus_ref, gbg_ref, awout_ref, awud_ref, awus_ref):
        i = pl.program_id(0)
        dxb = dx_ref[...].astype(BF16)
        oa, ob = oa_ref[...], ob_ref[...]
        ga, gb = _gates(gl_ref[...], bg_ref[...])
        ua, ub = _dot(oa, wud_ref[...]), _dot(ob, wus_ref[...])
        merged = (ga * ua + gb * ub).astype(BF16)
        dm = _dot_nt(dxb, wout_ref[...])
        dua = (dm * ga).astype(BF16)
        dub = (dm * gb).astype(BF16)
        dgla = dm * ua * ga * (1.0 - ga)
        dglb = dm * ub * gb * (1.0 - gb)
        doa_ref[...] = _dot_nt(dua, wud_ref[...]).astype(doa_ref.dtype)
        dob_ref[...] = _dot_nt(dub, wus_ref[...]).astype(dob_ref.dtype)
        dgl_ref[:, :D_MODEL] = dgla.astype(dgl_ref.dtype)
        dgl_ref[:, D_MODEL:] = dglb.astype(dgl_ref.dtype)
        parts = [(gwout_ref, awout_ref, _dot_tn(merged, dxb)), (gwud_ref, awud_ref, _dot_tn(oa, dua)),
                 (gwus_ref, awus_ref, _dot_tn(ob, dub))]
        for out, r, v in parts:

            @pl.when(i == 0)
            def _(r=r, v=v):
                r[...] = v

            @pl.when(i > 0)
            def _(r=r, v=v):
                r[...] += v

            @pl.when(i == nm - 1)
            def _(out=out, r=r):
                if len(out.shape) == 2:
                    out[...] = r[...].astype(out.dtype)
                else:
                    for p in range(N_DEV):
                        out[p] = r[:, p * out.shape[2]:(p + 1) * out.shape[2]].astype(out.dtype)

        sa = jnp.sum(dgla, axis=0, keepdims=True)
        sb = jnp.sum(dglb, axis=0, keepdims=True)

        @pl.when(i == 0)
        def _():
            gbg_ref[:, :D_MODEL] = sa
            gbg_ref[:, D_MODEL:] = sb

        @pl.when(i > 0)
        def _():
            gbg_ref[:, :D_MODEL] += sa
            gbg_ref[:, D_MODEL:] += sb

    row = lambda w: pl.BlockSpec((tm, w), lambda i: (i, 0))
    full = lambda a: pl.BlockSpec(a.shape, lambda i: (0, 0), pipeline_mode=pl.Buffered(1))
    chunks = lambda r: (N_DEV, r, D_MODEL // N_DEV)
    return _call(
        body, (dx1, o_a, o_b, gl, bg, w_ud, w_us, w_out), rider,
        name="mixer_bwd",
        grid=(nm,),
        in_specs=[row(D_MODEL), row(DIL_OUT_WIDTH), row(SB_WIDTH), row(2 * D_MODEL),
                  full(bg), full(w_ud), full(w_us), full(w_out)],
        out_specs=[row(DIL_OUT_WIDTH), row(SB_WIDTH), row(2 * D_MODEL),
                   pl.BlockSpec((D_MODEL, D_MODEL), lambda i: (0, 0)),
                   pl.BlockSpec(chunks(DIL_OUT_WIDTH), lambda i: (0, 0, 0)),
                   pl.BlockSpec(chunks(SB_WIDTH), lambda i: (0, 0, 0)),
                   pl.BlockSpec((1, 2 * D_MODEL), lambda i: (0, 0))],
        out_shape=[jax.ShapeDtypeStruct((s, DIL_OUT_WIDTH), BF16), jax.ShapeDtypeStruct((s, SB_WIDTH), BF16),
                   jax.ShapeDtypeStruct((s, 2 * D_MODEL), BF16),
                   jax.ShapeDtypeStruct((D_MODEL, D_MODEL), BF16), jax.ShapeDtypeStruct(chunks(DIL_OUT_WIDTH), BF16),
                   jax.ShapeDtypeStruct(chunks(SB_WIDTH), BF16), jax.ShapeDtypeStruct((1, 2 * D_MODEL), F32)],
        scratch_shapes=[pltpu.VMEM((D_MODEL, D_MODEL), F32), pltpu.VMEM((DIL_OUT_WIDTH, D_MODEL), F32),
                        pltpu.VMEM((SB_WIDTH, D_MODEL), F32)],
        compiler_params=_cparams(1),
    )


def _all_gather(shards):
    n = len(shards)

    def body(*refs):
        x_refs, out_refs = refs[:n], refs[n:2 * n]
        send_sems, recv_sems, local_sems = refs[2 * n:]
        x, y, c = lax.axis_index("x"), lax.axis_index("y"), lax.axis_index("c")
        me, sibling = (x, y, c), (x, y, 1 - c)
        chips = [(1 - x, y), (x, 1 - y), (1 - x, 1 - y)]

        def slot(a, px, py, pc):
            return out_refs[a].at[4 * px + 2 * py + pc]

        def copy(a, k, block, to, own=False):
            return pltpu.make_async_remote_copy(
                src_ref=x_refs[a] if own else slot(a, *block), dst_ref=slot(a, *block),
                send_sem=send_sems.at[7 * a + k], recv_sem=recv_sems.at[7 * a + k], device_id=to, device_id_type=_MESH)

        mine = [pltpu.make_async_copy(x_refs[a], slot(a, *me), local_sems.at[a]) for a in range(n)]
        for cp in mine:
            cp.start()
        first = []
        for a in range(n):
            first.append(copy(a, 0, me, sibling, own=True))
            first += [copy(a, 1 + j, me, (*chip, c), own=True) for j, chip in enumerate(chips)]
        for cp in first:
            cp.start()
        passed = []
        for a in range(n):
            for j, chip in enumerate(chips):
                copy(a, 1 + j, (*chip, c), me).wait_recv()
                passed.append(copy(a, 4 + j, (*chip, c), sibling))
                passed[-1].start()
        for a in range(n):
            copy(a, 0, sibling, me).wait_recv()
            for j, chip in enumerate(chips):
                copy(a, 4 + j, (*chip, 1 - c), me).wait_recv()
        for cp in first + passed:
            cp.wait_send()
        for cp in mine:
            cp.wait()

    return _pcall(
        body,
        name="all_gather_weights",
        in_specs=[_HBM] * n,
        out_specs=[_HBM] * n,
        out_shape=[jax.ShapeDtypeStruct((N_DEV,) + s.shape, s.dtype) for s in shards],
        scratch_shapes=[pltpu.SemaphoreType.DMA((7 * n,)), pltpu.SemaphoreType.DMA((7 * n,)),
                        pltpu.SemaphoreType.DMA((n,))],
    )(*shards)


def _exchange(chunks):
    n = len(chunks)

    def body(*refs):
        g_refs, o_refs = refs[:n], refs[n:2 * n]
        send_sems, recv_sems, local_sems = refs[2 * n:]
        x, y, c = lax.axis_index("x"), lax.axis_index("y"), lax.axis_index("c")
        me = 4 * x + 2 * y + c
        own = [pltpu.make_async_copy(g_refs[a].at[me], o_refs[a].at[me], local_sems.at[a]) for a in range(n)]
        for cp in own:
            cp.start()
        copies = []
        for a in range(n):
            for k in range(1, N_DEV):
                px, py, pc = x ^ (k >> 2), y ^ ((k >> 1) & 1), c ^ (k & 1)
                peer = 4 * px + 2 * py + pc
                copies.append(pltpu.make_async_remote_copy(
                    src_ref=g_refs[a].at[peer], dst_ref=o_refs[a].at[me], send_sem=send_sems.at[7 * a + k - 1],
                    recv_sem=recv_sems.at[7 * a + k - 1], device_id=(px, py, pc), device_id_type=_MESH))
        for cp in copies:
            cp.start()
        for cp in copies:
            cp.wait()
        for cp in own:
            cp.wait()

    return _pcall(
        body,
        name="exchange_grads",
        in_specs=[_HBM] * n,
        out_specs=[_HBM] * n,
        out_shape=[jax.ShapeDtypeStruct(g.shape, g.dtype) for g in chunks],
        scratch_shapes=[pltpu.SemaphoreType.DMA((7 * n,)), pltpu.SemaphoreType.DMA((7 * n,)),
                        pltpu.SemaphoreType.DMA((n,))],
    )(*chunks)


def _reduce_adamw(name, parts, w, m, v, tr):
    _, rows, cols = parts.shape
    tr = min(tr, rows)
    assert rows % tr == 0
    c1 = 1.0 / (1.0 - ADAM_B1 ** ADAM_STEP)
    c2 = 1.0 / (1.0 - ADAM_B2 ** ADAM_STEP)

    def body(p_ref, w_ref, m_ref, v_ref, g_out, d_out, m_out, v_out):
        g = p_ref[0].astype(F32)
        for d in range(1, N_DEV):
            g = g + p_ref[d].astype(F32)
        mn = ADAM_B1 * m_ref[...] + (1.0 - ADAM_B1) * g
        vn = ADAM_B2 * v_ref[...] + (1.0 - ADAM_B2) * (g * g)
        g_out[...] = g
        m_out[...] = mn
        v_out[...] = vn
        d_out[...] = -ADAM_LR * ((mn * c1) / (jnp.sqrt(vn * c2) + ADAM_EPS) + ADAM_WD * w_ref[...])

    spec = pl.BlockSpec((tr, cols), lambda i: (i, 0))
    return _pcall(
        body,
        name=name,
        grid=(rows // tr,),
        in_specs=[pl.BlockSpec((N_DEV, tr, cols), lambda i: (0, i, 0)), spec, spec, spec],
        out_specs=[spec] * 4,
        out_shape=[jax.ShapeDtypeStruct((rows, cols), F32)] * 4,
        compiler_params=_cparams(1),
    )(parts, w, m, v)


_SHARDED = ("w_in", "w_up_dil", "w_up_sb", "w_out", "w_mlp_in", "w_mlp_out")
_FULL_SHAPES = {"w_in": (D_MODEL, IN_COLS), "w_up_dil": (DIL_OUT_WIDTH, D_MODEL), "w_up_sb": (SB_WIDTH, D_MODEL),
                "w_out": (D_MODEL, D_MODEL), "w_mlp_in": (D_MODEL, D_FF), "w_mlp_out": (D_FF, D_MODEL)}
_ROW_SHARDED = ("w_out", "w_mlp_out")


def _shard_shape(name):
    r, c = _FULL_SHAPES[name]
    return (r // N_DEV, c) if name in _ROW_SHARDED else (r, c // N_DEV)


def _assemble(name, gathered):
    r, c = _shard_shape(name)
    if name in _ROW_SHARDED:
        return gathered.reshape(N_DEV * r, c)
    return gathered.transpose(1, 0, 2).reshape(r, N_DEV * c)


def _chunk(name, full):
    r, c = _shard_shape(name)
    if name in _ROW_SHARDED:
        return full.reshape(N_DEV, r, c)
    return full.reshape(r, N_DEV, c).transpose(1, 0, 2)


def _chunk_cols(pieces):
    starts = [sum(p.shape[1] for p in pieces[:n]) for n in range(len(pieces))]
    c = (starts[-1] + pieces[-1].shape[1]) // N_DEV
    chunks = []
    for dev in range(N_DEV):
        lo, hi = dev * c, (dev + 1) * c
        cuts = [p[:, max(lo, st) - st:min(hi, st + p.shape[1]) - st]
                for p, st in zip(pieces, starts) if max(lo, st) < min(hi, st + p.shape[1])]
        chunks.append(cuts[0] if len(cuts) == 1 else jnp.concatenate(cuts, axis=1))
    return jnp.stack(chunks)


_SMALL = (("norm_mix_g", D_MODEL), ("b_gate", 2 * D_MODEL), ("norm_mlp_g", D_MODEL), ("norm_final_g", D_MODEL))
_SMALL_N = sum(n for _, n in _SMALL) + LANES


def _pack_small(vals, tail):
    return jnp.concatenate([vals[n].reshape(1, -1) for n, _ in _SMALL] + [tail], axis=1)


def _unpack_small(vec, shapes):
    out, pos = {}, 0
    for n, width in _SMALL:
        out[n] = vec[:, pos:pos + width].reshape(shapes[n])
        pos += width
    return out, vec[:, pos:]


def kernel(x, norm_mix_g, w_in, b_gate, w_up_dil, w_up_sb, w_out, norm_mlp_g, w_mlp_in, w_mlp_out, norm_final_g, loss_target, m_norm_mix_g, m_w_in, m_b_gate, m_w_up_dil, m_w_up_sb, m_w_out, m_norm_mlp_g, m_w_mlp_in, m_w_mlp_out, m_norm_final_g, v_norm_mix_g, v_w_in, v_b_gate, v_w_up_dil, v_w_up_sb, v_w_out, v_norm_mlp_g, v_w_mlp_in, v_w_mlp_out, v_norm_final_g):
    given = dict(locals())
    s = x.shape[1]
    x0 = x.reshape(s, D_MODEL)
    target = loss_target.reshape(s, D_MODEL)
    g1 = norm_mix_g.reshape(1, D_MODEL)
    g2 = norm_mlp_g.reshape(1, D_MODEL)
    g3 = norm_final_g.reshape(1, D_MODEL)
    bg = b_gate.reshape(1, 2 * D_MODEL)
    w_shards = {n: given[n].reshape(_shard_shape(n)) for n in _SHARDED}
    m_shards = {n: given["m_" + n].reshape(_shard_shape(n)) for n in _SHARDED}
    v_shards = {n: given["v_" + n].reshape(_shard_shape(n)) for n in _SHARDED}

    shard_b = {n: w_shards[n].astype(BF16) for n in _SHARDED}
    (gathered_w_in,) = _all_gather([shard_b["w_in"]])
    w_in_f = _assemble("w_in", gathered_w_in)
    w_qkv, w_gl = _group_major(w_in_f[:, :QKV_COLS]), w_in_f[:, QKV_COLS:]
    full = {}

    def norm1(_, rows, consts):
        _, xh = _rms_stats(rows[0])
        return [xh * consts[0]], []

    (h1,) = _rowk("norm_mix", tm=1024, rows=[x0], consts=[g1], row_outs=[(D_MODEL, BF16)], epilogue=norm1)
    qkv, (land,) = _mm("proj_qkv", h1, w_qkv, out_dtype=BF16, tm=1024, tn=768, tk=D_MODEL,
                       rider=_Spread([shard_b["w_mlp_in"]], chunked=False))
    full["w_mlp_in"] = _assemble("w_mlp_in", land)
    gl = _mm("proj_gates", h1, w_gl, out_dtype=BF16, tm=1024, tn=1024, tk=D_MODEL)
    views = [_dil_view(qkv, g) for g in range(len(DIL_GROUPS))]
    dil = [_dil_fwd(views[g], g) for g in range(len(DIL_GROUPS))]
    os_, lses = [d[0] for d in dil], [d[1] for d in dil]
    o_a = _dil_mix_fwd(os_, lses)
    riding = ("w_mlp_out", "w_out", "w_up_sb", "w_up_dil")
    (o_b, tot_b, sb_steps), lands = _sb_fwd(qkv, rider=_Spread([shard_b[n] for n in riding], chunked=False))
    full.update({n: _assemble(n, land) for n, land in zip(riding, lands)})
    x1, h2 = _mixer_fwd(o_a, o_b, gl, x0, bg, g2, full["w_up_dil"], full["w_up_sb"], full["w_out"], 512)
    f = _mm("mlp_in", h2, full["w_mlp_in"], out_dtype=BF16, tm=1024, tn=1024, tk=D_MODEL,
            epilogue=lambda r, _: jnp.square(jnp.maximum(r, 0.0)))

    def head(acc, rows, consts):
        x1v, tv = rows
        g3v = consts[0]
        x2 = x1v + acc
        r, xh = _rms_stats(x2)
        diff = xh * g3v - tv
        loss = (0.5 / D_MODEL) * jnp.sum(jnp.sum(diff * diff, axis=0, keepdims=True), axis=1, keepdims=True)
        dy = diff * (1.0 / D_MODEL)
        dx2, dg = _rms_bwd(dy, xh, r, g3v)
        return [dx2, dx2], [dg, jnp.broadcast_to(loss, (1, LANES))]

    dx2, dx2b, gg3, loss_part = _rowk(
        "mlp_out_loss", a=f, w=full["w_mlp_out"], tm=512, tk=D_FF, rows=[x1, target], consts=[g3],
        row_outs=[(D_MODEL, F32), (D_MODEL, BF16)], acc_outs=[D_MODEL, LANES], epilogue=head)

    da = _mm("mlp_out_bwd", dx2b, full["w_mlp_out"], tb=True, out_dtype=BF16, tm=1024, tn=1024, tk=D_MODEL, extra=f,
             epilogue=lambda r, fv: r * (2.0 * jnp.sqrt(fv.astype(F32))))
    g_w_mlp_out = _mm("grad_w_mlp_out", f, dx2b, ta=True, out_dtype=BF16, tm=1024, tn=1024, tk=2048)
    g_w_mlp_in = _mm("grad_w_mlp_in", h2, da, ta=True, out_dtype=BF16, tm=1024, tn=1024, tk=2048, col_chunks=N_DEV)

    def norm_bwd(acc, rows, consts):
        xv, dres = rows
        r, xh = _rms_stats(xv)
        dx, dg = _rms_bwd(acc, xh, r, consts[0])
        return [dres + dx], [dg]

    parts = {}
    (dx1, gg2), (parts["w_mlp_in"],) = _rowk(
        "mlp_in_bwd", a=da, w=full["w_mlp_in"], nt=True, tm=512, tk=D_FF, rows=[x1, dx2], consts=[g2],
        row_outs=[(D_MODEL, F32)], acc_outs=[D_MODEL], epilogue=norm_bwd,
        rider=_Spread([g_w_mlp_in], chunked=True))
    (do_a, do_b, dgl, g_w_out, g_w_ud, g_w_us, g_bg), (parts["w_mlp_out"],) = _mixer_bwd(
        dx1, o_a, o_b, gl, bg, full["w_up_dil"], full["w_up_sb"], full["w_out"], 512,
        rider=_Spread([_chunk("w_mlp_out", g_w_mlp_out)], chunked=True))
    mix = _dil_mix_bwd(do_a, os_, lses)
    small_three = {"w_out": _chunk("w_out", g_w_out), "w_up_sb": g_w_us, "w_up_dil": g_w_ud}
    grads, lands = _dil_bwd(views[0], mix[0], lses[0], mix[3], 0,
                            rider=_Spread(list(small_three.values()), chunked=True))
    parts.update(dict(zip(small_three, lands)))
    dil_b = [grads] + [_dil_bwd(views[g], mix[g], lses[g], mix[3 + g], g) for g in (1, 2)]
    dq_b, dk_b, dv_b = _sb_bwd(qkv, do_b, tot_b, sb_steps)
    dproj = [d[0] for d in dil_b] + [d[1] for d in dil_b] + [d[2] for d in dil_b] + [dq_b, dk_b, dv_b, dgl]
    g_w_in = _chunk_cols([
        _grad_cols("grad_w_in_dil", h1, dproj[:9], tm=D_MODEL, tk=1024),
        _grad_cols("grad_w_in_sb", h1, dproj[9:12], tm=D_MODEL, tk=1024),
        _grad_cols("grad_w_in_gates", h1, dproj[12:], tm=D_MODEL, tk=1024)])
    (grad_x, gg1), (parts["w_in"],) = _rowk(
        "in_proj_bwd", a=dproj, w=w_in_f, nt=True, tm=512, tk=IN_COLS, rows=[x0, dx1], consts=[g1],
        row_outs=[(D_MODEL, F32)], acc_outs=[D_MODEL], epilogue=norm_bwd,
        rider=_Spread([g_w_in], chunked=True))

    small_part = _pack_small({"norm_mix_g": gg1, "b_gate": g_bg, "norm_mlp_g": gg2, "norm_final_g": gg3}, loss_part)
    (small_parts,) = _exchange([jnp.broadcast_to(small_part[None], (N_DEV, 1, _SMALL_N))])

    tags = ("grad_", "delta_", "new_m_", "new_v_")
    outs = {}
    for n, p in parts.items():
        res = _reduce_adamw("adamw_" + n, p, w_shards[n], m_shards[n], v_shards[n], 256)
        for tag, val in zip(tags, res):
            outs[tag + n] = val.reshape(given[n].shape)
    small_w = _pack_small(given, jnp.zeros((1, LANES), F32))
    small_m = _pack_small({n: given["m_" + n] for n, _ in _SMALL}, jnp.zeros((1, LANES), F32))
    small_v = _pack_small({n: given["v_" + n] for n, _ in _SMALL}, jnp.ones((1, LANES), F32))
    small_res = _reduce_adamw("adamw_replicated", small_parts, small_w, small_m, small_v, 8)

    small_shapes = {n: given[n].shape for n, _ in _SMALL}
    for tag, small in zip(tags, small_res):
        small_vals, tail = _unpack_small(small, small_shapes)
        for n, val in small_vals.items():
            outs[tag + n] = val
        if tag == "grad_":
            loss = tail[0, 0]
    names = ["norm_mix_g", "w_in", "b_gate", "w_up_dil", "w_up_sb", "w_out", "norm_mlp_g", "w_mlp_in", "w_mlp_out",
             "norm_final_g"]
    return (loss, grad_x.reshape(x.shape), *[outs["grad_" + n] for n in names], *[outs["delta_" + n] for n in names],
            *[outs["new_m_" + n] for n in names], *[outs["new_v_" + n] for n in names])
```

```python
import functools
import math

import jax
import jax.numpy as jnp
from jax import lax
from jax.experimental import pallas as pl
from jax.experimental.pallas import tpu as pltpu

_pcall = pl.pallas_call

F32 = jnp.float32
BF16 = jnp.bfloat16

D_MODEL = 1024
HEAD_DIM = 64
DIL_GROUPS = ((128, 1), (512, 4), (2048, 16))
DIL_HEADS_PER_GROUP = 4
N_DIL_HEADS = 12
N_SB_HEADS = 8
DIL_WIDTH = 768
DIL_OUT_WIDTH = 256
SB_WIDTH = 512
D_FF = 4096
BLOCK = 128
RMS_EPS = 1e-6
NEG_INF = -1e30
QKV_COLS = 3 * DIL_WIDTH + 3 * SB_WIDTH
IN_COLS = QKV_COLS + 2 * D_MODEL
N_DEV = 8

ADAM_LR = 0.001
ADAM_B1 = 0.9
ADAM_B2 = 0.999
ADAM_EPS = 1e-08
ADAM_WD = 0.01
ADAM_STEP = 10

VMEM_LIMIT = 56 * 1024 * 1024
SB_TK = 256
LANES = 128

_ARB = pltpu.ARBITRARY


def _cparams(n_axes, **kw):
    return pltpu.CompilerParams(dimension_semantics=(_ARB,) * n_axes, vmem_limit_bytes=VMEM_LIMIT, **kw)


def _dot(a, b):
    return jnp.dot(a, b, preferred_element_type=F32)


def _dot_nt(a, b):
    return lax.dot_general(a, b, (((1,), (1,)), ((), ())), preferred_element_type=F32)


def _dot_tn(a, b):
    return lax.dot_general(a, b, (((0,), (0,)), ((), ())), preferred_element_type=F32)


def _split_hi_lo(x):
    hi = x.astype(BF16)
    lo = (x - hi.astype(F32)).astype(BF16)
    return hi, lo


def _dot_hi_lo(x, m):
    hi, lo = _split_hi_lo(x)
    return _dot(hi, m) + _dot(lo, m)


def _sigmoid(x):
    return 1.0 / (1.0 + jnp.exp(-x))


_HBM = pl.BlockSpec(memory_space=pltpu.HBM)
_MESH = pl.DeviceIdType.MESH


class _Spread:
    def __init__(self, srcs, chunked):
        self.srcs, self.chunked, self.n = list(srcs), chunked, len(srcs)

    def land_shapes(self):
        return [jax.ShapeDtypeStruct((N_DEV,) + (s.shape[1:] if self.chunked else s.shape), s.dtype) for s in self.srcs]

    def scratch(self):
        dma = pltpu.SemaphoreType.DMA
        return [dma((7 * self.n,)), dma((7 * self.n,)), dma((self.n,))]

    def copies(self, src_refs, land_refs, send_sems, recv_sems, local_sems):
        x, y, c = lax.axis_index("x"), lax.axis_index("y"), lax.axis_index("c")
        me = 4 * x + 2 * y + c
        out = []
        for a, (src, land) in enumerate(zip(src_refs, land_refs)):
            out.append(pltpu.make_async_copy(src.at[me] if self.chunked else src, land.at[me], local_sems.at[a]))
            for k in range(1, N_DEV):
                px, py, pc = x ^ (k >> 2), y ^ ((k >> 1) & 1), c ^ (k & 1)
                out.append(pltpu.make_async_remote_copy(
                    src_ref=src.at[4 * px + 2 * py + pc] if self.chunked else src, dst_ref=land.at[me],
                    send_sem=send_sems.at[7 * a + k - 1], recv_sem=recv_sems.at[7 * a + k - 1],
                    device_id=(px, py, pc), device_id_type=_MESH))
        return out


def _call(body, args, rider=None, **kw):
    if rider is None:
        return _pcall(body, **kw)(*args)
    grid = kw["grid"]
    single = not isinstance(kw["out_shape"], (list, tuple))
    out_specs = [kw["out_specs"]] if single else list(kw["out_specs"])
    out_shape = [kw["out_shape"]] if single else list(kw["out_shape"])
    in_specs, scratch = list(kw["in_specs"]), list(kw.get("scratch_shapes", []))
    n_in, n_out, n_s, n = len(in_specs), len(out_shape), len(scratch), rider.n

    def hosted(*refs):
        ins, srcs = refs[:n_in], refs[n_in:n_in + n]
        outs, lands = refs[n_in + n:n_in + n + n_out], refs[n_in + n + n_out:n_in + 2 * n + n_out]
        own_scratch, sems = refs[n_in + 2 * n + n_out:n_in + 2 * n + n_out + n_s], refs[n_in + 2 * n + n_out + n_s:]
        ids = [pl.program_id(d) for d in range(len(grid))]
        first = functools.reduce(jnp.logical_and, [i == 0 for i in ids])
        last = functools.reduce(jnp.logical_and, [i == g - 1 for i, g in zip(ids, grid)])
        copies = rider.copies(srcs, lands, *sems)

        @pl.when(first)
        def _():
            for cp in copies:
                cp.start()

        body(*ins, *outs, *own_scratch)

        @pl.when(last)
        def _():
            for cp in copies:
                cp.wait()

    kw = dict(kw, in_specs=in_specs + [_HBM] * n, out_specs=out_specs + [_HBM] * n,
              out_shape=out_shape + rider.land_shapes(), scratch_shapes=scratch + rider.scratch())
    res = _pcall(hosted, **kw)(*args, *rider.srcs)
    return (res[0] if single else list(res[:n_out])), list(res[n_out:])


def _mm(name, a, b, *, ta=False, tb=False, out_dtype, tm, tn, tk, epilogue=None, extra=None, rider=None,
        col_chunks=None):
    m = a.shape[1] if ta else a.shape[0]
    k = a.shape[0] if ta else a.shape[1]
    n = b.shape[0] if tb else b.shape[1]
    assert (b.shape[1] if tb else b.shape[0]) == k
    tm, tn, tk = min(tm, m), min(tn, n), min(tk, k)
    assert m % tm == 0 and n % tn == 0 and k % tk == 0, (name, m, n, k, tm, tn, tk)
    nk = k // tk
    dn = (((0 if ta else 1,), (1 if tb else 0,)), ((), ()))
    in_place = nk > 1 and epilogue is None and out_dtype == F32 and col_chunks is None
    cw = n // col_chunks if col_chunks else None
    assert cw is None or (tn % cw == 0 and cw % LANES == 0)

    def body(*refs):
        if extra is not None:
            a_ref, b_ref, e_ref, o_ref = refs[:4]
        else:
            a_ref, b_ref, o_ref = refs[:3]
            e_ref = None

        def finish(r):
            if epilogue is not None:
                r = epilogue(r, None if e_ref is None else e_ref[...])
            if cw is None:
                o_ref[...] = r.astype(out_dtype)
            else:
                for c in range(tn // cw):
                    o_ref[c] = r[:, c * cw:(c + 1) * cw].astype(out_dtype)

        part = lax.dot_general(a_ref[...].astype(BF16), b_ref[...].astype(BF16), dn, preferred_element_type=F32)
        if nk == 1:
            finish(part)
        else:
            acc_ref = o_ref if in_place else refs[-1]
            kk = pl.program_id(2)

            @pl.when(kk == 0)
            def _():
                acc_ref[...] = part

            @pl.when(kk > 0)
            def _():
                acc_ref[...] += part

            if not in_place:

                @pl.when(kk == nk - 1)
                def _():
                    finish(acc_ref[...])

    a_spec = pl.BlockSpec((tk, tm), lambda j, i, kk: (kk, i)) if ta else pl.BlockSpec((tm, tk), lambda j, i, kk: (i, kk))
    b_spec = pl.BlockSpec((tn, tk), lambda j, i, kk: (j, kk)) if tb else pl.BlockSpec((tk, tn), lambda j, i, kk: (kk, j))
    o_spec = pl.BlockSpec((tm, tn), lambda j, i, kk: (i, j))
    in_specs = [a_spec, b_spec]
    args = [a, b]
    if extra is not None:
        in_specs.append(o_spec)
        args.append(extra)
    out_shape = jax.ShapeDtypeStruct((m, n), out_dtype)
    if cw is not None:
        o_spec = pl.BlockSpec((tn // cw, tm, cw), lambda j, i, kk: (j, i, 0))
        out_shape = jax.ShapeDtypeStruct((col_chunks, m, cw), out_dtype)
    return _call(
        body, args, rider,
        name=name,
        grid=(n // tn, m // tm, nk),
        in_specs=in_specs,
        out_specs=o_spec,
        out_shape=out_shape,
        scratch_shapes=[pltpu.VMEM((tm, tn), F32)] if (nk > 1 and not in_place) else [],
        compiler_params=_cparams(3),
    )


def _grad_cols(name, a, parts, *, tm, tk, rider=None):
    k, m = a.shape
    n = sum(p.shape[1] for p in parts)
    assert m % tm == 0 and k % tk == 0
    nk = k // tk

    def body(*refs):
        a_ref, p_refs, o_ref, acc_ref = refs[0], refs[1:1 + len(parts)], refs[1 + len(parts)], refs[2 + len(parts)]
        kk = pl.program_id(1)
        side_by_side = jnp.concatenate([p_ref[...].astype(BF16) for p_ref in p_refs], axis=1)
        term = _dot_tn(a_ref[...].astype(BF16), side_by_side)

        @pl.when(kk == 0)
        def _():
            acc_ref[...] = term

        @pl.when(kk > 0)
        def _():
            acc_ref[...] += term

        @pl.when(kk == nk - 1)
        def _():
            o_ref[...] = acc_ref[...].astype(o_ref.dtype)

    return _call(
        body, [a] + list(parts), rider,
        name=name,
        grid=(m // tm, nk),
        in_specs=[pl.BlockSpec((tk, tm), lambda i, kk: (kk, i))]
        + [pl.BlockSpec((tk, p.shape[1]), lambda i, kk: (kk, 0)) for p in parts],
        out_specs=pl.BlockSpec((tm, n), lambda i, kk: (i, 0)),
        out_shape=jax.ShapeDtypeStruct((m, n), BF16),
        scratch_shapes=[pltpu.VMEM((tm, n), F32)],
        compiler_params=_cparams(2),
    )


def _rowk(name, *, a=None, w=None, nt=False, tm, tk=None, rows=(), consts=(), row_outs=(), acc_outs=(), epilogue,
          rider=None):
    has_mm = a is not None
    a_parts = list(a) if isinstance(a, (list, tuple)) else ([a] if has_mm else [])
    n_a = len(a_parts)
    m = a_parts[0].shape[0] if has_mm else rows[0].shape[0]
    assert m % tm == 0
    nm = m // tm
    if has_mm:
        k = sum(p.shape[1] for p in a_parts)
        n = w.shape[0] if nt else w.shape[1]
        tk = min(tk, k)
        assert k % tk == 0 and (n_a == 1 or tk == k)
        nk = k // tk
    else:
        nk = 1
    n_rows, n_consts, n_ro, n_ao = len(rows), len(consts), len(row_outs), len(acc_outs)

    def body(*refs):
        pos = 0
        if has_mm:
            a_refs, w_ref = refs[:n_a], refs[n_a]
            pos = n_a + 1
        row_refs = refs[pos:pos + n_rows]
        pos += n_rows
        const_refs = refs[pos:pos + n_consts]
        pos += n_consts
        ro_refs = refs[pos:pos + n_ro]
        pos += n_ro
        ao_refs = refs[pos:pos + n_ao]
        pos += n_ao
        i = pl.program_id(0)
        kk = pl.program_id(1)

        def finish(acc):
            ro_vals, ao_vals = epilogue(acc, [r[...] for r in row_refs], [c[...] for c in const_refs])
            for r, v in zip(ro_refs, ro_vals):
                r[...] = v.astype(r.dtype)
            for r, v in zip(ao_refs, ao_vals):

                @pl.when(i == 0)
                def _(r=r, v=v):
                    r[...] = v

                @pl.when(i > 0)
                def _(r=r, v=v):
                    r[...] += v

        if not has_mm:
            finish(None)
            return
        part, off = None, 0
        for a_ref in a_refs:
            width = a_ref.shape[1]
            cols = slice(None) if n_a == 1 else slice(off, off + width)
            av = a_ref[...].astype(BF16)
            term = _dot_nt(av, w_ref[:, cols]) if nt else _dot(av, w_ref[cols, :])
            part = term if part is None else part + term
            off += width
        if nk == 1:
            finish(part)
        else:
            acc_ref = refs[pos]

            @pl.when(kk == 0)
            def _():
                acc_ref[...] = part

            @pl.when(kk > 0)
            def _():
                acc_ref[...] += part

            @pl.when(kk == nk - 1)
            def _():
                finish(acc_ref[...])

    once = pl.Buffered(1)
    in_specs, args = [], []
    if has_mm:
        for part in a_parts:
            in_specs.append(pl.BlockSpec((tm, tk if n_a == 1 else part.shape[1]), lambda i, kk: (i, kk)))
        w_mode = once if nk == 1 else None
        in_specs.append(pl.BlockSpec((n, tk), lambda i, kk: (0, kk), pipeline_mode=w_mode) if nt
                        else pl.BlockSpec((tk, n), lambda i, kk: (kk, 0), pipeline_mode=w_mode))
        args += a_parts + [w]
    for r in rows:
        in_specs.append(pl.BlockSpec((tm, r.shape[1]), lambda i, kk: (i, 0)))
        args.append(r)
    for c in consts:
        in_specs.append(pl.BlockSpec(c.shape, lambda i, kk: (0,) * c.ndim, pipeline_mode=once))
        args.append(c)
    out_specs, out_shape = [], []
    for width, dt in row_outs:
        out_specs.append(pl.BlockSpec((tm, width), lambda i, kk: (i, 0)))
        out_shape.append(jax.ShapeDtypeStruct((m, width), dt))
    for width in acc_outs:
        out_specs.append(pl.BlockSpec((1, width), lambda i, kk: (0, 0)))
        out_shape.append(jax.ShapeDtypeStruct((1, width), F32))
    return _call(
        body, args, rider,
        name=name,
        grid=(nm, nk),
        in_specs=in_specs,
        out_specs=out_specs,
        out_shape=out_shape,
        scratch_shapes=[pltpu.VMEM((tm, n), F32)] if (has_mm and nk > 1) else [],
        compiler_params=_cparams(2),
    )


def _rms_stats(x):
    r = lax.rsqrt(jnp.mean(x * x, axis=-1, keepdims=True) + RMS_EPS)
    return r, x * r


def _rms_bwd(dh, xh, r, g):
    gy = dh * g
    dx = r * (gy - xh * jnp.mean(gy * xh, axis=-1, keepdims=True))
    return dx, jnp.sum(dh * xh, axis=0, keepdims=True)


def _alibi_slope(head):
    return 2.0 ** (-8.0 * (head + 1) / N_DIL_HEADS)


DIL_STEP_BLOCKS = 4


def _dil_band(first_block):
    qi = lax.broadcasted_iota(jnp.int32, (BLOCK, 2 * BLOCK), 0)
    kj = lax.broadcasted_iota(jnp.int32, (BLOCK, 2 * BLOCK), 1)
    steps = qi + BLOCK - kj
    valid = (steps >= 0) & (steps <= BLOCK)
    if first_block is not False:
        valid = valid & ((kj >= BLOCK) | jnp.logical_not(first_block))
    return steps.astype(F32), valid


def _dil_step_specs(ncb, cols, nblk, clamp):
    def own(col):
        return pl.BlockSpec((nblk * BLOCK, DIL_OUT_WIDTH), lambda r, i: (clamp(i), r * ncb + col))

    def before(col):
        return pl.BlockSpec((BLOCK, DIL_OUT_WIDTH), lambda r, i: (jnp.maximum(clamp(i) * nblk - 1, 0), r * ncb + col))

    return [own(cols[0]), own(cols[1]), before(cols[1]), own(cols[2]), before(cols[2])]


DIL_RELAYOUT_ROWS = 1024


def _view_scratch(width):
    return pltpu.VMEM((width // LANES, DIL_RELAYOUT_ROWS, LANES), F32)


def _rows_from_view(src, scr, d, w):
    sub = src.shape[0]
    for j in range(w // LANES):
        for r in range(d):
            scr[j, pl.ds(r, sub, stride=d), :] = src[:, r * w + j * LANES:r * w + (j + 1) * LANES].astype(F32)


def _rows_to_view(scr, dst, d, w):
    sub = dst.shape[0]
    for j in range(w // LANES):
        for r in range(d):
            dst[:, r * w + j * LANES:r * w + (j + 1) * LANES] = scr[j, pl.ds(r, sub, stride=d), :].astype(dst.dtype)


def _dil_relayout(name, xs, dilation, to_view, col_block=0, width=None):
    d = dilation
    tm = DIL_RELAYOUT_ROWS
    rows = tm // d
    if to_view:
        s = xs[0].shape[0]
        widths = [width or x.shape[1] for x in xs]
    else:
        s = xs[0].shape[0] * d
        widths = [v.shape[1] // d for v in xs]
    assert s % tm == 0 and all(w % LANES == 0 for w in widths) and all(x.dtype == BF16 for x in xs)
    n = len(xs)
    blk = 256
    per = blk // d
    assert per % 16 == 0 and tm % blk == 0

    n_steps = s // tm
    slots = 3
    in_rows = tm if to_view else rows

    def body(*refs):
        hbm_refs, out_refs, rings, sem = refs[:n], refs[n:2 * n], refs[2 * n:3 * n], refs[3 * n]
        i = pl.program_id(0)

        def fetch(step, k):
            lines = pl.ds(pl.multiple_of(step * in_rows, in_rows), in_rows)
            src = hbm_refs[k].at[lines, col_block * widths[k]:(col_block + 1) * widths[k]] if to_view \
                else hbm_refs[k].at[lines, :]
            return pltpu.make_async_copy(src, rings[k].at[step % slots], sem.at[k, step % slots])

        for ahead in range(slots - 1):

            @pl.when(i == 0)
            def _(ahead=ahead):
                if ahead < n_steps:
                    for k in range(n):
                        fetch(ahead, k).start()

        @pl.when(i + slots - 1 < n_steps)
        def _():
            for k in range(n):
                fetch(i + slots - 1, k).start()

        for k in range(n):
            fetch(i, k).wait()
        in_refs = [ring.at[i % slots] for ring in rings]
        i0 = lax.broadcasted_iota(jnp.int32, (blk, blk), 0)
        i1 = lax.broadcasted_iota(jnp.int32, (blk, blk), 1)
        sort = (i1 == (i0 % per) * d + i0 // per) if to_view else (i0 == (i1 % per) * d + i1 // per)
        sort = jnp.where(sort, 1.0, 0.0).astype(BF16)
        for src, dst, w in zip(in_refs, out_refs, widths):
            for b in range(tm // blk):
                if to_view:
                    y = _dot(sort, src[b * blk:(b + 1) * blk, :]).astype(BF16)
                    for r in range(d):
                        dst[b * per:(b + 1) * per, r * w:(r + 1) * w] = y[r * per:(r + 1) * per, :]
                else:
                    by_residue = jnp.concatenate(
                        [src[b * per:(b + 1) * per, r * w:(r + 1) * w] for r in range(d)], axis=0)
                    dst[b * blk:(b + 1) * blk, :] = _dot(sort, by_residue).astype(BF16)

    natural = [pl.BlockSpec((tm, w), lambda i: (i, 0)) for w in widths]
    viewed = [pl.BlockSpec((rows, d * w), lambda i: (i, 0)) for w in widths]
    return _pcall(
        body,
        name=name,
        grid=(n_steps,),
        in_specs=[_HBM] * n,
        out_specs=viewed if to_view else natural,
        out_shape=[jax.ShapeDtypeStruct((s // d, d * w) if to_view else (s, w), BF16) for w in widths],
        scratch_shapes=[pltpu.VMEM((slots, tm, w) if to_view else (slots, rows, d * w), BF16) for w in widths]
        + [pltpu.SemaphoreType.DMA((n, slots))],
        compiler_params=_cparams(1),
    )(*xs)


def _dil_fwd(view, group):
    window, dilation = DIL_GROUPS[group]
    qkv_v, ncb, cols = view
    sub = qkv_v.shape[0]
    s = sub * dilation
    nb = sub // BLOCK
    assert nb * BLOCK * dilation == s and window // dilation == BLOCK
    nblk = min(DIL_STEP_BLOCKS, nb)
    assert nb % nblk == 0
    slopes = [_alibi_slope(group * DIL_HEADS_PER_GROUP + h) * dilation for h in range(DIL_HEADS_PER_GROUP)]

    def body(q_ref, kc_ref, kp_ref, vc_ref, vp_ref, o_ref, lse_ref):
        i = pl.program_id(1)
        kk_all = jnp.concatenate([kp_ref[...], kc_ref[...]], axis=0)
        vv_all = jnp.concatenate([vp_ref[...], vc_ref[...]], axis=0)
        head_id = lax.broadcasted_iota(jnp.int32, (1, DIL_OUT_WIDTH), 1) // HEAD_DIM
        chains = [(b, h) for b in range(nblk) for h in range(DIL_HEADS_PER_GROUP)]
        rows = lambda b: slice(b * BLOCK, (b + 1) * BLOCK)
        keys = lambda b: slice(b * BLOCK, (b + 2) * BLOCK)
        bands = [_dil_band(i == 0 if b == 0 else False) for b in range(nblk)]
        qs = [q_ref[rows(b), :] for b in range(nblk)]
        scores = [_dot_nt(jnp.where(head_id == h, qs[b], jnp.zeros_like(qs[b])), kk_all[keys(b)]) for b, h in chains]
        ps, lses = [], []
        for (b, h), sc in zip(chains, scores):
            steps, valid = bands[b]
            logits = jnp.where(valid, sc * (1.0 / math.sqrt(HEAD_DIM)) - slopes[h] * steps, NEG_INF)
            mx = jnp.max(logits, axis=1, keepdims=True)
            e = jnp.exp(logits - mx)
            den = jnp.sum(e, axis=1, keepdims=True)
            lses.append(mx + jnp.log(den))
            ps.append((e * (1.0 / den)).astype(BF16))
        outs = [_dot(p, vv_all[keys(b)]) for (b, h), p in zip(chains, ps)]
        for b in range(nblk):
            mine = [n for n, ch in enumerate(chains) if ch[0] == b]
            o, lse_all = outs[mine[0]], lses[mine[0]]
            for n in mine[1:]:
                o = jnp.where(head_id == chains[n][1], outs[n], o)
                lse_all = jnp.where(head_id == chains[n][1], lses[n], lse_all)
            o_ref[rows(b), :] = o
            lse_ref[rows(b), :] = jnp.broadcast_to(lse_all, o.shape)

    out_spec = pl.BlockSpec((nblk * BLOCK, DIL_OUT_WIDTH), lambda r, i: (i, r))
    o, lse = _pcall(
        body,
        name=f"dil_fwd_g{group}",
        grid=(dilation, nb // nblk),
        in_specs=_dil_step_specs(ncb, cols, nblk, lambda i: i),
        out_specs=[out_spec, out_spec],
        out_shape=[jax.ShapeDtypeStruct((sub, dilation * DIL_OUT_WIDTH), F32)] * 2,
        compiler_params=_cparams(2),
    )(qkv_v, qkv_v, qkv_v, qkv_v, qkv_v)
    return o, lse


def _dil_bwd(view, do_g, lse_g, dterm_g, group, rider=None):
    window, dilation = DIL_GROUPS[group]
    qkv_v, ncb, cols = view
    sub = qkv_v.shape[0]
    nb = sub // BLOCK
    nblk = min(DIL_STEP_BLOCKS, nb)
    n_steps = nb // nblk
    slopes = [_alibi_slope(group * DIL_HEADS_PER_GROUP + h) * dilation for h in range(DIL_HEADS_PER_GROUP)]
    scale = 1.0 / math.sqrt(HEAD_DIM)
    tail = slice((nblk - 1) * BLOCK, nblk * BLOCK)
    single = n_steps == 1

    def body(q_ref, kc_ref, kp_ref, vc_ref, vp_ref, do_ref, lse_ref, dt_ref, dq_ref, dk_ref, dv_ref, *carry_refs):
        i = pl.program_id(1)

        def init():
            for carry_ref in carry_refs:
                carry_ref[...] = jnp.zeros_like(carry_ref)

        def compute():
            kk_all = jnp.concatenate([kp_ref[...], kc_ref[...]], axis=0)
            vv_all = jnp.concatenate([vp_ref[...], vc_ref[...]], axis=0)
            lane = lax.broadcasted_iota(jnp.int32, (1, DIL_OUT_WIDTH), 1)
            head_id = lane // HEAD_DIM
            chains = [(b, h) for b in range(nblk) for h in range(DIL_HEADS_PER_GROUP)]
            rows = lambda b: slice(b * BLOCK, (b + 1) * BLOCK)
            keys = lambda b: slice(b * BLOCK, (b + 2) * BLOCK)
            bands = [_dil_band(i == 0 if b == 0 else False) for b in range(nblk)]
            qms, doms = [], []
            for b, h in chains:
                q, do = q_ref[rows(b), :], do_ref[rows(b), :]
                qms.append(jnp.where(head_id == h, q, jnp.zeros_like(q)))
                doms.append(jnp.where(head_id == h, do, jnp.zeros_like(do)))
            scores = [_dot_nt(qm, kk_all[keys(b)]) for (b, h), qm in zip(chains, qms)]
            dps = [_dot_nt(dom, vv_all[keys(b)]) for (b, h), dom in zip(chains, doms)]
            pbs, dss = [], []
            for n, (b, h) in enumerate(chains):
                steps, valid = bands[b]
                first = lane == h * HEAD_DIM
                lse = jnp.sum(jnp.where(first, lse_ref[rows(b), :], 0.0), axis=1, keepdims=True)
                dt = jnp.sum(jnp.where(first, dt_ref[rows(b), :], 0.0), axis=1, keepdims=True)
                logits = jnp.where(valid, scores[n] * scale - slopes[h] * steps, NEG_INF)
                p = jnp.where(valid, jnp.exp(logits - lse), 0.0)
                pbs.append(p.astype(BF16))
                dss.append((p * (dps[n] + dt) * scale).astype(BF16))
            dqs = [_dot(ds, kk_all[keys(b)]) for (b, h), ds in zip(chains, dss)]
            dks = [_dot_tn(ds, qm) for ds, qm in zip(dss, qms)]
            dvs = [_dot_tn(pb, dom) for pb, dom in zip(pbs, doms)]
            dkk, dvv = [], []
            for b in range(nblk):
                mine = [n for n, ch in enumerate(chains) if ch[0] == b]
                dq = dqs[mine[0]]
                for n in mine[1:]:
                    dq = jnp.where(head_id == chains[n][1], dqs[n], dq)
                dq_ref[rows(b), :] = dq.astype(dq_ref.dtype)
                dkk.append((dks[mine[0]] + dks[mine[1]]) + (dks[mine[2]] + dks[mine[3]]))
                dvv.append((dvs[mine[0]] + dvs[mine[1]]) + (dvs[mine[2]] + dvs[mine[3]]))
            for n, (out_ref, parts) in enumerate(((dk_ref, dkk), (dv_ref, dvv))):
                done = [parts[b][BLOCK:] + parts[b + 1][:BLOCK] if b + 1 < nblk else parts[b][BLOCK:]
                        for b in range(nblk)]
                if single:
                    for b in range(nblk):
                        out_ref[rows(b), :] = done[b].astype(out_ref.dtype)
                    continue
                carry_ref = carry_refs[n]
                if nblk > 1:
                    out_ref[: (nblk - 1) * BLOCK, :] = carry_ref[: (nblk - 1) * BLOCK, :].astype(out_ref.dtype)
                out_ref[tail, :] = (carry_ref[tail, :] + parts[0][:BLOCK]).astype(out_ref.dtype)
                for b in range(nblk):
                    carry_ref[rows(b), :] = done[b]

        def flush():
            for out_ref, carry_ref in zip((dk_ref, dv_ref), carry_refs):
                out_ref[...] = carry_ref[...].astype(out_ref.dtype)

        if single:
            compute()
        else:
            pl.when(i == 0)(init)
            pl.when(i < n_steps)(compute)
            pl.when(i == n_steps)(flush)

    clamp = lambda i: jnp.minimum(i, n_steps - 1)
    row_spec = pl.BlockSpec((nblk * BLOCK, DIL_OUT_WIDTH), lambda r, i: (clamp(i), r))
    late_spec = pl.BlockSpec((nblk * BLOCK, DIL_OUT_WIDTH), lambda r, i: (jnp.maximum(i - 1, 0), r))
    res = _call(
        body, (qkv_v, qkv_v, qkv_v, qkv_v, qkv_v, do_g, lse_g, dterm_g), rider,
        name=f"dil_bwd_g{group}",
        grid=(dilation, n_steps + (0 if single else 1)),
        in_specs=_dil_step_specs(ncb, cols, nblk, clamp) + [row_spec, row_spec, row_spec],
        out_specs=[row_spec, row_spec, row_spec] if single else [row_spec, late_spec, late_spec],
        out_shape=[jax.ShapeDtypeStruct((sub, dilation * DIL_OUT_WIDTH), BF16)] * 3,
        scratch_shapes=[] if single else [pltpu.VMEM((nblk * BLOCK, DIL_OUT_WIDTH), F32)] * 2,
        compiler_params=_cparams(2),
    )
    grads, lands = res if rider is not None else (res, None)
    if dilation > 1:
        grads = _dil_relayout(f"dil_bwd_rows_g{group}", list(grads), dilation, to_view=False)
    return tuple(grads) if rider is None else (tuple(grads), lands)


def _dil_view(qkv, group):
    _, dilation = DIL_GROUPS[group]
    w = DIL_OUT_WIDTH
    if dilation == 1:
        return qkv, QKV_COLS // w, (3 * group, 3 * group + 1, 3 * group + 2)
    (own,) = _dil_relayout(f"dil_view_g{group}", [qkv], dilation, to_view=True, col_block=group, width=3 * w)
    return own, 3, (0, 1, 2)


def _group_major(w_qkv):
    w = DIL_OUT_WIDTH
    ng = len(DIL_GROUPS)
    cols = [w_qkv[:, (part * ng + g) * w:(part * ng + g + 1) * w] for g in range(ng) for part in range(3)]
    return jnp.concatenate(cols + [w_qkv[:, 3 * DIL_WIDTH:]], axis=1)


def _head_block_ones():
    r = lax.broadcasted_iota(jnp.int32, (DIL_OUT_WIDTH, DIL_OUT_WIDTH), 0) // HEAD_DIM
    c = lax.broadcasted_iota(jnp.int32, (DIL_OUT_WIDTH, DIL_OUT_WIDTH), 1) // HEAD_DIM
    return jnp.where(r == c, 1.0, 0.0).astype(BF16)


def _dil_mix_weights(l0, l1, l2):
    mx = jnp.maximum(jnp.maximum(l0, l1), l2)
    e0, e1, e2 = jnp.exp(l0 - mx), jnp.exp(l1 - mx), jnp.exp(l2 - mx)
    inv = 1.0 / (e0 + e1 + e2)
    return e0 * inv, e1 * inv, e2 * inv


def _dil_view_spec(dilation):
    return pl.BlockSpec((DIL_RELAYOUT_ROWS // dilation, dilation * DIL_OUT_WIDTH), lambda i: (i, 0))


def _dil_mix_call(name, body, s, ins, in_specs, outs, n_relaid):
    out_specs = [_dil_view_spec(d or 1) for d, _ in outs]
    out_shape = [jax.ShapeDtypeStruct((s // (d or 1), (d or 1) * DIL_OUT_WIDTH), dt) for d, dt in outs]
    return _pcall(
        body,
        name=name,
        grid=(s // DIL_RELAYOUT_ROWS,),
        in_specs=in_specs,
        out_specs=out_specs,
        out_shape=out_shape,
        scratch_shapes=[_view_scratch(DIL_OUT_WIDTH)] * n_relaid,
        compiler_params=_cparams(1),
    )(*ins)


DIL_MIX_CHUNK = 64
_DIL_SLABS = DIL_OUT_WIDTH // LANES


def _dil_rows(refs, scratch):
    dils = [d for _, d in DIL_GROUPS]
    assert dils[0] == 1
    readers = [lambda rows, j, ref=refs[0]: ref[rows, j * LANES:(j + 1) * LANES]]
    for ref, scr, d in zip(refs[1:], scratch, dils[1:]):
        _rows_from_view(ref, scr, d, DIL_OUT_WIDTH)
        readers.append(lambda rows, j, scr=scr: scr[j, rows, :])
    return readers


def _dil_mix_chunks(step):
    def chunk(c, carry):
        step(pl.ds(pl.multiple_of(c * DIL_MIX_CHUNK, DIL_MIX_CHUNK), DIL_MIX_CHUNK))
        return carry

    lax.fori_loop(0, DIL_RELAYOUT_ROWS // DIL_MIX_CHUNK, chunk, 0, unroll=4)


def _dil_mix_fwd(os_, lses):
    ng = len(DIL_GROUPS)
    s = os_[0].shape[0]

    def body(*refs):
        o_refs, l_refs, out_ref, scratch = refs[:ng], refs[ng:2 * ng], refs[2 * ng], refs[2 * ng + 1:]
        o_at = _dil_rows(o_refs, scratch[:ng - 1])
        l_at = _dil_rows(l_refs, scratch[ng - 1:])

        def step(rows):
            for j in range(_DIL_SLABS):
                w0, w1, w2 = _dil_mix_weights(*[at(rows, j) for at in l_at])
                o0, o1, o2 = [at(rows, j) for at in o_at]
                out_ref[rows, j * LANES:(j + 1) * LANES] = (w0 * o0 + w1 * o1 + w2 * o2).astype(out_ref.dtype)

        _dil_mix_chunks(step)

    specs = [_dil_view_spec(d) for _, d in DIL_GROUPS]
    (o_a,) = _dil_mix_call("dil_mix_fwd", body, s, list(os_) + list(lses), specs * 2, [(None, BF16)], 2 * (ng - 1))
    return o_a


def _dil_mix_bwd(do_a, os_, lses):
    ng = len(DIL_GROUPS)
    s = do_a.shape[0]
    dils = [d for _, d in DIL_GROUPS]

    def body(*refs):
        do_ref, o_refs, l_refs = refs[0], refs[1:1 + ng], refs[1 + ng:1 + 2 * ng]
        out_refs, scratch = refs[1 + 2 * ng:1 + 4 * ng], refs[1 + 4 * ng:]
        o_at = _dil_rows(o_refs, scratch[:ng - 1])
        l_at = _dil_rows(l_refs, scratch[ng - 1:2 * (ng - 1)])
        spare = iter(scratch[2 * (ng - 1):])
        staged = [None if dils[n % ng] == 1 else next(spare) for n in range(2 * ng)]
        ones = _head_block_ones()

        def step(rows):
            do = do_ref[rows, :].astype(F32)
            ws, prods = [], []
            for j in range(_DIL_SLABS):
                w0, w1, w2 = _dil_mix_weights(*[at(rows, j) for at in l_at])
                o0, o1, o2 = [at(rows, j) for at in o_at]
                ws.append((w0, w1, w2))
                prods.append(do[:, j * LANES:(j + 1) * LANES] * (w0 * o0 + w1 * o1 + w2 * o2))
            tot = _dot_hi_lo(jnp.concatenate(prods, axis=1), ones)
            for j in range(_DIL_SLABS):
                slab = slice(j * LANES, (j + 1) * LANES)
                vals = [w * do[:, slab] for w in ws[j]] + [-w * tot[:, slab] for w in ws[j]]
                for val, dst, scr in zip(vals, out_refs, staged):
                    if scr is None:
                        dst[rows, slab] = val.astype(dst.dtype)
                    else:
                        scr[j, rows, :] = val

        _dil_mix_chunks(step)
        for n, (dst, scr) in enumerate(zip(out_refs, staged)):
            if scr is not None:
                _rows_to_view(scr, dst, dils[n % ng], DIL_OUT_WIDTH)

    specs = [_dil_view_spec(d) for d in dils]
    return _dil_mix_call(
        "dil_mix_bwd", body, s, [do_a] + list(os_) + list(lses), [_dil_view_spec(1)] + specs * 2,
        [(d, BF16) for d in dils] + [(d, F32) for d in dils], 4 * (ng - 1))


_SB_Q0 = 3 * DIL_WIDTH // LANES
_SB_K0 = _SB_Q0 + SB_WIDTH // LANES
_SB_V0 = _SB_K0 + SB_WIDTH // LANES


_EXP_CLAMP = 88.0
_SB_DEAD = 104.0


def _tri(t, op):
    r = lax.broadcasted_iota(jnp.int32, (t, t), 0)
    c = lax.broadcasted_iota(jnp.int32, (t, t), 1)
    return jnp.where(op(r, c), 1.0, 0.0).astype(BF16)


def _softplus(z):
    return jnp.maximum(z, jnp.log(1.0 + jnp.exp(jnp.minimum(z, _EXP_CLAMP))))


def _sb_chain_head(qm, kj, mask):
    z = _dot_nt(qm, kj)
    sp = _softplus(z)
    return (sp if mask is None else jnp.where(mask, sp, 0.0)), z - sp


def _sb_fwd(qkv, rider=None):
    s = qkv.shape[0]
    t = SB_TK
    assert s % (2 * t) == 0
    nq = s // (2 * t)
    n_pairs = SB_WIDTH // LANES

    def body(q_ref, k_ref, v_ref, o_ref, tot_ref, steps_ref):
        p, i = pl.program_id(0), pl.program_id(1)
        lane_hi = lax.broadcasted_iota(jnp.int32, (1, LANES), 1) // HEAD_DIM
        later = _tri(t, lambda r, c: r > c)
        causal = lax.broadcasted_iota(jnp.int32, (t, t), 1) < lax.broadcasted_iota(jnp.int32, (t, t), 0)
        qms = []
        for x in range(2):
            q = q_ref[pl.ds(x * t, t), :] * (1.0 / math.sqrt(HEAD_DIM))
            qms.append([jnp.where(lane_hi == hh, q, jnp.zeros_like(q)) for hh in range(2)])

        def tile(j):
            off = pl.multiple_of(j * t, t)
            return k_ref[pl.ds(off, t), :], v_ref[pl.ds(off, t), :]

        def step(groups, carry):
            kv = [tile(j) for _, j, _ in groups]
            chains = [(g, x, hh) for g, (x, _, _) in enumerate(groups) for hh in range(2)]
            heads = [_sb_chain_head(qms[x][hh], kv[g][0], causal if groups[g][2] else None) for g, x, hh in chains]
            sufs = [_dot(sp.astype(BF16), later) for sp, _ in heads]
            cur = [list(carry[0]), list(carry[1])]
            for (g, x, hh), (sp, lpos), suf in zip(chains, heads, sufs):
                c, acc = cur[x][hh]
                a = jnp.exp(lpos - suf - c)
                if groups[g][2]:
                    a = jnp.where(causal, a, 0.0)
                cur[x][hh] = (c + jnp.sum(sp, axis=1, keepdims=True), acc + _dot(a.astype(BF16), kv[g][1]))
            return (tuple(cur[0]), tuple(cur[1]))

        def lowest(carry):
            return jnp.min(jnp.minimum(jnp.minimum(carry[0][0][0], carry[0][1][0]),
                                       jnp.minimum(carry[1][0][0], carry[1][1][0])))

        zero = (jnp.zeros((t, 1), F32), jnp.zeros((t, LANES), F32))
        start = ((zero, zero), (zero, zero))
        carry = lax.cond(
            i == 0,
            lambda ca: step([(0, 0, True), (1, 1, True), (1, 0, False)], ca),
            lambda ca: step([(0, 2 * i, True), (1, 2 * i + 1, True), (0, 2 * i - 1, False), (1, 2 * i, False)], ca),
            start)

        def walk(state):
            n, ca, _ = state
            ca = step([(0, 2 * i - 2 - n, False), (1, 2 * i - 1 - n, False)], ca)
            return n + 1, ca, lowest(ca)

        n_more, carry, low = lax.while_loop(
            lambda st: jnp.logical_and(st[0] + 1 < 2 * i, st[2] <= _SB_DEAD), walk, (jnp.int32(0), carry, lowest(carry)))
        b_last = jnp.logical_and(jnp.logical_and(i > 0, n_more + 1 == 2 * i), low <= _SB_DEAD)
        carry = lax.cond(b_last, lambda ca: step([(1, 0, False)], ca), lambda ca: ca, carry)
        for x in range(2):
            (c0, acc0), (c1, acc1) = carry[x]
            o_ref[pl.ds(x * t, t), :] = jnp.where(lane_hi == 0, acc0, acc1).astype(o_ref.dtype)
            tot_ref[pl.ds(x * t, t), :] = jnp.where(lane_hi == 0, c0, c1)
        steps_ref[p, i] = 1 + n_more + b_last.astype(jnp.int32)

    return _call(
        body, (qkv, qkv, qkv), rider,
        name="sb_fwd",
        grid=(n_pairs, nq),
        in_specs=[
            pl.BlockSpec((2 * t, LANES), lambda p, i: (i, _SB_Q0 + p)),
            pl.BlockSpec((s, LANES), lambda p, i: (0, _SB_K0 + p)),
            pl.BlockSpec((s, LANES), lambda p, i: (0, _SB_V0 + p)),
        ],
        out_specs=[pl.BlockSpec((2 * t, LANES), lambda p, i: (i, p))] * 2 + [pl.BlockSpec(memory_space=pltpu.SMEM)],
        out_shape=[jax.ShapeDtypeStruct((s, SB_WIDTH), BF16), jax.ShapeDtypeStruct((s, SB_WIDTH), F32),
                   jax.ShapeDtypeStruct((n_pairs, nq), jnp.int32)],
        compiler_params=_cparams(2),
    )


def _sb_bwd(qkv, do_b, tot_b, n_steps):
    s = qkv.shape[0]
    t = SB_TK
    nq = s // (2 * t)
    n_pairs = SB_WIDTH // LANES
    scale = 1.0 / math.sqrt(HEAD_DIM)

    def body(steps_ref, q_ref, k_ref, v_ref, do_ref, tot_ref, dq_ref, dk_ref, dv_ref):
        p, i = pl.program_id(0), pl.program_id(1)

        @pl.when(i == 0)
        def _():
            dk_ref[...] = jnp.zeros_like(dk_ref)
            dv_ref[...] = jnp.zeros_like(dv_ref)

        lane = lax.broadcasted_iota(jnp.int32, (1, LANES), 1)
        lane_hi = lane // HEAD_DIM
        later = _tri(t, lambda r, c: r > c)
        before = _tri(t, lambda r, c: r < c)
        causal = lax.broadcasted_iota(jnp.int32, (t, t), 1) < lax.broadcasted_iota(jnp.int32, (t, t), 0)
        qms, doms, tots = [], [], []
        for x in range(2):
            rows = pl.ds(x * t, t)
            q, do, tot_all = q_ref[rows, :] * scale, do_ref[rows, :], tot_ref[rows, :]
            qms.append([jnp.where(lane_hi == hh, q, jnp.zeros_like(q)) for hh in range(2)])
            doms.append([jnp.where(lane_hi == hh, do, jnp.zeros_like(do)) for hh in range(2)])
            tots.append([jnp.sum(jnp.where(lane == hh * HEAD_DIM, tot_all, 0.0), axis=1, keepdims=True)
                         for hh in range(2)])

        def step(groups, carry):
            offs = [pl.multiple_of(j * t, t) for _, j, _ in groups]
            ks = [k_ref[pl.ds(off, t), :] for off in offs]
            vs = [v_ref[pl.ds(off, t), :] for off in offs]
            chains = [(g, x, hh) for g, (x, _, _) in enumerate(groups) for hh in range(2)]
            heads = [_sb_chain_head(qms[x][hh], ks[g], causal if groups[g][2] else None) for g, x, hh in chains]
            sufs = [_dot(sp.astype(BF16), later) for sp, _ in heads]
            das = [_dot_nt(doms[x][hh], vs[g]) for g, x, hh in chains]
            cur = [list(carry[0]), list(carry[1])]
            sigs, gs, abs_, cg_before = [], [], [], []
            for (g_, x, hh), (sp, lpos), suf, da in zip(chains, heads, sufs, das):
                cl, cg, dq = cur[x][hh]
                cl = cl + jnp.sum(sp, axis=1, keepdims=True)
                sig = jnp.exp(lpos)
                a = sig * jnp.exp(-suf - (tots[x][hh] - cl))
                if groups[g_][2]:
                    a = jnp.where(causal, a, 0.0)
                g = a * da
                sigs.append(sig)
                gs.append(g)
                abs_.append(a.astype(BF16))
                cg_before.append(cg)
                cur[x][hh] = (cl, cg + jnp.sum(g, axis=1, keepdims=True), dq)
            prefs = [_dot(g.astype(BF16), before) for g in gs]
            dvs = [_dot_tn(ab, doms[x][hh]) for (_, x, hh), ab in zip(chains, abs_)]
            dzs = []
            for (g_, x, hh), sig, g, pref, cg in zip(chains, sigs, gs, prefs, cg_before):
                dz = g - sig * (g + pref + cg)
                if groups[g_][2]:
                    dz = jnp.where(causal, dz, 0.0)
                dzs.append(dz.astype(BF16))
            dqs = [_dot(dz, ks[g_]) for (g_, x, hh), dz in zip(chains, dzs)]
            dks = [_dot_tn(dz, qms[x][hh]) for (_, x, hh), dz in zip(chains, dzs)]
            for n, (_, x, hh) in enumerate(chains):
                cl, cg, dq = cur[x][hh]
                cur[x][hh] = (cl, cg, dq + dqs[n])
            for g_, off in enumerate(offs):
                dk_ref[pl.ds(off, t), :] += dks[2 * g_] + dks[2 * g_ + 1]
                dv_ref[pl.ds(off, t), :] += dvs[2 * g_] + dvs[2 * g_ + 1]
            return (tuple(cur[0]), tuple(cur[1]))

        taken = steps_ref[p, i]
        n_full = jnp.minimum(taken, 2 * i)
        zero = (jnp.zeros((t, 1), F32), jnp.zeros((t, 1), F32), jnp.zeros((t, LANES), F32))
        carry = ((zero, zero), (zero, zero))
        carry = lax.cond(jnp.logical_and(i > 0, taken > 2 * i), lambda ca: step([(1, 0, False)], ca), lambda ca: ca,
                         carry)
        carry = lax.fori_loop(
            0, n_full - 1,
            lambda n, ca: step([(0, 2 * i - n_full + n, False), (1, 2 * i + 1 - n_full + n, False)], ca), carry)
        carry = lax.cond(
            i == 0,
            lambda ca: step([(1, 0, False), (0, 0, True), (1, 1, True)], ca),
            lambda ca: step([(0, 2 * i - 1, False), (1, 2 * i, False), (0, 2 * i, True), (1, 2 * i + 1, True)], ca),
            carry)
        for x in range(2):
            dq = jnp.where(lane_hi == 0, carry[x][0][2], carry[x][1][2])
            dq_ref[pl.ds(x * t, t), :] = (dq * scale).astype(dq_ref.dtype)

    row_spec = pl.BlockSpec((2 * t, LANES), lambda p, i, ns: (i, p))
    full_spec = pl.BlockSpec((s, LANES), lambda p, i, ns: (0, p))
    return _pcall(
        body,
        name="sb_bwd",
        grid_spec=pltpu.PrefetchScalarGridSpec(
            num_scalar_prefetch=1,
            grid=(n_pairs, nq),
            in_specs=[
                pl.BlockSpec((2 * t, LANES), lambda p, i, ns: (i, _SB_Q0 + p)),
                pl.BlockSpec((s, LANES), lambda p, i, ns: (0, _SB_K0 + p)),
                pl.BlockSpec((s, LANES), lambda p, i, ns: (0, _SB_V0 + p)),
                row_spec, row_spec,
            ],
            out_specs=[row_spec, full_spec, full_spec],
        ),
        out_shape=[jax.ShapeDtypeStruct((s, SB_WIDTH), BF16), jax.ShapeDtypeStruct((s, SB_WIDTH), F32),
                   jax.ShapeDtypeStruct((s, SB_WIDTH), F32)],
        compiler_params=_cparams(2),
    )(n_steps, qkv, qkv, qkv, do_b, tot_b)


def _gates(gl, bg):
    return _sigmoid(gl[:, :D_MODEL] + bg[:, :D_MODEL]), _sigmoid(gl[:, D_MODEL:] + bg[:, D_MODEL:])


def _mixer_fwd(o_a, o_b, gl, x0, bg, g2, w_ud, w_us, w_out, tm):
    def epi(_, rows, consts):
        oa, ob, glv, x = rows
        bgv, g2v, wud, wus, wout = consts
        ga, gb = _gates(glv, bgv)
        merged = ga * _dot(oa, wud) + gb * _dot(ob, wus)
        x1 = x + _dot(merged.astype(BF16), wout)
        r, xh = _rms_stats(x1)
        return [x1, xh * g2v], []

    return _rowk("mixer_fwd", tm=tm, rows=[o_a, o_b, gl, x0], consts=[bg, g2, w_ud, w_us, w_out],
                 row_outs=[(D_MODEL, F32), (D_MODEL, BF16)], epilogue=epi)


def _mixer_bwd(dx1, o_a, o_b, gl, bg, w_ud, w_us, w_out, tm, rider=None):
    s = dx1.shape[0]
    nm = s // tm

    def body(dx_ref, oa_ref, ob_ref, gl_ref, bg_ref, wud_ref, wus_ref, wout_ref,
             doa_ref, dob_ref, dgl_ref, gwout_ref, gwud_ref, gwus_ref, gbg_ref, awout_ref, awud_ref, awus_ref):
        i = pl.program_id(0)
        dxb = dx_ref[...].astype(BF16)
        oa, ob = oa_ref[...], ob_ref[...]
        ga, gb = _gates(gl_ref[...], bg_ref[...])
        ua, ub = _dot(oa, wud_ref[...]), _dot(ob, wus_ref[...])
        merged = (ga * ua + gb * ub).astype(BF16)
        dm = _dot_nt(dxb, wout_ref[...])
        dua = (dm * ga).astype(BF16)
        dub = (dm * gb).astype(BF16)
        dgla = dm * ua * ga * (1.0 - ga)
        dglb = dm * ub * gb * (1.0 - gb)
        doa_ref[...] = _dot_nt(dua, wud_ref[...]).astype(doa_ref.dtype)
        dob_ref[...] = _dot_nt(dub, wus_ref[...]).astype(dob_ref.dtype)
        dgl_ref[:, :D_MODEL] = dgla.astype(dgl_ref.dtype)
        dgl_ref[:, D_MODEL:] = dglb.astype(dgl_ref.dtype)
        parts = [(gwout_ref, awout_ref, _dot_tn(merged, dxb)), (gwud_ref, awud_ref, _dot_tn(oa, dua)),
                 (gwus_ref, awus_ref, _dot_tn(ob, dub))]
        for out, r, v in parts:

            @pl.when(i == 0)
            def _(r=r, v=v):
                r[...] = v

            @pl.when(i > 0)
            def _(r=r, v=v):
                r[...] += v

            @pl.when(i == nm - 1)
            def _(out=out, r=r):
                if len(out.shape) == 2:
                    out[...] = r[...].astype(out.dtype)
                else:
                    for p in range(N_DEV):
                        out[p] = r[:, p * out.shape[2]:(p + 1) * out.shape[2]].astype(out.dtype)

        sa = jnp.sum(dgla, axis=0, keepdims=True)
        sb = jnp.sum(dglb, axis=0, keepdims=True)

        @pl.when(i == 0)
        def _():
            gbg_ref[:, :D_MODEL] = sa
            gbg_ref[:, D_MODEL:] = sb

        @pl.when(i > 0)
        def _():
            gbg_ref[:, :D_MODEL] += sa
            gbg_ref[:, D_MODEL:] += sb

    row = lambda w: pl.BlockSpec((tm, w), lambda i: (i, 0))
    full = lambda a: pl.BlockSpec(a.shape, lambda i: (0, 0), pipeline_mode=pl.Buffered(1))
    chunks = lambda r: (N_DEV, r, D_MODEL // N_DEV)
    return _call(
        body, (dx1, o_a, o_b, gl, bg, w_ud, w_us, w_out), rider,
        name="mixer_bwd",
        grid=(nm,),
        in_specs=[row(D_MODEL), row(DIL_OUT_WIDTH), row(SB_WIDTH), row(2 * D_MODEL),
                  full(bg), full(w_ud), full(w_us), full(w_out)],
        out_specs=[row(DIL_OUT_WIDTH), row(SB_WIDTH), row(2 * D_MODEL),
                   pl.BlockSpec((D_MODEL, D_MODEL), lambda i: (0, 0)),
                   pl.BlockSpec(chunks(DIL_OUT_WIDTH), lambda i: (0, 0, 0)),
                   pl.BlockSpec(chunks(SB_WIDTH), lambda i: (0, 0, 0)),
                   pl.BlockSpec((1, 2 * D_MODEL), lambda i: (0, 0))],
        out_shape=[jax.ShapeDtypeStruct((s, DIL_OUT_WIDTH), BF16), jax.ShapeDtypeStruct((s, SB_WIDTH), BF16),
                   jax.ShapeDtypeStruct((s, 2 * D_MODEL), BF16),
                   jax.ShapeDtypeStruct((D_MODEL, D_MODEL), BF16), jax.ShapeDtypeStruct(chunks(DIL_OUT_WIDTH), BF16),
                   jax.ShapeDtypeStruct(chunks(SB_WIDTH), BF16), jax.ShapeDtypeStruct((1, 2 * D_MODEL), F32)],
        scratch_shapes=[pltpu.VMEM((D_MODEL, D_MODEL), F32), pltpu.VMEM((DIL_OUT_WIDTH, D_MODEL), F32),
                        pltpu.VMEM((SB_WIDTH, D_MODEL), F32)],
        compiler_params=_cparams(1),
    )


def _all_gather(shards):
    n = len(shards)

    def body(*refs):
        x_refs, out_refs = refs[:n], refs[n:2 * n]
        send_sems, recv_sems, local_sems = refs[2 * n:]
        x, y, c = lax.axis_index("x"), lax.axis_index("y"), lax.axis_index("c")
        me, sibling = (x, y, c), (x, y, 1 - c)
        chips = [(1 - x, y), (x, 1 - y), (1 - x, 1 - y)]

        def slot(a, px, py, pc):
            return out_refs[a].at[4 * px + 2 * py + pc]

        def copy(a, k, block, to, own=False):
            return pltpu.make_async_remote_copy(
                src_ref=x_refs[a] if own else slot(a, *block), dst_ref=slot(a, *block),
                send_sem=send_sems.at[7 * a + k], recv_sem=recv_sems.at[7 * a + k], device_id=to, device_id_type=_MESH)

        mine = [pltpu.make_async_copy(x_refs[a], slot(a, *me), local_sems.at[a]) for a in range(n)]
        for cp in mine:
            cp.start()
        first = []
        for a in range(n):
            first.append(copy(a, 0, me, sibling, own=True))
            first += [copy(a, 1 + j, me, (*chip, c), own=True) for j, chip in enumerate(chips)]
        for cp in first:
            cp.start()
        passed = []
        for a in range(n):
            for j, chip in enumerate(chips):
                copy(a, 1 + j, (*chip, c), me).wait_recv()
                passed.append(copy(a, 4 + j, (*chip, c), sibling))
                passed[-1].start()
        for a in range(n):
            copy(a, 0, sibling, me).wait_recv()
            for j, chip in enumerate(chips):
                copy(a, 4 + j, (*chip, 1 - c), me).wait_recv()
        for cp in first + passed:
            cp.wait_send()
        for cp in mine:
            cp.wait()

    return _pcall(
        body,
        name="all_gather_weights",
        in_specs=[_HBM] * n,
        out_specs=[_HBM] * n,
        out_shape=[jax.ShapeDtypeStruct((N_DEV,) + s.shape, s.dtype) for s in shards],
        scratch_shapes=[pltpu.SemaphoreType.DMA((7 * n,)), pltpu.SemaphoreType.DMA((7 * n,)),
                        pltpu.SemaphoreType.DMA((n,))],
    )(*shards)


def _exchange(chunks):
    n = len(chunks)

    def body(*refs):
        g_refs, o_refs = refs[:n], refs[n:2 * n]
        send_sems, recv_sems, local_sems = refs[2 * n:]
        x, y, c = lax.axis_index("x"), lax.axis_index("y"), lax.axis_index("c")
        me = 4 * x + 2 * y + c
        own = [pltpu.make_async_copy(g_refs[a].at[me], o_refs[a].at[me], local_sems.at[a]) for a in range(n)]
        for cp in own:
            cp.start()
        copies = []
        for a in range(n):
            for k in range(1, N_DEV):
                px, py, pc = x ^ (k >> 2), y ^ ((k >> 1) & 1), c ^ (k & 1)
                peer = 4 * px + 2 * py + pc
                copies.append(pltpu.make_async_remote_copy(
                    src_ref=g_refs[a].at[peer], dst_ref=o_refs[a].at[me], send_sem=send_sems.at[7 * a + k - 1],
                    recv_sem=recv_sems.at[7 * a + k - 1], device_id=(px, py, pc), device_id_type=_MESH))
        for cp in copies:
            cp.start()
        for cp in copies:
            cp.wait()
        for cp in own:
            cp.wait()

    return _pcall(
        body,
        name="exchange_grads",
        in_specs=[_HBM] * n,
        out_specs=[_HBM] * n,
        out_shape=[jax.ShapeDtypeStruct(g.shape, g.dtype) for g in chunks],
        scratch_shapes=[pltpu.SemaphoreType.DMA((7 * n,)), pltpu.SemaphoreType.DMA((7 * n,)),
                        pltpu.SemaphoreType.DMA((n,))],
    )(*chunks)


def _reduce_adamw(name, parts, w, m, v, tr):
    _, rows, cols = parts.shape
    tr = min(tr, rows)
    assert rows % tr == 0
    c1 = 1.0 / (1.0 - ADAM_B1 ** ADAM_STEP)
    c2 = 1.0 / (1.0 - ADAM_B2 ** ADAM_STEP)

    def body(p_ref, w_ref, m_ref, v_ref, g_out, d_out, m_out, v_out):
        g = p_ref[0].astype(F32)
        for d in range(1, N_DEV):
            g = g + p_ref[d].astype(F32)
        mn = ADAM_B1 * m_ref[...] + (1.0 - ADAM_B1) * g
        vn = ADAM_B2 * v_ref[...] + (1.0 - ADAM_B2) * (g * g)
        g_out[...] = g
        m_out[...] = mn
        v_out[...] = vn
        d_out[...] = -ADAM_LR * ((mn * c1) / (jnp.sqrt(vn * c2) + ADAM_EPS) + ADAM_WD * w_ref[...])

    spec = pl.BlockSpec((tr, cols), lambda i: (i, 0))
    return _pcall(
        body,
        name=name,
        grid=(rows // tr,),
        in_specs=[pl.BlockSpec((N_DEV, tr, cols), lambda i: (0, i, 0)), spec, spec, spec],
        out_specs=[spec] * 4,
        out_shape=[jax.ShapeDtypeStruct((rows, cols), F32)] * 4,
        compiler_params=_cparams(1),
    )(parts, w, m, v)


_SHARDED = ("w_in", "w_up_dil", "w_up_sb", "w_out", "w_mlp_in", "w_mlp_out")
_FULL_SHAPES = {"w_in": (D_MODEL, IN_COLS), "w_up_dil": (DIL_OUT_WIDTH, D_MODEL), "w_up_sb": (SB_WIDTH, D_MODEL),
                "w_out": (D_MODEL, D_MODEL), "w_mlp_in": (D_MODEL, D_FF), "w_mlp_out": (D_FF, D_MODEL)}
_ROW_SHARDED = ("w_out", "w_mlp_out")


def _shard_shape(name):
    r, c = _FULL_SHAPES[name]
    return (r // N_DEV, c) if name in _ROW_SHARDED else (r, c // N_DEV)


def _assemble(name, gathered):
    r, c = _shard_shape(name)
    if name in _ROW_SHARDED:
        return gathered.reshape(N_DEV * r, c)
    return gathered.transpose(1, 0, 2).reshape(r, N_DEV * c)


def _chunk(name, full):
    r, c = _shard_shape(name)
    if name in _ROW_SHARDED:
        return full.reshape(N_DEV, r, c)
    return full.reshape(r, N_DEV, c).transpose(1, 0, 2)


_SMALL = (("norm_mix_g", D_MODEL), ("b_gate", 2 * D_MODEL), ("norm_mlp_g", D_MODEL), ("norm_final_g", D_MODEL))
_SMALL_N = sum(n for _, n in _SMALL) + LANES


def _pack_small(vals, tail):
    return jnp.concatenate([vals[n].reshape(1, -1) for n, _ in _SMALL] + [tail], axis=1)


def _unpack_small(vec, shapes):
    out, pos = {}, 0
    for n, width in _SMALL:
        out[n] = vec[:, pos:pos + width].reshape(shapes[n])
        pos += width
    return out, vec[:, pos:]


def kernel(x, norm_mix_g, w_in, b_gate, w_up_dil, w_up_sb, w_out, norm_mlp_g, w_mlp_in, w_mlp_out, norm_final_g, loss_target, m_norm_mix_g, m_w_in, m_b_gate, m_w_up_dil, m_w_up_sb, m_w_out, m_norm_mlp_g, m_w_mlp_in, m_w_mlp_out, m_norm_final_g, v_norm_mix_g, v_w_in, v_b_gate, v_w_up_dil, v_w_up_sb, v_w_out, v_norm_mlp_g, v_w_mlp_in, v_w_mlp_out, v_norm_final_g):
    given = dict(locals())
    s = x.shape[1]
    x0 = x.reshape(s, D_MODEL)
    target = loss_target.reshape(s, D_MODEL)
    g1 = norm_mix_g.reshape(1, D_MODEL)
    g2 = norm_mlp_g.reshape(1, D_MODEL)
    g3 = norm_final_g.reshape(1, D_MODEL)
    bg = b_gate.reshape(1, 2 * D_MODEL)
    w_shards = {n: given[n].reshape(_shard_shape(n)) for n in _SHARDED}
    m_shards = {n: given["m_" + n].reshape(_shard_shape(n)) for n in _SHARDED}
    v_shards = {n: given["v_" + n].reshape(_shard_shape(n)) for n in _SHARDED}

    shard_b = {n: w_shards[n].astype(BF16) for n in _SHARDED}
    (gathered_w_in,) = _all_gather([shard_b["w_in"]])
    w_in_f = _assemble("w_in", gathered_w_in)
    w_qkv, w_gl = _group_major(w_in_f[:, :QKV_COLS]), w_in_f[:, QKV_COLS:]
    full = {}

    def norm1(_, rows, consts):
        _, xh = _rms_stats(rows[0])
        return [xh * consts[0]], []

    (h1,) = _rowk("norm_mix", tm=1024, rows=[x0], consts=[g1], row_outs=[(D_MODEL, BF16)], epilogue=norm1)
    qkv, (land,) = _mm("proj_qkv", h1, w_qkv, out_dtype=BF16, tm=1024, tn=768, tk=D_MODEL,
                       rider=_Spread([shard_b["w_mlp_in"]], chunked=False))
    full["w_mlp_in"] = _assemble("w_mlp_in", land)
    gl = _mm("proj_gates", h1, w_gl, out_dtype=BF16, tm=1024, tn=1024, tk=D_MODEL)
    views = [_dil_view(qkv, g) for g in range(len(DIL_GROUPS))]
    dil = [_dil_fwd(views[g], g) for g in range(len(DIL_GROUPS))]
    os_, lses = [d[0] for d in dil], [d[1] for d in dil]
    o_a = _dil_mix_fwd(os_, lses)
    riding = ("w_mlp_out", "w_out", "w_up_sb", "w_up_dil")
    (o_b, tot_b, sb_steps), lands = _sb_fwd(qkv, rider=_Spread([shard_b[n] for n in riding], chunked=False))
    full.update({n: _assemble(n, land) for n, land in zip(riding, lands)})
    x1, h2 = _mixer_fwd(o_a, o_b, gl, x0, bg, g2, full["w_up_dil"], full["w_up_sb"], full["w_out"], 512)
    f = _mm("mlp_in", h2, full["w_mlp_in"], out_dtype=BF16, tm=1024, tn=1024, tk=D_MODEL,
            epilogue=lambda r, _: jnp.square(jnp.maximum(r, 0.0)))

    def head(acc, rows, consts):
        x1v, tv = rows
        g3v = consts[0]
        x2 = x1v + acc
        r, xh = _rms_stats(x2)
        diff = xh * g3v - tv
        loss = (0.5 / D_MODEL) * jnp.sum(jnp.sum(diff * diff, axis=0, keepdims=True), axis=1, keepdims=True)
        dy = diff * (1.0 / D_MODEL)
        dx2, dg = _rms_bwd(dy, xh, r, g3v)
        return [dx2, dx2], [dg, jnp.broadcast_to(loss, (1, LANES))]

    dx2, dx2b, gg3, loss_part = _rowk(
        "mlp_out_loss", a=f, w=full["w_mlp_out"], tm=512, tk=D_FF, rows=[x1, target], consts=[g3],
        row_outs=[(D_MODEL, F32), (D_MODEL, BF16)], acc_outs=[D_MODEL, LANES], epilogue=head)

    da = _mm("mlp_out_bwd", dx2b, full["w_mlp_out"], tb=True, out_dtype=BF16, tm=1024, tn=1024, tk=D_MODEL, extra=f,
             epilogue=lambda r, fv: r * (2.0 * jnp.sqrt(fv.astype(F32))))
    g_w_mlp_out = _mm("grad_w_mlp_out", f, dx2b, ta=True, out_dtype=BF16, tm=1024, tn=1024, tk=2048)
    g_w_mlp_in = _mm("grad_w_mlp_in", h2, da, ta=True, out_dtype=BF16, tm=1024, tn=1024, tk=2048, col_chunks=N_DEV)

    def norm_bwd(acc, rows, consts):
        xv, dres = rows
        r, xh = _rms_stats(xv)
        dx, dg = _rms_bwd(acc, xh, r, consts[0])
        return [dres + dx], [dg]

    parts = {}
    (dx1, gg2), (parts["w_mlp_in"],) = _rowk(
        "mlp_in_bwd", a=da, w=full["w_mlp_in"], nt=True, tm=512, tk=D_FF, rows=[x1, dx2], consts=[g2],
        row_outs=[(D_MODEL, F32)], acc_outs=[D_MODEL], epilogue=norm_bwd,
        rider=_Spread([g_w_mlp_in], chunked=True))
    (do_a, do_b, dgl, g_w_out, g_w_ud, g_w_us, g_bg), (parts["w_mlp_out"],) = _mixer_bwd(
        dx1, o_a, o_b, gl, bg, full["w_up_dil"], full["w_up_sb"], full["w_out"], 512,
        rider=_Spread([_chunk("w_mlp_out", g_w_mlp_out)], chunked=True))
    mix = _dil_mix_bwd(do_a, os_, lses)
    small_three = {"w_out": _chunk("w_out", g_w_out), "w_up_sb": g_w_us, "w_up_dil": g_w_ud}
    grads, lands = _dil_bwd(views[0], mix[0], lses[0], mix[3], 0,
                            rider=_Spread(list(small_three.values()), chunked=True))
    parts.update(dict(zip(small_three, lands)))
    dil_b = [grads] + [_dil_bwd(views[g], mix[g], lses[g], mix[3 + g], g) for g in (1, 2)]
    dq_b, dk_b, dv_b = _sb_bwd(qkv, do_b, tot_b, sb_steps)
    dproj = [d[0] for d in dil_b] + [d[1] for d in dil_b] + [d[2] for d in dil_b] + [dq_b, dk_b, dv_b, dgl]
    g_w_in = jnp.concatenate([
        _grad_cols("grad_w_in_dil", h1, dproj[:9], tm=D_MODEL, tk=1024),
        _grad_cols("grad_w_in_sb", h1, dproj[9:12], tm=D_MODEL, tk=1024),
        _grad_cols("grad_w_in_gates", h1, dproj[12:], tm=D_MODEL, tk=1024)], axis=1)
    (grad_x, gg1), (parts["w_in"],) = _rowk(
        "in_proj_bwd", a=dproj, w=w_in_f, nt=True, tm=512, tk=IN_COLS, rows=[x0, dx1], consts=[g1],
        row_outs=[(D_MODEL, F32)], acc_outs=[D_MODEL], epilogue=norm_bwd,
        rider=_Spread([_chunk("w_in", g_w_in)], chunked=True))

    small_part = _pack_small({"norm_mix_g": gg1, "b_gate": g_bg, "norm_mlp_g": gg2, "norm_final_g": gg3}, loss_part)
    (small_parts,) = _exchange([jnp.broadcast_to(small_part[None], (N_DEV, 1, _SMALL_N))])

    tags = ("grad_", "delta_", "new_m_", "new_v_")
    outs = {}
    for n, p in parts.items():
        res = _reduce_adamw("adamw_" + n, p, w_shards[n], m_shards[n], v_shards[n], 256)
        for tag, val in zip(tags, res):
            outs[tag + n] = val.reshape(given[n].shape)
    small_w = _pack_small(given, jnp.zeros((1, LANES), F32))
    small_m = _pack_small({n: given["m_" + n] for n, _ in _SMALL}, jnp.zeros((1, LANES), F32))
    small_v = _pack_small({n: given["v_" + n] for n, _ in _SMALL}, jnp.ones((1, LANES), F32))
    small_res = _reduce_adamw("adamw_replicated", small_parts, small_w, small_m, small_v, 8)

    small_shapes = {n: given[n].shape for n, _ in _SMALL}
    for tag, small in zip(tags, small_res):
        small_vals, tail = _unpack_small(small, small_shapes)
        for n, val in small_vals.items():
            outs[tag + n] = val
        if tag == "grad_":
            loss = tail[0, 0]
    names = ["norm_mix_g", "w_in", "b_gate", "w_up_dil", "w_up_sb", "w_out", "norm_mlp_g", "w_mlp_in", "w_mlp_out",
             "norm_final_g"]
    return (loss, grad_x.reshape(x.shape), *[outs["grad_" + n] for n in names], *[outs["delta_" + n] for n in names],
            *[outs["new_m_" + n] for n in names], *[outs["new_v_" + n] for n in names])
```

```python
import functools
import math

import jax
import jax.numpy as jnp
from jax import lax
from jax.experimental import pallas as pl
from jax.experimental.pallas import tpu as pltpu

_pcall = pl.pallas_call

F32 = jnp.float32
BF16 = jnp.bfloat16

D_MODEL = 1024
HEAD_DIM = 64
DIL_GROUPS = ((128, 1), (512, 4), (2048, 16))
DIL_HEADS_PER_GROUP = 4
N_DIL_HEADS = 12
N_SB_HEADS = 8
DIL_WIDTH = 768
DIL_OUT_WIDTH = 256
SB_WIDTH = 512
D_FF = 4096
BLOCK = 128
RMS_EPS = 1e-6
NEG_INF = -1e30
QKV_COLS = 3 * DIL_WIDTH + 3 * SB_WIDTH
IN_COLS = QKV_COLS + 2 * D_MODEL
N_DEV = 8

ADAM_LR = 0.001
ADAM_B1 = 0.9
ADAM_B2 = 0.999
ADAM_EPS = 1e-08
ADAM_WD = 0.01
ADAM_STEP = 10

VMEM_LIMIT = 56 * 1024 * 1024
SB_TK = 256
LANES = 128

_ARB = pltpu.ARBITRARY


def _cparams(n_axes, **kw):
    return pltpu.CompilerParams(dimension_semantics=(_ARB,) * n_axes, vmem_limit_bytes=VMEM_LIMIT, **kw)


def _dot(a, b):
    return jnp.dot(a, b, preferred_element_type=F32)


def _dot_nt(a, b):
    return lax.dot_general(a, b, (((1,), (1,)), ((), ())), preferred_element_type=F32)


def _dot_tn(a, b):
    return lax.dot_general(a, b, (((0,), (0,)), ((), ())), preferred_element_type=F32)


def _split_hi_lo(x):
    hi = x.astype(BF16)
    lo = (x - hi.astype(F32)).astype(BF16)
    return hi, lo


def _dot_hi_lo(x, m):
    hi, lo = _split_hi_lo(x)
    return _dot(hi, m) + _dot(lo, m)


def _sigmoid(x):
    return 1.0 / (1.0 + jnp.exp(-x))


_HBM = pl.BlockSpec(memory_space=pltpu.HBM)
_MESH = pl.DeviceIdType.MESH


class _Spread:
    def __init__(self, srcs, chunked):
        self.srcs, self.chunked, self.n = list(srcs), chunked, len(srcs)

    def land_shapes(self):
        return [jax.ShapeDtypeStruct((N_DEV,) + (s.shape[1:] if self.chunked else s.shape), s.dtype) for s in self.srcs]

    def scratch(self):
        dma = pltpu.SemaphoreType.DMA
        return [dma((7 * self.n,)), dma((7 * self.n,)), dma((self.n,))]

    def copies(self, src_refs, land_refs, send_sems, recv_sems, local_sems):
        x, y, c = lax.axis_index("x"), lax.axis_index("y"), lax.axis_index("c")
        me = 4 * x + 2 * y + c
        out = []
        for a, (src, land) in enumerate(zip(src_refs, land_refs)):
            out.append(pltpu.make_async_copy(src.at[me] if self.chunked else src, land.at[me], local_sems.at[a]))
            for k in range(1, N_DEV):
                px, py, pc = x ^ (k >> 2), y ^ ((k >> 1) & 1), c ^ (k & 1)
                out.append(pltpu.make_async_remote_copy(
                    src_ref=src.at[4 * px + 2 * py + pc] if self.chunked else src, dst_ref=land.at[me],
                    send_sem=send_sems.at[7 * a + k - 1], recv_sem=recv_sems.at[7 * a + k - 1],
                    device_id=(px, py, pc), device_id_type=_MESH))
        return out


def _call(body, args, rider=None, **kw):
    if rider is None:
        return _pcall(body, **kw)(*args)
    grid = kw["grid"]
    single = not isinstance(kw["out_shape"], (list, tuple))
    out_specs = [kw["out_specs"]] if single else list(kw["out_specs"])
    out_shape = [kw["out_shape"]] if single else list(kw["out_shape"])
    in_specs, scratch = list(kw["in_specs"]), list(kw.get("scratch_shapes", []))
    n_in, n_out, n_s, n = len(in_specs), len(out_shape), len(scratch), rider.n

    def hosted(*refs):
        ins, srcs = refs[:n_in], refs[n_in:n_in + n]
        outs, lands = refs[n_in + n:n_in + n + n_out], refs[n_in + n + n_out:n_in + 2 * n + n_out]
        own_scratch, sems = refs[n_in + 2 * n + n_out:n_in + 2 * n + n_out + n_s], refs[n_in + 2 * n + n_out + n_s:]
        ids = [pl.program_id(d) for d in range(len(grid))]
        first = functools.reduce(jnp.logical_and, [i == 0 for i in ids])
        last = functools.reduce(jnp.logical_and, [i == g - 1 for i, g in zip(ids, grid)])
        copies = rider.copies(srcs, lands, *sems)

        @pl.when(first)
        def _():
            for cp in copies:
                cp.start()

        body(*ins, *outs, *own_scratch)

        @pl.when(last)
        def _():
            for cp in copies:
                cp.wait()

    kw = dict(kw, in_specs=in_specs + [_HBM] * n, out_specs=out_specs + [_HBM] * n,
              out_shape=out_shape + rider.land_shapes(), scratch_shapes=scratch + rider.scratch())
    res = _pcall(hosted, **kw)(*args, *rider.srcs)
    return (res[0] if single else list(res[:n_out])), list(res[n_out:])


def _mm(name, a, b, *, ta=False, tb=False, out_dtype, tm, tn, tk, epilogue=None, extra=None, rider=None,
        col_chunks=None):
    m = a.shape[1] if ta else a.shape[0]
    k = a.shape[0] if ta else a.shape[1]
    n = b.shape[0] if tb else b.shape[1]
    assert (b.shape[1] if tb else b.shape[0]) == k
    tm, tn, tk = min(tm, m), min(tn, n), min(tk, k)
    assert m % tm == 0 and n % tn == 0 and k % tk == 0, (name, m, n, k, tm, tn, tk)
    nk = k // tk
    dn = (((0 if ta else 1,), (1 if tb else 0,)), ((), ()))
    in_place = nk > 1 and epilogue is None and out_dtype == F32 and col_chunks is None
    cw = n // col_chunks if col_chunks else None
    assert cw is None or (tn % cw == 0 and cw % LANES == 0)

    def body(*refs):
        if extra is not None:
            a_ref, b_ref, e_ref, o_ref = refs[:4]
        else:
            a_ref, b_ref, o_ref = refs[:3]
            e_ref = None

        def finish(r):
            if epilogue is not None:
                r = epilogue(r, None if e_ref is None else e_ref[...])
            if cw is None:
                o_ref[...] = r.astype(out_dtype)
            else:
                for c in range(tn // cw):
                    o_ref[c] = r[:, c * cw:(c + 1) * cw].astype(out_dtype)

        part = lax.dot_general(a_ref[...].astype(BF16), b_ref[...].astype(BF16), dn, preferred_element_type=F32)
        if nk == 1:
            finish(part)
        else:
            acc_ref = o_ref if in_place else refs[-1]
            kk = pl.program_id(2)

            @pl.when(kk == 0)
            def _():
                acc_ref[...] = part

            @pl.when(kk > 0)
            def _():
                acc_ref[...] += part

            if not in_place:

                @pl.when(kk == nk - 1)
                def _():
                    finish(acc_ref[...])

    a_spec = pl.BlockSpec((tk, tm), lambda j, i, kk: (kk, i)) if ta else pl.BlockSpec((tm, tk), lambda j, i, kk: (i, kk))
    b_spec = pl.BlockSpec((tn, tk), lambda j, i, kk: (j, kk)) if tb else pl.BlockSpec((tk, tn), lambda j, i, kk: (kk, j))
    o_spec = pl.BlockSpec((tm, tn), lambda j, i, kk: (i, j))
    in_specs = [a_spec, b_spec]
    args = [a, b]
    if extra is not None:
        in_specs.append(o_spec)
        args.append(extra)
    out_shape = jax.ShapeDtypeStruct((m, n), out_dtype)
    if cw is not None:
        o_spec = pl.BlockSpec((tn // cw, tm, cw), lambda j, i, kk: (j, i, 0))
        out_shape = jax.ShapeDtypeStruct((col_chunks, m, cw), out_dtype)
    return _call(
        body, args, rider,
        name=name,
        grid=(n // tn, m // tm, nk),
        in_specs=in_specs,
        out_specs=o_spec,
        out_shape=out_shape,
        scratch_shapes=[pltpu.VMEM((tm, tn), F32)] if (nk > 1 and not in_place) else [],
        compiler_params=_cparams(3),
    )


def _grad_cols(name, a, parts, *, tm, tk, rider=None):
    k, m = a.shape
    n = sum(p.shape[1] for p in parts)
    assert m % tm == 0 and k % tk == 0
    nk = k // tk

    def body(*refs):
        a_ref, p_refs, o_ref, acc_ref = refs[0], refs[1:1 + len(parts)], refs[1 + len(parts)], refs[2 + len(parts)]
        kk = pl.program_id(1)
        side_by_side = jnp.concatenate([p_ref[...].astype(BF16) for p_ref in p_refs], axis=1)
        term = _dot_tn(a_ref[...].astype(BF16), side_by_side)

        @pl.when(kk == 0)
        def _():
            acc_ref[...] = term

        @pl.when(kk > 0)
        def _():
            acc_ref[...] += term

        @pl.when(kk == nk - 1)
        def _():
            o_ref[...] = acc_ref[...].astype(o_ref.dtype)

    return _call(
        body, [a] + list(parts), rider,
        name=name,
        grid=(m // tm, nk),
        in_specs=[pl.BlockSpec((tk, tm), lambda i, kk: (kk, i))]
        + [pl.BlockSpec((tk, p.shape[1]), lambda i, kk: (kk, 0)) for p in parts],
        out_specs=pl.BlockSpec((tm, n), lambda i, kk: (i, 0)),
        out_shape=jax.ShapeDtypeStruct((m, n), BF16),
        scratch_shapes=[pltpu.VMEM((tm, n), F32)],
        compiler_params=_cparams(2),
    )


def _rowk(name, *, a=None, w=None, nt=False, tm, tk=None, rows=(), consts=(), row_outs=(), acc_outs=(), epilogue,
          rider=None):
    has_mm = a is not None
    a_parts = list(a) if isinstance(a, (list, tuple)) else ([a] if has_mm else [])
    n_a = len(a_parts)
    m = a_parts[0].shape[0] if has_mm else rows[0].shape[0]
    assert m % tm == 0
    nm = m // tm
    if has_mm:
        k = sum(p.shape[1] for p in a_parts)
        n = w.shape[0] if nt else w.shape[1]
        tk = min(tk, k)
        assert k % tk == 0 and (n_a == 1 or tk == k)
        nk = k // tk
    else:
        nk = 1
    n_rows, n_consts, n_ro, n_ao = len(rows), len(consts), len(row_outs), len(acc_outs)

    def body(*refs):
        pos = 0
        if has_mm:
            a_refs, w_ref = refs[:n_a], refs[n_a]
            pos = n_a + 1
        row_refs = refs[pos:pos + n_rows]
        pos += n_rows
        const_refs = refs[pos:pos + n_consts]
        pos += n_consts
        ro_refs = refs[pos:pos + n_ro]
        pos += n_ro
        ao_refs = refs[pos:pos + n_ao]
        pos += n_ao
        i = pl.program_id(0)
        kk = pl.program_id(1)

        def finish(acc):
            ro_vals, ao_vals = epilogue(acc, [r[...] for r in row_refs], [c[...] for c in const_refs])
            for r, v in zip(ro_refs, ro_vals):
                r[...] = v.astype(r.dtype)
            for r, v in zip(ao_refs, ao_vals):

                @pl.when(i == 0)
                def _(r=r, v=v):
                    r[...] = v

                @pl.when(i > 0)
                def _(r=r, v=v):
                    r[...] += v

        if not has_mm:
            finish(None)
            return
        part, off = None, 0
        for a_ref in a_refs:
            width = a_ref.shape[1]
            cols = slice(None) if n_a == 1 else slice(off, off + width)
            av = a_ref[...].astype(BF16)
            term = _dot_nt(av, w_ref[:, cols]) if nt else _dot(av, w_ref[cols, :])
            part = term if part is None else part + term
            off += width
        if nk == 1:
            finish(part)
        else:
            acc_ref = refs[pos]

            @pl.when(kk == 0)
            def _():
                acc_ref[...] = part

            @pl.when(kk > 0)
            def _():
                acc_ref[...] += part

            @pl.when(kk == nk - 1)
            def _():
                finish(acc_ref[...])

    once = pl.Buffered(1)
    in_specs, args = [], []
    if has_mm:
        for part in a_parts:
            in_specs.append(pl.BlockSpec((tm, tk if n_a == 1 else part.shape[1]), lambda i, kk: (i, kk)))
        w_mode = once if nk == 1 else None
        in_specs.append(pl.BlockSpec((n, tk), lambda i, kk: (0, kk), pipeline_mode=w_mode) if nt
                        else pl.BlockSpec((tk, n), lambda i, kk: (kk, 0), pipeline_mode=w_mode))
        args += a_parts + [w]
    for r in rows:
        in_specs.append(pl.BlockSpec((tm, r.shape[1]), lambda i, kk: (i, 0)))
        args.append(r)
    for c in consts:
        in_specs.append(pl.BlockSpec(c.shape, lambda i, kk: (0,) * c.ndim, pipeline_mode=once))
        args.append(c)
    out_specs, out_shape = [], []
    for width, dt in row_outs:
        out_specs.append(pl.BlockSpec((tm, width), lambda i, kk: (i, 0)))
        out_shape.append(jax.ShapeDtypeStruct((m, width), dt))
    for width in acc_outs:
        out_specs.append(pl.BlockSpec((1, width), lambda i, kk: (0, 0)))
        out_shape.append(jax.ShapeDtypeStruct((1, width), F32))
    return _call(
        body, args, rider,
        name=name,
        grid=(nm, nk),
        in_specs=in_specs,
        out_specs=out_specs,
        out_shape=out_shape,
        scratch_shapes=[pltpu.VMEM((tm, n), F32)] if (has_mm and nk > 1) else [],
        compiler_params=_cparams(2),
    )


def _rms_stats(x):
    r = lax.rsqrt(jnp.mean(x * x, axis=-1, keepdims=True) + RMS_EPS)
    return r, x * r


def _rms_bwd(dh, xh, r, g):
    gy = dh * g
    dx = r * (gy - xh * jnp.mean(gy * xh, axis=-1, keepdims=True))
    return dx, jnp.sum(dh * xh, axis=0, keepdims=True)


def _alibi_slope(head):
    return 2.0 ** (-8.0 * (head + 1) / N_DIL_HEADS)


DIL_STEP_BLOCKS = 4


def _dil_band(first_block):
    qi = lax.broadcasted_iota(jnp.int32, (BLOCK, 2 * BLOCK), 0)
    kj = lax.broadcasted_iota(jnp.int32, (BLOCK, 2 * BLOCK), 1)
    steps = qi + BLOCK - kj
    valid = (steps >= 0) & (steps <= BLOCK)
    if first_block is not False:
        valid = valid & ((kj >= BLOCK) | jnp.logical_not(first_block))
    return steps.astype(F32), valid


def _dil_step_specs(ncb, cols, nblk, clamp):
    def own(col):
        return pl.BlockSpec((nblk * BLOCK, DIL_OUT_WIDTH), lambda r, i: (clamp(i), r * ncb + col))

    def before(col):
        return pl.BlockSpec((BLOCK, DIL_OUT_WIDTH), lambda r, i: (jnp.maximum(clamp(i) * nblk - 1, 0), r * ncb + col))

    return [own(cols[0]), own(cols[1]), before(cols[1]), own(cols[2]), before(cols[2])]


DIL_RELAYOUT_ROWS = 1024


def _view_scratch(width):
    return pltpu.VMEM((width // LANES, DIL_RELAYOUT_ROWS, LANES), F32)


def _rows_from_view(src, scr, d, w):
    sub = src.shape[0]
    for j in range(w // LANES):
        for r in range(d):
            scr[j, pl.ds(r, sub, stride=d), :] = src[:, r * w + j * LANES:r * w + (j + 1) * LANES].astype(F32)


def _rows_to_view(scr, dst, d, w):
    sub = dst.shape[0]
    for j in range(w // LANES):
        for r in range(d):
            dst[:, r * w + j * LANES:r * w + (j + 1) * LANES] = scr[j, pl.ds(r, sub, stride=d), :].astype(dst.dtype)


RING_SLOTS = 3


def _ring_step(hbm_refs, rings, sem, windows, n_steps):
    i = pl.program_id(0)
    n = len(hbm_refs)

    def fetch(step, k):
        return pltpu.make_async_copy(windows[k](hbm_refs[k], step), rings[k].at[step % RING_SLOTS],
                                     sem.at[k, step % RING_SLOTS])

    for ahead in range(min(RING_SLOTS - 1, n_steps)):

        @pl.when(i == 0)
        def _(ahead=ahead):
            for k in range(n):
                fetch(ahead, k).start()

    @pl.when(i + RING_SLOTS - 1 < n_steps)
    def _():
        for k in range(n):
            fetch(i + RING_SLOTS - 1, k).start()

    for k in range(n):
        fetch(i, k).wait()
    return [ring.at[i % RING_SLOTS] for ring in rings]


def _row_window(rows, cols=None):
    def window(ref, step):
        lines = pl.ds(pl.multiple_of(step * rows, rows), rows)
        return ref.at[lines, :] if cols is None else ref.at[lines, cols]

    return window


def _dil_relayout(name, xs, dilation, to_view, col_block=0, width=None):
    d = dilation
    tm = DIL_RELAYOUT_ROWS
    rows = tm // d
    if to_view:
        s = xs[0].shape[0]
        widths = [width or x.shape[1] for x in xs]
    else:
        s = xs[0].shape[0] * d
        widths = [v.shape[1] // d for v in xs]
    assert s % tm == 0 and all(w % LANES == 0 for w in widths) and all(x.dtype == BF16 for x in xs)
    n = len(xs)
    blk = 256
    per = blk // d
    assert per % 16 == 0 and tm % blk == 0

    n_steps = s // tm
    slots = RING_SLOTS
    windows = [_row_window(tm, slice(col_block * w, (col_block + 1) * w)) if to_view else _row_window(rows)
               for w in widths]

    def body(*refs):
        hbm_refs, out_refs, rings, sem = refs[:n], refs[n:2 * n], refs[2 * n:3 * n], refs[3 * n]
        in_refs = _ring_step(hbm_refs, rings, sem, windows, n_steps)
        i0 = lax.broadcasted_iota(jnp.int32, (blk, blk), 0)
        i1 = lax.broadcasted_iota(jnp.int32, (blk, blk), 1)
        sort = (i1 == (i0 % per) * d + i0 // per) if to_view else (i0 == (i1 % per) * d + i1 // per)
        sort = jnp.where(sort, 1.0, 0.0).astype(BF16)
        for src, dst, w in zip(in_refs, out_refs, widths):
            for b in range(tm // blk):
                if to_view:
                    y = _dot(sort, src[b * blk:(b + 1) * blk, :]).astype(BF16)
                    for r in range(d):
                        dst[b * per:(b + 1) * per, r * w:(r + 1) * w] = y[r * per:(r + 1) * per, :]
                else:
                    by_residue = jnp.concatenate(
                        [src[b * per:(b + 1) * per, r * w:(r + 1) * w] for r in range(d)], axis=0)
                    dst[b * blk:(b + 1) * blk, :] = _dot(sort, by_residue).astype(BF16)

    natural = [pl.BlockSpec((tm, w), lambda i: (i, 0)) for w in widths]
    viewed = [pl.BlockSpec((rows, d * w), lambda i: (i, 0)) for w in widths]
    return _pcall(
        body,
        name=name,
        grid=(n_steps,),
        in_specs=[_HBM] * n,
        out_specs=viewed if to_view else natural,
        out_shape=[jax.ShapeDtypeStruct((s // d, d * w) if to_view else (s, w), BF16) for w in widths],
        scratch_shapes=[pltpu.VMEM((slots, tm, w) if to_view else (slots, rows, d * w), BF16) for w in widths]
        + [pltpu.SemaphoreType.DMA((n, slots))],
        compiler_params=_cparams(1),
    )(*xs)


def _dil_fwd(view, group):
    window, dilation = DIL_GROUPS[group]
    qkv_v, ncb, cols = view
    sub = qkv_v.shape[0]
    s = sub * dilation
    nb = sub // BLOCK
    assert nb * BLOCK * dilation == s and window // dilation == BLOCK
    nblk = min(DIL_STEP_BLOCKS, nb)
    assert nb % nblk == 0
    slopes = [_alibi_slope(group * DIL_HEADS_PER_GROUP + h) * dilation for h in range(DIL_HEADS_PER_GROUP)]

    def body(q_ref, kc_ref, kp_ref, vc_ref, vp_ref, o_ref, lse_ref):
        i = pl.program_id(1)
        kk_all = jnp.concatenate([kp_ref[...], kc_ref[...]], axis=0)
        vv_all = jnp.concatenate([vp_ref[...], vc_ref[...]], axis=0)
        head_id = lax.broadcasted_iota(jnp.int32, (1, DIL_OUT_WIDTH), 1) // HEAD_DIM
        chains = [(b, h) for b in range(nblk) for h in range(DIL_HEADS_PER_GROUP)]
        rows = lambda b: slice(b * BLOCK, (b + 1) * BLOCK)
        keys = lambda b: slice(b * BLOCK, (b + 2) * BLOCK)
        bands = [_dil_band(i == 0 if b == 0 else False) for b in range(nblk)]
        qs = [q_ref[rows(b), :] for b in range(nblk)]
        scores = [_dot_nt(jnp.where(head_id == h, qs[b], jnp.zeros_like(qs[b])), kk_all[keys(b)]) for b, h in chains]
        ps, lses = [], []
        for (b, h), sc in zip(chains, scores):
            steps, valid = bands[b]
            logits = jnp.where(valid, sc * (1.0 / math.sqrt(HEAD_DIM)) - slopes[h] * steps, NEG_INF)
            mx = jnp.max(logits, axis=1, keepdims=True)
            e = jnp.exp(logits - mx)
            den = jnp.sum(e, axis=1, keepdims=True)
            lses.append(mx + jnp.log(den))
            ps.append((e * (1.0 / den)).astype(BF16))
        outs = [_dot(p, vv_all[keys(b)]) for (b, h), p in zip(chains, ps)]
        for b in range(nblk):
            mine = [n for n, ch in enumerate(chains) if ch[0] == b]
            o, lse_all = outs[mine[0]], lses[mine[0]]
            for n in mine[1:]:
                o = jnp.where(head_id == chains[n][1], outs[n], o)
                lse_all = jnp.where(head_id == chains[n][1], lses[n], lse_all)
            o_ref[rows(b), :] = o
            lse_ref[rows(b), :] = jnp.broadcast_to(lse_all, o.shape)

    out_spec = pl.BlockSpec((nblk * BLOCK, DIL_OUT_WIDTH), lambda r, i: (i, r))
    o, lse = _pcall(
        body,
        name=f"dil_fwd_g{group}",
        grid=(dilation, nb // nblk),
        in_specs=_dil_step_specs(ncb, cols, nblk, lambda i: i),
        out_specs=[out_spec, out_spec],
        out_shape=[jax.ShapeDtypeStruct((sub, dilation * DIL_OUT_WIDTH), F32)] * 2,
        compiler_params=_cparams(2),
    )(qkv_v, qkv_v, qkv_v, qkv_v, qkv_v)
    return o, lse


def _dil_bwd(view, do_g, lse_g, dterm_g, group, rider=None):
    window, dilation = DIL_GROUPS[group]
    qkv_v, ncb, cols = view
    sub = qkv_v.shape[0]
    nb = sub // BLOCK
    nblk = min(DIL_STEP_BLOCKS, nb)
    n_steps = nb // nblk
    slopes = [_alibi_slope(group * DIL_HEADS_PER_GROUP + h) * dilation for h in range(DIL_HEADS_PER_GROUP)]
    scale = 1.0 / math.sqrt(HEAD_DIM)
    tail = slice((nblk - 1) * BLOCK, nblk * BLOCK)
    single = n_steps == 1

    def body(q_ref, kc_ref, kp_ref, vc_ref, vp_ref, do_ref, lse_ref, dt_ref, dq_ref, dk_ref, dv_ref, *carry_refs):
        i = pl.program_id(1)

        def init():
            for carry_ref in carry_refs:
                carry_ref[...] = jnp.zeros_like(carry_ref)

        def compute():
            kk_all = jnp.concatenate([kp_ref[...], kc_ref[...]], axis=0)
            vv_all = jnp.concatenate([vp_ref[...], vc_ref[...]], axis=0)
            lane = lax.broadcasted_iota(jnp.int32, (1, DIL_OUT_WIDTH), 1)
            head_id = lane // HEAD_DIM
            chains = [(b, h) for b in range(nblk) for h in range(DIL_HEADS_PER_GROUP)]
            rows = lambda b: slice(b * BLOCK, (b + 1) * BLOCK)
            keys = lambda b: slice(b * BLOCK, (b + 2) * BLOCK)
            bands = [_dil_band(i == 0 if b == 0 else False) for b in range(nblk)]
            qms, doms = [], []
            for b, h in chains:
                q, do = q_ref[rows(b), :], do_ref[rows(b), :]
                qms.append(jnp.where(head_id == h, q, jnp.zeros_like(q)))
                doms.append(jnp.where(head_id == h, do, jnp.zeros_like(do)))
            scores = [_dot_nt(qm, kk_all[keys(b)]) for (b, h), qm in zip(chains, qms)]
            dps = [_dot_nt(dom, vv_all[keys(b)]) for (b, h), dom in zip(chains, doms)]
            pbs, dss = [], []
            for n, (b, h) in enumerate(chains):
                steps, valid = bands[b]
                first = lane == h * HEAD_DIM
                lse = jnp.sum(jnp.where(first, lse_ref[rows(b), :], 0.0), axis=1, keepdims=True)
                dt = jnp.sum(jnp.where(first, dt_ref[rows(b), :], 0.0), axis=1, keepdims=True)
                logits = jnp.where(valid, scores[n] * scale - slopes[h] * steps, NEG_INF)
                p = jnp.where(valid, jnp.exp(logits - lse), 0.0)
                pbs.append(p.astype(BF16))
                dss.append((p * (dps[n] + dt) * scale).astype(BF16))
            dqs = [_dot(ds, kk_all[keys(b)]) for (b, h), ds in zip(chains, dss)]
            dks = [_dot_tn(ds, qm) for ds, qm in zip(dss, qms)]
            dvs = [_dot_tn(pb, dom) for pb, dom in zip(pbs, doms)]
            dkk, dvv = [], []
            for b in range(nblk):
                mine = [n for n, ch in enumerate(chains) if ch[0] == b]
                dq = dqs[mine[0]]
                for n in mine[1:]:
                    dq = jnp.where(head_id == chains[n][1], dqs[n], dq)
                dq_ref[rows(b), :] = dq.astype(dq_ref.dtype)
                dkk.append((dks[mine[0]] + dks[mine[1]]) + (dks[mine[2]] + dks[mine[3]]))
                dvv.append((dvs[mine[0]] + dvs[mine[1]]) + (dvs[mine[2]] + dvs[mine[3]]))
            for n, (out_ref, parts) in enumerate(((dk_ref, dkk), (dv_ref, dvv))):
                done = [parts[b][BLOCK:] + parts[b + 1][:BLOCK] if b + 1 < nblk else parts[b][BLOCK:]
                        for b in range(nblk)]
                if single:
                    for b in range(nblk):
                        out_ref[rows(b), :] = done[b].astype(out_ref.dtype)
                    continue
                carry_ref = carry_refs[n]
                if nblk > 1:
                    out_ref[: (nblk - 1) * BLOCK, :] = carry_ref[: (nblk - 1) * BLOCK, :].astype(out_ref.dtype)
                out_ref[tail, :] = (carry_ref[tail, :] + parts[0][:BLOCK]).astype(out_ref.dtype)
                for b in range(nblk):
                    carry_ref[rows(b), :] = done[b]

        def flush():
            for out_ref, carry_ref in zip((dk_ref, dv_ref), carry_refs):
                out_ref[...] = carry_ref[...].astype(out_ref.dtype)

        if single:
            compute()
        else:
            pl.when(i == 0)(init)
            pl.when(i < n_steps)(compute)
            pl.when(i == n_steps)(flush)

    clamp = lambda i: jnp.minimum(i, n_steps - 1)
    row_spec = pl.BlockSpec((nblk * BLOCK, DIL_OUT_WIDTH), lambda r, i: (clamp(i), r))
    late_spec = pl.BlockSpec((nblk * BLOCK, DIL_OUT_WIDTH), lambda r, i: (jnp.maximum(i - 1, 0), r))
    res = _call(
        body, (qkv_v, qkv_v, qkv_v, qkv_v, qkv_v, do_g, lse_g, dterm_g), rider,
        name=f"dil_bwd_g{group}",
        grid=(dilation, n_steps + (0 if single else 1)),
        in_specs=_dil_step_specs(ncb, cols, nblk, clamp) + [row_spec, row_spec, row_spec],
        out_specs=[row_spec, row_spec, row_spec] if single else [row_spec, late_spec, late_spec],
        out_shape=[jax.ShapeDtypeStruct((sub, dilation * DIL_OUT_WIDTH), BF16)] * 3,
        scratch_shapes=[] if single else [pltpu.VMEM((nblk * BLOCK, DIL_OUT_WIDTH), F32)] * 2,
        compiler_params=_cparams(2),
    )
    grads, lands = res if rider is not None else (res, None)
    if dilation > 1:
        grads = _dil_relayout(f"dil_bwd_rows_g{group}", list(grads), dilation, to_view=False)
    return tuple(grads) if rider is None else (tuple(grads), lands)


def _dil_view(qkv, group):
    _, dilation = DIL_GROUPS[group]
    w = DIL_OUT_WIDTH
    if dilation == 1:
        return qkv, QKV_COLS // w, (3 * group, 3 * group + 1, 3 * group + 2)
    (own,) = _dil_relayout(f"dil_view_g{group}", [qkv], dilation, to_view=True, col_block=group, width=3 * w)
    return own, 3, (0, 1, 2)


def _group_major(w_qkv):
    w = DIL_OUT_WIDTH
    ng = len(DIL_GROUPS)
    cols = [w_qkv[:, (part * ng + g) * w:(part * ng + g + 1) * w] for g in range(ng) for part in range(3)]
    return jnp.concatenate(cols + [w_qkv[:, 3 * DIL_WIDTH:]], axis=1)


def _head_block_ones():
    r = lax.broadcasted_iota(jnp.int32, (DIL_OUT_WIDTH, DIL_OUT_WIDTH), 0) // HEAD_DIM
    c = lax.broadcasted_iota(jnp.int32, (DIL_OUT_WIDTH, DIL_OUT_WIDTH), 1) // HEAD_DIM
    return jnp.where(r == c, 1.0, 0.0).astype(BF16)


def _dil_mix_weights(l0, l1, l2):
    mx = jnp.maximum(jnp.maximum(l0, l1), l2)
    e0, e1, e2 = jnp.exp(l0 - mx), jnp.exp(l1 - mx), jnp.exp(l2 - mx)
    inv = 1.0 / (e0 + e1 + e2)
    return e0 * inv, e1 * inv, e2 * inv


def _dil_view_spec(dilation):
    return pl.BlockSpec((DIL_RELAYOUT_ROWS // dilation, dilation * DIL_OUT_WIDTH), lambda i: (i, 0))


def _dil_mix_call(name, body, s, ins, in_dils, outs, n_relaid):
    out_specs = [_dil_view_spec(d or 1) for d, _ in outs]
    out_shape = [jax.ShapeDtypeStruct((s // (d or 1), (d or 1) * DIL_OUT_WIDTH), dt) for d, dt in outs]
    n_in, n_out = len(ins), len(outs)
    n_steps = s // DIL_RELAYOUT_ROWS
    blocks = [(DIL_RELAYOUT_ROWS // d, d * DIL_OUT_WIDTH) for d in in_dils]
    windows = [_row_window(rows) for rows, _ in blocks]

    def ringed(*refs):
        hbm_refs, out_refs = refs[:n_in], refs[n_in:n_in + n_out]
        scratch = refs[n_in + n_out:n_in + n_out + n_relaid]
        rings, sem = refs[n_in + n_out + n_relaid:-1], refs[-1]
        body(*_ring_step(hbm_refs, rings, sem, windows, n_steps), *out_refs, *scratch)

    return _pcall(
        ringed,
        name=name,
        grid=(n_steps,),
        in_specs=[_HBM] * n_in,
        out_specs=out_specs,
        out_shape=out_shape,
        scratch_shapes=[_view_scratch(DIL_OUT_WIDTH)] * n_relaid
        + [pltpu.VMEM((RING_SLOTS,) + blk, x.dtype) for blk, x in zip(blocks, ins)]
        + [pltpu.SemaphoreType.DMA((n_in, RING_SLOTS))],
        compiler_params=_cparams(1),
    )(*ins)


DIL_MIX_CHUNK = 64
_DIL_SLABS = DIL_OUT_WIDTH // LANES


def _dil_rows(refs, scratch):
    dils = [d for _, d in DIL_GROUPS]
    assert dils[0] == 1
    readers = [lambda rows, j, ref=refs[0]: ref[rows, j * LANES:(j + 1) * LANES]]
    for ref, scr, d in zip(refs[1:], scratch, dils[1:]):
        _rows_from_view(ref, scr, d, DIL_OUT_WIDTH)
        readers.append(lambda rows, j, scr=scr: scr[j, rows, :])
    return readers


def _dil_mix_chunks(step):
    def chunk(c, carry):
        step(pl.ds(pl.multiple_of(c * DIL_MIX_CHUNK, DIL_MIX_CHUNK), DIL_MIX_CHUNK))
        return carry

    lax.fori_loop(0, DIL_RELAYOUT_ROWS // DIL_MIX_CHUNK, chunk, 0, unroll=4)


def _dil_mix_fwd(os_, lses):
    ng = len(DIL_GROUPS)
    s = os_[0].shape[0]

    def body(*refs):
        o_refs, l_refs, out_ref, scratch = refs[:ng], refs[ng:2 * ng], refs[2 * ng], refs[2 * ng + 1:]
        o_at = _dil_rows(o_refs, scratch[:ng - 1])
        l_at = _dil_rows(l_refs, scratch[ng - 1:])

        def step(rows):
            for j in range(_DIL_SLABS):
                w0, w1, w2 = _dil_mix_weights(*[at(rows, j) for at in l_at])
                o0, o1, o2 = [at(rows, j) for at in o_at]
                out_ref[rows, j * LANES:(j + 1) * LANES] = (w0 * o0 + w1 * o1 + w2 * o2).astype(out_ref.dtype)

        _dil_mix_chunks(step)

    dils = [d for _, d in DIL_GROUPS]
    (o_a,) = _dil_mix_call("dil_mix_fwd", body, s, list(os_) + list(lses), dils * 2, [(None, BF16)], 2 * (ng - 1))
    return o_a


def _dil_mix_bwd(do_a, os_, lses):
    ng = len(DIL_GROUPS)
    s = do_a.shape[0]
    dils = [d for _, d in DIL_GROUPS]

    def body(*refs):
        do_ref, o_refs, l_refs = refs[0], refs[1:1 + ng], refs[1 + ng:1 + 2 * ng]
        out_refs, scratch = refs[1 + 2 * ng:1 + 4 * ng], refs[1 + 4 * ng:]
        o_at = _dil_rows(o_refs, scratch[:ng - 1])
        l_at = _dil_rows(l_refs, scratch[ng - 1:2 * (ng - 1)])
        spare = iter(scratch[2 * (ng - 1):])
        staged = [None if dils[n % ng] == 1 else next(spare) for n in range(2 * ng)]
        ones = _head_block_ones()

        def step(rows):
            do = do_ref[rows, :].astype(F32)
            ws, prods = [], []
            for j in range(_DIL_SLABS):
                w0, w1, w2 = _dil_mix_weights(*[at(rows, j) for at in l_at])
                o0, o1, o2 = [at(rows, j) for at in o_at]
                ws.append((w0, w1, w2))
                prods.append(do[:, j * LANES:(j + 1) * LANES] * (w0 * o0 + w1 * o1 + w2 * o2))
            tot = _dot_hi_lo(jnp.concatenate(prods, axis=1), ones)
            for j in range(_DIL_SLABS):
                slab = slice(j * LANES, (j + 1) * LANES)
                vals = [w * do[:, slab] for w in ws[j]] + [-w * tot[:, slab] for w in ws[j]]
                for val, dst, scr in zip(vals, out_refs, staged):
                    if scr is None:
                        dst[rows, slab] = val.astype(dst.dtype)
                    else:
                        scr[j, rows, :] = val

        _dil_mix_chunks(step)
        for n, (dst, scr) in enumerate(zip(out_refs, staged)):
            if scr is not None:
                _rows_to_view(scr, dst, dils[n % ng], DIL_OUT_WIDTH)

    return _dil_mix_call(
        "dil_mix_bwd", body, s, [do_a] + list(os_) + list(lses), [1] + dils * 2,
        [(d, BF16) for d in dils] + [(d, F32) for d in dils], 4 * (ng - 1))


_SB_Q0 = 3 * DIL_WIDTH // LANES
_SB_K0 = _SB_Q0 + SB_WIDTH // LANES
_SB_V0 = _SB_K0 + SB_WIDTH // LANES


_EXP_CLAMP = 88.0
_SB_DEAD = 104.0


def _tri(t, op):
    r = lax.broadcasted_iota(jnp.int32, (t, t), 0)
    c = lax.broadcasted_iota(jnp.int32, (t, t), 1)
    return jnp.where(op(r, c), 1.0, 0.0).astype(BF16)


def _softplus(z):
    return jnp.maximum(z, jnp.log(1.0 + jnp.exp(jnp.minimum(z, _EXP_CLAMP))))


def _sb_chain_head(qm, kj, mask):
    z = _dot_nt(qm, kj)
    sp = _softplus(z)
    return (sp if mask is None else jnp.where(mask, sp, 0.0)), z - sp


def _sb_fwd(qkv, rider=None):
    s = qkv.shape[0]
    t = SB_TK
    assert s % (2 * t) == 0
    nq = s // (2 * t)
    n_pairs = SB_WIDTH // LANES

    def body(q_ref, k_ref, v_ref, o_ref, tot_ref, steps_ref):
        p, i = pl.program_id(0), pl.program_id(1)
        lane_hi = lax.broadcasted_iota(jnp.int32, (1, LANES), 1) // HEAD_DIM
        later = _tri(t, lambda r, c: r > c)
        causal = lax.broadcasted_iota(jnp.int32, (t, t), 1) < lax.broadcasted_iota(jnp.int32, (t, t), 0)
        qms = []
        for x in range(2):
            q = q_ref[pl.ds(x * t, t), :] * (1.0 / math.sqrt(HEAD_DIM))
            qms.append([jnp.where(lane_hi == hh, q, jnp.zeros_like(q)) for hh in range(2)])

        def tile(j):
            off = pl.multiple_of(j * t, t)
            return k_ref[pl.ds(off, t), :], v_ref[pl.ds(off, t), :]

        def step(groups, carry):
            kv = [tile(j) for _, j, _ in groups]
            chains = [(g, x, hh) for g, (x, _, _) in enumerate(groups) for hh in range(2)]
            heads = [_sb_chain_head(qms[x][hh], kv[g][0], causal if groups[g][2] else None) for g, x, hh in chains]
            sufs = [_dot(sp.astype(BF16), later) for sp, _ in heads]
            cur = [list(carry[0]), list(carry[1])]
            for (g, x, hh), (sp, lpos), suf in zip(chains, heads, sufs):
                c, acc = cur[x][hh]
                a = jnp.exp(lpos - suf - c)
                if groups[g][2]:
                    a = jnp.where(causal, a, 0.0)
                cur[x][hh] = (c + jnp.sum(sp, axis=1, keepdims=True), acc + _dot(a.astype(BF16), kv[g][1]))
            return (tuple(cur[0]), tuple(cur[1]))

        def lowest(carry):
            return jnp.min(jnp.minimum(jnp.minimum(carry[0][0][0], carry[0][1][0]),
                                       jnp.minimum(carry[1][0][0], carry[1][1][0])))

        zero = (jnp.zeros((t, 1), F32), jnp.zeros((t, LANES), F32))
        start = ((zero, zero), (zero, zero))
        carry = lax.cond(
            i == 0,
            lambda ca: step([(0, 0, True), (1, 1, True), (1, 0, False)], ca),
            lambda ca: step([(0, 2 * i, True), (1, 2 * i + 1, True), (0, 2 * i - 1, False), (1, 2 * i, False)], ca),
            start)

        def walk(state):
            n, ca, _ = state
            ca = step([(0, 2 * i - 2 - n, False), (1, 2 * i - 1 - n, False)], ca)
            return n + 1, ca, lowest(ca)

        n_more, carry, low = lax.while_loop(
            lambda st: jnp.logical_and(st[0] + 1 < 2 * i, st[2] <= _SB_DEAD), walk, (jnp.int32(0), carry, lowest(carry)))
        b_last = jnp.logical_and(jnp.logical_and(i > 0, n_more + 1 == 2 * i), low <= _SB_DEAD)
        carry = lax.cond(b_last, lambda ca: step([(1, 0, False)], ca), lambda ca: ca, carry)
        for x in range(2):
            (c0, acc0), (c1, acc1) = carry[x]
            o_ref[pl.ds(x * t, t), :] = jnp.where(lane_hi == 0, acc0, acc1).astype(o_ref.dtype)
            tot_ref[pl.ds(x * t, t), :] = jnp.where(lane_hi == 0, c0, c1)
        steps_ref[p, i] = 1 + n_more + b_last.astype(jnp.int32)

    return _call(
        body, (qkv, qkv, qkv), rider,
        name="sb_fwd",
        grid=(n_pairs, nq),
        in_specs=[
            pl.BlockSpec((2 * t, LANES), lambda p, i: (i, _SB_Q0 + p)),
            pl.BlockSpec((s, LANES), lambda p, i: (0, _SB_K0 + p)),
            pl.BlockSpec((s, LANES), lambda p, i: (0, _SB_V0 + p)),
        ],
        out_specs=[pl.BlockSpec((2 * t, LANES), lambda p, i: (i, p))] * 2 + [pl.BlockSpec(memory_space=pltpu.SMEM)],
        out_shape=[jax.ShapeDtypeStruct((s, SB_WIDTH), BF16), jax.ShapeDtypeStruct((s, SB_WIDTH), F32),
                   jax.ShapeDtypeStruct((n_pairs, nq), jnp.int32)],
        compiler_params=_cparams(2),
    )


def _sb_bwd(qkv, do_b, tot_b, n_steps):
    s = qkv.shape[0]
    t = SB_TK
    nq = s // (2 * t)
    n_pairs = SB_WIDTH // LANES
    scale = 1.0 / math.sqrt(HEAD_DIM)

    def body(steps_ref, q_ref, k_ref, v_ref, do_ref, tot_ref, dq_ref, dk_ref, dv_ref):
        p, i = pl.program_id(0), pl.program_id(1)

        @pl.when(i == 0)
        def _():
            dk_ref[...] = jnp.zeros_like(dk_ref)
            dv_ref[...] = jnp.zeros_like(dv_ref)

        lane = lax.broadcasted_iota(jnp.int32, (1, LANES), 1)
        lane_hi = lane // HEAD_DIM
        later = _tri(t, lambda r, c: r > c)
        before = _tri(t, lambda r, c: r < c)
        causal = lax.broadcasted_iota(jnp.int32, (t, t), 1) < lax.broadcasted_iota(jnp.int32, (t, t), 0)
        qms, doms, tots = [], [], []
        for x in range(2):
            rows = pl.ds(x * t, t)
            q, do, tot_all = q_ref[rows, :] * scale, do_ref[rows, :], tot_ref[rows, :]
            qms.append([jnp.where(lane_hi == hh, q, jnp.zeros_like(q)) for hh in range(2)])
            doms.append([jnp.where(lane_hi == hh, do, jnp.zeros_like(do)) for hh in range(2)])
            tots.append([jnp.sum(jnp.where(lane == hh * HEAD_DIM, tot_all, 0.0), axis=1, keepdims=True)
                         for hh in range(2)])

        def step(groups, carry):
            offs = [pl.multiple_of(j * t, t) for _, j, _ in groups]
            ks = [k_ref[pl.ds(off, t), :] for off in offs]
            vs = [v_ref[pl.ds(off, t), :] for off in offs]
            chains = [(g, x, hh) for g, (x, _, _) in enumerate(groups) for hh in range(2)]
            heads = [_sb_chain_head(qms[x][hh], ks[g], causal if groups[g][2] else None) for g, x, hh in chains]
            sufs = [_dot(sp.astype(BF16), later) for sp, _ in heads]
            das = [_dot_nt(doms[x][hh], vs[g]) for g, x, hh in chains]
            cur = [list(carry[0]), list(carry[1])]
            sigs, gs, abs_, cg_before = [], [], [], []
            for (g_, x, hh), (sp, lpos), suf, da in zip(chains, heads, sufs, das):
                cl, cg, dq = cur[x][hh]
                cl = cl + jnp.sum(sp, axis=1, keepdims=True)
                sig = jnp.exp(lpos)
                a = sig * jnp.exp(-suf - (tots[x][hh] - cl))
                if groups[g_][2]:
                    a = jnp.where(causal, a, 0.0)
                g = a * da
                sigs.append(sig)
                gs.append(g)
                abs_.append(a.astype(BF16))
                cg_before.append(cg)
                cur[x][hh] = (cl, cg + jnp.sum(g, axis=1, keepdims=True), dq)
            prefs = [_dot(g.astype(BF16), before) for g in gs]
            dvs = [_dot_tn(ab, doms[x][hh]) for (_, x, hh), ab in zip(chains, abs_)]
            dzs = []
            for (g_, x, hh), sig, g, pref, cg in zip(chains, sigs, gs, prefs, cg_before):
                dz = g - sig * (g + pref + cg)
                if groups[g_][2]:
                    dz = jnp.where(causal, dz, 0.0)
                dzs.append(dz.astype(BF16))
            dqs = [_dot(dz, ks[g_]) for (g_, x, hh), dz in zip(chains, dzs)]
            dks = [_dot_tn(dz, qms[x][hh]) for (_, x, hh), dz in zip(chains, dzs)]
            for n, (_, x, hh) in enumerate(chains):
                cl, cg, dq = cur[x][hh]
                cur[x][hh] = (cl, cg, dq + dqs[n])
            for g_, off in enumerate(offs):
                dk_ref[pl.ds(off, t), :] += dks[2 * g_] + dks[2 * g_ + 1]
                dv_ref[pl.ds(off, t), :] += dvs[2 * g_] + dvs[2 * g_ + 1]
            return (tuple(cur[0]), tuple(cur[1]))

        taken = steps_ref[p, i]
        n_full = jnp.minimum(taken, 2 * i)
        zero = (jnp.zeros((t, 1), F32), jnp.zeros((t, 1), F32), jnp.zeros((t, LANES), F32))
        carry = ((zero, zero), (zero, zero))
        carry = lax.cond(jnp.logical_and(i > 0, taken > 2 * i), lambda ca: step([(1, 0, False)], ca), lambda ca: ca,
                         carry)
        carry = lax.fori_loop(
            0, n_full - 1,
            lambda n, ca: step([(0, 2 * i - n_full + n, False), (1, 2 * i + 1 - n_full + n, False)], ca), carry)
        carry = lax.cond(
            i == 0,
            lambda ca: step([(1, 0, False), (0, 0, True), (1, 1, True)], ca),
            lambda ca: step([(0, 2 * i - 1, False), (1, 2 * i, False), (0, 2 * i, True), (1, 2 * i + 1, True)], ca),
            carry)
        for x in range(2):
            dq = jnp.where(lane_hi == 0, carry[x][0][2], carry[x][1][2])
            dq_ref[pl.ds(x * t, t), :] = (dq * scale).astype(dq_ref.dtype)

    row_spec = pl.BlockSpec((2 * t, LANES), lambda p, i, ns: (i, p))
    full_spec = pl.BlockSpec((s, LANES), lambda p, i, ns: (0, p))
    return _pcall(
        body,
        name="sb_bwd",
        grid_spec=pltpu.PrefetchScalarGridSpec(
            num_scalar_prefetch=1,
            grid=(n_pairs, nq),
            in_specs=[
                pl.BlockSpec((2 * t, LANES), lambda p, i, ns: (i, _SB_Q0 + p)),
                pl.BlockSpec((s, LANES), lambda p, i, ns: (0, _SB_K0 + p)),
                pl.BlockSpec((s, LANES), lambda p, i, ns: (0, _SB_V0 + p)),
                row_spec, row_spec,
            ],
            out_specs=[row_spec, full_spec, full_spec],
        ),
        out_shape=[jax.ShapeDtypeStruct((s, SB_WIDTH), BF16), jax.ShapeDtypeStruct((s, SB_WIDTH), F32),
                   jax.ShapeDtypeStruct((s, SB_WIDTH), F32)],
        compiler_params=_cparams(2),
    )(n_steps, qkv, qkv, qkv, do_b, tot_b)


def _gates(gl, bg):
    return _sigmoid(gl[:, :D_MODEL] + bg[:, :D_MODEL]), _sigmoid(gl[:, D_MODEL:] + bg[:, D_MODEL:])


def _mixer_fwd(o_a, o_b, gl, x0, bg, g2, w_ud, w_us, w_out, tm):
    def epi(_, rows, consts):
        oa, ob, glv, x = rows
        bgv, g2v, wud, wus, wout = consts
        ga, gb = _gates(glv, bgv)
        merged = ga * _dot(oa, wud) + gb * _dot(ob, wus)
        x1 = x + _dot(merged.astype(BF16), wout)
        r, xh = _rms_stats(x1)
        return [x1, xh * g2v], []

    return _rowk("mixer_fwd", tm=tm, rows=[o_a, o_b, gl, x0], consts=[bg, g2, w_ud, w_us, w_out],
                 row_outs=[(D_MODEL, F32), (D_MODEL, BF16)], epilogue=epi)


def _mixer_bwd(dx1, o_a, o_b, gl, bg, w_ud, w_us, w_out, tm, rider=None):
    s = dx1.shape[0]
    nm = s // tm

    def body(dx_ref, oa_ref, ob_ref, gl_ref, bg_ref, wud_ref, wus_ref, wout_ref,
             doa_ref, dob_ref, dgl_ref, gwout_ref, gwud_ref, gwus_ref, gbg_ref, awout_ref, awud_ref, awus_ref):
        i = pl.program_id(0)
        dxb = dx_ref[...].astype(BF16)
        oa, ob = oa_ref[...], ob_ref[...]
        ga, gb = _gates(gl_ref[...], bg_ref[...])
        ua, ub = _dot(oa, wud_ref[...]), _dot(ob, wus_ref[...])
        merged = (ga * ua + gb * ub).astype(BF16)
        dm = _dot_nt(dxb, wout_ref[...])
        dua = (dm * ga).astype(BF16)
        dub = (dm * gb).astype(BF16)
        dgla = dm * ua * ga * (1.0 - ga)
        dglb = dm * ub * gb * (1.0 - gb)
        doa_ref[...] = _dot_nt(dua, wud_ref[...]).astype(doa_ref.dtype)
        dob_ref[...] = _dot_nt(dub, wus_ref[...]).astype(dob_ref.dtype)
        dgl_ref[:, :D_MODEL] = dgla.astype(dgl_ref.dtype)
        dgl_ref[:, D_MODEL:] = dglb.astype(dgl_ref.dtype)
        parts = [(gwout_ref, awout_ref, _dot_tn(merged, dxb)), (gwud_ref, awud_ref, _dot_tn(oa, dua)),
                 (gwus_ref, awus_ref, _dot_tn(ob, dub))]
        for out, r, v in parts:

            @pl.when(i == 0)
            def _(r=r, v=v):
                r[...] = v

            @pl.when(i > 0)
            def _(r=r, v=v):
                r[...] += v

            @pl.when(i == nm - 1)
            def _(out=out, r=r):
                if len(out.shape) == 2:
                    out[...] = r[...].astype(out.dtype)
                else:
                    for p in range(N_DEV):
                        out[p] = r[:, p * out.shape[2]:(p + 1) * out.shape[2]].astype(out.dtype)

        sa = jnp.sum(dgla, axis=0, keepdims=True)
        sb = jnp.sum(dglb, axis=0, keepdims=True)

        @pl.when(i == 0)
        def _():
            gbg_ref[:, :D_MODEL] = sa
            gbg_ref[:, D_MODEL:] = sb

        @pl.when(i > 0)
        def _():
            gbg_ref[:, :D_MODEL] += sa
            gbg_ref[:, D_MODEL:] += sb

    row = lambda w: pl.BlockSpec((tm, w), lambda i: (i, 0))
    full = lambda a: pl.BlockSpec(a.shape, lambda i: (0, 0), pipeline_mode=pl.Buffered(1))
    chunks = lambda r: (N_DEV, r, D_MODEL // N_DEV)
    return _call(
        body, (dx1, o_a, o_b, gl, bg, w_ud, w_us, w_out), rider,
        name="mixer_bwd",
        grid=(nm,),
        in_specs=[row(D_MODEL), row(DIL_OUT_WIDTH), row(SB_WIDTH), row(2 * D_MODEL),
                  full(bg), full(w_ud), full(w_us), full(w_out)],
        out_specs=[row(DIL_OUT_WIDTH), row(SB_WIDTH), row(2 * D_MODEL),
                   pl.BlockSpec((D_MODEL, D_MODEL), lambda i: (0, 0)),
                   pl.BlockSpec(chunks(DIL_OUT_WIDTH), lambda i: (0, 0, 0)),
                   pl.BlockSpec(chunks(SB_WIDTH), lambda i: (0, 0, 0)),
                   pl.BlockSpec((1, 2 * D_MODEL), lambda i: (0, 0))],
        out_shape=[jax.ShapeDtypeStruct((s, DIL_OUT_WIDTH), BF16), jax.ShapeDtypeStruct((s, SB_WIDTH), BF16),
                   jax.ShapeDtypeStruct((s, 2 * D_MODEL), BF16),
                   jax.ShapeDtypeStruct((D_MODEL, D_MODEL), BF16), jax.ShapeDtypeStruct(chunks(DIL_OUT_WIDTH), BF16),
                   jax.ShapeDtypeStruct(chunks(SB_WIDTH), BF16), jax.ShapeDtypeStruct((1, 2 * D_MODEL), F32)],
        scratch_shapes=[pltpu.VMEM((D_MODEL, D_MODEL), F32), pltpu.VMEM((DIL_OUT_WIDTH, D_MODEL), F32),
                        pltpu.VMEM((SB_WIDTH, D_MODEL), F32)],
        compiler_params=_cparams(1),
    )


def _all_gather(shards):
    n = len(shards)

    def body(*refs):
        x_refs, out_refs = refs[:n], refs[n:2 * n]
        send_sems, recv_sems, local_sems = refs[2 * n:]
        x, y, c = lax.axis_index("x"), lax.axis_index("y"), lax.axis_index("c")
        me, sibling = (x, y, c), (x, y, 1 - c)
        chips = [(1 - x, y), (x, 1 - y), (1 - x, 1 - y)]

        def slot(a, px, py, pc):
            return out_refs[a].at[4 * px + 2 * py + pc]

        def copy(a, k, block, to, own=False):
            return pltpu.make_async_remote_copy(
                src_ref=x_refs[a] if own else slot(a, *block), dst_ref=slot(a, *block),
                send_sem=send_sems.at[7 * a + k], recv_sem=recv_sems.at[7 * a + k], device_id=to, device_id_type=_MESH)

        mine = [pltpu.make_async_copy(x_refs[a], slot(a, *me), local_sems.at[a]) for a in range(n)]
        for cp in mine:
            cp.start()
        first = []
        for a in range(n):
            first.append(copy(a, 0, me, sibling, own=True))
            first += [copy(a, 1 + j, me, (*chip, c), own=True) for j, chip in enumerate(chips)]
        for cp in first:
            cp.start()
        passed = []
        for a in range(n):
            for j, chip in enumerate(chips):
                copy(a, 1 + j, (*chip, c), me).wait_recv()
                passed.append(copy(a, 4 + j, (*chip, c), sibling))
                passed[-1].start()
        for a in range(n):
            copy(a, 0, sibling, me).wait_recv()
            for j, chip in enumerate(chips):
                copy(a, 4 + j, (*chip, 1 - c), me).wait_recv()
        for cp in first + passed:
            cp.wait_send()
        for cp in mine:
            cp.wait()

    return _pcall(
        body,
        name="all_gather_weights",
        in_specs=[_HBM] * n,
        out_specs=[_HBM] * n,
        out_shape=[jax.ShapeDtypeStruct((N_DEV,) + s.shape, s.dtype) for s in shards],
        scratch_shapes=[pltpu.SemaphoreType.DMA((7 * n,)), pltpu.SemaphoreType.DMA((7 * n,)),
                        pltpu.SemaphoreType.DMA((n,))],
    )(*shards)


def _exchange(chunks):
    n = len(chunks)

    def body(*refs):
        g_refs, o_refs = refs[:n], refs[n:2 * n]
        send_sems, recv_sems, local_sems = refs[2 * n:]
        x, y, c = lax.axis_index("x"), lax.axis_index("y"), lax.axis_index("c")
        me = 4 * x + 2 * y + c
        own = [pltpu.make_async_copy(g_refs[a].at[me], o_refs[a].at[me], local_sems.at[a]) for a in range(n)]
        for cp in own:
            cp.start()
        copies = []
        for a in range(n):
            for k in range(1, N_DEV):
                px, py, pc = x ^ (k >> 2), y ^ ((k >> 1) & 1), c ^ (k & 1)
                peer = 4 * px + 2 * py + pc
                copies.append(pltpu.make_async_remote_copy(
                    src_ref=g_refs[a].at[peer], dst_ref=o_refs[a].at[me], send_sem=send_sems.at[7 * a + k - 1],
                    recv_sem=recv_sems.at[7 * a + k - 1], device_id=(px, py, pc), device_id_type=_MESH))
        for cp in copies:
            cp.start()
        for cp in copies:
            cp.wait()
        for cp in own:
            cp.wait()

    return _pcall(
        body,
        name="exchange_grads",
        in_specs=[_HBM] * n,
        out_specs=[_HBM] * n,
        out_shape=[jax.ShapeDtypeStruct(g.shape, g.dtype) for g in chunks],
        scratch_shapes=[pltpu.SemaphoreType.DMA((7 * n,)), pltpu.SemaphoreType.DMA((7 * n,)),
                        pltpu.SemaphoreType.DMA((n,))],
    )(*chunks)


def _reduce_adamw(name, parts, w, m, v, tr):
    _, rows, cols = parts.shape
    tr = min(tr, rows)
    assert rows % tr == 0
    c1 = 1.0 / (1.0 - ADAM_B1 ** ADAM_STEP)
    c2 = 1.0 / (1.0 - ADAM_B2 ** ADAM_STEP)

    def body(p_ref, w_ref, m_ref, v_ref, g_out, d_out, m_out, v_out):
        g = p_ref[0].astype(F32)
        for d in range(1, N_DEV):
            g = g + p_ref[d].astype(F32)
        mn = ADAM_B1 * m_ref[...] + (1.0 - ADAM_B1) * g
        vn = ADAM_B2 * v_ref[...] + (1.0 - ADAM_B2) * (g * g)
        g_out[...] = g
        m_out[...] = mn
        v_out[...] = vn
        d_out[...] = -ADAM_LR * ((mn * c1) / (jnp.sqrt(vn * c2) + ADAM_EPS) + ADAM_WD * w_ref[...])

    spec = pl.BlockSpec((tr, cols), lambda i: (i, 0))
    return _pcall(
        body,
        name=name,
        grid=(rows // tr,),
        in_specs=[pl.BlockSpec((N_DEV, tr, cols), lambda i: (0, i, 0)), spec, spec, spec],
        out_specs=[spec] * 4,
        out_shape=[jax.ShapeDtypeStruct((rows, cols), F32)] * 4,
        compiler_params=_cparams(1),
    )(parts, w, m, v)


_SHARDED = ("w_in", "w_up_dil", "w_up_sb", "w_out", "w_mlp_in", "w_mlp_out")
_FULL_SHAPES = {"w_in": (D_MODEL, IN_COLS), "w_up_dil": (DIL_OUT_WIDTH, D_MODEL), "w_up_sb": (SB_WIDTH, D_MODEL),
                "w_out": (D_MODEL, D_MODEL), "w_mlp_in": (D_MODEL, D_FF), "w_mlp_out": (D_FF, D_MODEL)}
_ROW_SHARDED = ("w_out", "w_mlp_out")


def _shard_shape(name):
    r, c = _FULL_SHAPES[name]
    return (r // N_DEV, c) if name in _ROW_SHARDED else (r, c // N_DEV)


def _assemble(name, gathered):
    r, c = _shard_shape(name)
    if name in _ROW_SHARDED:
        return gathered.reshape(N_DEV * r, c)
    return gathered.transpose(1, 0, 2).reshape(r, N_DEV * c)


def _chunk(name, full):
    r, c = _shard_shape(name)
    if name in _ROW_SHARDED:
        return full.reshape(N_DEV, r, c)
    return full.reshape(r, N_DEV, c).transpose(1, 0, 2)


_SMALL = (("norm_mix_g", D_MODEL), ("b_gate", 2 * D_MODEL), ("norm_mlp_g", D_MODEL), ("norm_final_g", D_MODEL))
_SMALL_N = sum(n for _, n in _SMALL) + LANES


def _pack_small(vals, tail):
    return jnp.concatenate([vals[n].reshape(1, -1) for n, _ in _SMALL] + [tail], axis=1)


def _unpack_small(vec, shapes):
    out, pos = {}, 0
    for n, width in _SMALL:
        out[n] = vec[:, pos:pos + width].reshape(shapes[n])
        pos += width
    return out, vec[:, pos:]


def kernel(x, norm_mix_g, w_in, b_gate, w_up_dil, w_up_sb, w_out, norm_mlp_g, w_mlp_in, w_mlp_out, norm_final_g, loss_target, m_norm_mix_g, m_w_in, m_b_gate, m_w_up_dil, m_w_up_sb, m_w_out, m_norm_mlp_g, m_w_mlp_in, m_w_mlp_out, m_norm_final_g, v_norm_mix_g, v_w_in, v_b_gate, v_w_up_dil, v_w_up_sb, v_w_out, v_norm_mlp_g, v_w_mlp_in, v_w_mlp_out, v_norm_final_g):
    given = dict(locals())
    s = x.shape[1]
    x0 = x.reshape(s, D_MODEL)
    target = loss_target.reshape(s, D_MODEL)
    g1 = norm_mix_g.reshape(1, D_MODEL)
    g2 = norm_mlp_g.reshape(1, D_MODEL)
    g3 = norm_final_g.reshape(1, D_MODEL)
    bg = b_gate.reshape(1, 2 * D_MODEL)
    w_shards = {n: given[n].reshape(_shard_shape(n)) for n in _SHARDED}
    m_shards = {n: given["m_" + n].reshape(_shard_shape(n)) for n in _SHARDED}
    v_shards = {n: given["v_" + n].reshape(_shard_shape(n)) for n in _SHARDED}

    shard_b = {n: w_shards[n].astype(BF16) for n in _SHARDED}
    (gathered_w_in,) = _all_gather([shard_b["w_in"]])
    w_in_f = _assemble("w_in", gathered_w_in)
    w_qkv, w_gl = _group_major(w_in_f[:, :QKV_COLS]), w_in_f[:, QKV_COLS:]
    full = {}

    def norm1(_, rows, consts):
        _, xh = _rms_stats(rows[0])
        return [xh * consts[0]], []

    (h1,) = _rowk("norm_mix", tm=1024, rows=[x0], consts=[g1], row_outs=[(D_MODEL, BF16)], epilogue=norm1)
    qkv, (land,) = _mm("proj_qkv", h1, w_qkv, out_dtype=BF16, tm=1024, tn=768, tk=D_MODEL,
                       rider=_Spread([shard_b["w_mlp_in"]], chunked=False))
    full["w_mlp_in"] = _assemble("w_mlp_in", land)
    gl = _mm("proj_gates", h1, w_gl, out_dtype=BF16, tm=1024, tn=1024, tk=D_MODEL)
    views = [_dil_view(qkv, g) for g in range(len(DIL_GROUPS))]
    dil = [_dil_fwd(views[g], g) for g in range(len(DIL_GROUPS))]
    os_, lses = [d[0] for d in dil], [d[1] for d in dil]
    o_a = _dil_mix_fwd(os_, lses)
    riding = ("w_mlp_out", "w_out", "w_up_sb", "w_up_dil")
    (o_b, tot_b, sb_steps), lands = _sb_fwd(qkv, rider=_Spread([shard_b[n] for n in riding], chunked=False))
    full.update({n: _assemble(n, land) for n, land in zip(riding, lands)})
    x1, h2 = _mixer_fwd(o_a, o_b, gl, x0, bg, g2, full["w_up_dil"], full["w_up_sb"], full["w_out"], 512)
    f = _mm("mlp_in", h2, full["w_mlp_in"], out_dtype=BF16, tm=1024, tn=1024, tk=D_MODEL,
            epilogue=lambda r, _: jnp.square(jnp.maximum(r, 0.0)))

    def head(acc, rows, consts):
        x1v, tv = rows
        g3v = consts[0]
        x2 = x1v + acc
        r, xh = _rms_stats(x2)
        diff = xh * g3v - tv
        loss = (0.5 / D_MODEL) * jnp.sum(jnp.sum(diff * diff, axis=0, keepdims=True), axis=1, keepdims=True)
        dy = diff * (1.0 / D_MODEL)
        dx2, dg = _rms_bwd(dy, xh, r, g3v)
        return [dx2, dx2], [dg, jnp.broadcast_to(loss, (1, LANES))]

    dx2, dx2b, gg3, loss_part = _rowk(
        "mlp_out_loss", a=f, w=full["w_mlp_out"], tm=512, tk=D_FF, rows=[x1, target], consts=[g3],
        row_outs=[(D_MODEL, F32), (D_MODEL, BF16)], acc_outs=[D_MODEL, LANES], epilogue=head)

    da = _mm("mlp_out_bwd", dx2b, full["w_mlp_out"], tb=True, out_dtype=BF16, tm=1024, tn=1024, tk=D_MODEL, extra=f,
             epilogue=lambda r, fv: r * (2.0 * jnp.sqrt(fv.astype(F32))))
    g_w_mlp_out = _mm("grad_w_mlp_out", f, dx2b, ta=True, out_dtype=BF16, tm=1024, tn=1024, tk=2048)
    g_w_mlp_in = _mm("grad_w_mlp_in", h2, da, ta=True, out_dtype=BF16, tm=1024, tn=1024, tk=2048, col_chunks=N_DEV)

    def norm_bwd(acc, rows, consts):
        xv, dres = rows
        r, xh = _rms_stats(xv)
        dx, dg = _rms_bwd(acc, xh, r, consts[0])
        return [dres + dx], [dg]

    parts = {}
    (dx1, gg2), (parts["w_mlp_in"],) = _rowk(
        "mlp_in_bwd", a=da, w=full["w_mlp_in"], nt=True, tm=512, tk=D_FF, rows=[x1, dx2], consts=[g2],
        row_outs=[(D_MODEL, F32)], acc_outs=[D_MODEL], epilogue=norm_bwd,
        rider=_Spread([g_w_mlp_in], chunked=True))
    (do_a, do_b, dgl, g_w_out, g_w_ud, g_w_us, g_bg), (parts["w_mlp_out"],) = _mixer_bwd(
        dx1, o_a, o_b, gl, bg, full["w_up_dil"], full["w_up_sb"], full["w_out"], 512,
        rider=_Spread([_chunk("w_mlp_out", g_w_mlp_out)], chunked=True))
    mix = _dil_mix_bwd(do_a, os_, lses)
    small_three = {"w_out": _chunk("w_out", g_w_out), "w_up_sb": g_w_us, "w_up_dil": g_w_ud}
    grads, lands = _dil_bwd(views[0], mix[0], lses[0], mix[3], 0,
                            rider=_Spread(list(small_three.values()), chunked=True))
    parts.update(dict(zip(small_three, lands)))
    dil_b = [grads] + [_dil_bwd(views[g], mix[g], lses[g], mix[3 + g], g) for g in (1, 2)]
    dq_b, dk_b, dv_b = _sb_bwd(qkv, do_b, tot_b, sb_steps)
    dproj = [d[0] for d in dil_b] + [d[1] for d in dil_b] + [d[2] for d in dil_b] + [dq_b, dk_b, dv_b, dgl]
    g_w_in = jnp.concatenate([
        _grad_cols("grad_w_in_dil", h1, dproj[:9], tm=D_MODEL, tk=1024),
        _grad_cols("grad_w_in_sb", h1, dproj[9:12], tm=D_MODEL, tk=1024),
        _grad_cols("grad_w_in_gates", h1, dproj[12:], tm=D_MODEL, tk=1024)], axis=1)
    (grad_x, gg1), (parts["w_in"],) = _rowk(
        "in_proj_bwd", a=dproj, w=w_in_f, nt=True, tm=512, tk=IN_COLS, rows=[x0, dx1], consts=[g1],
        row_outs=[(D_MODEL, F32)], acc_outs=[D_MODEL], epilogue=norm_bwd,
        rider=_Spread([_chunk("w_in", g_w_in)], chunked=True))

    small_part = _pack_small({"norm_mix_g": gg1, "b_gate": g_bg, "norm_mlp_g": gg2, "norm_final_g": gg3}, loss_part)
    (small_parts,) = _exchange([jnp.broadcast_to(small_part[None], (N_DEV, 1, _SMALL_N))])

    tags = ("grad_", "delta_", "new_m_", "new_v_")
    outs = {}
    for n, p in parts.items():
        res = _reduce_adamw("adamw_" + n, p, w_shards[n], m_shards[n], v_shards[n], 256)
        for tag, val in zip(tags, res):
            outs[tag + n] = val.reshape(given[n].shape)
    small_w = _pack_small(given, jnp.zeros((1, LANES), F32))
    small_m = _pack_small({n: given["m_" + n] for n, _ in _SMALL}, jnp.zeros((1, LANES), F32))
    small_v = _pack_small({n: given["v_" + n] for n, _ in _SMALL}, jnp.ones((1, LANES), F32))
    small_res = _reduce_adamw("adamw_replicated", small_parts, small_w, small_m, small_v, 8)

    small_shapes = {n: given[n].shape for n, _ in _SMALL}
    for tag, small in zip(tags, small_res):
        small_vals, tail = _unpack_small(small, small_shapes)
        for n, val in small_vals.items():
            outs[tag + n] = val
        if tag == "grad_":
            loss = tail[0, 0]
    names = ["norm_mix_g", "w_in", "b_gate", "w_up_dil", "w_up_sb", "w_out", "norm_mlp_g", "w_mlp_in", "w_mlp_out",
             "norm_final_g"]
    return (loss, grad_x.reshape(x.shape), *[outs["grad_" + n] for n in names], *[outs["delta_" + n] for n in names],
            *[outs["new_m_" + n] for n in names], *[outs["new_v_" + n] for n in names])
```
